```python
import math
import jax, jax.numpy as jnp
from jax import lax
import numpy as np

D_MODEL = 1024
BATCH = 8
SEQ = 8192
DEPTH = 1

CHUNK = 64
A_HEADS = 8
A_DK = 128
A_DV = 128
A_CONV = 4
A_W = A_HEADS * A_DV
B_HEADS = 16
B_DH = 64
B_W = B_HEADS * B_DH
B_PREV_CHUNKS = 8
B_MAX_REL = 256
B_REL_SIZE = CHUNK - 1 + B_MAX_REL + 1
D_FF = 2816
FFN_CONV = 3
N_BRANCHES = 2
IN_SPLITS = (3 * A_W, 4 * A_W, 4 * A_W + A_HEADS, 4 * A_W + 2 * A_HEADS,
             4 * A_W + 2 * A_HEADS + B_W, 4 * A_W + 2 * A_HEADS + 2 * B_W,
             4 * A_W + 2 * A_HEADS + 3 * B_W)
IN_COLS = 4 * A_W + 2 * A_HEADS + 3 * B_W + N_BRANCHES * D_MODEL
DEEPNORM_ALPHA = (2.0 * DEPTH) ** 0.25
DEEPNORM_BETA = (8.0 * DEPTH) ** -0.25
LN_EPS = 1e-5
RMS_EPS = 1e-6
L2_EPS = 1e-6
NEG_INF = -1e30

kernel_name = "hybrid_deltanet_bandattn_convffn_deepnorm_adaln"


def layernorm(x, g, b):
    xf = x.astype(jnp.float32)
    mu = jnp.mean(xf, axis=-1, keepdims=True)
    var = jnp.mean(jnp.square(xf - mu), axis=-1, keepdims=True)
    y = (xf - mu) * lax.rsqrt(var + LN_EPS) * g.astype(jnp.float32) + b.astype(jnp.float32)
    return y.astype(x.dtype)


def causal_dwconv(x, w):
    k_width, ch = w.shape
    return lax.conv_general_dilated(
        x, w[:, None, :].astype(x.dtype), window_strides=(1,), padding=[(k_width - 1, 0)],
        dimension_numbers=("NWC", "WIO", "NWC"), feature_group_count=ch)


def l2norm(x):
    return x * lax.rsqrt(jnp.sum(jnp.square(x), axis=-1, keepdims=True) + L2_EPS)


def chunk_gated_delta_rule(q, k, v, g, beta):
    b_, s_, h_, dk = q.shape
    dv = v.shape[-1]
    n_chunks = s_ // CHUNK

    def to_chunks(t):
        return t.reshape(b_, n_chunks, CHUNK, h_, -1).transpose(0, 1, 3, 2, 4)

    q, k, v = to_chunks(q), to_chunks(k), to_chunks(v)
    g = g.reshape(b_, n_chunks, CHUNK, h_).transpose(0, 1, 3, 2)
    beta = beta.reshape(b_, n_chunks, CHUNK, h_).transpose(0, 1, 3, 2)
    g = jnp.cumsum(g, axis=-1)

    causal = jnp.tril(jnp.ones((CHUNK, CHUNK), dtype=bool))
    strict = jnp.tril(jnp.ones((CHUNK, CHUNK), dtype=bool), k=-1)
    diff = g[..., :, None] - g[..., None, :]
    decay = jnp.where(causal, jnp.exp(jnp.where(causal, diff, 0.0)), 0.0)

    k_beta = k * beta[..., None]
    v_beta = v * beta[..., None]
    a_low = jnp.where(strict, jnp.einsum("bnhid,bnhjd->bnhij", k_beta, k) * decay, 0.0)
    eye = jnp.eye(CHUNK, dtype=jnp.float32)
    rhs = jnp.concatenate([v_beta, k_beta * jnp.exp(g)[..., None]], axis=-1)
    sol = lax.linalg.triangular_solve(a_low + eye, rhs, left_side=True, lower=True,
                                      unit_diagonal=True)
    u = sol[..., :dv]
    w = sol[..., dv:]
    qk = jnp.where(causal, jnp.einsum("bnhid,bnhjd->bnhij", q, k) * decay, 0.0)

    def step(state, inp):
        q_n, k_n, u_n, w_n, qk_n, g_n = inp
        v_new = u_n - jnp.einsum("bhck,bhkv->bhcv", w_n, state)
        o_n = (jnp.einsum("bhck,bhkv->bhcv", q_n * jnp.exp(g_n)[..., None], state)
               + jnp.einsum("bhij,bhjv->bhiv", qk_n, v_new))
        g_last = g_n[..., -1]
        k_dec = k_n * jnp.exp(g_last[..., None] - g_n)[..., None]
        state = state * jnp.exp(g_last)[..., None, None] + jnp.einsum("bhck,bhcv->bhkv", k_dec, v_new)
        return state, o_n

    xs = tuple(jnp.moveaxis(t, 1, 0) for t in (q, k, u, w, qk, g))
    state0 = jnp.zeros((b_, h_, dk, dv), jnp.float32)
    _, o = lax.scan(step, state0, xs)
    return o.transpose(1, 0, 3, 2, 4).reshape(b_, s_, h_, dv)


def gated_deltanet(qkv, z, beta_raw, a_raw, conv_w, a_log, dt_bias, norm_w):
    b_, s_, _ = qkv.shape
    qkv = jax.nn.silu(causal_dwconv(qkv, conv_w))
    q, k, v = jnp.split(qkv.astype(jnp.float32), 3, axis=-1)
    q = l2norm(q.reshape(b_, s_, A_HEADS, A_DK)) * (A_DK ** -0.5)
    k = l2norm(k.reshape(b_, s_, A_HEADS, A_DK))
    v = v.reshape(b_, s_, A_HEADS, A_DV)
    beta = jax.nn.sigmoid(beta_raw.astype(jnp.float32))
    g = -jnp.exp(a_log.astype(jnp.float32)) * jax.nn.softplus(
        a_raw.astype(jnp.float32) + dt_bias.astype(jnp.float32))
    o = chunk_gated_delta_rule(q, k, v, g, beta)
    o = o * lax.rsqrt(jnp.mean(jnp.square(o), axis=-1, keepdims=True) + RMS_EPS)
    o = o * norm_w.astype(jnp.float32) * jax.nn.silu(z.astype(jnp.float32).reshape(b_, s_, A_HEADS, A_DV))
    return o.reshape(b_, s_, A_W).astype(qkv.dtype)


def chunk_band_attention(q, k, v, rel_bias):
    b_, s_, h_, dh = q.shape
    n_chunks = s_ // CHUNK
    pad = B_PREV_CHUNKS * CHUNK
    band = (B_PREV_CHUNKS + 1) * CHUNK
    k_pad = jnp.pad(k, ((0, 0), (pad, 0), (0, 0), (0, 0)))
    v_pad = jnp.pad(v, ((0, 0), (pad, 0), (0, 0), (0, 0)))
    qi = np.arange(CHUNK)[:, None]
    kj = np.arange(band)[None, :]
    dist = pad + qi - kj
    idx = np.clip(dist, -(CHUNK - 1), B_MAX_REL) + (CHUNK - 1)
    bias = rel_bias.astype(jnp.float32)[:, idx]
    key_chunk = jnp.arange(band) // CHUNK
    scale = dh ** -0.5

    def one_chunk(n):
        q_n = lax.dynamic_slice_in_dim(q, n * CHUNK, CHUNK, axis=1)
        k_n = lax.dynamic_slice_in_dim(k_pad, n * CHUNK, band, axis=1)
        v_n = lax.dynamic_slice_in_dim(v_pad, n * CHUNK, band, axis=1)
        valid = (n - B_PREV_CHUNKS + key_chunk) >= 0
        s = jnp.einsum("bqhd,bkhd->bhqk", q_n, k_n).astype(jnp.float32) * scale + bias
        s = jnp.where(valid, s, NEG_INF)
        p = jax.nn.softmax(s, axis=-1).astype(v.dtype)
        return jnp.einsum("bhqk,bkhd->bqhd", p, v_n)

    out = lax.map(one_chunk, jnp.arange(n_chunks))
    return out.transpose(1, 0, 2, 3, 4).reshape(b_, s_, h_ * dh)


def _fwd_setup_inputs(seed: int = 0) -> dict:
    key = jax.random.key(seed)
    ks = jax.random.split(key, 24)

    def nrm(k, shape, scale):
        return jax.random.normal(k, shape, jnp.float32) * scale

    L = DEPTH
    x = nrm(ks[0], (BATCH, SEQ, D_MODEL), 1.0)
    c = nrm(ks[1], (BATCH, D_MODEL), 1.0)
    w_ada = nrm(ks[2], (L, D_MODEL, 6 * D_MODEL), D_MODEL ** -0.5)
    b_ada = nrm(ks[3], (L, 6 * D_MODEL), 0.02)
    w_in = nrm(ks[4], (L, D_MODEL, IN_COLS), D_MODEL ** -0.5)
    b_gate = nrm(ks[5], (L, N_BRANCHES * D_MODEL), 0.1)
    conv_a = nrm(ks[6], (L, A_CONV, 3 * A_W), A_CONV ** -0.5)
    a_log = jnp.log(jax.random.uniform(ks[7], (L, A_HEADS), jnp.float32, minval=1.0, maxval=16.0))
    dt = jnp.exp(jax.random.uniform(ks[8], (L, A_HEADS), jnp.float32,
                                    minval=math.log(1e-3), maxval=math.log(1e-1)))
    dt_bias = dt + jnp.log(-jnp.expm1(-dt))
    norm_a = 1.0 + nrm(ks[9], (L, A_DV), 0.05)
    rel_bias = nrm(ks[10], (L, B_HEADS, B_REL_SIZE), 0.2)
    w_branch_a = nrm(ks[11], (L, A_W, D_MODEL), A_W ** -0.5)
    w_branch_b = nrm(ks[12], (L, B_W, D_MODEL), B_W ** -0.5)
    w_o = nrm(ks[13], (L, D_MODEL, D_MODEL), DEEPNORM_BETA * D_MODEL ** -0.5)
    ln1_g = 1.0 + nrm(ks[14], (L, D_MODEL), 0.05)
    ln1_b = nrm(ks[15], (L, D_MODEL), 0.02)
    w_up = nrm(ks[16], (L, D_MODEL, 2 * D_FF), D_MODEL ** -0.5)
    conv_ffn = nrm(ks[17], (L, FFN_CONV, 2 * D_FF), FFN_CONV ** -0.5)
    b_conv_ffn = nrm(ks[18], (L, 2 * D_FF), 0.02)
    w_down = nrm(ks[19], (L, D_FF, D_MODEL), DEEPNORM_BETA * D_FF ** -0.5)
    ln2_g = 1.0 + nrm(ks[20], (L, D_MODEL), 0.05)
    ln2_b = nrm(ks[21], (L, D_MODEL), 0.02)
    return {"x": x, "c": c, "w_ada": w_ada, "b_ada": b_ada, "w_in": w_in, "b_gate": b_gate,
            "conv_a": conv_a, "a_log": a_log, "dt_bias": dt_bias, "norm_a": norm_a,
            "rel_bias": rel_bias, "w_branch_a": w_branch_a, "w_branch_b": w_branch_b, "w_o": w_o,
            "ln1_g": ln1_g, "ln1_b": ln1_b, "w_up": w_up, "conv_ffn": conv_ffn,
            "b_conv_ffn": b_conv_ffn, "w_down": w_down, "ln2_g": ln2_g, "ln2_b": ln2_b}


def _fwd_reference(x, c, w_ada, b_ada, w_in, b_gate, conv_a, a_log, dt_bias, norm_a, rel_bias,
              w_branch_a, w_branch_b, w_o, ln1_g, ln1_b, w_up, conv_ffn, b_conv_ffn, w_down,
              ln2_g, ln2_b):
    b_, s_, d_ = x.shape
    c_act = jax.nn.silu(c)
    for l in range(DEPTH):
        mod = (c_act @ w_ada[l] + b_ada[l])[:, None, :]
        shift_t, scale_t, gate_t, shift_f, scale_f, gate_f = jnp.split(mod, 6, axis=-1)

        h = x * (1.0 + scale_t) + shift_t
        proj = h @ w_in[l]
        qkv_a, z_a, beta_raw, a_raw, q_b, k_b, v_b, gates = jnp.split(proj, IN_SPLITS, axis=-1)

        o_a = gated_deltanet(qkv_a, z_a, beta_raw, a_raw, conv_a[l], a_log[l], dt_bias[l], norm_a[l])
        o_b = chunk_band_attention(q_b.reshape(b_, s_, B_HEADS, B_DH),
                                   k_b.reshape(b_, s_, B_HEADS, B_DH),
                                   v_b.reshape(b_, s_, B_HEADS, B_DH), rel_bias[l])

        gate_a, gate_b = jnp.split(jax.nn.sigmoid(gates + b_gate[l]), N_BRANCHES, axis=-1)
        merged = gate_a * (o_a @ w_branch_a[l]) + gate_b * (o_b @ w_branch_b[l])
        mix = merged @ w_o[l]
        x = layernorm(DEEPNORM_ALPHA * x + gate_t * mix, ln1_g[l], ln1_b[l])

        h = x * (1.0 + scale_f) + shift_f
        u = causal_dwconv(h @ w_up[l], conv_ffn[l]) + b_conv_ffn[l]
        u_gate, u_val = jnp.split(u, 2, axis=-1)
        ffn = (jax.nn.silu(u_gate) * u_val) @ w_down[l]
        x = layernorm(DEEPNORM_ALPHA * x + gate_f * ffn, ln2_g[l], ln2_b[l])
    return x


import jax as _jax
import jax.numpy as _jnp

TWIN_FORMAT = 'train_step'
FWD_PARAMS = ['x', 'c', 'w_ada', 'b_ada', 'w_in', 'b_gate', 'conv_a', 'a_log', 'dt_bias', 'norm_a', 'rel_bias', 'w_branch_a', 'w_branch_b', 'w_o', 'ln1_g', 'ln1_b', 'w_up', 'conv_ffn', 'b_conv_ffn', 'w_down', 'ln2_g', 'ln2_b']
TWIN_WEIGHTS = ['w_ada', 'b_ada', 'w_in', 'b_gate', 'conv_a', 'a_log', 'dt_bias', 'norm_a', 'rel_bias', 'w_branch_a', 'w_branch_b', 'w_o', 'ln1_g', 'ln1_b', 'w_up', 'conv_ffn', 'b_conv_ffn', 'w_down', 'ln2_g', 'ln2_b']
TWIN_DIFF_INPUT = 'x'
TWIN_INPUTS = ['x', 'c', 'w_ada', 'b_ada', 'w_in', 'b_gate', 'conv_a', 'a_log', 'dt_bias', 'norm_a', 'rel_bias', 'w_branch_a', 'w_branch_b', 'w_o', 'ln1_g', 'ln1_b', 'w_up', 'conv_ffn', 'b_conv_ffn', 'w_down', 'ln2_g', 'ln2_b', 'loss_target', 'm_w_ada', 'm_b_ada', 'm_w_in', 'm_b_gate', 'm_conv_a', 'm_a_log', 'm_dt_bias', 'm_norm_a', 'm_rel_bias', 'm_w_branch_a', 'm_w_branch_b', 'm_w_o', 'm_ln1_g', 'm_ln1_b', 'm_w_up', 'm_conv_ffn', 'm_b_conv_ffn', 'm_w_down', 'm_ln2_g', 'm_ln2_b', 'v_w_ada', 'v_b_ada', 'v_w_in', 'v_b_gate', 'v_conv_a', 'v_a_log', 'v_dt_bias', 'v_norm_a', 'v_rel_bias', 'v_w_branch_a', 'v_w_branch_b', 'v_w_o', 'v_ln1_g', 'v_ln1_b', 'v_w_up', 'v_conv_ffn', 'v_b_conv_ffn', 'v_w_down', 'v_ln2_g', 'v_ln2_b']
TWIN_OUTPUTS = ['loss', 'grad_x', 'grad_w_ada', 'grad_b_ada', 'grad_w_in', 'grad_b_gate', 'grad_conv_a', 'grad_a_log', 'grad_dt_bias', 'grad_norm_a', 'grad_rel_bias', 'grad_w_branch_a', 'grad_w_branch_b', 'grad_w_o', 'grad_ln1_g', 'grad_ln1_b', 'grad_w_up', 'grad_conv_ffn', 'grad_b_conv_ffn', 'grad_w_down', 'grad_ln2_g', 'grad_ln2_b', 'delta_w_ada', 'delta_b_ada', 'delta_w_in', 'delta_b_gate', 'delta_conv_a', 'delta_a_log', 'delta_dt_bias', 'delta_norm_a', 'delta_rel_bias', 'delta_w_branch_a', 'delta_w_branch_b', 'delta_w_o', 'delta_ln1_g', 'delta_ln1_b', 'delta_w_up', 'delta_conv_ffn', 'delta_b_conv_ffn', 'delta_w_down', 'delta_ln2_g', 'delta_ln2_b', 'new_m_w_ada', 'new_m_b_ada', 'new_m_w_in', 'new_m_b_gate', 'new_m_conv_a', 'new_m_a_log', 'new_m_dt_bias', 'new_m_norm_a', 'new_m_rel_bias', 'new_m_w_branch_a', 'new_m_w_branch_b', 'new_m_w_o', 'new_m_ln1_g', 'new_m_ln1_b', 'new_m_w_up', 'new_m_conv_ffn', 'new_m_b_conv_ffn', 'new_m_w_down', 'new_m_ln2_g', 'new_m_ln2_b', 'new_v_w_ada', 'new_v_b_ada', 'new_v_w_in', 'new_v_b_gate', 'new_v_conv_a', 'new_v_a_log', 'new_v_dt_bias', 'new_v_norm_a', 'new_v_rel_bias', 'new_v_w_branch_a', 'new_v_w_branch_b', 'new_v_w_o', 'new_v_ln1_g', 'new_v_ln1_b', 'new_v_w_up', 'new_v_conv_ffn', 'new_v_b_conv_ffn', 'new_v_w_down', 'new_v_ln2_g', 'new_v_ln2_b']
TWIN_LEAF_KINDS = {'loss': 'loss', 'grad_x': 'grad_x', 'grad_w_ada': 'grad_w', 'grad_b_ada': 'grad_w', 'grad_w_in': 'grad_w', 'grad_b_gate': 'grad_w', 'grad_conv_a': 'grad_w', 'grad_a_log': 'grad_w', 'grad_dt_bias': 'grad_w', 'grad_norm_a': 'grad_w', 'grad_rel_bias': 'grad_w', 'grad_w_branch_a': 'grad_w', 'grad_w_branch_b': 'grad_w', 'grad_w_o': 'grad_w', 'grad_ln1_g': 'grad_w', 'grad_ln1_b': 'grad_w', 'grad_w_up': 'grad_w', 'grad_conv_ffn': 'grad_w', 'grad_b_conv_ffn': 'grad_w', 'grad_w_down': 'grad_w', 'grad_ln2_g': 'grad_w', 'grad_ln2_b': 'grad_w', 'delta_w_ada': 'delta_w', 'delta_b_ada': 'delta_w', 'delta_w_in': 'delta_w', 'delta_b_gate': 'delta_w', 'delta_conv_a': 'delta_w', 'delta_a_log': 'delta_w', 'delta_dt_bias': 'delta_w', 'delta_norm_a': 'delta_w', 'delta_rel_bias': 'delta_w', 'delta_w_branch_a': 'delta_w', 'delta_w_branch_b': 'delta_w', 'delta_w_o': 'delta_w', 'delta_ln1_g': 'delta_w', 'delta_ln1_b': 'delta_w', 'delta_w_up': 'delta_w', 'delta_conv_ffn': 'delta_w', 'delta_b_conv_ffn': 'delta_w', 'delta_w_down': 'delta_w', 'delta_ln2_g': 'delta_w', 'delta_ln2_b': 'delta_w', 'new_m_w_ada': 'new_m', 'new_m_b_ada': 'new_m', 'new_m_w_in': 'new_m', 'new_m_b_gate': 'new_m', 'new_m_conv_a': 'new_m', 'new_m_a_log': 'new_m', 'new_m_dt_bias': 'new_m', 'new_m_norm_a': 'new_m', 'new_m_rel_bias': 'new_m', 'new_m_w_branch_a': 'new_m', 'new_m_w_branch_b': 'new_m', 'new_m_w_o': 'new_m', 'new_m_ln1_g': 'new_m', 'new_m_ln1_b': 'new_m', 'new_m_w_up': 'new_m', 'new_m_conv_ffn': 'new_m', 'new_m_b_conv_ffn': 'new_m', 'new_m_w_down': 'new_m', 'new_m_ln2_g': 'new_m', 'new_m_ln2_b': 'new_m', 'new_v_w_ada': 'new_v', 'new_v_b_ada': 'new_v', 'new_v_w_in': 'new_v', 'new_v_b_gate': 'new_v', 'new_v_conv_a': 'new_v', 'new_v_a_log': 'new_v', 'new_v_dt_bias': 'new_v', 'new_v_norm_a': 'new_v', 'new_v_rel_bias': 'new_v', 'new_v_w_branch_a': 'new_v', 'new_v_w_branch_b': 'new_v', 'new_v_w_o': 'new_v', 'new_v_ln1_g': 'new_v', 'new_v_ln1_b': 'new_v', 'new_v_w_up': 'new_v', 'new_v_conv_ffn': 'new_v', 'new_v_b_conv_ffn': 'new_v', 'new_v_w_down': 'new_v', 'new_v_ln2_g': 'new_v', 'new_v_ln2_b': 'new_v'}


def _forward(args):
    return _fwd_reference(*[args[k] for k in FWD_PARAMS])


def _output_shape():
    def fwd():
        inp = _fwd_setup_inputs(0)
        return _fwd_reference(*[inp[k] for k in FWD_PARAMS])
    out = _jax.eval_shape(fwd)
    return out.shape, out.dtype

N_MICROBATCH = 1
ADAM_LR = 0.001
ADAM_B1 = 0.9
ADAM_B2 = 0.999
ADAM_EPS = 1e-08
ADAM_WD = 0.01
ADAM_STEP = 10
PER_EXAMPLE_BATCH_AXIS = {'x': 0, 'c': 0, 'loss_target': 0}
SHARED_INPUTS = []
_WEIGHT_DTYPES = {'w_ada': _jnp.float32, 'b_ada': _jnp.float32, 'w_in': _jnp.float32, 'b_gate': _jnp.float32, 'conv_a': _jnp.float32, 'a_log': _jnp.float32, 'dt_bias': _jnp.float32, 'norm_a': _jnp.float32, 'rel_bias': _jnp.float32, 'w_branch_a': _jnp.float32, 'w_branch_b': _jnp.float32, 'w_o': _jnp.float32, 'ln1_g': _jnp.float32, 'ln1_b': _jnp.float32, 'w_up': _jnp.float32, 'conv_ffn': _jnp.float32, 'b_conv_ffn': _jnp.float32, 'w_down': _jnp.float32, 'ln2_g': _jnp.float32, 'ln2_b': _jnp.float32}
MOMENT_SCALE = {'w_ada': 8.334643e-02, 'b_ada': 1.668746e-01, 'w_in': 2.551565e-02, 'b_gate': 1.596082e-02, 'conv_a': 2.714080e-02, 'a_log': 2.425891e-01, 'dt_bias': 2.136366e-01, 'norm_a': 1.113546e-01, 'rel_bias': 4.680259e-03, 'w_branch_a': 3.988187e-02, 'w_branch_b': 3.958062e-02, 'w_o': 9.524529e-02, 'ln1_g': 5.382071e+00, 'ln1_b': 1.027184e+00, 'w_up': 5.291975e-02, 'conv_ffn': 5.510150e-02, 'b_conv_ffn': 4.711894e-02, 'w_down': 1.469136e-01, 'ln2_g': 6.454012e+01, 'ln2_b': 3.263591e+00}


def _to_microbatches(a, axis):
    t = _jnp.moveaxis(a, axis, 0)
    t = t.reshape((N_MICROBATCH, t.shape[0] // N_MICROBATCH) + t.shape[1:])
    return _jnp.moveaxis(t, 1, axis + 1)


def setup_inputs(seed: int = 0) -> dict:
    inp = _fwd_setup_inputs(seed)
    key = _jax.random.fold_in(_jax.random.key(seed), 7919)
    shape, _ = _output_shape()
    out = dict(inp)
    out["loss_target"] = _jax.random.normal(_jax.random.fold_in(key, 0), shape, _jnp.float32)
    for i, name in enumerate(TWIN_WEIGHTS):
        w = inp[name].astype(_jnp.float32)
        if MOMENT_SCALE is None:
            s = _jnp.sqrt(_jnp.mean(_jnp.square(w)) + 1e-30)
        else:
            s = MOMENT_SCALE[name]
        km, kv = _jax.random.split(_jax.random.fold_in(key, i + 1))
        out[name] = w
        out["m_" + name] = s * _jax.random.normal(km, w.shape, _jnp.float32)
        out["v_" + name] = (s * s) * _jax.random.uniform(kv, w.shape, _jnp.float32, 0.5, 1.5)
    if N_MICROBATCH > 1:
        for name, axis in PER_EXAMPLE_BATCH_AXIS.items():
            out[name] = _to_microbatches(out[name], axis)
    return {'x': out['x'], 'c': out['c'], 'w_ada': out['w_ada'], 'b_ada': out['b_ada'], 'w_in': out['w_in'], 'b_gate': out['b_gate'], 'conv_a': out['conv_a'], 'a_log': out['a_log'], 'dt_bias': out['dt_bias'], 'norm_a': out['norm_a'], 'rel_bias': out['rel_bias'], 'w_branch_a': out['w_branch_a'], 'w_branch_b': out['w_branch_b'], 'w_o': out['w_o'], 'ln1_g': out['ln1_g'], 'ln1_b': out['ln1_b'], 'w_up': out['w_up'], 'conv_ffn': out['conv_ffn'], 'b_conv_ffn': out['b_conv_ffn'], 'w_down': out['w_down'], 'ln2_g': out['ln2_g'], 'ln2_b': out['ln2_b'], 'loss_target': out['loss_target'], 'm_w_ada': out['m_w_ada'], 'm_b_ada': out['m_b_ada'], 'm_w_in': out['m_w_in'], 'm_b_gate': out['m_b_gate'], 'm_conv_a': out['m_conv_a'], 'm_a_log': out['m_a_log'], 'm_dt_bias': out['m_dt_bias'], 'm_norm_a': out['m_norm_a'], 'm_rel_bias': out['m_rel_bias'], 'm_w_branch_a': out['m_w_branch_a'], 'm_w_branch_b': out['m_w_branch_b'], 'm_w_o': out['m_w_o'], 'm_ln1_g': out['m_ln1_g'], 'm_ln1_b': out['m_ln1_b'], 'm_w_up': out['m_w_up'], 'm_conv_ffn': out['m_conv_ffn'], 'm_b_conv_ffn': out['m_b_conv_ffn'], 'm_w_down': out['m_w_down'], 'm_ln2_g': out['m_ln2_g'], 'm_ln2_b': out['m_ln2_b'], 'v_w_ada': out['v_w_ada'], 'v_b_ada': out['v_b_ada'], 'v_w_in': out['v_w_in'], 'v_b_gate': out['v_b_gate'], 'v_conv_a': out['v_conv_a'], 'v_a_log': out['v_a_log'], 'v_dt_bias': out['v_dt_bias'], 'v_norm_a': out['v_norm_a'], 'v_rel_bias': out['v_rel_bias'], 'v_w_branch_a': out['v_w_branch_a'], 'v_w_branch_b': out['v_w_branch_b'], 'v_w_o': out['v_w_o'], 'v_ln1_g': out['v_ln1_g'], 'v_ln1_b': out['v_ln1_b'], 'v_w_up': out['v_w_up'], 'v_conv_ffn': out['v_conv_ffn'], 'v_b_conv_ffn': out['v_b_conv_ffn'], 'v_w_down': out['v_w_down'], 'v_ln2_g': out['v_ln2_g'], 'v_ln2_b': out['v_ln2_b']}


def _loss(weights, diff, rest, loss_target):
    with _jax.named_scope("forward"):
        args = {**rest, TWIN_DIFF_INPUT: diff, **{k: w.astype(_WEIGHT_DTYPES[k]) for k, w in weights.items()}}
        y = _forward(args)
    with _jax.named_scope("loss_head"):
        err = _jnp.square(y.astype(_jnp.float32) - loss_target)
        return 0.5 * _jnp.sum(_jnp.mean(err, axis=-1)) if err.ndim else 0.5 * err


def _adamw(w, g, m, v):
    m = ADAM_B1 * m + (1.0 - ADAM_B1) * g
    v = ADAM_B2 * v + (1.0 - ADAM_B2) * _jnp.square(g)
    m_hat = m / (1.0 - ADAM_B1 ** ADAM_STEP)
    v_hat = v / (1.0 - ADAM_B2 ** ADAM_STEP)
    delta = -ADAM_LR * (m_hat / (_jnp.sqrt(v_hat) + ADAM_EPS) + ADAM_WD * w)
    return delta, m, v


def reference(x, c, w_ada, b_ada, w_in, b_gate, conv_a, a_log, dt_bias, norm_a, rel_bias, w_branch_a, w_branch_b, w_o, ln1_g, ln1_b, w_up, conv_ffn, b_conv_ffn, w_down, ln2_g, ln2_b, loss_target, m_w_ada, m_b_ada, m_w_in, m_b_gate, m_conv_a, m_a_log, m_dt_bias, m_norm_a, m_rel_bias, m_w_branch_a, m_w_branch_b, m_w_o, m_ln1_g, m_ln1_b, m_w_up, m_conv_ffn, m_b_conv_ffn, m_w_down, m_ln2_g, m_ln2_b, v_w_ada, v_b_ada, v_w_in, v_b_gate, v_conv_a, v_a_log, v_dt_bias, v_norm_a, v_rel_bias, v_w_branch_a, v_w_branch_b, v_w_o, v_ln1_g, v_ln1_b, v_w_up, v_conv_ffn, v_b_conv_ffn, v_w_down, v_ln2_g, v_ln2_b):
    given = dict(x=x, c=c, w_ada=w_ada, b_ada=b_ada, w_in=w_in, b_gate=b_gate, conv_a=conv_a, a_log=a_log, dt_bias=dt_bias, norm_a=norm_a, rel_bias=rel_bias, w_branch_a=w_branch_a, w_branch_b=w_branch_b, w_o=w_o, ln1_g=ln1_g, ln1_b=ln1_b, w_up=w_up, conv_ffn=conv_ffn, b_conv_ffn=b_conv_ffn, w_down=w_down, ln2_g=ln2_g, ln2_b=ln2_b, loss_target=loss_target, m_w_ada=m_w_ada, m_b_ada=m_b_ada, m_w_in=m_w_in, m_b_gate=m_b_gate, m_conv_a=m_conv_a, m_a_log=m_a_log, m_dt_bias=m_dt_bias, m_norm_a=m_norm_a, m_rel_bias=m_rel_bias, m_w_branch_a=m_w_branch_a, m_w_branch_b=m_w_branch_b, m_w_o=m_w_o, m_ln1_g=m_ln1_g, m_ln1_b=m_ln1_b, m_w_up=m_w_up, m_conv_ffn=m_conv_ffn, m_b_conv_ffn=m_b_conv_ffn, m_w_down=m_w_down, m_ln2_g=m_ln2_g, m_ln2_b=m_ln2_b, v_w_ada=v_w_ada, v_b_ada=v_b_ada, v_w_in=v_w_in, v_b_gate=v_b_gate, v_conv_a=v_conv_a, v_a_log=v_a_log, v_dt_bias=v_dt_bias, v_norm_a=v_norm_a, v_rel_bias=v_rel_bias, v_w_branch_a=v_w_branch_a, v_w_branch_b=v_w_branch_b, v_w_o=v_w_o, v_ln1_g=v_ln1_g, v_ln1_b=v_ln1_b, v_w_up=v_w_up, v_conv_ffn=v_conv_ffn, v_b_conv_ffn=v_b_conv_ffn, v_w_down=v_w_down, v_ln2_g=v_ln2_g, v_ln2_b=v_ln2_b)
    weights = {n: given[n] for n in TWIN_WEIGHTS}
    shared = {n: given[n] for n in SHARED_INPUTS}
    per_example = {n: given[n] for n in ['x', 'c']}
    grad_fn = _jax.value_and_grad(_loss, argnums=(0, 1))

    def one_microbatch(ex, loss_target):
        ex = dict(ex)
        diff = ex.pop(TWIN_DIFF_INPUT)
        return grad_fn(weights, diff, {**shared, **ex}, loss_target)

    if N_MICROBATCH == 1:
        loss, (grad_w, grad_x) = one_microbatch(per_example, given["loss_target"])
    else:
        def body(carry, xs):
            loss_sum, grad_sum = carry
            l_k, (gw_k, gx_k) = one_microbatch(xs[0], xs[1])
            with _jax.named_scope("update"):
                return (loss_sum + l_k, _jax.tree.map(_jnp.add, grad_sum, gw_k)), gx_k

        init = (_jnp.zeros((), _jnp.float32), _jax.tree.map(_jnp.zeros_like, weights))
        (loss, grad_w), grad_x = _jax.lax.scan(body, init, (per_example, given["loss_target"]))
    with _jax.named_scope("update"):
        delta_w, new_m, new_v = {}, {}, {}
        for n in TWIN_WEIGHTS:
            delta_w[n], new_m[n], new_v[n] = _adamw(weights[n], grad_w[n], given["m_" + n], given["v_" + n])
    return (loss, grad_x, *[grad_w[n] for n in TWIN_WEIGHTS], *[delta_w[n] for n in TWIN_WEIGHTS],
            *[new_m[n] for n in TWIN_WEIGHTS], *[new_v[n] for n in TWIN_WEIGHTS])
```

```python
import functools
import math

import numpy as np
import jax
import jax.numpy as jnp
from jax import lax
from jax.experimental import pallas as pl
from jax.experimental.pallas import tpu as pltpu

F32 = jnp.float32
BF16 = jnp.bfloat16
HI = lax.Precision.HIGHEST

D = 1024
CH = 64
AH, ADK = 8, 128
BH, BDH = 16, 64
BPREV = 8
BMAXREL = 256
RELSZ = CH + BMAXREL
DFF = 2816
ALPHA = 2.0 ** 0.25
LN_EPS, RMS_EPS, L2_EPS = 1e-5, 1e-6, 1e-6
NEG = -1e30
LR, B1, B2, AEPS, WD, STEP = 1e-3, 0.9, 0.999, 1e-8, 0.01, 10
NDEV = 8
HALO = 8
TQ = 512
VMEM_LIMIT = 56 * 1024 * 1024

C_QKVA, C_Z, C_QKVB, C_GATE, C_BA, NCAT = 0, 3072, 4096, 7168, 9216, 9728


def _cparams(n_axes=1, vmem=VMEM_LIMIT):
    return pltpu.CompilerParams(dimension_semantics=("arbitrary",) * n_axes, vmem_limit_bytes=vmem)


def _dg(a, b, ca, cb):
    return lax.dot_general(a.astype(BF16), b.astype(BF16), (((ca,), (cb,)), ((), ())),
                           preferred_element_type=F32)


@jax.custom_vjp
def mm_nn(a, b):
    return _dg(a, b, 1, 0)


@jax.custom_vjp
def mm_nt(a, b):
    return _dg(a, b, 1, 1)


@jax.custom_vjp
def mm_tn(a, b):
    return _dg(a, b, 0, 0)


mm_nn.defvjp(lambda a, b: (mm_nn(a, b), (a, b)),
             lambda r, g: (mm_nt(g, r[1]).astype(r[0].dtype), mm_tn(r[0], g).astype(r[1].dtype)))
mm_nt.defvjp(lambda a, b: (mm_nt(a, b), (a, b)),
             lambda r, g: (mm_nn(g, r[1]).astype(r[0].dtype), mm_tn(g, r[0]).astype(r[1].dtype)))
mm_tn.defvjp(lambda a, b: (mm_tn(a, b), (a, b)),
             lambda r, g: (mm_nt(r[1], g).astype(r[0].dtype), mm_nn(r[0], g).astype(r[1].dtype)))


@jax.custom_vjp
def mm_w(a, w):
    return _dg(a, w, 1, 0)


mm_w.defvjp(lambda a, w: (mm_w(a, w), (a, w)),
            lambda r, g: (mm_nt(g, r[1]).astype(r[0].dtype), jnp.zeros_like(r[1])))


def _mmh(a, b):
    return lax.dot_general(a, b, (((1,), (0,)), ((), ())), precision=HI, preferred_element_type=F32)


def _sigmoid(x):
    return 1.0 / (1.0 + jnp.exp(-x))


def _silu(x):
    return x * _sigmoid(x)


def _softplus(x):
    return jnp.maximum(x, 0.0) + jnp.log(1.0 + jnp.exp(-jnp.abs(x)))


def _layernorm(r, g, b):
    mu = jnp.mean(r, axis=-1, keepdims=True)
    xc = r - mu
    var = jnp.mean(xc * xc, axis=-1, keepdims=True)
    return xc * lax.rsqrt(var + LN_EPS) * g + b


def _iota2(shape, dim):
    return lax.broadcasted_iota(jnp.int32, shape, dim)


@jax.custom_vjp
def causal_conv(ext, rows):
    k = len(rows)
    y = None
    for j in range(k):
        s = k - 1 - j
        r = pltpu.roll(ext, s, 0) if s else ext
        t = r[HALO:] * rows[j]
        y = t if y is None else y + t
    return y


def _causal_conv_fwd(ext, rows):
    return causal_conv(ext, rows), (ext, rows)


def _causal_conv_bwd(res, g):
    ext, rows = res
    n = ext.shape[0]
    k = len(rows)
    gext = jnp.concatenate([jnp.zeros((HALO, g.shape[1]), g.dtype), g], axis=0)
    dext = None
    drows = []
    for j in range(k):
        s = k - 1 - j
        up = pltpu.roll(gext, n - s, 0) if s else gext
        t = up * rows[j]
        dext = t if dext is None else dext + t
        r = pltpu.roll(ext, s, 0) if s else ext
        drows.append(jnp.sum(g * r[HALO:], axis=0, keepdims=True))
    return dext, tuple(drows)


causal_conv.defvjp(_causal_conv_fwd, _causal_conv_bwd)


def _chunk_masks(tm):
    i = _iota2((tm, tm), 0)
    j = _iota2((tm, tm), 1)
    same = (i ^ j) < CH
    lower = jnp.where(same & (j <= i), 1.0, 0.0).astype(F32)
    upper = jnp.where(same & (i <= j), 1.0, 0.0).astype(F32)
    return lower, upper


@jax.custom_vjp
def chunk_cumsum(g):
    lower, _ = _chunk_masks(g.shape[0])
    return _mmh(lower, g)


def _chunk_cumsum_bwd(_, ct):
    _, upper = _chunk_masks(ct.shape[0])
    return (_mmh(upper, ct),)


chunk_cumsum.defvjp(lambda g: (chunk_cumsum(g), None), _chunk_cumsum_bwd)


@jax.custom_vjp
def inv_unit_lower(a):
    n = a.shape[0]
    eye = jnp.where(_iota2((n, n), 0) == _iota2((n, n), 1), 1.0, 0.0).astype(F32)
    x = eye - a
    p = _mmh(a, a)
    steps = int(math.log2(n)) - 1
    for s in range(steps):
        x = x + _mmh(x, p)
        if s + 1 < steps:
            p = _mmh(p, p)
    return x


def _inv_fwd(a):
    t = inv_unit_lower(a)
    return t, t


def _inv_bwd(t, g):
    return (-(_mmh(_mmh(t, g.T), t)).T,)


inv_unit_lower.defvjp(_inv_fwd, _inv_bwd)


def prep_fn(ext, bb, aa, rows, a_log, dtb):
    s = _silu(causal_conv(ext, rows))
    qs, ks = [], []
    for h in range(AH):
        qh = s[:, h * ADK:(h + 1) * ADK]
        kh = s[:, D + h * ADK:D + (h + 1) * ADK]
        qs.append(qh * lax.rsqrt(jnp.sum(qh * qh, axis=-1, keepdims=True) + L2_EPS) * (ADK ** -0.5))
        ks.append(kh * lax.rsqrt(jnp.sum(kh * kh, axis=-1, keepdims=True) + L2_EPS))
    q = jnp.concatenate(qs, axis=1)
    k = jnp.concatenate(ks, axis=1)
    v = s[:, 2 * D:]
    beta = _sigmoid(bb)
    g = -jnp.exp(a_log) * _softplus(aa + dtb)
    return q, k, v, chunk_cumsum(g), beta


def c1_head(q, k, v, gcol, grow, bcol):
    i = _iota2((CH, CH), 0)
    j = _iota2((CH, CH), 1)
    causal = j <= i
    strict = j < i
    diff = gcol - grow
    decay = jnp.where(causal, jnp.exp(jnp.where(causal, diff, 0.0)), 0.0)
    kb = k * bcol
    vb = v * bcol
    a_low = jnp.where(strict, mm_nt(kb, k) * decay, 0.0)
    tinv = inv_unit_lower(a_low)
    egc = jnp.exp(gcol)
    u = _mmh(tinv, vb)
    w = _mmh(tinv, kb * egc)
    qk = jnp.where(causal, mm_nt(q, k) * decay, 0.0)
    glast = jnp.sum(jnp.where(_iota2((CH, 1), 0) == CH - 1, gcol, 0.0), axis=0, keepdims=True)
    qg = q * egc
    kd = k * jnp.exp(glast - gcol)
    eg = jnp.exp(glast) * jnp.ones((1, ADK), F32)
    return u, w, qk, qg, kd, eg


def c2_head(s, u, w, qk, qg, kd, eg, z, nw):
    vn = u - mm_nn(w, s)
    o = mm_nn(qg, s) + mm_nn(qk, vn)
    s2 = s * eg + mm_tn(kd, vn)
    ms = jnp.mean(o * o, axis=-1, keepdims=True)
    og = o * lax.rsqrt(ms + RMS_EPS) * nw * _silu(z)
    return og, s2


def attn_pair(q, kp, kc, vp, vc, bias2, firstf):
    k = jnp.concatenate([kp, kc], axis=0)
    v = jnp.concatenate([vp, vc], axis=0)
    lane = _iota2((1, 2 * BDH), 1)
    col = _iota2((1, 2 * TQ), 1)
    nokey = jnp.where(col < TQ, firstf, 0.0) * NEG
    out = None
    for hh in range(2):
        hm = jnp.where((lane >= hh * BDH) & (lane < (hh + 1) * BDH), 1.0, 0.0).astype(F32)
        s = mm_nt(q * hm, k) * (BDH ** -0.5) + bias2[hh] + nokey
        m = lax.stop_gradient(jnp.max(s, axis=-1, keepdims=True))
        p = jnp.exp(s - m)
        p = p / jnp.sum(p, axis=-1, keepdims=True)
        o = mm_nn(p, v) * hm
        out = o if out is None else out + o
    return out


def merge_fn(x, oa, ob, gra, grb, p_pa, p_pb, p_mix, bga, bgb, gate_t, g1, b1, scale_f, shift_f,
             wa, wb, wo):
    ga = _sigmoid(gra + bga)
    gb = _sigmoid(grb + bgb)
    pa = mm_w(oa, wa) + p_pa
    pb = mm_w(ob, wb) + p_pb
    merged = ga * pa + gb * pb
    mix = mm_w(merged, wo) + p_mix
    y1 = _layernorm(ALPHA * x + gate_t * mix, g1, b1)
    return y1, merged


def ffn_act_fn(ext, rows, bconv):
    u = causal_conv(ext, rows) + bconv
    return _silu(u[:, :DFF]) * u[:, DFF:]


def head_fn(a, y1, p_ffn, gate_f, g2, b2, tgt, wd):
    ffn = mm_w(a, wd) + p_ffn
    y2 = _layernorm(ALPHA * y1 + gate_f * ffn, g2, b2)
    err = y2 - tgt
    return 0.5 * jnp.sum(jnp.mean(err * err, axis=-1, keepdims=True))


def _rows(tm, width, colblk=0, order=None):
    if order is None:
        return pl.BlockSpec((tm, width), lambda i: (i, colblk))
    return pl.BlockSpec((tm, width), lambda i: (order(i), colblk))


def _const(shape):
    nd = len(shape)
    return pl.BlockSpec(shape, lambda *_: (0,) * nd)


def _pick(n, cands):
    for c in cands:
        if n % c == 0:
            return c
    raise ValueError(f"no tile for {n}")


def _onehot_rows(k, j):
    return jnp.where(_iota2((k, 1), 0) == j, 1.0, 0.0).astype(F32)


def matmul(a, w, out_dtype, name):
    m, kdim = a.shape
    _, n = w.shape
    tm = _pick(m, (512, 256, 128))
    tn = _pick(n, (1024, 512, 256, 128))
    tk = _pick(kdim, (1024, 512, 256, 128))
    nk = kdim // tk

    def body(a_ref, w_ref, o_ref, *scratch):
        p = jnp.dot(a_ref[...].astype(BF16), w_ref[...].astype(BF16), preferred_element_type=F32)
        if nk == 1:
            o_ref[...] = p.astype(out_dtype)
            return
        acc = scratch[0]
        k = pl.program_id(2)

        @pl.when(k == 0)
        def _():
            acc[...] = p

        @pl.when(k > 0)
        def _():
            acc[...] += p

        @pl.when(k == nk - 1)
        def _():
            o_ref[...] = acc[...].astype(out_dtype)

    return pl.pallas_call(
        body, name=name,
        grid=(m // tm, n // tn, nk),
        in_specs=[pl.BlockSpec((tm, tk), lambda i, j, k: (i, k)),
                  pl.BlockSpec((tk, tn), lambda i, j, k: (k, j))],
        out_specs=pl.BlockSpec((tm, tn), lambda i, j, k: (i, j)),
        out_shape=jax.ShapeDtypeStruct((m, n), out_dtype),
        scratch_shapes=[] if nk == 1 else [pltpu.VMEM((tm, tn), F32)],
        compiler_params=_cparams(3),
    )(a, w)


def modulate(x, scale, shift, name):
    t, d = x.shape
    tm = _pick(t, (512, 256, 128))

    def body(x_ref, sc_ref, sh_ref, o_ref):
        o_ref[...] = (x_ref[...] * (1.0 + sc_ref[...]) + sh_ref[...]).astype(BF16)

    return pl.pallas_call(
        body, name=name, grid=(t // tm,),
        in_specs=[_rows(tm, d), _const((1, d)), _const((1, d))],
        out_specs=_rows(tm, d),
        out_shape=jax.ShapeDtypeStruct((t, d), BF16),
        compiler_params=_cparams(),
    )(x, scale, shift)


def modulate_bwd(dh, xin, dres, scale, name):
    t, d = dh.shape
    tm = _pick(t, (512, 256, 128))

    def body(dh_ref, x_ref, dr_ref, sc_ref, o_ref, dsc_ref, dsh_ref):
        i = pl.program_id(0)
        dh_v = dh_ref[...]
        o_ref[...] = dr_ref[...] + dh_v * (1.0 + sc_ref[...])

        @pl.when(i == 0)
        def _():
            dsc_ref[...] = jnp.zeros_like(dsc_ref)
            dsh_ref[...] = jnp.zeros_like(dsh_ref)

        dsc_ref[...] += jnp.sum(dh_v * x_ref[...], axis=0, keepdims=True)
        dsh_ref[...] += jnp.sum(dh_v, axis=0, keepdims=True)

    return pl.pallas_call(
        body, name=name, grid=(t // tm,),
        in_specs=[_rows(tm, d), _rows(tm, d), _rows(tm, d), _const((1, d))],
        out_specs=[_rows(tm, d), _const((1, d)), _const((1, d))],
        out_shape=[jax.ShapeDtypeStruct((t, d), F32), jax.ShapeDtypeStruct((1, d), F32),
                   jax.ShapeDtypeStruct((1, d), F32)],
        compiler_params=_cparams(),
    )(dh, xin, dres, scale)


PREP_TM = 128


def _halo_specs(tm, width, colblk, order):
    per = tm // HALO
    return [pl.BlockSpec((HALO, width), lambda i: (jnp.maximum(order(i) * per - 1, 0), colblk)),
            pl.BlockSpec((tm, width), lambda i: (order(i), colblk))]


def prep_fwd(proj, conv_a, a_log, dtb):
    t = proj.shape[0]
    tm = PREP_TM
    nt = t // tm
    wq = 3 * D

    def body(prev_ref, cur_ref, bb_ref, aa_ref, cw_ref, al_ref, dt_ref, q_ref, k_ref, v_ref, g_ref, b_ref):
        i = pl.program_id(0)
        flag = jnp.where(i > 0, 1.0, 0.0)
        ext = jnp.concatenate([prev_ref[...] * flag, cur_ref[...]], axis=0)
        rows = tuple(cw_ref[j:j + 1, :] for j in range(4))
        q, k, v, gcs, beta = prep_fn(ext, bb_ref[...], aa_ref[...], rows, al_ref[...], dt_ref[...])
        q_ref[...] = q
        k_ref[...] = k
        v_ref[...] = v
        g_ref[...] = gcs
        b_ref[...] = beta

    ident = lambda i: i
    return pl.pallas_call(
        body, name="prep_fwd", grid=(nt,),
        in_specs=_halo_specs(tm, wq, 0, ident) + [
            _rows(tm, 128, C_BA // 128), _rows(tm, 128, C_BA // 128 + 1),
            _const((4, wq)), _const((1, 128)), _const((1, 128))],
        out_specs=[_rows(tm, D), _rows(tm, D), _rows(tm, D), _rows(tm, 128), _rows(tm, 128)],
        out_shape=[jax.ShapeDtypeStruct((t, D), F32)] * 3 + [jax.ShapeDtypeStruct((t, 128), F32)] * 2,
        compiler_params=_cparams(),
    )(proj, proj, proj, proj, conv_a, a_log, dtb)


def prep_bwd(proj, conv_a, a_log, dtb, dq, dk, dv, dgcs, dbeta):
    t = proj.shape[0]
    tm = PREP_TM
    nt = t // tm
    wq = 3 * D
    rev = lambda i: nt - 1 - i

    def body(prev_ref, cur_ref, bb_ref, aa_ref, cw_ref, al_ref, dt_ref,
             dq_ref, dk_ref, dv_ref, dg_ref, db_ref,
             dpre_ref, dbb_ref, daa_ref, dcw_ref, dal_ref, ddt_ref, carry):
        i = pl.program_id(0)
        flag = jnp.where(i < nt - 1, 1.0, 0.0)
        ext = jnp.concatenate([prev_ref[...] * flag, cur_ref[...]], axis=0)
        rows = tuple(cw_ref[j:j + 1, :] for j in range(4))
        _, vjp = jax.vjp(prep_fn, ext, bb_ref[...], aa_ref[...], rows, al_ref[...], dt_ref[...])
        dext, dbb, daa, drows, dal, ddt = vjp((dq_ref[...], dk_ref[...], dv_ref[...], dg_ref[...], db_ref[...]))

        @pl.when(i == 0)
        def _():
            carry[...] = jnp.zeros_like(carry)
            dcw_ref[...] = jnp.zeros_like(dcw_ref)
            dal_ref[...] = jnp.zeros_like(dal_ref)
            ddt_ref[...] = jnp.zeros_like(ddt_ref)

        dcur = dext[HALO:]
        dpre_ref[...] = jnp.concatenate([dcur[:tm - HALO], dcur[tm - HALO:] + carry[...]], axis=0).astype(BF16)
        carry[...] = dext[:HALO]
        dbb_ref[...] = dbb.astype(BF16)
        daa_ref[...] = daa.astype(BF16)
        dcw = None
        for j in range(4):
            tj = _onehot_rows(4, j) * drows[j]
            dcw = tj if dcw is None else dcw + tj
        dcw_ref[...] += dcw
        dal_ref[...] += dal
        ddt_ref[...] += ddt

    return pl.pallas_call(
        body, name="prep_bwd", grid=(nt,),
        in_specs=_halo_specs(tm, wq, 0, rev) + [
            _rows(tm, 128, C_BA // 128, rev), _rows(tm, 128, C_BA // 128 + 1, rev),
            _const((4, wq)), _const((1, 128)), _const((1, 128)),
            _rows(tm, D, 0, rev), _rows(tm, D, 0, rev), _rows(tm, D, 0, rev),
            _rows(tm, 128, 0, rev), _rows(tm, 128, 0, rev)],
        out_specs=[_rows(tm, wq, 0, rev), _rows(tm, 128, 0, rev), _rows(tm, 128, 0, rev),
                   _const((4, wq)), _const((1, 128)), _const((1, 128))],
        out_shape=[jax.ShapeDtypeStruct((t, wq), BF16), jax.ShapeDtypeStruct((t, 128), BF16),
                   jax.ShapeDtypeStruct((t, 128), BF16), jax.ShapeDtypeStruct((4, wq), F32),
                   jax.ShapeDtypeStruct((1, 128), F32), jax.ShapeDtypeStruct((1, 128), F32)],
        scratch_shapes=[pltpu.VMEM((HALO, wq), F32)],
        compiler_params=_cparams(),
    )(proj, proj, proj, proj, conv_a, a_log, dtb, dq, dk, dv, dgcs, dbeta)


def _c1_specs(order):
    tok = pl.BlockSpec((CH, D), lambda n: (order(n), 0))
    col = pl.BlockSpec((AH, 1, CH, 1), lambda n: (0, order(n), 0, 0))
    row = pl.BlockSpec((AH, 1, 1, CH), lambda n: (0, order(n), 0, 0))
    qk = pl.BlockSpec((1, AH, CH, CH), lambda n: (order(n), 0, 0, 0))
    eg = pl.BlockSpec((1, AH, 1, ADK), lambda n: (order(n), 0, 0, 0))
    return tok, col, row, qk, eg


def c1_fwd(q, k, v, gcol, grow, bcol):
    t = q.shape[0]
    nc = t // CH
    tok, col, row, qks, egs = _c1_specs(lambda n: n)

    def body(q_ref, k_ref, v_ref, gc_ref, gr_ref, bc_ref, u_ref, w_ref, qg_ref, kd_ref, qk_ref, eg_ref):
        for h in range(AH):
            sl = slice(h * ADK, (h + 1) * ADK)
            u, w, qk, qg, kd, eg = c1_head(q_ref[:, sl], k_ref[:, sl], v_ref[:, sl],
                                           gc_ref[h, 0], gr_ref[h, 0], bc_ref[h, 0])
            u_ref[:, sl] = u
            w_ref[:, sl] = w
            qg_ref[:, sl] = qg
            kd_ref[:, sl] = kd
            qk_ref[0, h] = qk
            eg_ref[0, h] = eg

    return pl.pallas_call(
        body, name="c1_fwd", grid=(nc,),
        in_specs=[tok, tok, tok, col, row, col],
        out_specs=[tok, tok, tok, tok, qks, egs],
        out_shape=[jax.ShapeDtypeStruct((t, D), F32)] * 4 + [
            jax.ShapeDtypeStruct((nc, AH, CH, CH), F32), jax.ShapeDtypeStruct((nc, AH, 1, ADK), F32)],
        compiler_params=_cparams(),
    )(q, k, v, gcol, grow, bcol)


def c1_bwd(q, k, v, gcol, grow, bcol, du, dw, dqg, dkd, dqk, deg):
    t = q.shape[0]
    nc = t // CH
    tok, col, row, qks, egs = _c1_specs(lambda n: n)

    def body(q_ref, k_ref, v_ref, gc_ref, gr_ref, bc_ref, du_ref, dw_ref, dqg_ref, dkd_ref, dqk_ref, deg_ref,
             dq_ref, dk_ref, dv_ref, dgc_ref, dgr_ref, dbc_ref):
        for h in range(AH):
            sl = slice(h * ADK, (h + 1) * ADK)
            _, vjp = jax.vjp(c1_head, q_ref[:, sl], k_ref[:, sl], v_ref[:, sl],
                             gc_ref[h, 0], gr_ref[h, 0], bc_ref[h, 0])
            dq, dk, dv, dgc, dgr, dbc = vjp((du_ref[:, sl], dw_ref[:, sl], dqk_ref[0, h],
                                             dqg_ref[:, sl], dkd_ref[:, sl], deg_ref[0, h]))
            dq_ref[:, sl] = dq
            dk_ref[:, sl] = dk
            dv_ref[:, sl] = dv
            dgc_ref[h, 0] = dgc
            dgr_ref[h, 0] = dgr
            dbc_ref[h, 0] = dbc

    return pl.pallas_call(
        body, name="c1_bwd", grid=(nc,),
        in_specs=[tok, tok, tok, col, row, col, tok, tok, tok, tok, qks, egs],
        out_specs=[tok, tok, tok, col, row, col],
        out_shape=[jax.ShapeDtypeStruct((t, D), F32)] * 3 + [
            jax.ShapeDtypeStruct((AH, nc, CH, 1), F32), jax.ShapeDtypeStruct((AH, nc, 1, CH), F32),
            jax.ShapeDtypeStruct((AH, nc, CH, 1), F32)],
        compiler_params=_cparams(),
    )(q, k, v, gcol, grow, bcol, du, dw, dqg, dkd, dqk, deg)


def c2_fwd(u, w, qg, kd, qk, eg, proj, norm_a):
    t = u.shape[0]
    nc = t // CH
    tok, _, _, qks, egs = _c1_specs(lambda n: n)
    zspec = pl.BlockSpec((CH, D), lambda n: (n, C_Z // D))
    sspec = pl.BlockSpec((1, AH, ADK, ADK), lambda n: (n, 0, 0, 0))

    def body(u_ref, w_ref, qg_ref, kd_ref, qk_ref, eg_ref, z_ref, nw_ref, o_ref, sall_ref, st):
        n = pl.program_id(0)

        @pl.when(n == 0)
        def _():
            st[...] = jnp.zeros_like(st)

        for h in range(AH):
            sl = slice(h * ADK, (h + 1) * ADK)
            s = st[h]
            sall_ref[0, h] = s
            og, s2 = c2_head(s, u_ref[:, sl], w_ref[:, sl], qk_ref[0, h], qg_ref[:, sl], kd_ref[:, sl],
                             eg_ref[0, h], z_ref[:, sl], nw_ref[...])
            o_ref[:, sl] = og.astype(BF16)
            st[h] = s2

    return pl.pallas_call(
        body, name="c2_fwd", grid=(nc,),
        in_specs=[tok, tok, tok, tok, qks, egs, zspec, _const((1, ADK))],
        out_specs=[tok, sspec],
        out_shape=[jax.ShapeDtypeStruct((t, D), BF16), jax.ShapeDtypeStruct((nc, AH, ADK, ADK), F32)],
        scratch_shapes=[pltpu.VMEM((AH, ADK, ADK), F32)],
        compiler_params=_cparams(),
    )(u, w, qg, kd, qk, eg, proj, norm_a)


def c2_bwd(u, w, qg, kd, qk, eg, proj, norm_a, sall, do):
    t = u.shape[0]
    nc = t // CH
    rev = lambda n: nc - 1 - n
    tok, _, _, qks, egs = _c1_specs(rev)
    zspec = pl.BlockSpec((CH, D), lambda n: (rev(n), C_Z // D))
    sspec = pl.BlockSpec((1, AH, ADK, ADK), lambda n: (rev(n), 0, 0, 0))

    def body(u_ref, w_ref, qg_ref, kd_ref, qk_ref, eg_ref, z_ref, nw_ref, sall_ref, do_ref,
             du_ref, dw_ref, dqg_ref, dkd_ref, dqk_ref, deg_ref, dz_ref, dnw_ref, dst):
        n = pl.program_id(0)

        @pl.when(n == 0)
        def _():
            dst[...] = jnp.zeros_like(dst)
            dnw_ref[...] = jnp.zeros_like(dnw_ref)

        dnw = None
        for h in range(AH):
            sl = slice(h * ADK, (h + 1) * ADK)
            _, vjp = jax.vjp(c2_head, sall_ref[0, h], u_ref[:, sl], w_ref[:, sl], qk_ref[0, h], qg_ref[:, sl],
                             kd_ref[:, sl], eg_ref[0, h], z_ref[:, sl], nw_ref[...])
            ds, du, dw, dqk, dqg, dkd, deg, dz, dn = vjp((do_ref[:, sl], dst[h]))
            dst[h] = ds
            du_ref[:, sl] = du
            dw_ref[:, sl] = dw
            dqg_ref[:, sl] = dqg
            dkd_ref[:, sl] = dkd
            dqk_ref[0, h] = dqk
            deg_ref[0, h] = deg
            dz_ref[:, sl] = dz.astype(BF16)
            dnw = dn if dnw is None else dnw + dn
        dnw_ref[...] += dnw

    return pl.pallas_call(
        body, name="c2_bwd", grid=(nc,),
        in_specs=[tok, tok, tok, tok, qks, egs, zspec, _const((1, ADK)), sspec, tok],
        out_specs=[tok, tok, tok, tok, qks, egs, tok, _const((1, ADK))],
        out_shape=[jax.ShapeDtypeStruct((t, D), F32)] * 4 + [
            jax.ShapeDtypeStruct((nc, AH, CH, CH), F32), jax.ShapeDtypeStruct((nc, AH, 1, ADK), F32),
            jax.ShapeDtypeStruct((t, D), BF16), jax.ShapeDtypeStruct((1, ADK), F32)],
        scratch_shapes=[pltpu.VMEM((AH, ADK, ADK), F32)],
        compiler_params=_cparams(),
    )(u, w, qg, kd, qk, eg, proj, norm_a, sall, do)


def _band_mask_np():
    a = np.arange(TQ)[:, None]
    b = np.arange(2 * TQ)[None, :]
    dchunk = (TQ // CH + a // CH) - b // CH
    return (dchunk >= 0) & (dchunk <= BPREV)


def bias_tiles(rel_bias):
    nh = rel_bias.shape[0]
    lo = 2 * TQ - 1 - (TQ + CH - 1)
    ln = 3 * TQ - 1
    tvec = jnp.concatenate([jnp.broadcast_to(rel_bias[:, :1], (nh, lo)), rel_bias,
                            jnp.broadcast_to(rel_bias[:, -1:], (nh, ln - lo - RELSZ + 1))], axis=1)
    flat = jnp.tile(tvec, (1, TQ + 1))[:, :TQ * (3 * TQ + 1)]
    r = flat.reshape(nh, TQ, 3 * TQ + 1)[:, :, :2 * TQ]
    m = r[:, :, ::-1]
    return jnp.where(jnp.asarray(_band_mask_np())[None], m, NEG)


def bias_tiles_bwd_layout(dbias):
    nh = dbias.shape[0]
    d = jnp.where(jnp.asarray(_band_mask_np())[None], dbias, 0.0)[:, :, ::-1]
    d = jnp.pad(d, ((0, 0), (0, 0), (0, TQ + 1)))
    flat = jnp.pad(d.reshape(nh, TQ * (3 * TQ + 1)), ((0, 0), (0, (TQ + 1) * 3 * TQ - TQ * (3 * TQ + 1))))
    return flat.reshape(nh, TQ + 1, 3 * TQ)


def _fold_matrix_np():
    lo = 2 * TQ - 1 - (TQ + CH - 1)
    f = np.zeros((3 * TQ, 384), np.float32)
    for mcol in range(3 * TQ - 1):
        f[mcol, min(max(mcol - lo, 0), RELSZ - 1)] = 1.0
    return f


def relbias_reduce(dlay):
    nh, rows, cols = dlay.shape
    rpad = (-rows) % 8
    dlay = jnp.pad(dlay, ((0, 0), (0, rpad), (0, 0)))
    fold = jnp.asarray(_fold_matrix_np())

    def body(d_ref, f_ref, o_ref):
        cs = jnp.sum(d_ref[0], axis=0, keepdims=True)
        o_ref[0] = _mmh(jnp.broadcast_to(cs, (8, cols)), f_ref[...])

    out = pl.pallas_call(
        body, name="relbias_reduce", grid=(nh,),
        in_specs=[pl.BlockSpec((1, rows + rpad, cols), lambda h: (h, 0, 0)), _const((cols, 384))],
        out_specs=pl.BlockSpec((1, 8, 384), lambda h: (h, 0, 0)),
        out_shape=jax.ShapeDtypeStruct((nh, 8, 384), F32),
        compiler_params=_cparams(),
    )(dlay, fold)
    return out[:, 0, :RELSZ]


def attn_fwd(proj, bias):
    t = proj.shape[0]
    nt = t // TQ
    cb = C_QKVB // 128

    def body(q_ref, kp_ref, kc_ref, vp_ref, vc_ref, b_ref, o_ref):
        i = pl.program_id(1)
        firstf = jnp.where(i == 0, 1.0, 0.0)
        o_ref[...] = attn_pair(q_ref[...], kp_ref[...], kc_ref[...], vp_ref[...], vc_ref[...],
                               b_ref[...], firstf).astype(BF16)

    def blk(off, prev):
        if prev:
            return pl.BlockSpec((TQ, 128), lambda p, i: (jnp.maximum(i - 1, 0), cb + off + p))
        return pl.BlockSpec((TQ, 128), lambda p, i: (i, cb + off + p))

    return pl.pallas_call(
        body, name="attn_fwd", grid=(BH // 2, nt),
        in_specs=[blk(0, False), blk(8, True), blk(8, False), blk(16, True), blk(16, False),
                  pl.BlockSpec((2, TQ, 2 * TQ), lambda p, i: (p, 0, 0))],
        out_specs=pl.BlockSpec((TQ, 128), lambda p, i: (i, p)),
        out_shape=jax.ShapeDtypeStruct((t, D), BF16),
        compiler_params=_cparams(2),
    )(proj, proj, proj, proj, proj, bias)


def attn_bwd(proj, bias, do):
    t = proj.shape[0]
    nt = t // TQ
    cb = C_QKVB // 128

    def body(q_ref, kp_ref, kc_ref, vp_ref, vc_ref, b_ref, do_ref,
             dq_ref, dk_ref, dv_ref, db_ref, ck, cv):
        i = pl.program_id(1)

        @pl.when(i == 0)
        def _():
            ck[...] = jnp.zeros_like(ck)
            cv[...] = jnp.zeros_like(cv)
            db_ref[...] = jnp.zeros_like(db_ref)

        @pl.when(i < nt)
        def _():
            firstf = jnp.where(i == 0, 1.0, 0.0)
            _, vjp = jax.vjp(lambda q, kp, kc, vp, vc, b: attn_pair(q, kp, kc, vp, vc, b, firstf),
                             q_ref[...].astype(F32), kp_ref[...].astype(F32), kc_ref[...].astype(F32),
                             vp_ref[...].astype(F32), vc_ref[...].astype(F32), b_ref[...])
            dq, dkp, dkc, dvp, dvc, db = vjp(do_ref[...])
            dq_ref[...] = dq.astype(BF16)
            dk_ref[...] = (ck[...] + dkp).astype(BF16)
            dv_ref[...] = (cv[...] + dvp).astype(BF16)
            ck[...] = dkc
            cv[...] = dvc
            db_ref[...] += db

        @pl.when(i == nt)
        def _():
            dk_ref[...] = ck[...].astype(BF16)
            dv_ref[...] = cv[...].astype(BF16)

    def blk(off, prev):
        if prev:
            return pl.BlockSpec((TQ, 128), lambda p, i: (jnp.clip(i - 1, 0, nt - 1), cb + off + p))
        return pl.BlockSpec((TQ, 128), lambda p, i: (jnp.minimum(i, nt - 1), cb + off + p))

    own = pl.BlockSpec((TQ, 128), lambda p, i: (jnp.minimum(i, nt - 1), p))
    lag = pl.BlockSpec((TQ, 128), lambda p, i: (jnp.maximum(i - 1, 0), p))
    return pl.pallas_call(
        body, name="attn_bwd", grid=(BH // 2, nt + 1),
        in_specs=[blk(0, False), blk(8, True), blk(8, False), blk(16, True), blk(16, False),
                  pl.BlockSpec((2, TQ, 2 * TQ), lambda p, i: (p, 0, 0)), own],
        out_specs=[own, lag, lag, pl.BlockSpec((2, TQ, 2 * TQ), lambda p, i: (p, 0, 0))],
        out_shape=[jax.ShapeDtypeStruct((t, D), BF16)] * 3 + [jax.ShapeDtypeStruct((BH, TQ, 2 * TQ), F32)],
        scratch_shapes=[pltpu.VMEM((TQ, 128), F32), pltpu.VMEM((TQ, 128), F32)],
        compiler_params=_cparams(2),
    )(proj, proj, proj, proj, proj, bias, do)


MERGE_TM = 256


def merge_fwd(x, oa, ob, proj, vecs, wa, wb, wo):
    t = x.shape[0]
    tm = MERGE_TM
    names = ("bga", "bgb", "gate_t", "g1", "b1", "scale_f", "shift_f")

    def body(x_ref, oa_ref, ob_ref, gra_ref, grb_ref, *rest):
        vrefs = rest[:7]
        wa_ref, wb_ref, wo_ref, y_ref, h_ref = rest[7:]
        vv = [r[...] for r in vrefs]
        zero = jnp.zeros((tm, D), F32)
        y1, _ = merge_fn(x_ref[...], oa_ref[...], ob_ref[...], gra_ref[...], grb_ref[...], zero, zero, zero,
                         *vv, wa_ref[...], wb_ref[...], wo_ref[...])
        y_ref[...] = y1
        h_ref[...] = (y1 * (1.0 + vv[5]) + vv[6]).astype(BF16)

    return pl.pallas_call(
        body, name="merge_fwd", grid=(t // tm,),
        in_specs=[_rows(tm, D), _rows(tm, D), _rows(tm, D), _rows(tm, D, C_GATE // D), _rows(tm, D, C_GATE // D + 1)]
        + [_const((1, D))] * 7 + [_const((D, D))] * 3,
        out_specs=[_rows(tm, D), _rows(tm, D)],
        out_shape=[jax.ShapeDtypeStruct((t, D), F32), jax.ShapeDtypeStruct((t, D), BF16)],
        compiler_params=_cparams(),
    )(x, oa, ob, proj, proj, *[vecs[n] for n in names], wa, wb, wo)


def merge_bwd(x, oa, ob, proj, vecs, wa, wb, wo, dy1):
    t = x.shape[0]
    tm = MERGE_TM
    names = ("bga", "bgb", "gate_t", "g1", "b1", "scale_f", "shift_f")

    def body(x_ref, oa_ref, ob_ref, gra_ref, grb_ref, *rest):
        vrefs = rest[:7]
        wa_ref, wb_ref, wo_ref, dy_ref = rest[7:11]
        (dx_ref, doa_ref, dob_ref, dga_ref, dgb_ref, mg_ref, dmix_ref, dpa_ref, dpb_ref,
         dbga_ref, dbgb_ref, dgt_ref, dg1_ref, db1_ref) = rest[11:]
        i = pl.program_id(0)
        vv = [r[...] for r in vrefs]
        zero = jnp.zeros((tm, D), F32)

        def f(x_, oa_, ob_, gra_, grb_, ppa, ppb, pmix, bga, bgb, gate_t, g1, b1):
            return merge_fn(x_, oa_, ob_, gra_, grb_, ppa, ppb, pmix, bga, bgb, gate_t, g1, b1, vv[5], vv[6],
                            wa_ref[...], wb_ref[...], wo_ref[...])

        _, vjp, merged = jax.vjp(f, x_ref[...], oa_ref[...].astype(F32), ob_ref[...].astype(F32),
                                 gra_ref[...], grb_ref[...], zero, zero, zero, *vv[:5], has_aux=True)
        dx, doa, dob, dga, dgb, dpa, dpb, dmix, dbga, dbgb, dgt, dg1, db1 = vjp(dy_ref[...])
        dx_ref[...] = dx
        doa_ref[...] = doa
        dob_ref[...] = dob
        dga_ref[...] = dga.astype(BF16)
        dgb_ref[...] = dgb.astype(BF16)
        mg_ref[...] = merged.astype(BF16)
        dmix_ref[...] = dmix.astype(BF16)
        dpa_ref[...] = dpa.astype(BF16)
        dpb_ref[...] = dpb.astype(BF16)
        accs = (dbga_ref, dbgb_ref, dgt_ref, dg1_ref, db1_ref)

        @pl.when(i == 0)
        def _():
            for a in accs:
                a[...] = jnp.zeros_like(a)

        for a, val in zip(accs, (dbga, dbgb, dgt, dg1, db1)):
            a[...] += val

    return pl.pallas_call(
        body, name="merge_bwd", grid=(t // tm,),
        in_specs=[_rows(tm, D), _rows(tm, D), _rows(tm, D), _rows(tm, D, C_GATE // D), _rows(tm, D, C_GATE // D + 1)]
        + [_const((1, D))] * 7 + [_const((D, D))] * 3 + [_rows(tm, D)],
        out_specs=[_rows(tm, D)] * 9 + [_const((1, D))] * 5,
        out_shape=[jax.ShapeDtypeStruct((t, D), F32)] * 3 + [jax.ShapeDtypeStruct((t, D), BF16)] * 6
        + [jax.ShapeDtypeStruct((1, D), F32)] * 5,
        compiler_params=_cparams(),
    )(x, oa, ob, proj, proj, *[vecs[n] for n in names], wa, wb, wo, dy1)


FFN_TM = 128


def ffn_act_fwd(up, conv_w, bconv):
    t, wdt = up.shape
    tm = FFN_TM

    def body(prev_ref, cur_ref, cw_ref, bc_ref, a_ref):
        i = pl.program_id(0)
        flag = jnp.where(i > 0, 1.0, 0.0)
        ext = jnp.concatenate([prev_ref[...] * flag, cur_ref[...]], axis=0)
        rows = tuple(cw_ref[j:j + 1, :] for j in range(3))
        a_ref[...] = ffn_act_fn(ext, rows, bc_ref[...]).astype(BF16)

    return pl.pallas_call(
        body, name="ffn_act_fwd", grid=(t // tm,),
        in_specs=_halo_specs(tm, wdt, 0, lambda i: i) + [_const((3, wdt)), _const((1, wdt))],
        out_specs=_rows(tm, DFF),
        out_shape=jax.ShapeDtypeStruct((t, DFF), BF16),
        compiler_params=_cparams(),
    )(up, up, conv_w, bconv)


def ffn_act_bwd(up, conv_w, bconv, da):
    t, wdt = up.shape
    tm = FFN_TM
    nt = t // tm
    rev = lambda i: nt - 1 - i

    def body(prev_ref, cur_ref, cw_ref, bc_ref, da_ref, dup_ref, dcw_ref, dbc_ref, carry):
        i = pl.program_id(0)
        flag = jnp.where(i < nt - 1, 1.0, 0.0)
        ext = jnp.concatenate([prev_ref[...] * flag, cur_ref[...]], axis=0)
        rows = tuple(cw_ref[j:j + 1, :] for j in range(3))
        _, vjp = jax.vjp(ffn_act_fn, ext, rows, bc_ref[...])
        dext, drows, dbc = vjp(da_ref[...])

        @pl.when(i == 0)
        def _():
            carry[...] = jnp.zeros_like(carry)
            dcw_ref[...] = jnp.zeros_like(dcw_ref)
            dbc_ref[...] = jnp.zeros_like(dbc_ref)

        dcur = dext[HALO:]
        dup_ref[...] = jnp.concatenate([dcur[:tm - HALO], dcur[tm - HALO:] + carry[...]], axis=0).astype(BF16)
        carry[...] = dext[:HALO]
        dcw = None
        for j in range(3):
            tj = _onehot_rows(3, j) * drows[j]
            dcw = tj if dcw is None else dcw + tj
        dcw_ref[...] += dcw
        dbc_ref[...] += dbc

    return pl.pallas_call(
        body, name="ffn_act_bwd", grid=(nt,),
        in_specs=_halo_specs(tm, wdt, 0, rev) + [_const((3, wdt)), _const((1, wdt)), _rows(tm, DFF, 0, rev)],
        out_specs=[_rows(tm, wdt, 0, rev), _const((3, wdt)), _const((1, wdt))],
        out_shape=[jax.ShapeDtypeStruct((t, wdt), BF16), jax.ShapeDtypeStruct((3, wdt), F32),
                   jax.ShapeDtypeStruct((1, wdt), F32)],
        scratch_shapes=[pltpu.VMEM((HALO, wdt), F32)],
        compiler_params=_cparams(),
    )(up, up, conv_w, bconv, da)


HEAD_TM = 256


def head_fwd_bwd(a, y1, tgt, gate_f, g2, b2, wd):
    t = a.shape[0]
    tm = HEAD_TM

    def body(a_ref, y_ref, t_ref, gf_ref, g2_ref, b2_ref, wd_ref,
             da_ref, dy_ref, dffn_ref, dgf_ref, dg2_ref, db2_ref, loss_ref):
        i = pl.program_id(0)
        zero = jnp.zeros((tm, D), F32)

        def f(a_, y_, pf, gf, g2_, b2_):
            return head_fn(a_, y_, pf, gf, g2_, b2_, t_ref[...], wd_ref[...])

        loss, vjp = jax.vjp(f, a_ref[...].astype(F32), y_ref[...], zero, gf_ref[...], g2_ref[...], b2_ref[...])
        da, dy, dffn, dgf, dg2, db2 = vjp(jnp.ones((), F32))
        da_ref[...] = da
        dy_ref[...] = dy
        dffn_ref[...] = dffn.astype(BF16)
        accs = (dgf_ref, dg2_ref, db2_ref, loss_ref)

        @pl.when(i == 0)
        def _():
            for r in accs:
                r[...] = jnp.zeros_like(r)

        dgf_ref[...] += dgf
        dg2_ref[...] += dg2
        db2_ref[...] += db2
        loss_ref[...] += loss * jnp.ones((1, 128), F32)

    return pl.pallas_call(
        body, name="head_fwd_bwd", grid=(t // tm,),
        in_specs=[_rows(tm, DFF), _rows(tm, D), _rows(tm, D), _const((1, D)), _const((1, D)), _const((1, D)),
                  _const((DFF, D))],
        out_specs=[_rows(tm, DFF), _rows(tm, D), _rows(tm, D), _const((1, D)), _const((1, D)), _const((1, D)),
                   _const((1, 128))],
        out_shape=[jax.ShapeDtypeStruct((t, DFF), F32), jax.ShapeDtypeStruct((t, D), F32),
                   jax.ShapeDtypeStruct((t, D), BF16)] + [jax.ShapeDtypeStruct((1, D), F32)] * 3
        + [jax.ShapeDtypeStruct((1, 128), F32)],
        compiler_params=_cparams(),
    )(a, y1, tgt, gate_f, g2, b2, wd)


def ada_fwd(c_all, w_sh, b_sh):
    def body(c_ref, w_ref, b_ref, o_ref):
        o_ref[...] = _mmh(_silu(c_ref[...]), w_ref[...]) + b_ref[...]

    n = w_sh.shape[1]
    return pl.pallas_call(
        body, name="ada_fwd", out_shape=jax.ShapeDtypeStruct((NDEV, n), F32),
        in_specs=[pl.BlockSpec(memory_space=pltpu.VMEM)] * 3,
        out_specs=pl.BlockSpec(memory_space=pltpu.VMEM),
        compiler_params=pltpu.CompilerParams(vmem_limit_bytes=VMEM_LIMIT),
    )(c_all, w_sh, b_sh)


def ada_wgrad(c_all_t, dmod_sh):
    def body(c_ref, d_ref, o_ref):
        o_ref[...] = _mmh(_silu(c_ref[...]), d_ref[...])

    return pl.pallas_call(
        body, name="ada_wgrad", out_shape=jax.ShapeDtypeStruct((c_all_t.shape[0], dmod_sh.shape[1]), F32),
        in_specs=[pl.BlockSpec(memory_space=pltpu.VMEM)] * 2,
        out_specs=pl.BlockSpec(memory_space=pltpu.VMEM),
        compiler_params=pltpu.CompilerParams(vmem_limit_bytes=VMEM_LIMIT),
    )(c_all_t, dmod_sh)


def adamw(gparts, w, m, v, name):
    p, r, c = gparts.shape
    tr = r if r <= 256 else _pick(r, (256, 128, 64, 32, 16, 8))
    c1 = 1.0 - B1 ** STEP
    c2 = 1.0 - B2 ** STEP

    def body(g_ref, w_ref, m_ref, v_ref, go_ref, d_ref, mo_ref, vo_ref):
        g = g_ref[0].astype(F32)
        for s in range(1, p):
            g = g + g_ref[s].astype(F32)
        mn = B1 * m_ref[...] + (1.0 - B1) * g
        vn = B2 * v_ref[...] + (1.0 - B2) * (g * g)
        go_ref[...] = g
        d_ref[...] = -LR * ((mn / c1) / (jnp.sqrt(vn / c2) + AEPS) + WD * w_ref[...])
        mo_ref[...] = mn
        vo_ref[...] = vn

    spec = pl.BlockSpec((tr, c), lambda i: (i, 0))
    return pl.pallas_call(
        body, name=name, grid=(r // tr,),
        in_specs=[pl.BlockSpec((p, tr, c), lambda i: (0, i, 0)), spec, spec, spec],
        out_specs=[spec] * 4,
        out_shape=[jax.ShapeDtypeStruct((r, c), F32)] * 4,
        compiler_params=_cparams(),
    )(gparts, w, m, v)


def _me():
    x, y, c = lax.axis_index("x"), lax.axis_index("y"), lax.axis_index("c")
    return x, y, c, 4 * x + 2 * y + c


def _peer(x, y, c, d):
    px = 1 - x if (d >> 2) & 1 else x
    py = 1 - y if (d >> 1) & 1 else y
    pc = 1 - c if d & 1 else c
    return (px, py, pc), 4 * px + 2 * py + pc


def _exchange(arrs, name, scatter):
    n = len(arrs)

    def body(*refs):
        ins, outs = refs[:n], refs[n:2 * n]
        send, recv, lsem = refs[2 * n:]
        x, y, c, me = _me()
        remote, local = [], []
        for k in range(n):
            src = ins[k].at[me] if scatter else ins[k]
            cp = pltpu.make_async_copy(src, outs[k].at[me], lsem.at[k])
            cp.start()
            local.append(cp)
            for d in range(1, NDEV):
                dev, pid = _peer(x, y, c, d)
                src = ins[k].at[pid] if scatter else ins[k]
                cp = pltpu.make_async_remote_copy(src_ref=src, dst_ref=outs[k].at[me],
                                                  send_sem=send.at[k, d - 1], recv_sem=recv.at[k, d - 1],
                                                  device_id=dev, device_id_type=pl.DeviceIdType.MESH)
                cp.start()
                remote.append(cp)
        for cp in remote:
            cp.wait()
        for cp in local:
            cp.wait()

    shapes = [a.shape if scatter else (NDEV,) + a.shape for a in arrs]
    return pl.pallas_call(
        body, name=name,
        in_specs=[pl.BlockSpec(memory_space=pl.ANY)] * n,
        out_specs=[pl.BlockSpec(memory_space=pl.ANY)] * n,
        out_shape=[jax.ShapeDtypeStruct(s, a.dtype) for s, a in zip(shapes, arrs)],
        scratch_shapes=[pltpu.SemaphoreType.DMA((n, NDEV - 1)), pltpu.SemaphoreType.DMA((n, NDEV - 1)),
                        pltpu.SemaphoreType.DMA((n,))],
        compiler_params=pltpu.CompilerParams(has_side_effects=True),
    )(*arrs)


def all_gather(arrs, name):
    return _exchange(arrs, name, False)


def all_to_all(arrs, name):
    return _exchange(arrs, name, True)


def _to_cat(w_in):
    k = w_in.shape[0]
    ba = jnp.zeros((k, NCAT - C_BA), w_in.dtype)
    ba = ba.at[:, 0:8].set(w_in[:, 4096:4104]).at[:, 128:136].set(w_in[:, 4104:4112])
    return jnp.concatenate([w_in[:, 0:3072], w_in[:, 3072:4096], w_in[:, 4112:7184], w_in[:, 7184:9232], ba], axis=1)


def _from_cat(dw):
    return jnp.concatenate([dw[:, 0:3072], dw[:, 3072:4096], dw[:, C_BA:C_BA + 8], dw[:, C_BA + 128:C_BA + 136],
                            dw[:, 4096:7168], dw[:, 7168:9216]], axis=1)


def _pad128(v):
    return jnp.pad(v, ((0, 0), (0, 128 - v.shape[1])))


def local_step(x, tgt, mod, wts, small):
    t = x.shape[0]
    nc = t // CH
    shift_t, scale_t, gate_t, shift_f, scale_f, gate_f = mod
    wcat = _to_cat(wts["w_in"])
    a_log = _pad128(small["a_log"])
    dtb = _pad128(small["dt_bias"])
    vecs = dict(bga=small["b_gate"][:, :D], bgb=small["b_gate"][:, D:], gate_t=gate_t, g1=small["ln1_g"],
                b1=small["ln1_b"], scale_f=scale_f, shift_f=shift_f)

    h1 = modulate(x, scale_t, shift_t, "modulate_t")
    proj = matmul(h1, wcat, F32, "in_proj")
    q, k, v, gcs, beta = prep_fwd(proj, small["conv_a"], a_log, dtb)

    def col4(a):
        return a[:, :AH].reshape(nc, CH, AH).transpose(2, 0, 1)[..., None]

    gcol = col4(gcs)
    grow = gcol.reshape(AH, nc, 1, CH)
    bcol = col4(beta)
    u, w, qg, kd, qk, eg = c1_fwd(q, k, v, gcol, grow, bcol)
    oa, sall = c2_fwd(u, w, qg, kd, qk, eg, proj, small["norm_a"])
    bias = bias_tiles(small["rel_bias"])
    ob = attn_fwd(proj, bias)
    y1, h2 = merge_fwd(x, oa, ob, proj, vecs, wts["w_a"], wts["w_b"], wts["w_o"])
    up = matmul(h2, wts["w_up"], F32, "up_proj")
    a = ffn_act_fwd(up, small["conv_ffn"], small["b_conv_ffn"])

    da, dy1_res, dffn, dgate_f, dg2, db2, loss = head_fwd_bwd(a, y1, tgt, gate_f, small["ln2_g"], small["ln2_b"],
                                                            wts["w_down"])
    g_w_down = matmul(a.T, dffn, F32, "wgrad_down")
    dup, g_conv_ffn, g_bconv = ffn_act_bwd(up, small["conv_ffn"], small["b_conv_ffn"], da)
    dh2 = matmul(dup, wts["w_up"].T, F32, "dgrad_up")
    g_w_up = matmul(h2.T, dup, F32, "wgrad_up")
    dy1, dscale_f, dshift_f = modulate_bwd(dh2, y1, dy1_res, scale_f, "modulate_f_bwd")
    (dx_res, doa, dob, dga, dgb, merged, dmix, dpa, dpb,
     dbga, dbgb, dgate_t, dg1, db1) = merge_bwd(x, oa, ob, proj, vecs, wts["w_a"], wts["w_b"], wts["w_o"], dy1)
    g_w_o = matmul(merged.T, dmix, F32, "wgrad_o")
    g_w_a = matmul(oa.T, dpa, F32, "wgrad_a")
    g_w_b = matmul(ob.T, dpb, F32, "wgrad_b")
    dqb, dkb, dvb, dbias = attn_bwd(proj, bias, dob)
    g_rel = relbias_reduce(bias_tiles_bwd_layout(dbias))
    du, dw, dqg, dkd, dqk, deg, dz, g_norm = c2_bwd(u, w, qg, kd, qk, eg, proj, small["norm_a"], sall, doa)
    dq, dk, dv, dgc, dgr, dbc = c1_bwd(q, k, v, gcol, grow, bcol, du, dw, dqg, dkd, dqk, deg)

    def from4(a4):
        return _pad128(a4[..., 0].transpose(1, 2, 0).reshape(t, AH))

    dgcs = from4(dgc) + from4(dgr.reshape(AH, nc, CH, 1))
    dbeta = from4(dbc)
    dpre, dbb, daa, g_conv_a, g_alog, g_dtb = prep_bwd(proj, small["conv_a"], a_log, dtb, dq, dk, dv, dgcs, dbeta)
    dba = jnp.concatenate([dbb, daa, jnp.zeros((t, NCAT - C_BA - 256), BF16)], axis=1)
    dproj = jnp.concatenate([dpre, dz, dqb, dkb, dvb, dga, dgb, dba], axis=1)
    dh1 = matmul(dproj, wcat.T, F32, "dgrad_in")
    g_wcat = matmul(h1.T, dproj, F32, "wgrad_in")
    grad_x, dscale_t, dshift_t = modulate_bwd(dh1, x, dx_res, scale_t, "modulate_t_bwd")

    dmod = (dshift_t, dscale_t, dgate_t, dshift_f, dscale_f, dgate_f)
    grads = dict(w_in=_from_cat(g_wcat), w_up=g_w_up, w_down=g_w_down, w_a=g_w_a, w_b=g_w_b, w_o=g_w_o,
                 conv_a=g_conv_a, rel_bias=g_rel, conv_ffn=g_conv_ffn,
                 b_gate=jnp.concatenate([dbga, dbgb], axis=1), a_log=g_alog[:, :AH], dt_bias=g_dtb[:, :AH],
                 norm_a=g_norm, ln1_g=dg1, ln1_b=db1, b_conv_ffn=g_bconv, ln2_g=dg2, ln2_b=db2)
    return loss[0, 0], grad_x, dmod, grads


_REP = {}
_off = 0
for _n, _wd, _pw in (("b_ada", 6144, 6144), ("b_gate", 2048, 2048), ("a_log", 8, 128), ("dt_bias", 8, 128),
                     ("norm_a", 128, 128), ("ln1_g", 1024, 1024), ("ln1_b", 1024, 1024),
                     ("b_conv_ffn", 5632, 5632), ("ln2_g", 1024, 1024), ("ln2_b", 1024, 1024), ("loss", 1, 128)):
    _REP[_n] = (_off, _wd, _pw)
    _off += _pw
REP_LEN = _off
REP_NAMES = [n for n in _REP if n != "loss"]
_SH = (("conv_a", (4, 384)), ("rel_bias", (16, 40)), ("conv_ffn", (3, 704)))
SH_LEN = 4352


def _pack_rep(vals):
    parts = []
    for n, (_, wd, pw) in _REP.items():
        a = vals.get(n)
        a = jnp.zeros((1, pw), F32) if a is None else jnp.pad(a.reshape(1, wd), ((0, 0), (0, pw - wd)))
        parts.append(a)
    return jnp.concatenate(parts, axis=1)


def _unpack_rep(vec, name):
    o, wd, _ = _REP[name]
    return vec[:, o:o + wd]


def _pack_sh(vals):
    parts = [vals[n].reshape(vals[n].shape[:-2] + (-1,)) for n, _ in _SH]
    a = jnp.concatenate(parts, axis=-1)
    return jnp.pad(a, [(0, 0)] * (a.ndim - 1) + [(0, SH_LEN - a.shape[-1])])


def _unpack_sh(vec, name):
    o = 0
    for n, shp in _SH:
        sz = shp[0] * shp[1]
        if n == name:
            return vec[0, o:o + sz].reshape(shp)
        o += sz
    raise KeyError(name)


def _col_shards(a, n):
    return a.reshape(a.shape[0], NDEV, n).transpose(1, 0, 2)


def kernel(x, c, w_ada, b_ada, w_in, b_gate, conv_a, a_log, dt_bias, norm_a, rel_bias, w_branch_a, w_branch_b, w_o, ln1_g, ln1_b, w_up, conv_ffn, b_conv_ffn, w_down, ln2_g, ln2_b, loss_target, m_w_ada, m_b_ada, m_w_in, m_b_gate, m_conv_a, m_a_log, m_dt_bias, m_norm_a, m_rel_bias, m_w_branch_a, m_w_branch_b, m_w_o, m_ln1_g, m_ln1_b, m_w_up, m_conv_ffn, m_b_conv_ffn, m_w_down, m_ln2_g, m_ln2_b, v_w_ada, v_b_ada, v_w_in, v_b_gate, v_conv_a, v_a_log, v_dt_bias, v_norm_a, v_rel_bias, v_w_branch_a, v_w_branch_b, v_w_o, v_ln1_g, v_ln1_b, v_w_up, v_conv_ffn, v_b_conv_ffn, v_w_down, v_ln2_g, v_ln2_b):
    W = dict(w_ada=w_ada, b_ada=b_ada, w_in=w_in, b_gate=b_gate, conv_a=conv_a, a_log=a_log, dt_bias=dt_bias,
             norm_a=norm_a, rel_bias=rel_bias, w_branch_a=w_branch_a, w_branch_b=w_branch_b, w_o=w_o, ln1_g=ln1_g,
             ln1_b=ln1_b, w_up=w_up, conv_ffn=conv_ffn, b_conv_ffn=b_conv_ffn, w_down=w_down, ln2_g=ln2_g,
             ln2_b=ln2_b)
    M = dict(w_ada=m_w_ada, b_ada=m_b_ada, w_in=m_w_in, b_gate=m_b_gate, conv_a=m_conv_a, a_log=m_a_log,
             dt_bias=m_dt_bias, norm_a=m_norm_a, rel_bias=m_rel_bias, w_branch_a=m_w_branch_a,
             w_branch_b=m_w_branch_b, w_o=m_w_o, ln1_g=m_ln1_g, ln1_b=m_ln1_b, w_up=m_w_up, conv_ffn=m_conv_ffn,
             b_conv_ffn=m_b_conv_ffn, w_down=m_w_down, ln2_g=m_ln2_g, ln2_b=m_ln2_b)
    V = dict(w_ada=v_w_ada, b_ada=v_b_ada, w_in=v_w_in, b_gate=v_b_gate, conv_a=v_conv_a, a_log=v_a_log,
             dt_bias=v_dt_bias, norm_a=v_norm_a, rel_bias=v_rel_bias, w_branch_a=v_w_branch_a,
             w_branch_b=v_w_branch_b, w_o=v_w_o, ln1_g=v_ln1_g, ln1_b=v_ln1_b, w_up=v_w_up, conv_ffn=v_conv_ffn,
             b_conv_ffn=v_b_conv_ffn, w_down=v_w_down, ln2_g=v_ln2_g, ln2_b=v_ln2_b)
    W, M, V = ({n: a[0] for n, a in dct.items()} for dct in (W, M, V))
    me = 4 * lax.axis_index("x") + 2 * lax.axis_index("y") + lax.axis_index("c")
    big = ("w_in", "w_up", "w_down", "w_branch_a", "w_branch_b", "w_o")

    gathered = all_gather([W[n].astype(BF16) for n in big], "gather_weights")
    g_in, g_up, g_down, g_a, g_b, g_o = gathered
    wts = dict(w_in=g_in.transpose(1, 0, 2).reshape(D, -1), w_up=g_up.transpose(1, 0, 2).reshape(D, -1),
               w_down=g_down.reshape(DFF, D), w_a=g_a.reshape(D, D), w_b=g_b.reshape(D, D), w_o=g_o.reshape(D, D))
    c_all, sh_all = all_gather([c, _pack_sh({n: W[n] for n, _ in _SH})[None]], "gather_small")
    c_all = c_all.reshape(NDEV, D)
    sh_all = sh_all.reshape(NDEV, SH_LEN)

    def full_small(name, shp):
        o = 0
        for n, s in _SH:
            if n == name:
                break
            o += s[0] * s[1]
        sz = shp[0] * shp[1]
        return sh_all[:, o:o + sz].reshape(NDEV, shp[0], shp[1]).transpose(1, 0, 2).reshape(shp[0], NDEV * shp[1])

    small = dict(conv_a=full_small("conv_a", (4, 384)), rel_bias=full_small("rel_bias", (16, 40)),
                 conv_ffn=full_small("conv_ffn", (3, 704)),
                 b_gate=W["b_gate"][None], a_log=W["a_log"][None], dt_bias=W["dt_bias"][None],
                 norm_a=W["norm_a"][None], ln1_g=W["ln1_g"][None], ln1_b=W["ln1_b"][None],
                 b_conv_ffn=W["b_conv_ffn"][None], ln2_g=W["ln2_g"][None], ln2_b=W["ln2_b"][None])

    nsh = w_ada.shape[2]
    b_sh = lax.dynamic_slice(W["b_ada"][None], (0, me * nsh), (1, nsh))
    mod_sh = ada_fwd(c_all, W["w_ada"], b_sh)
    (mod_rows,) = all_to_all([mod_sh[:, None, :]], "scatter_mod")
    mod6 = mod_rows.reshape(6, D)
    mod = tuple(mod6[i:i + 1] for i in range(6))

    loss, grad_x, dmod, g = local_step(x[0], loss_target[0], mod, wts, small)

    rep_vals = {n: g[n] for n in REP_NAMES if n != "b_ada"}
    rep_vals["b_ada"] = jnp.concatenate(dmod, axis=1)
    rep_vals["loss"] = loss.reshape(1, 1)
    (rep_all,) = all_gather([_pack_rep(rep_vals)[None]], "gather_small_grads")
    rep_all = rep_all.reshape(NDEV, 1, REP_LEN)
    zero1 = jnp.zeros((1, 1), F32)
    rep_out = adamw(rep_all, _pack_rep({**{n: W[n][None] for n in REP_NAMES}, "loss": zero1}),
                    _pack_rep({**{n: M[n][None] for n in REP_NAMES}, "loss": zero1}),
                    _pack_rep({**{n: V[n][None] for n in REP_NAMES}, "loss": zero1}), "adamw_small")
    loss_total = _unpack_rep(rep_out[0], "loss")[0, 0]

    o_ada = _REP["b_ada"][0]
    dmod_all = rep_all[:, 0, o_ada:o_ada + 6 * D]
    dmod_sh = lax.dynamic_slice(dmod_all, (0, me * nsh), (NDEV, nsh))
    g_w_ada = ada_wgrad(c_all.T, dmod_sh)

    parts = all_to_all([
        _col_shards(g["w_in"], w_in.shape[2]).astype(BF16), _col_shards(g["w_up"], w_up.shape[2]).astype(BF16),
        g["w_down"].reshape(NDEV, -1, D).astype(BF16), g["w_a"].reshape(NDEV, -1, D).astype(BF16),
        g["w_b"].reshape(NDEV, -1, D).astype(BF16), g["w_o"].reshape(NDEV, -1, D).astype(BF16)], "scatter_grads")
    sh_parts = {"conv_a": _col_shards(g["conv_a"], 384), "rel_bias": _col_shards(g["rel_bias"], 40),
                "conv_ffn": _col_shards(g["conv_ffn"], 704)}
    (sh_recv,) = all_to_all([_pack_sh(sh_parts)[:, None, :]], "scatter_small_grads")

    res = {}
    for n, p in zip(big, parts):
        res[n] = adamw(p, W[n], M[n], V[n], "adamw_" + n)
    res["w_ada"] = adamw(g_w_ada[None], W["w_ada"], M["w_ada"], V["w_ada"], "adamw_w_ada")
    sh_out = adamw(sh_recv, _pack_sh({n: W[n] for n, _ in _SH})[None], _pack_sh({n: M[n] for n, _ in _SH})[None],
                   _pack_sh({n: V[n] for n, _ in _SH})[None], "adamw_small_sharded")
    for n, _ in _SH:
        res[n] = tuple(_unpack_sh(o, n) for o in sh_out)
    for n in REP_NAMES:
        res[n] = tuple(_unpack_rep(o, n)[0] for o in rep_out)

    order = ("w_ada", "b_ada", "w_in", "b_gate", "conv_a", "a_log", "dt_bias", "norm_a", "rel_bias", "w_branch_a",
             "w_branch_b", "w_o", "ln1_g", "ln1_b", "w_up", "conv_ffn", "b_conv_ffn", "w_down", "ln2_g", "ln2_b")
    outs = [loss_total, grad_x[None]]
    for kind in range(4):
        outs += [res[n][kind][None] for n in order]
    return tuple(outs)
```

```python
import functools
import math

import numpy as np
import jax
import jax.numpy as jnp
from jax import lax
from jax.experimental import pallas as pl
from jax.experimental.pallas import tpu as pltpu

F32 = jnp.float32
BF16 = jnp.bfloat16
HI = lax.Precision.HIGHEST

D = 1024
CH = 64
AH, ADK = 8, 128
BH, BDH = 16, 64
BPREV = 8
BMAXREL = 256
RELSZ = CH + BMAXREL
DFF = 2816
ALPHA = 2.0 ** 0.25
LN_EPS, RMS_EPS, L2_EPS = 1e-5, 1e-6, 1e-6
NEG = -1e30
LR, B1, B2, AEPS, WD, STEP = 1e-3, 0.9, 0.999, 1e-8, 0.01, 10
NDEV = 8
HALO = 8
TQ = 512
VMEM_LIMIT = 56 * 1024 * 1024

C_QKVA, C_Z, C_QKVB, C_GATE, C_BA, NCAT = 0, 3072, 4096, 7168, 9216, 9728


def _cparams(n_axes=1, vmem=VMEM_LIMIT):
    return pltpu.CompilerParams(dimension_semantics=("arbitrary",) * n_axes, vmem_limit_bytes=vmem)


def _dg(a, b, ca, cb):
    return lax.dot_general(a.astype(BF16), b.astype(BF16), (((ca,), (cb,)), ((), ())),
                           preferred_element_type=F32)


@jax.custom_vjp
def mm_nn(a, b):
    return _dg(a, b, 1, 0)


@jax.custom_vjp
def mm_nt(a, b):
    return _dg(a, b, 1, 1)


@jax.custom_vjp
def mm_tn(a, b):
    return _dg(a, b, 0, 0)


mm_nn.defvjp(lambda a, b: (mm_nn(a, b), (a, b)),
             lambda r, g: (mm_nt(g, r[1]).astype(r[0].dtype), mm_tn(r[0], g).astype(r[1].dtype)))
mm_nt.defvjp(lambda a, b: (mm_nt(a, b), (a, b)),
             lambda r, g: (mm_nn(g, r[1]).astype(r[0].dtype), mm_tn(g, r[0]).astype(r[1].dtype)))
mm_tn.defvjp(lambda a, b: (mm_tn(a, b), (a, b)),
             lambda r, g: (mm_nt(r[1], g).astype(r[0].dtype), mm_nn(r[0], g).astype(r[1].dtype)))


@jax.custom_vjp
def mm_w(a, w):
    return _dg(a, w, 1, 0)


mm_w.defvjp(lambda a, w: (mm_w(a, w), (a, w)),
            lambda r, g: (mm_nt(g, r[1]).astype(r[0].dtype), jnp.zeros_like(r[1])))


def _mmh(a, b):
    return lax.dot_general(a, b, (((1,), (0,)), ((), ())), precision=HI, preferred_element_type=F32)


def _bdg(a, b, ca, cb):
    return lax.dot_general(a.astype(BF16), b.astype(BF16), (((ca,), (cb,)), ((0,), (0,))),
                           preferred_element_type=F32)


@jax.custom_vjp
def bmm_nn(a, b):
    return _bdg(a, b, 2, 1)


@jax.custom_vjp
def bmm_nt(a, b):
    return _bdg(a, b, 2, 2)


@jax.custom_vjp
def bmm_tn(a, b):
    return _bdg(a, b, 1, 1)


bmm_nn.defvjp(lambda a, b: (bmm_nn(a, b), (a, b)), lambda r, g: (bmm_nt(g, r[1]), bmm_tn(r[0], g)))
bmm_nt.defvjp(lambda a, b: (bmm_nt(a, b), (a, b)), lambda r, g: (bmm_nn(g, r[1]), bmm_tn(g, r[0])))
bmm_tn.defvjp(lambda a, b: (bmm_tn(a, b), (a, b)), lambda r, g: (bmm_nt(r[1], g), bmm_nn(r[0], g)))


def _bdg3(a, b, ca, cb):
    return lax.dot_general(a, b, (((ca,), (cb,)), ((0,), (0,))), precision=HI, preferred_element_type=F32)


@jax.custom_vjp
def bmm3_nn(a, b):
    return _bdg3(a, b, 2, 1)


bmm3_nn.defvjp(lambda a, b: (bmm3_nn(a, b), (a, b)),
               lambda r, g: (_bdg3(g, r[1], 2, 2), _bdg3(r[0], g, 1, 1)))


def _sigmoid(x):
    return 1.0 / (1.0 + jnp.exp(-x))


def _silu(x):
    return x * _sigmoid(x)


def _softplus(x):
    return jnp.maximum(x, 0.0) + jnp.log(1.0 + jnp.exp(-jnp.abs(x)))


def _layernorm(r, g, b):
    mu = jnp.mean(r, axis=-1, keepdims=True)
    xc = r - mu
    var = jnp.mean(xc * xc, axis=-1, keepdims=True)
    return xc * lax.rsqrt(var + LN_EPS) * g + b


def _iota2(shape, dim):
    return lax.broadcasted_iota(jnp.int32, shape, dim)


@jax.custom_vjp
def causal_conv(ext, rows):
    k = len(rows)
    y = None
    for j in range(k):
        s = k - 1 - j
        r = pltpu.roll(ext, s, 0) if s else ext
        t = r[HALO:] * rows[j]
        y = t if y is None else y + t
    return y


def _causal_conv_fwd(ext, rows):
    return causal_conv(ext, rows), (ext, rows)


def _causal_conv_bwd(res, g):
    ext, rows = res
    n = ext.shape[0]
    k = len(rows)
    gext = jnp.concatenate([jnp.zeros((HALO, g.shape[1]), g.dtype), g], axis=0)
    dext = None
    drows = []
    for j in range(k):
        s = k - 1 - j
        up = pltpu.roll(gext, n - s, 0) if s else gext
        t = up * rows[j]
        dext = t if dext is None else dext + t
        r = pltpu.roll(ext, s, 0) if s else ext
        drows.append(jnp.sum(g * r[HALO:], axis=0, keepdims=True))
    return dext, tuple(drows)


causal_conv.defvjp(_causal_conv_fwd, _causal_conv_bwd)


def _chunk_masks(tm):
    i = _iota2((tm, tm), 0)
    j = _iota2((tm, tm), 1)
    same = (i ^ j) < CH
    lower = jnp.where(same & (j <= i), 1.0, 0.0).astype(F32)
    upper = jnp.where(same & (i <= j), 1.0, 0.0).astype(F32)
    return lower, upper


@jax.custom_vjp
def chunk_cumsum(g):
    lower, _ = _chunk_masks(g.shape[0])
    return _mmh(lower, g)


def _chunk_cumsum_bwd(_, ct):
    _, upper = _chunk_masks(ct.shape[0])
    return (_mmh(upper, ct),)


chunk_cumsum.defvjp(lambda g: (chunk_cumsum(g), None), _chunk_cumsum_bwd)


@jax.custom_vjp
def inv_unit_lower(a):
    n = a.shape[-1]
    eye = jnp.where(_iota2((1, n, n), 1) == _iota2((1, n, n), 2), 1.0, 0.0).astype(F32)
    x = eye - a
    p = _bdg3(a, a, 2, 1)
    steps = int(math.log2(n)) - 1
    for s in range(steps):
        x = x + _bdg3(x, p, 2, 1)
        if s + 1 < steps:
            p = _bdg3(p, p, 2, 1)
    return x


def _inv_fwd(a):
    t = inv_unit_lower(a)
    return t, t


def _inv_bwd(t, g):
    return (-_bdg3(_bdg3(t, g, 1, 1), t, 2, 2),)


inv_unit_lower.defvjp(_inv_fwd, _inv_bwd)


def prep_fn(ext, bb, aa, rows, a_log, dtb):
    s = _silu(causal_conv(ext, rows))
    qs, ks, vs = [], [], []
    for h in range(AH):
        qh = s[:, h * ADK:(h + 1) * ADK]
        kh = s[:, D + h * ADK:D + (h + 1) * ADK]
        qs.append(qh * lax.rsqrt(jnp.sum(qh * qh, axis=-1, keepdims=True) + L2_EPS) * (ADK ** -0.5))
        ks.append(kh * lax.rsqrt(jnp.sum(kh * kh, axis=-1, keepdims=True) + L2_EPS))
        vs.append(s[:, 2 * D + h * ADK:2 * D + (h + 1) * ADK])
    beta = _sigmoid(bb)
    g = -jnp.exp(a_log) * _softplus(aa + dtb)
    return tuple(qs), tuple(ks), tuple(vs), chunk_cumsum(g), beta


def c1_heads(q, k, v, gcol, grow, bcol):
    i = _iota2((1, CH, CH), 1)
    j = _iota2((1, CH, CH), 2)
    causal = j <= i
    strict = j < i
    diff = gcol - grow
    decay = jnp.where(causal, jnp.exp(jnp.where(causal, diff, 0.0)), 0.0)
    kb = k * bcol
    vb = v * bcol
    a_low = jnp.where(strict, bmm_nt(kb, k) * decay, 0.0)
    tinv = inv_unit_lower(a_low)
    egc = jnp.exp(gcol)
    u = bmm3_nn(tinv, vb)
    w = bmm3_nn(tinv, kb * egc)
    qk = jnp.where(causal, bmm_nt(q, k) * decay, 0.0)
    glast = jnp.sum(jnp.where(_iota2((1, CH, 1), 1) == CH - 1, gcol, 0.0), axis=1, keepdims=True)
    qg = q * egc
    kd = k * jnp.exp(glast - gcol)
    eg = jnp.exp(glast) * jnp.ones((1, 1, ADK), F32)
    return u, w, qk, qg, kd, eg


def c2_heads(s, u, w, qk, qg, kd, eg, z, nw):
    vn = u - bmm_nn(w, s)
    o = bmm_nn(qg, s) + bmm_nn(qk, vn)
    s2 = s * eg + bmm_tn(kd, vn)
    ms = jnp.mean(o * o, axis=-1, keepdims=True)
    og = o * lax.rsqrt(ms + RMS_EPS) * nw * _silu(z)
    return og, s2


def attn_pair(q, kp, kc, vp, vc, bias2, firstf):
    k = jnp.concatenate([kp, kc], axis=0)
    v = jnp.concatenate([vp, vc], axis=0)
    lane = _iota2((1, 2 * BDH), 1)
    col = _iota2((1, 2 * TQ), 1)
    nokey = jnp.where(col < TQ, firstf, 0.0) * NEG
    out = None
    for hh in range(2):
        hm = jnp.where((lane >= hh * BDH) & (lane < (hh + 1) * BDH), 1.0, 0.0).astype(F32)
        s = mm_nt(q * hm, k) * (BDH ** -0.5) + assemble_bias(bias2[hh]) + nokey
        m = lax.stop_gradient(jnp.max(s, axis=-1, keepdims=True))
        p = jnp.exp(s - m)
        p = p / jnp.sum(p, axis=-1, keepdims=True)
        o = mm_nn(p, v) * hm
        out = o if out is None else out + o
    return out


def merge_fn(x, oa, ob, gra, grb, p_pa, p_pb, p_mix, bga, bgb, gate_t, g1, b1, scale_f, shift_f,
             wa, wb, wo):
    ga = _sigmoid(gra + bga)
    gb = _sigmoid(grb + bgb)
    pa = mm_w(oa, wa) + p_pa
    pb = mm_w(ob, wb) + p_pb
    merged = ga * pa + gb * pb
    mix = mm_w(merged, wo) + p_mix
    y1 = _layernorm(ALPHA * x + gate_t * mix, g1, b1)
    return y1, merged


def ffn_act_fn(ext, rows, bconv):
    u = causal_conv(ext, rows) + bconv
    return _silu(u[:, :DFF]) * u[:, DFF:]


def head_fn(a, y1, p_ffn, gate_f, g2, b2, tgt, wd):
    ffn = mm_w(a, wd) + p_ffn
    y2 = _layernorm(ALPHA * y1 + gate_f * ffn, g2, b2)
    err = y2 - tgt
    return 0.5 * jnp.sum(jnp.mean(err * err, axis=-1, keepdims=True))


def _rows(tm, width, colblk=0, order=None):
    if order is None:
        return pl.BlockSpec((tm, width), lambda i: (i, colblk))
    return pl.BlockSpec((tm, width), lambda i: (order(i), colblk))


def _const(shape):
    nd = len(shape)
    return pl.BlockSpec(shape, lambda *_: (0,) * nd)


def _pick(n, cands):
    for c in cands:
        if n % c == 0:
            return c
    raise ValueError(f"no tile for {n}")


def _onehot_rows(k, j):
    return jnp.where(_iota2((k, 1), 0) == j, 1.0, 0.0).astype(F32)


def matmul(a, w, out_dtype, name):
    m, kdim = a.shape
    _, n = w.shape
    tm = _pick(m, (1024, 512, 256, 128))
    tn = _pick(n, (1024, 512, 256, 128))
    tk = _pick(kdim, (1024, 512, 256, 128))
    nk = kdim // tk

    def body(a_ref, w_ref, o_ref, *scratch):
        p = _dg(a_ref[...], w_ref[...], 1, 0)
        if nk == 1:
            o_ref[...] = p.astype(out_dtype)
            return
        acc = scratch[0]
        k = pl.program_id(2)

        @pl.when(k == 0)
        def _():
            acc[...] = p

        @pl.when(k > 0)
        def _():
            acc[...] += p

        @pl.when(k == nk - 1)
        def _():
            o_ref[...] = acc[...].astype(out_dtype)

    return pl.pallas_call(
        body, name=name,
        grid=(m // tm, n // tn, nk),
        in_specs=[pl.BlockSpec((tm, tk), lambda i, j, k: (i, k)),
                  pl.BlockSpec((tk, tn), lambda i, j, k: (k, j))],
        out_specs=pl.BlockSpec((tm, tn), lambda i, j, k: (i, j)),
        out_shape=jax.ShapeDtypeStruct((m, n), out_dtype),
        scratch_shapes=[] if nk == 1 else [pltpu.VMEM((tm, tn), F32)],
        compiler_params=_cparams(3),
    )(a, w)


def modulate(x, scale, shift, name):
    t, d = x.shape
    tm = _pick(t, (512, 256, 128))

    def body(x_ref, sc_ref, sh_ref, o_ref):
        o_ref[...] = (x_ref[...] * (1.0 + sc_ref[...]) + sh_ref[...]).astype(BF16)

    return pl.pallas_call(
        body, name=name, grid=(t // tm,),
        in_specs=[_rows(tm, d), _const((1, d)), _const((1, d))],
        out_specs=_rows(tm, d),
        out_shape=jax.ShapeDtypeStruct((t, d), BF16),
        compiler_params=_cparams(),
    )(x, scale, shift)


def modulate_bwd(dh, xin, dres, scale, name):
    t, d = dh.shape
    tm = _pick(t, (512, 256, 128))

    def body(dh_ref, x_ref, dr_ref, sc_ref, o_ref, dsc_ref, dsh_ref):
        i = pl.program_id(0)
        dh_v = dh_ref[...]
        o_ref[...] = dr_ref[...] + dh_v * (1.0 + sc_ref[...])

        @pl.when(i == 0)
        def _():
            dsc_ref[...] = jnp.zeros_like(dsc_ref)
            dsh_ref[...] = jnp.zeros_like(dsh_ref)

        dsc_ref[...] += jnp.sum(dh_v * x_ref[...], axis=0, keepdims=True)
        dsh_ref[...] += jnp.sum(dh_v, axis=0, keepdims=True)

    return pl.pallas_call(
        body, name=name, grid=(t // tm,),
        in_specs=[_rows(tm, d), _rows(tm, d), _rows(tm, d), _const((1, d))],
        out_specs=[_rows(tm, d), _const((1, d)), _const((1, d))],
        out_shape=[jax.ShapeDtypeStruct((t, d), F32), jax.ShapeDtypeStruct((1, d), F32),
                   jax.ShapeDtypeStruct((1, d), F32)],
        compiler_params=_cparams(),
    )(dh, xin, dres, scale)


PREP_TM = 128


def _halo_specs(tm, width, colblk, order):
    per = tm // HALO
    return [pl.BlockSpec((HALO, width), lambda i: (jnp.maximum(order(i) * per - 1, 0), colblk)),
            pl.BlockSpec((tm, width), lambda i: (order(i), colblk))]


def prep_fwd(proj, conv_a, a_log, dtb):
    t = proj.shape[0]
    tm = PREP_TM
    nt = t // tm
    wq = 3 * D

    def body(prev_ref, cur_ref, bb_ref, aa_ref, cw_ref, al_ref, dt_ref, q_ref, k_ref, v_ref, g_ref, b_ref):
        i = pl.program_id(0)
        flag = jnp.where(i > 0, 1.0, 0.0)
        ext = jnp.concatenate([prev_ref[...] * flag, cur_ref[...]], axis=0)
        rows = tuple(cw_ref[j:j + 1, :] for j in range(4))
        qs, ks, vs, gcs, beta = prep_fn(ext, bb_ref[...], aa_ref[...], rows, al_ref[...], dt_ref[...])
        for h in range(AH):
            q_ref[h] = qs[h]
            k_ref[h] = ks[h]
            v_ref[h] = vs[h]
        g_ref[...] = gcs
        b_ref[...] = beta

    ident = lambda i: i
    hm = pl.BlockSpec((AH, tm, ADK), lambda i: (0, i, 0))
    return pl.pallas_call(
        body, name="prep_fwd", grid=(nt,),
        in_specs=_halo_specs(tm, wq, 0, ident) + [
            _rows(tm, 128, C_BA // 128), _rows(tm, 128, C_BA // 128 + 1),
            _const((4, wq)), _const((1, 128)), _const((1, 128))],
        out_specs=[hm, hm, hm, _rows(tm, 128), _rows(tm, 128)],
        out_shape=[jax.ShapeDtypeStruct((AH, t, ADK), F32)] * 3 + [jax.ShapeDtypeStruct((t, 128), F32)] * 2,
        compiler_params=_cparams(),
    )(proj, proj, proj, proj, conv_a, a_log, dtb)


def prep_bwd(proj, conv_a, a_log, dtb, dq, dk, dv, dgcs, dbeta):
    t = proj.shape[0]
    tm = PREP_TM
    nt = t // tm
    wq = 3 * D
    rev = lambda i: nt - 1 - i

    def body(prev_ref, cur_ref, bb_ref, aa_ref, cw_ref, al_ref, dt_ref,
             dq_ref, dk_ref, dv_ref, dg_ref, db_ref,
             dpre_ref, dbb_ref, daa_ref, dcw_ref, dal_ref, ddt_ref, carry):
        i = pl.program_id(0)
        flag = jnp.where(i < nt - 1, 1.0, 0.0)
        ext = jnp.concatenate([prev_ref[...] * flag, cur_ref[...]], axis=0)
        rows = tuple(cw_ref[j:j + 1, :] for j in range(4))
        _, vjp = jax.vjp(prep_fn, ext, bb_ref[...], aa_ref[...], rows, al_ref[...], dt_ref[...])
        dext, dbb, daa, drows, dal, ddt = vjp((tuple(dq_ref[h] for h in range(AH)),
                                               tuple(dk_ref[h] for h in range(AH)),
                                               tuple(dv_ref[h] for h in range(AH)), dg_ref[...], db_ref[...]))

        @pl.when(i == 0)
        def _():
            carry[...] = jnp.zeros_like(carry)
            dcw_ref[...] = jnp.zeros_like(dcw_ref)
            dal_ref[...] = jnp.zeros_like(dal_ref)
            ddt_ref[...] = jnp.zeros_like(ddt_ref)

        dcur = dext[HALO:]
        dpre_ref[...] = jnp.concatenate([dcur[:tm - HALO], dcur[tm - HALO:] + carry[...]], axis=0).astype(BF16)
        carry[...] = dext[:HALO]
        dbb_ref[...] = dbb.astype(BF16)
        daa_ref[...] = daa.astype(BF16)
        dcw = None
        for j in range(4):
            tj = _onehot_rows(4, j) * drows[j]
            dcw = tj if dcw is None else dcw + tj
        dcw_ref[...] += dcw
        dal_ref[...] += dal
        ddt_ref[...] += ddt

    hm = pl.BlockSpec((AH, tm, ADK), lambda i: (0, rev(i), 0))
    return pl.pallas_call(
        body, name="prep_bwd", grid=(nt,),
        in_specs=_halo_specs(tm, wq, 0, rev) + [
            _rows(tm, 128, C_BA // 128, rev), _rows(tm, 128, C_BA // 128 + 1, rev),
            _const((4, wq)), _const((1, 128)), _const((1, 128)),
            hm, hm, hm, _rows(tm, 128, 0, rev), _rows(tm, 128, 0, rev)],
        out_specs=[_rows(tm, wq, 0, rev), _rows(tm, 128, 0, rev), _rows(tm, 128, 0, rev),
                   _const((4, wq)), _const((1, 128)), _const((1, 128))],
        out_shape=[jax.ShapeDtypeStruct((t, wq), BF16), jax.ShapeDtypeStruct((t, 128), BF16),
                   jax.ShapeDtypeStruct((t, 128), BF16), jax.ShapeDtypeStruct((4, wq), F32),
                   jax.ShapeDtypeStruct((1, 128), F32), jax.ShapeDtypeStruct((1, 128), F32)],
        scratch_shapes=[pltpu.VMEM((HALO, wq), F32)],
        compiler_params=_cparams(),
    )(proj, proj, proj, proj, conv_a, a_log, dtb, dq, dk, dv, dgcs, dbeta)


def _c1_specs(order):
    hm = pl.BlockSpec((AH, CH, ADK), lambda n: (0, order(n), 0))
    col = pl.BlockSpec((AH, 1, CH, 1), lambda n: (0, order(n), 0, 0))
    row = pl.BlockSpec((AH, 1, 1, CH), lambda n: (0, order(n), 0, 0))
    qk = pl.BlockSpec((1, AH, CH, CH), lambda n: (order(n), 0, 0, 0))
    eg = pl.BlockSpec((1, AH, 1, ADK), lambda n: (order(n), 0, 0, 0))
    return hm, col, row, qk, eg


def _heads(ref):
    return jnp.stack([ref[:, h * ADK:(h + 1) * ADK] for h in range(AH)], axis=0)


def c1_fwd(q, k, v, gcol, grow, bcol):
    t = q.shape[1]
    nc = t // CH
    hm, col, row, qks, egs = _c1_specs(lambda n: n)

    def body(q_ref, k_ref, v_ref, gc_ref, gr_ref, bc_ref, u_ref, w_ref, qg_ref, kd_ref, qk_ref, eg_ref):
        u, w, qk, qg, kd, eg = c1_heads(q_ref[...], k_ref[...], v_ref[...],
                                        gc_ref[:, 0], gr_ref[:, 0], bc_ref[:, 0])
        u_ref[...] = u
        w_ref[...] = w
        qg_ref[...] = qg
        kd_ref[...] = kd
        qk_ref[0] = qk
        eg_ref[0] = eg

    return pl.pallas_call(
        body, name="c1_fwd", grid=(nc,),
        in_specs=[hm, hm, hm, col, row, col],
        out_specs=[hm, hm, hm, hm, qks, egs],
        out_shape=[jax.ShapeDtypeStruct((AH, t, ADK), F32)] * 4 + [
            jax.ShapeDtypeStruct((nc, AH, CH, CH), F32), jax.ShapeDtypeStruct((nc, AH, 1, ADK), F32)],
        compiler_params=_cparams(),
    )(q, k, v, gcol, grow, bcol)


def c1_bwd(q, k, v, gcol, grow, bcol, du, dw, dqg, dkd, dqk, deg):
    t = q.shape[1]
    nc = t // CH
    hm, col, row, qks, egs = _c1_specs(lambda n: n)

    def body(q_ref, k_ref, v_ref, gc_ref, gr_ref, bc_ref, du_ref, dw_ref, dqg_ref, dkd_ref, dqk_ref, deg_ref,
             dq_ref, dk_ref, dv_ref, dgc_ref, dgr_ref, dbc_ref):
        _, vjp = jax.vjp(c1_heads, q_ref[...], k_ref[...], v_ref[...], gc_ref[:, 0], gr_ref[:, 0], bc_ref[:, 0])
        dq, dk, dv, dgc, dgr, dbc = vjp((du_ref[...], dw_ref[...], dqk_ref[0], dqg_ref[...], dkd_ref[...],
                                         deg_ref[0]))
        dq_ref[...] = dq
        dk_ref[...] = dk
        dv_ref[...] = dv
        dgc_ref[:, 0] = dgc
        dgr_ref[:, 0] = dgr
        dbc_ref[:, 0] = dbc

    return pl.pallas_call(
        body, name="c1_bwd", grid=(nc,),
        in_specs=[hm, hm, hm, col, row, col, hm, hm, hm, hm, qks, egs],
        out_specs=[hm, hm, hm, col, row, col],
        out_shape=[jax.ShapeDtypeStruct((AH, t, ADK), F32)] * 3 + [
            jax.ShapeDtypeStruct((AH, nc, CH, 1), F32), jax.ShapeDtypeStruct((AH, nc, 1, CH), F32),
            jax.ShapeDtypeStruct((AH, nc, CH, 1), F32)],
        compiler_params=_cparams(),
    )(q, k, v, gcol, grow, bcol, du, dw, dqg, dkd, dqk, deg)


def c2_fwd(u, w, qg, kd, qk, eg, proj, norm_a):
    t = u.shape[1]
    nc = t // CH
    hm, _, _, qks, egs = _c1_specs(lambda n: n)
    tok = pl.BlockSpec((CH, D), lambda n: (n, 0))
    zspec = pl.BlockSpec((CH, D), lambda n: (n, C_Z // D))
    sspec = pl.BlockSpec((1, AH, ADK, ADK), lambda n: (n, 0, 0, 0))

    def body(u_ref, w_ref, qg_ref, kd_ref, qk_ref, eg_ref, z_ref, nw_ref, o_ref, sall_ref, st):
        n = pl.program_id(0)

        @pl.when(n == 0)
        def _():
            st[...] = jnp.zeros_like(st)

        s = st[...]
        sall_ref[0] = s
        og, s2 = c2_heads(s, u_ref[...], w_ref[...], qk_ref[0], qg_ref[...], kd_ref[...], eg_ref[0],
                          _heads(z_ref), nw_ref[...])
        st[...] = s2
        for h in range(AH):
            o_ref[:, h * ADK:(h + 1) * ADK] = og[h].astype(BF16)

    return pl.pallas_call(
        body, name="c2_fwd", grid=(nc,),
        in_specs=[hm, hm, hm, hm, qks, egs, zspec, _const((1, ADK))],
        out_specs=[tok, sspec],
        out_shape=[jax.ShapeDtypeStruct((t, D), BF16), jax.ShapeDtypeStruct((nc, AH, ADK, ADK), F32)],
        scratch_shapes=[pltpu.VMEM((AH, ADK, ADK), F32)],
        compiler_params=_cparams(),
    )(u, w, qg, kd, qk, eg, proj, norm_a)


def c2_bwd(u, w, qg, kd, qk, eg, proj, norm_a, sall, do):
    t = u.shape[1]
    nc = t // CH
    rev = lambda n: nc - 1 - n
    hm, _, _, qks, egs = _c1_specs(rev)
    tok = pl.BlockSpec((CH, D), lambda n: (rev(n), 0))
    zspec = pl.BlockSpec((CH, D), lambda n: (rev(n), C_Z // D))
    sspec = pl.BlockSpec((1, AH, ADK, ADK), lambda n: (rev(n), 0, 0, 0))

    def body(u_ref, w_ref, qg_ref, kd_ref, qk_ref, eg_ref, z_ref, nw_ref, sall_ref, do_ref,
             du_ref, dw_ref, dqg_ref, dkd_ref, dqk_ref, deg_ref, dz_ref, dnw_ref, dst):
        n = pl.program_id(0)

        @pl.when(n == 0)
        def _():
            dst[...] = jnp.zeros_like(dst)
            dnw_ref[...] = jnp.zeros_like(dnw_ref)

        _, vjp = jax.vjp(c2_heads, sall_ref[0], u_ref[...], w_ref[...], qk_ref[0], qg_ref[...], kd_ref[...],
                         eg_ref[0], _heads(z_ref), nw_ref[...])
        ds, du, dw, dqk, dqg, dkd, deg, dz, dn = vjp((_heads(do_ref), dst[...]))
        dst[...] = ds
        du_ref[...] = du
        dw_ref[...] = dw
        dqg_ref[...] = dqg
        dkd_ref[...] = dkd
        dqk_ref[0] = dqk
        deg_ref[0] = deg
        for h in range(AH):
            dz_ref[:, h * ADK:(h + 1) * ADK] = dz[h].astype(BF16)
        dnw_ref[...] += dn

    return pl.pallas_call(
        body, name="c2_bwd", grid=(nc,),
        in_specs=[hm, hm, hm, hm, qks, egs, zspec, _const((1, ADK)), sspec, tok],
        out_specs=[hm, hm, hm, hm, qks, egs, tok, _const((1, ADK))],
        out_shape=[jax.ShapeDtypeStruct((AH, t, ADK), F32)] * 4 + [
            jax.ShapeDtypeStruct((nc, AH, CH, CH), F32), jax.ShapeDtypeStruct((nc, AH, 1, ADK), F32),
            jax.ShapeDtypeStruct((t, D), BF16), jax.ShapeDtypeStruct((1, ADK), F32)],
        scratch_shapes=[pltpu.VMEM((AH, ADK, ADK), F32)],
        compiler_params=_cparams(),
    )(u, w, qg, kd, qk, eg, proj, norm_a, sall, do)


NQB = TQ // CH
NKB = 2 * TQ // CH
NDIST = BPREV + 1
KLO = -(NQB - 2)
NPAIR = NKB - 1 - KLO + 1


def bias_table(rel_bias):
    nh = rel_bias.shape[0]
    relx = jnp.concatenate([rel_bias, jnp.broadcast_to(rel_bias[:, -1:], (nh, CH * BPREV + 2 * CH - 1 - RELSZ))],
                           axis=1)
    t = jnp.stack([relx[:, CH * k:CH * k + 2 * CH - 1] for k in range(NDIST)], axis=1)
    trev = t[:, :, ::-1]
    g2 = jnp.concatenate([trev[:, :, CH - 1:], jnp.zeros((nh, NDIST, 1), F32), trev[:, :, :CH - 1]], axis=2)
    flat = jnp.tile(g2, (1, 1, CH + 1))[:, :, :CH * (2 * CH - 1)]
    blk = flat.reshape(nh, NDIST, CH, 2 * CH - 1)[..., :CH]
    neg = jnp.full((nh, NQB - 1, CH, CH), NEG, F32)
    asc = jnp.concatenate([neg, blk, neg], axis=1)
    return jnp.concatenate([asc[:, 1:], asc[:, :-1]], axis=-1)


def assemble_bias(tab):
    rows = [jnp.concatenate([tab[NQB + a - 2 * b - KLO] for b in range(NKB // 2)], axis=1) for a in range(NQB)]
    return jnp.concatenate(rows, axis=0)


def bias_table_bwd_layout(dtab):
    nh = dtab.shape[0]
    dasc = (jnp.pad(dtab[..., :CH], ((0, 0), (1, 0), (0, 0), (0, 0)))
            + jnp.pad(dtab[..., CH:], ((0, 0), (0, 1), (0, 0), (0, 0))))
    dblk = dasc[:, NQB - 1:NQB - 1 + NDIST]
    dr = jnp.pad(dblk, ((0, 0), (0, 0), (0, 0), (0, CH - 1)))
    flat = jnp.pad(dr.reshape(nh, NDIST, CH * (2 * CH - 1)), ((0, 0), (0, 0), (0, 3 * CH)))
    return flat.reshape(nh, NDIST, CH + 1, 2 * CH).transpose(0, 2, 1, 3).reshape(nh, CH + 1, NDIST * 2 * CH)


def _fold_matrix_np():
    f = np.zeros((NDIST * 2 * CH, 384), np.float32)
    for k in range(NDIST):
        s = k
        for xx in range(2 * CH):
            if xx == CH:
                continue
            m = CH - 1 - xx if xx < CH else 3 * CH - 1 - xx
            f[s * 2 * CH + xx, min(CH * k + m, RELSZ - 1)] = 1.0
    return f


def relbias_reduce(dlay):
    nh, rows, cols = dlay.shape
    rpad = (-rows) % 8
    dlay = jnp.pad(dlay, ((0, 0), (0, rpad), (0, 0)))
    fold = jnp.asarray(_fold_matrix_np())

    def body(d_ref, f_ref, o_ref):
        cs = jnp.sum(d_ref[0], axis=0, keepdims=True)
        o_ref[0] = _mmh(jnp.broadcast_to(cs, (8, cols)), f_ref[...])

    out = pl.pallas_call(
        body, name="relbias_reduce", grid=(nh,),
        in_specs=[pl.BlockSpec((1, rows + rpad, cols), lambda h: (h, 0, 0)), _const((cols, 384))],
        out_specs=pl.BlockSpec((1, 8, 384), lambda h: (h, 0, 0)),
        out_shape=jax.ShapeDtypeStruct((nh, 8, 384), F32),
        compiler_params=_cparams(),
    )(dlay, fold)
    return out[:, 0, :RELSZ]


def attn_fwd(proj, bias):
    t = proj.shape[0]
    nt = t // TQ
    cb = C_QKVB // 128

    def body(q_ref, kp_ref, kc_ref, vp_ref, vc_ref, b_ref, o_ref):
        i = pl.program_id(1)
        firstf = jnp.where(i == 0, 1.0, 0.0)
        o_ref[...] = attn_pair(q_ref[...], kp_ref[...], kc_ref[...], vp_ref[...], vc_ref[...],
                               b_ref[...], firstf).astype(BF16)

    def blk(off, prev):
        if prev:
            return pl.BlockSpec((TQ, 128), lambda p, i: (jnp.maximum(i - 1, 0), cb + off + p))
        return pl.BlockSpec((TQ, 128), lambda p, i: (i, cb + off + p))

    return pl.pallas_call(
        body, name="attn_fwd", grid=(BH // 2, nt),
        in_specs=[blk(0, False), blk(8, True), blk(8, False), blk(16, True), blk(16, False),
                  pl.BlockSpec((2, NPAIR, CH, 2 * CH), lambda p, i: (p, 0, 0, 0))],
        out_specs=pl.BlockSpec((TQ, 128), lambda p, i: (i, p)),
        out_shape=jax.ShapeDtypeStruct((t, D), BF16),
        compiler_params=_cparams(2),
    )(proj, proj, proj, proj, proj, bias)


def attn_bwd(proj, bias, do):
    t = proj.shape[0]
    nt = t // TQ
    cb = C_QKVB // 128

    def body(q_ref, kp_ref, kc_ref, vp_ref, vc_ref, b_ref, do_ref,
             dq_ref, dk_ref, dv_ref, db_ref, ck, cv):
        i = pl.program_id(1)

        @pl.when(i == 0)
        def _():
            ck[...] = jnp.zeros_like(ck)
            cv[...] = jnp.zeros_like(cv)
            db_ref[...] = jnp.zeros_like(db_ref)

        @pl.when(i < nt)
        def _():
            firstf = jnp.where(i == 0, 1.0, 0.0)
            _, vjp = jax.vjp(lambda q, kp, kc, vp, vc, b: attn_pair(q, kp, kc, vp, vc, b, firstf),
                             q_ref[...].astype(F32), kp_ref[...].astype(F32), kc_ref[...].astype(F32),
                             vp_ref[...].astype(F32), vc_ref[...].astype(F32), b_ref[...])
            dq, dkp, dkc, dvp, dvc, db = vjp(do_ref[...])
            dq_ref[...] = dq.astype(BF16)
            dk_ref[...] = (ck[...] + dkp).astype(BF16)
            dv_ref[...] = (cv[...] + dvp).astype(BF16)
            ck[...] = dkc
            cv[...] = dvc
            db_ref[...] += db

        @pl.when(i == nt)
        def _():
            dk_ref[...] = ck[...].astype(BF16)
            dv_ref[...] = cv[...].astype(BF16)

    def blk(off, prev):
        if prev:
            return pl.BlockSpec((TQ, 128), lambda p, i: (jnp.clip(i - 1, 0, nt - 1), cb + off + p))
        return pl.BlockSpec((TQ, 128), lambda p, i: (jnp.minimum(i, nt - 1), cb + off + p))

    own = pl.BlockSpec((TQ, 128), lambda p, i: (jnp.minimum(i, nt - 1), p))
    lag = pl.BlockSpec((TQ, 128), lambda p, i: (jnp.maximum(i - 1, 0), p))
    return pl.pallas_call(
        body, name="attn_bwd", grid=(BH // 2, nt + 1),
        in_specs=[blk(0, False), blk(8, True), blk(8, False), blk(16, True), blk(16, False),
                  pl.BlockSpec((2, NPAIR, CH, 2 * CH), lambda p, i: (p, 0, 0, 0)), own],
        out_specs=[own, lag, lag, pl.BlockSpec((2, NPAIR, CH, 2 * CH), lambda p, i: (p, 0, 0, 0))],
        out_shape=[jax.ShapeDtypeStruct((t, D), BF16)] * 3 + [jax.ShapeDtypeStruct((BH, NPAIR, CH, 2 * CH), F32)],
        scratch_shapes=[pltpu.VMEM((TQ, 128), F32), pltpu.VMEM((TQ, 128), F32)],
        compiler_params=_cparams(2),
    )(proj, proj, proj, proj, proj, bias, do)


MERGE_TM = 256


def merge_fwd(x, oa, ob, proj, vecs, wa, wb, wo):
    t = x.shape[0]
    tm = MERGE_TM
    names = ("bga", "bgb", "gate_t", "g1", "b1", "scale_f", "shift_f")

    def body(x_ref, oa_ref, ob_ref, gra_ref, grb_ref, *rest):
        vrefs = rest[:7]
        wa_ref, wb_ref, wo_ref, y_ref, h_ref = rest[7:]
        vv = [r[...] for r in vrefs]
        zero = jnp.zeros((tm, D), F32)
        y1, _ = merge_fn(x_ref[...], oa_ref[...], ob_ref[...], gra_ref[...], grb_ref[...], zero, zero, zero,
                         *vv, wa_ref[...], wb_ref[...], wo_ref[...])
        y_ref[...] = y1
        h_ref[...] = (y1 * (1.0 + vv[5]) + vv[6]).astype(BF16)

    return pl.pallas_call(
        body, name="merge_fwd", grid=(t // tm,),
        in_specs=[_rows(tm, D), _rows(tm, D), _rows(tm, D), _rows(tm, D, C_GATE // D), _rows(tm, D, C_GATE // D + 1)]
        + [_const((1, D))] * 7 + [_const((D, D))] * 3,
        out_specs=[_rows(tm, D), _rows(tm, D)],
        out_shape=[jax.ShapeDtypeStruct((t, D), F32), jax.ShapeDtypeStruct((t, D), BF16)],
        compiler_params=_cparams(),
    )(x, oa, ob, proj, proj, *[vecs[n] for n in names], wa, wb, wo)


def merge_bwd(x, oa, ob, proj, vecs, wa, wb, wo, dy1):
    t = x.shape[0]
    tm = MERGE_TM
    names = ("bga", "bgb", "gate_t", "g1", "b1", "scale_f", "shift_f")

    def body(x_ref, oa_ref, ob_ref, gra_ref, grb_ref, *rest):
        vrefs = rest[:7]
        wa_ref, wb_ref, wo_ref, dy_ref = rest[7:11]
        (dx_ref, doa_ref, dob_ref, dga_ref, dgb_ref, mg_ref, dmix_ref, dpa_ref, dpb_ref,
         dbga_ref, dbgb_ref, dgt_ref, dg1_ref, db1_ref) = rest[11:]
        i = pl.program_id(0)
        vv = [r[...] for r in vrefs]
        zero = jnp.zeros((tm, D), F32)

        def f(x_, oa_, ob_, gra_, grb_, ppa, ppb, pmix, bga, bgb, gate_t, g1, b1):
            return merge_fn(x_, oa_, ob_, gra_, grb_, ppa, ppb, pmix, bga, bgb, gate_t, g1, b1, vv[5], vv[6],
                            wa_ref[...], wb_ref[...], wo_ref[...])

        _, vjp, merged = jax.vjp(f, x_ref[...], oa_ref[...].astype(F32), ob_ref[...].astype(F32),
                                 gra_ref[...], grb_ref[...], zero, zero, zero, *vv[:5], has_aux=True)
        dx, doa, dob, dga, dgb, dpa, dpb, dmix, dbga, dbgb, dgt, dg1, db1 = vjp(dy_ref[...])
        dx_ref[...] = dx
        doa_ref[...] = doa
        dob_ref[...] = dob
        dga_ref[...] = dga.astype(BF16)
        dgb_ref[...] = dgb.astype(BF16)
        mg_ref[...] = merged.astype(BF16)
        dmix_ref[...] = dmix.astype(BF16)
        dpa_ref[...] = dpa.astype(BF16)
        dpb_ref[...] = dpb.astype(BF16)
        accs = (dbga_ref, dbgb_ref, dgt_ref, dg1_ref, db1_ref)

        @pl.when(i == 0)
        def _():
            for a in accs:
                a[...] = jnp.zeros_like(a)

        for a, val in zip(accs, (dbga, dbgb, dgt, dg1, db1)):
            a[...] += val

    return pl.pallas_call(
        body, name="merge_bwd", grid=(t // tm,),
        in_specs=[_rows(tm, D), _rows(tm, D), _rows(tm, D), _rows(tm, D, C_GATE // D), _rows(tm, D, C_GATE // D + 1)]
        + [_const((1, D))] * 7 + [_const((D, D))] * 3 + [_rows(tm, D)],
        out_specs=[_rows(tm, D)] * 9 + [_const((1, D))] * 5,
        out_shape=[jax.ShapeDtypeStruct((t, D), F32)] * 3 + [jax.ShapeDtypeStruct((t, D), BF16)] * 6
        + [jax.ShapeDtypeStruct((1, D), F32)] * 5,
        compiler_params=_cparams(),
    )(x, oa, ob, proj, proj, *[vecs[n] for n in names], wa, wb, wo, dy1)


FFN_TM = 128


def ffn_act_fwd(up, conv_w, bconv):
    t, wdt = up.shape
    tm = FFN_TM

    def body(prev_ref, cur_ref, cw_ref, bc_ref, a_ref):
        i = pl.program_id(0)
        flag = jnp.where(i > 0, 1.0, 0.0)
        ext = jnp.concatenate([prev_ref[...] * flag, cur_ref[...]], axis=0)
        rows = tuple(cw_ref[j:j + 1, :] for j in range(3))
        a_ref[...] = ffn_act_fn(ext, rows, bc_ref[...]).astype(BF16)

    return pl.pallas_call(
        body, name="ffn_act_fwd", grid=(t // tm,),
        in_specs=_halo_specs(tm, wdt, 0, lambda i: i) + [_const((3, wdt)), _const((1, wdt))],
        out_specs=_rows(tm, DFF),
        out_shape=jax.ShapeDtypeStruct((t, DFF), BF16),
        compiler_params=_cparams(),
    )(up, up, conv_w, bconv)


def ffn_act_bwd(up, conv_w, bconv, da):
    t, wdt = up.shape
    tm = FFN_TM
    nt = t // tm
    rev = lambda i: nt - 1 - i

    def body(prev_ref, cur_ref, cw_ref, bc_ref, da_ref, dup_ref, dcw_ref, dbc_ref, carry):
        i = pl.program_id(0)
        flag = jnp.where(i < nt - 1, 1.0, 0.0)
        ext = jnp.concatenate([prev_ref[...] * flag, cur_ref[...]], axis=0)
        rows = tuple(cw_ref[j:j + 1, :] for j in range(3))
        _, vjp = jax.vjp(ffn_act_fn, ext, rows, bc_ref[...])
        dext, drows, dbc = vjp(da_ref[...])

        @pl.when(i == 0)
        def _():
            carry[...] = jnp.zeros_like(carry)
            dcw_ref[...] = jnp.zeros_like(dcw_ref)
            dbc_ref[...] = jnp.zeros_like(dbc_ref)

        dcur = dext[HALO:]
        dup_ref[...] = jnp.concatenate([dcur[:tm - HALO], dcur[tm - HALO:] + carry[...]], axis=0).astype(BF16)
        carry[...] = dext[:HALO]
        dcw = None
        for j in range(3):
            tj = _onehot_rows(3, j) * drows[j]
            dcw = tj if dcw is None else dcw + tj
        dcw_ref[...] += dcw
        dbc_ref[...] += dbc

    return pl.pallas_call(
        body, name="ffn_act_bwd", grid=(nt,),
        in_specs=_halo_specs(tm, wdt, 0, rev) + [_const((3, wdt)), _const((1, wdt)), _rows(tm, DFF, 0, rev)],
        out_specs=[_rows(tm, wdt, 0, rev), _const((3, wdt)), _const((1, wdt))],
        out_shape=[jax.ShapeDtypeStruct((t, wdt), BF16), jax.ShapeDtypeStruct((3, wdt), F32),
                   jax.ShapeDtypeStruct((1, wdt), F32)],
        scratch_shapes=[pltpu.VMEM((HALO, wdt), F32)],
        compiler_params=_cparams(),
    )(up, up, conv_w, bconv, da)


HEAD_TM = 256


def head_fwd_bwd(a, y1, tgt, gate_f, g2, b2, wd):
    t = a.shape[0]
    tm = HEAD_TM

    def body(a_ref, y_ref, t_ref, gf_ref, g2_ref, b2_ref, wd_ref,
             da_ref, dy_ref, dffn_ref, dgf_ref, dg2_ref, db2_ref, loss_ref):
        i = pl.program_id(0)
        zero = jnp.zeros((tm, D), F32)

        def f(a_, y_, pf, gf, g2_, b2_):
            return head_fn(a_, y_, pf, gf, g2_, b2_, t_ref[...], wd_ref[...])

        loss, vjp = jax.vjp(f, a_ref[...].astype(F32), y_ref[...], zero, gf_ref[...], g2_ref[...], b2_ref[...])
        da, dy, dffn, dgf, dg2, db2 = vjp(jnp.ones((), F32))
        da_ref[...] = da
        dy_ref[...] = dy
        dffn_ref[...] = dffn.astype(BF16)
        accs = (dgf_ref, dg2_ref, db2_ref, loss_ref)

        @pl.when(i == 0)
        def _():
            for r in accs:
                r[...] = jnp.zeros_like(r)

        dgf_ref[...] += dgf
        dg2_ref[...] += dg2
        db2_ref[...] += db2
        loss_ref[...] += loss * jnp.ones((1, 128), F32)

    return pl.pallas_call(
        body, name="head_fwd_bwd", grid=(t // tm,),
        in_specs=[_rows(tm, DFF), _rows(tm, D), _rows(tm, D), _const((1, D)), _const((1, D)), _const((1, D)),
                  _const((DFF, D))],
        out_specs=[_rows(tm, DFF), _rows(tm, D), _rows(tm, D), _const((1, D)), _const((1, D)), _const((1, D)),
                   _const((1, 128))],
        out_shape=[jax.ShapeDtypeStruct((t, DFF), F32), jax.ShapeDtypeStruct((t, D), F32),
                   jax.ShapeDtypeStruct((t, D), BF16)] + [jax.ShapeDtypeStruct((1, D), F32)] * 3
        + [jax.ShapeDtypeStruct((1, 128), F32)],
        compiler_params=_cparams(),
    )(a, y1, tgt, gate_f, g2, b2, wd)


def ada_fwd(c_all, w_sh, b_sh):
    def body(c_ref, w_ref, b_ref, o_ref):
        o_ref[...] = _mmh(_silu(c_ref[...]), w_ref[...]) + b_ref[...]

    n = w_sh.shape[1]
    return pl.pallas_call(
        body, name="ada_fwd", out_shape=jax.ShapeDtypeStruct((NDEV, n), F32),
        in_specs=[pl.BlockSpec(memory_space=pltpu.VMEM)] * 3,
        out_specs=pl.BlockSpec(memory_space=pltpu.VMEM),
        compiler_params=pltpu.CompilerParams(vmem_limit_bytes=VMEM_LIMIT),
    )(c_all, w_sh, b_sh)


def ada_wgrad(c_all_t, dmod_sh):
    def body(c_ref, d_ref, o_ref):
        o_ref[...] = _mmh(_silu(c_ref[...]), d_ref[...])

    return pl.pallas_call(
        body, name="ada_wgrad", out_shape=jax.ShapeDtypeStruct((c_all_t.shape[0], dmod_sh.shape[1]), F32),
        in_specs=[pl.BlockSpec(memory_space=pltpu.VMEM)] * 2,
        out_specs=pl.BlockSpec(memory_space=pltpu.VMEM),
        compiler_params=pltpu.CompilerParams(vmem_limit_bytes=VMEM_LIMIT),
    )(c_all_t, dmod_sh)


def adamw(gparts, w, m, v, name):
    p, r, c = gparts.shape
    tr = r if r <= 256 else _pick(r, (256, 128, 64, 32, 16, 8))
    c1 = 1.0 - B1 ** STEP
    c2 = 1.0 - B2 ** STEP

    def body(g_ref, w_ref, m_ref, v_ref, go_ref, d_ref, mo_ref, vo_ref):
        g = g_ref[0].astype(F32)
        for s in range(1, p):
            g = g + g_ref[s].astype(F32)
        mn = B1 * m_ref[...] + (1.0 - B1) * g
        vn = B2 * v_ref[...] + (1.0 - B2) * (g * g)
        go_ref[...] = g
        d_ref[...] = -LR * ((mn / c1) / (jnp.sqrt(vn / c2) + AEPS) + WD * w_ref[...])
        mo_ref[...] = mn
        vo_ref[...] = vn

    spec = pl.BlockSpec((tr, c), lambda i: (i, 0))
    return pl.pallas_call(
        body, name=name, grid=(r // tr,),
        in_specs=[pl.BlockSpec((p, tr, c), lambda i: (0, i, 0)), spec, spec, spec],
        out_specs=[spec] * 4,
        out_shape=[jax.ShapeDtypeStruct((r, c), F32)] * 4,
        compiler_params=_cparams(),
    )(gparts, w, m, v)


def _me():
    x, y, c = lax.axis_index("x"), lax.axis_index("y"), lax.axis_index("c")
    return x, y, c, 4 * x + 2 * y + c


def _peer(x, y, c, d):
    px = 1 - x if (d >> 2) & 1 else x
    py = 1 - y if (d >> 1) & 1 else y
    pc = 1 - c if d & 1 else c
    return (px, py, pc), 4 * px + 2 * py + pc


def _exchange(arrs, name, scatter):
    n = len(arrs)

    def body(*refs):
        ins, outs = refs[:n], refs[n:2 * n]
        send, recv, lsem = refs[2 * n:]
        x, y, c, me = _me()
        remote, local = [], []
        for k in range(n):
            src = ins[k].at[me] if scatter else ins[k]
            cp = pltpu.make_async_copy(src, outs[k].at[me], lsem.at[k])
            cp.start()
            local.append(cp)
            for d in range(1, NDEV):
                dev, pid = _peer(x, y, c, d)
                src = ins[k].at[pid] if scatter else ins[k]
                cp = pltpu.make_async_remote_copy(src_ref=src, dst_ref=outs[k].at[me],
                                                  send_sem=send.at[k, d - 1], recv_sem=recv.at[k, d - 1],
                                                  device_id=dev, device_id_type=pl.DeviceIdType.MESH)
                cp.start()
                remote.append(cp)
        for cp in remote:
            cp.wait()
        for cp in local:
            cp.wait()

    shapes = [a.shape if scatter else (NDEV,) + a.shape for a in arrs]
    return pl.pallas_call(
        body, name=name,
        in_specs=[pl.BlockSpec(memory_space=pl.ANY)] * n,
        out_specs=[pl.BlockSpec(memory_space=pl.ANY)] * n,
        out_shape=[jax.ShapeDtypeStruct(s, a.dtype) for s, a in zip(shapes, arrs)],
        scratch_shapes=[pltpu.SemaphoreType.DMA((n, NDEV - 1)), pltpu.SemaphoreType.DMA((n, NDEV - 1)),
                        pltpu.SemaphoreType.DMA((n,))],
        compiler_params=pltpu.CompilerParams(has_side_effects=True),
    )(*arrs)


def all_gather(arrs, name):
    return _exchange(arrs, name, False)


def all_to_all(arrs, name):
    return _exchange(arrs, name, True)


def _to_cat(w_in):
    k = w_in.shape[0]
    ba = jnp.zeros((k, NCAT - C_BA), w_in.dtype)
    ba = ba.at[:, 0:8].set(w_in[:, 4096:4104]).at[:, 128:136].set(w_in[:, 4104:4112])
    return jnp.concatenate([w_in[:, 0:3072], w_in[:, 3072:4096], w_in[:, 4112:7184], w_in[:, 7184:9232], ba], axis=1)


def _from_cat(dw):
    return jnp.concatenate([dw[:, 0:3072], dw[:, 3072:4096], dw[:, C_BA:C_BA + 8], dw[:, C_BA + 128:C_BA + 136],
                            dw[:, 4096:7168], dw[:, 7168:9216]], axis=1)


def _pad128(v):
    return jnp.pad(v, ((0, 0), (0, 128 - v.shape[1])))


def local_step(x, tgt, mod, wts, small):
    t = x.shape[0]
    nc = t // CH
    shift_t, scale_t, gate_t, shift_f, scale_f, gate_f = mod
    wcat = _to_cat(wts["w_in"])
    a_log = _pad128(small["a_log"])
    dtb = _pad128(small["dt_bias"])
    vecs = dict(bga=small["b_gate"][:, :D], bgb=small["b_gate"][:, D:], gate_t=gate_t, g1=small["ln1_g"],
                b1=small["ln1_b"], scale_f=scale_f, shift_f=shift_f)

    h1 = modulate(x, scale_t, shift_t, "modulate_t")
    proj = matmul(h1, wcat, F32, "in_proj")
    q, k, v, gcs, beta = prep_fwd(proj, small["conv_a"], a_log, dtb)

    def col4(a):
        return a[:, :AH].reshape(nc, CH, AH).transpose(2, 0, 1)[..., None]

    gcol = col4(gcs)
    grow = gcol.reshape(AH, nc, 1, CH)
    bcol = col4(beta)
    u, w, qg, kd, qk, eg = c1_fwd(q, k, v, gcol, grow, bcol)
    oa, sall = c2_fwd(u, w, qg, kd, qk, eg, proj, small["norm_a"])
    bias = bias_table(small["rel_bias"])
    ob = attn_fwd(proj, bias)
    y1, h2 = merge_fwd(x, oa, ob, proj, vecs, wts["w_a"], wts["w_b"], wts["w_o"])
    up = matmul(h2, wts["w_up"], F32, "up_proj")
    a = ffn_act_fwd(up, small["conv_ffn"], small["b_conv_ffn"])

    da, dy1_res, dffn, dgate_f, dg2, db2, loss = head_fwd_bwd(a, y1, tgt, gate_f, small["ln2_g"], small["ln2_b"],
                                                            wts["w_down"])
    g_w_down = matmul(a.T, dffn, F32, "wgrad_down")
    dup, g_conv_ffn, g_bconv = ffn_act_bwd(up, small["conv_ffn"], small["b_conv_ffn"], da)
    dh2 = matmul(dup, wts["w_up"].T, F32, "dgrad_up")
    g_w_up = matmul(h2.T, dup, F32, "wgrad_up")
    dy1, dscale_f, dshift_f = modulate_bwd(dh2, y1, dy1_res, scale_f, "modulate_f_bwd")
    (dx_res, doa, dob, dga, dgb, merged, dmix, dpa, dpb,
     dbga, dbgb, dgate_t, dg1, db1) = merge_bwd(x, oa, ob, proj, vecs, wts["w_a"], wts["w_b"], wts["w_o"], dy1)
    g_w_o = matmul(merged.T, dmix, F32, "wgrad_o")
    g_w_a = matmul(oa.T, dpa, F32, "wgrad_a")
    g_w_b = matmul(ob.T, dpb, F32, "wgrad_b")
    dqb, dkb, dvb, dbias = attn_bwd(proj, bias, dob)
    g_rel = relbias_reduce(bias_table_bwd_layout(dbias))
    du, dw, dqg, dkd, dqk, deg, dz, g_norm = c2_bwd(u, w, qg, kd, qk, eg, proj, small["norm_a"], sall, doa)
    dq, dk, dv, dgc, dgr, dbc = c1_bwd(q, k, v, gcol, grow, bcol, du, dw, dqg, dkd, dqk, deg)

    def from4(a4):
        return _pad128(a4[..., 0].transpose(1, 2, 0).reshape(t, AH))

    dgcs = from4(dgc) + from4(dgr.reshape(AH, nc, CH, 1))
    dbeta = from4(dbc)
    dpre, dbb, daa, g_conv_a, g_alog, g_dtb = prep_bwd(proj, small["conv_a"], a_log, dtb, dq, dk, dv, dgcs, dbeta)
    dba = jnp.concatenate([dbb, daa, jnp.zeros((t, NCAT - C_BA - 256), BF16)], axis=1)
    dproj = jnp.concatenate([dpre, dz, dqb, dkb, dvb, dga, dgb, dba], axis=1)
    dh1 = matmul(dproj, wcat.T, F32, "dgrad_in")
    g_wcat = matmul(h1.T, dproj, F32, "wgrad_in")
    grad_x, dscale_t, dshift_t = modulate_bwd(dh1, x, dx_res, scale_t, "modulate_t_bwd")

    dmod = (dshift_t, dscale_t, dgate_t, dshift_f, dscale_f, dgate_f)
    grads = dict(w_in=_from_cat(g_wcat), w_up=g_w_up, w_down=g_w_down, w_a=g_w_a, w_b=g_w_b, w_o=g_w_o,
                 conv_a=g_conv_a, rel_bias=g_rel, conv_ffn=g_conv_ffn,
                 b_gate=jnp.concatenate([dbga, dbgb], axis=1), a_log=g_alog[:, :AH], dt_bias=g_dtb[:, :AH],
                 norm_a=g_norm, ln1_g=dg1, ln1_b=db1, b_conv_ffn=g_bconv, ln2_g=dg2, ln2_b=db2)
    return loss[0, 0], grad_x, dmod, grads


_REP = {}
_off = 0
for _n, _wd, _pw in (("b_ada", 6144, 6144), ("b_gate", 2048, 2048), ("a_log", 8, 128), ("dt_bias", 8, 128),
                     ("norm_a", 128, 128), ("ln1_g", 1024, 1024), ("ln1_b", 1024, 1024),
                     ("b_conv_ffn", 5632, 5632), ("ln2_g", 1024, 1024), ("ln2_b", 1024, 1024), ("loss", 1, 128)):
    _REP[_n] = (_off, _wd, _pw)
    _off += _pw
REP_LEN = _off
REP_NAMES = [n for n in _REP if n != "loss"]
_SH = (("conv_a", (4, 384)), ("rel_bias", (16, 40)), ("conv_ffn", (3, 704)))
SH_LEN = 4352


def _pack_rep(vals):
    parts = []
    for n, (_, wd, pw) in _REP.items():
        a = vals.get(n)
        a = jnp.zeros((1, pw), F32) if a is None else jnp.pad(a.reshape(1, wd), ((0, 0), (0, pw - wd)))
        parts.append(a)
    return jnp.concatenate(parts, axis=1)


def _unpack_rep(vec, name):
    o, wd, _ = _REP[name]
    return vec[:, o:o + wd]


def _pack_sh(vals):
    parts = [vals[n].reshape(vals[n].shape[:-2] + (-1,)) for n, _ in _SH]
    a = jnp.concatenate(parts, axis=-1)
    return jnp.pad(a, [(0, 0)] * (a.ndim - 1) + [(0, SH_LEN - a.shape[-1])])


def _unpack_sh(vec, name):
    o = 0
    for n, shp in _SH:
        sz = shp[0] * shp[1]
        if n == name:
            return vec[0, o:o + sz].reshape(shp)
        o += sz
    raise KeyError(name)


def _col_shards(a, n):
    return a.reshape(a.shape[0], NDEV, n).transpose(1, 0, 2)


def kernel(x, c, w_ada, b_ada, w_in, b_gate, conv_a, a_log, dt_bias, norm_a, rel_bias, w_branch_a, w_branch_b, w_o, ln1_g, ln1_b, w_up, conv_ffn, b_conv_ffn, w_down, ln2_g, ln2_b, loss_target, m_w_ada, m_b_ada, m_w_in, m_b_gate, m_conv_a, m_a_log, m_dt_bias, m_norm_a, m_rel_bias, m_w_branch_a, m_w_branch_b, m_w_o, m_ln1_g, m_ln1_b, m_w_up, m_conv_ffn, m_b_conv_ffn, m_w_down, m_ln2_g, m_ln2_b, v_w_ada, v_b_ada, v_w_in, v_b_gate, v_conv_a, v_a_log, v_dt_bias, v_norm_a, v_rel_bias, v_w_branch_a, v_w_branch_b, v_w_o, v_ln1_g, v_ln1_b, v_w_up, v_conv_ffn, v_b_conv_ffn, v_w_down, v_ln2_g, v_ln2_b):
    W = dict(w_ada=w_ada, b_ada=b_ada, w_in=w_in, b_gate=b_gate, conv_a=conv_a, a_log=a_log, dt_bias=dt_bias,
             norm_a=norm_a, rel_bias=rel_bias, w_branch_a=w_branch_a, w_branch_b=w_branch_b, w_o=w_o, ln1_g=ln1_g,
             ln1_b=ln1_b, w_up=w_up, conv_ffn=conv_ffn, b_conv_ffn=b_conv_ffn, w_down=w_down, ln2_g=ln2_g,
             ln2_b=ln2_b)
    M = dict(w_ada=m_w_ada, b_ada=m_b_ada, w_in=m_w_in, b_gate=m_b_gate, conv_a=m_conv_a, a_log=m_a_log,
             dt_bias=m_dt_bias, norm_a=m_norm_a, rel_bias=m_rel_bias, w_branch_a=m_w_branch_a,
             w_branch_b=m_w_branch_b, w_o=m_w_o, ln1_g=m_ln1_g, ln1_b=m_ln1_b, w_up=m_w_up, conv_ffn=m_conv_ffn,
             b_conv_ffn=m_b_conv_ffn, w_down=m_w_down, ln2_g=m_ln2_g, ln2_b=m_ln2_b)
    V = dict(w_ada=v_w_ada, b_ada=v_b_ada, w_in=v_w_in, b_gate=v_b_gate, conv_a=v_conv_a, a_log=v_a_log,
             dt_bias=v_dt_bias, norm_a=v_norm_a, rel_bias=v_rel_bias, w_branch_a=v_w_branch_a,
             w_branch_b=v_w_branch_b, w_o=v_w_o, ln1_g=v_ln1_g, ln1_b=v_ln1_b, w_up=v_w_up, conv_ffn=v_conv_ffn,
             b_conv_ffn=v_b_conv_ffn, w_down=v_w_down, ln2_g=v_ln2_g, ln2_b=v_ln2_b)
    W, M, V = ({n: a[0] for n, a in dct.items()} for dct in (W, M, V))
    me = 4 * lax.axis_index("x") + 2 * lax.axis_index("y") + lax.axis_index("c")
    big = ("w_in", "w_up", "w_down", "w_branch_a", "w_branch_b", "w_o")

    gathered = all_gather([W[n].astype(BF16) for n in big], "gather_weights")
    g_in, g_up, g_down, g_a, g_b, g_o = gathered
    wts = dict(w_in=g_in.transpose(1, 0, 2).reshape(D, -1), w_up=g_up.transpose(1, 0, 2).reshape(D, -1),
               w_down=g_down.reshape(DFF, D), w_a=g_a.reshape(D, D), w_b=g_b.reshape(D, D), w_o=g_o.reshape(D, D))
    c_all, sh_all = all_gather([c, _pack_sh({n: W[n] for n, _ in _SH})[None]], "gather_small")
    c_all = c_all.reshape(NDEV, D)
    sh_all = sh_all.reshape(NDEV, SH_LEN)

    def full_small(name, shp):
        o = 0
        for n, s in _SH:
            if n == name:
                break
            o += s[0] * s[1]
        sz = shp[0] * shp[1]
        return sh_all[:, o:o + sz].reshape(NDEV, shp[0], shp[1]).transpose(1, 0, 2).reshape(shp[0], NDEV * shp[1])

    small = dict(conv_a=full_small("conv_a", (4, 384)), rel_bias=full_small("rel_bias", (16, 40)),
                 conv_ffn=full_small("conv_ffn", (3, 704)),
                 b_gate=W["b_gate"][None], a_log=W["a_log"][None], dt_bias=W["dt_bias"][None],
                 norm_a=W["norm_a"][None], ln1_g=W["ln1_g"][None], ln1_b=W["ln1_b"][None],
                 b_conv_ffn=W["b_conv_ffn"][None], ln2_g=W["ln2_g"][None], ln2_b=W["ln2_b"][None])

    nsh = w_ada.shape[2]
    b_sh = lax.dynamic_slice(W["b_ada"][None], (0, me * nsh), (1, nsh))
    mod_sh = ada_fwd(c_all, W["w_ada"], b_sh)
    (mod_rows,) = all_to_all([mod_sh[:, None, :]], "scatter_mod")
    mod6 = mod_rows.reshape(6, D)
    mod = tuple(mod6[i:i + 1] for i in range(6))

    loss, grad_x, dmod, g = local_step(x[0], loss_target[0], mod, wts, small)

    rep_vals = {n: g[n] for n in REP_NAMES if n != "b_ada"}
    rep_vals["b_ada"] = jnp.concatenate(dmod, axis=1)
    rep_vals["loss"] = loss.reshape(1, 1)
    (rep_all,) = all_gather([_pack_rep(rep_vals)[None]], "gather_small_grads")
    rep_all = rep_all.reshape(NDEV, 1, REP_LEN)
    zero1 = jnp.zeros((1, 1), F32)
    rep_out = adamw(rep_all, _pack_rep({**{n: W[n][None] for n in REP_NAMES}, "loss": zero1}),
                    _pack_rep({**{n: M[n][None] for n in REP_NAMES}, "loss": zero1}),
                    _pack_rep({**{n: V[n][None] for n in REP_NAMES}, "loss": zero1}), "adamw_small")
    loss_total = _unpack_rep(rep_out[0], "loss")[0, 0]

    o_ada = _REP["b_ada"][0]
    dmod_all = rep_all[:, 0, o_ada:o_ada + 6 * D]
    dmod_sh = lax.dynamic_slice(dmod_all, (0, me * nsh), (NDEV, nsh))
    g_w_ada = ada_wgrad(c_all.T, dmod_sh)

    parts = all_to_all([
        _col_shards(g["w_in"], w_in.shape[2]).astype(BF16), _col_shards(g["w_up"], w_up.shape[2]).astype(BF16),
        g["w_down"].reshape(NDEV, -1, D).astype(BF16), g["w_a"].reshape(NDEV, -1, D).astype(BF16),
        g["w_b"].reshape(NDEV, -1, D).astype(BF16), g["w_o"].reshape(NDEV, -1, D).astype(BF16)], "scatter_grads")
    sh_parts = {"conv_a": _col_shards(g["conv_a"], 384), "rel_bias": _col_shards(g["rel_bias"], 40),
                "conv_ffn": _col_shards(g["conv_ffn"], 704)}
    (sh_recv,) = all_to_all([_pack_sh(sh_parts)[:, None, :]], "scatter_small_grads")

    res = {}
    for n, p in zip(big, parts):
        res[n] = adamw(p, W[n], M[n], V[n], "adamw_" + n)
    res["w_ada"] = adamw(g_w_ada[None], W["w_ada"], M["w_ada"], V["w_ada"], "adamw_w_ada")
    sh_out = adamw(sh_recv, _pack_sh({n: W[n] for n, _ in _SH})[None], _pack_sh({n: M[n] for n, _ in _SH})[None],
                   _pack_sh({n: V[n] for n, _ in _SH})[None], "adamw_small_sharded")
    for n, _ in _SH:
        res[n] = tuple(_unpack_sh(o, n) for o in sh_out)
    for n in REP_NAMES:
        res[n] = tuple(_unpack_rep(o, n)[0] for o in rep_out)

    order = ("w_ada", "b_ada", "w_in", "b_gate", "conv_a", "a_log", "dt_bias", "norm_a", "rel_bias", "w_branch_a",
             "w_branch_b", "w_o", "ln1_g", "ln1_b", "w_up", "conv_ffn", "b_conv_ffn", "w_down", "ln2_g", "ln2_b")
    outs = [loss_total, grad_x[None]]
    for kind in range(4):
        outs += [res[n][kind][None] for n in order]
    return tuple(outs)
```

```python
import functools
import math

import numpy as np
import jax
import jax.numpy as jnp
from jax import lax
from jax.experimental import pallas as pl
from jax.experimental.pallas import tpu as pltpu

F32 = jnp.float32
BF16 = jnp.bfloat16
HI = lax.Precision.HIGHEST

D = 1024
CH = 64
AH, ADK = 8, 128
BH, BDH = 16, 64
BPREV = 8
BMAXREL = 256
RELSZ = CH + BMAXREL
DFF = 2816
ALPHA = 2.0 ** 0.25
LN_EPS, RMS_EPS, L2_EPS = 1e-5, 1e-6, 1e-6
NEG = -1e30
LR, B1, B2, AEPS, WD, STEP = 1e-3, 0.9, 0.999, 1e-8, 0.01, 10
NDEV = 8
HALO = 8
TQ = 512
VMEM_LIMIT = 56 * 1024 * 1024

C_QKVA, C_Z, C_QKVB, C_GATE, C_BA, NCAT = 0, 3072, 4096, 7168, 9216, 9728


def _cparams(n_axes=1, vmem=VMEM_LIMIT):
    return pltpu.CompilerParams(dimension_semantics=("arbitrary",) * n_axes, vmem_limit_bytes=vmem)


def _dg(a, b, ca, cb):
    return lax.dot_general(a.astype(BF16), b.astype(BF16), (((ca,), (cb,)), ((), ())),
                           preferred_element_type=F32)


@jax.custom_vjp
def mm_nn(a, b):
    return _dg(a, b, 1, 0)


@jax.custom_vjp
def mm_nt(a, b):
    return _dg(a, b, 1, 1)


@jax.custom_vjp
def mm_tn(a, b):
    return _dg(a, b, 0, 0)


mm_nn.defvjp(lambda a, b: (mm_nn(a, b), (a, b)),
             lambda r, g: (mm_nt(g, r[1]).astype(r[0].dtype), mm_tn(r[0], g).astype(r[1].dtype)))
mm_nt.defvjp(lambda a, b: (mm_nt(a, b), (a, b)),
             lambda r, g: (mm_nn(g, r[1]).astype(r[0].dtype), mm_tn(g, r[0]).astype(r[1].dtype)))
mm_tn.defvjp(lambda a, b: (mm_tn(a, b), (a, b)),
             lambda r, g: (mm_nt(r[1], g).astype(r[0].dtype), mm_nn(r[0], g).astype(r[1].dtype)))


@jax.custom_vjp
def mm_w(a, w):
    return _dg(a, w, 1, 0)


mm_w.defvjp(lambda a, w: (mm_w(a, w), (a, w)),
            lambda r, g: (mm_nt(g, r[1]).astype(r[0].dtype), jnp.zeros_like(r[1])))


def _mmh(a, b):
    return lax.dot_general(a, b, (((1,), (0,)), ((), ())), precision=HI, preferred_element_type=F32)


def _bdg(a, b, ca, cb):
    return lax.dot_general(a.astype(BF16), b.astype(BF16), (((ca,), (cb,)), ((0,), (0,))),
                           preferred_element_type=F32)


@jax.custom_vjp
def bmm_nn(a, b):
    return _bdg(a, b, 2, 1)


@jax.custom_vjp
def bmm_nt(a, b):
    return _bdg(a, b, 2, 2)


@jax.custom_vjp
def bmm_tn(a, b):
    return _bdg(a, b, 1, 1)


bmm_nn.defvjp(lambda a, b: (bmm_nn(a, b), (a, b)), lambda r, g: (bmm_nt(g, r[1]), bmm_tn(r[0], g)))
bmm_nt.defvjp(lambda a, b: (bmm_nt(a, b), (a, b)), lambda r, g: (bmm_nn(g, r[1]), bmm_tn(g, r[0])))
bmm_tn.defvjp(lambda a, b: (bmm_tn(a, b), (a, b)), lambda r, g: (bmm_nt(r[1], g), bmm_nn(r[0], g)))


def _bdg3(a, b, ca, cb):
    return lax.dot_general(a, b, (((ca,), (cb,)), ((0,), (0,))), precision=HI, preferred_element_type=F32)


@jax.custom_vjp
def bmm3_nn(a, b):
    return _bdg3(a, b, 2, 1)


bmm3_nn.defvjp(lambda a, b: (bmm3_nn(a, b), (a, b)),
               lambda r, g: (_bdg3(g, r[1], 2, 2), _bdg3(r[0], g, 1, 1)))


def _sigmoid(x):
    return 1.0 / (1.0 + jnp.exp(-x))


def _silu(x):
    return x * _sigmoid(x)


def _softplus(x):
    return jnp.maximum(x, 0.0) + jnp.log(1.0 + jnp.exp(-jnp.abs(x)))


def _layernorm(r, g, b):
    mu = jnp.mean(r, axis=-1, keepdims=True)
    xc = r - mu
    var = jnp.mean(xc * xc, axis=-1, keepdims=True)
    return xc * lax.rsqrt(var + LN_EPS) * g + b


def _iota2(shape, dim):
    return lax.broadcasted_iota(jnp.int32, shape, dim)


@jax.custom_vjp
def causal_conv(ext, rows):
    k = len(rows)
    y = None
    for j in range(k):
        s = k - 1 - j
        r = pltpu.roll(ext, s, 0) if s else ext
        t = r[HALO:] * rows[j]
        y = t if y is None else y + t
    return y


def _causal_conv_fwd(ext, rows):
    return causal_conv(ext, rows), (ext, rows)


def _causal_conv_bwd(res, g):
    ext, rows = res
    n = ext.shape[0]
    k = len(rows)
    gext = jnp.concatenate([jnp.zeros((HALO, g.shape[1]), g.dtype), g], axis=0)
    dext = None
    drows = []
    for j in range(k):
        s = k - 1 - j
        up = pltpu.roll(gext, n - s, 0) if s else gext
        t = up * rows[j]
        dext = t if dext is None else dext + t
        r = pltpu.roll(ext, s, 0) if s else ext
        drows.append(jnp.sum(g * r[HALO:], axis=0, keepdims=True))
    return dext, tuple(drows)


causal_conv.defvjp(_causal_conv_fwd, _causal_conv_bwd)


def _chunk_masks(tm):
    i = _iota2((tm, tm), 0)
    j = _iota2((tm, tm), 1)
    same = (i ^ j) < CH
    lower = jnp.where(same & (j <= i), 1.0, 0.0).astype(F32)
    upper = jnp.where(same & (i <= j), 1.0, 0.0).astype(F32)
    return lower, upper


@jax.custom_vjp
def chunk_cumsum(g):
    lower, _ = _chunk_masks(g.shape[0])
    return _mmh(lower, g)


def _chunk_cumsum_bwd(_, ct):
    _, upper = _chunk_masks(ct.shape[0])
    return (_mmh(upper, ct),)


chunk_cumsum.defvjp(lambda g: (chunk_cumsum(g), None), _chunk_cumsum_bwd)


@jax.custom_vjp
def inv_unit_lower(a):
    n = a.shape[-1]
    eye = jnp.where(_iota2((1, n, n), 1) == _iota2((1, n, n), 2), 1.0, 0.0).astype(F32)
    x = eye - a
    p = _bdg3(a, a, 2, 1)
    steps = int(math.log2(n)) - 1
    for s in range(steps):
        x = x + _bdg3(x, p, 2, 1)
        if s + 1 < steps:
            p = _bdg3(p, p, 2, 1)
    return x


def _inv_fwd(a):
    t = inv_unit_lower(a)
    return t, t


def _inv_bwd(t, g):
    return (-_bdg3(_bdg3(t, g, 1, 1), t, 2, 2),)


inv_unit_lower.defvjp(_inv_fwd, _inv_bwd)


def prep_fn(ext, bb, aa, rows, a_log, dtb):
    s = _silu(causal_conv(ext, rows))
    qs, ks, vs = [], [], []
    for h in range(AH):
        qh = s[:, h * ADK:(h + 1) * ADK]
        kh = s[:, D + h * ADK:D + (h + 1) * ADK]
        qs.append(qh * lax.rsqrt(jnp.sum(qh * qh, axis=-1, keepdims=True) + L2_EPS) * (ADK ** -0.5))
        ks.append(kh * lax.rsqrt(jnp.sum(kh * kh, axis=-1, keepdims=True) + L2_EPS))
        vs.append(s[:, 2 * D + h * ADK:2 * D + (h + 1) * ADK])
    beta = _sigmoid(bb)
    g = -jnp.exp(a_log) * _softplus(aa + dtb)
    return tuple(qs), tuple(ks), tuple(vs), chunk_cumsum(g), beta


def c1_heads(q, k, v, gcol, grow, bcol):
    i = _iota2((1, CH, CH), 1)
    j = _iota2((1, CH, CH), 2)
    causal = j <= i
    strict = j < i
    diff = gcol - grow
    decay = jnp.where(causal, jnp.exp(jnp.where(causal, diff, 0.0)), 0.0)
    kb = k * bcol
    vb = v * bcol
    a_low = jnp.where(strict, bmm_nt(kb, k) * decay, 0.0)
    tinv = inv_unit_lower(a_low)
    egc = jnp.exp(gcol)
    u = bmm3_nn(tinv, vb)
    w = bmm3_nn(tinv, kb * egc)
    qk = jnp.where(causal, bmm_nt(q, k) * decay, 0.0)
    glast = jnp.sum(jnp.where(_iota2((1, CH, 1), 1) == CH - 1, gcol, 0.0), axis=1, keepdims=True)
    qg = q * egc
    kd = k * jnp.exp(glast - gcol)
    eg = jnp.exp(glast) * jnp.ones((1, 1, ADK), F32)
    return u, w, qk, qg, kd, eg


def c2_heads(s, u, w, qk, qg, kd, eg, z, nw):
    vn = u - bmm_nn(w, s)
    o = bmm_nn(qg, s) + bmm_nn(qk, vn)
    s2 = s * eg + bmm_tn(kd, vn)
    ms = jnp.mean(o * o, axis=-1, keepdims=True)
    og = o * lax.rsqrt(ms + RMS_EPS) * nw * _silu(z)
    return og, s2


def attn_pair(q, kp, kc, vp, vc, bias2, firstf):
    k = jnp.concatenate([kp, kc], axis=0)
    v = jnp.concatenate([vp, vc], axis=0)
    lane = _iota2((1, 2 * BDH), 1)
    col = _iota2((1, 2 * TQ), 1)
    nokey = jnp.where(col < TQ, firstf, 0.0) * NEG
    out = None
    for hh in range(2):
        hm = jnp.where((lane >= hh * BDH) & (lane < (hh + 1) * BDH), 1.0, 0.0).astype(F32)
        s = mm_nt(q * hm, k) * (BDH ** -0.5) + assemble_bias(bias2[hh]) + nokey
        m = lax.stop_gradient(jnp.max(s, axis=-1, keepdims=True))
        p = jnp.exp(s - m)
        p = p / jnp.sum(p, axis=-1, keepdims=True)
        o = mm_nn(p, v) * hm
        out = o if out is None else out + o
    return out


def merge_fn(x, oa, ob, gra, grb, p_pa, p_pb, p_mix, bga, bgb, gate_t, g1, b1, scale_f, shift_f,
             wa, wb, wo):
    ga = _sigmoid(gra + bga)
    gb = _sigmoid(grb + bgb)
    pa = mm_w(oa, wa) + p_pa
    pb = mm_w(ob, wb) + p_pb
    merged = ga * pa + gb * pb
    mix = mm_w(merged, wo) + p_mix
    y1 = _layernorm(ALPHA * x + gate_t * mix, g1, b1)
    return y1, merged


def ffn_act_fn(ext, rows, bconv):
    u = causal_conv(ext, rows) + bconv
    return _silu(u[:, :DFF]) * u[:, DFF:]


def head_fn(a, y1, p_ffn, gate_f, g2, b2, tgt, wd):
    ffn = mm_w(a, wd) + p_ffn
    y2 = _layernorm(ALPHA * y1 + gate_f * ffn, g2, b2)
    err = y2 - tgt
    return 0.5 * jnp.sum(jnp.mean(err * err, axis=-1, keepdims=True))


def _rows(tm, width, colblk=0, order=None):
    if order is None:
        return pl.BlockSpec((tm, width), lambda i: (i, colblk))
    return pl.BlockSpec((tm, width), lambda i: (order(i), colblk))


def _const(shape):
    nd = len(shape)
    return pl.BlockSpec(shape, lambda *_: (0,) * nd)


def _pick(n, cands):
    for c in cands:
        if n % c == 0:
            return c
    raise ValueError(f"no tile for {n}")


def _onehot_rows(k, j):
    return jnp.where(_iota2((k, 1), 0) == j, 1.0, 0.0).astype(F32)


def matmul(a, w, out_dtype, name):
    m, kdim = a.shape
    _, n = w.shape
    tm = _pick(m, (1024, 512, 256, 128))
    tn = _pick(n, (1024, 512, 256, 128))
    tk = _pick(kdim, (1024, 512, 256, 128))
    nk = kdim // tk

    def body(a_ref, w_ref, o_ref, *scratch):
        p = _dg(a_ref[...], w_ref[...], 1, 0)
        if nk == 1:
            o_ref[...] = p.astype(out_dtype)
            return
        acc = scratch[0]
        k = pl.program_id(2)

        @pl.when(k == 0)
        def _():
            acc[...] = p

        @pl.when(k > 0)
        def _():
            acc[...] += p

        @pl.when(k == nk - 1)
        def _():
            o_ref[...] = acc[...].astype(out_dtype)

    return pl.pallas_call(
        body, name=name,
        grid=(m // tm, n // tn, nk),
        in_specs=[pl.BlockSpec((tm, tk), lambda i, j, k: (i, k)),
                  pl.BlockSpec((tk, tn), lambda i, j, k: (k, j))],
        out_specs=pl.BlockSpec((tm, tn), lambda i, j, k: (i, j)),
        out_shape=jax.ShapeDtypeStruct((m, n), out_dtype),
        scratch_shapes=[] if nk == 1 else [pltpu.VMEM((tm, tn), F32)],
        compiler_params=_cparams(3),
    )(a, w)


def modulate(x, scale, shift, name):
    t, d = x.shape
    tm = _pick(t, (512, 256, 128))

    def body(x_ref, sc_ref, sh_ref, o_ref):
        o_ref[...] = (x_ref[...] * (1.0 + sc_ref[...]) + sh_ref[...]).astype(BF16)

    return pl.pallas_call(
        body, name=name, grid=(t // tm,),
        in_specs=[_rows(tm, d), _const((1, d)), _const((1, d))],
        out_specs=_rows(tm, d),
        out_shape=jax.ShapeDtypeStruct((t, d), BF16),
        compiler_params=_cparams(),
    )(x, scale, shift)


def modulate_bwd(dh, xin, dres, scale, name):
    t, d = dh.shape
    tm = _pick(t, (512, 256, 128))

    def body(dh_ref, x_ref, dr_ref, sc_ref, o_ref, dsc_ref, dsh_ref):
        i = pl.program_id(0)
        dh_v = dh_ref[...]
        o_ref[...] = dr_ref[...] + dh_v * (1.0 + sc_ref[...])

        @pl.when(i == 0)
        def _():
            dsc_ref[...] = jnp.zeros_like(dsc_ref)
            dsh_ref[...] = jnp.zeros_like(dsh_ref)

        dsc_ref[...] += jnp.sum(dh_v * x_ref[...], axis=0, keepdims=True)
        dsh_ref[...] += jnp.sum(dh_v, axis=0, keepdims=True)

    return pl.pallas_call(
        body, name=name, grid=(t // tm,),
        in_specs=[_rows(tm, d), _rows(tm, d), _rows(tm, d), _const((1, d))],
        out_specs=[_rows(tm, d), _const((1, d)), _const((1, d))],
        out_shape=[jax.ShapeDtypeStruct((t, d), F32), jax.ShapeDtypeStruct((1, d), F32),
                   jax.ShapeDtypeStruct((1, d), F32)],
        compiler_params=_cparams(),
    )(dh, xin, dres, scale)


PREP_TM = 128


def _halo_specs(tm, width, colblk, order):
    per = tm // HALO
    return [pl.BlockSpec((HALO, width), lambda i: (jnp.maximum(order(i) * per - 1, 0), colblk)),
            pl.BlockSpec((tm, width), lambda i: (order(i), colblk))]


def prep_fwd(proj, conv_a, a_log, dtb):
    t = proj.shape[0]
    tm = PREP_TM
    nt = t // tm
    wq = 3 * D

    def body(prev_ref, cur_ref, bb_ref, aa_ref, cw_ref, al_ref, dt_ref, q_ref, k_ref, v_ref, g_ref, b_ref):
        i = pl.program_id(0)
        flag = jnp.where(i > 0, 1.0, 0.0)
        ext = jnp.concatenate([prev_ref[...] * flag, cur_ref[...]], axis=0)
        rows = tuple(cw_ref[j:j + 1, :] for j in range(4))
        qs, ks, vs, gcs, beta = prep_fn(ext, bb_ref[...], aa_ref[...], rows, al_ref[...], dt_ref[...])
        for h in range(AH):
            q_ref[h] = qs[h]
            k_ref[h] = ks[h]
            v_ref[h] = vs[h]
        g_ref[...] = gcs
        b_ref[...] = beta

    ident = lambda i: i
    hm = pl.BlockSpec((AH, tm, ADK), lambda i: (0, i, 0))
    return pl.pallas_call(
        body, name="prep_fwd", grid=(nt,),
        in_specs=_halo_specs(tm, wq, 0, ident) + [
            _rows(tm, 128, C_BA // 128), _rows(tm, 128, C_BA // 128 + 1),
            _const((4, wq)), _const((1, 128)), _const((1, 128))],
        out_specs=[hm, hm, hm, _rows(tm, 128), _rows(tm, 128)],
        out_shape=[jax.ShapeDtypeStruct((AH, t, ADK), F32)] * 3 + [jax.ShapeDtypeStruct((t, 128), F32)] * 2,
        compiler_params=_cparams(),
    )(proj, proj, proj, proj, conv_a, a_log, dtb)


def prep_bwd(proj, conv_a, a_log, dtb, dq, dk, dv, dgcs, dbeta):
    t = proj.shape[0]
    tm = PREP_TM
    nt = t // tm
    wq = 3 * D
    rev = lambda i: nt - 1 - i

    def body(prev_ref, cur_ref, bb_ref, aa_ref, cw_ref, al_ref, dt_ref,
             dq_ref, dk_ref, dv_ref, dg_ref, db_ref,
             dpre_ref, dbb_ref, daa_ref, dcw_ref, dal_ref, ddt_ref, carry):
        i = pl.program_id(0)
        flag = jnp.where(i < nt - 1, 1.0, 0.0)
        ext = jnp.concatenate([prev_ref[...] * flag, cur_ref[...]], axis=0)
        rows = tuple(cw_ref[j:j + 1, :] for j in range(4))
        _, vjp = jax.vjp(prep_fn, ext, bb_ref[...], aa_ref[...], rows, al_ref[...], dt_ref[...])
        dext, dbb, daa, drows, dal, ddt = vjp((tuple(dq_ref[h] for h in range(AH)),
                                               tuple(dk_ref[h] for h in range(AH)),
                                               tuple(dv_ref[h] for h in range(AH)), dg_ref[...], db_ref[...]))

        @pl.when(i == 0)
        def _():
            carry[...] = jnp.zeros_like(carry)
            dcw_ref[...] = jnp.zeros_like(dcw_ref)
            dal_ref[...] = jnp.zeros_like(dal_ref)
            ddt_ref[...] = jnp.zeros_like(ddt_ref)

        dcur = dext[HALO:]
        dpre_ref[...] = jnp.concatenate([dcur[:tm - HALO], dcur[tm - HALO:] + carry[...]], axis=0).astype(BF16)
        carry[...] = dext[:HALO]
        dbb_ref[...] = dbb.astype(BF16)
        daa_ref[...] = daa.astype(BF16)
        dcw = None
        for j in range(4):
            tj = _onehot_rows(4, j) * drows[j]
            dcw = tj if dcw is None else dcw + tj
        dcw_ref[...] += dcw
        dal_ref[...] += dal
        ddt_ref[...] += ddt

    hm = pl.BlockSpec((AH, tm, ADK), lambda i: (0, rev(i), 0))
    return pl.pallas_call(
        body, name="prep_bwd", grid=(nt,),
        in_specs=_halo_specs(tm, wq, 0, rev) + [
            _rows(tm, 128, C_BA // 128, rev), _rows(tm, 128, C_BA // 128 + 1, rev),
            _const((4, wq)), _const((1, 128)), _const((1, 128)),
            hm, hm, hm, _rows(tm, 128, 0, rev), _rows(tm, 128, 0, rev)],
        out_specs=[_rows(tm, wq, 0, rev), _rows(tm, 128, 0, rev), _rows(tm, 128, 0, rev),
                   _const((4, wq)), _const((1, 128)), _const((1, 128))],
        out_shape=[jax.ShapeDtypeStruct((t, wq), BF16), jax.ShapeDtypeStruct((t, 128), BF16),
                   jax.ShapeDtypeStruct((t, 128), BF16), jax.ShapeDtypeStruct((4, wq), F32),
                   jax.ShapeDtypeStruct((1, 128), F32), jax.ShapeDtypeStruct((1, 128), F32)],
        scratch_shapes=[pltpu.VMEM((HALO, wq), F32)],
        compiler_params=_cparams(),
    )(proj, proj, proj, proj, conv_a, a_log, dtb, dq, dk, dv, dgcs, dbeta)


def _c1_specs(order):
    hm = pl.BlockSpec((AH, CH, ADK), lambda n: (0, order(n), 0))
    col = pl.BlockSpec((AH, 1, CH, 1), lambda n: (0, order(n), 0, 0))
    row = pl.BlockSpec((AH, 1, 1, CH), lambda n: (0, order(n), 0, 0))
    qk = pl.BlockSpec((1, AH, CH, CH), lambda n: (order(n), 0, 0, 0))
    eg = pl.BlockSpec((1, AH, 1, ADK), lambda n: (order(n), 0, 0, 0))
    return hm, col, row, qk, eg


def _heads(ref):
    return jnp.stack([ref[:, h * ADK:(h + 1) * ADK] for h in range(AH)], axis=0)


def c1_fwd(q, k, v, gcol, grow, bcol):
    t = q.shape[1]
    nc = t // CH
    hm, col, row, qks, egs = _c1_specs(lambda n: n)

    def body(q_ref, k_ref, v_ref, gc_ref, gr_ref, bc_ref, u_ref, w_ref, qg_ref, kd_ref, qk_ref, eg_ref):
        u, w, qk, qg, kd, eg = c1_heads(q_ref[...], k_ref[...], v_ref[...],
                                        gc_ref[:, 0], gr_ref[:, 0], bc_ref[:, 0])
        u_ref[...] = u
        w_ref[...] = w
        qg_ref[...] = qg
        kd_ref[...] = kd
        qk_ref[0] = qk
        eg_ref[0] = eg

    return pl.pallas_call(
        body, name="c1_fwd", grid=(nc,),
        in_specs=[hm, hm, hm, col, row, col],
        out_specs=[hm, hm, hm, hm, qks, egs],
        out_shape=[jax.ShapeDtypeStruct((AH, t, ADK), F32)] * 4 + [
            jax.ShapeDtypeStruct((nc, AH, CH, CH), F32), jax.ShapeDtypeStruct((nc, AH, 1, ADK), F32)],
        compiler_params=_cparams(),
    )(q, k, v, gcol, grow, bcol)


def c1_bwd(q, k, v, gcol, grow, bcol, du, dw, dqg, dkd, dqk, deg):
    t = q.shape[1]
    nc = t // CH
    hm, col, row, qks, egs = _c1_specs(lambda n: n)

    def body(q_ref, k_ref, v_ref, gc_ref, gr_ref, bc_ref, du_ref, dw_ref, dqg_ref, dkd_ref, dqk_ref, deg_ref,
             dq_ref, dk_ref, dv_ref, dgc_ref, dgr_ref, dbc_ref):
        _, vjp = jax.vjp(c1_heads, q_ref[...], k_ref[...], v_ref[...], gc_ref[:, 0], gr_ref[:, 0], bc_ref[:, 0])
        dq, dk, dv, dgc, dgr, dbc = vjp((du_ref[...], dw_ref[...], dqk_ref[0], dqg_ref[...], dkd_ref[...],
                                         deg_ref[0]))
        dq_ref[...] = dq
        dk_ref[...] = dk
        dv_ref[...] = dv
        dgc_ref[:, 0] = dgc
        dgr_ref[:, 0] = dgr
        dbc_ref[:, 0] = dbc

    return pl.pallas_call(
        body, name="c1_bwd", grid=(nc,),
        in_specs=[hm, hm, hm, col, row, col, hm, hm, hm, hm, qks, egs],
        out_specs=[hm, hm, hm, col, row, col],
        out_shape=[jax.ShapeDtypeStruct((AH, t, ADK), F32)] * 3 + [
            jax.ShapeDtypeStruct((AH, nc, CH, 1), F32), jax.ShapeDtypeStruct((AH, nc, 1, CH), F32),
            jax.ShapeDtypeStruct((AH, nc, CH, 1), F32)],
        compiler_params=_cparams(),
    )(q, k, v, gcol, grow, bcol, du, dw, dqg, dkd, dqk, deg)


def c2_fwd(u, w, qg, kd, qk, eg, proj, norm_a):
    t = u.shape[1]
    nc = t // CH
    hm, _, _, qks, egs = _c1_specs(lambda n: n)
    tok = pl.BlockSpec((CH, D), lambda n: (n, 0))
    zspec = pl.BlockSpec((CH, D), lambda n: (n, C_Z // D))
    sspec = pl.BlockSpec((1, AH, ADK, ADK), lambda n: (n, 0, 0, 0))

    def body(u_ref, w_ref, qg_ref, kd_ref, qk_ref, eg_ref, z_ref, nw_ref, o_ref, sall_ref, st):
        n = pl.program_id(0)

        @pl.when(n == 0)
        def _():
            st[...] = jnp.zeros_like(st)

        s = st[...]
        sall_ref[0] = s
        og, s2 = c2_heads(s, u_ref[...], w_ref[...], qk_ref[0], qg_ref[...], kd_ref[...], eg_ref[0],
                          _heads(z_ref), nw_ref[...])
        st[...] = s2
        for h in range(AH):
            o_ref[:, h * ADK:(h + 1) * ADK] = og[h].astype(BF16)

    return pl.pallas_call(
        body, name="c2_fwd", grid=(nc,),
        in_specs=[hm, hm, hm, hm, qks, egs, zspec, _const((1, ADK))],
        out_specs=[tok, sspec],
        out_shape=[jax.ShapeDtypeStruct((t, D), BF16), jax.ShapeDtypeStruct((nc, AH, ADK, ADK), F32)],
        scratch_shapes=[pltpu.VMEM((AH, ADK, ADK), F32)],
        compiler_params=_cparams(),
    )(u, w, qg, kd, qk, eg, proj, norm_a)


def c2_bwd(u, w, qg, kd, qk, eg, proj, norm_a, sall, do):
    t = u.shape[1]
    nc = t // CH
    rev = lambda n: nc - 1 - n
    hm, _, _, qks, egs = _c1_specs(rev)
    tok = pl.BlockSpec((CH, D), lambda n: (rev(n), 0))
    zspec = pl.BlockSpec((CH, D), lambda n: (rev(n), C_Z // D))
    sspec = pl.BlockSpec((1, AH, ADK, ADK), lambda n: (rev(n), 0, 0, 0))

    def body(u_ref, w_ref, qg_ref, kd_ref, qk_ref, eg_ref, z_ref, nw_ref, sall_ref, do_ref,
             du_ref, dw_ref, dqg_ref, dkd_ref, dqk_ref, deg_ref, dz_ref, dnw_ref, dst):
        n = pl.program_id(0)

        @pl.when(n == 0)
        def _():
            dst[...] = jnp.zeros_like(dst)
            dnw_ref[...] = jnp.zeros_like(dnw_ref)

        _, vjp = jax.vjp(c2_heads, sall_ref[0], u_ref[...], w_ref[...], qk_ref[0], qg_ref[...], kd_ref[...],
                         eg_ref[0], _heads(z_ref), nw_ref[...])
        ds, du, dw, dqk, dqg, dkd, deg, dz, dn = vjp((_heads(do_ref), dst[...]))
        dst[...] = ds
        du_ref[...] = du
        dw_ref[...] = dw
        dqg_ref[...] = dqg
        dkd_ref[...] = dkd
        dqk_ref[0] = dqk
        deg_ref[0] = deg
        for h in range(AH):
            dz_ref[:, h * ADK:(h + 1) * ADK] = dz[h].astype(BF16)
        dnw_ref[...] += dn

    return pl.pallas_call(
        body, name="c2_bwd", grid=(nc,),
        in_specs=[hm, hm, hm, hm, qks, egs, zspec, _const((1, ADK)), sspec, tok],
        out_specs=[hm, hm, hm, hm, qks, egs, tok, _const((1, ADK))],
        out_shape=[jax.ShapeDtypeStruct((AH, t, ADK), F32)] * 4 + [
            jax.ShapeDtypeStruct((nc, AH, CH, CH), F32), jax.ShapeDtypeStruct((nc, AH, 1, ADK), F32),
            jax.ShapeDtypeStruct((t, D), BF16), jax.ShapeDtypeStruct((1, ADK), F32)],
        scratch_shapes=[pltpu.VMEM((AH, ADK, ADK), F32)],
        compiler_params=_cparams(),
    )(u, w, qg, kd, qk, eg, proj, norm_a, sall, do)


NQB = TQ // CH
NKB = 2 * TQ // CH
NDIST = BPREV + 1
KLO = -(NQB - 2)
NPAIR = NKB - 1 - KLO + 1


def bias_table(rel_bias):
    nh = rel_bias.shape[0]
    relx = jnp.concatenate([rel_bias, jnp.broadcast_to(rel_bias[:, -1:], (nh, CH * BPREV + 2 * CH - 1 - RELSZ))],
                           axis=1)
    t = jnp.stack([relx[:, CH * k:CH * k + 2 * CH - 1] for k in range(NDIST)], axis=1)
    trev = t[:, :, ::-1]
    g2 = jnp.concatenate([trev[:, :, CH - 1:], jnp.zeros((nh, NDIST, 1), F32), trev[:, :, :CH - 1]], axis=2)
    flat = jnp.tile(g2, (1, 1, CH + 1))[:, :, :CH * (2 * CH - 1)]
    blk = flat.reshape(nh, NDIST, CH, 2 * CH - 1)[..., :CH]
    neg = jnp.full((nh, NQB - 1, CH, CH), NEG, F32)
    asc = jnp.concatenate([neg, blk, neg], axis=1)
    return jnp.concatenate([asc[:, 1:], asc[:, :-1]], axis=-1)


def assemble_bias(tab):
    rows = [jnp.concatenate([tab[NQB + a - 2 * b - KLO] for b in range(NKB // 2)], axis=1) for a in range(NQB)]
    return jnp.concatenate(rows, axis=0)


def bias_table_bwd_layout(dtab):
    nh = dtab.shape[0]
    dasc = (jnp.pad(dtab[..., :CH], ((0, 0), (1, 0), (0, 0), (0, 0)))
            + jnp.pad(dtab[..., CH:], ((0, 0), (0, 1), (0, 0), (0, 0))))
    dblk = dasc[:, NQB - 1:NQB - 1 + NDIST]
    dr = jnp.pad(dblk, ((0, 0), (0, 0), (0, 0), (0, CH - 1)))
    flat = jnp.pad(dr.reshape(nh, NDIST, CH * (2 * CH - 1)), ((0, 0), (0, 0), (0, 3 * CH)))
    return flat.reshape(nh, NDIST, CH + 1, 2 * CH).transpose(0, 2, 1, 3).reshape(nh, CH + 1, NDIST * 2 * CH)


def _fold_matrix_np():
    f = np.zeros((NDIST * 2 * CH, 384), np.float32)
    for k in range(NDIST):
        s = k
        for xx in range(2 * CH):
            if xx == CH:
                continue
            m = CH - 1 - xx if xx < CH else 3 * CH - 1 - xx
            f[s * 2 * CH + xx, min(CH * k + m, RELSZ - 1)] = 1.0
    return f


def relbias_reduce(dlay):
    nh, rows, cols = dlay.shape
    rpad = (-rows) % 8
    dlay = jnp.pad(dlay, ((0, 0), (0, rpad), (0, 0)))
    fold = jnp.asarray(_fold_matrix_np())

    def body(d_ref, f_ref, o_ref):
        cs = jnp.sum(d_ref[0], axis=0, keepdims=True)
        o_ref[0] = _mmh(jnp.broadcast_to(cs, (8, cols)), f_ref[...])

    out = pl.pallas_call(
        body, name="relbias_reduce", grid=(nh,),
        in_specs=[pl.BlockSpec((1, rows + rpad, cols), lambda h: (h, 0, 0)), _const((cols, 384))],
        out_specs=pl.BlockSpec((1, 8, 384), lambda h: (h, 0, 0)),
        out_shape=jax.ShapeDtypeStruct((nh, 8, 384), F32),
        compiler_params=_cparams(),
    )(dlay, fold)
    return out[:, 0, :RELSZ]


def attn_fwd(proj, bias):
    t = proj.shape[0]
    nt = t // TQ
    cb = C_QKVB // 128

    def body(q_ref, kp_ref, kc_ref, vp_ref, vc_ref, b_ref, o_ref):
        i = pl.program_id(1)
        firstf = jnp.where(i == 0, 1.0, 0.0)
        o_ref[...] = attn_pair(q_ref[...], kp_ref[...], kc_ref[...], vp_ref[...], vc_ref[...],
                               b_ref[...], firstf).astype(BF16)

    def blk(off, prev):
        if prev:
            return pl.BlockSpec((TQ, 128), lambda p, i: (jnp.maximum(i - 1, 0), cb + off + p))
        return pl.BlockSpec((TQ, 128), lambda p, i: (i, cb + off + p))

    return pl.pallas_call(
        body, name="attn_fwd", grid=(BH // 2, nt),
        in_specs=[blk(0, False), blk(8, True), blk(8, False), blk(16, True), blk(16, False),
                  pl.BlockSpec((2, NPAIR, CH, 2 * CH), lambda p, i: (p, 0, 0, 0))],
        out_specs=pl.BlockSpec((TQ, 128), lambda p, i: (i, p)),
        out_shape=jax.ShapeDtypeStruct((t, D), BF16),
        compiler_params=_cparams(2),
    )(proj, proj, proj, proj, proj, bias)


def attn_bwd(proj, bias, do):
    t = proj.shape[0]
    nt = t // TQ
    cb = C_QKVB // 128

    def body(q_ref, kp_ref, kc_ref, vp_ref, vc_ref, b_ref, do_ref,
             dq_ref, dk_ref, dv_ref, db_ref, ck, cv):
        i = pl.program_id(1)

        @pl.when(i == 0)
        def _():
            ck[...] = jnp.zeros_like(ck)
            cv[...] = jnp.zeros_like(cv)
            db_ref[...] = jnp.zeros_like(db_ref)

        @pl.when(i < nt)
        def _():
            firstf = jnp.where(i == 0, 1.0, 0.0)
            _, vjp = jax.vjp(lambda q, kp, kc, vp, vc, b: attn_pair(q, kp, kc, vp, vc, b, firstf),
                             q_ref[...].astype(F32), kp_ref[...].astype(F32), kc_ref[...].astype(F32),
                             vp_ref[...].astype(F32), vc_ref[...].astype(F32), b_ref[...])
            dq, dkp, dkc, dvp, dvc, db = vjp(do_ref[...])
            dq_ref[...] = dq.astype(BF16)
            dk_ref[...] = (ck[...] + dkp).astype(BF16)
            dv_ref[...] = (cv[...] + dvp).astype(BF16)
            ck[...] = dkc
            cv[...] = dvc
            db_ref[...] += db

        @pl.when(i == nt)
        def _():
            dk_ref[...] = ck[...].astype(BF16)
            dv_ref[...] = cv[...].astype(BF16)

    def blk(off, prev):
        if prev:
            return pl.BlockSpec((TQ, 128), lambda p, i: (jnp.clip(i - 1, 0, nt - 1), cb + off + p))
        return pl.BlockSpec((TQ, 128), lambda p, i: (jnp.minimum(i, nt - 1), cb + off + p))

    own = pl.BlockSpec((TQ, 128), lambda p, i: (jnp.minimum(i, nt - 1), p))
    lag = pl.BlockSpec((TQ, 128), lambda p, i: (jnp.maximum(i - 1, 0), p))
    return pl.pallas_call(
        body, name="attn_bwd", grid=(BH // 2, nt + 1),
        in_specs=[blk(0, False), blk(8, True), blk(8, False), blk(16, True), blk(16, False),
                  pl.BlockSpec((2, NPAIR, CH, 2 * CH), lambda p, i: (p, 0, 0, 0)), own],
        out_specs=[own, lag, lag, pl.BlockSpec((2, NPAIR, CH, 2 * CH), lambda p, i: (p, 0, 0, 0))],
        out_shape=[jax.ShapeDtypeStruct((t, D), BF16)] * 3 + [jax.ShapeDtypeStruct((BH, NPAIR, CH, 2 * CH), F32)],
        scratch_shapes=[pltpu.VMEM((TQ, 128), F32), pltpu.VMEM((TQ, 128), F32)],
        compiler_params=_cparams(2),
    )(proj, proj, proj, proj, proj, bias, do)


MERGE_TM = 256


def merge_fwd(x, oa, ob, proj, vecs, wa, wb, wo):
    t = x.shape[0]
    tm = MERGE_TM
    names = ("bga", "bgb", "gate_t", "g1", "b1", "scale_f", "shift_f")

    def body(x_ref, oa_ref, ob_ref, gra_ref, grb_ref, *rest):
        vrefs = rest[:7]
        wa_ref, wb_ref, wo_ref, y_ref, h_ref = rest[7:]
        vv = [r[...] for r in vrefs]
        zero = jnp.zeros((tm, D), F32)
        y1, _ = merge_fn(x_ref[...], oa_ref[...], ob_ref[...], gra_ref[...], grb_ref[...], zero, zero, zero,
                         *vv, wa_ref[...], wb_ref[...], wo_ref[...])
        y_ref[...] = y1
        h_ref[...] = (y1 * (1.0 + vv[5]) + vv[6]).astype(BF16)

    return pl.pallas_call(
        body, name="merge_fwd", grid=(t // tm,),
        in_specs=[_rows(tm, D), _rows(tm, D), _rows(tm, D), _rows(tm, D, C_GATE // D), _rows(tm, D, C_GATE // D + 1)]
        + [_const((1, D))] * 7 + [_const((D, D))] * 3,
        out_specs=[_rows(tm, D), _rows(tm, D)],
        out_shape=[jax.ShapeDtypeStruct((t, D), F32), jax.ShapeDtypeStruct((t, D), BF16)],
        compiler_params=_cparams(),
    )(x, oa, ob, proj, proj, *[vecs[n] for n in names], wa, wb, wo)


def merge_bwd(x, oa, ob, proj, vecs, wa, wb, wo, dy1):
    t = x.shape[0]
    tm = MERGE_TM
    names = ("bga", "bgb", "gate_t", "g1", "b1", "scale_f", "shift_f")

    def body(x_ref, oa_ref, ob_ref, gra_ref, grb_ref, *rest):
        vrefs = rest[:7]
        wa_ref, wb_ref, wo_ref, dy_ref = rest[7:11]
        (dx_ref, doa_ref, dob_ref, dga_ref, dgb_ref, mg_ref, dmix_ref, dpa_ref, dpb_ref,
         dbga_ref, dbgb_ref, dgt_ref, dg1_ref, db1_ref) = rest[11:]
        i = pl.program_id(0)
        vv = [r[...] for r in vrefs]
        zero = jnp.zeros((tm, D), F32)

        def f(x_, oa_, ob_, gra_, grb_, ppa, ppb, pmix, bga, bgb, gate_t, g1, b1):
            return merge_fn(x_, oa_, ob_, gra_, grb_, ppa, ppb, pmix, bga, bgb, gate_t, g1, b1, vv[5], vv[6],
                            wa_ref[...], wb_ref[...], wo_ref[...])

        _, vjp, merged = jax.vjp(f, x_ref[...], oa_ref[...].astype(F32), ob_ref[...].astype(F32),
                                 gra_ref[...], grb_ref[...], zero, zero, zero, *vv[:5], has_aux=True)
        dx, doa, dob, dga, dgb, dpa, dpb, dmix, dbga, dbgb, dgt, dg1, db1 = vjp(dy_ref[...])
        dx_ref[...] = dx
        doa_ref[...] = doa
        dob_ref[...] = dob
        dga_ref[...] = dga.astype(BF16)
        dgb_ref[...] = dgb.astype(BF16)
        mg_ref[...] = merged.astype(BF16)
        dmix_ref[...] = dmix.astype(BF16)
        dpa_ref[...] = dpa.astype(BF16)
        dpb_ref[...] = dpb.astype(BF16)
        accs = (dbga_ref, dbgb_ref, dgt_ref, dg1_ref, db1_ref)

        @pl.when(i == 0)
        def _():
            for a in accs:
                a[...] = jnp.zeros_like(a)

        for a, val in zip(accs, (dbga, dbgb, dgt, dg1, db1)):
            a[...] += val

    return pl.pallas_call(
        body, name="merge_bwd", grid=(t // tm,),
        in_specs=[_rows(tm, D), _rows(tm, D), _rows(tm, D), _rows(tm, D, C_GATE // D), _rows(tm, D, C_GATE // D + 1)]
        + [_const((1, D))] * 7 + [_const((D, D))] * 3 + [_rows(tm, D)],
        out_specs=[_rows(tm, D)] * 9 + [_const((1, D))] * 5,
        out_shape=[jax.ShapeDtypeStruct((t, D), F32)] * 3 + [jax.ShapeDtypeStruct((t, D), BF16)] * 6
        + [jax.ShapeDtypeStruct((1, D), F32)] * 5,
        compiler_params=_cparams(),
    )(x, oa, ob, proj, proj, *[vecs[n] for n in names], wa, wb, wo, dy1)


FFN_TM = 128


def ffn_act_fwd(up, conv_w, bconv):
    t, wdt = up.shape
    tm = FFN_TM

    def body(prev_ref, cur_ref, cw_ref, bc_ref, a_ref):
        i = pl.program_id(0)
        flag = jnp.where(i > 0, 1.0, 0.0)
        ext = jnp.concatenate([prev_ref[...] * flag, cur_ref[...]], axis=0)
        rows = tuple(cw_ref[j:j + 1, :] for j in range(3))
        a_ref[...] = ffn_act_fn(ext, rows, bc_ref[...]).astype(BF16)

    return pl.pallas_call(
        body, name="ffn_act_fwd", grid=(t // tm,),
        in_specs=_halo_specs(tm, wdt, 0, lambda i: i) + [_const((3, wdt)), _const((1, wdt))],
        out_specs=_rows(tm, DFF),
        out_shape=jax.ShapeDtypeStruct((t, DFF), BF16),
        compiler_params=_cparams(),
    )(up, up, conv_w, bconv)


def ffn_act_bwd(up, conv_w, bconv, da):
    t, wdt = up.shape
    tm = FFN_TM
    nt = t // tm
    rev = lambda i: nt - 1 - i

    def body(prev_ref, cur_ref, cw_ref, bc_ref, da_ref, dup_ref, dcw_ref, dbc_ref, carry):
        i = pl.program_id(0)
        flag = jnp.where(i < nt - 1, 1.0, 0.0)
        ext = jnp.concatenate([prev_ref[...] * flag, cur_ref[...]], axis=0)
        rows = tuple(cw_ref[j:j + 1, :] for j in range(3))
        _, vjp = jax.vjp(ffn_act_fn, ext, rows, bc_ref[...])
        dext, drows, dbc = vjp(da_ref[...])

        @pl.when(i == 0)
        def _():
            carry[...] = jnp.zeros_like(carry)
            dcw_ref[...] = jnp.zeros_like(dcw_ref)
            dbc_ref[...] = jnp.zeros_like(dbc_ref)

        dcur = dext[HALO:]
        dup_ref[...] = jnp.concatenate([dcur[:tm - HALO], dcur[tm - HALO:] + carry[...]], axis=0).astype(BF16)
        carry[...] = dext[:HALO]
        dcw = None
        for j in range(3):
            tj = _onehot_rows(3, j) * drows[j]
            dcw = tj if dcw is None else dcw + tj
        dcw_ref[...] += dcw
        dbc_ref[...] += dbc

    return pl.pallas_call(
        body, name="ffn_act_bwd", grid=(nt,),
        in_specs=_halo_specs(tm, wdt, 0, rev) + [_const((3, wdt)), _const((1, wdt)), _rows(tm, DFF, 0, rev)],
        out_specs=[_rows(tm, wdt, 0, rev), _const((3, wdt)), _const((1, wdt))],
        out_shape=[jax.ShapeDtypeStruct((t, wdt), BF16), jax.ShapeDtypeStruct((3, wdt), F32),
                   jax.ShapeDtypeStruct((1, wdt), F32)],
        scratch_shapes=[pltpu.VMEM((HALO, wdt), F32)],
        compiler_params=_cparams(),
    )(up, up, conv_w, bconv, da)


HEAD_TM = 256


def head_fwd_bwd(a, y1, tgt, gate_f, g2, b2, wd):
    t = a.shape[0]
    tm = HEAD_TM

    def body(a_ref, y_ref, t_ref, gf_ref, g2_ref, b2_ref, wd_ref,
             da_ref, dy_ref, dffn_ref, dgf_ref, dg2_ref, db2_ref, loss_ref):
        i = pl.program_id(0)
        zero = jnp.zeros((tm, D), F32)

        def f(a_, y_, pf, gf, g2_, b2_):
            return head_fn(a_, y_, pf, gf, g2_, b2_, t_ref[...], wd_ref[...])

        loss, vjp = jax.vjp(f, a_ref[...].astype(F32), y_ref[...], zero, gf_ref[...], g2_ref[...], b2_ref[...])
        da, dy, dffn, dgf, dg2, db2 = vjp(jnp.ones((), F32))
        da_ref[...] = da
        dy_ref[...] = dy
        dffn_ref[...] = dffn.astype(BF16)
        accs = (dgf_ref, dg2_ref, db2_ref, loss_ref)

        @pl.when(i == 0)
        def _():
            for r in accs:
                r[...] = jnp.zeros_like(r)

        dgf_ref[...] += dgf
        dg2_ref[...] += dg2
        db2_ref[...] += db2
        loss_ref[...] += loss * jnp.ones((1, 128), F32)

    return pl.pallas_call(
        body, name="head_fwd_bwd", grid=(t // tm,),
        in_specs=[_rows(tm, DFF), _rows(tm, D), _rows(tm, D), _const((1, D)), _const((1, D)), _const((1, D)),
                  _const((DFF, D))],
        out_specs=[_rows(tm, DFF), _rows(tm, D), _rows(tm, D), _const((1, D)), _const((1, D)), _const((1, D)),
                   _const((1, 128))],
        out_shape=[jax.ShapeDtypeStruct((t, DFF), F32), jax.ShapeDtypeStruct((t, D), F32),
                   jax.ShapeDtypeStruct((t, D), BF16)] + [jax.ShapeDtypeStruct((1, D), F32)] * 3
        + [jax.ShapeDtypeStruct((1, 128), F32)],
        compiler_params=_cparams(),
    )(a, y1, tgt, gate_f, g2, b2, wd)


def ada_fwd(c_all, w_sh, b_sh):
    def body(c_ref, w_ref, b_ref, o_ref):
        o_ref[...] = _mmh(_silu(c_ref[...]), w_ref[...]) + b_ref[...]

    n = w_sh.shape[1]
    return pl.pallas_call(
        body, name="ada_fwd", out_shape=jax.ShapeDtypeStruct((NDEV, n), F32),
        in_specs=[pl.BlockSpec(memory_space=pltpu.VMEM)] * 3,
        out_specs=pl.BlockSpec(memory_space=pltpu.VMEM),
        compiler_params=pltpu.CompilerParams(vmem_limit_bytes=VMEM_LIMIT),
    )(c_all, w_sh, b_sh)


def ada_wgrad(c_all_t, dmod_sh):
    def body(c_ref, d_ref, o_ref):
        o_ref[...] = _mmh(_silu(c_ref[...]), d_ref[...])

    return pl.pallas_call(
        body, name="ada_wgrad", out_shape=jax.ShapeDtypeStruct((c_all_t.shape[0], dmod_sh.shape[1]), F32),
        in_specs=[pl.BlockSpec(memory_space=pltpu.VMEM)] * 2,
        out_specs=pl.BlockSpec(memory_space=pltpu.VMEM),
        compiler_params=pltpu.CompilerParams(vmem_limit_bytes=VMEM_LIMIT),
    )(c_all_t, dmod_sh)


def adamw(gparts, w, m, v, name):
    p, r, c = gparts.shape
    tr = r if r <= 256 else _pick(r, (256, 128, 64, 32, 16, 8))
    c1 = 1.0 - B1 ** STEP
    c2 = 1.0 - B2 ** STEP

    def body(g_ref, w_ref, m_ref, v_ref, go_ref, d_ref, mo_ref, vo_ref):
        g = g_ref[0].astype(F32)
        for s in range(1, p):
            g = g + g_ref[s].astype(F32)
        mn = B1 * m_ref[...] + (1.0 - B1) * g
        vn = B2 * v_ref[...] + (1.0 - B2) * (g * g)
        go_ref[...] = g
        d_ref[...] = -LR * ((mn / c1) / (jnp.sqrt(vn / c2) + AEPS) + WD * w_ref[...])
        mo_ref[...] = mn
        vo_ref[...] = vn

    spec = pl.BlockSpec((tr, c), lambda i: (i, 0))
    return pl.pallas_call(
        body, name=name, grid=(r // tr,),
        in_specs=[pl.BlockSpec((p, tr, c), lambda i: (0, i, 0)), spec, spec, spec],
        out_specs=[spec] * 4,
        out_shape=[jax.ShapeDtypeStruct((r, c), F32)] * 4,
        compiler_params=_cparams(),
    )(gparts, w, m, v)


def _me():
    x, y, c = lax.axis_index("x"), lax.axis_index("y"), lax.axis_index("c")
    return x, y, c, 4 * x + 2 * y + c


def _peer(x, y, c, d):
    px = 1 - x if (d >> 2) & 1 else x
    py = 1 - y if (d >> 1) & 1 else y
    pc = 1 - c if d & 1 else c
    return (px, py, pc), 4 * px + 2 * py + pc


def _exchange(arrs, name, scatter):
    n = len(arrs)

    def body(*refs):
        ins, outs = refs[:n], refs[n:2 * n]
        send, recv, lsem = refs[2 * n:]
        x, y, c, me = _me()
        remote, local = [], []
        for k in range(n):
            src = ins[k].at[me] if scatter else ins[k]
            cp = pltpu.make_async_copy(src, outs[k].at[me], lsem.at[k])
            cp.start()
            local.append(cp)
            for d in range(1, NDEV):
                dev, pid = _peer(x, y, c, d)
                src = ins[k].at[pid] if scatter else ins[k]
                cp = pltpu.make_async_remote_copy(src_ref=src, dst_ref=outs[k].at[me],
                                                  send_sem=send.at[k, d - 1], recv_sem=recv.at[k, d - 1],
                                                  device_id=dev, device_id_type=pl.DeviceIdType.MESH)
                cp.start()
                remote.append(cp)
        for cp in remote:
            cp.wait()
        for cp in local:
            cp.wait()

    shapes = [a.shape if scatter else (NDEV,) + a.shape for a in arrs]
    return pl.pallas_call(
        body, name=name,
        in_specs=[pl.BlockSpec(memory_space=pl.ANY)] * n,
        out_specs=[pl.BlockSpec(memory_space=pl.ANY)] * n,
        out_shape=[jax.ShapeDtypeStruct(s, a.dtype) for s, a in zip(shapes, arrs)],
        scratch_shapes=[pltpu.SemaphoreType.DMA((n, NDEV - 1)), pltpu.SemaphoreType.DMA((n, NDEV - 1)),
                        pltpu.SemaphoreType.DMA((n,))],
        compiler_params=pltpu.CompilerParams(has_side_effects=True),
    )(*arrs)


def all_gather(arrs, name):
    return _exchange(arrs, name, False)


def all_to_all(arrs, name):
    return _exchange(arrs, name, True)


_HBM = pl.BlockSpec(memory_space=pltpu.HBM)
_SEM = pl.BlockSpec(memory_space=pltpu.SEMAPHORE)
_EFFECT = pltpu.SideEffectType.DATAFLOW_SIDE_EFFECTING
NPEER = NDEV - 1


def exchange_start(arrs, name, scatter):
    n = len(arrs)
    lands = [lax.empty(a.shape if scatter else (NDEV,) + a.shape, a.dtype) for a in arrs]

    def body(*refs):
        ins, lrefs = refs[:n], refs[n:2 * n]
        send, recv, token = refs[2 * n], refs[2 * n + 1], refs[-1]
        x, y, c, me = _me()
        for k in range(n):
            for d in range(1, NDEV):
                dev, pid = _peer(x, y, c, d)
                src = ins[k].at[pid] if scatter else ins[k]
                pltpu.make_async_remote_copy(src_ref=src, dst_ref=lrefs[k].at[me],
                                             send_sem=send.at[k * NPEER + d - 1], recv_sem=recv.at[k * NPEER + d - 1],
                                             device_id=dev, device_id_type=pl.DeviceIdType.MESH).start()
        token[...] = jnp.zeros_like(token)

    thru = [pltpu.HBM(a.shape, a.dtype) for a in list(arrs) + lands]
    outs = pl.pallas_call(
        body, name=name,
        out_shape=(pltpu.SemaphoreType.DMA((n * NPEER,)), pltpu.SemaphoreType.DMA((n * NPEER,)), *thru,
                   jax.ShapeDtypeStruct((8, 128), F32)),
        in_specs=[_HBM] * (2 * n),
        out_specs=(_SEM, _SEM, *([_HBM] * (2 * n)), pl.BlockSpec(memory_space=pltpu.VMEM)),
        input_output_aliases={i: 2 + i for i in range(2 * n)},
        compiler_params=pltpu.CompilerParams(has_side_effects=_EFFECT),
    )(*[pltpu.with_memory_space_constraint(a, pltpu.HBM) for a in list(arrs) + lands])
    handle = dict(send=outs[0], recv=outs[1], src=list(outs[2:2 + n]), land=list(outs[2 + n:2 + 2 * n]),
                  scatter=scatter)
    return handle, outs[-1][0, 0]


def exchange_wait(handle, after, name):
    n = len(handle["src"])
    scatter = handle["scatter"]

    def body(*refs):
        ins, lrefs = refs[:n], refs[n:2 * n]
        send, recv = refs[2 * n], refs[2 * n + 1]
        x, y, c, _ = _me()
        for k in range(n):
            for d in range(1, NDEV):
                dev, _ = _peer(x, y, c, d)
                src = ins[k].at[0] if scatter else ins[k]
                cp = pltpu.make_async_remote_copy(src_ref=src, dst_ref=lrefs[k].at[0],
                                                  send_sem=send.at[k * NPEER + d - 1],
                                                  recv_sem=recv.at[k * NPEER + d - 1],
                                                  device_id=dev, device_id_type=pl.DeviceIdType.MESH)
                cp.wait_send()
                cp.wait_recv()

    arrs = handle["src"] + handle["land"]
    outs = pl.pallas_call(
        body, name=name,
        out_shape=tuple(pltpu.HBM(a.shape, a.dtype) for a in arrs),
        in_specs=[_HBM] * (2 * n) + [_SEM, _SEM, pl.BlockSpec(memory_space=pl.ANY)],
        out_specs=tuple([_HBM] * (2 * n)),
        input_output_aliases={i: i for i in range(2 * n)},
        compiler_params=pltpu.CompilerParams(has_side_effects=_EFFECT),
    )(*arrs, handle["send"], handle["recv"], after)
    me = 4 * lax.axis_index("x") + 2 * lax.axis_index("y") + lax.axis_index("c")
    landed = []
    for own, land in zip(outs[:n], outs[n:]):
        mine = lax.dynamic_index_in_dim(own, me, 0, keepdims=True) if scatter else own[None]
        landed.append(lax.dynamic_update_slice_in_dim(land, mine, me, 0))
    return landed


def _to_cat(w_in):
    k = w_in.shape[0]
    ba = jnp.zeros((k, NCAT - C_BA), w_in.dtype)
    ba = ba.at[:, 0:8].set(w_in[:, 4096:4104]).at[:, 128:136].set(w_in[:, 4104:4112])
    return jnp.concatenate([w_in[:, 0:3072], w_in[:, 3072:4096], w_in[:, 4112:7184], w_in[:, 7184:9232], ba], axis=1)


def _from_cat(dw):
    return jnp.concatenate([dw[:, 0:3072], dw[:, 3072:4096], dw[:, C_BA:C_BA + 8], dw[:, C_BA + 128:C_BA + 136],
                            dw[:, 4096:7168], dw[:, 7168:9216]], axis=1)


def _pad128(v):
    return jnp.pad(v, ((0, 0), (0, 128 - v.shape[1])))


def local_step(x, tgt, mod, wts, small, late_weights=None, on_grads=None):
    if on_grads is None:
        on_grads = lambda group, gd: jnp.zeros((), F32)
    t = x.shape[0]
    nc = t // CH
    shift_t, scale_t, gate_t, shift_f, scale_f, gate_f = mod
    wcat = _to_cat(wts["w_in"])
    a_log = _pad128(small["a_log"])
    dtb = _pad128(small["dt_bias"])
    vecs = dict(bga=small["b_gate"][:, :D], bgb=small["b_gate"][:, D:], gate_t=gate_t, g1=small["ln1_g"],
                b1=small["ln1_b"], scale_f=scale_f, shift_f=shift_f)

    h1 = modulate(x, scale_t, shift_t, "modulate_t")
    proj = matmul(h1, wcat, F32, "in_proj")
    q, k, v, gcs, beta = prep_fwd(proj, small["conv_a"], a_log, dtb)

    def col4(a):
        return a[:, :AH].reshape(nc, CH, AH).transpose(2, 0, 1)[..., None]

    gcol = col4(gcs)
    grow = gcol.reshape(AH, nc, 1, CH)
    bcol = col4(beta)
    u, w, qg, kd, qk, eg = c1_fwd(q, k, v, gcol, grow, bcol)
    oa, sall = c2_fwd(u, w, qg, kd, qk, eg, proj, small["norm_a"])
    bias = bias_table(small["rel_bias"])
    ob = attn_fwd(proj, bias)
    if late_weights is not None:
        wts = {**wts, **late_weights(ob)}
    y1, h2 = merge_fwd(x, oa, ob, proj, vecs, wts["w_a"], wts["w_b"], wts["w_o"])
    up = matmul(h2, wts["w_up"], F32, "up_proj")
    a = ffn_act_fwd(up, small["conv_ffn"], small["b_conv_ffn"])

    da, dy1_res, dffn, dgate_f, dg2, db2, loss = head_fwd_bwd(a, y1, tgt, gate_f, small["ln2_g"], small["ln2_b"],
                                                            wts["w_down"])
    g_w_down = matmul(a.T, dffn, F32, "wgrad_down")
    dup, g_conv_ffn, g_bconv = ffn_act_bwd(up, small["conv_ffn"], small["b_conv_ffn"], da)
    dh2 = matmul(dup, wts["w_up"].T, F32, "dgrad_up")
    g_w_up = matmul(h2.T, dup, F32, "wgrad_up")
    tok = on_grads("ffn", dict(w_up=g_w_up, w_down=g_w_down))
    dy1, dscale_f, dshift_f = modulate_bwd(dh2, y1, dy1_res, scale_f + tok, "modulate_f_bwd")
    (dx_res, doa, dob, dga, dgb, merged, dmix, dpa, dpb,
     dbga, dbgb, dgate_t, dg1, db1) = merge_bwd(x, oa, ob, proj, vecs, wts["w_a"], wts["w_b"], wts["w_o"], dy1)
    g_w_o = matmul(merged.T, dmix, F32, "wgrad_o")
    g_w_a = matmul(oa.T, dpa, F32, "wgrad_a")
    g_w_b = matmul(ob.T, dpb, F32, "wgrad_b")
    tok = on_grads("mix", dict(w_o=g_w_o, w_a=g_w_a, w_b=g_w_b))
    dqb, dkb, dvb, dbias = attn_bwd(proj, bias, dob)
    g_rel = relbias_reduce(bias_table_bwd_layout(dbias))
    du, dw, dqg, dkd, dqk, deg, dz, g_norm = c2_bwd(u, w, qg, kd, qk, eg, proj, small["norm_a"] + tok, sall, doa)
    dq, dk, dv, dgc, dgr, dbc = c1_bwd(q, k, v, gcol, grow, bcol, du, dw, dqg, dkd, dqk, deg)

    def from4(a4):
        return _pad128(a4[..., 0].transpose(1, 2, 0).reshape(t, AH))

    dgcs = from4(dgc) + from4(dgr.reshape(AH, nc, CH, 1))
    dbeta = from4(dbc)
    dpre, dbb, daa, g_conv_a, g_alog, g_dtb = prep_bwd(proj, small["conv_a"], a_log, dtb, dq, dk, dv, dgcs, dbeta)
    dba = jnp.concatenate([dbb, daa, jnp.zeros((t, NCAT - C_BA - 256), BF16)], axis=1)
    dproj = jnp.concatenate([dpre, dz, dqb, dkb, dvb, dga, dgb, dba], axis=1)
    g_wcat = matmul(h1.T, dproj, F32, "wgrad_in")
    tok = on_grads("in", dict(w_in=_from_cat(g_wcat)))
    dh1 = matmul(dproj, wcat.T, F32, "dgrad_in")
    grad_x, dscale_t, dshift_t = modulate_bwd(dh1, x, dx_res, scale_t + tok, "modulate_t_bwd")

    dmod = (dshift_t, dscale_t, dgate_t, dshift_f, dscale_f, dgate_f)
    grads = dict(w_in=_from_cat(g_wcat), w_up=g_w_up, w_down=g_w_down, w_a=g_w_a, w_b=g_w_b, w_o=g_w_o,
                 conv_a=g_conv_a, rel_bias=g_rel, conv_ffn=g_conv_ffn,
                 b_gate=jnp.concatenate([dbga, dbgb], axis=1), a_log=g_alog[:, :AH], dt_bias=g_dtb[:, :AH],
                 norm_a=g_norm, ln1_g=dg1, ln1_b=db1, b_conv_ffn=g_bconv, ln2_g=dg2, ln2_b=db2)
    return loss[0, 0], grad_x, dmod, grads


_REP = {}
_off = 0
for _n, _wd, _pw in (("b_ada", 6144, 6144), ("b_gate", 2048, 2048), ("a_log", 8, 128), ("dt_bias", 8, 128),
                     ("norm_a", 128, 128), ("ln1_g", 1024, 1024), ("ln1_b", 1024, 1024),
                     ("b_conv_ffn", 5632, 5632), ("ln2_g", 1024, 1024), ("ln2_b", 1024, 1024), ("loss", 1, 128)):
    _REP[_n] = (_off, _wd, _pw)
    _off += _pw
REP_LEN = _off
REP_NAMES = [n for n in _REP if n != "loss"]
_SH = (("conv_a", (4, 384)), ("rel_bias", (16, 40)), ("conv_ffn", (3, 704)))
SH_LEN = 4352


def _pack_rep(vals):
    parts = []
    for n, (_, wd, pw) in _REP.items():
        a = vals.get(n)
        a = jnp.zeros((1, pw), F32) if a is None else jnp.pad(a.reshape(1, wd), ((0, 0), (0, pw - wd)))
        parts.append(a)
    return jnp.concatenate(parts, axis=1)


def _unpack_rep(vec, name):
    o, wd, _ = _REP[name]
    return vec[:, o:o + wd]


def _pack_sh(vals):
    parts = [vals[n].reshape(vals[n].shape[:-2] + (-1,)) for n, _ in _SH]
    a = jnp.concatenate(parts, axis=-1)
    return jnp.pad(a, [(0, 0)] * (a.ndim - 1) + [(0, SH_LEN - a.shape[-1])])


def _unpack_sh(vec, name):
    o = 0
    for n, shp in _SH:
        sz = shp[0] * shp[1]
        if n == name:
            return vec[0, o:o + sz].reshape(shp)
        o += sz
    raise KeyError(name)


def _col_shards(a, n):
    return a.reshape(a.shape[0], NDEV, n).transpose(1, 0, 2)


def kernel(x, c, w_ada, b_ada, w_in, b_gate, conv_a, a_log, dt_bias, norm_a, rel_bias, w_branch_a, w_branch_b, w_o, ln1_g, ln1_b, w_up, conv_ffn, b_conv_ffn, w_down, ln2_g, ln2_b, loss_target, m_w_ada, m_b_ada, m_w_in, m_b_gate, m_conv_a, m_a_log, m_dt_bias, m_norm_a, m_rel_bias, m_w_branch_a, m_w_branch_b, m_w_o, m_ln1_g, m_ln1_b, m_w_up, m_conv_ffn, m_b_conv_ffn, m_w_down, m_ln2_g, m_ln2_b, v_w_ada, v_b_ada, v_w_in, v_b_gate, v_conv_a, v_a_log, v_dt_bias, v_norm_a, v_rel_bias, v_w_branch_a, v_w_branch_b, v_w_o, v_ln1_g, v_ln1_b, v_w_up, v_conv_ffn, v_b_conv_ffn, v_w_down, v_ln2_g, v_ln2_b):
    W = dict(w_ada=w_ada, b_ada=b_ada, w_in=w_in, b_gate=b_gate, conv_a=conv_a, a_log=a_log, dt_bias=dt_bias,
             norm_a=norm_a, rel_bias=rel_bias, w_branch_a=w_branch_a, w_branch_b=w_branch_b, w_o=w_o, ln1_g=ln1_g,
             ln1_b=ln1_b, w_up=w_up, conv_ffn=conv_ffn, b_conv_ffn=b_conv_ffn, w_down=w_down, ln2_g=ln2_g,
             ln2_b=ln2_b)
    M = dict(w_ada=m_w_ada, b_ada=m_b_ada, w_in=m_w_in, b_gate=m_b_gate, conv_a=m_conv_a, a_log=m_a_log,
             dt_bias=m_dt_bias, norm_a=m_norm_a, rel_bias=m_rel_bias, w_branch_a=m_w_branch_a,
             w_branch_b=m_w_branch_b, w_o=m_w_o, ln1_g=m_ln1_g, ln1_b=m_ln1_b, w_up=m_w_up, conv_ffn=m_conv_ffn,
             b_conv_ffn=m_b_conv_ffn, w_down=m_w_down, ln2_g=m_ln2_g, ln2_b=m_ln2_b)
    V = dict(w_ada=v_w_ada, b_ada=v_b_ada, w_in=v_w_in, b_gate=v_b_gate, conv_a=v_conv_a, a_log=v_a_log,
             dt_bias=v_dt_bias, norm_a=v_norm_a, rel_bias=v_rel_bias, w_branch_a=v_w_branch_a,
             w_branch_b=v_w_branch_b, w_o=v_w_o, ln1_g=v_ln1_g, ln1_b=v_ln1_b, w_up=v_w_up, conv_ffn=v_conv_ffn,
             b_conv_ffn=v_b_conv_ffn, w_down=v_w_down, ln2_g=v_ln2_g, ln2_b=v_ln2_b)
    W, M, V = ({n: a[0] for n, a in dct.items()} for dct in (W, M, V))
    me = 4 * lax.axis_index("x") + 2 * lax.axis_index("y") + lax.axis_index("c")
    big = ("w_in", "w_up", "w_down", "w_branch_a", "w_branch_b", "w_o")

    (g_in,) = all_gather([W["w_in"].astype(BF16)], "gather_w_in")
    wts = dict(w_in=g_in.transpose(1, 0, 2).reshape(D, -1))
    late, late_tok = exchange_start([W[n].astype(BF16) for n in big[1:]], "gather_late_start", False)

    def late_weights(after):
        g_up, g_down, g_a, g_b, g_o = exchange_wait(late, after, "gather_late_wait")
        return dict(w_up=g_up.transpose(1, 0, 2).reshape(D, -1), w_down=g_down.reshape(DFF, D),
                    w_a=g_a.reshape(D, D), w_b=g_b.reshape(D, D), w_o=g_o.reshape(D, D))

    c_all, sh_all = all_gather([c, _pack_sh({n: W[n] for n, _ in _SH})[None]], "gather_small")
    c_all = c_all.reshape(NDEV, D)
    sh_all = sh_all.reshape(NDEV, SH_LEN)

    def full_small(name, shp):
        o = 0
        for n, s in _SH:
            if n == name:
                break
            o += s[0] * s[1]
        sz = shp[0] * shp[1]
        return sh_all[:, o:o + sz].reshape(NDEV, shp[0], shp[1]).transpose(1, 0, 2).reshape(shp[0], NDEV * shp[1])

    small = dict(conv_a=full_small("conv_a", (4, 384)), rel_bias=full_small("rel_bias", (16, 40)),
                 conv_ffn=full_small("conv_ffn", (3, 704)),
                 b_gate=W["b_gate"][None], a_log=W["a_log"][None], dt_bias=W["dt_bias"][None],
                 norm_a=W["norm_a"][None], ln1_g=W["ln1_g"][None], ln1_b=W["ln1_b"][None],
                 b_conv_ffn=W["b_conv_ffn"][None], ln2_g=W["ln2_g"][None], ln2_b=W["ln2_b"][None])

    nsh = w_ada.shape[2]
    b_sh = lax.dynamic_slice(W["b_ada"][None], (0, me * nsh), (1, nsh))
    mod_sh = ada_fwd(c_all, W["w_ada"], b_sh)
    (mod_rows,) = all_to_all([mod_sh[:, None, :]], "scatter_mod")
    mod6 = mod_rows.reshape(6, D)
    mod6 = mod6 + late_tok
    mod = tuple(mod6[i:i + 1] for i in range(6))

    pending = {}

    def on_grads(group, gd):
        if group == "ffn":
            slabs = [_col_shards(gd["w_up"], w_up.shape[2]), gd["w_down"].reshape(NDEV, -1, D)]
        elif group == "mix":
            slabs = [gd[n].reshape(NDEV, -1, D) for n in ("w_a", "w_b", "w_o")]
        else:
            slabs = [_col_shards(gd["w_in"], w_in.shape[2])]
        pending[group], tok = exchange_start([s.astype(BF16) for s in slabs], "scatter_" + group + "_start", True)
        return tok

    loss, grad_x, dmod, g = local_step(x[0], loss_target[0], mod, wts, small, late_weights, on_grads)

    rep_vals = {n: g[n] for n in REP_NAMES if n != "b_ada"}
    rep_vals["b_ada"] = jnp.concatenate(dmod, axis=1)
    rep_vals["loss"] = loss.reshape(1, 1)
    (rep_all,) = all_gather([_pack_rep(rep_vals)[None]], "gather_small_grads")
    rep_all = rep_all.reshape(NDEV, 1, REP_LEN)
    zero1 = jnp.zeros((1, 1), F32)
    rep_out = adamw(rep_all, _pack_rep({**{n: W[n][None] for n in REP_NAMES}, "loss": zero1}),
                    _pack_rep({**{n: M[n][None] for n in REP_NAMES}, "loss": zero1}),
                    _pack_rep({**{n: V[n][None] for n in REP_NAMES}, "loss": zero1}), "adamw_small")
    loss_total = _unpack_rep(rep_out[0], "loss")[0, 0]

    o_ada = _REP["b_ada"][0]
    dmod_all = rep_all[:, 0, o_ada:o_ada + 6 * D]
    dmod_sh = lax.dynamic_slice(dmod_all, (0, me * nsh), (NDEV, nsh))
    g_w_ada = ada_wgrad(c_all.T, dmod_sh)

    p_up, p_down = exchange_wait(pending["ffn"], grad_x, "scatter_ffn_wait")
    p_a, p_b, p_o = exchange_wait(pending["mix"], grad_x, "scatter_mix_wait")
    (p_in,) = exchange_wait(pending["in"], grad_x, "scatter_in_wait")
    parts = [p_in, p_up, p_down, p_a, p_b, p_o]
    sh_parts = {"conv_a": _col_shards(g["conv_a"], 384), "rel_bias": _col_shards(g["rel_bias"], 40),
                "conv_ffn": _col_shards(g["conv_ffn"], 704)}
    (sh_recv,) = all_to_all([_pack_sh(sh_parts)[:, None, :]], "scatter_small_grads")

    res = {}
    for n, p in zip(big, parts):
        res[n] = adamw(p, W[n], M[n], V[n], "adamw_" + n)
    res["w_ada"] = adamw(g_w_ada[None], W["w_ada"], M["w_ada"], V["w_ada"], "adamw_w_ada")
    sh_out = adamw(sh_recv, _pack_sh({n: W[n] for n, _ in _SH})[None], _pack_sh({n: M[n] for n, _ in _SH})[None],
                   _pack_sh({n: V[n] for n, _ in _SH})[None], "adamw_small_sharded")
    for n, _ in _SH:
        res[n] = tuple(_unpack_sh(o, n) for o in sh_out)
    for n in REP_NAMES:
        res[n] = tuple(_unpack_rep(o, n)[0] for o in rep_out)

    order = ("w_ada", "b_ada", "w_in", "b_gate", "conv_a", "a_log", "dt_bias", "norm_a", "rel_bias", "w_branch_a",
             "w_branch_b", "w_o", "ln1_g", "ln1_b", "w_up", "conv_ffn", "b_conv_ffn", "w_down", "ln2_g", "ln2_b")
    outs = [loss_total, grad_x[None]]
    for kind in range(4):
        outs += [res[n][kind][None] for n in order]
    return tuple(outs)
```

```python
import functools
import math

import numpy as np
import jax
import jax.numpy as jnp
from jax import lax
from jax.experimental import pallas as pl
from jax.experimental.pallas import tpu as pltpu

F32 = jnp.float32
BF16 = jnp.bfloat16
HI = lax.Precision.HIGHEST

D = 1024
CH = 64
AH, ADK = 8, 128
BH, BDH = 16, 64
BPREV = 8
BMAXREL = 256
RELSZ = CH + BMAXREL
DFF = 2816
ALPHA = 2.0 ** 0.25
LN_EPS, RMS_EPS, L2_EPS = 1e-5, 1e-6, 1e-6
NEG = -1e30
LR, B1, B2, AEPS, WD, STEP = 1e-3, 0.9, 0.999, 1e-8, 0.01, 10
NDEV = 8
HALO = 8
TQ = 512
VMEM_LIMIT = 56 * 1024 * 1024

C_QKVA, C_Z, C_QKVB, C_GATE, C_BA, NCAT = 0, 3072, 4096, 7168, 9216, 9728


def _cparams(n_axes=1, vmem=VMEM_LIMIT):
    return pltpu.CompilerParams(dimension_semantics=("arbitrary",) * n_axes, vmem_limit_bytes=vmem)


def _dg(a, b, ca, cb):
    return lax.dot_general(a.astype(BF16), b.astype(BF16), (((ca,), (cb,)), ((), ())),
                           preferred_element_type=F32)


@jax.custom_vjp
def mm_nn(a, b):
    return _dg(a, b, 1, 0)


@jax.custom_vjp
def mm_nt(a, b):
    return _dg(a, b, 1, 1)


@jax.custom_vjp
def mm_tn(a, b):
    return _dg(a, b, 0, 0)


mm_nn.defvjp(lambda a, b: (mm_nn(a, b), (a, b)),
             lambda r, g: (mm_nt(g, r[1]).astype(r[0].dtype), mm_tn(r[0], g).astype(r[1].dtype)))
mm_nt.defvjp(lambda a, b: (mm_nt(a, b), (a, b)),
             lambda r, g: (mm_nn(g, r[1]).astype(r[0].dtype), mm_tn(g, r[0]).astype(r[1].dtype)))
mm_tn.defvjp(lambda a, b: (mm_tn(a, b), (a, b)),
             lambda r, g: (mm_nt(r[1], g).astype(r[0].dtype), mm_nn(r[0], g).astype(r[1].dtype)))


@jax.custom_vjp
def mm_w(a, w):
    return _dg(a, w, 1, 0)


mm_w.defvjp(lambda a, w: (mm_w(a, w), (a, w)),
            lambda r, g: (mm_nt(g, r[1]).astype(r[0].dtype), jnp.zeros_like(r[1])))


def _mmh(a, b):
    return lax.dot_general(a, b, (((1,), (0,)), ((), ())), precision=HI, preferred_element_type=F32)


def _bdg(a, b, ca, cb):
    return lax.dot_general(a.astype(BF16), b.astype(BF16), (((ca,), (cb,)), ((0,), (0,))),
                           preferred_element_type=F32)


@jax.custom_vjp
def bmm_nn(a, b):
    return _bdg(a, b, 2, 1)


@jax.custom_vjp
def bmm_nt(a, b):
    return _bdg(a, b, 2, 2)


@jax.custom_vjp
def bmm_tn(a, b):
    return _bdg(a, b, 1, 1)


bmm_nn.defvjp(lambda a, b: (bmm_nn(a, b), (a, b)), lambda r, g: (bmm_nt(g, r[1]), bmm_tn(r[0], g)))
bmm_nt.defvjp(lambda a, b: (bmm_nt(a, b), (a, b)), lambda r, g: (bmm_nn(g, r[1]), bmm_tn(g, r[0])))
bmm_tn.defvjp(lambda a, b: (bmm_tn(a, b), (a, b)), lambda r, g: (bmm_nt(r[1], g), bmm_nn(r[0], g)))


def _bdg3(a, b, ca, cb):
    return lax.dot_general(a, b, (((ca,), (cb,)), ((0,), (0,))), precision=HI, preferred_element_type=F32)


@jax.custom_vjp
def bmm3_nn(a, b):
    return _bdg3(a, b, 2, 1)


bmm3_nn.defvjp(lambda a, b: (bmm3_nn(a, b), (a, b)),
               lambda r, g: (_bdg3(g, r[1], 2, 2), _bdg3(r[0], g, 1, 1)))


def _sigmoid(x):
    return 0.5 * jnp.tanh(0.5 * x) + 0.5


def _silu(x):
    return x * _sigmoid(x)


def _softplus(x):
    return jnp.maximum(x, 0.0) + jnp.log(1.0 + jnp.exp(-jnp.abs(x)))


def _layernorm(r, g, b):
    mu = jnp.mean(r, axis=-1, keepdims=True)
    xc = r - mu
    var = jnp.mean(xc * xc, axis=-1, keepdims=True)
    return xc * lax.rsqrt(var + LN_EPS) * g + b


def _iota2(shape, dim):
    return lax.broadcasted_iota(jnp.int32, shape, dim)


@jax.custom_vjp
def causal_conv(ext, rows):
    k = len(rows)
    y = None
    for j in range(k):
        s = k - 1 - j
        r = pltpu.roll(ext, s, 0) if s else ext
        t = r[HALO:] * rows[j]
        y = t if y is None else y + t
    return y


def _causal_conv_fwd(ext, rows):
    return causal_conv(ext, rows), (ext, rows)


def _causal_conv_bwd(res, g):
    ext, rows = res
    n = ext.shape[0]
    k = len(rows)
    gext = jnp.concatenate([jnp.zeros((HALO, g.shape[1]), g.dtype), g], axis=0)
    dext = None
    drows = []
    for j in range(k):
        s = k - 1 - j
        up = pltpu.roll(gext, n - s, 0) if s else gext
        t = up * rows[j]
        dext = t if dext is None else dext + t
        r = pltpu.roll(ext, s, 0) if s else ext
        drows.append(jnp.sum(g * r[HALO:], axis=0, keepdims=True))
    return dext, tuple(drows)


causal_conv.defvjp(_causal_conv_fwd, _causal_conv_bwd)


def _chunk_masks(tm):
    i = _iota2((tm, tm), 0)
    j = _iota2((tm, tm), 1)
    same = (i ^ j) < CH
    lower = jnp.where(same & (j <= i), 1.0, 0.0).astype(F32)
    upper = jnp.where(same & (i <= j), 1.0, 0.0).astype(F32)
    return lower, upper


@jax.custom_vjp
def chunk_cumsum(g):
    lower, _ = _chunk_masks(g.shape[0])
    return _mmh(lower, g)


def _chunk_cumsum_bwd(_, ct):
    _, upper = _chunk_masks(ct.shape[0])
    return (_mmh(upper, ct),)


chunk_cumsum.defvjp(lambda g: (chunk_cumsum(g), None), _chunk_cumsum_bwd)


@jax.custom_vjp
def inv_unit_lower(a):
    n = a.shape[-1]
    eye = jnp.where(_iota2((1, n, n), 1) == _iota2((1, n, n), 2), 1.0, 0.0).astype(F32)
    x = eye - a
    p = _bdg3(a, a, 2, 1)
    steps = int(math.log2(n)) - 1
    for s in range(steps):
        x = x + _bdg3(x, p, 2, 1)
        if s + 1 < steps:
            p = _bdg3(p, p, 2, 1)
    return x


def _inv_fwd(a):
    t = inv_unit_lower(a)
    return t, t


def _inv_bwd(t, g):
    return (-_bdg3(_bdg3(t, g, 1, 1), t, 2, 2),)


inv_unit_lower.defvjp(_inv_fwd, _inv_bwd)


def prep_fn(ext, bb, aa, rows, a_log, dtb):
    s = _silu(causal_conv(ext, rows))
    qs, ks, vs = [], [], []
    for h in range(AH):
        qh = s[:, h * ADK:(h + 1) * ADK]
        kh = s[:, D + h * ADK:D + (h + 1) * ADK]
        qs.append(qh * lax.rsqrt(jnp.sum(qh * qh, axis=-1, keepdims=True) + L2_EPS) * (ADK ** -0.5))
        ks.append(kh * lax.rsqrt(jnp.sum(kh * kh, axis=-1, keepdims=True) + L2_EPS))
        vs.append(s[:, 2 * D + h * ADK:2 * D + (h + 1) * ADK])
    beta = _sigmoid(bb)
    g = -jnp.exp(a_log) * _softplus(aa + dtb)
    return tuple(qs), tuple(ks), tuple(vs), chunk_cumsum(g), beta


def c1_heads(q, k, v, gcol, grow, bcol):
    i = _iota2((1, CH, CH), 1)
    j = _iota2((1, CH, CH), 2)
    causal = j <= i
    strict = j < i
    diff = gcol - grow
    decay = jnp.where(causal, jnp.exp(jnp.where(causal, diff, 0.0)), 0.0)
    kb = k * bcol
    vb = v * bcol
    a_low = jnp.where(strict, bmm_nt(kb, k) * decay, 0.0)
    tinv = inv_unit_lower(a_low)
    egc = jnp.exp(gcol)
    u = bmm3_nn(tinv, vb)
    w = bmm3_nn(tinv, kb * egc)
    qk = jnp.where(causal, bmm_nt(q, k) * decay, 0.0)
    glast = jnp.sum(jnp.where(_iota2((1, CH, 1), 1) == CH - 1, gcol, 0.0), axis=1, keepdims=True)
    qg = q * egc
    kd = k * jnp.exp(glast - gcol)
    eg = jnp.exp(glast) * jnp.ones((1, 1, ADK), F32)
    return u, w, qk, qg, kd, eg


def c2_heads(s, u, w, qk, qg, kd, eg, z, nw):
    vn = u - bmm_nn(w, s)
    o = bmm_nn(qg, s) + bmm_nn(qk, vn)
    s2 = s * eg + bmm_tn(kd, vn)
    ms = jnp.mean(o * o, axis=-1, keepdims=True)
    og = o * lax.rsqrt(ms + RMS_EPS) * nw * _silu(z)
    return og, s2


def attn_pair(q, kp, kc, vp, vc, bias2, firstf):
    k = jnp.concatenate([kp, kc], axis=0)
    v = jnp.concatenate([vp, vc], axis=0)
    lane = _iota2((1, 2 * BDH), 1)
    col = _iota2((1, 2 * TQ), 1)
    nokey = jnp.where(col < TQ, firstf, 0.0) * NEG
    out = None
    for hh in range(2):
        hm = jnp.where((lane >= hh * BDH) & (lane < (hh + 1) * BDH), 1.0, 0.0).astype(F32)
        s = mm_nt(q * hm, k) * (BDH ** -0.5) + assemble_bias(bias2[hh]) + nokey
        m = lax.stop_gradient(jnp.max(s, axis=-1, keepdims=True))
        p = jnp.exp(s - m)
        p = p / jnp.sum(p, axis=-1, keepdims=True)
        o = mm_nn(p, v) * hm
        out = o if out is None else out + o
    return out


def merge_fn(x, oa, ob, gra, grb, p_pa, p_pb, p_mix, bga, bgb, gate_t, g1, b1, scale_f, shift_f,
             wa, wb, wo):
    ga = _sigmoid(gra + bga)
    gb = _sigmoid(grb + bgb)
    pa = mm_w(oa, wa) + p_pa
    pb = mm_w(ob, wb) + p_pb
    merged = ga * pa + gb * pb
    mix = mm_w(merged, wo) + p_mix
    y1 = _layernorm(ALPHA * x + gate_t * mix, g1, b1)
    return y1, merged


def ffn_act_fn(ext, rows, bconv):
    u = causal_conv(ext, rows) + bconv
    return _silu(u[:, :DFF]) * u[:, DFF:]


def head_fn(a, y1, p_ffn, gate_f, g2, b2, tgt, wd):
    ffn = mm_w(a, wd) + p_ffn
    y2 = _layernorm(ALPHA * y1 + gate_f * ffn, g2, b2)
    err = y2 - tgt
    return 0.5 * jnp.sum(jnp.mean(err * err, axis=-1, keepdims=True))


def _rows(tm, width, colblk=0, order=None):
    if order is None:
        return pl.BlockSpec((tm, width), lambda i: (i, colblk))
    return pl.BlockSpec((tm, width), lambda i: (order(i), colblk))


def _const(shape):
    nd = len(shape)
    return pl.BlockSpec(shape, lambda *_: (0,) * nd)


def _pick(n, cands):
    for c in cands:
        if n % c == 0:
            return c
    raise ValueError(f"no tile for {n}")


def _onehot_rows(k, j):
    return jnp.where(_iota2((k, 1), 0) == j, 1.0, 0.0).astype(F32)


def matmul(a, w, out_dtype, name, ta=False, tb=False):
    kdim, m = a.shape if ta else a.shape[::-1]
    n = w.shape[0] if tb else w.shape[1]
    tm = _pick(m, (1024, 512, 256, 128))
    tn = _pick(n, (1024, 512, 256, 128))
    tk = _pick(kdim, (1024, 512, 256, 128))
    nk = kdim // tk
    a_spec = (pl.BlockSpec((tk, tm), lambda i, j, k: (k, i)) if ta
              else pl.BlockSpec((tm, tk), lambda i, j, k: (i, k)))
    w_spec = (pl.BlockSpec((tn, tk), lambda i, j, k: (j, k)) if tb
              else pl.BlockSpec((tk, tn), lambda i, j, k: (k, j)))

    def body(a_ref, w_ref, o_ref, *scratch):
        p = _dg(a_ref[...], w_ref[...], 0 if ta else 1, 1 if tb else 0)
        if nk == 1:
            o_ref[...] = p.astype(out_dtype)
            return
        acc = scratch[0]
        k = pl.program_id(2)

        @pl.when(k == 0)
        def _():
            acc[...] = p

        @pl.when(k > 0)
        def _():
            acc[...] += p

        @pl.when(k == nk - 1)
        def _():
            o_ref[...] = acc[...].astype(out_dtype)

    return pl.pallas_call(
        body, name=name,
        grid=(m // tm, n // tn, nk),
        in_specs=[a_spec, w_spec],
        out_specs=pl.BlockSpec((tm, tn), lambda i, j, k: (i, j)),
        out_shape=jax.ShapeDtypeStruct((m, n), out_dtype),
        scratch_shapes=[] if nk == 1 else [pltpu.VMEM((tm, tn), F32)],
        compiler_params=_cparams(3),
    )(a, w)


def modulate(x, scale, shift, name):
    t, d = x.shape
    tm = _pick(t, (512, 256, 128))

    def body(x_ref, sc_ref, sh_ref, o_ref):
        o_ref[...] = (x_ref[...] * (1.0 + sc_ref[...]) + sh_ref[...]).astype(BF16)

    return pl.pallas_call(
        body, name=name, grid=(t // tm,),
        in_specs=[_rows(tm, d), _const((1, d)), _const((1, d))],
        out_specs=_rows(tm, d),
        out_shape=jax.ShapeDtypeStruct((t, d), BF16),
        compiler_params=_cparams(),
    )(x, scale, shift)


def modulate_bwd(dh, xin, dres, scale, name):
    t, d = dh.shape
    tm = _pick(t, (512, 256, 128))

    def body(dh_ref, x_ref, dr_ref, sc_ref, o_ref, dsc_ref, dsh_ref):
        i = pl.program_id(0)
        dh_v = dh_ref[...]
        o_ref[...] = dr_ref[...] + dh_v * (1.0 + sc_ref[...])

        @pl.when(i == 0)
        def _():
            dsc_ref[...] = jnp.zeros_like(dsc_ref)
            dsh_ref[...] = jnp.zeros_like(dsh_ref)

        dsc_ref[...] += jnp.sum(dh_v * x_ref[...], axis=0, keepdims=True)
        dsh_ref[...] += jnp.sum(dh_v, axis=0, keepdims=True)

    return pl.pallas_call(
        body, name=name, grid=(t // tm,),
        in_specs=[_rows(tm, d), _rows(tm, d), _rows(tm, d), _const((1, d))],
        out_specs=[_rows(tm, d), _const((1, d)), _const((1, d))],
        out_shape=[jax.ShapeDtypeStruct((t, d), F32), jax.ShapeDtypeStruct((1, d), F32),
                   jax.ShapeDtypeStruct((1, d), F32)],
        compiler_params=_cparams(),
    )(dh, xin, dres, scale)


PREP_TM = 128


def _halo_specs(tm, width, colblk, order):
    per = tm // HALO
    return [pl.BlockSpec((HALO, width), lambda i: (jnp.maximum(order(i) * per - 1, 0), colblk)),
            pl.BlockSpec((tm, width), lambda i: (order(i), colblk))]


def prep_fwd(proj, conv_a, a_log, dtb):
    t = proj.shape[0]
    tm = PREP_TM
    nt = t // tm
    wq = 3 * D

    def body(prev_ref, cur_ref, bb_ref, aa_ref, cw_ref, al_ref, dt_ref, q_ref, k_ref, v_ref, g_ref, b_ref):
        i = pl.program_id(0)
        flag = jnp.where(i > 0, 1.0, 0.0)
        ext = jnp.concatenate([prev_ref[...] * flag, cur_ref[...]], axis=0)
        rows = tuple(cw_ref[j:j + 1, :] for j in range(4))
        qs, ks, vs, gcs, beta = prep_fn(ext, bb_ref[...], aa_ref[...], rows, al_ref[...], dt_ref[...])
        for h in range(AH):
            q_ref[h] = qs[h]
            k_ref[h] = ks[h]
            v_ref[h] = vs[h]
        g_ref[...] = gcs
        b_ref[...] = beta

    ident = lambda i: i
    hm = pl.BlockSpec((AH, tm, ADK), lambda i: (0, i, 0))
    return pl.pallas_call(
        body, name="prep_fwd", grid=(nt,),
        in_specs=_halo_specs(tm, wq, 0, ident) + [
            _rows(tm, 128, C_BA // 128), _rows(tm, 128, C_BA // 128 + 1),
            _const((4, wq)), _const((1, 128)), _const((1, 128))],
        out_specs=[hm, hm, hm, _rows(tm, 128), _rows(tm, 128)],
        out_shape=[jax.ShapeDtypeStruct((AH, t, ADK), F32)] * 3 + [jax.ShapeDtypeStruct((t, 128), F32)] * 2,
        compiler_params=_cparams(),
    )(proj, proj, proj, proj, conv_a, a_log, dtb)


def prep_bwd(proj, conv_a, a_log, dtb, dq, dk, dv, dgcs, dbeta):
    t = proj.shape[0]
    tm = PREP_TM
    nt = t // tm
    wq = 3 * D
    rev = lambda i: nt - 1 - i

    def body(prev_ref, cur_ref, bb_ref, aa_ref, cw_ref, al_ref, dt_ref,
             dq_ref, dk_ref, dv_ref, dg_ref, db_ref,
             dpre_ref, dbb_ref, daa_ref, dcw_ref, dal_ref, ddt_ref, carry):
        i = pl.program_id(0)
        flag = jnp.where(i < nt - 1, 1.0, 0.0)
        ext = jnp.concatenate([prev_ref[...] * flag, cur_ref[...]], axis=0)
        rows = tuple(cw_ref[j:j + 1, :] for j in range(4))
        _, vjp = jax.vjp(prep_fn, ext, bb_ref[...], aa_ref[...], rows, al_ref[...], dt_ref[...])
        dext, dbb, daa, drows, dal, ddt = vjp((tuple(dq_ref[h] for h in range(AH)),
                                               tuple(dk_ref[h] for h in range(AH)),
                                               tuple(dv_ref[h] for h in range(AH)), dg_ref[...], db_ref[...]))

        @pl.when(i == 0)
        def _():
            carry[...] = jnp.zeros_like(carry)
            dcw_ref[...] = jnp.zeros_like(dcw_ref)
            dal_ref[...] = jnp.zeros_like(dal_ref)
            ddt_ref[...] = jnp.zeros_like(ddt_ref)

        dcur = dext[HALO:]
        dpre_ref[...] = jnp.concatenate([dcur[:tm - HALO], dcur[tm - HALO:] + carry[...]], axis=0).astype(BF16)
        carry[...] = dext[:HALO]
        dbb_ref[...] = dbb.astype(BF16)
        daa_ref[...] = daa.astype(BF16)
        dcw = None
        for j in range(4):
            tj = _onehot_rows(4, j) * drows[j]
            dcw = tj if dcw is None else dcw + tj
        dcw_ref[...] += dcw
        dal_ref[...] += dal
        ddt_ref[...] += ddt

    hm = pl.BlockSpec((AH, tm, ADK), lambda i: (0, rev(i), 0))
    return pl.pallas_call(
        body, name="prep_bwd", grid=(nt,),
        in_specs=_halo_specs(tm, wq, 0, rev) + [
            _rows(tm, 128, C_BA // 128, rev), _rows(tm, 128, C_BA // 128 + 1, rev),
            _const((4, wq)), _const((1, 128)), _const((1, 128)),
            hm, hm, hm, _rows(tm, 128, 0, rev), _rows(tm, 128, 0, rev)],
        out_specs=[_rows(tm, wq, 0, rev), _rows(tm, 128, 0, rev), _rows(tm, 128, 0, rev),
                   _const((4, wq)), _const((1, 128)), _const((1, 128))],
        out_shape=[jax.ShapeDtypeStruct((t, wq), BF16), jax.ShapeDtypeStruct((t, 128), BF16),
                   jax.ShapeDtypeStruct((t, 128), BF16), jax.ShapeDtypeStruct((4, wq), F32),
                   jax.ShapeDtypeStruct((1, 128), F32), jax.ShapeDtypeStruct((1, 128), F32)],
        scratch_shapes=[pltpu.VMEM((HALO, wq), F32)],
        compiler_params=_cparams(),
    )(proj, proj, proj, proj, conv_a, a_log, dtb, dq, dk, dv, dgcs, dbeta)


def _c1_specs(order):
    hm = pl.BlockSpec((AH, CH, ADK), lambda n: (0, order(n), 0))
    col = pl.BlockSpec((AH, 1, CH, 1), lambda n: (0, order(n), 0, 0))
    row = pl.BlockSpec((AH, 1, 1, CH), lambda n: (0, order(n), 0, 0))
    qk = pl.BlockSpec((1, AH, CH, CH), lambda n: (order(n), 0, 0, 0))
    eg = pl.BlockSpec((1, AH, 1, ADK), lambda n: (order(n), 0, 0, 0))
    return hm, col, row, qk, eg


def _heads(ref):
    return jnp.stack([ref[:, h * ADK:(h + 1) * ADK] for h in range(AH)], axis=0)


def c1_fwd(q, k, v, gcol, grow, bcol):
    t = q.shape[1]
    nc = t // CH
    hm, col, row, qks, egs = _c1_specs(lambda n: n)

    def body(q_ref, k_ref, v_ref, gc_ref, gr_ref, bc_ref, u_ref, w_ref, qg_ref, kd_ref, qk_ref, eg_ref):
        u, w, qk, qg, kd, eg = c1_heads(q_ref[...], k_ref[...], v_ref[...],
                                        gc_ref[:, 0], gr_ref[:, 0], bc_ref[:, 0])
        u_ref[...] = u
        w_ref[...] = w
        qg_ref[...] = qg
        kd_ref[...] = kd
        qk_ref[0] = qk
        eg_ref[0] = eg

    return pl.pallas_call(
        body, name="c1_fwd", grid=(nc,),
        in_specs=[hm, hm, hm, col, row, col],
        out_specs=[hm, hm, hm, hm, qks, egs],
        out_shape=[jax.ShapeDtypeStruct((AH, t, ADK), F32)] * 4 + [
            jax.ShapeDtypeStruct((nc, AH, CH, CH), F32), jax.ShapeDtypeStruct((nc, AH, 1, ADK), F32)],
        compiler_params=_cparams(),
    )(q, k, v, gcol, grow, bcol)


def c1_bwd(q, k, v, gcol, grow, bcol, du, dw, dqg, dkd, dqk, deg):
    t = q.shape[1]
    nc = t // CH
    hm, col, row, qks, egs = _c1_specs(lambda n: n)

    def body(q_ref, k_ref, v_ref, gc_ref, gr_ref, bc_ref, du_ref, dw_ref, dqg_ref, dkd_ref, dqk_ref, deg_ref,
             dq_ref, dk_ref, dv_ref, dgc_ref, dgr_ref, dbc_ref):
        _, vjp = jax.vjp(c1_heads, q_ref[...], k_ref[...], v_ref[...], gc_ref[:, 0], gr_ref[:, 0], bc_ref[:, 0])
        dq, dk, dv, dgc, dgr, dbc = vjp((du_ref[...], dw_ref[...], dqk_ref[0], dqg_ref[...], dkd_ref[...],
                                         deg_ref[0]))
        dq_ref[...] = dq
        dk_ref[...] = dk
        dv_ref[...] = dv
        dgc_ref[:, 0] = dgc
        dgr_ref[:, 0] = dgr
        dbc_ref[:, 0] = dbc

    return pl.pallas_call(
        body, name="c1_bwd", grid=(nc,),
        in_specs=[hm, hm, hm, col, row, col, hm, hm, hm, hm, qks, egs],
        out_specs=[hm, hm, hm, col, row, col],
        out_shape=[jax.ShapeDtypeStruct((AH, t, ADK), F32)] * 3 + [
            jax.ShapeDtypeStruct((AH, nc, CH, 1), F32), jax.ShapeDtypeStruct((AH, nc, 1, CH), F32),
            jax.ShapeDtypeStruct((AH, nc, CH, 1), F32)],
        compiler_params=_cparams(),
    )(q, k, v, gcol, grow, bcol, du, dw, dqg, dkd, dqk, deg)


def c2_fwd(u, w, qg, kd, qk, eg, proj, norm_a):
    t = u.shape[1]
    nc = t // CH
    hm, _, _, qks, egs = _c1_specs(lambda n: n)
    tok = pl.BlockSpec((CH, D), lambda n: (n, 0))
    zspec = pl.BlockSpec((CH, D), lambda n: (n, C_Z // D))
    sspec = pl.BlockSpec((1, AH, ADK, ADK), lambda n: (n, 0, 0, 0))

    def body(u_ref, w_ref, qg_ref, kd_ref, qk_ref, eg_ref, z_ref, nw_ref, o_ref, sall_ref, st):
        n = pl.program_id(0)

        @pl.when(n == 0)
        def _():
            st[...] = jnp.zeros_like(st)

        s = st[...]
        sall_ref[0] = s
        og, s2 = c2_heads(s, u_ref[...], w_ref[...], qk_ref[0], qg_ref[...], kd_ref[...], eg_ref[0],
                          _heads(z_ref), nw_ref[...])
        st[...] = s2
        for h in range(AH):
            o_ref[:, h * ADK:(h + 1) * ADK] = og[h].astype(BF16)

    return pl.pallas_call(
        body, name="c2_fwd", grid=(nc,),
        in_specs=[hm, hm, hm, hm, qks, egs, zspec, _const((1, ADK))],
        out_specs=[tok, sspec],
        out_shape=[jax.ShapeDtypeStruct((t, D), BF16), jax.ShapeDtypeStruct((nc, AH, ADK, ADK), F32)],
        scratch_shapes=[pltpu.VMEM((AH, ADK, ADK), F32)],
        compiler_params=_cparams(),
    )(u, w, qg, kd, qk, eg, proj, norm_a)


def c2_bwd(u, w, qg, kd, qk, eg, proj, norm_a, sall, do):
    t = u.shape[1]
    nc = t // CH
    rev = lambda n: nc - 1 - n
    hm, _, _, qks, egs = _c1_specs(rev)
    tok = pl.BlockSpec((CH, D), lambda n: (rev(n), 0))
    zspec = pl.BlockSpec((CH, D), lambda n: (rev(n), C_Z // D))
    sspec = pl.BlockSpec((1, AH, ADK, ADK), lambda n: (rev(n), 0, 0, 0))

    def body(u_ref, w_ref, qg_ref, kd_ref, qk_ref, eg_ref, z_ref, nw_ref, sall_ref, do_ref,
             du_ref, dw_ref, dqg_ref, dkd_ref, dqk_ref, deg_ref, dz_ref, dnw_ref, dst):
        n = pl.program_id(0)

        @pl.when(n == 0)
        def _():
            dst[...] = jnp.zeros_like(dst)
            dnw_ref[...] = jnp.zeros_like(dnw_ref)

        _, vjp = jax.vjp(c2_heads, sall_ref[0], u_ref[...], w_ref[...], qk_ref[0], qg_ref[...], kd_ref[...],
                         eg_ref[0], _heads(z_ref), nw_ref[...])
        ds, du, dw, dqk, dqg, dkd, deg, dz, dn = vjp((_heads(do_ref), dst[...]))
        dst[...] = ds
        du_ref[...] = du
        dw_ref[...] = dw
        dqg_ref[...] = dqg
        dkd_ref[...] = dkd
        dqk_ref[0] = dqk
        deg_ref[0] = deg
        for h in range(AH):
            dz_ref[:, h * ADK:(h + 1) * ADK] = dz[h].astype(BF16)
        dnw_ref[...] += dn

    return pl.pallas_call(
        body, name="c2_bwd", grid=(nc,),
        in_specs=[hm, hm, hm, hm, qks, egs, zspec, _const((1, ADK)), sspec, tok],
        out_specs=[hm, hm, hm, hm, qks, egs, tok, _const((1, ADK))],
        out_shape=[jax.ShapeDtypeStruct((AH, t, ADK), F32)] * 4 + [
            jax.ShapeDtypeStruct((nc, AH, CH, CH), F32), jax.ShapeDtypeStruct((nc, AH, 1, ADK), F32),
            jax.ShapeDtypeStruct((t, D), BF16), jax.ShapeDtypeStruct((1, ADK), F32)],
        scratch_shapes=[pltpu.VMEM((AH, ADK, ADK), F32)],
        compiler_params=_cparams(),
    )(u, w, qg, kd, qk, eg, proj, norm_a, sall, do)


NQB = TQ // CH
NKB = 2 * TQ // CH
NDIST = BPREV + 1
KLO = -(NQB - 2)
NPAIR = NKB - 1 - KLO + 1


def bias_table(rel_bias):
    nh = rel_bias.shape[0]
    relx = jnp.concatenate([rel_bias, jnp.broadcast_to(rel_bias[:, -1:], (nh, CH * BPREV + 2 * CH - 1 - RELSZ))],
                           axis=1)
    t = jnp.stack([relx[:, CH * k:CH * k + 2 * CH - 1] for k in range(NDIST)], axis=1)
    trev = t[:, :, ::-1]
    g2 = jnp.concatenate([trev[:, :, CH - 1:], jnp.zeros((nh, NDIST, 1), F32), trev[:, :, :CH - 1]], axis=2)
    flat = jnp.tile(g2, (1, 1, CH + 1))[:, :, :CH * (2 * CH - 1)]
    blk = flat.reshape(nh, NDIST, CH, 2 * CH - 1)[..., :CH]
    neg = jnp.full((nh, NQB - 1, CH, CH), NEG, F32)
    asc = jnp.concatenate([neg, blk, neg], axis=1)
    return jnp.concatenate([asc[:, 1:], asc[:, :-1]], axis=-1)


def assemble_bias(tab):
    rows = [jnp.concatenate([tab[NQB + a - 2 * b - KLO] for b in range(NKB // 2)], axis=1) for a in range(NQB)]
    return jnp.concatenate(rows, axis=0)


def bias_table_bwd_layout(dtab):
    nh = dtab.shape[0]
    dasc = (jnp.pad(dtab[..., :CH], ((0, 0), (1, 0), (0, 0), (0, 0)))
            + jnp.pad(dtab[..., CH:], ((0, 0), (0, 1), (0, 0), (0, 0))))
    dblk = dasc[:, NQB - 1:NQB - 1 + NDIST]
    dr = jnp.pad(dblk, ((0, 0), (0, 0), (0, 0), (0, CH - 1)))
    flat = jnp.pad(dr.reshape(nh, NDIST, CH * (2 * CH - 1)), ((0, 0), (0, 0), (0, 3 * CH)))
    return flat.reshape(nh, NDIST, CH + 1, 2 * CH).transpose(0, 2, 1, 3).reshape(nh, CH + 1, NDIST * 2 * CH)


def _fold_matrix_np():
    f = np.zeros((NDIST * 2 * CH, 384), np.float32)
    for k in range(NDIST):
        s = k
        for xx in range(2 * CH):
            if xx == CH:
                continue
            m = CH - 1 - xx if xx < CH else 3 * CH - 1 - xx
            f[s * 2 * CH + xx, min(CH * k + m, RELSZ - 1)] = 1.0
    return f


def relbias_reduce(dlay):
    nh, rows, cols = dlay.shape
    rpad = (-rows) % 8
    dlay = jnp.pad(dlay, ((0, 0), (0, rpad), (0, 0)))
    fold = jnp.asarray(_fold_matrix_np())

    def body(d_ref, f_ref, o_ref):
        cs = jnp.sum(d_ref[0], axis=0, keepdims=True)
        o_ref[0] = _mmh(jnp.broadcast_to(cs, (8, cols)), f_ref[...])

    out = pl.pallas_call(
        body, name="relbias_reduce", grid=(nh,),
        in_specs=[pl.BlockSpec((1, rows + rpad, cols), lambda h: (h, 0, 0)), _const((cols, 384))],
        out_specs=pl.BlockSpec((1, 8, 384), lambda h: (h, 0, 0)),
        out_shape=jax.ShapeDtypeStruct((nh, 8, 384), F32),
        compiler_params=_cparams(),
    )(dlay, fold)
    return out[:, 0, :RELSZ]


def attn_fwd(proj, bias):
    t = proj.shape[0]
    nt = t // TQ
    cb = C_QKVB // 128

    def body(q_ref, kp_ref, kc_ref, vp_ref, vc_ref, b_ref, o_ref):
        i = pl.program_id(1)
        firstf = jnp.where(i == 0, 1.0, 0.0)
        o_ref[...] = attn_pair(q_ref[...], kp_ref[...], kc_ref[...], vp_ref[...], vc_ref[...],
                               b_ref[...], firstf).astype(BF16)

    def blk(off, prev):
        if prev:
            return pl.BlockSpec((TQ, 128), lambda p, i: (jnp.maximum(i - 1, 0), cb + off + p))
        return pl.BlockSpec((TQ, 128), lambda p, i: (i, cb + off + p))

    return pl.pallas_call(
        body, name="attn_fwd", grid=(BH // 2, nt),
        in_specs=[blk(0, False), blk(8, True), blk(8, False), blk(16, True), blk(16, False),
                  pl.BlockSpec((2, NPAIR, CH, 2 * CH), lambda p, i: (p, 0, 0, 0))],
        out_specs=pl.BlockSpec((TQ, 128), lambda p, i: (i, p)),
        out_shape=jax.ShapeDtypeStruct((t, D), BF16),
        compiler_params=_cparams(2),
    )(proj, proj, proj, proj, proj, bias)


def attn_bwd(proj, bias, do):
    t = proj.shape[0]
    nt = t // TQ
    cb = C_QKVB // 128

    def body(q_ref, kp_ref, kc_ref, vp_ref, vc_ref, b_ref, do_ref,
             dq_ref, dk_ref, dv_ref, db_ref, ck, cv):
        i = pl.program_id(1)

        @pl.when(i == 0)
        def _():
            ck[...] = jnp.zeros_like(ck)
            cv[...] = jnp.zeros_like(cv)
            db_ref[...] = jnp.zeros_like(db_ref)

        @pl.when(i < nt)
        def _():
            firstf = jnp.where(i == 0, 1.0, 0.0)
            _, vjp = jax.vjp(lambda q, kp, kc, vp, vc, b: attn_pair(q, kp, kc, vp, vc, b, firstf),
                             q_ref[...].astype(F32), kp_ref[...].astype(F32), kc_ref[...].astype(F32),
                             vp_ref[...].astype(F32), vc_ref[...].astype(F32), b_ref[...])
            dq, dkp, dkc, dvp, dvc, db = vjp(do_ref[...])
            dq_ref[...] = dq.astype(BF16)
            dk_ref[...] = (ck[...] + dkp).astype(BF16)
            dv_ref[...] = (cv[...] + dvp).astype(BF16)
            ck[...] = dkc
            cv[...] = dvc
            db_ref[...] += db

        @pl.when(i == nt)
        def _():
            dk_ref[...] = ck[...].astype(BF16)
            dv_ref[...] = cv[...].astype(BF16)

    def blk(off, prev):
        if prev:
            return pl.BlockSpec((TQ, 128), lambda p, i: (jnp.clip(i - 1, 0, nt - 1), cb + off + p))
        return pl.BlockSpec((TQ, 128), lambda p, i: (jnp.minimum(i, nt - 1), cb + off + p))

    own = pl.BlockSpec((TQ, 128), lambda p, i: (jnp.minimum(i, nt - 1), p))
    lag = pl.BlockSpec((TQ, 128), lambda p, i: (jnp.maximum(i - 1, 0), p))
    return pl.pallas_call(
        body, name="attn_bwd", grid=(BH // 2, nt + 1),
        in_specs=[blk(0, False), blk(8, True), blk(8, False), blk(16, True), blk(16, False),
                  pl.BlockSpec((2, NPAIR, CH, 2 * CH), lambda p, i: (p, 0, 0, 0)), own],
        out_specs=[own, lag, lag, pl.BlockSpec((2, NPAIR, CH, 2 * CH), lambda p, i: (p, 0, 0, 0))],
        out_shape=[jax.ShapeDtypeStruct((t, D), BF16)] * 3 + [jax.ShapeDtypeStruct((BH, NPAIR, CH, 2 * CH), F32)],
        scratch_shapes=[pltpu.VMEM((TQ, 128), F32), pltpu.VMEM((TQ, 128), F32)],
        compiler_params=_cparams(2),
    )(proj, proj, proj, proj, proj, bias, do)


MERGE_TM = 256


def merge_fwd(x, oa, ob, proj, vecs, wa, wb, wo):
    t = x.shape[0]
    tm = MERGE_TM
    names = ("bga", "bgb", "gate_t", "g1", "b1", "scale_f", "shift_f")

    def body(x_ref, oa_ref, ob_ref, gra_ref, grb_ref, *rest):
        vrefs = rest[:7]
        wa_ref, wb_ref, wo_ref, y_ref, h_ref = rest[7:]
        vv = [r[...] for r in vrefs]
        zero = jnp.zeros((tm, D), F32)
        y1, _ = merge_fn(x_ref[...], oa_ref[...], ob_ref[...], gra_ref[...], grb_ref[...], zero, zero, zero,
                         *vv, wa_ref[...], wb_ref[...], wo_ref[...])
        y_ref[...] = y1
        h_ref[...] = (y1 * (1.0 + vv[5]) + vv[6]).astype(BF16)

    return pl.pallas_call(
        body, name="merge_fwd", grid=(t // tm,),
        in_specs=[_rows(tm, D), _rows(tm, D), _rows(tm, D), _rows(tm, D, C_GATE // D), _rows(tm, D, C_GATE // D + 1)]
        + [_const((1, D))] * 7 + [_const((D, D))] * 3,
        out_specs=[_rows(tm, D), _rows(tm, D)],
        out_shape=[jax.ShapeDtypeStruct((t, D), F32), jax.ShapeDtypeStruct((t, D), BF16)],
        compiler_params=_cparams(),
    )(x, oa, ob, proj, proj, *[vecs[n] for n in names], wa, wb, wo)


def merge_bwd(x, oa, ob, proj, vecs, wa, wb, wo, dy1):
    t = x.shape[0]
    tm = MERGE_TM
    names = ("bga", "bgb", "gate_t", "g1", "b1", "scale_f", "shift_f")

    def body(x_ref, oa_ref, ob_ref, gra_ref, grb_ref, *rest):
        vrefs = rest[:7]
        wa_ref, wb_ref, wo_ref, dy_ref = rest[7:11]
        (dx_ref, doa_ref, dob_ref, dga_ref, dgb_ref, mg_ref, dmix_ref, dpa_ref, dpb_ref,
         dbga_ref, dbgb_ref, dgt_ref, dg1_ref, db1_ref) = rest[11:]
        i = pl.program_id(0)
        vv = [r[...] for r in vrefs]
        zero = jnp.zeros((tm, D), F32)

        def f(x_, oa_, ob_, gra_, grb_, ppa, ppb, pmix, bga, bgb, gate_t, g1, b1):
            return merge_fn(x_, oa_, ob_, gra_, grb_, ppa, ppb, pmix, bga, bgb, gate_t, g1, b1, vv[5], vv[6],
                            wa_ref[...], wb_ref[...], wo_ref[...])

        _, vjp, merged = jax.vjp(f, x_ref[...], oa_ref[...].astype(F32), ob_ref[...].astype(F32),
                                 gra_ref[...], grb_ref[...], zero, zero, zero, *vv[:5], has_aux=True)
        dx, doa, dob, dga, dgb, dpa, dpb, dmix, dbga, dbgb, dgt, dg1, db1 = vjp(dy_ref[...])
        dx_ref[...] = dx
        doa_ref[...] = doa
        dob_ref[...] = dob
        dga_ref[...] = dga.astype(BF16)
        dgb_ref[...] = dgb.astype(BF16)
        mg_ref[...] = merged.astype(BF16)
        dmix_ref[...] = dmix.astype(BF16)
        dpa_ref[...] = dpa.astype(BF16)
        dpb_ref[...] = dpb.astype(BF16)
        accs = (dbga_ref, dbgb_ref, dgt_ref, dg1_ref, db1_ref)

        @pl.when(i == 0)
        def _():
            for a in accs:
                a[...] = jnp.zeros_like(a)

        for a, val in zip(accs, (dbga, dbgb, dgt, dg1, db1)):
            a[...] += val

    return pl.pallas_call(
        body, name="merge_bwd", grid=(t // tm,),
        in_specs=[_rows(tm, D), _rows(tm, D), _rows(tm, D), _rows(tm, D, C_GATE // D), _rows(tm, D, C_GATE // D + 1)]
        + [_const((1, D))] * 7 + [_const((D, D))] * 3 + [_rows(tm, D)],
        out_specs=[_rows(tm, D)] * 9 + [_const((1, D))] * 5,
        out_shape=[jax.ShapeDtypeStruct((t, D), F32)] * 3 + [jax.ShapeDtypeStruct((t, D), BF16)] * 6
        + [jax.ShapeDtypeStruct((1, D), F32)] * 5,
        compiler_params=_cparams(),
    )(x, oa, ob, proj, proj, *[vecs[n] for n in names], wa, wb, wo, dy1)


FFN_TM = 128


def ffn_act_fwd(up, conv_w, bconv):
    t, wdt = up.shape
    tm = FFN_TM

    def body(prev_ref, cur_ref, cw_ref, bc_ref, a_ref):
        i = pl.program_id(0)
        flag = jnp.where(i > 0, 1.0, 0.0)
        ext = jnp.concatenate([prev_ref[...] * flag, cur_ref[...]], axis=0)
        rows = tuple(cw_ref[j:j + 1, :] for j in range(3))
        a_ref[...] = ffn_act_fn(ext, rows, bc_ref[...]).astype(BF16)

    return pl.pallas_call(
        body, name="ffn_act_fwd", grid=(t // tm,),
        in_specs=_halo_specs(tm, wdt, 0, lambda i: i) + [_const((3, wdt)), _const((1, wdt))],
        out_specs=_rows(tm, DFF),
        out_shape=jax.ShapeDtypeStruct((t, DFF), BF16),
        compiler_params=_cparams(),
    )(up, up, conv_w, bconv)


def ffn_act_bwd(up, conv_w, bconv, da):
    t, wdt = up.shape
    tm = FFN_TM
    nt = t // tm
    rev = lambda i: nt - 1 - i

    def body(prev_ref, cur_ref, cw_ref, bc_ref, da_ref, dup_ref, dcw_ref, dbc_ref, carry):
        i = pl.program_id(0)
        flag = jnp.where(i < nt - 1, 1.0, 0.0)
        ext = jnp.concatenate([prev_ref[...] * flag, cur_ref[...]], axis=0)
        rows = tuple(cw_ref[j:j + 1, :] for j in range(3))
        _, vjp = jax.vjp(ffn_act_fn, ext, rows, bc_ref[...])
        dext, drows, dbc = vjp(da_ref[...])

        @pl.when(i == 0)
        def _():
            carry[...] = jnp.zeros_like(carry)
            dcw_ref[...] = jnp.zeros_like(dcw_ref)
            dbc_ref[...] = jnp.zeros_like(dbc_ref)

        dcur = dext[HALO:]
        dup_ref[...] = jnp.concatenate([dcur[:tm - HALO], dcur[tm - HALO:] + carry[...]], axis=0).astype(BF16)
        carry[...] = dext[:HALO]
        dcw = None
        for j in range(3):
            tj = _onehot_rows(3, j) * drows[j]
            dcw = tj if dcw is None else dcw + tj
        dcw_ref[...] += dcw
        dbc_ref[...] += dbc

    return pl.pallas_call(
        body, name="ffn_act_bwd", grid=(nt,),
        in_specs=_halo_specs(tm, wdt, 0, rev) + [_const((3, wdt)), _const((1, wdt)), _rows(tm, DFF, 0, rev)],
        out_specs=[_rows(tm, wdt, 0, rev), _const((3, wdt)), _const((1, wdt))],
        out_shape=[jax.ShapeDtypeStruct((t, wdt), BF16), jax.ShapeDtypeStruct((3, wdt), F32),
                   jax.ShapeDtypeStruct((1, wdt), F32)],
        scratch_shapes=[pltpu.VMEM((HALO, wdt), F32)],
        compiler_params=_cparams(),
    )(up, up, conv_w, bconv, da)


HEAD_TM = 256


def head_fwd_bwd(a, y1, tgt, gate_f, g2, b2, wd):
    t = a.shape[0]
    tm = HEAD_TM

    def body(a_ref, y_ref, t_ref, gf_ref, g2_ref, b2_ref, wd_ref,
             da_ref, dy_ref, dffn_ref, dgf_ref, dg2_ref, db2_ref, loss_ref):
        i = pl.program_id(0)
        zero = jnp.zeros((tm, D), F32)

        def f(a_, y_, pf, gf, g2_, b2_):
            return head_fn(a_, y_, pf, gf, g2_, b2_, t_ref[...], wd_ref[...])

        loss, vjp = jax.vjp(f, a_ref[...].astype(F32), y_ref[...], zero, gf_ref[...], g2_ref[...], b2_ref[...])
        da, dy, dffn, dgf, dg2, db2 = vjp(jnp.ones((), F32))
        da_ref[...] = da
        dy_ref[...] = dy
        dffn_ref[...] = dffn.astype(BF16)
        accs = (dgf_ref, dg2_ref, db2_ref, loss_ref)

        @pl.when(i == 0)
        def _():
            for r in accs:
                r[...] = jnp.zeros_like(r)

        dgf_ref[...] += dgf
        dg2_ref[...] += dg2
        db2_ref[...] += db2
        loss_ref[...] += loss * jnp.ones((1, 128), F32)

    return pl.pallas_call(
        body, name="head_fwd_bwd", grid=(t // tm,),
        in_specs=[_rows(tm, DFF), _rows(tm, D), _rows(tm, D), _const((1, D)), _const((1, D)), _const((1, D)),
                  _const((DFF, D))],
        out_specs=[_rows(tm, DFF), _rows(tm, D), _rows(tm, D), _const((1, D)), _const((1, D)), _const((1, D)),
                   _const((1, 128))],
        out_shape=[jax.ShapeDtypeStruct((t, DFF), F32), jax.ShapeDtypeStruct((t, D), F32),
                   jax.ShapeDtypeStruct((t, D), BF16)] + [jax.ShapeDtypeStruct((1, D), F32)] * 3
        + [jax.ShapeDtypeStruct((1, 128), F32)],
        compiler_params=_cparams(),
    )(a, y1, tgt, gate_f, g2, b2, wd)


def ada_fwd(c_all, w_sh, b_sh):
    def body(c_ref, w_ref, b_ref, o_ref):
        o_ref[...] = _mmh(_silu(c_ref[...]), w_ref[...]) + b_ref[...]

    n = w_sh.shape[1]
    return pl.pallas_call(
        body, name="ada_fwd", out_shape=jax.ShapeDtypeStruct((NDEV, n), F32),
        in_specs=[pl.BlockSpec(memory_space=pltpu.VMEM)] * 3,
        out_specs=pl.BlockSpec(memory_space=pltpu.VMEM),
        compiler_params=pltpu.CompilerParams(vmem_limit_bytes=VMEM_LIMIT),
    )(c_all, w_sh, b_sh)


def ada_wgrad(c_all_t, dmod_sh):
    def body(c_ref, d_ref, o_ref):
        o_ref[...] = _mmh(_silu(c_ref[...]), d_ref[...])

    return pl.pallas_call(
        body, name="ada_wgrad", out_shape=jax.ShapeDtypeStruct((c_all_t.shape[0], dmod_sh.shape[1]), F32),
        in_specs=[pl.BlockSpec(memory_space=pltpu.VMEM)] * 2,
        out_specs=pl.BlockSpec(memory_space=pltpu.VMEM),
        compiler_params=pltpu.CompilerParams(vmem_limit_bytes=VMEM_LIMIT),
    )(c_all_t, dmod_sh)


def adamw(gparts, w, m, v, name):
    p, r, c = gparts.shape
    tr = r if r <= 256 else _pick(r, (256, 128, 64, 32, 16, 8))
    c1 = 1.0 - B1 ** STEP
    c2 = 1.0 - B2 ** STEP

    def body(g_ref, w_ref, m_ref, v_ref, go_ref, d_ref, mo_ref, vo_ref):
        g = g_ref[0].astype(F32)
        for s in range(1, p):
            g = g + g_ref[s].astype(F32)
        mn = B1 * m_ref[...] + (1.0 - B1) * g
        vn = B2 * v_ref[...] + (1.0 - B2) * (g * g)
        go_ref[...] = g
        d_ref[...] = -LR * ((mn / c1) / (jnp.sqrt(vn / c2) + AEPS) + WD * w_ref[...])
        mo_ref[...] = mn
        vo_ref[...] = vn

    spec = pl.BlockSpec((tr, c), lambda i: (i, 0))
    return pl.pallas_call(
        body, name=name, grid=(r // tr,),
        in_specs=[pl.BlockSpec((p, tr, c), lambda i: (0, i, 0)), spec, spec, spec],
        out_specs=[spec] * 4,
        out_shape=[jax.ShapeDtypeStruct((r, c), F32)] * 4,
        compiler_params=_cparams(),
    )(gparts, w, m, v)


def _me():
    x, y, c = lax.axis_index("x"), lax.axis_index("y"), lax.axis_index("c")
    return x, y, c, 4 * x + 2 * y + c


def _peer(x, y, c, d):
    px = 1 - x if (d >> 2) & 1 else x
    py = 1 - y if (d >> 1) & 1 else y
    pc = 1 - c if d & 1 else c
    return (px, py, pc), 4 * px + 2 * py + pc


def _exchange(arrs, name, scatter):
    n = len(arrs)

    def body(*refs):
        ins, outs = refs[:n], refs[n:2 * n]
        send, recv, lsem = refs[2 * n:]
        x, y, c, me = _me()
        remote, local = [], []
        for k in range(n):
            src = ins[k].at[me] if scatter else ins[k]
            cp = pltpu.make_async_copy(src, outs[k].at[me], lsem.at[k])
            cp.start()
            local.append(cp)
            for d in range(1, NDEV):
                dev, pid = _peer(x, y, c, d)
                src = ins[k].at[pid] if scatter else ins[k]
                cp = pltpu.make_async_remote_copy(src_ref=src, dst_ref=outs[k].at[me],
                                                  send_sem=send.at[k, d - 1], recv_sem=recv.at[k, d - 1],
                                                  device_id=dev, device_id_type=pl.DeviceIdType.MESH)
                cp.start()
                remote.append(cp)
        for cp in remote:
            cp.wait()
        for cp in local:
            cp.wait()

    shapes = [a.shape if scatter else (NDEV,) + a.shape for a in arrs]
    return pl.pallas_call(
        body, name=name,
        in_specs=[pl.BlockSpec(memory_space=pl.ANY)] * n,
        out_specs=[pl.BlockSpec(memory_space=pl.ANY)] * n,
        out_shape=[jax.ShapeDtypeStruct(s, a.dtype) for s, a in zip(shapes, arrs)],
        scratch_shapes=[pltpu.SemaphoreType.DMA((n, NDEV - 1)), pltpu.SemaphoreType.DMA((n, NDEV - 1)),
                        pltpu.SemaphoreType.DMA((n,))],
        compiler_params=pltpu.CompilerParams(has_side_effects=True),
    )(*arrs)


def all_gather(arrs, name):
    return _exchange(arrs, name, False)


def all_to_all(arrs, name):
    return _exchange(arrs, name, True)


_HBM = pl.BlockSpec(memory_space=pltpu.HBM)
_SEM = pl.BlockSpec(memory_space=pltpu.SEMAPHORE)
_EFFECT = pltpu.SideEffectType.DATAFLOW_SIDE_EFFECTING
NPEER = NDEV - 1


def exchange_start(arrs, name, scatter):
    n = len(arrs)
    lands = [lax.empty(a.shape if scatter else (NDEV,) + a.shape, a.dtype) for a in arrs]

    def body(*refs):
        ins, lrefs = refs[:n], refs[n:2 * n]
        send, recv, token = refs[2 * n], refs[2 * n + 1], refs[-1]
        x, y, c, me = _me()
        for k in range(n):
            for d in range(1, NDEV):
                dev, pid = _peer(x, y, c, d)
                src = ins[k].at[pid] if scatter else ins[k]
                pltpu.make_async_remote_copy(src_ref=src, dst_ref=lrefs[k].at[me],
                                             send_sem=send.at[k * NPEER + d - 1], recv_sem=recv.at[k * NPEER + d - 1],
                                             device_id=dev, device_id_type=pl.DeviceIdType.MESH).start()
        token[...] = jnp.zeros_like(token)

    thru = [pltpu.HBM(a.shape, a.dtype) for a in list(arrs) + lands]
    outs = pl.pallas_call(
        body, name=name,
        out_shape=(pltpu.SemaphoreType.DMA((n * NPEER,)), pltpu.SemaphoreType.DMA((n * NPEER,)), *thru,
                   jax.ShapeDtypeStruct((8, 128), F32)),
        in_specs=[_HBM] * (2 * n),
        out_specs=(_SEM, _SEM, *([_HBM] * (2 * n)), pl.BlockSpec(memory_space=pltpu.VMEM)),
        input_output_aliases={i: 2 + i for i in range(2 * n)},
        compiler_params=pltpu.CompilerParams(has_side_effects=_EFFECT),
    )(*[pltpu.with_memory_space_constraint(a, pltpu.HBM) for a in list(arrs) + lands])
    handle = dict(send=outs[0], recv=outs[1], src=list(outs[2:2 + n]), land=list(outs[2 + n:2 + 2 * n]),
                  scatter=scatter)
    return handle, outs[-1][0, 0]


def exchange_wait(handle, after, name):
    n = len(handle["src"])
    scatter = handle["scatter"]

    def body(*refs):
        ins, lrefs = refs[:n], refs[n:2 * n]
        send, recv = refs[2 * n], refs[2 * n + 1]
        x, y, c, _ = _me()
        for k in range(n):
            for d in range(1, NDEV):
                dev, _ = _peer(x, y, c, d)
                src = ins[k].at[0] if scatter else ins[k]
                cp = pltpu.make_async_remote_copy(src_ref=src, dst_ref=lrefs[k].at[0],
                                                  send_sem=send.at[k * NPEER + d - 1],
                                                  recv_sem=recv.at[k * NPEER + d - 1],
                                                  device_id=dev, device_id_type=pl.DeviceIdType.MESH)
                cp.wait_send()
                cp.wait_recv()

    arrs = handle["src"] + handle["land"]
    outs = pl.pallas_call(
        body, name=name,
        out_shape=tuple(pltpu.HBM(a.shape, a.dtype) for a in arrs),
        in_specs=[_HBM] * (2 * n) + [_SEM, _SEM, pl.BlockSpec(memory_space=pl.ANY)],
        out_specs=tuple([_HBM] * (2 * n)),
        input_output_aliases={i: i for i in range(2 * n)},
        compiler_params=pltpu.CompilerParams(has_side_effects=_EFFECT),
    )(*arrs, handle["send"], handle["recv"], after)
    me = 4 * lax.axis_index("x") + 2 * lax.axis_index("y") + lax.axis_index("c")
    landed = []
    for own, land in zip(outs[:n], outs[n:]):
        mine = lax.dynamic_index_in_dim(own, me, 0, keepdims=True) if scatter else own[None]
        landed.append(lax.dynamic_update_slice_in_dim(land, mine, me, 0))
    return landed


def _to_cat(w_in):
    k = w_in.shape[0]
    ba = jnp.zeros((k, NCAT - C_BA), w_in.dtype)
    ba = ba.at[:, 0:8].set(w_in[:, 4096:4104]).at[:, 128:136].set(w_in[:, 4104:4112])
    return jnp.concatenate([w_in[:, 0:3072], w_in[:, 3072:4096], w_in[:, 4112:7184], w_in[:, 7184:9232], ba], axis=1)


def _from_cat(dw):
    return jnp.concatenate([dw[:, 0:3072], dw[:, 3072:4096], dw[:, C_BA:C_BA + 8], dw[:, C_BA + 128:C_BA + 136],
                            dw[:, 4096:7168], dw[:, 7168:9216]], axis=1)


def _pad128(v):
    return jnp.pad(v, ((0, 0), (0, 128 - v.shape[1])))


def local_step(x, tgt, mod, wts, small, late_weights=None, on_grads=None):
    if on_grads is None:
        on_grads = lambda group, gd: jnp.zeros((), F32)
    t = x.shape[0]
    nc = t // CH
    shift_t, scale_t, gate_t, shift_f, scale_f, gate_f = mod
    wcat = _to_cat(wts["w_in"])
    a_log = _pad128(small["a_log"])
    dtb = _pad128(small["dt_bias"])
    vecs = dict(bga=small["b_gate"][:, :D], bgb=small["b_gate"][:, D:], gate_t=gate_t, g1=small["ln1_g"],
                b1=small["ln1_b"], scale_f=scale_f, shift_f=shift_f)

    h1 = modulate(x, scale_t, shift_t, "modulate_t")
    proj = matmul(h1, wcat, F32, "in_proj")
    q, k, v, gcs, beta = prep_fwd(proj, small["conv_a"], a_log, dtb)

    def col4(a):
        return a[:, :AH].reshape(nc, CH, AH).transpose(2, 0, 1)[..., None]

    gcol = col4(gcs)
    grow = gcol.reshape(AH, nc, 1, CH)
    bcol = col4(beta)
    u, w, qg, kd, qk, eg = c1_fwd(q, k, v, gcol, grow, bcol)
    oa, sall = c2_fwd(u, w, qg, kd, qk, eg, proj, small["norm_a"])
    bias = bias_table(small["rel_bias"])
    ob = attn_fwd(proj, bias)
    if late_weights is not None:
        wts = {**wts, **late_weights(ob)}
    y1, h2 = merge_fwd(x, oa, ob, proj, vecs, wts["w_a"], wts["w_b"], wts["w_o"])
    up = matmul(h2, wts["w_up"], F32, "up_proj")
    a = ffn_act_fwd(up, small["conv_ffn"], small["b_conv_ffn"])

    da, dy1_res, dffn, dgate_f, dg2, db2, loss = head_fwd_bwd(a, y1, tgt, gate_f, small["ln2_g"], small["ln2_b"],
                                                            wts["w_down"])
    g_w_down = matmul(a, dffn, F32, "wgrad_down", ta=True)
    dup, g_conv_ffn, g_bconv = ffn_act_bwd(up, small["conv_ffn"], small["b_conv_ffn"], da)
    dh2 = matmul(dup, wts["w_up"], F32, "dgrad_up", tb=True)
    g_w_up = matmul(h2, dup, F32, "wgrad_up", ta=True)
    tok = on_grads("ffn", dict(w_up=g_w_up, w_down=g_w_down))
    dy1, dscale_f, dshift_f = modulate_bwd(dh2, y1, dy1_res, scale_f + tok, "modulate_f_bwd")
    (dx_res, doa, dob, dga, dgb, merged, dmix, dpa, dpb,
     dbga, dbgb, dgate_t, dg1, db1) = merge_bwd(x, oa, ob, proj, vecs, wts["w_a"], wts["w_b"], wts["w_o"], dy1)
    g_w_o = matmul(merged, dmix, F32, "wgrad_o", ta=True)
    g_w_a = matmul(oa, dpa, F32, "wgrad_a", ta=True)
    g_w_b = matmul(ob, dpb, F32, "wgrad_b", ta=True)
    tok = on_grads("mix", dict(w_o=g_w_o, w_a=g_w_a, w_b=g_w_b))
    dqb, dkb, dvb, dbias = attn_bwd(proj, bias, dob)
    g_rel = relbias_reduce(bias_table_bwd_layout(dbias))
    du, dw, dqg, dkd, dqk, deg, dz, g_norm = c2_bwd(u, w, qg, kd, qk, eg, proj, small["norm_a"] + tok, sall, doa)
    dq, dk, dv, dgc, dgr, dbc = c1_bwd(q, k, v, gcol, grow, bcol, du, dw, dqg, dkd, dqk, deg)

    def from4(a4):
        return _pad128(a4[..., 0].transpose(1, 2, 0).reshape(t, AH))

    dgcs = from4(dgc) + from4(dgr.reshape(AH, nc, CH, 1))
    dbeta = from4(dbc)
    dpre, dbb, daa, g_conv_a, g_alog, g_dtb = prep_bwd(proj, small["conv_a"], a_log, dtb, dq, dk, dv, dgcs, dbeta)
    dba = jnp.concatenate([dbb, daa, jnp.zeros((t, NCAT - C_BA - 256), BF16)], axis=1)
    dproj = jnp.concatenate([dpre, dz, dqb, dkb, dvb, dga, dgb, dba], axis=1)
    g_wcat = matmul(h1, dproj, F32, "wgrad_in", ta=True)
    tok = on_grads("in", dict(w_in=_from_cat(g_wcat)))
    dh1 = matmul(dproj, wcat + tok.astype(BF16), F32, "dgrad_in", tb=True)
    grad_x, dscale_t, dshift_t = modulate_bwd(dh1, x, dx_res, scale_t + tok, "modulate_t_bwd")

    dmod = (dshift_t, dscale_t, dgate_t, dshift_f, dscale_f, dgate_f)
    grads = dict(w_in=_from_cat(g_wcat), w_up=g_w_up, w_down=g_w_down, w_a=g_w_a, w_b=g_w_b, w_o=g_w_o,
                 conv_a=g_conv_a, rel_bias=g_rel, conv_ffn=g_conv_ffn,
                 b_gate=jnp.concatenate([dbga, dbgb], axis=1), a_log=g_alog[:, :AH], dt_bias=g_dtb[:, :AH],
                 norm_a=g_norm, ln1_g=dg1, ln1_b=db1, b_conv_ffn=g_bconv, ln2_g=dg2, ln2_b=db2)
    return loss[0, 0], grad_x, dmod, grads


_REP = {}
_off = 0
for _n, _wd, _pw in (("b_ada", 6144, 6144), ("b_gate", 2048, 2048), ("a_log", 8, 128), ("dt_bias", 8, 128),
                     ("norm_a", 128, 128), ("ln1_g", 1024, 1024), ("ln1_b", 1024, 1024),
                     ("b_conv_ffn", 5632, 5632), ("ln2_g", 1024, 1024), ("ln2_b", 1024, 1024), ("loss", 1, 128)):
    _REP[_n] = (_off, _wd, _pw)
    _off += _pw
REP_LEN = _off
REP_NAMES = [n for n in _REP if n != "loss"]
_SH = (("conv_a", (4, 384)), ("rel_bias", (16, 40)), ("conv_ffn", (3, 704)))
SH_LEN = 4352


def _pack_rep(vals):
    parts = []
    for n, (_, wd, pw) in _REP.items():
        a = vals.get(n)
        a = jnp.zeros((1, pw), F32) if a is None else jnp.pad(a.reshape(1, wd), ((0, 0), (0, pw - wd)))
        parts.append(a)
    return jnp.concatenate(parts, axis=1)


def _unpack_rep(vec, name):
    o, wd, _ = _REP[name]
    return vec[:, o:o + wd]


def _pack_sh(vals):
    parts = [vals[n].reshape(vals[n].shape[:-2] + (-1,)) for n, _ in _SH]
    a = jnp.concatenate(parts, axis=-1)
    return jnp.pad(a, [(0, 0)] * (a.ndim - 1) + [(0, SH_LEN - a.shape[-1])])


def _unpack_sh(vec, name):
    o = 0
    for n, shp in _SH:
        sz = shp[0] * shp[1]
        if n == name:
            return vec[0, o:o + sz].reshape(shp)
        o += sz
    raise KeyError(name)


def _col_shards(a, n):
    return a.reshape(a.shape[0], NDEV, n).transpose(1, 0, 2)


def kernel(x, c, w_ada, b_ada, w_in, b_gate, conv_a, a_log, dt_bias, norm_a, rel_bias, w_branch_a, w_branch_b, w_o, ln1_g, ln1_b, w_up, conv_ffn, b_conv_ffn, w_down, ln2_g, ln2_b, loss_target, m_w_ada, m_b_ada, m_w_in, m_b_gate, m_conv_a, m_a_log, m_dt_bias, m_norm_a, m_rel_bias, m_w_branch_a, m_w_branch_b, m_w_o, m_ln1_g, m_ln1_b, m_w_up, m_conv_ffn, m_b_conv_ffn, m_w_down, m_ln2_g, m_ln2_b, v_w_ada, v_b_ada, v_w_in, v_b_gate, v_conv_a, v_a_log, v_dt_bias, v_norm_a, v_rel_bias, v_w_branch_a, v_w_branch_b, v_w_o, v_ln1_g, v_ln1_b, v_w_up, v_conv_ffn, v_b_conv_ffn, v_w_down, v_ln2_g, v_ln2_b):
    W = dict(w_ada=w_ada, b_ada=b_ada, w_in=w_in, b_gate=b_gate, conv_a=conv_a, a_log=a_log, dt_bias=dt_bias,
             norm_a=norm_a, rel_bias=rel_bias, w_branch_a=w_branch_a, w_branch_b=w_branch_b, w_o=w_o, ln1_g=ln1_g,
             ln1_b=ln1_b, w_up=w_up, conv_ffn=conv_ffn, b_conv_ffn=b_conv_ffn, w_down=w_down, ln2_g=ln2_g,
             ln2_b=ln2_b)
    M = dict(w_ada=m_w_ada, b_ada=m_b_ada, w_in=m_w_in, b_gate=m_b_gate, conv_a=m_conv_a, a_log=m_a_log,
             dt_bias=m_dt_bias, norm_a=m_norm_a, rel_bias=m_rel_bias, w_branch_a=m_w_branch_a,
             w_branch_b=m_w_branch_b, w_o=m_w_o, ln1_g=m_ln1_g, ln1_b=m_ln1_b, w_up=m_w_up, conv_ffn=m_conv_ffn,
             b_conv_ffn=m_b_conv_ffn, w_down=m_w_down, ln2_g=m_ln2_g, ln2_b=m_ln2_b)
    V = dict(w_ada=v_w_ada, b_ada=v_b_ada, w_in=v_w_in, b_gate=v_b_gate, conv_a=v_conv_a, a_log=v_a_log,
             dt_bias=v_dt_bias, norm_a=v_norm_a, rel_bias=v_rel_bias, w_branch_a=v_w_branch_a,
             w_branch_b=v_w_branch_b, w_o=v_w_o, ln1_g=v_ln1_g, ln1_b=v_ln1_b, w_up=v_w_up, conv_ffn=v_conv_ffn,
             b_conv_ffn=v_b_conv_ffn, w_down=v_w_down, ln2_g=v_ln2_g, ln2_b=v_ln2_b)
    W, M, V = ({n: a[0] for n, a in dct.items()} for dct in (W, M, V))
    me = 4 * lax.axis_index("x") + 2 * lax.axis_index("y") + lax.axis_index("c")
    big = ("w_in", "w_up", "w_down", "w_branch_a", "w_branch_b", "w_o")

    (g_in,) = all_gather([W["w_in"].astype(BF16)], "gather_w_in")
    wts = dict(w_in=g_in.transpose(1, 0, 2).reshape(D, -1))
    late, late_tok = exchange_start([W[n].astype(BF16) for n in big[1:]], "gather_late_start", False)

    def late_weights(after):
        g_up, g_down, g_a, g_b, g_o = exchange_wait(late, after, "gather_late_wait")
        return dict(w_up=g_up.transpose(1, 0, 2).reshape(D, -1), w_down=g_down.reshape(DFF, D),
                    w_a=g_a.reshape(D, D), w_b=g_b.reshape(D, D), w_o=g_o.reshape(D, D))

    c_all, sh_all = all_gather([c, _pack_sh({n: W[n] for n, _ in _SH})[None]], "gather_small")
    c_all = c_all.reshape(NDEV, D)
    sh_all = sh_all.reshape(NDEV, SH_LEN)

    def full_small(name, shp):
        o = 0
        for n, s in _SH:
            if n == name:
                break
            o += s[0] * s[1]
        sz = shp[0] * shp[1]
        return sh_all[:, o:o + sz].reshape(NDEV, shp[0], shp[1]).transpose(1, 0, 2).reshape(shp[0], NDEV * shp[1])

    small = dict(conv_a=full_small("conv_a", (4, 384)), rel_bias=full_small("rel_bias", (16, 40)),
                 conv_ffn=full_small("conv_ffn", (3, 704)),
                 b_gate=W["b_gate"][None], a_log=W["a_log"][None], dt_bias=W["dt_bias"][None],
                 norm_a=W["norm_a"][None], ln1_g=W["ln1_g"][None], ln1_b=W["ln1_b"][None],
                 b_conv_ffn=W["b_conv_ffn"][None], ln2_g=W["ln2_g"][None], ln2_b=W["ln2_b"][None])

    nsh = w_ada.shape[2]
    b_sh = lax.dynamic_slice(W["b_ada"][None], (0, me * nsh), (1, nsh))
    mod_sh = ada_fwd(c_all, W["w_ada"], b_sh)
    (mod_rows,) = all_to_all([mod_sh[:, None, :]], "scatter_mod")
    mod6 = mod_rows.reshape(6, D)
    mod6 = mod6 + late_tok
    mod = tuple(mod6[i:i + 1] for i in range(6))

    pending = {}

    def on_grads(group, gd):
        if group == "ffn":
            slabs = [_col_shards(gd["w_up"], w_up.shape[2]), gd["w_down"].reshape(NDEV, -1, D)]
        elif group == "mix":
            slabs = [gd[n].reshape(NDEV, -1, D) for n in ("w_a", "w_b", "w_o")]
        else:
            slabs = [_col_shards(gd["w_in"], w_in.shape[2])]
        pending[group], tok = exchange_start([s.astype(BF16) for s in slabs], "scatter_" + group + "_start", True)
        return tok

    loss, grad_x, dmod, g = local_step(x[0], loss_target[0], mod, wts, small, late_weights, on_grads)

    rep_vals = {n: g[n] for n in REP_NAMES if n != "b_ada"}
    rep_vals["b_ada"] = jnp.concatenate(dmod, axis=1)
    rep_vals["loss"] = loss.reshape(1, 1)
    (rep_all,) = all_gather([_pack_rep(rep_vals)[None]], "gather_small_grads")
    rep_all = rep_all.reshape(NDEV, 1, REP_LEN)
    zero1 = jnp.zeros((1, 1), F32)
    rep_out = adamw(rep_all, _pack_rep({**{n: W[n][None] for n in REP_NAMES}, "loss": zero1}),
                    _pack_rep({**{n: M[n][None] for n in REP_NAMES}, "loss": zero1}),
                    _pack_rep({**{n: V[n][None] for n in REP_NAMES}, "loss": zero1}), "adamw_small")
    loss_total = _unpack_rep(rep_out[0], "loss")[0, 0]

    o_ada = _REP["b_ada"][0]
    dmod_all = rep_all[:, 0, o_ada:o_ada + 6 * D]
    dmod_sh = lax.dynamic_slice(dmod_all, (0, me * nsh), (NDEV, nsh))
    g_w_ada = ada_wgrad(c_all.T, dmod_sh)

    p_up, p_down = exchange_wait(pending["ffn"], grad_x, "scatter_ffn_wait")
    p_a, p_b, p_o = exchange_wait(pending["mix"], grad_x, "scatter_mix_wait")
    (p_in,) = exchange_wait(pending["in"], grad_x, "scatter_in_wait")
    parts = [p_in, p_up, p_down, p_a, p_b, p_o]
    sh_parts = {"conv_a": _col_shards(g["conv_a"], 384), "rel_bias": _col_shards(g["rel_bias"], 40),
                "conv_ffn": _col_shards(g["conv_ffn"], 704)}
    (sh_recv,) = all_to_all([_pack_sh(sh_parts)[:, None, :]], "scatter_small_grads")

    res = {}
    for n, p in zip(big, parts):
        res[n] = adamw(p, W[n], M[n], V[n], "adamw_" + n)
    res["w_ada"] = adamw(g_w_ada[None], W["w_ada"], M["w_ada"], V["w_ada"], "adamw_w_ada")
    sh_out = adamw(sh_recv, _pack_sh({n: W[n] for n, _ in _SH})[None], _pack_sh({n: M[n] for n, _ in _SH})[None],
                   _pack_sh({n: V[n] for n, _ in _SH})[None], "adamw_small_sharded")
    for n, _ in _SH:
        res[n] = tuple(_unpack_sh(o, n) for o in sh_out)
    for n in REP_NAMES:
        res[n] = tuple(_unpack_rep(o, n)[0] for o in rep_out)

    order = ("w_ada", "b_ada", "w_in", "b_gate", "conv_a", "a_log", "dt_bias", "norm_a", "rel_bias", "w_branch_a",
             "w_branch_b", "w_o", "ln1_g", "ln1_b", "w_up", "conv_ffn", "b_conv_ffn", "w_down", "ln2_g", "ln2_b")
    outs = [loss_total, grad_x[None]]
    for kind in range(4):
        outs += [res[n][kind][None] for n in order]
    return tuple(outs)
```

```python
import functools
import math

import numpy as np
import jax
import jax.numpy as jnp
from jax import lax
from jax.experimental import pallas as pl
from jax.experimental.pallas import tpu as pltpu

F32 = jnp.float32
BF16 = jnp.bfloat16
HI = lax.Precision.HIGHEST

D = 1024
CH = 64
AH, ADK = 8, 128
BH, BDH = 16, 64
BPREV = 8
BMAXREL = 256
RELSZ = CH + BMAXREL
DFF = 2816
ALPHA = 2.0 ** 0.25
LN_EPS, RMS_EPS, L2_EPS = 1e-5, 1e-6, 1e-6
NEG = -1e30
LR, B1, B2, AEPS, WD, STEP = 1e-3, 0.9, 0.999, 1e-8, 0.01, 10
NDEV = 8
HALO = 8
LANE = 128
TQ = 512
VMEM_LIMIT = 56 * 1024 * 1024

C_QKVA, C_Z, C_QKVB, C_GATE, C_BA, NCAT = 0, 3072, 4096, 7168, 9216, 9728


def _cparams(n_axes=1, vmem=VMEM_LIMIT):
    return pltpu.CompilerParams(dimension_semantics=("arbitrary",) * n_axes, vmem_limit_bytes=vmem)


def _dg(a, b, ca, cb):
    return lax.dot_general(a.astype(BF16), b.astype(BF16), (((ca,), (cb,)), ((), ())),
                           preferred_element_type=F32)


@jax.custom_vjp
def mm_nn(a, b):
    return _dg(a, b, 1, 0)


@jax.custom_vjp
def mm_nt(a, b):
    return _dg(a, b, 1, 1)


@jax.custom_vjp
def mm_tn(a, b):
    return _dg(a, b, 0, 0)


mm_nn.defvjp(lambda a, b: (mm_nn(a, b), (a, b)),
             lambda r, g: (mm_nt(g, r[1]).astype(r[0].dtype), mm_tn(r[0], g).astype(r[1].dtype)))
mm_nt.defvjp(lambda a, b: (mm_nt(a, b), (a, b)),
             lambda r, g: (mm_nn(g, r[1]).astype(r[0].dtype), mm_tn(g, r[0]).astype(r[1].dtype)))
mm_tn.defvjp(lambda a, b: (mm_tn(a, b), (a, b)),
             lambda r, g: (mm_nt(r[1], g).astype(r[0].dtype), mm_nn(r[0], g).astype(r[1].dtype)))


@jax.custom_vjp
def mm_w(a, w):
    return _dg(a, w, 1, 0)


mm_w.defvjp(lambda a, w: (mm_w(a, w), (a, w)),
            lambda r, g: (mm_nt(g, r[1]).astype(r[0].dtype), jnp.zeros_like(r[1])))


def _mmh(a, b):
    return lax.dot_general(a, b, (((1,), (0,)), ((), ())), precision=HI, preferred_element_type=F32)


def _bdg(a, b, ca, cb):
    return lax.dot_general(a.astype(BF16), b.astype(BF16), (((ca,), (cb,)), ((0,), (0,))),
                           preferred_element_type=F32)


@jax.custom_vjp
def bmm_nn(a, b):
    return _bdg(a, b, 2, 1)


@jax.custom_vjp
def bmm_nt(a, b):
    return _bdg(a, b, 2, 2)


@jax.custom_vjp
def bmm_tn(a, b):
    return _bdg(a, b, 1, 1)


bmm_nn.defvjp(lambda a, b: (bmm_nn(a, b), (a, b)), lambda r, g: (bmm_nt(g, r[1]), bmm_tn(r[0], g)))
bmm_nt.defvjp(lambda a, b: (bmm_nt(a, b), (a, b)), lambda r, g: (bmm_nn(g, r[1]), bmm_tn(g, r[0])))
bmm_tn.defvjp(lambda a, b: (bmm_tn(a, b), (a, b)), lambda r, g: (bmm_nt(r[1], g), bmm_nn(r[0], g)))


def _bdg3(a, b, ca, cb):
    return lax.dot_general(a, b, (((ca,), (cb,)), ((0,), (0,))), precision=HI, preferred_element_type=F32)


@jax.custom_vjp
def bmm3_nn(a, b):
    return _bdg3(a, b, 2, 1)


bmm3_nn.defvjp(lambda a, b: (bmm3_nn(a, b), (a, b)),
               lambda r, g: (_bdg3(g, r[1], 2, 2), _bdg3(r[0], g, 1, 1)))


def _sigmoid(x):
    return 0.5 * jnp.tanh(0.5 * x) + 0.5


def _silu(x):
    return x * _sigmoid(x)


def _softplus(x):
    return jnp.maximum(x, 0.0) + jnp.log(1.0 + jnp.exp(-jnp.abs(x)))


def _layernorm(r, g, b):
    mu = jnp.mean(r, axis=-1, keepdims=True)
    xc = r - mu
    var = jnp.mean(xc * xc, axis=-1, keepdims=True)
    return xc * lax.rsqrt(var + LN_EPS) * g + b


def _iota2(shape, dim):
    return lax.broadcasted_iota(jnp.int32, shape, dim)


@jax.custom_vjp
def causal_conv(ext, rows):
    k = len(rows)
    y = None
    for j in range(k):
        s = k - 1 - j
        r = pltpu.roll(ext, s, 0) if s else ext
        t = r[HALO:] * rows[j]
        y = t if y is None else y + t
    return y


def _causal_conv_fwd(ext, rows):
    return causal_conv(ext, rows), (ext, rows)


def _causal_conv_bwd(res, g):
    ext, rows = res
    n = ext.shape[0]
    k = len(rows)
    gext = jnp.concatenate([jnp.zeros((HALO, g.shape[1]), g.dtype), g], axis=0)
    dext = None
    drows = []
    for j in range(k):
        s = k - 1 - j
        up = pltpu.roll(gext, n - s, 0) if s else gext
        t = up * rows[j]
        dext = t if dext is None else dext + t
        r = pltpu.roll(ext, s, 0) if s else ext
        drows.append(jnp.sum(g * r[HALO:], axis=0, keepdims=True))
    return dext, tuple(drows)


causal_conv.defvjp(_causal_conv_fwd, _causal_conv_bwd)


def _chunk_masks(tm):
    i = _iota2((tm, tm), 0)
    j = _iota2((tm, tm), 1)
    same = (i ^ j) < CH
    lower = jnp.where(same & (j <= i), 1.0, 0.0).astype(F32)
    upper = jnp.where(same & (i <= j), 1.0, 0.0).astype(F32)
    return lower, upper


@jax.custom_vjp
def chunk_cumsum(g):
    lower, _ = _chunk_masks(g.shape[0])
    return _mmh(lower, g)


def _chunk_cumsum_bwd(_, ct):
    _, upper = _chunk_masks(ct.shape[0])
    return (_mmh(upper, ct),)


chunk_cumsum.defvjp(lambda g: (chunk_cumsum(g), None), _chunk_cumsum_bwd)


@jax.custom_vjp
def inv_unit_lower(a):
    n = a.shape[-1]
    eye = jnp.where(_iota2((1, n, n), 1) == _iota2((1, n, n), 2), 1.0, 0.0).astype(F32)
    x = eye - a
    p = _bdg3(a, a, 2, 1)
    steps = int(math.log2(n)) - 1
    for s in range(steps):
        x = x + _bdg3(x, p, 2, 1)
        if s + 1 < steps:
            p = _bdg3(p, p, 2, 1)
    return x


def _inv_fwd(a):
    t = inv_unit_lower(a)
    return t, t


def _inv_bwd(t, g):
    return (-_bdg3(_bdg3(t, g, 1, 1), t, 2, 2),)


inv_unit_lower.defvjp(_inv_fwd, _inv_bwd)


def prep_head_fn(ext, rows, scale):
    s = _silu(causal_conv(ext, rows))
    if scale is None:
        return s
    return s * (lax.rsqrt(jnp.sum(s * s, axis=-1, keepdims=True) + L2_EPS) * scale)


def prep_gate_fn(bb, aa, a_log, dtb):
    g = -jnp.exp(a_log) * _softplus(aa + dtb)
    return chunk_cumsum(g), _sigmoid(bb)


PREP_SCALES = (ADK ** -0.5, 1.0, None)


def _head_cols(a):
    lane = _iota2((1, LANE), 1)
    return jnp.concatenate([jnp.sum(jnp.where(lane == h, a, 0.0), axis=1, keepdims=True)[None]
                            for h in range(AH)], axis=0)


def _head_rows(a):
    at = a.T[:AH]
    sub = _iota2((AH, 1), 0)
    return jnp.concatenate([jnp.sum(jnp.where(sub == h, at, 0.0), axis=0, keepdims=True)[None]
                            for h in range(AH)], axis=0)


def c1_heads(q, k, v, gcs, beta):
    gcol = _head_cols(gcs)
    grow = _head_rows(gcs)
    bcol = _head_cols(beta)
    i = _iota2((1, CH, CH), 1)
    j = _iota2((1, CH, CH), 2)
    causal = j <= i
    strict = j < i
    diff = gcol - grow
    decay = jnp.where(causal, jnp.exp(jnp.where(causal, diff, 0.0)), 0.0)
    kb = k * bcol
    vb = v * bcol
    a_low = jnp.where(strict, bmm_nt(kb, k) * decay, 0.0)
    tinv = inv_unit_lower(a_low)
    egc = jnp.exp(gcol)
    u = bmm3_nn(tinv, vb)
    w = bmm3_nn(tinv, kb * egc)
    qk = jnp.where(causal, bmm_nt(q, k) * decay, 0.0)
    glast = jnp.sum(jnp.where(_iota2((1, CH, 1), 1) == CH - 1, gcol, 0.0), axis=1, keepdims=True)
    qg = q * egc
    kd = k * jnp.exp(glast - gcol)
    eg = jnp.exp(glast) * jnp.ones((1, 1, ADK), F32)
    return u, w, qk, qg, kd, eg


def c2_heads(s, u, w, qk, qg, kd, eg, z, nw):
    vn = u - bmm_nn(w, s)
    o = bmm_nn(qg, s) + bmm_nn(qk, vn)
    s2 = s * eg + bmm_tn(kd, vn)
    ms = jnp.mean(o * o, axis=-1, keepdims=True)
    og = o * lax.rsqrt(ms + RMS_EPS) * nw * _silu(z)
    return og, s2


def _attn_core_fwd(qh, k, v, bias):
    s = mm_nt(qh, k) * (BDH ** -0.5) + bias
    p = jnp.exp(s - jnp.max(s, axis=-1, keepdims=True))
    inv = 1.0 / jnp.sum(p, axis=-1, keepdims=True)
    o = mm_nn(p, v) * inv
    return o, (qh, k, v, p, inv, o)


def _attn_core_bwd(res, do):
    qh, k, v, p, inv, o = res
    p = p * inv
    dv = mm_tn(p, do)
    dp = mm_nt(do, v)
    ds = p * (dp - jnp.sum(do * o, axis=-1, keepdims=True))
    return mm_nn(ds, k) * (BDH ** -0.5), mm_tn(ds, qh) * (BDH ** -0.5), dv, ds


@jax.custom_vjp
def attn_core(qh, k, v, bias):
    return _attn_core_fwd(qh, k, v, bias)[0]


attn_core.defvjp(_attn_core_fwd, _attn_core_bwd)


def attn_sub(q, k, v, bias2, r, firstf):
    lane = _iota2((1, 2 * BDH), 1)
    col = _iota2((1, KWIN), 1) + r * SUBQ
    nokey = jnp.where(col < TQ, firstf, 0.0) * NEG
    out = None
    for hh in range(2):
        hm = jnp.where((lane >= hh * BDH) & (lane < (hh + 1) * BDH), 1.0, 0.0).astype(F32)
        o = attn_core(q * hm, k, v, assemble_bias(bias2[hh], r) + nokey) * hm
        out = o if out is None else out + o
    return out


def merge_fn(x, oa, ob, gra, grb, p_pa, p_pb, p_mix, bga, bgb, gate_t, g1, b1, scale_f, shift_f,
             wa, wb, wo):
    ga = _sigmoid(gra + bga)
    gb = _sigmoid(grb + bgb)
    pa = mm_w(oa, wa) + p_pa
    pb = mm_w(ob, wb) + p_pb
    merged = ga * pa + gb * pb
    mix = mm_w(merged, wo) + p_mix
    y1 = _layernorm(ALPHA * x + gate_t * mix, g1, b1)
    return y1, merged


def ffn_act_fn(extg, extv, rows_g, rows_v, bg, bv):
    return _silu(causal_conv(extg, rows_g) + bg) * (causal_conv(extv, rows_v) + bv)


def head_fn(a, y1, p_ffn, gate_f, g2, b2, tgt, wd):
    ffn = mm_w(a, wd) + p_ffn
    y2 = _layernorm(ALPHA * y1 + gate_f * ffn, g2, b2)
    err = y2 - tgt
    return 0.5 * jnp.sum(jnp.mean(err * err, axis=-1, keepdims=True))


def _rows(tm, width, colblk=0, order=None):
    if order is None:
        return pl.BlockSpec((tm, width), lambda i: (i, colblk))
    return pl.BlockSpec((tm, width), lambda i: (order(i), colblk))


def _const(shape):
    nd = len(shape)
    return pl.BlockSpec(shape, lambda *_: (0,) * nd)


def _pick(n, cands):
    for c in cands:
        if n % c == 0:
            return c
    raise ValueError(f"no tile for {n}")


def _onehot_rows(k, j):
    return jnp.where(_iota2((k, 1), 0) == j, 1.0, 0.0).astype(F32)


def _stack_rows(drows):
    k = len(drows)
    out = None
    for j in range(k):
        tj = _onehot_rows(k, j) * drows[j]
        out = tj if out is None else out + tj
    return out


def matmul(a, w, out_dtype, name, ta=False, tb=False):
    kdim, m = a.shape if ta else a.shape[::-1]
    n = w.shape[0] if tb else w.shape[1]
    tm = _pick(m, (1024, 512, 256, 128))
    tn = _pick(n, (1024, 512, 256, 128))
    tk = _pick(kdim, (1024, 512, 256, 128))
    nk = kdim // tk
    a_spec = (pl.BlockSpec((tk, tm), lambda i, j, k: (k, i)) if ta
              else pl.BlockSpec((tm, tk), lambda i, j, k: (i, k)))
    w_spec = (pl.BlockSpec((tn, tk), lambda i, j, k: (j, k)) if tb
              else pl.BlockSpec((tk, tn), lambda i, j, k: (k, j)))

    def body(a_ref, w_ref, o_ref, *scratch):
        p = _dg(a_ref[...], w_ref[...], 0 if ta else 1, 1 if tb else 0)
        if nk == 1:
            o_ref[...] = p.astype(out_dtype)
            return
        acc = scratch[0]
        k = pl.program_id(2)

        @pl.when(k == 0)
        def _():
            acc[...] = p

        @pl.when(k > 0)
        def _():
            acc[...] += p

        @pl.when(k == nk - 1)
        def _():
            o_ref[...] = acc[...].astype(out_dtype)

    return pl.pallas_call(
        body, name=name,
        grid=(m // tm, n // tn, nk),
        in_specs=[a_spec, w_spec],
        out_specs=pl.BlockSpec((tm, tn), lambda i, j, k: (i, j)),
        out_shape=jax.ShapeDtypeStruct((m, n), out_dtype),
        scratch_shapes=[] if nk == 1 else [pltpu.VMEM((tm, tn), F32)],
        compiler_params=_cparams(3),
    )(a, w)


def modulate(x, scale, shift, name):
    t, d = x.shape
    tm = _pick(t, (512, 256, 128))

    def body(x_ref, sc_ref, sh_ref, o_ref):
        o_ref[...] = (x_ref[...] * (1.0 + sc_ref[...]) + sh_ref[...]).astype(BF16)

    return pl.pallas_call(
        body, name=name, grid=(t // tm,),
        in_specs=[_rows(tm, d), _const((1, d)), _const((1, d))],
        out_specs=_rows(tm, d),
        out_shape=jax.ShapeDtypeStruct((t, d), BF16),
        compiler_params=_cparams(),
    )(x, scale, shift)


def modulate_bwd(dh, xin, dres, scale, name):
    t, d = dh.shape
    tm = _pick(t, (512, 256, 128))

    def body(dh_ref, x_ref, dr_ref, sc_ref, o_ref, dsc_ref, dsh_ref):
        i = pl.program_id(0)
        dh_v = dh_ref[...]
        o_ref[...] = dr_ref[...] + dh_v * (1.0 + sc_ref[...])

        @pl.when(i == 0)
        def _():
            dsc_ref[...] = jnp.zeros_like(dsc_ref)
            dsh_ref[...] = jnp.zeros_like(dsh_ref)

        dsc_ref[...] += jnp.sum(dh_v * x_ref[...], axis=0, keepdims=True)
        dsh_ref[...] += jnp.sum(dh_v, axis=0, keepdims=True)

    return pl.pallas_call(
        body, name=name, grid=(t // tm,),
        in_specs=[_rows(tm, d), _rows(tm, d), _rows(tm, d), _const((1, d))],
        out_specs=[_rows(tm, d), _const((1, d)), _const((1, d))],
        out_shape=[jax.ShapeDtypeStruct((t, d), F32), jax.ShapeDtypeStruct((1, d), F32),
                   jax.ShapeDtypeStruct((1, d), F32)],
        compiler_params=_cparams(),
    )(dh, xin, dres, scale)


PREP_TM = 128


def _halo_specs(tm, width, colblk, order):
    per = tm // HALO
    return [pl.BlockSpec((HALO, width), lambda i: (jnp.maximum(order(i) * per - 1, 0), colblk)),
            pl.BlockSpec((tm, width), lambda i: (order(i), colblk))]


def prep_fwd(proj, conv_a, a_log, dtb):
    t = proj.shape[0]
    tm = PREP_TM
    nt = t // tm
    wq = 3 * D

    def body(prev_ref, cur_ref, bb_ref, aa_ref, cw_ref, al_ref, dt_ref, q_ref, k_ref, v_ref, g_ref, b_ref):
        i = pl.program_id(0)
        flag = jnp.where(i > 0, 1.0, 0.0)
        for part, o_ref in enumerate((q_ref, k_ref, v_ref)):
            for h in range(AH):
                sl = slice(part * D + h * ADK, part * D + (h + 1) * ADK)
                ext = jnp.concatenate([prev_ref[:, sl] * flag, cur_ref[:, sl]], axis=0)
                rows = tuple(cw_ref[j:j + 1, sl] for j in range(4))
                o_ref[h] = prep_head_fn(ext, rows, PREP_SCALES[part])
        gcs, beta = prep_gate_fn(bb_ref[...], aa_ref[...], al_ref[...], dt_ref[...])
        g_ref[...] = gcs
        b_ref[...] = beta

    ident = lambda i: i
    hm = pl.BlockSpec((AH, tm, ADK), lambda i: (0, i, 0))
    return pl.pallas_call(
        body, name="prep_fwd", grid=(nt,),
        in_specs=_halo_specs(tm, wq, 0, ident) + [
            _rows(tm, 128, C_BA // 128), _rows(tm, 128, C_BA // 128 + 1),
            _const((4, wq)), _const((1, 128)), _const((1, 128))],
        out_specs=[hm, hm, hm, _rows(tm, 128), _rows(tm, 128)],
        out_shape=[jax.ShapeDtypeStruct((AH, t, ADK), F32)] * 3 + [jax.ShapeDtypeStruct((t, 128), F32)] * 2,
        compiler_params=_cparams(),
    )(proj, proj, proj, proj, conv_a, a_log, dtb)


def prep_bwd(proj, conv_a, a_log, dtb, dq, dk, dv, dgcs, dbeta):
    t = proj.shape[0]
    tm = PREP_TM
    nt = t // tm
    wq = 3 * D
    rev = lambda i: nt - 1 - i

    def body(prev_ref, cur_ref, bb_ref, aa_ref, cw_ref, al_ref, dt_ref,
             dq_ref, dk_ref, dv_ref, dg_ref, db_ref,
             dpre_ref, dbb_ref, daa_ref, dcw_ref, dal_ref, ddt_ref, carry):
        i = pl.program_id(0)
        flag = jnp.where(i < nt - 1, 1.0, 0.0)

        @pl.when(i == 0)
        def _():
            carry[...] = jnp.zeros_like(carry)
            dcw_ref[...] = jnp.zeros_like(dcw_ref)
            dal_ref[...] = jnp.zeros_like(dal_ref)
            ddt_ref[...] = jnp.zeros_like(ddt_ref)

        for part, d_ref in enumerate((dq_ref, dk_ref, dv_ref)):
            for h in range(AH):
                sl = slice(part * D + h * ADK, part * D + (h + 1) * ADK)
                ext = jnp.concatenate([prev_ref[:, sl] * flag, cur_ref[:, sl]], axis=0)
                rows = tuple(cw_ref[j:j + 1, sl] for j in range(4))
                _, vjp = jax.vjp(lambda e, r: prep_head_fn(e, r, PREP_SCALES[part]), ext, rows)
                dext, drows = vjp(d_ref[h])
                dcur = dext[HALO:]
                dpre_ref[:, sl] = jnp.concatenate([dcur[:tm - HALO], dcur[tm - HALO:] + carry[:, sl]],
                                                  axis=0).astype(BF16)
                carry[:, sl] = dext[:HALO]
                dcw_ref[:, sl] += _stack_rows(drows)
        _, vjp = jax.vjp(prep_gate_fn, bb_ref[...], aa_ref[...], al_ref[...], dt_ref[...])
        dbb, daa, dal, ddt = vjp((dg_ref[...], db_ref[...]))
        dbb_ref[...] = dbb.astype(BF16)
        daa_ref[...] = daa.astype(BF16)
        dal_ref[...] += dal
        ddt_ref[...] += ddt

    hm = pl.BlockSpec((AH, tm, ADK), lambda i: (0, rev(i), 0))
    return pl.pallas_call(
        body, name="prep_bwd", grid=(nt,),
        in_specs=_halo_specs(tm, wq, 0, rev) + [
            _rows(tm, 128, C_BA // 128, rev), _rows(tm, 128, C_BA // 128 + 1, rev),
            _const((4, wq)), _const((1, 128)), _const((1, 128)),
            hm, hm, hm, _rows(tm, 128, 0, rev), _rows(tm, 128, 0, rev)],
        out_specs=[_rows(tm, wq, 0, rev), _rows(tm, 128, 0, rev), _rows(tm, 128, 0, rev),
                   _const((4, wq)), _const((1, 128)), _const((1, 128))],
        out_shape=[jax.ShapeDtypeStruct((t, wq), BF16), jax.ShapeDtypeStruct((t, 128), BF16),
                   jax.ShapeDtypeStruct((t, 128), BF16), jax.ShapeDtypeStruct((4, wq), F32),
                   jax.ShapeDtypeStruct((1, 128), F32), jax.ShapeDtypeStruct((1, 128), F32)],
        scratch_shapes=[pltpu.VMEM((HALO, wq), F32)],
        compiler_params=_cparams(),
    )(proj, proj, proj, proj, conv_a, a_log, dtb, dq, dk, dv, dgcs, dbeta)


def _c1_specs(order):
    hm = pl.BlockSpec((AH, CH, ADK), lambda n: (0, order(n), 0))
    col = pl.BlockSpec((CH, LANE), lambda n: (order(n), 0))
    qk = pl.BlockSpec((1, AH, CH, CH), lambda n: (order(n), 0, 0, 0))
    eg = pl.BlockSpec((1, AH, 1, ADK), lambda n: (order(n), 0, 0, 0))
    return hm, col, qk, eg


def _heads(ref):
    return jnp.stack([ref[:, h * ADK:(h + 1) * ADK] for h in range(AH)], axis=0)


def c1_fwd(q, k, v, gcs, beta):
    t = q.shape[1]
    nc = t // CH
    hm, col, qks, egs = _c1_specs(lambda n: n)

    def body(q_ref, k_ref, v_ref, g_ref, b_ref, u_ref, w_ref, qg_ref, kd_ref, qk_ref, eg_ref):
        u, w, qk, qg, kd, eg = c1_heads(q_ref[...], k_ref[...], v_ref[...], g_ref[...], b_ref[...])
        u_ref[...] = u
        w_ref[...] = w
        qg_ref[...] = qg
        kd_ref[...] = kd
        qk_ref[0] = qk
        eg_ref[0] = eg

    return pl.pallas_call(
        body, name="c1_fwd", grid=(nc,),
        in_specs=[hm, hm, hm, col, col],
        out_specs=[hm, hm, hm, hm, qks, egs],
        out_shape=[jax.ShapeDtypeStruct((AH, t, ADK), F32)] * 4 + [
            jax.ShapeDtypeStruct((nc, AH, CH, CH), F32), jax.ShapeDtypeStruct((nc, AH, 1, ADK), F32)],
        compiler_params=_cparams(),
    )(q, k, v, gcs, beta)


def c1_bwd(q, k, v, gcs, beta, du, dw, dqg, dkd, dqk, deg):
    t = q.shape[1]
    nc = t // CH
    hm, col, qks, egs = _c1_specs(lambda n: n)

    def body(q_ref, k_ref, v_ref, g_ref, b_ref, du_ref, dw_ref, dqg_ref, dkd_ref, dqk_ref, deg_ref,
             dq_ref, dk_ref, dv_ref, dg_ref, db_ref):
        _, vjp = jax.vjp(c1_heads, q_ref[...], k_ref[...], v_ref[...], g_ref[...], b_ref[...])
        dq, dk, dv, dg, db = vjp((du_ref[...], dw_ref[...], dqk_ref[0], dqg_ref[...], dkd_ref[...], deg_ref[0]))
        dq_ref[...] = dq
        dk_ref[...] = dk
        dv_ref[...] = dv
        dg_ref[...] = dg
        db_ref[...] = db

    return pl.pallas_call(
        body, name="c1_bwd", grid=(nc,),
        in_specs=[hm, hm, hm, col, col, hm, hm, hm, hm, qks, egs],
        out_specs=[hm, hm, hm, col, col],
        out_shape=[jax.ShapeDtypeStruct((AH, t, ADK), F32)] * 3 + [jax.ShapeDtypeStruct((t, LANE), F32)] * 2,
        compiler_params=_cparams(),
    )(q, k, v, gcs, beta, du, dw, dqg, dkd, dqk, deg)


def c2_fwd(u, w, qg, kd, qk, eg, proj, norm_a):
    t = u.shape[1]
    nc = t // CH
    hm, _, qks, egs = _c1_specs(lambda n: n)
    tok = pl.BlockSpec((CH, D), lambda n: (n, 0))
    zspec = pl.BlockSpec((CH, D), lambda n: (n, C_Z // D))
    sspec = pl.BlockSpec((1, AH, ADK, ADK), lambda n: (n, 0, 0, 0))

    def body(u_ref, w_ref, qg_ref, kd_ref, qk_ref, eg_ref, z_ref, nw_ref, o_ref, sall_ref, st):
        n = pl.program_id(0)

        @pl.when(n == 0)
        def _():
            st[...] = jnp.zeros_like(st)

        s = st[...]
        sall_ref[0] = s
        og, s2 = c2_heads(s, u_ref[...], w_ref[...], qk_ref[0], qg_ref[...], kd_ref[...], eg_ref[0],
                          _heads(z_ref), nw_ref[...])
        st[...] = s2
        for h in range(AH):
            o_ref[:, h * ADK:(h + 1) * ADK] = og[h].astype(BF16)

    return pl.pallas_call(
        body, name="c2_fwd", grid=(nc,),
        in_specs=[hm, hm, hm, hm, qks, egs, zspec, _const((1, ADK))],
        out_specs=[tok, sspec],
        out_shape=[jax.ShapeDtypeStruct((t, D), BF16), jax.ShapeDtypeStruct((nc, AH, ADK, ADK), F32)],
        scratch_shapes=[pltpu.VMEM((AH, ADK, ADK), F32)],
        compiler_params=_cparams(),
    )(u, w, qg, kd, qk, eg, proj, norm_a)


def c2_bwd(u, w, qg, kd, qk, eg, proj, norm_a, sall, do):
    t = u.shape[1]
    nc = t // CH
    rev = lambda n: nc - 1 - n
    hm, _, qks, egs = _c1_specs(rev)
    tok = pl.BlockSpec((CH, D), lambda n: (rev(n), 0))
    zspec = pl.BlockSpec((CH, D), lambda n: (rev(n), C_Z // D))
    sspec = pl.BlockSpec((1, AH, ADK, ADK), lambda n: (rev(n), 0, 0, 0))

    def body(u_ref, w_ref, qg_ref, kd_ref, qk_ref, eg_ref, z_ref, nw_ref, sall_ref, do_ref,
             du_ref, dw_ref, dqg_ref, dkd_ref, dqk_ref, deg_ref, dz_ref, dnw_ref, dst):
        n = pl.program_id(0)

        @pl.when(n == 0)
        def _():
            dst[...] = jnp.zeros_like(dst)
            dnw_ref[...] = jnp.zeros_like(dnw_ref)

        _, vjp = jax.vjp(c2_heads, sall_ref[0], u_ref[...], w_ref[...], qk_ref[0], qg_ref[...], kd_ref[...],
                         eg_ref[0], _heads(z_ref), nw_ref[...])
        ds, du, dw, dqk, dqg, dkd, deg, dz, dn = vjp((_heads(do_ref), dst[...]))
        dst[...] = ds
        du_ref[...] = du
        dw_ref[...] = dw
        dqg_ref[...] = dqg
        dkd_ref[...] = dkd
        dqk_ref[0] = dqk
        deg_ref[0] = deg
        for h in range(AH):
            dz_ref[:, h * ADK:(h + 1) * ADK] = dz[h].astype(BF16)
        dnw_ref[...] += dn

    return pl.pallas_call(
        body, name="c2_bwd", grid=(nc,),
        in_specs=[hm, hm, hm, hm, qks, egs, zspec, _const((1, ADK)), sspec, tok],
        out_specs=[hm, hm, hm, hm, qks, egs, tok, _const((1, ADK))],
        out_shape=[jax.ShapeDtypeStruct((AH, t, ADK), F32)] * 4 + [
            jax.ShapeDtypeStruct((nc, AH, CH, CH), F32), jax.ShapeDtypeStruct((nc, AH, 1, ADK), F32),
            jax.ShapeDtypeStruct((t, D), BF16), jax.ShapeDtypeStruct((1, ADK), F32)],
        scratch_shapes=[pltpu.VMEM((AH, ADK, ADK), F32)],
        compiler_params=_cparams(),
    )(u, w, qg, kd, qk, eg, proj, norm_a, sall, do)


NQB = TQ // CH
NKB = 2 * TQ // CH
NDIST = BPREV + 1
KLO = -(NQB - 2)
NPAIR = NKB - 1 - KLO + 1


def bias_table(rel_bias):
    nh = rel_bias.shape[0]
    relx = jnp.concatenate([rel_bias, jnp.broadcast_to(rel_bias[:, -1:], (nh, CH * BPREV + 2 * CH - 1 - RELSZ))],
                           axis=1)
    t = jnp.stack([relx[:, CH * k:CH * k + 2 * CH - 1] for k in range(NDIST)], axis=1)
    trev = t[:, :, ::-1]
    g2 = jnp.concatenate([trev[:, :, CH - 1:], jnp.zeros((nh, NDIST, 1), F32), trev[:, :, :CH - 1]], axis=2)
    flat = jnp.tile(g2, (1, 1, CH + 1))[:, :, :CH * (2 * CH - 1)]
    blk = flat.reshape(nh, NDIST, CH, 2 * CH - 1)[..., :CH]
    neg = jnp.full((nh, NQB - 1, CH, CH), NEG, F32)
    asc = jnp.concatenate([neg, blk, neg], axis=1)
    return jnp.concatenate([asc[:, 1:], asc[:, :-1]], axis=-1)


SUBQ = 4 * CH
NSUB = TQ // SUBQ
KWIN = SUBQ + BPREV * CH


def assemble_bias(tab, r):
    b0 = r * SUBQ // (2 * CH)
    rows = [jnp.concatenate([tab[NQB + a - 2 * b - KLO] for b in range(b0, b0 + KWIN // (2 * CH))], axis=1)
            for a in range(r * SUBQ // CH, (r + 1) * SUBQ // CH)]
    return jnp.concatenate(rows, axis=0)


def bias_table_bwd_layout(dtab):
    nh = dtab.shape[0]
    dasc = (jnp.pad(dtab[..., :CH], ((0, 0), (1, 0), (0, 0), (0, 0)))
            + jnp.pad(dtab[..., CH:], ((0, 0), (0, 1), (0, 0), (0, 0))))
    dblk = dasc[:, NQB - 1:NQB - 1 + NDIST]
    dr = jnp.pad(dblk, ((0, 0), (0, 0), (0, 0), (0, CH - 1)))
    flat = jnp.pad(dr.reshape(nh, NDIST, CH * (2 * CH - 1)), ((0, 0), (0, 0), (0, 3 * CH)))
    return flat.reshape(nh, NDIST, CH + 1, 2 * CH).transpose(0, 2, 1, 3).reshape(nh, CH + 1, NDIST * 2 * CH)


def _fold_matrix_np():
    f = np.zeros((NDIST * 2 * CH, 384), np.float32)
    for k in range(NDIST):
        s = k
        for xx in range(2 * CH):
            if xx == CH:
                continue
            m = CH - 1 - xx if xx < CH else 3 * CH - 1 - xx
            f[s * 2 * CH + xx, min(CH * k + m, RELSZ - 1)] = 1.0
    return f


def relbias_reduce(dlay):
    nh, rows, cols = dlay.shape
    rpad = (-rows) % 8
    dlay = jnp.pad(dlay, ((0, 0), (0, rpad), (0, 0)))
    fold = jnp.asarray(_fold_matrix_np())

    def body(d_ref, f_ref, o_ref):
        cs = jnp.sum(d_ref[0], axis=0, keepdims=True)
        o_ref[0] = _mmh(jnp.broadcast_to(cs, (8, cols)), f_ref[...])

    out = pl.pallas_call(
        body, name="relbias_reduce", grid=(nh,),
        in_specs=[pl.BlockSpec((1, rows + rpad, cols), lambda h: (h, 0, 0)), _const((cols, 384))],
        out_specs=pl.BlockSpec((1, 8, 384), lambda h: (h, 0, 0)),
        out_shape=jax.ShapeDtypeStruct((nh, 8, 384), F32),
        compiler_params=_cparams(),
    )(dlay, fold)
    return out[:, 0, :RELSZ]


def attn_fwd(proj, bias):
    t = proj.shape[0]
    nt = t // TQ
    cb = C_QKVB // 128

    def body(q_ref, kp_ref, kc_ref, vp_ref, vc_ref, b_ref, o_ref):
        i = pl.program_id(1)
        firstf = jnp.where(i == 0, 1.0, 0.0)
        for r in range(NSUB):
            lo, hi = r * SUBQ, r * SUBQ + KWIN - TQ
            kw = jnp.concatenate([kp_ref[lo:, :], kc_ref[:hi, :]], axis=0)
            vw = jnp.concatenate([vp_ref[lo:, :], vc_ref[:hi, :]], axis=0)
            o_ref[lo:lo + SUBQ, :] = attn_sub(q_ref[lo:lo + SUBQ, :], kw, vw, b_ref[...], r, firstf).astype(BF16)

    def blk(off, prev):
        if prev:
            return pl.BlockSpec((TQ, 128), lambda p, i: (jnp.maximum(i - 1, 0), cb + off + p))
        return pl.BlockSpec((TQ, 128), lambda p, i: (i, cb + off + p))

    return pl.pallas_call(
        body, name="attn_fwd", grid=(BH // 2, nt),
        in_specs=[blk(0, False), blk(8, True), blk(8, False), blk(16, True), blk(16, False),
                  pl.BlockSpec((2, NPAIR, CH, 2 * CH), lambda p, i: (p, 0, 0, 0))],
        out_specs=pl.BlockSpec((TQ, 128), lambda p, i: (i, p)),
        out_shape=jax.ShapeDtypeStruct((t, D), BF16),
        compiler_params=_cparams(2),
    )(proj, proj, proj, proj, proj, bias)


def attn_bwd(proj, bias, do):
    t = proj.shape[0]
    nt = t // TQ
    cb = C_QKVB // 128

    def body(q_ref, kp_ref, kc_ref, vp_ref, vc_ref, b_ref, do_ref,
             dq_ref, dk_ref, dv_ref, db_ref, ck, cv, ak, av):
        i = pl.program_id(1)

        @pl.when(i == 0)
        def _():
            ck[...] = jnp.zeros_like(ck)
            cv[...] = jnp.zeros_like(cv)
            db_ref[...] = jnp.zeros_like(db_ref)

        @pl.when(i < nt)
        def _():
            firstf = jnp.where(i == 0, 1.0, 0.0)
            ak[...] = jnp.zeros_like(ak)
            av[...] = jnp.zeros_like(av)
            db = None
            for r in range(NSUB):
                lo, hi = r * SUBQ, r * SUBQ + KWIN - TQ
                kw = jnp.concatenate([kp_ref[lo:, :], kc_ref[:hi, :]], axis=0).astype(F32)
                vw = jnp.concatenate([vp_ref[lo:, :], vc_ref[:hi, :]], axis=0).astype(F32)
                _, vjp = jax.vjp(lambda q, k, v, b: attn_sub(q, k, v, b, r, firstf),
                                 q_ref[lo:lo + SUBQ, :].astype(F32), kw, vw, b_ref[...])
                dq, dkw, dvw, dbr = vjp(do_ref[lo:lo + SUBQ, :])
                dq_ref[lo:lo + SUBQ, :] = dq.astype(BF16)
                ak[lo:lo + KWIN, :] += dkw
                av[lo:lo + KWIN, :] += dvw
                db = dbr if db is None else db + dbr
            dk_ref[...] = (ck[...] + ak[:TQ, :]).astype(BF16)
            dv_ref[...] = (cv[...] + av[:TQ, :]).astype(BF16)
            ck[...] = ak[TQ:, :]
            cv[...] = av[TQ:, :]
            db_ref[...] += db

        @pl.when(i == nt)
        def _():
            dk_ref[...] = ck[...].astype(BF16)
            dv_ref[...] = cv[...].astype(BF16)

    def blk(off, prev):
        if prev:
            return pl.BlockSpec((TQ, 128), lambda p, i: (jnp.clip(i - 1, 0, nt - 1), cb + off + p))
        return pl.BlockSpec((TQ, 128), lambda p, i: (jnp.minimum(i, nt - 1), cb + off + p))

    own = pl.BlockSpec((TQ, 128), lambda p, i: (jnp.minimum(i, nt - 1), p))
    lag = pl.BlockSpec((TQ, 128), lambda p, i: (jnp.maximum(i - 1, 0), p))
    return pl.pallas_call(
        body, name="attn_bwd", grid=(BH // 2, nt + 1),
        in_specs=[blk(0, False), blk(8, True), blk(8, False), blk(16, True), blk(16, False),
                  pl.BlockSpec((2, NPAIR, CH, 2 * CH), lambda p, i: (p, 0, 0, 0)), own],
        out_specs=[own, lag, lag, pl.BlockSpec((2, NPAIR, CH, 2 * CH), lambda p, i: (p, 0, 0, 0))],
        out_shape=[jax.ShapeDtypeStruct((t, D), BF16)] * 3 + [jax.ShapeDtypeStruct((BH, NPAIR, CH, 2 * CH), F32)],
        scratch_shapes=[pltpu.VMEM((TQ, 128), F32), pltpu.VMEM((TQ, 128), F32),
                        pltpu.VMEM((2 * TQ, 128), F32), pltpu.VMEM((2 * TQ, 128), F32)],
        compiler_params=_cparams(2),
    )(proj, proj, proj, proj, proj, bias, do)


MERGE_TM = 256


def merge_fwd(x, oa, ob, proj, vecs, wa, wb, wo):
    t = x.shape[0]
    tm = MERGE_TM
    names = ("bga", "bgb", "gate_t", "g1", "b1", "scale_f", "shift_f")

    def body(x_ref, oa_ref, ob_ref, gra_ref, grb_ref, *rest):
        vrefs = rest[:7]
        wa_ref, wb_ref, wo_ref, y_ref, h_ref = rest[7:]
        vv = [r[...] for r in vrefs]
        zero = jnp.zeros((tm, D), F32)
        y1, _ = merge_fn(x_ref[...], oa_ref[...], ob_ref[...], gra_ref[...], grb_ref[...], zero, zero, zero,
                         *vv, wa_ref[...], wb_ref[...], wo_ref[...])
        y_ref[...] = y1
        h_ref[...] = (y1 * (1.0 + vv[5]) + vv[6]).astype(BF16)

    return pl.pallas_call(
        body, name="merge_fwd", grid=(t // tm,),
        in_specs=[_rows(tm, D), _rows(tm, D), _rows(tm, D), _rows(tm, D, C_GATE // D), _rows(tm, D, C_GATE // D + 1)]
        + [_const((1, D))] * 7 + [_const((D, D))] * 3,
        out_specs=[_rows(tm, D), _rows(tm, D)],
        out_shape=[jax.ShapeDtypeStruct((t, D), F32), jax.ShapeDtypeStruct((t, D), BF16)],
        compiler_params=_cparams(),
    )(x, oa, ob, proj, proj, *[vecs[n] for n in names], wa, wb, wo)


def merge_bwd(x, oa, ob, proj, vecs, wa, wb, wo, dy1):
    t = x.shape[0]
    tm = MERGE_TM
    names = ("bga", "bgb", "gate_t", "g1", "b1", "scale_f", "shift_f")

    def body(x_ref, oa_ref, ob_ref, gra_ref, grb_ref, *rest):
        vrefs = rest[:7]
        wa_ref, wb_ref, wo_ref, dy_ref = rest[7:11]
        (dx_ref, doa_ref, dob_ref, dga_ref, dgb_ref, mg_ref, dmix_ref, dpa_ref, dpb_ref,
         dbga_ref, dbgb_ref, dgt_ref, dg1_ref, db1_ref) = rest[11:]
        i = pl.program_id(0)
        vv = [r[...] for r in vrefs]
        zero = jnp.zeros((tm, D), F32)

        def f(x_, oa_, ob_, gra_, grb_, ppa, ppb, pmix, bga, bgb, gate_t, g1, b1):
            return merge_fn(x_, oa_, ob_, gra_, grb_, ppa, ppb, pmix, bga, bgb, gate_t, g1, b1, vv[5], vv[6],
                            wa_ref[...], wb_ref[...], wo_ref[...])

        _, vjp, merged = jax.vjp(f, x_ref[...], oa_ref[...].astype(F32), ob_ref[...].astype(F32),
                                 gra_ref[...], grb_ref[...], zero, zero, zero, *vv[:5], has_aux=True)
        dx, doa, dob, dga, dgb, dpa, dpb, dmix, dbga, dbgb, dgt, dg1, db1 = vjp(dy_ref[...])
        dx_ref[...] = dx
        doa_ref[...] = doa
        dob_ref[...] = dob
        dga_ref[...] = dga.astype(BF16)
        dgb_ref[...] = dgb.astype(BF16)
        mg_ref[...] = merged.astype(BF16)
        dmix_ref[...] = dmix.astype(BF16)
        dpa_ref[...] = dpa.astype(BF16)
        dpb_ref[...] = dpb.astype(BF16)
        accs = (dbga_ref, dbgb_ref, dgt_ref, dg1_ref, db1_ref)

        @pl.when(i == 0)
        def _():
            for a in accs:
                a[...] = jnp.zeros_like(a)

        for a, val in zip(accs, (dbga, dbgb, dgt, dg1, db1)):
            a[...] += val

    return pl.pallas_call(
        body, name="merge_bwd", grid=(t // tm,),
        in_specs=[_rows(tm, D), _rows(tm, D), _rows(tm, D), _rows(tm, D, C_GATE // D), _rows(tm, D, C_GATE // D + 1)]
        + [_const((1, D))] * 7 + [_const((D, D))] * 3 + [_rows(tm, D)],
        out_specs=[_rows(tm, D)] * 9 + [_const((1, D))] * 5,
        out_shape=[jax.ShapeDtypeStruct((t, D), F32)] * 3 + [jax.ShapeDtypeStruct((t, D), BF16)] * 6
        + [jax.ShapeDtypeStruct((1, D), F32)] * 5,
        compiler_params=_cparams(),
    )(x, oa, ob, proj, proj, *[vecs[n] for n in names], wa, wb, wo, dy1)


FFN_TM = 128


def ffn_act_fwd(up, conv_w, bconv):
    t, wdt = up.shape
    tm = FFN_TM

    def body(prev_ref, cur_ref, cw_ref, bc_ref, a_ref):
        i = pl.program_id(0)
        flag = jnp.where(i > 0, 1.0, 0.0)

        def ext(sl):
            return jnp.concatenate([prev_ref[:, sl] * flag, cur_ref[:, sl]], axis=0)

        def rows(sl):
            return tuple(cw_ref[j:j + 1, sl] for j in range(3))

        for cb in range(DFF // LANE):
            g = slice(cb * LANE, (cb + 1) * LANE)
            v = slice(DFF + cb * LANE, DFF + (cb + 1) * LANE)
            a_ref[:, g] = ffn_act_fn(ext(g), ext(v), rows(g), rows(v), bc_ref[:, g], bc_ref[:, v]).astype(BF16)

    return pl.pallas_call(
        body, name="ffn_act_fwd", grid=(t // tm,),
        in_specs=_halo_specs(tm, wdt, 0, lambda i: i) + [_const((3, wdt)), _const((1, wdt))],
        out_specs=_rows(tm, DFF),
        out_shape=jax.ShapeDtypeStruct((t, DFF), BF16),
        compiler_params=_cparams(),
    )(up, up, conv_w, bconv)


def ffn_act_bwd(up, conv_w, bconv, da):
    t, wdt = up.shape
    tm = FFN_TM
    nt = t // tm
    rev = lambda i: nt - 1 - i

    def body(prev_ref, cur_ref, cw_ref, bc_ref, da_ref, dup_ref, dcw_ref, dbc_ref, carry):
        i = pl.program_id(0)
        flag = jnp.where(i < nt - 1, 1.0, 0.0)

        @pl.when(i == 0)
        def _():
            carry[...] = jnp.zeros_like(carry)
            dcw_ref[...] = jnp.zeros_like(dcw_ref)
            dbc_ref[...] = jnp.zeros_like(dbc_ref)

        def ext(sl):
            return jnp.concatenate([prev_ref[:, sl] * flag, cur_ref[:, sl]], axis=0)

        def rows(sl):
            return tuple(cw_ref[j:j + 1, sl] for j in range(3))

        def emit(sl, dext, drows, dbc):
            dcur = dext[HALO:]
            dup_ref[:, sl] = jnp.concatenate([dcur[:tm - HALO], dcur[tm - HALO:] + carry[:, sl]], axis=0).astype(BF16)
            carry[:, sl] = dext[:HALO]
            dcw_ref[:, sl] += _stack_rows(drows)
            dbc_ref[:, sl] += dbc

        for cb in range(DFF // LANE):
            g = slice(cb * LANE, (cb + 1) * LANE)
            v = slice(DFF + cb * LANE, DFF + (cb + 1) * LANE)
            _, vjp = jax.vjp(ffn_act_fn, ext(g), ext(v), rows(g), rows(v), bc_ref[:, g], bc_ref[:, v])
            dxg, dxv, drg, drv, dbg, dbv = vjp(da_ref[:, g])
            emit(g, dxg, drg, dbg)
            emit(v, dxv, drv, dbv)

    return pl.pallas_call(
        body, name="ffn_act_bwd", grid=(nt,),
        in_specs=_halo_specs(tm, wdt, 0, rev) + [_const((3, wdt)), _const((1, wdt)), _rows(tm, DFF, 0, rev)],
        out_specs=[_rows(tm, wdt, 0, rev), _const((3, wdt)), _const((1, wdt))],
        out_shape=[jax.ShapeDtypeStruct((t, wdt), BF16), jax.ShapeDtypeStruct((3, wdt), F32),
                   jax.ShapeDtypeStruct((1, wdt), F32)],
        scratch_shapes=[pltpu.VMEM((HALO, wdt), F32)],
        compiler_params=_cparams(),
    )(up, up, conv_w, bconv, da)


HEAD_TM = 256


def head_fwd_bwd(a, y1, tgt, gate_f, g2, b2, wd):
    t = a.shape[0]
    tm = HEAD_TM

    def body(a_ref, y_ref, t_ref, gf_ref, g2_ref, b2_ref, wd_ref,
             da_ref, dy_ref, dffn_ref, dgf_ref, dg2_ref, db2_ref, loss_ref):
        i = pl.program_id(0)
        zero = jnp.zeros((tm, D), F32)

        def f(a_, y_, pf, gf, g2_, b2_):
            return head_fn(a_, y_, pf, gf, g2_, b2_, t_ref[...], wd_ref[...])

        loss, vjp = jax.vjp(f, a_ref[...].astype(F32), y_ref[...], zero, gf_ref[...], g2_ref[...], b2_ref[...])
        da, dy, dffn, dgf, dg2, db2 = vjp(jnp.ones((), F32))
        da_ref[...] = da
        dy_ref[...] = dy
        dffn_ref[...] = dffn.astype(BF16)
        accs = (dgf_ref, dg2_ref, db2_ref, loss_ref)

        @pl.when(i == 0)
        def _():
            for r in accs:
                r[...] = jnp.zeros_like(r)

        dgf_ref[...] += dgf
        dg2_ref[...] += dg2
        db2_ref[...] += db2
        loss_ref[...] += loss * jnp.ones((1, 128), F32)

    return pl.pallas_call(
        body, name="head_fwd_bwd", grid=(t // tm,),
        in_specs=[_rows(tm, DFF), _rows(tm, D), _rows(tm, D), _const((1, D)), _const((1, D)), _const((1, D)),
                  _const((DFF, D))],
        out_specs=[_rows(tm, DFF), _rows(tm, D), _rows(tm, D), _const((1, D)), _const((1, D)), _const((1, D)),
                   _const((1, 128))],
        out_shape=[jax.ShapeDtypeStruct((t, DFF), F32), jax.ShapeDtypeStruct((t, D), F32),
                   jax.ShapeDtypeStruct((t, D), BF16)] + [jax.ShapeDtypeStruct((1, D), F32)] * 3
        + [jax.ShapeDtypeStruct((1, 128), F32)],
        compiler_params=_cparams(),
    )(a, y1, tgt, gate_f, g2, b2, wd)


def ada_fwd(c_all, w_sh, b_sh):
    def body(c_ref, w_ref, b_ref, o_ref):
        o_ref[...] = _mmh(_silu(c_ref[...]), w_ref[...]) + b_ref[...]

    n = w_sh.shape[1]
    return pl.pallas_call(
        body, name="ada_fwd", out_shape=jax.ShapeDtypeStruct((NDEV, n), F32),
        in_specs=[pl.BlockSpec(memory_space=pltpu.VMEM)] * 3,
        out_specs=pl.BlockSpec(memory_space=pltpu.VMEM),
        compiler_params=pltpu.CompilerParams(vmem_limit_bytes=VMEM_LIMIT),
    )(c_all, w_sh, b_sh)


def ada_wgrad(c_all_t, dmod_sh):
    def body(c_ref, d_ref, o_ref):
        o_ref[...] = _mmh(_silu(c_ref[...]), d_ref[...])

    return pl.pallas_call(
        body, name="ada_wgrad", out_shape=jax.ShapeDtypeStruct((c_all_t.shape[0], dmod_sh.shape[1]), F32),
        in_specs=[pl.BlockSpec(memory_space=pltpu.VMEM)] * 2,
        out_specs=pl.BlockSpec(memory_space=pltpu.VMEM),
        compiler_params=pltpu.CompilerParams(vmem_limit_bytes=VMEM_LIMIT),
    )(c_all_t, dmod_sh)


def adamw(gparts, w, m, v, name):
    p, r, c = gparts.shape
    tr = r if r <= 256 else _pick(r, (256, 128, 64, 32, 16, 8))
    c1 = 1.0 - B1 ** STEP
    c2 = 1.0 - B2 ** STEP

    def body(g_ref, w_ref, m_ref, v_ref, go_ref, d_ref, mo_ref, vo_ref):
        g = g_ref[0].astype(F32)
        for s in range(1, p):
            g = g + g_ref[s].astype(F32)
        mn = B1 * m_ref[...] + (1.0 - B1) * g
        vn = B2 * v_ref[...] + (1.0 - B2) * (g * g)
        go_ref[...] = g
        d_ref[...] = -LR * ((mn / c1) / (jnp.sqrt(vn / c2) + AEPS) + WD * w_ref[...])
        mo_ref[...] = mn
        vo_ref[...] = vn

    spec = pl.BlockSpec((tr, c), lambda i: (i, 0))
    return pl.pallas_call(
        body, name=name, grid=(r // tr,),
        in_specs=[pl.BlockSpec((p, tr, c), lambda i: (0, i, 0)), spec, spec, spec],
        out_specs=[spec] * 4,
        out_shape=[jax.ShapeDtypeStruct((r, c), F32)] * 4,
        compiler_params=_cparams(),
    )(gparts, w, m, v)


def _me():
    x, y, c = lax.axis_index("x"), lax.axis_index("y"), lax.axis_index("c")
    return x, y, c, 4 * x + 2 * y + c


def _peer(x, y, c, d):
    px = 1 - x if (d >> 2) & 1 else x
    py = 1 - y if (d >> 1) & 1 else y
    pc = 1 - c if d & 1 else c
    return (px, py, pc), 4 * px + 2 * py + pc


def _exchange(arrs, name, scatter):
    n = len(arrs)

    def body(*refs):
        ins, outs = refs[:n], refs[n:2 * n]
        send, recv, lsem = refs[2 * n:]
        x, y, c, me = _me()
        remote, local = [], []
        for k in range(n):
            src = ins[k].at[me] if scatter else ins[k]
            cp = pltpu.make_async_copy(src, outs[k].at[me], lsem.at[k])
            cp.start()
            local.append(cp)
            for d in range(1, NDEV):
                dev, pid = _peer(x, y, c, d)
                src = ins[k].at[pid] if scatter else ins[k]
                cp = pltpu.make_async_remote_copy(src_ref=src, dst_ref=outs[k].at[me],
                                                  send_sem=send.at[k, d - 1], recv_sem=recv.at[k, d - 1],
                                                  device_id=dev, device_id_type=pl.DeviceIdType.MESH)
                cp.start()
                remote.append(cp)
        for cp in remote:
            cp.wait()
        for cp in local:
            cp.wait()

    shapes = [a.shape if scatter else (NDEV,) + a.shape for a in arrs]
    return pl.pallas_call(
        body, name=name,
        in_specs=[pl.BlockSpec(memory_space=pl.ANY)] * n,
        out_specs=[pl.BlockSpec(memory_space=pl.ANY)] * n,
        out_shape=[jax.ShapeDtypeStruct(s, a.dtype) for s, a in zip(shapes, arrs)],
        scratch_shapes=[pltpu.SemaphoreType.DMA((n, NDEV - 1)), pltpu.SemaphoreType.DMA((n, NDEV - 1)),
                        pltpu.SemaphoreType.DMA((n,))],
        compiler_params=pltpu.CompilerParams(has_side_effects=True),
    )(*arrs)


def all_gather(arrs, name):
    return _exchange(arrs, name, False)


def all_to_all(arrs, name):
    return _exchange(arrs, name, True)


_HBM = pl.BlockSpec(memory_space=pltpu.HBM)
_SEM = pl.BlockSpec(memory_space=pltpu.SEMAPHORE)
_EFFECT = pltpu.SideEffectType.DATAFLOW_SIDE_EFFECTING
NPEER = NDEV - 1


def exchange_start(arrs, name, scatter):
    n = len(arrs)
    lands = [lax.empty(a.shape if scatter else (NDEV,) + a.shape, a.dtype) for a in arrs]

    def body(*refs):
        ins, lrefs = refs[:n], refs[n:2 * n]
        send, recv, token = refs[2 * n], refs[2 * n + 1], refs[-1]
        x, y, c, me = _me()
        for k in range(n):
            for d in range(1, NDEV):
                dev, pid = _peer(x, y, c, d)
                src = ins[k].at[pid] if scatter else ins[k]
                pltpu.make_async_remote_copy(src_ref=src, dst_ref=lrefs[k].at[me],
                                             send_sem=send.at[k * NPEER + d - 1], recv_sem=recv.at[k * NPEER + d - 1],
                                             device_id=dev, device_id_type=pl.DeviceIdType.MESH).start()
        token[...] = jnp.zeros_like(token)

    thru = [pltpu.HBM(a.shape, a.dtype) for a in list(arrs) + lands]
    outs = pl.pallas_call(
        body, name=name,
        out_shape=(pltpu.SemaphoreType.DMA((n * NPEER,)), pltpu.SemaphoreType.DMA((n * NPEER,)), *thru,
                   jax.ShapeDtypeStruct((8, 128), F32)),
        in_specs=[_HBM] * (2 * n),
        out_specs=(_SEM, _SEM, *([_HBM] * (2 * n)), pl.BlockSpec(memory_space=pltpu.VMEM)),
        input_output_aliases={i: 2 + i for i in range(2 * n)},
        compiler_params=pltpu.CompilerParams(has_side_effects=_EFFECT),
    )(*[pltpu.with_memory_space_constraint(a, pltpu.HBM) for a in list(arrs) + lands])
    handle = dict(send=outs[0], recv=outs[1], src=list(outs[2:2 + n]), land=list(outs[2 + n:2 + 2 * n]),
                  scatter=scatter)
    return handle, outs[-1][0, 0]


def exchange_wait(handle, after, name):
    n = len(handle["src"])
    scatter = handle["scatter"]

    def body(*refs):
        ins, lrefs = refs[:n], refs[n:2 * n]
        send, recv = refs[2 * n], refs[2 * n + 1]
        x, y, c, _ = _me()
        for k in range(n):
            for d in range(1, NDEV):
                dev, _ = _peer(x, y, c, d)
                src = ins[k].at[0] if scatter else ins[k]
                cp = pltpu.make_async_remote_copy(src_ref=src, dst_ref=lrefs[k].at[0],
                                                  send_sem=send.at[k * NPEER + d - 1],
                                                  recv_sem=recv.at[k * NPEER + d - 1],
                                                  device_id=dev, device_id_type=pl.DeviceIdType.MESH)
                cp.wait_send()
                cp.wait_recv()

    arrs = handle["src"] + handle["land"]
    outs = pl.pallas_call(
        body, name=name,
        out_shape=tuple(pltpu.HBM(a.shape, a.dtype) for a in arrs),
        in_specs=[_HBM] * (2 * n) + [_SEM, _SEM, pl.BlockSpec(memory_space=pl.ANY)],
        out_specs=tuple([_HBM] * (2 * n)),
        input_output_aliases={i: i for i in range(2 * n)},
        compiler_params=pltpu.CompilerParams(has_side_effects=_EFFECT),
    )(*arrs, handle["send"], handle["recv"], after)
    me = 4 * lax.axis_index("x") + 2 * lax.axis_index("y") + lax.axis_index("c")
    landed = []
    for own, land in zip(outs[:n], outs[n:]):
        mine = lax.dynamic_index_in_dim(own, me, 0, keepdims=True) if scatter else own[None]
        landed.append(lax.dynamic_update_slice_in_dim(land, mine, me, 0))
    return landed


def _to_cat(w_in):
    k = w_in.shape[0]
    ba = jnp.zeros((k, NCAT - C_BA), w_in.dtype)
    ba = ba.at[:, 0:8].set(w_in[:, 4096:4104]).at[:, 128:136].set(w_in[:, 4104:4112])
    return jnp.concatenate([w_in[:, 0:3072], w_in[:, 3072:4096], w_in[:, 4112:7184], w_in[:, 7184:9232], ba], axis=1)


def _from_cat(dw):
    return jnp.concatenate([dw[:, 0:3072], dw[:, 3072:4096], dw[:, C_BA:C_BA + 8], dw[:, C_BA + 128:C_BA + 136],
                            dw[:, 4096:7168], dw[:, 7168:9216]], axis=1)


def _pad128(v):
    return jnp.pad(v, ((0, 0), (0, 128 - v.shape[1])))


def local_step(x, tgt, mod, wts, small, late_weights=None, on_grads=None):
    if on_grads is None:
        on_grads = lambda group, gd: jnp.zeros((), F32)
    t = x.shape[0]
    nc = t // CH
    shift_t, scale_t, gate_t, shift_f, scale_f, gate_f = mod
    wcat = _to_cat(wts["w_in"])
    a_log = _pad128(small["a_log"])
    dtb = _pad128(small["dt_bias"])
    vecs = dict(bga=small["b_gate"][:, :D], bgb=small["b_gate"][:, D:], gate_t=gate_t, g1=small["ln1_g"],
                b1=small["ln1_b"], scale_f=scale_f, shift_f=shift_f)

    h1 = modulate(x, scale_t, shift_t, "modulate_t")
    proj = matmul(h1, wcat, F32, "in_proj")
    q, k, v, gcs, beta = prep_fwd(proj, small["conv_a"], a_log, dtb)

    u, w, qg, kd, qk, eg = c1_fwd(q, k, v, gcs, beta)
    oa, sall = c2_fwd(u, w, qg, kd, qk, eg, proj, small["norm_a"])
    bias = bias_table(small["rel_bias"])
    ob = attn_fwd(proj, bias)
    if late_weights is not None:
        wts = {**wts, **late_weights(ob)}
    y1, h2 = merge_fwd(x, oa, ob, proj, vecs, wts["w_a"], wts["w_b"], wts["w_o"])
    up = matmul(h2, wts["w_up"], F32, "up_proj")
    a = ffn_act_fwd(up, small["conv_ffn"], small["b_conv_ffn"])

    da, dy1_res, dffn, dgate_f, dg2, db2, loss = head_fwd_bwd(a, y1, tgt, gate_f, small["ln2_g"], small["ln2_b"],
                                                            wts["w_down"])
    g_w_down = matmul(a, dffn, F32, "wgrad_down", ta=True)
    dup, g_conv_ffn, g_bconv = ffn_act_bwd(up, small["conv_ffn"], small["b_conv_ffn"], da)
    dh2 = matmul(dup, wts["w_up"], F32, "dgrad_up", tb=True)
    g_w_up = matmul(h2, dup, F32, "wgrad_up", ta=True)
    tok = on_grads("ffn", dict(w_up=g_w_up, w_down=g_w_down))
    dy1, dscale_f, dshift_f = modulate_bwd(dh2, y1, dy1_res, scale_f + tok, "modulate_f_bwd")
    (dx_res, doa, dob, dga, dgb, merged, dmix, dpa, dpb,
     dbga, dbgb, dgate_t, dg1, db1) = merge_bwd(x, oa, ob, proj, vecs, wts["w_a"], wts["w_b"], wts["w_o"], dy1)
    g_w_o = matmul(merged, dmix, F32, "wgrad_o", ta=True)
    g_w_a = matmul(oa, dpa, F32, "wgrad_a", ta=True)
    g_w_b = matmul(ob, dpb, F32, "wgrad_b", ta=True)
    tok = on_grads("mix", dict(w_o=g_w_o, w_a=g_w_a, w_b=g_w_b))
    dqb, dkb, dvb, dbias = attn_bwd(proj, bias, dob)
    g_rel = relbias_reduce(bias_table_bwd_layout(dbias))
    du, dw, dqg, dkd, dqk, deg, dz, g_norm = c2_bwd(u, w, qg, kd, qk, eg, proj, small["norm_a"] + tok, sall, doa)
    dq, dk, dv, dgcs, dbeta = c1_bwd(q, k, v, gcs, beta, du, dw, dqg, dkd, dqk, deg)
    dpre, dbb, daa, g_conv_a, g_alog, g_dtb = prep_bwd(proj, small["conv_a"], a_log, dtb, dq, dk, dv, dgcs, dbeta)
    dba = jnp.concatenate([dbb, daa, jnp.zeros((t, NCAT - C_BA - 256), BF16)], axis=1)
    dproj = jnp.concatenate([dpre, dz, dqb, dkb, dvb, dga, dgb, dba], axis=1)
    g_wcat = matmul(h1, dproj, F32, "wgrad_in", ta=True)
    tok = on_grads("in", dict(w_in=_from_cat(g_wcat)))
    dh1 = matmul(dproj, wcat + tok.astype(BF16), F32, "dgrad_in", tb=True)
    grad_x, dscale_t, dshift_t = modulate_bwd(dh1, x, dx_res, scale_t + tok, "modulate_t_bwd")

    dmod = (dshift_t, dscale_t, dgate_t, dshift_f, dscale_f, dgate_f)
    grads = dict(w_in=_from_cat(g_wcat), w_up=g_w_up, w_down=g_w_down, w_a=g_w_a, w_b=g_w_b, w_o=g_w_o,
                 conv_a=g_conv_a, rel_bias=g_rel, conv_ffn=g_conv_ffn,
                 b_gate=jnp.concatenate([dbga, dbgb], axis=1), a_log=g_alog[:, :AH], dt_bias=g_dtb[:, :AH],
                 norm_a=g_norm, ln1_g=dg1, ln1_b=db1, b_conv_ffn=g_bconv, ln2_g=dg2, ln2_b=db2)
    return loss[0, 0], grad_x, dmod, grads


_REP = {}
_off = 0
for _n, _wd, _pw in (("b_ada", 6144, 6144), ("b_gate", 2048, 2048), ("a_log", 8, 128), ("dt_bias", 8, 128),
                     ("norm_a", 128, 128), ("ln1_g", 1024, 1024), ("ln1_b", 1024, 1024),
                     ("b_conv_ffn", 5632, 5632), ("ln2_g", 1024, 1024), ("ln2_b", 1024, 1024), ("loss", 1, 128)):
    _REP[_n] = (_off, _wd, _pw)
    _off += _pw
REP_LEN = _off
REP_NAMES = [n for n in _REP if n != "loss"]
_SH = (("conv_a", (4, 384)), ("rel_bias", (16, 40)), ("conv_ffn", (3, 704)))
SH_LEN = 4352


def _pack_rep(vals):
    parts = []
    for n, (_, wd, pw) in _REP.items():
        a = vals.get(n)
        a = jnp.zeros((1, pw), F32) if a is None else jnp.pad(a.reshape(1, wd), ((0, 0), (0, pw - wd)))
        parts.append(a)
    return jnp.concatenate(parts, axis=1)


def _unpack_rep(vec, name):
    o, wd, _ = _REP[name]
    return vec[:, o:o + wd]


def _pack_sh(vals):
    parts = [vals[n].reshape(vals[n].shape[:-2] + (-1,)) for n, _ in _SH]
    a = jnp.concatenate(parts, axis=-1)
    return jnp.pad(a, [(0, 0)] * (a.ndim - 1) + [(0, SH_LEN - a.shape[-1])])


def _unpack_sh(vec, name):
    o = 0
    for n, shp in _SH:
        sz = shp[0] * shp[1]
        if n == name:
            return vec[0, o:o + sz].reshape(shp)
        o += sz
    raise KeyError(name)


def _col_shards(a, n):
    return a.reshape(a.shape[0], NDEV, n).transpose(1, 0, 2)


def kernel(x, c, w_ada, b_ada, w_in, b_gate, conv_a, a_log, dt_bias, norm_a, rel_bias, w_branch_a, w_branch_b, w_o, ln1_g, ln1_b, w_up, conv_ffn, b_conv_ffn, w_down, ln2_g, ln2_b, loss_target, m_w_ada, m_b_ada, m_w_in, m_b_gate, m_conv_a, m_a_log, m_dt_bias, m_norm_a, m_rel_bias, m_w_branch_a, m_w_branch_b, m_w_o, m_ln1_g, m_ln1_b, m_w_up, m_conv_ffn, m_b_conv_ffn, m_w_down, m_ln2_g, m_ln2_b, v_w_ada, v_b_ada, v_w_in, v_b_gate, v_conv_a, v_a_log, v_dt_bias, v_norm_a, v_rel_bias, v_w_branch_a, v_w_branch_b, v_w_o, v_ln1_g, v_ln1_b, v_w_up, v_conv_ffn, v_b_conv_ffn, v_w_down, v_ln2_g, v_ln2_b):
    W = dict(w_ada=w_ada, b_ada=b_ada, w_in=w_in, b_gate=b_gate, conv_a=conv_a, a_log=a_log, dt_bias=dt_bias,
             norm_a=norm_a, rel_bias=rel_bias, w_branch_a=w_branch_a, w_branch_b=w_branch_b, w_o=w_o, ln1_g=ln1_g,
             ln1_b=ln1_b, w_up=w_up, conv_ffn=conv_ffn, b_conv_ffn=b_conv_ffn, w_down=w_down, ln2_g=ln2_g,
             ln2_b=ln2_b)
    M = dict(w_ada=m_w_ada, b_ada=m_b_ada, w_in=m_w_in, b_gate=m_b_gate, conv_a=m_conv_a, a_log=m_a_log,
             dt_bias=m_dt_bias, norm_a=m_norm_a, rel_bias=m_rel_bias, w_branch_a=m_w_branch_a,
             w_branch_b=m_w_branch_b, w_o=m_w_o, ln1_g=m_ln1_g, ln1_b=m_ln1_b, w_up=m_w_up, conv_ffn=m_conv_ffn,
             b_conv_ffn=m_b_conv_ffn, w_down=m_w_down, ln2_g=m_ln2_g, ln2_b=m_ln2_b)
    V = dict(w_ada=v_w_ada, b_ada=v_b_ada, w_in=v_w_in, b_gate=v_b_gate, conv_a=v_conv_a, a_log=v_a_log,
             dt_bias=v_dt_bias, norm_a=v_norm_a, rel_bias=v_rel_bias, w_branch_a=v_w_branch_a,
             w_branch_b=v_w_branch_b, w_o=v_w_o, ln1_g=v_ln1_g, ln1_b=v_ln1_b, w_up=v_w_up, conv_ffn=v_conv_ffn,
             b_conv_ffn=v_b_conv_ffn, w_down=v_w_down, ln2_g=v_ln2_g, ln2_b=v_ln2_b)
    W, M, V = ({n: a[0] for n, a in dct.items()} for dct in (W, M, V))
    me = 4 * lax.axis_index("x") + 2 * lax.axis_index("y") + lax.axis_index("c")
    big = ("w_in", "w_up", "w_down", "w_branch_a", "w_branch_b", "w_o")

    (g_in,) = all_gather([W["w_in"].astype(BF16)], "gather_w_in")
    wts = dict(w_in=g_in.transpose(1, 0, 2).reshape(D, -1))
    late, late_tok = exchange_start([W[n].astype(BF16) for n in big[1:]], "gather_late_start", False)

    def late_weights(after):
        g_up, g_down, g_a, g_b, g_o = exchange_wait(late, after, "gather_late_wait")
        return dict(w_up=g_up.transpose(1, 0, 2).reshape(D, -1), w_down=g_down.reshape(DFF, D),
                    w_a=g_a.reshape(D, D), w_b=g_b.reshape(D, D), w_o=g_o.reshape(D, D))

    c_all, sh_all = all_gather([c, _pack_sh({n: W[n] for n, _ in _SH})[None]], "gather_small")
    c_all = c_all.reshape(NDEV, D)
    sh_all = sh_all.reshape(NDEV, SH_LEN)

    def full_small(name, shp):
        o = 0
        for n, s in _SH:
            if n == name:
                break
            o += s[0] * s[1]
        sz = shp[0] * shp[1]
        return sh_all[:, o:o + sz].reshape(NDEV, shp[0], shp[1]).transpose(1, 0, 2).reshape(shp[0], NDEV * shp[1])

    small = dict(conv_a=full_small("conv_a", (4, 384)), rel_bias=full_small("rel_bias", (16, 40)),
                 conv_ffn=full_small("conv_ffn", (3, 704)),
                 b_gate=W["b_gate"][None], a_log=W["a_log"][None], dt_bias=W["dt_bias"][None],
                 norm_a=W["norm_a"][None], ln1_g=W["ln1_g"][None], ln1_b=W["ln1_b"][None],
                 b_conv_ffn=W["b_conv_ffn"][None], ln2_g=W["ln2_g"][None], ln2_b=W["ln2_b"][None])

    nsh = w_ada.shape[2]
    b_sh = lax.dynamic_slice(W["b_ada"][None], (0, me * nsh), (1, nsh))
    mod_sh = ada_fwd(c_all, W["w_ada"], b_sh)
    (mod_rows,) = all_to_all([mod_sh[:, None, :]], "scatter_mod")
    mod6 = mod_rows.reshape(6, D)
    mod6 = mod6 + late_tok
    mod = tuple(mod6[i:i + 1] for i in range(6))

    pending = {}

    def on_grads(group, gd):
        if group == "ffn":
            slabs = [_col_shards(gd["w_up"], w_up.shape[2]), gd["w_down"].reshape(NDEV, -1, D)]
        elif group == "mix":
            slabs = [gd[n].reshape(NDEV, -1, D) for n in ("w_a", "w_b", "w_o")]
        else:
            slabs = [_col_shards(gd["w_in"], w_in.shape[2])]
        pending[group], tok = exchange_start([s.astype(BF16) for s in slabs], "scatter_" + group + "_start", True)
        return tok

    loss, grad_x, dmod, g = local_step(x[0], loss_target[0], mod, wts, small, late_weights, on_grads)

    rep_vals = {n: g[n] for n in REP_NAMES if n != "b_ada"}
    rep_vals["b_ada"] = jnp.concatenate(dmod, axis=1)
    rep_vals["loss"] = loss.reshape(1, 1)
    (rep_all,) = all_gather([_pack_rep(rep_vals)[None]], "gather_small_grads")
    rep_all = rep_all.reshape(NDEV, 1, REP_LEN)
    zero1 = jnp.zeros((1, 1), F32)
    rep_out = adamw(rep_all, _pack_rep({**{n: W[n][None] for n in REP_NAMES}, "loss": zero1}),
                    _pack_rep({**{n: M[n][None] for n in REP_NAMES}, "loss": zero1}),
                    _pack_rep({**{n: V[n][None] for n in REP_NAMES}, "loss": zero1}), "adamw_small")
    loss_total = _unpack_rep(rep_out[0], "loss")[0, 0]

    o_ada = _REP["b_ada"][0]
    dmod_all = rep_all[:, 0, o_ada:o_ada + 6 * D]
    dmod_sh = lax.dynamic_slice(dmod_all, (0, me * nsh), (NDEV, nsh))
    g_w_ada = ada_wgrad(c_all.T, dmod_sh)

    p_up, p_down = exchange_wait(pending["ffn"], grad_x, "scatter_ffn_wait")
    p_a, p_b, p_o = exchange_wait(pending["mix"], grad_x, "scatter_mix_wait")
    (p_in,) = exchange_wait(pending["in"], grad_x, "scatter_in_wait")
    parts = [p_in, p_up, p_down, p_a, p_b, p_o]
    sh_parts = {"conv_a": _col_shards(g["conv_a"], 384), "rel_bias": _col_shards(g["rel_bias"], 40),
                "conv_ffn": _col_shards(g["conv_ffn"], 704)}
    (sh_recv,) = all_to_all([_pack_sh(sh_parts)[:, None, :]], "scatter_small_grads")

    res = {}
    for n, p in zip(big, parts):
        res[n] = adamw(p, W[n], M[n], V[n], "adamw_" + n)
    res["w_ada"] = adamw(g_w_ada[None], W["w_ada"], M["w_ada"], V["w_ada"], "adamw_w_ada")
    sh_out = adamw(sh_recv, _pack_sh({n: W[n] for n, _ in _SH})[None], _pack_sh({n: M[n] for n, _ in _SH})[None],
                   _pack_sh({n: V[n] for n, _ in _SH})[None], "adamw_small_sharded")
    for n, _ in _SH:
        res[n] = tuple(_unpack_sh(o, n) for o in sh_out)
    for n in REP_NAMES:
        res[n] = tuple(_unpack_rep(o, n)[0] for o in rep_out)

    order = ("w_ada", "b_ada", "w_in", "b_gate", "conv_a", "a_log", "dt_bias", "norm_a", "rel_bias", "w_branch_a",
             "w_branch_b", "w_o", "ln1_g", "ln1_b", "w_up", "conv_ffn", "b_conv_ffn", "w_down", "ln2_g", "ln2_b")
    outs = [loss_total, grad_x[None]]
    for kind in range(4):
        outs += [res[n][kind][None] for n in order]
    return tuple(outs)
```

```python
import functools
import math

import numpy as np
import jax
import jax.numpy as jnp
from jax import lax
from jax.experimental import pallas as pl
from jax.experimental.pallas import tpu as pltpu

F32 = jnp.float32
BF16 = jnp.bfloat16
HI = lax.Precision.HIGHEST

D = 1024
CH = 64
AH, ADK = 8, 128
BH, BDH = 16, 64
BPREV = 8
BMAXREL = 256
RELSZ = CH + BMAXREL
DFF = 2816
ALPHA = 2.0 ** 0.25
LN_EPS, RMS_EPS, L2_EPS = 1e-5, 1e-6, 1e-6
NEG = -1e30
LR, B1, B2, AEPS, WD, STEP = 1e-3, 0.9, 0.999, 1e-8, 0.01, 10
NDEV = 8
HALO = 8
LANE = 128
TQ = 512
VMEM_LIMIT = 56 * 1024 * 1024

C_QKVA, C_Z, C_QKVB, C_GATE, C_BA, NCAT = 0, 3072, 4096, 7168, 9216, 9728


def _cparams(n_axes=1, vmem=VMEM_LIMIT):
    return pltpu.CompilerParams(dimension_semantics=("arbitrary",) * n_axes, vmem_limit_bytes=vmem)


def _dg(a, b, ca, cb):
    return lax.dot_general(a.astype(BF16), b.astype(BF16), (((ca,), (cb,)), ((), ())),
                           preferred_element_type=F32)


@jax.custom_vjp
def mm_nn(a, b):
    return _dg(a, b, 1, 0)


@jax.custom_vjp
def mm_nt(a, b):
    return _dg(a, b, 1, 1)


@jax.custom_vjp
def mm_tn(a, b):
    return _dg(a, b, 0, 0)


mm_nn.defvjp(lambda a, b: (mm_nn(a, b), (a, b)),
             lambda r, g: (mm_nt(g, r[1]).astype(r[0].dtype), mm_tn(r[0], g).astype(r[1].dtype)))
mm_nt.defvjp(lambda a, b: (mm_nt(a, b), (a, b)),
             lambda r, g: (mm_nn(g, r[1]).astype(r[0].dtype), mm_tn(g, r[0]).astype(r[1].dtype)))
mm_tn.defvjp(lambda a, b: (mm_tn(a, b), (a, b)),
             lambda r, g: (mm_nt(r[1], g).astype(r[0].dtype), mm_nn(r[0], g).astype(r[1].dtype)))


@jax.custom_vjp
def mm_w(a, w):
    return _dg(a, w, 1, 0)


mm_w.defvjp(lambda a, w: (mm_w(a, w), (a, w)),
            lambda r, g: (mm_nt(g, r[1]).astype(r[0].dtype), jnp.zeros_like(r[1])))


def _mmh(a, b):
    return lax.dot_general(a, b, (((1,), (0,)), ((), ())), precision=HI, preferred_element_type=F32)


def _bdg(a, b, ca, cb):
    return lax.dot_general(a.astype(BF16), b.astype(BF16), (((ca,), (cb,)), ((0,), (0,))),
                           preferred_element_type=F32)


@jax.custom_vjp
def bmm_nn(a, b):
    return _bdg(a, b, 2, 1)


@jax.custom_vjp
def bmm_nt(a, b):
    return _bdg(a, b, 2, 2)


@jax.custom_vjp
def bmm_tn(a, b):
    return _bdg(a, b, 1, 1)


bmm_nn.defvjp(lambda a, b: (bmm_nn(a, b), (a, b)), lambda r, g: (bmm_nt(g, r[1]), bmm_tn(r[0], g)))
bmm_nt.defvjp(lambda a, b: (bmm_nt(a, b), (a, b)), lambda r, g: (bmm_nn(g, r[1]), bmm_tn(g, r[0])))
bmm_tn.defvjp(lambda a, b: (bmm_tn(a, b), (a, b)), lambda r, g: (bmm_nt(r[1], g), bmm_nn(r[0], g)))


def _bdg3(a, b, ca, cb):
    return lax.dot_general(a, b, (((ca,), (cb,)), ((0,), (0,))), precision=HI, preferred_element_type=F32)


@jax.custom_vjp
def bmm3_nn(a, b):
    return _bdg3(a, b, 2, 1)


bmm3_nn.defvjp(lambda a, b: (bmm3_nn(a, b), (a, b)),
               lambda r, g: (_bdg3(g, r[1], 2, 2), _bdg3(r[0], g, 1, 1)))


def _sigmoid(x):
    return 0.5 * jnp.tanh(0.5 * x) + 0.5


def _silu(x):
    return x * _sigmoid(x)


def _softplus(x):
    return jnp.maximum(x, 0.0) + jnp.log(1.0 + jnp.exp(-jnp.abs(x)))


def _layernorm(r, g, b):
    mu = jnp.mean(r, axis=-1, keepdims=True)
    xc = r - mu
    var = jnp.mean(xc * xc, axis=-1, keepdims=True)
    return xc * lax.rsqrt(var + LN_EPS) * g + b


def _iota2(shape, dim):
    return lax.broadcasted_iota(jnp.int32, shape, dim)


@jax.custom_vjp
def causal_conv(ext, rows):
    k = len(rows)
    y = None
    for j in range(k):
        s = k - 1 - j
        r = pltpu.roll(ext, s, 0) if s else ext
        t = r[HALO:] * rows[j]
        y = t if y is None else y + t
    return y


def _causal_conv_fwd(ext, rows):
    return causal_conv(ext, rows), (ext, rows)


def _causal_conv_bwd(res, g):
    ext, rows = res
    n = ext.shape[0]
    k = len(rows)
    gext = jnp.concatenate([jnp.zeros((HALO, g.shape[1]), g.dtype), g], axis=0)
    dext = None
    drows = []
    for j in range(k):
        s = k - 1 - j
        up = pltpu.roll(gext, n - s, 0) if s else gext
        t = up * rows[j]
        dext = t if dext is None else dext + t
        r = pltpu.roll(ext, s, 0) if s else ext
        drows.append(jnp.sum(g * r[HALO:], axis=0, keepdims=True))
    return dext, tuple(drows)


causal_conv.defvjp(_causal_conv_fwd, _causal_conv_bwd)


def _chunk_masks(tm):
    i = _iota2((tm, tm), 0)
    j = _iota2((tm, tm), 1)
    same = (i ^ j) < CH
    lower = jnp.where(same & (j <= i), 1.0, 0.0).astype(F32)
    upper = jnp.where(same & (i <= j), 1.0, 0.0).astype(F32)
    return lower, upper


@jax.custom_vjp
def chunk_cumsum(g):
    lower, _ = _chunk_masks(g.shape[0])
    return _mmh(lower, g)


def _chunk_cumsum_bwd(_, ct):
    _, upper = _chunk_masks(ct.shape[0])
    return (_mmh(upper, ct),)


chunk_cumsum.defvjp(lambda g: (chunk_cumsum(g), None), _chunk_cumsum_bwd)


@jax.custom_vjp
def inv_unit_lower(a):
    n = a.shape[-1]
    eye = jnp.where(_iota2((1, n, n), 1) == _iota2((1, n, n), 2), 1.0, 0.0).astype(F32)
    x = eye - a
    p = _bdg3(a, a, 2, 1)
    steps = int(math.log2(n)) - 1
    for s in range(steps):
        x = x + _bdg3(x, p, 2, 1)
        if s + 1 < steps:
            p = _bdg3(p, p, 2, 1)
    return x


def _inv_fwd(a):
    t = inv_unit_lower(a)
    return t, t


def _inv_bwd(t, g):
    return (-_bdg3(_bdg3(t, g, 1, 1), t, 2, 2),)


inv_unit_lower.defvjp(_inv_fwd, _inv_bwd)


@jax.custom_vjp
def inv_known(a, t):
    return t


inv_known.defvjp(lambda a, t: (t, t), lambda t, g: (_inv_bwd(t, g)[0], jnp.zeros_like(t)))


def prep_head_fn(ext, rows, scale):
    s = _silu(causal_conv(ext, rows))
    if scale is None:
        return s
    return s * (lax.rsqrt(jnp.sum(s * s, axis=-1, keepdims=True) + L2_EPS) * scale)


def prep_gate_fn(bb, aa, a_log, dtb):
    g = -jnp.exp(a_log) * _softplus(aa + dtb)
    return chunk_cumsum(g), _sigmoid(bb)


PREP_SCALES = (ADK ** -0.5, 1.0, None)


def _head_cols(a):
    lane = _iota2((1, LANE), 1)
    return jnp.concatenate([jnp.sum(jnp.where(lane == h, a, 0.0), axis=1, keepdims=True)[None]
                            for h in range(AH)], axis=0)


def _head_rows(a):
    at = a.T[:AH]
    sub = _iota2((AH, 1), 0)
    return jnp.concatenate([jnp.sum(jnp.where(sub == h, at, 0.0), axis=0, keepdims=True)[None]
                            for h in range(AH)], axis=0)


def c1_heads(q, k, v, gcs, beta, tinv_saved=None):
    gcol = _head_cols(gcs)
    grow = _head_rows(gcs)
    bcol = _head_cols(beta)
    i = _iota2((1, CH, CH), 1)
    j = _iota2((1, CH, CH), 2)
    causal = j <= i
    strict = j < i
    diff = gcol - grow
    decay = jnp.where(causal, jnp.exp(jnp.where(causal, diff, 0.0)), 0.0)
    kb = k * bcol
    vb = v * bcol
    a_low = jnp.where(strict, bmm_nt(kb, k) * decay, 0.0)
    tinv = inv_unit_lower(a_low) if tinv_saved is None else inv_known(a_low, tinv_saved)
    egc = jnp.exp(gcol)
    u = bmm3_nn(tinv, vb)
    w = bmm3_nn(tinv, kb * egc)
    qk = jnp.where(causal, bmm_nt(q, k) * decay, 0.0)
    glast = jnp.sum(jnp.where(_iota2((1, CH, 1), 1) == CH - 1, gcol, 0.0), axis=1, keepdims=True)
    qg = q * egc
    kd = k * jnp.exp(glast - gcol)
    eg = jnp.exp(glast) * jnp.ones((1, 1, ADK), F32)
    return u, w, qk, qg, kd, eg, tinv


def c2_heads(s, u, w, qk, qg, kd, eg, z, nw):
    vn = u - bmm_nn(w, s)
    o = bmm_nn(qg, s) + bmm_nn(qk, vn)
    s2 = s * eg + bmm_tn(kd, vn)
    ms = jnp.mean(o * o, axis=-1, keepdims=True)
    og = o * lax.rsqrt(ms + RMS_EPS) * nw * _silu(z)
    return og, s2


def _attn_core_fwd(qh, k, v, bias):
    s = mm_nt(qh, k) * (BDH ** -0.5) + bias
    p = jnp.exp(s - jnp.max(s, axis=-1, keepdims=True))
    inv = 1.0 / jnp.sum(p, axis=-1, keepdims=True)
    o = mm_nn(p, v) * inv
    return o, (qh, k, v, p, inv, o)


def _attn_core_bwd(res, do):
    qh, k, v, p, inv, o = res
    p = p * inv
    dv = mm_tn(p, do)
    dp = mm_nt(do, v)
    ds = p * (dp - jnp.sum(do * o, axis=-1, keepdims=True))
    return mm_nn(ds, k) * (BDH ** -0.5), mm_tn(ds, qh) * (BDH ** -0.5), dv, ds


@jax.custom_vjp
def attn_core(qh, k, v, bias):
    return _attn_core_fwd(qh, k, v, bias)[0]


attn_core.defvjp(_attn_core_fwd, _attn_core_bwd)


def attn_sub(q, k, v, bias2, r, firstf):
    lane = _iota2((1, 2 * BDH), 1)
    col = _iota2((1, KWIN), 1) + r * SUBQ
    nokey = jnp.where(col < TQ, firstf, 0.0) * NEG
    out = None
    for hh in range(2):
        hm = jnp.where((lane >= hh * BDH) & (lane < (hh + 1) * BDH), 1.0, 0.0).astype(F32)
        o = attn_core(q * hm, k, v, assemble_bias(bias2[hh], r) + nokey) * hm
        out = o if out is None else out + o
    return out


def merge_fn(x, oa, ob, gra, grb, p_pa, p_pb, p_mix, bga, bgb, gate_t, g1, b1, scale_f, shift_f,
             wa, wb, wo):
    ga = _sigmoid(gra + bga)
    gb = _sigmoid(grb + bgb)
    pa = mm_w(oa, wa) + p_pa
    pb = mm_w(ob, wb) + p_pb
    merged = ga * pa + gb * pb
    mix = mm_w(merged, wo) + p_mix
    y1 = _layernorm(ALPHA * x + gate_t * mix, g1, b1)
    return y1, merged


def ffn_act_fn(extg, extv, rows_g, rows_v, bg, bv):
    return _silu(causal_conv(extg, rows_g) + bg) * (causal_conv(extv, rows_v) + bv)


def head_fn(a, y1, p_ffn, gate_f, g2, b2, tgt, wd):
    ffn = mm_w(a, wd) + p_ffn
    y2 = _layernorm(ALPHA * y1 + gate_f * ffn, g2, b2)
    err = y2 - tgt
    return 0.5 * jnp.sum(jnp.mean(err * err, axis=-1, keepdims=True))


def _rows(tm, width, colblk=0, order=None):
    if order is None:
        return pl.BlockSpec((tm, width), lambda i: (i, colblk))
    return pl.BlockSpec((tm, width), lambda i: (order(i), colblk))


def _const(shape):
    nd = len(shape)
    return pl.BlockSpec(shape, lambda *_: (0,) * nd)


def _pick(n, cands):
    for c in cands:
        if n % c == 0:
            return c
    raise ValueError(f"no tile for {n}")


def _tile(n, cap):
    best = None
    for c in range(LANE, min(n, cap) + 1, LANE):
        if n % c == 0:
            best = c
    if best is None:
        raise ValueError(f"no tile for {n}")
    return best


def _onehot_rows(k, j):
    return jnp.where(_iota2((k, 1), 0) == j, 1.0, 0.0).astype(F32)


def _stack_rows(drows):
    k = len(drows)
    out = None
    for j in range(k):
        tj = _onehot_rows(k, j) * drows[j]
        out = tj if out is None else out + tj
    return out


def matmul(a, w, out_dtype, name, ta=False, tb=False):
    kdim, m = a.shape if ta else a.shape[::-1]
    n = w.shape[0] if tb else w.shape[1]
    tm = _tile(m, 2048 if kdim <= 1024 else 1024)
    tn = _tile(n, 1024)
    tk = _tile(kdim, 2560)
    nk = kdim // tk
    a_spec = (pl.BlockSpec((tk, tm), lambda i, j, k: (k, i)) if ta
              else pl.BlockSpec((tm, tk), lambda i, j, k: (i, k)))
    w_spec = (pl.BlockSpec((tn, tk), lambda i, j, k: (j, k)) if tb
              else pl.BlockSpec((tk, tn), lambda i, j, k: (k, j)))

    def body(a_ref, w_ref, o_ref, *scratch):
        p = _dg(a_ref[...], w_ref[...], 0 if ta else 1, 1 if tb else 0)
        if nk == 1:
            o_ref[...] = p.astype(out_dtype)
            return
        acc = scratch[0]
        k = pl.program_id(2)

        @pl.when(k == 0)
        def _():
            acc[...] = p

        @pl.when(k > 0)
        def _():
            acc[...] += p

        @pl.when(k == nk - 1)
        def _():
            o_ref[...] = acc[...].astype(out_dtype)

    return pl.pallas_call(
        body, name=name,
        grid=(m // tm, n // tn, nk),
        in_specs=[a_spec, w_spec],
        out_specs=pl.BlockSpec((tm, tn), lambda i, j, k: (i, j)),
        out_shape=jax.ShapeDtypeStruct((m, n), out_dtype),
        scratch_shapes=[] if nk == 1 else [pltpu.VMEM((tm, tn), F32)],
        compiler_params=_cparams(3),
    )(a, w)


def modulate(x, scale, shift, name):
    t, d = x.shape
    tm = _pick(t, (512, 256, 128))

    def body(x_ref, sc_ref, sh_ref, o_ref):
        o_ref[...] = (x_ref[...] * (1.0 + sc_ref[...]) + sh_ref[...]).astype(BF16)

    return pl.pallas_call(
        body, name=name, grid=(t // tm,),
        in_specs=[_rows(tm, d), _const((1, d)), _const((1, d))],
        out_specs=_rows(tm, d),
        out_shape=jax.ShapeDtypeStruct((t, d), BF16),
        compiler_params=_cparams(),
    )(x, scale, shift)


def modulate_bwd(dh, xin, dres, scale, name):
    t, d = dh.shape
    tm = _pick(t, (512, 256, 128))

    def body(dh_ref, x_ref, dr_ref, sc_ref, o_ref, dsc_ref, dsh_ref):
        i = pl.program_id(0)
        dh_v = dh_ref[...]
        o_ref[...] = dr_ref[...] + dh_v * (1.0 + sc_ref[...])

        @pl.when(i == 0)
        def _():
            dsc_ref[...] = jnp.zeros_like(dsc_ref)
            dsh_ref[...] = jnp.zeros_like(dsh_ref)

        dsc_ref[...] += jnp.sum(dh_v * x_ref[...], axis=0, keepdims=True)
        dsh_ref[...] += jnp.sum(dh_v, axis=0, keepdims=True)

    return pl.pallas_call(
        body, name=name, grid=(t // tm,),
        in_specs=[_rows(tm, d), _rows(tm, d), _rows(tm, d), _const((1, d))],
        out_specs=[_rows(tm, d), _const((1, d)), _const((1, d))],
        out_shape=[jax.ShapeDtypeStruct((t, d), F32), jax.ShapeDtypeStruct((1, d), F32),
                   jax.ShapeDtypeStruct((1, d), F32)],
        compiler_params=_cparams(),
    )(dh, xin, dres, scale)


PREP_TM = 128


def _halo_specs(tm, width, colblk, order):
    per = tm // HALO
    return [pl.BlockSpec((HALO, width), lambda i: (jnp.maximum(order(i) * per - 1, 0), colblk)),
            pl.BlockSpec((tm, width), lambda i: (order(i), colblk))]


def prep_fwd(proj, conv_a, a_log, dtb):
    t = proj.shape[0]
    tm = PREP_TM
    nt = t // tm
    wq = 3 * D

    def body(prev_ref, cur_ref, bb_ref, aa_ref, cw_ref, al_ref, dt_ref, q_ref, k_ref, v_ref, g_ref, b_ref):
        i = pl.program_id(0)
        flag = jnp.where(i > 0, 1.0, 0.0)
        for part, o_ref in enumerate((q_ref, k_ref, v_ref)):
            for h in range(AH):
                sl = slice(part * D + h * ADK, part * D + (h + 1) * ADK)
                ext = jnp.concatenate([prev_ref[:, sl] * flag, cur_ref[:, sl]], axis=0)
                rows = tuple(cw_ref[j:j + 1, sl] for j in range(4))
                o_ref[h] = prep_head_fn(ext, rows, PREP_SCALES[part])
        gcs, beta = prep_gate_fn(bb_ref[...], aa_ref[...], al_ref[...], dt_ref[...])
        g_ref[...] = gcs
        b_ref[...] = beta

    ident = lambda i: i
    hm = pl.BlockSpec((AH, tm, ADK), lambda i: (0, i, 0))
    return pl.pallas_call(
        body, name="prep_fwd", grid=(nt,),
        in_specs=_halo_specs(tm, wq, 0, ident) + [
            _rows(tm, 128, C_BA // 128), _rows(tm, 128, C_BA // 128 + 1),
            _const((4, wq)), _const((1, 128)), _const((1, 128))],
        out_specs=[hm, hm, hm, _rows(tm, 128), _rows(tm, 128)],
        out_shape=[jax.ShapeDtypeStruct((AH, t, ADK), F32)] * 3 + [jax.ShapeDtypeStruct((t, 128), F32)] * 2,
        compiler_params=_cparams(),
    )(proj, proj, proj, proj, conv_a, a_log, dtb)


def prep_bwd(proj, conv_a, a_log, dtb, dq, dk, dv, dgcs, dbeta):
    t = proj.shape[0]
    tm = PREP_TM
    nt = t // tm
    wq = 3 * D
    rev = lambda i: nt - 1 - i

    def body(prev_ref, cur_ref, bb_ref, aa_ref, cw_ref, al_ref, dt_ref,
             dq_ref, dk_ref, dv_ref, dg_ref, db_ref,
             dpre_ref, dbb_ref, daa_ref, dcw_ref, dal_ref, ddt_ref, carry):
        i = pl.program_id(0)
        flag = jnp.where(i < nt - 1, 1.0, 0.0)

        @pl.when(i == 0)
        def _():
            carry[...] = jnp.zeros_like(carry)
            dcw_ref[...] = jnp.zeros_like(dcw_ref)
            dal_ref[...] = jnp.zeros_like(dal_ref)
            ddt_ref[...] = jnp.zeros_like(ddt_ref)

        for part, d_ref in enumerate((dq_ref, dk_ref, dv_ref)):
            for h in range(AH):
                sl = slice(part * D + h * ADK, part * D + (h + 1) * ADK)
                ext = jnp.concatenate([prev_ref[:, sl] * flag, cur_ref[:, sl]], axis=0)
                rows = tuple(cw_ref[j:j + 1, sl] for j in range(4))
                _, vjp = jax.vjp(lambda e, r: prep_head_fn(e, r, PREP_SCALES[part]), ext, rows)
                dext, drows = vjp(d_ref[h])
                dcur = dext[HALO:]
                dpre_ref[:, sl] = jnp.concatenate([dcur[:tm - HALO], dcur[tm - HALO:] + carry[:, sl]],
                                                  axis=0).astype(BF16)
                carry[:, sl] = dext[:HALO]
                dcw_ref[:, sl] += _stack_rows(drows)
        _, vjp = jax.vjp(prep_gate_fn, bb_ref[...], aa_ref[...], al_ref[...], dt_ref[...])
        dbb, daa, dal, ddt = vjp((dg_ref[...], db_ref[...]))
        dbb_ref[...] = dbb.astype(BF16)
        daa_ref[...] = daa.astype(BF16)
        dal_ref[...] += dal
        ddt_ref[...] += ddt

    hm = pl.BlockSpec((AH, tm, ADK), lambda i: (0, rev(i), 0))
    return pl.pallas_call(
        body, name="prep_bwd", grid=(nt,),
        in_specs=_halo_specs(tm, wq, 0, rev) + [
            _rows(tm, 128, C_BA // 128, rev), _rows(tm, 128, C_BA // 128 + 1, rev),
            _const((4, wq)), _const((1, 128)), _const((1, 128)),
            hm, hm, hm, _rows(tm, 128, 0, rev), _rows(tm, 128, 0, rev)],
        out_specs=[_rows(tm, wq, 0, rev), _rows(tm, 128, 0, rev), _rows(tm, 128, 0, rev),
                   _const((4, wq)), _const((1, 128)), _const((1, 128))],
        out_shape=[jax.ShapeDtypeStruct((t, wq), BF16), jax.ShapeDtypeStruct((t, 128), BF16),
                   jax.ShapeDtypeStruct((t, 128), BF16), jax.ShapeDtypeStruct((4, wq), F32),
                   jax.ShapeDtypeStruct((1, 128), F32), jax.ShapeDtypeStruct((1, 128), F32)],
        scratch_shapes=[pltpu.VMEM((HALO, wq), F32)],
        compiler_params=_cparams(),
    )(proj, proj, proj, proj, conv_a, a_log, dtb, dq, dk, dv, dgcs, dbeta)


def _c1_specs(order):
    hm = pl.BlockSpec((AH, CH, ADK), lambda n: (0, order(n), 0))
    col = pl.BlockSpec((CH, LANE), lambda n: (order(n), 0))
    qk = pl.BlockSpec((1, AH, CH, CH), lambda n: (order(n), 0, 0, 0))
    eg = pl.BlockSpec((1, AH, 1, ADK), lambda n: (order(n), 0, 0, 0))
    return hm, col, qk, eg


def _heads(ref):
    return jnp.stack([ref[:, h * ADK:(h + 1) * ADK] for h in range(AH)], axis=0)


def c1_fwd(q, k, v, gcs, beta):
    t = q.shape[1]
    nc = t // CH
    hm, col, qks, egs = _c1_specs(lambda n: n)

    def body(q_ref, k_ref, v_ref, g_ref, b_ref, u_ref, w_ref, qg_ref, kd_ref, qk_ref, eg_ref, ti_ref):
        u, w, qk, qg, kd, eg, tinv = c1_heads(q_ref[...], k_ref[...], v_ref[...], g_ref[...], b_ref[...])
        u_ref[...] = u
        w_ref[...] = w
        qg_ref[...] = qg
        kd_ref[...] = kd
        qk_ref[0] = qk
        eg_ref[0] = eg
        ti_ref[0] = tinv

    return pl.pallas_call(
        body, name="c1_fwd", grid=(nc,),
        in_specs=[hm, hm, hm, col, col],
        out_specs=[hm, hm, hm, hm, qks, egs, qks],
        out_shape=[jax.ShapeDtypeStruct((AH, t, ADK), F32)] * 4 + [
            jax.ShapeDtypeStruct((nc, AH, CH, CH), F32), jax.ShapeDtypeStruct((nc, AH, 1, ADK), F32),
            jax.ShapeDtypeStruct((nc, AH, CH, CH), F32)],
        compiler_params=_cparams(),
    )(q, k, v, gcs, beta)


def c1_bwd(q, k, v, gcs, beta, tinv, du, dw, dqg, dkd, dqk, deg):
    t = q.shape[1]
    nc = t // CH
    hm, col, qks, egs = _c1_specs(lambda n: n)

    def body(q_ref, k_ref, v_ref, g_ref, b_ref, ti_ref, du_ref, dw_ref, dqg_ref, dkd_ref, dqk_ref, deg_ref,
             dq_ref, dk_ref, dv_ref, dg_ref, db_ref):
        _, vjp = jax.vjp(lambda q_, k_, v_, g_, b_: c1_heads(q_, k_, v_, g_, b_, ti_ref[0]),
                         q_ref[...], k_ref[...], v_ref[...], g_ref[...], b_ref[...])
        dq, dk, dv, dg, db = vjp((du_ref[...], dw_ref[...], dqk_ref[0], dqg_ref[...], dkd_ref[...], deg_ref[0],
                                  jnp.zeros((AH, CH, CH), F32)))
        dq_ref[...] = dq
        dk_ref[...] = dk
        dv_ref[...] = dv
        dg_ref[...] = dg
        db_ref[...] = db

    return pl.pallas_call(
        body, name="c1_bwd", grid=(nc,),
        in_specs=[hm, hm, hm, col, col, qks, hm, hm, hm, hm, qks, egs],
        out_specs=[hm, hm, hm, col, col],
        out_shape=[jax.ShapeDtypeStruct((AH, t, ADK), F32)] * 3 + [jax.ShapeDtypeStruct((t, LANE), F32)] * 2,
        compiler_params=_cparams(),
    )(q, k, v, gcs, beta, tinv, du, dw, dqg, dkd, dqk, deg)


def c2_fwd(u, w, qg, kd, qk, eg, proj, norm_a):
    t = u.shape[1]
    nc = t // CH
    hm, _, qks, egs = _c1_specs(lambda n: n)
    tok = pl.BlockSpec((CH, D), lambda n: (n, 0))
    zspec = pl.BlockSpec((CH, D), lambda n: (n, C_Z // D))
    sspec = pl.BlockSpec((1, AH, ADK, ADK), lambda n: (n, 0, 0, 0))

    def body(u_ref, w_ref, qg_ref, kd_ref, qk_ref, eg_ref, z_ref, nw_ref, o_ref, sall_ref, st):
        n = pl.program_id(0)

        @pl.when(n == 0)
        def _():
            st[...] = jnp.zeros_like(st)

        s = st[...]
        sall_ref[0] = s
        og, s2 = c2_heads(s, u_ref[...], w_ref[...], qk_ref[0], qg_ref[...], kd_ref[...], eg_ref[0],
                          _heads(z_ref), nw_ref[...])
        st[...] = s2
        for h in range(AH):
            o_ref[:, h * ADK:(h + 1) * ADK] = og[h].astype(BF16)

    return pl.pallas_call(
        body, name="c2_fwd", grid=(nc,),
        in_specs=[hm, hm, hm, hm, qks, egs, zspec, _const((1, ADK))],
        out_specs=[tok, sspec],
        out_shape=[jax.ShapeDtypeStruct((t, D), BF16), jax.ShapeDtypeStruct((nc, AH, ADK, ADK), F32)],
        scratch_shapes=[pltpu.VMEM((AH, ADK, ADK), F32)],
        compiler_params=_cparams(),
    )(u, w, qg, kd, qk, eg, proj, norm_a)


def c2_bwd(u, w, qg, kd, qk, eg, proj, norm_a, sall, do):
    t = u.shape[1]
    nc = t // CH
    rev = lambda n: nc - 1 - n
    hm, _, qks, egs = _c1_specs(rev)
    tok = pl.BlockSpec((CH, D), lambda n: (rev(n), 0))
    zspec = pl.BlockSpec((CH, D), lambda n: (rev(n), C_Z // D))
    sspec = pl.BlockSpec((1, AH, ADK, ADK), lambda n: (rev(n), 0, 0, 0))

    def body(u_ref, w_ref, qg_ref, kd_ref, qk_ref, eg_ref, z_ref, nw_ref, sall_ref, do_ref,
             du_ref, dw_ref, dqg_ref, dkd_ref, dqk_ref, deg_ref, dz_ref, dnw_ref, dst):
        n = pl.program_id(0)

        @pl.when(n == 0)
        def _():
            dst[...] = jnp.zeros_like(dst)
            dnw_ref[...] = jnp.zeros_like(dnw_ref)

        _, vjp = jax.vjp(c2_heads, sall_ref[0], u_ref[...], w_ref[...], qk_ref[0], qg_ref[...], kd_ref[...],
                         eg_ref[0], _heads(z_ref), nw_ref[...])
        ds, du, dw, dqk, dqg, dkd, deg, dz, dn = vjp((_heads(do_ref), dst[...]))
        dst[...] = ds
        du_ref[...] = du
        dw_ref[...] = dw
        dqg_ref[...] = dqg
        dkd_ref[...] = dkd
        dqk_ref[0] = dqk
        deg_ref[0] = deg
        for h in range(AH):
            dz_ref[:, h * ADK:(h + 1) * ADK] = dz[h].astype(BF16)
        dnw_ref[...] += dn

    return pl.pallas_call(
        body, name="c2_bwd", grid=(nc,),
        in_specs=[hm, hm, hm, hm, qks, egs, zspec, _const((1, ADK)), sspec, tok],
        out_specs=[hm, hm, hm, hm, qks, egs, tok, _const((1, ADK))],
        out_shape=[jax.ShapeDtypeStruct((AH, t, ADK), F32)] * 4 + [
            jax.ShapeDtypeStruct((nc, AH, CH, CH), F32), jax.ShapeDtypeStruct((nc, AH, 1, ADK), F32),
            jax.ShapeDtypeStruct((t, D), BF16), jax.ShapeDtypeStruct((1, ADK), F32)],
        scratch_shapes=[pltpu.VMEM((AH, ADK, ADK), F32)],
        compiler_params=_cparams(),
    )(u, w, qg, kd, qk, eg, proj, norm_a, sall, do)


NQB = TQ // CH
NKB = 2 * TQ // CH
NDIST = BPREV + 1
KLO = -(NQB - 2)
NPAIR = NKB - 1 - KLO + 1


def bias_table(rel_bias):
    nh = rel_bias.shape[0]
    relx = jnp.concatenate([rel_bias, jnp.broadcast_to(rel_bias[:, -1:], (nh, CH * BPREV + 2 * CH - 1 - RELSZ))],
                           axis=1)
    t = jnp.stack([relx[:, CH * k:CH * k + 2 * CH - 1] for k in range(NDIST)], axis=1)
    trev = t[:, :, ::-1]
    g2 = jnp.concatenate([trev[:, :, CH - 1:], jnp.zeros((nh, NDIST, 1), F32), trev[:, :, :CH - 1]], axis=2)
    flat = jnp.tile(g2, (1, 1, CH + 1))[:, :, :CH * (2 * CH - 1)]
    blk = flat.reshape(nh, NDIST, CH, 2 * CH - 1)[..., :CH]
    neg = jnp.full((nh, NQB - 1, CH, CH), NEG, F32)
    asc = jnp.concatenate([neg, blk, neg], axis=1)
    return jnp.concatenate([asc[:, 1:], asc[:, :-1]], axis=-1)


SUBQ = 4 * CH
NSUB = TQ // SUBQ
KWIN = SUBQ + BPREV * CH


def assemble_bias(tab, r):
    b0 = r * SUBQ // (2 * CH)
    rows = [jnp.concatenate([tab[NQB + a - 2 * b - KLO] for b in range(b0, b0 + KWIN // (2 * CH))], axis=1)
            for a in range(r * SUBQ // CH, (r + 1) * SUBQ // CH)]
    return jnp.concatenate(rows, axis=0)


def bias_table_bwd_layout(dtab):
    nh = dtab.shape[0]
    dasc = (jnp.pad(dtab[..., :CH], ((0, 0), (1, 0), (0, 0), (0, 0)))
            + jnp.pad(dtab[..., CH:], ((0, 0), (0, 1), (0, 0), (0, 0))))
    dblk = dasc[:, NQB - 1:NQB - 1 + NDIST]
    dr = jnp.pad(dblk, ((0, 0), (0, 0), (0, 0), (0, CH - 1)))
    flat = jnp.pad(dr.reshape(nh, NDIST, CH * (2 * CH - 1)), ((0, 0), (0, 0), (0, 3 * CH)))
    return flat.reshape(nh, NDIST, CH + 1, 2 * CH).transpose(0, 2, 1, 3).reshape(nh, CH + 1, NDIST * 2 * CH)


def _fold_matrix_np():
    f = np.zeros((NDIST * 2 * CH, 384), np.float32)
    for k in range(NDIST):
        s = k
        for xx in range(2 * CH):
            if xx == CH:
                continue
            m = CH - 1 - xx if xx < CH else 3 * CH - 1 - xx
            f[s * 2 * CH + xx, min(CH * k + m, RELSZ - 1)] = 1.0
    return f


def relbias_reduce(dlay):
    nh, rows, cols = dlay.shape
    rpad = (-rows) % 8
    dlay = jnp.pad(dlay, ((0, 0), (0, rpad), (0, 0)))
    fold = jnp.asarray(_fold_matrix_np())

    def body(d_ref, f_ref, o_ref):
        cs = jnp.sum(d_ref[0], axis=0, keepdims=True)
        o_ref[0] = _mmh(jnp.broadcast_to(cs, (8, cols)), f_ref[...])

    out = pl.pallas_call(
        body, name="relbias_reduce", grid=(nh,),
        in_specs=[pl.BlockSpec((1, rows + rpad, cols), lambda h: (h, 0, 0)), _const((cols, 384))],
        out_specs=pl.BlockSpec((1, 8, 384), lambda h: (h, 0, 0)),
        out_shape=jax.ShapeDtypeStruct((nh, 8, 384), F32),
        compiler_params=_cparams(),
    )(dlay, fold)
    return out[:, 0, :RELSZ]


def attn_fwd(proj, bias):
    t = proj.shape[0]
    nt = t // TQ
    cb = C_QKVB // 128

    def body(q_ref, kp_ref, kc_ref, vp_ref, vc_ref, b_ref, o_ref):
        i = pl.program_id(1)
        firstf = jnp.where(i == 0, 1.0, 0.0)
        for r in range(NSUB):
            lo, hi = r * SUBQ, r * SUBQ + KWIN - TQ
            kw = jnp.concatenate([kp_ref[lo:, :], kc_ref[:hi, :]], axis=0)
            vw = jnp.concatenate([vp_ref[lo:, :], vc_ref[:hi, :]], axis=0)
            o_ref[lo:lo + SUBQ, :] = attn_sub(q_ref[lo:lo + SUBQ, :], kw, vw, b_ref[...], r, firstf).astype(BF16)

    def blk(off, prev):
        if prev:
            return pl.BlockSpec((TQ, 128), lambda p, i: (jnp.maximum(i - 1, 0), cb + off + p))
        return pl.BlockSpec((TQ, 128), lambda p, i: (i, cb + off + p))

    return pl.pallas_call(
        body, name="attn_fwd", grid=(BH // 2, nt),
        in_specs=[blk(0, False), blk(8, True), blk(8, False), blk(16, True), blk(16, False),
                  pl.BlockSpec((2, NPAIR, CH, 2 * CH), lambda p, i: (p, 0, 0, 0))],
        out_specs=pl.BlockSpec((TQ, 128), lambda p, i: (i, p)),
        out_shape=jax.ShapeDtypeStruct((t, D), BF16),
        compiler_params=_cparams(2),
    )(proj, proj, proj, proj, proj, bias)


def attn_bwd(proj, bias, do):
    t = proj.shape[0]
    nt = t // TQ
    cb = C_QKVB // 128

    def body(q_ref, kp_ref, kc_ref, vp_ref, vc_ref, b_ref, do_ref,
             dq_ref, dk_ref, dv_ref, db_ref, ck, cv, ak, av):
        i = pl.program_id(1)

        @pl.when(i == 0)
        def _():
            ck[...] = jnp.zeros_like(ck)
            cv[...] = jnp.zeros_like(cv)
            db_ref[...] = jnp.zeros_like(db_ref)

        @pl.when(i < nt)
        def _():
            firstf = jnp.where(i == 0, 1.0, 0.0)
            ak[...] = jnp.zeros_like(ak)
            av[...] = jnp.zeros_like(av)
            db = None
            for r in range(NSUB):
                lo, hi = r * SUBQ, r * SUBQ + KWIN - TQ
                kw = jnp.concatenate([kp_ref[lo:, :], kc_ref[:hi, :]], axis=0).astype(F32)
                vw = jnp.concatenate([vp_ref[lo:, :], vc_ref[:hi, :]], axis=0).astype(F32)
                _, vjp = jax.vjp(lambda q, k, v, b: attn_sub(q, k, v, b, r, firstf),
                                 q_ref[lo:lo + SUBQ, :].astype(F32), kw, vw, b_ref[...])
                dq, dkw, dvw, dbr = vjp(do_ref[lo:lo + SUBQ, :])
                dq_ref[lo:lo + SUBQ, :] = dq.astype(BF16)
                ak[lo:lo + KWIN, :] += dkw
                av[lo:lo + KWIN, :] += dvw
                db = dbr if db is None else db + dbr
            dk_ref[...] = (ck[...] + ak[:TQ, :]).astype(BF16)
            dv_ref[...] = (cv[...] + av[:TQ, :]).astype(BF16)
            ck[...] = ak[TQ:, :]
            cv[...] = av[TQ:, :]
            db_ref[...] += db

        @pl.when(i == nt)
        def _():
            dk_ref[...] = ck[...].astype(BF16)
            dv_ref[...] = cv[...].astype(BF16)

    def blk(off, prev):
        if prev:
            return pl.BlockSpec((TQ, 128), lambda p, i: (jnp.clip(i - 1, 0, nt - 1), cb + off + p))
        return pl.BlockSpec((TQ, 128), lambda p, i: (jnp.minimum(i, nt - 1), cb + off + p))

    own = pl.BlockSpec((TQ, 128), lambda p, i: (jnp.minimum(i, nt - 1), p))
    lag = pl.BlockSpec((TQ, 128), lambda p, i: (jnp.maximum(i - 1, 0), p))
    return pl.pallas_call(
        body, name="attn_bwd", grid=(BH // 2, nt + 1),
        in_specs=[blk(0, False), blk(8, True), blk(8, False), blk(16, True), blk(16, False),
                  pl.BlockSpec((2, NPAIR, CH, 2 * CH), lambda p, i: (p, 0, 0, 0)), own],
        out_specs=[own, lag, lag, pl.BlockSpec((2, NPAIR, CH, 2 * CH), lambda p, i: (p, 0, 0, 0))],
        out_shape=[jax.ShapeDtypeStruct((t, D), BF16)] * 3 + [jax.ShapeDtypeStruct((BH, NPAIR, CH, 2 * CH), F32)],
        scratch_shapes=[pltpu.VMEM((TQ, 128), F32), pltpu.VMEM((TQ, 128), F32),
                        pltpu.VMEM((2 * TQ, 128), F32), pltpu.VMEM((2 * TQ, 128), F32)],
        compiler_params=_cparams(2),
    )(proj, proj, proj, proj, proj, bias, do)


MERGE_TM = 256


def merge_fwd(x, oa, ob, proj, vecs, wa, wb, wo):
    t = x.shape[0]
    tm = MERGE_TM
    names = ("bga", "bgb", "gate_t", "g1", "b1", "scale_f", "shift_f")

    def body(x_ref, oa_ref, ob_ref, gra_ref, grb_ref, *rest):
        vrefs = rest[:7]
        wa_ref, wb_ref, wo_ref, y_ref, h_ref = rest[7:]
        vv = [r[...] for r in vrefs]
        zero = jnp.zeros((tm, D), F32)
        y1, _ = merge_fn(x_ref[...], oa_ref[...], ob_ref[...], gra_ref[...], grb_ref[...], zero, zero, zero,
                         *vv, wa_ref[...], wb_ref[...], wo_ref[...])
        y_ref[...] = y1
        h_ref[...] = (y1 * (1.0 + vv[5]) + vv[6]).astype(BF16)

    return pl.pallas_call(
        body, name="merge_fwd", grid=(t // tm,),
        in_specs=[_rows(tm, D), _rows(tm, D), _rows(tm, D), _rows(tm, D, C_GATE // D), _rows(tm, D, C_GATE // D + 1)]
        + [_const((1, D))] * 7 + [_const((D, D))] * 3,
        out_specs=[_rows(tm, D), _rows(tm, D)],
        out_shape=[jax.ShapeDtypeStruct((t, D), F32), jax.ShapeDtypeStruct((t, D), BF16)],
        compiler_params=_cparams(),
    )(x, oa, ob, proj, proj, *[vecs[n] for n in names], wa, wb, wo)


def merge_bwd(x, oa, ob, proj, vecs, wa, wb, wo, dy1):
    t = x.shape[0]
    tm = MERGE_TM
    names = ("bga", "bgb", "gate_t", "g1", "b1", "scale_f", "shift_f")

    def body(x_ref, oa_ref, ob_ref, gra_ref, grb_ref, *rest):
        vrefs = rest[:7]
        wa_ref, wb_ref, wo_ref, dy_ref = rest[7:11]
        (dx_ref, doa_ref, dob_ref, dga_ref, dgb_ref, mg_ref, dmix_ref, dpa_ref, dpb_ref,
         dbga_ref, dbgb_ref, dgt_ref, dg1_ref, db1_ref) = rest[11:]
        i = pl.program_id(0)
        vv = [r[...] for r in vrefs]
        zero = jnp.zeros((tm, D), F32)

        def f(x_, oa_, ob_, gra_, grb_, ppa, ppb, pmix, bga, bgb, gate_t, g1, b1):
            return merge_fn(x_, oa_, ob_, gra_, grb_, ppa, ppb, pmix, bga, bgb, gate_t, g1, b1, vv[5], vv[6],
                            wa_ref[...], wb_ref[...], wo_ref[...])

        _, vjp, merged = jax.vjp(f, x_ref[...], oa_ref[...].astype(F32), ob_ref[...].astype(F32),
                                 gra_ref[...], grb_ref[...], zero, zero, zero, *vv[:5], has_aux=True)
        dx, doa, dob, dga, dgb, dpa, dpb, dmix, dbga, dbgb, dgt, dg1, db1 = vjp(dy_ref[...])
        dx_ref[...] = dx
        doa_ref[...] = doa
        dob_ref[...] = dob
        dga_ref[...] = dga.astype(BF16)
        dgb_ref[...] = dgb.astype(BF16)
        mg_ref[...] = merged.astype(BF16)
        dmix_ref[...] = dmix.astype(BF16)
        dpa_ref[...] = dpa.astype(BF16)
        dpb_ref[...] = dpb.astype(BF16)
        accs = (dbga_ref, dbgb_ref, dgt_ref, dg1_ref, db1_ref)

        @pl.when(i == 0)
        def _():
            for a in accs:
                a[...] = jnp.zeros_like(a)

        for a, val in zip(accs, (dbga, dbgb, dgt, dg1, db1)):
            a[...] += val

    return pl.pallas_call(
        body, name="merge_bwd", grid=(t // tm,),
        in_specs=[_rows(tm, D), _rows(tm, D), _rows(tm, D), _rows(tm, D, C_GATE // D), _rows(tm, D, C_GATE // D + 1)]
        + [_const((1, D))] * 7 + [_const((D, D))] * 3 + [_rows(tm, D)],
        out_specs=[_rows(tm, D)] * 9 + [_const((1, D))] * 5,
        out_shape=[jax.ShapeDtypeStruct((t, D), F32)] * 3 + [jax.ShapeDtypeStruct((t, D), BF16)] * 6
        + [jax.ShapeDtypeStruct((1, D), F32)] * 5,
        compiler_params=_cparams(),
    )(x, oa, ob, proj, proj, *[vecs[n] for n in names], wa, wb, wo, dy1)


FFN_TM = 128


def ffn_act_fwd(up, conv_w, bconv):
    t, wdt = up.shape
    tm = FFN_TM

    def body(prev_ref, cur_ref, cw_ref, bc_ref, a_ref):
        i = pl.program_id(0)
        flag = jnp.where(i > 0, 1.0, 0.0)

        def ext(sl):
            return jnp.concatenate([prev_ref[:, sl] * flag, cur_ref[:, sl]], axis=0)

        def rows(sl):
            return tuple(cw_ref[j:j + 1, sl] for j in range(3))

        for cb in range(DFF // LANE):
            g = slice(cb * LANE, (cb + 1) * LANE)
            v = slice(DFF + cb * LANE, DFF + (cb + 1) * LANE)
            a_ref[:, g] = ffn_act_fn(ext(g), ext(v), rows(g), rows(v), bc_ref[:, g], bc_ref[:, v]).astype(BF16)

    return pl.pallas_call(
        body, name="ffn_act_fwd", grid=(t // tm,),
        in_specs=_halo_specs(tm, wdt, 0, lambda i: i) + [_const((3, wdt)), _const((1, wdt))],
        out_specs=_rows(tm, DFF),
        out_shape=jax.ShapeDtypeStruct((t, DFF), BF16),
        compiler_params=_cparams(),
    )(up, up, conv_w, bconv)


def ffn_act_bwd(up, conv_w, bconv, da):
    t, wdt = up.shape
    tm = FFN_TM
    nt = t // tm
    rev = lambda i: nt - 1 - i

    def body(prev_ref, cur_ref, cw_ref, bc_ref, da_ref, dup_ref, dcw_ref, dbc_ref, carry):
        i = pl.program_id(0)
        flag = jnp.where(i < nt - 1, 1.0, 0.0)

        @pl.when(i == 0)
        def _():
            carry[...] = jnp.zeros_like(carry)
            dcw_ref[...] = jnp.zeros_like(dcw_ref)
            dbc_ref[...] = jnp.zeros_like(dbc_ref)

        def ext(sl):
            return jnp.concatenate([prev_ref[:, sl] * flag, cur_ref[:, sl]], axis=0)

        def rows(sl):
            return tuple(cw_ref[j:j + 1, sl] for j in range(3))

        def emit(sl, dext, drows, dbc):
            dcur = dext[HALO:]
            dup_ref[:, sl] = jnp.concatenate([dcur[:tm - HALO], dcur[tm - HALO:] + carry[:, sl]], axis=0).astype(BF16)
            carry[:, sl] = dext[:HALO]
            dcw_ref[:, sl] += _stack_rows(drows)
            dbc_ref[:, sl] += dbc

        for cb in range(DFF // LANE):
            g = slice(cb * LANE, (cb + 1) * LANE)
            v = slice(DFF + cb * LANE, DFF + (cb + 1) * LANE)
            _, vjp = jax.vjp(ffn_act_fn, ext(g), ext(v), rows(g), rows(v), bc_ref[:, g], bc_ref[:, v])
            dxg, dxv, drg, drv, dbg, dbv = vjp(da_ref[:, g])
            emit(g, dxg, drg, dbg)
            emit(v, dxv, drv, dbv)

    return pl.pallas_call(
        body, name="ffn_act_bwd", grid=(nt,),
        in_specs=_halo_specs(tm, wdt, 0, rev) + [_const((3, wdt)), _const((1, wdt)), _rows(tm, DFF, 0, rev)],
        out_specs=[_rows(tm, wdt, 0, rev), _const((3, wdt)), _const((1, wdt))],
        out_shape=[jax.ShapeDtypeStruct((t, wdt), BF16), jax.ShapeDtypeStruct((3, wdt), F32),
                   jax.ShapeDtypeStruct((1, wdt), F32)],
        scratch_shapes=[pltpu.VMEM((HALO, wdt), F32)],
        compiler_params=_cparams(),
    )(up, up, conv_w, bconv, da)


HEAD_TM = 256


def head_fwd_bwd(a, y1, tgt, gate_f, g2, b2, wd):
    t = a.shape[0]
    tm = HEAD_TM

    def body(a_ref, y_ref, t_ref, gf_ref, g2_ref, b2_ref, wd_ref,
             da_ref, dy_ref, dffn_ref, dgf_ref, dg2_ref, db2_ref, loss_ref):
        i = pl.program_id(0)
        zero = jnp.zeros((tm, D), F32)

        def f(a_, y_, pf, gf, g2_, b2_):
            return head_fn(a_, y_, pf, gf, g2_, b2_, t_ref[...], wd_ref[...])

        loss, vjp = jax.vjp(f, a_ref[...].astype(F32), y_ref[...], zero, gf_ref[...], g2_ref[...], b2_ref[...])
        da, dy, dffn, dgf, dg2, db2 = vjp(jnp.ones((), F32))
        da_ref[...] = da
        dy_ref[...] = dy
        dffn_ref[...] = dffn.astype(BF16)
        accs = (dgf_ref, dg2_ref, db2_ref, loss_ref)

        @pl.when(i == 0)
        def _():
            for r in accs:
                r[...] = jnp.zeros_like(r)

        dgf_ref[...] += dgf
        dg2_ref[...] += dg2
        db2_ref[...] += db2
        loss_ref[...] += loss * jnp.ones((1, 128), F32)

    return pl.pallas_call(
        body, name="head_fwd_bwd", grid=(t // tm,),
        in_specs=[_rows(tm, DFF), _rows(tm, D), _rows(tm, D), _const((1, D)), _const((1, D)), _const((1, D)),
                  _const((DFF, D))],
        out_specs=[_rows(tm, DFF), _rows(tm, D), _rows(tm, D), _const((1, D)), _const((1, D)), _const((1, D)),
                   _const((1, 128))],
        out_shape=[jax.ShapeDtypeStruct((t, DFF), F32), jax.ShapeDtypeStruct((t, D), F32),
                   jax.ShapeDtypeStruct((t, D), BF16)] + [jax.ShapeDtypeStruct((1, D), F32)] * 3
        + [jax.ShapeDtypeStruct((1, 128), F32)],
        compiler_params=_cparams(),
    )(a, y1, tgt, gate_f, g2, b2, wd)


def ada_fwd(c_all, w_sh, b_sh):
    def body(c_ref, w_ref, b_ref, o_ref):
        o_ref[...] = _mmh(_silu(c_ref[...]), w_ref[...]) + b_ref[...]

    n = w_sh.shape[1]
    return pl.pallas_call(
        body, name="ada_fwd", out_shape=jax.ShapeDtypeStruct((NDEV, n), F32),
        in_specs=[pl.BlockSpec(memory_space=pltpu.VMEM)] * 3,
        out_specs=pl.BlockSpec(memory_space=pltpu.VMEM),
        compiler_params=pltpu.CompilerParams(vmem_limit_bytes=VMEM_LIMIT),
    )(c_all, w_sh, b_sh)


def ada_wgrad(c_all_t, dmod_sh):
    def body(c_ref, d_ref, o_ref):
        o_ref[...] = _mmh(_silu(c_ref[...]), d_ref[...])

    return pl.pallas_call(
        body, name="ada_wgrad", out_shape=jax.ShapeDtypeStruct((c_all_t.shape[0], dmod_sh.shape[1]), F32),
        in_specs=[pl.BlockSpec(memory_space=pltpu.VMEM)] * 2,
        out_specs=pl.BlockSpec(memory_space=pltpu.VMEM),
        compiler_params=pltpu.CompilerParams(vmem_limit_bytes=VMEM_LIMIT),
    )(c_all_t, dmod_sh)


def adamw(gparts, w, m, v, name):
    p, r, c = gparts.shape
    tr = r if r <= 256 else _pick(r, (256, 128, 64, 32, 16, 8))
    c1 = 1.0 - B1 ** STEP
    c2 = 1.0 - B2 ** STEP

    def body(g_ref, w_ref, m_ref, v_ref, go_ref, d_ref, mo_ref, vo_ref):
        g = g_ref[0].astype(F32)
        for s in range(1, p):
            g = g + g_ref[s].astype(F32)
        mn = B1 * m_ref[...] + (1.0 - B1) * g
        vn = B2 * v_ref[...] + (1.0 - B2) * (g * g)
        go_ref[...] = g
        d_ref[...] = -LR * ((mn / c1) / (jnp.sqrt(vn / c2) + AEPS) + WD * w_ref[...])
        mo_ref[...] = mn
        vo_ref[...] = vn

    spec = pl.BlockSpec((tr, c), lambda i: (i, 0))
    return pl.pallas_call(
        body, name=name, grid=(r // tr,),
        in_specs=[pl.BlockSpec((p, tr, c), lambda i: (0, i, 0)), spec, spec, spec],
        out_specs=[spec] * 4,
        out_shape=[jax.ShapeDtypeStruct((r, c), F32)] * 4,
        compiler_params=_cparams(),
    )(gparts, w, m, v)


def _me():
    x, y, c = lax.axis_index("x"), lax.axis_index("y"), lax.axis_index("c")
    return x, y, c, 4 * x + 2 * y + c


def _peer(x, y, c, d):
    px = 1 - x if (d >> 2) & 1 else x
    py = 1 - y if (d >> 1) & 1 else y
    pc = 1 - c if d & 1 else c
    return (px, py, pc), 4 * px + 2 * py + pc


def _exchange(arrs, name, scatter):
    n = len(arrs)

    def body(*refs):
        ins, outs = refs[:n], refs[n:2 * n]
        send, recv, lsem = refs[2 * n:]
        x, y, c, me = _me()
        remote, local = [], []
        for k in range(n):
            src = ins[k].at[me] if scatter else ins[k]
            cp = pltpu.make_async_copy(src, outs[k].at[me], lsem.at[k])
            cp.start()
            local.append(cp)
            for d in range(1, NDEV):
                dev, pid = _peer(x, y, c, d)
                src = ins[k].at[pid] if scatter else ins[k]
                cp = pltpu.make_async_remote_copy(src_ref=src, dst_ref=outs[k].at[me],
                                                  send_sem=send.at[k, d - 1], recv_sem=recv.at[k, d - 1],
                                                  device_id=dev, device_id_type=pl.DeviceIdType.MESH)
                cp.start()
                remote.append(cp)
        for cp in remote:
            cp.wait()
        for cp in local:
            cp.wait()

    shapes = [a.shape if scatter else (NDEV,) + a.shape for a in arrs]
    return pl.pallas_call(
        body, name=name,
        in_specs=[pl.BlockSpec(memory_space=pl.ANY)] * n,
        out_specs=[pl.BlockSpec(memory_space=pl.ANY)] * n,
        out_shape=[jax.ShapeDtypeStruct(s, a.dtype) for s, a in zip(shapes, arrs)],
        scratch_shapes=[pltpu.SemaphoreType.DMA((n, NDEV - 1)), pltpu.SemaphoreType.DMA((n, NDEV - 1)),
                        pltpu.SemaphoreType.DMA((n,))],
        compiler_params=pltpu.CompilerParams(has_side_effects=True),
    )(*arrs)


def all_gather(arrs, name):
    return _exchange(arrs, name, False)


def all_to_all(arrs, name):
    return _exchange(arrs, name, True)


_HBM = pl.BlockSpec(memory_space=pltpu.HBM)
_SEM = pl.BlockSpec(memory_space=pltpu.SEMAPHORE)
_EFFECT = pltpu.SideEffectType.DATAFLOW_SIDE_EFFECTING
NPEER = NDEV - 1


def exchange_start(arrs, name, scatter):
    n = len(arrs)
    lands = [lax.empty(a.shape if scatter else (NDEV,) + a.shape, a.dtype) for a in arrs]

    def body(*refs):
        ins, lrefs = refs[:n], refs[n:2 * n]
        send, recv, token = refs[2 * n], refs[2 * n + 1], refs[-1]
        x, y, c, me = _me()
        for k in range(n):
            for d in range(1, NDEV):
                dev, pid = _peer(x, y, c, d)
                src = ins[k].at[pid] if scatter else ins[k]
                pltpu.make_async_remote_copy(src_ref=src, dst_ref=lrefs[k].at[me],
                                             send_sem=send.at[k * NPEER + d - 1], recv_sem=recv.at[k * NPEER + d - 1],
                                             device_id=dev, device_id_type=pl.DeviceIdType.MESH).start()
        token[...] = jnp.zeros_like(token)

    thru = [pltpu.HBM(a.shape, a.dtype) for a in list(arrs) + lands]
    outs = pl.pallas_call(
        body, name=name,
        out_shape=(pltpu.SemaphoreType.DMA((n * NPEER,)), pltpu.SemaphoreType.DMA((n * NPEER,)), *thru,
                   jax.ShapeDtypeStruct((8, 128), F32)),
        in_specs=[_HBM] * (2 * n),
        out_specs=(_SEM, _SEM, *([_HBM] * (2 * n)), pl.BlockSpec(memory_space=pltpu.VMEM)),
        input_output_aliases={i: 2 + i for i in range(2 * n)},
        compiler_params=pltpu.CompilerParams(has_side_effects=_EFFECT),
    )(*[pltpu.with_memory_space_constraint(a, pltpu.HBM) for a in list(arrs) + lands])
    handle = dict(send=outs[0], recv=outs[1], src=list(outs[2:2 + n]), land=list(outs[2 + n:2 + 2 * n]),
                  scatter=scatter)
    return handle, outs[-1][0, 0]


def exchange_wait(handle, after, name):
    n = len(handle["src"])
    scatter = handle["scatter"]

    def body(*refs):
        ins, lrefs = refs[:n], refs[n:2 * n]
        send, recv = refs[2 * n], refs[2 * n + 1]
        x, y, c, _ = _me()
        for k in range(n):
            for d in range(1, NDEV):
                dev, _ = _peer(x, y, c, d)
                src = ins[k].at[0] if scatter else ins[k]
                cp = pltpu.make_async_remote_copy(src_ref=src, dst_ref=lrefs[k].at[0],
                                                  send_sem=send.at[k * NPEER + d - 1],
                                                  recv_sem=recv.at[k * NPEER + d - 1],
                                                  device_id=dev, device_id_type=pl.DeviceIdType.MESH)
                cp.wait_send()
                cp.wait_recv()

    arrs = handle["src"] + handle["land"]
    outs = pl.pallas_call(
        body, name=name,
        out_shape=tuple(pltpu.HBM(a.shape, a.dtype) for a in arrs),
        in_specs=[_HBM] * (2 * n) + [_SEM, _SEM, pl.BlockSpec(memory_space=pl.ANY)],
        out_specs=tuple([_HBM] * (2 * n)),
        input_output_aliases={i: i for i in range(2 * n)},
        compiler_params=pltpu.CompilerParams(has_side_effects=_EFFECT),
    )(*arrs, handle["send"], handle["recv"], after)
    me = 4 * lax.axis_index("x") + 2 * lax.axis_index("y") + lax.axis_index("c")
    landed = []
    for own, land in zip(outs[:n], outs[n:]):
        mine = lax.dynamic_index_in_dim(own, me, 0, keepdims=True) if scatter else own[None]
        landed.append(lax.dynamic_update_slice_in_dim(land, mine, me, 0))
    return landed


def _to_cat(w_in):
    k = w_in.shape[0]
    ba = jnp.zeros((k, NCAT - C_BA), w_in.dtype)
    ba = ba.at[:, 0:8].set(w_in[:, 4096:4104]).at[:, 128:136].set(w_in[:, 4104:4112])
    return jnp.concatenate([w_in[:, 0:3072], w_in[:, 3072:4096], w_in[:, 4112:7184], w_in[:, 7184:9232], ba], axis=1)


def _from_cat(dw):
    return jnp.concatenate([dw[:, 0:3072], dw[:, 3072:4096], dw[:, C_BA:C_BA + 8], dw[:, C_BA + 128:C_BA + 136],
                            dw[:, 4096:7168], dw[:, 7168:9216]], axis=1)


def _pad128(v):
    return jnp.pad(v, ((0, 0), (0, 128 - v.shape[1])))


def local_step(x, tgt, mod, wts, small, late_weights=None, on_grads=None):
    if on_grads is None:
        on_grads = lambda group, gd: jnp.zeros((), F32)
    t = x.shape[0]
    nc = t // CH
    shift_t, scale_t, gate_t, shift_f, scale_f, gate_f = mod
    wcat = _to_cat(wts["w_in"])
    a_log = _pad128(small["a_log"])
    dtb = _pad128(small["dt_bias"])
    vecs = dict(bga=small["b_gate"][:, :D], bgb=small["b_gate"][:, D:], gate_t=gate_t, g1=small["ln1_g"],
                b1=small["ln1_b"], scale_f=scale_f, shift_f=shift_f)

    h1 = modulate(x, scale_t, shift_t, "modulate_t")
    proj = matmul(h1, wcat, F32, "in_proj")
    q, k, v, gcs, beta = prep_fwd(proj, small["conv_a"], a_log, dtb)

    u, w, qg, kd, qk, eg, tinv = c1_fwd(q, k, v, gcs, beta)
    oa, sall = c2_fwd(u, w, qg, kd, qk, eg, proj, small["norm_a"])
    bias = bias_table(small["rel_bias"])
    ob = attn_fwd(proj, bias)
    if late_weights is not None:
        wts = {**wts, **late_weights(ob)}
    y1, h2 = merge_fwd(x, oa, ob, proj, vecs, wts["w_a"], wts["w_b"], wts["w_o"])
    up = matmul(h2, wts["w_up"], F32, "up_proj")
    a = ffn_act_fwd(up, small["conv_ffn"], small["b_conv_ffn"])

    da, dy1_res, dffn, dgate_f, dg2, db2, loss = head_fwd_bwd(a, y1, tgt, gate_f, small["ln2_g"], small["ln2_b"],
                                                            wts["w_down"])
    g_w_down = matmul(a, dffn, F32, "wgrad_down", ta=True)
    dup, g_conv_ffn, g_bconv = ffn_act_bwd(up, small["conv_ffn"], small["b_conv_ffn"], da)
    dh2 = matmul(dup, wts["w_up"], F32, "dgrad_up", tb=True)
    g_w_up = matmul(h2, dup, F32, "wgrad_up", ta=True)
    tok = on_grads("ffn", dict(w_up=g_w_up, w_down=g_w_down))
    dy1, dscale_f, dshift_f = modulate_bwd(dh2, y1, dy1_res, scale_f + tok, "modulate_f_bwd")
    (dx_res, doa, dob, dga, dgb, merged, dmix, dpa, dpb,
     dbga, dbgb, dgate_t, dg1, db1) = merge_bwd(x, oa, ob, proj, vecs, wts["w_a"], wts["w_b"], wts["w_o"], dy1)
    g_w_o = matmul(merged, dmix, F32, "wgrad_o", ta=True)
    g_w_a = matmul(oa, dpa, F32, "wgrad_a", ta=True)
    g_w_b = matmul(ob, dpb, F32, "wgrad_b", ta=True)
    tok = on_grads("mix", dict(w_o=g_w_o, w_a=g_w_a, w_b=g_w_b))
    dqb, dkb, dvb, dbias = attn_bwd(proj, bias, dob)
    g_rel = relbias_reduce(bias_table_bwd_layout(dbias))
    du, dw, dqg, dkd, dqk, deg, dz, g_norm = c2_bwd(u, w, qg, kd, qk, eg, proj, small["norm_a"] + tok, sall, doa)
    dq, dk, dv, dgcs, dbeta = c1_bwd(q, k, v, gcs, beta, tinv, du, dw, dqg, dkd, dqk, deg)
    dpre, dbb, daa, g_conv_a, g_alog, g_dtb = prep_bwd(proj, small["conv_a"], a_log, dtb, dq, dk, dv, dgcs, dbeta)
    dba = jnp.concatenate([dbb, daa, jnp.zeros((t, NCAT - C_BA - 256), BF16)], axis=1)
    dproj = jnp.concatenate([dpre, dz, dqb, dkb, dvb, dga, dgb, dba], axis=1)
    g_wcat = matmul(h1, dproj, F32, "wgrad_in", ta=True)
    tok = on_grads("in", dict(w_in=_from_cat(g_wcat)))
    dh1 = matmul(dproj, wcat + tok.astype(BF16), F32, "dgrad_in", tb=True)
    grad_x, dscale_t, dshift_t = modulate_bwd(dh1, x, dx_res, scale_t + tok, "modulate_t_bwd")

    dmod = (dshift_t, dscale_t, dgate_t, dshift_f, dscale_f, dgate_f)
    grads = dict(w_in=_from_cat(g_wcat), w_up=g_w_up, w_down=g_w_down, w_a=g_w_a, w_b=g_w_b, w_o=g_w_o,
                 conv_a=g_conv_a, rel_bias=g_rel, conv_ffn=g_conv_ffn,
                 b_gate=jnp.concatenate([dbga, dbgb], axis=1), a_log=g_alog[:, :AH], dt_bias=g_dtb[:, :AH],
                 norm_a=g_norm, ln1_g=dg1, ln1_b=db1, b_conv_ffn=g_bconv, ln2_g=dg2, ln2_b=db2)
    return loss[0, 0], grad_x, dmod, grads


_REP = {}
_off = 0
for _n, _wd, _pw in (("b_ada", 6144, 6144), ("b_gate", 2048, 2048), ("a_log", 8, 128), ("dt_bias", 8, 128),
                     ("norm_a", 128, 128), ("ln1_g", 1024, 1024), ("ln1_b", 1024, 1024),
                     ("b_conv_ffn", 5632, 5632), ("ln2_g", 1024, 1024), ("ln2_b", 1024, 1024), ("loss", 1, 128)):
    _REP[_n] = (_off, _wd, _pw)
    _off += _pw
REP_LEN = _off
REP_NAMES = [n for n in _REP if n != "loss"]
_SH = (("conv_a", (4, 384)), ("rel_bias", (16, 40)), ("conv_ffn", (3, 704)))
SH_LEN = 4352


def _pack_rep(vals):
    parts = []
    for n, (_, wd, pw) in _REP.items():
        a = vals.get(n)
        a = jnp.zeros((1, pw), F32) if a is None else jnp.pad(a.reshape(1, wd), ((0, 0), (0, pw - wd)))
        parts.append(a)
    return jnp.concatenate(parts, axis=1)


def _unpack_rep(vec, name):
    o, wd, _ = _REP[name]
    return vec[:, o:o + wd]


def _pack_sh(vals):
    parts = [vals[n].reshape(vals[n].shape[:-2] + (-1,)) for n, _ in _SH]
    a = jnp.concatenate(parts, axis=-1)
    return jnp.pad(a, [(0, 0)] * (a.ndim - 1) + [(0, SH_LEN - a.shape[-1])])


def _unpack_sh(vec, name):
    o = 0
    for n, shp in _SH:
        sz = shp[0] * shp[1]
        if n == name:
            return vec[0, o:o + sz].reshape(shp)
        o += sz
    raise KeyError(name)


def _col_shards(a, n):
    return a.reshape(a.shape[0], NDEV, n).transpose(1, 0, 2)


def kernel(x, c, w_ada, b_ada, w_in, b_gate, conv_a, a_log, dt_bias, norm_a, rel_bias, w_branch_a, w_branch_b, w_o, ln1_g, ln1_b, w_up, conv_ffn, b_conv_ffn, w_down, ln2_g, ln2_b, loss_target, m_w_ada, m_b_ada, m_w_in, m_b_gate, m_conv_a, m_a_log, m_dt_bias, m_norm_a, m_rel_bias, m_w_branch_a, m_w_branch_b, m_w_o, m_ln1_g, m_ln1_b, m_w_up, m_conv_ffn, m_b_conv_ffn, m_w_down, m_ln2_g, m_ln2_b, v_w_ada, v_b_ada, v_w_in, v_b_gate, v_conv_a, v_a_log, v_dt_bias, v_norm_a, v_rel_bias, v_w_branch_a, v_w_branch_b, v_w_o, v_ln1_g, v_ln1_b, v_w_up, v_conv_ffn, v_b_conv_ffn, v_w_down, v_ln2_g, v_ln2_b):
    W = dict(w_ada=w_ada, b_ada=b_ada, w_in=w_in, b_gate=b_gate, conv_a=conv_a, a_log=a_log, dt_bias=dt_bias,
             norm_a=norm_a, rel_bias=rel_bias, w_branch_a=w_branch_a, w_branch_b=w_branch_b, w_o=w_o, ln1_g=ln1_g,
             ln1_b=ln1_b, w_up=w_up, conv_ffn=conv_ffn, b_conv_ffn=b_conv_ffn, w_down=w_down, ln2_g=ln2_g,
             ln2_b=ln2_b)
    M = dict(w_ada=m_w_ada, b_ada=m_b_ada, w_in=m_w_in, b_gate=m_b_gate, conv_a=m_conv_a, a_log=m_a_log,
             dt_bias=m_dt_bias, norm_a=m_norm_a, rel_bias=m_rel_bias, w_branch_a=m_w_branch_a,
             w_branch_b=m_w_branch_b, w_o=m_w_o, ln1_g=m_ln1_g, ln1_b=m_ln1_b, w_up=m_w_up, conv_ffn=m_conv_ffn,
             b_conv_ffn=m_b_conv_ffn, w_down=m_w_down, ln2_g=m_ln2_g, ln2_b=m_ln2_b)
    V = dict(w_ada=v_w_ada, b_ada=v_b_ada, w_in=v_w_in, b_gate=v_b_gate, conv_a=v_conv_a, a_log=v_a_log,
             dt_bias=v_dt_bias, norm_a=v_norm_a, rel_bias=v_rel_bias, w_branch_a=v_w_branch_a,
             w_branch_b=v_w_branch_b, w_o=v_w_o, ln1_g=v_ln1_g, ln1_b=v_ln1_b, w_up=v_w_up, conv_ffn=v_conv_ffn,
             b_conv_ffn=v_b_conv_ffn, w_down=v_w_down, ln2_g=v_ln2_g, ln2_b=v_ln2_b)
    W, M, V = ({n: a[0] for n, a in dct.items()} for dct in (W, M, V))
    me = 4 * lax.axis_index("x") + 2 * lax.axis_index("y") + lax.axis_index("c")
    big = ("w_in", "w_up", "w_down", "w_branch_a", "w_branch_b", "w_o")

    (g_in,) = all_gather([W["w_in"].astype(BF16)], "gather_w_in")
    wts = dict(w_in=g_in.transpose(1, 0, 2).reshape(D, -1))
    late, late_tok = exchange_start([W[n].astype(BF16) for n in big[1:]], "gather_late_start", False)

    def late_weights(after):
        g_up, g_down, g_a, g_b, g_o = exchange_wait(late, after, "gather_late_wait")
        return dict(w_up=g_up.transpose(1, 0, 2).reshape(D, -1), w_down=g_down.reshape(DFF, D),
                    w_a=g_a.reshape(D, D), w_b=g_b.reshape(D, D), w_o=g_o.reshape(D, D))

    c_all, sh_all = all_gather([c, _pack_sh({n: W[n] for n, _ in _SH})[None]], "gather_small")
    c_all = c_all.reshape(NDEV, D)
    sh_all = sh_all.reshape(NDEV, SH_LEN)

    def full_small(name, shp):
        o = 0
        for n, s in _SH:
            if n == name:
                break
            o += s[0] * s[1]
        sz = shp[0] * shp[1]
        return sh_all[:, o:o + sz].reshape(NDEV, shp[0], shp[1]).transpose(1, 0, 2).reshape(shp[0], NDEV * shp[1])

    small = dict(conv_a=full_small("conv_a", (4, 384)), rel_bias=full_small("rel_bias", (16, 40)),
                 conv_ffn=full_small("conv_ffn", (3, 704)),
                 b_gate=W["b_gate"][None], a_log=W["a_log"][None], dt_bias=W["dt_bias"][None],
                 norm_a=W["norm_a"][None], ln1_g=W["ln1_g"][None], ln1_b=W["ln1_b"][None],
                 b_conv_ffn=W["b_conv_ffn"][None], ln2_g=W["ln2_g"][None], ln2_b=W["ln2_b"][None])

    nsh = w_ada.shape[2]
    b_sh = lax.dynamic_slice(W["b_ada"][None], (0, me * nsh), (1, nsh))
    mod_sh = ada_fwd(c_all, W["w_ada"], b_sh)
    (mod_rows,) = all_to_all([mod_sh[:, None, :]], "scatter_mod")
    mod6 = mod_rows.reshape(6, D)
    mod6 = mod6 + late_tok
    mod = tuple(mod6[i:i + 1] for i in range(6))

    pending = {}

    def on_grads(group, gd):
        if group == "ffn":
            slabs = [_col_shards(gd["w_up"], w_up.shape[2]), gd["w_down"].reshape(NDEV, -1, D)]
        elif group == "mix":
            slabs = [gd[n].reshape(NDEV, -1, D) for n in ("w_a", "w_b", "w_o")]
        else:
            slabs = [_col_shards(gd["w_in"], w_in.shape[2])]
        pending[group], tok = exchange_start([s.astype(BF16) for s in slabs], "scatter_" + group + "_start", True)
        return tok

    loss, grad_x, dmod, g = local_step(x[0], loss_target[0], mod, wts, small, late_weights, on_grads)

    rep_vals = {n: g[n] for n in REP_NAMES if n != "b_ada"}
    rep_vals["b_ada"] = jnp.concatenate(dmod, axis=1)
    rep_vals["loss"] = loss.reshape(1, 1)
    (rep_all,) = all_gather([_pack_rep(rep_vals)[None]], "gather_small_grads")
    rep_all = rep_all.reshape(NDEV, 1, REP_LEN)
    zero1 = jnp.zeros((1, 1), F32)
    rep_out = adamw(rep_all, _pack_rep({**{n: W[n][None] for n in REP_NAMES}, "loss": zero1}),
                    _pack_rep({**{n: M[n][None] for n in REP_NAMES}, "loss": zero1}),
                    _pack_rep({**{n: V[n][None] for n in REP_NAMES}, "loss": zero1}), "adamw_small")
    loss_total = _unpack_rep(rep_out[0], "loss")[0, 0]

    o_ada = _REP["b_ada"][0]
    dmod_all = rep_all[:, 0, o_ada:o_ada + 6 * D]
    dmod_sh = lax.dynamic_slice(dmod_all, (0, me * nsh), (NDEV, nsh))
    g_w_ada = ada_wgrad(c_all.T, dmod_sh)

    p_up, p_down = exchange_wait(pending["ffn"], grad_x, "scatter_ffn_wait")
    p_a, p_b, p_o = exchange_wait(pending["mix"], grad_x, "scatter_mix_wait")
    (p_in,) = exchange_wait(pending["in"], grad_x, "scatter_in_wait")
    parts = [p_in, p_up, p_down, p_a, p_b, p_o]
    sh_parts = {"conv_a": _col_shards(g["conv_a"], 384), "rel_bias": _col_shards(g["rel_bias"], 40),
                "conv_ffn": _col_shards(g["conv_ffn"], 704)}
    (sh_recv,) = all_to_all([_pack_sh(sh_parts)[:, None, :]], "scatter_small_grads")

    res = {}
    for n, p in zip(big, parts):
        res[n] = adamw(p, W[n], M[n], V[n], "adamw_" + n)
    res["w_ada"] = adamw(g_w_ada[None], W["w_ada"], M["w_ada"], V["w_ada"], "adamw_w_ada")
    sh_out = adamw(sh_recv, _pack_sh({n: W[n] for n, _ in _SH})[None], _pack_sh({n: M[n] for n, _ in _SH})[None],
                   _pack_sh({n: V[n] for n, _ in _SH})[None], "adamw_small_sharded")
    for n, _ in _SH:
        res[n] = tuple(_unpack_sh(o, n) for o in sh_out)
    for n in REP_NAMES:
        res[n] = tuple(_unpack_rep(o, n)[0] for o in rep_out)

    order = ("w_ada", "b_ada", "w_in", "b_gate", "conv_a", "a_log", "dt_bias", "norm_a", "rel_bias", "w_branch_a",
             "w_branch_b", "w_o", "ln1_g", "ln1_b", "w_up", "conv_ffn", "b_conv_ffn", "w_down", "ln2_g", "ln2_b")
    outs = [loss_total, grad_x[None]]
    for kind in range(4):
        outs += [res[n][kind][None] for n in order]
    return tuple(outs)
```

```python
import functools
import math

import numpy as np
import jax
import jax.numpy as jnp
from jax import lax
from jax.experimental import pallas as pl
from jax.experimental.pallas import tpu as pltpu

F32 = jnp.float32
BF16 = jnp.bfloat16
HI = lax.Precision.HIGHEST

D = 1024
CH = 64
AH, ADK = 8, 128
BH, BDH = 16, 64
BPREV = 8
BMAXREL = 256
RELSZ = CH + BMAXREL
DFF = 2816
ALPHA = 2.0 ** 0.25
LN_EPS, RMS_EPS, L2_EPS = 1e-5, 1e-6, 1e-6
NEG = -1e30
LR, B1, B2, AEPS, WD, STEP = 1e-3, 0.9, 0.999, 1e-8, 0.01, 10
NDEV = 8
HALO = 8
LANE = 128
TQ = 512
VMEM_LIMIT = 56 * 1024 * 1024

C_QKVA, C_Z, C_QKVB, C_GATE, C_BA, NCAT = 0, 3072, 4096, 7168, 9216, 9728


def _cparams(n_axes=1, vmem=VMEM_LIMIT):
    return pltpu.CompilerParams(dimension_semantics=("arbitrary",) * n_axes, vmem_limit_bytes=vmem)


def _dg(a, b, ca, cb):
    return lax.dot_general(a.astype(BF16), b.astype(BF16), (((ca,), (cb,)), ((), ())),
                           preferred_element_type=F32)


@jax.custom_vjp
def mm_nn(a, b):
    return _dg(a, b, 1, 0)


@jax.custom_vjp
def mm_nt(a, b):
    return _dg(a, b, 1, 1)


@jax.custom_vjp
def mm_tn(a, b):
    return _dg(a, b, 0, 0)


mm_nn.defvjp(lambda a, b: (mm_nn(a, b), (a, b)),
             lambda r, g: (mm_nt(g, r[1]).astype(r[0].dtype), mm_tn(r[0], g).astype(r[1].dtype)))
mm_nt.defvjp(lambda a, b: (mm_nt(a, b), (a, b)),
             lambda r, g: (mm_nn(g, r[1]).astype(r[0].dtype), mm_tn(g, r[0]).astype(r[1].dtype)))
mm_tn.defvjp(lambda a, b: (mm_tn(a, b), (a, b)),
             lambda r, g: (mm_nt(r[1], g).astype(r[0].dtype), mm_nn(r[0], g).astype(r[1].dtype)))


@jax.custom_vjp
def mm_w(a, w):
    return _dg(a, w, 1, 0)


mm_w.defvjp(lambda a, w: (mm_w(a, w), (a, w)),
            lambda r, g: (mm_nt(g, r[1]).astype(r[0].dtype), jnp.zeros_like(r[1])))


def _mmh(a, b):
    return lax.dot_general(a, b, (((1,), (0,)), ((), ())), precision=HI, preferred_element_type=F32)


def _bdg(a, b, ca, cb):
    return lax.dot_general(a.astype(BF16), b.astype(BF16), (((ca,), (cb,)), ((0,), (0,))),
                           preferred_element_type=F32)


@jax.custom_vjp
def bmm_nn(a, b):
    return _bdg(a, b, 2, 1)


@jax.custom_vjp
def bmm_nt(a, b):
    return _bdg(a, b, 2, 2)


@jax.custom_vjp
def bmm_tn(a, b):
    return _bdg(a, b, 1, 1)


bmm_nn.defvjp(lambda a, b: (bmm_nn(a, b), (a, b)), lambda r, g: (bmm_nt(g, r[1]), bmm_tn(r[0], g)))
bmm_nt.defvjp(lambda a, b: (bmm_nt(a, b), (a, b)), lambda r, g: (bmm_nn(g, r[1]), bmm_tn(g, r[0])))
bmm_tn.defvjp(lambda a, b: (bmm_tn(a, b), (a, b)), lambda r, g: (bmm_nt(r[1], g), bmm_nn(r[0], g)))


def _bdg3(a, b, ca, cb):
    return lax.dot_general(a, b, (((ca,), (cb,)), ((0,), (0,))), precision=HI, preferred_element_type=F32)


@jax.custom_vjp
def bmm3_nn(a, b):
    return _bdg3(a, b, 2, 1)


bmm3_nn.defvjp(lambda a, b: (bmm3_nn(a, b), (a, b)),
               lambda r, g: (_bdg3(g, r[1], 2, 2), _bdg3(r[0], g, 1, 1)))


def _sigmoid(x):
    return 0.5 * jnp.tanh(0.5 * x) + 0.5


def _silu(x):
    return x * _sigmoid(x)


def _softplus(x):
    return jnp.maximum(x, 0.0) + jnp.log(1.0 + jnp.exp(-jnp.abs(x)))


def _layernorm(r, g, b):
    mu = jnp.mean(r, axis=-1, keepdims=True)
    xc = r - mu
    var = jnp.mean(xc * xc, axis=-1, keepdims=True)
    return xc * lax.rsqrt(var + LN_EPS) * g + b


def _iota2(shape, dim):
    return lax.broadcasted_iota(jnp.int32, shape, dim)


@jax.custom_vjp
def causal_conv(ext, rows):
    k = len(rows)
    y = None
    for j in range(k):
        s = k - 1 - j
        r = pltpu.roll(ext, s, 0) if s else ext
        t = r[HALO:] * rows[j]
        y = t if y is None else y + t
    return y


def _causal_conv_fwd(ext, rows):
    return causal_conv(ext, rows), (ext, rows)


def _causal_conv_bwd(res, g):
    ext, rows = res
    n = ext.shape[0]
    k = len(rows)
    gext = jnp.concatenate([jnp.zeros((HALO, g.shape[1]), g.dtype), g], axis=0)
    dext = None
    drows = []
    for j in range(k):
        s = k - 1 - j
        up = pltpu.roll(gext, n - s, 0) if s else gext
        t = up * rows[j]
        dext = t if dext is None else dext + t
        r = pltpu.roll(ext, s, 0) if s else ext
        drows.append(jnp.sum(g * r[HALO:], axis=0, keepdims=True))
    return dext, tuple(drows)


causal_conv.defvjp(_causal_conv_fwd, _causal_conv_bwd)


def _chunk_masks(tm):
    i = _iota2((tm, tm), 0)
    j = _iota2((tm, tm), 1)
    same = (i ^ j) < CH
    lower = jnp.where(same & (j <= i), 1.0, 0.0).astype(F32)
    upper = jnp.where(same & (i <= j), 1.0, 0.0).astype(F32)
    return lower, upper


@jax.custom_vjp
def chunk_cumsum(g):
    lower, _ = _chunk_masks(g.shape[0])
    return _mmh(lower, g)


def _chunk_cumsum_bwd(_, ct):
    _, upper = _chunk_masks(ct.shape[0])
    return (_mmh(upper, ct),)


chunk_cumsum.defvjp(lambda g: (chunk_cumsum(g), None), _chunk_cumsum_bwd)


@jax.custom_vjp
def inv_unit_lower(a):
    n = a.shape[-1]
    eye = jnp.where(_iota2((1, n, n), 1) == _iota2((1, n, n), 2), 1.0, 0.0).astype(F32)
    x = eye - a
    p = _bdg3(a, a, 2, 1)
    steps = int(math.log2(n)) - 1
    for s in range(steps):
        x = x + _bdg3(x, p, 2, 1)
        if s + 1 < steps:
            p = _bdg3(p, p, 2, 1)
    return x


def _inv_fwd(a):
    t = inv_unit_lower(a)
    return t, t


def _inv_bwd(t, g):
    return (-_bdg3(_bdg3(t, g, 1, 1), t, 2, 2),)


inv_unit_lower.defvjp(_inv_fwd, _inv_bwd)


@jax.custom_vjp
def inv_known(a, t):
    return t


inv_known.defvjp(lambda a, t: (t, t), lambda t, g: (_inv_bwd(t, g)[0], jnp.zeros_like(t)))


def prep_head_fn(ext, rows, scale):
    s = _silu(causal_conv(ext, rows))
    if scale is None:
        return s
    return s * (lax.rsqrt(jnp.sum(s * s, axis=-1, keepdims=True) + L2_EPS) * scale)


def prep_gate_fn(bb, aa, a_log, dtb):
    g = -jnp.exp(a_log) * _softplus(aa + dtb)
    return chunk_cumsum(g), _sigmoid(bb)


PREP_SCALES = (ADK ** -0.5, 1.0, None)


def _head_cols(a):
    lane = _iota2((1, LANE), 1)
    return jnp.concatenate([jnp.sum(jnp.where(lane == h, a, 0.0), axis=1, keepdims=True)[None]
                            for h in range(AH)], axis=0)


def _head_rows(a):
    at = a.T[:AH]
    sub = _iota2((AH, 1), 0)
    return jnp.concatenate([jnp.sum(jnp.where(sub == h, at, 0.0), axis=0, keepdims=True)[None]
                            for h in range(AH)], axis=0)


def c1_heads(q, k, v, gcs, beta, tinv_saved=None):
    gcol = _head_cols(gcs)
    grow = _head_rows(gcs)
    bcol = _head_cols(beta)
    i = _iota2((1, CH, CH), 1)
    j = _iota2((1, CH, CH), 2)
    causal = j <= i
    strict = j < i
    diff = gcol - grow
    decay = jnp.where(causal, jnp.exp(jnp.where(causal, diff, 0.0)), 0.0)
    kb = k * bcol
    vb = v * bcol
    a_low = jnp.where(strict, bmm_nt(kb, k) * decay, 0.0)
    tinv = inv_unit_lower(a_low) if tinv_saved is None else inv_known(a_low, tinv_saved)
    egc = jnp.exp(gcol)
    u = bmm3_nn(tinv, vb)
    w = bmm3_nn(tinv, kb * egc)
    qk = jnp.where(causal, bmm_nt(q, k) * decay, 0.0)
    glast = jnp.sum(jnp.where(_iota2((1, CH, 1), 1) == CH - 1, gcol, 0.0), axis=1, keepdims=True)
    qg = q * egc
    kd = k * jnp.exp(glast - gcol)
    eg = jnp.exp(glast) * jnp.ones((1, 1, ADK), F32)
    return u, w, qk, qg, kd, eg, tinv


def c2_heads(s, u, w, qk, qg, kd, eg, z, nw):
    vn = u - bmm_nn(w, s)
    o = bmm_nn(qg, s) + bmm_nn(qk, vn)
    s2 = s * eg + bmm_tn(kd, vn)
    ms = jnp.mean(o * o, axis=-1, keepdims=True)
    og = o * lax.rsqrt(ms + RMS_EPS) * nw * _silu(z)
    return og, s2


def _attn_core_fwd(qh, k, v, bias):
    s = mm_nt(qh, k) * (BDH ** -0.5) + bias
    p = jnp.exp(s - jnp.max(s, axis=-1, keepdims=True))
    inv = 1.0 / jnp.sum(p, axis=-1, keepdims=True)
    o = mm_nn(p, v) * inv
    return o, (qh, k, v, p, inv, o)


def _attn_core_bwd(res, do):
    qh, k, v, p, inv, o = res
    p = p * inv
    dv = mm_tn(p, do)
    dp = mm_nt(do, v)
    ds = p * (dp - jnp.sum(do * o, axis=-1, keepdims=True))
    return mm_nn(ds, k) * (BDH ** -0.5), mm_tn(ds, qh) * (BDH ** -0.5), dv, ds


@jax.custom_vjp
def attn_core(qh, k, v, bias):
    return _attn_core_fwd(qh, k, v, bias)[0]


attn_core.defvjp(_attn_core_fwd, _attn_core_bwd)


def attn_sub(q, k, v, bias2, r, firstf):
    lane = _iota2((1, 2 * BDH), 1)
    col = _iota2((1, KWIN), 1) + r * SUBQ
    nokey = jnp.where(col < TQ, firstf, 0.0) * NEG
    out = None
    for hh in range(2):
        hm = jnp.where((lane >= hh * BDH) & (lane < (hh + 1) * BDH), 1.0, 0.0).astype(F32)
        o = attn_core(q * hm, k, v, assemble_bias(bias2[hh], r) + nokey) * hm
        out = o if out is None else out + o
    return out


def merge_fn(x, oa, ob, gra, grb, p_pa, p_pb, p_mix, bga, bgb, gate_t, g1, b1, scale_f, shift_f,
             wa, wb, wo):
    ga = _sigmoid(gra + bga)
    gb = _sigmoid(grb + bgb)
    pa = mm_w(oa, wa) + p_pa
    pb = mm_w(ob, wb) + p_pb
    merged = ga * pa + gb * pb
    mix = mm_w(merged, wo) + p_mix
    y1 = _layernorm(ALPHA * x + gate_t * mix, g1, b1)
    return y1, merged


def ffn_act_fn(extg, extv, rows_g, rows_v, bg, bv):
    return _silu(causal_conv(extg, rows_g) + bg) * (causal_conv(extv, rows_v) + bv)


def head_fn(a, y1, p_ffn, gate_f, g2, b2, tgt, wd):
    ffn = mm_w(a, wd) + p_ffn
    y2 = _layernorm(ALPHA * y1 + gate_f * ffn, g2, b2)
    err = y2 - tgt
    return 0.5 * jnp.sum(jnp.mean(err * err, axis=-1, keepdims=True))


def _rows(tm, width, colblk=0, order=None):
    if order is None:
        return pl.BlockSpec((tm, width), lambda i: (i, colblk))
    return pl.BlockSpec((tm, width), lambda i: (order(i), colblk))


def _const(shape):
    nd = len(shape)
    return pl.BlockSpec(shape, lambda *_: (0,) * nd)


def _pick(n, cands):
    for c in cands:
        if n % c == 0:
            return c
    raise ValueError(f"no tile for {n}")


def _tile(n, cap):
    best = None
    for c in range(LANE, min(n, cap) + 1, LANE):
        if n % c == 0:
            best = c
    if best is None:
        raise ValueError(f"no tile for {n}")
    return best


def _onehot_rows(k, j):
    return jnp.where(_iota2((k, 1), 0) == j, 1.0, 0.0).astype(F32)


def _stack_rows(drows):
    k = len(drows)
    out = None
    for j in range(k):
        tj = _onehot_rows(k, j) * drows[j]
        out = tj if out is None else out + tj
    return out


def matmul(a, w, out_dtype, name, ta=False, tb=False):
    kdim, m = a.shape if ta else a.shape[::-1]
    n = w.shape[0] if tb else w.shape[1]
    tm = _tile(m, 2048 if kdim <= 1024 else 1024)
    tn = _tile(n, 1024)
    tk = _tile(kdim, 2560)
    nk = kdim // tk
    a_spec = (pl.BlockSpec((tk, tm), lambda i, j, k: (k, i)) if ta
              else pl.BlockSpec((tm, tk), lambda i, j, k: (i, k)))
    w_spec = (pl.BlockSpec((tn, tk), lambda i, j, k: (j, k)) if tb
              else pl.BlockSpec((tk, tn), lambda i, j, k: (k, j)))

    def body(a_ref, w_ref, o_ref, *scratch):
        p = _dg(a_ref[...], w_ref[...], 0 if ta else 1, 1 if tb else 0)
        if nk == 1:
            o_ref[...] = p.astype(out_dtype)
            return
        acc = scratch[0]
        k = pl.program_id(2)

        @pl.when(k == 0)
        def _():
            acc[...] = p

        @pl.when(k > 0)
        def _():
            acc[...] += p

        @pl.when(k == nk - 1)
        def _():
            o_ref[...] = acc[...].astype(out_dtype)

    return pl.pallas_call(
        body, name=name,
        grid=(m // tm, n // tn, nk),
        in_specs=[a_spec, w_spec],
        out_specs=pl.BlockSpec((tm, tn), lambda i, j, k: (i, j)),
        out_shape=jax.ShapeDtypeStruct((m, n), out_dtype),
        scratch_shapes=[] if nk == 1 else [pltpu.VMEM((tm, tn), F32)],
        compiler_params=_cparams(3),
    )(a, w)


def dgrad_pieces(pieces, tail, w, name):
    m = pieces[0][0].shape[0]
    n, ktot = w.shape
    tk = 1024
    tm = _tile(m, 512)
    wt = tail.shape[1]
    ranges, k0 = [], 0
    for arr, off in pieces:
        assert off == k0 * tk and arr.shape[1] % tk == 0
        ranges.append((k0, k0 + arr.shape[1] // tk))
        k0 = ranges[-1][1]
    nk = k0
    npc = len(pieces)

    def body(*refs):
        a_refs, t_ref, w_ref, wt_ref, o_ref, acc = refs[:npc], refs[npc], refs[npc + 1], refs[npc + 2], refs[npc + 3], refs[npc + 4]
        k = pl.program_id(1)

        @pl.when(k == 0)
        def _():
            acc[...] = _dg(t_ref[...], wt_ref[...], 1, 1)

        for a_ref, (lo, hi) in zip(a_refs, ranges):
            @pl.when((k >= lo) & (k < hi))
            def _(a_ref=a_ref):
                acc[...] += _dg(a_ref[...], w_ref[...], 1, 1)

        @pl.when(k == nk - 1)
        def _():
            o_ref[...] = acc[...]

    def piece_spec(lo, hi):
        return pl.BlockSpec((tm, tk), lambda i, k: (i, jnp.clip(k - lo, 0, hi - lo - 1)))

    return pl.pallas_call(
        body, name=name, grid=(m // tm, nk),
        in_specs=[piece_spec(lo, hi) for lo, hi in ranges] + [
            pl.BlockSpec((tm, wt), lambda i, k: (i, 0)),
            pl.BlockSpec((n, tk), lambda i, k: (0, k)),
            pl.BlockSpec((n, wt), lambda i, k: (0, (ktot - wt) // wt))],
        out_specs=pl.BlockSpec((tm, n), lambda i, k: (i, 0)),
        out_shape=jax.ShapeDtypeStruct((m, n), F32),
        scratch_shapes=[pltpu.VMEM((tm, n), F32)],
        compiler_params=_cparams(2),
    )(*[a for a, _ in pieces], tail, w, w)


def modulate(x, scale, shift, name):
    t, d = x.shape
    tm = _pick(t, (512, 256, 128))

    def body(x_ref, sc_ref, sh_ref, o_ref):
        o_ref[...] = (x_ref[...] * (1.0 + sc_ref[...]) + sh_ref[...]).astype(BF16)

    return pl.pallas_call(
        body, name=name, grid=(t // tm,),
        in_specs=[_rows(tm, d), _const((1, d)), _const((1, d))],
        out_specs=_rows(tm, d),
        out_shape=jax.ShapeDtypeStruct((t, d), BF16),
        compiler_params=_cparams(),
    )(x, scale, shift)


def modulate_bwd(dh, xin, dres, scale, name):
    t, d = dh.shape
    tm = _pick(t, (512, 256, 128))

    def body(dh_ref, x_ref, dr_ref, sc_ref, o_ref, dsc_ref, dsh_ref):
        i = pl.program_id(0)
        dh_v = dh_ref[...]
        o_ref[...] = dr_ref[...] + dh_v * (1.0 + sc_ref[...])

        @pl.when(i == 0)
        def _():
            dsc_ref[...] = jnp.zeros_like(dsc_ref)
            dsh_ref[...] = jnp.zeros_like(dsh_ref)

        dsc_ref[...] += jnp.sum(dh_v * x_ref[...], axis=0, keepdims=True)
        dsh_ref[...] += jnp.sum(dh_v, axis=0, keepdims=True)

    return pl.pallas_call(
        body, name=name, grid=(t // tm,),
        in_specs=[_rows(tm, d), _rows(tm, d), _rows(tm, d), _const((1, d))],
        out_specs=[_rows(tm, d), _const((1, d)), _const((1, d))],
        out_shape=[jax.ShapeDtypeStruct((t, d), F32), jax.ShapeDtypeStruct((1, d), F32),
                   jax.ShapeDtypeStruct((1, d), F32)],
        compiler_params=_cparams(),
    )(dh, xin, dres, scale)


PREP_TM = 128


def _halo_specs(tm, width, colblk, order):
    per = tm // HALO
    return [pl.BlockSpec((HALO, width), lambda i: (jnp.maximum(order(i) * per - 1, 0), colblk)),
            pl.BlockSpec((tm, width), lambda i: (order(i), colblk))]


def prep_fwd(proj, conv_a, a_log, dtb):
    t = proj.shape[0]
    tm = PREP_TM
    nt = t // tm
    wq = 3 * D

    def body(prev_ref, cur_ref, bb_ref, aa_ref, cw_ref, al_ref, dt_ref, q_ref, k_ref, v_ref, g_ref, b_ref):
        i = pl.program_id(0)
        flag = jnp.where(i > 0, 1.0, 0.0)
        for part, o_ref in enumerate((q_ref, k_ref, v_ref)):
            for h in range(AH):
                sl = slice(part * D + h * ADK, part * D + (h + 1) * ADK)
                ext = jnp.concatenate([prev_ref[:, sl] * flag, cur_ref[:, sl]], axis=0)
                rows = tuple(cw_ref[j:j + 1, sl] for j in range(4))
                o_ref[h] = prep_head_fn(ext, rows, PREP_SCALES[part])
        gcs, beta = prep_gate_fn(bb_ref[...], aa_ref[...], al_ref[...], dt_ref[...])
        g_ref[...] = gcs
        b_ref[...] = beta

    ident = lambda i: i
    hm = pl.BlockSpec((AH, tm, ADK), lambda i: (0, i, 0))
    return pl.pallas_call(
        body, name="prep_fwd", grid=(nt,),
        in_specs=_halo_specs(tm, wq, 0, ident) + [
            _rows(tm, 128, C_BA // 128), _rows(tm, 128, C_BA // 128 + 1),
            _const((4, wq)), _const((1, 128)), _const((1, 128))],
        out_specs=[hm, hm, hm, _rows(tm, 128), _rows(tm, 128)],
        out_shape=[jax.ShapeDtypeStruct((AH, t, ADK), F32)] * 3 + [jax.ShapeDtypeStruct((t, 128), F32)] * 2,
        compiler_params=_cparams(),
    )(proj, proj, proj, proj, conv_a, a_log, dtb)


def prep_bwd(proj, conv_a, a_log, dtb, dq, dk, dv, dgcs, dbeta):
    t = proj.shape[0]
    tm = PREP_TM
    nt = t // tm
    wq = 3 * D
    rev = lambda i: nt - 1 - i

    def body(prev_ref, cur_ref, bb_ref, aa_ref, cw_ref, al_ref, dt_ref,
             dq_ref, dk_ref, dv_ref, dg_ref, db_ref,
             dpre_ref, dbb_ref, daa_ref, dcw_ref, dal_ref, ddt_ref, carry):
        i = pl.program_id(0)
        flag = jnp.where(i < nt - 1, 1.0, 0.0)

        @pl.when(i == 0)
        def _():
            carry[...] = jnp.zeros_like(carry)
            dcw_ref[...] = jnp.zeros_like(dcw_ref)
            dal_ref[...] = jnp.zeros_like(dal_ref)
            ddt_ref[...] = jnp.zeros_like(ddt_ref)

        for part, d_ref in enumerate((dq_ref, dk_ref, dv_ref)):
            for h in range(AH):
                sl = slice(part * D + h * ADK, part * D + (h + 1) * ADK)
                ext = jnp.concatenate([prev_ref[:, sl] * flag, cur_ref[:, sl]], axis=0)
                rows = tuple(cw_ref[j:j + 1, sl] for j in range(4))
                _, vjp = jax.vjp(lambda e, r: prep_head_fn(e, r, PREP_SCALES[part]), ext, rows)
                dext, drows = vjp(d_ref[h])
                dcur = dext[HALO:]
                dpre_ref[:, sl] = jnp.concatenate([dcur[:tm - HALO], dcur[tm - HALO:] + carry[:, sl]],
                                                  axis=0).astype(BF16)
                carry[:, sl] = dext[:HALO]
                dcw_ref[:, sl] += _stack_rows(drows)
        _, vjp = jax.vjp(prep_gate_fn, bb_ref[...], aa_ref[...], al_ref[...], dt_ref[...])
        dbb, daa, dal, ddt = vjp((dg_ref[...], db_ref[...]))
        dbb_ref[...] = dbb.astype(BF16)
        daa_ref[...] = daa.astype(BF16)
        dal_ref[...] += dal
        ddt_ref[...] += ddt

    hm = pl.BlockSpec((AH, tm, ADK), lambda i: (0, rev(i), 0))
    return pl.pallas_call(
        body, name="prep_bwd", grid=(nt,),
        in_specs=_halo_specs(tm, wq, 0, rev) + [
            _rows(tm, 128, C_BA // 128, rev), _rows(tm, 128, C_BA // 128 + 1, rev),
            _const((4, wq)), _const((1, 128)), _const((1, 128)),
            hm, hm, hm, _rows(tm, 128, 0, rev), _rows(tm, 128, 0, rev)],
        out_specs=[_rows(tm, wq, 0, rev), _rows(tm, 128, 0, rev), _rows(tm, 128, 0, rev),
                   _const((4, wq)), _const((1, 128)), _const((1, 128))],
        out_shape=[jax.ShapeDtypeStruct((t, wq), BF16), jax.ShapeDtypeStruct((t, 128), BF16),
                   jax.ShapeDtypeStruct((t, 128), BF16), jax.ShapeDtypeStruct((4, wq), F32),
                   jax.ShapeDtypeStruct((1, 128), F32), jax.ShapeDtypeStruct((1, 128), F32)],
        scratch_shapes=[pltpu.VMEM((HALO, wq), F32)],
        compiler_params=_cparams(),
    )(proj, proj, proj, proj, conv_a, a_log, dtb, dq, dk, dv, dgcs, dbeta)


def _c1_specs(order):
    hm = pl.BlockSpec((AH, CH, ADK), lambda n: (0, order(n), 0))
    col = pl.BlockSpec((CH, LANE), lambda n: (order(n), 0))
    qk = pl.BlockSpec((1, AH, CH, CH), lambda n: (order(n), 0, 0, 0))
    eg = pl.BlockSpec((1, AH, 1, ADK), lambda n: (order(n), 0, 0, 0))
    return hm, col, qk, eg


def _heads(ref):
    return jnp.stack([ref[:, h * ADK:(h + 1) * ADK] for h in range(AH)], axis=0)


def c1_fwd(q, k, v, gcs, beta):
    t = q.shape[1]
    nc = t // CH
    hm, col, qks, egs = _c1_specs(lambda n: n)

    def body(q_ref, k_ref, v_ref, g_ref, b_ref, u_ref, w_ref, qg_ref, kd_ref, qk_ref, eg_ref, ti_ref):
        u, w, qk, qg, kd, eg, tinv = c1_heads(q_ref[...], k_ref[...], v_ref[...], g_ref[...], b_ref[...])
        u_ref[...] = u
        w_ref[...] = w
        qg_ref[...] = qg
        kd_ref[...] = kd
        qk_ref[0] = qk
        eg_ref[0] = eg
        ti_ref[0] = tinv

    return pl.pallas_call(
        body, name="c1_fwd", grid=(nc,),
        in_specs=[hm, hm, hm, col, col],
        out_specs=[hm, hm, hm, hm, qks, egs, qks],
        out_shape=[jax.ShapeDtypeStruct((AH, t, ADK), F32)] * 4 + [
            jax.ShapeDtypeStruct((nc, AH, CH, CH), F32), jax.ShapeDtypeStruct((nc, AH, 1, ADK), F32),
            jax.ShapeDtypeStruct((nc, AH, CH, CH), F32)],
        compiler_params=_cparams(),
    )(q, k, v, gcs, beta)


def c1_bwd(q, k, v, gcs, beta, tinv, du, dw, dqg, dkd, dqk, deg):
    t = q.shape[1]
    nc = t // CH
    hm, col, qks, egs = _c1_specs(lambda n: n)

    def body(q_ref, k_ref, v_ref, g_ref, b_ref, ti_ref, du_ref, dw_ref, dqg_ref, dkd_ref, dqk_ref, deg_ref,
             dq_ref, dk_ref, dv_ref, dg_ref, db_ref):
        _, vjp = jax.vjp(lambda q_, k_, v_, g_, b_: c1_heads(q_, k_, v_, g_, b_, ti_ref[0]),
                         q_ref[...], k_ref[...], v_ref[...], g_ref[...], b_ref[...])
        dq, dk, dv, dg, db = vjp((du_ref[...], dw_ref[...], dqk_ref[0], dqg_ref[...], dkd_ref[...], deg_ref[0],
                                  jnp.zeros((AH, CH, CH), F32)))
        dq_ref[...] = dq
        dk_ref[...] = dk
        dv_ref[...] = dv
        dg_ref[...] = dg
        db_ref[...] = db

    return pl.pallas_call(
        body, name="c1_bwd", grid=(nc,),
        in_specs=[hm, hm, hm, col, col, qks, hm, hm, hm, hm, qks, egs],
        out_specs=[hm, hm, hm, col, col],
        out_shape=[jax.ShapeDtypeStruct((AH, t, ADK), F32)] * 3 + [jax.ShapeDtypeStruct((t, LANE), F32)] * 2,
        compiler_params=_cparams(),
    )(q, k, v, gcs, beta, tinv, du, dw, dqg, dkd, dqk, deg)


def c2_fwd(u, w, qg, kd, qk, eg, proj, norm_a):
    t = u.shape[1]
    nc = t // CH
    hm, _, qks, egs = _c1_specs(lambda n: n)
    tok = pl.BlockSpec((CH, D), lambda n: (n, 0))
    zspec = pl.BlockSpec((CH, D), lambda n: (n, C_Z // D))
    sspec = pl.BlockSpec((1, AH, ADK, ADK), lambda n: (n, 0, 0, 0))

    def body(u_ref, w_ref, qg_ref, kd_ref, qk_ref, eg_ref, z_ref, nw_ref, o_ref, sall_ref, st):
        n = pl.program_id(0)

        @pl.when(n == 0)
        def _():
            st[...] = jnp.zeros_like(st)

        s = st[...]
        sall_ref[0] = s
        og, s2 = c2_heads(s, u_ref[...], w_ref[...], qk_ref[0], qg_ref[...], kd_ref[...], eg_ref[0],
                          _heads(z_ref), nw_ref[...])
        st[...] = s2
        for h in range(AH):
            o_ref[:, h * ADK:(h + 1) * ADK] = og[h].astype(BF16)

    return pl.pallas_call(
        body, name="c2_fwd", grid=(nc,),
        in_specs=[hm, hm, hm, hm, qks, egs, zspec, _const((1, ADK))],
        out_specs=[tok, sspec],
        out_shape=[jax.ShapeDtypeStruct((t, D), BF16), jax.ShapeDtypeStruct((nc, AH, ADK, ADK), F32)],
        scratch_shapes=[pltpu.VMEM((AH, ADK, ADK), F32)],
        compiler_params=_cparams(),
    )(u, w, qg, kd, qk, eg, proj, norm_a)


def c2_bwd(u, w, qg, kd, qk, eg, proj, norm_a, sall, do):
    t = u.shape[1]
    nc = t // CH
    rev = lambda n: nc - 1 - n
    hm, _, qks, egs = _c1_specs(rev)
    tok = pl.BlockSpec((CH, D), lambda n: (rev(n), 0))
    zspec = pl.BlockSpec((CH, D), lambda n: (rev(n), C_Z // D))
    sspec = pl.BlockSpec((1, AH, ADK, ADK), lambda n: (rev(n), 0, 0, 0))

    def body(u_ref, w_ref, qg_ref, kd_ref, qk_ref, eg_ref, z_ref, nw_ref, sall_ref, do_ref,
             du_ref, dw_ref, dqg_ref, dkd_ref, dqk_ref, deg_ref, dz_ref, dnw_ref, dst):
        n = pl.program_id(0)

        @pl.when(n == 0)
        def _():
            dst[...] = jnp.zeros_like(dst)
            dnw_ref[...] = jnp.zeros_like(dnw_ref)

        _, vjp = jax.vjp(c2_heads, sall_ref[0], u_ref[...], w_ref[...], qk_ref[0], qg_ref[...], kd_ref[...],
                         eg_ref[0], _heads(z_ref), nw_ref[...])
        ds, du, dw, dqk, dqg, dkd, deg, dz, dn = vjp((_heads(do_ref), dst[...]))
        dst[...] = ds
        du_ref[...] = du
        dw_ref[...] = dw
        dqg_ref[...] = dqg
        dkd_ref[...] = dkd
        dqk_ref[0] = dqk
        deg_ref[0] = deg
        for h in range(AH):
            dz_ref[:, h * ADK:(h + 1) * ADK] = dz[h].astype(BF16)
        dnw_ref[...] += dn

    return pl.pallas_call(
        body, name="c2_bwd", grid=(nc,),
        in_specs=[hm, hm, hm, hm, qks, egs, zspec, _const((1, ADK)), sspec, tok],
        out_specs=[hm, hm, hm, hm, qks, egs, tok, _const((1, ADK))],
        out_shape=[jax.ShapeDtypeStruct((AH, t, ADK), F32)] * 4 + [
            jax.ShapeDtypeStruct((nc, AH, CH, CH), F32), jax.ShapeDtypeStruct((nc, AH, 1, ADK), F32),
            jax.ShapeDtypeStruct((t, D), BF16), jax.ShapeDtypeStruct((1, ADK), F32)],
        scratch_shapes=[pltpu.VMEM((AH, ADK, ADK), F32)],
        compiler_params=_cparams(),
    )(u, w, qg, kd, qk, eg, proj, norm_a, sall, do)


NQB = TQ // CH
NKB = 2 * TQ // CH
NDIST = BPREV + 1
KLO = -(NQB - 2)
NPAIR = NKB - 1 - KLO + 1


def bias_table(rel_bias):
    nh = rel_bias.shape[0]
    relx = jnp.concatenate([rel_bias, jnp.broadcast_to(rel_bias[:, -1:], (nh, CH * BPREV + 2 * CH - 1 - RELSZ))],
                           axis=1)
    t = jnp.stack([relx[:, CH * k:CH * k + 2 * CH - 1] for k in range(NDIST)], axis=1)
    trev = t[:, :, ::-1]
    g2 = jnp.concatenate([trev[:, :, CH - 1:], jnp.zeros((nh, NDIST, 1), F32), trev[:, :, :CH - 1]], axis=2)
    flat = jnp.tile(g2, (1, 1, CH + 1))[:, :, :CH * (2 * CH - 1)]
    blk = flat.reshape(nh, NDIST, CH, 2 * CH - 1)[..., :CH]
    neg = jnp.full((nh, NQB - 1, CH, CH), NEG, F32)
    asc = jnp.concatenate([neg, blk, neg], axis=1)
    return jnp.concatenate([asc[:, 1:], asc[:, :-1]], axis=-1)


SUBQ = 4 * CH
NSUB = TQ // SUBQ
KWIN = SUBQ + BPREV * CH


def assemble_bias(tab, r):
    b0 = r * SUBQ // (2 * CH)
    rows = [jnp.concatenate([tab[NQB + a - 2 * b - KLO] for b in range(b0, b0 + KWIN // (2 * CH))], axis=1)
            for a in range(r * SUBQ // CH, (r + 1) * SUBQ // CH)]
    return jnp.concatenate(rows, axis=0)


def bias_table_bwd_layout(dtab):
    nh = dtab.shape[0]
    dasc = (jnp.pad(dtab[..., :CH], ((0, 0), (1, 0), (0, 0), (0, 0)))
            + jnp.pad(dtab[..., CH:], ((0, 0), (0, 1), (0, 0), (0, 0))))
    dblk = dasc[:, NQB - 1:NQB - 1 + NDIST]
    dr = jnp.pad(dblk, ((0, 0), (0, 0), (0, 0), (0, CH - 1)))
    flat = jnp.pad(dr.reshape(nh, NDIST, CH * (2 * CH - 1)), ((0, 0), (0, 0), (0, 3 * CH)))
    return flat.reshape(nh, NDIST, CH + 1, 2 * CH).transpose(0, 2, 1, 3).reshape(nh, CH + 1, NDIST * 2 * CH)


def _fold_matrix_np():
    f = np.zeros((NDIST * 2 * CH, 384), np.float32)
    for k in range(NDIST):
        s = k
        for xx in range(2 * CH):
            if xx == CH:
                continue
            m = CH - 1 - xx if xx < CH else 3 * CH - 1 - xx
            f[s * 2 * CH + xx, min(CH * k + m, RELSZ - 1)] = 1.0
    return f


def relbias_reduce(dlay):
    nh, rows, cols = dlay.shape
    rpad = (-rows) % 8
    dlay = jnp.pad(dlay, ((0, 0), (0, rpad), (0, 0)))
    fold = jnp.asarray(_fold_matrix_np())

    def body(d_ref, f_ref, o_ref):
        cs = jnp.sum(d_ref[0], axis=0, keepdims=True)
        o_ref[0] = _mmh(jnp.broadcast_to(cs, (8, cols)), f_ref[...])

    out = pl.pallas_call(
        body, name="relbias_reduce", grid=(nh,),
        in_specs=[pl.BlockSpec((1, rows + rpad, cols), lambda h: (h, 0, 0)), _const((cols, 384))],
        out_specs=pl.BlockSpec((1, 8, 384), lambda h: (h, 0, 0)),
        out_shape=jax.ShapeDtypeStruct((nh, 8, 384), F32),
        compiler_params=_cparams(),
    )(dlay, fold)
    return out[:, 0, :RELSZ]


def attn_fwd(proj, bias):
    t = proj.shape[0]
    nt = t // TQ
    cb = C_QKVB // 128

    def body(q_ref, kp_ref, kc_ref, vp_ref, vc_ref, b_ref, o_ref):
        i = pl.program_id(1)
        firstf = jnp.where(i == 0, 1.0, 0.0)
        for r in range(NSUB):
            lo, hi = r * SUBQ, r * SUBQ + KWIN - TQ
            kw = jnp.concatenate([kp_ref[lo:, :], kc_ref[:hi, :]], axis=0)
            vw = jnp.concatenate([vp_ref[lo:, :], vc_ref[:hi, :]], axis=0)
            o_ref[lo:lo + SUBQ, :] = attn_sub(q_ref[lo:lo + SUBQ, :], kw, vw, b_ref[...], r, firstf).astype(BF16)

    def blk(off, prev):
        if prev:
            return pl.BlockSpec((TQ, 128), lambda p, i: (jnp.maximum(i - 1, 0), cb + off + p))
        return pl.BlockSpec((TQ, 128), lambda p, i: (i, cb + off + p))

    return pl.pallas_call(
        body, name="attn_fwd", grid=(BH // 2, nt),
        in_specs=[blk(0, False), blk(8, True), blk(8, False), blk(16, True), blk(16, False),
                  pl.BlockSpec((2, NPAIR, CH, 2 * CH), lambda p, i: (p, 0, 0, 0))],
        out_specs=pl.BlockSpec((TQ, 128), lambda p, i: (i, p)),
        out_shape=jax.ShapeDtypeStruct((t, D), BF16),
        compiler_params=_cparams(2),
    )(proj, proj, proj, proj, proj, bias)


def attn_bwd(proj, bias, do):
    t = proj.shape[0]
    nt = t // TQ
    cb = C_QKVB // 128

    def body(q_ref, kp_ref, kc_ref, vp_ref, vc_ref, b_ref, do_ref,
             dq_ref, dk_ref, dv_ref, db_ref, ck, cv, ak, av):
        i = pl.program_id(1)

        @pl.when(i == 0)
        def _():
            ck[...] = jnp.zeros_like(ck)
            cv[...] = jnp.zeros_like(cv)
            db_ref[...] = jnp.zeros_like(db_ref)

        @pl.when(i < nt)
        def _():
            firstf = jnp.where(i == 0, 1.0, 0.0)
            ak[...] = jnp.zeros_like(ak)
            av[...] = jnp.zeros_like(av)
            db = None
            for r in range(NSUB):
                lo, hi = r * SUBQ, r * SUBQ + KWIN - TQ
                kw = jnp.concatenate([kp_ref[lo:, :], kc_ref[:hi, :]], axis=0).astype(F32)
                vw = jnp.concatenate([vp_ref[lo:, :], vc_ref[:hi, :]], axis=0).astype(F32)
                _, vjp = jax.vjp(lambda q, k, v, b: attn_sub(q, k, v, b, r, firstf),
                                 q_ref[lo:lo + SUBQ, :].astype(F32), kw, vw, b_ref[...])
                dq, dkw, dvw, dbr = vjp(do_ref[lo:lo + SUBQ, :])
                dq_ref[lo:lo + SUBQ, :] = dq.astype(BF16)
                ak[lo:lo + KWIN, :] += dkw
                av[lo:lo + KWIN, :] += dvw
                db = dbr if db is None else db + dbr
            dk_ref[...] = (ck[...] + ak[:TQ, :]).astype(BF16)
            dv_ref[...] = (cv[...] + av[:TQ, :]).astype(BF16)
            ck[...] = ak[TQ:, :]
            cv[...] = av[TQ:, :]
            db_ref[...] += db

        @pl.when(i == nt)
        def _():
            dk_ref[...] = ck[...].astype(BF16)
            dv_ref[...] = cv[...].astype(BF16)

    def blk(off, prev):
        if prev:
            return pl.BlockSpec((TQ, 128), lambda p, i: (jnp.clip(i - 1, 0, nt - 1), cb + off + p))
        return pl.BlockSpec((TQ, 128), lambda p, i: (jnp.minimum(i, nt - 1), cb + off + p))

    own = pl.BlockSpec((TQ, 128), lambda p, i: (jnp.minimum(i, nt - 1), p))
    lag = pl.BlockSpec((TQ, 128), lambda p, i: (jnp.maximum(i - 1, 0), p))
    return pl.pallas_call(
        body, name="attn_bwd", grid=(BH // 2, nt + 1),
        in_specs=[blk(0, False), blk(8, True), blk(8, False), blk(16, True), blk(16, False),
                  pl.BlockSpec((2, NPAIR, CH, 2 * CH), lambda p, i: (p, 0, 0, 0)), own],
        out_specs=[own, lag, lag, pl.BlockSpec((2, NPAIR, CH, 2 * CH), lambda p, i: (p, 0, 0, 0))],
        out_shape=[jax.ShapeDtypeStruct((t, D), BF16)] * 3 + [jax.ShapeDtypeStruct((BH, NPAIR, CH, 2 * CH), F32)],
        scratch_shapes=[pltpu.VMEM((TQ, 128), F32), pltpu.VMEM((TQ, 128), F32),
                        pltpu.VMEM((2 * TQ, 128), F32), pltpu.VMEM((2 * TQ, 128), F32)],
        compiler_params=_cparams(2),
    )(proj, proj, proj, proj, proj, bias, do)


MERGE_TM = 256


def merge_fwd(x, oa, ob, proj, vecs, wa, wb, wo):
    t = x.shape[0]
    tm = MERGE_TM
    names = ("bga", "bgb", "gate_t", "g1", "b1", "scale_f", "shift_f")

    def body(x_ref, oa_ref, ob_ref, gra_ref, grb_ref, *rest):
        vrefs = rest[:7]
        wa_ref, wb_ref, wo_ref, y_ref, h_ref = rest[7:]
        vv = [r[...] for r in vrefs]
        zero = jnp.zeros((tm, D), F32)
        y1, _ = merge_fn(x_ref[...], oa_ref[...], ob_ref[...], gra_ref[...], grb_ref[...], zero, zero, zero,
                         *vv, wa_ref[...], wb_ref[...], wo_ref[...])
        y_ref[...] = y1
        h_ref[...] = (y1 * (1.0 + vv[5]) + vv[6]).astype(BF16)

    return pl.pallas_call(
        body, name="merge_fwd", grid=(t // tm,),
        in_specs=[_rows(tm, D), _rows(tm, D), _rows(tm, D), _rows(tm, D, C_GATE // D), _rows(tm, D, C_GATE // D + 1)]
        + [_const((1, D))] * 7 + [_const((D, D))] * 3,
        out_specs=[_rows(tm, D), _rows(tm, D)],
        out_shape=[jax.ShapeDtypeStruct((t, D), F32), jax.ShapeDtypeStruct((t, D), BF16)],
        compiler_params=_cparams(),
    )(x, oa, ob, proj, proj, *[vecs[n] for n in names], wa, wb, wo)


def merge_bwd(x, oa, ob, proj, vecs, wa, wb, wo, dy1):
    t = x.shape[0]
    tm = MERGE_TM
    names = ("bga", "bgb", "gate_t", "g1", "b1", "scale_f", "shift_f")

    def body(x_ref, oa_ref, ob_ref, gra_ref, grb_ref, *rest):
        vrefs = rest[:7]
        wa_ref, wb_ref, wo_ref, dy_ref = rest[7:11]
        (dx_ref, doa_ref, dob_ref, dga_ref, dgb_ref, mg_ref, dmix_ref, dpa_ref, dpb_ref,
         dbga_ref, dbgb_ref, dgt_ref, dg1_ref, db1_ref) = rest[11:]
        i = pl.program_id(0)
        vv = [r[...] for r in vrefs]
        zero = jnp.zeros((tm, D), F32)

        def f(x_, oa_, ob_, gra_, grb_, ppa, ppb, pmix, bga, bgb, gate_t, g1, b1):
            return merge_fn(x_, oa_, ob_, gra_, grb_, ppa, ppb, pmix, bga, bgb, gate_t, g1, b1, vv[5], vv[6],
                            wa_ref[...], wb_ref[...], wo_ref[...])

        _, vjp, merged = jax.vjp(f, x_ref[...], oa_ref[...].astype(F32), ob_ref[...].astype(F32),
                                 gra_ref[...], grb_ref[...], zero, zero, zero, *vv[:5], has_aux=True)
        dx, doa, dob, dga, dgb, dpa, dpb, dmix, dbga, dbgb, dgt, dg1, db1 = vjp(dy_ref[...])
        dx_ref[...] = dx
        doa_ref[...] = doa
        dob_ref[...] = dob
        dga_ref[...] = dga.astype(BF16)
        dgb_ref[...] = dgb.astype(BF16)
        mg_ref[...] = merged.astype(BF16)
        dmix_ref[...] = dmix.astype(BF16)
        dpa_ref[...] = dpa.astype(BF16)
        dpb_ref[...] = dpb.astype(BF16)
        accs = (dbga_ref, dbgb_ref, dgt_ref, dg1_ref, db1_ref)

        @pl.when(i == 0)
        def _():
            for a in accs:
                a[...] = jnp.zeros_like(a)

        for a, val in zip(accs, (dbga, dbgb, dgt, dg1, db1)):
            a[...] += val

    return pl.pallas_call(
        body, name="merge_bwd", grid=(t // tm,),
        in_specs=[_rows(tm, D), _rows(tm, D), _rows(tm, D), _rows(tm, D, C_GATE // D), _rows(tm, D, C_GATE // D + 1)]
        + [_const((1, D))] * 7 + [_const((D, D))] * 3 + [_rows(tm, D)],
        out_specs=[_rows(tm, D)] * 9 + [_const((1, D))] * 5,
        out_shape=[jax.ShapeDtypeStruct((t, D), F32)] * 3 + [jax.ShapeDtypeStruct((t, D), BF16)] * 6
        + [jax.ShapeDtypeStruct((1, D), F32)] * 5,
        compiler_params=_cparams(),
    )(x, oa, ob, proj, proj, *[vecs[n] for n in names], wa, wb, wo, dy1)


FFN_TM = 128


def ffn_act_fwd(up, conv_w, bconv):
    t, wdt = up.shape
    tm = FFN_TM

    def body(prev_ref, cur_ref, cw_ref, bc_ref, a_ref):
        i = pl.program_id(0)
        flag = jnp.where(i > 0, 1.0, 0.0)

        def ext(sl):
            return jnp.concatenate([prev_ref[:, sl] * flag, cur_ref[:, sl]], axis=0)

        def rows(sl):
            return tuple(cw_ref[j:j + 1, sl] for j in range(3))

        for cb in range(DFF // LANE):
            g = slice(cb * LANE, (cb + 1) * LANE)
            v = slice(DFF + cb * LANE, DFF + (cb + 1) * LANE)
            a_ref[:, g] = ffn_act_fn(ext(g), ext(v), rows(g), rows(v), bc_ref[:, g], bc_ref[:, v]).astype(BF16)

    return pl.pallas_call(
        body, name="ffn_act_fwd", grid=(t // tm,),
        in_specs=_halo_specs(tm, wdt, 0, lambda i: i) + [_const((3, wdt)), _const((1, wdt))],
        out_specs=_rows(tm, DFF),
        out_shape=jax.ShapeDtypeStruct((t, DFF), BF16),
        compiler_params=_cparams(),
    )(up, up, conv_w, bconv)


def ffn_act_bwd(up, conv_w, bconv, da):
    t, wdt = up.shape
    tm = FFN_TM
    nt = t // tm
    rev = lambda i: nt - 1 - i

    def body(prev_ref, cur_ref, cw_ref, bc_ref, da_ref, dup_ref, dcw_ref, dbc_ref, carry):
        i = pl.program_id(0)
        flag = jnp.where(i < nt - 1, 1.0, 0.0)

        @pl.when(i == 0)
        def _():
            carry[...] = jnp.zeros_like(carry)
            dcw_ref[...] = jnp.zeros_like(dcw_ref)
            dbc_ref[...] = jnp.zeros_like(dbc_ref)

        def ext(sl):
            return jnp.concatenate([prev_ref[:, sl] * flag, cur_ref[:, sl]], axis=0)

        def rows(sl):
            return tuple(cw_ref[j:j + 1, sl] for j in range(3))

        def emit(sl, dext, drows, dbc):
            dcur = dext[HALO:]
            dup_ref[:, sl] = jnp.concatenate([dcur[:tm - HALO], dcur[tm - HALO:] + carry[:, sl]], axis=0).astype(BF16)
            carry[:, sl] = dext[:HALO]
            dcw_ref[:, sl] += _stack_rows(drows)
            dbc_ref[:, sl] += dbc

        for cb in range(DFF // LANE):
            g = slice(cb * LANE, (cb + 1) * LANE)
            v = slice(DFF + cb * LANE, DFF + (cb + 1) * LANE)
            _, vjp = jax.vjp(ffn_act_fn, ext(g), ext(v), rows(g), rows(v), bc_ref[:, g], bc_ref[:, v])
            dxg, dxv, drg, drv, dbg, dbv = vjp(da_ref[:, g])
            emit(g, dxg, drg, dbg)
            emit(v, dxv, drv, dbv)

    return pl.pallas_call(
        body, name="ffn_act_bwd", grid=(nt,),
        in_specs=_halo_specs(tm, wdt, 0, rev) + [_const((3, wdt)), _const((1, wdt)), _rows(tm, DFF, 0, rev)],
        out_specs=[_rows(tm, wdt, 0, rev), _const((3, wdt)), _const((1, wdt))],
        out_shape=[jax.ShapeDtypeStruct((t, wdt), BF16), jax.ShapeDtypeStruct((3, wdt), F32),
                   jax.ShapeDtypeStruct((1, wdt), F32)],
        scratch_shapes=[pltpu.VMEM((HALO, wdt), F32)],
        compiler_params=_cparams(),
    )(up, up, conv_w, bconv, da)


HEAD_TM = 256


def head_fwd_bwd(a, y1, tgt, gate_f, g2, b2, wd):
    t = a.shape[0]
    tm = HEAD_TM

    def body(a_ref, y_ref, t_ref, gf_ref, g2_ref, b2_ref, wd_ref,
             da_ref, dy_ref, dffn_ref, dgf_ref, dg2_ref, db2_ref, loss_ref):
        i = pl.program_id(0)
        zero = jnp.zeros((tm, D), F32)

        def f(a_, y_, pf, gf, g2_, b2_):
            return head_fn(a_, y_, pf, gf, g2_, b2_, t_ref[...], wd_ref[...])

        loss, vjp = jax.vjp(f, a_ref[...].astype(F32), y_ref[...], zero, gf_ref[...], g2_ref[...], b2_ref[...])
        da, dy, dffn, dgf, dg2, db2 = vjp(jnp.ones((), F32))
        da_ref[...] = da
        dy_ref[...] = dy
        dffn_ref[...] = dffn.astype(BF16)
        accs = (dgf_ref, dg2_ref, db2_ref, loss_ref)

        @pl.when(i == 0)
        def _():
            for r in accs:
                r[...] = jnp.zeros_like(r)

        dgf_ref[...] += dgf
        dg2_ref[...] += dg2
        db2_ref[...] += db2
        loss_ref[...] += loss * jnp.ones((1, 128), F32)

    return pl.pallas_call(
        body, name="head_fwd_bwd", grid=(t // tm,),
        in_specs=[_rows(tm, DFF), _rows(tm, D), _rows(tm, D), _const((1, D)), _const((1, D)), _const((1, D)),
                  _const((DFF, D))],
        out_specs=[_rows(tm, DFF), _rows(tm, D), _rows(tm, D), _const((1, D)), _const((1, D)), _const((1, D)),
                   _const((1, 128))],
        out_shape=[jax.ShapeDtypeStruct((t, DFF), F32), jax.ShapeDtypeStruct((t, D), F32),
                   jax.ShapeDtypeStruct((t, D), BF16)] + [jax.ShapeDtypeStruct((1, D), F32)] * 3
        + [jax.ShapeDtypeStruct((1, 128), F32)],
        compiler_params=_cparams(),
    )(a, y1, tgt, gate_f, g2, b2, wd)


def ada_fwd(c_all, w_sh, b_sh):
    def body(c_ref, w_ref, b_ref, o_ref):
        o_ref[...] = _mmh(_silu(c_ref[...]), w_ref[...]) + b_ref[...]

    n = w_sh.shape[1]
    return pl.pallas_call(
        body, name="ada_fwd", out_shape=jax.ShapeDtypeStruct((NDEV, n), F32),
        in_specs=[pl.BlockSpec(memory_space=pltpu.VMEM)] * 3,
        out_specs=pl.BlockSpec(memory_space=pltpu.VMEM),
        compiler_params=pltpu.CompilerParams(vmem_limit_bytes=VMEM_LIMIT),
    )(c_all, w_sh, b_sh)


def ada_wgrad(c_all_t, dmod_sh):
    def body(c_ref, d_ref, o_ref):
        o_ref[...] = _mmh(_silu(c_ref[...]), d_ref[...])

    return pl.pallas_call(
        body, name="ada_wgrad", out_shape=jax.ShapeDtypeStruct((c_all_t.shape[0], dmod_sh.shape[1]), F32),
        in_specs=[pl.BlockSpec(memory_space=pltpu.VMEM)] * 2,
        out_specs=pl.BlockSpec(memory_space=pltpu.VMEM),
        compiler_params=pltpu.CompilerParams(vmem_limit_bytes=VMEM_LIMIT),
    )(c_all_t, dmod_sh)


def adamw(gparts, w, m, v, name):
    p, r, c = gparts.shape
    tr = r if r <= 256 else _pick(r, (256, 128, 64, 32, 16, 8))
    c1 = 1.0 - B1 ** STEP
    c2 = 1.0 - B2 ** STEP

    def body(g_ref, w_ref, m_ref, v_ref, go_ref, d_ref, mo_ref, vo_ref):
        g = g_ref[0].astype(F32)
        for s in range(1, p):
            g = g + g_ref[s].astype(F32)
        mn = B1 * m_ref[0] + (1.0 - B1) * g
        vn = B2 * v_ref[0] + (1.0 - B2) * (g * g)
        go_ref[0] = g
        d_ref[0] = -LR * ((mn / c1) / (jnp.sqrt(vn / c2) + AEPS) + WD * w_ref[0])
        mo_ref[0] = mn
        vo_ref[0] = vn

    spec = pl.BlockSpec((1, tr, c), lambda i: (0, i, 0))
    return pl.pallas_call(
        body, name=name, grid=(r // tr,),
        in_specs=[pl.BlockSpec((p, tr, c), lambda i: (0, i, 0)), spec, spec, spec],
        out_specs=[spec] * 4,
        out_shape=[jax.ShapeDtypeStruct((1, r, c), F32)] * 4,
        compiler_params=_cparams(),
    )(gparts, w, m, v)


def _me():
    x, y, c = lax.axis_index("x"), lax.axis_index("y"), lax.axis_index("c")
    return x, y, c, 4 * x + 2 * y + c


def _peer(x, y, c, d):
    px = 1 - x if (d >> 2) & 1 else x
    py = 1 - y if (d >> 1) & 1 else y
    pc = 1 - c if d & 1 else c
    return (px, py, pc), 4 * px + 2 * py + pc


def _exchange(arrs, name, scatter):
    n = len(arrs)

    def body(*refs):
        ins, outs = refs[:n], refs[n:2 * n]
        send, recv, lsem = refs[2 * n:]
        x, y, c, me = _me()
        remote, local = [], []
        for k in range(n):
            src = ins[k].at[me] if scatter else ins[k]
            cp = pltpu.make_async_copy(src, outs[k].at[me], lsem.at[k])
            cp.start()
            local.append(cp)
            for d in range(1, NDEV):
                dev, pid = _peer(x, y, c, d)
                src = ins[k].at[pid] if scatter else ins[k]
                cp = pltpu.make_async_remote_copy(src_ref=src, dst_ref=outs[k].at[me],
                                                  send_sem=send.at[k, d - 1], recv_sem=recv.at[k, d - 1],
                                                  device_id=dev, device_id_type=pl.DeviceIdType.MESH)
                cp.start()
                remote.append(cp)
        for cp in remote:
            cp.wait()
        for cp in local:
            cp.wait()

    shapes = [a.shape if scatter else (NDEV,) + a.shape for a in arrs]
    return pl.pallas_call(
        body, name=name,
        in_specs=[pl.BlockSpec(memory_space=pl.ANY)] * n,
        out_specs=[pl.BlockSpec(memory_space=pl.ANY)] * n,
        out_shape=[jax.ShapeDtypeStruct(s, a.dtype) for s, a in zip(shapes, arrs)],
        scratch_shapes=[pltpu.SemaphoreType.DMA((n, NDEV - 1)), pltpu.SemaphoreType.DMA((n, NDEV - 1)),
                        pltpu.SemaphoreType.DMA((n,))],
        compiler_params=pltpu.CompilerParams(has_side_effects=True),
    )(*arrs)


def all_gather(arrs, name):
    return _exchange(arrs, name, False)


def all_to_all(arrs, name):
    return _exchange(arrs, name, True)


_HBM = pl.BlockSpec(memory_space=pltpu.HBM)
_SEM = pl.BlockSpec(memory_space=pltpu.SEMAPHORE)
_EFFECT = pltpu.SideEffectType.DATAFLOW_SIDE_EFFECTING
NPEER = NDEV - 1


def exchange_start(arrs, name, scatter):
    n = len(arrs)
    lands = [lax.empty(a.shape if scatter else (NDEV,) + a.shape, a.dtype) for a in arrs]

    def body(*refs):
        ins, lrefs = refs[:n], refs[n:2 * n]
        send, recv, token = refs[2 * n], refs[2 * n + 1], refs[-1]
        x, y, c, me = _me()
        for k in range(n):
            for d in range(1, NDEV):
                dev, pid = _peer(x, y, c, d)
                src = ins[k].at[pid] if scatter else ins[k]
                pltpu.make_async_remote_copy(src_ref=src, dst_ref=lrefs[k].at[me],
                                             send_sem=send.at[k * NPEER + d - 1], recv_sem=recv.at[k * NPEER + d - 1],
                                             device_id=dev, device_id_type=pl.DeviceIdType.MESH).start()
        token[...] = jnp.zeros_like(token)

    thru = [pltpu.HBM(a.shape, a.dtype) for a in list(arrs) + lands]
    outs = pl.pallas_call(
        body, name=name,
        out_shape=(pltpu.SemaphoreType.DMA((n * NPEER,)), pltpu.SemaphoreType.DMA((n * NPEER,)), *thru,
                   jax.ShapeDtypeStruct((8, 128), F32)),
        in_specs=[_HBM] * (2 * n),
        out_specs=(_SEM, _SEM, *([_HBM] * (2 * n)), pl.BlockSpec(memory_space=pltpu.VMEM)),
        input_output_aliases={i: 2 + i for i in range(2 * n)},
        compiler_params=pltpu.CompilerParams(has_side_effects=_EFFECT),
    )(*[pltpu.with_memory_space_constraint(a, pltpu.HBM) for a in list(arrs) + lands])
    handle = dict(send=outs[0], recv=outs[1], src=list(outs[2:2 + n]), land=list(outs[2 + n:2 + 2 * n]),
                  scatter=scatter)
    return handle, outs[-1][0, 0]


def exchange_wait(handle, after, name):
    n = len(handle["src"])
    scatter = handle["scatter"]

    def body(*refs):
        ins, lrefs = refs[:n], refs[n:2 * n]
        send, recv = refs[2 * n], refs[2 * n + 1]
        x, y, c, _ = _me()
        for k in range(n):
            for d in range(1, NDEV):
                dev, _ = _peer(x, y, c, d)
                src = ins[k].at[0] if scatter else ins[k]
                cp = pltpu.make_async_remote_copy(src_ref=src, dst_ref=lrefs[k].at[0],
                                                  send_sem=send.at[k * NPEER + d - 1],
                                                  recv_sem=recv.at[k * NPEER + d - 1],
                                                  device_id=dev, device_id_type=pl.DeviceIdType.MESH)
                cp.wait_send()
                cp.wait_recv()

    arrs = handle["src"] + handle["land"]
    outs = pl.pallas_call(
        body, name=name,
        out_shape=tuple(pltpu.HBM(a.shape, a.dtype) for a in arrs),
        in_specs=[_HBM] * (2 * n) + [_SEM, _SEM, pl.BlockSpec(memory_space=pl.ANY)],
        out_specs=tuple([_HBM] * (2 * n)),
        input_output_aliases={i: i for i in range(2 * n)},
        compiler_params=pltpu.CompilerParams(has_side_effects=_EFFECT),
    )(*arrs, handle["send"], handle["recv"], after)
    me = 4 * lax.axis_index("x") + 2 * lax.axis_index("y") + lax.axis_index("c")
    landed = []
    for own, land in zip(outs[:n], outs[n:]):
        mine = lax.dynamic_index_in_dim(own, me, 0, keepdims=True) if scatter else own[None]
        landed.append(lax.dynamic_update_slice_in_dim(land, mine, me, 0))
    return landed


def _cat_from_slabs(slabs):
    _, k, n = slabs.shape

    def cols(lo, hi):
        parts, c = [], lo
        while c < hi:
            j = c // n
            e = min(hi, (j + 1) * n)
            parts.append(slabs[j][:, c - j * n:e - j * n])
            c = e
        return parts

    def zeros(w):
        return [jnp.zeros((k, w), slabs.dtype)]

    return jnp.concatenate(cols(0, 4096) + cols(4112, 9232) + cols(4096, 4104) + zeros(LANE - AH)
                           + cols(4104, 4112) + zeros(NCAT - C_BA - LANE - AH), axis=1)


IN_PIECES = (("pre", C_QKVA, 3072), ("z", C_Z, 1024), ("qb", C_QKVB, 1024), ("kb", C_QKVB + 1024, 1024),
             ("vb", C_QKVB + 2048, 1024), ("ga", C_GATE, 1024), ("gb", C_GATE + 1024, 1024))
_ORIG_SEGS = ((0, 3072, "pre", 0), (3072, 4096, "z", 0), (4096, 4104, "ba", 0), (4104, 4112, "ba", LANE),
              (4112, 5136, "qb", 0), (5136, 6160, "kb", 0), (6160, 7184, "vb", 0), (7184, 8208, "ga", 0),
              (8208, 9232, "gb", 0))


def _orig_cols_from_pieces(gp, lo, hi):
    parts = []
    for a, b, name, off in _ORIG_SEGS:
        s, e = max(a, lo), min(b, hi)
        if s < e:
            parts.append(gp[name][:, off + s - a:off + e - a])
    return parts[0] if len(parts) == 1 else jnp.concatenate(parts, axis=1)


def _pad128(v):
    return jnp.pad(v, ((0, 0), (0, 128 - v.shape[1])))


def local_step(x, tgt, mod, wts, small, late_weights=None, on_grads=None):
    if on_grads is None:
        on_grads = lambda group, gd: jnp.zeros((), F32)
    t = x.shape[0]
    nc = t // CH
    shift_t, scale_t, gate_t, shift_f, scale_f, gate_f = mod
    wcat = _cat_from_slabs(wts["w_in_slabs"])
    a_log = _pad128(small["a_log"])
    dtb = _pad128(small["dt_bias"])
    vecs = dict(bga=small["b_gate"][:, :D], bgb=small["b_gate"][:, D:], gate_t=gate_t, g1=small["ln1_g"],
                b1=small["ln1_b"], scale_f=scale_f, shift_f=shift_f)

    h1 = modulate(x, scale_t, shift_t, "modulate_t")
    proj = matmul(h1, wcat, F32, "in_proj")
    q, k, v, gcs, beta = prep_fwd(proj, small["conv_a"], a_log, dtb)

    u, w, qg, kd, qk, eg, tinv = c1_fwd(q, k, v, gcs, beta)
    oa, sall = c2_fwd(u, w, qg, kd, qk, eg, proj, small["norm_a"])
    bias = bias_table(small["rel_bias"])
    ob = attn_fwd(proj, bias)
    if late_weights is not None:
        wts = {**wts, **late_weights(ob)}
    y1, h2 = merge_fwd(x, oa, ob, proj, vecs, wts["w_a"], wts["w_b"], wts["w_o"])
    up = matmul(h2, wts["w_up"], F32, "up_proj")
    a = ffn_act_fwd(up, small["conv_ffn"], small["b_conv_ffn"])

    da, dy1_res, dffn, dgate_f, dg2, db2, loss = head_fwd_bwd(a, y1, tgt, gate_f, small["ln2_g"], small["ln2_b"],
                                                            wts["w_down"])
    g_w_down = matmul(a, dffn, F32, "wgrad_down", ta=True)
    dup, g_conv_ffn, g_bconv = ffn_act_bwd(up, small["conv_ffn"], small["b_conv_ffn"], da)
    dh2 = matmul(dup, wts["w_up"], F32, "dgrad_up", tb=True)
    g_w_up = matmul(h2, dup, F32, "wgrad_up", ta=True)
    tok = on_grads("ffn", dict(w_up=g_w_up, w_down=g_w_down))
    dy1, dscale_f, dshift_f = modulate_bwd(dh2, y1, dy1_res, scale_f + tok, "modulate_f_bwd")
    (dx_res, doa, dob, dga, dgb, merged, dmix, dpa, dpb,
     dbga, dbgb, dgate_t, dg1, db1) = merge_bwd(x, oa, ob, proj, vecs, wts["w_a"], wts["w_b"], wts["w_o"], dy1)
    g_w_o = matmul(merged, dmix, F32, "wgrad_o", ta=True)
    g_w_a = matmul(oa, dpa, F32, "wgrad_a", ta=True)
    g_w_b = matmul(ob, dpb, F32, "wgrad_b", ta=True)
    tok = on_grads("mix", dict(w_o=g_w_o, w_a=g_w_a, w_b=g_w_b))
    dqb, dkb, dvb, dbias = attn_bwd(proj, bias, dob)
    g_rel = relbias_reduce(bias_table_bwd_layout(dbias))
    du, dw, dqg, dkd, dqk, deg, dz, g_norm = c2_bwd(u, w, qg, kd, qk, eg, proj, small["norm_a"] + tok, sall, doa)
    dq, dk, dv, dgcs, dbeta = c1_bwd(q, k, v, gcs, beta, tinv, du, dw, dqg, dkd, dqk, deg)
    dpre, dbb, daa, g_conv_a, g_alog, g_dtb = prep_bwd(proj, small["conv_a"], a_log, dtb, dq, dk, dv, dgcs, dbeta)
    dba = jnp.concatenate([dbb, daa, jnp.zeros((t, NCAT - C_BA - 2 * LANE), BF16)], axis=1)
    dpieces = dict(pre=dpre, z=dz, qb=dqb, kb=dkb, vb=dvb, ga=dga, gb=dgb)
    g_in = {n: matmul(h1, dpieces[n], F32, "wgrad_in_" + n, ta=True) for n, _, _ in IN_PIECES}
    g_in["ba"] = matmul(h1, dba, F32, "wgrad_in_ba", ta=True)
    tok = on_grads("in", g_in)
    dh1 = dgrad_pieces([(dpieces[n], off) for n, off, _ in IN_PIECES], dba, wcat + tok.astype(BF16),
                       "dgrad_in")
    grad_x, dscale_t, dshift_t = modulate_bwd(dh1, x, dx_res, scale_t + tok, "modulate_t_bwd")

    dmod = (dshift_t, dscale_t, dgate_t, dshift_f, dscale_f, dgate_f)
    grads = dict(w_in=_orig_cols_from_pieces(g_in, 0, 9232), w_up=g_w_up, w_down=g_w_down, w_a=g_w_a, w_b=g_w_b, w_o=g_w_o,
                 conv_a=g_conv_a, rel_bias=g_rel, conv_ffn=g_conv_ffn,
                 b_gate=jnp.concatenate([dbga, dbgb], axis=1), a_log=g_alog[:, :AH], dt_bias=g_dtb[:, :AH],
                 norm_a=g_norm, ln1_g=dg1, ln1_b=db1, b_conv_ffn=g_bconv, ln2_g=dg2, ln2_b=db2)
    return loss[0, 0], grad_x, dmod, grads


_REP = {}
_off = 0
for _n, _wd, _pw in (("b_ada", 6144, 6144), ("b_gate", 2048, 2048), ("a_log", 8, 128), ("dt_bias", 8, 128),
                     ("norm_a", 128, 128), ("ln1_g", 1024, 1024), ("ln1_b", 1024, 1024),
                     ("b_conv_ffn", 5632, 5632), ("ln2_g", 1024, 1024), ("ln2_b", 1024, 1024), ("loss", 1, 128)):
    _REP[_n] = (_off, _wd, _pw)
    _off += _pw
REP_LEN = _off
REP_NAMES = [n for n in _REP if n != "loss"]
_SH = (("conv_a", (4, 384)), ("rel_bias", (16, 40)), ("conv_ffn", (3, 704)))
SH_LEN = 4352


def _pack_rep(vals):
    parts = []
    for n, (_, wd, pw) in _REP.items():
        a = vals.get(n)
        a = jnp.zeros((1, pw), F32) if a is None else jnp.pad(a.reshape(1, wd), ((0, 0), (0, pw - wd)))
        parts.append(a)
    return jnp.concatenate(parts, axis=1)


def _unpack_rep(vec, name):
    o, wd, _ = _REP[name]
    return vec[:, o:o + wd]


def _pack_sh(vals):
    parts = [vals[n].reshape(vals[n].shape[:-2] + (-1,)) for n, _ in _SH]
    a = jnp.concatenate(parts, axis=-1)
    return jnp.pad(a, [(0, 0)] * (a.ndim - 1) + [(0, SH_LEN - a.shape[-1])])


def _unpack_sh(vec, name):
    o = 0
    for n, shp in _SH:
        sz = shp[0] * shp[1]
        if n == name:
            return vec[0, o:o + sz].reshape(shp)
        o += sz
    raise KeyError(name)


def _col_shards(a, n):
    return a.reshape(a.shape[0], NDEV, n).transpose(1, 0, 2)


def kernel(x, c, w_ada, b_ada, w_in, b_gate, conv_a, a_log, dt_bias, norm_a, rel_bias, w_branch_a, w_branch_b, w_o, ln1_g, ln1_b, w_up, conv_ffn, b_conv_ffn, w_down, ln2_g, ln2_b, loss_target, m_w_ada, m_b_ada, m_w_in, m_b_gate, m_conv_a, m_a_log, m_dt_bias, m_norm_a, m_rel_bias, m_w_branch_a, m_w_branch_b, m_w_o, m_ln1_g, m_ln1_b, m_w_up, m_conv_ffn, m_b_conv_ffn, m_w_down, m_ln2_g, m_ln2_b, v_w_ada, v_b_ada, v_w_in, v_b_gate, v_conv_a, v_a_log, v_dt_bias, v_norm_a, v_rel_bias, v_w_branch_a, v_w_branch_b, v_w_o, v_ln1_g, v_ln1_b, v_w_up, v_conv_ffn, v_b_conv_ffn, v_w_down, v_ln2_g, v_ln2_b):
    W = dict(w_ada=w_ada, b_ada=b_ada, w_in=w_in, b_gate=b_gate, conv_a=conv_a, a_log=a_log, dt_bias=dt_bias,
             norm_a=norm_a, rel_bias=rel_bias, w_branch_a=w_branch_a, w_branch_b=w_branch_b, w_o=w_o, ln1_g=ln1_g,
             ln1_b=ln1_b, w_up=w_up, conv_ffn=conv_ffn, b_conv_ffn=b_conv_ffn, w_down=w_down, ln2_g=ln2_g,
             ln2_b=ln2_b)
    M = dict(w_ada=m_w_ada, b_ada=m_b_ada, w_in=m_w_in, b_gate=m_b_gate, conv_a=m_conv_a, a_log=m_a_log,
             dt_bias=m_dt_bias, norm_a=m_norm_a, rel_bias=m_rel_bias, w_branch_a=m_w_branch_a,
             w_branch_b=m_w_branch_b, w_o=m_w_o, ln1_g=m_ln1_g, ln1_b=m_ln1_b, w_up=m_w_up, conv_ffn=m_conv_ffn,
             b_conv_ffn=m_b_conv_ffn, w_down=m_w_down, ln2_g=m_ln2_g, ln2_b=m_ln2_b)
    V = dict(w_ada=v_w_ada, b_ada=v_b_ada, w_in=v_w_in, b_gate=v_b_gate, conv_a=v_conv_a, a_log=v_a_log,
             dt_bias=v_dt_bias, norm_a=v_norm_a, rel_bias=v_rel_bias, w_branch_a=v_w_branch_a,
             w_branch_b=v_w_branch_b, w_o=v_w_o, ln1_g=v_ln1_g, ln1_b=v_ln1_b, w_up=v_w_up, conv_ffn=v_conv_ffn,
             b_conv_ffn=v_b_conv_ffn, w_down=v_w_down, ln2_g=v_ln2_g, ln2_b=v_ln2_b)
    W3, M3, V3 = W, M, V
    W, M, V = ({n: a[0] for n, a in dct.items()} for dct in (W, M, V))
    me = 4 * lax.axis_index("x") + 2 * lax.axis_index("y") + lax.axis_index("c")
    big = ("w_in", "w_up", "w_down", "w_branch_a", "w_branch_b", "w_o")

    (g_in,) = all_gather([W["w_in"].astype(BF16)], "gather_w_in")
    wts = dict(w_in_slabs=g_in)
    late, late_tok = exchange_start([W[n].astype(BF16) for n in big[1:]], "gather_late_start", False)

    def late_weights(after):
        g_up, g_down, g_a, g_b, g_o = exchange_wait(late, after, "gather_late_wait")
        return dict(w_up=g_up.transpose(1, 0, 2).reshape(D, -1), w_down=g_down.reshape(DFF, D),
                    w_a=g_a.reshape(D, D), w_b=g_b.reshape(D, D), w_o=g_o.reshape(D, D))

    c_all, sh_all = all_gather([c, _pack_sh({n: W[n] for n, _ in _SH})[None]], "gather_small")
    c_all = c_all.reshape(NDEV, D)
    sh_all = sh_all.reshape(NDEV, SH_LEN)

    def full_small(name, shp):
        o = 0
        for n, s in _SH:
            if n == name:
                break
            o += s[0] * s[1]
        sz = shp[0] * shp[1]
        return sh_all[:, o:o + sz].reshape(NDEV, shp[0], shp[1]).transpose(1, 0, 2).reshape(shp[0], NDEV * shp[1])

    small = dict(conv_a=full_small("conv_a", (4, 384)), rel_bias=full_small("rel_bias", (16, 40)),
                 conv_ffn=full_small("conv_ffn", (3, 704)),
                 b_gate=W["b_gate"][None], a_log=W["a_log"][None], dt_bias=W["dt_bias"][None],
                 norm_a=W["norm_a"][None], ln1_g=W["ln1_g"][None], ln1_b=W["ln1_b"][None],
                 b_conv_ffn=W["b_conv_ffn"][None], ln2_g=W["ln2_g"][None], ln2_b=W["ln2_b"][None])

    nsh = w_ada.shape[2]
    b_sh = lax.dynamic_slice(W["b_ada"][None], (0, me * nsh), (1, nsh))
    mod_sh = ada_fwd(c_all, W["w_ada"], b_sh)
    (mod_rows,) = all_to_all([mod_sh[:, None, :]], "scatter_mod")
    mod6 = mod_rows.reshape(6, D)
    mod6 = mod6 + late_tok
    mod = tuple(mod6[i:i + 1] for i in range(6))

    pending = {}

    def on_grads(group, gd):
        if group == "ffn":
            slabs = [_col_shards(gd["w_up"], w_up.shape[2]), gd["w_down"].reshape(NDEV, -1, D)]
        elif group == "mix":
            slabs = [gd[n].reshape(NDEV, -1, D) for n in ("w_a", "w_b", "w_o")]
        else:
            nin = w_in.shape[2]
            slabs = [jnp.stack([_orig_cols_from_pieces(gd, j * nin, (j + 1) * nin) for j in range(NDEV)], axis=0)]
        pending[group], tok = exchange_start([s.astype(BF16) for s in slabs], "scatter_" + group + "_start", True)
        return tok

    loss, grad_x, dmod, g = local_step(x[0], loss_target[0], mod, wts, small, late_weights, on_grads)

    rep_vals = {n: g[n] for n in REP_NAMES if n != "b_ada"}
    rep_vals["b_ada"] = jnp.concatenate(dmod, axis=1)
    rep_vals["loss"] = loss.reshape(1, 1)
    (rep_all,) = all_gather([_pack_rep(rep_vals)[None]], "gather_small_grads")
    rep_all = rep_all.reshape(NDEV, 1, REP_LEN)
    zero1 = jnp.zeros((1, 1), F32)
    rep_out = adamw(rep_all, _pack_rep({**{n: W[n][None] for n in REP_NAMES}, "loss": zero1})[None],
                    _pack_rep({**{n: M[n][None] for n in REP_NAMES}, "loss": zero1})[None],
                    _pack_rep({**{n: V[n][None] for n in REP_NAMES}, "loss": zero1})[None], "adamw_small")
    rep_out = [o[0] for o in rep_out]
    loss_total = _unpack_rep(rep_out[0], "loss")[0, 0]

    o_ada = _REP["b_ada"][0]
    dmod_all = rep_all[:, 0, o_ada:o_ada + 6 * D]
    dmod_sh = lax.dynamic_slice(dmod_all, (0, me * nsh), (NDEV, nsh))
    g_w_ada = ada_wgrad(c_all.T, dmod_sh)

    p_up, p_down = exchange_wait(pending["ffn"], grad_x, "scatter_ffn_wait")
    p_a, p_b, p_o = exchange_wait(pending["mix"], grad_x, "scatter_mix_wait")
    (p_in,) = exchange_wait(pending["in"], grad_x, "scatter_in_wait")
    parts = [p_in, p_up, p_down, p_a, p_b, p_o]
    sh_parts = {"conv_a": _col_shards(g["conv_a"], 384), "rel_bias": _col_shards(g["rel_bias"], 40),
                "conv_ffn": _col_shards(g["conv_ffn"], 704)}
    (sh_recv,) = all_to_all([_pack_sh(sh_parts)[:, None, :]], "scatter_small_grads")

    res = {}
    for n, p in zip(big, parts):
        res[n] = adamw(p, W3[n], M3[n], V3[n], "adamw_" + n)
    res["w_ada"] = adamw(g_w_ada[None], W3["w_ada"], M3["w_ada"], V3["w_ada"], "adamw_w_ada")
    sh_out = adamw(sh_recv, _pack_sh({n: W[n] for n, _ in _SH})[None, None],
                   _pack_sh({n: M[n] for n, _ in _SH})[None, None],
                   _pack_sh({n: V[n] for n, _ in _SH})[None, None], "adamw_small_sharded")
    for n, _ in _SH:
        res[n] = tuple(_unpack_sh(o[0], n)[None] for o in sh_out)
    for n in REP_NAMES:
        res[n] = tuple(_unpack_rep(o, n) for o in rep_out)

    order = ("w_ada", "b_ada", "w_in", "b_gate", "conv_a", "a_log", "dt_bias", "norm_a", "rel_bias", "w_branch_a",
             "w_branch_b", "w_o", "ln1_g", "ln1_b", "w_up", "conv_ffn", "b_conv_ffn", "w_down", "ln2_g", "ln2_b")
    outs = [loss_total, grad_x[None]]
    for kind in range(4):
        outs += [res[n][kind] for n in order]
    return tuple(outs)
```

```python
import functools
import math

import numpy as np
import jax
import jax.numpy as jnp
from jax import lax
from jax.experimental import pallas as pl
from jax.experimental.pallas import tpu as pltpu

F32 = jnp.float32
BF16 = jnp.bfloat16
HI = lax.Precision.HIGHEST

D = 1024
CH = 64
AH, ADK = 8, 128
BH, BDH = 16, 64
BPREV = 8
BMAXREL = 256
RELSZ = CH + BMAXREL
DFF = 2816
ALPHA = 2.0 ** 0.25
LN_EPS, RMS_EPS, L2_EPS = 1e-5, 1e-6, 1e-6
NEG = -1e30
LR, B1, B2, AEPS, WD, STEP = 1e-3, 0.9, 0.999, 1e-8, 0.01, 10
NDEV = 8
HALO = 8
LANE = 128
TQ = 512
VMEM_LIMIT = 56 * 1024 * 1024

C_QKVA, C_Z, C_QKVB, C_GATE, C_BA, NCAT = 0, 3072, 4096, 7168, 9216, 9728


def _cparams(n_axes=1, vmem=VMEM_LIMIT):
    return pltpu.CompilerParams(dimension_semantics=("arbitrary",) * n_axes, vmem_limit_bytes=vmem)


def _dg(a, b, ca, cb):
    return lax.dot_general(a.astype(BF16), b.astype(BF16), (((ca,), (cb,)), ((), ())),
                           preferred_element_type=F32)


@jax.custom_vjp
def mm_nn(a, b):
    return _dg(a, b, 1, 0)


@jax.custom_vjp
def mm_nt(a, b):
    return _dg(a, b, 1, 1)


@jax.custom_vjp
def mm_tn(a, b):
    return _dg(a, b, 0, 0)


mm_nn.defvjp(lambda a, b: (mm_nn(a, b), (a, b)),
             lambda r, g: (mm_nt(g, r[1]).astype(r[0].dtype), mm_tn(r[0], g).astype(r[1].dtype)))
mm_nt.defvjp(lambda a, b: (mm_nt(a, b), (a, b)),
             lambda r, g: (mm_nn(g, r[1]).astype(r[0].dtype), mm_tn(g, r[0]).astype(r[1].dtype)))
mm_tn.defvjp(lambda a, b: (mm_tn(a, b), (a, b)),
             lambda r, g: (mm_nt(r[1], g).astype(r[0].dtype), mm_nn(r[0], g).astype(r[1].dtype)))


@jax.custom_vjp
def mm_w(a, w):
    return _dg(a, w, 1, 0)


mm_w.defvjp(lambda a, w: (mm_w(a, w), (a, w)),
            lambda r, g: (mm_nt(g, r[1]).astype(r[0].dtype), jnp.zeros_like(r[1])))


def _mmh(a, b):
    return lax.dot_general(a, b, (((1,), (0,)), ((), ())), precision=HI, preferred_element_type=F32)


def _bdg(a, b, ca, cb):
    return lax.dot_general(a.astype(BF16), b.astype(BF16), (((ca,), (cb,)), ((0,), (0,))),
                           preferred_element_type=F32)


@jax.custom_vjp
def bmm_nn(a, b):
    return _bdg(a, b, 2, 1)


@jax.custom_vjp
def bmm_nt(a, b):
    return _bdg(a, b, 2, 2)


@jax.custom_vjp
def bmm_tn(a, b):
    return _bdg(a, b, 1, 1)


bmm_nn.defvjp(lambda a, b: (bmm_nn(a, b), (a, b)), lambda r, g: (bmm_nt(g, r[1]), bmm_tn(r[0], g)))
bmm_nt.defvjp(lambda a, b: (bmm_nt(a, b), (a, b)), lambda r, g: (bmm_nn(g, r[1]), bmm_tn(g, r[0])))
bmm_tn.defvjp(lambda a, b: (bmm_tn(a, b), (a, b)), lambda r, g: (bmm_nt(r[1], g), bmm_nn(r[0], g)))


def _bdg3(a, b, ca, cb):
    return lax.dot_general(a, b, (((ca,), (cb,)), ((0,), (0,))), precision=HI, preferred_element_type=F32)


@jax.custom_vjp
def bmm3_nn(a, b):
    return _bdg3(a, b, 2, 1)


bmm3_nn.defvjp(lambda a, b: (bmm3_nn(a, b), (a, b)),
               lambda r, g: (_bdg3(g, r[1], 2, 2), _bdg3(r[0], g, 1, 1)))


def _sigmoid(x):
    return 0.5 * jnp.tanh(0.5 * x) + 0.5


def _silu(x):
    return x * _sigmoid(x)


def _softplus(x):
    return jnp.maximum(x, 0.0) + jnp.log(1.0 + jnp.exp(-jnp.abs(x)))


def _layernorm(r, g, b):
    mu = jnp.mean(r, axis=-1, keepdims=True)
    xc = r - mu
    var = jnp.mean(xc * xc, axis=-1, keepdims=True)
    return xc * lax.rsqrt(var + LN_EPS) * g + b


def _iota2(shape, dim):
    return lax.broadcasted_iota(jnp.int32, shape, dim)


@jax.custom_vjp
def causal_conv(ext, rows):
    k = len(rows)
    y = None
    for j in range(k):
        s = k - 1 - j
        r = pltpu.roll(ext, s, 0) if s else ext
        t = r[HALO:] * rows[j]
        y = t if y is None else y + t
    return y


def _causal_conv_fwd(ext, rows):
    return causal_conv(ext, rows), (ext, rows)


def _causal_conv_bwd(res, g):
    ext, rows = res
    n = ext.shape[0]
    k = len(rows)
    gext = jnp.concatenate([jnp.zeros((HALO, g.shape[1]), g.dtype), g], axis=0)
    dext = None
    drows = []
    for j in range(k):
        s = k - 1 - j
        up = pltpu.roll(gext, n - s, 0) if s else gext
        t = up * rows[j]
        dext = t if dext is None else dext + t
        r = pltpu.roll(ext, s, 0) if s else ext
        drows.append(jnp.sum(g * r[HALO:], axis=0, keepdims=True))
    return dext, tuple(drows)


causal_conv.defvjp(_causal_conv_fwd, _causal_conv_bwd)


def _chunk_masks(tm):
    i = _iota2((tm, tm), 0)
    j = _iota2((tm, tm), 1)
    same = (i ^ j) < CH
    lower = jnp.where(same & (j <= i), 1.0, 0.0).astype(F32)
    upper = jnp.where(same & (i <= j), 1.0, 0.0).astype(F32)
    return lower, upper


@jax.custom_vjp
def chunk_cumsum(g):
    lower, _ = _chunk_masks(g.shape[0])
    return _mmh(lower, g)


def _chunk_cumsum_bwd(_, ct):
    _, upper = _chunk_masks(ct.shape[0])
    return (_mmh(upper, ct),)


chunk_cumsum.defvjp(lambda g: (chunk_cumsum(g), None), _chunk_cumsum_bwd)


@jax.custom_vjp
def inv_unit_lower(a):
    n = a.shape[-1]
    eye = jnp.where(_iota2((1, n, n), 1) == _iota2((1, n, n), 2), 1.0, 0.0).astype(F32)
    x = eye - a
    p = _bdg3(a, a, 2, 1)
    steps = int(math.log2(n)) - 1
    for s in range(steps):
        x = x + _bdg3(x, p, 2, 1)
        if s + 1 < steps:
            p = _bdg3(p, p, 2, 1)
    return x


def _inv_fwd(a):
    t = inv_unit_lower(a)
    return t, t


def _inv_bwd(t, g):
    return (-_bdg3(_bdg3(t, g, 1, 1), t, 2, 2),)


inv_unit_lower.defvjp(_inv_fwd, _inv_bwd)


@jax.custom_vjp
def inv_known(a, t):
    return t


inv_known.defvjp(lambda a, t: (t, t), lambda t, g: (_inv_bwd(t, g)[0], jnp.zeros_like(t)))


def prep_head_fn(ext, rows, scale):
    s = _silu(causal_conv(ext, rows))
    if scale is None:
        return s
    return s * (lax.rsqrt(jnp.sum(s * s, axis=-1, keepdims=True) + L2_EPS) * scale)


def prep_gate_fn(bb, aa, a_log, dtb):
    g = -jnp.exp(a_log) * _softplus(aa + dtb)
    return chunk_cumsum(g), _sigmoid(bb)


PREP_SCALES = (ADK ** -0.5, 1.0, None)


def _head_cols(a):
    lane = _iota2((1, LANE), 1)
    return jnp.concatenate([jnp.sum(jnp.where(lane == h, a, 0.0), axis=1, keepdims=True)[None]
                            for h in range(AH)], axis=0)


def _head_rows(a):
    at = a.T[:AH]
    sub = _iota2((AH, 1), 0)
    return jnp.concatenate([jnp.sum(jnp.where(sub == h, at, 0.0), axis=0, keepdims=True)[None]
                            for h in range(AH)], axis=0)


def c1_heads(q, k, v, gcs, beta, tinv_saved=None):
    gcol = _head_cols(gcs)
    grow = _head_rows(gcs)
    bcol = _head_cols(beta)
    i = _iota2((1, CH, CH), 1)
    j = _iota2((1, CH, CH), 2)
    causal = j <= i
    strict = j < i
    diff = gcol - grow
    decay = jnp.where(causal, jnp.exp(jnp.where(causal, diff, 0.0)), 0.0)
    kb = k * bcol
    vb = v * bcol
    a_low = jnp.where(strict, bmm_nt(kb, k) * decay, 0.0)
    tinv = inv_unit_lower(a_low) if tinv_saved is None else inv_known(a_low, tinv_saved)
    egc = jnp.exp(gcol)
    u = bmm3_nn(tinv, vb)
    w = bmm3_nn(tinv, kb * egc)
    qk = jnp.where(causal, bmm_nt(q, k) * decay, 0.0)
    glast = jnp.sum(jnp.where(_iota2((1, CH, 1), 1) == CH - 1, gcol, 0.0), axis=1, keepdims=True)
    qg = q * egc
    kd = k * jnp.exp(glast - gcol)
    eg = jnp.exp(glast) * jnp.ones((1, 1, ADK), F32)
    return u, w, qk, qg, kd, eg, tinv


def c2_heads(s, u, w, qk, qg, kd, eg, z, nw):
    vn = u - bmm_nn(w, s)
    o = bmm_nn(qg, s) + bmm_nn(qk, vn)
    s2 = s * eg + bmm_tn(kd, vn)
    ms = jnp.mean(o * o, axis=-1, keepdims=True)
    og = o * lax.rsqrt(ms + RMS_EPS) * nw * _silu(z)
    return og, s2


def _attn_core_fwd(qh, k, v, bias):
    s = mm_nt(qh, k) * (BDH ** -0.5) + bias
    p = jnp.exp(s - jnp.max(s, axis=-1, keepdims=True))
    inv = 1.0 / jnp.sum(p, axis=-1, keepdims=True)
    o = mm_nn(p, v) * inv
    return o, (qh, k, v, p, inv, o)


def _attn_core_bwd(res, do):
    qh, k, v, p, inv, o = res
    p = p * inv
    dv = mm_tn(p, do)
    dp = mm_nt(do, v)
    ds = p * (dp - jnp.sum(do * o, axis=-1, keepdims=True))
    return mm_nn(ds, k) * (BDH ** -0.5), mm_tn(ds, qh) * (BDH ** -0.5), dv, ds


@jax.custom_vjp
def attn_core(qh, k, v, bias):
    return _attn_core_fwd(qh, k, v, bias)[0]


attn_core.defvjp(_attn_core_fwd, _attn_core_bwd)


def attn_sub(q, k, v, bias2, r, firstf):
    lane = _iota2((1, 2 * BDH), 1)
    col = _iota2((1, KWIN), 1) + r * SUBQ
    nokey = jnp.where(col < TQ, firstf, 0.0) * NEG
    out = None
    for hh in range(2):
        hm = jnp.where((lane >= hh * BDH) & (lane < (hh + 1) * BDH), 1.0, 0.0).astype(F32)
        o = attn_core(q * hm, k, v, assemble_bias(bias2[hh], r) + nokey) * hm
        out = o if out is None else out + o
    return out


def merge_fn(x, oa, ob, gra, grb, p_pa, p_pb, p_mix, bga, bgb, gate_t, g1, b1, scale_f, shift_f,
             wa, wb, wo):
    ga = _sigmoid(gra + bga)
    gb = _sigmoid(grb + bgb)
    pa = mm_w(oa, wa) + p_pa
    pb = mm_w(ob, wb) + p_pb
    merged = ga * pa + gb * pb
    mix = mm_w(merged, wo) + p_mix
    y1 = _layernorm(ALPHA * x + gate_t * mix, g1, b1)
    return y1, merged


def ffn_act_fn(extg, extv, rows_g, rows_v, bg, bv):
    return _silu(causal_conv(extg, rows_g) + bg) * (causal_conv(extv, rows_v) + bv)


def head_fn(a, y1, p_ffn, gate_f, g2, b2, tgt, wd):
    ffn = mm_w(a, wd) + p_ffn
    y2 = _layernorm(ALPHA * y1 + gate_f * ffn, g2, b2)
    err = y2 - tgt
    return 0.5 * jnp.sum(jnp.mean(err * err, axis=-1, keepdims=True))


def _rows(tm, width, colblk=0, order=None):
    if order is None:
        return pl.BlockSpec((tm, width), lambda i: (i, colblk))
    return pl.BlockSpec((tm, width), lambda i: (order(i), colblk))


def _const(shape):
    nd = len(shape)
    return pl.BlockSpec(shape, lambda *_: (0,) * nd)


def _pick(n, cands):
    for c in cands:
        if n % c == 0:
            return c
    raise ValueError(f"no tile for {n}")


def _tile(n, cap):
    best = None
    for c in range(LANE, min(n, cap) + 1, LANE):
        if n % c == 0:
            best = c
    if best is None:
        raise ValueError(f"no tile for {n}")
    return best


def _onehot_rows(k, j):
    return jnp.where(_iota2((k, 1), 0) == j, 1.0, 0.0).astype(F32)


def _stack_rows(drows):
    k = len(drows)
    out = None
    for j in range(k):
        tj = _onehot_rows(k, j) * drows[j]
        out = tj if out is None else out + tj
    return out


def matmul(a, w, out_dtype, name, ta=False, tb=False):
    kdim, m = a.shape if ta else a.shape[::-1]
    n = w.shape[0] if tb else w.shape[1]
    tm = _tile(m, 2048 if kdim <= 1024 else 1024)
    tn = _tile(n, 1024)
    tk = _tile(kdim, 2560)
    nk = kdim // tk
    a_spec = (pl.BlockSpec((tk, tm), lambda i, j, k: (k, i)) if ta
              else pl.BlockSpec((tm, tk), lambda i, j, k: (i, k)))
    w_spec = (pl.BlockSpec((tn, tk), lambda i, j, k: (j, k)) if tb
              else pl.BlockSpec((tk, tn), lambda i, j, k: (k, j)))

    def body(a_ref, w_ref, o_ref, *scratch):
        p = _dg(a_ref[...], w_ref[...], 0 if ta else 1, 1 if tb else 0)
        if nk == 1:
            o_ref[...] = p.astype(out_dtype)
            return
        acc = scratch[0]
        k = pl.program_id(2)

        @pl.when(k == 0)
        def _():
            acc[...] = p

        @pl.when(k > 0)
        def _():
            acc[...] += p

        @pl.when(k == nk - 1)
        def _():
            o_ref[...] = acc[...].astype(out_dtype)

    return pl.pallas_call(
        body, name=name,
        grid=(m // tm, n // tn, nk),
        in_specs=[a_spec, w_spec],
        out_specs=pl.BlockSpec((tm, tn), lambda i, j, k: (i, j)),
        out_shape=jax.ShapeDtypeStruct((m, n), out_dtype),
        scratch_shapes=[] if nk == 1 else [pltpu.VMEM((tm, tn), F32)],
        compiler_params=_cparams(3),
    )(a, w)


def dgrad_pieces(pieces, tail, w, name):
    m = pieces[0][0].shape[0]
    n, ktot = w.shape
    tk = 1024
    tm = _tile(m, 512)
    wt = tail.shape[1]
    ranges, k0 = [], 0
    for arr, off in pieces:
        assert off == k0 * tk and arr.shape[1] % tk == 0
        ranges.append((k0, k0 + arr.shape[1] // tk))
        k0 = ranges[-1][1]
    nk = k0
    npc = len(pieces)

    def body(*refs):
        a_refs, t_ref, w_ref, wt_ref, o_ref, acc = refs[:npc], refs[npc], refs[npc + 1], refs[npc + 2], refs[npc + 3], refs[npc + 4]
        k = pl.program_id(1)

        @pl.when(k == 0)
        def _():
            acc[...] = _dg(t_ref[...], wt_ref[...], 1, 1)

        for a_ref, (lo, hi) in zip(a_refs, ranges):
            @pl.when((k >= lo) & (k < hi))
            def _(a_ref=a_ref):
                acc[...] += _dg(a_ref[...], w_ref[...], 1, 1)

        @pl.when(k == nk - 1)
        def _():
            o_ref[...] = acc[...]

    def piece_spec(lo, hi):
        return pl.BlockSpec((tm, tk), lambda i, k: (i, jnp.clip(k - lo, 0, hi - lo - 1)))

    return pl.pallas_call(
        body, name=name, grid=(m // tm, nk),
        in_specs=[piece_spec(lo, hi) for lo, hi in ranges] + [
            pl.BlockSpec((tm, wt), lambda i, k: (i, 0)),
            pl.BlockSpec((n, tk), lambda i, k: (0, k)),
            pl.BlockSpec((n, wt), lambda i, k: (0, (ktot - wt) // wt))],
        out_specs=pl.BlockSpec((tm, n), lambda i, k: (i, 0)),
        out_shape=jax.ShapeDtypeStruct((m, n), F32),
        scratch_shapes=[pltpu.VMEM((tm, n), F32)],
        compiler_params=_cparams(2),
    )(*[a for a, _ in pieces], tail, w, w)


def modulate(x, scale, shift, name):
    t, d = x.shape
    tm = _pick(t, (512, 256, 128))

    def body(x_ref, sc_ref, sh_ref, o_ref):
        o_ref[...] = (x_ref[...] * (1.0 + sc_ref[...]) + sh_ref[...]).astype(BF16)

    return pl.pallas_call(
        body, name=name, grid=(t // tm,),
        in_specs=[_rows(tm, d), _const((1, d)), _const((1, d))],
        out_specs=_rows(tm, d),
        out_shape=jax.ShapeDtypeStruct((t, d), BF16),
        compiler_params=_cparams(),
    )(x, scale, shift)


def modulate_bwd(dh, xin, dres, scale, name):
    t, d = dh.shape
    tm = _pick(t, (512, 256, 128))

    def body(dh_ref, x_ref, dr_ref, sc_ref, o_ref, dsc_ref, dsh_ref):
        i = pl.program_id(0)
        dh_v = dh_ref[...]
        o_ref[...] = dr_ref[...] + dh_v * (1.0 + sc_ref[...])

        @pl.when(i == 0)
        def _():
            dsc_ref[...] = jnp.zeros_like(dsc_ref)
            dsh_ref[...] = jnp.zeros_like(dsh_ref)

        dsc_ref[...] += jnp.sum(dh_v * x_ref[...], axis=0, keepdims=True)
        dsh_ref[...] += jnp.sum(dh_v, axis=0, keepdims=True)

    return pl.pallas_call(
        body, name=name, grid=(t // tm,),
        in_specs=[_rows(tm, d), _rows(tm, d), _rows(tm, d), _const((1, d))],
        out_specs=[_rows(tm, d), _const((1, d)), _const((1, d))],
        out_shape=[jax.ShapeDtypeStruct((t, d), F32), jax.ShapeDtypeStruct((1, d), F32),
                   jax.ShapeDtypeStruct((1, d), F32)],
        compiler_params=_cparams(),
    )(dh, xin, dres, scale)


PREP_TM = 128


def _halo_specs(tm, width, colblk, order):
    per = tm // HALO
    return [pl.BlockSpec((HALO, width), lambda i: (jnp.maximum(order(i) * per - 1, 0), colblk)),
            pl.BlockSpec((tm, width), lambda i: (order(i), colblk))]


def prep_fwd(proj, conv_a, a_log, dtb):
    t = proj.shape[0]
    tm = PREP_TM
    nt = t // tm
    wq = 3 * D

    def body(prev_ref, cur_ref, bb_ref, aa_ref, cw_ref, al_ref, dt_ref, q_ref, k_ref, v_ref, g_ref, b_ref):
        i = pl.program_id(0)
        flag = jnp.where(i > 0, 1.0, 0.0)
        for part, o_ref in enumerate((q_ref, k_ref, v_ref)):
            for h in range(AH):
                sl = slice(part * D + h * ADK, part * D + (h + 1) * ADK)
                ext = jnp.concatenate([prev_ref[:, sl] * flag, cur_ref[:, sl]], axis=0)
                rows = tuple(cw_ref[j:j + 1, sl] for j in range(4))
                o_ref[h] = prep_head_fn(ext, rows, PREP_SCALES[part])
        gcs, beta = prep_gate_fn(bb_ref[...], aa_ref[...], al_ref[...], dt_ref[...])
        g_ref[...] = gcs
        b_ref[...] = beta

    ident = lambda i: i
    hm = pl.BlockSpec((AH, tm, ADK), lambda i: (0, i, 0))
    return pl.pallas_call(
        body, name="prep_fwd", grid=(nt,),
        in_specs=_halo_specs(tm, wq, 0, ident) + [
            _rows(tm, 128, C_BA // 128), _rows(tm, 128, C_BA // 128 + 1),
            _const((4, wq)), _const((1, 128)), _const((1, 128))],
        out_specs=[hm, hm, hm, _rows(tm, 128), _rows(tm, 128)],
        out_shape=[jax.ShapeDtypeStruct((AH, t, ADK), F32)] * 3 + [jax.ShapeDtypeStruct((t, 128), F32)] * 2,
        compiler_params=_cparams(),
    )(proj, proj, proj, proj, conv_a, a_log, dtb)


def prep_bwd(proj, conv_a, a_log, dtb, dq, dk, dv, dgcs, dbeta):
    t = proj.shape[0]
    tm = PREP_TM
    nt = t // tm
    wq = 3 * D
    rev = lambda i: nt - 1 - i

    def body(prev_ref, cur_ref, bb_ref, aa_ref, cw_ref, al_ref, dt_ref,
             dq_ref, dk_ref, dv_ref, dg_ref, db_ref,
             dpre_ref, dbb_ref, daa_ref, dcw_ref, dal_ref, ddt_ref, carry):
        i = pl.program_id(0)
        flag = jnp.where(i < nt - 1, 1.0, 0.0)

        @pl.when(i == 0)
        def _():
            carry[...] = jnp.zeros_like(carry)
            dcw_ref[...] = jnp.zeros_like(dcw_ref)
            dal_ref[...] = jnp.zeros_like(dal_ref)
            ddt_ref[...] = jnp.zeros_like(ddt_ref)

        for part, d_ref in enumerate((dq_ref, dk_ref, dv_ref)):
            for h in range(AH):
                sl = slice(part * D + h * ADK, part * D + (h + 1) * ADK)
                ext = jnp.concatenate([prev_ref[:, sl] * flag, cur_ref[:, sl]], axis=0)
                rows = tuple(cw_ref[j:j + 1, sl] for j in range(4))
                _, vjp = jax.vjp(lambda e, r: prep_head_fn(e, r, PREP_SCALES[part]), ext, rows)
                dext, drows = vjp(d_ref[h])
                dcur = dext[HALO:]
                dpre_ref[:, sl] = jnp.concatenate([dcur[:tm - HALO], dcur[tm - HALO:] + carry[:, sl]],
                                                  axis=0).astype(BF16)
                carry[:, sl] = dext[:HALO]
                dcw_ref[:, sl] += _stack_rows(drows)
        _, vjp = jax.vjp(prep_gate_fn, bb_ref[...], aa_ref[...], al_ref[...], dt_ref[...])
        dbb, daa, dal, ddt = vjp((dg_ref[...], db_ref[...]))
        dbb_ref[...] = dbb.astype(BF16)
        daa_ref[...] = daa.astype(BF16)
        dal_ref[...] += dal
        ddt_ref[...] += ddt

    hm = pl.BlockSpec((AH, tm, ADK), lambda i: (0, rev(i), 0))
    return pl.pallas_call(
        body, name="prep_bwd", grid=(nt,),
        in_specs=_halo_specs(tm, wq, 0, rev) + [
            _rows(tm, 128, C_BA // 128, rev), _rows(tm, 128, C_BA // 128 + 1, rev),
            _const((4, wq)), _const((1, 128)), _const((1, 128)),
            hm, hm, hm, _rows(tm, 128, 0, rev), _rows(tm, 128, 0, rev)],
        out_specs=[_rows(tm, wq, 0, rev), _rows(tm, 128, 0, rev), _rows(tm, 128, 0, rev),
                   _const((4, wq)), _const((1, 128)), _const((1, 128))],
        out_shape=[jax.ShapeDtypeStruct((t, wq), BF16), jax.ShapeDtypeStruct((t, 128), BF16),
                   jax.ShapeDtypeStruct((t, 128), BF16), jax.ShapeDtypeStruct((4, wq), F32),
                   jax.ShapeDtypeStruct((1, 128), F32), jax.ShapeDtypeStruct((1, 128), F32)],
        scratch_shapes=[pltpu.VMEM((HALO, wq), F32)],
        compiler_params=_cparams(),
    )(proj, proj, proj, proj, conv_a, a_log, dtb, dq, dk, dv, dgcs, dbeta)


def _c1_specs(order):
    hm = pl.BlockSpec((AH, CH, ADK), lambda n: (0, order(n), 0))
    col = pl.BlockSpec((CH, LANE), lambda n: (order(n), 0))
    qk = pl.BlockSpec((1, AH, CH, CH), lambda n: (order(n), 0, 0, 0))
    eg = pl.BlockSpec((1, AH, 1, ADK), lambda n: (order(n), 0, 0, 0))
    return hm, col, qk, eg


def _heads(ref):
    return jnp.stack([ref[:, h * ADK:(h + 1) * ADK] for h in range(AH)], axis=0)


def c1_fwd(q, k, v, gcs, beta):
    t = q.shape[1]
    nc = t // CH
    hm, col, qks, egs = _c1_specs(lambda n: n)

    def body(q_ref, k_ref, v_ref, g_ref, b_ref, u_ref, w_ref, qg_ref, kd_ref, qk_ref, eg_ref, ti_ref):
        u, w, qk, qg, kd, eg, tinv = c1_heads(q_ref[...], k_ref[...], v_ref[...], g_ref[...], b_ref[...])
        u_ref[...] = u
        w_ref[...] = w
        qg_ref[...] = qg
        kd_ref[...] = kd
        qk_ref[0] = qk
        eg_ref[0] = eg
        ti_ref[0] = tinv

    return pl.pallas_call(
        body, name="c1_fwd", grid=(nc,),
        in_specs=[hm, hm, hm, col, col],
        out_specs=[hm, hm, hm, hm, qks, egs, qks],
        out_shape=[jax.ShapeDtypeStruct((AH, t, ADK), F32)] * 4 + [
            jax.ShapeDtypeStruct((nc, AH, CH, CH), F32), jax.ShapeDtypeStruct((nc, AH, 1, ADK), F32),
            jax.ShapeDtypeStruct((nc, AH, CH, CH), F32)],
        compiler_params=_cparams(),
    )(q, k, v, gcs, beta)


def c1_bwd(q, k, v, gcs, beta, tinv, du, dw, dqg, dkd, dqk, deg):
    t = q.shape[1]
    nc = t // CH
    hm, col, qks, egs = _c1_specs(lambda n: n)

    def body(q_ref, k_ref, v_ref, g_ref, b_ref, ti_ref, du_ref, dw_ref, dqg_ref, dkd_ref, dqk_ref, deg_ref,
             dq_ref, dk_ref, dv_ref, dg_ref, db_ref):
        _, vjp = jax.vjp(lambda q_, k_, v_, g_, b_: c1_heads(q_, k_, v_, g_, b_, ti_ref[0]),
                         q_ref[...], k_ref[...], v_ref[...], g_ref[...], b_ref[...])
        dq, dk, dv, dg, db = vjp((du_ref[...], dw_ref[...], dqk_ref[0], dqg_ref[...], dkd_ref[...], deg_ref[0],
                                  jnp.zeros((AH, CH, CH), F32)))
        dq_ref[...] = dq
        dk_ref[...] = dk
        dv_ref[...] = dv
        dg_ref[...] = dg
        db_ref[...] = db

    return pl.pallas_call(
        body, name="c1_bwd", grid=(nc,),
        in_specs=[hm, hm, hm, col, col, qks, hm, hm, hm, hm, qks, egs],
        out_specs=[hm, hm, hm, col, col],
        out_shape=[jax.ShapeDtypeStruct((AH, t, ADK), F32)] * 3 + [jax.ShapeDtypeStruct((t, LANE), F32)] * 2,
        compiler_params=_cparams(),
    )(q, k, v, gcs, beta, tinv, du, dw, dqg, dkd, dqk, deg)


def c2_fwd(u, w, qg, kd, qk, eg, proj, norm_a):
    t = u.shape[1]
    nc = t // CH
    hm, _, qks, egs = _c1_specs(lambda n: n)
    tok = pl.BlockSpec((CH, D), lambda n: (n, 0))
    zspec = pl.BlockSpec((CH, D), lambda n: (n, C_Z // D))
    sspec = pl.BlockSpec((1, AH, ADK, ADK), lambda n: (n, 0, 0, 0))

    def body(u_ref, w_ref, qg_ref, kd_ref, qk_ref, eg_ref, z_ref, nw_ref, o_ref, sall_ref, st):
        n = pl.program_id(0)

        @pl.when(n == 0)
        def _():
            st[...] = jnp.zeros_like(st)

        s = st[...]
        sall_ref[0] = s
        og, s2 = c2_heads(s, u_ref[...], w_ref[...], qk_ref[0], qg_ref[...], kd_ref[...], eg_ref[0],
                          _heads(z_ref), nw_ref[...])
        st[...] = s2
        for h in range(AH):
            o_ref[:, h * ADK:(h + 1) * ADK] = og[h].astype(BF16)

    return pl.pallas_call(
        body, name="c2_fwd", grid=(nc,),
        in_specs=[hm, hm, hm, hm, qks, egs, zspec, _const((1, ADK))],
        out_specs=[tok, sspec],
        out_shape=[jax.ShapeDtypeStruct((t, D), BF16), jax.ShapeDtypeStruct((nc, AH, ADK, ADK), F32)],
        scratch_shapes=[pltpu.VMEM((AH, ADK, ADK), F32)],
        compiler_params=_cparams(),
    )(u, w, qg, kd, qk, eg, proj, norm_a)


def c2_bwd(u, w, qg, kd, qk, eg, proj, norm_a, sall, do):
    t = u.shape[1]
    nc = t // CH
    rev = lambda n: nc - 1 - n
    hm, _, qks, egs = _c1_specs(rev)
    tok = pl.BlockSpec((CH, D), lambda n: (rev(n), 0))
    zspec = pl.BlockSpec((CH, D), lambda n: (rev(n), C_Z // D))
    sspec = pl.BlockSpec((1, AH, ADK, ADK), lambda n: (rev(n), 0, 0, 0))

    def body(u_ref, w_ref, qg_ref, kd_ref, qk_ref, eg_ref, z_ref, nw_ref, sall_ref, do_ref,
             du_ref, dw_ref, dqg_ref, dkd_ref, dqk_ref, deg_ref, dz_ref, dnw_ref, dst):
        n = pl.program_id(0)

        @pl.when(n == 0)
        def _():
            dst[...] = jnp.zeros_like(dst)
            dnw_ref[...] = jnp.zeros_like(dnw_ref)

        _, vjp = jax.vjp(c2_heads, sall_ref[0], u_ref[...], w_ref[...], qk_ref[0], qg_ref[...], kd_ref[...],
                         eg_ref[0], _heads(z_ref), nw_ref[...])
        ds, du, dw, dqk, dqg, dkd, deg, dz, dn = vjp((_heads(do_ref), dst[...]))
        dst[...] = ds
        du_ref[...] = du
        dw_ref[...] = dw
        dqg_ref[...] = dqg
        dkd_ref[...] = dkd
        dqk_ref[0] = dqk
        deg_ref[0] = deg
        for h in range(AH):
            dz_ref[:, h * ADK:(h + 1) * ADK] = dz[h].astype(BF16)
        dnw_ref[...] += dn

    return pl.pallas_call(
        body, name="c2_bwd", grid=(nc,),
        in_specs=[hm, hm, hm, hm, qks, egs, zspec, _const((1, ADK)), sspec, tok],
        out_specs=[hm, hm, hm, hm, qks, egs, tok, _const((1, ADK))],
        out_shape=[jax.ShapeDtypeStruct((AH, t, ADK), F32)] * 4 + [
            jax.ShapeDtypeStruct((nc, AH, CH, CH), F32), jax.ShapeDtypeStruct((nc, AH, 1, ADK), F32),
            jax.ShapeDtypeStruct((t, D), BF16), jax.ShapeDtypeStruct((1, ADK), F32)],
        scratch_shapes=[pltpu.VMEM((AH, ADK, ADK), F32)],
        compiler_params=_cparams(),
    )(u, w, qg, kd, qk, eg, proj, norm_a, sall, do)


NQB = TQ // CH
NKB = 2 * TQ // CH
NDIST = BPREV + 1
KLO = -(NQB - 2)
NPAIR = NKB - 1 - KLO + 1


def bias_table(rel_bias):
    nh = rel_bias.shape[0]
    relx = jnp.concatenate([rel_bias, jnp.broadcast_to(rel_bias[:, -1:], (nh, CH * BPREV + 2 * CH - 1 - RELSZ))],
                           axis=1)
    t = jnp.stack([relx[:, CH * k:CH * k + 2 * CH - 1] for k in range(NDIST)], axis=1)
    trev = t[:, :, ::-1]
    g2 = jnp.concatenate([trev[:, :, CH - 1:], jnp.zeros((nh, NDIST, 1), F32), trev[:, :, :CH - 1]], axis=2)
    flat = jnp.tile(g2, (1, 1, CH + 1))[:, :, :CH * (2 * CH - 1)]
    blk = flat.reshape(nh, NDIST, CH, 2 * CH - 1)[..., :CH]
    neg = jnp.full((nh, NQB - 1, CH, CH), NEG, F32)
    asc = jnp.concatenate([neg, blk, neg], axis=1)
    return jnp.concatenate([asc[:, 1:], asc[:, :-1]], axis=-1)


SUBQ = 4 * CH
NSUB = TQ // SUBQ
KWIN = SUBQ + BPREV * CH


def assemble_bias(tab, r):
    b0 = r * SUBQ // (2 * CH)
    rows = [jnp.concatenate([tab[NQB + a - 2 * b - KLO] for b in range(b0, b0 + KWIN // (2 * CH))], axis=1)
            for a in range(r * SUBQ // CH, (r + 1) * SUBQ // CH)]
    return jnp.concatenate(rows, axis=0)


def bias_table_bwd_layout(dtab):
    nh = dtab.shape[0]
    dasc = (jnp.pad(dtab[..., :CH], ((0, 0), (1, 0), (0, 0), (0, 0)))
            + jnp.pad(dtab[..., CH:], ((0, 0), (0, 1), (0, 0), (0, 0))))
    dblk = dasc[:, NQB - 1:NQB - 1 + NDIST]
    dr = jnp.pad(dblk, ((0, 0), (0, 0), (0, 0), (0, CH - 1)))
    flat = jnp.pad(dr.reshape(nh, NDIST, CH * (2 * CH - 1)), ((0, 0), (0, 0), (0, 3 * CH)))
    return flat.reshape(nh, NDIST, CH + 1, 2 * CH).transpose(0, 2, 1, 3).reshape(nh, CH + 1, NDIST * 2 * CH)


def _fold_matrix_np():
    f = np.zeros((NDIST * 2 * CH, 384), np.float32)
    for k in range(NDIST):
        s = k
        for xx in range(2 * CH):
            if xx == CH:
                continue
            m = CH - 1 - xx if xx < CH else 3 * CH - 1 - xx
            f[s * 2 * CH + xx, min(CH * k + m, RELSZ - 1)] = 1.0
    return f


def relbias_reduce(dlay):
    nh, rows, cols = dlay.shape
    rpad = (-rows) % 8
    dlay = jnp.pad(dlay, ((0, 0), (0, rpad), (0, 0)))
    fold = jnp.asarray(_fold_matrix_np())

    def body(d_ref, f_ref, o_ref):
        cs = jnp.sum(d_ref[0], axis=0, keepdims=True)
        o_ref[0] = _mmh(jnp.broadcast_to(cs, (8, cols)), f_ref[...])

    out = pl.pallas_call(
        body, name="relbias_reduce", grid=(nh,),
        in_specs=[pl.BlockSpec((1, rows + rpad, cols), lambda h: (h, 0, 0)), _const((cols, 384))],
        out_specs=pl.BlockSpec((1, 8, 384), lambda h: (h, 0, 0)),
        out_shape=jax.ShapeDtypeStruct((nh, 8, 384), F32),
        compiler_params=_cparams(),
    )(dlay, fold)
    return out[:, 0, :RELSZ]


def attn_fwd(proj, bias):
    t = proj.shape[0]
    nt = t // TQ
    cb = C_QKVB // 128

    def body(q_ref, kp_ref, kc_ref, vp_ref, vc_ref, b_ref, o_ref):
        i = pl.program_id(1)
        firstf = jnp.where(i == 0, 1.0, 0.0)
        for r in range(NSUB):
            lo, hi = r * SUBQ, r * SUBQ + KWIN - TQ
            kw = jnp.concatenate([kp_ref[lo:, :], kc_ref[:hi, :]], axis=0)
            vw = jnp.concatenate([vp_ref[lo:, :], vc_ref[:hi, :]], axis=0)
            o_ref[lo:lo + SUBQ, :] = attn_sub(q_ref[lo:lo + SUBQ, :], kw, vw, b_ref[...], r, firstf).astype(BF16)

    def blk(off, prev):
        if prev:
            return pl.BlockSpec((TQ, 128), lambda p, i: (jnp.maximum(i - 1, 0), cb + off + p))
        return pl.BlockSpec((TQ, 128), lambda p, i: (i, cb + off + p))

    return pl.pallas_call(
        body, name="attn_fwd", grid=(BH // 2, nt),
        in_specs=[blk(0, False), blk(8, True), blk(8, False), blk(16, True), blk(16, False),
                  pl.BlockSpec((2, NPAIR, CH, 2 * CH), lambda p, i: (p, 0, 0, 0))],
        out_specs=pl.BlockSpec((TQ, 128), lambda p, i: (i, p)),
        out_shape=jax.ShapeDtypeStruct((t, D), BF16),
        compiler_params=_cparams(2),
    )(proj, proj, proj, proj, proj, bias)


def attn_bwd(proj, bias, do):
    t = proj.shape[0]
    nt = t // TQ
    cb = C_QKVB // 128

    def body(q_ref, kp_ref, kc_ref, vp_ref, vc_ref, b_ref, do_ref,
             dq_ref, dk_ref, dv_ref, db_ref, ck, cv, ak, av):
        i = pl.program_id(1)

        @pl.when(i == 0)
        def _():
            ck[...] = jnp.zeros_like(ck)
            cv[...] = jnp.zeros_like(cv)
            db_ref[...] = jnp.zeros_like(db_ref)

        @pl.when(i < nt)
        def _():
            firstf = jnp.where(i == 0, 1.0, 0.0)
            ak[...] = jnp.zeros_like(ak)
            av[...] = jnp.zeros_like(av)
            db = None
            for r in range(NSUB):
                lo, hi = r * SUBQ, r * SUBQ + KWIN - TQ
                kw = jnp.concatenate([kp_ref[lo:, :], kc_ref[:hi, :]], axis=0).astype(F32)
                vw = jnp.concatenate([vp_ref[lo:, :], vc_ref[:hi, :]], axis=0).astype(F32)
                _, vjp = jax.vjp(lambda q, k, v, b: attn_sub(q, k, v, b, r, firstf),
                                 q_ref[lo:lo + SUBQ, :].astype(F32), kw, vw, b_ref[...])
                dq, dkw, dvw, dbr = vjp(do_ref[lo:lo + SUBQ, :])
                dq_ref[lo:lo + SUBQ, :] = dq.astype(BF16)
                ak[lo:lo + KWIN, :] += dkw
                av[lo:lo + KWIN, :] += dvw
                db = dbr if db is None else db + dbr
            dk_ref[...] = (ck[...] + ak[:TQ, :]).astype(BF16)
            dv_ref[...] = (cv[...] + av[:TQ, :]).astype(BF16)
            ck[...] = ak[TQ:, :]
            cv[...] = av[TQ:, :]
            db_ref[...] += db

        @pl.when(i == nt)
        def _():
            dk_ref[...] = ck[...].astype(BF16)
            dv_ref[...] = cv[...].astype(BF16)

    def blk(off, prev):
        if prev:
            return pl.BlockSpec((TQ, 128), lambda p, i: (jnp.clip(i - 1, 0, nt - 1), cb + off + p))
        return pl.BlockSpec((TQ, 128), lambda p, i: (jnp.minimum(i, nt - 1), cb + off + p))

    own = pl.BlockSpec((TQ, 128), lambda p, i: (jnp.minimum(i, nt - 1), p))
    lag = pl.BlockSpec((TQ, 128), lambda p, i: (jnp.maximum(i - 1, 0), p))
    return pl.pallas_call(
        body, name="attn_bwd", grid=(BH // 2, nt + 1),
        in_specs=[blk(0, False), blk(8, True), blk(8, False), blk(16, True), blk(16, False),
                  pl.BlockSpec((2, NPAIR, CH, 2 * CH), lambda p, i: (p, 0, 0, 0)), own],
        out_specs=[own, lag, lag, pl.BlockSpec((2, NPAIR, CH, 2 * CH), lambda p, i: (p, 0, 0, 0))],
        out_shape=[jax.ShapeDtypeStruct((t, D), BF16)] * 3 + [jax.ShapeDtypeStruct((BH, NPAIR, CH, 2 * CH), F32)],
        scratch_shapes=[pltpu.VMEM((TQ, 128), F32), pltpu.VMEM((TQ, 128), F32),
                        pltpu.VMEM((2 * TQ, 128), F32), pltpu.VMEM((2 * TQ, 128), F32)],
        compiler_params=_cparams(2),
    )(proj, proj, proj, proj, proj, bias, do)


MERGE_TM = 256


def merge_fwd(x, oa, ob, proj, vecs, wa, wb, wo):
    t = x.shape[0]
    tm = MERGE_TM
    names = ("bga", "bgb", "gate_t", "g1", "b1", "scale_f", "shift_f")

    def body(x_ref, oa_ref, ob_ref, gra_ref, grb_ref, *rest):
        vrefs = rest[:7]
        wa_ref, wb_ref, wo_ref, y_ref, h_ref = rest[7:]
        vv = [r[...] for r in vrefs]
        zero = jnp.zeros((tm, D), F32)
        y1, _ = merge_fn(x_ref[...], oa_ref[...], ob_ref[...], gra_ref[...], grb_ref[...], zero, zero, zero,
                         *vv, wa_ref[...], wb_ref[...], wo_ref[...])
        y_ref[...] = y1
        h_ref[...] = (y1 * (1.0 + vv[5]) + vv[6]).astype(BF16)

    return pl.pallas_call(
        body, name="merge_fwd", grid=(t // tm,),
        in_specs=[_rows(tm, D), _rows(tm, D), _rows(tm, D), _rows(tm, D, C_GATE // D), _rows(tm, D, C_GATE // D + 1)]
        + [_const((1, D))] * 7 + [_const((D, D))] * 3,
        out_specs=[_rows(tm, D), _rows(tm, D)],
        out_shape=[jax.ShapeDtypeStruct((t, D), F32), jax.ShapeDtypeStruct((t, D), BF16)],
        compiler_params=_cparams(),
    )(x, oa, ob, proj, proj, *[vecs[n] for n in names], wa, wb, wo)


def merge_bwd(x, oa, ob, proj, vecs, wa, wb, wo, dy1):
    t = x.shape[0]
    tm = MERGE_TM
    names = ("bga", "bgb", "gate_t", "g1", "b1", "scale_f", "shift_f")

    def body(x_ref, oa_ref, ob_ref, gra_ref, grb_ref, *rest):
        vrefs = rest[:7]
        wa_ref, wb_ref, wo_ref, dy_ref = rest[7:11]
        (dx_ref, doa_ref, dob_ref, dga_ref, dgb_ref, mg_ref, dmix_ref, dpa_ref, dpb_ref,
         dbga_ref, dbgb_ref, dgt_ref, dg1_ref, db1_ref) = rest[11:]
        i = pl.program_id(0)
        vv = [r[...] for r in vrefs]
        zero = jnp.zeros((tm, D), F32)

        def f(x_, oa_, ob_, gra_, grb_, ppa, ppb, pmix, bga, bgb, gate_t, g1, b1):
            return merge_fn(x_, oa_, ob_, gra_, grb_, ppa, ppb, pmix, bga, bgb, gate_t, g1, b1, vv[5], vv[6],
                            wa_ref[...], wb_ref[...], wo_ref[...])

        _, vjp, merged = jax.vjp(f, x_ref[...], oa_ref[...].astype(F32), ob_ref[...].astype(F32),
                                 gra_ref[...], grb_ref[...], zero, zero, zero, *vv[:5], has_aux=True)
        dx, doa, dob, dga, dgb, dpa, dpb, dmix, dbga, dbgb, dgt, dg1, db1 = vjp(dy_ref[...])
        dx_ref[...] = dx
        doa_ref[...] = doa
        dob_ref[...] = dob
        dga_ref[...] = dga.astype(BF16)
        dgb_ref[...] = dgb.astype(BF16)
        mg_ref[...] = merged.astype(BF16)
        dmix_ref[...] = dmix.astype(BF16)
        dpa_ref[...] = dpa.astype(BF16)
        dpb_ref[...] = dpb.astype(BF16)
        accs = (dbga_ref, dbgb_ref, dgt_ref, dg1_ref, db1_ref)

        @pl.when(i == 0)
        def _():
            for a in accs:
                a[...] = jnp.zeros_like(a)

        for a, val in zip(accs, (dbga, dbgb, dgt, dg1, db1)):
            a[...] += val

    return pl.pallas_call(
        body, name="merge_bwd", grid=(t // tm,),
        in_specs=[_rows(tm, D), _rows(tm, D), _rows(tm, D), _rows(tm, D, C_GATE // D), _rows(tm, D, C_GATE // D + 1)]
        + [_const((1, D))] * 7 + [_const((D, D))] * 3 + [_rows(tm, D)],
        out_specs=[_rows(tm, D)] * 9 + [_const((1, D))] * 5,
        out_shape=[jax.ShapeDtypeStruct((t, D), F32)] * 3 + [jax.ShapeDtypeStruct((t, D), BF16)] * 6
        + [jax.ShapeDtypeStruct((1, D), F32)] * 5,
        compiler_params=_cparams(),
    )(x, oa, ob, proj, proj, *[vecs[n] for n in names], wa, wb, wo, dy1)


FFN_TM = 128


def ffn_act_fwd(up, conv_w, bconv):
    t, wdt = up.shape
    tm = FFN_TM

    def body(prev_ref, cur_ref, cw_ref, bc_ref, a_ref):
        i = pl.program_id(0)
        flag = jnp.where(i > 0, 1.0, 0.0)

        def ext(sl):
            return jnp.concatenate([prev_ref[:, sl] * flag, cur_ref[:, sl]], axis=0)

        def rows(sl):
            return tuple(cw_ref[j:j + 1, sl] for j in range(3))

        for cb in range(DFF // LANE):
            g = slice(cb * LANE, (cb + 1) * LANE)
            v = slice(DFF + cb * LANE, DFF + (cb + 1) * LANE)
            a_ref[:, g] = ffn_act_fn(ext(g), ext(v), rows(g), rows(v), bc_ref[:, g], bc_ref[:, v]).astype(BF16)

    return pl.pallas_call(
        body, name="ffn_act_fwd", grid=(t // tm,),
        in_specs=_halo_specs(tm, wdt, 0, lambda i: i) + [_const((3, wdt)), _const((1, wdt))],
        out_specs=_rows(tm, DFF),
        out_shape=jax.ShapeDtypeStruct((t, DFF), BF16),
        compiler_params=_cparams(),
    )(up, up, conv_w, bconv)


def ffn_act_bwd(up, conv_w, bconv, da):
    t, wdt = up.shape
    tm = FFN_TM
    nt = t // tm
    rev = lambda i: nt - 1 - i

    def body(prev_ref, cur_ref, cw_ref, bc_ref, da_ref, dup_ref, dcw_ref, dbc_ref, carry):
        i = pl.program_id(0)
        flag = jnp.where(i < nt - 1, 1.0, 0.0)

        @pl.when(i == 0)
        def _():
            carry[...] = jnp.zeros_like(carry)
            dcw_ref[...] = jnp.zeros_like(dcw_ref)
            dbc_ref[...] = jnp.zeros_like(dbc_ref)

        def ext(sl):
            return jnp.concatenate([prev_ref[:, sl] * flag, cur_ref[:, sl]], axis=0)

        def rows(sl):
            return tuple(cw_ref[j:j + 1, sl] for j in range(3))

        def emit(sl, dext, drows, dbc):
            dcur = dext[HALO:]
            dup_ref[:, sl] = jnp.concatenate([dcur[:tm - HALO], dcur[tm - HALO:] + carry[:, sl]], axis=0).astype(BF16)
            carry[:, sl] = dext[:HALO]
            dcw_ref[:, sl] += _stack_rows(drows)
            dbc_ref[:, sl] += dbc

        for cb in range(DFF // LANE):
            g = slice(cb * LANE, (cb + 1) * LANE)
            v = slice(DFF + cb * LANE, DFF + (cb + 1) * LANE)
            _, vjp = jax.vjp(ffn_act_fn, ext(g), ext(v), rows(g), rows(v), bc_ref[:, g], bc_ref[:, v])
            dxg, dxv, drg, drv, dbg, dbv = vjp(da_ref[:, g])
            emit(g, dxg, drg, dbg)
            emit(v, dxv, drv, dbv)

    return pl.pallas_call(
        body, name="ffn_act_bwd", grid=(nt,),
        in_specs=_halo_specs(tm, wdt, 0, rev) + [_const((3, wdt)), _const((1, wdt)), _rows(tm, DFF, 0, rev)],
        out_specs=[_rows(tm, wdt, 0, rev), _const((3, wdt)), _const((1, wdt))],
        out_shape=[jax.ShapeDtypeStruct((t, wdt), BF16), jax.ShapeDtypeStruct((3, wdt), F32),
                   jax.ShapeDtypeStruct((1, wdt), F32)],
        scratch_shapes=[pltpu.VMEM((HALO, wdt), F32)],
        compiler_params=_cparams(),
    )(up, up, conv_w, bconv, da)


HEAD_TM = 256


def head_fwd_bwd(a, y1, tgt, gate_f, g2, b2, wd):
    t = a.shape[0]
    tm = HEAD_TM

    def body(a_ref, y_ref, t_ref, gf_ref, g2_ref, b2_ref, wd_ref,
             da_ref, dy_ref, dffn_ref, dgf_ref, dg2_ref, db2_ref, loss_ref):
        i = pl.program_id(0)
        zero = jnp.zeros((tm, D), F32)

        def f(a_, y_, pf, gf, g2_, b2_):
            return head_fn(a_, y_, pf, gf, g2_, b2_, t_ref[...], wd_ref[...])

        loss, vjp = jax.vjp(f, a_ref[...].astype(F32), y_ref[...], zero, gf_ref[...], g2_ref[...], b2_ref[...])
        da, dy, dffn, dgf, dg2, db2 = vjp(jnp.ones((), F32))
        da_ref[...] = da
        dy_ref[...] = dy
        dffn_ref[...] = dffn.astype(BF16)
        accs = (dgf_ref, dg2_ref, db2_ref, loss_ref)

        @pl.when(i == 0)
        def _():
            for r in accs:
                r[...] = jnp.zeros_like(r)

        dgf_ref[...] += dgf
        dg2_ref[...] += dg2
        db2_ref[...] += db2
        loss_ref[...] += loss * jnp.ones((1, 128), F32)

    return pl.pallas_call(
        body, name="head_fwd_bwd", grid=(t // tm,),
        in_specs=[_rows(tm, DFF), _rows(tm, D), _rows(tm, D), _const((1, D)), _const((1, D)), _const((1, D)),
                  _const((DFF, D))],
        out_specs=[_rows(tm, DFF), _rows(tm, D), _rows(tm, D), _const((1, D)), _const((1, D)), _const((1, D)),
                   _const((1, 128))],
        out_shape=[jax.ShapeDtypeStruct((t, DFF), F32), jax.ShapeDtypeStruct((t, D), F32),
                   jax.ShapeDtypeStruct((t, D), BF16)] + [jax.ShapeDtypeStruct((1, D), F32)] * 3
        + [jax.ShapeDtypeStruct((1, 128), F32)],
        compiler_params=_cparams(),
    )(a, y1, tgt, gate_f, g2, b2, wd)


def ada_fwd(c_all, w_sh, b_sh):
    def body(c_ref, w_ref, b_ref, o_ref):
        o_ref[...] = _mmh(_silu(c_ref[...]), w_ref[...]) + b_ref[...]

    n = w_sh.shape[1]
    return pl.pallas_call(
        body, name="ada_fwd", out_shape=jax.ShapeDtypeStruct((NDEV, n), F32),
        in_specs=[pl.BlockSpec(memory_space=pltpu.VMEM)] * 3,
        out_specs=pl.BlockSpec(memory_space=pltpu.VMEM),
        compiler_params=pltpu.CompilerParams(vmem_limit_bytes=VMEM_LIMIT),
    )(c_all, w_sh, b_sh)


def ada_wgrad(c_all_t, dmod_sh):
    def body(c_ref, d_ref, o_ref):
        o_ref[...] = _mmh(_silu(c_ref[...]), d_ref[...])

    return pl.pallas_call(
        body, name="ada_wgrad", out_shape=jax.ShapeDtypeStruct((c_all_t.shape[0], dmod_sh.shape[1]), F32),
        in_specs=[pl.BlockSpec(memory_space=pltpu.VMEM)] * 2,
        out_specs=pl.BlockSpec(memory_space=pltpu.VMEM),
        compiler_params=pltpu.CompilerParams(vmem_limit_bytes=VMEM_LIMIT),
    )(c_all_t, dmod_sh)


def adamw(gparts, w, m, v, name):
    p, r, c = gparts.shape
    tr = r if r <= 256 else _pick(r, (256, 128, 64, 32, 16, 8))
    c1 = 1.0 - B1 ** STEP
    c2 = 1.0 - B2 ** STEP

    def body(g_ref, w_ref, m_ref, v_ref, go_ref, d_ref, mo_ref, vo_ref):
        g = g_ref[0].astype(F32)
        for s in range(1, p):
            g = g + g_ref[s].astype(F32)
        mn = B1 * m_ref[0] + (1.0 - B1) * g
        vn = B2 * v_ref[0] + (1.0 - B2) * (g * g)
        go_ref[0] = g
        d_ref[0] = -LR * ((mn / c1) / (jnp.sqrt(vn / c2) + AEPS) + WD * w_ref[0])
        mo_ref[0] = mn
        vo_ref[0] = vn

    spec = pl.BlockSpec((1, tr, c), lambda i: (0, i, 0))
    return pl.pallas_call(
        body, name=name, grid=(r // tr,),
        in_specs=[pl.BlockSpec((p, tr, c), lambda i: (0, i, 0)), spec, spec, spec],
        out_specs=[spec] * 4,
        out_shape=[jax.ShapeDtypeStruct((1, r, c), F32)] * 4,
        compiler_params=_cparams(),
    )(gparts, w, m, v)


def _me():
    x, y, c = lax.axis_index("x"), lax.axis_index("y"), lax.axis_index("c")
    return x, y, c, 4 * x + 2 * y + c


def _peer(x, y, c, d):
    px = 1 - x if (d >> 2) & 1 else x
    py = 1 - y if (d >> 1) & 1 else y
    pc = 1 - c if d & 1 else c
    return (px, py, pc), 4 * px + 2 * py + pc


def _exchange(arrs, name, scatter):
    n = len(arrs)

    def body(*refs):
        ins, outs = refs[:n], refs[n:2 * n]
        send, recv, lsem = refs[2 * n:]
        x, y, c, me = _me()
        remote, local = [], []
        for k in range(n):
            src = ins[k].at[me] if scatter else ins[k]
            cp = pltpu.make_async_copy(src, outs[k].at[me], lsem.at[k])
            cp.start()
            local.append(cp)
            for d in range(1, NDEV):
                dev, pid = _peer(x, y, c, d)
                src = ins[k].at[pid] if scatter else ins[k]
                cp = pltpu.make_async_remote_copy(src_ref=src, dst_ref=outs[k].at[me],
                                                  send_sem=send.at[k, d - 1], recv_sem=recv.at[k, d - 1],
                                                  device_id=dev, device_id_type=pl.DeviceIdType.MESH)
                cp.start()
                remote.append(cp)
        for cp in remote:
            cp.wait()
        for cp in local:
            cp.wait()

    shapes = [a.shape if scatter else (NDEV,) + a.shape for a in arrs]
    return pl.pallas_call(
        body, name=name,
        in_specs=[pl.BlockSpec(memory_space=pl.ANY)] * n,
        out_specs=[pl.BlockSpec(memory_space=pl.ANY)] * n,
        out_shape=[jax.ShapeDtypeStruct(s, a.dtype) for s, a in zip(shapes, arrs)],
        scratch_shapes=[pltpu.SemaphoreType.DMA((n, NDEV - 1)), pltpu.SemaphoreType.DMA((n, NDEV - 1)),
                        pltpu.SemaphoreType.DMA((n,))],
        compiler_params=pltpu.CompilerParams(has_side_effects=True),
    )(*arrs)


def all_gather(arrs, name):
    return _exchange(arrs, name, False)


def all_gather_two_level(shard, name):
    def body(x_ref, out_ref, send, recv, lsem):
        x, y, c, _ = _me()
        sibling = (x, y, 1 - c)
        chips = [(1 - x, y), (x, 1 - y), (1 - x, 1 - y)]

        def slot(px, py, pc):
            return out_ref.at[4 * px + 2 * py + pc]

        def copy(k, block, to, src=None):
            return pltpu.make_async_remote_copy(
                src_ref=slot(*block) if src is None else src, dst_ref=slot(*block),
                send_sem=send.at[k], recv_sem=recv.at[k], device_id=to, device_id_type=pl.DeviceIdType.MESH)

        mine = pltpu.make_async_copy(x_ref, slot(x, y, c), lsem)
        mine.start()
        first = [copy(0, (x, y, c), sibling, src=x_ref)]
        first += [copy(1 + j, (x, y, c), (*chip, c), src=x_ref) for j, chip in enumerate(chips)]
        for cp in first:
            cp.start()
        passed = [copy(4 + j, (*chip, c), sibling) for j, chip in enumerate(chips)]
        for j, chip in enumerate(chips):
            copy(1 + j, (*chip, c), (x, y, c)).wait_recv()
            passed[j].start()
        copy(0, sibling, (x, y, c)).wait_recv()
        for j, chip in enumerate(chips):
            copy(4 + j, (*chip, 1 - c), (x, y, c)).wait_recv()
        for cp in first + passed:
            cp.wait_send()
        mine.wait()

    return pl.pallas_call(
        body, name=name,
        in_specs=[pl.BlockSpec(memory_space=pl.ANY)],
        out_specs=pl.BlockSpec(memory_space=pl.ANY),
        out_shape=jax.ShapeDtypeStruct((NDEV,) + shard.shape, shard.dtype),
        scratch_shapes=[pltpu.SemaphoreType.DMA((NPEER,)), pltpu.SemaphoreType.DMA((NPEER,)),
                        pltpu.SemaphoreType.DMA],
        compiler_params=pltpu.CompilerParams(has_side_effects=True),
    )(shard)


def all_to_all(arrs, name):
    return _exchange(arrs, name, True)


_HBM = pl.BlockSpec(memory_space=pltpu.HBM)
_SEM = pl.BlockSpec(memory_space=pltpu.SEMAPHORE)
_EFFECT = pltpu.SideEffectType.DATAFLOW_SIDE_EFFECTING
NPEER = NDEV - 1


def exchange_start(arrs, name, scatter):
    n = len(arrs)
    lands = [lax.empty(a.shape if scatter else (NDEV,) + a.shape, a.dtype) for a in arrs]

    def body(*refs):
        ins, lrefs = refs[:n], refs[n:2 * n]
        send, recv, token = refs[2 * n], refs[2 * n + 1], refs[-1]
        x, y, c, me = _me()
        for k in range(n):
            for d in range(1, NDEV):
                dev, pid = _peer(x, y, c, d)
                src = ins[k].at[pid] if scatter else ins[k]
                pltpu.make_async_remote_copy(src_ref=src, dst_ref=lrefs[k].at[me],
                                             send_sem=send.at[k * NPEER + d - 1], recv_sem=recv.at[k * NPEER + d - 1],
                                             device_id=dev, device_id_type=pl.DeviceIdType.MESH).start()
        token[...] = jnp.zeros_like(token)

    thru = [pltpu.HBM(a.shape, a.dtype) for a in list(arrs) + lands]
    outs = pl.pallas_call(
        body, name=name,
        out_shape=(pltpu.SemaphoreType.DMA((n * NPEER,)), pltpu.SemaphoreType.DMA((n * NPEER,)), *thru,
                   jax.ShapeDtypeStruct((8, 128), F32)),
        in_specs=[_HBM] * (2 * n),
        out_specs=(_SEM, _SEM, *([_HBM] * (2 * n)), pl.BlockSpec(memory_space=pltpu.VMEM)),
        input_output_aliases={i: 2 + i for i in range(2 * n)},
        compiler_params=pltpu.CompilerParams(has_side_effects=_EFFECT),
    )(*[pltpu.with_memory_space_constraint(a, pltpu.HBM) for a in list(arrs) + lands])
    handle = dict(send=outs[0], recv=outs[1], src=list(outs[2:2 + n]), land=list(outs[2 + n:2 + 2 * n]),
                  scatter=scatter)
    return handle, outs[-1][0, 0]


def exchange_wait(handle, after, name):
    n = len(handle["src"])
    scatter = handle["scatter"]

    def body(*refs):
        ins, lrefs = refs[:n], refs[n:2 * n]
        send, recv = refs[2 * n], refs[2 * n + 1]
        x, y, c, _ = _me()
        for k in range(n):
            for d in range(1, NDEV):
                dev, _ = _peer(x, y, c, d)
                src = ins[k].at[0] if scatter else ins[k]
                cp = pltpu.make_async_remote_copy(src_ref=src, dst_ref=lrefs[k].at[0],
                                                  send_sem=send.at[k * NPEER + d - 1],
                                                  recv_sem=recv.at[k * NPEER + d - 1],
                                                  device_id=dev, device_id_type=pl.DeviceIdType.MESH)
                cp.wait_send()
                cp.wait_recv()

    arrs = handle["src"] + handle["land"]
    outs = pl.pallas_call(
        body, name=name,
        out_shape=tuple(pltpu.HBM(a.shape, a.dtype) for a in arrs),
        in_specs=[_HBM] * (2 * n) + [_SEM, _SEM, pl.BlockSpec(memory_space=pl.ANY)],
        out_specs=tuple([_HBM] * (2 * n)),
        input_output_aliases={i: i for i in range(2 * n)},
        compiler_params=pltpu.CompilerParams(has_side_effects=_EFFECT),
    )(*arrs, handle["send"], handle["recv"], after)
    me = 4 * lax.axis_index("x") + 2 * lax.axis_index("y") + lax.axis_index("c")
    landed = []
    for own, land in zip(outs[:n], outs[n:]):
        mine = lax.dynamic_index_in_dim(own, me, 0, keepdims=True) if scatter else own[None]
        landed.append(lax.dynamic_update_slice_in_dim(land, mine, me, 0))
    return landed


def _cat_from_slabs(slabs):
    _, k, n = slabs.shape

    def cols(lo, hi):
        parts, c = [], lo
        while c < hi:
            j = c // n
            e = min(hi, (j + 1) * n)
            parts.append(slabs[j][:, c - j * n:e - j * n])
            c = e
        return parts

    def zeros(w):
        return [jnp.zeros((k, w), slabs.dtype)]

    return jnp.concatenate(cols(0, 4096) + cols(4112, 9232) + cols(4096, 4104) + zeros(LANE - AH)
                           + cols(4104, 4112) + zeros(NCAT - C_BA - LANE - AH), axis=1)


IN_PIECES = (("pre", C_QKVA, 3072), ("z", C_Z, 1024), ("qb", C_QKVB, 1024), ("kb", C_QKVB + 1024, 1024),
             ("vb", C_QKVB + 2048, 1024), ("ga", C_GATE, 1024), ("gb", C_GATE + 1024, 1024))
_ORIG_SEGS = ((0, 3072, "pre", 0), (3072, 4096, "z", 0), (4096, 4104, "ba", 0), (4104, 4112, "ba", LANE),
              (4112, 5136, "qb", 0), (5136, 6160, "kb", 0), (6160, 7184, "vb", 0), (7184, 8208, "ga", 0),
              (8208, 9232, "gb", 0))


def _orig_cols_from_pieces(gp, lo, hi):
    parts = []
    for a, b, name, off in _ORIG_SEGS:
        s, e = max(a, lo), min(b, hi)
        if s < e:
            parts.append(gp[name][:, off + s - a:off + e - a])
    return parts[0] if len(parts) == 1 else jnp.concatenate(parts, axis=1)


def _pad128(v):
    return jnp.pad(v, ((0, 0), (0, 128 - v.shape[1])))


def local_step(x, tgt, mod, wts, small, late_weights=None, on_grads=None):
    if on_grads is None:
        on_grads = lambda group, gd: jnp.zeros((), F32)
    t = x.shape[0]
    nc = t // CH
    shift_t, scale_t, gate_t, shift_f, scale_f, gate_f = mod
    wcat = _cat_from_slabs(wts["w_in_slabs"])
    a_log = _pad128(small["a_log"])
    dtb = _pad128(small["dt_bias"])
    vecs = dict(bga=small["b_gate"][:, :D], bgb=small["b_gate"][:, D:], gate_t=gate_t, g1=small["ln1_g"],
                b1=small["ln1_b"], scale_f=scale_f, shift_f=shift_f)

    h1 = modulate(x, scale_t, shift_t, "modulate_t")
    proj = matmul(h1, wcat, F32, "in_proj")
    q, k, v, gcs, beta = prep_fwd(proj, small["conv_a"], a_log, dtb)

    u, w, qg, kd, qk, eg, tinv = c1_fwd(q, k, v, gcs, beta)
    oa, sall = c2_fwd(u, w, qg, kd, qk, eg, proj, small["norm_a"])
    bias = bias_table(small["rel_bias"])
    ob = attn_fwd(proj, bias)
    if late_weights is not None:
        wts = {**wts, **late_weights(ob)}
    y1, h2 = merge_fwd(x, oa, ob, proj, vecs, wts["w_a"], wts["w_b"], wts["w_o"])
    up = matmul(h2, wts["w_up"], F32, "up_proj")
    a = ffn_act_fwd(up, small["conv_ffn"], small["b_conv_ffn"])

    da, dy1_res, dffn, dgate_f, dg2, db2, loss = head_fwd_bwd(a, y1, tgt, gate_f, small["ln2_g"], small["ln2_b"],
                                                            wts["w_down"])
    g_w_down = matmul(a, dffn, F32, "wgrad_down", ta=True)
    dup, g_conv_ffn, g_bconv = ffn_act_bwd(up, small["conv_ffn"], small["b_conv_ffn"], da)
    dh2 = matmul(dup, wts["w_up"], F32, "dgrad_up", tb=True)
    g_w_up = matmul(h2, dup, F32, "wgrad_up", ta=True)
    tok = on_grads("ffn", dict(w_up=g_w_up, w_down=g_w_down))
    dy1, dscale_f, dshift_f = modulate_bwd(dh2, y1, dy1_res, scale_f + tok, "modulate_f_bwd")
    (dx_res, doa, dob, dga, dgb, merged, dmix, dpa, dpb,
     dbga, dbgb, dgate_t, dg1, db1) = merge_bwd(x, oa, ob, proj, vecs, wts["w_a"], wts["w_b"], wts["w_o"], dy1)
    g_w_o = matmul(merged, dmix, F32, "wgrad_o", ta=True)
    g_w_a = matmul(oa, dpa, F32, "wgrad_a", ta=True)
    g_w_b = matmul(ob, dpb, F32, "wgrad_b", ta=True)
    tok = on_grads("mix", dict(w_o=g_w_o, w_a=g_w_a, w_b=g_w_b))
    dqb, dkb, dvb, dbias = attn_bwd(proj, bias, dob)
    g_rel = relbias_reduce(bias_table_bwd_layout(dbias))
    du, dw, dqg, dkd, dqk, deg, dz, g_norm = c2_bwd(u, w, qg, kd, qk, eg, proj, small["norm_a"] + tok, sall, doa)
    dq, dk, dv, dgcs, dbeta = c1_bwd(q, k, v, gcs, beta, tinv, du, dw, dqg, dkd, dqk, deg)
    dpre, dbb, daa, g_conv_a, g_alog, g_dtb = prep_bwd(proj, small["conv_a"], a_log, dtb, dq, dk, dv, dgcs, dbeta)
    tok = on_grads("small", dict(conv_a=g_conv_a, rel_bias=g_rel, conv_ffn=g_conv_ffn))
    dba = jnp.concatenate([dbb, daa, jnp.zeros((t, NCAT - C_BA - 2 * LANE), BF16)], axis=1) + tok.astype(BF16)
    dpieces = dict(pre=dpre, z=dz, qb=dqb, kb=dkb, vb=dvb, ga=dga, gb=dgb)
    g_in = {n: matmul(h1, dpieces[n], F32, "wgrad_in_" + n, ta=True) for n, _, _ in IN_PIECES}
    g_in["ba"] = matmul(h1, dba, F32, "wgrad_in_ba", ta=True)
    tok = on_grads("in", g_in)
    dh1 = dgrad_pieces([(dpieces[n], off) for n, off, _ in IN_PIECES], dba, wcat + tok.astype(BF16),
                       "dgrad_in")
    grad_x, dscale_t, dshift_t = modulate_bwd(dh1, x, dx_res, scale_t + tok, "modulate_t_bwd")

    dmod = (dshift_t, dscale_t, dgate_t, dshift_f, dscale_f, dgate_f)
    grads = dict(w_in=_orig_cols_from_pieces(g_in, 0, 9232), w_up=g_w_up, w_down=g_w_down, w_a=g_w_a, w_b=g_w_b, w_o=g_w_o,
                 conv_a=g_conv_a, rel_bias=g_rel, conv_ffn=g_conv_ffn,
                 b_gate=jnp.concatenate([dbga, dbgb], axis=1), a_log=g_alog[:, :AH], dt_bias=g_dtb[:, :AH],
                 norm_a=g_norm, ln1_g=dg1, ln1_b=db1, b_conv_ffn=g_bconv, ln2_g=dg2, ln2_b=db2)
    return loss[0, 0], grad_x, dmod, grads


_REP = {}
_off = 0
for _n, _wd, _pw in (("b_ada", 6144, 6144), ("b_gate", 2048, 2048), ("a_log", 8, 128), ("dt_bias", 8, 128),
                     ("norm_a", 128, 128), ("ln1_g", 1024, 1024), ("ln1_b", 1024, 1024),
                     ("b_conv_ffn", 5632, 5632), ("ln2_g", 1024, 1024), ("ln2_b", 1024, 1024), ("loss", 1, 128)):
    _REP[_n] = (_off, _wd, _pw)
    _off += _pw
REP_LEN = _off
REP_NAMES = [n for n in _REP if n != "loss"]
_SH = (("conv_a", (4, 384)), ("rel_bias", (16, 40)), ("conv_ffn", (3, 704)))
SH_LEN = 4352


def _pack_rep(vals):
    parts = []
    for n, (_, wd, pw) in _REP.items():
        a = vals.get(n)
        a = jnp.zeros((1, pw), F32) if a is None else jnp.pad(a.reshape(1, wd), ((0, 0), (0, pw - wd)))
        parts.append(a)
    return jnp.concatenate(parts, axis=1)


def _unpack_rep(vec, name):
    o, wd, _ = _REP[name]
    return vec[:, o:o + wd]


def _pack_sh(vals):
    parts = [vals[n].reshape(vals[n].shape[:-2] + (-1,)) for n, _ in _SH]
    a = jnp.concatenate(parts, axis=-1)
    return jnp.pad(a, [(0, 0)] * (a.ndim - 1) + [(0, SH_LEN - a.shape[-1])])


def _unpack_sh(vec, name):
    o = 0
    for n, shp in _SH:
        sz = shp[0] * shp[1]
        if n == name:
            return vec[0, o:o + sz].reshape(shp)
        o += sz
    raise KeyError(name)


def _col_shards(a, n):
    return a.reshape(a.shape[0], NDEV, n).transpose(1, 0, 2)


def kernel(x, c, w_ada, b_ada, w_in, b_gate, conv_a, a_log, dt_bias, norm_a, rel_bias, w_branch_a, w_branch_b, w_o, ln1_g, ln1_b, w_up, conv_ffn, b_conv_ffn, w_down, ln2_g, ln2_b, loss_target, m_w_ada, m_b_ada, m_w_in, m_b_gate, m_conv_a, m_a_log, m_dt_bias, m_norm_a, m_rel_bias, m_w_branch_a, m_w_branch_b, m_w_o, m_ln1_g, m_ln1_b, m_w_up, m_conv_ffn, m_b_conv_ffn, m_w_down, m_ln2_g, m_ln2_b, v_w_ada, v_b_ada, v_w_in, v_b_gate, v_conv_a, v_a_log, v_dt_bias, v_norm_a, v_rel_bias, v_w_branch_a, v_w_branch_b, v_w_o, v_ln1_g, v_ln1_b, v_w_up, v_conv_ffn, v_b_conv_ffn, v_w_down, v_ln2_g, v_ln2_b):
    W = dict(w_ada=w_ada, b_ada=b_ada, w_in=w_in, b_gate=b_gate, conv_a=conv_a, a_log=a_log, dt_bias=dt_bias,
             norm_a=norm_a, rel_bias=rel_bias, w_branch_a=w_branch_a, w_branch_b=w_branch_b, w_o=w_o, ln1_g=ln1_g,
             ln1_b=ln1_b, w_up=w_up, conv_ffn=conv_ffn, b_conv_ffn=b_conv_ffn, w_down=w_down, ln2_g=ln2_g,
             ln2_b=ln2_b)
    M = dict(w_ada=m_w_ada, b_ada=m_b_ada, w_in=m_w_in, b_gate=m_b_gate, conv_a=m_conv_a, a_log=m_a_log,
             dt_bias=m_dt_bias, norm_a=m_norm_a, rel_bias=m_rel_bias, w_branch_a=m_w_branch_a,
             w_branch_b=m_w_branch_b, w_o=m_w_o, ln1_g=m_ln1_g, ln1_b=m_ln1_b, w_up=m_w_up, conv_ffn=m_conv_ffn,
             b_conv_ffn=m_b_conv_ffn, w_down=m_w_down, ln2_g=m_ln2_g, ln2_b=m_ln2_b)
    V = dict(w_ada=v_w_ada, b_ada=v_b_ada, w_in=v_w_in, b_gate=v_b_gate, conv_a=v_conv_a, a_log=v_a_log,
             dt_bias=v_dt_bias, norm_a=v_norm_a, rel_bias=v_rel_bias, w_branch_a=v_w_branch_a,
             w_branch_b=v_w_branch_b, w_o=v_w_o, ln1_g=v_ln1_g, ln1_b=v_ln1_b, w_up=v_w_up, conv_ffn=v_conv_ffn,
             b_conv_ffn=v_b_conv_ffn, w_down=v_w_down, ln2_g=v_ln2_g, ln2_b=v_ln2_b)
    W3, M3, V3 = W, M, V
    W, M, V = ({n: a[0] for n, a in dct.items()} for dct in (W, M, V))
    me = 4 * lax.axis_index("x") + 2 * lax.axis_index("y") + lax.axis_index("c")
    big = ("w_in", "w_up", "w_down", "w_branch_a", "w_branch_b", "w_o")

    g_in = all_gather_two_level(W["w_in"].astype(BF16), "gather_w_in")
    wts = dict(w_in_slabs=g_in)
    c_all, sh_all = all_gather([c, _pack_sh({n: W[n] for n, _ in _SH})[None]], "gather_small")
    c_all = c_all.reshape(NDEV, D)
    sh_all = sh_all.reshape(NDEV, SH_LEN)

    def full_small(name, shp):
        o = 0
        for n, s in _SH:
            if n == name:
                break
            o += s[0] * s[1]
        sz = shp[0] * shp[1]
        return sh_all[:, o:o + sz].reshape(NDEV, shp[0], shp[1]).transpose(1, 0, 2).reshape(shp[0], NDEV * shp[1])

    small = dict(conv_a=full_small("conv_a", (4, 384)), rel_bias=full_small("rel_bias", (16, 40)),
                 conv_ffn=full_small("conv_ffn", (3, 704)),
                 b_gate=W["b_gate"][None], a_log=W["a_log"][None], dt_bias=W["dt_bias"][None],
                 norm_a=W["norm_a"][None], ln1_g=W["ln1_g"][None], ln1_b=W["ln1_b"][None],
                 b_conv_ffn=W["b_conv_ffn"][None], ln2_g=W["ln2_g"][None], ln2_b=W["ln2_b"][None])

    nsh = w_ada.shape[2]
    b_sh = lax.dynamic_slice(W["b_ada"][None], (0, me * nsh), (1, nsh))
    mod_sh = ada_fwd(c_all, W["w_ada"], b_sh)
    (mod_rows,) = all_to_all([mod_sh[:, None, :]], "scatter_mod")
    mod6 = mod_rows.reshape(6, D)

    after_small = (g_in[0, 0, 0].astype(F32) * 0.0 + mod6[0, 0] * 0.0).astype(BF16)
    late, late_tok = exchange_start([W[n].astype(BF16) + after_small for n in big[1:]], "gather_late_start", False)

    def late_weights(after):
        g_up, g_down, g_a, g_b, g_o = exchange_wait(late, after, "gather_late_wait")
        return dict(w_up=g_up.transpose(1, 0, 2).reshape(D, -1), w_down=g_down.reshape(DFF, D),
                    w_a=g_a.reshape(D, D), w_b=g_b.reshape(D, D), w_o=g_o.reshape(D, D))

    mod6 = mod6 + late_tok
    mod = tuple(mod6[i:i + 1] for i in range(6))

    pending = {}

    def on_grads(group, gd):
        if group == "small":
            sh_parts = {"conv_a": _col_shards(gd["conv_a"], 384), "rel_bias": _col_shards(gd["rel_bias"], 40),
                        "conv_ffn": _col_shards(gd["conv_ffn"], 704)}
            (pending["small"],) = all_to_all([_pack_sh(sh_parts)[:, None, :]], "scatter_small_grads")
            return pending["small"][0, 0, 0] * 0.0
        if group == "ffn":
            slabs = [_col_shards(gd["w_up"], w_up.shape[2]), gd["w_down"].reshape(NDEV, -1, D)]
        elif group == "mix":
            slabs = [gd[n].reshape(NDEV, -1, D) for n in ("w_a", "w_b", "w_o")]
        else:
            nin = w_in.shape[2]
            slabs = [jnp.stack([_orig_cols_from_pieces(gd, j * nin, (j + 1) * nin) for j in range(NDEV)], axis=0)]
        pending[group], tok = exchange_start([s.astype(BF16) for s in slabs], "scatter_" + group + "_start", True)
        return tok

    loss, grad_x, dmod, g = local_step(x[0], loss_target[0], mod, wts, small, late_weights, on_grads)

    rep_vals = {n: g[n] for n in REP_NAMES if n != "b_ada"}
    rep_vals["b_ada"] = jnp.concatenate(dmod, axis=1)
    rep_vals["loss"] = loss.reshape(1, 1)
    (rep_all,) = all_gather([_pack_rep(rep_vals)[None]], "gather_small_grads")
    rep_all = rep_all.reshape(NDEV, 1, REP_LEN)
    zero1 = jnp.zeros((1, 1), F32)
    rep_out = adamw(rep_all, _pack_rep({**{n: W[n][None] for n in REP_NAMES}, "loss": zero1})[None],
                    _pack_rep({**{n: M[n][None] for n in REP_NAMES}, "loss": zero1})[None],
                    _pack_rep({**{n: V[n][None] for n in REP_NAMES}, "loss": zero1})[None], "adamw_small")
    rep_out = [o[0] for o in rep_out]
    loss_total = _unpack_rep(rep_out[0], "loss")[0, 0]

    o_ada = _REP["b_ada"][0]
    dmod_all = rep_all[:, 0, o_ada:o_ada + 6 * D]
    dmod_sh = lax.dynamic_slice(dmod_all, (0, me * nsh), (NDEV, nsh))
    g_w_ada = ada_wgrad(c_all.T, dmod_sh)

    p_up, p_down = exchange_wait(pending["ffn"], grad_x, "scatter_ffn_wait")
    p_a, p_b, p_o = exchange_wait(pending["mix"], grad_x, "scatter_mix_wait")
    (p_in,) = exchange_wait(pending["in"], grad_x, "scatter_in_wait")
    parts = [p_in, p_up, p_down, p_a, p_b, p_o]
    sh_recv = pending["small"]

    res = {}
    for n, p in zip(big, parts):
        res[n] = adamw(p, W3[n], M3[n], V3[n], "adamw_" + n)
    res["w_ada"] = adamw(g_w_ada[None], W3["w_ada"], M3["w_ada"], V3["w_ada"], "adamw_w_ada")
    sh_out = adamw(sh_recv, _pack_sh({n: W[n] for n, _ in _SH})[None, None],
                   _pack_sh({n: M[n] for n, _ in _SH})[None, None],
                   _pack_sh({n: V[n] for n, _ in _SH})[None, None], "adamw_small_sharded")
    for n, _ in _SH:
        res[n] = tuple(_unpack_sh(o[0], n)[None] for o in sh_out)
    for n in REP_NAMES:
        res[n] = tuple(_unpack_rep(o, n) for o in rep_out)

    order = ("w_ada", "b_ada", "w_in", "b_gate", "conv_a", "a_log", "dt_bias", "norm_a", "rel_bias", "w_branch_a",
             "w_branch_b", "w_o", "ln1_g", "ln1_b", "w_up", "conv_ffn", "b_conv_ffn", "w_down", "ln2_g", "ln2_b")
    outs = [loss_total, grad_x[None]]
    for kind in range(4):
        outs += [res[n][kind] for n in order]
    return tuple(outs)
```

```python
import functools
import math

import numpy as np
import jax
import jax.numpy as jnp
from jax import lax
from jax.experimental import pallas as pl
from jax.experimental.pallas import tpu as pltpu

F32 = jnp.float32
BF16 = jnp.bfloat16
HI = lax.Precision.HIGHEST

D = 1024
CH = 64
AH, ADK = 8, 128
BH, BDH = 16, 64
BPREV = 8
BMAXREL = 256
RELSZ = CH + BMAXREL
DFF = 2816
ALPHA = 2.0 ** 0.25
LN_EPS, RMS_EPS, L2_EPS = 1e-5, 1e-6, 1e-6
NEG = -1e30
LR, B1, B2, AEPS, WD, STEP = 1e-3, 0.9, 0.999, 1e-8, 0.01, 10
NDEV = 8
HALO = 8
LANE = 128
TQ = 512
VMEM_LIMIT = 56 * 1024 * 1024

C_QKVA, C_Z, C_QKVB, C_GATE, C_BA, NCAT = 0, 3072, 4096, 7168, 9216, 9728


def _cparams(n_axes=1, vmem=VMEM_LIMIT):
    return pltpu.CompilerParams(dimension_semantics=("arbitrary",) * n_axes, vmem_limit_bytes=vmem)


def _dg(a, b, ca, cb):
    return lax.dot_general(a.astype(BF16), b.astype(BF16), (((ca,), (cb,)), ((), ())),
                           preferred_element_type=F32)


@jax.custom_vjp
def mm_nn(a, b):
    return _dg(a, b, 1, 0)


@jax.custom_vjp
def mm_nt(a, b):
    return _dg(a, b, 1, 1)


@jax.custom_vjp
def mm_tn(a, b):
    return _dg(a, b, 0, 0)


mm_nn.defvjp(lambda a, b: (mm_nn(a, b), (a, b)),
             lambda r, g: (mm_nt(g, r[1]).astype(r[0].dtype), mm_tn(r[0], g).astype(r[1].dtype)))
mm_nt.defvjp(lambda a, b: (mm_nt(a, b), (a, b)),
             lambda r, g: (mm_nn(g, r[1]).astype(r[0].dtype), mm_tn(g, r[0]).astype(r[1].dtype)))
mm_tn.defvjp(lambda a, b: (mm_tn(a, b), (a, b)),
             lambda r, g: (mm_nt(r[1], g).astype(r[0].dtype), mm_nn(r[0], g).astype(r[1].dtype)))


@jax.custom_vjp
def mm_w(a, w):
    return _dg(a, w, 1, 0)


mm_w.defvjp(lambda a, w: (mm_w(a, w), (a, w)),
            lambda r, g: (mm_nt(g, r[1]).astype(r[0].dtype), jnp.zeros_like(r[1])))


def _mmh(a, b):
    return lax.dot_general(a, b, (((1,), (0,)), ((), ())), precision=HI, preferred_element_type=F32)


def _bdg(a, b, ca, cb):
    return lax.dot_general(a.astype(BF16), b.astype(BF16), (((ca,), (cb,)), ((0,), (0,))),
                           preferred_element_type=F32)


@jax.custom_vjp
def bmm_nn(a, b):
    return _bdg(a, b, 2, 1)


@jax.custom_vjp
def bmm_nt(a, b):
    return _bdg(a, b, 2, 2)


@jax.custom_vjp
def bmm_tn(a, b):
    return _bdg(a, b, 1, 1)


bmm_nn.defvjp(lambda a, b: (bmm_nn(a, b), (a, b)), lambda r, g: (bmm_nt(g, r[1]), bmm_tn(r[0], g)))
bmm_nt.defvjp(lambda a, b: (bmm_nt(a, b), (a, b)), lambda r, g: (bmm_nn(g, r[1]), bmm_tn(g, r[0])))
bmm_tn.defvjp(lambda a, b: (bmm_tn(a, b), (a, b)), lambda r, g: (bmm_nt(r[1], g), bmm_nn(r[0], g)))


def _bdg3(a, b, ca, cb):
    return lax.dot_general(a, b, (((ca,), (cb,)), ((0,), (0,))), precision=HI, preferred_element_type=F32)


def _bdgp(a, b, ca, cb):
    return _bdg(a, b, ca, cb)


@jax.custom_vjp
def bmm3_nn(a, b):
    return _bdgp(a, b, 2, 1)


bmm3_nn.defvjp(lambda a, b: (bmm3_nn(a, b), (a, b)),
               lambda r, g: (_bdgp(g, r[1], 2, 2), _bdgp(r[0], g, 1, 1)))


def _sigmoid(x):
    return 0.5 * jnp.tanh(0.5 * x) + 0.5


def _silu(x):
    return x * _sigmoid(x)


def _softplus(x):
    return jnp.maximum(x, 0.0) + jnp.log(1.0 + jnp.exp(-jnp.abs(x)))


def _layernorm(r, g, b):
    mu = jnp.mean(r, axis=-1, keepdims=True)
    xc = r - mu
    var = jnp.mean(xc * xc, axis=-1, keepdims=True)
    return xc * lax.rsqrt(var + LN_EPS) * g + b


def _iota2(shape, dim):
    return lax.broadcasted_iota(jnp.int32, shape, dim)


@jax.custom_vjp
def causal_conv(ext, rows):
    k = len(rows)
    y = None
    for j in range(k):
        s = k - 1 - j
        r = pltpu.roll(ext, s, 0) if s else ext
        t = r[HALO:] * rows[j]
        y = t if y is None else y + t
    return y


def _causal_conv_fwd(ext, rows):
    return causal_conv(ext, rows), (ext, rows)


def _causal_conv_bwd(res, g):
    ext, rows = res
    n = ext.shape[0]
    k = len(rows)
    gext = jnp.concatenate([jnp.zeros((HALO, g.shape[1]), g.dtype), g], axis=0)
    dext = None
    drows = []
    for j in range(k):
        s = k - 1 - j
        up = pltpu.roll(gext, n - s, 0) if s else gext
        t = up * rows[j]
        dext = t if dext is None else dext + t
        r = pltpu.roll(ext, s, 0) if s else ext
        drows.append(jnp.sum(g * r[HALO:], axis=0, keepdims=True))
    return dext, tuple(drows)


causal_conv.defvjp(_causal_conv_fwd, _causal_conv_bwd)


def _chunk_masks(tm):
    i = _iota2((tm, tm), 0)
    j = _iota2((tm, tm), 1)
    same = (i ^ j) < CH
    lower = jnp.where(same & (j <= i), 1.0, 0.0).astype(F32)
    upper = jnp.where(same & (i <= j), 1.0, 0.0).astype(F32)
    return lower, upper


@jax.custom_vjp
def chunk_cumsum(g):
    lower, _ = _chunk_masks(g.shape[0])
    return _mmh(lower, g)


def _chunk_cumsum_bwd(_, ct):
    _, upper = _chunk_masks(ct.shape[0])
    return (_mmh(upper, ct),)


chunk_cumsum.defvjp(lambda g: (chunk_cumsum(g), None), _chunk_cumsum_bwd)


@jax.custom_vjp
def inv_unit_lower(a):
    n = a.shape[-1]
    eye = jnp.where(_iota2((1, n, n), 1) == _iota2((1, n, n), 2), 1.0, 0.0).astype(F32)
    x = eye - a
    p = _bdg3(a, a, 2, 1)
    steps = int(math.log2(n)) - 1
    for s in range(steps):
        x = x + _bdg3(x, p, 2, 1)
        if s + 1 < steps:
            p = _bdg3(p, p, 2, 1)
    return x


def _inv_fwd(a):
    t = inv_unit_lower(a)
    return t, t


def _inv_bwd(t, g):
    return (-_bdgp(_bdgp(t, g, 1, 1), t, 2, 2),)


inv_unit_lower.defvjp(_inv_fwd, _inv_bwd)


@jax.custom_vjp
def inv_known(a, t):
    return t


inv_known.defvjp(lambda a, t: (t, t), lambda t, g: (_inv_bwd(t, g)[0], jnp.zeros_like(t)))


def prep_head_fn(ext, rows, scale):
    s = _silu(causal_conv(ext, rows))
    if scale is None:
        return s
    return s * (lax.rsqrt(jnp.sum(s * s, axis=-1, keepdims=True) + L2_EPS) * scale)


def prep_gate_fn(bb, aa, a_log, dtb):
    g = -jnp.exp(a_log) * _softplus(aa + dtb)
    return chunk_cumsum(g), _sigmoid(bb)


PREP_SCALES = (ADK ** -0.5, 1.0, None)


def _head_cols(a):
    lane = _iota2((1, LANE), 1)
    return jnp.concatenate([jnp.sum(jnp.where(lane == h, a, 0.0), axis=1, keepdims=True)[None]
                            for h in range(AH)], axis=0)


def _head_rows(a):
    at = a.T[:AH]
    sub = _iota2((AH, 1), 0)
    return jnp.concatenate([jnp.sum(jnp.where(sub == h, at, 0.0), axis=0, keepdims=True)[None]
                            for h in range(AH)], axis=0)


def c1_heads(q, k, v, gcs, beta, tinv_saved=None):
    gcol = _head_cols(gcs)
    grow = _head_rows(gcs)
    bcol = _head_cols(beta)
    i = _iota2((1, CH, CH), 1)
    j = _iota2((1, CH, CH), 2)
    causal = j <= i
    strict = j < i
    diff = gcol - grow
    decay = jnp.where(causal, jnp.exp(jnp.where(causal, diff, 0.0)), 0.0)
    kb = k * bcol
    vb = v * bcol
    a_low = jnp.where(strict, bmm_nt(kb, k) * decay, 0.0)
    tinv = inv_unit_lower(a_low) if tinv_saved is None else inv_known(a_low, tinv_saved)
    egc = jnp.exp(gcol)
    u = bmm3_nn(tinv, vb)
    w = bmm3_nn(tinv, kb * egc)
    qk = jnp.where(causal, bmm_nt(q, k) * decay, 0.0)
    glast = jnp.sum(jnp.where(_iota2((1, CH, 1), 1) == CH - 1, gcol, 0.0), axis=1, keepdims=True)
    qg = q * egc
    kd = k * jnp.exp(glast - gcol)
    eg = jnp.exp(glast) * jnp.ones((1, 1, ADK), F32)
    return u, w, qk, qg, kd, eg, tinv


def c2_heads(s, u, w, qk, qg, kd, eg, z, nw):
    vn = u - bmm_nn(w, s)
    o = bmm_nn(qg, s) + bmm_nn(qk, vn)
    s2 = s * eg + bmm_tn(kd, vn)
    ms = jnp.mean(o * o, axis=-1, keepdims=True)
    og = o * lax.rsqrt(ms + RMS_EPS) * nw * _silu(z)
    return og, s2


def _attn_core_fwd(qh, k, v, bias):
    s = mm_nt(qh, k) * (BDH ** -0.5) + bias
    p = jnp.exp(s - jnp.max(s, axis=-1, keepdims=True))
    inv = 1.0 / jnp.sum(p, axis=-1, keepdims=True)
    o = mm_nn(p, v) * inv
    return o, (qh, k, v, p, inv, o)


def _attn_core_bwd(res, do):
    qh, k, v, p, inv, o = res
    p = p * inv
    dv = mm_tn(p, do)
    dp = mm_nt(do, v)
    ds = p * (dp - jnp.sum(do * o, axis=-1, keepdims=True))
    return mm_nn(ds, k) * (BDH ** -0.5), mm_tn(ds, qh) * (BDH ** -0.5), dv, ds


@jax.custom_vjp
def attn_core(qh, k, v, bias):
    return _attn_core_fwd(qh, k, v, bias)[0]


attn_core.defvjp(_attn_core_fwd, _attn_core_bwd)


def attn_sub(q, k, v, bias2, r, firstf):
    lane = _iota2((1, 2 * BDH), 1)
    col = _iota2((1, KWIN), 1) + r * SUBQ
    nokey = jnp.where(col < TQ, firstf, 0.0) * NEG
    out = None
    for hh in range(2):
        hm = jnp.where((lane >= hh * BDH) & (lane < (hh + 1) * BDH), 1.0, 0.0).astype(F32)
        o = attn_core(q * hm, k, v, assemble_bias(bias2[hh], r) + nokey) * hm
        out = o if out is None else out + o
    return out


def merge_fn(x, oa, ob, gra, grb, p_pa, p_pb, p_mix, bga, bgb, gate_t, g1, b1, scale_f, shift_f,
             wa, wb, wo):
    ga = _sigmoid(gra + bga)
    gb = _sigmoid(grb + bgb)
    pa = mm_w(oa, wa) + p_pa
    pb = mm_w(ob, wb) + p_pb
    merged = ga * pa + gb * pb
    mix = mm_w(merged, wo) + p_mix
    y1 = _layernorm(ALPHA * x + gate_t * mix, g1, b1)
    return y1, merged


def ffn_act_fn(extg, extv, rows_g, rows_v, bg, bv):
    return _silu(causal_conv(extg, rows_g) + bg) * (causal_conv(extv, rows_v) + bv)


def head_fn(a, y1, p_ffn, gate_f, g2, b2, tgt, wd):
    ffn = mm_w(a, wd) + p_ffn
    y2 = _layernorm(ALPHA * y1 + gate_f * ffn, g2, b2)
    err = y2 - tgt
    return 0.5 * jnp.sum(jnp.mean(err * err, axis=-1, keepdims=True))


def _rows(tm, width, colblk=0, order=None):
    if order is None:
        return pl.BlockSpec((tm, width), lambda i: (i, colblk))
    return pl.BlockSpec((tm, width), lambda i: (order(i), colblk))


def _const(shape):
    nd = len(shape)
    return pl.BlockSpec(shape, lambda *_: (0,) * nd)


def _pick(n, cands):
    for c in cands:
        if n % c == 0:
            return c
    raise ValueError(f"no tile for {n}")


def _tile(n, cap):
    best = None
    for c in range(LANE, min(n, cap) + 1, LANE):
        if n % c == 0:
            best = c
    if best is None:
        raise ValueError(f"no tile for {n}")
    return best


def _onehot_rows(k, j):
    return jnp.where(_iota2((k, 1), 0) == j, 1.0, 0.0).astype(F32)


def _stack_rows(drows):
    k = len(drows)
    out = None
    for j in range(k):
        tj = _onehot_rows(k, j) * drows[j]
        out = tj if out is None else out + tj
    return out


def matmul(a, w, out_dtype, name, ta=False, tb=False):
    kdim, m = a.shape if ta else a.shape[::-1]
    n = w.shape[0] if tb else w.shape[1]
    tm = _tile(m, 2048 if kdim <= 1024 else 1024)
    tn = _tile(n, 1024)
    tk = _tile(kdim, 2560)
    nk = kdim // tk
    a_spec = (pl.BlockSpec((tk, tm), lambda i, j, k: (k, i)) if ta
              else pl.BlockSpec((tm, tk), lambda i, j, k: (i, k)))
    w_spec = (pl.BlockSpec((tn, tk), lambda i, j, k: (j, k)) if tb
              else pl.BlockSpec((tk, tn), lambda i, j, k: (k, j)))

    def body(a_ref, w_ref, o_ref, *scratch):
        p = _dg(a_ref[...], w_ref[...], 0 if ta else 1, 1 if tb else 0)
        if nk == 1:
            o_ref[...] = p.astype(out_dtype)
            return
        acc = scratch[0]
        k = pl.program_id(2)

        @pl.when(k == 0)
        def _():
            acc[...] = p

        @pl.when(k > 0)
        def _():
            acc[...] += p

        @pl.when(k == nk - 1)
        def _():
            o_ref[...] = acc[...].astype(out_dtype)

    return pl.pallas_call(
        body, name=name,
        grid=(m // tm, n // tn, nk),
        in_specs=[a_spec, w_spec],
        out_specs=pl.BlockSpec((tm, tn), lambda i, j, k: (i, j)),
        out_shape=jax.ShapeDtypeStruct((m, n), out_dtype),
        scratch_shapes=[] if nk == 1 else [pltpu.VMEM((tm, tn), F32)],
        compiler_params=_cparams(3),
    )(a, w)


def dgrad_pieces(pieces, tail, w, name):
    m = pieces[0][0].shape[0]
    n, ktot = w.shape
    tk = 1024
    tm = _tile(m, 512)
    wt = tail.shape[1]
    ranges, k0 = [], 0
    for arr, off in pieces:
        assert off == k0 * tk and arr.shape[1] % tk == 0
        ranges.append((k0, k0 + arr.shape[1] // tk))
        k0 = ranges[-1][1]
    nk = k0
    npc = len(pieces)

    def body(*refs):
        a_refs, t_ref, w_ref, wt_ref, o_ref, acc = refs[:npc], refs[npc], refs[npc + 1], refs[npc + 2], refs[npc + 3], refs[npc + 4]
        k = pl.program_id(1)

        @pl.when(k == 0)
        def _():
            acc[...] = _dg(t_ref[...], wt_ref[...], 1, 1)

        for a_ref, (lo, hi) in zip(a_refs, ranges):
            @pl.when((k >= lo) & (k < hi))
            def _(a_ref=a_ref):
                acc[...] += _dg(a_ref[...], w_ref[...], 1, 1)

        @pl.when(k == nk - 1)
        def _():
            o_ref[...] = acc[...]

    def piece_spec(lo, hi):
        return pl.BlockSpec((tm, tk), lambda i, k: (i, jnp.clip(k - lo, 0, hi - lo - 1)))

    return pl.pallas_call(
        body, name=name, grid=(m // tm, nk),
        in_specs=[piece_spec(lo, hi) for lo, hi in ranges] + [
            pl.BlockSpec((tm, wt), lambda i, k: (i, 0)),
            pl.BlockSpec((n, tk), lambda i, k: (0, k)),
            pl.BlockSpec((n, wt), lambda i, k: (0, (ktot - wt) // wt))],
        out_specs=pl.BlockSpec((tm, n), lambda i, k: (i, 0)),
        out_shape=jax.ShapeDtypeStruct((m, n), F32),
        scratch_shapes=[pltpu.VMEM((tm, n), F32)],
        compiler_params=_cparams(2),
    )(*[a for a, _ in pieces], tail, w, w)


def modulate(x, scale, shift, name):
    t, d = x.shape
    tm = _pick(t, (512, 256, 128))

    def body(x_ref, sc_ref, sh_ref, o_ref):
        o_ref[...] = (x_ref[...] * (1.0 + sc_ref[...]) + sh_ref[...]).astype(BF16)

    return pl.pallas_call(
        body, name=name, grid=(t // tm,),
        in_specs=[_rows(tm, d), _const((1, d)), _const((1, d))],
        out_specs=_rows(tm, d),
        out_shape=jax.ShapeDtypeStruct((t, d), BF16),
        compiler_params=_cparams(),
    )(x, scale, shift)


def modulate_bwd(dh, xin, dres, scale, name):
    t, d = dh.shape
    tm = _pick(t, (512, 256, 128))

    def body(dh_ref, x_ref, dr_ref, sc_ref, o_ref, dsc_ref, dsh_ref):
        i = pl.program_id(0)
        dh_v = dh_ref[...]
        o_ref[...] = dr_ref[...] + dh_v * (1.0 + sc_ref[...])

        @pl.when(i == 0)
        def _():
            dsc_ref[...] = jnp.zeros_like(dsc_ref)
            dsh_ref[...] = jnp.zeros_like(dsh_ref)

        dsc_ref[...] += jnp.sum(dh_v * x_ref[...], axis=0, keepdims=True)
        dsh_ref[...] += jnp.sum(dh_v, axis=0, keepdims=True)

    return pl.pallas_call(
        body, name=name, grid=(t // tm,),
        in_specs=[_rows(tm, d), _rows(tm, d), _rows(tm, d), _const((1, d))],
        out_specs=[_rows(tm, d), _const((1, d)), _const((1, d))],
        out_shape=[jax.ShapeDtypeStruct((t, d), F32), jax.ShapeDtypeStruct((1, d), F32),
                   jax.ShapeDtypeStruct((1, d), F32)],
        compiler_params=_cparams(),
    )(dh, xin, dres, scale)


PREP_TM = 128


def _halo_specs(tm, width, colblk, order):
    per = tm // HALO
    return [pl.BlockSpec((HALO, width), lambda i: (jnp.maximum(order(i) * per - 1, 0), colblk)),
            pl.BlockSpec((tm, width), lambda i: (order(i), colblk))]


def prep_fwd(proj, conv_a, a_log, dtb):
    t = proj.shape[0]
    tm = PREP_TM
    nt = t // tm
    wq = 3 * D

    def body(prev_ref, cur_ref, bb_ref, aa_ref, cw_ref, al_ref, dt_ref, q_ref, k_ref, v_ref, g_ref, b_ref):
        i = pl.program_id(0)
        flag = jnp.where(i > 0, 1.0, 0.0)
        for part, o_ref in enumerate((q_ref, k_ref, v_ref)):
            for h in range(AH):
                sl = slice(part * D + h * ADK, part * D + (h + 1) * ADK)
                ext = jnp.concatenate([prev_ref[:, sl] * flag, cur_ref[:, sl]], axis=0)
                rows = tuple(cw_ref[j:j + 1, sl] for j in range(4))
                o_ref[h] = prep_head_fn(ext, rows, PREP_SCALES[part])
        gcs, beta = prep_gate_fn(bb_ref[...], aa_ref[...], al_ref[...], dt_ref[...])
        g_ref[...] = gcs
        b_ref[...] = beta

    ident = lambda i: i
    hm = pl.BlockSpec((AH, tm, ADK), lambda i: (0, i, 0))
    return pl.pallas_call(
        body, name="prep_fwd", grid=(nt,),
        in_specs=_halo_specs(tm, wq, 0, ident) + [
            _rows(tm, 128, C_BA // 128), _rows(tm, 128, C_BA // 128 + 1),
            _const((4, wq)), _const((1, 128)), _const((1, 128))],
        out_specs=[hm, hm, hm, _rows(tm, 128), _rows(tm, 128)],
        out_shape=[jax.ShapeDtypeStruct((AH, t, ADK), F32)] * 3 + [jax.ShapeDtypeStruct((t, 128), F32)] * 2,
        compiler_params=_cparams(),
    )(proj, proj, proj, proj, conv_a, a_log, dtb)


def prep_bwd(proj, conv_a, a_log, dtb, dq, dk, dv, dgcs, dbeta):
    t = proj.shape[0]
    tm = PREP_TM
    nt = t // tm
    wq = 3 * D
    rev = lambda i: nt - 1 - i

    def body(prev_ref, cur_ref, bb_ref, aa_ref, cw_ref, al_ref, dt_ref,
             dq_ref, dk_ref, dv_ref, dg_ref, db_ref,
             dpre_ref, dbb_ref, daa_ref, dcw_ref, dal_ref, ddt_ref, carry):
        i = pl.program_id(0)
        flag = jnp.where(i < nt - 1, 1.0, 0.0)

        @pl.when(i == 0)
        def _():
            carry[...] = jnp.zeros_like(carry)
            dcw_ref[...] = jnp.zeros_like(dcw_ref)
            dal_ref[...] = jnp.zeros_like(dal_ref)
            ddt_ref[...] = jnp.zeros_like(ddt_ref)

        for part, d_ref in enumerate((dq_ref, dk_ref, dv_ref)):
            for h in range(AH):
                sl = slice(part * D + h * ADK, part * D + (h + 1) * ADK)
                ext = jnp.concatenate([prev_ref[:, sl] * flag, cur_ref[:, sl]], axis=0)
                rows = tuple(cw_ref[j:j + 1, sl] for j in range(4))
                _, vjp = jax.vjp(lambda e, r: prep_head_fn(e, r, PREP_SCALES[part]), ext, rows)
                dext, drows = vjp(d_ref[h])
                dcur = dext[HALO:]
                dpre_ref[:, sl] = jnp.concatenate([dcur[:tm - HALO], dcur[tm - HALO:] + carry[:, sl]],
                                                  axis=0).astype(BF16)
                carry[:, sl] = dext[:HALO]
                dcw_ref[:, sl] += _stack_rows(drows)
        _, vjp = jax.vjp(prep_gate_fn, bb_ref[...], aa_ref[...], al_ref[...], dt_ref[...])
        dbb, daa, dal, ddt = vjp((dg_ref[...], db_ref[...]))
        dbb_ref[...] = dbb.astype(BF16)
        daa_ref[...] = daa.astype(BF16)
        dal_ref[...] += dal
        ddt_ref[...] += ddt

    hm = pl.BlockSpec((AH, tm, ADK), lambda i: (0, rev(i), 0))
    return pl.pallas_call(
        body, name="prep_bwd", grid=(nt,),
        in_specs=_halo_specs(tm, wq, 0, rev) + [
            _rows(tm, 128, C_BA // 128, rev), _rows(tm, 128, C_BA // 128 + 1, rev),
            _const((4, wq)), _const((1, 128)), _const((1, 128)),
            hm, hm, hm, _rows(tm, 128, 0, rev), _rows(tm, 128, 0, rev)],
        out_specs=[_rows(tm, wq, 0, rev), _rows(tm, 128, 0, rev), _rows(tm, 128, 0, rev),
                   _const((4, wq)), _const((1, 128)), _const((1, 128))],
        out_shape=[jax.ShapeDtypeStruct((t, wq), BF16), jax.ShapeDtypeStruct((t, 128), BF16),
                   jax.ShapeDtypeStruct((t, 128), BF16), jax.ShapeDtypeStruct((4, wq), F32),
                   jax.ShapeDtypeStruct((1, 128), F32), jax.ShapeDtypeStruct((1, 128), F32)],
        scratch_shapes=[pltpu.VMEM((HALO, wq), F32)],
        compiler_params=_cparams(),
    )(proj, proj, proj, proj, conv_a, a_log, dtb, dq, dk, dv, dgcs, dbeta)


def _c1_specs(order):
    hm = pl.BlockSpec((AH, CH, ADK), lambda n: (0, order(n), 0))
    col = pl.BlockSpec((CH, LANE), lambda n: (order(n), 0))
    qk = pl.BlockSpec((1, AH, CH, CH), lambda n: (order(n), 0, 0, 0))
    eg = pl.BlockSpec((1, AH, 1, ADK), lambda n: (order(n), 0, 0, 0))
    return hm, col, qk, eg


def _heads(ref):
    return jnp.stack([ref[:, h * ADK:(h + 1) * ADK] for h in range(AH)], axis=0)


def c1_fwd(q, k, v, gcs, beta):
    t = q.shape[1]
    nc = t // CH
    hm, col, qks, egs = _c1_specs(lambda n: n)

    def body(q_ref, k_ref, v_ref, g_ref, b_ref, u_ref, w_ref, qg_ref, kd_ref, qk_ref, eg_ref, ti_ref):
        u, w, qk, qg, kd, eg, tinv = c1_heads(q_ref[...], k_ref[...], v_ref[...], g_ref[...], b_ref[...])
        u_ref[...] = u
        w_ref[...] = w
        qg_ref[...] = qg
        kd_ref[...] = kd
        qk_ref[0] = qk
        eg_ref[0] = eg
        ti_ref[0] = tinv

    return pl.pallas_call(
        body, name="c1_fwd", grid=(nc,),
        in_specs=[hm, hm, hm, col, col],
        out_specs=[hm, hm, hm, hm, qks, egs, qks],
        out_shape=[jax.ShapeDtypeStruct((AH, t, ADK), F32)] * 4 + [
            jax.ShapeDtypeStruct((nc, AH, CH, CH), F32), jax.ShapeDtypeStruct((nc, AH, 1, ADK), F32),
            jax.ShapeDtypeStruct((nc, AH, CH, CH), F32)],
        compiler_params=_cparams(),
    )(q, k, v, gcs, beta)


def c1_bwd(q, k, v, gcs, beta, tinv, du, dw, dqg, dkd, dqk, deg):
    t = q.shape[1]
    nc = t // CH
    hm, col, qks, egs = _c1_specs(lambda n: n)

    def body(q_ref, k_ref, v_ref, g_ref, b_ref, ti_ref, du_ref, dw_ref, dqg_ref, dkd_ref, dqk_ref, deg_ref,
             dq_ref, dk_ref, dv_ref, dg_ref, db_ref):
        _, vjp = jax.vjp(lambda q_, k_, v_, g_, b_: c1_heads(q_, k_, v_, g_, b_, ti_ref[0]),
                         q_ref[...], k_ref[...], v_ref[...], g_ref[...], b_ref[...])
        dq, dk, dv, dg, db = vjp((du_ref[...], dw_ref[...], dqk_ref[0], dqg_ref[...], dkd_ref[...], deg_ref[0],
                                  jnp.zeros((AH, CH, CH), F32)))
        dq_ref[...] = dq
        dk_ref[...] = dk
        dv_ref[...] = dv
        dg_ref[...] = dg
        db_ref[...] = db

    return pl.pallas_call(
        body, name="c1_bwd", grid=(nc,),
        in_specs=[hm, hm, hm, col, col, qks, hm, hm, hm, hm, qks, egs],
        out_specs=[hm, hm, hm, col, col],
        out_shape=[jax.ShapeDtypeStruct((AH, t, ADK), F32)] * 3 + [jax.ShapeDtypeStruct((t, LANE), F32)] * 2,
        compiler_params=_cparams(),
    )(q, k, v, gcs, beta, tinv, du, dw, dqg, dkd, dqk, deg)


def c2_fwd(u, w, qg, kd, qk, eg, proj, norm_a):
    t = u.shape[1]
    nc = t // CH
    hm, _, qks, egs = _c1_specs(lambda n: n)
    tok = pl.BlockSpec((CH, D), lambda n: (n, 0))
    zspec = pl.BlockSpec((CH, D), lambda n: (n, C_Z // D))
    sspec = pl.BlockSpec((1, AH, ADK, ADK), lambda n: (n, 0, 0, 0))

    def body(u_ref, w_ref, qg_ref, kd_ref, qk_ref, eg_ref, z_ref, nw_ref, o_ref, sall_ref, st):
        n = pl.program_id(0)

        @pl.when(n == 0)
        def _():
            st[...] = jnp.zeros_like(st)

        s = st[...]
        sall_ref[0] = s
        og, s2 = c2_heads(s, u_ref[...], w_ref[...], qk_ref[0], qg_ref[...], kd_ref[...], eg_ref[0],
                          _heads(z_ref), nw_ref[...])
        st[...] = s2
        for h in range(AH):
            o_ref[:, h * ADK:(h + 1) * ADK] = og[h].astype(BF16)

    return pl.pallas_call(
        body, name="c2_fwd", grid=(nc,),
        in_specs=[hm, hm, hm, hm, qks, egs, zspec, _const((1, ADK))],
        out_specs=[tok, sspec],
        out_shape=[jax.ShapeDtypeStruct((t, D), BF16), jax.ShapeDtypeStruct((nc, AH, ADK, ADK), F32)],
        scratch_shapes=[pltpu.VMEM((AH, ADK, ADK), F32)],
        compiler_params=_cparams(),
    )(u, w, qg, kd, qk, eg, proj, norm_a)


def c2_bwd(u, w, qg, kd, qk, eg, proj, norm_a, sall, do):
    t = u.shape[1]
    nc = t // CH
    rev = lambda n: nc - 1 - n
    hm, _, qks, egs = _c1_specs(rev)
    tok = pl.BlockSpec((CH, D), lambda n: (rev(n), 0))
    zspec = pl.BlockSpec((CH, D), lambda n: (rev(n), C_Z // D))
    sspec = pl.BlockSpec((1, AH, ADK, ADK), lambda n: (rev(n), 0, 0, 0))

    def body(u_ref, w_ref, qg_ref, kd_ref, qk_ref, eg_ref, z_ref, nw_ref, sall_ref, do_ref,
             du_ref, dw_ref, dqg_ref, dkd_ref, dqk_ref, deg_ref, dz_ref, dnw_ref, dst):
        n = pl.program_id(0)

        @pl.when(n == 0)
        def _():
            dst[...] = jnp.zeros_like(dst)
            dnw_ref[...] = jnp.zeros_like(dnw_ref)

        _, vjp = jax.vjp(c2_heads, sall_ref[0], u_ref[...], w_ref[...], qk_ref[0], qg_ref[...], kd_ref[...],
                         eg_ref[0], _heads(z_ref), nw_ref[...])
        ds, du, dw, dqk, dqg, dkd, deg, dz, dn = vjp((_heads(do_ref), dst[...]))
        dst[...] = ds
        du_ref[...] = du
        dw_ref[...] = dw
        dqg_ref[...] = dqg
        dkd_ref[...] = dkd
        dqk_ref[0] = dqk
        deg_ref[0] = deg
        for h in range(AH):
            dz_ref[:, h * ADK:(h + 1) * ADK] = dz[h].astype(BF16)
        dnw_ref[...] += dn

    return pl.pallas_call(
        body, name="c2_bwd", grid=(nc,),
        in_specs=[hm, hm, hm, hm, qks, egs, zspec, _const((1, ADK)), sspec, tok],
        out_specs=[hm, hm, hm, hm, qks, egs, tok, _const((1, ADK))],
        out_shape=[jax.ShapeDtypeStruct((AH, t, ADK), F32)] * 4 + [
            jax.ShapeDtypeStruct((nc, AH, CH, CH), F32), jax.ShapeDtypeStruct((nc, AH, 1, ADK), F32),
            jax.ShapeDtypeStruct((t, D), BF16), jax.ShapeDtypeStruct((1, ADK), F32)],
        scratch_shapes=[pltpu.VMEM((AH, ADK, ADK), F32)],
        compiler_params=_cparams(),
    )(u, w, qg, kd, qk, eg, proj, norm_a, sall, do)


NQB = TQ // CH
NKB = 2 * TQ // CH
NDIST = BPREV + 1
KLO = -(NQB - 2)
NPAIR = NKB - 1 - KLO + 1


def bias_table(rel_bias):
    nh = rel_bias.shape[0]
    relx = jnp.concatenate([rel_bias, jnp.broadcast_to(rel_bias[:, -1:], (nh, CH * BPREV + 2 * CH - 1 - RELSZ))],
                           axis=1)
    t = jnp.stack([relx[:, CH * k:CH * k + 2 * CH - 1] for k in range(NDIST)], axis=1)
    trev = t[:, :, ::-1]
    g2 = jnp.concatenate([trev[:, :, CH - 1:], jnp.zeros((nh, NDIST, 1), F32), trev[:, :, :CH - 1]], axis=2)
    flat = jnp.tile(g2, (1, 1, CH + 1))[:, :, :CH * (2 * CH - 1)]
    blk = flat.reshape(nh, NDIST, CH, 2 * CH - 1)[..., :CH]
    neg = jnp.full((nh, NQB - 1, CH, CH), NEG, F32)
    asc = jnp.concatenate([neg, blk, neg], axis=1)
    return jnp.concatenate([asc[:, 1:], asc[:, :-1]], axis=-1)


SUBQ = 4 * CH
NSUB = TQ // SUBQ
KWIN = SUBQ + BPREV * CH


def assemble_bias(tab, r):
    b0 = r * SUBQ // (2 * CH)
    rows = [jnp.concatenate([tab[NQB + a - 2 * b - KLO] for b in range(b0, b0 + KWIN // (2 * CH))], axis=1)
            for a in range(r * SUBQ // CH, (r + 1) * SUBQ // CH)]
    return jnp.concatenate(rows, axis=0)


def bias_table_bwd_layout(dtab):
    nh = dtab.shape[0]
    dasc = (jnp.pad(dtab[..., :CH], ((0, 0), (1, 0), (0, 0), (0, 0)))
            + jnp.pad(dtab[..., CH:], ((0, 0), (0, 1), (0, 0), (0, 0))))
    dblk = dasc[:, NQB - 1:NQB - 1 + NDIST]
    dr = jnp.pad(dblk, ((0, 0), (0, 0), (0, 0), (0, CH - 1)))
    flat = jnp.pad(dr.reshape(nh, NDIST, CH * (2 * CH - 1)), ((0, 0), (0, 0), (0, 3 * CH)))
    return flat.reshape(nh, NDIST, CH + 1, 2 * CH).transpose(0, 2, 1, 3).reshape(nh, CH + 1, NDIST * 2 * CH)


def _fold_matrix_np():
    f = np.zeros((NDIST * 2 * CH, 384), np.float32)
    for k in range(NDIST):
        s = k
        for xx in range(2 * CH):
            if xx == CH:
                continue
            m = CH - 1 - xx if xx < CH else 3 * CH - 1 - xx
            f[s * 2 * CH + xx, min(CH * k + m, RELSZ - 1)] = 1.0
    return f


def relbias_reduce(dlay):
    nh, rows, cols = dlay.shape
    rpad = (-rows) % 8
    dlay = jnp.pad(dlay, ((0, 0), (0, rpad), (0, 0)))
    fold = jnp.asarray(_fold_matrix_np())

    def body(d_ref, f_ref, o_ref):
        cs = jnp.sum(d_ref[0], axis=0, keepdims=True)
        o_ref[0] = _mmh(jnp.broadcast_to(cs, (8, cols)), f_ref[...])

    out = pl.pallas_call(
        body, name="relbias_reduce", grid=(nh,),
        in_specs=[pl.BlockSpec((1, rows + rpad, cols), lambda h: (h, 0, 0)), _const((cols, 384))],
        out_specs=pl.BlockSpec((1, 8, 384), lambda h: (h, 0, 0)),
        out_shape=jax.ShapeDtypeStruct((nh, 8, 384), F32),
        compiler_params=_cparams(),
    )(dlay, fold)
    return out[:, 0, :RELSZ]


def attn_fwd(proj, bias):
    t = proj.shape[0]
    nt = t // TQ
    cb = C_QKVB // 128

    def body(q_ref, kp_ref, kc_ref, vp_ref, vc_ref, b_ref, o_ref):
        i = pl.program_id(1)
        firstf = jnp.where(i == 0, 1.0, 0.0)
        for r in range(NSUB):
            lo, hi = r * SUBQ, r * SUBQ + KWIN - TQ
            kw = jnp.concatenate([kp_ref[lo:, :], kc_ref[:hi, :]], axis=0)
            vw = jnp.concatenate([vp_ref[lo:, :], vc_ref[:hi, :]], axis=0)
            o_ref[lo:lo + SUBQ, :] = attn_sub(q_ref[lo:lo + SUBQ, :], kw, vw, b_ref[...], r, firstf).astype(BF16)

    def blk(off, prev):
        if prev:
            return pl.BlockSpec((TQ, 128), lambda p, i: (jnp.maximum(i - 1, 0), cb + off + p))
        return pl.BlockSpec((TQ, 128), lambda p, i: (i, cb + off + p))

    return pl.pallas_call(
        body, name="attn_fwd", grid=(BH // 2, nt),
        in_specs=[blk(0, False), blk(8, True), blk(8, False), blk(16, True), blk(16, False),
                  pl.BlockSpec((2, NPAIR, CH, 2 * CH), lambda p, i: (p, 0, 0, 0))],
        out_specs=pl.BlockSpec((TQ, 128), lambda p, i: (i, p)),
        out_shape=jax.ShapeDtypeStruct((t, D), BF16),
        compiler_params=_cparams(2),
    )(proj, proj, proj, proj, proj, bias)


def attn_bwd(proj, bias, do):
    t = proj.shape[0]
    nt = t // TQ
    cb = C_QKVB // 128

    def body(q_ref, kp_ref, kc_ref, vp_ref, vc_ref, b_ref, do_ref,
             dq_ref, dk_ref, dv_ref, db_ref, ck, cv, ak, av):
        i = pl.program_id(1)

        @pl.when(i == 0)
        def _():
            ck[...] = jnp.zeros_like(ck)
            cv[...] = jnp.zeros_like(cv)
            db_ref[...] = jnp.zeros_like(db_ref)

        @pl.when(i < nt)
        def _():
            firstf = jnp.where(i == 0, 1.0, 0.0)
            ak[...] = jnp.zeros_like(ak)
            av[...] = jnp.zeros_like(av)
            db = None
            for r in range(NSUB):
                lo, hi = r * SUBQ, r * SUBQ + KWIN - TQ
                kw = jnp.concatenate([kp_ref[lo:, :], kc_ref[:hi, :]], axis=0).astype(F32)
                vw = jnp.concatenate([vp_ref[lo:, :], vc_ref[:hi, :]], axis=0).astype(F32)
                _, vjp = jax.vjp(lambda q, k, v, b: attn_sub(q, k, v, b, r, firstf),
                                 q_ref[lo:lo + SUBQ, :].astype(F32), kw, vw, b_ref[...])
                dq, dkw, dvw, dbr = vjp(do_ref[lo:lo + SUBQ, :])
                dq_ref[lo:lo + SUBQ, :] = dq.astype(BF16)
                ak[lo:lo + KWIN, :] += dkw
                av[lo:lo + KWIN, :] += dvw
                db = dbr if db is None else db + dbr
            dk_ref[...] = (ck[...] + ak[:TQ, :]).astype(BF16)
            dv_ref[...] = (cv[...] + av[:TQ, :]).astype(BF16)
            ck[...] = ak[TQ:, :]
            cv[...] = av[TQ:, :]
            db_ref[...] += db

        @pl.when(i == nt)
        def _():
            dk_ref[...] = ck[...].astype(BF16)
            dv_ref[...] = cv[...].astype(BF16)

    def blk(off, prev):
        if prev:
            return pl.BlockSpec((TQ, 128), lambda p, i: (jnp.clip(i - 1, 0, nt - 1), cb + off + p))
        return pl.BlockSpec((TQ, 128), lambda p, i: (jnp.minimum(i, nt - 1), cb + off + p))

    own = pl.BlockSpec((TQ, 128), lambda p, i: (jnp.minimum(i, nt - 1), p))
    lag = pl.BlockSpec((TQ, 128), lambda p, i: (jnp.maximum(i - 1, 0), p))
    return pl.pallas_call(
        body, name="attn_bwd", grid=(BH // 2, nt + 1),
        in_specs=[blk(0, False), blk(8, True), blk(8, False), blk(16, True), blk(16, False),
                  pl.BlockSpec((2, NPAIR, CH, 2 * CH), lambda p, i: (p, 0, 0, 0)), own],
        out_specs=[own, lag, lag, pl.BlockSpec((2, NPAIR, CH, 2 * CH), lambda p, i: (p, 0, 0, 0))],
        out_shape=[jax.ShapeDtypeStruct((t, D), BF16)] * 3 + [jax.ShapeDtypeStruct((BH, NPAIR, CH, 2 * CH), F32)],
        scratch_shapes=[pltpu.VMEM((TQ, 128), F32), pltpu.VMEM((TQ, 128), F32),
                        pltpu.VMEM((2 * TQ, 128), F32), pltpu.VMEM((2 * TQ, 128), F32)],
        compiler_params=_cparams(2),
    )(proj, proj, proj, proj, proj, bias, do)


MERGE_TM = 256


def merge_fwd(x, oa, ob, proj, vecs, wa, wb, wo):
    t = x.shape[0]
    tm = MERGE_TM
    names = ("bga", "bgb", "gate_t", "g1", "b1", "scale_f", "shift_f")

    def body(x_ref, oa_ref, ob_ref, gra_ref, grb_ref, *rest):
        vrefs = rest[:7]
        wa_ref, wb_ref, wo_ref, y_ref, h_ref = rest[7:]
        vv = [r[...] for r in vrefs]
        zero = jnp.zeros((tm, D), F32)
        y1, _ = merge_fn(x_ref[...], oa_ref[...], ob_ref[...], gra_ref[...], grb_ref[...], zero, zero, zero,
                         *vv, wa_ref[...], wb_ref[...], wo_ref[...])
        y_ref[...] = y1
        h_ref[...] = (y1 * (1.0 + vv[5]) + vv[6]).astype(BF16)

    return pl.pallas_call(
        body, name="merge_fwd", grid=(t // tm,),
        in_specs=[_rows(tm, D), _rows(tm, D), _rows(tm, D), _rows(tm, D, C_GATE // D), _rows(tm, D, C_GATE // D + 1)]
        + [_const((1, D))] * 7 + [_const((D, D))] * 3,
        out_specs=[_rows(tm, D), _rows(tm, D)],
        out_shape=[jax.ShapeDtypeStruct((t, D), F32), jax.ShapeDtypeStruct((t, D), BF16)],
        compiler_params=_cparams(),
    )(x, oa, ob, proj, proj, *[vecs[n] for n in names], wa, wb, wo)


def merge_bwd(x, oa, ob, proj, vecs, wa, wb, wo, dy1):
    t = x.shape[0]
    tm = MERGE_TM
    names = ("bga", "bgb", "gate_t", "g1", "b1", "scale_f", "shift_f")

    def body(x_ref, oa_ref, ob_ref, gra_ref, grb_ref, *rest):
        vrefs = rest[:7]
        wa_ref, wb_ref, wo_ref, dy_ref = rest[7:11]
        (dx_ref, doa_ref, dob_ref, dga_ref, dgb_ref, mg_ref, dmix_ref, dpa_ref, dpb_ref,
         dbga_ref, dbgb_ref, dgt_ref, dg1_ref, db1_ref) = rest[11:]
        i = pl.program_id(0)
        vv = [r[...] for r in vrefs]
        zero = jnp.zeros((tm, D), F32)

        def f(x_, oa_, ob_, gra_, grb_, ppa, ppb, pmix, bga, bgb, gate_t, g1, b1):
            return merge_fn(x_, oa_, ob_, gra_, grb_, ppa, ppb, pmix, bga, bgb, gate_t, g1, b1, vv[5], vv[6],
                            wa_ref[...], wb_ref[...], wo_ref[...])

        _, vjp, merged = jax.vjp(f, x_ref[...], oa_ref[...].astype(F32), ob_ref[...].astype(F32),
                                 gra_ref[...], grb_ref[...], zero, zero, zero, *vv[:5], has_aux=True)
        dx, doa, dob, dga, dgb, dpa, dpb, dmix, dbga, dbgb, dgt, dg1, db1 = vjp(dy_ref[...])
        dx_ref[...] = dx
        doa_ref[...] = doa
        dob_ref[...] = dob
        dga_ref[...] = dga.astype(BF16)
        dgb_ref[...] = dgb.astype(BF16)
        mg_ref[...] = merged.astype(BF16)
        dmix_ref[...] = dmix.astype(BF16)
        dpa_ref[...] = dpa.astype(BF16)
        dpb_ref[...] = dpb.astype(BF16)
        accs = (dbga_ref, dbgb_ref, dgt_ref, dg1_ref, db1_ref)

        @pl.when(i == 0)
        def _():
            for a in accs:
                a[...] = jnp.zeros_like(a)

        for a, val in zip(accs, (dbga, dbgb, dgt, dg1, db1)):
            a[...] += val

    return pl.pallas_call(
        body, name="merge_bwd", grid=(t // tm,),
        in_specs=[_rows(tm, D), _rows(tm, D), _rows(tm, D), _rows(tm, D, C_GATE // D), _rows(tm, D, C_GATE // D + 1)]
        + [_const((1, D))] * 7 + [_const((D, D))] * 3 + [_rows(tm, D)],
        out_specs=[_rows(tm, D)] * 9 + [_const((1, D))] * 5,
        out_shape=[jax.ShapeDtypeStruct((t, D), F32)] * 3 + [jax.ShapeDtypeStruct((t, D), BF16)] * 6
        + [jax.ShapeDtypeStruct((1, D), F32)] * 5,
        compiler_params=_cparams(),
    )(x, oa, ob, proj, proj, *[vecs[n] for n in names], wa, wb, wo, dy1)


FFN_TM = 128


def ffn_act_fwd(up, conv_w, bconv):
    t, wdt = up.shape
    tm = FFN_TM

    def body(prev_ref, cur_ref, cw_ref, bc_ref, a_ref):
        i = pl.program_id(0)
        flag = jnp.where(i > 0, 1.0, 0.0)

        def ext(sl):
            return jnp.concatenate([prev_ref[:, sl] * flag, cur_ref[:, sl]], axis=0)

        def rows(sl):
            return tuple(cw_ref[j:j + 1, sl] for j in range(3))

        for cb in range(DFF // LANE):
            g = slice(cb * LANE, (cb + 1) * LANE)
            v = slice(DFF + cb * LANE, DFF + (cb + 1) * LANE)
            a_ref[:, g] = ffn_act_fn(ext(g), ext(v), rows(g), rows(v), bc_ref[:, g], bc_ref[:, v]).astype(BF16)

    return pl.pallas_call(
        body, name="ffn_act_fwd", grid=(t // tm,),
        in_specs=_halo_specs(tm, wdt, 0, lambda i: i) + [_const((3, wdt)), _const((1, wdt))],
        out_specs=_rows(tm, DFF),
        out_shape=jax.ShapeDtypeStruct((t, DFF), BF16),
        compiler_params=_cparams(),
    )(up, up, conv_w, bconv)


def ffn_act_bwd(up, conv_w, bconv, da):
    t, wdt = up.shape
    tm = FFN_TM
    nt = t // tm
    rev = lambda i: nt - 1 - i

    def body(prev_ref, cur_ref, cw_ref, bc_ref, da_ref, dup_ref, dcw_ref, dbc_ref, carry):
        i = pl.program_id(0)
        flag = jnp.where(i < nt - 1, 1.0, 0.0)

        @pl.when(i == 0)
        def _():
            carry[...] = jnp.zeros_like(carry)
            dcw_ref[...] = jnp.zeros_like(dcw_ref)
            dbc_ref[...] = jnp.zeros_like(dbc_ref)

        def ext(sl):
            return jnp.concatenate([prev_ref[:, sl] * flag, cur_ref[:, sl]], axis=0)

        def rows(sl):
            return tuple(cw_ref[j:j + 1, sl] for j in range(3))

        def emit(sl, dext, drows, dbc):
            dcur = dext[HALO:]
            dup_ref[:, sl] = jnp.concatenate([dcur[:tm - HALO], dcur[tm - HALO:] + carry[:, sl]], axis=0).astype(BF16)
            carry[:, sl] = dext[:HALO]
            dcw_ref[:, sl] += _stack_rows(drows)
            dbc_ref[:, sl] += dbc

        for cb in range(DFF // LANE):
            g = slice(cb * LANE, (cb + 1) * LANE)
            v = slice(DFF + cb * LANE, DFF + (cb + 1) * LANE)
            _, vjp = jax.vjp(ffn_act_fn, ext(g), ext(v), rows(g), rows(v), bc_ref[:, g], bc_ref[:, v])
            dxg, dxv, drg, drv, dbg, dbv = vjp(da_ref[:, g])
            emit(g, dxg, drg, dbg)
            emit(v, dxv, drv, dbv)

    return pl.pallas_call(
        body, name="ffn_act_bwd", grid=(nt,),
        in_specs=_halo_specs(tm, wdt, 0, rev) + [_const((3, wdt)), _const((1, wdt)), _rows(tm, DFF, 0, rev)],
        out_specs=[_rows(tm, wdt, 0, rev), _const((3, wdt)), _const((1, wdt))],
        out_shape=[jax.ShapeDtypeStruct((t, wdt), BF16), jax.ShapeDtypeStruct((3, wdt), F32),
                   jax.ShapeDtypeStruct((1, wdt), F32)],
        scratch_shapes=[pltpu.VMEM((HALO, wdt), F32)],
        compiler_params=_cparams(),
    )(up, up, conv_w, bconv, da)


HEAD_TM = 256


def head_fwd_bwd(a, y1, tgt, gate_f, g2, b2, wd):
    t = a.shape[0]
    tm = HEAD_TM

    def body(a_ref, y_ref, t_ref, gf_ref, g2_ref, b2_ref, wd_ref,
             da_ref, dy_ref, dffn_ref, dgf_ref, dg2_ref, db2_ref, loss_ref):
        i = pl.program_id(0)
        zero = jnp.zeros((tm, D), F32)

        def f(a_, y_, pf, gf, g2_, b2_):
            return head_fn(a_, y_, pf, gf, g2_, b2_, t_ref[...], wd_ref[...])

        loss, vjp = jax.vjp(f, a_ref[...].astype(F32), y_ref[...], zero, gf_ref[...], g2_ref[...], b2_ref[...])
        da, dy, dffn, dgf, dg2, db2 = vjp(jnp.ones((), F32))
        da_ref[...] = da
        dy_ref[...] = dy
        dffn_ref[...] = dffn.astype(BF16)
        accs = (dgf_ref, dg2_ref, db2_ref, loss_ref)

        @pl.when(i == 0)
        def _():
            for r in accs:
                r[...] = jnp.zeros_like(r)

        dgf_ref[...] += dgf
        dg2_ref[...] += dg2
        db2_ref[...] += db2
        loss_ref[...] += loss * jnp.ones((1, 128), F32)

    return pl.pallas_call(
        body, name="head_fwd_bwd", grid=(t // tm,),
        in_specs=[_rows(tm, DFF), _rows(tm, D), _rows(tm, D), _const((1, D)), _const((1, D)), _const((1, D)),
                  _const((DFF, D))],
        out_specs=[_rows(tm, DFF), _rows(tm, D), _rows(tm, D), _const((1, D)), _const((1, D)), _const((1, D)),
                   _const((1, 128))],
        out_shape=[jax.ShapeDtypeStruct((t, DFF), F32), jax.ShapeDtypeStruct((t, D), F32),
                   jax.ShapeDtypeStruct((t, D), BF16)] + [jax.ShapeDtypeStruct((1, D), F32)] * 3
        + [jax.ShapeDtypeStruct((1, 128), F32)],
        compiler_params=_cparams(),
    )(a, y1, tgt, gate_f, g2, b2, wd)


def ada_fwd(c_all, w_sh, b_sh):
    def body(c_ref, w_ref, b_ref, o_ref):
        o_ref[...] = _mmh(_silu(c_ref[...]), w_ref[...]) + b_ref[...]

    n = w_sh.shape[1]
    return pl.pallas_call(
        body, name="ada_fwd", out_shape=jax.ShapeDtypeStruct((NDEV, n), F32),
        in_specs=[pl.BlockSpec(memory_space=pltpu.VMEM)] * 3,
        out_specs=pl.BlockSpec(memory_space=pltpu.VMEM),
        compiler_params=pltpu.CompilerParams(vmem_limit_bytes=VMEM_LIMIT),
    )(c_all, w_sh, b_sh)


def ada_wgrad(c_all_t, dmod_sh):
    def body(c_ref, d_ref, o_ref):
        o_ref[...] = _mmh(_silu(c_ref[...]), d_ref[...])

    return pl.pallas_call(
        body, name="ada_wgrad", out_shape=jax.ShapeDtypeStruct((c_all_t.shape[0], dmod_sh.shape[1]), F32),
        in_specs=[pl.BlockSpec(memory_space=pltpu.VMEM)] * 2,
        out_specs=pl.BlockSpec(memory_space=pltpu.VMEM),
        compiler_params=pltpu.CompilerParams(vmem_limit_bytes=VMEM_LIMIT),
    )(c_all_t, dmod_sh)


def adamw(gparts, w, m, v, name):
    p, r, c = gparts.shape
    tr = r if r <= 256 else _pick(r, (256, 128, 64, 32, 16, 8))
    c1 = 1.0 - B1 ** STEP
    c2 = 1.0 - B2 ** STEP

    def body(g_ref, w_ref, m_ref, v_ref, go_ref, d_ref, mo_ref, vo_ref):
        g = g_ref[0].astype(F32)
        for s in range(1, p):
            g = g + g_ref[s].astype(F32)
        mn = B1 * m_ref[0] + (1.0 - B1) * g
        vn = B2 * v_ref[0] + (1.0 - B2) * (g * g)
        go_ref[0] = g
        d_ref[0] = -LR * ((mn / c1) / (jnp.sqrt(vn / c2) + AEPS) + WD * w_ref[0])
        mo_ref[0] = mn
        vo_ref[0] = vn

    spec = pl.BlockSpec((1, tr, c), lambda i: (0, i, 0))
    return pl.pallas_call(
        body, name=name, grid=(r // tr,),
        in_specs=[pl.BlockSpec((p, tr, c), lambda i: (0, i, 0)), spec, spec, spec],
        out_specs=[spec] * 4,
        out_shape=[jax.ShapeDtypeStruct((1, r, c), F32)] * 4,
        compiler_params=_cparams(),
    )(gparts, w, m, v)


def _me():
    x, y, c = lax.axis_index("x"), lax.axis_index("y"), lax.axis_index("c")
    return x, y, c, 4 * x + 2 * y + c


def _peer(x, y, c, d):
    px = 1 - x if (d >> 2) & 1 else x
    py = 1 - y if (d >> 1) & 1 else y
    pc = 1 - c if d & 1 else c
    return (px, py, pc), 4 * px + 2 * py + pc


def _exchange(arrs, name, scatter):
    n = len(arrs)

    def body(*refs):
        ins, outs = refs[:n], refs[n:2 * n]
        send, recv, lsem = refs[2 * n:]
        x, y, c, me = _me()
        remote, local = [], []
        for k in range(n):
            src = ins[k].at[me] if scatter else ins[k]
            cp = pltpu.make_async_copy(src, outs[k].at[me], lsem.at[k])
            cp.start()
            local.append(cp)
            for d in range(1, NDEV):
                dev, pid = _peer(x, y, c, d)
                src = ins[k].at[pid] if scatter else ins[k]
                cp = pltpu.make_async_remote_copy(src_ref=src, dst_ref=outs[k].at[me],
                                                  send_sem=send.at[k, d - 1], recv_sem=recv.at[k, d - 1],
                                                  device_id=dev, device_id_type=pl.DeviceIdType.MESH)
                cp.start()
                remote.append(cp)
        for cp in remote:
            cp.wait()
        for cp in local:
            cp.wait()

    shapes = [a.shape if scatter else (NDEV,) + a.shape for a in arrs]
    return pl.pallas_call(
        body, name=name,
        in_specs=[pl.BlockSpec(memory_space=pl.ANY)] * n,
        out_specs=[pl.BlockSpec(memory_space=pl.ANY)] * n,
        out_shape=[jax.ShapeDtypeStruct(s, a.dtype) for s, a in zip(shapes, arrs)],
        scratch_shapes=[pltpu.SemaphoreType.DMA((n, NDEV - 1)), pltpu.SemaphoreType.DMA((n, NDEV - 1)),
                        pltpu.SemaphoreType.DMA((n,))],
        compiler_params=pltpu.CompilerParams(has_side_effects=True),
    )(*arrs)


def all_gather(arrs, name):
    return _exchange(arrs, name, False)


def all_gather_two_level(shard, name):
    def body(x_ref, out_ref, send, recv, lsem):
        x, y, c, _ = _me()
        sibling = (x, y, 1 - c)
        chips = [(1 - x, y), (x, 1 - y), (1 - x, 1 - y)]

        def slot(px, py, pc):
            return out_ref.at[4 * px + 2 * py + pc]

        def copy(k, block, to, src=None):
            return pltpu.make_async_remote_copy(
                src_ref=slot(*block) if src is None else src, dst_ref=slot(*block),
                send_sem=send.at[k], recv_sem=recv.at[k], device_id=to, device_id_type=pl.DeviceIdType.MESH)

        mine = pltpu.make_async_copy(x_ref, slot(x, y, c), lsem)
        mine.start()
        first = [copy(0, (x, y, c), sibling, src=x_ref)]
        first += [copy(1 + j, (x, y, c), (*chip, c), src=x_ref) for j, chip in enumerate(chips)]
        for cp in first:
            cp.start()
        passed = [copy(4 + j, (*chip, c), sibling) for j, chip in enumerate(chips)]
        for j, chip in enumerate(chips):
            copy(1 + j, (*chip, c), (x, y, c)).wait_recv()
            passed[j].start()
        copy(0, sibling, (x, y, c)).wait_recv()
        for j, chip in enumerate(chips):
            copy(4 + j, (*chip, 1 - c), (x, y, c)).wait_recv()
        for cp in first + passed:
            cp.wait_send()
        mine.wait()

    return pl.pallas_call(
        body, name=name,
        in_specs=[pl.BlockSpec(memory_space=pl.ANY)],
        out_specs=pl.BlockSpec(memory_space=pl.ANY),
        out_shape=jax.ShapeDtypeStruct((NDEV,) + shard.shape, shard.dtype),
        scratch_shapes=[pltpu.SemaphoreType.DMA((NPEER,)), pltpu.SemaphoreType.DMA((NPEER,)),
                        pltpu.SemaphoreType.DMA],
        compiler_params=pltpu.CompilerParams(has_side_effects=True),
    )(shard)


def all_to_all(arrs, name):
    return _exchange(arrs, name, True)


_HBM = pl.BlockSpec(memory_space=pltpu.HBM)
_SEM = pl.BlockSpec(memory_space=pltpu.SEMAPHORE)
_EFFECT = pltpu.SideEffectType.DATAFLOW_SIDE_EFFECTING
NPEER = NDEV - 1


def exchange_start(arrs, name, scatter):
    n = len(arrs)
    lands = [lax.empty(a.shape if scatter else (NDEV,) + a.shape, a.dtype) for a in arrs]

    def body(*refs):
        ins, lrefs = refs[:n], refs[n:2 * n]
        send, recv, token = refs[2 * n], refs[2 * n + 1], refs[-1]
        x, y, c, me = _me()
        for k in range(n):
            for d in range(1, NDEV):
                dev, pid = _peer(x, y, c, d)
                src = ins[k].at[pid] if scatter else ins[k]
                pltpu.make_async_remote_copy(src_ref=src, dst_ref=lrefs[k].at[me],
                                             send_sem=send.at[k * NPEER + d - 1], recv_sem=recv.at[k * NPEER + d - 1],
                                             device_id=dev, device_id_type=pl.DeviceIdType.MESH).start()
        token[...] = jnp.zeros_like(token)

    thru = [pltpu.HBM(a.shape, a.dtype) for a in list(arrs) + lands]
    outs = pl.pallas_call(
        body, name=name,
        out_shape=(pltpu.SemaphoreType.DMA((n * NPEER,)), pltpu.SemaphoreType.DMA((n * NPEER,)), *thru,
                   jax.ShapeDtypeStruct((8, 128), F32)),
        in_specs=[_HBM] * (2 * n),
        out_specs=(_SEM, _SEM, *([_HBM] * (2 * n)), pl.BlockSpec(memory_space=pltpu.VMEM)),
        input_output_aliases={i: 2 + i for i in range(2 * n)},
        compiler_params=pltpu.CompilerParams(has_side_effects=_EFFECT),
    )(*[pltpu.with_memory_space_constraint(a, pltpu.HBM) for a in list(arrs) + lands])
    handle = dict(send=outs[0], recv=outs[1], src=list(outs[2:2 + n]), land=list(outs[2 + n:2 + 2 * n]),
                  scatter=scatter)
    return handle, outs[-1][0, 0]


def exchange_wait(handle, after, name):
    n = len(handle["src"])
    scatter = handle["scatter"]

    def body(*refs):
        ins, lrefs = refs[:n], refs[n:2 * n]
        send, recv = refs[2 * n], refs[2 * n + 1]
        x, y, c, _ = _me()
        for k in range(n):
            for d in range(1, NDEV):
                dev, _ = _peer(x, y, c, d)
                src = ins[k].at[0] if scatter else ins[k]
                cp = pltpu.make_async_remote_copy(src_ref=src, dst_ref=lrefs[k].at[0],
                                                  send_sem=send.at[k * NPEER + d - 1],
                                                  recv_sem=recv.at[k * NPEER + d - 1],
                                                  device_id=dev, device_id_type=pl.DeviceIdType.MESH)
                cp.wait_send()
                cp.wait_recv()

    arrs = handle["src"] + handle["land"]
    outs = pl.pallas_call(
        body, name=name,
        out_shape=tuple(pltpu.HBM(a.shape, a.dtype) for a in arrs),
        in_specs=[_HBM] * (2 * n) + [_SEM, _SEM, pl.BlockSpec(memory_space=pl.ANY)],
        out_specs=tuple([_HBM] * (2 * n)),
        input_output_aliases={i: i for i in range(2 * n)},
        compiler_params=pltpu.CompilerParams(has_side_effects=_EFFECT),
    )(*arrs, handle["send"], handle["recv"], after)
    me = 4 * lax.axis_index("x") + 2 * lax.axis_index("y") + lax.axis_index("c")
    landed = []
    for own, land in zip(outs[:n], outs[n:]):
        mine = lax.dynamic_index_in_dim(own, me, 0, keepdims=True) if scatter else own[None]
        landed.append(lax.dynamic_update_slice_in_dim(land, mine, me, 0))
    return landed


def _cat_from_slabs(slabs):
    _, k, n = slabs.shape

    def cols(lo, hi):
        parts, c = [], lo
        while c < hi:
            j = c // n
            e = min(hi, (j + 1) * n)
            parts.append(slabs[j][:, c - j * n:e - j * n])
            c = e
        return parts

    def zeros(w):
        return [jnp.zeros((k, w), slabs.dtype)]

    return jnp.concatenate(cols(0, 4096) + cols(4112, 9232) + cols(4096, 4104) + zeros(LANE - AH)
                           + cols(4104, 4112) + zeros(NCAT - C_BA - LANE - AH), axis=1)


IN_PIECES = (("pre", C_QKVA, 3072), ("z", C_Z, 1024), ("qb", C_QKVB, 1024), ("kb", C_QKVB + 1024, 1024),
             ("vb", C_QKVB + 2048, 1024), ("ga", C_GATE, 1024), ("gb", C_GATE + 1024, 1024))
_ORIG_SEGS = ((0, 3072, "pre", 0), (3072, 4096, "z", 0), (4096, 4104, "ba", 0), (4104, 4112, "ba", LANE),
              (4112, 5136, "qb", 0), (5136, 6160, "kb", 0), (6160, 7184, "vb", 0), (7184, 8208, "ga", 0),
              (8208, 9232, "gb", 0))


def _orig_cols_from_pieces(gp, lo, hi):
    parts = []
    for a, b, name, off in _ORIG_SEGS:
        s, e = max(a, lo), min(b, hi)
        if s < e:
            parts.append(gp[name][:, off + s - a:off + e - a])
    return parts[0] if len(parts) == 1 else jnp.concatenate(parts, axis=1)


def _pad128(v):
    return jnp.pad(v, ((0, 0), (0, 128 - v.shape[1])))


def local_step(x, tgt, mod, wts, small, late_weights=None, on_grads=None):
    if on_grads is None:
        on_grads = lambda group, gd: jnp.zeros((), F32)
    t = x.shape[0]
    nc = t // CH
    shift_t, scale_t, gate_t, shift_f, scale_f, gate_f = mod
    wcat = _cat_from_slabs(wts["w_in_slabs"])
    a_log = _pad128(small["a_log"])
    dtb = _pad128(small["dt_bias"])
    vecs = dict(bga=small["b_gate"][:, :D], bgb=small["b_gate"][:, D:], gate_t=gate_t, g1=small["ln1_g"],
                b1=small["ln1_b"], scale_f=scale_f, shift_f=shift_f)

    h1 = modulate(x, scale_t, shift_t, "modulate_t")
    proj = matmul(h1, wcat, F32, "in_proj")
    q, k, v, gcs, beta = prep_fwd(proj, small["conv_a"], a_log, dtb)

    u, w, qg, kd, qk, eg, tinv = c1_fwd(q, k, v, gcs, beta)
    oa, sall = c2_fwd(u, w, qg, kd, qk, eg, proj, small["norm_a"])
    bias = bias_table(small["rel_bias"])
    ob = attn_fwd(proj, bias)
    if late_weights is not None:
        wts = {**wts, **late_weights(ob)}
    y1, h2 = merge_fwd(x, oa, ob, proj, vecs, wts["w_a"], wts["w_b"], wts["w_o"])
    up = matmul(h2, wts["w_up"], F32, "up_proj")
    a = ffn_act_fwd(up, small["conv_ffn"], small["b_conv_ffn"])

    da, dy1_res, dffn, dgate_f, dg2, db2, loss = head_fwd_bwd(a, y1, tgt, gate_f, small["ln2_g"], small["ln2_b"],
                                                            wts["w_down"])
    g_w_down = matmul(a, dffn, F32, "wgrad_down", ta=True)
    dup, g_conv_ffn, g_bconv = ffn_act_bwd(up, small["conv_ffn"], small["b_conv_ffn"], da)
    dh2 = matmul(dup, wts["w_up"], F32, "dgrad_up", tb=True)
    g_w_up = matmul(h2, dup, F32, "wgrad_up", ta=True)
    tok = on_grads("ffn", dict(w_up=g_w_up, w_down=g_w_down))
    dy1, dscale_f, dshift_f = modulate_bwd(dh2, y1, dy1_res, scale_f + tok, "modulate_f_bwd")
    (dx_res, doa, dob, dga, dgb, merged, dmix, dpa, dpb,
     dbga, dbgb, dgate_t, dg1, db1) = merge_bwd(x, oa, ob, proj, vecs, wts["w_a"], wts["w_b"], wts["w_o"], dy1)
    g_w_o = matmul(merged, dmix, F32, "wgrad_o", ta=True)
    g_w_a = matmul(oa, dpa, F32, "wgrad_a", ta=True)
    g_w_b = matmul(ob, dpb, F32, "wgrad_b", ta=True)
    tok = on_grads("mix", dict(w_o=g_w_o, w_a=g_w_a, w_b=g_w_b))
    dqb, dkb, dvb, dbias = attn_bwd(proj, bias, dob)
    g_rel = relbias_reduce(bias_table_bwd_layout(dbias))
    du, dw, dqg, dkd, dqk, deg, dz, g_norm = c2_bwd(u, w, qg, kd, qk, eg, proj, small["norm_a"] + tok, sall, doa)
    dq, dk, dv, dgcs, dbeta = c1_bwd(q, k, v, gcs, beta, tinv, du, dw, dqg, dkd, dqk, deg)
    dpre, dbb, daa, g_conv_a, g_alog, g_dtb = prep_bwd(proj, small["conv_a"], a_log, dtb, dq, dk, dv, dgcs, dbeta)
    tok = on_grads("small", dict(conv_a=g_conv_a, rel_bias=g_rel, conv_ffn=g_conv_ffn))
    dba = jnp.concatenate([dbb, daa, jnp.zeros((t, NCAT - C_BA - 2 * LANE), BF16)], axis=1) + tok.astype(BF16)
    dpieces = dict(pre=dpre, z=dz, qb=dqb, kb=dkb, vb=dvb, ga=dga, gb=dgb)
    g_in = {n: matmul(h1, dpieces[n], F32, "wgrad_in_" + n, ta=True) for n, _, _ in IN_PIECES}
    g_in["ba"] = matmul(h1, dba, F32, "wgrad_in_ba", ta=True)
    tok = on_grads("in", g_in)
    dh1 = dgrad_pieces([(dpieces[n], off) for n, off, _ in IN_PIECES], dba, wcat + tok.astype(BF16),
                       "dgrad_in")
    grad_x, dscale_t, dshift_t = modulate_bwd(dh1, x, dx_res, scale_t + tok, "modulate_t_bwd")

    dmod = (dshift_t, dscale_t, dgate_t, dshift_f, dscale_f, dgate_f)
    grads = dict(w_in=_orig_cols_from_pieces(g_in, 0, 9232), w_up=g_w_up, w_down=g_w_down, w_a=g_w_a, w_b=g_w_b, w_o=g_w_o,
                 conv_a=g_conv_a, rel_bias=g_rel, conv_ffn=g_conv_ffn,
                 b_gate=jnp.concatenate([dbga, dbgb], axis=1), a_log=g_alog[:, :AH], dt_bias=g_dtb[:, :AH],
                 norm_a=g_norm, ln1_g=dg1, ln1_b=db1, b_conv_ffn=g_bconv, ln2_g=dg2, ln2_b=db2)
    return loss[0, 0], grad_x, dmod, grads


_REP = {}
_off = 0
for _n, _wd, _pw in (("b_ada", 6144, 6144), ("b_gate", 2048, 2048), ("a_log", 8, 128), ("dt_bias", 8, 128),
                     ("norm_a", 128, 128), ("ln1_g", 1024, 1024), ("ln1_b", 1024, 1024),
                     ("b_conv_ffn", 5632, 5632), ("ln2_g", 1024, 1024), ("ln2_b", 1024, 1024), ("loss", 1, 128)):
    _REP[_n] = (_off, _wd, _pw)
    _off += _pw
REP_LEN = _off
REP_NAMES = [n for n in _REP if n != "loss"]
_SH = (("conv_a", (4, 384)), ("rel_bias", (16, 40)), ("conv_ffn", (3, 704)))
SH_LEN = 4352


def _pack_rep(vals):
    parts = []
    for n, (_, wd, pw) in _REP.items():
        a = vals.get(n)
        a = jnp.zeros((1, pw), F32) if a is None else jnp.pad(a.reshape(1, wd), ((0, 0), (0, pw - wd)))
        parts.append(a)
    return jnp.concatenate(parts, axis=1)


def _unpack_rep(vec, name):
    o, wd, _ = _REP[name]
    return vec[:, o:o + wd]


def _pack_sh(vals):
    parts = [vals[n].reshape(vals[n].shape[:-2] + (-1,)) for n, _ in _SH]
    a = jnp.concatenate(parts, axis=-1)
    return jnp.pad(a, [(0, 0)] * (a.ndim - 1) + [(0, SH_LEN - a.shape[-1])])


def _unpack_sh(vec, name):
    o = 0
    for n, shp in _SH:
        sz = shp[0] * shp[1]
        if n == name:
            return vec[0, o:o + sz].reshape(shp)
        o += sz
    raise KeyError(name)


def _col_shards(a, n):
    return a.reshape(a.shape[0], NDEV, n).transpose(1, 0, 2)


def kernel(x, c, w_ada, b_ada, w_in, b_gate, conv_a, a_log, dt_bias, norm_a, rel_bias, w_branch_a, w_branch_b, w_o, ln1_g, ln1_b, w_up, conv_ffn, b_conv_ffn, w_down, ln2_g, ln2_b, loss_target, m_w_ada, m_b_ada, m_w_in, m_b_gate, m_conv_a, m_a_log, m_dt_bias, m_norm_a, m_rel_bias, m_w_branch_a, m_w_branch_b, m_w_o, m_ln1_g, m_ln1_b, m_w_up, m_conv_ffn, m_b_conv_ffn, m_w_down, m_ln2_g, m_ln2_b, v_w_ada, v_b_ada, v_w_in, v_b_gate, v_conv_a, v_a_log, v_dt_bias, v_norm_a, v_rel_bias, v_w_branch_a, v_w_branch_b, v_w_o, v_ln1_g, v_ln1_b, v_w_up, v_conv_ffn, v_b_conv_ffn, v_w_down, v_ln2_g, v_ln2_b):
    W = dict(w_ada=w_ada, b_ada=b_ada, w_in=w_in, b_gate=b_gate, conv_a=conv_a, a_log=a_log, dt_bias=dt_bias,
             norm_a=norm_a, rel_bias=rel_bias, w_branch_a=w_branch_a, w_branch_b=w_branch_b, w_o=w_o, ln1_g=ln1_g,
             ln1_b=ln1_b, w_up=w_up, conv_ffn=conv_ffn, b_conv_ffn=b_conv_ffn, w_down=w_down, ln2_g=ln2_g,
             ln2_b=ln2_b)
    M = dict(w_ada=m_w_ada, b_ada=m_b_ada, w_in=m_w_in, b_gate=m_b_gate, conv_a=m_conv_a, a_log=m_a_log,
             dt_bias=m_dt_bias, norm_a=m_norm_a, rel_bias=m_rel_bias, w_branch_a=m_w_branch_a,
             w_branch_b=m_w_branch_b, w_o=m_w_o, ln1_g=m_ln1_g, ln1_b=m_ln1_b, w_up=m_w_up, conv_ffn=m_conv_ffn,
             b_conv_ffn=m_b_conv_ffn, w_down=m_w_down, ln2_g=m_ln2_g, ln2_b=m_ln2_b)
    V = dict(w_ada=v_w_ada, b_ada=v_b_ada, w_in=v_w_in, b_gate=v_b_gate, conv_a=v_conv_a, a_log=v_a_log,
             dt_bias=v_dt_bias, norm_a=v_norm_a, rel_bias=v_rel_bias, w_branch_a=v_w_branch_a,
             w_branch_b=v_w_branch_b, w_o=v_w_o, ln1_g=v_ln1_g, ln1_b=v_ln1_b, w_up=v_w_up, conv_ffn=v_conv_ffn,
             b_conv_ffn=v_b_conv_ffn, w_down=v_w_down, ln2_g=v_ln2_g, ln2_b=v_ln2_b)
    W3, M3, V3 = W, M, V
    W, M, V = ({n: a[0] for n, a in dct.items()} for dct in (W, M, V))
    me = 4 * lax.axis_index("x") + 2 * lax.axis_index("y") + lax.axis_index("c")
    big = ("w_in", "w_up", "w_down", "w_branch_a", "w_branch_b", "w_o")

    g_in = all_gather_two_level(W["w_in"].astype(BF16), "gather_w_in")
    wts = dict(w_in_slabs=g_in)
    c_all, sh_all = all_gather([c, _pack_sh({n: W[n] for n, _ in _SH})[None]], "gather_small")
    c_all = c_all.reshape(NDEV, D)
    sh_all = sh_all.reshape(NDEV, SH_LEN)

    def full_small(name, shp):
        o = 0
        for n, s in _SH:
            if n == name:
                break
            o += s[0] * s[1]
        sz = shp[0] * shp[1]
        return sh_all[:, o:o + sz].reshape(NDEV, shp[0], shp[1]).transpose(1, 0, 2).reshape(shp[0], NDEV * shp[1])

    small = dict(conv_a=full_small("conv_a", (4, 384)), rel_bias=full_small("rel_bias", (16, 40)),
                 conv_ffn=full_small("conv_ffn", (3, 704)),
                 b_gate=W["b_gate"][None], a_log=W["a_log"][None], dt_bias=W["dt_bias"][None],
                 norm_a=W["norm_a"][None], ln1_g=W["ln1_g"][None], ln1_b=W["ln1_b"][None],
                 b_conv_ffn=W["b_conv_ffn"][None], ln2_g=W["ln2_g"][None], ln2_b=W["ln2_b"][None])

    nsh = w_ada.shape[2]
    b_sh = lax.dynamic_slice(W["b_ada"][None], (0, me * nsh), (1, nsh))
    mod_sh = ada_fwd(c_all, W["w_ada"], b_sh)
    (mod_rows,) = all_to_all([mod_sh[:, None, :]], "scatter_mod")
    mod6 = mod_rows.reshape(6, D)

    after_small = (g_in[0, 0, 0].astype(F32) * 0.0 + mod6[0, 0] * 0.0).astype(BF16)
    late, late_tok = exchange_start([W[n].astype(BF16) + after_small for n in big[1:]], "gather_late_start", False)

    def late_weights(after):
        g_up, g_down, g_a, g_b, g_o = exchange_wait(late, after, "gather_late_wait")
        return dict(w_up=g_up.transpose(1, 0, 2).reshape(D, -1), w_down=g_down.reshape(DFF, D),
                    w_a=g_a.reshape(D, D), w_b=g_b.reshape(D, D), w_o=g_o.reshape(D, D))

    mod6 = mod6 + late_tok
    mod = tuple(mod6[i:i + 1] for i in range(6))

    pending = {}

    def on_grads(group, gd):
        if group == "small":
            sh_parts = {"conv_a": _col_shards(gd["conv_a"], 384), "rel_bias": _col_shards(gd["rel_bias"], 40),
                        "conv_ffn": _col_shards(gd["conv_ffn"], 704)}
            (pending["small"],) = all_to_all([_pack_sh(sh_parts)[:, None, :]], "scatter_small_grads")
            return pending["small"][0, 0, 0] * 0.0
        if group == "ffn":
            slabs = [_col_shards(gd["w_up"], w_up.shape[2]), gd["w_down"].reshape(NDEV, -1, D)]
        elif group == "mix":
            slabs = [gd[n].reshape(NDEV, -1, D) for n in ("w_a", "w_b", "w_o")]
        else:
            nin = w_in.shape[2]
            slabs = [jnp.stack([_orig_cols_from_pieces(gd, j * nin, (j + 1) * nin) for j in range(NDEV)], axis=0)]
        pending[group], tok = exchange_start([s.astype(BF16) for s in slabs], "scatter_" + group + "_start", True)
        return tok

    loss, grad_x, dmod, g = local_step(x[0], loss_target[0], mod, wts, small, late_weights, on_grads)

    rep_vals = {n: g[n] for n in REP_NAMES if n != "b_ada"}
    rep_vals["b_ada"] = jnp.concatenate(dmod, axis=1)
    rep_vals["loss"] = loss.reshape(1, 1)
    (rep_all,) = all_gather([_pack_rep(rep_vals)[None]], "gather_small_grads")
    rep_all = rep_all.reshape(NDEV, 1, REP_LEN)
    zero1 = jnp.zeros((1, 1), F32)
    rep_out = adamw(rep_all, _pack_rep({**{n: W[n][None] for n in REP_NAMES}, "loss": zero1})[None],
                    _pack_rep({**{n: M[n][None] for n in REP_NAMES}, "loss": zero1})[None],
                    _pack_rep({**{n: V[n][None] for n in REP_NAMES}, "loss": zero1})[None], "adamw_small")
    rep_out = [o[0] for o in rep_out]
    loss_total = _unpack_rep(rep_out[0], "loss")[0, 0]

    o_ada = _REP["b_ada"][0]
    dmod_all = rep_all[:, 0, o_ada:o_ada + 6 * D]
    dmod_sh = lax.dynamic_slice(dmod_all, (0, me * nsh), (NDEV, nsh))
    g_w_ada = ada_wgrad(c_all.T, dmod_sh)

    p_up, p_down = exchange_wait(pending["ffn"], grad_x, "scatter_ffn_wait")
    p_a, p_b, p_o = exchange_wait(pending["mix"], grad_x, "scatter_mix_wait")
    (p_in,) = exchange_wait(pending["in"], grad_x, "scatter_in_wait")
    parts = [p_in, p_up, p_down, p_a, p_b, p_o]
    sh_recv = pending["small"]

    res = {}
    for n, p in zip(big, parts):
        res[n] = adamw(p, W3[n], M3[n], V3[n], "adamw_" + n)
    res["w_ada"] = adamw(g_w_ada[None], W3["w_ada"], M3["w_ada"], V3["w_ada"], "adamw_w_ada")
    sh_out = adamw(sh_recv, _pack_sh({n: W[n] for n, _ in _SH})[None, None],
                   _pack_sh({n: M[n] for n, _ in _SH})[None, None],
                   _pack_sh({n: V[n] for n, _ in _SH})[None, None], "adamw_small_sharded")
    for n, _ in _SH:
        res[n] = tuple(_unpack_sh(o[0], n)[None] for o in sh_out)
    for n in REP_NAMES:
        res[n] = tuple(_unpack_rep(o, n) for o in rep_out)

    order = ("w_ada", "b_ada", "w_in", "b_gate", "conv_a", "a_log", "dt_bias", "norm_a", "rel_bias", "w_branch_a",
             "w_branch_b", "w_o", "ln1_g", "ln1_b", "w_up", "conv_ffn", "b_conv_ffn", "w_down", "ln2_g", "ln2_b")
    outs = [loss_total, grad_x[None]]
    for kind in range(4):
        outs += [res[n][kind] for n in order]
    return tuple(outs)
```

```python
import functools
import math

import numpy as np
import jax
import jax.numpy as jnp
from jax import lax
from jax.experimental import pallas as pl
from jax.experimental.pallas import tpu as pltpu

F32 = jnp.float32
BF16 = jnp.bfloat16
HI = lax.Precision.HIGHEST

D = 1024
CH = 64
AH, ADK = 8, 128
BH, BDH = 16, 64
BPREV = 8
BMAXREL = 256
RELSZ = CH + BMAXREL
DFF = 2816
ALPHA = 2.0 ** 0.25
LN_EPS, RMS_EPS, L2_EPS = 1e-5, 1e-6, 1e-6
NEG = -1e30
LR, B1, B2, AEPS, WD, STEP = 1e-3, 0.9, 0.999, 1e-8, 0.01, 10
NDEV = 8
HALO = 8
LANE = 128
TQ = 512
VMEM_LIMIT = 56 * 1024 * 1024

C_QKVA, C_Z, C_QKVB, C_GATE, C_BA, NCAT = 0, 3072, 4096, 7168, 9216, 9728


def _cparams(n_axes=1, vmem=VMEM_LIMIT):
    return pltpu.CompilerParams(dimension_semantics=("arbitrary",) * n_axes, vmem_limit_bytes=vmem)


def _dg(a, b, ca, cb):
    return lax.dot_general(a.astype(BF16), b.astype(BF16), (((ca,), (cb,)), ((), ())),
                           preferred_element_type=F32)


@jax.custom_vjp
def mm_nn(a, b):
    return _dg(a, b, 1, 0)


@jax.custom_vjp
def mm_nt(a, b):
    return _dg(a, b, 1, 1)


@jax.custom_vjp
def mm_tn(a, b):
    return _dg(a, b, 0, 0)


mm_nn.defvjp(lambda a, b: (mm_nn(a, b), (a, b)),
             lambda r, g: (mm_nt(g, r[1]).astype(r[0].dtype), mm_tn(r[0], g).astype(r[1].dtype)))
mm_nt.defvjp(lambda a, b: (mm_nt(a, b), (a, b)),
             lambda r, g: (mm_nn(g, r[1]).astype(r[0].dtype), mm_tn(g, r[0]).astype(r[1].dtype)))
mm_tn.defvjp(lambda a, b: (mm_tn(a, b), (a, b)),
             lambda r, g: (mm_nt(r[1], g).astype(r[0].dtype), mm_nn(r[0], g).astype(r[1].dtype)))


@jax.custom_vjp
def mm_w(a, w):
    return _dg(a, w, 1, 0)


mm_w.defvjp(lambda a, w: (mm_w(a, w), (a, w)),
            lambda r, g: (mm_nt(g, r[1]).astype(r[0].dtype), jnp.zeros_like(r[1])))


def _mmh(a, b):
    return lax.dot_general(a, b, (((1,), (0,)), ((), ())), precision=HI, preferred_element_type=F32)


def _bdg(a, b, ca, cb):
    return lax.dot_general(a.astype(BF16), b.astype(BF16), (((ca,), (cb,)), ((0,), (0,))),
                           preferred_element_type=F32)


@jax.custom_vjp
def bmm_nn(a, b):
    return _bdg(a, b, 2, 1)


@jax.custom_vjp
def bmm_nt(a, b):
    return _bdg(a, b, 2, 2)


@jax.custom_vjp
def bmm_tn(a, b):
    return _bdg(a, b, 1, 1)


bmm_nn.defvjp(lambda a, b: (bmm_nn(a, b), (a, b)), lambda r, g: (bmm_nt(g, r[1]), bmm_tn(r[0], g)))
bmm_nt.defvjp(lambda a, b: (bmm_nt(a, b), (a, b)), lambda r, g: (bmm_nn(g, r[1]), bmm_tn(g, r[0])))
bmm_tn.defvjp(lambda a, b: (bmm_tn(a, b), (a, b)), lambda r, g: (bmm_nt(r[1], g), bmm_nn(r[0], g)))


def _bdg3(a, b, ca, cb):
    return lax.dot_general(a, b, (((ca,), (cb,)), ((0,), (0,))), precision=HI, preferred_element_type=F32)


def _bdgp(a, b, ca, cb):
    return _bdg(a, b, ca, cb)


@jax.custom_vjp
def bmm3_nn(a, b):
    return _bdgp(a, b, 2, 1)


bmm3_nn.defvjp(lambda a, b: (bmm3_nn(a, b), (a, b)),
               lambda r, g: (_bdgp(g, r[1], 2, 2), _bdgp(r[0], g, 1, 1)))


def _sigmoid(x):
    return 0.5 * jnp.tanh(0.5 * x) + 0.5


def _silu(x):
    return x * _sigmoid(x)


def _softplus(x):
    return jnp.maximum(x, 0.0) + jnp.log(1.0 + jnp.exp(-jnp.abs(x)))


def _layernorm(r, g, b):
    mu = jnp.mean(r, axis=-1, keepdims=True)
    xc = r - mu
    var = jnp.mean(xc * xc, axis=-1, keepdims=True)
    return xc * lax.rsqrt(var + LN_EPS) * g + b


def _iota2(shape, dim):
    return lax.broadcasted_iota(jnp.int32, shape, dim)


@jax.custom_vjp
def causal_conv(ext, rows):
    k = len(rows)
    y = None
    for j in range(k):
        s = k - 1 - j
        r = pltpu.roll(ext, s, 0) if s else ext
        t = r[HALO:] * rows[j]
        y = t if y is None else y + t
    return y


def _causal_conv_fwd(ext, rows):
    return causal_conv(ext, rows), (ext, rows)


def _causal_conv_bwd(res, g):
    ext, rows = res
    n = ext.shape[0]
    k = len(rows)
    gext = jnp.concatenate([jnp.zeros((HALO, g.shape[1]), g.dtype), g], axis=0)
    dext = None
    drows = []
    for j in range(k):
        s = k - 1 - j
        up = pltpu.roll(gext, n - s, 0) if s else gext
        t = up * rows[j]
        dext = t if dext is None else dext + t
        r = pltpu.roll(ext, s, 0) if s else ext
        drows.append(jnp.sum(g * r[HALO:], axis=0, keepdims=True))
    return dext, tuple(drows)


causal_conv.defvjp(_causal_conv_fwd, _causal_conv_bwd)


def _chunk_masks(tm):
    i = _iota2((tm, tm), 0)
    j = _iota2((tm, tm), 1)
    same = (i ^ j) < CH
    lower = jnp.where(same & (j <= i), 1.0, 0.0).astype(F32)
    upper = jnp.where(same & (i <= j), 1.0, 0.0).astype(F32)
    return lower, upper


@jax.custom_vjp
def chunk_cumsum(g):
    lower, _ = _chunk_masks(g.shape[0])
    return _mmh(lower, g)


def _chunk_cumsum_bwd(_, ct):
    _, upper = _chunk_masks(ct.shape[0])
    return (_mmh(upper, ct),)


chunk_cumsum.defvjp(lambda g: (chunk_cumsum(g), None), _chunk_cumsum_bwd)


@jax.custom_vjp
def inv_unit_lower(a):
    n = a.shape[-1]
    eye = jnp.where(_iota2((1, n, n), 1) == _iota2((1, n, n), 2), 1.0, 0.0).astype(F32)
    x = eye - a
    p = _bdg3(a, a, 2, 1)
    steps = int(math.log2(n)) - 1
    for s in range(steps):
        x = x + _bdg3(x, p, 2, 1)
        if s + 1 < steps:
            p = _bdg3(p, p, 2, 1)
    return x


def _inv_fwd(a):
    t = inv_unit_lower(a)
    return t, t


def _inv_bwd(t, g):
    return (-_bdgp(_bdgp(t, g, 1, 1), t, 2, 2),)


inv_unit_lower.defvjp(_inv_fwd, _inv_bwd)


@jax.custom_vjp
def inv_known(a, t):
    return t


inv_known.defvjp(lambda a, t: (t, t), lambda t, g: (_inv_bwd(t, g)[0], jnp.zeros_like(t)))


def prep_head_fn(ext, rows, scale):
    s = _silu(causal_conv(ext, rows))
    if scale is None:
        return s
    return s * (lax.rsqrt(jnp.sum(s * s, axis=-1, keepdims=True) + L2_EPS) * scale)


def prep_gate_fn(bb, aa, a_log, dtb):
    g = -jnp.exp(a_log) * _softplus(aa + dtb)
    return chunk_cumsum(g), _sigmoid(bb)


PREP_SCALES = (ADK ** -0.5, 1.0, None)


def _head_cols(a):
    lane = _iota2((1, LANE), 1)
    return jnp.concatenate([jnp.sum(jnp.where(lane == h, a, 0.0), axis=1, keepdims=True)[None]
                            for h in range(AH)], axis=0)


def _head_rows(a):
    at = a.T[:AH]
    sub = _iota2((AH, 1), 0)
    return jnp.concatenate([jnp.sum(jnp.where(sub == h, at, 0.0), axis=0, keepdims=True)[None]
                            for h in range(AH)], axis=0)


def c1_heads(q, k, v, gcs, beta, tinv_saved=None):
    gcol = _head_cols(gcs)
    grow = _head_rows(gcs)
    bcol = _head_cols(beta)
    i = _iota2((1, CH, CH), 1)
    j = _iota2((1, CH, CH), 2)
    causal = j <= i
    strict = j < i
    diff = gcol - grow
    decay = jnp.where(causal, jnp.exp(jnp.where(causal, diff, 0.0)), 0.0)
    kb = k * bcol
    vb = v * bcol
    a_low = jnp.where(strict, bmm_nt(kb, k) * decay, 0.0)
    tinv = inv_unit_lower(a_low) if tinv_saved is None else inv_known(a_low, tinv_saved)
    egc = jnp.exp(gcol)
    u = bmm3_nn(tinv, vb)
    w = bmm3_nn(tinv, kb * egc)
    qk = jnp.where(causal, bmm_nt(q, k) * decay, 0.0)
    glast = jnp.sum(jnp.where(_iota2((1, CH, 1), 1) == CH - 1, gcol, 0.0), axis=1, keepdims=True)
    qg = q * egc
    kd = k * jnp.exp(glast - gcol)
    eg = jnp.exp(glast) * jnp.ones((1, 1, ADK), F32)
    return u, w, qk, qg, kd, eg, tinv


def c2_heads(s, u, w, qk, qg, kd, eg, z, nw):
    vn = u - bmm_nn(w, s)
    o = bmm_nn(qg, s) + bmm_nn(qk, vn)
    s2 = s * eg + bmm_tn(kd, vn)
    ms = jnp.mean(o * o, axis=-1, keepdims=True)
    og = o * lax.rsqrt(ms + RMS_EPS) * nw * _silu(z)
    return og, s2


def _attn_core_fwd(qh, k, v, bias):
    qs = qh * (BDH ** -0.5)
    s = mm_nt(qs, k) + bias
    p = jnp.exp(s - jnp.max(s, axis=-1, keepdims=True))
    inv = 1.0 / jnp.sum(p, axis=-1, keepdims=True)
    o = mm_nn(p, v) * inv
    return o, (qs, k, v, p, inv, o)


def _attn_core_bwd(res, do):
    qs, k, v, p, inv, o = res
    p = p * inv
    dv = mm_tn(p, do)
    dp = mm_nt(do, v)
    ds = p * (dp - jnp.sum(do * o, axis=-1, keepdims=True))
    return mm_nn(ds, k) * (BDH ** -0.5), mm_tn(ds, qs), dv, ds


@jax.custom_vjp
def attn_core(qh, k, v, bias):
    return _attn_core_fwd(qh, k, v, bias)[0]


attn_core.defvjp(_attn_core_fwd, _attn_core_bwd)


def attn_sub(q, k, v, bias2, r, firstf):
    lane = _iota2((1, 2 * BDH), 1)
    col = _iota2((1, KWIN), 1) + r * SUBQ
    nokey = jnp.where(col < TQ, firstf, 0.0) * NEG
    out = None
    for hh in range(2):
        hm = jnp.where((lane >= hh * BDH) & (lane < (hh + 1) * BDH), 1.0, 0.0).astype(F32)
        o = attn_core(q * hm, k, v, assemble_bias(bias2[hh], r) + nokey) * hm
        out = o if out is None else out + o
    return out


def merge_fn(x, oa, ob, gra, grb, p_pa, p_pb, p_mix, bga, bgb, gate_t, g1, b1, scale_f, shift_f,
             wa, wb, wo):
    ga = _sigmoid(gra + bga)
    gb = _sigmoid(grb + bgb)
    pa = mm_w(oa, wa) + p_pa
    pb = mm_w(ob, wb) + p_pb
    merged = ga * pa + gb * pb
    mix = mm_w(merged, wo) + p_mix
    y1 = _layernorm(ALPHA * x + gate_t * mix, g1, b1)
    return y1, merged


def ffn_act_fn(extg, extv, rows_g, rows_v, bg, bv):
    return _silu(causal_conv(extg, rows_g) + bg) * (causal_conv(extv, rows_v) + bv)


def head_fn(a, y1, p_ffn, gate_f, g2, b2, tgt, wd):
    ffn = mm_w(a, wd) + p_ffn
    y2 = _layernorm(ALPHA * y1 + gate_f * ffn, g2, b2)
    err = y2 - tgt
    return 0.5 * jnp.sum(jnp.mean(err * err, axis=-1, keepdims=True))


def _rows(tm, width, colblk=0, order=None):
    if order is None:
        return pl.BlockSpec((tm, width), lambda i: (i, colblk))
    return pl.BlockSpec((tm, width), lambda i: (order(i), colblk))


def _const(shape):
    nd = len(shape)
    return pl.BlockSpec(shape, lambda *_: (0,) * nd)


def _pick(n, cands):
    for c in cands:
        if n % c == 0:
            return c
    raise ValueError(f"no tile for {n}")


def _tile(n, cap):
    best = None
    for c in range(LANE, min(n, cap) + 1, LANE):
        if n % c == 0:
            best = c
    if best is None:
        raise ValueError(f"no tile for {n}")
    return best


def _onehot_rows(k, j):
    return jnp.where(_iota2((k, 1), 0) == j, 1.0, 0.0).astype(F32)


def _stack_rows(drows):
    k = len(drows)
    out = None
    for j in range(k):
        tj = _onehot_rows(k, j) * drows[j]
        out = tj if out is None else out + tj
    return out


def matmul(a, w, out_dtype, name, ta=False, tb=False):
    kdim, m = a.shape if ta else a.shape[::-1]
    n = w.shape[0] if tb else w.shape[1]
    tm = _tile(m, 2048 if kdim <= 1024 else 1024)
    tn = _tile(n, 1024)
    tk = _tile(kdim, 2560)
    nk = kdim // tk
    a_spec = (pl.BlockSpec((tk, tm), lambda i, j, k: (k, i)) if ta
              else pl.BlockSpec((tm, tk), lambda i, j, k: (i, k)))
    w_spec = (pl.BlockSpec((tn, tk), lambda i, j, k: (j, k)) if tb
              else pl.BlockSpec((tk, tn), lambda i, j, k: (k, j)))

    def body(a_ref, w_ref, o_ref, *scratch):
        p = _dg(a_ref[...], w_ref[...], 0 if ta else 1, 1 if tb else 0)
        if nk == 1:
            o_ref[...] = p.astype(out_dtype)
            return
        acc = scratch[0]
        k = pl.program_id(2)

        @pl.when(k == 0)
        def _():
            acc[...] = p

        @pl.when(k > 0)
        def _():
            acc[...] += p

        @pl.when(k == nk - 1)
        def _():
            o_ref[...] = acc[...].astype(out_dtype)

    return pl.pallas_call(
        body, name=name,
        grid=(m // tm, n // tn, nk),
        in_specs=[a_spec, w_spec],
        out_specs=pl.BlockSpec((tm, tn), lambda i, j, k: (i, j)),
        out_shape=jax.ShapeDtypeStruct((m, n), out_dtype),
        scratch_shapes=[] if nk == 1 else [pltpu.VMEM((tm, tn), F32)],
        compiler_params=_cparams(3),
    )(a, w)


def dgrad_pieces(pieces, tail, w, name):
    m = pieces[0][0].shape[0]
    n, ktot = w.shape
    tk = 1024
    tm = _tile(m, 1024)
    wt = tail.shape[1]
    ranges, k0 = [], 0
    for arr, off in pieces:
        assert off == k0 * tk and arr.shape[1] % tk == 0
        ranges.append((k0, k0 + arr.shape[1] // tk))
        k0 = ranges[-1][1]
    nk = k0
    npc = len(pieces)

    def body(*refs):
        a_refs, t_ref, w_ref, wt_ref, o_ref, acc = refs[:npc], refs[npc], refs[npc + 1], refs[npc + 2], refs[npc + 3], refs[npc + 4]
        k = pl.program_id(1)

        @pl.when(k == 0)
        def _():
            acc[...] = _dg(t_ref[...], wt_ref[...], 1, 1)

        for a_ref, (lo, hi) in zip(a_refs, ranges):
            @pl.when((k >= lo) & (k < hi))
            def _(a_ref=a_ref):
                acc[...] += _dg(a_ref[...], w_ref[...], 1, 1)

        @pl.when(k == nk - 1)
        def _():
            o_ref[...] = acc[...]

    def piece_spec(lo, hi):
        return pl.BlockSpec((tm, tk), lambda i, k: (i, jnp.clip(k - lo, 0, hi - lo - 1)))

    return pl.pallas_call(
        body, name=name, grid=(m // tm, nk),
        in_specs=[piece_spec(lo, hi) for lo, hi in ranges] + [
            pl.BlockSpec((tm, wt), lambda i, k: (i, 0)),
            pl.BlockSpec((n, tk), lambda i, k: (0, k)),
            pl.BlockSpec((n, wt), lambda i, k: (0, (ktot - wt) // wt))],
        out_specs=pl.BlockSpec((tm, n), lambda i, k: (i, 0)),
        out_shape=jax.ShapeDtypeStruct((m, n), F32),
        scratch_shapes=[pltpu.VMEM((tm, n), F32)],
        compiler_params=_cparams(2),
    )(*[a for a, _ in pieces], tail, w, w)


def modulate(x, scale, shift, name):
    t, d = x.shape
    tm = _pick(t, (512, 256, 128))

    def body(x_ref, sc_ref, sh_ref, o_ref):
        o_ref[...] = (x_ref[...] * (1.0 + sc_ref[...]) + sh_ref[...]).astype(BF16)

    return pl.pallas_call(
        body, name=name, grid=(t // tm,),
        in_specs=[_rows(tm, d), _const((1, d)), _const((1, d))],
        out_specs=_rows(tm, d),
        out_shape=jax.ShapeDtypeStruct((t, d), BF16),
        compiler_params=_cparams(),
    )(x, scale, shift)


def modulate_bwd(dh, xin, dres, scale, name):
    t, d = dh.shape
    tm = _pick(t, (512, 256, 128))

    def body(dh_ref, x_ref, dr_ref, sc_ref, o_ref, dsc_ref, dsh_ref):
        i = pl.program_id(0)
        dh_v = dh_ref[...]
        o_ref[...] = dr_ref[...] + dh_v * (1.0 + sc_ref[...])

        @pl.when(i == 0)
        def _():
            dsc_ref[...] = jnp.zeros_like(dsc_ref)
            dsh_ref[...] = jnp.zeros_like(dsh_ref)

        dsc_ref[...] += jnp.sum(dh_v * x_ref[...], axis=0, keepdims=True)
        dsh_ref[...] += jnp.sum(dh_v, axis=0, keepdims=True)

    return pl.pallas_call(
        body, name=name, grid=(t // tm,),
        in_specs=[_rows(tm, d), _rows(tm, d), _rows(tm, d), _const((1, d))],
        out_specs=[_rows(tm, d), _const((1, d)), _const((1, d))],
        out_shape=[jax.ShapeDtypeStruct((t, d), F32), jax.ShapeDtypeStruct((1, d), F32),
                   jax.ShapeDtypeStruct((1, d), F32)],
        compiler_params=_cparams(),
    )(dh, xin, dres, scale)


PREP_TM = 128


def _halo_specs(tm, width, colblk, order):
    per = tm // HALO
    return [pl.BlockSpec((HALO, width), lambda i: (jnp.maximum(order(i) * per - 1, 0), colblk)),
            pl.BlockSpec((tm, width), lambda i: (order(i), colblk))]


def prep_fwd(proj, conv_a, a_log, dtb):
    t = proj.shape[0]
    tm = PREP_TM
    nt = t // tm
    wq = 3 * D

    def body(prev_ref, cur_ref, bb_ref, aa_ref, cw_ref, al_ref, dt_ref, q_ref, k_ref, v_ref, g_ref, b_ref):
        i = pl.program_id(0)
        flag = jnp.where(i > 0, 1.0, 0.0)
        for part, o_ref in enumerate((q_ref, k_ref, v_ref)):
            for h in range(AH):
                sl = slice(part * D + h * ADK, part * D + (h + 1) * ADK)
                ext = jnp.concatenate([prev_ref[:, sl] * flag, cur_ref[:, sl]], axis=0)
                rows = tuple(cw_ref[j:j + 1, sl] for j in range(4))
                o_ref[h] = prep_head_fn(ext, rows, PREP_SCALES[part])
        gcs, beta = prep_gate_fn(bb_ref[...], aa_ref[...], al_ref[...], dt_ref[...])
        g_ref[...] = gcs
        b_ref[...] = beta

    ident = lambda i: i
    hm = pl.BlockSpec((AH, tm, ADK), lambda i: (0, i, 0))
    return pl.pallas_call(
        body, name="prep_fwd", grid=(nt,),
        in_specs=_halo_specs(tm, wq, 0, ident) + [
            _rows(tm, 128, C_BA // 128), _rows(tm, 128, C_BA // 128 + 1),
            _const((4, wq)), _const((1, 128)), _const((1, 128))],
        out_specs=[hm, hm, hm, _rows(tm, 128), _rows(tm, 128)],
        out_shape=[jax.ShapeDtypeStruct((AH, t, ADK), F32)] * 3 + [jax.ShapeDtypeStruct((t, 128), F32)] * 2,
        compiler_params=_cparams(),
    )(proj, proj, proj, proj, conv_a, a_log, dtb)


def prep_bwd(proj, conv_a, a_log, dtb, dq, dk, dv, dgcs, dbeta):
    t = proj.shape[0]
    tm = PREP_TM
    nt = t // tm
    wq = 3 * D
    rev = lambda i: nt - 1 - i

    def body(prev_ref, cur_ref, bb_ref, aa_ref, cw_ref, al_ref, dt_ref,
             dq_ref, dk_ref, dv_ref, dg_ref, db_ref,
             dpre_ref, dbb_ref, daa_ref, dcw_ref, dal_ref, ddt_ref, carry):
        i = pl.program_id(0)
        flag = jnp.where(i < nt - 1, 1.0, 0.0)

        @pl.when(i == 0)
        def _():
            carry[...] = jnp.zeros_like(carry)
            dcw_ref[...] = jnp.zeros_like(dcw_ref)
            dal_ref[...] = jnp.zeros_like(dal_ref)
            ddt_ref[...] = jnp.zeros_like(ddt_ref)

        for part, d_ref in enumerate((dq_ref, dk_ref, dv_ref)):
            for h in range(AH):
                sl = slice(part * D + h * ADK, part * D + (h + 1) * ADK)
                ext = jnp.concatenate([prev_ref[:, sl] * flag, cur_ref[:, sl]], axis=0)
                rows = tuple(cw_ref[j:j + 1, sl] for j in range(4))
                _, vjp = jax.vjp(lambda e, r: prep_head_fn(e, r, PREP_SCALES[part]), ext, rows)
                dext, drows = vjp(d_ref[h])
                dcur = dext[HALO:]
                dpre_ref[:, sl] = jnp.concatenate([dcur[:tm - HALO], dcur[tm - HALO:] + carry[:, sl]],
                                                  axis=0).astype(BF16)
                carry[:, sl] = dext[:HALO]
                dcw_ref[:, sl] += _stack_rows(drows)
        _, vjp = jax.vjp(prep_gate_fn, bb_ref[...], aa_ref[...], al_ref[...], dt_ref[...])
        dbb, daa, dal, ddt = vjp((dg_ref[...], db_ref[...]))
        dbb_ref[...] = dbb.astype(BF16)
        daa_ref[...] = daa.astype(BF16)
        dal_ref[...] += dal
        ddt_ref[...] += ddt

    hm = pl.BlockSpec((AH, tm, ADK), lambda i: (0, rev(i), 0))
    return pl.pallas_call(
        body, name="prep_bwd", grid=(nt,),
        in_specs=_halo_specs(tm, wq, 0, rev) + [
            _rows(tm, 128, C_BA // 128, rev), _rows(tm, 128, C_BA // 128 + 1, rev),
            _const((4, wq)), _const((1, 128)), _const((1, 128)),
            hm, hm, hm, _rows(tm, 128, 0, rev), _rows(tm, 128, 0, rev)],
        out_specs=[_rows(tm, wq, 0, rev), _rows(tm, 128, 0, rev), _rows(tm, 128, 0, rev),
                   _const((4, wq)), _const((1, 128)), _const((1, 128))],
        out_shape=[jax.ShapeDtypeStruct((t, wq), BF16), jax.ShapeDtypeStruct((t, 128), BF16),
                   jax.ShapeDtypeStruct((t, 128), BF16), jax.ShapeDtypeStruct((4, wq), F32),
                   jax.ShapeDtypeStruct((1, 128), F32), jax.ShapeDtypeStruct((1, 128), F32)],
        scratch_shapes=[pltpu.VMEM((HALO, wq), F32)],
        compiler_params=_cparams(),
    )(proj, proj, proj, proj, conv_a, a_log, dtb, dq, dk, dv, dgcs, dbeta)


def _c1_specs(order):
    hm = pl.BlockSpec((AH, CH, ADK), lambda n: (0, order(n), 0))
    col = pl.BlockSpec((CH, LANE), lambda n: (order(n), 0))
    qk = pl.BlockSpec((1, AH, CH, CH), lambda n: (order(n), 0, 0, 0))
    eg = pl.BlockSpec((1, AH, 1, ADK), lambda n: (order(n), 0, 0, 0))
    return hm, col, qk, eg


def _heads(ref):
    return jnp.stack([ref[:, h * ADK:(h + 1) * ADK] for h in range(AH)], axis=0)


def c1_fwd(q, k, v, gcs, beta):
    t = q.shape[1]
    nc = t // CH
    hm, col, qks, egs = _c1_specs(lambda n: n)

    def body(q_ref, k_ref, v_ref, g_ref, b_ref, u_ref, w_ref, qg_ref, kd_ref, qk_ref, eg_ref, ti_ref):
        u, w, qk, qg, kd, eg, tinv = c1_heads(q_ref[...], k_ref[...], v_ref[...], g_ref[...], b_ref[...])
        u_ref[...] = u
        w_ref[...] = w.astype(BF16)
        qg_ref[...] = qg.astype(BF16)
        kd_ref[...] = kd.astype(BF16)
        qk_ref[0] = qk.astype(BF16)
        eg_ref[0] = eg
        ti_ref[0] = tinv

    return pl.pallas_call(
        body, name="c1_fwd", grid=(nc,),
        in_specs=[hm, hm, hm, col, col],
        out_specs=[hm, hm, hm, hm, qks, egs, qks],
        out_shape=[jax.ShapeDtypeStruct((AH, t, ADK), F32)] + [jax.ShapeDtypeStruct((AH, t, ADK), BF16)] * 3 + [
            jax.ShapeDtypeStruct((nc, AH, CH, CH), BF16), jax.ShapeDtypeStruct((nc, AH, 1, ADK), F32),
            jax.ShapeDtypeStruct((nc, AH, CH, CH), F32)],
        compiler_params=_cparams(),
    )(q, k, v, gcs, beta)


def c1_bwd(q, k, v, gcs, beta, tinv, du, dw, dqg, dkd, dqk, deg):
    t = q.shape[1]
    nc = t // CH
    hm, col, qks, egs = _c1_specs(lambda n: n)

    def body(q_ref, k_ref, v_ref, g_ref, b_ref, ti_ref, du_ref, dw_ref, dqg_ref, dkd_ref, dqk_ref, deg_ref,
             dq_ref, dk_ref, dv_ref, dg_ref, db_ref):
        _, vjp = jax.vjp(lambda q_, k_, v_, g_, b_: c1_heads(q_, k_, v_, g_, b_, ti_ref[0]),
                         q_ref[...], k_ref[...], v_ref[...], g_ref[...], b_ref[...])
        dq, dk, dv, dg, db = vjp((du_ref[...], dw_ref[...], dqk_ref[0], dqg_ref[...], dkd_ref[...], deg_ref[0],
                                  jnp.zeros((AH, CH, CH), F32)))
        dq_ref[...] = dq
        dk_ref[...] = dk
        dv_ref[...] = dv
        dg_ref[...] = dg
        db_ref[...] = db

    return pl.pallas_call(
        body, name="c1_bwd", grid=(nc,),
        in_specs=[hm, hm, hm, col, col, qks, hm, hm, hm, hm, qks, egs],
        out_specs=[hm, hm, hm, col, col],
        out_shape=[jax.ShapeDtypeStruct((AH, t, ADK), F32)] * 3 + [jax.ShapeDtypeStruct((t, LANE), F32)] * 2,
        compiler_params=_cparams(),
    )(q, k, v, gcs, beta, tinv, du, dw, dqg, dkd, dqk, deg)


def c2_fwd(u, w, qg, kd, qk, eg, proj, norm_a):
    t = u.shape[1]
    nc = t // CH
    hm, _, qks, egs = _c1_specs(lambda n: n)
    tok = pl.BlockSpec((CH, D), lambda n: (n, 0))
    zspec = pl.BlockSpec((CH, D), lambda n: (n, C_Z // D))
    sspec = pl.BlockSpec((1, AH, ADK, ADK), lambda n: (n, 0, 0, 0))

    def body(u_ref, w_ref, qg_ref, kd_ref, qk_ref, eg_ref, z_ref, nw_ref, o_ref, sall_ref, st):
        n = pl.program_id(0)

        @pl.when(n == 0)
        def _():
            st[...] = jnp.zeros_like(st)

        s = st[...]
        sall_ref[0] = s
        og, s2 = c2_heads(s, u_ref[...], w_ref[...], qk_ref[0], qg_ref[...], kd_ref[...], eg_ref[0],
                          _heads(z_ref), nw_ref[...])
        st[...] = s2
        for h in range(AH):
            o_ref[:, h * ADK:(h + 1) * ADK] = og[h].astype(BF16)

    return pl.pallas_call(
        body, name="c2_fwd", grid=(nc,),
        in_specs=[hm, hm, hm, hm, qks, egs, zspec, _const((1, ADK))],
        out_specs=[tok, sspec],
        out_shape=[jax.ShapeDtypeStruct((t, D), BF16), jax.ShapeDtypeStruct((nc, AH, ADK, ADK), F32)],
        scratch_shapes=[pltpu.VMEM((AH, ADK, ADK), F32)],
        compiler_params=_cparams(),
    )(u, w, qg, kd, qk, eg, proj, norm_a)


def c2_bwd(u, w, qg, kd, qk, eg, proj, norm_a, sall, do):
    t = u.shape[1]
    nc = t // CH
    rev = lambda n: nc - 1 - n
    hm, _, qks, egs = _c1_specs(rev)
    tok = pl.BlockSpec((CH, D), lambda n: (rev(n), 0))
    zspec = pl.BlockSpec((CH, D), lambda n: (rev(n), C_Z // D))
    sspec = pl.BlockSpec((1, AH, ADK, ADK), lambda n: (rev(n), 0, 0, 0))

    def body(u_ref, w_ref, qg_ref, kd_ref, qk_ref, eg_ref, z_ref, nw_ref, sall_ref, do_ref,
             du_ref, dw_ref, dqg_ref, dkd_ref, dqk_ref, deg_ref, dz_ref, dnw_ref, dst):
        n = pl.program_id(0)

        @pl.when(n == 0)
        def _():
            dst[...] = jnp.zeros_like(dst)
            dnw_ref[...] = jnp.zeros_like(dnw_ref)

        _, vjp = jax.vjp(c2_heads, sall_ref[0], u_ref[...], w_ref[...].astype(F32), qk_ref[0].astype(F32),
                         qg_ref[...].astype(F32), kd_ref[...].astype(F32), eg_ref[0], _heads(z_ref), nw_ref[...])
        ds, du, dw, dqk, dqg, dkd, deg, dz, dn = vjp((_heads(do_ref), dst[...]))
        dst[...] = ds
        du_ref[...] = du
        dw_ref[...] = dw
        dqg_ref[...] = dqg
        dkd_ref[...] = dkd
        dqk_ref[0] = dqk
        deg_ref[0] = deg
        for h in range(AH):
            dz_ref[:, h * ADK:(h + 1) * ADK] = dz[h].astype(BF16)
        dnw_ref[...] += dn

    return pl.pallas_call(
        body, name="c2_bwd", grid=(nc,),
        in_specs=[hm, hm, hm, hm, qks, egs, zspec, _const((1, ADK)), sspec, tok],
        out_specs=[hm, hm, hm, hm, qks, egs, tok, _const((1, ADK))],
        out_shape=[jax.ShapeDtypeStruct((AH, t, ADK), F32)] * 4 + [
            jax.ShapeDtypeStruct((nc, AH, CH, CH), F32), jax.ShapeDtypeStruct((nc, AH, 1, ADK), F32),
            jax.ShapeDtypeStruct((t, D), BF16), jax.ShapeDtypeStruct((1, ADK), F32)],
        scratch_shapes=[pltpu.VMEM((AH, ADK, ADK), F32)],
        compiler_params=_cparams(),
    )(u, w, qg, kd, qk, eg, proj, norm_a, sall, do)


NQB = TQ // CH
NKB = 2 * TQ // CH
NDIST = BPREV + 1
KLO = -(NQB - 2)
NPAIR = NKB - 1 - KLO + 1


def bias_table(rel_bias):
    nh = rel_bias.shape[0]
    relx = jnp.concatenate([rel_bias, jnp.broadcast_to(rel_bias[:, -1:], (nh, CH * BPREV + 2 * CH - 1 - RELSZ))],
                           axis=1)
    t = jnp.stack([relx[:, CH * k:CH * k + 2 * CH - 1] for k in range(NDIST)], axis=1)
    trev = t[:, :, ::-1]
    g2 = jnp.concatenate([trev[:, :, CH - 1:], jnp.zeros((nh, NDIST, 1), F32), trev[:, :, :CH - 1]], axis=2)
    flat = jnp.tile(g2, (1, 1, CH + 1))[:, :, :CH * (2 * CH - 1)]
    blk = flat.reshape(nh, NDIST, CH, 2 * CH - 1)[..., :CH]
    neg = jnp.full((nh, NQB - 1, CH, CH), NEG, F32)
    asc = jnp.concatenate([neg, blk, neg], axis=1)
    return jnp.concatenate([asc[:, 1:], asc[:, :-1]], axis=-1)


SUBQ = 4 * CH
NSUB = TQ // SUBQ
KWIN = SUBQ + BPREV * CH


def assemble_bias(tab, r):
    b0 = r * SUBQ // (2 * CH)
    rows = [jnp.concatenate([tab[NQB + a - 2 * b - KLO] for b in range(b0, b0 + KWIN // (2 * CH))], axis=1)
            for a in range(r * SUBQ // CH, (r + 1) * SUBQ // CH)]
    return jnp.concatenate(rows, axis=0)


def bias_table_bwd_layout(dtab):
    nh = dtab.shape[0]
    dasc = (jnp.pad(dtab[..., :CH], ((0, 0), (1, 0), (0, 0), (0, 0)))
            + jnp.pad(dtab[..., CH:], ((0, 0), (0, 1), (0, 0), (0, 0))))
    dblk = dasc[:, NQB - 1:NQB - 1 + NDIST]
    dr = jnp.pad(dblk, ((0, 0), (0, 0), (0, 0), (0, CH - 1)))
    flat = jnp.pad(dr.reshape(nh, NDIST, CH * (2 * CH - 1)), ((0, 0), (0, 0), (0, 3 * CH)))
    return flat.reshape(nh, NDIST, CH + 1, 2 * CH).transpose(0, 2, 1, 3).reshape(nh, CH + 1, NDIST * 2 * CH)


def _fold_matrix_np():
    f = np.zeros((NDIST * 2 * CH, 384), np.float32)
    for k in range(NDIST):
        s = k
        for xx in range(2 * CH):
            if xx == CH:
                continue
            m = CH - 1 - xx if xx < CH else 3 * CH - 1 - xx
            f[s * 2 * CH + xx, min(CH * k + m, RELSZ - 1)] = 1.0
    return f


def relbias_reduce(dlay):
    nh, rows, cols = dlay.shape
    rpad = (-rows) % 8
    dlay = jnp.pad(dlay, ((0, 0), (0, rpad), (0, 0)))
    fold = jnp.asarray(_fold_matrix_np())

    def body(d_ref, f_ref, o_ref):
        cs = jnp.sum(d_ref[0], axis=0, keepdims=True)
        o_ref[0] = _mmh(jnp.broadcast_to(cs, (8, cols)), f_ref[...])

    out = pl.pallas_call(
        body, name="relbias_reduce", grid=(nh,),
        in_specs=[pl.BlockSpec((1, rows + rpad, cols), lambda h: (h, 0, 0)), _const((cols, 384))],
        out_specs=pl.BlockSpec((1, 8, 384), lambda h: (h, 0, 0)),
        out_shape=jax.ShapeDtypeStruct((nh, 8, 384), F32),
        compiler_params=_cparams(),
    )(dlay, fold)
    return out[:, 0, :RELSZ]


def attn_fwd(proj, bias):
    t = proj.shape[0]
    nt = t // TQ
    cb = C_QKVB // 128

    def body(q_ref, kp_ref, kc_ref, vp_ref, vc_ref, b_ref, o_ref):
        i = pl.program_id(1)
        firstf = jnp.where(i == 0, 1.0, 0.0)
        for r in range(NSUB):
            lo, hi = r * SUBQ, r * SUBQ + KWIN - TQ
            kw = jnp.concatenate([kp_ref[lo:, :], kc_ref[:hi, :]], axis=0)
            vw = jnp.concatenate([vp_ref[lo:, :], vc_ref[:hi, :]], axis=0)
            o_ref[lo:lo + SUBQ, :] = attn_sub(q_ref[lo:lo + SUBQ, :], kw, vw, b_ref[...], r, firstf).astype(BF16)

    def blk(off, prev):
        if prev:
            return pl.BlockSpec((TQ, 128), lambda p, i: (jnp.maximum(i - 1, 0), cb + off + p))
        return pl.BlockSpec((TQ, 128), lambda p, i: (i, cb + off + p))

    return pl.pallas_call(
        body, name="attn_fwd", grid=(BH // 2, nt),
        in_specs=[blk(0, False), blk(8, True), blk(8, False), blk(16, True), blk(16, False),
                  pl.BlockSpec((2, NPAIR, CH, 2 * CH), lambda p, i: (p, 0, 0, 0))],
        out_specs=pl.BlockSpec((TQ, 128), lambda p, i: (i, p)),
        out_shape=jax.ShapeDtypeStruct((t, D), BF16),
        compiler_params=_cparams(2),
    )(proj, proj, proj, proj, proj, bias)


def attn_bwd(proj, bias, do):
    t = proj.shape[0]
    nt = t // TQ
    cb = C_QKVB // 128

    def body(q_ref, kp_ref, kc_ref, vp_ref, vc_ref, b_ref, do_ref,
             dq_ref, dk_ref, dv_ref, db_ref, ck, cv, ak, av):
        i = pl.program_id(1)

        @pl.when(i == 0)
        def _():
            ck[...] = jnp.zeros_like(ck)
            cv[...] = jnp.zeros_like(cv)
            db_ref[...] = jnp.zeros_like(db_ref)

        @pl.when(i < nt)
        def _():
            firstf = jnp.where(i == 0, 1.0, 0.0)
            ak[...] = jnp.zeros_like(ak)
            av[...] = jnp.zeros_like(av)
            db = None
            for r in range(NSUB):
                lo, hi = r * SUBQ, r * SUBQ + KWIN - TQ
                kw = jnp.concatenate([kp_ref[lo:, :], kc_ref[:hi, :]], axis=0).astype(F32)
                vw = jnp.concatenate([vp_ref[lo:, :], vc_ref[:hi, :]], axis=0).astype(F32)
                _, vjp = jax.vjp(lambda q, k, v, b: attn_sub(q, k, v, b, r, firstf),
                                 q_ref[lo:lo + SUBQ, :].astype(F32), kw, vw, b_ref[...])
                dq, dkw, dvw, dbr = vjp(do_ref[lo:lo + SUBQ, :])
                dq_ref[lo:lo + SUBQ, :] = dq.astype(BF16)
                ak[lo:lo + KWIN, :] += dkw
                av[lo:lo + KWIN, :] += dvw
                db = dbr if db is None else db + dbr
            dk_ref[...] = (ck[...] + ak[:TQ, :]).astype(BF16)
            dv_ref[...] = (cv[...] + av[:TQ, :]).astype(BF16)
            ck[...] = ak[TQ:, :]
            cv[...] = av[TQ:, :]
            db_ref[...] += db

        @pl.when(i == nt)
        def _():
            dk_ref[...] = ck[...].astype(BF16)
            dv_ref[...] = cv[...].astype(BF16)

    def blk(off, prev):
        if prev:
            return pl.BlockSpec((TQ, 128), lambda p, i: (jnp.clip(i - 1, 0, nt - 1), cb + off + p))
        return pl.BlockSpec((TQ, 128), lambda p, i: (jnp.minimum(i, nt - 1), cb + off + p))

    own = pl.BlockSpec((TQ, 128), lambda p, i: (jnp.minimum(i, nt - 1), p))
    lag = pl.BlockSpec((TQ, 128), lambda p, i: (jnp.maximum(i - 1, 0), p))
    return pl.pallas_call(
        body, name="attn_bwd", grid=(BH // 2, nt + 1),
        in_specs=[blk(0, False), blk(8, True), blk(8, False), blk(16, True), blk(16, False),
                  pl.BlockSpec((2, NPAIR, CH, 2 * CH), lambda p, i: (p, 0, 0, 0)), own],
        out_specs=[own, lag, lag, pl.BlockSpec((2, NPAIR, CH, 2 * CH), lambda p, i: (p, 0, 0, 0))],
        out_shape=[jax.ShapeDtypeStruct((t, D), BF16)] * 3 + [jax.ShapeDtypeStruct((BH, NPAIR, CH, 2 * CH), F32)],
        scratch_shapes=[pltpu.VMEM((TQ, 128), F32), pltpu.VMEM((TQ, 128), F32),
                        pltpu.VMEM((2 * TQ, 128), F32), pltpu.VMEM((2 * TQ, 128), F32)],
        compiler_params=_cparams(2),
    )(proj, proj, proj, proj, proj, bias, do)


MERGE_TM = 256


def merge_fwd(x, oa, ob, proj, vecs, wa, wb, wo):
    t = x.shape[0]
    tm = MERGE_TM
    names = ("bga", "bgb", "gate_t", "g1", "b1", "scale_f", "shift_f")

    def body(x_ref, oa_ref, ob_ref, gra_ref, grb_ref, *rest):
        vrefs = rest[:7]
        wa_ref, wb_ref, wo_ref, y_ref, h_ref = rest[7:]
        vv = [r[...] for r in vrefs]
        zero = jnp.zeros((tm, D), F32)
        y1, _ = merge_fn(x_ref[...], oa_ref[...], ob_ref[...], gra_ref[...], grb_ref[...], zero, zero, zero,
                         *vv, wa_ref[...], wb_ref[...], wo_ref[...])
        y_ref[...] = y1
        h_ref[...] = (y1 * (1.0 + vv[5]) + vv[6]).astype(BF16)

    return pl.pallas_call(
        body, name="merge_fwd", grid=(t // tm,),
        in_specs=[_rows(tm, D), _rows(tm, D), _rows(tm, D), _rows(tm, D, C_GATE // D), _rows(tm, D, C_GATE // D + 1)]
        + [_const((1, D))] * 7 + [_const((D, D))] * 3,
        out_specs=[_rows(tm, D), _rows(tm, D)],
        out_shape=[jax.ShapeDtypeStruct((t, D), F32), jax.ShapeDtypeStruct((t, D), BF16)],
        compiler_params=_cparams(),
    )(x, oa, ob, proj, proj, *[vecs[n] for n in names], wa, wb, wo)


def merge_bwd(x, oa, ob, proj, vecs, wa, wb, wo, dy1):
    t = x.shape[0]
    tm = MERGE_TM
    names = ("bga", "bgb", "gate_t", "g1", "b1", "scale_f", "shift_f")

    def body(x_ref, oa_ref, ob_ref, gra_ref, grb_ref, *rest):
        vrefs = rest[:7]
        wa_ref, wb_ref, wo_ref, dy_ref = rest[7:11]
        (dx_ref, doa_ref, dob_ref, dga_ref, dgb_ref, mg_ref, dmix_ref, dpa_ref, dpb_ref,
         dbga_ref, dbgb_ref, dgt_ref, dg1_ref, db1_ref) = rest[11:]
        i = pl.program_id(0)
        vv = [r[...] for r in vrefs]
        zero = jnp.zeros((tm, D), F32)

        def f(x_, oa_, ob_, gra_, grb_, ppa, ppb, pmix, bga, bgb, gate_t, g1, b1):
            return merge_fn(x_, oa_, ob_, gra_, grb_, ppa, ppb, pmix, bga, bgb, gate_t, g1, b1, vv[5], vv[6],
                            wa_ref[...], wb_ref[...], wo_ref[...])

        _, vjp, merged = jax.vjp(f, x_ref[...], oa_ref[...].astype(F32), ob_ref[...].astype(F32),
                                 gra_ref[...], grb_ref[...], zero, zero, zero, *vv[:5], has_aux=True)
        dx, doa, dob, dga, dgb, dpa, dpb, dmix, dbga, dbgb, dgt, dg1, db1 = vjp(dy_ref[...])
        dx_ref[...] = dx
        doa_ref[...] = doa
        dob_ref[...] = dob
        dga_ref[...] = dga.astype(BF16)
        dgb_ref[...] = dgb.astype(BF16)
        mg_ref[...] = merged.astype(BF16)
        dmix_ref[...] = dmix.astype(BF16)
        dpa_ref[...] = dpa.astype(BF16)
        dpb_ref[...] = dpb.astype(BF16)
        accs = (dbga_ref, dbgb_ref, dgt_ref, dg1_ref, db1_ref)

        @pl.when(i == 0)
        def _():
            for a in accs:
                a[...] = jnp.zeros_like(a)

        for a, val in zip(accs, (dbga, dbgb, dgt, dg1, db1)):
            a[...] += val

    return pl.pallas_call(
        body, name="merge_bwd", grid=(t // tm,),
        in_specs=[_rows(tm, D), _rows(tm, D), _rows(tm, D), _rows(tm, D, C_GATE // D), _rows(tm, D, C_GATE // D + 1)]
        + [_const((1, D))] * 7 + [_const((D, D))] * 3 + [_rows(tm, D)],
        out_specs=[_rows(tm, D)] * 9 + [_const((1, D))] * 5,
        out_shape=[jax.ShapeDtypeStruct((t, D), F32)] * 3 + [jax.ShapeDtypeStruct((t, D), BF16)] * 6
        + [jax.ShapeDtypeStruct((1, D), F32)] * 5,
        compiler_params=_cparams(),
    )(x, oa, ob, proj, proj, *[vecs[n] for n in names], wa, wb, wo, dy1)


FFN_TM = 128


def ffn_act_fwd(up, conv_w, bconv):
    t, wdt = up.shape
    tm = FFN_TM

    def body(prev_ref, cur_ref, cw_ref, bc_ref, a_ref):
        i = pl.program_id(0)
        flag = jnp.where(i > 0, 1.0, 0.0)

        def ext(sl):
            return jnp.concatenate([prev_ref[:, sl] * flag, cur_ref[:, sl]], axis=0)

        def rows(sl):
            return tuple(cw_ref[j:j + 1, sl] for j in range(3))

        for cb in range(DFF // LANE):
            g = slice(cb * LANE, (cb + 1) * LANE)
            v = slice(DFF + cb * LANE, DFF + (cb + 1) * LANE)
            a_ref[:, g] = ffn_act_fn(ext(g), ext(v), rows(g), rows(v), bc_ref[:, g], bc_ref[:, v]).astype(BF16)

    return pl.pallas_call(
        body, name="ffn_act_fwd", grid=(t // tm,),
        in_specs=_halo_specs(tm, wdt, 0, lambda i: i) + [_const((3, wdt)), _const((1, wdt))],
        out_specs=_rows(tm, DFF),
        out_shape=jax.ShapeDtypeStruct((t, DFF), BF16),
        compiler_params=_cparams(),
    )(up, up, conv_w, bconv)


def ffn_act_bwd(up, conv_w, bconv, da):
    t, wdt = up.shape
    tm = FFN_TM
    nt = t // tm
    rev = lambda i: nt - 1 - i

    def body(prev_ref, cur_ref, cw_ref, bc_ref, da_ref, dup_ref, dcw_ref, dbc_ref, carry):
        i = pl.program_id(0)
        flag = jnp.where(i < nt - 1, 1.0, 0.0)

        @pl.when(i == 0)
        def _():
            carry[...] = jnp.zeros_like(carry)
            dcw_ref[...] = jnp.zeros_like(dcw_ref)
            dbc_ref[...] = jnp.zeros_like(dbc_ref)

        def ext(sl):
            return jnp.concatenate([prev_ref[:, sl] * flag, cur_ref[:, sl]], axis=0)

        def rows(sl):
            return tuple(cw_ref[j:j + 1, sl] for j in range(3))

        def emit(sl, dext, drows, dbc):
            dcur = dext[HALO:]
            dup_ref[:, sl] = jnp.concatenate([dcur[:tm - HALO], dcur[tm - HALO:] + carry[:, sl]], axis=0).astype(BF16)
            carry[:, sl] = dext[:HALO]
            dcw_ref[:, sl] += _stack_rows(drows)
            dbc_ref[:, sl] += dbc

        for cb in range(DFF // LANE):
            g = slice(cb * LANE, (cb + 1) * LANE)
            v = slice(DFF + cb * LANE, DFF + (cb + 1) * LANE)
            _, vjp = jax.vjp(ffn_act_fn, ext(g), ext(v), rows(g), rows(v), bc_ref[:, g], bc_ref[:, v])
            dxg, dxv, drg, drv, dbg, dbv = vjp(da_ref[:, g])
            emit(g, dxg, drg, dbg)
            emit(v, dxv, drv, dbv)

    return pl.pallas_call(
        body, name="ffn_act_bwd", grid=(nt,),
        in_specs=_halo_specs(tm, wdt, 0, rev) + [_const((3, wdt)), _const((1, wdt)), _rows(tm, DFF, 0, rev)],
        out_specs=[_rows(tm, wdt, 0, rev), _const((3, wdt)), _const((1, wdt))],
        out_shape=[jax.ShapeDtypeStruct((t, wdt), BF16), jax.ShapeDtypeStruct((3, wdt), F32),
                   jax.ShapeDtypeStruct((1, wdt), F32)],
        scratch_shapes=[pltpu.VMEM((HALO, wdt), F32)],
        compiler_params=_cparams(),
    )(up, up, conv_w, bconv, da)


HEAD_TM = 256


def head_fwd_bwd(a, y1, tgt, gate_f, g2, b2, wd):
    t = a.shape[0]
    tm = HEAD_TM

    def body(a_ref, y_ref, t_ref, gf_ref, g2_ref, b2_ref, wd_ref,
             da_ref, dy_ref, dffn_ref, dgf_ref, dg2_ref, db2_ref, loss_ref):
        i = pl.program_id(0)
        zero = jnp.zeros((tm, D), F32)

        def f(a_, y_, pf, gf, g2_, b2_):
            return head_fn(a_, y_, pf, gf, g2_, b2_, t_ref[...], wd_ref[...])

        loss, vjp = jax.vjp(f, a_ref[...].astype(F32), y_ref[...], zero, gf_ref[...], g2_ref[...], b2_ref[...])
        da, dy, dffn, dgf, dg2, db2 = vjp(jnp.ones((), F32))
        da_ref[...] = da
        dy_ref[...] = dy
        dffn_ref[...] = dffn.astype(BF16)
        accs = (dgf_ref, dg2_ref, db2_ref, loss_ref)

        @pl.when(i == 0)
        def _():
            for r in accs:
                r[...] = jnp.zeros_like(r)

        dgf_ref[...] += dgf
        dg2_ref[...] += dg2
        db2_ref[...] += db2
        loss_ref[...] += loss * jnp.ones((1, 128), F32)

    return pl.pallas_call(
        body, name="head_fwd_bwd", grid=(t // tm,),
        in_specs=[_rows(tm, DFF), _rows(tm, D), _rows(tm, D), _const((1, D)), _const((1, D)), _const((1, D)),
                  _const((DFF, D))],
        out_specs=[_rows(tm, DFF), _rows(tm, D), _rows(tm, D), _const((1, D)), _const((1, D)), _const((1, D)),
                   _const((1, 128))],
        out_shape=[jax.ShapeDtypeStruct((t, DFF), F32), jax.ShapeDtypeStruct((t, D), F32),
                   jax.ShapeDtypeStruct((t, D), BF16)] + [jax.ShapeDtypeStruct((1, D), F32)] * 3
        + [jax.ShapeDtypeStruct((1, 128), F32)],
        compiler_params=_cparams(),
    )(a, y1, tgt, gate_f, g2, b2, wd)


def ada_fwd(c_all, w_sh, b_sh):
    def body(c_ref, w_ref, b_ref, o_ref):
        o_ref[...] = _mmh(_silu(c_ref[...]), w_ref[...]) + b_ref[...]

    n = w_sh.shape[1]
    return pl.pallas_call(
        body, name="ada_fwd", out_shape=jax.ShapeDtypeStruct((NDEV, n), F32),
        in_specs=[pl.BlockSpec(memory_space=pltpu.VMEM)] * 3,
        out_specs=pl.BlockSpec(memory_space=pltpu.VMEM),
        compiler_params=pltpu.CompilerParams(vmem_limit_bytes=VMEM_LIMIT),
    )(c_all, w_sh, b_sh)


def ada_wgrad(c_all_t, dmod_sh):
    def body(c_ref, d_ref, o_ref):
        o_ref[...] = _mmh(_silu(c_ref[...]), d_ref[...])

    return pl.pallas_call(
        body, name="ada_wgrad", out_shape=jax.ShapeDtypeStruct((c_all_t.shape[0], dmod_sh.shape[1]), F32),
        in_specs=[pl.BlockSpec(memory_space=pltpu.VMEM)] * 2,
        out_specs=pl.BlockSpec(memory_space=pltpu.VMEM),
        compiler_params=pltpu.CompilerParams(vmem_limit_bytes=VMEM_LIMIT),
    )(c_all_t, dmod_sh)


def adamw(gparts, w, m, v, name):
    p, r, c = gparts.shape
    tr = r if r <= 256 else _pick(r, (256, 128, 64, 32, 16, 8))
    c1 = 1.0 - B1 ** STEP
    c2 = 1.0 - B2 ** STEP

    def body(g_ref, w_ref, m_ref, v_ref, go_ref, d_ref, mo_ref, vo_ref):
        g = g_ref[0].astype(F32)
        for s in range(1, p):
            g = g + g_ref[s].astype(F32)
        mn = B1 * m_ref[0] + (1.0 - B1) * g
        vn = B2 * v_ref[0] + (1.0 - B2) * (g * g)
        go_ref[0] = g
        d_ref[0] = -LR * ((mn / c1) / (jnp.sqrt(vn / c2) + AEPS) + WD * w_ref[0])
        mo_ref[0] = mn
        vo_ref[0] = vn

    spec = pl.BlockSpec((1, tr, c), lambda i: (0, i, 0))
    return pl.pallas_call(
        body, name=name, grid=(r // tr,),
        in_specs=[pl.BlockSpec((p, tr, c), lambda i: (0, i, 0)), spec, spec, spec],
        out_specs=[spec] * 4,
        out_shape=[jax.ShapeDtypeStruct((1, r, c), F32)] * 4,
        compiler_params=_cparams(),
    )(gparts, w, m, v)


def _me():
    x, y, c = lax.axis_index("x"), lax.axis_index("y"), lax.axis_index("c")
    return x, y, c, 4 * x + 2 * y + c


def _peer(x, y, c, d):
    px = 1 - x if (d >> 2) & 1 else x
    py = 1 - y if (d >> 1) & 1 else y
    pc = 1 - c if d & 1 else c
    return (px, py, pc), 4 * px + 2 * py + pc


def _exchange(arrs, name, scatter):
    n = len(arrs)

    def body(*refs):
        ins, outs = refs[:n], refs[n:2 * n]
        send, recv, lsem = refs[2 * n:]
        x, y, c, me = _me()
        remote, local = [], []
        for k in range(n):
            src = ins[k].at[me] if scatter else ins[k]
            cp = pltpu.make_async_copy(src, outs[k].at[me], lsem.at[k])
            cp.start()
            local.append(cp)
            for d in range(1, NDEV):
                dev, pid = _peer(x, y, c, d)
                src = ins[k].at[pid] if scatter else ins[k]
                cp = pltpu.make_async_remote_copy(src_ref=src, dst_ref=outs[k].at[me],
                                                  send_sem=send.at[k, d - 1], recv_sem=recv.at[k, d - 1],
                                                  device_id=dev, device_id_type=pl.DeviceIdType.MESH)
                cp.start()
                remote.append(cp)
        for cp in remote:
            cp.wait()
        for cp in local:
            cp.wait()

    shapes = [a.shape if scatter else (NDEV,) + a.shape for a in arrs]
    return pl.pallas_call(
        body, name=name,
        in_specs=[pl.BlockSpec(memory_space=pl.ANY)] * n,
        out_specs=[pl.BlockSpec(memory_space=pl.ANY)] * n,
        out_shape=[jax.ShapeDtypeStruct(s, a.dtype) for s, a in zip(shapes, arrs)],
        scratch_shapes=[pltpu.SemaphoreType.DMA((n, NDEV - 1)), pltpu.SemaphoreType.DMA((n, NDEV - 1)),
                        pltpu.SemaphoreType.DMA((n,))],
        compiler_params=pltpu.CompilerParams(has_side_effects=True),
    )(*arrs)


def all_gather(arrs, name):
    return _exchange(arrs, name, False)


def all_gather_two_level(shard, name):
    def body(x_ref, out_ref, send, recv, lsem):
        x, y, c, _ = _me()
        sibling = (x, y, 1 - c)
        chips = [(1 - x, y), (x, 1 - y), (1 - x, 1 - y)]

        def slot(px, py, pc):
            return out_ref.at[4 * px + 2 * py + pc]

        def copy(k, block, to, src=None):
            return pltpu.make_async_remote_copy(
                src_ref=slot(*block) if src is None else src, dst_ref=slot(*block),
                send_sem=send.at[k], recv_sem=recv.at[k], device_id=to, device_id_type=pl.DeviceIdType.MESH)

        mine = pltpu.make_async_copy(x_ref, slot(x, y, c), lsem)
        mine.start()
        first = [copy(0, (x, y, c), sibling, src=x_ref)]
        first += [copy(1 + j, (x, y, c), (*chip, c), src=x_ref) for j, chip in enumerate(chips)]
        for cp in first:
            cp.start()
        passed = [copy(4 + j, (*chip, c), sibling) for j, chip in enumerate(chips)]
        for j, chip in enumerate(chips):
            copy(1 + j, (*chip, c), (x, y, c)).wait_recv()
            passed[j].start()
        copy(0, sibling, (x, y, c)).wait_recv()
        for j, chip in enumerate(chips):
            copy(4 + j, (*chip, 1 - c), (x, y, c)).wait_recv()
        for cp in first + passed:
            cp.wait_send()
        mine.wait()

    return pl.pallas_call(
        body, name=name,
        in_specs=[pl.BlockSpec(memory_space=pl.ANY)],
        out_specs=pl.BlockSpec(memory_space=pl.ANY),
        out_shape=jax.ShapeDtypeStruct((NDEV,) + shard.shape, shard.dtype),
        scratch_shapes=[pltpu.SemaphoreType.DMA((NPEER,)), pltpu.SemaphoreType.DMA((NPEER,)),
                        pltpu.SemaphoreType.DMA],
        compiler_params=pltpu.CompilerParams(has_side_effects=True),
    )(shard)


def all_to_all(arrs, name):
    return _exchange(arrs, name, True)


_HBM = pl.BlockSpec(memory_space=pltpu.HBM)
_SEM = pl.BlockSpec(memory_space=pltpu.SEMAPHORE)
_EFFECT = pltpu.SideEffectType.DATAFLOW_SIDE_EFFECTING
NPEER = NDEV - 1


def exchange_start(arrs, name, scatter):
    n = len(arrs)
    lands = [lax.empty(a.shape if scatter else (NDEV,) + a.shape, a.dtype) for a in arrs]

    def body(*refs):
        ins, lrefs = refs[:n], refs[n:2 * n]
        send, recv, token = refs[2 * n], refs[2 * n + 1], refs[-1]
        x, y, c, me = _me()
        for k in range(n):
            for d in range(1, NDEV):
                dev, pid = _peer(x, y, c, d)
                src = ins[k].at[pid] if scatter else ins[k]
                pltpu.make_async_remote_copy(src_ref=src, dst_ref=lrefs[k].at[me],
                                             send_sem=send.at[k * NPEER + d - 1], recv_sem=recv.at[k * NPEER + d - 1],
                                             device_id=dev, device_id_type=pl.DeviceIdType.MESH).start()
        token[...] = jnp.zeros_like(token)

    thru = [pltpu.HBM(a.shape, a.dtype) for a in list(arrs) + lands]
    outs = pl.pallas_call(
        body, name=name,
        out_shape=(pltpu.SemaphoreType.DMA((n * NPEER,)), pltpu.SemaphoreType.DMA((n * NPEER,)), *thru,
                   jax.ShapeDtypeStruct((8, 128), F32)),
        in_specs=[_HBM] * (2 * n),
        out_specs=(_SEM, _SEM, *([_HBM] * (2 * n)), pl.BlockSpec(memory_space=pltpu.VMEM)),
        input_output_aliases={i: 2 + i for i in range(2 * n)},
        compiler_params=pltpu.CompilerParams(has_side_effects=_EFFECT),
    )(*[pltpu.with_memory_space_constraint(a, pltpu.HBM) for a in list(arrs) + lands])
    handle = dict(send=outs[0], recv=outs[1], src=list(outs[2:2 + n]), land=list(outs[2 + n:2 + 2 * n]),
                  scatter=scatter)
    return handle, outs[-1][0, 0]


def exchange_wait(handle, after, name):
    n = len(handle["src"])
    scatter = handle["scatter"]

    def body(*refs):
        ins, lrefs = refs[:n], refs[n:2 * n]
        send, recv = refs[2 * n], refs[2 * n + 1]
        x, y, c, _ = _me()
        for k in range(n):
            for d in range(1, NDEV):
                dev, _ = _peer(x, y, c, d)
                src = ins[k].at[0] if scatter else ins[k]
                cp = pltpu.make_async_remote_copy(src_ref=src, dst_ref=lrefs[k].at[0],
                                                  send_sem=send.at[k * NPEER + d - 1],
                                                  recv_sem=recv.at[k * NPEER + d - 1],
                                                  device_id=dev, device_id_type=pl.DeviceIdType.MESH)
                cp.wait_send()
                cp.wait_recv()

    arrs = handle["src"] + handle["land"]
    outs = pl.pallas_call(
        body, name=name,
        out_shape=tuple(pltpu.HBM(a.shape, a.dtype) for a in arrs),
        in_specs=[_HBM] * (2 * n) + [_SEM, _SEM, pl.BlockSpec(memory_space=pl.ANY)],
        out_specs=tuple([_HBM] * (2 * n)),
        input_output_aliases={i: i for i in range(2 * n)},
        compiler_params=pltpu.CompilerParams(has_side_effects=_EFFECT),
    )(*arrs, handle["send"], handle["recv"], after)
    me = 4 * lax.axis_index("x") + 2 * lax.axis_index("y") + lax.axis_index("c")
    landed = []
    for own, land in zip(outs[:n], outs[n:]):
        mine = lax.dynamic_index_in_dim(own, me, 0, keepdims=True) if scatter else own[None]
        landed.append(lax.dynamic_update_slice_in_dim(land, mine, me, 0))
    return landed


def _cat_from_slabs(slabs):
    _, k, n = slabs.shape

    def cols(lo, hi):
        parts, c = [], lo
        while c < hi:
            j = c // n
            e = min(hi, (j + 1) * n)
            parts.append(slabs[j][:, c - j * n:e - j * n])
            c = e
        return parts

    def zeros(w):
        return [jnp.zeros((k, w), slabs.dtype)]

    return jnp.concatenate(cols(0, 4096) + cols(4112, 9232) + cols(4096, 4104) + zeros(LANE - AH)
                           + cols(4104, 4112) + zeros(NCAT - C_BA - LANE - AH), axis=1)


IN_PIECES = (("pre", C_QKVA, 3072), ("z", C_Z, 1024), ("qb", C_QKVB, 1024), ("kb", C_QKVB + 1024, 1024),
             ("vb", C_QKVB + 2048, 1024), ("ga", C_GATE, 1024), ("gb", C_GATE + 1024, 1024))
_ORIG_SEGS = ((0, 3072, "pre", 0), (3072, 4096, "z", 0), (4096, 4104, "ba", 0), (4104, 4112, "ba", LANE),
              (4112, 5136, "qb", 0), (5136, 6160, "kb", 0), (6160, 7184, "vb", 0), (7184, 8208, "ga", 0),
              (8208, 9232, "gb", 0))


def _orig_cols_from_pieces(gp, lo, hi):
    parts = []
    for a, b, name, off in _ORIG_SEGS:
        s, e = max(a, lo), min(b, hi)
        if s < e:
            parts.append(gp[name][:, off + s - a:off + e - a])
    return parts[0] if len(parts) == 1 else jnp.concatenate(parts, axis=1)


def _pad128(v):
    return jnp.pad(v, ((0, 0), (0, 128 - v.shape[1])))


def local_step(x, tgt, mod, wts, small, late_weights=None, on_grads=None):
    if on_grads is None:
        on_grads = lambda group, gd: jnp.zeros((), F32)
    t = x.shape[0]
    nc = t // CH
    shift_t, scale_t, gate_t, shift_f, scale_f, gate_f = mod
    wcat = _cat_from_slabs(wts["w_in_slabs"])
    a_log = _pad128(small["a_log"])
    dtb = _pad128(small["dt_bias"])
    vecs = dict(bga=small["b_gate"][:, :D], bgb=small["b_gate"][:, D:], gate_t=gate_t, g1=small["ln1_g"],
                b1=small["ln1_b"], scale_f=scale_f, shift_f=shift_f)

    h1 = modulate(x, scale_t, shift_t, "modulate_t")
    proj = matmul(h1, wcat, F32, "in_proj")
    q, k, v, gcs, beta = prep_fwd(proj, small["conv_a"], a_log, dtb)

    u, w, qg, kd, qk, eg, tinv = c1_fwd(q, k, v, gcs, beta)
    oa, sall = c2_fwd(u, w, qg, kd, qk, eg, proj, small["norm_a"])
    bias = bias_table(small["rel_bias"])
    ob = attn_fwd(proj, bias)
    if late_weights is not None:
        wts = {**wts, **late_weights(ob)}
    y1, h2 = merge_fwd(x, oa, ob, proj, vecs, wts["w_a"], wts["w_b"], wts["w_o"])
    up = matmul(h2, wts["w_up"], F32, "up_proj")
    a = ffn_act_fwd(up, small["conv_ffn"], small["b_conv_ffn"])

    da, dy1_res, dffn, dgate_f, dg2, db2, loss = head_fwd_bwd(a, y1, tgt, gate_f, small["ln2_g"], small["ln2_b"],
                                                            wts["w_down"])
    g_w_down = matmul(a, dffn, F32, "wgrad_down", ta=True)
    dup, g_conv_ffn, g_bconv = ffn_act_bwd(up, small["conv_ffn"], small["b_conv_ffn"], da)
    dh2 = matmul(dup, wts["w_up"], F32, "dgrad_up", tb=True)
    g_w_up = matmul(h2, dup, F32, "wgrad_up", ta=True)
    tok = on_grads("ffn", dict(w_up=g_w_up, w_down=g_w_down))
    dy1, dscale_f, dshift_f = modulate_bwd(dh2, y1, dy1_res, scale_f + tok, "modulate_f_bwd")
    (dx_res, doa, dob, dga, dgb, merged, dmix, dpa, dpb,
     dbga, dbgb, dgate_t, dg1, db1) = merge_bwd(x, oa, ob, proj, vecs, wts["w_a"], wts["w_b"], wts["w_o"], dy1)
    g_w_o = matmul(merged, dmix, F32, "wgrad_o", ta=True)
    g_w_a = matmul(oa, dpa, F32, "wgrad_a", ta=True)
    g_w_b = matmul(ob, dpb, F32, "wgrad_b", ta=True)
    tok = on_grads("mix", dict(w_o=g_w_o, w_a=g_w_a, w_b=g_w_b))
    dqb, dkb, dvb, dbias = attn_bwd(proj, bias, dob)
    g_rel = relbias_reduce(bias_table_bwd_layout(dbias))
    du, dw, dqg, dkd, dqk, deg, dz, g_norm = c2_bwd(u, w, qg, kd, qk, eg, proj, small["norm_a"] + tok, sall, doa)
    dq, dk, dv, dgcs, dbeta = c1_bwd(q, k, v, gcs, beta, tinv, du, dw, dqg, dkd, dqk, deg)
    dpre, dbb, daa, g_conv_a, g_alog, g_dtb = prep_bwd(proj, small["conv_a"], a_log, dtb, dq, dk, dv, dgcs, dbeta)
    tok = on_grads("small", dict(conv_a=g_conv_a, rel_bias=g_rel, conv_ffn=g_conv_ffn))
    dba = jnp.concatenate([dbb, daa, jnp.zeros((t, NCAT - C_BA - 2 * LANE), BF16)], axis=1) + tok.astype(BF16)
    dpieces = dict(pre=dpre, z=dz, qb=dqb, kb=dkb, vb=dvb, ga=dga, gb=dgb)
    g_in = {n: matmul(h1, dpieces[n], F32, "wgrad_in_" + n, ta=True) for n, _, _ in IN_PIECES}
    g_in["ba"] = matmul(h1, dba, F32, "wgrad_in_ba", ta=True)
    tok = on_grads("in", g_in)
    dh1 = dgrad_pieces([(dpieces[n], off) for n, off, _ in IN_PIECES], dba, wcat + tok.astype(BF16),
                       "dgrad_in")
    grad_x, dscale_t, dshift_t = modulate_bwd(dh1, x, dx_res, scale_t + tok, "modulate_t_bwd")

    dmod = (dshift_t, dscale_t, dgate_t, dshift_f, dscale_f, dgate_f)
    grads = dict(w_in=_orig_cols_from_pieces(g_in, 0, 9232), w_up=g_w_up, w_down=g_w_down, w_a=g_w_a, w_b=g_w_b, w_o=g_w_o,
                 conv_a=g_conv_a, rel_bias=g_rel, conv_ffn=g_conv_ffn,
                 b_gate=jnp.concatenate([dbga, dbgb], axis=1), a_log=g_alog[:, :AH], dt_bias=g_dtb[:, :AH],
                 norm_a=g_norm, ln1_g=dg1, ln1_b=db1, b_conv_ffn=g_bconv, ln2_g=dg2, ln2_b=db2)
    return loss[0, 0], grad_x, dmod, grads


_REP = {}
_off = 0
for _n, _wd, _pw in (("b_ada", 6144, 6144), ("b_gate", 2048, 2048), ("a_log", 8, 128), ("dt_bias", 8, 128),
                     ("norm_a", 128, 128), ("ln1_g", 1024, 1024), ("ln1_b", 1024, 1024),
                     ("b_conv_ffn", 5632, 5632), ("ln2_g", 1024, 1024), ("ln2_b", 1024, 1024), ("loss", 1, 128)):
    _REP[_n] = (_off, _wd, _pw)
    _off += _pw
REP_LEN = _off
REP_NAMES = [n for n in _REP if n != "loss"]
_SH = (("conv_a", (4, 384)), ("rel_bias", (16, 40)), ("conv_ffn", (3, 704)))
SH_LEN = 4352


def _pack_rep(vals):
    parts = []
    for n, (_, wd, pw) in _REP.items():
        a = vals.get(n)
        a = jnp.zeros((1, pw), F32) if a is None else jnp.pad(a.reshape(1, wd), ((0, 0), (0, pw - wd)))
        parts.append(a)
    return jnp.concatenate(parts, axis=1)


def _unpack_rep(vec, name):
    o, wd, _ = _REP[name]
    return vec[:, o:o + wd]


def _pack_sh(vals):
    parts = [vals[n].reshape(vals[n].shape[:-2] + (-1,)) for n, _ in _SH]
    a = jnp.concatenate(parts, axis=-1)
    return jnp.pad(a, [(0, 0)] * (a.ndim - 1) + [(0, SH_LEN - a.shape[-1])])


def _unpack_sh(vec, name):
    o = 0
    for n, shp in _SH:
        sz = shp[0] * shp[1]
        if n == name:
            return vec[0, o:o + sz].reshape(shp)
        o += sz
    raise KeyError(name)


def _col_shards(a, n):
    return a.reshape(a.shape[0], NDEV, n).transpose(1, 0, 2)


def kernel(x, c, w_ada, b_ada, w_in, b_gate, conv_a, a_log, dt_bias, norm_a, rel_bias, w_branch_a, w_branch_b, w_o, ln1_g, ln1_b, w_up, conv_ffn, b_conv_ffn, w_down, ln2_g, ln2_b, loss_target, m_w_ada, m_b_ada, m_w_in, m_b_gate, m_conv_a, m_a_log, m_dt_bias, m_norm_a, m_rel_bias, m_w_branch_a, m_w_branch_b, m_w_o, m_ln1_g, m_ln1_b, m_w_up, m_conv_ffn, m_b_conv_ffn, m_w_down, m_ln2_g, m_ln2_b, v_w_ada, v_b_ada, v_w_in, v_b_gate, v_conv_a, v_a_log, v_dt_bias, v_norm_a, v_rel_bias, v_w_branch_a, v_w_branch_b, v_w_o, v_ln1_g, v_ln1_b, v_w_up, v_conv_ffn, v_b_conv_ffn, v_w_down, v_ln2_g, v_ln2_b):
    W = dict(w_ada=w_ada, b_ada=b_ada, w_in=w_in, b_gate=b_gate, conv_a=conv_a, a_log=a_log, dt_bias=dt_bias,
             norm_a=norm_a, rel_bias=rel_bias, w_branch_a=w_branch_a, w_branch_b=w_branch_b, w_o=w_o, ln1_g=ln1_g,
             ln1_b=ln1_b, w_up=w_up, conv_ffn=conv_ffn, b_conv_ffn=b_conv_ffn, w_down=w_down, ln2_g=ln2_g,
             ln2_b=ln2_b)
    M = dict(w_ada=m_w_ada, b_ada=m_b_ada, w_in=m_w_in, b_gate=m_b_gate, conv_a=m_conv_a, a_log=m_a_log,
             dt_bias=m_dt_bias, norm_a=m_norm_a, rel_bias=m_rel_bias, w_branch_a=m_w_branch_a,
             w_branch_b=m_w_branch_b, w_o=m_w_o, ln1_g=m_ln1_g, ln1_b=m_ln1_b, w_up=m_w_up, conv_ffn=m_conv_ffn,
             b_conv_ffn=m_b_conv_ffn, w_down=m_w_down, ln2_g=m_ln2_g, ln2_b=m_ln2_b)
    V = dict(w_ada=v_w_ada, b_ada=v_b_ada, w_in=v_w_in, b_gate=v_b_gate, conv_a=v_conv_a, a_log=v_a_log,
             dt_bias=v_dt_bias, norm_a=v_norm_a, rel_bias=v_rel_bias, w_branch_a=v_w_branch_a,
             w_branch_b=v_w_branch_b, w_o=v_w_o, ln1_g=v_ln1_g, ln1_b=v_ln1_b, w_up=v_w_up, conv_ffn=v_conv_ffn,
             b_conv_ffn=v_b_conv_ffn, w_down=v_w_down, ln2_g=v_ln2_g, ln2_b=v_ln2_b)
    W3, M3, V3 = W, M, V
    W, M, V = ({n: a[0] for n, a in dct.items()} for dct in (W, M, V))
    me = 4 * lax.axis_index("x") + 2 * lax.axis_index("y") + lax.axis_index("c")
    big = ("w_in", "w_up", "w_down", "w_branch_a", "w_branch_b", "w_o")

    g_in = all_gather_two_level(W["w_in"].astype(BF16), "gather_w_in")
    wts = dict(w_in_slabs=g_in)
    c_all, sh_all = all_gather([c, _pack_sh({n: W[n] for n, _ in _SH})[None]], "gather_small")
    c_all = c_all.reshape(NDEV, D)
    sh_all = sh_all.reshape(NDEV, SH_LEN)

    def full_small(name, shp):
        o = 0
        for n, s in _SH:
            if n == name:
                break
            o += s[0] * s[1]
        sz = shp[0] * shp[1]
        return sh_all[:, o:o + sz].reshape(NDEV, shp[0], shp[1]).transpose(1, 0, 2).reshape(shp[0], NDEV * shp[1])

    small = dict(conv_a=full_small("conv_a", (4, 384)), rel_bias=full_small("rel_bias", (16, 40)),
                 conv_ffn=full_small("conv_ffn", (3, 704)),
                 b_gate=W["b_gate"][None], a_log=W["a_log"][None], dt_bias=W["dt_bias"][None],
                 norm_a=W["norm_a"][None], ln1_g=W["ln1_g"][None], ln1_b=W["ln1_b"][None],
                 b_conv_ffn=W["b_conv_ffn"][None], ln2_g=W["ln2_g"][None], ln2_b=W["ln2_b"][None])

    nsh = w_ada.shape[2]
    b_sh = lax.dynamic_slice(W["b_ada"][None], (0, me * nsh), (1, nsh))
    mod_sh = ada_fwd(c_all, W["w_ada"], b_sh)
    (mod_rows,) = all_to_all([mod_sh[:, None, :]], "scatter_mod")
    mod6 = mod_rows.reshape(6, D)

    after_small = (g_in[0, 0, 0].astype(F32) * 0.0 + mod6[0, 0] * 0.0).astype(BF16)
    late, late_tok = exchange_start([W[n].astype(BF16) + after_small for n in big[1:]], "gather_late_start", False)

    def late_weights(after):
        g_up, g_down, g_a, g_b, g_o = exchange_wait(late, after, "gather_late_wait")
        return dict(w_up=g_up.transpose(1, 0, 2).reshape(D, -1), w_down=g_down.reshape(DFF, D),
                    w_a=g_a.reshape(D, D), w_b=g_b.reshape(D, D), w_o=g_o.reshape(D, D))

    mod6 = mod6 + late_tok
    mod = tuple(mod6[i:i + 1] for i in range(6))

    pending = {}

    def on_grads(group, gd):
        if group == "small":
            sh_parts = {"conv_a": _col_shards(gd["conv_a"], 384), "rel_bias": _col_shards(gd["rel_bias"], 40),
                        "conv_ffn": _col_shards(gd["conv_ffn"], 704)}
            (pending["small"],) = all_to_all([_pack_sh(sh_parts)[:, None, :]], "scatter_small_grads")
            return pending["small"][0, 0, 0] * 0.0
        if group == "ffn":
            slabs = [_col_shards(gd["w_up"], w_up.shape[2]), gd["w_down"].reshape(NDEV, -1, D)]
        elif group == "mix":
            slabs = [gd[n].reshape(NDEV, -1, D) for n in ("w_a", "w_b", "w_o")]
        else:
            nin = w_in.shape[2]
            slabs = [jnp.stack([_orig_cols_from_pieces(gd, j * nin, (j + 1) * nin) for j in range(NDEV)], axis=0)]
        pending[group], tok = exchange_start([s.astype(BF16) for s in slabs], "scatter_" + group + "_start", True)
        return tok

    loss, grad_x, dmod, g = local_step(x[0], loss_target[0], mod, wts, small, late_weights, on_grads)

    rep_vals = {n: g[n] for n in REP_NAMES if n != "b_ada"}
    rep_vals["b_ada"] = jnp.concatenate(dmod, axis=1)
    rep_vals["loss"] = loss.reshape(1, 1)
    (rep_all,) = all_gather([_pack_rep(rep_vals)[None]], "gather_small_grads")
    rep_all = rep_all.reshape(NDEV, 1, REP_LEN)
    zero1 = jnp.zeros((1, 1), F32)
    rep_out = adamw(rep_all, _pack_rep({**{n: W[n][None] for n in REP_NAMES}, "loss": zero1})[None],
                    _pack_rep({**{n: M[n][None] for n in REP_NAMES}, "loss": zero1})[None],
                    _pack_rep({**{n: V[n][None] for n in REP_NAMES}, "loss": zero1})[None], "adamw_small")
    rep_out = [o[0] for o in rep_out]
    loss_total = _unpack_rep(rep_out[0], "loss")[0, 0]

    o_ada = _REP["b_ada"][0]
    dmod_all = rep_all[:, 0, o_ada:o_ada + 6 * D]
    dmod_sh = lax.dynamic_slice(dmod_all, (0, me * nsh), (NDEV, nsh))
    g_w_ada = ada_wgrad(c_all.T, dmod_sh)

    p_up, p_down = exchange_wait(pending["ffn"], grad_x, "scatter_ffn_wait")
    p_a, p_b, p_o = exchange_wait(pending["mix"], grad_x, "scatter_mix_wait")
    (p_in,) = exchange_wait(pending["in"], grad_x, "scatter_in_wait")
    parts = [p_in, p_up, p_down, p_a, p_b, p_o]
    sh_recv = pending["small"]

    res = {}
    for n, p in zip(big, parts):
        res[n] = adamw(p, W3[n], M3[n], V3[n], "adamw_" + n)
    res["w_ada"] = adamw(g_w_ada[None], W3["w_ada"], M3["w_ada"], V3["w_ada"], "adamw_w_ada")
    sh_out = adamw(sh_recv, _pack_sh({n: W[n] for n, _ in _SH})[None, None],
                   _pack_sh({n: M[n] for n, _ in _SH})[None, None],
                   _pack_sh({n: V[n] for n, _ in _SH})[None, None], "adamw_small_sharded")
    for n, _ in _SH:
        res[n] = tuple(_unpack_sh(o[0], n)[None] for o in sh_out)
    for n in REP_NAMES:
        res[n] = tuple(_unpack_rep(o, n) for o in rep_out)

    order = ("w_ada", "b_ada", "w_in", "b_gate", "conv_a", "a_log", "dt_bias", "norm_a", "rel_bias", "w_branch_a",
             "w_branch_b", "w_o", "ln1_g", "ln1_b", "w_up", "conv_ffn", "b_conv_ffn", "w_down", "ln2_g", "ln2_b")
    outs = [loss_total, grad_x[None]]
    for kind in range(4):
        outs += [res[n][kind] for n in order]
    return tuple(outs)
```

```python
import functools
import math

import numpy as np
import jax
import jax.numpy as jnp
from jax import lax
from jax.experimental import pallas as pl
from jax.experimental.pallas import tpu as pltpu

F32 = jnp.float32
BF16 = jnp.bfloat16
HI = lax.Precision.HIGHEST

D = 1024
CH = 64
AH, ADK = 8, 128
BH, BDH = 16, 64
BPREV = 8
BMAXREL = 256
RELSZ = CH + BMAXREL
DFF = 2816
ALPHA = 2.0 ** 0.25
LN_EPS, RMS_EPS, L2_EPS = 1e-5, 1e-6, 1e-6
NEG = -1e30
LR, B1, B2, AEPS, WD, STEP = 1e-3, 0.9, 0.999, 1e-8, 0.01, 10
NDEV = 8
HALO = 8
LANE = 128
TQ = 512
VMEM_LIMIT = 56 * 1024 * 1024

C_QKVA, C_Z, C_QKVB, C_GATE, C_BA, NCAT = 0, 3072, 4096, 7168, 9216, 9728


def _cparams(n_axes=1, vmem=VMEM_LIMIT):
    return pltpu.CompilerParams(dimension_semantics=("arbitrary",) * n_axes, vmem_limit_bytes=vmem)


def _dg(a, b, ca, cb):
    return lax.dot_general(a.astype(BF16), b.astype(BF16), (((ca,), (cb,)), ((), ())),
                           preferred_element_type=F32)


@jax.custom_vjp
def mm_nn(a, b):
    return _dg(a, b, 1, 0)


@jax.custom_vjp
def mm_nt(a, b):
    return _dg(a, b, 1, 1)


@jax.custom_vjp
def mm_tn(a, b):
    return _dg(a, b, 0, 0)


mm_nn.defvjp(lambda a, b: (mm_nn(a, b), (a, b)),
             lambda r, g: (mm_nt(g, r[1]).astype(r[0].dtype), mm_tn(r[0], g).astype(r[1].dtype)))
mm_nt.defvjp(lambda a, b: (mm_nt(a, b), (a, b)),
             lambda r, g: (mm_nn(g, r[1]).astype(r[0].dtype), mm_tn(g, r[0]).astype(r[1].dtype)))
mm_tn.defvjp(lambda a, b: (mm_tn(a, b), (a, b)),
             lambda r, g: (mm_nt(r[1], g).astype(r[0].dtype), mm_nn(r[0], g).astype(r[1].dtype)))


@jax.custom_vjp
def mm_w(a, w):
    return _dg(a, w, 1, 0)


mm_w.defvjp(lambda a, w: (mm_w(a, w), (a, w)),
            lambda r, g: (mm_nt(g, r[1]).astype(r[0].dtype), jnp.zeros_like(r[1])))


def _mmh(a, b):
    return lax.dot_general(a, b, (((1,), (0,)), ((), ())), precision=HI, preferred_element_type=F32)


def _bdg(a, b, ca, cb):
    return lax.dot_general(a.astype(BF16), b.astype(BF16), (((ca,), (cb,)), ((0,), (0,))),
                           preferred_element_type=F32)


@jax.custom_vjp
def bmm_nn(a, b):
    return _bdg(a, b, 2, 1)


@jax.custom_vjp
def bmm_nt(a, b):
    return _bdg(a, b, 2, 2)


@jax.custom_vjp
def bmm_tn(a, b):
    return _bdg(a, b, 1, 1)


bmm_nn.defvjp(lambda a, b: (bmm_nn(a, b), (a, b)), lambda r, g: (bmm_nt(g, r[1]), bmm_tn(r[0], g)))
bmm_nt.defvjp(lambda a, b: (bmm_nt(a, b), (a, b)), lambda r, g: (bmm_nn(g, r[1]), bmm_tn(g, r[0])))
bmm_tn.defvjp(lambda a, b: (bmm_tn(a, b), (a, b)), lambda r, g: (bmm_nt(r[1], g), bmm_nn(r[0], g)))


def _bdg3(a, b, ca, cb):
    return lax.dot_general(a, b, (((ca,), (cb,)), ((0,), (0,))), precision=HI, preferred_element_type=F32)


def _bdgp(a, b, ca, cb):
    return _bdg(a, b, ca, cb)


@jax.custom_vjp
def bmm3_nn(a, b):
    return _bdgp(a, b, 2, 1)


bmm3_nn.defvjp(lambda a, b: (bmm3_nn(a, b), (a, b)),
               lambda r, g: (_bdgp(g, r[1], 2, 2), _bdgp(r[0], g, 1, 1)))


def _sigmoid(x):
    return 0.5 * jnp.tanh(0.5 * x) + 0.5


def _silu(x):
    return x * _sigmoid(x)


def _softplus(x):
    return jnp.maximum(x, 0.0) + jnp.log(1.0 + jnp.exp(-jnp.abs(x)))


def _layernorm(r, g, b):
    mu = jnp.mean(r, axis=-1, keepdims=True)
    xc = r - mu
    var = jnp.mean(xc * xc, axis=-1, keepdims=True)
    return xc * lax.rsqrt(var + LN_EPS) * g + b


def _iota2(shape, dim):
    return lax.broadcasted_iota(jnp.int32, shape, dim)


@jax.custom_vjp
def causal_conv(ext, rows):
    k = len(rows)
    y = None
    for j in range(k):
        s = k - 1 - j
        r = pltpu.roll(ext, s, 0) if s else ext
        t = r[HALO:] * rows[j]
        y = t if y is None else y + t
    return y


def _causal_conv_fwd(ext, rows):
    return causal_conv(ext, rows), (ext, rows)


def _causal_conv_bwd(res, g):
    ext, rows = res
    n = ext.shape[0]
    k = len(rows)
    gext = jnp.concatenate([jnp.zeros((HALO, g.shape[1]), g.dtype), g], axis=0)
    dext = None
    drows = []
    for j in range(k):
        s = k - 1 - j
        up = pltpu.roll(gext, n - s, 0) if s else gext
        t = up * rows[j]
        dext = t if dext is None else dext + t
        r = pltpu.roll(ext, s, 0) if s else ext
        drows.append(jnp.sum(g * r[HALO:], axis=0, keepdims=True))
    return dext, tuple(drows)


causal_conv.defvjp(_causal_conv_fwd, _causal_conv_bwd)


def _chunk_masks(tm):
    i = _iota2((tm, tm), 0)
    j = _iota2((tm, tm), 1)
    same = (i ^ j) < CH
    lower = jnp.where(same & (j <= i), 1.0, 0.0).astype(F32)
    upper = jnp.where(same & (i <= j), 1.0, 0.0).astype(F32)
    return lower, upper


@jax.custom_vjp
def chunk_cumsum(g):
    lower, _ = _chunk_masks(g.shape[0])
    return _mmh(lower, g)


def _chunk_cumsum_bwd(_, ct):
    _, upper = _chunk_masks(ct.shape[0])
    return (_mmh(upper, ct),)


chunk_cumsum.defvjp(lambda g: (chunk_cumsum(g), None), _chunk_cumsum_bwd)


@jax.custom_vjp
def inv_unit_lower(a):
    n = a.shape[-1]
    eye = jnp.where(_iota2((1, n, n), 1) == _iota2((1, n, n), 2), 1.0, 0.0).astype(F32)
    x = eye - a
    p = _bdg3(a, a, 2, 1)
    steps = int(math.log2(n)) - 1
    for s in range(steps):
        x = x + _bdg3(x, p, 2, 1)
        if s + 1 < steps:
            p = _bdg3(p, p, 2, 1)
    return x


def _inv_fwd(a):
    t = inv_unit_lower(a)
    return t, t


def _inv_bwd(t, g):
    return (-_bdgp(_bdgp(t, g, 1, 1), t, 2, 2),)


inv_unit_lower.defvjp(_inv_fwd, _inv_bwd)


@jax.custom_vjp
def inv_known(a, t):
    return t


inv_known.defvjp(lambda a, t: (t, t), lambda t, g: (_inv_bwd(t, g)[0], jnp.zeros_like(t)))


def prep_head_fn(ext, rows, scale):
    s = _silu(causal_conv(ext, rows))
    if scale is None:
        return s
    return s * (lax.rsqrt(jnp.sum(s * s, axis=-1, keepdims=True) + L2_EPS) * scale)


def prep_gate_fn(bb, aa, a_log, dtb):
    g = -jnp.exp(a_log) * _softplus(aa + dtb)
    return chunk_cumsum(g), _sigmoid(bb)


PREP_SCALES = (ADK ** -0.5, 1.0, None)


def _head_cols(a):
    lane = _iota2((1, LANE), 1)
    return jnp.concatenate([jnp.sum(jnp.where(lane == h, a, 0.0), axis=1, keepdims=True)[None]
                            for h in range(AH)], axis=0)


def _head_rows(a):
    at = a.T[:AH]
    sub = _iota2((AH, 1), 0)
    return jnp.concatenate([jnp.sum(jnp.where(sub == h, at, 0.0), axis=0, keepdims=True)[None]
                            for h in range(AH)], axis=0)


def c1_heads(q, k, v, gcs, beta, tinv_saved=None):
    gcol = _head_cols(gcs)
    grow = _head_rows(gcs)
    bcol = _head_cols(beta)
    i = _iota2((1, CH, CH), 1)
    j = _iota2((1, CH, CH), 2)
    causal = j <= i
    strict = j < i
    diff = gcol - grow
    decay = jnp.where(causal, jnp.exp(jnp.where(causal, diff, 0.0)), 0.0)
    kb = k * bcol
    vb = v * bcol
    a_low = jnp.where(strict, bmm_nt(kb, k) * decay, 0.0)
    tinv = inv_unit_lower(a_low) if tinv_saved is None else inv_known(a_low, tinv_saved)
    egc = jnp.exp(gcol)
    u = bmm3_nn(tinv, vb)
    w = bmm3_nn(tinv, kb * egc)
    qk = jnp.where(causal, bmm_nt(q, k) * decay, 0.0)
    glast = jnp.sum(jnp.where(_iota2((1, CH, 1), 1) == CH - 1, gcol, 0.0), axis=1, keepdims=True)
    qg = q * egc
    kd = k * jnp.exp(glast - gcol)
    eg = jnp.exp(glast) * jnp.ones((1, 1, ADK), F32)
    return u, w, qk, qg, kd, eg, tinv


def c2_heads(s, u, w, qk, qg, kd, eg, z, nw):
    vn = u - bmm_nn(w, s)
    o = bmm_nn(qg, s) + bmm_nn(qk, vn)
    s2 = s * eg + bmm_tn(kd, vn)
    ms = jnp.mean(o * o, axis=-1, keepdims=True)
    og = o * lax.rsqrt(ms + RMS_EPS) * nw * _silu(z)
    return og, s2


ATT_SCALE = BDH ** -0.5


def _head_mask(hh):
    lane = _iota2((1, 2 * BDH), 1)
    return jnp.where((lane >= hh * BDH) & (lane < (hh + 1) * BDH), 1.0, 0.0).astype(F32)


def attn_sub_fwd(q, k, v, bias2, r, firstf):
    col = _iota2((1, KWIN), 1) + r * SUBQ
    nokey = jnp.where(col < TQ, firstf, 0.0) * NEG
    out, probs = None, []
    for hh in range(2):
        hm = _head_mask(hh)
        s = mm_nt(q * (hm * ATT_SCALE), k) + (assemble_bias(bias2[hh], r) + nokey)
        p = jnp.exp(s - jnp.max(s, axis=-1, keepdims=True))
        inv = 1.0 / jnp.sum(p, axis=-1, keepdims=True)
        o = mm_nn(p, v) * (inv * hm)
        out = o if out is None else out + o
        probs.append(p * inv)
    return out, probs


def attn_sub_bwd(q, k, v, o, do, probs, r):
    dq, dk, dv, dss = None, None, None, []
    for hh in range(2):
        hm = _head_mask(hh)
        p = probs[hh]
        doh = do * hm
        ds = p * (mm_nt(doh, v) - jnp.sum(doh * o, axis=-1, keepdims=True))
        dqh = mm_nn(ds, k) * (hm * ATT_SCALE)
        dkh = mm_tn(ds, q * (hm * ATT_SCALE))
        dvh = mm_tn(p, doh)
        dq = dqh if dq is None else dq + dqh
        dk = dkh if dk is None else dk + dkh
        dv = dvh if dv is None else dv + dvh
        dss.append(ds)
    return dq, dk, dv, dss


def merge_fn(x, oa, ob, gra, grb, p_pa, p_pb, p_mix, bga, bgb, gate_t, g1, b1, scale_f, shift_f,
             wa, wb, wo):
    ga = _sigmoid(gra + bga)
    gb = _sigmoid(grb + bgb)
    pa = mm_w(oa, wa) + p_pa
    pb = mm_w(ob, wb) + p_pb
    merged = ga * pa + gb * pb
    mix = mm_w(merged, wo) + p_mix
    y1 = _layernorm(ALPHA * x + gate_t * mix, g1, b1)
    return y1, merged


def ffn_act_fn(extg, extv, rows_g, rows_v, bg, bv):
    return _silu(causal_conv(extg, rows_g) + bg) * (causal_conv(extv, rows_v) + bv)


def head_fn(a, y1, p_ffn, gate_f, g2, b2, tgt, wd):
    ffn = mm_w(a, wd) + p_ffn
    y2 = _layernorm(ALPHA * y1 + gate_f * ffn, g2, b2)
    err = y2 - tgt
    return 0.5 * jnp.sum(jnp.mean(err * err, axis=-1, keepdims=True))


def _rows(tm, width, colblk=0, order=None):
    if order is None:
        return pl.BlockSpec((tm, width), lambda i: (i, colblk))
    return pl.BlockSpec((tm, width), lambda i: (order(i), colblk))


def _const(shape):
    nd = len(shape)
    return pl.BlockSpec(shape, lambda *_: (0,) * nd)


def _pick(n, cands):
    for c in cands:
        if n % c == 0:
            return c
    raise ValueError(f"no tile for {n}")


def _tile(n, cap):
    best = None
    for c in range(LANE, min(n, cap) + 1, LANE):
        if n % c == 0:
            best = c
    if best is None:
        raise ValueError(f"no tile for {n}")
    return best


def _onehot_rows(k, j):
    return jnp.where(_iota2((k, 1), 0) == j, 1.0, 0.0).astype(F32)


def _stack_rows(drows):
    k = len(drows)
    out = None
    for j in range(k):
        tj = _onehot_rows(k, j) * drows[j]
        out = tj if out is None else out + tj
    return out


def matmul(a, w, out_dtype, name, ta=False, tb=False):
    kdim, m = a.shape if ta else a.shape[::-1]
    n = w.shape[0] if tb else w.shape[1]
    tm = _tile(m, 2048 if kdim <= 1024 else 1024)
    tn = _tile(n, 1024)
    tk = _tile(kdim, 2560)
    nk = kdim // tk
    a_spec = (pl.BlockSpec((tk, tm), lambda i, j, k: (k, i)) if ta
              else pl.BlockSpec((tm, tk), lambda i, j, k: (i, k)))
    w_spec = (pl.BlockSpec((tn, tk), lambda i, j, k: (j, k)) if tb
              else pl.BlockSpec((tk, tn), lambda i, j, k: (k, j)))

    def body(a_ref, w_ref, o_ref, *scratch):
        p = _dg(a_ref[...], w_ref[...], 0 if ta else 1, 1 if tb else 0)
        if nk == 1:
            o_ref[...] = p.astype(out_dtype)
            return
        acc = scratch[0]
        k = pl.program_id(2)

        @pl.when(k == 0)
        def _():
            acc[...] = p

        @pl.when(k > 0)
        def _():
            acc[...] += p

        @pl.when(k == nk - 1)
        def _():
            o_ref[...] = acc[...].astype(out_dtype)

    return pl.pallas_call(
        body, name=name,
        grid=(m // tm, n // tn, nk),
        in_specs=[a_spec, w_spec],
        out_specs=pl.BlockSpec((tm, tn), lambda i, j, k: (i, j)),
        out_shape=jax.ShapeDtypeStruct((m, n), out_dtype),
        scratch_shapes=[] if nk == 1 else [pltpu.VMEM((tm, tn), F32)],
        compiler_params=_cparams(3),
    )(a, w)


def dgrad_pieces(pieces, tail, w, name):
    m = pieces[0][0].shape[0]
    n, ktot = w.shape
    tk = 1024
    tm = _tile(m, 1024)
    wt = tail.shape[1]
    ranges, k0 = [], 0
    for arr, off in pieces:
        assert off == k0 * tk and arr.shape[1] % tk == 0
        ranges.append((k0, k0 + arr.shape[1] // tk))
        k0 = ranges[-1][1]
    nk = k0
    npc = len(pieces)

    def body(*refs):
        a_refs, t_ref, w_ref, wt_ref, o_ref, acc = refs[:npc], refs[npc], refs[npc + 1], refs[npc + 2], refs[npc + 3], refs[npc + 4]
        k = pl.program_id(1)

        @pl.when(k == 0)
        def _():
            acc[...] = _dg(t_ref[...], wt_ref[...], 1, 1)

        for a_ref, (lo, hi) in zip(a_refs, ranges):
            @pl.when((k >= lo) & (k < hi))
            def _(a_ref=a_ref):
                acc[...] += _dg(a_ref[...], w_ref[...], 1, 1)

        @pl.when(k == nk - 1)
        def _():
            o_ref[...] = acc[...]

    def piece_spec(lo, hi):
        return pl.BlockSpec((tm, tk), lambda i, k: (i, jnp.clip(k - lo, 0, hi - lo - 1)))

    return pl.pallas_call(
        body, name=name, grid=(m // tm, nk),
        in_specs=[piece_spec(lo, hi) for lo, hi in ranges] + [
            pl.BlockSpec((tm, wt), lambda i, k: (i, 0)),
            pl.BlockSpec((n, tk), lambda i, k: (0, k)),
            pl.BlockSpec((n, wt), lambda i, k: (0, (ktot - wt) // wt))],
        out_specs=pl.BlockSpec((tm, n), lambda i, k: (i, 0)),
        out_shape=jax.ShapeDtypeStruct((m, n), F32),
        scratch_shapes=[pltpu.VMEM((tm, n), F32)],
        compiler_params=_cparams(2),
    )(*[a for a, _ in pieces], tail, w, w)


def modulate(x, scale, shift, name):
    t, d = x.shape
    tm = _pick(t, (512, 256, 128))

    def body(x_ref, sc_ref, sh_ref, o_ref):
        o_ref[...] = (x_ref[...] * (1.0 + sc_ref[...]) + sh_ref[...]).astype(BF16)

    return pl.pallas_call(
        body, name=name, grid=(t // tm,),
        in_specs=[_rows(tm, d), _const((1, d)), _const((1, d))],
        out_specs=_rows(tm, d),
        out_shape=jax.ShapeDtypeStruct((t, d), BF16),
        compiler_params=_cparams(),
    )(x, scale, shift)


def modulate_bwd(dh, xin, dres, scale, name):
    t, d = dh.shape
    tm = _pick(t, (512, 256, 128))

    def body(dh_ref, x_ref, dr_ref, sc_ref, o_ref, dsc_ref, dsh_ref):
        i = pl.program_id(0)
        dh_v = dh_ref[...]
        o_ref[...] = dr_ref[...] + dh_v * (1.0 + sc_ref[...])

        @pl.when(i == 0)
        def _():
            dsc_ref[...] = jnp.zeros_like(dsc_ref)
            dsh_ref[...] = jnp.zeros_like(dsh_ref)

        dsc_ref[...] += jnp.sum(dh_v * x_ref[...], axis=0, keepdims=True)
        dsh_ref[...] += jnp.sum(dh_v, axis=0, keepdims=True)

    return pl.pallas_call(
        body, name=name, grid=(t // tm,),
        in_specs=[_rows(tm, d), _rows(tm, d), _rows(tm, d), _const((1, d))],
        out_specs=[_rows(tm, d), _const((1, d)), _const((1, d))],
        out_shape=[jax.ShapeDtypeStruct((t, d), F32), jax.ShapeDtypeStruct((1, d), F32),
                   jax.ShapeDtypeStruct((1, d), F32)],
        compiler_params=_cparams(),
    )(dh, xin, dres, scale)


PREP_TM = 128


def _halo_specs(tm, width, colblk, order):
    per = tm // HALO
    return [pl.BlockSpec((HALO, width), lambda i: (jnp.maximum(order(i) * per - 1, 0), colblk)),
            pl.BlockSpec((tm, width), lambda i: (order(i), colblk))]


def prep_fwd(proj, conv_a, a_log, dtb):
    t = proj.shape[0]
    tm = PREP_TM
    nt = t // tm
    wq = 3 * D

    def body(prev_ref, cur_ref, bb_ref, aa_ref, cw_ref, al_ref, dt_ref, q_ref, k_ref, v_ref, g_ref, b_ref):
        i = pl.program_id(0)
        flag = jnp.where(i > 0, 1.0, 0.0)
        for part, o_ref in enumerate((q_ref, k_ref, v_ref)):
            for h in range(AH):
                sl = slice(part * D + h * ADK, part * D + (h + 1) * ADK)
                ext = jnp.concatenate([prev_ref[:, sl] * flag, cur_ref[:, sl]], axis=0)
                rows = tuple(cw_ref[j:j + 1, sl] for j in range(4))
                o_ref[h] = prep_head_fn(ext, rows, PREP_SCALES[part])
        gcs, beta = prep_gate_fn(bb_ref[...], aa_ref[...], al_ref[...], dt_ref[...])
        g_ref[...] = gcs
        b_ref[...] = beta

    ident = lambda i: i
    hm = pl.BlockSpec((AH, tm, ADK), lambda i: (0, i, 0))
    return pl.pallas_call(
        body, name="prep_fwd", grid=(nt,),
        in_specs=_halo_specs(tm, wq, 0, ident) + [
            _rows(tm, 128, C_BA // 128), _rows(tm, 128, C_BA // 128 + 1),
            _const((4, wq)), _const((1, 128)), _const((1, 128))],
        out_specs=[hm, hm, hm, _rows(tm, 128), _rows(tm, 128)],
        out_shape=[jax.ShapeDtypeStruct((AH, t, ADK), F32)] * 3 + [jax.ShapeDtypeStruct((t, 128), F32)] * 2,
        compiler_params=_cparams(),
    )(proj, proj, proj, proj, conv_a, a_log, dtb)


def prep_bwd(proj, conv_a, a_log, dtb, dq, dk, dv, dgcs, dbeta):
    t = proj.shape[0]
    tm = PREP_TM
    nt = t // tm
    wq = 3 * D
    rev = lambda i: nt - 1 - i

    def body(prev_ref, cur_ref, bb_ref, aa_ref, cw_ref, al_ref, dt_ref,
             dq_ref, dk_ref, dv_ref, dg_ref, db_ref,
             dpre_ref, dbb_ref, daa_ref, dcw_ref, dal_ref, ddt_ref, carry):
        i = pl.program_id(0)
        flag = jnp.where(i < nt - 1, 1.0, 0.0)

        @pl.when(i == 0)
        def _():
            carry[...] = jnp.zeros_like(carry)
            dcw_ref[...] = jnp.zeros_like(dcw_ref)
            dal_ref[...] = jnp.zeros_like(dal_ref)
            ddt_ref[...] = jnp.zeros_like(ddt_ref)

        for part, d_ref in enumerate((dq_ref, dk_ref, dv_ref)):
            for h in range(AH):
                sl = slice(part * D + h * ADK, part * D + (h + 1) * ADK)
                ext = jnp.concatenate([prev_ref[:, sl] * flag, cur_ref[:, sl]], axis=0)
                rows = tuple(cw_ref[j:j + 1, sl] for j in range(4))
                _, vjp = jax.vjp(lambda e, r: prep_head_fn(e, r, PREP_SCALES[part]), ext, rows)
                dext, drows = vjp(d_ref[h])
                dcur = dext[HALO:]
                dpre_ref[:, sl] = jnp.concatenate([dcur[:tm - HALO], dcur[tm - HALO:] + carry[:, sl]],
                                                  axis=0).astype(BF16)
                carry[:, sl] = dext[:HALO]
                dcw_ref[:, sl] += _stack_rows(drows)
        _, vjp = jax.vjp(prep_gate_fn, bb_ref[...], aa_ref[...], al_ref[...], dt_ref[...])
        dbb, daa, dal, ddt = vjp((dg_ref[...], db_ref[...]))
        dbb_ref[...] = dbb.astype(BF16)
        daa_ref[...] = daa.astype(BF16)
        dal_ref[...] += dal
        ddt_ref[...] += ddt

    hm = pl.BlockSpec((AH, tm, ADK), lambda i: (0, rev(i), 0))
    return pl.pallas_call(
        body, name="prep_bwd", grid=(nt,),
        in_specs=_halo_specs(tm, wq, 0, rev) + [
            _rows(tm, 128, C_BA // 128, rev), _rows(tm, 128, C_BA // 128 + 1, rev),
            _const((4, wq)), _const((1, 128)), _const((1, 128)),
            hm, hm, hm, _rows(tm, 128, 0, rev), _rows(tm, 128, 0, rev)],
        out_specs=[_rows(tm, wq, 0, rev), _rows(tm, 128, 0, rev), _rows(tm, 128, 0, rev),
                   _const((4, wq)), _const((1, 128)), _const((1, 128))],
        out_shape=[jax.ShapeDtypeStruct((t, wq), BF16), jax.ShapeDtypeStruct((t, 128), BF16),
                   jax.ShapeDtypeStruct((t, 128), BF16), jax.ShapeDtypeStruct((4, wq), F32),
                   jax.ShapeDtypeStruct((1, 128), F32), jax.ShapeDtypeStruct((1, 128), F32)],
        scratch_shapes=[pltpu.VMEM((HALO, wq), F32)],
        compiler_params=_cparams(),
    )(proj, proj, proj, proj, conv_a, a_log, dtb, dq, dk, dv, dgcs, dbeta)


def _c1_specs(order):
    hm = pl.BlockSpec((AH, CH, ADK), lambda n: (0, order(n), 0))
    col = pl.BlockSpec((CH, LANE), lambda n: (order(n), 0))
    qk = pl.BlockSpec((1, AH, CH, CH), lambda n: (order(n), 0, 0, 0))
    eg = pl.BlockSpec((1, AH, 1, ADK), lambda n: (order(n), 0, 0, 0))
    return hm, col, qk, eg


def _heads(ref):
    return jnp.stack([ref[:, h * ADK:(h + 1) * ADK] for h in range(AH)], axis=0)


def c1_fwd(q, k, v, gcs, beta):
    t = q.shape[1]
    nc = t // CH
    hm, col, qks, egs = _c1_specs(lambda n: n)

    def body(q_ref, k_ref, v_ref, g_ref, b_ref, u_ref, w_ref, qg_ref, kd_ref, qk_ref, eg_ref, ti_ref):
        u, w, qk, qg, kd, eg, tinv = c1_heads(q_ref[...], k_ref[...], v_ref[...], g_ref[...], b_ref[...])
        u_ref[...] = u
        w_ref[...] = w.astype(BF16)
        qg_ref[...] = qg.astype(BF16)
        kd_ref[...] = kd.astype(BF16)
        qk_ref[0] = qk.astype(BF16)
        eg_ref[0] = eg
        ti_ref[0] = tinv

    return pl.pallas_call(
        body, name="c1_fwd", grid=(nc,),
        in_specs=[hm, hm, hm, col, col],
        out_specs=[hm, hm, hm, hm, qks, egs, qks],
        out_shape=[jax.ShapeDtypeStruct((AH, t, ADK), F32)] + [jax.ShapeDtypeStruct((AH, t, ADK), BF16)] * 3 + [
            jax.ShapeDtypeStruct((nc, AH, CH, CH), BF16), jax.ShapeDtypeStruct((nc, AH, 1, ADK), F32),
            jax.ShapeDtypeStruct((nc, AH, CH, CH), F32)],
        compiler_params=_cparams(),
    )(q, k, v, gcs, beta)


def c1_bwd(q, k, v, gcs, beta, tinv, du, dw, dqg, dkd, dqk, deg):
    t = q.shape[1]
    nc = t // CH
    hm, col, qks, egs = _c1_specs(lambda n: n)

    def body(q_ref, k_ref, v_ref, g_ref, b_ref, ti_ref, du_ref, dw_ref, dqg_ref, dkd_ref, dqk_ref, deg_ref,
             dq_ref, dk_ref, dv_ref, dg_ref, db_ref):
        _, vjp = jax.vjp(lambda q_, k_, v_, g_, b_: c1_heads(q_, k_, v_, g_, b_, ti_ref[0]),
                         q_ref[...], k_ref[...], v_ref[...], g_ref[...], b_ref[...])
        dq, dk, dv, dg, db = vjp((du_ref[...], dw_ref[...], dqk_ref[0], dqg_ref[...], dkd_ref[...], deg_ref[0],
                                  jnp.zeros((AH, CH, CH), F32)))
        dq_ref[...] = dq
        dk_ref[...] = dk
        dv_ref[...] = dv
        dg_ref[...] = dg
        db_ref[...] = db

    return pl.pallas_call(
        body, name="c1_bwd", grid=(nc,),
        in_specs=[hm, hm, hm, col, col, qks, hm, hm, hm, hm, qks, egs],
        out_specs=[hm, hm, hm, col, col],
        out_shape=[jax.ShapeDtypeStruct((AH, t, ADK), F32)] * 3 + [jax.ShapeDtypeStruct((t, LANE), F32)] * 2,
        compiler_params=_cparams(),
    )(q, k, v, gcs, beta, tinv, du, dw, dqg, dkd, dqk, deg)


def c2_fwd(u, w, qg, kd, qk, eg, proj, norm_a):
    t = u.shape[1]
    nc = t // CH
    hm, _, qks, egs = _c1_specs(lambda n: n)
    tok = pl.BlockSpec((CH, D), lambda n: (n, 0))
    zspec = pl.BlockSpec((CH, D), lambda n: (n, C_Z // D))
    sspec = pl.BlockSpec((1, AH, ADK, ADK), lambda n: (n, 0, 0, 0))

    def body(u_ref, w_ref, qg_ref, kd_ref, qk_ref, eg_ref, z_ref, nw_ref, o_ref, sall_ref, st):
        n = pl.program_id(0)

        @pl.when(n == 0)
        def _():
            st[...] = jnp.zeros_like(st)

        s = st[...]
        sall_ref[0] = s
        og, s2 = c2_heads(s, u_ref[...], w_ref[...], qk_ref[0], qg_ref[...], kd_ref[...], eg_ref[0],
                          _heads(z_ref), nw_ref[...])
        st[...] = s2
        for h in range(AH):
            o_ref[:, h * ADK:(h + 1) * ADK] = og[h].astype(BF16)

    return pl.pallas_call(
        body, name="c2_fwd", grid=(nc,),
        in_specs=[hm, hm, hm, hm, qks, egs, zspec, _const((1, ADK))],
        out_specs=[tok, sspec],
        out_shape=[jax.ShapeDtypeStruct((t, D), BF16), jax.ShapeDtypeStruct((nc, AH, ADK, ADK), F32)],
        scratch_shapes=[pltpu.VMEM((AH, ADK, ADK), F32)],
        compiler_params=_cparams(),
    )(u, w, qg, kd, qk, eg, proj, norm_a)


def c2_bwd(u, w, qg, kd, qk, eg, proj, norm_a, sall, do):
    t = u.shape[1]
    nc = t // CH
    rev = lambda n: nc - 1 - n
    hm, _, qks, egs = _c1_specs(rev)
    tok = pl.BlockSpec((CH, D), lambda n: (rev(n), 0))
    zspec = pl.BlockSpec((CH, D), lambda n: (rev(n), C_Z // D))
    sspec = pl.BlockSpec((1, AH, ADK, ADK), lambda n: (rev(n), 0, 0, 0))

    def body(u_ref, w_ref, qg_ref, kd_ref, qk_ref, eg_ref, z_ref, nw_ref, sall_ref, do_ref,
             du_ref, dw_ref, dqg_ref, dkd_ref, dqk_ref, deg_ref, dz_ref, dnw_ref, dst):
        n = pl.program_id(0)

        @pl.when(n == 0)
        def _():
            dst[...] = jnp.zeros_like(dst)
            dnw_ref[...] = jnp.zeros_like(dnw_ref)

        _, vjp = jax.vjp(c2_heads, sall_ref[0], u_ref[...], w_ref[...].astype(F32), qk_ref[0].astype(F32),
                         qg_ref[...].astype(F32), kd_ref[...].astype(F32), eg_ref[0], _heads(z_ref), nw_ref[...])
        ds, du, dw, dqk, dqg, dkd, deg, dz, dn = vjp((_heads(do_ref), dst[...]))
        dst[...] = ds
        du_ref[...] = du
        dw_ref[...] = dw
        dqg_ref[...] = dqg
        dkd_ref[...] = dkd
        dqk_ref[0] = dqk
        deg_ref[0] = deg
        for h in range(AH):
            dz_ref[:, h * ADK:(h + 1) * ADK] = dz[h].astype(BF16)
        dnw_ref[...] += dn

    return pl.pallas_call(
        body, name="c2_bwd", grid=(nc,),
        in_specs=[hm, hm, hm, hm, qks, egs, zspec, _const((1, ADK)), sspec, tok],
        out_specs=[hm, hm, hm, hm, qks, egs, tok, _const((1, ADK))],
        out_shape=[jax.ShapeDtypeStruct((AH, t, ADK), F32)] * 4 + [
            jax.ShapeDtypeStruct((nc, AH, CH, CH), F32), jax.ShapeDtypeStruct((nc, AH, 1, ADK), F32),
            jax.ShapeDtypeStruct((t, D), BF16), jax.ShapeDtypeStruct((1, ADK), F32)],
        scratch_shapes=[pltpu.VMEM((AH, ADK, ADK), F32)],
        compiler_params=_cparams(),
    )(u, w, qg, kd, qk, eg, proj, norm_a, sall, do)


NQB = TQ // CH
NKB = 2 * TQ // CH
NDIST = BPREV + 1
KLO = -(NQB - 2)
NPAIR = NKB - 1 - KLO + 1


def bias_table(rel_bias):
    nh = rel_bias.shape[0]
    relx = jnp.concatenate([rel_bias, jnp.broadcast_to(rel_bias[:, -1:], (nh, CH * BPREV + 2 * CH - 1 - RELSZ))],
                           axis=1)
    t = jnp.stack([relx[:, CH * k:CH * k + 2 * CH - 1] for k in range(NDIST)], axis=1)
    trev = t[:, :, ::-1]
    g2 = jnp.concatenate([trev[:, :, CH - 1:], jnp.zeros((nh, NDIST, 1), F32), trev[:, :, :CH - 1]], axis=2)
    flat = jnp.tile(g2, (1, 1, CH + 1))[:, :, :CH * (2 * CH - 1)]
    blk = flat.reshape(nh, NDIST, CH, 2 * CH - 1)[..., :CH]
    neg = jnp.full((nh, NQB - 1, CH, CH), NEG, F32)
    asc = jnp.concatenate([neg, blk, neg], axis=1)
    return jnp.concatenate([asc[:, 1:], asc[:, :-1]], axis=-1)


SUBQ = 4 * CH
NSUB = TQ // SUBQ
KWIN = SUBQ + BPREV * CH


def assemble_bias(tab, r):
    b0 = r * SUBQ // (2 * CH)
    rows = [jnp.concatenate([tab[NQB + a - 2 * b - KLO] for b in range(b0, b0 + KWIN // (2 * CH))], axis=1)
            for a in range(r * SUBQ // CH, (r + 1) * SUBQ // CH)]
    return jnp.concatenate(rows, axis=0)


def bias_table_bwd_layout(dtab):
    nh = dtab.shape[0]
    dasc = (jnp.pad(dtab[..., :CH], ((0, 0), (1, 0), (0, 0), (0, 0)))
            + jnp.pad(dtab[..., CH:], ((0, 0), (0, 1), (0, 0), (0, 0))))
    dblk = dasc[:, NQB - 1:NQB - 1 + NDIST]
    dr = jnp.pad(dblk, ((0, 0), (0, 0), (0, 0), (0, CH - 1)))
    flat = jnp.pad(dr.reshape(nh, NDIST, CH * (2 * CH - 1)), ((0, 0), (0, 0), (0, 3 * CH)))
    return flat.reshape(nh, NDIST, CH + 1, 2 * CH).transpose(0, 2, 1, 3).reshape(nh, CH + 1, NDIST * 2 * CH)


def _fold_matrix_np():
    f = np.zeros((NDIST * 2 * CH, 384), np.float32)
    for k in range(NDIST):
        s = k
        for xx in range(2 * CH):
            if xx == CH:
                continue
            m = CH - 1 - xx if xx < CH else 3 * CH - 1 - xx
            f[s * 2 * CH + xx, min(CH * k + m, RELSZ - 1)] = 1.0
    return f


def relbias_reduce(dlay):
    nh, rows, cols = dlay.shape
    rpad = (-rows) % 8
    dlay = jnp.pad(dlay, ((0, 0), (0, rpad), (0, 0)))
    fold = jnp.asarray(_fold_matrix_np())

    def body(d_ref, f_ref, o_ref):
        cs = jnp.sum(d_ref[0], axis=0, keepdims=True)
        o_ref[0] = _mmh(jnp.broadcast_to(cs, (8, cols)), f_ref[...])

    out = pl.pallas_call(
        body, name="relbias_reduce", grid=(nh,),
        in_specs=[pl.BlockSpec((1, rows + rpad, cols), lambda h: (h, 0, 0)), _const((cols, 384))],
        out_specs=pl.BlockSpec((1, 8, 384), lambda h: (h, 0, 0)),
        out_shape=jax.ShapeDtypeStruct((nh, 8, 384), F32),
        compiler_params=_cparams(),
    )(dlay, fold)
    return out[:, 0, :RELSZ]


def attn_fwd(proj, bias):
    t = proj.shape[0]
    nt = t // TQ
    cb = C_QKVB // 128

    def body(q_ref, kp_ref, kc_ref, vp_ref, vc_ref, b_ref, o_ref, p_ref):
        i = pl.program_id(1)
        firstf = jnp.where(i == 0, 1.0, 0.0)
        for r in range(NSUB):
            lo, hi = r * SUBQ, r * SUBQ + KWIN - TQ
            kw = jnp.concatenate([kp_ref[lo:, :], kc_ref[:hi, :]], axis=0)
            vw = jnp.concatenate([vp_ref[lo:, :], vc_ref[:hi, :]], axis=0)
            out, probs = attn_sub_fwd(q_ref[lo:lo + SUBQ, :].astype(F32), kw, vw, b_ref[...], r, firstf)
            o_ref[lo:lo + SUBQ, :] = out.astype(BF16)
            for hh in range(2):
                p_ref[hh, lo:lo + SUBQ, :] = probs[hh].astype(BF16)

    def blk(off, prev):
        if prev:
            return pl.BlockSpec((TQ, 128), lambda p, i: (jnp.maximum(i - 1, 0), cb + off + p))
        return pl.BlockSpec((TQ, 128), lambda p, i: (i, cb + off + p))

    return pl.pallas_call(
        body, name="attn_fwd", grid=(BH // 2, nt),
        in_specs=[blk(0, False), blk(8, True), blk(8, False), blk(16, True), blk(16, False),
                  pl.BlockSpec((2, NPAIR, CH, 2 * CH), lambda p, i: (p, 0, 0, 0))],
        out_specs=[pl.BlockSpec((TQ, 128), lambda p, i: (i, p)),
                   pl.BlockSpec((2, TQ, KWIN), lambda p, i: (p, i, 0))],
        out_shape=[jax.ShapeDtypeStruct((t, D), BF16), jax.ShapeDtypeStruct((BH, t, KWIN), BF16)],
        compiler_params=_cparams(2),
    )(proj, proj, proj, proj, proj, bias)


def attn_bwd(proj, ob, probs, do):
    t = proj.shape[0]
    nt = t // TQ
    cb = C_QKVB // 128

    def body(q_ref, kp_ref, kc_ref, vp_ref, vc_ref, o_ref, p_ref, do_ref,
             dq_ref, dk_ref, dv_ref, db_ref, ck, cv, ak, av):
        i = pl.program_id(1)

        @pl.when(i == 0)
        def _():
            ck[...] = jnp.zeros_like(ck)
            cv[...] = jnp.zeros_like(cv)
            db_ref[...] = jnp.zeros_like(db_ref)

        @pl.when(i < nt)
        def _():
            ak[...] = jnp.zeros_like(ak)
            av[...] = jnp.zeros_like(av)
            for r in range(NSUB):
                lo, hi = r * SUBQ, r * SUBQ + KWIN - TQ
                rows = slice(lo, lo + SUBQ)
                kw = jnp.concatenate([kp_ref[lo:, :], kc_ref[:hi, :]], axis=0)
                vw = jnp.concatenate([vp_ref[lo:, :], vc_ref[:hi, :]], axis=0)
                probs_r = [p_ref[hh, rows, :].astype(F32) for hh in range(2)]
                dq, dkw, dvw, dss = attn_sub_bwd(q_ref[rows, :].astype(F32), kw, vw, o_ref[rows, :].astype(F32),
                                                 do_ref[rows, :], probs_r, r)
                dq_ref[rows, :] = dq.astype(BF16)
                ak[lo:lo + KWIN, :] += dkw
                av[lo:lo + KWIN, :] += dvw
                for hh in range(2):
                    _, scatter = jax.vjp(lambda tab: assemble_bias(tab, r), jnp.zeros((NPAIR, CH, 2 * CH), F32))
                    db_ref[hh] += scatter(dss[hh])[0]
            dk_ref[...] = (ck[...] + ak[:TQ, :]).astype(BF16)
            dv_ref[...] = (cv[...] + av[:TQ, :]).astype(BF16)
            ck[...] = ak[TQ:, :]
            cv[...] = av[TQ:, :]

        @pl.when(i == nt)
        def _():
            dk_ref[...] = ck[...].astype(BF16)
            dv_ref[...] = cv[...].astype(BF16)

    def blk(off, prev):
        if prev:
            return pl.BlockSpec((TQ, 128), lambda p, i: (jnp.clip(i - 1, 0, nt - 1), cb + off + p))
        return pl.BlockSpec((TQ, 128), lambda p, i: (jnp.minimum(i, nt - 1), cb + off + p))

    own = pl.BlockSpec((TQ, 128), lambda p, i: (jnp.minimum(i, nt - 1), p))
    lag = pl.BlockSpec((TQ, 128), lambda p, i: (jnp.maximum(i - 1, 0), p))
    return pl.pallas_call(
        body, name="attn_bwd", grid=(BH // 2, nt + 1),
        in_specs=[blk(0, False), blk(8, True), blk(8, False), blk(16, True), blk(16, False), own,
                  pl.BlockSpec((2, TQ, KWIN), lambda p, i: (p, jnp.minimum(i, nt - 1), 0)), own],
        out_specs=[own, lag, lag, pl.BlockSpec((2, NPAIR, CH, 2 * CH), lambda p, i: (p, 0, 0, 0))],
        out_shape=[jax.ShapeDtypeStruct((t, D), BF16)] * 3 + [jax.ShapeDtypeStruct((BH, NPAIR, CH, 2 * CH), F32)],
        scratch_shapes=[pltpu.VMEM((TQ, 128), F32), pltpu.VMEM((TQ, 128), F32),
                        pltpu.VMEM((2 * TQ, 128), F32), pltpu.VMEM((2 * TQ, 128), F32)],
        compiler_params=_cparams(2),
    )(proj, proj, proj, proj, proj, ob, probs, do)


MERGE_TM = 256


def merge_fwd(x, oa, ob, proj, vecs, wa, wb, wo):
    t = x.shape[0]
    tm = MERGE_TM
    names = ("bga", "bgb", "gate_t", "g1", "b1", "scale_f", "shift_f")

    def body(x_ref, oa_ref, ob_ref, gra_ref, grb_ref, *rest):
        vrefs = rest[:7]
        wa_ref, wb_ref, wo_ref, y_ref, h_ref = rest[7:]
        vv = [r[...] for r in vrefs]
        zero = jnp.zeros((tm, D), F32)
        y1, _ = merge_fn(x_ref[...], oa_ref[...], ob_ref[...], gra_ref[...], grb_ref[...], zero, zero, zero,
                         *vv, wa_ref[...], wb_ref[...], wo_ref[...])
        y_ref[...] = y1
        h_ref[...] = (y1 * (1.0 + vv[5]) + vv[6]).astype(BF16)

    return pl.pallas_call(
        body, name="merge_fwd", grid=(t // tm,),
        in_specs=[_rows(tm, D), _rows(tm, D), _rows(tm, D), _rows(tm, D, C_GATE // D), _rows(tm, D, C_GATE // D + 1)]
        + [_const((1, D))] * 7 + [_const((D, D))] * 3,
        out_specs=[_rows(tm, D), _rows(tm, D)],
        out_shape=[jax.ShapeDtypeStruct((t, D), F32), jax.ShapeDtypeStruct((t, D), BF16)],
        compiler_params=_cparams(),
    )(x, oa, ob, proj, proj, *[vecs[n] for n in names], wa, wb, wo)


def merge_bwd(x, oa, ob, proj, vecs, wa, wb, wo, dy1):
    t = x.shape[0]
    tm = MERGE_TM
    names = ("bga", "bgb", "gate_t", "g1", "b1", "scale_f", "shift_f")

    def body(x_ref, oa_ref, ob_ref, gra_ref, grb_ref, *rest):
        vrefs = rest[:7]
        wa_ref, wb_ref, wo_ref, dy_ref = rest[7:11]
        (dx_ref, doa_ref, dob_ref, dga_ref, dgb_ref, mg_ref, dmix_ref, dpa_ref, dpb_ref,
         dbga_ref, dbgb_ref, dgt_ref, dg1_ref, db1_ref) = rest[11:]
        i = pl.program_id(0)
        vv = [r[...] for r in vrefs]
        zero = jnp.zeros((tm, D), F32)

        def f(x_, oa_, ob_, gra_, grb_, ppa, ppb, pmix, bga, bgb, gate_t, g1, b1):
            return merge_fn(x_, oa_, ob_, gra_, grb_, ppa, ppb, pmix, bga, bgb, gate_t, g1, b1, vv[5], vv[6],
                            wa_ref[...], wb_ref[...], wo_ref[...])

        _, vjp, merged = jax.vjp(f, x_ref[...], oa_ref[...].astype(F32), ob_ref[...].astype(F32),
                                 gra_ref[...], grb_ref[...], zero, zero, zero, *vv[:5], has_aux=True)
        dx, doa, dob, dga, dgb, dpa, dpb, dmix, dbga, dbgb, dgt, dg1, db1 = vjp(dy_ref[...])
        dx_ref[...] = dx
        doa_ref[...] = doa
        dob_ref[...] = dob
        dga_ref[...] = dga.astype(BF16)
        dgb_ref[...] = dgb.astype(BF16)
        mg_ref[...] = merged.astype(BF16)
        dmix_ref[...] = dmix.astype(BF16)
        dpa_ref[...] = dpa.astype(BF16)
        dpb_ref[...] = dpb.astype(BF16)
        accs = (dbga_ref, dbgb_ref, dgt_ref, dg1_ref, db1_ref)

        @pl.when(i == 0)
        def _():
            for a in accs:
                a[...] = jnp.zeros_like(a)

        for a, val in zip(accs, (dbga, dbgb, dgt, dg1, db1)):
            a[...] += val

    return pl.pallas_call(
        body, name="merge_bwd", grid=(t // tm,),
        in_specs=[_rows(tm, D), _rows(tm, D), _rows(tm, D), _rows(tm, D, C_GATE // D), _rows(tm, D, C_GATE // D + 1)]
        + [_const((1, D))] * 7 + [_const((D, D))] * 3 + [_rows(tm, D)],
        out_specs=[_rows(tm, D)] * 9 + [_const((1, D))] * 5,
        out_shape=[jax.ShapeDtypeStruct((t, D), F32)] * 3 + [jax.ShapeDtypeStruct((t, D), BF16)] * 6
        + [jax.ShapeDtypeStruct((1, D), F32)] * 5,
        compiler_params=_cparams(),
    )(x, oa, ob, proj, proj, *[vecs[n] for n in names], wa, wb, wo, dy1)


FFN_TM = 128


def ffn_act_fwd(up, conv_w, bconv):
    t, wdt = up.shape
    tm = FFN_TM

    def body(prev_ref, cur_ref, cw_ref, bc_ref, a_ref):
        i = pl.program_id(0)
        flag = jnp.where(i > 0, 1.0, 0.0)

        def ext(sl):
            return jnp.concatenate([prev_ref[:, sl] * flag, cur_ref[:, sl]], axis=0)

        def rows(sl):
            return tuple(cw_ref[j:j + 1, sl] for j in range(3))

        for cb in range(DFF // LANE):
            g = slice(cb * LANE, (cb + 1) * LANE)
            v = slice(DFF + cb * LANE, DFF + (cb + 1) * LANE)
            a_ref[:, g] = ffn_act_fn(ext(g), ext(v), rows(g), rows(v), bc_ref[:, g], bc_ref[:, v]).astype(BF16)

    return pl.pallas_call(
        body, name="ffn_act_fwd", grid=(t // tm,),
        in_specs=_halo_specs(tm, wdt, 0, lambda i: i) + [_const((3, wdt)), _const((1, wdt))],
        out_specs=_rows(tm, DFF),
        out_shape=jax.ShapeDtypeStruct((t, DFF), BF16),
        compiler_params=_cparams(),
    )(up, up, conv_w, bconv)


def ffn_act_bwd(up, conv_w, bconv, da):
    t, wdt = up.shape
    tm = FFN_TM
    nt = t // tm
    rev = lambda i: nt - 1 - i

    def body(prev_ref, cur_ref, cw_ref, bc_ref, da_ref, dup_ref, dcw_ref, dbc_ref, carry):
        i = pl.program_id(0)
        flag = jnp.where(i < nt - 1, 1.0, 0.0)

        @pl.when(i == 0)
        def _():
            carry[...] = jnp.zeros_like(carry)
            dcw_ref[...] = jnp.zeros_like(dcw_ref)
            dbc_ref[...] = jnp.zeros_like(dbc_ref)

        def ext(sl):
            return jnp.concatenate([prev_ref[:, sl] * flag, cur_ref[:, sl]], axis=0)

        def rows(sl):
            return tuple(cw_ref[j:j + 1, sl] for j in range(3))

        def emit(sl, dext, drows, dbc):
            dcur = dext[HALO:]
            dup_ref[:, sl] = jnp.concatenate([dcur[:tm - HALO], dcur[tm - HALO:] + carry[:, sl]], axis=0).astype(BF16)
            carry[:, sl] = dext[:HALO]
            dcw_ref[:, sl] += _stack_rows(drows)
            dbc_ref[:, sl] += dbc

        for cb in range(DFF // LANE):
            g = slice(cb * LANE, (cb + 1) * LANE)
            v = slice(DFF + cb * LANE, DFF + (cb + 1) * LANE)
            _, vjp = jax.vjp(ffn_act_fn, ext(g), ext(v), rows(g), rows(v), bc_ref[:, g], bc_ref[:, v])
            dxg, dxv, drg, drv, dbg, dbv = vjp(da_ref[:, g])
            emit(g, dxg, drg, dbg)
            emit(v, dxv, drv, dbv)

    return pl.pallas_call(
        body, name="ffn_act_bwd", grid=(nt,),
        in_specs=_halo_specs(tm, wdt, 0, rev) + [_const((3, wdt)), _const((1, wdt)), _rows(tm, DFF, 0, rev)],
        out_specs=[_rows(tm, wdt, 0, rev), _const((3, wdt)), _const((1, wdt))],
        out_shape=[jax.ShapeDtypeStruct((t, wdt), BF16), jax.ShapeDtypeStruct((3, wdt), F32),
                   jax.ShapeDtypeStruct((1, wdt), F32)],
        scratch_shapes=[pltpu.VMEM((HALO, wdt), F32)],
        compiler_params=_cparams(),
    )(up, up, conv_w, bconv, da)


HEAD_TM = 256


def head_fwd_bwd(a, y1, tgt, gate_f, g2, b2, wd):
    t = a.shape[0]
    tm = HEAD_TM

    def body(a_ref, y_ref, t_ref, gf_ref, g2_ref, b2_ref, wd_ref,
             da_ref, dy_ref, dffn_ref, dgf_ref, dg2_ref, db2_ref, loss_ref):
        i = pl.program_id(0)
        zero = jnp.zeros((tm, D), F32)

        def f(a_, y_, pf, gf, g2_, b2_):
            return head_fn(a_, y_, pf, gf, g2_, b2_, t_ref[...], wd_ref[...])

        loss, vjp = jax.vjp(f, a_ref[...].astype(F32), y_ref[...], zero, gf_ref[...], g2_ref[...], b2_ref[...])
        da, dy, dffn, dgf, dg2, db2 = vjp(jnp.ones((), F32))
        da_ref[...] = da
        dy_ref[...] = dy
        dffn_ref[...] = dffn.astype(BF16)
        accs = (dgf_ref, dg2_ref, db2_ref, loss_ref)

        @pl.when(i == 0)
        def _():
            for r in accs:
                r[...] = jnp.zeros_like(r)

        dgf_ref[...] += dgf
        dg2_ref[...] += dg2
        db2_ref[...] += db2
        loss_ref[...] += loss * jnp.ones((1, 128), F32)

    return pl.pallas_call(
        body, name="head_fwd_bwd", grid=(t // tm,),
        in_specs=[_rows(tm, DFF), _rows(tm, D), _rows(tm, D), _const((1, D)), _const((1, D)), _const((1, D)),
                  _const((DFF, D))],
        out_specs=[_rows(tm, DFF), _rows(tm, D), _rows(tm, D), _const((1, D)), _const((1, D)), _const((1, D)),
                   _const((1, 128))],
        out_shape=[jax.ShapeDtypeStruct((t, DFF), F32), jax.ShapeDtypeStruct((t, D), F32),
                   jax.ShapeDtypeStruct((t, D), BF16)] + [jax.ShapeDtypeStruct((1, D), F32)] * 3
        + [jax.ShapeDtypeStruct((1, 128), F32)],
        compiler_params=_cparams(),
    )(a, y1, tgt, gate_f, g2, b2, wd)


def ada_fwd(c_all, w_sh, b_sh):
    def body(c_ref, w_ref, b_ref, o_ref):
        o_ref[...] = _mmh(_silu(c_ref[...]), w_ref[...]) + b_ref[...]

    n = w_sh.shape[1]
    return pl.pallas_call(
        body, name="ada_fwd", out_shape=jax.ShapeDtypeStruct((NDEV, n), F32),
        in_specs=[pl.BlockSpec(memory_space=pltpu.VMEM)] * 3,
        out_specs=pl.BlockSpec(memory_space=pltpu.VMEM),
        compiler_params=pltpu.CompilerParams(vmem_limit_bytes=VMEM_LIMIT),
    )(c_all, w_sh, b_sh)


def ada_wgrad(c_all_t, dmod_sh):
    def body(c_ref, d_ref, o_ref):
        o_ref[...] = _mmh(_silu(c_ref[...]), d_ref[...])

    return pl.pallas_call(
        body, name="ada_wgrad", out_shape=jax.ShapeDtypeStruct((c_all_t.shape[0], dmod_sh.shape[1]), F32),
        in_specs=[pl.BlockSpec(memory_space=pltpu.VMEM)] * 2,
        out_specs=pl.BlockSpec(memory_space=pltpu.VMEM),
        compiler_params=pltpu.CompilerParams(vmem_limit_bytes=VMEM_LIMIT),
    )(c_all_t, dmod_sh)


def adamw(gparts, w, m, v, name):
    p, r, c = gparts.shape
    tr = r if r <= 256 else _pick(r, (256, 128, 64, 32, 16, 8))
    c1 = 1.0 - B1 ** STEP
    c2 = 1.0 - B2 ** STEP

    def body(g_ref, w_ref, m_ref, v_ref, go_ref, d_ref, mo_ref, vo_ref):
        g = g_ref[0].astype(F32)
        for s in range(1, p):
            g = g + g_ref[s].astype(F32)
        mn = B1 * m_ref[0] + (1.0 - B1) * g
        vn = B2 * v_ref[0] + (1.0 - B2) * (g * g)
        go_ref[0] = g
        d_ref[0] = -LR * ((mn / c1) / (jnp.sqrt(vn / c2) + AEPS) + WD * w_ref[0])
        mo_ref[0] = mn
        vo_ref[0] = vn

    spec = pl.BlockSpec((1, tr, c), lambda i: (0, i, 0))
    return pl.pallas_call(
        body, name=name, grid=(r // tr,),
        in_specs=[pl.BlockSpec((p, tr, c), lambda i: (0, i, 0)), spec, spec, spec],
        out_specs=[spec] * 4,
        out_shape=[jax.ShapeDtypeStruct((1, r, c), F32)] * 4,
        compiler_params=_cparams(),
    )(gparts, w, m, v)


def _me():
    x, y, c = lax.axis_index("x"), lax.axis_index("y"), lax.axis_index("c")
    return x, y, c, 4 * x + 2 * y + c


def _peer(x, y, c, d):
    px = 1 - x if (d >> 2) & 1 else x
    py = 1 - y if (d >> 1) & 1 else y
    pc = 1 - c if d & 1 else c
    return (px, py, pc), 4 * px + 2 * py + pc


def _exchange(arrs, name, scatter):
    n = len(arrs)

    def body(*refs):
        ins, outs = refs[:n], refs[n:2 * n]
        send, recv, lsem = refs[2 * n:]
        x, y, c, me = _me()
        remote, local = [], []
        for k in range(n):
            src = ins[k].at[me] if scatter else ins[k]
            cp = pltpu.make_async_copy(src, outs[k].at[me], lsem.at[k])
            cp.start()
            local.append(cp)
            for d in range(1, NDEV):
                dev, pid = _peer(x, y, c, d)
                src = ins[k].at[pid] if scatter else ins[k]
                cp = pltpu.make_async_remote_copy(src_ref=src, dst_ref=outs[k].at[me],
                                                  send_sem=send.at[k, d - 1], recv_sem=recv.at[k, d - 1],
                                                  device_id=dev, device_id_type=pl.DeviceIdType.MESH)
                cp.start()
                remote.append(cp)
        for cp in remote:
            cp.wait()
        for cp in local:
            cp.wait()

    shapes = [a.shape if scatter else (NDEV,) + a.shape for a in arrs]
    return pl.pallas_call(
        body, name=name,
        in_specs=[pl.BlockSpec(memory_space=pl.ANY)] * n,
        out_specs=[pl.BlockSpec(memory_space=pl.ANY)] * n,
        out_shape=[jax.ShapeDtypeStruct(s, a.dtype) for s, a in zip(shapes, arrs)],
        scratch_shapes=[pltpu.SemaphoreType.DMA((n, NDEV - 1)), pltpu.SemaphoreType.DMA((n, NDEV - 1)),
                        pltpu.SemaphoreType.DMA((n,))],
        compiler_params=pltpu.CompilerParams(has_side_effects=True),
    )(*arrs)


def all_gather(arrs, name):
    return _exchange(arrs, name, False)


def all_gather_two_level(shard, name):
    def body(x_ref, out_ref, send, recv, lsem):
        x, y, c, _ = _me()
        sibling = (x, y, 1 - c)
        chips = [(1 - x, y), (x, 1 - y), (1 - x, 1 - y)]

        def slot(px, py, pc):
            return out_ref.at[4 * px + 2 * py + pc]

        def copy(k, block, to, src=None):
            return pltpu.make_async_remote_copy(
                src_ref=slot(*block) if src is None else src, dst_ref=slot(*block),
                send_sem=send.at[k], recv_sem=recv.at[k], device_id=to, device_id_type=pl.DeviceIdType.MESH)

        mine = pltpu.make_async_copy(x_ref, slot(x, y, c), lsem)
        mine.start()
        first = [copy(0, (x, y, c), sibling, src=x_ref)]
        first += [copy(1 + j, (x, y, c), (*chip, c), src=x_ref) for j, chip in enumerate(chips)]
        for cp in first:
            cp.start()
        passed = [copy(4 + j, (*chip, c), sibling) for j, chip in enumerate(chips)]
        for j, chip in enumerate(chips):
            copy(1 + j, (*chip, c), (x, y, c)).wait_recv()
            passed[j].start()
        copy(0, sibling, (x, y, c)).wait_recv()
        for j, chip in enumerate(chips):
            copy(4 + j, (*chip, 1 - c), (x, y, c)).wait_recv()
        for cp in first + passed:
            cp.wait_send()
        mine.wait()

    return pl.pallas_call(
        body, name=name,
        in_specs=[pl.BlockSpec(memory_space=pl.ANY)],
        out_specs=pl.BlockSpec(memory_space=pl.ANY),
        out_shape=jax.ShapeDtypeStruct((NDEV,) + shard.shape, shard.dtype),
        scratch_shapes=[pltpu.SemaphoreType.DMA((NPEER,)), pltpu.SemaphoreType.DMA((NPEER,)),
                        pltpu.SemaphoreType.DMA],
        compiler_params=pltpu.CompilerParams(has_side_effects=True),
    )(shard)


def all_to_all(arrs, name):
    return _exchange(arrs, name, True)


_HBM = pl.BlockSpec(memory_space=pltpu.HBM)
_SEM = pl.BlockSpec(memory_space=pltpu.SEMAPHORE)
_EFFECT = pltpu.SideEffectType.DATAFLOW_SIDE_EFFECTING
NPEER = NDEV - 1


def exchange_start(arrs, name, scatter):
    n = len(arrs)
    lands = [lax.empty(a.shape if scatter else (NDEV,) + a.shape, a.dtype) for a in arrs]

    def body(*refs):
        ins, lrefs = refs[:n], refs[n:2 * n]
        send, recv, token = refs[2 * n], refs[2 * n + 1], refs[-1]
        x, y, c, me = _me()
        for k in range(n):
            for d in range(1, NDEV):
                dev, pid = _peer(x, y, c, d)
                src = ins[k].at[pid] if scatter else ins[k]
                pltpu.make_async_remote_copy(src_ref=src, dst_ref=lrefs[k].at[me],
                                             send_sem=send.at[k * NPEER + d - 1], recv_sem=recv.at[k * NPEER + d - 1],
                                             device_id=dev, device_id_type=pl.DeviceIdType.MESH).start()
        token[...] = jnp.zeros_like(token)

    thru = [pltpu.HBM(a.shape, a.dtype) for a in list(arrs) + lands]
    outs = pl.pallas_call(
        body, name=name,
        out_shape=(pltpu.SemaphoreType.DMA((n * NPEER,)), pltpu.SemaphoreType.DMA((n * NPEER,)), *thru,
                   jax.ShapeDtypeStruct((8, 128), F32)),
        in_specs=[_HBM] * (2 * n),
        out_specs=(_SEM, _SEM, *([_HBM] * (2 * n)), pl.BlockSpec(memory_space=pltpu.VMEM)),
        input_output_aliases={i: 2 + i for i in range(2 * n)},
        compiler_params=pltpu.CompilerParams(has_side_effects=_EFFECT),
    )(*[pltpu.with_memory_space_constraint(a, pltpu.HBM) for a in list(arrs) + lands])
    handle = dict(send=outs[0], recv=outs[1], src=list(outs[2:2 + n]), land=list(outs[2 + n:2 + 2 * n]),
                  scatter=scatter)
    return handle, outs[-1][0, 0]


def exchange_wait(handle, after, name):
    n = len(handle["src"])
    scatter = handle["scatter"]

    def body(*refs):
        ins, lrefs = refs[:n], refs[n:2 * n]
        send, recv = refs[2 * n], refs[2 * n + 1]
        x, y, c, _ = _me()
        for k in range(n):
            for d in range(1, NDEV):
                dev, _ = _peer(x, y, c, d)
                src = ins[k].at[0] if scatter else ins[k]
                cp = pltpu.make_async_remote_copy(src_ref=src, dst_ref=lrefs[k].at[0],
                                                  send_sem=send.at[k * NPEER + d - 1],
                                                  recv_sem=recv.at[k * NPEER + d - 1],
                                                  device_id=dev, device_id_type=pl.DeviceIdType.MESH)
                cp.wait_send()
                cp.wait_recv()

    arrs = handle["src"] + handle["land"]
    outs = pl.pallas_call(
        body, name=name,
        out_shape=tuple(pltpu.HBM(a.shape, a.dtype) for a in arrs),
        in_specs=[_HBM] * (2 * n) + [_SEM, _SEM, pl.BlockSpec(memory_space=pl.ANY)],
        out_specs=tuple([_HBM] * (2 * n)),
        input_output_aliases={i: i for i in range(2 * n)},
        compiler_params=pltpu.CompilerParams(has_side_effects=_EFFECT),
    )(*arrs, handle["send"], handle["recv"], after)
    me = 4 * lax.axis_index("x") + 2 * lax.axis_index("y") + lax.axis_index("c")
    landed = []
    for own, land in zip(outs[:n], outs[n:]):
        mine = lax.dynamic_index_in_dim(own, me, 0, keepdims=True) if scatter else own[None]
        landed.append(lax.dynamic_update_slice_in_dim(land, mine, me, 0))
    return landed


def _cat_from_slabs(slabs):
    _, k, n = slabs.shape

    def cols(lo, hi):
        parts, c = [], lo
        while c < hi:
            j = c // n
            e = min(hi, (j + 1) * n)
            parts.append(slabs[j][:, c - j * n:e - j * n])
            c = e
        return parts

    def zeros(w):
        return [jnp.zeros((k, w), slabs.dtype)]

    return jnp.concatenate(cols(0, 4096) + cols(4112, 9232) + cols(4096, 4104) + zeros(LANE - AH)
                           + cols(4104, 4112) + zeros(NCAT - C_BA - LANE - AH), axis=1)


IN_PIECES = (("pre", C_QKVA, 3072), ("z", C_Z, 1024), ("qb", C_QKVB, 1024), ("kb", C_QKVB + 1024, 1024),
             ("vb", C_QKVB + 2048, 1024), ("ga", C_GATE, 1024), ("gb", C_GATE + 1024, 1024))
_ORIG_SEGS = ((0, 3072, "pre", 0), (3072, 4096, "z", 0), (4096, 4104, "ba", 0), (4104, 4112, "ba", LANE),
              (4112, 5136, "qb", 0), (5136, 6160, "kb", 0), (6160, 7184, "vb", 0), (7184, 8208, "ga", 0),
              (8208, 9232, "gb", 0))


def _orig_cols_from_pieces(gp, lo, hi):
    parts = []
    for a, b, name, off in _ORIG_SEGS:
        s, e = max(a, lo), min(b, hi)
        if s < e:
            parts.append(gp[name][:, off + s - a:off + e - a])
    return parts[0] if len(parts) == 1 else jnp.concatenate(parts, axis=1)


def _pad128(v):
    return jnp.pad(v, ((0, 0), (0, 128 - v.shape[1])))


def local_step(x, tgt, mod, wts, small, late_weights=None, on_grads=None):
    if on_grads is None:
        on_grads = lambda group, gd: jnp.zeros((), F32)
    t = x.shape[0]
    nc = t // CH
    shift_t, scale_t, gate_t, shift_f, scale_f, gate_f = mod
    wcat = _cat_from_slabs(wts["w_in_slabs"])
    a_log = _pad128(small["a_log"])
    dtb = _pad128(small["dt_bias"])
    vecs = dict(bga=small["b_gate"][:, :D], bgb=small["b_gate"][:, D:], gate_t=gate_t, g1=small["ln1_g"],
                b1=small["ln1_b"], scale_f=scale_f, shift_f=shift_f)

    h1 = modulate(x, scale_t, shift_t, "modulate_t")
    proj = matmul(h1, wcat, F32, "in_proj")
    q, k, v, gcs, beta = prep_fwd(proj, small["conv_a"], a_log, dtb)

    u, w, qg, kd, qk, eg, tinv = c1_fwd(q, k, v, gcs, beta)
    oa, sall = c2_fwd(u, w, qg, kd, qk, eg, proj, small["norm_a"])
    bias = bias_table(small["rel_bias"])
    ob, probs = attn_fwd(proj, bias)
    if late_weights is not None:
        wts = {**wts, **late_weights(ob)}
    y1, h2 = merge_fwd(x, oa, ob, proj, vecs, wts["w_a"], wts["w_b"], wts["w_o"])
    up = matmul(h2, wts["w_up"], F32, "up_proj")
    a = ffn_act_fwd(up, small["conv_ffn"], small["b_conv_ffn"])

    da, dy1_res, dffn, dgate_f, dg2, db2, loss = head_fwd_bwd(a, y1, tgt, gate_f, small["ln2_g"], small["ln2_b"],
                                                            wts["w_down"])
    g_w_down = matmul(a, dffn, F32, "wgrad_down", ta=True)
    dup, g_conv_ffn, g_bconv = ffn_act_bwd(up, small["conv_ffn"], small["b_conv_ffn"], da)
    dh2 = matmul(dup, wts["w_up"], F32, "dgrad_up", tb=True)
    g_w_up = matmul(h2, dup, F32, "wgrad_up", ta=True)
    tok = on_grads("ffn", dict(w_up=g_w_up, w_down=g_w_down))
    dy1, dscale_f, dshift_f = modulate_bwd(dh2, y1, dy1_res, scale_f + tok, "modulate_f_bwd")
    (dx_res, doa, dob, dga, dgb, merged, dmix, dpa, dpb,
     dbga, dbgb, dgate_t, dg1, db1) = merge_bwd(x, oa, ob, proj, vecs, wts["w_a"], wts["w_b"], wts["w_o"], dy1)
    g_w_o = matmul(merged, dmix, F32, "wgrad_o", ta=True)
    g_w_a = matmul(oa, dpa, F32, "wgrad_a", ta=True)
    g_w_b = matmul(ob, dpb, F32, "wgrad_b", ta=True)
    tok = on_grads("mix", dict(w_o=g_w_o, w_a=g_w_a, w_b=g_w_b))
    dqb, dkb, dvb, dbias = attn_bwd(proj, ob, probs, dob)
    g_rel = relbias_reduce(bias_table_bwd_layout(dbias))
    du, dw, dqg, dkd, dqk, deg, dz, g_norm = c2_bwd(u, w, qg, kd, qk, eg, proj, small["norm_a"] + tok, sall, doa)
    dq, dk, dv, dgcs, dbeta = c1_bwd(q, k, v, gcs, beta, tinv, du, dw, dqg, dkd, dqk, deg)
    dpre, dbb, daa, g_conv_a, g_alog, g_dtb = prep_bwd(proj, small["conv_a"], a_log, dtb, dq, dk, dv, dgcs, dbeta)
    tok = on_grads("small", dict(conv_a=g_conv_a, rel_bias=g_rel, conv_ffn=g_conv_ffn))
    dba = jnp.concatenate([dbb, daa, jnp.zeros((t, NCAT - C_BA - 2 * LANE), BF16)], axis=1) + tok.astype(BF16)
    dpieces = dict(pre=dpre, z=dz, qb=dqb, kb=dkb, vb=dvb, ga=dga, gb=dgb)
    g_in = {n: matmul(h1, dpieces[n], F32, "wgrad_in_" + n, ta=True) for n, _, _ in IN_PIECES}
    g_in["ba"] = matmul(h1, dba, F32, "wgrad_in_ba", ta=True)
    tok = on_grads("in", g_in)
    dh1 = dgrad_pieces([(dpieces[n], off) for n, off, _ in IN_PIECES], dba, wcat + tok.astype(BF16),
                       "dgrad_in")
    grad_x, dscale_t, dshift_t = modulate_bwd(dh1, x, dx_res, scale_t + tok, "modulate_t_bwd")

    dmod = (dshift_t, dscale_t, dgate_t, dshift_f, dscale_f, dgate_f)
    grads = dict(w_in=_orig_cols_from_pieces(g_in, 0, 9232), w_up=g_w_up, w_down=g_w_down, w_a=g_w_a, w_b=g_w_b, w_o=g_w_o,
                 conv_a=g_conv_a, rel_bias=g_rel, conv_ffn=g_conv_ffn,
                 b_gate=jnp.concatenate([dbga, dbgb], axis=1), a_log=g_alog[:, :AH], dt_bias=g_dtb[:, :AH],
                 norm_a=g_norm, ln1_g=dg1, ln1_b=db1, b_conv_ffn=g_bconv, ln2_g=dg2, ln2_b=db2)
    return loss[0, 0], grad_x, dmod, grads


_REP = {}
_off = 0
for _n, _wd, _pw in (("b_ada", 6144, 6144), ("b_gate", 2048, 2048), ("a_log", 8, 128), ("dt_bias", 8, 128),
                     ("norm_a", 128, 128), ("ln1_g", 1024, 1024), ("ln1_b", 1024, 1024),
                     ("b_conv_ffn", 5632, 5632), ("ln2_g", 1024, 1024), ("ln2_b", 1024, 1024), ("loss", 1, 128)):
    _REP[_n] = (_off, _wd, _pw)
    _off += _pw
REP_LEN = _off
REP_NAMES = [n for n in _REP if n != "loss"]
_SH = (("conv_a", (4, 384)), ("rel_bias", (16, 40)), ("conv_ffn", (3, 704)))
SH_LEN = 4352


def _pack_rep(vals):
    parts = []
    for n, (_, wd, pw) in _REP.items():
        a = vals.get(n)
        a = jnp.zeros((1, pw), F32) if a is None else jnp.pad(a.reshape(1, wd), ((0, 0), (0, pw - wd)))
        parts.append(a)
    return jnp.concatenate(parts, axis=1)


def _unpack_rep(vec, name):
    o, wd, _ = _REP[name]
    return vec[:, o:o + wd]


def _pack_sh(vals):
    parts = [vals[n].reshape(vals[n].shape[:-2] + (-1,)) for n, _ in _SH]
    a = jnp.concatenate(parts, axis=-1)
    return jnp.pad(a, [(0, 0)] * (a.ndim - 1) + [(0, SH_LEN - a.shape[-1])])


def _unpack_sh(vec, name):
    o = 0
    for n, shp in _SH:
        sz = shp[0] * shp[1]
        if n == name:
            return vec[0, o:o + sz].reshape(shp)
        o += sz
    raise KeyError(name)


def _col_shards(a, n):
    return a.reshape(a.shape[0], NDEV, n).transpose(1, 0, 2)


def kernel(x, c, w_ada, b_ada, w_in, b_gate, conv_a, a_log, dt_bias, norm_a, rel_bias, w_branch_a, w_branch_b, w_o, ln1_g, ln1_b, w_up, conv_ffn, b_conv_ffn, w_down, ln2_g, ln2_b, loss_target, m_w_ada, m_b_ada, m_w_in, m_b_gate, m_conv_a, m_a_log, m_dt_bias, m_norm_a, m_rel_bias, m_w_branch_a, m_w_branch_b, m_w_o, m_ln1_g, m_ln1_b, m_w_up, m_conv_ffn, m_b_conv_ffn, m_w_down, m_ln2_g, m_ln2_b, v_w_ada, v_b_ada, v_w_in, v_b_gate, v_conv_a, v_a_log, v_dt_bias, v_norm_a, v_rel_bias, v_w_branch_a, v_w_branch_b, v_w_o, v_ln1_g, v_ln1_b, v_w_up, v_conv_ffn, v_b_conv_ffn, v_w_down, v_ln2_g, v_ln2_b):
    W = dict(w_ada=w_ada, b_ada=b_ada, w_in=w_in, b_gate=b_gate, conv_a=conv_a, a_log=a_log, dt_bias=dt_bias,
             norm_a=norm_a, rel_bias=rel_bias, w_branch_a=w_branch_a, w_branch_b=w_branch_b, w_o=w_o, ln1_g=ln1_g,
             ln1_b=ln1_b, w_up=w_up, conv_ffn=conv_ffn, b_conv_ffn=b_conv_ffn, w_down=w_down, ln2_g=ln2_g,
             ln2_b=ln2_b)
    M = dict(w_ada=m_w_ada, b_ada=m_b_ada, w_in=m_w_in, b_gate=m_b_gate, conv_a=m_conv_a, a_log=m_a_log,
             dt_bias=m_dt_bias, norm_a=m_norm_a, rel_bias=m_rel_bias, w_branch_a=m_w_branch_a,
             w_branch_b=m_w_branch_b, w_o=m_w_o, ln1_g=m_ln1_g, ln1_b=m_ln1_b, w_up=m_w_up, conv_ffn=m_conv_ffn,
             b_conv_ffn=m_b_conv_ffn, w_down=m_w_down, ln2_g=m_ln2_g, ln2_b=m_ln2_b)
    V = dict(w_ada=v_w_ada, b_ada=v_b_ada, w_in=v_w_in, b_gate=v_b_gate, conv_a=v_conv_a, a_log=v_a_log,
             dt_bias=v_dt_bias, norm_a=v_norm_a, rel_bias=v_rel_bias, w_branch_a=v_w_branch_a,
             w_branch_b=v_w_branch_b, w_o=v_w_o, ln1_g=v_ln1_g, ln1_b=v_ln1_b, w_up=v_w_up, conv_ffn=v_conv_ffn,
             b_conv_ffn=v_b_conv_ffn, w_down=v_w_down, ln2_g=v_ln2_g, ln2_b=v_ln2_b)
    W3, M3, V3 = W, M, V
    W, M, V = ({n: a[0] for n, a in dct.items()} for dct in (W, M, V))
    me = 4 * lax.axis_index("x") + 2 * lax.axis_index("y") + lax.axis_index("c")
    big = ("w_in", "w_up", "w_down", "w_branch_a", "w_branch_b", "w_o")

    g_in = all_gather_two_level(W["w_in"].astype(BF16), "gather_w_in")
    wts = dict(w_in_slabs=g_in)
    c_all, sh_all = all_gather([c, _pack_sh({n: W[n] for n, _ in _SH})[None]], "gather_small")
    c_all = c_all.reshape(NDEV, D)
    sh_all = sh_all.reshape(NDEV, SH_LEN)

    def full_small(name, shp):
        o = 0
        for n, s in _SH:
            if n == name:
                break
            o += s[0] * s[1]
        sz = shp[0] * shp[1]
        return sh_all[:, o:o + sz].reshape(NDEV, shp[0], shp[1]).transpose(1, 0, 2).reshape(shp[0], NDEV * shp[1])

    small = dict(conv_a=full_small("conv_a", (4, 384)), rel_bias=full_small("rel_bias", (16, 40)),
                 conv_ffn=full_small("conv_ffn", (3, 704)),
                 b_gate=W["b_gate"][None], a_log=W["a_log"][None], dt_bias=W["dt_bias"][None],
                 norm_a=W["norm_a"][None], ln1_g=W["ln1_g"][None], ln1_b=W["ln1_b"][None],
                 b_conv_ffn=W["b_conv_ffn"][None], ln2_g=W["ln2_g"][None], ln2_b=W["ln2_b"][None])

    nsh = w_ada.shape[2]
    b_sh = lax.dynamic_slice(W["b_ada"][None], (0, me * nsh), (1, nsh))
    mod_sh = ada_fwd(c_all, W["w_ada"], b_sh)
    (mod_rows,) = all_to_all([mod_sh[:, None, :]], "scatter_mod")
    mod6 = mod_rows.reshape(6, D)

    after_small = (g_in[0, 0, 0].astype(F32) * 0.0 + mod6[0, 0] * 0.0).astype(BF16)
    late, late_tok = exchange_start([W[n].astype(BF16) + after_small for n in big[1:]], "gather_late_start", False)

    def late_weights(after):
        g_up, g_down, g_a, g_b, g_o = exchange_wait(late, after, "gather_late_wait")
        return dict(w_up=g_up.transpose(1, 0, 2).reshape(D, -1), w_down=g_down.reshape(DFF, D),
                    w_a=g_a.reshape(D, D), w_b=g_b.reshape(D, D), w_o=g_o.reshape(D, D))

    mod6 = mod6 + late_tok
    mod = tuple(mod6[i:i + 1] for i in range(6))

    pending = {}

    def on_grads(group, gd):
        if group == "small":
            sh_parts = {"conv_a": _col_shards(gd["conv_a"], 384), "rel_bias": _col_shards(gd["rel_bias"], 40),
                        "conv_ffn": _col_shards(gd["conv_ffn"], 704)}
            (pending["small"],) = all_to_all([_pack_sh(sh_parts)[:, None, :]], "scatter_small_grads")
            return pending["small"][0, 0, 0] * 0.0
        if group == "ffn":
            slabs = [_col_shards(gd["w_up"], w_up.shape[2]), gd["w_down"].reshape(NDEV, -1, D)]
        elif group == "mix":
            slabs = [gd[n].reshape(NDEV, -1, D) for n in ("w_a", "w_b", "w_o")]
        else:
            nin = w_in.shape[2]
            slabs = [jnp.stack([_orig_cols_from_pieces(gd, j * nin, (j + 1) * nin) for j in range(NDEV)], axis=0)]
        pending[group], tok = exchange_start([s.astype(BF16) for s in slabs], "scatter_" + group + "_start", True)
        return tok

    loss, grad_x, dmod, g = local_step(x[0], loss_target[0], mod, wts, small, late_weights, on_grads)

    rep_vals = {n: g[n] for n in REP_NAMES if n != "b_ada"}
    rep_vals["b_ada"] = jnp.concatenate(dmod, axis=1)
    rep_vals["loss"] = loss.reshape(1, 1)
    (rep_all,) = all_gather([_pack_rep(rep_vals)[None]], "gather_small_grads")
    rep_all = rep_all.reshape(NDEV, 1, REP_LEN)
    zero1 = jnp.zeros((1, 1), F32)
    rep_out = adamw(rep_all, _pack_rep({**{n: W[n][None] for n in REP_NAMES}, "loss": zero1})[None],
                    _pack_rep({**{n: M[n][None] for n in REP_NAMES}, "loss": zero1})[None],
                    _pack_rep({**{n: V[n][None] for n in REP_NAMES}, "loss": zero1})[None], "adamw_small")
    rep_out = [o[0] for o in rep_out]
    loss_total = _unpack_rep(rep_out[0], "loss")[0, 0]

    o_ada = _REP["b_ada"][0]
    dmod_all = rep_all[:, 0, o_ada:o_ada + 6 * D]
    dmod_sh = lax.dynamic_slice(dmod_all, (0, me * nsh), (NDEV, nsh))
    g_w_ada = ada_wgrad(c_all.T, dmod_sh)

    p_up, p_down = exchange_wait(pending["ffn"], grad_x, "scatter_ffn_wait")
    p_a, p_b, p_o = exchange_wait(pending["mix"], grad_x, "scatter_mix_wait")
    (p_in,) = exchange_wait(pending["in"], grad_x, "scatter_in_wait")
    parts = [p_in, p_up, p_down, p_a, p_b, p_o]
    sh_recv = pending["small"]

    res = {}
    for n, p in zip(big, parts):
        res[n] = adamw(p, W3[n], M3[n], V3[n], "adamw_" + n)
    res["w_ada"] = adamw(g_w_ada[None], W3["w_ada"], M3["w_ada"], V3["w_ada"], "adamw_w_ada")
    sh_out = adamw(sh_recv, _pack_sh({n: W[n] for n, _ in _SH})[None, None],
                   _pack_sh({n: M[n] for n, _ in _SH})[None, None],
                   _pack_sh({n: V[n] for n, _ in _SH})[None, None], "adamw_small_sharded")
    for n, _ in _SH:
        res[n] = tuple(_unpack_sh(o[0], n)[None] for o in sh_out)
    for n in REP_NAMES:
        res[n] = tuple(_unpack_rep(o, n) for o in rep_out)

    order = ("w_ada", "b_ada", "w_in", "b_gate", "conv_a", "a_log", "dt_bias", "norm_a", "rel_bias", "w_branch_a",
             "w_branch_b", "w_o", "ln1_g", "ln1_b", "w_up", "conv_ffn", "b_conv_ffn", "w_down", "ln2_g", "ln2_b")
    outs = [loss_total, grad_x[None]]
    for kind in range(4):
        outs += [res[n][kind] for n in order]
    return tuple(outs)
```

```python
import functools
import math

import numpy as np
import jax
import jax.numpy as jnp
from jax import lax
from jax.experimental import pallas as pl
from jax.experimental.pallas import tpu as pltpu

F32 = jnp.float32
BF16 = jnp.bfloat16
HI = lax.Precision.HIGHEST

D = 1024
CH = 64
AH, ADK = 8, 128
BH, BDH = 16, 64
BPREV = 8
BMAXREL = 256
RELSZ = CH + BMAXREL
DFF = 2816
ALPHA = 2.0 ** 0.25
LN_EPS, RMS_EPS, L2_EPS = 1e-5, 1e-6, 1e-6
NEG = -1e30
LR, B1, B2, AEPS, WD, STEP = 1e-3, 0.9, 0.999, 1e-8, 0.01, 10
NDEV = 8
HALO = 8
LANE = 128
TQ = 512
VMEM_LIMIT = 56 * 1024 * 1024

C_QKVA, C_Z, C_QKVB, C_GATE, C_BA, NCAT = 0, 3072, 4096, 7168, 9216, 9728


def _cparams(n_axes=1, vmem=VMEM_LIMIT):
    return pltpu.CompilerParams(dimension_semantics=("arbitrary",) * n_axes, vmem_limit_bytes=vmem)


def _dg(a, b, ca, cb):
    return lax.dot_general(a.astype(BF16), b.astype(BF16), (((ca,), (cb,)), ((), ())),
                           preferred_element_type=F32)


@jax.custom_vjp
def mm_nn(a, b):
    return _dg(a, b, 1, 0)


@jax.custom_vjp
def mm_nt(a, b):
    return _dg(a, b, 1, 1)


@jax.custom_vjp
def mm_tn(a, b):
    return _dg(a, b, 0, 0)


mm_nn.defvjp(lambda a, b: (mm_nn(a, b), (a, b)),
             lambda r, g: (mm_nt(g, r[1]).astype(r[0].dtype), mm_tn(r[0], g).astype(r[1].dtype)))
mm_nt.defvjp(lambda a, b: (mm_nt(a, b), (a, b)),
             lambda r, g: (mm_nn(g, r[1]).astype(r[0].dtype), mm_tn(g, r[0]).astype(r[1].dtype)))
mm_tn.defvjp(lambda a, b: (mm_tn(a, b), (a, b)),
             lambda r, g: (mm_nt(r[1], g).astype(r[0].dtype), mm_nn(r[0], g).astype(r[1].dtype)))


@jax.custom_vjp
def mm_w(a, w):
    return _dg(a, w, 1, 0)


mm_w.defvjp(lambda a, w: (mm_w(a, w), (a, w)),
            lambda r, g: (mm_nt(g, r[1]).astype(r[0].dtype), jnp.zeros_like(r[1])))


def _mmh(a, b):
    return lax.dot_general(a, b, (((1,), (0,)), ((), ())), precision=HI, preferred_element_type=F32)


def _bdg(a, b, ca, cb):
    return lax.dot_general(a.astype(BF16), b.astype(BF16), (((ca,), (cb,)), ((0,), (0,))),
                           preferred_element_type=F32)


@jax.custom_vjp
def bmm_nn(a, b):
    return _bdg(a, b, 2, 1)


@jax.custom_vjp
def bmm_nt(a, b):
    return _bdg(a, b, 2, 2)


@jax.custom_vjp
def bmm_tn(a, b):
    return _bdg(a, b, 1, 1)


bmm_nn.defvjp(lambda a, b: (bmm_nn(a, b), (a, b)), lambda r, g: (bmm_nt(g, r[1]), bmm_tn(r[0], g)))
bmm_nt.defvjp(lambda a, b: (bmm_nt(a, b), (a, b)), lambda r, g: (bmm_nn(g, r[1]), bmm_tn(g, r[0])))
bmm_tn.defvjp(lambda a, b: (bmm_tn(a, b), (a, b)), lambda r, g: (bmm_nt(r[1], g), bmm_nn(r[0], g)))


def _bdg3(a, b, ca, cb):
    return lax.dot_general(a, b, (((ca,), (cb,)), ((0,), (0,))), precision=HI, preferred_element_type=F32)


def _bdgp(a, b, ca, cb):
    return _bdg(a, b, ca, cb)


@jax.custom_vjp
def bmm3_nn(a, b):
    return _bdgp(a, b, 2, 1)


bmm3_nn.defvjp(lambda a, b: (bmm3_nn(a, b), (a, b)),
               lambda r, g: (_bdgp(g, r[1], 2, 2), _bdgp(r[0], g, 1, 1)))


def _sigmoid(x):
    return 0.5 * jnp.tanh(0.5 * x) + 0.5


def _silu(x):
    return x * _sigmoid(x)


def _softplus(x):
    return jnp.maximum(x, 0.0) + jnp.log(1.0 + jnp.exp(-jnp.abs(x)))


def _layernorm(r, g, b):
    mu = jnp.mean(r, axis=-1, keepdims=True)
    xc = r - mu
    var = jnp.mean(xc * xc, axis=-1, keepdims=True)
    return xc * lax.rsqrt(var + LN_EPS) * g + b


def _iota2(shape, dim):
    return lax.broadcasted_iota(jnp.int32, shape, dim)


@jax.custom_vjp
def causal_conv(ext, rows):
    k = len(rows)
    y = None
    for j in range(k):
        s = k - 1 - j
        r = pltpu.roll(ext, s, 0) if s else ext
        t = r[HALO:] * rows[j]
        y = t if y is None else y + t
    return y


def _causal_conv_fwd(ext, rows):
    return causal_conv(ext, rows), (ext, rows)


def _causal_conv_bwd(res, g):
    ext, rows = res
    n = ext.shape[0]
    k = len(rows)
    gext = jnp.concatenate([jnp.zeros((HALO, g.shape[1]), g.dtype), g], axis=0)
    dext = None
    drows = []
    for j in range(k):
        s = k - 1 - j
        up = pltpu.roll(gext, n - s, 0) if s else gext
        t = up * rows[j]
        dext = t if dext is None else dext + t
        r = pltpu.roll(ext, s, 0) if s else ext
        drows.append(jnp.sum(g * r[HALO:], axis=0, keepdims=True))
    return dext, tuple(drows)


causal_conv.defvjp(_causal_conv_fwd, _causal_conv_bwd)


def _chunk_masks(tm):
    i = _iota2((tm, tm), 0)
    j = _iota2((tm, tm), 1)
    same = (i ^ j) < CH
    lower = jnp.where(same & (j <= i), 1.0, 0.0).astype(F32)
    upper = jnp.where(same & (i <= j), 1.0, 0.0).astype(F32)
    return lower, upper


@jax.custom_vjp
def chunk_cumsum(g):
    lower, _ = _chunk_masks(g.shape[0])
    return _mmh(lower, g)


def _chunk_cumsum_bwd(_, ct):
    _, upper = _chunk_masks(ct.shape[0])
    return (_mmh(upper, ct),)


chunk_cumsum.defvjp(lambda g: (chunk_cumsum(g), None), _chunk_cumsum_bwd)


@jax.custom_vjp
def inv_unit_lower(a):
    n = a.shape[-1]
    eye = jnp.where(_iota2((1, n, n), 1) == _iota2((1, n, n), 2), 1.0, 0.0).astype(F32)
    x = eye - a
    p = _bdg3(a, a, 2, 1)
    steps = int(math.log2(n)) - 1
    for s in range(steps):
        x = x + _bdg3(x, p, 2, 1)
        if s + 1 < steps:
            p = _bdg3(p, p, 2, 1)
    return x


def _inv_fwd(a):
    t = inv_unit_lower(a)
    return t, t


def _inv_bwd(t, g):
    return (-_bdgp(_bdgp(t, g, 1, 1), t, 2, 2),)


inv_unit_lower.defvjp(_inv_fwd, _inv_bwd)


@jax.custom_vjp
def inv_known(a, t):
    return t


inv_known.defvjp(lambda a, t: (t, t), lambda t, g: (_inv_bwd(t, g)[0], jnp.zeros_like(t)))


def prep_head_fn(ext, rows, scale):
    s = _silu(causal_conv(ext, rows))
    if scale is None:
        return s
    return s * (lax.rsqrt(jnp.sum(s * s, axis=-1, keepdims=True) + L2_EPS) * scale)


def prep_gate_fn(bb, aa, a_log, dtb):
    g = -jnp.exp(a_log) * _softplus(aa + dtb)
    return chunk_cumsum(g), _sigmoid(bb)


PREP_SCALES = (ADK ** -0.5, 1.0, None)


def _head_cols(a):
    lane = _iota2((1, LANE), 1)
    return jnp.concatenate([jnp.sum(jnp.where(lane == h, a, 0.0), axis=1, keepdims=True)[None]
                            for h in range(AH)], axis=0)


def _head_rows(a):
    at = a.T[:AH]
    sub = _iota2((AH, 1), 0)
    return jnp.concatenate([jnp.sum(jnp.where(sub == h, at, 0.0), axis=0, keepdims=True)[None]
                            for h in range(AH)], axis=0)


def c1_heads(q, k, v, gcs, beta, tinv_saved=None):
    gcol = _head_cols(gcs)
    grow = _head_rows(gcs)
    bcol = _head_cols(beta)
    i = _iota2((1, CH, CH), 1)
    j = _iota2((1, CH, CH), 2)
    causal = j <= i
    strict = j < i
    diff = gcol - grow
    decay = jnp.where(causal, jnp.exp(jnp.where(causal, diff, 0.0)), 0.0)
    kb = k * bcol
    vb = v * bcol
    a_low = jnp.where(strict, bmm_nt(kb, k) * decay, 0.0)
    tinv = inv_unit_lower(a_low) if tinv_saved is None else inv_known(a_low, tinv_saved)
    egc = jnp.exp(gcol)
    u = bmm3_nn(tinv, vb)
    w = bmm3_nn(tinv, kb * egc)
    qk = jnp.where(causal, bmm_nt(q, k) * decay, 0.0)
    glast = jnp.sum(jnp.where(_iota2((1, CH, 1), 1) == CH - 1, gcol, 0.0), axis=1, keepdims=True)
    qg = q * egc
    kd = k * jnp.exp(glast - gcol)
    eg = jnp.exp(glast) * jnp.ones((1, 1, ADK), F32)
    return u, w, qk, qg, kd, eg, tinv


def c2_heads(s, u, w, qk, qg, kd, eg, z, nw):
    vn = u - bmm_nn(w, s)
    o = bmm_nn(qg, s) + bmm_nn(qk, vn)
    s2 = s * eg + bmm_tn(kd, vn)
    ms = jnp.mean(o * o, axis=-1, keepdims=True)
    og = o * lax.rsqrt(ms + RMS_EPS) * nw * _silu(z)
    return og, s2


ATT_SCALE = BDH ** -0.5


def _head_mask(hh):
    lane = _iota2((1, 2 * BDH), 1)
    return jnp.where((lane >= hh * BDH) & (lane < (hh + 1) * BDH), 1.0, 0.0).astype(F32)


def attn_sub_fwd(q, k, v, bias2, r, firstf):
    col = _iota2((1, KWIN), 1) + r * SUBQ
    nokey = jnp.where(col < TQ, firstf, 0.0) * NEG
    out, probs = None, []
    for hh in range(2):
        hm = _head_mask(hh)
        s = mm_nt(q * (hm * ATT_SCALE), k) + (assemble_bias(bias2[hh], r) + nokey)
        p = jnp.exp(s - jnp.max(s, axis=-1, keepdims=True))
        inv = 1.0 / jnp.sum(p, axis=-1, keepdims=True)
        o = mm_nn(p, v) * (inv * hm)
        out = o if out is None else out + o
        probs.append(p * inv)
    return out, probs


def attn_sub_bwd(q, k, v, o, do, probs, r):
    dq, dk, dv, dss = None, None, None, []
    for hh in range(2):
        hm = _head_mask(hh)
        p = probs[hh]
        doh = do * hm
        ds = p * (mm_nt(doh, v) - jnp.sum(doh * o, axis=-1, keepdims=True))
        dqh = mm_nn(ds, k) * (hm * ATT_SCALE)
        dkh = mm_tn(ds, q * (hm * ATT_SCALE))
        dvh = mm_tn(p, doh)
        dq = dqh if dq is None else dq + dqh
        dk = dkh if dk is None else dk + dkh
        dv = dvh if dv is None else dv + dvh
        dss.append(ds)
    return dq, dk, dv, dss


def merge_fn(x, oa, ob, gra, grb, p_pa, p_pb, p_mix, bga, bgb, gate_t, g1, b1, scale_f, shift_f,
             wa, wb, wo):
    ga = _sigmoid(gra + bga)
    gb = _sigmoid(grb + bgb)
    pa = mm_w(oa, wa) + p_pa
    pb = mm_w(ob, wb) + p_pb
    merged = ga * pa + gb * pb
    mix = mm_w(merged, wo) + p_mix
    y1 = _layernorm(ALPHA * x + gate_t * mix, g1, b1)
    return y1, merged


def ffn_act_fn(extg, extv, rows_g, rows_v, bg, bv):
    return _silu(causal_conv(extg, rows_g) + bg) * (causal_conv(extv, rows_v) + bv)


def head_fn(a, y1, p_ffn, gate_f, g2, b2, tgt, wd):
    ffn = mm_w(a, wd) + p_ffn
    y2 = _layernorm(ALPHA * y1 + gate_f * ffn, g2, b2)
    err = y2 - tgt
    return 0.5 * jnp.sum(jnp.mean(err * err, axis=-1, keepdims=True))


def _rows(tm, width, colblk=0, order=None):
    if order is None:
        return pl.BlockSpec((tm, width), lambda i: (i, colblk))
    return pl.BlockSpec((tm, width), lambda i: (order(i), colblk))


def _const(shape):
    nd = len(shape)
    return pl.BlockSpec(shape, lambda *_: (0,) * nd)


def _pick(n, cands):
    for c in cands:
        if n % c == 0:
            return c
    raise ValueError(f"no tile for {n}")


def _tile(n, cap):
    best = None
    for c in range(LANE, min(n, cap) + 1, LANE):
        if n % c == 0:
            best = c
    if best is None:
        raise ValueError(f"no tile for {n}")
    return best


def _onehot_rows(k, j):
    return jnp.where(_iota2((k, 1), 0) == j, 1.0, 0.0).astype(F32)


def _stack_rows(drows):
    k = len(drows)
    out = None
    for j in range(k):
        tj = _onehot_rows(k, j) * drows[j]
        out = tj if out is None else out + tj
    return out


def matmul(a, w, out_dtype, name, ta=False, tb=False):
    kdim, m = a.shape if ta else a.shape[::-1]
    n = w.shape[0] if tb else w.shape[1]
    tm = _tile(m, 2048 if kdim <= 1024 else 1024)
    tn = _tile(n, 1024)
    tk = _tile(kdim, 2560)
    nk = kdim // tk
    a_spec = (pl.BlockSpec((tk, tm), lambda i, j, k: (k, i)) if ta
              else pl.BlockSpec((tm, tk), lambda i, j, k: (i, k)))
    w_spec = (pl.BlockSpec((tn, tk), lambda i, j, k: (j, k)) if tb
              else pl.BlockSpec((tk, tn), lambda i, j, k: (k, j)))

    def body(a_ref, w_ref, o_ref, *scratch):
        p = _dg(a_ref[...], w_ref[...], 0 if ta else 1, 1 if tb else 0)
        if nk == 1:
            o_ref[...] = p.astype(out_dtype)
            return
        acc = scratch[0]
        k = pl.program_id(2)

        @pl.when(k == 0)
        def _():
            acc[...] = p

        @pl.when(k > 0)
        def _():
            acc[...] += p

        @pl.when(k == nk - 1)
        def _():
            o_ref[...] = acc[...].astype(out_dtype)

    return pl.pallas_call(
        body, name=name,
        grid=(m // tm, n // tn, nk),
        in_specs=[a_spec, w_spec],
        out_specs=pl.BlockSpec((tm, tn), lambda i, j, k: (i, j)),
        out_shape=jax.ShapeDtypeStruct((m, n), out_dtype),
        scratch_shapes=[] if nk == 1 else [pltpu.VMEM((tm, tn), F32)],
        compiler_params=_cparams(3),
    )(a, w)


def dgrad_pieces(pieces, tail, w, name):
    m = pieces[0][0].shape[0]
    n, ktot = w.shape
    tk = 1024
    tm = _tile(m, 1024)
    wt = tail.shape[1]
    ranges, k0 = [], 0
    for arr, off in pieces:
        assert off == k0 * tk and arr.shape[1] % tk == 0
        ranges.append((k0, k0 + arr.shape[1] // tk))
        k0 = ranges[-1][1]
    nk = k0
    npc = len(pieces)

    def body(*refs):
        a_refs, t_ref, w_ref, wt_ref, o_ref, acc = refs[:npc], refs[npc], refs[npc + 1], refs[npc + 2], refs[npc + 3], refs[npc + 4]
        k = pl.program_id(1)

        @pl.when(k == 0)
        def _():
            acc[...] = _dg(t_ref[...], wt_ref[...], 1, 1)

        for a_ref, (lo, hi) in zip(a_refs, ranges):
            @pl.when((k >= lo) & (k < hi))
            def _(a_ref=a_ref):
                acc[...] += _dg(a_ref[...], w_ref[...], 1, 1)

        @pl.when(k == nk - 1)
        def _():
            o_ref[...] = acc[...]

    def piece_spec(lo, hi):
        return pl.BlockSpec((tm, tk), lambda i, k: (i, jnp.clip(k - lo, 0, hi - lo - 1)))

    return pl.pallas_call(
        body, name=name, grid=(m // tm, nk),
        in_specs=[piece_spec(lo, hi) for lo, hi in ranges] + [
            pl.BlockSpec((tm, wt), lambda i, k: (i, 0)),
            pl.BlockSpec((n, tk), lambda i, k: (0, k)),
            pl.BlockSpec((n, wt), lambda i, k: (0, (ktot - wt) // wt))],
        out_specs=pl.BlockSpec((tm, n), lambda i, k: (i, 0)),
        out_shape=jax.ShapeDtypeStruct((m, n), F32),
        scratch_shapes=[pltpu.VMEM((tm, n), F32)],
        compiler_params=_cparams(2),
    )(*[a for a, _ in pieces], tail, w, w)


def modulate(x, scale, shift, name):
    t, d = x.shape
    tm = _pick(t, (512, 256, 128))

    def body(x_ref, sc_ref, sh_ref, o_ref):
        o_ref[...] = (x_ref[...] * (1.0 + sc_ref[...]) + sh_ref[...]).astype(BF16)

    return pl.pallas_call(
        body, name=name, grid=(t // tm,),
        in_specs=[_rows(tm, d), _const((1, d)), _const((1, d))],
        out_specs=_rows(tm, d),
        out_shape=jax.ShapeDtypeStruct((t, d), BF16),
        compiler_params=_cparams(),
    )(x, scale, shift)


def modulate_bwd(dh, xin, dres, scale, name):
    t, d = dh.shape
    tm = _pick(t, (512, 256, 128))

    def body(dh_ref, x_ref, dr_ref, sc_ref, o_ref, dsc_ref, dsh_ref):
        i = pl.program_id(0)
        dh_v = dh_ref[...]
        o_ref[...] = dr_ref[...] + dh_v * (1.0 + sc_ref[...])

        @pl.when(i == 0)
        def _():
            dsc_ref[...] = jnp.zeros_like(dsc_ref)
            dsh_ref[...] = jnp.zeros_like(dsh_ref)

        dsc_ref[...] += jnp.sum(dh_v * x_ref[...], axis=0, keepdims=True)
        dsh_ref[...] += jnp.sum(dh_v, axis=0, keepdims=True)

    return pl.pallas_call(
        body, name=name, grid=(t // tm,),
        in_specs=[_rows(tm, d), _rows(tm, d), _rows(tm, d), _const((1, d))],
        out_specs=[_rows(tm, d), _const((1, d)), _const((1, d))],
        out_shape=[jax.ShapeDtypeStruct((t, d), F32), jax.ShapeDtypeStruct((1, d), F32),
                   jax.ShapeDtypeStruct((1, d), F32)],
        compiler_params=_cparams(),
    )(dh, xin, dres, scale)


PREP_TM = 128


def _halo_specs(tm, width, colblk, order):
    per = tm // HALO
    return [pl.BlockSpec((HALO, width), lambda i: (jnp.maximum(order(i) * per - 1, 0), colblk)),
            pl.BlockSpec((tm, width), lambda i: (order(i), colblk))]


def prep_fwd(proj, conv_a, a_log, dtb):
    t = proj.shape[0]
    tm = PREP_TM
    nt = t // tm
    wq = 3 * D

    def body(prev_ref, cur_ref, bb_ref, aa_ref, cw_ref, al_ref, dt_ref, q_ref, k_ref, v_ref, g_ref, b_ref):
        i = pl.program_id(0)
        flag = jnp.where(i > 0, 1.0, 0.0)
        for part, o_ref in enumerate((q_ref, k_ref, v_ref)):
            for h in range(AH):
                sl = slice(part * D + h * ADK, part * D + (h + 1) * ADK)
                ext = jnp.concatenate([prev_ref[:, sl] * flag, cur_ref[:, sl]], axis=0)
                rows = tuple(cw_ref[j:j + 1, sl] for j in range(4))
                o_ref[h] = prep_head_fn(ext, rows, PREP_SCALES[part])
        gcs, beta = prep_gate_fn(bb_ref[...], aa_ref[...], al_ref[...], dt_ref[...])
        g_ref[...] = gcs
        b_ref[...] = beta

    ident = lambda i: i
    hm = pl.BlockSpec((AH, tm, ADK), lambda i: (0, i, 0))
    return pl.pallas_call(
        body, name="prep_fwd", grid=(nt,),
        in_specs=_halo_specs(tm, wq, 0, ident) + [
            _rows(tm, 128, C_BA // 128), _rows(tm, 128, C_BA // 128 + 1),
            _const((4, wq)), _const((1, 128)), _const((1, 128))],
        out_specs=[hm, hm, hm, _rows(tm, 128), _rows(tm, 128)],
        out_shape=[jax.ShapeDtypeStruct((AH, t, ADK), F32)] * 3 + [jax.ShapeDtypeStruct((t, 128), F32)] * 2,
        compiler_params=_cparams(),
    )(proj, proj, proj, proj, conv_a, a_log, dtb)


def prep_bwd(proj, conv_a, a_log, dtb, dq, dk, dv, dgcs, dbeta):
    t = proj.shape[0]
    tm = PREP_TM
    nt = t // tm
    wq = 3 * D
    rev = lambda i: nt - 1 - i

    def body(prev_ref, cur_ref, bb_ref, aa_ref, cw_ref, al_ref, dt_ref,
             dq_ref, dk_ref, dv_ref, dg_ref, db_ref,
             dpre_ref, dbb_ref, daa_ref, dcw_ref, dal_ref, ddt_ref, carry):
        i = pl.program_id(0)
        flag = jnp.where(i < nt - 1, 1.0, 0.0)

        @pl.when(i == 0)
        def _():
            carry[...] = jnp.zeros_like(carry)
            dcw_ref[...] = jnp.zeros_like(dcw_ref)
            dal_ref[...] = jnp.zeros_like(dal_ref)
            ddt_ref[...] = jnp.zeros_like(ddt_ref)

        for part, d_ref in enumerate((dq_ref, dk_ref, dv_ref)):
            for h in range(AH):
                sl = slice(part * D + h * ADK, part * D + (h + 1) * ADK)
                ext = jnp.concatenate([prev_ref[:, sl] * flag, cur_ref[:, sl]], axis=0)
                rows = tuple(cw_ref[j:j + 1, sl] for j in range(4))
                _, vjp = jax.vjp(lambda e, r: prep_head_fn(e, r, PREP_SCALES[part]), ext, rows)
                dext, drows = vjp(d_ref[h])
                dcur = dext[HALO:]
                dpre_ref[:, sl] = jnp.concatenate([dcur[:tm - HALO], dcur[tm - HALO:] + carry[:, sl]],
                                                  axis=0).astype(BF16)
                carry[:, sl] = dext[:HALO]
                dcw_ref[:, sl] += _stack_rows(drows)
        _, vjp = jax.vjp(prep_gate_fn, bb_ref[...], aa_ref[...], al_ref[...], dt_ref[...])
        dbb, daa, dal, ddt = vjp((dg_ref[...], db_ref[...]))
        dbb_ref[...] = dbb.astype(BF16)
        daa_ref[...] = daa.astype(BF16)
        dal_ref[...] += dal
        ddt_ref[...] += ddt

    hm = pl.BlockSpec((AH, tm, ADK), lambda i: (0, rev(i), 0))
    return pl.pallas_call(
        body, name="prep_bwd", grid=(nt,),
        in_specs=_halo_specs(tm, wq, 0, rev) + [
            _rows(tm, 128, C_BA // 128, rev), _rows(tm, 128, C_BA // 128 + 1, rev),
            _const((4, wq)), _const((1, 128)), _const((1, 128)),
            hm, hm, hm, _rows(tm, 128, 0, rev), _rows(tm, 128, 0, rev)],
        out_specs=[_rows(tm, wq, 0, rev), _rows(tm, 128, 0, rev), _rows(tm, 128, 0, rev),
                   _const((4, wq)), _const((1, 128)), _const((1, 128))],
        out_shape=[jax.ShapeDtypeStruct((t, wq), BF16), jax.ShapeDtypeStruct((t, 128), BF16),
                   jax.ShapeDtypeStruct((t, 128), BF16), jax.ShapeDtypeStruct((4, wq), F32),
                   jax.ShapeDtypeStruct((1, 128), F32), jax.ShapeDtypeStruct((1, 128), F32)],
        scratch_shapes=[pltpu.VMEM((HALO, wq), F32)],
        compiler_params=_cparams(),
    )(proj, proj, proj, proj, conv_a, a_log, dtb, dq, dk, dv, dgcs, dbeta)


def _c1_specs(order):
    hm = pl.BlockSpec((AH, CH, ADK), lambda n: (0, order(n), 0))
    col = pl.BlockSpec((CH, LANE), lambda n: (order(n), 0))
    qk = pl.BlockSpec((1, AH, CH, CH), lambda n: (order(n), 0, 0, 0))
    eg = pl.BlockSpec((1, AH, 1, ADK), lambda n: (order(n), 0, 0, 0))
    return hm, col, qk, eg


def _heads(ref):
    return jnp.stack([ref[:, h * ADK:(h + 1) * ADK] for h in range(AH)], axis=0)


def c1_fwd(q, k, v, gcs, beta):
    t = q.shape[1]
    nc = t // CH
    hm, col, qks, egs = _c1_specs(lambda n: n)

    def body(q_ref, k_ref, v_ref, g_ref, b_ref, u_ref, w_ref, qg_ref, kd_ref, qk_ref, eg_ref, ti_ref):
        u, w, qk, qg, kd, eg, tinv = c1_heads(q_ref[...], k_ref[...], v_ref[...], g_ref[...], b_ref[...])
        u_ref[...] = u
        w_ref[...] = w.astype(BF16)
        qg_ref[...] = qg.astype(BF16)
        kd_ref[...] = kd.astype(BF16)
        qk_ref[0] = qk.astype(BF16)
        eg_ref[0] = eg
        ti_ref[0] = tinv

    return pl.pallas_call(
        body, name="c1_fwd", grid=(nc,),
        in_specs=[hm, hm, hm, col, col],
        out_specs=[hm, hm, hm, hm, qks, egs, qks],
        out_shape=[jax.ShapeDtypeStruct((AH, t, ADK), F32)] + [jax.ShapeDtypeStruct((AH, t, ADK), BF16)] * 3 + [
            jax.ShapeDtypeStruct((nc, AH, CH, CH), BF16), jax.ShapeDtypeStruct((nc, AH, 1, ADK), F32),
            jax.ShapeDtypeStruct((nc, AH, CH, CH), F32)],
        compiler_params=_cparams(),
    )(q, k, v, gcs, beta)


def c1_bwd(q, k, v, gcs, beta, tinv, du, dw, dqg, dkd, dqk, deg):
    t = q.shape[1]
    nc = t // CH
    hm, col, qks, egs = _c1_specs(lambda n: n)

    def body(q_ref, k_ref, v_ref, g_ref, b_ref, ti_ref, du_ref, dw_ref, dqg_ref, dkd_ref, dqk_ref, deg_ref,
             dq_ref, dk_ref, dv_ref, dg_ref, db_ref):
        _, vjp = jax.vjp(lambda q_, k_, v_, g_, b_: c1_heads(q_, k_, v_, g_, b_, ti_ref[0]),
                         q_ref[...], k_ref[...], v_ref[...], g_ref[...], b_ref[...])
        dq, dk, dv, dg, db = vjp((du_ref[...].astype(F32), dw_ref[...].astype(F32), dqk_ref[0], dqg_ref[...],
                                  dkd_ref[...], deg_ref[0],
                                  jnp.zeros((AH, CH, CH), F32)))
        dq_ref[...] = dq
        dk_ref[...] = dk
        dv_ref[...] = dv
        dg_ref[...] = dg
        db_ref[...] = db

    return pl.pallas_call(
        body, name="c1_bwd", grid=(nc,),
        in_specs=[hm, hm, hm, col, col, qks, hm, hm, hm, hm, qks, egs],
        out_specs=[hm, hm, hm, col, col],
        out_shape=[jax.ShapeDtypeStruct((AH, t, ADK), F32)] * 3 + [jax.ShapeDtypeStruct((t, LANE), F32)] * 2,
        compiler_params=_cparams(),
    )(q, k, v, gcs, beta, tinv, du, dw, dqg, dkd, dqk, deg)


def c2_fwd(u, w, qg, kd, qk, eg, proj, norm_a):
    t = u.shape[1]
    nc = t // CH
    hm, _, qks, egs = _c1_specs(lambda n: n)
    tok = pl.BlockSpec((CH, D), lambda n: (n, 0))
    zspec = pl.BlockSpec((CH, D), lambda n: (n, C_Z // D))
    sspec = pl.BlockSpec((1, AH, ADK, ADK), lambda n: (n, 0, 0, 0))

    def body(u_ref, w_ref, qg_ref, kd_ref, qk_ref, eg_ref, z_ref, nw_ref, o_ref, sall_ref, st):
        n = pl.program_id(0)

        @pl.when(n == 0)
        def _():
            st[...] = jnp.zeros_like(st)

        s = st[...]
        sall_ref[0] = s
        og, s2 = c2_heads(s, u_ref[...], w_ref[...], qk_ref[0], qg_ref[...], kd_ref[...], eg_ref[0],
                          _heads(z_ref), nw_ref[...])
        st[...] = s2
        for h in range(AH):
            o_ref[:, h * ADK:(h + 1) * ADK] = og[h].astype(BF16)

    return pl.pallas_call(
        body, name="c2_fwd", grid=(nc,),
        in_specs=[hm, hm, hm, hm, qks, egs, zspec, _const((1, ADK))],
        out_specs=[tok, sspec],
        out_shape=[jax.ShapeDtypeStruct((t, D), BF16), jax.ShapeDtypeStruct((nc, AH, ADK, ADK), F32)],
        scratch_shapes=[pltpu.VMEM((AH, ADK, ADK), F32)],
        compiler_params=_cparams(),
    )(u, w, qg, kd, qk, eg, proj, norm_a)


def c2_bwd(u, w, qg, kd, qk, eg, proj, norm_a, sall, do):
    t = u.shape[1]
    nc = t // CH
    rev = lambda n: nc - 1 - n
    hm, _, qks, egs = _c1_specs(rev)
    tok = pl.BlockSpec((CH, D), lambda n: (rev(n), 0))
    zspec = pl.BlockSpec((CH, D), lambda n: (rev(n), C_Z // D))
    sspec = pl.BlockSpec((1, AH, ADK, ADK), lambda n: (rev(n), 0, 0, 0))

    def body(u_ref, w_ref, qg_ref, kd_ref, qk_ref, eg_ref, z_ref, nw_ref, sall_ref, do_ref,
             du_ref, dw_ref, dqg_ref, dkd_ref, dqk_ref, deg_ref, dz_ref, dnw_ref, dst):
        n = pl.program_id(0)

        @pl.when(n == 0)
        def _():
            dst[...] = jnp.zeros_like(dst)
            dnw_ref[...] = jnp.zeros_like(dnw_ref)

        _, vjp = jax.vjp(c2_heads, sall_ref[0], u_ref[...], w_ref[...].astype(F32), qk_ref[0].astype(F32),
                         qg_ref[...].astype(F32), kd_ref[...].astype(F32), eg_ref[0], _heads(z_ref), nw_ref[...])
        ds, du, dw, dqk, dqg, dkd, deg, dz, dn = vjp((_heads(do_ref), dst[...]))
        dst[...] = ds
        du_ref[...] = du.astype(BF16)
        dw_ref[...] = dw.astype(BF16)
        dqg_ref[...] = dqg
        dkd_ref[...] = dkd
        dqk_ref[0] = dqk
        deg_ref[0] = deg
        for h in range(AH):
            dz_ref[:, h * ADK:(h + 1) * ADK] = dz[h].astype(BF16)
        dnw_ref[...] += dn

    return pl.pallas_call(
        body, name="c2_bwd", grid=(nc,),
        in_specs=[hm, hm, hm, hm, qks, egs, zspec, _const((1, ADK)), sspec, tok],
        out_specs=[hm, hm, hm, hm, qks, egs, tok, _const((1, ADK))],
        out_shape=[jax.ShapeDtypeStruct((AH, t, ADK), BF16)] * 2 + [jax.ShapeDtypeStruct((AH, t, ADK), F32)] * 2 + [
            jax.ShapeDtypeStruct((nc, AH, CH, CH), F32), jax.ShapeDtypeStruct((nc, AH, 1, ADK), F32),
            jax.ShapeDtypeStruct((t, D), BF16), jax.ShapeDtypeStruct((1, ADK), F32)],
        scratch_shapes=[pltpu.VMEM((AH, ADK, ADK), F32)],
        compiler_params=_cparams(),
    )(u, w, qg, kd, qk, eg, proj, norm_a, sall, do)


NQB = TQ // CH
NKB = 2 * TQ // CH
NDIST = BPREV + 1
KLO = -(NQB - 2)
NPAIR = NKB - 1 - KLO + 1


def bias_table(rel_bias):
    nh = rel_bias.shape[0]
    relx = jnp.concatenate([rel_bias, jnp.broadcast_to(rel_bias[:, -1:], (nh, CH * BPREV + 2 * CH - 1 - RELSZ))],
                           axis=1)
    t = jnp.stack([relx[:, CH * k:CH * k + 2 * CH - 1] for k in range(NDIST)], axis=1)
    trev = t[:, :, ::-1]
    g2 = jnp.concatenate([trev[:, :, CH - 1:], jnp.zeros((nh, NDIST, 1), F32), trev[:, :, :CH - 1]], axis=2)
    flat = jnp.tile(g2, (1, 1, CH + 1))[:, :, :CH * (2 * CH - 1)]
    blk = flat.reshape(nh, NDIST, CH, 2 * CH - 1)[..., :CH]
    neg = jnp.full((nh, NQB - 1, CH, CH), NEG, F32)
    asc = jnp.concatenate([neg, blk, neg], axis=1)
    return jnp.concatenate([asc[:, 1:], asc[:, :-1]], axis=-1)


SUBQ = 4 * CH
NSUB = TQ // SUBQ
KWIN = SUBQ + BPREV * CH


def assemble_bias(tab, r):
    b0 = r * SUBQ // (2 * CH)
    rows = [jnp.concatenate([tab[NQB + a - 2 * b - KLO] for b in range(b0, b0 + KWIN // (2 * CH))], axis=1)
            for a in range(r * SUBQ // CH, (r + 1) * SUBQ // CH)]
    return jnp.concatenate(rows, axis=0)


def bias_table_bwd_layout(dtab):
    nh = dtab.shape[0]
    dasc = (jnp.pad(dtab[..., :CH], ((0, 0), (1, 0), (0, 0), (0, 0)))
            + jnp.pad(dtab[..., CH:], ((0, 0), (0, 1), (0, 0), (0, 0))))
    dblk = dasc[:, NQB - 1:NQB - 1 + NDIST]
    dr = jnp.pad(dblk, ((0, 0), (0, 0), (0, 0), (0, CH - 1)))
    flat = jnp.pad(dr.reshape(nh, NDIST, CH * (2 * CH - 1)), ((0, 0), (0, 0), (0, 3 * CH)))
    return flat.reshape(nh, NDIST, CH + 1, 2 * CH).transpose(0, 2, 1, 3).reshape(nh, CH + 1, NDIST * 2 * CH)


def _fold_matrix_np():
    f = np.zeros((NDIST * 2 * CH, 384), np.float32)
    for k in range(NDIST):
        s = k
        for xx in range(2 * CH):
            if xx == CH:
                continue
            m = CH - 1 - xx if xx < CH else 3 * CH - 1 - xx
            f[s * 2 * CH + xx, min(CH * k + m, RELSZ - 1)] = 1.0
    return f


def relbias_reduce(dlay):
    nh, rows, cols = dlay.shape
    rpad = (-rows) % 8
    dlay = jnp.pad(dlay, ((0, 0), (0, rpad), (0, 0)))
    fold = jnp.asarray(_fold_matrix_np())

    def body(d_ref, f_ref, o_ref):
        cs = jnp.sum(d_ref[0], axis=0, keepdims=True)
        o_ref[0] = _mmh(jnp.broadcast_to(cs, (8, cols)), f_ref[...])

    out = pl.pallas_call(
        body, name="relbias_reduce", grid=(nh,),
        in_specs=[pl.BlockSpec((1, rows + rpad, cols), lambda h: (h, 0, 0)), _const((cols, 384))],
        out_specs=pl.BlockSpec((1, 8, 384), lambda h: (h, 0, 0)),
        out_shape=jax.ShapeDtypeStruct((nh, 8, 384), F32),
        compiler_params=_cparams(),
    )(dlay, fold)
    return out[:, 0, :RELSZ]


def attn_fwd(proj, bias):
    t = proj.shape[0]
    nt = t // TQ
    cb = C_QKVB // 128

    def body(q_ref, kp_ref, kc_ref, vp_ref, vc_ref, b_ref, o_ref, p_ref):
        i = pl.program_id(1)
        firstf = jnp.where(i == 0, 1.0, 0.0)
        for r in range(NSUB):
            lo, hi = r * SUBQ, r * SUBQ + KWIN - TQ
            kw = jnp.concatenate([kp_ref[lo:, :], kc_ref[:hi, :]], axis=0)
            vw = jnp.concatenate([vp_ref[lo:, :], vc_ref[:hi, :]], axis=0)
            out, probs = attn_sub_fwd(q_ref[lo:lo + SUBQ, :].astype(F32), kw, vw, b_ref[...], r, firstf)
            o_ref[lo:lo + SUBQ, :] = out.astype(BF16)
            for hh in range(2):
                p_ref[hh, lo:lo + SUBQ, :] = probs[hh].astype(BF16)

    def blk(off, prev):
        if prev:
            return pl.BlockSpec((TQ, 128), lambda p, i: (jnp.maximum(i - 1, 0), cb + off + p))
        return pl.BlockSpec((TQ, 128), lambda p, i: (i, cb + off + p))

    return pl.pallas_call(
        body, name="attn_fwd", grid=(BH // 2, nt),
        in_specs=[blk(0, False), blk(8, True), blk(8, False), blk(16, True), blk(16, False),
                  pl.BlockSpec((2, NPAIR, CH, 2 * CH), lambda p, i: (p, 0, 0, 0))],
        out_specs=[pl.BlockSpec((TQ, 128), lambda p, i: (i, p)),
                   pl.BlockSpec((2, TQ, KWIN), lambda p, i: (p, i, 0))],
        out_shape=[jax.ShapeDtypeStruct((t, D), BF16), jax.ShapeDtypeStruct((BH, t, KWIN), BF16)],
        compiler_params=_cparams(2),
    )(proj, proj, proj, proj, proj, bias)


def attn_bwd(proj, ob, probs, do):
    t = proj.shape[0]
    nt = t // TQ
    cb = C_QKVB // 128

    def body(q_ref, kp_ref, kc_ref, vp_ref, vc_ref, o_ref, p_ref, do_ref,
             dq_ref, dk_ref, dv_ref, db_ref, ck, cv, ak, av):
        i = pl.program_id(1)

        @pl.when(i == 0)
        def _():
            ck[...] = jnp.zeros_like(ck)
            cv[...] = jnp.zeros_like(cv)
            db_ref[...] = jnp.zeros_like(db_ref)

        @pl.when(i < nt)
        def _():
            ak[...] = jnp.zeros_like(ak)
            av[...] = jnp.zeros_like(av)
            for r in range(NSUB):
                lo, hi = r * SUBQ, r * SUBQ + KWIN - TQ
                rows = slice(lo, lo + SUBQ)
                kw = jnp.concatenate([kp_ref[lo:, :], kc_ref[:hi, :]], axis=0)
                vw = jnp.concatenate([vp_ref[lo:, :], vc_ref[:hi, :]], axis=0)
                probs_r = [p_ref[hh, rows, :].astype(F32) for hh in range(2)]
                dq, dkw, dvw, dss = attn_sub_bwd(q_ref[rows, :].astype(F32), kw, vw, o_ref[rows, :].astype(F32),
                                                 do_ref[rows, :], probs_r, r)
                dq_ref[rows, :] = dq.astype(BF16)
                ak[lo:lo + KWIN, :] += dkw
                av[lo:lo + KWIN, :] += dvw
                for hh in range(2):
                    _, scatter = jax.vjp(lambda tab: assemble_bias(tab, r), jnp.zeros((NPAIR, CH, 2 * CH), F32))
                    db_ref[hh] += scatter(dss[hh])[0]
            dk_ref[...] = (ck[...] + ak[:TQ, :]).astype(BF16)
            dv_ref[...] = (cv[...] + av[:TQ, :]).astype(BF16)
            ck[...] = ak[TQ:, :]
            cv[...] = av[TQ:, :]

        @pl.when(i == nt)
        def _():
            dk_ref[...] = ck[...].astype(BF16)
            dv_ref[...] = cv[...].astype(BF16)

    def blk(off, prev):
        if prev:
            return pl.BlockSpec((TQ, 128), lambda p, i: (jnp.clip(i - 1, 0, nt - 1), cb + off + p))
        return pl.BlockSpec((TQ, 128), lambda p, i: (jnp.minimum(i, nt - 1), cb + off + p))

    own = pl.BlockSpec((TQ, 128), lambda p, i: (jnp.minimum(i, nt - 1), p))
    lag = pl.BlockSpec((TQ, 128), lambda p, i: (jnp.maximum(i - 1, 0), p))
    return pl.pallas_call(
        body, name="attn_bwd", grid=(BH // 2, nt + 1),
        in_specs=[blk(0, False), blk(8, True), blk(8, False), blk(16, True), blk(16, False), own,
                  pl.BlockSpec((2, TQ, KWIN), lambda p, i: (p, jnp.minimum(i, nt - 1), 0)), own],
        out_specs=[own, lag, lag, pl.BlockSpec((2, NPAIR, CH, 2 * CH), lambda p, i: (p, 0, 0, 0))],
        out_shape=[jax.ShapeDtypeStruct((t, D), BF16)] * 3 + [jax.ShapeDtypeStruct((BH, NPAIR, CH, 2 * CH), F32)],
        scratch_shapes=[pltpu.VMEM((TQ, 128), F32), pltpu.VMEM((TQ, 128), F32),
                        pltpu.VMEM((2 * TQ, 128), F32), pltpu.VMEM((2 * TQ, 128), F32)],
        compiler_params=_cparams(2),
    )(proj, proj, proj, proj, proj, ob, probs, do)


MERGE_TM = 256


def merge_fwd(x, oa, ob, proj, vecs, wa, wb, wo):
    t = x.shape[0]
    tm = MERGE_TM
    names = ("bga", "bgb", "gate_t", "g1", "b1", "scale_f", "shift_f")

    def body(x_ref, oa_ref, ob_ref, gra_ref, grb_ref, *rest):
        vrefs = rest[:7]
        wa_ref, wb_ref, wo_ref, y_ref, h_ref = rest[7:]
        vv = [r[...] for r in vrefs]
        zero = jnp.zeros((tm, D), F32)
        y1, _ = merge_fn(x_ref[...], oa_ref[...], ob_ref[...], gra_ref[...], grb_ref[...], zero, zero, zero,
                         *vv, wa_ref[...], wb_ref[...], wo_ref[...])
        y_ref[...] = y1
        h_ref[...] = (y1 * (1.0 + vv[5]) + vv[6]).astype(BF16)

    return pl.pallas_call(
        body, name="merge_fwd", grid=(t // tm,),
        in_specs=[_rows(tm, D), _rows(tm, D), _rows(tm, D), _rows(tm, D, C_GATE // D), _rows(tm, D, C_GATE // D + 1)]
        + [_const((1, D))] * 7 + [_const((D, D))] * 3,
        out_specs=[_rows(tm, D), _rows(tm, D)],
        out_shape=[jax.ShapeDtypeStruct((t, D), F32), jax.ShapeDtypeStruct((t, D), BF16)],
        compiler_params=_cparams(),
    )(x, oa, ob, proj, proj, *[vecs[n] for n in names], wa, wb, wo)


def merge_bwd(x, oa, ob, proj, vecs, wa, wb, wo, dy1):
    t = x.shape[0]
    tm = MERGE_TM
    names = ("bga", "bgb", "gate_t", "g1", "b1", "scale_f", "shift_f")

    def body(x_ref, oa_ref, ob_ref, gra_ref, grb_ref, *rest):
        vrefs = rest[:7]
        wa_ref, wb_ref, wo_ref, dy_ref = rest[7:11]
        (dx_ref, doa_ref, dob_ref, dga_ref, dgb_ref, mg_ref, dmix_ref, dpa_ref, dpb_ref,
         dbga_ref, dbgb_ref, dgt_ref, dg1_ref, db1_ref) = rest[11:]
        i = pl.program_id(0)
        vv = [r[...] for r in vrefs]
        zero = jnp.zeros((tm, D), F32)

        def f(x_, oa_, ob_, gra_, grb_, ppa, ppb, pmix, bga, bgb, gate_t, g1, b1):
            return merge_fn(x_, oa_, ob_, gra_, grb_, ppa, ppb, pmix, bga, bgb, gate_t, g1, b1, vv[5], vv[6],
                            wa_ref[...], wb_ref[...], wo_ref[...])

        _, vjp, merged = jax.vjp(f, x_ref[...], oa_ref[...].astype(F32), ob_ref[...].astype(F32),
                                 gra_ref[...], grb_ref[...], zero, zero, zero, *vv[:5], has_aux=True)
        dx, doa, dob, dga, dgb, dpa, dpb, dmix, dbga, dbgb, dgt, dg1, db1 = vjp(dy_ref[...])
        dx_ref[...] = dx
        doa_ref[...] = doa
        dob_ref[...] = dob
        dga_ref[...] = dga.astype(BF16)
        dgb_ref[...] = dgb.astype(BF16)
        mg_ref[...] = merged.astype(BF16)
        dmix_ref[...] = dmix.astype(BF16)
        dpa_ref[...] = dpa.astype(BF16)
        dpb_ref[...] = dpb.astype(BF16)
        accs = (dbga_ref, dbgb_ref, dgt_ref, dg1_ref, db1_ref)

        @pl.when(i == 0)
        def _():
            for a in accs:
                a[...] = jnp.zeros_like(a)

        for a, val in zip(accs, (dbga, dbgb, dgt, dg1, db1)):
            a[...] += val

    return pl.pallas_call(
        body, name="merge_bwd", grid=(t // tm,),
        in_specs=[_rows(tm, D), _rows(tm, D), _rows(tm, D), _rows(tm, D, C_GATE // D), _rows(tm, D, C_GATE // D + 1)]
        + [_const((1, D))] * 7 + [_const((D, D))] * 3 + [_rows(tm, D)],
        out_specs=[_rows(tm, D)] * 9 + [_const((1, D))] * 5,
        out_shape=[jax.ShapeDtypeStruct((t, D), F32)] * 3 + [jax.ShapeDtypeStruct((t, D), BF16)] * 6
        + [jax.ShapeDtypeStruct((1, D), F32)] * 5,
        compiler_params=_cparams(),
    )(x, oa, ob, proj, proj, *[vecs[n] for n in names], wa, wb, wo, dy1)


FFN_TM = 128


def ffn_act_fwd(up, conv_w, bconv):
    t, wdt = up.shape
    tm = FFN_TM

    def body(prev_ref, cur_ref, cw_ref, bc_ref, a_ref):
        i = pl.program_id(0)
        flag = jnp.where(i > 0, 1.0, 0.0)

        def ext(sl):
            return jnp.concatenate([prev_ref[:, sl] * flag, cur_ref[:, sl]], axis=0)

        def rows(sl):
            return tuple(cw_ref[j:j + 1, sl] for j in range(3))

        for cb in range(DFF // LANE):
            g = slice(cb * LANE, (cb + 1) * LANE)
            v = slice(DFF + cb * LANE, DFF + (cb + 1) * LANE)
            a_ref[:, g] = ffn_act_fn(ext(g), ext(v), rows(g), rows(v), bc_ref[:, g], bc_ref[:, v]).astype(BF16)

    return pl.pallas_call(
        body, name="ffn_act_fwd", grid=(t // tm,),
        in_specs=_halo_specs(tm, wdt, 0, lambda i: i) + [_const((3, wdt)), _const((1, wdt))],
        out_specs=_rows(tm, DFF),
        out_shape=jax.ShapeDtypeStruct((t, DFF), BF16),
        compiler_params=_cparams(),
    )(up, up, conv_w, bconv)


def ffn_act_bwd(up, conv_w, bconv, da):
    t, wdt = up.shape
    tm = FFN_TM
    nt = t // tm
    rev = lambda i: nt - 1 - i

    def body(prev_ref, cur_ref, cw_ref, bc_ref, da_ref, dup_ref, dcw_ref, dbc_ref, carry):
        i = pl.program_id(0)
        flag = jnp.where(i < nt - 1, 1.0, 0.0)

        @pl.when(i == 0)
        def _():
            carry[...] = jnp.zeros_like(carry)
            dcw_ref[...] = jnp.zeros_like(dcw_ref)
            dbc_ref[...] = jnp.zeros_like(dbc_ref)

        def ext(sl):
            return jnp.concatenate([prev_ref[:, sl] * flag, cur_ref[:, sl]], axis=0)

        def rows(sl):
            return tuple(cw_ref[j:j + 1, sl] for j in range(3))

        def emit(sl, dext, drows, dbc):
            dcur = dext[HALO:]
            dup_ref[:, sl] = jnp.concatenate([dcur[:tm - HALO], dcur[tm - HALO:] + carry[:, sl]], axis=0).astype(BF16)
            carry[:, sl] = dext[:HALO]
            dcw_ref[:, sl] += _stack_rows(drows)
            dbc_ref[:, sl] += dbc

        for cb in range(DFF // LANE):
            g = slice(cb * LANE, (cb + 1) * LANE)
            v = slice(DFF + cb * LANE, DFF + (cb + 1) * LANE)
            _, vjp = jax.vjp(ffn_act_fn, ext(g), ext(v), rows(g), rows(v), bc_ref[:, g], bc_ref[:, v])
            dxg, dxv, drg, drv, dbg, dbv = vjp(da_ref[:, g])
            emit(g, dxg, drg, dbg)
            emit(v, dxv, drv, dbv)

    return pl.pallas_call(
        body, name="ffn_act_bwd", grid=(nt,),
        in_specs=_halo_specs(tm, wdt, 0, rev) + [_const((3, wdt)), _const((1, wdt)), _rows(tm, DFF, 0, rev)],
        out_specs=[_rows(tm, wdt, 0, rev), _const((3, wdt)), _const((1, wdt))],
        out_shape=[jax.ShapeDtypeStruct((t, wdt), BF16), jax.ShapeDtypeStruct((3, wdt), F32),
                   jax.ShapeDtypeStruct((1, wdt), F32)],
        scratch_shapes=[pltpu.VMEM((HALO, wdt), F32)],
        compiler_params=_cparams(),
    )(up, up, conv_w, bconv, da)


HEAD_TM = 256


def head_fwd_bwd(a, y1, tgt, gate_f, g2, b2, wd):
    t = a.shape[0]
    tm = HEAD_TM

    def body(a_ref, y_ref, t_ref, gf_ref, g2_ref, b2_ref, wd_ref,
             da_ref, dy_ref, dffn_ref, dgf_ref, dg2_ref, db2_ref, loss_ref):
        i = pl.program_id(0)
        zero = jnp.zeros((tm, D), F32)

        def f(a_, y_, pf, gf, g2_, b2_):
            return head_fn(a_, y_, pf, gf, g2_, b2_, t_ref[...], wd_ref[...])

        loss, vjp = jax.vjp(f, a_ref[...].astype(F32), y_ref[...], zero, gf_ref[...], g2_ref[...], b2_ref[...])
        da, dy, dffn, dgf, dg2, db2 = vjp(jnp.ones((), F32))
        da_ref[...] = da
        dy_ref[...] = dy
        dffn_ref[...] = dffn.astype(BF16)
        accs = (dgf_ref, dg2_ref, db2_ref, loss_ref)

        @pl.when(i == 0)
        def _():
            for r in accs:
                r[...] = jnp.zeros_like(r)

        dgf_ref[...] += dgf
        dg2_ref[...] += dg2
        db2_ref[...] += db2
        loss_ref[...] += loss * jnp.ones((1, 128), F32)

    return pl.pallas_call(
        body, name="head_fwd_bwd", grid=(t // tm,),
        in_specs=[_rows(tm, DFF), _rows(tm, D), _rows(tm, D), _const((1, D)), _const((1, D)), _const((1, D)),
                  _const((DFF, D))],
        out_specs=[_rows(tm, DFF), _rows(tm, D), _rows(tm, D), _const((1, D)), _const((1, D)), _const((1, D)),
                   _const((1, 128))],
        out_shape=[jax.ShapeDtypeStruct((t, DFF), F32), jax.ShapeDtypeStruct((t, D), F32),
                   jax.ShapeDtypeStruct((t, D), BF16)] + [jax.ShapeDtypeStruct((1, D), F32)] * 3
        + [jax.ShapeDtypeStruct((1, 128), F32)],
        compiler_params=_cparams(),
    )(a, y1, tgt, gate_f, g2, b2, wd)


def ada_fwd(c_all, w_sh, b_sh):
    def body(c_ref, w_ref, b_ref, o_ref):
        o_ref[...] = _mmh(_silu(c_ref[...]), w_ref[...]) + b_ref[...]

    n = w_sh.shape[1]
    return pl.pallas_call(
        body, name="ada_fwd", out_shape=jax.ShapeDtypeStruct((NDEV, n), F32),
        in_specs=[pl.BlockSpec(memory_space=pltpu.VMEM)] * 3,
        out_specs=pl.BlockSpec(memory_space=pltpu.VMEM),
        compiler_params=pltpu.CompilerParams(vmem_limit_bytes=VMEM_LIMIT),
    )(c_all, w_sh, b_sh)


def ada_wgrad(c_all_t, dmod_sh):
    def body(c_ref, d_ref, o_ref):
        o_ref[...] = _mmh(_silu(c_ref[...]), d_ref[...])

    return pl.pallas_call(
        body, name="ada_wgrad", out_shape=jax.ShapeDtypeStruct((c_all_t.shape[0], dmod_sh.shape[1]), F32),
        in_specs=[pl.BlockSpec(memory_space=pltpu.VMEM)] * 2,
        out_specs=pl.BlockSpec(memory_space=pltpu.VMEM),
        compiler_params=pltpu.CompilerParams(vmem_limit_bytes=VMEM_LIMIT),
    )(c_all_t, dmod_sh)


def adamw(gparts, w, m, v, name):
    p, r, c = gparts.shape
    tr = r if r <= 256 else _pick(r, (256, 128, 64, 32, 16, 8))
    c1 = 1.0 - B1 ** STEP
    c2 = 1.0 - B2 ** STEP

    def body(g_ref, w_ref, m_ref, v_ref, go_ref, d_ref, mo_ref, vo_ref):
        g = g_ref[0].astype(F32)
        for s in range(1, p):
            g = g + g_ref[s].astype(F32)
        mn = B1 * m_ref[0] + (1.0 - B1) * g
        vn = B2 * v_ref[0] + (1.0 - B2) * (g * g)
        go_ref[0] = g
        d_ref[0] = -LR * ((mn / c1) / (jnp.sqrt(vn / c2) + AEPS) + WD * w_ref[0])
        mo_ref[0] = mn
        vo_ref[0] = vn

    spec = pl.BlockSpec((1, tr, c), lambda i: (0, i, 0))
    return pl.pallas_call(
        body, name=name, grid=(r // tr,),
        in_specs=[pl.BlockSpec((p, tr, c), lambda i: (0, i, 0)), spec, spec, spec],
        out_specs=[spec] * 4,
        out_shape=[jax.ShapeDtypeStruct((1, r, c), F32)] * 4,
        compiler_params=_cparams(),
    )(gparts, w, m, v)


def _me():
    x, y, c = lax.axis_index("x"), lax.axis_index("y"), lax.axis_index("c")
    return x, y, c, 4 * x + 2 * y + c


def _peer(x, y, c, d):
    px = 1 - x if (d >> 2) & 1 else x
    py = 1 - y if (d >> 1) & 1 else y
    pc = 1 - c if d & 1 else c
    return (px, py, pc), 4 * px + 2 * py + pc


def _exchange(arrs, name, scatter):
    n = len(arrs)

    def body(*refs):
        ins, outs = refs[:n], refs[n:2 * n]
        send, recv, lsem = refs[2 * n:]
        x, y, c, me = _me()
        remote, local = [], []
        for k in range(n):
            src = ins[k].at[me] if scatter else ins[k]
            cp = pltpu.make_async_copy(src, outs[k].at[me], lsem.at[k])
            cp.start()
            local.append(cp)
            for d in range(1, NDEV):
                dev, pid = _peer(x, y, c, d)
                src = ins[k].at[pid] if scatter else ins[k]
                cp = pltpu.make_async_remote_copy(src_ref=src, dst_ref=outs[k].at[me],
                                                  send_sem=send.at[k, d - 1], recv_sem=recv.at[k, d - 1],
                                                  device_id=dev, device_id_type=pl.DeviceIdType.MESH)
                cp.start()
                remote.append(cp)
        for cp in remote:
            cp.wait()
        for cp in local:
            cp.wait()

    shapes = [a.shape if scatter else (NDEV,) + a.shape for a in arrs]
    return pl.pallas_call(
        body, name=name,
        in_specs=[pl.BlockSpec(memory_space=pl.ANY)] * n,
        out_specs=[pl.BlockSpec(memory_space=pl.ANY)] * n,
        out_shape=[jax.ShapeDtypeStruct(s, a.dtype) for s, a in zip(shapes, arrs)],
        scratch_shapes=[pltpu.SemaphoreType.DMA((n, NDEV - 1)), pltpu.SemaphoreType.DMA((n, NDEV - 1)),
                        pltpu.SemaphoreType.DMA((n,))],
        compiler_params=pltpu.CompilerParams(has_side_effects=True),
    )(*arrs)


def all_gather(arrs, name):
    return _exchange(arrs, name, False)


def all_gather_two_level(shard, name):
    def body(x_ref, out_ref, send, recv, lsem):
        x, y, c, _ = _me()
        sibling = (x, y, 1 - c)
        chips = [(1 - x, y), (x, 1 - y), (1 - x, 1 - y)]

        def slot(px, py, pc):
            return out_ref.at[4 * px + 2 * py + pc]

        def copy(k, block, to, src=None):
            return pltpu.make_async_remote_copy(
                src_ref=slot(*block) if src is None else src, dst_ref=slot(*block),
                send_sem=send.at[k], recv_sem=recv.at[k], device_id=to, device_id_type=pl.DeviceIdType.MESH)

        mine = pltpu.make_async_copy(x_ref, slot(x, y, c), lsem)
        mine.start()
        first = [copy(0, (x, y, c), sibling, src=x_ref)]
        first += [copy(1 + j, (x, y, c), (*chip, c), src=x_ref) for j, chip in enumerate(chips)]
        for cp in first:
            cp.start()
        passed = [copy(4 + j, (*chip, c), sibling) for j, chip in enumerate(chips)]
        for j, chip in enumerate(chips):
            copy(1 + j, (*chip, c), (x, y, c)).wait_recv()
            passed[j].start()
        copy(0, sibling, (x, y, c)).wait_recv()
        for j, chip in enumerate(chips):
            copy(4 + j, (*chip, 1 - c), (x, y, c)).wait_recv()
        for cp in first + passed:
            cp.wait_send()
        mine.wait()

    return pl.pallas_call(
        body, name=name,
        in_specs=[pl.BlockSpec(memory_space=pl.ANY)],
        out_specs=pl.BlockSpec(memory_space=pl.ANY),
        out_shape=jax.ShapeDtypeStruct((NDEV,) + shard.shape, shard.dtype),
        scratch_shapes=[pltpu.SemaphoreType.DMA((NPEER,)), pltpu.SemaphoreType.DMA((NPEER,)),
                        pltpu.SemaphoreType.DMA],
        compiler_params=pltpu.CompilerParams(has_side_effects=True),
    )(shard)


def all_to_all(arrs, name):
    return _exchange(arrs, name, True)


_HBM = pl.BlockSpec(memory_space=pltpu.HBM)
_SEM = pl.BlockSpec(memory_space=pltpu.SEMAPHORE)
_EFFECT = pltpu.SideEffectType.DATAFLOW_SIDE_EFFECTING
NPEER = NDEV - 1


def exchange_start(arrs, name, scatter):
    n = len(arrs)
    lands = [lax.empty(a.shape if scatter else (NDEV,) + a.shape, a.dtype) for a in arrs]

    def body(*refs):
        ins, lrefs = refs[:n], refs[n:2 * n]
        send, recv, token = refs[2 * n], refs[2 * n + 1], refs[-1]
        x, y, c, me = _me()
        for k in range(n):
            for d in range(1, NDEV):
                dev, pid = _peer(x, y, c, d)
                src = ins[k].at[pid] if scatter else ins[k]
                pltpu.make_async_remote_copy(src_ref=src, dst_ref=lrefs[k].at[me],
                                             send_sem=send.at[k * NPEER + d - 1], recv_sem=recv.at[k * NPEER + d - 1],
                                             device_id=dev, device_id_type=pl.DeviceIdType.MESH).start()
        token[...] = jnp.zeros_like(token)

    thru = [pltpu.HBM(a.shape, a.dtype) for a in list(arrs) + lands]
    outs = pl.pallas_call(
        body, name=name,
        out_shape=(pltpu.SemaphoreType.DMA((n * NPEER,)), pltpu.SemaphoreType.DMA((n * NPEER,)), *thru,
                   jax.ShapeDtypeStruct((8, 128), F32)),
        in_specs=[_HBM] * (2 * n),
        out_specs=(_SEM, _SEM, *([_HBM] * (2 * n)), pl.BlockSpec(memory_space=pltpu.VMEM)),
        input_output_aliases={i: 2 + i for i in range(2 * n)},
        compiler_params=pltpu.CompilerParams(has_side_effects=_EFFECT),
    )(*[pltpu.with_memory_space_constraint(a, pltpu.HBM) for a in list(arrs) + lands])
    handle = dict(send=outs[0], recv=outs[1], src=list(outs[2:2 + n]), land=list(outs[2 + n:2 + 2 * n]),
                  scatter=scatter)
    return handle, outs[-1][0, 0]


def exchange_wait(handle, after, name):
    n = len(handle["src"])
    scatter = handle["scatter"]

    def body(*refs):
        ins, lrefs = refs[:n], refs[n:2 * n]
        send, recv = refs[2 * n], refs[2 * n + 1]
        x, y, c, _ = _me()
        for k in range(n):
            for d in range(1, NDEV):
                dev, _ = _peer(x, y, c, d)
                src = ins[k].at[0] if scatter else ins[k]
                cp = pltpu.make_async_remote_copy(src_ref=src, dst_ref=lrefs[k].at[0],
                                                  send_sem=send.at[k * NPEER + d - 1],
                                                  recv_sem=recv.at[k * NPEER + d - 1],
                                                  device_id=dev, device_id_type=pl.DeviceIdType.MESH)
                cp.wait_send()
                cp.wait_recv()

    arrs = handle["src"] + handle["land"]
    outs = pl.pallas_call(
        body, name=name,
        out_shape=tuple(pltpu.HBM(a.shape, a.dtype) for a in arrs),
        in_specs=[_HBM] * (2 * n) + [_SEM, _SEM, pl.BlockSpec(memory_space=pl.ANY)],
        out_specs=tuple([_HBM] * (2 * n)),
        input_output_aliases={i: i for i in range(2 * n)},
        compiler_params=pltpu.CompilerParams(has_side_effects=_EFFECT),
    )(*arrs, handle["send"], handle["recv"], after)
    me = 4 * lax.axis_index("x") + 2 * lax.axis_index("y") + lax.axis_index("c")
    landed = []
    for own, land in zip(outs[:n], outs[n:]):
        mine = lax.dynamic_index_in_dim(own, me, 0, keepdims=True) if scatter else own[None]
        landed.append(lax.dynamic_update_slice_in_dim(land, mine, me, 0))
    return landed


def _cat_from_slabs(slabs):
    _, k, n = slabs.shape

    def cols(lo, hi):
        parts, c = [], lo
        while c < hi:
            j = c // n
            e = min(hi, (j + 1) * n)
            parts.append(slabs[j][:, c - j * n:e - j * n])
            c = e
        return parts

    def zeros(w):
        return [jnp.zeros((k, w), slabs.dtype)]

    return jnp.concatenate(cols(0, 4096) + cols(4112, 9232) + cols(4096, 4104) + zeros(LANE - AH)
                           + cols(4104, 4112) + zeros(NCAT - C_BA - LANE - AH), axis=1)


IN_PIECES = (("pre", C_QKVA, 3072), ("z", C_Z, 1024), ("qb", C_QKVB, 1024), ("kb", C_QKVB + 1024, 1024),
             ("vb", C_QKVB + 2048, 1024), ("ga", C_GATE, 1024), ("gb", C_GATE + 1024, 1024))
_ORIG_SEGS = ((0, 3072, "pre", 0), (3072, 4096, "z", 0), (4096, 4104, "ba", 0), (4104, 4112, "ba", LANE),
              (4112, 5136, "qb", 0), (5136, 6160, "kb", 0), (6160, 7184, "vb", 0), (7184, 8208, "ga", 0),
              (8208, 9232, "gb", 0))


def _orig_cols_from_pieces(gp, lo, hi):
    parts = []
    for a, b, name, off in _ORIG_SEGS:
        s, e = max(a, lo), min(b, hi)
        if s < e:
            parts.append(gp[name][:, off + s - a:off + e - a])
    return parts[0] if len(parts) == 1 else jnp.concatenate(parts, axis=1)


def _pad128(v):
    return jnp.pad(v, ((0, 0), (0, 128 - v.shape[1])))


def local_step(x, tgt, mod, wts, small, late_weights=None, on_grads=None):
    if on_grads is None:
        on_grads = lambda group, gd: jnp.zeros((), F32)
    t = x.shape[0]
    nc = t // CH
    shift_t, scale_t, gate_t, shift_f, scale_f, gate_f = mod
    wcat = _cat_from_slabs(wts["w_in_slabs"])
    a_log = _pad128(small["a_log"])
    dtb = _pad128(small["dt_bias"])
    vecs = dict(bga=small["b_gate"][:, :D], bgb=small["b_gate"][:, D:], gate_t=gate_t, g1=small["ln1_g"],
                b1=small["ln1_b"], scale_f=scale_f, shift_f=shift_f)

    h1 = modulate(x, scale_t, shift_t, "modulate_t")
    proj = matmul(h1, wcat, F32, "in_proj")
    q, k, v, gcs, beta = prep_fwd(proj, small["conv_a"], a_log, dtb)

    u, w, qg, kd, qk, eg, tinv = c1_fwd(q, k, v, gcs, beta)
    oa, sall = c2_fwd(u, w, qg, kd, qk, eg, proj, small["norm_a"])
    bias = bias_table(small["rel_bias"])
    ob, probs = attn_fwd(proj, bias)
    if late_weights is not None:
        wts = {**wts, **late_weights(ob)}
    y1, h2 = merge_fwd(x, oa, ob, proj, vecs, wts["w_a"], wts["w_b"], wts["w_o"])
    up = matmul(h2, wts["w_up"], F32, "up_proj")
    a = ffn_act_fwd(up, small["conv_ffn"], small["b_conv_ffn"])

    da, dy1_res, dffn, dgate_f, dg2, db2, loss = head_fwd_bwd(a, y1, tgt, gate_f, small["ln2_g"], small["ln2_b"],
                                                            wts["w_down"])
    g_w_down = matmul(a, dffn, BF16, "wgrad_down", ta=True)
    dup, g_conv_ffn, g_bconv = ffn_act_bwd(up, small["conv_ffn"], small["b_conv_ffn"], da)
    dh2 = matmul(dup, wts["w_up"], F32, "dgrad_up", tb=True)
    g_w_up = matmul(h2, dup, BF16, "wgrad_up", ta=True)
    tok = on_grads("ffn", dict(w_up=g_w_up, w_down=g_w_down))
    dy1, dscale_f, dshift_f = modulate_bwd(dh2, y1, dy1_res, scale_f + tok, "modulate_f_bwd")
    (dx_res, doa, dob, dga, dgb, merged, dmix, dpa, dpb,
     dbga, dbgb, dgate_t, dg1, db1) = merge_bwd(x, oa, ob, proj, vecs, wts["w_a"], wts["w_b"], wts["w_o"], dy1)
    g_w_o = matmul(merged, dmix, BF16, "wgrad_o", ta=True)
    g_w_a = matmul(oa, dpa, BF16, "wgrad_a", ta=True)
    g_w_b = matmul(ob, dpb, BF16, "wgrad_b", ta=True)
    tok = on_grads("mix", dict(w_o=g_w_o, w_a=g_w_a, w_b=g_w_b))
    dqb, dkb, dvb, dbias = attn_bwd(proj, ob, probs, dob)
    g_rel = relbias_reduce(bias_table_bwd_layout(dbias))
    du, dw, dqg, dkd, dqk, deg, dz, g_norm = c2_bwd(u, w, qg, kd, qk, eg, proj, small["norm_a"] + tok, sall, doa)
    dq, dk, dv, dgcs, dbeta = c1_bwd(q, k, v, gcs, beta, tinv, du, dw, dqg, dkd, dqk, deg)
    dpre, dbb, daa, g_conv_a, g_alog, g_dtb = prep_bwd(proj, small["conv_a"], a_log, dtb, dq, dk, dv, dgcs, dbeta)
    tok = on_grads("small", dict(conv_a=g_conv_a, rel_bias=g_rel, conv_ffn=g_conv_ffn))
    dba = jnp.concatenate([dbb, daa, jnp.zeros((t, NCAT - C_BA - 2 * LANE), BF16)], axis=1) + tok.astype(BF16)
    dpieces = dict(pre=dpre, z=dz, qb=dqb, kb=dkb, vb=dvb, ga=dga, gb=dgb)
    g_in = {n: matmul(h1, dpieces[n], BF16, "wgrad_in_" + n, ta=True) for n, _, _ in IN_PIECES}
    g_in["ba"] = matmul(h1, dba, BF16, "wgrad_in_ba", ta=True)
    tok = on_grads("in", g_in)
    dh1 = dgrad_pieces([(dpieces[n], off) for n, off, _ in IN_PIECES], dba + tok.astype(BF16), wcat,
                       "dgrad_in")
    grad_x, dscale_t, dshift_t = modulate_bwd(dh1, x, dx_res, scale_t + tok, "modulate_t_bwd")

    dmod = (dshift_t, dscale_t, dgate_t, dshift_f, dscale_f, dgate_f)
    grads = dict(w_in=_orig_cols_from_pieces(g_in, 0, 9232), w_up=g_w_up, w_down=g_w_down, w_a=g_w_a, w_b=g_w_b, w_o=g_w_o,
                 conv_a=g_conv_a, rel_bias=g_rel, conv_ffn=g_conv_ffn,
                 b_gate=jnp.concatenate([dbga, dbgb], axis=1), a_log=g_alog[:, :AH], dt_bias=g_dtb[:, :AH],
                 norm_a=g_norm, ln1_g=dg1, ln1_b=db1, b_conv_ffn=g_bconv, ln2_g=dg2, ln2_b=db2)
    return loss[0, 0], grad_x, dmod, grads


_REP = {}
_off = 0
for _n, _wd, _pw in (("b_ada", 6144, 6144), ("b_gate", 2048, 2048), ("a_log", 8, 128), ("dt_bias", 8, 128),
                     ("norm_a", 128, 128), ("ln1_g", 1024, 1024), ("ln1_b", 1024, 1024),
                     ("b_conv_ffn", 5632, 5632), ("ln2_g", 1024, 1024), ("ln2_b", 1024, 1024), ("loss", 1, 128)):
    _REP[_n] = (_off, _wd, _pw)
    _off += _pw
REP_LEN = _off
REP_NAMES = [n for n in _REP if n != "loss"]
_SH = (("conv_a", (4, 384)), ("rel_bias", (16, 40)), ("conv_ffn", (3, 704)))
SH_LEN = 4352


def _pack_rep(vals):
    parts = []
    for n, (_, wd, pw) in _REP.items():
        a = vals.get(n)
        a = jnp.zeros((1, pw), F32) if a is None else jnp.pad(a.reshape(1, wd), ((0, 0), (0, pw - wd)))
        parts.append(a)
    return jnp.concatenate(parts, axis=1)


def _unpack_rep(vec, name):
    o, wd, _ = _REP[name]
    return vec[:, o:o + wd]


def _pack_sh(vals):
    parts = [vals[n].reshape(vals[n].shape[:-2] + (-1,)) for n, _ in _SH]
    a = jnp.concatenate(parts, axis=-1)
    return jnp.pad(a, [(0, 0)] * (a.ndim - 1) + [(0, SH_LEN - a.shape[-1])])


def _unpack_sh(vec, name):
    o = 0
    for n, shp in _SH:
        sz = shp[0] * shp[1]
        if n == name:
            return vec[0, o:o + sz].reshape(shp)
        o += sz
    raise KeyError(name)


def _col_shards(a, n):
    return a.reshape(a.shape[0], NDEV, n).transpose(1, 0, 2)


def kernel(x, c, w_ada, b_ada, w_in, b_gate, conv_a, a_log, dt_bias, norm_a, rel_bias, w_branch_a, w_branch_b, w_o, ln1_g, ln1_b, w_up, conv_ffn, b_conv_ffn, w_down, ln2_g, ln2_b, loss_target, m_w_ada, m_b_ada, m_w_in, m_b_gate, m_conv_a, m_a_log, m_dt_bias, m_norm_a, m_rel_bias, m_w_branch_a, m_w_branch_b, m_w_o, m_ln1_g, m_ln1_b, m_w_up, m_conv_ffn, m_b_conv_ffn, m_w_down, m_ln2_g, m_ln2_b, v_w_ada, v_b_ada, v_w_in, v_b_gate, v_conv_a, v_a_log, v_dt_bias, v_norm_a, v_rel_bias, v_w_branch_a, v_w_branch_b, v_w_o, v_ln1_g, v_ln1_b, v_w_up, v_conv_ffn, v_b_conv_ffn, v_w_down, v_ln2_g, v_ln2_b):
    W = dict(w_ada=w_ada, b_ada=b_ada, w_in=w_in, b_gate=b_gate, conv_a=conv_a, a_log=a_log, dt_bias=dt_bias,
             norm_a=norm_a, rel_bias=rel_bias, w_branch_a=w_branch_a, w_branch_b=w_branch_b, w_o=w_o, ln1_g=ln1_g,
             ln1_b=ln1_b, w_up=w_up, conv_ffn=conv_ffn, b_conv_ffn=b_conv_ffn, w_down=w_down, ln2_g=ln2_g,
             ln2_b=ln2_b)
    M = dict(w_ada=m_w_ada, b_ada=m_b_ada, w_in=m_w_in, b_gate=m_b_gate, conv_a=m_conv_a, a_log=m_a_log,
             dt_bias=m_dt_bias, norm_a=m_norm_a, rel_bias=m_rel_bias, w_branch_a=m_w_branch_a,
             w_branch_b=m_w_branch_b, w_o=m_w_o, ln1_g=m_ln1_g, ln1_b=m_ln1_b, w_up=m_w_up, conv_ffn=m_conv_ffn,
             b_conv_ffn=m_b_conv_ffn, w_down=m_w_down, ln2_g=m_ln2_g, ln2_b=m_ln2_b)
    V = dict(w_ada=v_w_ada, b_ada=v_b_ada, w_in=v_w_in, b_gate=v_b_gate, conv_a=v_conv_a, a_log=v_a_log,
             dt_bias=v_dt_bias, norm_a=v_norm_a, rel_bias=v_rel_bias, w_branch_a=v_w_branch_a,
             w_branch_b=v_w_branch_b, w_o=v_w_o, ln1_g=v_ln1_g, ln1_b=v_ln1_b, w_up=v_w_up, conv_ffn=v_conv_ffn,
             b_conv_ffn=v_b_conv_ffn, w_down=v_w_down, ln2_g=v_ln2_g, ln2_b=v_ln2_b)
    W3, M3, V3 = W, M, V
    W, M, V = ({n: a[0] for n, a in dct.items()} for dct in (W, M, V))
    me = 4 * lax.axis_index("x") + 2 * lax.axis_index("y") + lax.axis_index("c")
    big = ("w_in", "w_up", "w_down", "w_branch_a", "w_branch_b", "w_o")

    g_in = all_gather_two_level(W["w_in"].astype(BF16), "gather_w_in")
    wts = dict(w_in_slabs=g_in)
    c_all, sh_all = all_gather([c, _pack_sh({n: W[n] for n, _ in _SH})[None]], "gather_small")
    c_all = c_all.reshape(NDEV, D)
    sh_all = sh_all.reshape(NDEV, SH_LEN)

    def full_small(name, shp):
        o = 0
        for n, s in _SH:
            if n == name:
                break
            o += s[0] * s[1]
        sz = shp[0] * shp[1]
        return sh_all[:, o:o + sz].reshape(NDEV, shp[0], shp[1]).transpose(1, 0, 2).reshape(shp[0], NDEV * shp[1])

    small = dict(conv_a=full_small("conv_a", (4, 384)), rel_bias=full_small("rel_bias", (16, 40)),
                 conv_ffn=full_small("conv_ffn", (3, 704)),
                 b_gate=W["b_gate"][None], a_log=W["a_log"][None], dt_bias=W["dt_bias"][None],
                 norm_a=W["norm_a"][None], ln1_g=W["ln1_g"][None], ln1_b=W["ln1_b"][None],
                 b_conv_ffn=W["b_conv_ffn"][None], ln2_g=W["ln2_g"][None], ln2_b=W["ln2_b"][None])

    nsh = w_ada.shape[2]
    b_sh = lax.dynamic_slice(W["b_ada"][None], (0, me * nsh), (1, nsh))
    mod_sh = ada_fwd(c_all, W["w_ada"], b_sh)
    (mod_rows,) = all_to_all([mod_sh[:, None, :]], "scatter_mod")
    mod6 = mod_rows.reshape(6, D)

    after_small = (g_in[0, 0, 0].astype(F32) * 0.0 + mod6[0, 0] * 0.0).astype(BF16)
    late, late_tok = exchange_start([W[n].astype(BF16) + after_small for n in big[1:]], "gather_late_start", False)

    def late_weights(after):
        g_up, g_down, g_a, g_b, g_o = exchange_wait(late, after, "gather_late_wait")
        return dict(w_up=g_up.transpose(1, 0, 2).reshape(D, -1), w_down=g_down.reshape(DFF, D),
                    w_a=g_a.reshape(D, D), w_b=g_b.reshape(D, D), w_o=g_o.reshape(D, D))

    mod6 = mod6 + late_tok
    mod = tuple(mod6[i:i + 1] for i in range(6))

    pending = {}

    def on_grads(group, gd):
        if group == "small":
            sh_parts = {"conv_a": _col_shards(gd["conv_a"], 384), "rel_bias": _col_shards(gd["rel_bias"], 40),
                        "conv_ffn": _col_shards(gd["conv_ffn"], 704)}
            (pending["small"],) = all_to_all([_pack_sh(sh_parts)[:, None, :]], "scatter_small_grads")
            return pending["small"][0, 0, 0] * 0.0
        if group == "ffn":
            slabs = [_col_shards(gd["w_up"], w_up.shape[2]), gd["w_down"].reshape(NDEV, -1, D)]
        elif group == "mix":
            slabs = [gd[n].reshape(NDEV, -1, D) for n in ("w_a", "w_b", "w_o")]
        else:
            nin = w_in.shape[2]
            slabs = [jnp.stack([_orig_cols_from_pieces(gd, j * nin, (j + 1) * nin) for j in range(NDEV)], axis=0)]
        pending[group], tok = exchange_start([s.astype(BF16) for s in slabs], "scatter_" + group + "_start", True)
        return tok

    loss, grad_x, dmod, g = local_step(x[0], loss_target[0], mod, wts, small, late_weights, on_grads)

    rep_vals = {n: g[n] for n in REP_NAMES if n != "b_ada"}
    rep_vals["b_ada"] = jnp.concatenate(dmod, axis=1)
    rep_vals["loss"] = loss.reshape(1, 1)
    (rep_all,) = all_gather([_pack_rep(rep_vals)[None]], "gather_small_grads")
    rep_all = rep_all.reshape(NDEV, 1, REP_LEN)
    zero1 = jnp.zeros((1, 1), F32)
    rep_out = adamw(rep_all, _pack_rep({**{n: W[n][None] for n in REP_NAMES}, "loss": zero1})[None],
                    _pack_rep({**{n: M[n][None] for n in REP_NAMES}, "loss": zero1})[None],
                    _pack_rep({**{n: V[n][None] for n in REP_NAMES}, "loss": zero1})[None], "adamw_small")
    rep_out = [o[0] for o in rep_out]
    loss_total = _unpack_rep(rep_out[0], "loss")[0, 0]

    o_ada = _REP["b_ada"][0]
    dmod_all = rep_all[:, 0, o_ada:o_ada + 6 * D]
    dmod_sh = lax.dynamic_slice(dmod_all, (0, me * nsh), (NDEV, nsh))
    g_w_ada = ada_wgrad(c_all.T, dmod_sh)

    p_up, p_down = exchange_wait(pending["ffn"], grad_x, "scatter_ffn_wait")
    p_a, p_b, p_o = exchange_wait(pending["mix"], grad_x, "scatter_mix_wait")
    (p_in,) = exchange_wait(pending["in"], grad_x, "scatter_in_wait")
    parts = [p_in, p_up, p_down, p_a, p_b, p_o]
    sh_recv = pending["small"]

    res = {}
    for n, p in zip(big, parts):
        res[n] = adamw(p, W3[n], M3[n], V3[n], "adamw_" + n)
    res["w_ada"] = adamw(g_w_ada[None], W3["w_ada"], M3["w_ada"], V3["w_ada"], "adamw_w_ada")
    sh_out = adamw(sh_recv, _pack_sh({n: W[n] for n, _ in _SH})[None, None],
                   _pack_sh({n: M[n] for n, _ in _SH})[None, None],
                   _pack_sh({n: V[n] for n, _ in _SH})[None, None], "adamw_small_sharded")
    for n, _ in _SH:
        res[n] = tuple(_unpack_sh(o[0], n)[None] for o in sh_out)
    for n in REP_NAMES:
        res[n] = tuple(_unpack_rep(o, n) for o in rep_out)

    order = ("w_ada", "b_ada", "w_in", "b_gate", "conv_a", "a_log", "dt_bias", "norm_a", "rel_bias", "w_branch_a",
             "w_branch_b", "w_o", "ln1_g", "ln1_b", "w_up", "conv_ffn", "b_conv_ffn", "w_down", "ln2_g", "ln2_b")
    outs = [loss_total, grad_x[None]]
    for kind in range(4):
        outs += [res[n][kind] for n in order]
    return tuple(outs)
```

```python
import functools
import math

import numpy as np
import jax
import jax.numpy as jnp
from jax import lax
from jax.experimental import pallas as pl
from jax.experimental.pallas import tpu as pltpu

F32 = jnp.float32
BF16 = jnp.bfloat16
HI = lax.Precision.HIGHEST

D = 1024
CH = 64
AH, ADK = 8, 128
BH, BDH = 16, 64
BPREV = 8
BMAXREL = 256
RELSZ = CH + BMAXREL
DFF = 2816
ALPHA = 2.0 ** 0.25
LN_EPS, RMS_EPS, L2_EPS = 1e-5, 1e-6, 1e-6
NEG = -1e30
LR, B1, B2, AEPS, WD, STEP = 1e-3, 0.9, 0.999, 1e-8, 0.01, 10
NDEV = 8
HALO = 8
LANE = 128
TQ = 512
VMEM_LIMIT = 56 * 1024 * 1024

C_QKVA, C_Z, C_QKVB, C_GATE, C_BA, NCAT = 0, 3072, 4096, 7168, 9216, 9728


def _cparams(n_axes=1, vmem=VMEM_LIMIT):
    return pltpu.CompilerParams(dimension_semantics=("arbitrary",) * n_axes, vmem_limit_bytes=vmem)


def _dg(a, b, ca, cb):
    return lax.dot_general(a.astype(BF16), b.astype(BF16), (((ca,), (cb,)), ((), ())),
                           preferred_element_type=F32)


@jax.custom_vjp
def mm_nn(a, b):
    return _dg(a, b, 1, 0)


@jax.custom_vjp
def mm_nt(a, b):
    return _dg(a, b, 1, 1)


@jax.custom_vjp
def mm_tn(a, b):
    return _dg(a, b, 0, 0)


mm_nn.defvjp(lambda a, b: (mm_nn(a, b), (a, b)),
             lambda r, g: (mm_nt(g, r[1]).astype(r[0].dtype), mm_tn(r[0], g).astype(r[1].dtype)))
mm_nt.defvjp(lambda a, b: (mm_nt(a, b), (a, b)),
             lambda r, g: (mm_nn(g, r[1]).astype(r[0].dtype), mm_tn(g, r[0]).astype(r[1].dtype)))
mm_tn.defvjp(lambda a, b: (mm_tn(a, b), (a, b)),
             lambda r, g: (mm_nt(r[1], g).astype(r[0].dtype), mm_nn(r[0], g).astype(r[1].dtype)))


@jax.custom_vjp
def mm_w(a, w):
    return _dg(a, w, 1, 0)


mm_w.defvjp(lambda a, w: (mm_w(a, w), (a, w)),
            lambda r, g: (mm_nt(g, r[1]).astype(r[0].dtype), jnp.zeros_like(r[1])))


def _mmh(a, b):
    return lax.dot_general(a, b, (((1,), (0,)), ((), ())), precision=HI, preferred_element_type=F32)


def _bdg(a, b, ca, cb):
    return lax.dot_general(a.astype(BF16), b.astype(BF16), (((ca,), (cb,)), ((0,), (0,))),
                           preferred_element_type=F32)


@jax.custom_vjp
def bmm_nn(a, b):
    return _bdg(a, b, 2, 1)


@jax.custom_vjp
def bmm_nt(a, b):
    return _bdg(a, b, 2, 2)


@jax.custom_vjp
def bmm_tn(a, b):
    return _bdg(a, b, 1, 1)


bmm_nn.defvjp(lambda a, b: (bmm_nn(a, b), (a, b)), lambda r, g: (bmm_nt(g, r[1]), bmm_tn(r[0], g)))
bmm_nt.defvjp(lambda a, b: (bmm_nt(a, b), (a, b)), lambda r, g: (bmm_nn(g, r[1]), bmm_tn(g, r[0])))
bmm_tn.defvjp(lambda a, b: (bmm_tn(a, b), (a, b)), lambda r, g: (bmm_nt(r[1], g), bmm_nn(r[0], g)))


def _bdg3(a, b, ca, cb):
    return lax.dot_general(a, b, (((ca,), (cb,)), ((0,), (0,))), precision=HI, preferred_element_type=F32)


def _bdgp(a, b, ca, cb):
    return _bdg(a, b, ca, cb)


@jax.custom_vjp
def bmm3_nn(a, b):
    return _bdgp(a, b, 2, 1)


bmm3_nn.defvjp(lambda a, b: (bmm3_nn(a, b), (a, b)),
               lambda r, g: (_bdgp(g, r[1], 2, 2), _bdgp(r[0], g, 1, 1)))


def _sigmoid(x):
    return 0.5 * jnp.tanh(0.5 * x) + 0.5


def _silu(x):
    return x * _sigmoid(x)


def _softplus(x):
    return jnp.maximum(x, 0.0) + jnp.log(1.0 + jnp.exp(-jnp.abs(x)))


def _layernorm(r, g, b):
    mu = jnp.mean(r, axis=-1, keepdims=True)
    xc = r - mu
    var = jnp.mean(xc * xc, axis=-1, keepdims=True)
    return xc * lax.rsqrt(var + LN_EPS) * g + b


def _iota2(shape, dim):
    return lax.broadcasted_iota(jnp.int32, shape, dim)


@jax.custom_vjp
def causal_conv(ext, rows):
    k = len(rows)
    y = None
    for j in range(k):
        s = k - 1 - j
        r = pltpu.roll(ext, s, 0) if s else ext
        t = r[HALO:] * rows[j]
        y = t if y is None else y + t
    return y


def _causal_conv_fwd(ext, rows):
    return causal_conv(ext, rows), (ext, rows)


def _causal_conv_bwd(res, g):
    ext, rows = res
    n = ext.shape[0]
    k = len(rows)
    gext = jnp.concatenate([jnp.zeros((HALO, g.shape[1]), g.dtype), g], axis=0)
    dext = None
    drows = []
    for j in range(k):
        s = k - 1 - j
        up = pltpu.roll(gext, n - s, 0) if s else gext
        t = up * rows[j]
        dext = t if dext is None else dext + t
        r = pltpu.roll(ext, s, 0) if s else ext
        drows.append(jnp.sum(g * r[HALO:], axis=0, keepdims=True))
    return dext, tuple(drows)


causal_conv.defvjp(_causal_conv_fwd, _causal_conv_bwd)


def _chunk_masks(tm):
    i = _iota2((tm, tm), 0)
    j = _iota2((tm, tm), 1)
    same = (i ^ j) < CH
    lower = jnp.where(same & (j <= i), 1.0, 0.0).astype(F32)
    upper = jnp.where(same & (i <= j), 1.0, 0.0).astype(F32)
    return lower, upper


@jax.custom_vjp
def chunk_cumsum(g):
    lower, _ = _chunk_masks(g.shape[0])
    return _mmh(lower, g)


def _chunk_cumsum_bwd(_, ct):
    _, upper = _chunk_masks(ct.shape[0])
    return (_mmh(upper, ct),)


chunk_cumsum.defvjp(lambda g: (chunk_cumsum(g), None), _chunk_cumsum_bwd)


@jax.custom_vjp
def inv_unit_lower(a):
    n = a.shape[-1]
    eye = jnp.where(_iota2((1, n, n), 1) == _iota2((1, n, n), 2), 1.0, 0.0).astype(F32)
    x = eye - a
    p = _bdg3(a, a, 2, 1)
    steps = int(math.log2(n)) - 1
    for s in range(steps):
        x = x + _bdg3(x, p, 2, 1)
        if s + 1 < steps:
            p = _bdg3(p, p, 2, 1)
    return x


def _inv_fwd(a):
    t = inv_unit_lower(a)
    return t, t


def _inv_bwd(t, g):
    return (-_bdgp(_bdgp(t, g, 1, 1), t, 2, 2),)


inv_unit_lower.defvjp(_inv_fwd, _inv_bwd)


@jax.custom_vjp
def inv_known(a, t):
    return t


inv_known.defvjp(lambda a, t: (t, t), lambda t, g: (_inv_bwd(t, g)[0], jnp.zeros_like(t)))


def prep_head_fn(ext, rows, scale):
    s = _silu(causal_conv(ext, rows))
    if scale is None:
        return s
    return s * (lax.rsqrt(jnp.sum(s * s, axis=-1, keepdims=True) + L2_EPS) * scale)


def prep_gate_fn(bb, aa, a_log, dtb):
    g = -jnp.exp(a_log) * _softplus(aa + dtb)
    return chunk_cumsum(g), _sigmoid(bb)


PREP_SCALES = (ADK ** -0.5, 1.0, None)


def _head_cols(a):
    lane = _iota2((1, LANE), 1)
    return jnp.concatenate([jnp.sum(jnp.where(lane == h, a, 0.0), axis=1, keepdims=True)[None]
                            for h in range(AH)], axis=0)


def _head_rows(a):
    at = a.T[:AH]
    sub = _iota2((AH, 1), 0)
    return jnp.concatenate([jnp.sum(jnp.where(sub == h, at, 0.0), axis=0, keepdims=True)[None]
                            for h in range(AH)], axis=0)


def c1_heads(q, k, v, gcs, beta, tinv_saved=None):
    gcol = _head_cols(gcs)
    grow = _head_rows(gcs)
    bcol = _head_cols(beta)
    i = _iota2((1, CH, CH), 1)
    j = _iota2((1, CH, CH), 2)
    causal = j <= i
    strict = j < i
    diff = gcol - grow
    decay = jnp.where(causal, jnp.exp(jnp.where(causal, diff, 0.0)), 0.0)
    kb = k * bcol
    vb = v * bcol
    a_low = jnp.where(strict, bmm_nt(kb, k) * decay, 0.0)
    tinv = inv_unit_lower(a_low) if tinv_saved is None else inv_known(a_low, tinv_saved)
    egc = jnp.exp(gcol)
    u = bmm3_nn(tinv, vb)
    w = bmm3_nn(tinv, kb * egc)
    qk = jnp.where(causal, bmm_nt(q, k) * decay, 0.0)
    glast = jnp.sum(jnp.where(_iota2((1, CH, 1), 1) == CH - 1, gcol, 0.0), axis=1, keepdims=True)
    qg = q * egc
    kd = k * jnp.exp(glast - gcol)
    eg = jnp.exp(glast) * jnp.ones((1, 1, ADK), F32)
    return u, w, qk, qg, kd, eg, tinv


def c2_heads(s, u, w, qk, qg, kd, eg, z, nw):
    vn = u - bmm_nn(w, s)
    o = bmm_nn(qg, s) + bmm_nn(qk, vn)
    s2 = s * eg + bmm_tn(kd, vn)
    ms = jnp.mean(o * o, axis=-1, keepdims=True)
    og = o * lax.rsqrt(ms + RMS_EPS) * nw * _silu(z)
    return og, s2


ATT_SCALE = BDH ** -0.5


def _head_mask(hh):
    lane = _iota2((1, 2 * BDH), 1)
    return jnp.where((lane >= hh * BDH) & (lane < (hh + 1) * BDH), 1.0, 0.0).astype(F32)


def attn_sub_fwd(q, k, v, bias2, r, firstf):
    col = _iota2((1, KWIN), 1) + r * SUBQ
    nokey = jnp.where(col < TQ, firstf, 0.0) * NEG
    out, probs = None, []
    for hh in range(2):
        hm = _head_mask(hh)
        s = mm_nt(q * (hm * ATT_SCALE), k) + (assemble_bias(bias2[hh], r) + nokey)
        p = jnp.exp(s - jnp.max(s, axis=-1, keepdims=True))
        inv = 1.0 / jnp.sum(p, axis=-1, keepdims=True)
        o = mm_nn(p, v) * (inv * hm)
        out = o if out is None else out + o
        probs.append(p * inv)
    return out, probs


def attn_sub_bwd(q, k, v, o, do, probs, r):
    dq, dk, dv, dss = None, None, None, []
    for hh in range(2):
        hm = _head_mask(hh)
        p = probs[hh]
        doh = do * hm
        ds = p * (mm_nt(doh, v) - jnp.sum(doh * o, axis=-1, keepdims=True))
        dqh = mm_nn(ds, k) * (hm * ATT_SCALE)
        dkh = mm_tn(ds, q * (hm * ATT_SCALE))
        dvh = mm_tn(p, doh)
        dq = dqh if dq is None else dq + dqh
        dk = dkh if dk is None else dk + dkh
        dv = dvh if dv is None else dv + dvh
        dss.append(ds)
    return dq, dk, dv, dss


def merge_fn(x, oa, ob, gra, grb, p_pa, p_pb, p_mix, bga, bgb, gate_t, g1, b1, scale_f, shift_f,
             wa, wb, wo):
    ga = _sigmoid(gra + bga)
    gb = _sigmoid(grb + bgb)
    pa = mm_w(oa, wa) + p_pa
    pb = mm_w(ob, wb) + p_pb
    merged = ga * pa + gb * pb
    mix = mm_w(merged, wo) + p_mix
    y1 = _layernorm(ALPHA * x + gate_t * mix, g1, b1)
    return y1, merged


def ffn_act_fn(extg, extv, rows_g, rows_v, bg, bv):
    return _silu(causal_conv(extg, rows_g) + bg) * (causal_conv(extv, rows_v) + bv)


def head_fn(a, y1, p_ffn, gate_f, g2, b2, tgt, wd):
    ffn = mm_w(a, wd) + p_ffn
    y2 = _layernorm(ALPHA * y1 + gate_f * ffn, g2, b2)
    err = y2 - tgt
    return 0.5 * jnp.sum(jnp.mean(err * err, axis=-1, keepdims=True))


def _rows(tm, width, colblk=0, order=None):
    if order is None:
        return pl.BlockSpec((tm, width), lambda i: (i, colblk))
    return pl.BlockSpec((tm, width), lambda i: (order(i), colblk))


def _const(shape):
    nd = len(shape)
    return pl.BlockSpec(shape, lambda *_: (0,) * nd)


def _pick(n, cands):
    for c in cands:
        if n % c == 0:
            return c
    raise ValueError(f"no tile for {n}")


def _tile(n, cap):
    best = None
    for c in range(LANE, min(n, cap) + 1, LANE):
        if n % c == 0:
            best = c
    if best is None:
        raise ValueError(f"no tile for {n}")
    return best


def _onehot_rows(k, j):
    return jnp.where(_iota2((k, 1), 0) == j, 1.0, 0.0).astype(F32)


def _stack_rows(drows):
    k = len(drows)
    out = None
    for j in range(k):
        tj = _onehot_rows(k, j) * drows[j]
        out = tj if out is None else out + tj
    return out


def matmul(a, w, out_dtype, name, ta=False, tb=False):
    kdim, m = a.shape if ta else a.shape[::-1]
    n = w.shape[0] if tb else w.shape[1]
    tm = _tile(m, 2048 if kdim <= 1024 else 1024)
    tn = _tile(n, 1024)
    tk = _tile(kdim, 2560)
    nk = kdim // tk
    a_spec = (pl.BlockSpec((tk, tm), lambda i, j, k: (k, i)) if ta
              else pl.BlockSpec((tm, tk), lambda i, j, k: (i, k)))
    w_spec = (pl.BlockSpec((tn, tk), lambda i, j, k: (j, k)) if tb
              else pl.BlockSpec((tk, tn), lambda i, j, k: (k, j)))

    def body(a_ref, w_ref, o_ref, *scratch):
        p = _dg(a_ref[...], w_ref[...], 0 if ta else 1, 1 if tb else 0)
        if nk == 1:
            o_ref[...] = p.astype(out_dtype)
            return
        acc = scratch[0]
        k = pl.program_id(2)

        @pl.when(k == 0)
        def _():
            acc[...] = p

        @pl.when(k > 0)
        def _():
            acc[...] += p

        @pl.when(k == nk - 1)
        def _():
            o_ref[...] = acc[...].astype(out_dtype)

    return pl.pallas_call(
        body, name=name,
        grid=(m // tm, n // tn, nk),
        in_specs=[a_spec, w_spec],
        out_specs=pl.BlockSpec((tm, tn), lambda i, j, k: (i, j)),
        out_shape=jax.ShapeDtypeStruct((m, n), out_dtype),
        scratch_shapes=[] if nk == 1 else [pltpu.VMEM((tm, tn), F32)],
        compiler_params=_cparams(3),
    )(a, w)


def dgrad_pieces(pieces, tail, w, name):
    m = pieces[0][0].shape[0]
    n, ktot = w.shape
    tk = 1024
    tm = _tile(m, 1024)
    wt = tail.shape[1]
    ranges, k0 = [], 0
    for arr, off in pieces:
        assert off == k0 * tk and arr.shape[1] % tk == 0
        ranges.append((k0, k0 + arr.shape[1] // tk))
        k0 = ranges[-1][1]
    nk = k0
    npc = len(pieces)

    def body(*refs):
        a_refs, t_ref, w_ref, wt_ref, o_ref, acc = refs[:npc], refs[npc], refs[npc + 1], refs[npc + 2], refs[npc + 3], refs[npc + 4]
        k = pl.program_id(1)

        @pl.when(k == 0)
        def _():
            acc[...] = _dg(t_ref[...], wt_ref[...], 1, 1)

        for a_ref, (lo, hi) in zip(a_refs, ranges):
            @pl.when((k >= lo) & (k < hi))
            def _(a_ref=a_ref):
                acc[...] += _dg(a_ref[...], w_ref[...], 1, 1)

        @pl.when(k == nk - 1)
        def _():
            o_ref[...] = acc[...]

    def piece_spec(lo, hi):
        return pl.BlockSpec((tm, tk), lambda i, k: (i, jnp.clip(k - lo, 0, hi - lo - 1)))

    return pl.pallas_call(
        body, name=name, grid=(m // tm, nk),
        in_specs=[piece_spec(lo, hi) for lo, hi in ranges] + [
            pl.BlockSpec((tm, wt), lambda i, k: (i, 0)),
            pl.BlockSpec((n, tk), lambda i, k: (0, k)),
            pl.BlockSpec((n, wt), lambda i, k: (0, (ktot - wt) // wt))],
        out_specs=pl.BlockSpec((tm, n), lambda i, k: (i, 0)),
        out_shape=jax.ShapeDtypeStruct((m, n), F32),
        scratch_shapes=[pltpu.VMEM((tm, n), F32)],
        compiler_params=_cparams(2),
    )(*[a for a, _ in pieces], tail, w, w)


def modulate(x, scale, shift, name):
    t, d = x.shape
    tm = _pick(t, (512, 256, 128))

    def body(x_ref, sc_ref, sh_ref, o_ref):
        o_ref[...] = (x_ref[...] * (1.0 + sc_ref[...]) + sh_ref[...]).astype(BF16)

    return pl.pallas_call(
        body, name=name, grid=(t // tm,),
        in_specs=[_rows(tm, d), _const((1, d)), _const((1, d))],
        out_specs=_rows(tm, d),
        out_shape=jax.ShapeDtypeStruct((t, d), BF16),
        compiler_params=_cparams(),
    )(x, scale, shift)


def modulate_bwd(dh, xin, dres, scale, name):
    t, d = dh.shape
    tm = _pick(t, (512, 256, 128))

    def body(dh_ref, x_ref, dr_ref, sc_ref, o_ref, dsc_ref, dsh_ref):
        i = pl.program_id(0)
        dh_v = dh_ref[...]
        o_ref[...] = dr_ref[...] + dh_v * (1.0 + sc_ref[...])

        @pl.when(i == 0)
        def _():
            dsc_ref[...] = jnp.zeros_like(dsc_ref)
            dsh_ref[...] = jnp.zeros_like(dsh_ref)

        dsc_ref[...] += jnp.sum(dh_v * x_ref[...], axis=0, keepdims=True)
        dsh_ref[...] += jnp.sum(dh_v, axis=0, keepdims=True)

    return pl.pallas_call(
        body, name=name, grid=(t // tm,),
        in_specs=[_rows(tm, d), _rows(tm, d), _rows(tm, d), _const((1, d))],
        out_specs=[_rows(tm, d), _const((1, d)), _const((1, d))],
        out_shape=[jax.ShapeDtypeStruct((t, d), F32), jax.ShapeDtypeStruct((1, d), F32),
                   jax.ShapeDtypeStruct((1, d), F32)],
        compiler_params=_cparams(),
    )(dh, xin, dres, scale)


PREP_TM = 128


def _halo_specs(tm, width, colblk, order):
    per = tm // HALO
    return [pl.BlockSpec((HALO, width), lambda i: (jnp.maximum(order(i) * per - 1, 0), colblk)),
            pl.BlockSpec((tm, width), lambda i: (order(i), colblk))]


def prep_fwd(proj, conv_a, a_log, dtb):
    t = proj.shape[0]
    tm = PREP_TM
    nt = t // tm
    wq = 3 * D

    def body(prev_ref, cur_ref, bb_ref, aa_ref, cw_ref, al_ref, dt_ref, q_ref, k_ref, v_ref, g_ref, b_ref):
        i = pl.program_id(0)
        flag = jnp.where(i > 0, 1.0, 0.0)
        for part, o_ref in enumerate((q_ref, k_ref, v_ref)):
            for h in range(AH):
                sl = slice(part * D + h * ADK, part * D + (h + 1) * ADK)
                ext = jnp.concatenate([prev_ref[:, sl] * flag, cur_ref[:, sl]], axis=0)
                rows = tuple(cw_ref[j:j + 1, sl] for j in range(4))
                o_ref[h] = prep_head_fn(ext, rows, PREP_SCALES[part])
        gcs, beta = prep_gate_fn(bb_ref[...], aa_ref[...], al_ref[...], dt_ref[...])
        g_ref[...] = gcs
        b_ref[...] = beta

    ident = lambda i: i
    hm = pl.BlockSpec((AH, tm, ADK), lambda i: (0, i, 0))
    return pl.pallas_call(
        body, name="prep_fwd", grid=(nt,),
        in_specs=_halo_specs(tm, wq, 0, ident) + [
            _rows(tm, 128, C_BA // 128), _rows(tm, 128, C_BA // 128 + 1),
            _const((4, wq)), _const((1, 128)), _const((1, 128))],
        out_specs=[hm, hm, hm, _rows(tm, 128), _rows(tm, 128)],
        out_shape=[jax.ShapeDtypeStruct((AH, t, ADK), F32)] * 3 + [jax.ShapeDtypeStruct((t, 128), F32)] * 2,
        compiler_params=_cparams(),
    )(proj, proj, proj, proj, conv_a, a_log, dtb)


def prep_bwd(proj, conv_a, a_log, dtb, dq, dk, dv, dgcs, dbeta):
    t = proj.shape[0]
    tm = PREP_TM
    nt = t // tm
    wq = 3 * D
    rev = lambda i: nt - 1 - i

    def body(prev_ref, cur_ref, bb_ref, aa_ref, cw_ref, al_ref, dt_ref,
             dq_ref, dk_ref, dv_ref, dg_ref, db_ref,
             dpre_ref, dbb_ref, daa_ref, dcw_ref, dal_ref, ddt_ref, carry):
        i = pl.program_id(0)
        flag = jnp.where(i < nt - 1, 1.0, 0.0)

        @pl.when(i == 0)
        def _():
            carry[...] = jnp.zeros_like(carry)
            dcw_ref[...] = jnp.zeros_like(dcw_ref)
            dal_ref[...] = jnp.zeros_like(dal_ref)
            ddt_ref[...] = jnp.zeros_like(ddt_ref)

        for part, d_ref in enumerate((dq_ref, dk_ref, dv_ref)):
            for h in range(AH):
                sl = slice(part * D + h * ADK, part * D + (h + 1) * ADK)
                ext = jnp.concatenate([prev_ref[:, sl] * flag, cur_ref[:, sl]], axis=0)
                rows = tuple(cw_ref[j:j + 1, sl] for j in range(4))
                _, vjp = jax.vjp(lambda e, r: prep_head_fn(e, r, PREP_SCALES[part]), ext, rows)
                dext, drows = vjp(d_ref[h])
                dcur = dext[HALO:]
                dpre_ref[:, sl] = jnp.concatenate([dcur[:tm - HALO], dcur[tm - HALO:] + carry[:, sl]],
                                                  axis=0).astype(BF16)
                carry[:, sl] = dext[:HALO]
                dcw_ref[:, sl] += _stack_rows(drows)
        _, vjp = jax.vjp(prep_gate_fn, bb_ref[...], aa_ref[...], al_ref[...], dt_ref[...])
        dbb, daa, dal, ddt = vjp((dg_ref[...], db_ref[...]))
        dbb_ref[...] = dbb.astype(BF16)
        daa_ref[...] = daa.astype(BF16)
        dal_ref[...] += dal
        ddt_ref[...] += ddt

    hm = pl.BlockSpec((AH, tm, ADK), lambda i: (0, rev(i), 0))
    return pl.pallas_call(
        body, name="prep_bwd", grid=(nt,),
        in_specs=_halo_specs(tm, wq, 0, rev) + [
            _rows(tm, 128, C_BA // 128, rev), _rows(tm, 128, C_BA // 128 + 1, rev),
            _const((4, wq)), _const((1, 128)), _const((1, 128)),
            hm, hm, hm, _rows(tm, 128, 0, rev), _rows(tm, 128, 0, rev)],
        out_specs=[_rows(tm, wq, 0, rev), _rows(tm, 128, 0, rev), _rows(tm, 128, 0, rev),
                   _const((4, wq)), _const((1, 128)), _const((1, 128))],
        out_shape=[jax.ShapeDtypeStruct((t, wq), BF16), jax.ShapeDtypeStruct((t, 128), BF16),
                   jax.ShapeDtypeStruct((t, 128), BF16), jax.ShapeDtypeStruct((4, wq), F32),
                   jax.ShapeDtypeStruct((1, 128), F32), jax.ShapeDtypeStruct((1, 128), F32)],
        scratch_shapes=[pltpu.VMEM((HALO, wq), F32)],
        compiler_params=_cparams(),
    )(proj, proj, proj, proj, conv_a, a_log, dtb, dq, dk, dv, dgcs, dbeta)


def _c1_specs(order):
    hm = pl.BlockSpec((AH, CH, ADK), lambda n: (0, order(n), 0))
    col = pl.BlockSpec((CH, LANE), lambda n: (order(n), 0))
    qk = pl.BlockSpec((1, AH, CH, CH), lambda n: (order(n), 0, 0, 0))
    eg = pl.BlockSpec((1, AH, 1, ADK), lambda n: (order(n), 0, 0, 0))
    return hm, col, qk, eg


def _heads(ref):
    return jnp.stack([ref[:, h * ADK:(h + 1) * ADK] for h in range(AH)], axis=0)


def c1_fwd(q, k, v, gcs, beta):
    t = q.shape[1]
    nc = t // CH
    hm, col, qks, egs = _c1_specs(lambda n: n)

    def body(q_ref, k_ref, v_ref, g_ref, b_ref, u_ref, w_ref, qg_ref, kd_ref, qk_ref, eg_ref, ti_ref):
        u, w, qk, qg, kd, eg, tinv = c1_heads(q_ref[...], k_ref[...], v_ref[...], g_ref[...], b_ref[...])
        u_ref[...] = u
        w_ref[...] = w.astype(BF16)
        qg_ref[...] = qg.astype(BF16)
        kd_ref[...] = kd.astype(BF16)
        qk_ref[0] = qk.astype(BF16)
        eg_ref[0] = eg
        ti_ref[0] = tinv

    return pl.pallas_call(
        body, name="c1_fwd", grid=(nc,),
        in_specs=[hm, hm, hm, col, col],
        out_specs=[hm, hm, hm, hm, qks, egs, qks],
        out_shape=[jax.ShapeDtypeStruct((AH, t, ADK), F32)] + [jax.ShapeDtypeStruct((AH, t, ADK), BF16)] * 3 + [
            jax.ShapeDtypeStruct((nc, AH, CH, CH), BF16), jax.ShapeDtypeStruct((nc, AH, 1, ADK), F32),
            jax.ShapeDtypeStruct((nc, AH, CH, CH), F32)],
        compiler_params=_cparams(),
    )(q, k, v, gcs, beta)


def c1_bwd(q, k, v, gcs, beta, tinv, du, dw, dqg, dkd, dqk, deg):
    t = q.shape[1]
    nc = t // CH
    hm, col, qks, egs = _c1_specs(lambda n: n)

    def body(q_ref, k_ref, v_ref, g_ref, b_ref, ti_ref, du_ref, dw_ref, dqg_ref, dkd_ref, dqk_ref, deg_ref,
             dq_ref, dk_ref, dv_ref, dg_ref, db_ref):
        _, vjp = jax.vjp(lambda q_, k_, v_, g_, b_: c1_heads(q_, k_, v_, g_, b_, ti_ref[0]),
                         q_ref[...], k_ref[...], v_ref[...], g_ref[...], b_ref[...])
        dq, dk, dv, dg, db = vjp((du_ref[...].astype(F32), dw_ref[...].astype(F32), dqk_ref[0], dqg_ref[...],
                                  dkd_ref[...], deg_ref[0],
                                  jnp.zeros((AH, CH, CH), F32)))
        dq_ref[...] = dq
        dk_ref[...] = dk
        dv_ref[...] = dv
        dg_ref[...] = dg
        db_ref[...] = db

    return pl.pallas_call(
        body, name="c1_bwd", grid=(nc,),
        in_specs=[hm, hm, hm, col, col, qks, hm, hm, hm, hm, qks, egs],
        out_specs=[hm, hm, hm, col, col],
        out_shape=[jax.ShapeDtypeStruct((AH, t, ADK), F32)] * 3 + [jax.ShapeDtypeStruct((t, LANE), F32)] * 2,
        compiler_params=_cparams(),
    )(q, k, v, gcs, beta, tinv, du, dw, dqg, dkd, dqk, deg)


def c2_fwd(u, w, qg, kd, qk, eg, proj, norm_a):
    t = u.shape[1]
    nc = t // CH
    hm, _, qks, egs = _c1_specs(lambda n: n)
    tok = pl.BlockSpec((CH, D), lambda n: (n, 0))
    zspec = pl.BlockSpec((CH, D), lambda n: (n, C_Z // D))
    sspec = pl.BlockSpec((1, AH, ADK, ADK), lambda n: (n, 0, 0, 0))

    def body(u_ref, w_ref, qg_ref, kd_ref, qk_ref, eg_ref, z_ref, nw_ref, o_ref, sall_ref, st):
        n = pl.program_id(0)

        @pl.when(n == 0)
        def _():
            st[...] = jnp.zeros_like(st)

        s = st[...]
        sall_ref[0] = s
        og, s2 = c2_heads(s, u_ref[...], w_ref[...], qk_ref[0], qg_ref[...], kd_ref[...], eg_ref[0],
                          _heads(z_ref), nw_ref[...])
        st[...] = s2
        for h in range(AH):
            o_ref[:, h * ADK:(h + 1) * ADK] = og[h].astype(BF16)

    return pl.pallas_call(
        body, name="c2_fwd", grid=(nc,),
        in_specs=[hm, hm, hm, hm, qks, egs, zspec, _const((1, ADK))],
        out_specs=[tok, sspec],
        out_shape=[jax.ShapeDtypeStruct((t, D), BF16), jax.ShapeDtypeStruct((nc, AH, ADK, ADK), F32)],
        scratch_shapes=[pltpu.VMEM((AH, ADK, ADK), F32)],
        compiler_params=_cparams(),
    )(u, w, qg, kd, qk, eg, proj, norm_a)


def c2_bwd(u, w, qg, kd, qk, eg, proj, norm_a, sall, do):
    t = u.shape[1]
    nc = t // CH
    rev = lambda n: nc - 1 - n
    hm, _, qks, egs = _c1_specs(rev)
    tok = pl.BlockSpec((CH, D), lambda n: (rev(n), 0))
    zspec = pl.BlockSpec((CH, D), lambda n: (rev(n), C_Z // D))
    sspec = pl.BlockSpec((1, AH, ADK, ADK), lambda n: (rev(n), 0, 0, 0))

    def body(u_ref, w_ref, qg_ref, kd_ref, qk_ref, eg_ref, z_ref, nw_ref, sall_ref, do_ref,
             du_ref, dw_ref, dqg_ref, dkd_ref, dqk_ref, deg_ref, dz_ref, dnw_ref, dst):
        n = pl.program_id(0)

        @pl.when(n == 0)
        def _():
            dst[...] = jnp.zeros_like(dst)
            dnw_ref[...] = jnp.zeros_like(dnw_ref)

        _, vjp = jax.vjp(c2_heads, sall_ref[0], u_ref[...], w_ref[...].astype(F32), qk_ref[0].astype(F32),
                         qg_ref[...].astype(F32), kd_ref[...].astype(F32), eg_ref[0], _heads(z_ref), nw_ref[...])
        ds, du, dw, dqk, dqg, dkd, deg, dz, dn = vjp((_heads(do_ref), dst[...]))
        dst[...] = ds
        du_ref[...] = du.astype(BF16)
        dw_ref[...] = dw.astype(BF16)
        dqg_ref[...] = dqg
        dkd_ref[...] = dkd
        dqk_ref[0] = dqk
        deg_ref[0] = deg
        for h in range(AH):
            dz_ref[:, h * ADK:(h + 1) * ADK] = dz[h].astype(BF16)
        dnw_ref[...] += dn

    return pl.pallas_call(
        body, name="c2_bwd", grid=(nc,),
        in_specs=[hm, hm, hm, hm, qks, egs, zspec, _const((1, ADK)), sspec, tok],
        out_specs=[hm, hm, hm, hm, qks, egs, tok, _const((1, ADK))],
        out_shape=[jax.ShapeDtypeStruct((AH, t, ADK), BF16)] * 2 + [jax.ShapeDtypeStruct((AH, t, ADK), F32)] * 2 + [
            jax.ShapeDtypeStruct((nc, AH, CH, CH), F32), jax.ShapeDtypeStruct((nc, AH, 1, ADK), F32),
            jax.ShapeDtypeStruct((t, D), BF16), jax.ShapeDtypeStruct((1, ADK), F32)],
        scratch_shapes=[pltpu.VMEM((AH, ADK, ADK), F32)],
        compiler_params=_cparams(),
    )(u, w, qg, kd, qk, eg, proj, norm_a, sall, do)


NQB = TQ // CH
NKB = 2 * TQ // CH
NDIST = BPREV + 1
KLO = -(NQB - 2)
NPAIR = NKB - 1 - KLO + 1


def bias_table(rel_bias):
    nh = rel_bias.shape[0]
    relx = jnp.concatenate([rel_bias, jnp.broadcast_to(rel_bias[:, -1:], (nh, CH * BPREV + 2 * CH - 1 - RELSZ))],
                           axis=1)
    t = jnp.stack([relx[:, CH * k:CH * k + 2 * CH - 1] for k in range(NDIST)], axis=1)
    trev = t[:, :, ::-1]
    g2 = jnp.concatenate([trev[:, :, CH - 1:], jnp.zeros((nh, NDIST, 1), F32), trev[:, :, :CH - 1]], axis=2)
    flat = jnp.tile(g2, (1, 1, CH + 1))[:, :, :CH * (2 * CH - 1)]
    blk = flat.reshape(nh, NDIST, CH, 2 * CH - 1)[..., :CH]
    neg = jnp.full((nh, NQB - 1, CH, CH), NEG, F32)
    asc = jnp.concatenate([neg, blk, neg], axis=1)
    return jnp.concatenate([asc[:, 1:], asc[:, :-1]], axis=-1)


SUBQ = 4 * CH
NSUB = TQ // SUBQ
KWIN = SUBQ + BPREV * CH


def assemble_bias(tab, r):
    b0 = r * SUBQ // (2 * CH)
    rows = [jnp.concatenate([tab[NQB + a - 2 * b - KLO] for b in range(b0, b0 + KWIN // (2 * CH))], axis=1)
            for a in range(r * SUBQ // CH, (r + 1) * SUBQ // CH)]
    return jnp.concatenate(rows, axis=0)


def bias_table_bwd_layout(dtab):
    nh = dtab.shape[0]
    dasc = (jnp.pad(dtab[..., :CH], ((0, 0), (1, 0), (0, 0), (0, 0)))
            + jnp.pad(dtab[..., CH:], ((0, 0), (0, 1), (0, 0), (0, 0))))
    dblk = dasc[:, NQB - 1:NQB - 1 + NDIST]
    dr = jnp.pad(dblk, ((0, 0), (0, 0), (0, 0), (0, CH - 1)))
    flat = jnp.pad(dr.reshape(nh, NDIST, CH * (2 * CH - 1)), ((0, 0), (0, 0), (0, 3 * CH)))
    return flat.reshape(nh, NDIST, CH + 1, 2 * CH).transpose(0, 2, 1, 3).reshape(nh, CH + 1, NDIST * 2 * CH)


def _fold_matrix_np():
    f = np.zeros((NDIST * 2 * CH, 384), np.float32)
    for k in range(NDIST):
        s = k
        for xx in range(2 * CH):
            if xx == CH:
                continue
            m = CH - 1 - xx if xx < CH else 3 * CH - 1 - xx
            f[s * 2 * CH + xx, min(CH * k + m, RELSZ - 1)] = 1.0
    return f


def relbias_reduce(dlay):
    nh, rows, cols = dlay.shape
    rpad = (-rows) % 8
    dlay = jnp.pad(dlay, ((0, 0), (0, rpad), (0, 0)))
    fold = jnp.asarray(_fold_matrix_np())

    def body(d_ref, f_ref, o_ref):
        cs = jnp.sum(d_ref[0], axis=0, keepdims=True)
        o_ref[0] = _mmh(jnp.broadcast_to(cs, (8, cols)), f_ref[...])

    out = pl.pallas_call(
        body, name="relbias_reduce", grid=(nh,),
        in_specs=[pl.BlockSpec((1, rows + rpad, cols), lambda h: (h, 0, 0)), _const((cols, 384))],
        out_specs=pl.BlockSpec((1, 8, 384), lambda h: (h, 0, 0)),
        out_shape=jax.ShapeDtypeStruct((nh, 8, 384), F32),
        compiler_params=_cparams(),
    )(dlay, fold)
    return out[:, 0, :RELSZ]


def attn_fwd(proj, bias):
    t = proj.shape[0]
    nt = t // TQ
    cb = C_QKVB // 128

    def body(q_ref, kp_ref, kc_ref, vp_ref, vc_ref, b_ref, o_ref, p_ref):
        i = pl.program_id(1)
        firstf = jnp.where(i == 0, 1.0, 0.0)
        for r in range(NSUB):
            lo, hi = r * SUBQ, r * SUBQ + KWIN - TQ
            kw = jnp.concatenate([kp_ref[lo:, :], kc_ref[:hi, :]], axis=0)
            vw = jnp.concatenate([vp_ref[lo:, :], vc_ref[:hi, :]], axis=0)
            out, probs = attn_sub_fwd(q_ref[lo:lo + SUBQ, :].astype(F32), kw, vw, b_ref[...], r, firstf)
            o_ref[lo:lo + SUBQ, :] = out.astype(BF16)
            for hh in range(2):
                p_ref[hh, lo:lo + SUBQ, :] = probs[hh].astype(BF16)

    def blk(off, prev):
        if prev:
            return pl.BlockSpec((TQ, 128), lambda p, i: (jnp.maximum(i - 1, 0), cb + off + p))
        return pl.BlockSpec((TQ, 128), lambda p, i: (i, cb + off + p))

    return pl.pallas_call(
        body, name="attn_fwd", grid=(BH // 2, nt),
        in_specs=[blk(0, False), blk(8, True), blk(8, False), blk(16, True), blk(16, False),
                  pl.BlockSpec((2, NPAIR, CH, 2 * CH), lambda p, i: (p, 0, 0, 0))],
        out_specs=[pl.BlockSpec((TQ, 128), lambda p, i: (i, p)),
                   pl.BlockSpec((2, TQ, KWIN), lambda p, i: (p, i, 0))],
        out_shape=[jax.ShapeDtypeStruct((t, D), BF16), jax.ShapeDtypeStruct((BH, t, KWIN), BF16)],
        compiler_params=_cparams(2),
    )(proj, proj, proj, proj, proj, bias)


def attn_bwd(proj, ob, probs, do):
    t = proj.shape[0]
    nt = t // TQ
    cb = C_QKVB // 128

    def body(q_ref, kp_ref, kc_ref, vp_ref, vc_ref, o_ref, p_ref, do_ref,
             dq_ref, dk_ref, dv_ref, db_ref, ck, cv, ak, av):
        i = pl.program_id(1)

        @pl.when(i == 0)
        def _():
            ck[...] = jnp.zeros_like(ck)
            cv[...] = jnp.zeros_like(cv)
            db_ref[...] = jnp.zeros_like(db_ref)

        @pl.when(i < nt)
        def _():
            ak[...] = jnp.zeros_like(ak)
            av[...] = jnp.zeros_like(av)
            for r in range(NSUB):
                lo, hi = r * SUBQ, r * SUBQ + KWIN - TQ
                rows = slice(lo, lo + SUBQ)
                kw = jnp.concatenate([kp_ref[lo:, :], kc_ref[:hi, :]], axis=0)
                vw = jnp.concatenate([vp_ref[lo:, :], vc_ref[:hi, :]], axis=0)
                probs_r = [p_ref[hh, rows, :].astype(F32) for hh in range(2)]
                dq, dkw, dvw, dss = attn_sub_bwd(q_ref[rows, :].astype(F32), kw, vw, o_ref[rows, :].astype(F32),
                                                 do_ref[rows, :], probs_r, r)
                dq_ref[rows, :] = dq.astype(BF16)
                ak[lo:lo + KWIN, :] += dkw
                av[lo:lo + KWIN, :] += dvw
                for hh in range(2):
                    _, scatter = jax.vjp(lambda tab: assemble_bias(tab, r), jnp.zeros((NPAIR, CH, 2 * CH), F32))
                    db_ref[hh] += scatter(dss[hh])[0]
            dk_ref[...] = (ck[...] + ak[:TQ, :]).astype(BF16)
            dv_ref[...] = (cv[...] + av[:TQ, :]).astype(BF16)
            ck[...] = ak[TQ:, :]
            cv[...] = av[TQ:, :]

        @pl.when(i == nt)
        def _():
            dk_ref[...] = ck[...].astype(BF16)
            dv_ref[...] = cv[...].astype(BF16)

    def blk(off, prev):
        if prev:
            return pl.BlockSpec((TQ, 128), lambda p, i: (jnp.clip(i - 1, 0, nt - 1), cb + off + p))
        return pl.BlockSpec((TQ, 128), lambda p, i: (jnp.minimum(i, nt - 1), cb + off + p))

    own = pl.BlockSpec((TQ, 128), lambda p, i: (jnp.minimum(i, nt - 1), p))
    lag = pl.BlockSpec((TQ, 128), lambda p, i: (jnp.maximum(i - 1, 0), p))
    return pl.pallas_call(
        body, name="attn_bwd", grid=(BH // 2, nt + 1),
        in_specs=[blk(0, False), blk(8, True), blk(8, False), blk(16, True), blk(16, False), own,
                  pl.BlockSpec((2, TQ, KWIN), lambda p, i: (p, jnp.minimum(i, nt - 1), 0)), own],
        out_specs=[own, lag, lag, pl.BlockSpec((2, NPAIR, CH, 2 * CH), lambda p, i: (p, 0, 0, 0))],
        out_shape=[jax.ShapeDtypeStruct((t, D), BF16)] * 3 + [jax.ShapeDtypeStruct((BH, NPAIR, CH, 2 * CH), F32)],
        scratch_shapes=[pltpu.VMEM((TQ, 128), F32), pltpu.VMEM((TQ, 128), F32),
                        pltpu.VMEM((2 * TQ, 128), F32), pltpu.VMEM((2 * TQ, 128), F32)],
        compiler_params=_cparams(2),
    )(proj, proj, proj, proj, proj, ob, probs, do)


MERGE_TM = 256


def merge_fwd(x, oa, ob, proj, vecs, wa, wb, wo):
    t = x.shape[0]
    tm = MERGE_TM
    names = ("bga", "bgb", "gate_t", "g1", "b1", "scale_f", "shift_f")

    def body(x_ref, oa_ref, ob_ref, gra_ref, grb_ref, *rest):
        vrefs = rest[:7]
        wa_ref, wb_ref, wo_ref, y_ref, h_ref = rest[7:]
        vv = [r[...] for r in vrefs]
        zero = jnp.zeros((tm, D), F32)
        y1, _ = merge_fn(x_ref[...], oa_ref[...], ob_ref[...], gra_ref[...], grb_ref[...], zero, zero, zero,
                         *vv, wa_ref[...], wb_ref[...], wo_ref[...])
        y_ref[...] = y1
        h_ref[...] = (y1 * (1.0 + vv[5]) + vv[6]).astype(BF16)

    return pl.pallas_call(
        body, name="merge_fwd", grid=(t // tm,),
        in_specs=[_rows(tm, D), _rows(tm, D), _rows(tm, D), _rows(tm, D, C_GATE // D), _rows(tm, D, C_GATE // D + 1)]
        + [_const((1, D))] * 7 + [_const((D, D))] * 3,
        out_specs=[_rows(tm, D), _rows(tm, D)],
        out_shape=[jax.ShapeDtypeStruct((t, D), F32), jax.ShapeDtypeStruct((t, D), BF16)],
        compiler_params=_cparams(),
    )(x, oa, ob, proj, proj, *[vecs[n] for n in names], wa, wb, wo)


def merge_bwd(x, oa, ob, proj, vecs, wa, wb, wo, dy1):
    t = x.shape[0]
    tm = MERGE_TM
    names = ("bga", "bgb", "gate_t", "g1", "b1", "scale_f", "shift_f")

    def body(x_ref, oa_ref, ob_ref, gra_ref, grb_ref, *rest):
        vrefs = rest[:7]
        wa_ref, wb_ref, wo_ref, dy_ref = rest[7:11]
        (dx_ref, doa_ref, dob_ref, dga_ref, dgb_ref, mg_ref, dmix_ref, dpa_ref, dpb_ref,
         dbga_ref, dbgb_ref, dgt_ref, dg1_ref, db1_ref) = rest[11:]
        i = pl.program_id(0)
        vv = [r[...] for r in vrefs]
        zero = jnp.zeros((tm, D), F32)

        def f(x_, oa_, ob_, gra_, grb_, ppa, ppb, pmix, bga, bgb, gate_t, g1, b1):
            return merge_fn(x_, oa_, ob_, gra_, grb_, ppa, ppb, pmix, bga, bgb, gate_t, g1, b1, vv[5], vv[6],
                            wa_ref[...], wb_ref[...], wo_ref[...])

        _, vjp, merged = jax.vjp(f, x_ref[...], oa_ref[...].astype(F32), ob_ref[...].astype(F32),
                                 gra_ref[...], grb_ref[...], zero, zero, zero, *vv[:5], has_aux=True)
        dx, doa, dob, dga, dgb, dpa, dpb, dmix, dbga, dbgb, dgt, dg1, db1 = vjp(dy_ref[...])
        dx_ref[...] = dx
        doa_ref[...] = doa
        dob_ref[...] = dob
        dga_ref[...] = dga.astype(BF16)
        dgb_ref[...] = dgb.astype(BF16)
        mg_ref[...] = merged.astype(BF16)
        dmix_ref[...] = dmix.astype(BF16)
        dpa_ref[...] = dpa.astype(BF16)
        dpb_ref[...] = dpb.astype(BF16)
        accs = (dbga_ref, dbgb_ref, dgt_ref, dg1_ref, db1_ref)

        @pl.when(i == 0)
        def _():
            for a in accs:
                a[...] = jnp.zeros_like(a)

        for a, val in zip(accs, (dbga, dbgb, dgt, dg1, db1)):
            a[...] += val

    return pl.pallas_call(
        body, name="merge_bwd", grid=(t // tm,),
        in_specs=[_rows(tm, D), _rows(tm, D), _rows(tm, D), _rows(tm, D, C_GATE // D), _rows(tm, D, C_GATE // D + 1)]
        + [_const((1, D))] * 7 + [_const((D, D))] * 3 + [_rows(tm, D)],
        out_specs=[_rows(tm, D)] * 9 + [_const((1, D))] * 5,
        out_shape=[jax.ShapeDtypeStruct((t, D), F32)] * 3 + [jax.ShapeDtypeStruct((t, D), BF16)] * 6
        + [jax.ShapeDtypeStruct((1, D), F32)] * 5,
        compiler_params=_cparams(),
    )(x, oa, ob, proj, proj, *[vecs[n] for n in names], wa, wb, wo, dy1)


FFN_TM = 128


def ffn_act_fwd(up, conv_w, bconv):
    t, wdt = up.shape
    tm = FFN_TM

    def body(prev_ref, cur_ref, cw_ref, bc_ref, a_ref):
        i = pl.program_id(0)
        flag = jnp.where(i > 0, 1.0, 0.0)

        def ext(sl):
            return jnp.concatenate([prev_ref[:, sl] * flag, cur_ref[:, sl]], axis=0)

        def rows(sl):
            return tuple(cw_ref[j:j + 1, sl] for j in range(3))

        for cb in range(DFF // LANE):
            g = slice(cb * LANE, (cb + 1) * LANE)
            v = slice(DFF + cb * LANE, DFF + (cb + 1) * LANE)
            a_ref[:, g] = ffn_act_fn(ext(g), ext(v), rows(g), rows(v), bc_ref[:, g], bc_ref[:, v]).astype(BF16)

    return pl.pallas_call(
        body, name="ffn_act_fwd", grid=(t // tm,),
        in_specs=_halo_specs(tm, wdt, 0, lambda i: i) + [_const((3, wdt)), _const((1, wdt))],
        out_specs=_rows(tm, DFF),
        out_shape=jax.ShapeDtypeStruct((t, DFF), BF16),
        compiler_params=_cparams(),
    )(up, up, conv_w, bconv)


def ffn_act_bwd(up, conv_w, bconv, da):
    t, wdt = up.shape
    tm = FFN_TM
    nt = t // tm
    rev = lambda i: nt - 1 - i

    def body(prev_ref, cur_ref, cw_ref, bc_ref, da_ref, dup_ref, dcw_ref, dbc_ref, carry):
        i = pl.program_id(0)
        flag = jnp.where(i < nt - 1, 1.0, 0.0)

        @pl.when(i == 0)
        def _():
            carry[...] = jnp.zeros_like(carry)
            dcw_ref[...] = jnp.zeros_like(dcw_ref)
            dbc_ref[...] = jnp.zeros_like(dbc_ref)

        def ext(sl):
            return jnp.concatenate([prev_ref[:, sl] * flag, cur_ref[:, sl]], axis=0)

        def rows(sl):
            return tuple(cw_ref[j:j + 1, sl] for j in range(3))

        def emit(sl, dext, drows, dbc):
            dcur = dext[HALO:]
            dup_ref[:, sl] = jnp.concatenate([dcur[:tm - HALO], dcur[tm - HALO:] + carry[:, sl]], axis=0).astype(BF16)
            carry[:, sl] = dext[:HALO]
            dcw_ref[:, sl] += _stack_rows(drows)
            dbc_ref[:, sl] += dbc

        for cb in range(DFF // LANE):
            g = slice(cb * LANE, (cb + 1) * LANE)
            v = slice(DFF + cb * LANE, DFF + (cb + 1) * LANE)
            _, vjp = jax.vjp(ffn_act_fn, ext(g), ext(v), rows(g), rows(v), bc_ref[:, g], bc_ref[:, v])
            dxg, dxv, drg, drv, dbg, dbv = vjp(da_ref[:, g])
            emit(g, dxg, drg, dbg)
            emit(v, dxv, drv, dbv)

    return pl.pallas_call(
        body, name="ffn_act_bwd", grid=(nt,),
        in_specs=_halo_specs(tm, wdt, 0, rev) + [_const((3, wdt)), _const((1, wdt)), _rows(tm, DFF, 0, rev)],
        out_specs=[_rows(tm, wdt, 0, rev), _const((3, wdt)), _const((1, wdt))],
        out_shape=[jax.ShapeDtypeStruct((t, wdt), BF16), jax.ShapeDtypeStruct((3, wdt), F32),
                   jax.ShapeDtypeStruct((1, wdt), F32)],
        scratch_shapes=[pltpu.VMEM((HALO, wdt), F32)],
        compiler_params=_cparams(),
    )(up, up, conv_w, bconv, da)


HEAD_TM = 256


def head_fwd_bwd(a, y1, tgt, gate_f, g2, b2, wd):
    t = a.shape[0]
    tm = HEAD_TM

    def body(a_ref, y_ref, t_ref, gf_ref, g2_ref, b2_ref, wd_ref,
             da_ref, dy_ref, dffn_ref, dgf_ref, dg2_ref, db2_ref, loss_ref):
        i = pl.program_id(0)
        zero = jnp.zeros((tm, D), F32)

        def f(a_, y_, pf, gf, g2_, b2_):
            return head_fn(a_, y_, pf, gf, g2_, b2_, t_ref[...], wd_ref[...])

        loss, vjp = jax.vjp(f, a_ref[...].astype(F32), y_ref[...], zero, gf_ref[...], g2_ref[...], b2_ref[...])
        da, dy, dffn, dgf, dg2, db2 = vjp(jnp.ones((), F32))
        da_ref[...] = da
        dy_ref[...] = dy
        dffn_ref[...] = dffn.astype(BF16)
        accs = (dgf_ref, dg2_ref, db2_ref, loss_ref)

        @pl.when(i == 0)
        def _():
            for r in accs:
                r[...] = jnp.zeros_like(r)

        dgf_ref[...] += dgf
        dg2_ref[...] += dg2
        db2_ref[...] += db2
        loss_ref[...] += loss * jnp.ones((1, 128), F32)

    return pl.pallas_call(
        body, name="head_fwd_bwd", grid=(t // tm,),
        in_specs=[_rows(tm, DFF), _rows(tm, D), _rows(tm, D), _const((1, D)), _const((1, D)), _const((1, D)),
                  _const((DFF, D))],
        out_specs=[_rows(tm, DFF), _rows(tm, D), _rows(tm, D), _const((1, D)), _const((1, D)), _const((1, D)),
                   _const((1, 128))],
        out_shape=[jax.ShapeDtypeStruct((t, DFF), F32), jax.ShapeDtypeStruct((t, D), F32),
                   jax.ShapeDtypeStruct((t, D), BF16)] + [jax.ShapeDtypeStruct((1, D), F32)] * 3
        + [jax.ShapeDtypeStruct((1, 128), F32)],
        compiler_params=_cparams(),
    )(a, y1, tgt, gate_f, g2, b2, wd)


def ada_fwd(c_all, w_sh, b_sh):
    def body(c_ref, w_ref, b_ref, o_ref):
        o_ref[...] = _mmh(_silu(c_ref[...]), w_ref[...]) + b_ref[...]

    n = w_sh.shape[1]
    return pl.pallas_call(
        body, name="ada_fwd", out_shape=jax.ShapeDtypeStruct((NDEV, n), F32),
        in_specs=[pl.BlockSpec(memory_space=pltpu.VMEM)] * 3,
        out_specs=pl.BlockSpec(memory_space=pltpu.VMEM),
        compiler_params=pltpu.CompilerParams(vmem_limit_bytes=VMEM_LIMIT),
    )(c_all, w_sh, b_sh)


def ada_wgrad(c_all_t, dmod_sh):
    def body(c_ref, d_ref, o_ref):
        o_ref[...] = _mmh(_silu(c_ref[...]), d_ref[...])

    return pl.pallas_call(
        body, name="ada_wgrad", out_shape=jax.ShapeDtypeStruct((c_all_t.shape[0], dmod_sh.shape[1]), F32),
        in_specs=[pl.BlockSpec(memory_space=pltpu.VMEM)] * 2,
        out_specs=pl.BlockSpec(memory_space=pltpu.VMEM),
        compiler_params=pltpu.CompilerParams(vmem_limit_bytes=VMEM_LIMIT),
    )(c_all_t, dmod_sh)


def adamw(gparts, w, m, v, name):
    p, r, c = gparts.shape
    tr = r if r <= 256 else _pick(r, (256, 128, 64, 32, 16, 8))
    c1 = 1.0 - B1 ** STEP
    c2 = 1.0 - B2 ** STEP

    def body(g_ref, w_ref, m_ref, v_ref, go_ref, d_ref, mo_ref, vo_ref):
        g = g_ref[0].astype(F32)
        for s in range(1, p):
            g = g + g_ref[s].astype(F32)
        mn = B1 * m_ref[0] + (1.0 - B1) * g
        vn = B2 * v_ref[0] + (1.0 - B2) * (g * g)
        go_ref[0] = g
        d_ref[0] = -LR * ((mn / c1) / (jnp.sqrt(vn / c2) + AEPS) + WD * w_ref[0])
        mo_ref[0] = mn
        vo_ref[0] = vn

    spec = pl.BlockSpec((1, tr, c), lambda i: (0, i, 0))
    return pl.pallas_call(
        body, name=name, grid=(r // tr,),
        in_specs=[pl.BlockSpec((p, tr, c), lambda i: (0, i, 0)), spec, spec, spec],
        out_specs=[spec] * 4,
        out_shape=[jax.ShapeDtypeStruct((1, r, c), F32)] * 4,
        compiler_params=_cparams(),
    )(gparts, w, m, v)


def adamw_small(gs, ws, ms, vs, loss_parts, name):
    n = len(ws)
    c1 = 1.0 - B1 ** STEP
    c2 = 1.0 - B2 ** STEP

    def slots(ref):
        acc = ref[0]
        for s in range(1, ref.shape[0]):
            acc = acc + ref[s]
        return acc

    def body(*refs):
        g_refs, w_refs, m_refs, v_refs = (refs[k * n:(k + 1) * n] for k in range(4))
        l_ref, outs = refs[4 * n], refs[4 * n + 1:]
        for i in range(n):
            g = slots(g_refs[i])
            mn = B1 * m_refs[i][...] + (1.0 - B1) * g
            vn = B2 * v_refs[i][...] + (1.0 - B2) * (g * g)
            outs[i][...] = g
            outs[n + i][...] = -LR * ((mn / c1) / (jnp.sqrt(vn / c2) + AEPS) + WD * w_refs[i][...])
            outs[2 * n + i][...] = mn
            outs[3 * n + i][...] = vn
        outs[4 * n][...] = slots(l_ref)

    vmem = pl.BlockSpec(memory_space=pltpu.VMEM)
    outs = pl.pallas_call(
        body, name=name,
        in_specs=[vmem] * (4 * n + 1), out_specs=[vmem] * (4 * n + 1),
        out_shape=[jax.ShapeDtypeStruct(w.shape, F32) for w in ws] * 4 + [jax.ShapeDtypeStruct((1, LANE), F32)],
        compiler_params=pltpu.CompilerParams(vmem_limit_bytes=VMEM_LIMIT),
    )(*gs, *ws, *ms, *vs, loss_parts)
    return outs[:n], outs[n:2 * n], outs[2 * n:3 * n], outs[3 * n:4 * n], outs[4 * n]


def _me():
    x, y, c = lax.axis_index("x"), lax.axis_index("y"), lax.axis_index("c")
    return x, y, c, 4 * x + 2 * y + c


def _peer(x, y, c, d):
    px = 1 - x if (d >> 2) & 1 else x
    py = 1 - y if (d >> 1) & 1 else y
    pc = 1 - c if d & 1 else c
    return (px, py, pc), 4 * px + 2 * py + pc


def _exchange(arrs, name, scatter):
    n = len(arrs)

    def body(*refs):
        ins, outs = refs[:n], refs[n:2 * n]
        send, recv, lsem = refs[2 * n:]
        x, y, c, me = _me()
        remote, local = [], []
        for k in range(n):
            src = ins[k].at[me] if scatter else ins[k]
            cp = pltpu.make_async_copy(src, outs[k].at[me], lsem.at[k])
            cp.start()
            local.append(cp)
            for d in range(1, NDEV):
                dev, pid = _peer(x, y, c, d)
                src = ins[k].at[pid] if scatter else ins[k]
                cp = pltpu.make_async_remote_copy(src_ref=src, dst_ref=outs[k].at[me],
                                                  send_sem=send.at[k, d - 1], recv_sem=recv.at[k, d - 1],
                                                  device_id=dev, device_id_type=pl.DeviceIdType.MESH)
                cp.start()
                remote.append(cp)
        for cp in remote:
            cp.wait()
        for cp in local:
            cp.wait()

    shapes = [a.shape if scatter else (NDEV,) + a.shape for a in arrs]
    return pl.pallas_call(
        body, name=name,
        in_specs=[pl.BlockSpec(memory_space=pl.ANY)] * n,
        out_specs=[pl.BlockSpec(memory_space=pl.ANY)] * n,
        out_shape=[jax.ShapeDtypeStruct(s, a.dtype) for s, a in zip(shapes, arrs)],
        scratch_shapes=[pltpu.SemaphoreType.DMA((n, NDEV - 1)), pltpu.SemaphoreType.DMA((n, NDEV - 1)),
                        pltpu.SemaphoreType.DMA((n,))],
        compiler_params=pltpu.CompilerParams(has_side_effects=True),
    )(*arrs)


def all_gather(arrs, name):
    return _exchange(arrs, name, False)


def all_gather_two_level(shard, name):
    def body(x_ref, out_ref, send, recv, lsem):
        x, y, c, _ = _me()
        sibling = (x, y, 1 - c)
        chips = [(1 - x, y), (x, 1 - y), (1 - x, 1 - y)]

        def slot(px, py, pc):
            return out_ref.at[4 * px + 2 * py + pc]

        def copy(k, block, to, src=None):
            return pltpu.make_async_remote_copy(
                src_ref=slot(*block) if src is None else src, dst_ref=slot(*block),
                send_sem=send.at[k], recv_sem=recv.at[k], device_id=to, device_id_type=pl.DeviceIdType.MESH)

        mine = pltpu.make_async_copy(x_ref, slot(x, y, c), lsem)
        mine.start()
        first = [copy(0, (x, y, c), sibling, src=x_ref)]
        first += [copy(1 + j, (x, y, c), (*chip, c), src=x_ref) for j, chip in enumerate(chips)]
        for cp in first:
            cp.start()
        passed = [copy(4 + j, (*chip, c), sibling) for j, chip in enumerate(chips)]
        for j, chip in enumerate(chips):
            copy(1 + j, (*chip, c), (x, y, c)).wait_recv()
            passed[j].start()
        copy(0, sibling, (x, y, c)).wait_recv()
        for j, chip in enumerate(chips):
            copy(4 + j, (*chip, 1 - c), (x, y, c)).wait_recv()
        for cp in first + passed:
            cp.wait_send()
        mine.wait()

    return pl.pallas_call(
        body, name=name,
        in_specs=[pl.BlockSpec(memory_space=pl.ANY)],
        out_specs=pl.BlockSpec(memory_space=pl.ANY),
        out_shape=jax.ShapeDtypeStruct((NDEV,) + shard.shape, shard.dtype),
        scratch_shapes=[pltpu.SemaphoreType.DMA((NPEER,)), pltpu.SemaphoreType.DMA((NPEER,)),
                        pltpu.SemaphoreType.DMA],
        compiler_params=pltpu.CompilerParams(has_side_effects=True),
    )(shard)


def all_to_all(arrs, name):
    return _exchange(arrs, name, True)


_HBM = pl.BlockSpec(memory_space=pltpu.HBM)
_SEM = pl.BlockSpec(memory_space=pltpu.SEMAPHORE)
_EFFECT = pltpu.SideEffectType.DATAFLOW_SIDE_EFFECTING
NPEER = NDEV - 1


def exchange_start(arrs, name, scatter):
    n = len(arrs)
    lands = [lax.empty(a.shape if scatter else (NDEV,) + a.shape, a.dtype) for a in arrs]

    def body(*refs):
        ins, lrefs = refs[:n], refs[n:2 * n]
        send, recv, token = refs[2 * n], refs[2 * n + 1], refs[-1]
        x, y, c, me = _me()
        for k in range(n):
            for d in range(1, NDEV):
                dev, pid = _peer(x, y, c, d)
                src = ins[k].at[pid] if scatter else ins[k]
                pltpu.make_async_remote_copy(src_ref=src, dst_ref=lrefs[k].at[me],
                                             send_sem=send.at[k * NPEER + d - 1], recv_sem=recv.at[k * NPEER + d - 1],
                                             device_id=dev, device_id_type=pl.DeviceIdType.MESH).start()
        token[...] = jnp.zeros_like(token)

    thru = [pltpu.HBM(a.shape, a.dtype) for a in list(arrs) + lands]
    outs = pl.pallas_call(
        body, name=name,
        out_shape=(pltpu.SemaphoreType.DMA((n * NPEER,)), pltpu.SemaphoreType.DMA((n * NPEER,)), *thru,
                   jax.ShapeDtypeStruct((8, 128), F32)),
        in_specs=[_HBM] * (2 * n),
        out_specs=(_SEM, _SEM, *([_HBM] * (2 * n)), pl.BlockSpec(memory_space=pltpu.VMEM)),
        input_output_aliases={i: 2 + i for i in range(2 * n)},
        compiler_params=pltpu.CompilerParams(has_side_effects=_EFFECT),
    )(*[pltpu.with_memory_space_constraint(a, pltpu.HBM) for a in list(arrs) + lands])
    handle = dict(send=outs[0], recv=outs[1], src=list(outs[2:2 + n]), land=list(outs[2 + n:2 + 2 * n]),
                  scatter=scatter)
    return handle, outs[-1][0, 0]


def exchange_wait(handle, after, name):
    n = len(handle["src"])
    scatter = handle["scatter"]

    def body(*refs):
        ins, lrefs = refs[:n], refs[n:2 * n]
        send, recv = refs[2 * n], refs[2 * n + 1]
        x, y, c, _ = _me()
        for k in range(n):
            for d in range(1, NDEV):
                dev, _ = _peer(x, y, c, d)
                src = ins[k].at[0] if scatter else ins[k]
                cp = pltpu.make_async_remote_copy(src_ref=src, dst_ref=lrefs[k].at[0],
                                                  send_sem=send.at[k * NPEER + d - 1],
                                                  recv_sem=recv.at[k * NPEER + d - 1],
                                                  device_id=dev, device_id_type=pl.DeviceIdType.MESH)
                cp.wait_send()
                cp.wait_recv()

    arrs = handle["src"] + handle["land"]
    outs = pl.pallas_call(
        body, name=name,
        out_shape=tuple(pltpu.HBM(a.shape, a.dtype) for a in arrs),
        in_specs=[_HBM] * (2 * n) + [_SEM, _SEM, pl.BlockSpec(memory_space=pl.ANY)],
        out_specs=tuple([_HBM] * (2 * n)),
        input_output_aliases={i: i for i in range(2 * n)},
        compiler_params=pltpu.CompilerParams(has_side_effects=_EFFECT),
    )(*arrs, handle["send"], handle["recv"], after)
    me = 4 * lax.axis_index("x") + 2 * lax.axis_index("y") + lax.axis_index("c")
    landed = []
    for own, land in zip(outs[:n], outs[n:]):
        mine = lax.dynamic_index_in_dim(own, me, 0, keepdims=True) if scatter else own[None]
        landed.append(lax.dynamic_update_slice_in_dim(land, mine, me, 0))
    return landed


def _cat_from_slabs(slabs):
    _, k, n = slabs.shape

    def cols(lo, hi):
        parts, c = [], lo
        while c < hi:
            j = c // n
            e = min(hi, (j + 1) * n)
            parts.append(slabs[j][:, c - j * n:e - j * n])
            c = e
        return parts

    def zeros(w):
        return [jnp.zeros((k, w), slabs.dtype)]

    return jnp.concatenate(cols(0, 4096) + cols(4112, 9232) + cols(4096, 4104) + zeros(LANE - AH)
                           + cols(4104, 4112) + zeros(NCAT - C_BA - LANE - AH), axis=1)


IN_PIECES = (("pre", C_QKVA, 3072), ("z", C_Z, 1024), ("qb", C_QKVB, 1024), ("kb", C_QKVB + 1024, 1024),
             ("vb", C_QKVB + 2048, 1024), ("ga", C_GATE, 1024), ("gb", C_GATE + 1024, 1024))
_ORIG_SEGS = ((0, 3072, "pre", 0), (3072, 4096, "z", 0), (4096, 4104, "ba", 0), (4104, 4112, "ba", LANE),
              (4112, 5136, "qb", 0), (5136, 6160, "kb", 0), (6160, 7184, "vb", 0), (7184, 8208, "ga", 0),
              (8208, 9232, "gb", 0))


def _orig_cols_from_pieces(gp, lo, hi):
    parts = []
    for a, b, name, off in _ORIG_SEGS:
        s, e = max(a, lo), min(b, hi)
        if s < e:
            parts.append(gp[name][:, off + s - a:off + e - a])
    return parts[0] if len(parts) == 1 else jnp.concatenate(parts, axis=1)


def _pad128(v):
    return jnp.pad(v, ((0, 0), (0, 128 - v.shape[1])))


def local_step(x, tgt, mod, wts, small, late_weights=None, on_grads=None):
    if on_grads is None:
        on_grads = lambda group, gd: jnp.zeros((), F32)
    t = x.shape[0]
    nc = t // CH
    shift_t, scale_t, gate_t, shift_f, scale_f, gate_f = mod
    wcat = _cat_from_slabs(wts["w_in_slabs"])
    a_log = _pad128(small["a_log"])
    dtb = _pad128(small["dt_bias"])
    vecs = dict(bga=small["b_gate"][:, :D], bgb=small["b_gate"][:, D:], gate_t=gate_t, g1=small["ln1_g"],
                b1=small["ln1_b"], scale_f=scale_f, shift_f=shift_f)

    h1 = modulate(x, scale_t, shift_t, "modulate_t")
    proj = matmul(h1, wcat, F32, "in_proj")
    q, k, v, gcs, beta = prep_fwd(proj, small["conv_a"], a_log, dtb)

    u, w, qg, kd, qk, eg, tinv = c1_fwd(q, k, v, gcs, beta)
    oa, sall = c2_fwd(u, w, qg, kd, qk, eg, proj, small["norm_a"])
    bias = bias_table(small["rel_bias"])
    ob, probs = attn_fwd(proj, bias)
    if late_weights is not None:
        wts = {**wts, **late_weights(ob)}
    y1, h2 = merge_fwd(x, oa, ob, proj, vecs, wts["w_a"], wts["w_b"], wts["w_o"])
    up = matmul(h2, wts["w_up"], F32, "up_proj")
    a = ffn_act_fwd(up, small["conv_ffn"], small["b_conv_ffn"])

    da, dy1_res, dffn, dgate_f, dg2, db2, loss = head_fwd_bwd(a, y1, tgt, gate_f, small["ln2_g"], small["ln2_b"],
                                                            wts["w_down"])
    g_w_down = matmul(a, dffn, BF16, "wgrad_down", ta=True)
    dup, g_conv_ffn, g_bconv = ffn_act_bwd(up, small["conv_ffn"], small["b_conv_ffn"], da)
    dh2 = matmul(dup, wts["w_up"], F32, "dgrad_up", tb=True)
    g_w_up = matmul(h2, dup, BF16, "wgrad_up", ta=True)
    tok = on_grads("ffn", dict(w_up=g_w_up, w_down=g_w_down))
    dy1, dscale_f, dshift_f = modulate_bwd(dh2, y1, dy1_res, scale_f + tok, "modulate_f_bwd")
    (dx_res, doa, dob, dga, dgb, merged, dmix, dpa, dpb,
     dbga, dbgb, dgate_t, dg1, db1) = merge_bwd(x, oa, ob, proj, vecs, wts["w_a"], wts["w_b"], wts["w_o"], dy1)
    g_w_o = matmul(merged, dmix, BF16, "wgrad_o", ta=True)
    g_w_a = matmul(oa, dpa, BF16, "wgrad_a", ta=True)
    g_w_b = matmul(ob, dpb, BF16, "wgrad_b", ta=True)
    tok = on_grads("mix", dict(w_o=g_w_o, w_a=g_w_a, w_b=g_w_b))
    dqb, dkb, dvb, dbias = attn_bwd(proj, ob, probs, dob)
    g_rel = relbias_reduce(bias_table_bwd_layout(dbias))
    du, dw, dqg, dkd, dqk, deg, dz, g_norm = c2_bwd(u, w, qg, kd, qk, eg, proj, small["norm_a"] + tok, sall, doa)
    dq, dk, dv, dgcs, dbeta = c1_bwd(q, k, v, gcs, beta, tinv, du, dw, dqg, dkd, dqk, deg)
    dpre, dbb, daa, g_conv_a, g_alog, g_dtb = prep_bwd(proj, small["conv_a"], a_log, dtb, dq, dk, dv, dgcs, dbeta)
    tok = on_grads("small", dict(conv_a=g_conv_a, rel_bias=g_rel, conv_ffn=g_conv_ffn))
    dba = jnp.concatenate([dbb, daa, jnp.zeros((t, NCAT - C_BA - 2 * LANE), BF16)], axis=1) + tok.astype(BF16)
    dpieces = dict(pre=dpre, z=dz, qb=dqb, kb=dkb, vb=dvb, ga=dga, gb=dgb)
    g_in = {n: matmul(h1, dpieces[n], BF16, "wgrad_in_" + n, ta=True) for n, _, _ in IN_PIECES}
    g_in["ba"] = matmul(h1, dba, BF16, "wgrad_in_ba", ta=True)
    tok = on_grads("in", g_in)
    dh1 = dgrad_pieces([(dpieces[n], off) for n, off, _ in IN_PIECES], dba + tok.astype(BF16), wcat,
                       "dgrad_in")
    grad_x, dscale_t, dshift_t = modulate_bwd(dh1, x, dx_res, scale_t + tok, "modulate_t_bwd")

    dmod = (dshift_t, dscale_t, dgate_t, dshift_f, dscale_f, dgate_f)
    grads = dict(w_in=_orig_cols_from_pieces(g_in, 0, 9232), w_up=g_w_up, w_down=g_w_down, w_a=g_w_a, w_b=g_w_b, w_o=g_w_o,
                 conv_a=g_conv_a, rel_bias=g_rel, conv_ffn=g_conv_ffn,
                 b_gate=jnp.concatenate([dbga, dbgb], axis=1), a_log=g_alog[:, :AH], dt_bias=g_dtb[:, :AH],
                 norm_a=g_norm, ln1_g=dg1, ln1_b=db1, b_conv_ffn=g_bconv, ln2_g=dg2, ln2_b=db2)
    return loss[0, 0], grad_x, dmod, grads


REP_NAMES = ["b_ada", "b_gate", "a_log", "dt_bias", "norm_a", "ln1_g", "ln1_b", "b_conv_ffn", "ln2_g", "ln2_b"]
SH_NAMES = ["conv_a", "rel_bias", "conv_ffn"]


def _col_shards(a, n):
    return a.reshape(a.shape[0], NDEV, n).transpose(1, 0, 2)


def kernel(x, c, w_ada, b_ada, w_in, b_gate, conv_a, a_log, dt_bias, norm_a, rel_bias, w_branch_a, w_branch_b, w_o, ln1_g, ln1_b, w_up, conv_ffn, b_conv_ffn, w_down, ln2_g, ln2_b, loss_target, m_w_ada, m_b_ada, m_w_in, m_b_gate, m_conv_a, m_a_log, m_dt_bias, m_norm_a, m_rel_bias, m_w_branch_a, m_w_branch_b, m_w_o, m_ln1_g, m_ln1_b, m_w_up, m_conv_ffn, m_b_conv_ffn, m_w_down, m_ln2_g, m_ln2_b, v_w_ada, v_b_ada, v_w_in, v_b_gate, v_conv_a, v_a_log, v_dt_bias, v_norm_a, v_rel_bias, v_w_branch_a, v_w_branch_b, v_w_o, v_ln1_g, v_ln1_b, v_w_up, v_conv_ffn, v_b_conv_ffn, v_w_down, v_ln2_g, v_ln2_b):
    W = dict(w_ada=w_ada, b_ada=b_ada, w_in=w_in, b_gate=b_gate, conv_a=conv_a, a_log=a_log, dt_bias=dt_bias,
             norm_a=norm_a, rel_bias=rel_bias, w_branch_a=w_branch_a, w_branch_b=w_branch_b, w_o=w_o, ln1_g=ln1_g,
             ln1_b=ln1_b, w_up=w_up, conv_ffn=conv_ffn, b_conv_ffn=b_conv_ffn, w_down=w_down, ln2_g=ln2_g,
             ln2_b=ln2_b)
    M = dict(w_ada=m_w_ada, b_ada=m_b_ada, w_in=m_w_in, b_gate=m_b_gate, conv_a=m_conv_a, a_log=m_a_log,
             dt_bias=m_dt_bias, norm_a=m_norm_a, rel_bias=m_rel_bias, w_branch_a=m_w_branch_a,
             w_branch_b=m_w_branch_b, w_o=m_w_o, ln1_g=m_ln1_g, ln1_b=m_ln1_b, w_up=m_w_up, conv_ffn=m_conv_ffn,
             b_conv_ffn=m_b_conv_ffn, w_down=m_w_down, ln2_g=m_ln2_g, ln2_b=m_ln2_b)
    V = dict(w_ada=v_w_ada, b_ada=v_b_ada, w_in=v_w_in, b_gate=v_b_gate, conv_a=v_conv_a, a_log=v_a_log,
             dt_bias=v_dt_bias, norm_a=v_norm_a, rel_bias=v_rel_bias, w_branch_a=v_w_branch_a,
             w_branch_b=v_w_branch_b, w_o=v_w_o, ln1_g=v_ln1_g, ln1_b=v_ln1_b, w_up=v_w_up, conv_ffn=v_conv_ffn,
             b_conv_ffn=v_b_conv_ffn, w_down=v_w_down, ln2_g=v_ln2_g, ln2_b=v_ln2_b)
    W3, M3, V3 = W, M, V
    W, M, V = ({n: a[0] for n, a in dct.items()} for dct in (W, M, V))
    me = 4 * lax.axis_index("x") + 2 * lax.axis_index("y") + lax.axis_index("c")
    big = ("w_in", "w_up", "w_down", "w_branch_a", "w_branch_b", "w_o")

    g_in = all_gather_two_level(W["w_in"].astype(BF16), "gather_w_in")
    wts = dict(w_in_slabs=g_in)
    c_all, *sh_all = all_gather([c] + [W[n] for n in SH_NAMES], "gather_small")
    c_all = c_all.reshape(NDEV, D)

    def full_small(g8):
        return g8.transpose(1, 0, 2).reshape(g8.shape[1], -1)

    small = dict(conv_a=full_small(sh_all[0]), rel_bias=full_small(sh_all[1]), conv_ffn=full_small(sh_all[2]),
                 b_gate=W["b_gate"][None], a_log=W["a_log"][None], dt_bias=W["dt_bias"][None],
                 norm_a=W["norm_a"][None], ln1_g=W["ln1_g"][None], ln1_b=W["ln1_b"][None],
                 b_conv_ffn=W["b_conv_ffn"][None], ln2_g=W["ln2_g"][None], ln2_b=W["ln2_b"][None])

    nsh = w_ada.shape[2]
    b_sh = lax.dynamic_slice(W["b_ada"][None], (0, me * nsh), (1, nsh))
    mod_sh = ada_fwd(c_all, W["w_ada"], b_sh)
    (mod_rows,) = all_to_all([mod_sh[:, None, :]], "scatter_mod")
    mod6 = mod_rows.reshape(6, D)

    after_small = (g_in[0, 0, 0].astype(F32) * 0.0 + mod6[0, 0] * 0.0).astype(BF16)
    late, late_tok = exchange_start([W[n].astype(BF16) + after_small for n in big[1:]], "gather_late_start", False)

    def late_weights(after):
        g_up, g_down, g_a, g_b, g_o = exchange_wait(late, after, "gather_late_wait")
        return dict(w_up=g_up.transpose(1, 0, 2).reshape(D, -1), w_down=g_down.reshape(DFF, D),
                    w_a=g_a.reshape(D, D), w_b=g_b.reshape(D, D), w_o=g_o.reshape(D, D))

    mod6 = mod6 + late_tok
    mod = tuple(mod6[i:i + 1] for i in range(6))

    pending = {}

    def on_grads(group, gd):
        if group == "small":
            pending["small"] = all_to_all([_col_shards(gd[n], W[n].shape[1]) for n in SH_NAMES],
                                          "scatter_small_grads")
            return pending["small"][0][0, 0, 0] * 0.0
        if group == "ffn":
            slabs = [_col_shards(gd["w_up"], w_up.shape[2]), gd["w_down"].reshape(NDEV, -1, D)]
        elif group == "mix":
            slabs = [gd[n].reshape(NDEV, -1, D) for n in ("w_a", "w_b", "w_o")]
        else:
            nin = w_in.shape[2]
            slabs = [jnp.stack([_orig_cols_from_pieces(gd, j * nin, (j + 1) * nin) for j in range(NDEV)], axis=0)]
        pending[group], tok = exchange_start([s.astype(BF16) for s in slabs], "scatter_" + group + "_start", True)
        return tok

    loss, grad_x, dmod, g = local_step(x[0], loss_target[0], mod, wts, small, late_weights, on_grads)

    rep_grads = {n: g[n] for n in REP_NAMES if n != "b_ada"}
    rep_grads["b_ada"] = jnp.concatenate(dmod, axis=1)
    gathered = all_gather([rep_grads[n] for n in REP_NAMES] + [jnp.broadcast_to(loss, (1, LANE))],
                          "gather_small_grads")
    rep_all = dict(zip(REP_NAMES, gathered))
    sh_recv = [p[:, None] for p in pending["small"]]
    small_names = REP_NAMES + SH_NAMES
    sg, sd, sm, sv, loss_row = adamw_small([rep_all[n] for n in REP_NAMES] + sh_recv,
                                           [W3[n] for n in small_names], [M3[n] for n in small_names],
                                           [V3[n] for n in small_names], gathered[-1], "adamw_small")
    loss_total = loss_row[0, 0]

    dmod_all = rep_all["b_ada"][:, 0]
    dmod_sh = lax.dynamic_slice(dmod_all, (0, me * nsh), (NDEV, nsh))
    g_w_ada = ada_wgrad(c_all.T, dmod_sh)

    p_up, p_down = exchange_wait(pending["ffn"], grad_x, "scatter_ffn_wait")
    p_a, p_b, p_o = exchange_wait(pending["mix"], grad_x, "scatter_mix_wait")
    (p_in,) = exchange_wait(pending["in"], grad_x, "scatter_in_wait")
    parts = [p_in, p_up, p_down, p_a, p_b, p_o]

    res = {}
    for n, p in zip(big, parts):
        res[n] = adamw(p, W3[n], M3[n], V3[n], "adamw_" + n)
    res["w_ada"] = adamw(g_w_ada[None], W3["w_ada"], M3["w_ada"], V3["w_ada"], "adamw_w_ada")
    for i, n in enumerate(small_names):
        res[n] = (sg[i], sd[i], sm[i], sv[i])

    order = ("w_ada", "b_ada", "w_in", "b_gate", "conv_a", "a_log", "dt_bias", "norm_a", "rel_bias", "w_branch_a",
             "w_branch_b", "w_o", "ln1_g", "ln1_b", "w_up", "conv_ffn", "b_conv_ffn", "w_down", "ln2_g", "ln2_b")
    outs = [loss_total, grad_x[None]]
    for kind in range(4):
        outs += [res[n][kind] for n in order]
    return tuple(outs)
```

```python
import functools
import math

import numpy as np
import jax
import jax.numpy as jnp
from jax import lax
from jax.experimental import pallas as pl
from jax.experimental.pallas import tpu as pltpu

F32 = jnp.float32
BF16 = jnp.bfloat16
HI = lax.Precision.HIGHEST

D = 1024
CH = 64
AH, ADK = 8, 128
BH, BDH = 16, 64
BPREV = 8
BMAXREL = 256
RELSZ = CH + BMAXREL
DFF = 2816
ALPHA = 2.0 ** 0.25
LN_EPS, RMS_EPS, L2_EPS = 1e-5, 1e-6, 1e-6
NEG = -1e30
LR, B1, B2, AEPS, WD, STEP = 1e-3, 0.9, 0.999, 1e-8, 0.01, 10
NDEV = 8
HALO = 8
LANE = 128
TQ = 512
VMEM_LIMIT = 56 * 1024 * 1024

C_QKVA, C_Z, C_QKVB, C_GATE, C_BA, NCAT = 0, 3072, 4096, 7168, 9216, 9728


def _cparams(n_axes=1, vmem=VMEM_LIMIT):
    return pltpu.CompilerParams(dimension_semantics=("arbitrary",) * n_axes, vmem_limit_bytes=vmem)


def _dg(a, b, ca, cb):
    return lax.dot_general(a.astype(BF16), b.astype(BF16), (((ca,), (cb,)), ((), ())),
                           preferred_element_type=F32)


@jax.custom_vjp
def mm_nn(a, b):
    return _dg(a, b, 1, 0)


@jax.custom_vjp
def mm_nt(a, b):
    return _dg(a, b, 1, 1)


@jax.custom_vjp
def mm_tn(a, b):
    return _dg(a, b, 0, 0)


mm_nn.defvjp(lambda a, b: (mm_nn(a, b), (a, b)),
             lambda r, g: (mm_nt(g, r[1]).astype(r[0].dtype), mm_tn(r[0], g).astype(r[1].dtype)))
mm_nt.defvjp(lambda a, b: (mm_nt(a, b), (a, b)),
             lambda r, g: (mm_nn(g, r[1]).astype(r[0].dtype), mm_tn(g, r[0]).astype(r[1].dtype)))
mm_tn.defvjp(lambda a, b: (mm_tn(a, b), (a, b)),
             lambda r, g: (mm_nt(r[1], g).astype(r[0].dtype), mm_nn(r[0], g).astype(r[1].dtype)))


@jax.custom_vjp
def mm_w(a, w):
    return _dg(a, w, 1, 0)


mm_w.defvjp(lambda a, w: (mm_w(a, w), (a, w)),
            lambda r, g: (mm_nt(g, r[1]).astype(r[0].dtype), jnp.zeros_like(r[1])))


def _mmh(a, b):
    return lax.dot_general(a, b, (((1,), (0,)), ((), ())), precision=HI, preferred_element_type=F32)


def _bdg(a, b, ca, cb):
    return lax.dot_general(a.astype(BF16), b.astype(BF16), (((ca,), (cb,)), ((0,), (0,))),
                           preferred_element_type=F32)


@jax.custom_vjp
def bmm_nn(a, b):
    return _bdg(a, b, 2, 1)


@jax.custom_vjp
def bmm_nt(a, b):
    return _bdg(a, b, 2, 2)


@jax.custom_vjp
def bmm_tn(a, b):
    return _bdg(a, b, 1, 1)


bmm_nn.defvjp(lambda a, b: (bmm_nn(a, b), (a, b)), lambda r, g: (bmm_nt(g, r[1]), bmm_tn(r[0], g)))
bmm_nt.defvjp(lambda a, b: (bmm_nt(a, b), (a, b)), lambda r, g: (bmm_nn(g, r[1]), bmm_tn(g, r[0])))
bmm_tn.defvjp(lambda a, b: (bmm_tn(a, b), (a, b)), lambda r, g: (bmm_nt(r[1], g), bmm_nn(r[0], g)))


def _bdg3(a, b, ca, cb):
    return lax.dot_general(a, b, (((ca,), (cb,)), ((0,), (0,))), precision=lax.Precision.HIGH,
                           preferred_element_type=F32)


NEWTON_STEPS = 2


def _bdgp(a, b, ca, cb):
    return _bdg(a, b, ca, cb)


@jax.custom_vjp
def bmm3_nn(a, b):
    return _bdgp(a, b, 2, 1)


bmm3_nn.defvjp(lambda a, b: (bmm3_nn(a, b), (a, b)),
               lambda r, g: (_bdgp(g, r[1], 2, 2), _bdgp(r[0], g, 1, 1)))


def _sigmoid(x):
    return 0.5 * jnp.tanh(0.5 * x) + 0.5


def _silu(x):
    return x * _sigmoid(x)


def _softplus(x):
    return jnp.maximum(x, 0.0) + jnp.log(1.0 + jnp.exp(-jnp.abs(x)))


def _layernorm(r, g, b):
    mu = jnp.mean(r, axis=-1, keepdims=True)
    xc = r - mu
    var = jnp.mean(xc * xc, axis=-1, keepdims=True)
    return xc * lax.rsqrt(var + LN_EPS) * g + b


def _iota2(shape, dim):
    return lax.broadcasted_iota(jnp.int32, shape, dim)


@jax.custom_vjp
def causal_conv(ext, rows):
    k = len(rows)
    y = None
    for j in range(k):
        s = k - 1 - j
        r = pltpu.roll(ext, s, 0) if s else ext
        t = r[HALO:] * rows[j]
        y = t if y is None else y + t
    return y


def _causal_conv_fwd(ext, rows):
    return causal_conv(ext, rows), (ext, rows)


def _causal_conv_bwd(res, g):
    ext, rows = res
    n = ext.shape[0]
    k = len(rows)
    gext = jnp.concatenate([jnp.zeros((HALO, g.shape[1]), g.dtype), g], axis=0)
    dext = None
    drows = []
    for j in range(k):
        s = k - 1 - j
        up = pltpu.roll(gext, n - s, 0) if s else gext
        t = up * rows[j]
        dext = t if dext is None else dext + t
        r = pltpu.roll(ext, s, 0) if s else ext
        drows.append(jnp.sum(g * r[HALO:], axis=0, keepdims=True))
    return dext, tuple(drows)


causal_conv.defvjp(_causal_conv_fwd, _causal_conv_bwd)


def _chunk_masks(tm):
    i = _iota2((tm, tm), 0)
    j = _iota2((tm, tm), 1)
    same = (i ^ j) < CH
    lower = jnp.where(same & (j <= i), 1.0, 0.0).astype(F32)
    upper = jnp.where(same & (i <= j), 1.0, 0.0).astype(F32)
    return lower, upper


@jax.custom_vjp
def chunk_cumsum(g):
    lower, _ = _chunk_masks(g.shape[0])
    return _mmh(lower, g)


def _chunk_cumsum_bwd(_, ct):
    _, upper = _chunk_masks(ct.shape[0])
    return (_mmh(upper, ct),)


chunk_cumsum.defvjp(lambda g: (chunk_cumsum(g), None), _chunk_cumsum_bwd)


@jax.custom_vjp
def inv_unit_lower(a):
    n = a.shape[-1]
    eye = jnp.where(_iota2((1, n, n), 1) == _iota2((1, n, n), 2), 1.0, 0.0).astype(F32)
    x = eye - a
    p = _bdg3(a, a, 2, 1)
    steps = int(math.log2(n)) - 1
    for s in range(steps):
        x = x + _bdg3(x, p, 2, 1)
        if s + 1 < steps:
            p = _bdg3(p, p, 2, 1)
    for _ in range(NEWTON_STEPS):
        r = (eye - x) - _bdg3(a, x, 2, 1)
        x = x + _bdg3(x, r, 2, 1)
    return x


def _inv_fwd(a):
    t = inv_unit_lower(a)
    return t, t


def _inv_bwd(t, g):
    return (-_bdgp(_bdgp(t, g, 1, 1), t, 2, 2),)


inv_unit_lower.defvjp(_inv_fwd, _inv_bwd)


@jax.custom_vjp
def inv_known(a, t):
    return t


inv_known.defvjp(lambda a, t: (t, t), lambda t, g: (_inv_bwd(t, g)[0], jnp.zeros_like(t)))


def prep_head_fn(ext, rows, scale):
    s = _silu(causal_conv(ext, rows))
    if scale is None:
        return s
    return s * (lax.rsqrt(jnp.sum(s * s, axis=-1, keepdims=True) + L2_EPS) * scale)


def prep_gate_fn(bb, aa, a_log, dtb):
    g = -jnp.exp(a_log) * _softplus(aa + dtb)
    return chunk_cumsum(g), _sigmoid(bb)


PREP_SCALES = (ADK ** -0.5, 1.0, None)


def _head_cols(a):
    lane = _iota2((1, LANE), 1)
    return jnp.concatenate([jnp.sum(jnp.where(lane == h, a, 0.0), axis=1, keepdims=True)[None]
                            for h in range(AH)], axis=0)


def _head_rows(a):
    at = a.T[:AH]
    sub = _iota2((AH, 1), 0)
    return jnp.concatenate([jnp.sum(jnp.where(sub == h, at, 0.0), axis=0, keepdims=True)[None]
                            for h in range(AH)], axis=0)


def c1_heads(q, k, v, gcs, beta, tinv_saved=None):
    gcol = _head_cols(gcs)
    grow = _head_rows(gcs)
    bcol = _head_cols(beta)
    i = _iota2((1, CH, CH), 1)
    j = _iota2((1, CH, CH), 2)
    causal = j <= i
    strict = j < i
    diff = gcol - grow
    decay = jnp.where(causal, jnp.exp(jnp.where(causal, diff, 0.0)), 0.0)
    kb = k * bcol
    vb = v * bcol
    a_low = jnp.where(strict, bmm_nt(kb, k) * decay, 0.0)
    tinv = inv_unit_lower(a_low) if tinv_saved is None else inv_known(a_low, tinv_saved)
    egc = jnp.exp(gcol)
    u = bmm3_nn(tinv, vb)
    w = bmm3_nn(tinv, kb * egc)
    qk = jnp.where(causal, bmm_nt(q, k) * decay, 0.0)
    glast = jnp.sum(jnp.where(_iota2((1, CH, 1), 1) == CH - 1, gcol, 0.0), axis=1, keepdims=True)
    qg = q * egc
    kd = k * jnp.exp(glast - gcol)
    eg = jnp.exp(glast) * jnp.ones((1, 1, ADK), F32)
    return u, w, qk, qg, kd, eg, tinv


def c2_heads(s, u, w, qk, qg, kd, eg, z, nw):
    vn = u - bmm_nn(w, s)
    o = bmm_nn(qg, s) + bmm_nn(qk, vn)
    s2 = s * eg + bmm_tn(kd, vn)
    ms = jnp.mean(o * o, axis=-1, keepdims=True)
    og = o * lax.rsqrt(ms + RMS_EPS) * nw * _silu(z)
    return og, s2


ATT_SCALE = BDH ** -0.5


def _head_mask(hh):
    lane = _iota2((1, 2 * BDH), 1)
    return jnp.where((lane >= hh * BDH) & (lane < (hh + 1) * BDH), 1.0, 0.0).astype(F32)


def attn_sub_fwd(q, k, v, bias2, r, firstf):
    col = _iota2((1, KWIN), 1) + r * SUBQ
    nokey = jnp.where(col < TQ, firstf, 0.0) * NEG
    out, probs = None, []
    for hh in range(2):
        hm = _head_mask(hh)
        s = mm_nt(q * (hm * ATT_SCALE), k) + (assemble_bias(bias2[hh], r) + nokey)
        p = jnp.exp(s - jnp.max(s, axis=-1, keepdims=True))
        inv = 1.0 / jnp.sum(p, axis=-1, keepdims=True)
        o = mm_nn(p, v) * (inv * hm)
        out = o if out is None else out + o
        probs.append(p * inv)
    return out, probs


def attn_sub_bwd(q, k, v, o, do, probs, r):
    dq, dk, dv, dss = None, None, None, []
    for hh in range(2):
        hm = _head_mask(hh)
        p = probs[hh]
        doh = do * hm
        ds = p * (mm_nt(doh, v) - jnp.sum(doh * o, axis=-1, keepdims=True))
        dqh = mm_nn(ds, k) * (hm * ATT_SCALE)
        dkh = mm_tn(ds, q * (hm * ATT_SCALE))
        dvh = mm_tn(p, doh)
        dq = dqh if dq is None else dq + dqh
        dk = dkh if dk is None else dk + dkh
        dv = dvh if dv is None else dv + dvh
        dss.append(ds)
    return dq, dk, dv, dss


def merge_fn(x, oa, ob, gra, grb, p_pa, p_pb, p_mix, bga, bgb, gate_t, g1, b1, scale_f, shift_f,
             wa, wb, wo):
    ga = _sigmoid(gra + bga)
    gb = _sigmoid(grb + bgb)
    pa = mm_w(oa, wa) + p_pa
    pb = mm_w(ob, wb) + p_pb
    merged = ga * pa + gb * pb
    mix = mm_w(merged, wo) + p_mix
    y1 = _layernorm(ALPHA * x + gate_t * mix, g1, b1)
    return y1, merged


def ffn_act_fn(extg, extv, rows_g, rows_v, bg, bv):
    return _silu(causal_conv(extg, rows_g) + bg) * (causal_conv(extv, rows_v) + bv)


def head_fn(a, y1, p_ffn, gate_f, g2, b2, tgt, wd):
    ffn = mm_w(a, wd) + p_ffn
    y2 = _layernorm(ALPHA * y1 + gate_f * ffn, g2, b2)
    err = y2 - tgt
    return 0.5 * jnp.sum(jnp.mean(err * err, axis=-1, keepdims=True))


def _rows(tm, width, colblk=0, order=None):
    if order is None:
        return pl.BlockSpec((tm, width), lambda i: (i, colblk))
    return pl.BlockSpec((tm, width), lambda i: (order(i), colblk))


def _const(shape):
    nd = len(shape)
    return pl.BlockSpec(shape, lambda *_: (0,) * nd)


def _pick(n, cands):
    for c in cands:
        if n % c == 0:
            return c
    raise ValueError(f"no tile for {n}")


def _tile(n, cap):
    best = None
    for c in range(LANE, min(n, cap) + 1, LANE):
        if n % c == 0:
            best = c
    if best is None:
        raise ValueError(f"no tile for {n}")
    return best


def _onehot_rows(k, j):
    return jnp.where(_iota2((k, 1), 0) == j, 1.0, 0.0).astype(F32)


def _stack_rows(drows):
    k = len(drows)
    out = None
    for j in range(k):
        tj = _onehot_rows(k, j) * drows[j]
        out = tj if out is None else out + tj
    return out


def matmul(a, w, out_dtype, name, ta=False, tb=False):
    kdim, m = a.shape if ta else a.shape[::-1]
    n = w.shape[0] if tb else w.shape[1]
    tm = _tile(m, 2048 if kdim <= 1024 else 1024)
    tn = _tile(n, 1024)
    tk = _tile(kdim, 2560)
    nk = kdim // tk
    a_spec = (pl.BlockSpec((tk, tm), lambda i, j, k: (k, i)) if ta
              else pl.BlockSpec((tm, tk), lambda i, j, k: (i, k)))
    w_spec = (pl.BlockSpec((tn, tk), lambda i, j, k: (j, k)) if tb
              else pl.BlockSpec((tk, tn), lambda i, j, k: (k, j)))

    def body(a_ref, w_ref, o_ref, *scratch):
        p = _dg(a_ref[...], w_ref[...], 0 if ta else 1, 1 if tb else 0)
        if nk == 1:
            o_ref[...] = p.astype(out_dtype)
            return
        acc = scratch[0]
        k = pl.program_id(2)

        @pl.when(k == 0)
        def _():
            acc[...] = p

        @pl.when(k > 0)
        def _():
            acc[...] += p

        @pl.when(k == nk - 1)
        def _():
            o_ref[...] = acc[...].astype(out_dtype)

    return pl.pallas_call(
        body, name=name,
        grid=(m // tm, n // tn, nk),
        in_specs=[a_spec, w_spec],
        out_specs=pl.BlockSpec((tm, tn), lambda i, j, k: (i, j)),
        out_shape=jax.ShapeDtypeStruct((m, n), out_dtype),
        scratch_shapes=[] if nk == 1 else [pltpu.VMEM((tm, tn), F32)],
        compiler_params=_cparams(3),
    )(a, w)


def dgrad_pieces(pieces, tail, w, name):
    m = pieces[0][0].shape[0]
    n, ktot = w.shape
    tk = 1024
    tm = _tile(m, 1024)
    wt = tail.shape[1]
    ranges, k0 = [], 0
    for arr, off in pieces:
        assert off == k0 * tk and arr.shape[1] % tk == 0
        ranges.append((k0, k0 + arr.shape[1] // tk))
        k0 = ranges[-1][1]
    nk = k0
    npc = len(pieces)

    def body(*refs):
        a_refs, t_ref, w_ref, wt_ref, o_ref, acc = refs[:npc], refs[npc], refs[npc + 1], refs[npc + 2], refs[npc + 3], refs[npc + 4]
        k = pl.program_id(1)

        @pl.when(k == 0)
        def _():
            acc[...] = _dg(t_ref[...], wt_ref[...], 1, 1)

        for a_ref, (lo, hi) in zip(a_refs, ranges):
            @pl.when((k >= lo) & (k < hi))
            def _(a_ref=a_ref):
                acc[...] += _dg(a_ref[...], w_ref[...], 1, 1)

        @pl.when(k == nk - 1)
        def _():
            o_ref[...] = acc[...]

    def piece_spec(lo, hi):
        return pl.BlockSpec((tm, tk), lambda i, k: (i, jnp.clip(k - lo, 0, hi - lo - 1)))

    return pl.pallas_call(
        body, name=name, grid=(m // tm, nk),
        in_specs=[piece_spec(lo, hi) for lo, hi in ranges] + [
            pl.BlockSpec((tm, wt), lambda i, k: (i, 0)),
            pl.BlockSpec((n, tk), lambda i, k: (0, k)),
            pl.BlockSpec((n, wt), lambda i, k: (0, (ktot - wt) // wt))],
        out_specs=pl.BlockSpec((tm, n), lambda i, k: (i, 0)),
        out_shape=jax.ShapeDtypeStruct((m, n), F32),
        scratch_shapes=[pltpu.VMEM((tm, n), F32)],
        compiler_params=_cparams(2),
    )(*[a for a, _ in pieces], tail, w, w)


def modulate(x, scale, shift, name):
    t, d = x.shape
    tm = _pick(t, (512, 256, 128))

    def body(x_ref, sc_ref, sh_ref, o_ref):
        o_ref[...] = (x_ref[...] * (1.0 + sc_ref[...]) + sh_ref[...]).astype(BF16)

    return pl.pallas_call(
        body, name=name, grid=(t // tm,),
        in_specs=[_rows(tm, d), _const((1, d)), _const((1, d))],
        out_specs=_rows(tm, d),
        out_shape=jax.ShapeDtypeStruct((t, d), BF16),
        compiler_params=_cparams(),
    )(x, scale, shift)


def modulate_bwd(dh, xin, dres, scale, name):
    t, d = dh.shape
    tm = _pick(t, (512, 256, 128))

    def body(dh_ref, x_ref, dr_ref, sc_ref, o_ref, dsc_ref, dsh_ref):
        i = pl.program_id(0)
        dh_v = dh_ref[...]
        o_ref[...] = dr_ref[...] + dh_v * (1.0 + sc_ref[...])

        @pl.when(i == 0)
        def _():
            dsc_ref[...] = jnp.zeros_like(dsc_ref)
            dsh_ref[...] = jnp.zeros_like(dsh_ref)

        dsc_ref[...] += jnp.sum(dh_v * x_ref[...], axis=0, keepdims=True)
        dsh_ref[...] += jnp.sum(dh_v, axis=0, keepdims=True)

    return pl.pallas_call(
        body, name=name, grid=(t // tm,),
        in_specs=[_rows(tm, d), _rows(tm, d), _rows(tm, d), _const((1, d))],
        out_specs=[_rows(tm, d), _const((1, d)), _const((1, d))],
        out_shape=[jax.ShapeDtypeStruct((t, d), F32), jax.ShapeDtypeStruct((1, d), F32),
                   jax.ShapeDtypeStruct((1, d), F32)],
        compiler_params=_cparams(),
    )(dh, xin, dres, scale)


PREP_TM = 128


def _halo_specs(tm, width, colblk, order):
    per = tm // HALO
    return [pl.BlockSpec((HALO, width), lambda i: (jnp.maximum(order(i) * per - 1, 0), colblk)),
            pl.BlockSpec((tm, width), lambda i: (order(i), colblk))]


def prep_fwd(proj, conv_a, a_log, dtb):
    t = proj.shape[0]
    tm = PREP_TM
    nt = t // tm
    wq = 3 * D

    def body(prev_ref, cur_ref, bb_ref, aa_ref, cw_ref, al_ref, dt_ref, q_ref, k_ref, v_ref, g_ref, b_ref):
        i = pl.program_id(0)
        flag = jnp.where(i > 0, 1.0, 0.0)
        for part, o_ref in enumerate((q_ref, k_ref, v_ref)):
            for h in range(AH):
                sl = slice(part * D + h * ADK, part * D + (h + 1) * ADK)
                ext = jnp.concatenate([prev_ref[:, sl] * flag, cur_ref[:, sl]], axis=0)
                rows = tuple(cw_ref[j:j + 1, sl] for j in range(4))
                o_ref[h] = prep_head_fn(ext, rows, PREP_SCALES[part])
        gcs, beta = prep_gate_fn(bb_ref[...], aa_ref[...], al_ref[...], dt_ref[...])
        g_ref[...] = gcs
        b_ref[...] = beta

    ident = lambda i: i
    hm = pl.BlockSpec((AH, tm, ADK), lambda i: (0, i, 0))
    return pl.pallas_call(
        body, name="prep_fwd", grid=(nt,),
        in_specs=_halo_specs(tm, wq, 0, ident) + [
            _rows(tm, 128, C_BA // 128), _rows(tm, 128, C_BA // 128 + 1),
            _const((4, wq)), _const((1, 128)), _const((1, 128))],
        out_specs=[hm, hm, hm, _rows(tm, 128), _rows(tm, 128)],
        out_shape=[jax.ShapeDtypeStruct((AH, t, ADK), F32)] * 3 + [jax.ShapeDtypeStruct((t, 128), F32)] * 2,
        compiler_params=_cparams(),
    )(proj, proj, proj, proj, conv_a, a_log, dtb)


def prep_bwd(proj, conv_a, a_log, dtb, dq, dk, dv, dgcs, dbeta):
    t = proj.shape[0]
    tm = PREP_TM
    nt = t // tm
    wq = 3 * D
    rev = lambda i: nt - 1 - i

    def body(prev_ref, cur_ref, bb_ref, aa_ref, cw_ref, al_ref, dt_ref,
             dq_ref, dk_ref, dv_ref, dg_ref, db_ref,
             dpre_ref, dbb_ref, daa_ref, dcw_ref, dal_ref, ddt_ref, carry):
        i = pl.program_id(0)
        flag = jnp.where(i < nt - 1, 1.0, 0.0)

        @pl.when(i == 0)
        def _():
            carry[...] = jnp.zeros_like(carry)
            dcw_ref[...] = jnp.zeros_like(dcw_ref)
            dal_ref[...] = jnp.zeros_like(dal_ref)
            ddt_ref[...] = jnp.zeros_like(ddt_ref)

        for part, d_ref in enumerate((dq_ref, dk_ref, dv_ref)):
            for h in range(AH):
                sl = slice(part * D + h * ADK, part * D + (h + 1) * ADK)
                ext = jnp.concatenate([prev_ref[:, sl] * flag, cur_ref[:, sl]], axis=0)
                rows = tuple(cw_ref[j:j + 1, sl] for j in range(4))
                _, vjp = jax.vjp(lambda e, r: prep_head_fn(e, r, PREP_SCALES[part]), ext, rows)
                dext, drows = vjp(d_ref[h])
                dcur = dext[HALO:]
                dpre_ref[:, sl] = jnp.concatenate([dcur[:tm - HALO], dcur[tm - HALO:] + carry[:, sl]],
                                                  axis=0).astype(BF16)
                carry[:, sl] = dext[:HALO]
                dcw_ref[:, sl] += _stack_rows(drows)
        _, vjp = jax.vjp(prep_gate_fn, bb_ref[...], aa_ref[...], al_ref[...], dt_ref[...])
        dbb, daa, dal, ddt = vjp((dg_ref[...], db_ref[...]))
        dbb_ref[...] = dbb.astype(BF16)
        daa_ref[...] = daa.astype(BF16)
        dal_ref[...] += dal
        ddt_ref[...] += ddt

    hm = pl.BlockSpec((AH, tm, ADK), lambda i: (0, rev(i), 0))
    return pl.pallas_call(
        body, name="prep_bwd", grid=(nt,),
        in_specs=_halo_specs(tm, wq, 0, rev) + [
            _rows(tm, 128, C_BA // 128, rev), _rows(tm, 128, C_BA // 128 + 1, rev),
            _const((4, wq)), _const((1, 128)), _const((1, 128)),
            hm, hm, hm, _rows(tm, 128, 0, rev), _rows(tm, 128, 0, rev)],
        out_specs=[_rows(tm, wq, 0, rev), _rows(tm, 128, 0, rev), _rows(tm, 128, 0, rev),
                   _const((4, wq)), _const((1, 128)), _const((1, 128))],
        out_shape=[jax.ShapeDtypeStruct((t, wq), BF16), jax.ShapeDtypeStruct((t, 128), BF16),
                   jax.ShapeDtypeStruct((t, 128), BF16), jax.ShapeDtypeStruct((4, wq), F32),
                   jax.ShapeDtypeStruct((1, 128), F32), jax.ShapeDtypeStruct((1, 128), F32)],
        scratch_shapes=[pltpu.VMEM((HALO, wq), F32)],
        compiler_params=_cparams(),
    )(proj, proj, proj, proj, conv_a, a_log, dtb, dq, dk, dv, dgcs, dbeta)


def _c1_specs(order):
    hm = pl.BlockSpec((AH, CH, ADK), lambda n: (0, order(n), 0))
    col = pl.BlockSpec((CH, LANE), lambda n: (order(n), 0))
    qk = pl.BlockSpec((1, AH, CH, CH), lambda n: (order(n), 0, 0, 0))
    eg = pl.BlockSpec((1, AH, 1, ADK), lambda n: (order(n), 0, 0, 0))
    return hm, col, qk, eg


def _heads(ref):
    return jnp.stack([ref[:, h * ADK:(h + 1) * ADK] for h in range(AH)], axis=0)


def c1_fwd(q, k, v, gcs, beta):
    t = q.shape[1]
    nc = t // CH
    hm, col, qks, egs = _c1_specs(lambda n: n)

    def body(q_ref, k_ref, v_ref, g_ref, b_ref, u_ref, w_ref, qg_ref, kd_ref, qk_ref, eg_ref, ti_ref):
        u, w, qk, qg, kd, eg, tinv = c1_heads(q_ref[...], k_ref[...], v_ref[...], g_ref[...], b_ref[...])
        u_ref[...] = u
        w_ref[...] = w.astype(BF16)
        qg_ref[...] = qg.astype(BF16)
        kd_ref[...] = kd.astype(BF16)
        qk_ref[0] = qk.astype(BF16)
        eg_ref[0] = eg
        ti_ref[0] = tinv

    return pl.pallas_call(
        body, name="c1_fwd", grid=(nc,),
        in_specs=[hm, hm, hm, col, col],
        out_specs=[hm, hm, hm, hm, qks, egs, qks],
        out_shape=[jax.ShapeDtypeStruct((AH, t, ADK), F32)] + [jax.ShapeDtypeStruct((AH, t, ADK), BF16)] * 3 + [
            jax.ShapeDtypeStruct((nc, AH, CH, CH), BF16), jax.ShapeDtypeStruct((nc, AH, 1, ADK), F32),
            jax.ShapeDtypeStruct((nc, AH, CH, CH), F32)],
        compiler_params=_cparams(),
    )(q, k, v, gcs, beta)


def c1_bwd(q, k, v, gcs, beta, tinv, du, dw, dqg, dkd, dqk, deg):
    t = q.shape[1]
    nc = t // CH
    hm, col, qks, egs = _c1_specs(lambda n: n)

    def body(q_ref, k_ref, v_ref, g_ref, b_ref, ti_ref, du_ref, dw_ref, dqg_ref, dkd_ref, dqk_ref, deg_ref,
             dq_ref, dk_ref, dv_ref, dg_ref, db_ref):
        _, vjp = jax.vjp(lambda q_, k_, v_, g_, b_: c1_heads(q_, k_, v_, g_, b_, ti_ref[0]),
                         q_ref[...], k_ref[...], v_ref[...], g_ref[...], b_ref[...])
        dq, dk, dv, dg, db = vjp((du_ref[...].astype(F32), dw_ref[...].astype(F32), dqk_ref[0], dqg_ref[...],
                                  dkd_ref[...], deg_ref[0],
                                  jnp.zeros((AH, CH, CH), F32)))
        dq_ref[...] = dq
        dk_ref[...] = dk
        dv_ref[...] = dv
        dg_ref[...] = dg
        db_ref[...] = db

    return pl.pallas_call(
        body, name="c1_bwd", grid=(nc,),
        in_specs=[hm, hm, hm, col, col, qks, hm, hm, hm, hm, qks, egs],
        out_specs=[hm, hm, hm, col, col],
        out_shape=[jax.ShapeDtypeStruct((AH, t, ADK), F32)] * 3 + [jax.ShapeDtypeStruct((t, LANE), F32)] * 2,
        compiler_params=_cparams(),
    )(q, k, v, gcs, beta, tinv, du, dw, dqg, dkd, dqk, deg)


def c2_fwd(u, w, qg, kd, qk, eg, proj, norm_a):
    t = u.shape[1]
    nc = t // CH
    hm, _, qks, egs = _c1_specs(lambda n: n)
    tok = pl.BlockSpec((CH, D), lambda n: (n, 0))
    zspec = pl.BlockSpec((CH, D), lambda n: (n, C_Z // D))
    sspec = pl.BlockSpec((1, AH, ADK, ADK), lambda n: (n, 0, 0, 0))

    def body(u_ref, w_ref, qg_ref, kd_ref, qk_ref, eg_ref, z_ref, nw_ref, o_ref, sall_ref, st):
        n = pl.program_id(0)

        @pl.when(n == 0)
        def _():
            st[...] = jnp.zeros_like(st)

        s = st[...]
        sall_ref[0] = s
        og, s2 = c2_heads(s, u_ref[...], w_ref[...], qk_ref[0], qg_ref[...], kd_ref[...], eg_ref[0],
                          _heads(z_ref), nw_ref[...])
        st[...] = s2
        for h in range(AH):
            o_ref[:, h * ADK:(h + 1) * ADK] = og[h].astype(BF16)

    return pl.pallas_call(
        body, name="c2_fwd", grid=(nc,),
        in_specs=[hm, hm, hm, hm, qks, egs, zspec, _const((1, ADK))],
        out_specs=[tok, sspec],
        out_shape=[jax.ShapeDtypeStruct((t, D), BF16), jax.ShapeDtypeStruct((nc, AH, ADK, ADK), F32)],
        scratch_shapes=[pltpu.VMEM((AH, ADK, ADK), F32)],
        compiler_params=_cparams(),
    )(u, w, qg, kd, qk, eg, proj, norm_a)


def c2_bwd(u, w, qg, kd, qk, eg, proj, norm_a, sall, do):
    t = u.shape[1]
    nc = t // CH
    rev = lambda n: nc - 1 - n
    hm, _, qks, egs = _c1_specs(rev)
    tok = pl.BlockSpec((CH, D), lambda n: (rev(n), 0))
    zspec = pl.BlockSpec((CH, D), lambda n: (rev(n), C_Z // D))
    sspec = pl.BlockSpec((1, AH, ADK, ADK), lambda n: (rev(n), 0, 0, 0))

    def body(u_ref, w_ref, qg_ref, kd_ref, qk_ref, eg_ref, z_ref, nw_ref, sall_ref, do_ref,
             du_ref, dw_ref, dqg_ref, dkd_ref, dqk_ref, deg_ref, dz_ref, dnw_ref, dst):
        n = pl.program_id(0)

        @pl.when(n == 0)
        def _():
            dst[...] = jnp.zeros_like(dst)
            dnw_ref[...] = jnp.zeros_like(dnw_ref)

        _, vjp = jax.vjp(c2_heads, sall_ref[0], u_ref[...], w_ref[...].astype(F32), qk_ref[0].astype(F32),
                         qg_ref[...].astype(F32), kd_ref[...].astype(F32), eg_ref[0], _heads(z_ref), nw_ref[...])
        ds, du, dw, dqk, dqg, dkd, deg, dz, dn = vjp((_heads(do_ref), dst[...]))
        dst[...] = ds
        du_ref[...] = du.astype(BF16)
        dw_ref[...] = dw.astype(BF16)
        dqg_ref[...] = dqg
        dkd_ref[...] = dkd
        dqk_ref[0] = dqk
        deg_ref[0] = deg
        for h in range(AH):
            dz_ref[:, h * ADK:(h + 1) * ADK] = dz[h].astype(BF16)
        dnw_ref[...] += dn

    return pl.pallas_call(
        body, name="c2_bwd", grid=(nc,),
        in_specs=[hm, hm, hm, hm, qks, egs, zspec, _const((1, ADK)), sspec, tok],
        out_specs=[hm, hm, hm, hm, qks, egs, tok, _const((1, ADK))],
        out_shape=[jax.ShapeDtypeStruct((AH, t, ADK), BF16)] * 2 + [jax.ShapeDtypeStruct((AH, t, ADK), F32)] * 2 + [
            jax.ShapeDtypeStruct((nc, AH, CH, CH), F32), jax.ShapeDtypeStruct((nc, AH, 1, ADK), F32),
            jax.ShapeDtypeStruct((t, D), BF16), jax.ShapeDtypeStruct((1, ADK), F32)],
        scratch_shapes=[pltpu.VMEM((AH, ADK, ADK), F32)],
        compiler_params=_cparams(),
    )(u, w, qg, kd, qk, eg, proj, norm_a, sall, do)


NQB = TQ // CH
NKB = 2 * TQ // CH
NDIST = BPREV + 1
KLO = -(NQB - 2)
NPAIR = NKB - 1 - KLO + 1


def bias_table(rel_bias):
    nh = rel_bias.shape[0]
    relx = jnp.concatenate([rel_bias, jnp.broadcast_to(rel_bias[:, -1:], (nh, CH * BPREV + 2 * CH - 1 - RELSZ))],
                           axis=1)
    t = jnp.stack([relx[:, CH * k:CH * k + 2 * CH - 1] for k in range(NDIST)], axis=1)
    trev = t[:, :, ::-1]
    g2 = jnp.concatenate([trev[:, :, CH - 1:], jnp.zeros((nh, NDIST, 1), F32), trev[:, :, :CH - 1]], axis=2)
    flat = jnp.tile(g2, (1, 1, CH + 1))[:, :, :CH * (2 * CH - 1)]
    blk = flat.reshape(nh, NDIST, CH, 2 * CH - 1)[..., :CH]
    neg = jnp.full((nh, NQB - 1, CH, CH), NEG, F32)
    asc = jnp.concatenate([neg, blk, neg], axis=1)
    return jnp.concatenate([asc[:, 1:], asc[:, :-1]], axis=-1)


SUBQ = 4 * CH
NSUB = TQ // SUBQ
KWIN = SUBQ + BPREV * CH


def assemble_bias(tab, r):
    b0 = r * SUBQ // (2 * CH)
    rows = [jnp.concatenate([tab[NQB + a - 2 * b - KLO] for b in range(b0, b0 + KWIN // (2 * CH))], axis=1)
            for a in range(r * SUBQ // CH, (r + 1) * SUBQ // CH)]
    return jnp.concatenate(rows, axis=0)


def bias_table_bwd_layout(dtab):
    nh = dtab.shape[0]
    dasc = (jnp.pad(dtab[..., :CH], ((0, 0), (1, 0), (0, 0), (0, 0)))
            + jnp.pad(dtab[..., CH:], ((0, 0), (0, 1), (0, 0), (0, 0))))
    dblk = dasc[:, NQB - 1:NQB - 1 + NDIST]
    dr = jnp.pad(dblk, ((0, 0), (0, 0), (0, 0), (0, CH - 1)))
    flat = jnp.pad(dr.reshape(nh, NDIST, CH * (2 * CH - 1)), ((0, 0), (0, 0), (0, 3 * CH)))
    return flat.reshape(nh, NDIST, CH + 1, 2 * CH).transpose(0, 2, 1, 3).reshape(nh, CH + 1, NDIST * 2 * CH)


def _fold_matrix_np():
    f = np.zeros((NDIST * 2 * CH, 384), np.float32)
    for k in range(NDIST):
        s = k
        for xx in range(2 * CH):
            if xx == CH:
                continue
            m = CH - 1 - xx if xx < CH else 3 * CH - 1 - xx
            f[s * 2 * CH + xx, min(CH * k + m, RELSZ - 1)] = 1.0
    return f


def relbias_reduce(dlay):
    nh, rows, cols = dlay.shape
    rpad = (-rows) % 8
    dlay = jnp.pad(dlay, ((0, 0), (0, rpad), (0, 0)))
    fold = jnp.asarray(_fold_matrix_np())

    def body(d_ref, f_ref, o_ref):
        cs = jnp.sum(d_ref[0], axis=0, keepdims=True)
        o_ref[0] = _mmh(jnp.broadcast_to(cs, (8, cols)), f_ref[...])

    out = pl.pallas_call(
        body, name="relbias_reduce", grid=(nh,),
        in_specs=[pl.BlockSpec((1, rows + rpad, cols), lambda h: (h, 0, 0)), _const((cols, 384))],
        out_specs=pl.BlockSpec((1, 8, 384), lambda h: (h, 0, 0)),
        out_shape=jax.ShapeDtypeStruct((nh, 8, 384), F32),
        compiler_params=_cparams(),
    )(dlay, fold)
    return out[:, 0, :RELSZ]


def attn_fwd(proj, bias):
    t = proj.shape[0]
    nt = t // TQ
    cb = C_QKVB // 128

    def body(q_ref, kp_ref, kc_ref, vp_ref, vc_ref, b_ref, o_ref, p_ref):
        i = pl.program_id(1)
        firstf = jnp.where(i == 0, 1.0, 0.0)
        for r in range(NSUB):
            lo, hi = r * SUBQ, r * SUBQ + KWIN - TQ
            kw = jnp.concatenate([kp_ref[lo:, :], kc_ref[:hi, :]], axis=0)
            vw = jnp.concatenate([vp_ref[lo:, :], vc_ref[:hi, :]], axis=0)
            out, probs = attn_sub_fwd(q_ref[lo:lo + SUBQ, :].astype(F32), kw, vw, b_ref[...], r, firstf)
            o_ref[lo:lo + SUBQ, :] = out.astype(BF16)
            for hh in range(2):
                p_ref[hh, lo:lo + SUBQ, :] = probs[hh].astype(BF16)

    def blk(off, prev):
        if prev:
            return pl.BlockSpec((TQ, 128), lambda p, i: (jnp.maximum(i - 1, 0), cb + off + p))
        return pl.BlockSpec((TQ, 128), lambda p, i: (i, cb + off + p))

    return pl.pallas_call(
        body, name="attn_fwd", grid=(BH // 2, nt),
        in_specs=[blk(0, False), blk(8, True), blk(8, False), blk(16, True), blk(16, False),
                  pl.BlockSpec((2, NPAIR, CH, 2 * CH), lambda p, i: (p, 0, 0, 0))],
        out_specs=[pl.BlockSpec((TQ, 128), lambda p, i: (i, p)),
                   pl.BlockSpec((2, TQ, KWIN), lambda p, i: (p, i, 0))],
        out_shape=[jax.ShapeDtypeStruct((t, D), BF16), jax.ShapeDtypeStruct((BH, t, KWIN), BF16)],
        compiler_params=_cparams(2),
    )(proj, proj, proj, proj, proj, bias)


def attn_bwd(proj, ob, probs, do):
    t = proj.shape[0]
    nt = t // TQ
    cb = C_QKVB // 128

    def body(q_ref, kp_ref, kc_ref, vp_ref, vc_ref, o_ref, p_ref, do_ref,
             dq_ref, dk_ref, dv_ref, db_ref, ck, cv, ak, av):
        i = pl.program_id(1)

        @pl.when(i == 0)
        def _():
            ck[...] = jnp.zeros_like(ck)
            cv[...] = jnp.zeros_like(cv)
            db_ref[...] = jnp.zeros_like(db_ref)

        @pl.when(i < nt)
        def _():
            ak[...] = jnp.zeros_like(ak)
            av[...] = jnp.zeros_like(av)
            for r in range(NSUB):
                lo, hi = r * SUBQ, r * SUBQ + KWIN - TQ
                rows = slice(lo, lo + SUBQ)
                kw = jnp.concatenate([kp_ref[lo:, :], kc_ref[:hi, :]], axis=0)
                vw = jnp.concatenate([vp_ref[lo:, :], vc_ref[:hi, :]], axis=0)
                probs_r = [p_ref[hh, rows, :].astype(F32) for hh in range(2)]
                dq, dkw, dvw, dss = attn_sub_bwd(q_ref[rows, :].astype(F32), kw, vw, o_ref[rows, :].astype(F32),
                                                 do_ref[rows, :], probs_r, r)
                dq_ref[rows, :] = dq.astype(BF16)
                ak[lo:lo + KWIN, :] += dkw
                av[lo:lo + KWIN, :] += dvw
                for hh in range(2):
                    _, scatter = jax.vjp(lambda tab: assemble_bias(tab, r), jnp.zeros((NPAIR, CH, 2 * CH), F32))
                    db_ref[hh] += scatter(dss[hh])[0]
            dk_ref[...] = (ck[...] + ak[:TQ, :]).astype(BF16)
            dv_ref[...] = (cv[...] + av[:TQ, :]).astype(BF16)
            ck[...] = ak[TQ:, :]
            cv[...] = av[TQ:, :]

        @pl.when(i == nt)
        def _():
            dk_ref[...] = ck[...].astype(BF16)
            dv_ref[...] = cv[...].astype(BF16)

    def blk(off, prev):
        if prev:
            return pl.BlockSpec((TQ, 128), lambda p, i: (jnp.clip(i - 1, 0, nt - 1), cb + off + p))
        return pl.BlockSpec((TQ, 128), lambda p, i: (jnp.minimum(i, nt - 1), cb + off + p))

    own = pl.BlockSpec((TQ, 128), lambda p, i: (jnp.minimum(i, nt - 1), p))
    lag = pl.BlockSpec((TQ, 128), lambda p, i: (jnp.maximum(i - 1, 0), p))
    return pl.pallas_call(
        body, name="attn_bwd", grid=(BH // 2, nt + 1),
        in_specs=[blk(0, False), blk(8, True), blk(8, False), blk(16, True), blk(16, False), own,
                  pl.BlockSpec((2, TQ, KWIN), lambda p, i: (p, jnp.minimum(i, nt - 1), 0)), own],
        out_specs=[own, lag, lag, pl.BlockSpec((2, NPAIR, CH, 2 * CH), lambda p, i: (p, 0, 0, 0))],
        out_shape=[jax.ShapeDtypeStruct((t, D), BF16)] * 3 + [jax.ShapeDtypeStruct((BH, NPAIR, CH, 2 * CH), F32)],
        scratch_shapes=[pltpu.VMEM((TQ, 128), F32), pltpu.VMEM((TQ, 128), F32),
                        pltpu.VMEM((2 * TQ, 128), F32), pltpu.VMEM((2 * TQ, 128), F32)],
        compiler_params=_cparams(2),
    )(proj, proj, proj, proj, proj, ob, probs, do)


MERGE_TM = 256


def merge_fwd(x, oa, ob, proj, vecs, wa, wb, wo):
    t = x.shape[0]
    tm = MERGE_TM
    names = ("bga", "bgb", "gate_t", "g1", "b1", "scale_f", "shift_f")

    def body(x_ref, oa_ref, ob_ref, gra_ref, grb_ref, *rest):
        vrefs = rest[:7]
        wa_ref, wb_ref, wo_ref, y_ref, h_ref = rest[7:]
        vv = [r[...] for r in vrefs]
        zero = jnp.zeros((tm, D), F32)
        y1, _ = merge_fn(x_ref[...], oa_ref[...], ob_ref[...], gra_ref[...], grb_ref[...], zero, zero, zero,
                         *vv, wa_ref[...], wb_ref[...], wo_ref[...])
        y_ref[...] = y1
        h_ref[...] = (y1 * (1.0 + vv[5]) + vv[6]).astype(BF16)

    return pl.pallas_call(
        body, name="merge_fwd", grid=(t // tm,),
        in_specs=[_rows(tm, D), _rows(tm, D), _rows(tm, D), _rows(tm, D, C_GATE // D), _rows(tm, D, C_GATE // D + 1)]
        + [_const((1, D))] * 7 + [_const((D, D))] * 3,
        out_specs=[_rows(tm, D), _rows(tm, D)],
        out_shape=[jax.ShapeDtypeStruct((t, D), F32), jax.ShapeDtypeStruct((t, D), BF16)],
        compiler_params=_cparams(),
    )(x, oa, ob, proj, proj, *[vecs[n] for n in names], wa, wb, wo)


def merge_bwd(x, oa, ob, proj, vecs, wa, wb, wo, dy1):
    t = x.shape[0]
    tm = MERGE_TM
    names = ("bga", "bgb", "gate_t", "g1", "b1", "scale_f", "shift_f")

    def body(x_ref, oa_ref, ob_ref, gra_ref, grb_ref, *rest):
        vrefs = rest[:7]
        wa_ref, wb_ref, wo_ref, dy_ref = rest[7:11]
        (dx_ref, doa_ref, dob_ref, dga_ref, dgb_ref, mg_ref, dmix_ref, dpa_ref, dpb_ref,
         dbga_ref, dbgb_ref, dgt_ref, dg1_ref, db1_ref) = rest[11:]
        i = pl.program_id(0)
        vv = [r[...] for r in vrefs]
        zero = jnp.zeros((tm, D), F32)

        def f(x_, oa_, ob_, gra_, grb_, ppa, ppb, pmix, bga, bgb, gate_t, g1, b1):
            return merge_fn(x_, oa_, ob_, gra_, grb_, ppa, ppb, pmix, bga, bgb, gate_t, g1, b1, vv[5], vv[6],
                            wa_ref[...], wb_ref[...], wo_ref[...])

        _, vjp, merged = jax.vjp(f, x_ref[...], oa_ref[...].astype(F32), ob_ref[...].astype(F32),
                                 gra_ref[...], grb_ref[...], zero, zero, zero, *vv[:5], has_aux=True)
        dx, doa, dob, dga, dgb, dpa, dpb, dmix, dbga, dbgb, dgt, dg1, db1 = vjp(dy_ref[...])
        dx_ref[...] = dx
        doa_ref[...] = doa
        dob_ref[...] = dob
        dga_ref[...] = dga.astype(BF16)
        dgb_ref[...] = dgb.astype(BF16)
        mg_ref[...] = merged.astype(BF16)
        dmix_ref[...] = dmix.astype(BF16)
        dpa_ref[...] = dpa.astype(BF16)
        dpb_ref[...] = dpb.astype(BF16)
        accs = (dbga_ref, dbgb_ref, dgt_ref, dg1_ref, db1_ref)

        @pl.when(i == 0)
        def _():
            for a in accs:
                a[...] = jnp.zeros_like(a)

        for a, val in zip(accs, (dbga, dbgb, dgt, dg1, db1)):
            a[...] += val

    return pl.pallas_call(
        body, name="merge_bwd", grid=(t // tm,),
        in_specs=[_rows(tm, D), _rows(tm, D), _rows(tm, D), _rows(tm, D, C_GATE // D), _rows(tm, D, C_GATE // D + 1)]
        + [_const((1, D))] * 7 + [_const((D, D))] * 3 + [_rows(tm, D)],
        out_specs=[_rows(tm, D)] * 9 + [_const((1, D))] * 5,
        out_shape=[jax.ShapeDtypeStruct((t, D), F32)] * 3 + [jax.ShapeDtypeStruct((t, D), BF16)] * 6
        + [jax.ShapeDtypeStruct((1, D), F32)] * 5,
        compiler_params=_cparams(),
    )(x, oa, ob, proj, proj, *[vecs[n] for n in names], wa, wb, wo, dy1)


FFN_TM = 128


def ffn_act_fwd(up, conv_w, bconv):
    t, wdt = up.shape
    tm = FFN_TM

    def body(prev_ref, cur_ref, cw_ref, bc_ref, a_ref):
        i = pl.program_id(0)
        flag = jnp.where(i > 0, 1.0, 0.0)

        def ext(sl):
            return jnp.concatenate([prev_ref[:, sl] * flag, cur_ref[:, sl]], axis=0)

        def rows(sl):
            return tuple(cw_ref[j:j + 1, sl] for j in range(3))

        for cb in range(DFF // LANE):
            g = slice(cb * LANE, (cb + 1) * LANE)
            v = slice(DFF + cb * LANE, DFF + (cb + 1) * LANE)
            a_ref[:, g] = ffn_act_fn(ext(g), ext(v), rows(g), rows(v), bc_ref[:, g], bc_ref[:, v]).astype(BF16)

    return pl.pallas_call(
        body, name="ffn_act_fwd", grid=(t // tm,),
        in_specs=_halo_specs(tm, wdt, 0, lambda i: i) + [_const((3, wdt)), _const((1, wdt))],
        out_specs=_rows(tm, DFF),
        out_shape=jax.ShapeDtypeStruct((t, DFF), BF16),
        compiler_params=_cparams(),
    )(up, up, conv_w, bconv)


def ffn_act_bwd(up, conv_w, bconv, da):
    t, wdt = up.shape
    tm = FFN_TM
    nt = t // tm
    rev = lambda i: nt - 1 - i

    def body(prev_ref, cur_ref, cw_ref, bc_ref, da_ref, dup_ref, dcw_ref, dbc_ref, carry):
        i = pl.program_id(0)
        flag = jnp.where(i < nt - 1, 1.0, 0.0)

        @pl.when(i == 0)
        def _():
            carry[...] = jnp.zeros_like(carry)
            dcw_ref[...] = jnp.zeros_like(dcw_ref)
            dbc_ref[...] = jnp.zeros_like(dbc_ref)

        def ext(sl):
            return jnp.concatenate([prev_ref[:, sl] * flag, cur_ref[:, sl]], axis=0)

        def rows(sl):
            return tuple(cw_ref[j:j + 1, sl] for j in range(3))

        def emit(sl, dext, drows, dbc):
            dcur = dext[HALO:]
            dup_ref[:, sl] = jnp.concatenate([dcur[:tm - HALO], dcur[tm - HALO:] + carry[:, sl]], axis=0).astype(BF16)
            carry[:, sl] = dext[:HALO]
            dcw_ref[:, sl] += _stack_rows(drows)
            dbc_ref[:, sl] += dbc

        for cb in range(DFF // LANE):
            g = slice(cb * LANE, (cb + 1) * LANE)
            v = slice(DFF + cb * LANE, DFF + (cb + 1) * LANE)
            _, vjp = jax.vjp(ffn_act_fn, ext(g), ext(v), rows(g), rows(v), bc_ref[:, g], bc_ref[:, v])
            dxg, dxv, drg, drv, dbg, dbv = vjp(da_ref[:, g])
            emit(g, dxg, drg, dbg)
            emit(v, dxv, drv, dbv)

    return pl.pallas_call(
        body, name="ffn_act_bwd", grid=(nt,),
        in_specs=_halo_specs(tm, wdt, 0, rev) + [_const((3, wdt)), _const((1, wdt)), _rows(tm, DFF, 0, rev)],
        out_specs=[_rows(tm, wdt, 0, rev), _const((3, wdt)), _const((1, wdt))],
        out_shape=[jax.ShapeDtypeStruct((t, wdt), BF16), jax.ShapeDtypeStruct((3, wdt), F32),
                   jax.ShapeDtypeStruct((1, wdt), F32)],
        scratch_shapes=[pltpu.VMEM((HALO, wdt), F32)],
        compiler_params=_cparams(),
    )(up, up, conv_w, bconv, da)


HEAD_TM = 256


def head_fwd_bwd(a, y1, tgt, gate_f, g2, b2, wd):
    t = a.shape[0]
    tm = HEAD_TM

    def body(a_ref, y_ref, t_ref, gf_ref, g2_ref, b2_ref, wd_ref,
             da_ref, dy_ref, dffn_ref, dgf_ref, dg2_ref, db2_ref, loss_ref):
        i = pl.program_id(0)
        zero = jnp.zeros((tm, D), F32)

        def f(a_, y_, pf, gf, g2_, b2_):
            return head_fn(a_, y_, pf, gf, g2_, b2_, t_ref[...], wd_ref[...])

        loss, vjp = jax.vjp(f, a_ref[...].astype(F32), y_ref[...], zero, gf_ref[...], g2_ref[...], b2_ref[...])
        da, dy, dffn, dgf, dg2, db2 = vjp(jnp.ones((), F32))
        da_ref[...] = da
        dy_ref[...] = dy
        dffn_ref[...] = dffn.astype(BF16)
        accs = (dgf_ref, dg2_ref, db2_ref, loss_ref)

        @pl.when(i == 0)
        def _():
            for r in accs:
                r[...] = jnp.zeros_like(r)

        dgf_ref[...] += dgf
        dg2_ref[...] += dg2
        db2_ref[...] += db2
        loss_ref[...] += loss * jnp.ones((1, 128), F32)

    return pl.pallas_call(
        body, name="head_fwd_bwd", grid=(t // tm,),
        in_specs=[_rows(tm, DFF), _rows(tm, D), _rows(tm, D), _const((1, D)), _const((1, D)), _const((1, D)),
                  _const((DFF, D))],
        out_specs=[_rows(tm, DFF), _rows(tm, D), _rows(tm, D), _const((1, D)), _const((1, D)), _const((1, D)),
                   _const((1, 128))],
        out_shape=[jax.ShapeDtypeStruct((t, DFF), F32), jax.ShapeDtypeStruct((t, D), F32),
                   jax.ShapeDtypeStruct((t, D), BF16)] + [jax.ShapeDtypeStruct((1, D), F32)] * 3
        + [jax.ShapeDtypeStruct((1, 128), F32)],
        compiler_params=_cparams(),
    )(a, y1, tgt, gate_f, g2, b2, wd)


def ada_fwd(c_all, w_sh, b_sh):
    def body(c_ref, w_ref, b_ref, o_ref):
        o_ref[...] = _mmh(_silu(c_ref[...]), w_ref[...]) + b_ref[...]

    n = w_sh.shape[1]
    return pl.pallas_call(
        body, name="ada_fwd", out_shape=jax.ShapeDtypeStruct((NDEV, n), F32),
        in_specs=[pl.BlockSpec(memory_space=pltpu.VMEM)] * 3,
        out_specs=pl.BlockSpec(memory_space=pltpu.VMEM),
        compiler_params=pltpu.CompilerParams(vmem_limit_bytes=VMEM_LIMIT),
    )(c_all, w_sh, b_sh)


def ada_wgrad(c_all_t, dmod_sh):
    def body(c_ref, d_ref, o_ref):
        o_ref[...] = _mmh(_silu(c_ref[...]), d_ref[...])

    return pl.pallas_call(
        body, name="ada_wgrad", out_shape=jax.ShapeDtypeStruct((c_all_t.shape[0], dmod_sh.shape[1]), F32),
        in_specs=[pl.BlockSpec(memory_space=pltpu.VMEM)] * 2,
        out_specs=pl.BlockSpec(memory_space=pltpu.VMEM),
        compiler_params=pltpu.CompilerParams(vmem_limit_bytes=VMEM_LIMIT),
    )(c_all_t, dmod_sh)


def adamw(gparts, w, m, v, name):
    p, r, c = gparts.shape
    tr = r if r <= 256 else _pick(r, (256, 128, 64, 32, 16, 8))
    c1 = 1.0 - B1 ** STEP
    c2 = 1.0 - B2 ** STEP

    def body(g_ref, w_ref, m_ref, v_ref, go_ref, d_ref, mo_ref, vo_ref):
        g = g_ref[0].astype(F32)
        for s in range(1, p):
            g = g + g_ref[s].astype(F32)
        mn = B1 * m_ref[0] + (1.0 - B1) * g
        vn = B2 * v_ref[0] + (1.0 - B2) * (g * g)
        go_ref[0] = g
        d_ref[0] = -LR * ((mn / c1) / (jnp.sqrt(vn / c2) + AEPS) + WD * w_ref[0])
        mo_ref[0] = mn
        vo_ref[0] = vn

    spec = pl.BlockSpec((1, tr, c), lambda i: (0, i, 0))
    return pl.pallas_call(
        body, name=name, grid=(r // tr,),
        in_specs=[pl.BlockSpec((p, tr, c), lambda i: (0, i, 0)), spec, spec, spec],
        out_specs=[spec] * 4,
        out_shape=[jax.ShapeDtypeStruct((1, r, c), F32)] * 4,
        compiler_params=_cparams(),
    )(gparts, w, m, v)


def adamw_small(gs, ws, ms, vs, loss_parts, name):
    n = len(ws)
    c1 = 1.0 - B1 ** STEP
    c2 = 1.0 - B2 ** STEP

    def slots(ref):
        acc = ref[0]
        for s in range(1, ref.shape[0]):
            acc = acc + ref[s]
        return acc

    def body(*refs):
        g_refs, w_refs, m_refs, v_refs = (refs[k * n:(k + 1) * n] for k in range(4))
        l_ref, outs = refs[4 * n], refs[4 * n + 1:]
        for i in range(n):
            g = slots(g_refs[i])
            mn = B1 * m_refs[i][...] + (1.0 - B1) * g
            vn = B2 * v_refs[i][...] + (1.0 - B2) * (g * g)
            outs[i][...] = g
            outs[n + i][...] = -LR * ((mn / c1) / (jnp.sqrt(vn / c2) + AEPS) + WD * w_refs[i][...])
            outs[2 * n + i][...] = mn
            outs[3 * n + i][...] = vn
        outs[4 * n][...] = slots(l_ref)

    vmem = pl.BlockSpec(memory_space=pltpu.VMEM)
    outs = pl.pallas_call(
        body, name=name,
        in_specs=[vmem] * (4 * n + 1), out_specs=[vmem] * (4 * n + 1),
        out_shape=[jax.ShapeDtypeStruct(w.shape, F32) for w in ws] * 4 + [jax.ShapeDtypeStruct((1, LANE), F32)],
        compiler_params=pltpu.CompilerParams(vmem_limit_bytes=VMEM_LIMIT),
    )(*gs, *ws, *ms, *vs, loss_parts)
    return outs[:n], outs[n:2 * n], outs[2 * n:3 * n], outs[3 * n:4 * n], outs[4 * n]


def _me():
    x, y, c = lax.axis_index("x"), lax.axis_index("y"), lax.axis_index("c")
    return x, y, c, 4 * x + 2 * y + c


def _peer(x, y, c, d):
    px = 1 - x if (d >> 2) & 1 else x
    py = 1 - y if (d >> 1) & 1 else y
    pc = 1 - c if d & 1 else c
    return (px, py, pc), 4 * px + 2 * py + pc


def _exchange(arrs, name, scatter):
    n = len(arrs)

    def body(*refs):
        ins, outs = refs[:n], refs[n:2 * n]
        send, recv, lsem = refs[2 * n:]
        x, y, c, me = _me()
        remote, local = [], []
        for k in range(n):
            src = ins[k].at[me] if scatter else ins[k]
            cp = pltpu.make_async_copy(src, outs[k].at[me], lsem.at[k])
            cp.start()
            local.append(cp)
            for d in range(1, NDEV):
                dev, pid = _peer(x, y, c, d)
                src = ins[k].at[pid] if scatter else ins[k]
                cp = pltpu.make_async_remote_copy(src_ref=src, dst_ref=outs[k].at[me],
                                                  send_sem=send.at[k, d - 1], recv_sem=recv.at[k, d - 1],
                                                  device_id=dev, device_id_type=pl.DeviceIdType.MESH)
                cp.start()
                remote.append(cp)
        for cp in remote:
            cp.wait()
        for cp in local:
            cp.wait()

    shapes = [a.shape if scatter else (NDEV,) + a.shape for a in arrs]
    return pl.pallas_call(
        body, name=name,
        in_specs=[pl.BlockSpec(memory_space=pl.ANY)] * n,
        out_specs=[pl.BlockSpec(memory_space=pl.ANY)] * n,
        out_shape=[jax.ShapeDtypeStruct(s, a.dtype) for s, a in zip(shapes, arrs)],
        scratch_shapes=[pltpu.SemaphoreType.DMA((n, NDEV - 1)), pltpu.SemaphoreType.DMA((n, NDEV - 1)),
                        pltpu.SemaphoreType.DMA((n,))],
        compiler_params=pltpu.CompilerParams(has_side_effects=True),
    )(*arrs)


def all_gather(arrs, name):
    return _exchange(arrs, name, False)


def all_gather_two_level(shard, name):
    def body(x_ref, out_ref, send, recv, lsem):
        x, y, c, _ = _me()
        sibling = (x, y, 1 - c)
        chips = [(1 - x, y), (x, 1 - y), (1 - x, 1 - y)]

        def slot(px, py, pc):
            return out_ref.at[4 * px + 2 * py + pc]

        def copy(k, block, to, src=None):
            return pltpu.make_async_remote_copy(
                src_ref=slot(*block) if src is None else src, dst_ref=slot(*block),
                send_sem=send.at[k], recv_sem=recv.at[k], device_id=to, device_id_type=pl.DeviceIdType.MESH)

        mine = pltpu.make_async_copy(x_ref, slot(x, y, c), lsem)
        mine.start()
        first = [copy(0, (x, y, c), sibling, src=x_ref)]
        first += [copy(1 + j, (x, y, c), (*chip, c), src=x_ref) for j, chip in enumerate(chips)]
        for cp in first:
            cp.start()
        passed = [copy(4 + j, (*chip, c), sibling) for j, chip in enumerate(chips)]
        for j, chip in enumerate(chips):
            copy(1 + j, (*chip, c), (x, y, c)).wait_recv()
            passed[j].start()
        copy(0, sibling, (x, y, c)).wait_recv()
        for j, chip in enumerate(chips):
            copy(4 + j, (*chip, 1 - c), (x, y, c)).wait_recv()
        for cp in first + passed:
            cp.wait_send()
        mine.wait()

    return pl.pallas_call(
        body, name=name,
        in_specs=[pl.BlockSpec(memory_space=pl.ANY)],
        out_specs=pl.BlockSpec(memory_space=pl.ANY),
        out_shape=jax.ShapeDtypeStruct((NDEV,) + shard.shape, shard.dtype),
        scratch_shapes=[pltpu.SemaphoreType.DMA((NPEER,)), pltpu.SemaphoreType.DMA((NPEER,)),
                        pltpu.SemaphoreType.DMA],
        compiler_params=pltpu.CompilerParams(has_side_effects=True),
    )(shard)


def all_to_all(arrs, name):
    return _exchange(arrs, name, True)


_HBM = pl.BlockSpec(memory_space=pltpu.HBM)
_SEM = pl.BlockSpec(memory_space=pltpu.SEMAPHORE)
_EFFECT = pltpu.SideEffectType.DATAFLOW_SIDE_EFFECTING
NPEER = NDEV - 1


def exchange_start(arrs, name, scatter):
    n = len(arrs)
    lands = [lax.empty(a.shape if scatter else (NDEV,) + a.shape, a.dtype) for a in arrs]

    def body(*refs):
        ins, lrefs = refs[:n], refs[n:2 * n]
        send, recv, token = refs[2 * n], refs[2 * n + 1], refs[-1]
        x, y, c, me = _me()
        for k in range(n):
            for d in range(1, NDEV):
                dev, pid = _peer(x, y, c, d)
                src = ins[k].at[pid] if scatter else ins[k]
                pltpu.make_async_remote_copy(src_ref=src, dst_ref=lrefs[k].at[me],
                                             send_sem=send.at[k * NPEER + d - 1], recv_sem=recv.at[k * NPEER + d - 1],
                                             device_id=dev, device_id_type=pl.DeviceIdType.MESH).start()
        token[...] = jnp.zeros_like(token)

    thru = [pltpu.HBM(a.shape, a.dtype) for a in list(arrs) + lands]
    outs = pl.pallas_call(
        body, name=name,
        out_shape=(pltpu.SemaphoreType.DMA((n * NPEER,)), pltpu.SemaphoreType.DMA((n * NPEER,)), *thru,
                   jax.ShapeDtypeStruct((8, 128), F32)),
        in_specs=[_HBM] * (2 * n),
        out_specs=(_SEM, _SEM, *([_HBM] * (2 * n)), pl.BlockSpec(memory_space=pltpu.VMEM)),
        input_output_aliases={i: 2 + i for i in range(2 * n)},
        compiler_params=pltpu.CompilerParams(has_side_effects=_EFFECT),
    )(*[pltpu.with_memory_space_constraint(a, pltpu.HBM) for a in list(arrs) + lands])
    handle = dict(send=outs[0], recv=outs[1], src=list(outs[2:2 + n]), land=list(outs[2 + n:2 + 2 * n]),
                  scatter=scatter)
    return handle, outs[-1][0, 0]


def exchange_wait(handle, after, name):
    n = len(handle["src"])
    scatter = handle["scatter"]

    def body(*refs):
        ins, lrefs = refs[:n], refs[n:2 * n]
        send, recv = refs[2 * n], refs[2 * n + 1]
        x, y, c, _ = _me()
        for k in range(n):
            for d in range(1, NDEV):
                dev, _ = _peer(x, y, c, d)
                src = ins[k].at[0] if scatter else ins[k]
                cp = pltpu.make_async_remote_copy(src_ref=src, dst_ref=lrefs[k].at[0],
                                                  send_sem=send.at[k * NPEER + d - 1],
                                                  recv_sem=recv.at[k * NPEER + d - 1],
                                                  device_id=dev, device_id_type=pl.DeviceIdType.MESH)
                cp.wait_send()
                cp.wait_recv()

    arrs = handle["src"] + handle["land"]
    outs = pl.pallas_call(
        body, name=name,
        out_shape=tuple(pltpu.HBM(a.shape, a.dtype) for a in arrs),
        in_specs=[_HBM] * (2 * n) + [_SEM, _SEM, pl.BlockSpec(memory_space=pl.ANY)],
        out_specs=tuple([_HBM] * (2 * n)),
        input_output_aliases={i: i for i in range(2 * n)},
        compiler_params=pltpu.CompilerParams(has_side_effects=_EFFECT),
    )(*arrs, handle["send"], handle["recv"], after)
    me = 4 * lax.axis_index("x") + 2 * lax.axis_index("y") + lax.axis_index("c")
    landed = []
    for own, land in zip(outs[:n], outs[n:]):
        mine = lax.dynamic_index_in_dim(own, me, 0, keepdims=True) if scatter else own[None]
        landed.append(lax.dynamic_update_slice_in_dim(land, mine, me, 0))
    return landed


def _cat_from_slabs(slabs):
    _, k, n = slabs.shape

    def cols(lo, hi):
        parts, c = [], lo
        while c < hi:
            j = c // n
            e = min(hi, (j + 1) * n)
            parts.append(slabs[j][:, c - j * n:e - j * n])
            c = e
        return parts

    def zeros(w):
        return [jnp.zeros((k, w), slabs.dtype)]

    return jnp.concatenate(cols(0, 4096) + cols(4112, 9232) + cols(4096, 4104) + zeros(LANE - AH)
                           + cols(4104, 4112) + zeros(NCAT - C_BA - LANE - AH), axis=1)


IN_PIECES = (("pre", C_QKVA, 3072), ("z", C_Z, 1024), ("qb", C_QKVB, 1024), ("kb", C_QKVB + 1024, 1024),
             ("vb", C_QKVB + 2048, 1024), ("ga", C_GATE, 1024), ("gb", C_GATE + 1024, 1024))
_ORIG_SEGS = ((0, 3072, "pre", 0), (3072, 4096, "z", 0), (4096, 4104, "ba", 0), (4104, 4112, "ba", LANE),
              (4112, 5136, "qb", 0), (5136, 6160, "kb", 0), (6160, 7184, "vb", 0), (7184, 8208, "ga", 0),
              (8208, 9232, "gb", 0))


def _orig_cols_from_pieces(gp, lo, hi):
    parts = []
    for a, b, name, off in _ORIG_SEGS:
        s, e = max(a, lo), min(b, hi)
        if s < e:
            parts.append(gp[name][:, off + s - a:off + e - a])
    return parts[0] if len(parts) == 1 else jnp.concatenate(parts, axis=1)


def _pad128(v):
    return jnp.pad(v, ((0, 0), (0, 128 - v.shape[1])))


def local_step(x, tgt, mod, wts, small, late_weights=None, on_grads=None):
    if on_grads is None:
        on_grads = lambda group, gd: jnp.zeros((), F32)
    t = x.shape[0]
    nc = t // CH
    shift_t, scale_t, gate_t, shift_f, scale_f, gate_f = mod
    wcat = _cat_from_slabs(wts["w_in_slabs"])
    a_log = _pad128(small["a_log"])
    dtb = _pad128(small["dt_bias"])
    vecs = dict(bga=small["b_gate"][:, :D], bgb=small["b_gate"][:, D:], gate_t=gate_t, g1=small["ln1_g"],
                b1=small["ln1_b"], scale_f=scale_f, shift_f=shift_f)

    h1 = modulate(x, scale_t, shift_t, "modulate_t")
    proj = matmul(h1, wcat, F32, "in_proj")
    q, k, v, gcs, beta = prep_fwd(proj, small["conv_a"], a_log, dtb)

    u, w, qg, kd, qk, eg, tinv = c1_fwd(q, k, v, gcs, beta)
    oa, sall = c2_fwd(u, w, qg, kd, qk, eg, proj, small["norm_a"])
    bias = bias_table(small["rel_bias"])
    ob, probs = attn_fwd(proj, bias)
    if late_weights is not None:
        wts = {**wts, **late_weights(ob)}
    y1, h2 = merge_fwd(x, oa, ob, proj, vecs, wts["w_a"], wts["w_b"], wts["w_o"])
    up = matmul(h2, wts["w_up"], F32, "up_proj")
    a = ffn_act_fwd(up, small["conv_ffn"], small["b_conv_ffn"])

    da, dy1_res, dffn, dgate_f, dg2, db2, loss = head_fwd_bwd(a, y1, tgt, gate_f, small["ln2_g"], small["ln2_b"],
                                                            wts["w_down"])
    g_w_down = matmul(a, dffn, BF16, "wgrad_down", ta=True)
    dup, g_conv_ffn, g_bconv = ffn_act_bwd(up, small["conv_ffn"], small["b_conv_ffn"], da)
    dh2 = matmul(dup, wts["w_up"], F32, "dgrad_up", tb=True)
    g_w_up = matmul(h2, dup, BF16, "wgrad_up", ta=True)
    tok = on_grads("ffn", dict(w_up=g_w_up, w_down=g_w_down))
    dy1, dscale_f, dshift_f = modulate_bwd(dh2, y1, dy1_res, scale_f + tok, "modulate_f_bwd")
    (dx_res, doa, dob, dga, dgb, merged, dmix, dpa, dpb,
     dbga, dbgb, dgate_t, dg1, db1) = merge_bwd(x, oa, ob, proj, vecs, wts["w_a"], wts["w_b"], wts["w_o"], dy1)
    g_w_o = matmul(merged, dmix, BF16, "wgrad_o", ta=True)
    g_w_a = matmul(oa, dpa, BF16, "wgrad_a", ta=True)
    g_w_b = matmul(ob, dpb, BF16, "wgrad_b", ta=True)
    tok = on_grads("mix", dict(w_o=g_w_o, w_a=g_w_a, w_b=g_w_b))
    dqb, dkb, dvb, dbias = attn_bwd(proj, ob, probs, dob)
    g_rel = relbias_reduce(bias_table_bwd_layout(dbias))
    du, dw, dqg, dkd, dqk, deg, dz, g_norm = c2_bwd(u, w, qg, kd, qk, eg, proj, small["norm_a"] + tok, sall, doa)
    dq, dk, dv, dgcs, dbeta = c1_bwd(q, k, v, gcs, beta, tinv, du, dw, dqg, dkd, dqk, deg)
    dpre, dbb, daa, g_conv_a, g_alog, g_dtb = prep_bwd(proj, small["conv_a"], a_log, dtb, dq, dk, dv, dgcs, dbeta)
    tok = on_grads("small", dict(conv_a=g_conv_a, rel_bias=g_rel, conv_ffn=g_conv_ffn))
    dba = jnp.concatenate([dbb, daa, jnp.zeros((t, NCAT - C_BA - 2 * LANE), BF16)], axis=1) + tok.astype(BF16)
    dpieces = dict(pre=dpre, z=dz, qb=dqb, kb=dkb, vb=dvb, ga=dga, gb=dgb)
    g_in = {n: matmul(h1, dpieces[n], BF16, "wgrad_in_" + n, ta=True) for n, _, _ in IN_PIECES}
    g_in["ba"] = matmul(h1, dba, BF16, "wgrad_in_ba", ta=True)
    tok = on_grads("in", g_in)
    dh1 = dgrad_pieces([(dpieces[n], off) for n, off, _ in IN_PIECES], dba + tok.astype(BF16), wcat,
                       "dgrad_in")
    grad_x, dscale_t, dshift_t = modulate_bwd(dh1, x, dx_res, scale_t + tok, "modulate_t_bwd")

    dmod = (dshift_t, dscale_t, dgate_t, dshift_f, dscale_f, dgate_f)
    grads = dict(w_in=_orig_cols_from_pieces(g_in, 0, 9232), w_up=g_w_up, w_down=g_w_down, w_a=g_w_a, w_b=g_w_b, w_o=g_w_o,
                 conv_a=g_conv_a, rel_bias=g_rel, conv_ffn=g_conv_ffn,
                 b_gate=jnp.concatenate([dbga, dbgb], axis=1), a_log=g_alog[:, :AH], dt_bias=g_dtb[:, :AH],
                 norm_a=g_norm, ln1_g=dg1, ln1_b=db1, b_conv_ffn=g_bconv, ln2_g=dg2, ln2_b=db2)
    return loss[0, 0], grad_x, dmod, grads


REP_NAMES = ["b_ada", "b_gate", "a_log", "dt_bias", "norm_a", "ln1_g", "ln1_b", "b_conv_ffn", "ln2_g", "ln2_b"]
SH_NAMES = ["conv_a", "rel_bias", "conv_ffn"]


def _col_shards(a, n):
    return a.reshape(a.shape[0], NDEV, n).transpose(1, 0, 2)


def kernel(x, c, w_ada, b_ada, w_in, b_gate, conv_a, a_log, dt_bias, norm_a, rel_bias, w_branch_a, w_branch_b, w_o, ln1_g, ln1_b, w_up, conv_ffn, b_conv_ffn, w_down, ln2_g, ln2_b, loss_target, m_w_ada, m_b_ada, m_w_in, m_b_gate, m_conv_a, m_a_log, m_dt_bias, m_norm_a, m_rel_bias, m_w_branch_a, m_w_branch_b, m_w_o, m_ln1_g, m_ln1_b, m_w_up, m_conv_ffn, m_b_conv_ffn, m_w_down, m_ln2_g, m_ln2_b, v_w_ada, v_b_ada, v_w_in, v_b_gate, v_conv_a, v_a_log, v_dt_bias, v_norm_a, v_rel_bias, v_w_branch_a, v_w_branch_b, v_w_o, v_ln1_g, v_ln1_b, v_w_up, v_conv_ffn, v_b_conv_ffn, v_w_down, v_ln2_g, v_ln2_b):
    W = dict(w_ada=w_ada, b_ada=b_ada, w_in=w_in, b_gate=b_gate, conv_a=conv_a, a_log=a_log, dt_bias=dt_bias,
             norm_a=norm_a, rel_bias=rel_bias, w_branch_a=w_branch_a, w_branch_b=w_branch_b, w_o=w_o, ln1_g=ln1_g,
             ln1_b=ln1_b, w_up=w_up, conv_ffn=conv_ffn, b_conv_ffn=b_conv_ffn, w_down=w_down, ln2_g=ln2_g,
             ln2_b=ln2_b)
    M = dict(w_ada=m_w_ada, b_ada=m_b_ada, w_in=m_w_in, b_gate=m_b_gate, conv_a=m_conv_a, a_log=m_a_log,
             dt_bias=m_dt_bias, norm_a=m_norm_a, rel_bias=m_rel_bias, w_branch_a=m_w_branch_a,
             w_branch_b=m_w_branch_b, w_o=m_w_o, ln1_g=m_ln1_g, ln1_b=m_ln1_b, w_up=m_w_up, conv_ffn=m_conv_ffn,
             b_conv_ffn=m_b_conv_ffn, w_down=m_w_down, ln2_g=m_ln2_g, ln2_b=m_ln2_b)
    V = dict(w_ada=v_w_ada, b_ada=v_b_ada, w_in=v_w_in, b_gate=v_b_gate, conv_a=v_conv_a, a_log=v_a_log,
             dt_bias=v_dt_bias, norm_a=v_norm_a, rel_bias=v_rel_bias, w_branch_a=v_w_branch_a,
             w_branch_b=v_w_branch_b, w_o=v_w_o, ln1_g=v_ln1_g, ln1_b=v_ln1_b, w_up=v_w_up, conv_ffn=v_conv_ffn,
             b_conv_ffn=v_b_conv_ffn, w_down=v_w_down, ln2_g=v_ln2_g, ln2_b=v_ln2_b)
    W3, M3, V3 = W, M, V
    W, M, V = ({n: a[0] for n, a in dct.items()} for dct in (W, M, V))
    me = 4 * lax.axis_index("x") + 2 * lax.axis_index("y") + lax.axis_index("c")
    big = ("w_in", "w_up", "w_down", "w_branch_a", "w_branch_b", "w_o")

    g_in = all_gather_two_level(W["w_in"].astype(BF16), "gather_w_in")
    wts = dict(w_in_slabs=g_in)
    c_all, *sh_all = all_gather([c] + [W[n] for n in SH_NAMES], "gather_small")
    c_all = c_all.reshape(NDEV, D)

    def full_small(g8):
        return g8.transpose(1, 0, 2).reshape(g8.shape[1], -1)

    small = dict(conv_a=full_small(sh_all[0]), rel_bias=full_small(sh_all[1]), conv_ffn=full_small(sh_all[2]),
                 b_gate=W["b_gate"][None], a_log=W["a_log"][None], dt_bias=W["dt_bias"][None],
                 norm_a=W["norm_a"][None], ln1_g=W["ln1_g"][None], ln1_b=W["ln1_b"][None],
                 b_conv_ffn=W["b_conv_ffn"][None], ln2_g=W["ln2_g"][None], ln2_b=W["ln2_b"][None])

    nsh = w_ada.shape[2]
    b_sh = lax.dynamic_slice(W["b_ada"][None], (0, me * nsh), (1, nsh))
    mod_sh = ada_fwd(c_all, W["w_ada"], b_sh)
    (mod_rows,) = all_to_all([mod_sh[:, None, :]], "scatter_mod")
    mod6 = mod_rows.reshape(6, D)

    after_small = (g_in[0, 0, 0].astype(F32) * 0.0 + mod6[0, 0] * 0.0).astype(BF16)
    late, late_tok = exchange_start([W[n].astype(BF16) + after_small for n in big[1:]], "gather_late_start", False)

    def late_weights(after):
        g_up, g_down, g_a, g_b, g_o = exchange_wait(late, after, "gather_late_wait")
        return dict(w_up=g_up.transpose(1, 0, 2).reshape(D, -1), w_down=g_down.reshape(DFF, D),
                    w_a=g_a.reshape(D, D), w_b=g_b.reshape(D, D), w_o=g_o.reshape(D, D))

    mod6 = mod6 + late_tok
    mod = tuple(mod6[i:i + 1] for i in range(6))

    pending = {}

    def on_grads(group, gd):
        if group == "small":
            pending["small"] = all_to_all([_col_shards(gd[n], W[n].shape[1]) for n in SH_NAMES],
                                          "scatter_small_grads")
            return pending["small"][0][0, 0, 0] * 0.0
        if group == "ffn":
            slabs = [_col_shards(gd["w_up"], w_up.shape[2]), gd["w_down"].reshape(NDEV, -1, D)]
        elif group == "mix":
            slabs = [gd[n].reshape(NDEV, -1, D) for n in ("w_a", "w_b", "w_o")]
        else:
            nin = w_in.shape[2]
            slabs = [jnp.stack([_orig_cols_from_pieces(gd, j * nin, (j + 1) * nin) for j in range(NDEV)], axis=0)]
        pending[group], tok = exchange_start([s.astype(BF16) for s in slabs], "scatter_" + group + "_start", True)
        return tok

    loss, grad_x, dmod, g = local_step(x[0], loss_target[0], mod, wts, small, late_weights, on_grads)

    rep_grads = {n: g[n] for n in REP_NAMES if n != "b_ada"}
    rep_grads["b_ada"] = jnp.concatenate(dmod, axis=1)
    gathered = all_gather([rep_grads[n] for n in REP_NAMES] + [jnp.broadcast_to(loss, (1, LANE))],
                          "gather_small_grads")
    rep_all = dict(zip(REP_NAMES, gathered))
    sh_recv = [p[:, None] for p in pending["small"]]
    small_names = REP_NAMES + SH_NAMES
    sg, sd, sm, sv, loss_row = adamw_small([rep_all[n] for n in REP_NAMES] + sh_recv,
                                           [W3[n] for n in small_names], [M3[n] for n in small_names],
                                           [V3[n] for n in small_names], gathered[-1], "adamw_small")
    loss_total = loss_row[0, 0]

    dmod_all = rep_all["b_ada"][:, 0]
    dmod_sh = lax.dynamic_slice(dmod_all, (0, me * nsh), (NDEV, nsh))
    g_w_ada = ada_wgrad(c_all.T, dmod_sh)

    p_up, p_down = exchange_wait(pending["ffn"], grad_x, "scatter_ffn_wait")
    p_a, p_b, p_o = exchange_wait(pending["mix"], grad_x, "scatter_mix_wait")
    (p_in,) = exchange_wait(pending["in"], grad_x, "scatter_in_wait")
    parts = [p_in, p_up, p_down, p_a, p_b, p_o]

    res = {}
    for n, p in zip(big, parts):
        res[n] = adamw(p, W3[n], M3[n], V3[n], "adamw_" + n)
    res["w_ada"] = adamw(g_w_ada[None], W3["w_ada"], M3["w_ada"], V3["w_ada"], "adamw_w_ada")
    for i, n in enumerate(small_names):
        res[n] = (sg[i], sd[i], sm[i], sv[i])

    order = ("w_ada", "b_ada", "w_in", "b_gate", "conv_a", "a_log", "dt_bias", "norm_a", "rel_bias", "w_branch_a",
             "w_branch_b", "w_o", "ln1_g", "ln1_b", "w_up", "conv_ffn", "b_conv_ffn", "w_down", "ln2_g", "ln2_b")
    outs = [loss_total, grad_x[None]]
    for kind in range(4):
        outs += [res[n][kind] for n in order]
    return tuple(outs)
```

```python
import functools
import math

import numpy as np
import jax
import jax.numpy as jnp
from jax import lax
from jax.experimental import pallas as pl
from jax.experimental.pallas import tpu as pltpu

F32 = jnp.float32
BF16 = jnp.bfloat16
HI = lax.Precision.HIGHEST

D = 1024
CH = 64
AH, ADK = 8, 128
BH, BDH = 16, 64
BPREV = 8
BMAXREL = 256
RELSZ = CH + BMAXREL
DFF = 2816
ALPHA = 2.0 ** 0.25
LN_EPS, RMS_EPS, L2_EPS = 1e-5, 1e-6, 1e-6
NEG = -1e30
LR, B1, B2, AEPS, WD, STEP = 1e-3, 0.9, 0.999, 1e-8, 0.01, 10
NDEV = 8
HALO = 8
LANE = 128
TQ = 512
VMEM_LIMIT = 56 * 1024 * 1024

C_QKVA, C_Z, C_QKVB, C_GATE, C_BA, NCAT = 0, 3072, 4096, 7168, 9216, 9728


def _cparams(n_axes=1, vmem=VMEM_LIMIT):
    return pltpu.CompilerParams(dimension_semantics=("arbitrary",) * n_axes, vmem_limit_bytes=vmem)


def _dg(a, b, ca, cb):
    return lax.dot_general(a.astype(BF16), b.astype(BF16), (((ca,), (cb,)), ((), ())),
                           preferred_element_type=F32)


@jax.custom_vjp
def mm_nn(a, b):
    return _dg(a, b, 1, 0)


@jax.custom_vjp
def mm_nt(a, b):
    return _dg(a, b, 1, 1)


@jax.custom_vjp
def mm_tn(a, b):
    return _dg(a, b, 0, 0)


mm_nn.defvjp(lambda a, b: (mm_nn(a, b), (a, b)),
             lambda r, g: (mm_nt(g, r[1]).astype(r[0].dtype), mm_tn(r[0], g).astype(r[1].dtype)))
mm_nt.defvjp(lambda a, b: (mm_nt(a, b), (a, b)),
             lambda r, g: (mm_nn(g, r[1]).astype(r[0].dtype), mm_tn(g, r[0]).astype(r[1].dtype)))
mm_tn.defvjp(lambda a, b: (mm_tn(a, b), (a, b)),
             lambda r, g: (mm_nt(r[1], g).astype(r[0].dtype), mm_nn(r[0], g).astype(r[1].dtype)))


@jax.custom_vjp
def mm_w(a, w):
    return _dg(a, w, 1, 0)


mm_w.defvjp(lambda a, w: (mm_w(a, w), (a, w)),
            lambda r, g: (mm_nt(g, r[1]).astype(r[0].dtype), jnp.zeros_like(r[1])))


def _mmh(a, b):
    return lax.dot_general(a, b, (((1,), (0,)), ((), ())), precision=HI, preferred_element_type=F32)


def _bdg(a, b, ca, cb):
    return lax.dot_general(a.astype(BF16), b.astype(BF16), (((ca,), (cb,)), ((0,), (0,))),
                           preferred_element_type=F32)


@jax.custom_vjp
def bmm_nn(a, b):
    return _bdg(a, b, 2, 1)


@jax.custom_vjp
def bmm_nt(a, b):
    return _bdg(a, b, 2, 2)


@jax.custom_vjp
def bmm_tn(a, b):
    return _bdg(a, b, 1, 1)


bmm_nn.defvjp(lambda a, b: (bmm_nn(a, b), (a, b)), lambda r, g: (bmm_nt(g, r[1]), bmm_tn(r[0], g)))
bmm_nt.defvjp(lambda a, b: (bmm_nt(a, b), (a, b)), lambda r, g: (bmm_nn(g, r[1]), bmm_tn(g, r[0])))
bmm_tn.defvjp(lambda a, b: (bmm_tn(a, b), (a, b)), lambda r, g: (bmm_nt(r[1], g), bmm_nn(r[0], g)))


def _bdg3(a, b, ca, cb):
    return lax.dot_general(a, b, (((ca,), (cb,)), ((0,), (0,))), precision=lax.Precision.HIGH,
                           preferred_element_type=F32)


NEWTON_STEPS = 2


def _bdgp(a, b, ca, cb):
    return _bdg(a, b, ca, cb)


@jax.custom_vjp
def bmm3_nn(a, b):
    return _bdgp(a, b, 2, 1)


bmm3_nn.defvjp(lambda a, b: (bmm3_nn(a, b), (a, b)),
               lambda r, g: (_bdgp(g, r[1], 2, 2), _bdgp(r[0], g, 1, 1)))


def _sigmoid(x):
    return 0.5 * jnp.tanh(0.5 * x) + 0.5


def _silu(x):
    return x * _sigmoid(x)


def _softplus(x):
    return jnp.maximum(x, 0.0) + jnp.log(1.0 + jnp.exp(-jnp.abs(x)))


def _layernorm(r, g, b):
    mu = jnp.mean(r, axis=-1, keepdims=True)
    xc = r - mu
    var = jnp.mean(xc * xc, axis=-1, keepdims=True)
    return xc * lax.rsqrt(var + LN_EPS) * g + b


def _iota2(shape, dim):
    return lax.broadcasted_iota(jnp.int32, shape, dim)


@jax.custom_vjp
def causal_conv(ext, rows):
    k = len(rows)
    y = None
    for j in range(k):
        s = k - 1 - j
        r = pltpu.roll(ext, s, 0) if s else ext
        t = r[HALO:] * rows[j]
        y = t if y is None else y + t
    return y


def _causal_conv_fwd(ext, rows):
    return causal_conv(ext, rows), (ext, rows)


def _causal_conv_bwd(res, g):
    ext, rows = res
    n = ext.shape[0]
    k = len(rows)
    gext = jnp.concatenate([jnp.zeros((HALO, g.shape[1]), g.dtype), g], axis=0)
    dext = None
    drows = []
    for j in range(k):
        s = k - 1 - j
        up = pltpu.roll(gext, n - s, 0) if s else gext
        t = up * rows[j]
        dext = t if dext is None else dext + t
        r = pltpu.roll(ext, s, 0) if s else ext
        drows.append(jnp.sum(g * r[HALO:], axis=0, keepdims=True))
    return dext, tuple(drows)


causal_conv.defvjp(_causal_conv_fwd, _causal_conv_bwd)


def _chunk_masks(tm):
    i = _iota2((tm, tm), 0)
    j = _iota2((tm, tm), 1)
    same = (i ^ j) < CH
    lower = jnp.where(same & (j <= i), 1.0, 0.0).astype(F32)
    upper = jnp.where(same & (i <= j), 1.0, 0.0).astype(F32)
    return lower, upper


@jax.custom_vjp
def chunk_cumsum(g):
    lower, _ = _chunk_masks(g.shape[0])
    return _mmh(lower, g)


def _chunk_cumsum_bwd(_, ct):
    _, upper = _chunk_masks(ct.shape[0])
    return (_mmh(upper, ct),)


chunk_cumsum.defvjp(lambda g: (chunk_cumsum(g), None), _chunk_cumsum_bwd)


@jax.custom_vjp
def inv_unit_lower(a):
    n = a.shape[-1]
    eye = jnp.where(_iota2((1, n, n), 1) == _iota2((1, n, n), 2), 1.0, 0.0).astype(F32)
    x = eye - a
    p = _bdg3(a, a, 2, 1)
    steps = int(math.log2(n)) - 1
    for s in range(steps):
        x = x + _bdg3(x, p, 2, 1)
        if s + 1 < steps:
            p = _bdg3(p, p, 2, 1)
    for _ in range(NEWTON_STEPS):
        r = (eye - x) - _bdg3(a, x, 2, 1)
        x = x + _bdg3(x, r, 2, 1)
    return x


def _inv_fwd(a):
    t = inv_unit_lower(a)
    return t, t


def _inv_bwd(t, g):
    return (-_bdgp(_bdgp(t, g, 1, 1), t, 2, 2),)


inv_unit_lower.defvjp(_inv_fwd, _inv_bwd)


@jax.custom_vjp
def inv_known(a, t):
    return t


inv_known.defvjp(lambda a, t: (t, t), lambda t, g: (_inv_bwd(t, g)[0], jnp.zeros_like(t)))


def prep_head_fn(ext, rows, scale):
    s = _silu(causal_conv(ext, rows))
    if scale is None:
        return s
    return s * (lax.rsqrt(jnp.sum(s * s, axis=-1, keepdims=True) + L2_EPS) * scale)


def prep_gate_fn(bb, aa, a_log, dtb):
    g = -jnp.exp(a_log) * _softplus(aa + dtb)
    return chunk_cumsum(g), _sigmoid(bb)


PREP_SCALES = (ADK ** -0.5, 1.0, None)


def _head_cols(a):
    lane = _iota2((1, LANE), 1)
    return jnp.concatenate([jnp.sum(jnp.where(lane == h, a, 0.0), axis=1, keepdims=True)[None]
                            for h in range(AH)], axis=0)


def _head_rows(a):
    at = a.T[:AH]
    sub = _iota2((AH, 1), 0)
    return jnp.concatenate([jnp.sum(jnp.where(sub == h, at, 0.0), axis=0, keepdims=True)[None]
                            for h in range(AH)], axis=0)


def c1_heads(q, k, v, gcs, beta, tinv_saved=None):
    gcol = _head_cols(gcs)
    grow = _head_rows(gcs)
    bcol = _head_cols(beta)
    i = _iota2((1, CH, CH), 1)
    j = _iota2((1, CH, CH), 2)
    causal = j <= i
    strict = j < i
    diff = gcol - grow
    decay = jnp.where(causal, jnp.exp(jnp.where(causal, diff, 0.0)), 0.0)
    kb = k * bcol
    vb = v * bcol
    a_low = jnp.where(strict, bmm_nt(kb, k) * decay, 0.0)
    tinv = inv_unit_lower(a_low) if tinv_saved is None else inv_known(a_low, tinv_saved)
    egc = jnp.exp(gcol)
    u = bmm3_nn(tinv, vb)
    w = bmm3_nn(tinv, kb * egc)
    qk = jnp.where(causal, bmm_nt(q, k) * decay, 0.0)
    glast = jnp.sum(jnp.where(_iota2((1, CH, 1), 1) == CH - 1, gcol, 0.0), axis=1, keepdims=True)
    qg = q * egc
    kd = k * jnp.exp(glast - gcol)
    eg = jnp.exp(glast) * jnp.ones((1, 1, ADK), F32)
    return u, w, qk, qg, kd, eg, tinv


def c2_heads(s, u, w, qk, qg, kd, eg, z, nw):
    vn = u - bmm_nn(w, s)
    o = bmm_nn(qg, s) + bmm_nn(qk, vn)
    s2 = s * eg + bmm_tn(kd, vn)
    ms = jnp.mean(o * o, axis=-1, keepdims=True)
    og = o * lax.rsqrt(ms + RMS_EPS) * nw * _silu(z)
    return og, s2


ATT_SCALE = BDH ** -0.5


def _head_mask(hh):
    lane = _iota2((1, 2 * BDH), 1)
    return jnp.where((lane >= hh * BDH) & (lane < (hh + 1) * BDH), 1.0, 0.0).astype(F32)


def attn_sub_fwd(q, k, v, bias2, r, firstf):
    col = _iota2((1, KWIN), 1) + r * SUBQ
    nokey = jnp.where(col < TQ, firstf, 0.0) * NEG
    out, probs = None, []
    for hh in range(2):
        hm = _head_mask(hh)
        s = mm_nt(q * (hm * ATT_SCALE), k) + (assemble_bias(bias2[hh], r) + nokey)
        p = jnp.exp(s - jnp.max(s, axis=-1, keepdims=True))
        inv = 1.0 / jnp.sum(p, axis=-1, keepdims=True)
        o = mm_nn(p, v) * (inv * hm)
        out = o if out is None else out + o
        probs.append(p * inv)
    return out, probs


def attn_sub_bwd(q, k, v, o, do, probs, r):
    dq, dk, dv, dss = None, None, None, []
    for hh in range(2):
        hm = _head_mask(hh)
        p = probs[hh]
        doh = do * hm
        ds = p * (mm_nt(doh, v) - jnp.sum(doh * o, axis=-1, keepdims=True))
        dqh = mm_nn(ds, k) * (hm * ATT_SCALE)
        dkh = mm_tn(ds, q * (hm * ATT_SCALE))
        dvh = mm_tn(p, doh)
        dq = dqh if dq is None else dq + dqh
        dk = dkh if dk is None else dk + dkh
        dv = dvh if dv is None else dv + dvh
        dss.append(ds)
    return dq, dk, dv, dss


def merge_fn(x, oa, ob, gra, grb, p_pa, p_pb, p_mix, bga, bgb, gate_t, g1, b1, scale_f, shift_f,
             wa, wb, wo):
    ga = _sigmoid(gra + bga)
    gb = _sigmoid(grb + bgb)
    pa = mm_w(oa, wa) + p_pa
    pb = mm_w(ob, wb) + p_pb
    merged = ga * pa + gb * pb
    mix = mm_w(merged, wo) + p_mix
    y1 = _layernorm(ALPHA * x + gate_t * mix, g1, b1)
    return y1, merged


def ffn_act_fn(extg, extv, rows_g, rows_v, bg, bv):
    return _silu(causal_conv(extg, rows_g) + bg) * (causal_conv(extv, rows_v) + bv)


def head_fn(a, y1, p_ffn, gate_f, g2, b2, tgt, wd):
    ffn = mm_w(a, wd) + p_ffn
    y2 = _layernorm(ALPHA * y1 + gate_f * ffn, g2, b2)
    err = y2 - tgt
    return 0.5 * jnp.sum(jnp.mean(err * err, axis=-1, keepdims=True))


def _rows(tm, width, colblk=0, order=None):
    if order is None:
        return pl.BlockSpec((tm, width), lambda i: (i, colblk))
    return pl.BlockSpec((tm, width), lambda i: (order(i), colblk))


def _const(shape):
    nd = len(shape)
    return pl.BlockSpec(shape, lambda *_: (0,) * nd)


def _pick(n, cands):
    for c in cands:
        if n % c == 0:
            return c
    raise ValueError(f"no tile for {n}")


def _tile(n, cap):
    best = None
    for c in range(LANE, min(n, cap) + 1, LANE):
        if n % c == 0:
            best = c
    if best is None:
        raise ValueError(f"no tile for {n}")
    return best


def _onehot_rows(k, j):
    return jnp.where(_iota2((k, 1), 0) == j, 1.0, 0.0).astype(F32)


def _stack_rows(drows):
    k = len(drows)
    out = None
    for j in range(k):
        tj = _onehot_rows(k, j) * drows[j]
        out = tj if out is None else out + tj
    return out


def matmul(a, w, out_dtype, name, ta=False, tb=False):
    kdim, m = a.shape if ta else a.shape[::-1]
    n = w.shape[0] if tb else w.shape[1]
    tm = _tile(m, 2048 if kdim <= 1024 else 1024)
    tn = _tile(n, 1024)
    tk = _tile(kdim, 2560)
    nk = kdim // tk
    a_spec = (pl.BlockSpec((tk, tm), lambda i, j, k: (k, i)) if ta
              else pl.BlockSpec((tm, tk), lambda i, j, k: (i, k)))
    w_spec = (pl.BlockSpec((tn, tk), lambda i, j, k: (j, k)) if tb
              else pl.BlockSpec((tk, tn), lambda i, j, k: (k, j)))

    def body(a_ref, w_ref, o_ref, *scratch):
        p = _dg(a_ref[...], w_ref[...], 0 if ta else 1, 1 if tb else 0)
        if nk == 1:
            o_ref[...] = p.astype(out_dtype)
            return
        acc = scratch[0]
        k = pl.program_id(2)

        @pl.when(k == 0)
        def _():
            acc[...] = p

        @pl.when(k > 0)
        def _():
            acc[...] += p

        @pl.when(k == nk - 1)
        def _():
            o_ref[...] = acc[...].astype(out_dtype)

    return pl.pallas_call(
        body, name=name,
        grid=(m // tm, n // tn, nk),
        in_specs=[a_spec, w_spec],
        out_specs=pl.BlockSpec((tm, tn), lambda i, j, k: (i, j)),
        out_shape=jax.ShapeDtypeStruct((m, n), out_dtype),
        scratch_shapes=[] if nk == 1 else [pltpu.VMEM((tm, tn), F32)],
        compiler_params=_cparams(3),
    )(a, w)


def dgrad_modulated(a, w, xin, dres, scale, name):
    m, kdim = a.shape
    n = w.shape[0]
    tm = _tile(m, 1024)
    tk = _tile(kdim, 2560)
    nk = kdim // tk
    assert nk > 1

    def body(a_ref, w_ref, x_ref, r_ref, sc_ref, o_ref, dsc_ref, dsh_ref, acc):
        i = pl.program_id(0)
        k = pl.program_id(1)
        p = _dg(a_ref[...], w_ref[...], 1, 1)

        @pl.when(k == 0)
        def _():
            acc[...] = p

        @pl.when(k > 0)
        def _():
            acc[...] += p

        @pl.when((i == 0) & (k == 0))
        def _():
            dsc_ref[...] = jnp.zeros_like(dsc_ref)
            dsh_ref[...] = jnp.zeros_like(dsh_ref)

        @pl.when(k == nk - 1)
        def _():
            dh = acc[...]
            o_ref[...] = r_ref[...] + dh * (1.0 + sc_ref[...])
            dsc_ref[...] += jnp.sum(dh * x_ref[...], axis=0, keepdims=True)
            dsh_ref[...] += jnp.sum(dh, axis=0, keepdims=True)

    row = pl.BlockSpec((tm, n), lambda i, k: (i, 0))
    vec = pl.BlockSpec((1, n), lambda i, k: (0, 0))
    return pl.pallas_call(
        body, name=name, grid=(m // tm, nk),
        in_specs=[pl.BlockSpec((tm, tk), lambda i, k: (i, k)), pl.BlockSpec((n, tk), lambda i, k: (0, k)),
                  row, row, vec],
        out_specs=[row, vec, vec],
        out_shape=[jax.ShapeDtypeStruct((m, n), F32), jax.ShapeDtypeStruct((1, n), F32),
                   jax.ShapeDtypeStruct((1, n), F32)],
        scratch_shapes=[pltpu.VMEM((tm, n), F32)],
        compiler_params=_cparams(2),
    )(a, w, xin, dres, scale)


def dgrad_pieces(pieces, tail, w, name):
    m = pieces[0][0].shape[0]
    n, ktot = w.shape
    tk = 1024
    tm = _tile(m, 1024)
    wt = tail.shape[1]
    ranges, k0 = [], 0
    for arr, off in pieces:
        assert off == k0 * tk and arr.shape[1] % tk == 0
        ranges.append((k0, k0 + arr.shape[1] // tk))
        k0 = ranges[-1][1]
    nk = k0
    npc = len(pieces)

    def body(*refs):
        a_refs, t_ref, w_ref, wt_ref, o_ref, acc = refs[:npc], refs[npc], refs[npc + 1], refs[npc + 2], refs[npc + 3], refs[npc + 4]
        k = pl.program_id(1)

        @pl.when(k == 0)
        def _():
            acc[...] = _dg(t_ref[...], wt_ref[...], 1, 1)

        for a_ref, (lo, hi) in zip(a_refs, ranges):
            @pl.when((k >= lo) & (k < hi))
            def _(a_ref=a_ref):
                acc[...] += _dg(a_ref[...], w_ref[...], 1, 1)

        @pl.when(k == nk - 1)
        def _():
            o_ref[...] = acc[...]

    def piece_spec(lo, hi):
        return pl.BlockSpec((tm, tk), lambda i, k: (i, jnp.clip(k - lo, 0, hi - lo - 1)))

    return pl.pallas_call(
        body, name=name, grid=(m // tm, nk),
        in_specs=[piece_spec(lo, hi) for lo, hi in ranges] + [
            pl.BlockSpec((tm, wt), lambda i, k: (i, 0)),
            pl.BlockSpec((n, tk), lambda i, k: (0, k)),
            pl.BlockSpec((n, wt), lambda i, k: (0, (ktot - wt) // wt))],
        out_specs=pl.BlockSpec((tm, n), lambda i, k: (i, 0)),
        out_shape=jax.ShapeDtypeStruct((m, n), F32),
        scratch_shapes=[pltpu.VMEM((tm, n), F32)],
        compiler_params=_cparams(2),
    )(*[a for a, _ in pieces], tail, w, w)


def modulate(x, scale, shift, name):
    t, d = x.shape
    tm = _pick(t, (512, 256, 128))

    def body(x_ref, sc_ref, sh_ref, o_ref):
        o_ref[...] = (x_ref[...] * (1.0 + sc_ref[...]) + sh_ref[...]).astype(BF16)

    return pl.pallas_call(
        body, name=name, grid=(t // tm,),
        in_specs=[_rows(tm, d), _const((1, d)), _const((1, d))],
        out_specs=_rows(tm, d),
        out_shape=jax.ShapeDtypeStruct((t, d), BF16),
        compiler_params=_cparams(),
    )(x, scale, shift)


def modulate_bwd(dh, xin, dres, scale, name):
    t, d = dh.shape
    tm = _pick(t, (512, 256, 128))

    def body(dh_ref, x_ref, dr_ref, sc_ref, o_ref, dsc_ref, dsh_ref):
        i = pl.program_id(0)
        dh_v = dh_ref[...]
        o_ref[...] = dr_ref[...] + dh_v * (1.0 + sc_ref[...])

        @pl.when(i == 0)
        def _():
            dsc_ref[...] = jnp.zeros_like(dsc_ref)
            dsh_ref[...] = jnp.zeros_like(dsh_ref)

        dsc_ref[...] += jnp.sum(dh_v * x_ref[...], axis=0, keepdims=True)
        dsh_ref[...] += jnp.sum(dh_v, axis=0, keepdims=True)

    return pl.pallas_call(
        body, name=name, grid=(t // tm,),
        in_specs=[_rows(tm, d), _rows(tm, d), _rows(tm, d), _const((1, d))],
        out_specs=[_rows(tm, d), _const((1, d)), _const((1, d))],
        out_shape=[jax.ShapeDtypeStruct((t, d), F32), jax.ShapeDtypeStruct((1, d), F32),
                   jax.ShapeDtypeStruct((1, d), F32)],
        compiler_params=_cparams(),
    )(dh, xin, dres, scale)


PREP_TM = 128


def _halo_specs(tm, width, colblk, order):
    per = tm // HALO
    return [pl.BlockSpec((HALO, width), lambda i: (jnp.maximum(order(i) * per - 1, 0), colblk)),
            pl.BlockSpec((tm, width), lambda i: (order(i), colblk))]


def prep_fwd(proj, conv_a, a_log, dtb):
    t = proj.shape[0]
    tm = PREP_TM
    nt = t // tm
    wq = 3 * D

    def body(prev_ref, cur_ref, bb_ref, aa_ref, cw_ref, al_ref, dt_ref, q_ref, k_ref, v_ref, g_ref, b_ref):
        i = pl.program_id(0)
        flag = jnp.where(i > 0, 1.0, 0.0)
        for part, o_ref in enumerate((q_ref, k_ref, v_ref)):
            for h in range(AH):
                sl = slice(part * D + h * ADK, part * D + (h + 1) * ADK)
                ext = jnp.concatenate([prev_ref[:, sl] * flag, cur_ref[:, sl]], axis=0)
                rows = tuple(cw_ref[j:j + 1, sl] for j in range(4))
                o_ref[h] = prep_head_fn(ext, rows, PREP_SCALES[part])
        gcs, beta = prep_gate_fn(bb_ref[...], aa_ref[...], al_ref[...], dt_ref[...])
        g_ref[...] = gcs
        b_ref[...] = beta

    ident = lambda i: i
    hm = pl.BlockSpec((AH, tm, ADK), lambda i: (0, i, 0))
    return pl.pallas_call(
        body, name="prep_fwd", grid=(nt,),
        in_specs=_halo_specs(tm, wq, 0, ident) + [
            _rows(tm, 128, C_BA // 128), _rows(tm, 128, C_BA // 128 + 1),
            _const((4, wq)), _const((1, 128)), _const((1, 128))],
        out_specs=[hm, hm, hm, _rows(tm, 128), _rows(tm, 128)],
        out_shape=[jax.ShapeDtypeStruct((AH, t, ADK), F32)] * 3 + [jax.ShapeDtypeStruct((t, 128), F32)] * 2,
        compiler_params=_cparams(),
    )(proj, proj, proj, proj, conv_a, a_log, dtb)


def prep_bwd(proj, conv_a, a_log, dtb, dq, dk, dv, dgcs, dbeta):
    t = proj.shape[0]
    tm = PREP_TM
    nt = t // tm
    wq = 3 * D
    rev = lambda i: nt - 1 - i

    def body(prev_ref, cur_ref, bb_ref, aa_ref, cw_ref, al_ref, dt_ref,
             dq_ref, dk_ref, dv_ref, dg_ref, db_ref,
             dpre_ref, dbb_ref, daa_ref, dcw_ref, dal_ref, ddt_ref, carry):
        i = pl.program_id(0)
        flag = jnp.where(i < nt - 1, 1.0, 0.0)

        @pl.when(i == 0)
        def _():
            carry[...] = jnp.zeros_like(carry)
            dcw_ref[...] = jnp.zeros_like(dcw_ref)
            dal_ref[...] = jnp.zeros_like(dal_ref)
            ddt_ref[...] = jnp.zeros_like(ddt_ref)

        for part, d_ref in enumerate((dq_ref, dk_ref, dv_ref)):
            for h in range(AH):
                sl = slice(part * D + h * ADK, part * D + (h + 1) * ADK)
                ext = jnp.concatenate([prev_ref[:, sl] * flag, cur_ref[:, sl]], axis=0)
                rows = tuple(cw_ref[j:j + 1, sl] for j in range(4))
                _, vjp = jax.vjp(lambda e, r: prep_head_fn(e, r, PREP_SCALES[part]), ext, rows)
                dext, drows = vjp(d_ref[h])
                dcur = dext[HALO:]
                dpre_ref[:, sl] = jnp.concatenate([dcur[:tm - HALO], dcur[tm - HALO:] + carry[:, sl]],
                                                  axis=0).astype(BF16)
                carry[:, sl] = dext[:HALO]
                dcw_ref[:, sl] += _stack_rows(drows)
        _, vjp = jax.vjp(prep_gate_fn, bb_ref[...], aa_ref[...], al_ref[...], dt_ref[...])
        dbb, daa, dal, ddt = vjp((dg_ref[...], db_ref[...]))
        dbb_ref[...] = dbb.astype(BF16)
        daa_ref[...] = daa.astype(BF16)
        dal_ref[...] += dal
        ddt_ref[...] += ddt

    hm = pl.BlockSpec((AH, tm, ADK), lambda i: (0, rev(i), 0))
    return pl.pallas_call(
        body, name="prep_bwd", grid=(nt,),
        in_specs=_halo_specs(tm, wq, 0, rev) + [
            _rows(tm, 128, C_BA // 128, rev), _rows(tm, 128, C_BA // 128 + 1, rev),
            _const((4, wq)), _const((1, 128)), _const((1, 128)),
            hm, hm, hm, _rows(tm, 128, 0, rev), _rows(tm, 128, 0, rev)],
        out_specs=[_rows(tm, wq, 0, rev), _rows(tm, 128, 0, rev), _rows(tm, 128, 0, rev),
                   _const((4, wq)), _const((1, 128)), _const((1, 128))],
        out_shape=[jax.ShapeDtypeStruct((t, wq), BF16), jax.ShapeDtypeStruct((t, 128), BF16),
                   jax.ShapeDtypeStruct((t, 128), BF16), jax.ShapeDtypeStruct((4, wq), F32),
                   jax.ShapeDtypeStruct((1, 128), F32), jax.ShapeDtypeStruct((1, 128), F32)],
        scratch_shapes=[pltpu.VMEM((HALO, wq), F32)],
        compiler_params=_cparams(),
    )(proj, proj, proj, proj, conv_a, a_log, dtb, dq, dk, dv, dgcs, dbeta)


def _c1_specs(order):
    hm = pl.BlockSpec((AH, CH, ADK), lambda n: (0, order(n), 0))
    col = pl.BlockSpec((CH, LANE), lambda n: (order(n), 0))
    qk = pl.BlockSpec((1, AH, CH, CH), lambda n: (order(n), 0, 0, 0))
    eg = pl.BlockSpec((1, AH, 1, ADK), lambda n: (order(n), 0, 0, 0))
    return hm, col, qk, eg


def _heads(ref):
    return jnp.stack([ref[:, h * ADK:(h + 1) * ADK] for h in range(AH)], axis=0)


def c1_fwd(q, k, v, gcs, beta):
    t = q.shape[1]
    nc = t // CH
    hm, col, qks, egs = _c1_specs(lambda n: n)

    def body(q_ref, k_ref, v_ref, g_ref, b_ref, u_ref, w_ref, qg_ref, kd_ref, qk_ref, eg_ref, ti_ref):
        u, w, qk, qg, kd, eg, tinv = c1_heads(q_ref[...], k_ref[...], v_ref[...], g_ref[...], b_ref[...])
        u_ref[...] = u
        w_ref[...] = w.astype(BF16)
        qg_ref[...] = qg.astype(BF16)
        kd_ref[...] = kd.astype(BF16)
        qk_ref[0] = qk.astype(BF16)
        eg_ref[0] = eg
        ti_ref[0] = tinv

    return pl.pallas_call(
        body, name="c1_fwd", grid=(nc,),
        in_specs=[hm, hm, hm, col, col],
        out_specs=[hm, hm, hm, hm, qks, egs, qks],
        out_shape=[jax.ShapeDtypeStruct((AH, t, ADK), F32)] + [jax.ShapeDtypeStruct((AH, t, ADK), BF16)] * 3 + [
            jax.ShapeDtypeStruct((nc, AH, CH, CH), BF16), jax.ShapeDtypeStruct((nc, AH, 1, ADK), F32),
            jax.ShapeDtypeStruct((nc, AH, CH, CH), F32)],
        compiler_params=_cparams(),
    )(q, k, v, gcs, beta)


def c1_bwd(q, k, v, gcs, beta, tinv, du, dw, dqg, dkd, dqk, deg):
    t = q.shape[1]
    nc = t // CH
    hm, col, qks, egs = _c1_specs(lambda n: n)

    def body(q_ref, k_ref, v_ref, g_ref, b_ref, ti_ref, du_ref, dw_ref, dqg_ref, dkd_ref, dqk_ref, deg_ref,
             dq_ref, dk_ref, dv_ref, dg_ref, db_ref):
        _, vjp = jax.vjp(lambda q_, k_, v_, g_, b_: c1_heads(q_, k_, v_, g_, b_, ti_ref[0]),
                         q_ref[...], k_ref[...], v_ref[...], g_ref[...], b_ref[...])
        dq, dk, dv, dg, db = vjp((du_ref[...].astype(F32), dw_ref[...].astype(F32), dqk_ref[0], dqg_ref[...],
                                  dkd_ref[...], deg_ref[0],
                                  jnp.zeros((AH, CH, CH), F32)))
        dq_ref[...] = dq
        dk_ref[...] = dk
        dv_ref[...] = dv
        dg_ref[...] = dg
        db_ref[...] = db

    return pl.pallas_call(
        body, name="c1_bwd", grid=(nc,),
        in_specs=[hm, hm, hm, col, col, qks, hm, hm, hm, hm, qks, egs],
        out_specs=[hm, hm, hm, col, col],
        out_shape=[jax.ShapeDtypeStruct((AH, t, ADK), F32)] * 3 + [jax.ShapeDtypeStruct((t, LANE), F32)] * 2,
        compiler_params=_cparams(),
    )(q, k, v, gcs, beta, tinv, du, dw, dqg, dkd, dqk, deg)


def c2_fwd(u, w, qg, kd, qk, eg, proj, norm_a):
    t = u.shape[1]
    nc = t // CH
    hm, _, qks, egs = _c1_specs(lambda n: n)
    tok = pl.BlockSpec((CH, D), lambda n: (n, 0))
    zspec = pl.BlockSpec((CH, D), lambda n: (n, C_Z // D))
    sspec = pl.BlockSpec((1, AH, ADK, ADK), lambda n: (n, 0, 0, 0))

    def body(u_ref, w_ref, qg_ref, kd_ref, qk_ref, eg_ref, z_ref, nw_ref, o_ref, sall_ref, st):
        n = pl.program_id(0)

        @pl.when(n == 0)
        def _():
            st[...] = jnp.zeros_like(st)

        s = st[...]
        sall_ref[0] = s
        og, s2 = c2_heads(s, u_ref[...], w_ref[...], qk_ref[0], qg_ref[...], kd_ref[...], eg_ref[0],
                          _heads(z_ref), nw_ref[...])
        st[...] = s2
        for h in range(AH):
            o_ref[:, h * ADK:(h + 1) * ADK] = og[h].astype(BF16)

    return pl.pallas_call(
        body, name="c2_fwd", grid=(nc,),
        in_specs=[hm, hm, hm, hm, qks, egs, zspec, _const((1, ADK))],
        out_specs=[tok, sspec],
        out_shape=[jax.ShapeDtypeStruct((t, D), BF16), jax.ShapeDtypeStruct((nc, AH, ADK, ADK), F32)],
        scratch_shapes=[pltpu.VMEM((AH, ADK, ADK), F32)],
        compiler_params=_cparams(),
    )(u, w, qg, kd, qk, eg, proj, norm_a)


def c2_bwd(u, w, qg, kd, qk, eg, proj, norm_a, sall, do):
    t = u.shape[1]
    nc = t // CH
    rev = lambda n: nc - 1 - n
    hm, _, qks, egs = _c1_specs(rev)
    tok = pl.BlockSpec((CH, D), lambda n: (rev(n), 0))
    zspec = pl.BlockSpec((CH, D), lambda n: (rev(n), C_Z // D))
    sspec = pl.BlockSpec((1, AH, ADK, ADK), lambda n: (rev(n), 0, 0, 0))

    def body(u_ref, w_ref, qg_ref, kd_ref, qk_ref, eg_ref, z_ref, nw_ref, sall_ref, do_ref,
             du_ref, dw_ref, dqg_ref, dkd_ref, dqk_ref, deg_ref, dz_ref, dnw_ref, dst):
        n = pl.program_id(0)

        @pl.when(n == 0)
        def _():
            dst[...] = jnp.zeros_like(dst)
            dnw_ref[...] = jnp.zeros_like(dnw_ref)

        _, vjp = jax.vjp(c2_heads, sall_ref[0], u_ref[...], w_ref[...].astype(F32), qk_ref[0].astype(F32),
                         qg_ref[...].astype(F32), kd_ref[...].astype(F32), eg_ref[0], _heads(z_ref), nw_ref[...])
        ds, du, dw, dqk, dqg, dkd, deg, dz, dn = vjp((_heads(do_ref), dst[...]))
        dst[...] = ds
        du_ref[...] = du.astype(BF16)
        dw_ref[...] = dw.astype(BF16)
        dqg_ref[...] = dqg
        dkd_ref[...] = dkd
        dqk_ref[0] = dqk
        deg_ref[0] = deg
        for h in range(AH):
            dz_ref[:, h * ADK:(h + 1) * ADK] = dz[h].astype(BF16)
        dnw_ref[...] += dn

    return pl.pallas_call(
        body, name="c2_bwd", grid=(nc,),
        in_specs=[hm, hm, hm, hm, qks, egs, zspec, _const((1, ADK)), sspec, tok],
        out_specs=[hm, hm, hm, hm, qks, egs, tok, _const((1, ADK))],
        out_shape=[jax.ShapeDtypeStruct((AH, t, ADK), BF16)] * 2 + [jax.ShapeDtypeStruct((AH, t, ADK), F32)] * 2 + [
            jax.ShapeDtypeStruct((nc, AH, CH, CH), F32), jax.ShapeDtypeStruct((nc, AH, 1, ADK), F32),
            jax.ShapeDtypeStruct((t, D), BF16), jax.ShapeDtypeStruct((1, ADK), F32)],
        scratch_shapes=[pltpu.VMEM((AH, ADK, ADK), F32)],
        compiler_params=_cparams(),
    )(u, w, qg, kd, qk, eg, proj, norm_a, sall, do)


NQB = TQ // CH
NKB = 2 * TQ // CH
NDIST = BPREV + 1
KLO = -(NQB - 2)
NPAIR = NKB - 1 - KLO + 1


def bias_table(rel_bias):
    nh = rel_bias.shape[0]
    relx = jnp.concatenate([rel_bias, jnp.broadcast_to(rel_bias[:, -1:], (nh, CH * BPREV + 2 * CH - 1 - RELSZ))],
                           axis=1)
    t = jnp.stack([relx[:, CH * k:CH * k + 2 * CH - 1] for k in range(NDIST)], axis=1)
    trev = t[:, :, ::-1]
    g2 = jnp.concatenate([trev[:, :, CH - 1:], jnp.zeros((nh, NDIST, 1), F32), trev[:, :, :CH - 1]], axis=2)
    flat = jnp.tile(g2, (1, 1, CH + 1))[:, :, :CH * (2 * CH - 1)]
    blk = flat.reshape(nh, NDIST, CH, 2 * CH - 1)[..., :CH]
    neg = jnp.full((nh, NQB - 1, CH, CH), NEG, F32)
    asc = jnp.concatenate([neg, blk, neg], axis=1)
    return jnp.concatenate([asc[:, 1:], asc[:, :-1]], axis=-1)


SUBQ = 4 * CH
NSUB = TQ // SUBQ
KWIN = SUBQ + BPREV * CH


def assemble_bias(tab, r):
    b0 = r * SUBQ // (2 * CH)
    rows = [jnp.concatenate([tab[NQB + a - 2 * b - KLO] for b in range(b0, b0 + KWIN // (2 * CH))], axis=1)
            for a in range(r * SUBQ // CH, (r + 1) * SUBQ // CH)]
    return jnp.concatenate(rows, axis=0)


def bias_table_bwd_layout(dtab):
    nh = dtab.shape[0]
    dasc = (jnp.pad(dtab[..., :CH], ((0, 0), (1, 0), (0, 0), (0, 0)))
            + jnp.pad(dtab[..., CH:], ((0, 0), (0, 1), (0, 0), (0, 0))))
    dblk = dasc[:, NQB - 1:NQB - 1 + NDIST]
    dr = jnp.pad(dblk, ((0, 0), (0, 0), (0, 0), (0, CH - 1)))
    flat = jnp.pad(dr.reshape(nh, NDIST, CH * (2 * CH - 1)), ((0, 0), (0, 0), (0, 3 * CH)))
    return flat.reshape(nh, NDIST, CH + 1, 2 * CH).transpose(0, 2, 1, 3).reshape(nh, CH + 1, NDIST * 2 * CH)


def _fold_matrix_np():
    f = np.zeros((NDIST * 2 * CH, 384), np.float32)
    for k in range(NDIST):
        s = k
        for xx in range(2 * CH):
            if xx == CH:
                continue
            m = CH - 1 - xx if xx < CH else 3 * CH - 1 - xx
            f[s * 2 * CH + xx, min(CH * k + m, RELSZ - 1)] = 1.0
    return f


def relbias_reduce(dlay):
    nh, rows, cols = dlay.shape
    rpad = (-rows) % 8
    dlay = jnp.pad(dlay, ((0, 0), (0, rpad), (0, 0)))
    fold = jnp.asarray(_fold_matrix_np())

    def body(d_ref, f_ref, o_ref):
        cs = jnp.sum(d_ref[0], axis=0, keepdims=True)
        o_ref[0] = _mmh(jnp.broadcast_to(cs, (8, cols)), f_ref[...])

    out = pl.pallas_call(
        body, name="relbias_reduce", grid=(nh,),
        in_specs=[pl.BlockSpec((1, rows + rpad, cols), lambda h: (h, 0, 0)), _const((cols, 384))],
        out_specs=pl.BlockSpec((1, 8, 384), lambda h: (h, 0, 0)),
        out_shape=jax.ShapeDtypeStruct((nh, 8, 384), F32),
        compiler_params=_cparams(),
    )(dlay, fold)
    return out[:, 0, :RELSZ]


def attn_fwd(proj, bias):
    t = proj.shape[0]
    nt = t // TQ
    cb = C_QKVB // 128

    def body(q_ref, kp_ref, kc_ref, vp_ref, vc_ref, b_ref, o_ref, p_ref):
        i = pl.program_id(1)
        firstf = jnp.where(i == 0, 1.0, 0.0)
        for r in range(NSUB):
            lo, hi = r * SUBQ, r * SUBQ + KWIN - TQ
            kw = jnp.concatenate([kp_ref[lo:, :], kc_ref[:hi, :]], axis=0)
            vw = jnp.concatenate([vp_ref[lo:, :], vc_ref[:hi, :]], axis=0)
            out, probs = attn_sub_fwd(q_ref[lo:lo + SUBQ, :].astype(F32), kw, vw, b_ref[...], r, firstf)
            o_ref[lo:lo + SUBQ, :] = out.astype(BF16)
            for hh in range(2):
                p_ref[hh, lo:lo + SUBQ, :] = probs[hh].astype(BF16)

    def blk(off, prev):
        if prev:
            return pl.BlockSpec((TQ, 128), lambda p, i: (jnp.maximum(i - 1, 0), cb + off + p))
        return pl.BlockSpec((TQ, 128), lambda p, i: (i, cb + off + p))

    return pl.pallas_call(
        body, name="attn_fwd", grid=(BH // 2, nt),
        in_specs=[blk(0, False), blk(8, True), blk(8, False), blk(16, True), blk(16, False),
                  pl.BlockSpec((2, NPAIR, CH, 2 * CH), lambda p, i: (p, 0, 0, 0))],
        out_specs=[pl.BlockSpec((TQ, 128), lambda p, i: (i, p)),
                   pl.BlockSpec((2, TQ, KWIN), lambda p, i: (p, i, 0))],
        out_shape=[jax.ShapeDtypeStruct((t, D), BF16), jax.ShapeDtypeStruct((BH, t, KWIN), BF16)],
        compiler_params=_cparams(2),
    )(proj, proj, proj, proj, proj, bias)


def attn_bwd(proj, ob, probs, do):
    t = proj.shape[0]
    nt = t // TQ
    cb = C_QKVB // 128

    def body(q_ref, kp_ref, kc_ref, vp_ref, vc_ref, o_ref, p_ref, do_ref,
             dq_ref, dk_ref, dv_ref, db_ref, ck, cv, ak, av):
        i = pl.program_id(1)

        @pl.when(i == 0)
        def _():
            ck[...] = jnp.zeros_like(ck)
            cv[...] = jnp.zeros_like(cv)
            db_ref[...] = jnp.zeros_like(db_ref)

        @pl.when(i < nt)
        def _():
            ak[...] = jnp.zeros_like(ak)
            av[...] = jnp.zeros_like(av)
            for r in range(NSUB):
                lo, hi = r * SUBQ, r * SUBQ + KWIN - TQ
                rows = slice(lo, lo + SUBQ)
                kw = jnp.concatenate([kp_ref[lo:, :], kc_ref[:hi, :]], axis=0)
                vw = jnp.concatenate([vp_ref[lo:, :], vc_ref[:hi, :]], axis=0)
                probs_r = [p_ref[hh, rows, :].astype(F32) for hh in range(2)]
                dq, dkw, dvw, dss = attn_sub_bwd(q_ref[rows, :].astype(F32), kw, vw, o_ref[rows, :].astype(F32),
                                                 do_ref[rows, :], probs_r, r)
                dq_ref[rows, :] = dq.astype(BF16)
                ak[lo:lo + KWIN, :] += dkw
                av[lo:lo + KWIN, :] += dvw
                for hh in range(2):
                    _, scatter = jax.vjp(lambda tab: assemble_bias(tab, r), jnp.zeros((NPAIR, CH, 2 * CH), F32))
                    db_ref[hh] += scatter(dss[hh])[0]
            dk_ref[...] = (ck[...] + ak[:TQ, :]).astype(BF16)
            dv_ref[...] = (cv[...] + av[:TQ, :]).astype(BF16)
            ck[...] = ak[TQ:, :]
            cv[...] = av[TQ:, :]

        @pl.when(i == nt)
        def _():
            dk_ref[...] = ck[...].astype(BF16)
            dv_ref[...] = cv[...].astype(BF16)

    def blk(off, prev):
        if prev:
            return pl.BlockSpec((TQ, 128), lambda p, i: (jnp.clip(i - 1, 0, nt - 1), cb + off + p))
        return pl.BlockSpec((TQ, 128), lambda p, i: (jnp.minimum(i, nt - 1), cb + off + p))

    own = pl.BlockSpec((TQ, 128), lambda p, i: (jnp.minimum(i, nt - 1), p))
    lag = pl.BlockSpec((TQ, 128), lambda p, i: (jnp.maximum(i - 1, 0), p))
    return pl.pallas_call(
        body, name="attn_bwd", grid=(BH // 2, nt + 1),
        in_specs=[blk(0, False), blk(8, True), blk(8, False), blk(16, True), blk(16, False), own,
                  pl.BlockSpec((2, TQ, KWIN), lambda p, i: (p, jnp.minimum(i, nt - 1), 0)), own],
        out_specs=[own, lag, lag, pl.BlockSpec((2, NPAIR, CH, 2 * CH), lambda p, i: (p, 0, 0, 0))],
        out_shape=[jax.ShapeDtypeStruct((t, D), BF16)] * 3 + [jax.ShapeDtypeStruct((BH, NPAIR, CH, 2 * CH), F32)],
        scratch_shapes=[pltpu.VMEM((TQ, 128), F32), pltpu.VMEM((TQ, 128), F32),
                        pltpu.VMEM((2 * TQ, 128), F32), pltpu.VMEM((2 * TQ, 128), F32)],
        compiler_params=_cparams(2),
    )(proj, proj, proj, proj, proj, ob, probs, do)


MERGE_TM = 256


def merge_fwd(x, oa, ob, proj, vecs, wa, wb, wo):
    t = x.shape[0]
    tm = MERGE_TM
    names = ("bga", "bgb", "gate_t", "g1", "b1", "scale_f", "shift_f")

    def body(x_ref, oa_ref, ob_ref, gra_ref, grb_ref, *rest):
        vrefs = rest[:7]
        wa_ref, wb_ref, wo_ref, y_ref, h_ref = rest[7:]
        vv = [r[...] for r in vrefs]
        zero = jnp.zeros((tm, D), F32)
        y1, _ = merge_fn(x_ref[...], oa_ref[...], ob_ref[...], gra_ref[...], grb_ref[...], zero, zero, zero,
                         *vv, wa_ref[...], wb_ref[...], wo_ref[...])
        y_ref[...] = y1
        h_ref[...] = (y1 * (1.0 + vv[5]) + vv[6]).astype(BF16)

    return pl.pallas_call(
        body, name="merge_fwd", grid=(t // tm,),
        in_specs=[_rows(tm, D), _rows(tm, D), _rows(tm, D), _rows(tm, D, C_GATE // D), _rows(tm, D, C_GATE // D + 1)]
        + [_const((1, D))] * 7 + [_const((D, D))] * 3,
        out_specs=[_rows(tm, D), _rows(tm, D)],
        out_shape=[jax.ShapeDtypeStruct((t, D), F32), jax.ShapeDtypeStruct((t, D), BF16)],
        compiler_params=_cparams(),
    )(x, oa, ob, proj, proj, *[vecs[n] for n in names], wa, wb, wo)


def merge_bwd(x, oa, ob, proj, vecs, wa, wb, wo, dy1):
    t = x.shape[0]
    tm = MERGE_TM
    names = ("bga", "bgb", "gate_t", "g1", "b1", "scale_f", "shift_f")

    def body(x_ref, oa_ref, ob_ref, gra_ref, grb_ref, *rest):
        vrefs = rest[:7]
        wa_ref, wb_ref, wo_ref, dy_ref = rest[7:11]
        (dx_ref, doa_ref, dob_ref, dga_ref, dgb_ref, mg_ref, dmix_ref, dpa_ref, dpb_ref,
         dbga_ref, dbgb_ref, dgt_ref, dg1_ref, db1_ref) = rest[11:]
        i = pl.program_id(0)
        vv = [r[...] for r in vrefs]
        zero = jnp.zeros((tm, D), F32)

        def f(x_, oa_, ob_, gra_, grb_, ppa, ppb, pmix, bga, bgb, gate_t, g1, b1):
            return merge_fn(x_, oa_, ob_, gra_, grb_, ppa, ppb, pmix, bga, bgb, gate_t, g1, b1, vv[5], vv[6],
                            wa_ref[...], wb_ref[...], wo_ref[...])

        _, vjp, merged = jax.vjp(f, x_ref[...], oa_ref[...].astype(F32), ob_ref[...].astype(F32),
                                 gra_ref[...], grb_ref[...], zero, zero, zero, *vv[:5], has_aux=True)
        dx, doa, dob, dga, dgb, dpa, dpb, dmix, dbga, dbgb, dgt, dg1, db1 = vjp(dy_ref[...])
        dx_ref[...] = dx
        doa_ref[...] = doa
        dob_ref[...] = dob
        dga_ref[...] = dga.astype(BF16)
        dgb_ref[...] = dgb.astype(BF16)
        mg_ref[...] = merged.astype(BF16)
        dmix_ref[...] = dmix.astype(BF16)
        dpa_ref[...] = dpa.astype(BF16)
        dpb_ref[...] = dpb.astype(BF16)
        accs = (dbga_ref, dbgb_ref, dgt_ref, dg1_ref, db1_ref)

        @pl.when(i == 0)
        def _():
            for a in accs:
                a[...] = jnp.zeros_like(a)

        for a, val in zip(accs, (dbga, dbgb, dgt, dg1, db1)):
            a[...] += val

    return pl.pallas_call(
        body, name="merge_bwd", grid=(t // tm,),
        in_specs=[_rows(tm, D), _rows(tm, D), _rows(tm, D), _rows(tm, D, C_GATE // D), _rows(tm, D, C_GATE // D + 1)]
        + [_const((1, D))] * 7 + [_const((D, D))] * 3 + [_rows(tm, D)],
        out_specs=[_rows(tm, D)] * 9 + [_const((1, D))] * 5,
        out_shape=[jax.ShapeDtypeStruct((t, D), F32)] * 3 + [jax.ShapeDtypeStruct((t, D), BF16)] * 6
        + [jax.ShapeDtypeStruct((1, D), F32)] * 5,
        compiler_params=_cparams(),
    )(x, oa, ob, proj, proj, *[vecs[n] for n in names], wa, wb, wo, dy1)


FFN_TM = 128


def ffn_act_fwd(up, conv_w, bconv):
    t, wdt = up.shape
    tm = FFN_TM

    def body(prev_ref, cur_ref, cw_ref, bc_ref, a_ref):
        i = pl.program_id(0)
        flag = jnp.where(i > 0, 1.0, 0.0)

        def ext(sl):
            return jnp.concatenate([prev_ref[:, sl] * flag, cur_ref[:, sl]], axis=0)

        def rows(sl):
            return tuple(cw_ref[j:j + 1, sl] for j in range(3))

        for cb in range(DFF // LANE):
            g = slice(cb * LANE, (cb + 1) * LANE)
            v = slice(DFF + cb * LANE, DFF + (cb + 1) * LANE)
            a_ref[:, g] = ffn_act_fn(ext(g), ext(v), rows(g), rows(v), bc_ref[:, g], bc_ref[:, v]).astype(BF16)

    return pl.pallas_call(
        body, name="ffn_act_fwd", grid=(t // tm,),
        in_specs=_halo_specs(tm, wdt, 0, lambda i: i) + [_const((3, wdt)), _const((1, wdt))],
        out_specs=_rows(tm, DFF),
        out_shape=jax.ShapeDtypeStruct((t, DFF), BF16),
        compiler_params=_cparams(),
    )(up, up, conv_w, bconv)


def ffn_act_bwd(up, conv_w, bconv, da):
    t, wdt = up.shape
    tm = FFN_TM
    nt = t // tm
    rev = lambda i: nt - 1 - i

    def body(prev_ref, cur_ref, cw_ref, bc_ref, da_ref, dup_ref, dcw_ref, dbc_ref, carry):
        i = pl.program_id(0)
        flag = jnp.where(i < nt - 1, 1.0, 0.0)

        @pl.when(i == 0)
        def _():
            carry[...] = jnp.zeros_like(carry)
            dcw_ref[...] = jnp.zeros_like(dcw_ref)
            dbc_ref[...] = jnp.zeros_like(dbc_ref)

        def ext(sl):
            return jnp.concatenate([prev_ref[:, sl] * flag, cur_ref[:, sl]], axis=0)

        def rows(sl):
            return tuple(cw_ref[j:j + 1, sl] for j in range(3))

        def emit(sl, dext, drows, dbc):
            dcur = dext[HALO:]
            dup_ref[:, sl] = jnp.concatenate([dcur[:tm - HALO], dcur[tm - HALO:] + carry[:, sl]], axis=0).astype(BF16)
            carry[:, sl] = dext[:HALO]
            dcw_ref[:, sl] += _stack_rows(drows)
            dbc_ref[:, sl] += dbc

        for cb in range(DFF // LANE):
            g = slice(cb * LANE, (cb + 1) * LANE)
            v = slice(DFF + cb * LANE, DFF + (cb + 1) * LANE)
            _, vjp = jax.vjp(ffn_act_fn, ext(g), ext(v), rows(g), rows(v), bc_ref[:, g], bc_ref[:, v])
            dxg, dxv, drg, drv, dbg, dbv = vjp(da_ref[:, g])
            emit(g, dxg, drg, dbg)
            emit(v, dxv, drv, dbv)

    return pl.pallas_call(
        body, name="ffn_act_bwd", grid=(nt,),
        in_specs=_halo_specs(tm, wdt, 0, rev) + [_const((3, wdt)), _const((1, wdt)), _rows(tm, DFF, 0, rev)],
        out_specs=[_rows(tm, wdt, 0, rev), _const((3, wdt)), _const((1, wdt))],
        out_shape=[jax.ShapeDtypeStruct((t, wdt), BF16), jax.ShapeDtypeStruct((3, wdt), F32),
                   jax.ShapeDtypeStruct((1, wdt), F32)],
        scratch_shapes=[pltpu.VMEM((HALO, wdt), F32)],
        compiler_params=_cparams(),
    )(up, up, conv_w, bconv, da)


HEAD_TM = 256


def head_fwd_bwd(a, y1, tgt, gate_f, g2, b2, wd):
    t = a.shape[0]
    tm = HEAD_TM

    def body(a_ref, y_ref, t_ref, gf_ref, g2_ref, b2_ref, wd_ref,
             da_ref, dy_ref, dffn_ref, dgf_ref, dg2_ref, db2_ref, loss_ref):
        i = pl.program_id(0)
        zero = jnp.zeros((tm, D), F32)

        def f(a_, y_, pf, gf, g2_, b2_):
            return head_fn(a_, y_, pf, gf, g2_, b2_, t_ref[...], wd_ref[...])

        loss, vjp = jax.vjp(f, a_ref[...].astype(F32), y_ref[...], zero, gf_ref[...], g2_ref[...], b2_ref[...])
        da, dy, dffn, dgf, dg2, db2 = vjp(jnp.ones((), F32))
        da_ref[...] = da
        dy_ref[...] = dy
        dffn_ref[...] = dffn.astype(BF16)
        accs = (dgf_ref, dg2_ref, db2_ref, loss_ref)

        @pl.when(i == 0)
        def _():
            for r in accs:
                r[...] = jnp.zeros_like(r)

        dgf_ref[...] += dgf
        dg2_ref[...] += dg2
        db2_ref[...] += db2
        loss_ref[...] += loss * jnp.ones((1, 128), F32)

    return pl.pallas_call(
        body, name="head_fwd_bwd", grid=(t // tm,),
        in_specs=[_rows(tm, DFF), _rows(tm, D), _rows(tm, D), _const((1, D)), _const((1, D)), _const((1, D)),
                  _const((DFF, D))],
        out_specs=[_rows(tm, DFF), _rows(tm, D), _rows(tm, D), _const((1, D)), _const((1, D)), _const((1, D)),
                   _const((1, 128))],
        out_shape=[jax.ShapeDtypeStruct((t, DFF), F32), jax.ShapeDtypeStruct((t, D), F32),
                   jax.ShapeDtypeStruct((t, D), BF16)] + [jax.ShapeDtypeStruct((1, D), F32)] * 3
        + [jax.ShapeDtypeStruct((1, 128), F32)],
        compiler_params=_cparams(),
    )(a, y1, tgt, gate_f, g2, b2, wd)


def ada_fwd(c_all, w_sh, b_sh):
    def body(c_ref, w_ref, b_ref, o_ref):
        o_ref[...] = _mmh(_silu(c_ref[...]), w_ref[...]) + b_ref[...]

    n = w_sh.shape[1]
    return pl.pallas_call(
        body, name="ada_fwd", out_shape=jax.ShapeDtypeStruct((NDEV, n), F32),
        in_specs=[pl.BlockSpec(memory_space=pltpu.VMEM)] * 3,
        out_specs=pl.BlockSpec(memory_space=pltpu.VMEM),
        compiler_params=pltpu.CompilerParams(vmem_limit_bytes=VMEM_LIMIT),
    )(c_all, w_sh, b_sh)


def ada_wgrad(c_all_t, dmod_sh):
    def body(c_ref, d_ref, o_ref):
        o_ref[...] = _mmh(_silu(c_ref[...]), d_ref[...])

    return pl.pallas_call(
        body, name="ada_wgrad", out_shape=jax.ShapeDtypeStruct((c_all_t.shape[0], dmod_sh.shape[1]), F32),
        in_specs=[pl.BlockSpec(memory_space=pltpu.VMEM)] * 2,
        out_specs=pl.BlockSpec(memory_space=pltpu.VMEM),
        compiler_params=pltpu.CompilerParams(vmem_limit_bytes=VMEM_LIMIT),
    )(c_all_t, dmod_sh)


def adamw(gparts, w, m, v, name):
    p, r, c = gparts.shape
    tr = r if r <= 256 else _pick(r, (256, 128, 64, 32, 16, 8))
    c1 = 1.0 - B1 ** STEP
    c2 = 1.0 - B2 ** STEP

    def body(g_ref, w_ref, m_ref, v_ref, go_ref, d_ref, mo_ref, vo_ref):
        g = g_ref[0].astype(F32)
        for s in range(1, p):
            g = g + g_ref[s].astype(F32)
        mn = B1 * m_ref[0] + (1.0 - B1) * g
        vn = B2 * v_ref[0] + (1.0 - B2) * (g * g)
        go_ref[0] = g
        d_ref[0] = -LR * ((mn / c1) / (jnp.sqrt(vn / c2) + AEPS) + WD * w_ref[0])
        mo_ref[0] = mn
        vo_ref[0] = vn

    spec = pl.BlockSpec((1, tr, c), lambda i: (0, i, 0))
    return pl.pallas_call(
        body, name=name, grid=(r // tr,),
        in_specs=[pl.BlockSpec((p, tr, c), lambda i: (0, i, 0)), spec, spec, spec],
        out_specs=[spec] * 4,
        out_shape=[jax.ShapeDtypeStruct((1, r, c), F32)] * 4,
        compiler_params=_cparams(),
    )(gparts, w, m, v)


def adamw_small(gs, ws, ms, vs, loss_parts, name):
    n = len(ws)
    c1 = 1.0 - B1 ** STEP
    c2 = 1.0 - B2 ** STEP

    def slots(ref):
        acc = ref[0]
        for s in range(1, ref.shape[0]):
            acc = acc + ref[s]
        return acc

    def body(*refs):
        g_refs, w_refs, m_refs, v_refs = (refs[k * n:(k + 1) * n] for k in range(4))
        l_ref, outs = refs[4 * n], refs[4 * n + 1:]
        for i in range(n):
            g = slots(g_refs[i])
            mn = B1 * m_refs[i][...] + (1.0 - B1) * g
            vn = B2 * v_refs[i][...] + (1.0 - B2) * (g * g)
            outs[i][...] = g
            outs[n + i][...] = -LR * ((mn / c1) / (jnp.sqrt(vn / c2) + AEPS) + WD * w_refs[i][...])
            outs[2 * n + i][...] = mn
            outs[3 * n + i][...] = vn
        outs[4 * n][...] = slots(l_ref)

    vmem = pl.BlockSpec(memory_space=pltpu.VMEM)
    outs = pl.pallas_call(
        body, name=name,
        in_specs=[vmem] * (4 * n + 1), out_specs=[vmem] * (4 * n + 1),
        out_shape=[jax.ShapeDtypeStruct(w.shape, F32) for w in ws] * 4 + [jax.ShapeDtypeStruct((1, LANE), F32)],
        compiler_params=pltpu.CompilerParams(vmem_limit_bytes=VMEM_LIMIT),
    )(*gs, *ws, *ms, *vs, loss_parts)
    return outs[:n], outs[n:2 * n], outs[2 * n:3 * n], outs[3 * n:4 * n], outs[4 * n]


def _me():
    x, y, c = lax.axis_index("x"), lax.axis_index("y"), lax.axis_index("c")
    return x, y, c, 4 * x + 2 * y + c


def _peer(x, y, c, d):
    px = 1 - x if (d >> 2) & 1 else x
    py = 1 - y if (d >> 1) & 1 else y
    pc = 1 - c if d & 1 else c
    return (px, py, pc), 4 * px + 2 * py + pc


def _exchange(arrs, name, scatter):
    n = len(arrs)

    def body(*refs):
        ins, outs = refs[:n], refs[n:2 * n]
        send, recv, lsem = refs[2 * n:]
        x, y, c, me = _me()
        remote, local = [], []
        for k in range(n):
            src = ins[k].at[me] if scatter else ins[k]
            cp = pltpu.make_async_copy(src, outs[k].at[me], lsem.at[k])
            cp.start()
            local.append(cp)
            for d in range(1, NDEV):
                dev, pid = _peer(x, y, c, d)
                src = ins[k].at[pid] if scatter else ins[k]
                cp = pltpu.make_async_remote_copy(src_ref=src, dst_ref=outs[k].at[me],
                                                  send_sem=send.at[k, d - 1], recv_sem=recv.at[k, d - 1],
                                                  device_id=dev, device_id_type=pl.DeviceIdType.MESH)
                cp.start()
                remote.append(cp)
        for cp in remote:
            cp.wait()
        for cp in local:
            cp.wait()

    shapes = [a.shape if scatter else (NDEV,) + a.shape for a in arrs]
    return pl.pallas_call(
        body, name=name,
        in_specs=[pl.BlockSpec(memory_space=pl.ANY)] * n,
        out_specs=[pl.BlockSpec(memory_space=pl.ANY)] * n,
        out_shape=[jax.ShapeDtypeStruct(s, a.dtype) for s, a in zip(shapes, arrs)],
        scratch_shapes=[pltpu.SemaphoreType.DMA((n, NDEV - 1)), pltpu.SemaphoreType.DMA((n, NDEV - 1)),
                        pltpu.SemaphoreType.DMA((n,))],
        compiler_params=pltpu.CompilerParams(has_side_effects=True),
    )(*arrs)


def all_gather(arrs, name):
    return _exchange(arrs, name, False)


def all_gather_two_level(shard, name):
    def body(x_ref, out_ref, send, recv, lsem):
        x, y, c, _ = _me()
        sibling = (x, y, 1 - c)
        chips = [(1 - x, y), (x, 1 - y), (1 - x, 1 - y)]

        def slot(px, py, pc):
            return out_ref.at[4 * px + 2 * py + pc]

        def copy(k, block, to, src=None):
            return pltpu.make_async_remote_copy(
                src_ref=slot(*block) if src is None else src, dst_ref=slot(*block),
                send_sem=send.at[k], recv_sem=recv.at[k], device_id=to, device_id_type=pl.DeviceIdType.MESH)

        mine = pltpu.make_async_copy(x_ref, slot(x, y, c), lsem)
        mine.start()
        first = [copy(0, (x, y, c), sibling, src=x_ref)]
        first += [copy(1 + j, (x, y, c), (*chip, c), src=x_ref) for j, chip in enumerate(chips)]
        for cp in first:
            cp.start()
        passed = [copy(4 + j, (*chip, c), sibling) for j, chip in enumerate(chips)]
        for j, chip in enumerate(chips):
            copy(1 + j, (*chip, c), (x, y, c)).wait_recv()
            passed[j].start()
        copy(0, sibling, (x, y, c)).wait_recv()
        for j, chip in enumerate(chips):
            copy(4 + j, (*chip, 1 - c), (x, y, c)).wait_recv()
        for cp in first + passed:
            cp.wait_send()
        mine.wait()

    return pl.pallas_call(
        body, name=name,
        in_specs=[pl.BlockSpec(memory_space=pl.ANY)],
        out_specs=pl.BlockSpec(memory_space=pl.ANY),
        out_shape=jax.ShapeDtypeStruct((NDEV,) + shard.shape, shard.dtype),
        scratch_shapes=[pltpu.SemaphoreType.DMA((NPEER,)), pltpu.SemaphoreType.DMA((NPEER,)),
                        pltpu.SemaphoreType.DMA],
        compiler_params=pltpu.CompilerParams(has_side_effects=True),
    )(shard)


def all_to_all(arrs, name):
    return _exchange(arrs, name, True)


_HBM = pl.BlockSpec(memory_space=pltpu.HBM)
_SEM = pl.BlockSpec(memory_space=pltpu.SEMAPHORE)
_EFFECT = pltpu.SideEffectType.DATAFLOW_SIDE_EFFECTING
NPEER = NDEV - 1


def exchange_start(arrs, name, scatter):
    n = len(arrs)
    lands = [lax.empty(a.shape if scatter else (NDEV,) + a.shape, a.dtype) for a in arrs]

    def body(*refs):
        ins, lrefs = refs[:n], refs[n:2 * n]
        send, recv, token = refs[2 * n], refs[2 * n + 1], refs[-1]
        x, y, c, me = _me()
        for k in range(n):
            for d in range(1, NDEV):
                dev, pid = _peer(x, y, c, d)
                src = ins[k].at[pid] if scatter else ins[k]
                pltpu.make_async_remote_copy(src_ref=src, dst_ref=lrefs[k].at[me],
                                             send_sem=send.at[k * NPEER + d - 1], recv_sem=recv.at[k * NPEER + d - 1],
                                             device_id=dev, device_id_type=pl.DeviceIdType.MESH).start()
        token[...] = jnp.zeros_like(token)

    thru = [pltpu.HBM(a.shape, a.dtype) for a in list(arrs) + lands]
    outs = pl.pallas_call(
        body, name=name,
        out_shape=(pltpu.SemaphoreType.DMA((n * NPEER,)), pltpu.SemaphoreType.DMA((n * NPEER,)), *thru,
                   jax.ShapeDtypeStruct((8, 128), F32)),
        in_specs=[_HBM] * (2 * n),
        out_specs=(_SEM, _SEM, *([_HBM] * (2 * n)), pl.BlockSpec(memory_space=pltpu.VMEM)),
        input_output_aliases={i: 2 + i for i in range(2 * n)},
        compiler_params=pltpu.CompilerParams(has_side_effects=_EFFECT),
    )(*[pltpu.with_memory_space_constraint(a, pltpu.HBM) for a in list(arrs) + lands])
    handle = dict(send=outs[0], recv=outs[1], src=list(outs[2:2 + n]), land=list(outs[2 + n:2 + 2 * n]),
                  scatter=scatter)
    return handle, outs[-1][0, 0]


def exchange_wait(handle, after, name):
    n = len(handle["src"])
    scatter = handle["scatter"]

    def body(*refs):
        ins, lrefs = refs[:n], refs[n:2 * n]
        send, recv = refs[2 * n], refs[2 * n + 1]
        x, y, c, _ = _me()
        for k in range(n):
            for d in range(1, NDEV):
                dev, _ = _peer(x, y, c, d)
                src = ins[k].at[0] if scatter else ins[k]
                cp = pltpu.make_async_remote_copy(src_ref=src, dst_ref=lrefs[k].at[0],
                                                  send_sem=send.at[k * NPEER + d - 1],
                                                  recv_sem=recv.at[k * NPEER + d - 1],
                                                  device_id=dev, device_id_type=pl.DeviceIdType.MESH)
                cp.wait_send()
                cp.wait_recv()

    arrs = handle["src"] + handle["land"]
    outs = pl.pallas_call(
        body, name=name,
        out_shape=tuple(pltpu.HBM(a.shape, a.dtype) for a in arrs),
        in_specs=[_HBM] * (2 * n) + [_SEM, _SEM, pl.BlockSpec(memory_space=pl.ANY)],
        out_specs=tuple([_HBM] * (2 * n)),
        input_output_aliases={i: i for i in range(2 * n)},
        compiler_params=pltpu.CompilerParams(has_side_effects=_EFFECT),
    )(*arrs, handle["send"], handle["recv"], after)
    me = 4 * lax.axis_index("x") + 2 * lax.axis_index("y") + lax.axis_index("c")
    landed = []
    for own, land in zip(outs[:n], outs[n:]):
        mine = lax.dynamic_index_in_dim(own, me, 0, keepdims=True) if scatter else own[None]
        landed.append(lax.dynamic_update_slice_in_dim(land, mine, me, 0))
    return landed


def _cat_from_slabs(slabs):
    _, k, n = slabs.shape

    def cols(lo, hi):
        parts, c = [], lo
        while c < hi:
            j = c // n
            e = min(hi, (j + 1) * n)
            parts.append(slabs[j][:, c - j * n:e - j * n])
            c = e
        return parts

    def zeros(w):
        return [jnp.zeros((k, w), slabs.dtype)]

    return jnp.concatenate(cols(0, 4096) + cols(4112, 9232) + cols(4096, 4104) + zeros(LANE - AH)
                           + cols(4104, 4112) + zeros(NCAT - C_BA - LANE - AH), axis=1)


IN_PIECES = (("pre", C_QKVA, 3072), ("z", C_Z, 1024), ("qb", C_QKVB, 1024), ("kb", C_QKVB + 1024, 1024),
             ("vb", C_QKVB + 2048, 1024), ("ga", C_GATE, 1024), ("gb", C_GATE + 1024, 1024))
_ORIG_SEGS = ((0, 3072, "pre", 0), (3072, 4096, "z", 0), (4096, 4104, "ba", 0), (4104, 4112, "ba", LANE),
              (4112, 5136, "qb", 0), (5136, 6160, "kb", 0), (6160, 7184, "vb", 0), (7184, 8208, "ga", 0),
              (8208, 9232, "gb", 0))


def _orig_cols_from_pieces(gp, lo, hi):
    parts = []
    for a, b, name, off in _ORIG_SEGS:
        s, e = max(a, lo), min(b, hi)
        if s < e:
            parts.append(gp[name][:, off + s - a:off + e - a])
    return parts[0] if len(parts) == 1 else jnp.concatenate(parts, axis=1)


def _pad128(v):
    return jnp.pad(v, ((0, 0), (0, 128 - v.shape[1])))


def local_step(x, tgt, mod, wts, small, late_weights=None, on_grads=None):
    if on_grads is None:
        on_grads = lambda group, gd: jnp.zeros((), F32)
    t = x.shape[0]
    nc = t // CH
    shift_t, scale_t, gate_t, shift_f, scale_f, gate_f = mod
    wcat = _cat_from_slabs(wts["w_in_slabs"])
    a_log = _pad128(small["a_log"])
    dtb = _pad128(small["dt_bias"])
    vecs = dict(bga=small["b_gate"][:, :D], bgb=small["b_gate"][:, D:], gate_t=gate_t, g1=small["ln1_g"],
                b1=small["ln1_b"], scale_f=scale_f, shift_f=shift_f)

    h1 = modulate(x, scale_t, shift_t, "modulate_t")
    proj = matmul(h1, wcat, F32, "in_proj")
    q, k, v, gcs, beta = prep_fwd(proj, small["conv_a"], a_log, dtb)

    u, w, qg, kd, qk, eg, tinv = c1_fwd(q, k, v, gcs, beta)
    oa, sall = c2_fwd(u, w, qg, kd, qk, eg, proj, small["norm_a"])
    bias = bias_table(small["rel_bias"])
    ob, probs = attn_fwd(proj, bias)
    if late_weights is not None:
        wts = {**wts, **late_weights(ob)}
    y1, h2 = merge_fwd(x, oa, ob, proj, vecs, wts["w_a"], wts["w_b"], wts["w_o"])
    up = matmul(h2, wts["w_up"], F32, "up_proj")
    a = ffn_act_fwd(up, small["conv_ffn"], small["b_conv_ffn"])

    da, dy1_res, dffn, dgate_f, dg2, db2, loss = head_fwd_bwd(a, y1, tgt, gate_f, small["ln2_g"], small["ln2_b"],
                                                            wts["w_down"])
    g_w_down = matmul(a, dffn, BF16, "wgrad_down", ta=True)
    dup, g_conv_ffn, g_bconv = ffn_act_bwd(up, small["conv_ffn"], small["b_conv_ffn"], da)
    g_w_up = matmul(h2, dup, BF16, "wgrad_up", ta=True)
    tok = on_grads("ffn", dict(w_up=g_w_up, w_down=g_w_down))
    dy1, dscale_f, dshift_f = dgrad_modulated(dup, wts["w_up"], y1, dy1_res, scale_f + tok, "dgrad_up")
    (dx_res, doa, dob, dga, dgb, merged, dmix, dpa, dpb,
     dbga, dbgb, dgate_t, dg1, db1) = merge_bwd(x, oa, ob, proj, vecs, wts["w_a"], wts["w_b"], wts["w_o"], dy1)
    g_w_o = matmul(merged, dmix, BF16, "wgrad_o", ta=True)
    g_w_a = matmul(oa, dpa, BF16, "wgrad_a", ta=True)
    g_w_b = matmul(ob, dpb, BF16, "wgrad_b", ta=True)
    tok = on_grads("mix", dict(w_o=g_w_o, w_a=g_w_a, w_b=g_w_b))
    dqb, dkb, dvb, dbias = attn_bwd(proj, ob, probs, dob)
    g_rel = relbias_reduce(bias_table_bwd_layout(dbias))
    du, dw, dqg, dkd, dqk, deg, dz, g_norm = c2_bwd(u, w, qg, kd, qk, eg, proj, small["norm_a"] + tok, sall, doa)
    dq, dk, dv, dgcs, dbeta = c1_bwd(q, k, v, gcs, beta, tinv, du, dw, dqg, dkd, dqk, deg)
    dpre, dbb, daa, g_conv_a, g_alog, g_dtb = prep_bwd(proj, small["conv_a"], a_log, dtb, dq, dk, dv, dgcs, dbeta)
    tok = on_grads("small", dict(conv_a=g_conv_a, rel_bias=g_rel, conv_ffn=g_conv_ffn))
    dba = jnp.concatenate([dbb, daa, jnp.zeros((t, NCAT - C_BA - 2 * LANE), BF16)], axis=1) + tok.astype(BF16)
    dpieces = dict(pre=dpre, z=dz, qb=dqb, kb=dkb, vb=dvb, ga=dga, gb=dgb)
    g_in = {n: matmul(h1, dpieces[n], BF16, "wgrad_in_" + n, ta=True) for n, _, _ in IN_PIECES}
    g_in["ba"] = matmul(h1, dba, BF16, "wgrad_in_ba", ta=True)
    tok = on_grads("in", g_in)
    dh1 = dgrad_pieces([(dpieces[n], off) for n, off, _ in IN_PIECES], dba + tok.astype(BF16), wcat,
                       "dgrad_in")
    grad_x, dscale_t, dshift_t = modulate_bwd(dh1, x, dx_res, scale_t + tok, "modulate_t_bwd")

    dmod = (dshift_t, dscale_t, dgate_t, dshift_f, dscale_f, dgate_f)
    grads = dict(w_in=_orig_cols_from_pieces(g_in, 0, 9232), w_up=g_w_up, w_down=g_w_down, w_a=g_w_a, w_b=g_w_b, w_o=g_w_o,
                 conv_a=g_conv_a, rel_bias=g_rel, conv_ffn=g_conv_ffn,
                 b_gate=jnp.concatenate([dbga, dbgb], axis=1), a_log=g_alog[:, :AH], dt_bias=g_dtb[:, :AH],
                 norm_a=g_norm, ln1_g=dg1, ln1_b=db1, b_conv_ffn=g_bconv, ln2_g=dg2, ln2_b=db2)
    return loss[0, 0], grad_x, dmod, grads


REP_NAMES = ["b_ada", "b_gate", "a_log", "dt_bias", "norm_a", "ln1_g", "ln1_b", "b_conv_ffn", "ln2_g", "ln2_b"]
SH_NAMES = ["conv_a", "rel_bias", "conv_ffn"]


def _col_shards(a, n):
    return a.reshape(a.shape[0], NDEV, n).transpose(1, 0, 2)


def kernel(x, c, w_ada, b_ada, w_in, b_gate, conv_a, a_log, dt_bias, norm_a, rel_bias, w_branch_a, w_branch_b, w_o, ln1_g, ln1_b, w_up, conv_ffn, b_conv_ffn, w_down, ln2_g, ln2_b, loss_target, m_w_ada, m_b_ada, m_w_in, m_b_gate, m_conv_a, m_a_log, m_dt_bias, m_norm_a, m_rel_bias, m_w_branch_a, m_w_branch_b, m_w_o, m_ln1_g, m_ln1_b, m_w_up, m_conv_ffn, m_b_conv_ffn, m_w_down, m_ln2_g, m_ln2_b, v_w_ada, v_b_ada, v_w_in, v_b_gate, v_conv_a, v_a_log, v_dt_bias, v_norm_a, v_rel_bias, v_w_branch_a, v_w_branch_b, v_w_o, v_ln1_g, v_ln1_b, v_w_up, v_conv_ffn, v_b_conv_ffn, v_w_down, v_ln2_g, v_ln2_b):
    W = dict(w_ada=w_ada, b_ada=b_ada, w_in=w_in, b_gate=b_gate, conv_a=conv_a, a_log=a_log, dt_bias=dt_bias,
             norm_a=norm_a, rel_bias=rel_bias, w_branch_a=w_branch_a, w_branch_b=w_branch_b, w_o=w_o, ln1_g=ln1_g,
             ln1_b=ln1_b, w_up=w_up, conv_ffn=conv_ffn, b_conv_ffn=b_conv_ffn, w_down=w_down, ln2_g=ln2_g,
             ln2_b=ln2_b)
    M = dict(w_ada=m_w_ada, b_ada=m_b_ada, w_in=m_w_in, b_gate=m_b_gate, conv_a=m_conv_a, a_log=m_a_log,
             dt_bias=m_dt_bias, norm_a=m_norm_a, rel_bias=m_rel_bias, w_branch_a=m_w_branch_a,
             w_branch_b=m_w_branch_b, w_o=m_w_o, ln1_g=m_ln1_g, ln1_b=m_ln1_b, w_up=m_w_up, conv_ffn=m_conv_ffn,
             b_conv_ffn=m_b_conv_ffn, w_down=m_w_down, ln2_g=m_ln2_g, ln2_b=m_ln2_b)
    V = dict(w_ada=v_w_ada, b_ada=v_b_ada, w_in=v_w_in, b_gate=v_b_gate, conv_a=v_conv_a, a_log=v_a_log,
             dt_bias=v_dt_bias, norm_a=v_norm_a, rel_bias=v_rel_bias, w_branch_a=v_w_branch_a,
             w_branch_b=v_w_branch_b, w_o=v_w_o, ln1_g=v_ln1_g, ln1_b=v_ln1_b, w_up=v_w_up, conv_ffn=v_conv_ffn,
             b_conv_ffn=v_b_conv_ffn, w_down=v_w_down, ln2_g=v_ln2_g, ln2_b=v_ln2_b)
    W3, M3, V3 = W, M, V
    W, M, V = ({n: a[0] for n, a in dct.items()} for dct in (W, M, V))
    me = 4 * lax.axis_index("x") + 2 * lax.axis_index("y") + lax.axis_index("c")
    big = ("w_in", "w_up", "w_down", "w_branch_a", "w_branch_b", "w_o")

    g_in = all_gather_two_level(W["w_in"].astype(BF16), "gather_w_in")
    wts = dict(w_in_slabs=g_in)
    c_all, *sh_all = all_gather([c] + [W[n] for n in SH_NAMES], "gather_small")
    c_all = c_all.reshape(NDEV, D)

    def full_small(g8):
        return g8.transpose(1, 0, 2).reshape(g8.shape[1], -1)

    small = dict(conv_a=full_small(sh_all[0]), rel_bias=full_small(sh_all[1]), conv_ffn=full_small(sh_all[2]),
                 b_gate=W["b_gate"][None], a_log=W["a_log"][None], dt_bias=W["dt_bias"][None],
                 norm_a=W["norm_a"][None], ln1_g=W["ln1_g"][None], ln1_b=W["ln1_b"][None],
                 b_conv_ffn=W["b_conv_ffn"][None], ln2_g=W["ln2_g"][None], ln2_b=W["ln2_b"][None])

    nsh = w_ada.shape[2]
    b_sh = lax.dynamic_slice(W["b_ada"][None], (0, me * nsh), (1, nsh))
    mod_sh = ada_fwd(c_all, W["w_ada"], b_sh)
    (mod_rows,) = all_to_all([mod_sh[:, None, :]], "scatter_mod")
    mod6 = mod_rows.reshape(6, D)

    after_small = (g_in[0, 0, 0].astype(F32) * 0.0 + mod6[0, 0] * 0.0).astype(BF16)
    late, late_tok = exchange_start([W[n].astype(BF16) + after_small for n in big[1:]], "gather_late_start", False)

    def late_weights(after):
        g_up, g_down, g_a, g_b, g_o = exchange_wait(late, after, "gather_late_wait")
        return dict(w_up=g_up.transpose(1, 0, 2).reshape(D, -1), w_down=g_down.reshape(DFF, D),
                    w_a=g_a.reshape(D, D), w_b=g_b.reshape(D, D), w_o=g_o.reshape(D, D))

    mod6 = mod6 + late_tok
    mod = tuple(mod6[i:i + 1] for i in range(6))

    pending = {}

    def on_grads(group, gd):
        if group == "small":
            pending["small"] = all_to_all([_col_shards(gd[n], W[n].shape[1]) for n in SH_NAMES],
                                          "scatter_small_grads")
            return pending["small"][0][0, 0, 0] * 0.0
        if group == "ffn":
            slabs = [_col_shards(gd["w_up"], w_up.shape[2]), gd["w_down"].reshape(NDEV, -1, D)]
        elif group == "mix":
            slabs = [gd[n].reshape(NDEV, -1, D) for n in ("w_a", "w_b", "w_o")]
        else:
            nin = w_in.shape[2]
            slabs = [jnp.stack([_orig_cols_from_pieces(gd, j * nin, (j + 1) * nin) for j in range(NDEV)], axis=0)]
        pending[group], tok = exchange_start([s.astype(BF16) for s in slabs], "scatter_" + group + "_start", True)
        return tok

    loss, grad_x, dmod, g = local_step(x[0], loss_target[0], mod, wts, small, late_weights, on_grads)

    rep_grads = {n: g[n] for n in REP_NAMES if n != "b_ada"}
    rep_grads["b_ada"] = jnp.concatenate(dmod, axis=1)
    gathered = all_gather([rep_grads[n] for n in REP_NAMES] + [jnp.broadcast_to(loss, (1, LANE))],
                          "gather_small_grads")
    rep_all = dict(zip(REP_NAMES, gathered))
    sh_recv = [p[:, None] for p in pending["small"]]
    small_names = REP_NAMES + SH_NAMES
    sg, sd, sm, sv, loss_row = adamw_small([rep_all[n] for n in REP_NAMES] + sh_recv,
                                           [W3[n] for n in small_names], [M3[n] for n in small_names],
                                           [V3[n] for n in small_names], gathered[-1], "adamw_small")
    loss_total = loss_row[0, 0]

    dmod_all = rep_all["b_ada"][:, 0]
    dmod_sh = lax.dynamic_slice(dmod_all, (0, me * nsh), (NDEV, nsh))
    g_w_ada = ada_wgrad(c_all.T, dmod_sh)

    p_up, p_down = exchange_wait(pending["ffn"], grad_x, "scatter_ffn_wait")
    p_a, p_b, p_o = exchange_wait(pending["mix"], grad_x, "scatter_mix_wait")
    (p_in,) = exchange_wait(pending["in"], grad_x, "scatter_in_wait")
    parts = [p_in, p_up, p_down, p_a, p_b, p_o]

    res = {}
    for n, p in zip(big, parts):
        res[n] = adamw(p, W3[n], M3[n], V3[n], "adamw_" + n)
    res["w_ada"] = adamw(g_w_ada[None], W3["w_ada"], M3["w_ada"], V3["w_ada"], "adamw_w_ada")
    for i, n in enumerate(small_names):
        res[n] = (sg[i], sd[i], sm[i], sv[i])

    order = ("w_ada", "b_ada", "w_in", "b_gate", "conv_a", "a_log", "dt_bias", "norm_a", "rel_bias", "w_branch_a",
             "w_branch_b", "w_o", "ln1_g", "ln1_b", "w_up", "conv_ffn", "b_conv_ffn", "w_down", "ln2_g", "ln2_b")
    outs = [loss_total, grad_x[None]]
    for kind in range(4):
        outs += [res[n][kind] for n in order]
    return tuple(outs)
```

```python
import functools
import math

import numpy as np
import jax
import jax.numpy as jnp
from jax import lax
from jax.experimental import pallas as pl
from jax.experimental.pallas import tpu as pltpu

F32 = jnp.float32
BF16 = jnp.bfloat16
HI = lax.Precision.HIGHEST

D = 1024
CH = 64
AH, ADK = 8, 128
BH, BDH = 16, 64
BPREV = 8
BMAXREL = 256
RELSZ = CH + BMAXREL
DFF = 2816
ALPHA = 2.0 ** 0.25
LN_EPS, RMS_EPS, L2_EPS = 1e-5, 1e-6, 1e-6
NEG = -1e30
LR, B1, B2, AEPS, WD, STEP = 1e-3, 0.9, 0.999, 1e-8, 0.01, 10
NDEV = 8
HALO = 8
LANE = 128
TQ = 512
VMEM_LIMIT = 56 * 1024 * 1024

C_QKVA, C_Z, C_QKVB, C_GATE, C_BA, NCAT = 0, 3072, 4096, 7168, 9216, 9728


def _cparams(n_axes=1, vmem=VMEM_LIMIT):
    return pltpu.CompilerParams(dimension_semantics=("arbitrary",) * n_axes, vmem_limit_bytes=vmem)


def _dg(a, b, ca, cb):
    return lax.dot_general(a.astype(BF16), b.astype(BF16), (((ca,), (cb,)), ((), ())),
                           preferred_element_type=F32)


@jax.custom_vjp
def mm_nn(a, b):
    return _dg(a, b, 1, 0)


@jax.custom_vjp
def mm_nt(a, b):
    return _dg(a, b, 1, 1)


@jax.custom_vjp
def mm_tn(a, b):
    return _dg(a, b, 0, 0)


mm_nn.defvjp(lambda a, b: (mm_nn(a, b), (a, b)),
             lambda r, g: (mm_nt(g, r[1]).astype(r[0].dtype), mm_tn(r[0], g).astype(r[1].dtype)))
mm_nt.defvjp(lambda a, b: (mm_nt(a, b), (a, b)),
             lambda r, g: (mm_nn(g, r[1]).astype(r[0].dtype), mm_tn(g, r[0]).astype(r[1].dtype)))
mm_tn.defvjp(lambda a, b: (mm_tn(a, b), (a, b)),
             lambda r, g: (mm_nt(r[1], g).astype(r[0].dtype), mm_nn(r[0], g).astype(r[1].dtype)))


@jax.custom_vjp
def mm_w(a, w):
    return _dg(a, w, 1, 0)


mm_w.defvjp(lambda a, w: (mm_w(a, w), (a, w)),
            lambda r, g: (mm_nt(g, r[1]).astype(r[0].dtype), jnp.zeros_like(r[1])))


def _mmh(a, b):
    return lax.dot_general(a, b, (((1,), (0,)), ((), ())), precision=HI, preferred_element_type=F32)


def _bdg(a, b, ca, cb):
    return lax.dot_general(a.astype(BF16), b.astype(BF16), (((ca,), (cb,)), ((0,), (0,))),
                           preferred_element_type=F32)


@jax.custom_vjp
def bmm_nn(a, b):
    return _bdg(a, b, 2, 1)


@jax.custom_vjp
def bmm_nt(a, b):
    return _bdg(a, b, 2, 2)


@jax.custom_vjp
def bmm_tn(a, b):
    return _bdg(a, b, 1, 1)


bmm_nn.defvjp(lambda a, b: (bmm_nn(a, b), (a, b)), lambda r, g: (bmm_nt(g, r[1]), bmm_tn(r[0], g)))
bmm_nt.defvjp(lambda a, b: (bmm_nt(a, b), (a, b)), lambda r, g: (bmm_nn(g, r[1]), bmm_tn(g, r[0])))
bmm_tn.defvjp(lambda a, b: (bmm_tn(a, b), (a, b)), lambda r, g: (bmm_nt(r[1], g), bmm_nn(r[0], g)))


def _bdg3(a, b, ca, cb):
    return lax.dot_general(a, b, (((ca,), (cb,)), ((0,), (0,))), precision=lax.Precision.HIGH,
                           preferred_element_type=F32)


NEWTON_STEPS = 2


def _bdgp(a, b, ca, cb):
    return _bdg(a, b, ca, cb)


@jax.custom_vjp
def bmm3_nn(a, b):
    return _bdgp(a, b, 2, 1)


bmm3_nn.defvjp(lambda a, b: (bmm3_nn(a, b), (a, b)),
               lambda r, g: (_bdgp(g, r[1], 2, 2), _bdgp(r[0], g, 1, 1)))


def _sigmoid(x):
    return 0.5 * jnp.tanh(0.5 * x) + 0.5


def _silu(x):
    return x * _sigmoid(x)


def _softplus(x):
    return jnp.maximum(x, 0.0) + jnp.log(1.0 + jnp.exp(-jnp.abs(x)))


def _layernorm(r, g, b):
    mu = jnp.mean(r, axis=-1, keepdims=True)
    xc = r - mu
    var = jnp.mean(xc * xc, axis=-1, keepdims=True)
    return xc * lax.rsqrt(var + LN_EPS) * g + b


def _iota2(shape, dim):
    return lax.broadcasted_iota(jnp.int32, shape, dim)


@jax.custom_vjp
def causal_conv(ext, rows):
    k = len(rows)
    y = None
    for j in range(k):
        s = k - 1 - j
        r = pltpu.roll(ext, s, 0) if s else ext
        t = r[HALO:] * rows[j]
        y = t if y is None else y + t
    return y


def _causal_conv_fwd(ext, rows):
    return causal_conv(ext, rows), (ext, rows)


def _causal_conv_bwd(res, g):
    ext, rows = res
    n = ext.shape[0]
    k = len(rows)
    gext = jnp.concatenate([jnp.zeros((HALO, g.shape[1]), g.dtype), g], axis=0)
    dext = None
    drows = []
    for j in range(k):
        s = k - 1 - j
        up = pltpu.roll(gext, n - s, 0) if s else gext
        t = up * rows[j]
        dext = t if dext is None else dext + t
        r = pltpu.roll(ext, s, 0) if s else ext
        drows.append(jnp.sum(g * r[HALO:], axis=0, keepdims=True))
    return dext, tuple(drows)


causal_conv.defvjp(_causal_conv_fwd, _causal_conv_bwd)


def _chunk_masks(tm):
    i = _iota2((tm, tm), 0)
    j = _iota2((tm, tm), 1)
    same = (i ^ j) < CH
    lower = jnp.where(same & (j <= i), 1.0, 0.0).astype(F32)
    upper = jnp.where(same & (i <= j), 1.0, 0.0).astype(F32)
    return lower, upper


@jax.custom_vjp
def chunk_cumsum(g):
    lower, _ = _chunk_masks(g.shape[0])
    return _mmh(lower, g)


def _chunk_cumsum_bwd(_, ct):
    _, upper = _chunk_masks(ct.shape[0])
    return (_mmh(upper, ct),)


chunk_cumsum.defvjp(lambda g: (chunk_cumsum(g), None), _chunk_cumsum_bwd)


@jax.custom_vjp
def inv_unit_lower(a):
    n = a.shape[-1]
    eye = jnp.where(_iota2((1, n, n), 1) == _iota2((1, n, n), 2), 1.0, 0.0).astype(F32)
    x = eye - a
    p = _bdg3(a, a, 2, 1)
    steps = int(math.log2(n)) - 1
    for s in range(steps):
        x = x + _bdg3(x, p, 2, 1)
        if s + 1 < steps:
            p = _bdg3(p, p, 2, 1)
    for _ in range(NEWTON_STEPS):
        r = (eye - x) - _bdg3(a, x, 2, 1)
        x = x + _bdg3(x, r, 2, 1)
    return x


def _inv_fwd(a):
    t = inv_unit_lower(a)
    return t, t


def _inv_bwd(t, g):
    return (-_bdgp(_bdgp(t, g, 1, 1), t, 2, 2),)


inv_unit_lower.defvjp(_inv_fwd, _inv_bwd)


@jax.custom_vjp
def inv_known(a, t):
    return t


inv_known.defvjp(lambda a, t: (t, t), lambda t, g: (_inv_bwd(t, g)[0], jnp.zeros_like(t)))


def prep_head_fn(ext, rows, scale):
    s = _silu(causal_conv(ext, rows))
    if scale is None:
        return s
    return s * (lax.rsqrt(jnp.sum(s * s, axis=-1, keepdims=True) + L2_EPS) * scale)


def prep_gate_fn(bb, aa, a_log, dtb):
    g = -jnp.exp(a_log) * _softplus(aa + dtb)
    return chunk_cumsum(g), _sigmoid(bb)


PREP_SCALES = (ADK ** -0.5, 1.0, None)


def _head_cols(a, heads):
    lane = _iota2((1, LANE), 1)
    return jnp.concatenate([jnp.sum(jnp.where(lane == h, a, 0.0), axis=1, keepdims=True)[None]
                            for h in heads], axis=0)


def _head_rows(a, heads):
    at = a.T[:AH]
    sub = _iota2((AH, 1), 0)
    return jnp.concatenate([jnp.sum(jnp.where(sub == h, at, 0.0), axis=0, keepdims=True)[None]
                            for h in heads], axis=0)


def c1_heads(q, k, v, gcs, beta, tinv_saved=None):
    heads = range(AH)
    gcol = jnp.concatenate([_head_cols(g, heads) for g in gcs], axis=0)
    grow = jnp.concatenate([_head_rows(g, heads) for g in gcs], axis=0)
    bcol = jnp.concatenate([_head_cols(b, heads) for b in beta], axis=0)
    i = _iota2((1, CH, CH), 1)
    j = _iota2((1, CH, CH), 2)
    causal = j <= i
    strict = j < i
    diff = gcol - grow
    decay = jnp.where(causal, jnp.exp(jnp.where(causal, diff, 0.0)), 0.0)
    kb = k * bcol
    vb = v * bcol
    a_low = jnp.where(strict, bmm_nt(kb, k) * decay, 0.0)
    tinv = inv_unit_lower(a_low) if tinv_saved is None else inv_known(a_low, tinv_saved)
    egc = jnp.exp(gcol)
    u = bmm3_nn(tinv, vb)
    w = bmm3_nn(tinv, kb * egc)
    qk = jnp.where(causal, bmm_nt(q, k) * decay, 0.0)
    glast = jnp.sum(jnp.where(_iota2((1, CH, 1), 1) == CH - 1, gcol, 0.0), axis=1, keepdims=True)
    qg = q * egc
    kd = k * jnp.exp(glast - gcol)
    eg = jnp.exp(glast) * jnp.ones((1, 1, ADK), F32)
    return u, w, qk, qg, kd, eg, tinv


def c2_heads(s, u, w, qk, qg, kd, eg, z, nw):
    vn = u - bmm_nn(w, s)
    o = bmm_nn(qg, s) + bmm_nn(qk, vn)
    s2 = s * eg + bmm_tn(kd, vn)
    ms = jnp.mean(o * o, axis=-1, keepdims=True)
    og = o * lax.rsqrt(ms + RMS_EPS) * nw * _silu(z)
    return og, s2


ATT_SCALE = BDH ** -0.5


def _head_mask(hh):
    lane = _iota2((1, 2 * BDH), 1)
    return jnp.where((lane >= hh * BDH) & (lane < (hh + 1) * BDH), 1.0, 0.0).astype(F32)


def attn_sub_fwd(q, k, v, bias2, r, firstf):
    col = _iota2((1, KWIN), 1) + r * SUBQ
    nokey = jnp.where(col < TQ, firstf, 0.0) * NEG
    out, probs = None, []
    for hh in range(2):
        hm = _head_mask(hh)
        s = mm_nt(q * (hm * ATT_SCALE), k) + (assemble_bias(bias2[hh], r) + nokey)
        p = jnp.exp(s - jnp.max(s, axis=-1, keepdims=True))
        inv = 1.0 / jnp.sum(p, axis=-1, keepdims=True)
        o = mm_nn(p, v) * (inv * hm)
        out = o if out is None else out + o
        probs.append(p * inv)
    return out, probs


def attn_sub_bwd(q, k, v, o, do, probs, r):
    dq, dk, dv, dss = None, None, None, []
    for hh in range(2):
        hm = _head_mask(hh)
        p = probs[hh]
        doh = do * hm
        ds = p * (mm_nt(doh, v) - jnp.sum(doh * o, axis=-1, keepdims=True))
        dqh = mm_nn(ds, k) * (hm * ATT_SCALE)
        dkh = mm_tn(ds, q * (hm * ATT_SCALE))
        dvh = mm_tn(p, doh)
        dq = dqh if dq is None else dq + dqh
        dk = dkh if dk is None else dk + dkh
        dv = dvh if dv is None else dv + dvh
        dss.append(ds)
    return dq, dk, dv, dss


def merge_fn(x, oa, ob, gra, grb, p_pa, p_pb, p_mix, bga, bgb, gate_t, g1, b1, scale_f, shift_f,
             wa, wb, wo):
    ga = _sigmoid(gra + bga)
    gb = _sigmoid(grb + bgb)
    pa = mm_w(oa, wa) + p_pa
    pb = mm_w(ob, wb) + p_pb
    merged = ga * pa + gb * pb
    mix = mm_w(merged, wo) + p_mix
    y1 = _layernorm(ALPHA * x + gate_t * mix, g1, b1)
    return y1, merged


def ffn_act_fn(extg, extv, rows_g, rows_v, bg, bv):
    return _silu(causal_conv(extg, rows_g) + bg) * (causal_conv(extv, rows_v) + bv)


def head_fn(a, y1, p_ffn, gate_f, g2, b2, tgt, wd):
    ffn = mm_w(a, wd) + p_ffn
    y2 = _layernorm(ALPHA * y1 + gate_f * ffn, g2, b2)
    err = y2 - tgt
    return 0.5 * jnp.sum(jnp.mean(err * err, axis=-1, keepdims=True))


def _rows(tm, width, colblk=0, order=None):
    if order is None:
        return pl.BlockSpec((tm, width), lambda i: (i, colblk))
    return pl.BlockSpec((tm, width), lambda i: (order(i), colblk))


def _const(shape):
    nd = len(shape)
    return pl.BlockSpec(shape, lambda *_: (0,) * nd)


def _pick(n, cands):
    for c in cands:
        if n % c == 0:
            return c
    raise ValueError(f"no tile for {n}")


def _tile(n, cap):
    best = None
    for c in range(LANE, min(n, cap) + 1, LANE):
        if n % c == 0:
            best = c
    if best is None:
        raise ValueError(f"no tile for {n}")
    return best


def _onehot_rows(k, j):
    return jnp.where(_iota2((k, 1), 0) == j, 1.0, 0.0).astype(F32)


def _stack_rows(drows):
    k = len(drows)
    out = None
    for j in range(k):
        tj = _onehot_rows(k, j) * drows[j]
        out = tj if out is None else out + tj
    return out


def matmul(a, w, out_dtype, name, ta=False, tb=False):
    kdim, m = a.shape if ta else a.shape[::-1]
    n = w.shape[0] if tb else w.shape[1]
    tm = _tile(m, 2048 if kdim <= 1024 else 1024)
    tn = _tile(n, 1024)
    tk = _tile(kdim, 2560)
    nk = kdim // tk
    a_spec = (pl.BlockSpec((tk, tm), lambda i, j, k: (k, i)) if ta
              else pl.BlockSpec((tm, tk), lambda i, j, k: (i, k)))
    w_spec = (pl.BlockSpec((tn, tk), lambda i, j, k: (j, k)) if tb
              else pl.BlockSpec((tk, tn), lambda i, j, k: (k, j)))

    def body(a_ref, w_ref, o_ref, *scratch):
        p = _dg(a_ref[...], w_ref[...], 0 if ta else 1, 1 if tb else 0)
        if nk == 1:
            o_ref[...] = p.astype(out_dtype)
            return
        acc = scratch[0]
        k = pl.program_id(2)

        @pl.when(k == 0)
        def _():
            acc[...] = p

        @pl.when(k > 0)
        def _():
            acc[...] += p

        @pl.when(k == nk - 1)
        def _():
            o_ref[...] = acc[...].astype(out_dtype)

    return pl.pallas_call(
        body, name=name,
        grid=(m // tm, n // tn, nk),
        in_specs=[a_spec, w_spec],
        out_specs=pl.BlockSpec((tm, tn), lambda i, j, k: (i, j)),
        out_shape=jax.ShapeDtypeStruct((m, n), out_dtype),
        scratch_shapes=[] if nk == 1 else [pltpu.VMEM((tm, tn), F32)],
        compiler_params=_cparams(3),
    )(a, w)


def dgrad_modulated(a, w, xin, dres, scale, name):
    m, kdim = a.shape
    n = w.shape[0]
    tm = _tile(m, 1024)
    tk = _tile(kdim, 2560)
    nk = kdim // tk
    assert nk > 1

    def body(a_ref, w_ref, x_ref, r_ref, sc_ref, o_ref, dsc_ref, dsh_ref, acc):
        i = pl.program_id(0)
        k = pl.program_id(1)
        p = _dg(a_ref[...], w_ref[...], 1, 1)

        @pl.when(k == 0)
        def _():
            acc[...] = p

        @pl.when(k > 0)
        def _():
            acc[...] += p

        @pl.when((i == 0) & (k == 0))
        def _():
            dsc_ref[...] = jnp.zeros_like(dsc_ref)
            dsh_ref[...] = jnp.zeros_like(dsh_ref)

        @pl.when(k == nk - 1)
        def _():
            dh = acc[...]
            o_ref[...] = r_ref[...] + dh * (1.0 + sc_ref[...])
            dsc_ref[...] += jnp.sum(dh * x_ref[...], axis=0, keepdims=True)
            dsh_ref[...] += jnp.sum(dh, axis=0, keepdims=True)

    row = pl.BlockSpec((tm, n), lambda i, k: (i, 0))
    vec = pl.BlockSpec((1, n), lambda i, k: (0, 0))
    return pl.pallas_call(
        body, name=name, grid=(m // tm, nk),
        in_specs=[pl.BlockSpec((tm, tk), lambda i, k: (i, k)), pl.BlockSpec((n, tk), lambda i, k: (0, k)),
                  row, row, vec],
        out_specs=[row, vec, vec],
        out_shape=[jax.ShapeDtypeStruct((m, n), F32), jax.ShapeDtypeStruct((1, n), F32),
                   jax.ShapeDtypeStruct((1, n), F32)],
        scratch_shapes=[pltpu.VMEM((tm, n), F32)],
        compiler_params=_cparams(2),
    )(a, w, xin, dres, scale)


def dgrad_pieces(pieces, tail, w, name):
    m = pieces[0][0].shape[0]
    n, ktot = w.shape
    tk = 1024
    tm = _tile(m, 1024)
    wt = tail.shape[1]
    ranges, k0 = [], 0
    for arr, off in pieces:
        assert off == k0 * tk and arr.shape[1] % tk == 0
        ranges.append((k0, k0 + arr.shape[1] // tk))
        k0 = ranges[-1][1]
    nk = k0
    npc = len(pieces)

    def body(*refs):
        a_refs, t_ref, w_ref, wt_ref, o_ref, acc = refs[:npc], refs[npc], refs[npc + 1], refs[npc + 2], refs[npc + 3], refs[npc + 4]
        k = pl.program_id(1)

        @pl.when(k == 0)
        def _():
            acc[...] = _dg(t_ref[...], wt_ref[...], 1, 1)

        for a_ref, (lo, hi) in zip(a_refs, ranges):
            @pl.when((k >= lo) & (k < hi))
            def _(a_ref=a_ref):
                acc[...] += _dg(a_ref[...], w_ref[...], 1, 1)

        @pl.when(k == nk - 1)
        def _():
            o_ref[...] = acc[...]

    def piece_spec(lo, hi):
        return pl.BlockSpec((tm, tk), lambda i, k: (i, jnp.clip(k - lo, 0, hi - lo - 1)))

    return pl.pallas_call(
        body, name=name, grid=(m // tm, nk),
        in_specs=[piece_spec(lo, hi) for lo, hi in ranges] + [
            pl.BlockSpec((tm, wt), lambda i, k: (i, 0)),
            pl.BlockSpec((n, tk), lambda i, k: (0, k)),
            pl.BlockSpec((n, wt), lambda i, k: (0, (ktot - wt) // wt))],
        out_specs=pl.BlockSpec((tm, n), lambda i, k: (i, 0)),
        out_shape=jax.ShapeDtypeStruct((m, n), F32),
        scratch_shapes=[pltpu.VMEM((tm, n), F32)],
        compiler_params=_cparams(2),
    )(*[a for a, _ in pieces], tail, w, w)


def modulate(x, scale, shift, name):
    t, d = x.shape
    tm = _pick(t, (512, 256, 128))

    def body(x_ref, sc_ref, sh_ref, o_ref):
        o_ref[...] = (x_ref[...] * (1.0 + sc_ref[...]) + sh_ref[...]).astype(BF16)

    return pl.pallas_call(
        body, name=name, grid=(t // tm,),
        in_specs=[_rows(tm, d), _const((1, d)), _const((1, d))],
        out_specs=_rows(tm, d),
        out_shape=jax.ShapeDtypeStruct((t, d), BF16),
        compiler_params=_cparams(),
    )(x, scale, shift)


def modulate_bwd(dh, xin, dres, scale, name):
    t, d = dh.shape
    tm = _pick(t, (512, 256, 128))

    def body(dh_ref, x_ref, dr_ref, sc_ref, o_ref, dsc_ref, dsh_ref):
        i = pl.program_id(0)
        dh_v = dh_ref[...]
        o_ref[...] = dr_ref[...] + dh_v * (1.0 + sc_ref[...])

        @pl.when(i == 0)
        def _():
            dsc_ref[...] = jnp.zeros_like(dsc_ref)
            dsh_ref[...] = jnp.zeros_like(dsh_ref)

        dsc_ref[...] += jnp.sum(dh_v * x_ref[...], axis=0, keepdims=True)
        dsh_ref[...] += jnp.sum(dh_v, axis=0, keepdims=True)

    return pl.pallas_call(
        body, name=name, grid=(t // tm,),
        in_specs=[_rows(tm, d), _rows(tm, d), _rows(tm, d), _const((1, d))],
        out_specs=[_rows(tm, d), _const((1, d)), _const((1, d))],
        out_shape=[jax.ShapeDtypeStruct((t, d), F32), jax.ShapeDtypeStruct((1, d), F32),
                   jax.ShapeDtypeStruct((1, d), F32)],
        compiler_params=_cparams(),
    )(dh, xin, dres, scale)


PREP_TM = 128


def _halo_specs(tm, width, colblk, order):
    per = tm // HALO
    return [pl.BlockSpec((HALO, width), lambda i: (jnp.maximum(order(i) * per - 1, 0), colblk)),
            pl.BlockSpec((tm, width), lambda i: (order(i), colblk))]


def prep_fwd(proj, conv_a, a_log, dtb):
    t = proj.shape[0]
    tm = PREP_TM
    nt = t // tm
    wq = 3 * D

    def body(prev_ref, cur_ref, bb_ref, aa_ref, cw_ref, al_ref, dt_ref, q_ref, k_ref, v_ref, g_ref, b_ref):
        i = pl.program_id(0)
        flag = jnp.where(i > 0, 1.0, 0.0)
        for part, o_ref in enumerate((q_ref, k_ref, v_ref)):
            for h in range(AH):
                sl = slice(part * D + h * ADK, part * D + (h + 1) * ADK)
                ext = jnp.concatenate([prev_ref[:, sl] * flag, cur_ref[:, sl]], axis=0)
                rows = tuple(cw_ref[j:j + 1, sl] for j in range(4))
                o_ref[h] = prep_head_fn(ext, rows, PREP_SCALES[part])
        gcs, beta = prep_gate_fn(bb_ref[...], aa_ref[...], al_ref[...], dt_ref[...])
        g_ref[...] = gcs
        b_ref[...] = beta

    ident = lambda i: i
    hm = pl.BlockSpec((AH, tm, ADK), lambda i: (0, i, 0))
    return pl.pallas_call(
        body, name="prep_fwd", grid=(nt,),
        in_specs=_halo_specs(tm, wq, 0, ident) + [
            _rows(tm, 128, C_BA // 128), _rows(tm, 128, C_BA // 128 + 1),
            _const((4, wq)), _const((1, 128)), _const((1, 128))],
        out_specs=[hm, hm, hm, _rows(tm, 128), _rows(tm, 128)],
        out_shape=[jax.ShapeDtypeStruct((AH, t, ADK), F32)] * 3 + [jax.ShapeDtypeStruct((t, 128), F32)] * 2,
        compiler_params=_cparams(),
    )(proj, proj, proj, proj, conv_a, a_log, dtb)


def prep_bwd(proj, conv_a, a_log, dtb, dq, dk, dv, dgcs, dbeta):
    t = proj.shape[0]
    tm = PREP_TM
    nt = t // tm
    wq = 3 * D
    rev = lambda i: nt - 1 - i

    def body(prev_ref, cur_ref, bb_ref, aa_ref, cw_ref, al_ref, dt_ref,
             dq_ref, dk_ref, dv_ref, dg_ref, db_ref,
             dpre_ref, dbb_ref, daa_ref, dcw_ref, dal_ref, ddt_ref, carry):
        i = pl.program_id(0)
        flag = jnp.where(i < nt - 1, 1.0, 0.0)

        @pl.when(i == 0)
        def _():
            carry[...] = jnp.zeros_like(carry)
            dcw_ref[...] = jnp.zeros_like(dcw_ref)
            dal_ref[...] = jnp.zeros_like(dal_ref)
            ddt_ref[...] = jnp.zeros_like(ddt_ref)

        for part, d_ref in enumerate((dq_ref, dk_ref, dv_ref)):
            for h in range(AH):
                sl = slice(part * D + h * ADK, part * D + (h + 1) * ADK)
                ext = jnp.concatenate([prev_ref[:, sl] * flag, cur_ref[:, sl]], axis=0)
                rows = tuple(cw_ref[j:j + 1, sl] for j in range(4))
                _, vjp = jax.vjp(lambda e, r: prep_head_fn(e, r, PREP_SCALES[part]), ext, rows)
                dext, drows = vjp(d_ref[h])
                dcur = dext[HALO:]
                dpre_ref[:, sl] = jnp.concatenate([dcur[:tm - HALO], dcur[tm - HALO:] + carry[:, sl]],
                                                  axis=0).astype(BF16)
                carry[:, sl] = dext[:HALO]
                dcw_ref[:, sl] += _stack_rows(drows)
        _, vjp = jax.vjp(prep_gate_fn, bb_ref[...], aa_ref[...], al_ref[...], dt_ref[...])
        dbb, daa, dal, ddt = vjp((dg_ref[...], db_ref[...]))
        dbb_ref[...] = dbb.astype(BF16)
        daa_ref[...] = daa.astype(BF16)
        dal_ref[...] += dal
        ddt_ref[...] += ddt

    hm = pl.BlockSpec((AH, tm, ADK), lambda i: (0, rev(i), 0))
    return pl.pallas_call(
        body, name="prep_bwd", grid=(nt,),
        in_specs=_halo_specs(tm, wq, 0, rev) + [
            _rows(tm, 128, C_BA // 128, rev), _rows(tm, 128, C_BA // 128 + 1, rev),
            _const((4, wq)), _const((1, 128)), _const((1, 128)),
            hm, hm, hm, _rows(tm, 128, 0, rev), _rows(tm, 128, 0, rev)],
        out_specs=[_rows(tm, wq, 0, rev), _rows(tm, 128, 0, rev), _rows(tm, 128, 0, rev),
                   _const((4, wq)), _const((1, 128)), _const((1, 128))],
        out_shape=[jax.ShapeDtypeStruct((t, wq), BF16), jax.ShapeDtypeStruct((t, 128), BF16),
                   jax.ShapeDtypeStruct((t, 128), BF16), jax.ShapeDtypeStruct((4, wq), F32),
                   jax.ShapeDtypeStruct((1, 128), F32), jax.ShapeDtypeStruct((1, 128), F32)],
        scratch_shapes=[pltpu.VMEM((HALO, wq), F32)],
        compiler_params=_cparams(),
    )(proj, proj, proj, proj, conv_a, a_log, dtb, dq, dk, dv, dgcs, dbeta)


def _c1_specs(order, cps=1):
    hm = pl.BlockSpec((AH, cps * CH, ADK), lambda n: (0, order(n), 0))
    col = pl.BlockSpec((cps * CH, LANE), lambda n: (order(n), 0))
    qk = pl.BlockSpec((cps, AH, CH, CH), lambda n: (order(n), 0, 0, 0))
    eg = pl.BlockSpec((cps, AH, 1, ADK), lambda n: (order(n), 0, 0, 0))
    return hm, col, qk, eg


C1_CPS = 4


def _heads(ref):
    return jnp.stack([ref[:, h * ADK:(h + 1) * ADK] for h in range(AH)], axis=0)


def _chunks(ref):
    return jnp.concatenate([ref[:, c * CH:(c + 1) * CH] for c in range(C1_CPS)], axis=0)


def _chunk_rows(ref):
    return [ref[c * CH:(c + 1) * CH] for c in range(C1_CPS)]


def c1_fwd(q, k, v, gcs, beta):
    t = q.shape[1]
    nc = t // CH
    hm, col, qks, egs = _c1_specs(lambda n: n, C1_CPS)

    def body(q_ref, k_ref, v_ref, g_ref, b_ref, u_ref, w_ref, qg_ref, kd_ref, qk_ref, eg_ref, ti_ref):
        u, w, qk, qg, kd, eg, tinv = c1_heads(_chunks(q_ref), _chunks(k_ref), _chunks(v_ref), _chunk_rows(g_ref),
                                              _chunk_rows(b_ref))
        for c in range(C1_CPS):
            rows, hs = slice(c * CH, (c + 1) * CH), slice(c * AH, (c + 1) * AH)
            u_ref[:, rows] = u[hs]
            w_ref[:, rows] = w[hs].astype(BF16)
            qg_ref[:, rows] = qg[hs].astype(BF16)
            kd_ref[:, rows] = kd[hs].astype(BF16)
            qk_ref[c] = qk[hs].astype(BF16)
            eg_ref[c] = eg[hs]
            ti_ref[c] = tinv[hs]

    return pl.pallas_call(
        body, name="c1_fwd", grid=(nc // C1_CPS,),
        in_specs=[hm, hm, hm, col, col],
        out_specs=[hm, hm, hm, hm, qks, egs, qks],
        out_shape=[jax.ShapeDtypeStruct((AH, t, ADK), F32)] + [jax.ShapeDtypeStruct((AH, t, ADK), BF16)] * 3 + [
            jax.ShapeDtypeStruct((nc, AH, CH, CH), BF16), jax.ShapeDtypeStruct((nc, AH, 1, ADK), F32),
            jax.ShapeDtypeStruct((nc, AH, CH, CH), F32)],
        compiler_params=_cparams(),
    )(q, k, v, gcs, beta)


def c1_bwd(q, k, v, gcs, beta, tinv, du, dw, dqg, dkd, dqk, deg):
    t = q.shape[1]
    nc = t // CH
    hm, col, qks, egs = _c1_specs(lambda n: n, C1_CPS)

    def lead(ref):
        return jnp.concatenate([ref[c] for c in range(C1_CPS)], axis=0)

    def body(q_ref, k_ref, v_ref, g_ref, b_ref, ti_ref, du_ref, dw_ref, dqg_ref, dkd_ref, dqk_ref, deg_ref,
             dq_ref, dk_ref, dv_ref, dg_ref, db_ref):
        tinv = lead(ti_ref)
        _, vjp = jax.vjp(lambda q_, k_, v_, g_, b_: c1_heads(q_, k_, v_, g_, b_, tinv),
                         _chunks(q_ref), _chunks(k_ref), _chunks(v_ref), _chunk_rows(g_ref), _chunk_rows(b_ref))
        dq, dk, dv, dg, db = vjp((_chunks(du_ref).astype(F32), _chunks(dw_ref).astype(F32), lead(dqk_ref),
                                  _chunks(dqg_ref), _chunks(dkd_ref), lead(deg_ref),
                                  jnp.zeros((C1_CPS * AH, CH, CH), F32)))
        for c in range(C1_CPS):
            rows, hs = slice(c * CH, (c + 1) * CH), slice(c * AH, (c + 1) * AH)
            dq_ref[:, rows] = dq[hs]
            dk_ref[:, rows] = dk[hs]
            dv_ref[:, rows] = dv[hs]
            dg_ref[rows] = dg[c]
            db_ref[rows] = db[c]

    return pl.pallas_call(
        body, name="c1_bwd", grid=(nc // C1_CPS,),
        in_specs=[hm, hm, hm, col, col, qks, hm, hm, hm, hm, qks, egs],
        out_specs=[hm, hm, hm, col, col],
        out_shape=[jax.ShapeDtypeStruct((AH, t, ADK), F32)] * 3 + [jax.ShapeDtypeStruct((t, LANE), F32)] * 2,
        compiler_params=_cparams(),
    )(q, k, v, gcs, beta, tinv, du, dw, dqg, dkd, dqk, deg)


def c2_fwd(u, w, qg, kd, qk, eg, proj, norm_a):
    t = u.shape[1]
    nc = t // CH
    hm, _, qks, egs = _c1_specs(lambda n: n)
    tok = pl.BlockSpec((CH, D), lambda n: (n, 0))
    zspec = pl.BlockSpec((CH, D), lambda n: (n, C_Z // D))
    sspec = pl.BlockSpec((1, AH, ADK, ADK), lambda n: (n, 0, 0, 0))

    def body(u_ref, w_ref, qg_ref, kd_ref, qk_ref, eg_ref, z_ref, nw_ref, o_ref, sall_ref, st):
        n = pl.program_id(0)

        @pl.when(n == 0)
        def _():
            st[...] = jnp.zeros_like(st)

        s = st[...]
        sall_ref[0] = s
        og, s2 = c2_heads(s, u_ref[...], w_ref[...], qk_ref[0], qg_ref[...], kd_ref[...], eg_ref[0],
                          _heads(z_ref), nw_ref[...])
        st[...] = s2
        for h in range(AH):
            o_ref[:, h * ADK:(h + 1) * ADK] = og[h].astype(BF16)

    return pl.pallas_call(
        body, name="c2_fwd", grid=(nc,),
        in_specs=[hm, hm, hm, hm, qks, egs, zspec, _const((1, ADK))],
        out_specs=[tok, sspec],
        out_shape=[jax.ShapeDtypeStruct((t, D), BF16), jax.ShapeDtypeStruct((nc, AH, ADK, ADK), F32)],
        scratch_shapes=[pltpu.VMEM((AH, ADK, ADK), F32)],
        compiler_params=_cparams(),
    )(u, w, qg, kd, qk, eg, proj, norm_a)


def c2_bwd(u, w, qg, kd, qk, eg, proj, norm_a, sall, do):
    t = u.shape[1]
    nc = t // CH
    rev = lambda n: nc - 1 - n
    hm, _, qks, egs = _c1_specs(rev)
    tok = pl.BlockSpec((CH, D), lambda n: (rev(n), 0))
    zspec = pl.BlockSpec((CH, D), lambda n: (rev(n), C_Z // D))
    sspec = pl.BlockSpec((1, AH, ADK, ADK), lambda n: (rev(n), 0, 0, 0))

    def body(u_ref, w_ref, qg_ref, kd_ref, qk_ref, eg_ref, z_ref, nw_ref, sall_ref, do_ref,
             du_ref, dw_ref, dqg_ref, dkd_ref, dqk_ref, deg_ref, dz_ref, dnw_ref, dst):
        n = pl.program_id(0)

        @pl.when(n == 0)
        def _():
            dst[...] = jnp.zeros_like(dst)
            dnw_ref[...] = jnp.zeros_like(dnw_ref)

        _, vjp = jax.vjp(c2_heads, sall_ref[0], u_ref[...], w_ref[...].astype(F32), qk_ref[0].astype(F32),
                         qg_ref[...].astype(F32), kd_ref[...].astype(F32), eg_ref[0], _heads(z_ref), nw_ref[...])
        ds, du, dw, dqk, dqg, dkd, deg, dz, dn = vjp((_heads(do_ref), dst[...]))
        dst[...] = ds
        du_ref[...] = du.astype(BF16)
        dw_ref[...] = dw.astype(BF16)
        dqg_ref[...] = dqg
        dkd_ref[...] = dkd
        dqk_ref[0] = dqk
        deg_ref[0] = deg
        for h in range(AH):
            dz_ref[:, h * ADK:(h + 1) * ADK] = dz[h].astype(BF16)
        dnw_ref[...] += dn

    return pl.pallas_call(
        body, name="c2_bwd", grid=(nc,),
        in_specs=[hm, hm, hm, hm, qks, egs, zspec, _const((1, ADK)), sspec, tok],
        out_specs=[hm, hm, hm, hm, qks, egs, tok, _const((1, ADK))],
        out_shape=[jax.ShapeDtypeStruct((AH, t, ADK), BF16)] * 2 + [jax.ShapeDtypeStruct((AH, t, ADK), F32)] * 2 + [
            jax.ShapeDtypeStruct((nc, AH, CH, CH), F32), jax.ShapeDtypeStruct((nc, AH, 1, ADK), F32),
            jax.ShapeDtypeStruct((t, D), BF16), jax.ShapeDtypeStruct((1, ADK), F32)],
        scratch_shapes=[pltpu.VMEM((AH, ADK, ADK), F32)],
        compiler_params=_cparams(),
    )(u, w, qg, kd, qk, eg, proj, norm_a, sall, do)


NQB = TQ // CH
NKB = 2 * TQ // CH
NDIST = BPREV + 1
KLO = -(NQB - 2)
NPAIR = NKB - 1 - KLO + 1


def bias_table(rel_bias):
    nh = rel_bias.shape[0]
    relx = jnp.concatenate([rel_bias, jnp.broadcast_to(rel_bias[:, -1:], (nh, CH * BPREV + 2 * CH - 1 - RELSZ))],
                           axis=1)
    t = jnp.stack([relx[:, CH * k:CH * k + 2 * CH - 1] for k in range(NDIST)], axis=1)
    trev = t[:, :, ::-1]
    g2 = jnp.concatenate([trev[:, :, CH - 1:], jnp.zeros((nh, NDIST, 1), F32), trev[:, :, :CH - 1]], axis=2)
    flat = jnp.tile(g2, (1, 1, CH + 1))[:, :, :CH * (2 * CH - 1)]
    blk = flat.reshape(nh, NDIST, CH, 2 * CH - 1)[..., :CH]
    neg = jnp.full((nh, NQB - 1, CH, CH), NEG, F32)
    asc = jnp.concatenate([neg, blk, neg], axis=1)
    return jnp.concatenate([asc[:, 1:], asc[:, :-1]], axis=-1)


SUBQ = 4 * CH
NSUB = TQ // SUBQ
KWIN = SUBQ + BPREV * CH


def assemble_bias(tab, r):
    b0 = r * SUBQ // (2 * CH)
    rows = [jnp.concatenate([tab[NQB + a - 2 * b - KLO] for b in range(b0, b0 + KWIN // (2 * CH))], axis=1)
            for a in range(r * SUBQ // CH, (r + 1) * SUBQ // CH)]
    return jnp.concatenate(rows, axis=0)


def bias_table_bwd_layout(dtab):
    nh = dtab.shape[0]
    dasc = (jnp.pad(dtab[..., :CH], ((0, 0), (1, 0), (0, 0), (0, 0)))
            + jnp.pad(dtab[..., CH:], ((0, 0), (0, 1), (0, 0), (0, 0))))
    dblk = dasc[:, NQB - 1:NQB - 1 + NDIST]
    dr = jnp.pad(dblk, ((0, 0), (0, 0), (0, 0), (0, CH - 1)))
    flat = jnp.pad(dr.reshape(nh, NDIST, CH * (2 * CH - 1)), ((0, 0), (0, 0), (0, 3 * CH)))
    return flat.reshape(nh, NDIST, CH + 1, 2 * CH).transpose(0, 2, 1, 3).reshape(nh, CH + 1, NDIST * 2 * CH)


def _fold_matrix_np():
    f = np.zeros((NDIST * 2 * CH, 384), np.float32)
    for k in range(NDIST):
        s = k
        for xx in range(2 * CH):
            if xx == CH:
                continue
            m = CH - 1 - xx if xx < CH else 3 * CH - 1 - xx
            f[s * 2 * CH + xx, min(CH * k + m, RELSZ - 1)] = 1.0
    return f


def relbias_reduce(dlay):
    nh, rows, cols = dlay.shape
    rpad = (-rows) % 8
    dlay = jnp.pad(dlay, ((0, 0), (0, rpad), (0, 0)))
    fold = jnp.asarray(_fold_matrix_np())

    def body(d_ref, f_ref, o_ref):
        cs = jnp.sum(d_ref[0], axis=0, keepdims=True)
        o_ref[0] = _mmh(jnp.broadcast_to(cs, (8, cols)), f_ref[...])

    out = pl.pallas_call(
        body, name="relbias_reduce", grid=(nh,),
        in_specs=[pl.BlockSpec((1, rows + rpad, cols), lambda h: (h, 0, 0)), _const((cols, 384))],
        out_specs=pl.BlockSpec((1, 8, 384), lambda h: (h, 0, 0)),
        out_shape=jax.ShapeDtypeStruct((nh, 8, 384), F32),
        compiler_params=_cparams(),
    )(dlay, fold)
    return out[:, 0, :RELSZ]


def attn_fwd(proj, bias):
    t = proj.shape[0]
    nt = t // TQ
    cb = C_QKVB // 128

    def body(q_ref, kp_ref, kc_ref, vp_ref, vc_ref, b_ref, o_ref, p_ref):
        i = pl.program_id(1)
        firstf = jnp.where(i == 0, 1.0, 0.0)
        for r in range(NSUB):
            lo, hi = r * SUBQ, r * SUBQ + KWIN - TQ
            kw = jnp.concatenate([kp_ref[lo:, :], kc_ref[:hi, :]], axis=0)
            vw = jnp.concatenate([vp_ref[lo:, :], vc_ref[:hi, :]], axis=0)
            out, probs = attn_sub_fwd(q_ref[lo:lo + SUBQ, :].astype(F32), kw, vw, b_ref[...], r, firstf)
            o_ref[lo:lo + SUBQ, :] = out.astype(BF16)
            for hh in range(2):
                p_ref[hh, lo:lo + SUBQ, :] = probs[hh].astype(BF16)

    def blk(off, prev):
        if prev:
            return pl.BlockSpec((TQ, 128), lambda p, i: (jnp.maximum(i - 1, 0), cb + off + p))
        return pl.BlockSpec((TQ, 128), lambda p, i: (i, cb + off + p))

    return pl.pallas_call(
        body, name="attn_fwd", grid=(BH // 2, nt),
        in_specs=[blk(0, False), blk(8, True), blk(8, False), blk(16, True), blk(16, False),
                  pl.BlockSpec((2, NPAIR, CH, 2 * CH), lambda p, i: (p, 0, 0, 0))],
        out_specs=[pl.BlockSpec((TQ, 128), lambda p, i: (i, p)),
                   pl.BlockSpec((2, TQ, KWIN), lambda p, i: (p, i, 0))],
        out_shape=[jax.ShapeDtypeStruct((t, D), BF16), jax.ShapeDtypeStruct((BH, t, KWIN), BF16)],
        compiler_params=_cparams(2),
    )(proj, proj, proj, proj, proj, bias)


def attn_bwd(proj, ob, probs, do):
    t = proj.shape[0]
    nt = t // TQ
    cb = C_QKVB // 128

    def body(q_ref, kp_ref, kc_ref, vp_ref, vc_ref, o_ref, p_ref, do_ref,
             dq_ref, dk_ref, dv_ref, db_ref, ck, cv, ak, av):
        i = pl.program_id(1)

        @pl.when(i == 0)
        def _():
            ck[...] = jnp.zeros_like(ck)
            cv[...] = jnp.zeros_like(cv)
            db_ref[...] = jnp.zeros_like(db_ref)

        @pl.when(i < nt)
        def _():
            ak[...] = jnp.zeros_like(ak)
            av[...] = jnp.zeros_like(av)
            for r in range(NSUB):
                lo, hi = r * SUBQ, r * SUBQ + KWIN - TQ
                rows = slice(lo, lo + SUBQ)
                kw = jnp.concatenate([kp_ref[lo:, :], kc_ref[:hi, :]], axis=0)
                vw = jnp.concatenate([vp_ref[lo:, :], vc_ref[:hi, :]], axis=0)
                probs_r = [p_ref[hh, rows, :].astype(F32) for hh in range(2)]
                dq, dkw, dvw, dss = attn_sub_bwd(q_ref[rows, :].astype(F32), kw, vw, o_ref[rows, :].astype(F32),
                                                 do_ref[rows, :], probs_r, r)
                dq_ref[rows, :] = dq.astype(BF16)
                ak[lo:lo + KWIN, :] += dkw
                av[lo:lo + KWIN, :] += dvw
                for hh in range(2):
                    _, scatter = jax.vjp(lambda tab: assemble_bias(tab, r), jnp.zeros((NPAIR, CH, 2 * CH), F32))
                    db_ref[hh] += scatter(dss[hh])[0]
            dk_ref[...] = (ck[...] + ak[:TQ, :]).astype(BF16)
            dv_ref[...] = (cv[...] + av[:TQ, :]).astype(BF16)
            ck[...] = ak[TQ:, :]
            cv[...] = av[TQ:, :]

        @pl.when(i == nt)
        def _():
            dk_ref[...] = ck[...].astype(BF16)
            dv_ref[...] = cv[...].astype(BF16)

    def blk(off, prev):
        if prev:
            return pl.BlockSpec((TQ, 128), lambda p, i: (jnp.clip(i - 1, 0, nt - 1), cb + off + p))
        return pl.BlockSpec((TQ, 128), lambda p, i: (jnp.minimum(i, nt - 1), cb + off + p))

    own = pl.BlockSpec((TQ, 128), lambda p, i: (jnp.minimum(i, nt - 1), p))
    lag = pl.BlockSpec((TQ, 128), lambda p, i: (jnp.maximum(i - 1, 0), p))
    return pl.pallas_call(
        body, name="attn_bwd", grid=(BH // 2, nt + 1),
        in_specs=[blk(0, False), blk(8, True), blk(8, False), blk(16, True), blk(16, False), own,
                  pl.BlockSpec((2, TQ, KWIN), lambda p, i: (p, jnp.minimum(i, nt - 1), 0)), own],
        out_specs=[own, lag, lag, pl.BlockSpec((2, NPAIR, CH, 2 * CH), lambda p, i: (p, 0, 0, 0))],
        out_shape=[jax.ShapeDtypeStruct((t, D), BF16)] * 3 + [jax.ShapeDtypeStruct((BH, NPAIR, CH, 2 * CH), F32)],
        scratch_shapes=[pltpu.VMEM((TQ, 128), F32), pltpu.VMEM((TQ, 128), F32),
                        pltpu.VMEM((2 * TQ, 128), F32), pltpu.VMEM((2 * TQ, 128), F32)],
        compiler_params=_cparams(2),
    )(proj, proj, proj, proj, proj, ob, probs, do)


MERGE_TM = 256


def merge_fwd(x, oa, ob, proj, vecs, wa, wb, wo):
    t = x.shape[0]
    tm = MERGE_TM
    names = ("bga", "bgb", "gate_t", "g1", "b1", "scale_f", "shift_f")

    def body(x_ref, oa_ref, ob_ref, gra_ref, grb_ref, *rest):
        vrefs = rest[:7]
        wa_ref, wb_ref, wo_ref, y_ref, h_ref = rest[7:]
        vv = [r[...] for r in vrefs]
        zero = jnp.zeros((tm, D), F32)
        y1, _ = merge_fn(x_ref[...], oa_ref[...], ob_ref[...], gra_ref[...], grb_ref[...], zero, zero, zero,
                         *vv, wa_ref[...], wb_ref[...], wo_ref[...])
        y_ref[...] = y1
        h_ref[...] = (y1 * (1.0 + vv[5]) + vv[6]).astype(BF16)

    return pl.pallas_call(
        body, name="merge_fwd", grid=(t // tm,),
        in_specs=[_rows(tm, D), _rows(tm, D), _rows(tm, D), _rows(tm, D, C_GATE // D), _rows(tm, D, C_GATE // D + 1)]
        + [_const((1, D))] * 7 + [_const((D, D))] * 3,
        out_specs=[_rows(tm, D), _rows(tm, D)],
        out_shape=[jax.ShapeDtypeStruct((t, D), F32), jax.ShapeDtypeStruct((t, D), BF16)],
        compiler_params=_cparams(),
    )(x, oa, ob, proj, proj, *[vecs[n] for n in names], wa, wb, wo)


def merge_bwd(x, oa, ob, proj, vecs, wa, wb, wo, dy1):
    t = x.shape[0]
    tm = MERGE_TM
    names = ("bga", "bgb", "gate_t", "g1", "b1", "scale_f", "shift_f")

    def body(x_ref, oa_ref, ob_ref, gra_ref, grb_ref, *rest):
        vrefs = rest[:7]
        wa_ref, wb_ref, wo_ref, dy_ref = rest[7:11]
        (dx_ref, doa_ref, dob_ref, dga_ref, dgb_ref, mg_ref, dmix_ref, dpa_ref, dpb_ref,
         dbga_ref, dbgb_ref, dgt_ref, dg1_ref, db1_ref) = rest[11:]
        i = pl.program_id(0)
        vv = [r[...] for r in vrefs]
        zero = jnp.zeros((tm, D), F32)

        def f(x_, oa_, ob_, gra_, grb_, ppa, ppb, pmix, bga, bgb, gate_t, g1, b1):
            return merge_fn(x_, oa_, ob_, gra_, grb_, ppa, ppb, pmix, bga, bgb, gate_t, g1, b1, vv[5], vv[6],
                            wa_ref[...], wb_ref[...], wo_ref[...])

        _, vjp, merged = jax.vjp(f, x_ref[...], oa_ref[...].astype(F32), ob_ref[...].astype(F32),
                                 gra_ref[...], grb_ref[...], zero, zero, zero, *vv[:5], has_aux=True)
        dx, doa, dob, dga, dgb, dpa, dpb, dmix, dbga, dbgb, dgt, dg1, db1 = vjp(dy_ref[...])
        dx_ref[...] = dx
        doa_ref[...] = doa
        dob_ref[...] = dob
        dga_ref[...] = dga.astype(BF16)
        dgb_ref[...] = dgb.astype(BF16)
        mg_ref[...] = merged.astype(BF16)
        dmix_ref[...] = dmix.astype(BF16)
        dpa_ref[...] = dpa.astype(BF16)
        dpb_ref[...] = dpb.astype(BF16)
        accs = (dbga_ref, dbgb_ref, dgt_ref, dg1_ref, db1_ref)

        @pl.when(i == 0)
        def _():
            for a in accs:
                a[...] = jnp.zeros_like(a)

        for a, val in zip(accs, (dbga, dbgb, dgt, dg1, db1)):
            a[...] += val

    return pl.pallas_call(
        body, name="merge_bwd", grid=(t // tm,),
        in_specs=[_rows(tm, D), _rows(tm, D), _rows(tm, D), _rows(tm, D, C_GATE // D), _rows(tm, D, C_GATE // D + 1)]
        + [_const((1, D))] * 7 + [_const((D, D))] * 3 + [_rows(tm, D)],
        out_specs=[_rows(tm, D)] * 9 + [_const((1, D))] * 5,
        out_shape=[jax.ShapeDtypeStruct((t, D), F32)] * 3 + [jax.ShapeDtypeStruct((t, D), BF16)] * 6
        + [jax.ShapeDtypeStruct((1, D), F32)] * 5,
        compiler_params=_cparams(),
    )(x, oa, ob, proj, proj, *[vecs[n] for n in names], wa, wb, wo, dy1)


FFN_TM = 128


def ffn_act_fwd(up, conv_w, bconv):
    t, wdt = up.shape
    tm = FFN_TM

    def body(prev_ref, cur_ref, cw_ref, bc_ref, a_ref):
        i = pl.program_id(0)
        flag = jnp.where(i > 0, 1.0, 0.0)

        def ext(sl):
            return jnp.concatenate([prev_ref[:, sl] * flag, cur_ref[:, sl]], axis=0)

        def rows(sl):
            return tuple(cw_ref[j:j + 1, sl] for j in range(3))

        for cb in range(DFF // LANE):
            g = slice(cb * LANE, (cb + 1) * LANE)
            v = slice(DFF + cb * LANE, DFF + (cb + 1) * LANE)
            a_ref[:, g] = ffn_act_fn(ext(g), ext(v), rows(g), rows(v), bc_ref[:, g], bc_ref[:, v]).astype(BF16)

    return pl.pallas_call(
        body, name="ffn_act_fwd", grid=(t // tm,),
        in_specs=_halo_specs(tm, wdt, 0, lambda i: i) + [_const((3, wdt)), _const((1, wdt))],
        out_specs=_rows(tm, DFF),
        out_shape=jax.ShapeDtypeStruct((t, DFF), BF16),
        compiler_params=_cparams(),
    )(up, up, conv_w, bconv)


def ffn_act_bwd(up, conv_w, bconv, da):
    t, wdt = up.shape
    tm = FFN_TM
    nt = t // tm
    rev = lambda i: nt - 1 - i

    def body(prev_ref, cur_ref, cw_ref, bc_ref, da_ref, dup_ref, dcw_ref, dbc_ref, carry):
        i = pl.program_id(0)
        flag = jnp.where(i < nt - 1, 1.0, 0.0)

        @pl.when(i == 0)
        def _():
            carry[...] = jnp.zeros_like(carry)
            dcw_ref[...] = jnp.zeros_like(dcw_ref)
            dbc_ref[...] = jnp.zeros_like(dbc_ref)

        def ext(sl):
            return jnp.concatenate([prev_ref[:, sl] * flag, cur_ref[:, sl]], axis=0)

        def rows(sl):
            return tuple(cw_ref[j:j + 1, sl] for j in range(3))

        def emit(sl, dext, drows, dbc):
            dcur = dext[HALO:]
            dup_ref[:, sl] = jnp.concatenate([dcur[:tm - HALO], dcur[tm - HALO:] + carry[:, sl]], axis=0).astype(BF16)
            carry[:, sl] = dext[:HALO]
            dcw_ref[:, sl] += _stack_rows(drows)
            dbc_ref[:, sl] += dbc

        for cb in range(DFF // LANE):
            g = slice(cb * LANE, (cb + 1) * LANE)
            v = slice(DFF + cb * LANE, DFF + (cb + 1) * LANE)
            _, vjp = jax.vjp(ffn_act_fn, ext(g), ext(v), rows(g), rows(v), bc_ref[:, g], bc_ref[:, v])
            dxg, dxv, drg, drv, dbg, dbv = vjp(da_ref[:, g])
            emit(g, dxg, drg, dbg)
            emit(v, dxv, drv, dbv)

    return pl.pallas_call(
        body, name="ffn_act_bwd", grid=(nt,),
        in_specs=_halo_specs(tm, wdt, 0, rev) + [_const((3, wdt)), _const((1, wdt)), _rows(tm, DFF, 0, rev)],
        out_specs=[_rows(tm, wdt, 0, rev), _const((3, wdt)), _const((1, wdt))],
        out_shape=[jax.ShapeDtypeStruct((t, wdt), BF16), jax.ShapeDtypeStruct((3, wdt), F32),
                   jax.ShapeDtypeStruct((1, wdt), F32)],
        scratch_shapes=[pltpu.VMEM((HALO, wdt), F32)],
        compiler_params=_cparams(),
    )(up, up, conv_w, bconv, da)


HEAD_TM = 256


def head_fwd_bwd(a, y1, tgt, gate_f, g2, b2, wd):
    t = a.shape[0]
    tm = HEAD_TM

    def body(a_ref, y_ref, t_ref, gf_ref, g2_ref, b2_ref, wd_ref,
             da_ref, dy_ref, dffn_ref, dgf_ref, dg2_ref, db2_ref, loss_ref):
        i = pl.program_id(0)
        zero = jnp.zeros((tm, D), F32)

        def f(a_, y_, pf, gf, g2_, b2_):
            return head_fn(a_, y_, pf, gf, g2_, b2_, t_ref[...], wd_ref[...])

        loss, vjp = jax.vjp(f, a_ref[...].astype(F32), y_ref[...], zero, gf_ref[...], g2_ref[...], b2_ref[...])
        da, dy, dffn, dgf, dg2, db2 = vjp(jnp.ones((), F32))
        da_ref[...] = da
        dy_ref[...] = dy
        dffn_ref[...] = dffn.astype(BF16)
        accs = (dgf_ref, dg2_ref, db2_ref, loss_ref)

        @pl.when(i == 0)
        def _():
            for r in accs:
                r[...] = jnp.zeros_like(r)

        dgf_ref[...] += dgf
        dg2_ref[...] += dg2
        db2_ref[...] += db2
        loss_ref[...] += loss * jnp.ones((1, 128), F32)

    return pl.pallas_call(
        body, name="head_fwd_bwd", grid=(t // tm,),
        in_specs=[_rows(tm, DFF), _rows(tm, D), _rows(tm, D), _const((1, D)), _const((1, D)), _const((1, D)),
                  _const((DFF, D))],
        out_specs=[_rows(tm, DFF), _rows(tm, D), _rows(tm, D), _const((1, D)), _const((1, D)), _const((1, D)),
                   _const((1, 128))],
        out_shape=[jax.ShapeDtypeStruct((t, DFF), F32), jax.ShapeDtypeStruct((t, D), F32),
                   jax.ShapeDtypeStruct((t, D), BF16)] + [jax.ShapeDtypeStruct((1, D), F32)] * 3
        + [jax.ShapeDtypeStruct((1, 128), F32)],
        compiler_params=_cparams(),
    )(a, y1, tgt, gate_f, g2, b2, wd)


def ada_fwd(c_all, w_sh, b_sh):
    def body(c_ref, w_ref, b_ref, o_ref):
        o_ref[...] = _mmh(_silu(c_ref[...]), w_ref[...]) + b_ref[...]

    n = w_sh.shape[1]
    return pl.pallas_call(
        body, name="ada_fwd", out_shape=jax.ShapeDtypeStruct((NDEV, n), F32),
        in_specs=[pl.BlockSpec(memory_space=pltpu.VMEM)] * 3,
        out_specs=pl.BlockSpec(memory_space=pltpu.VMEM),
        compiler_params=pltpu.CompilerParams(vmem_limit_bytes=VMEM_LIMIT),
    )(c_all, w_sh, b_sh)


def ada_wgrad(c_all_t, dmod_sh):
    def body(c_ref, d_ref, o_ref):
        o_ref[...] = _mmh(_silu(c_ref[...]), d_ref[...])

    return pl.pallas_call(
        body, name="ada_wgrad", out_shape=jax.ShapeDtypeStruct((c_all_t.shape[0], dmod_sh.shape[1]), F32),
        in_specs=[pl.BlockSpec(memory_space=pltpu.VMEM)] * 2,
        out_specs=pl.BlockSpec(memory_space=pltpu.VMEM),
        compiler_params=pltpu.CompilerParams(vmem_limit_bytes=VMEM_LIMIT),
    )(c_all_t, dmod_sh)


def adamw(gparts, w, m, v, name):
    p, r, c = gparts.shape
    tr = r if r <= 256 else _pick(r, (256, 128, 64, 32, 16, 8))
    c1 = 1.0 - B1 ** STEP
    c2 = 1.0 - B2 ** STEP

    def body(g_ref, w_ref, m_ref, v_ref, go_ref, d_ref, mo_ref, vo_ref):
        g = g_ref[0].astype(F32)
        for s in range(1, p):
            g = g + g_ref[s].astype(F32)
        mn = B1 * m_ref[0] + (1.0 - B1) * g
        vn = B2 * v_ref[0] + (1.0 - B2) * (g * g)
        go_ref[0] = g
        d_ref[0] = -LR * ((mn / c1) / (jnp.sqrt(vn / c2) + AEPS) + WD * w_ref[0])
        mo_ref[0] = mn
        vo_ref[0] = vn

    spec = pl.BlockSpec((1, tr, c), lambda i: (0, i, 0))
    return pl.pallas_call(
        body, name=name, grid=(r // tr,),
        in_specs=[pl.BlockSpec((p, tr, c), lambda i: (0, i, 0)), spec, spec, spec],
        out_specs=[spec] * 4,
        out_shape=[jax.ShapeDtypeStruct((1, r, c), F32)] * 4,
        compiler_params=_cparams(),
    )(gparts, w, m, v)


def adamw_small(gs, ws, ms, vs, loss_parts, name):
    n = len(ws)
    c1 = 1.0 - B1 ** STEP
    c2 = 1.0 - B2 ** STEP

    def slots(ref):
        acc = ref[0]
        for s in range(1, ref.shape[0]):
            acc = acc + ref[s]
        return acc

    def body(*refs):
        g_refs, w_refs, m_refs, v_refs = (refs[k * n:(k + 1) * n] for k in range(4))
        l_ref, outs = refs[4 * n], refs[4 * n + 1:]
        for i in range(n):
            g = slots(g_refs[i])
            mn = B1 * m_refs[i][...] + (1.0 - B1) * g
            vn = B2 * v_refs[i][...] + (1.0 - B2) * (g * g)
            outs[i][...] = g
            outs[n + i][...] = -LR * ((mn / c1) / (jnp.sqrt(vn / c2) + AEPS) + WD * w_refs[i][...])
            outs[2 * n + i][...] = mn
            outs[3 * n + i][...] = vn
        outs[4 * n][...] = slots(l_ref)

    vmem = pl.BlockSpec(memory_space=pltpu.VMEM)
    outs = pl.pallas_call(
        body, name=name,
        in_specs=[vmem] * (4 * n + 1), out_specs=[vmem] * (4 * n + 1),
        out_shape=[jax.ShapeDtypeStruct(w.shape, F32) for w in ws] * 4 + [jax.ShapeDtypeStruct((1, LANE), F32)],
        compiler_params=pltpu.CompilerParams(vmem_limit_bytes=VMEM_LIMIT),
    )(*gs, *ws, *ms, *vs, loss_parts)
    return outs[:n], outs[n:2 * n], outs[2 * n:3 * n], outs[3 * n:4 * n], outs[4 * n]


def _me():
    x, y, c = lax.axis_index("x"), lax.axis_index("y"), lax.axis_index("c")
    return x, y, c, 4 * x + 2 * y + c


def _peer(x, y, c, d):
    px = 1 - x if (d >> 2) & 1 else x
    py = 1 - y if (d >> 1) & 1 else y
    pc = 1 - c if d & 1 else c
    return (px, py, pc), 4 * px + 2 * py + pc


def _exchange(arrs, name, scatter):
    n = len(arrs)

    def body(*refs):
        ins, outs = refs[:n], refs[n:2 * n]
        send, recv, lsem = refs[2 * n:]
        x, y, c, me = _me()
        remote, local = [], []
        for k in range(n):
            src = ins[k].at[me] if scatter else ins[k]
            cp = pltpu.make_async_copy(src, outs[k].at[me], lsem.at[k])
            cp.start()
            local.append(cp)
            for d in range(1, NDEV):
                dev, pid = _peer(x, y, c, d)
                src = ins[k].at[pid] if scatter else ins[k]
                cp = pltpu.make_async_remote_copy(src_ref=src, dst_ref=outs[k].at[me],
                                                  send_sem=send.at[k, d - 1], recv_sem=recv.at[k, d - 1],
                                                  device_id=dev, device_id_type=pl.DeviceIdType.MESH)
                cp.start()
                remote.append(cp)
        for cp in remote:
            cp.wait()
        for cp in local:
            cp.wait()

    shapes = [a.shape if scatter else (NDEV,) + a.shape for a in arrs]
    return pl.pallas_call(
        body, name=name,
        in_specs=[pl.BlockSpec(memory_space=pl.ANY)] * n,
        out_specs=[pl.BlockSpec(memory_space=pl.ANY)] * n,
        out_shape=[jax.ShapeDtypeStruct(s, a.dtype) for s, a in zip(shapes, arrs)],
        scratch_shapes=[pltpu.SemaphoreType.DMA((n, NDEV - 1)), pltpu.SemaphoreType.DMA((n, NDEV - 1)),
                        pltpu.SemaphoreType.DMA((n,))],
        compiler_params=pltpu.CompilerParams(has_side_effects=True),
    )(*arrs)


def all_gather(arrs, name):
    return _exchange(arrs, name, False)


def all_gather_two_level(shard, name):
    def body(x_ref, out_ref, send, recv, lsem):
        x, y, c, _ = _me()
        sibling = (x, y, 1 - c)
        chips = [(1 - x, y), (x, 1 - y), (1 - x, 1 - y)]

        def slot(px, py, pc):
            return out_ref.at[4 * px + 2 * py + pc]

        def copy(k, block, to, src=None):
            return pltpu.make_async_remote_copy(
                src_ref=slot(*block) if src is None else src, dst_ref=slot(*block),
                send_sem=send.at[k], recv_sem=recv.at[k], device_id=to, device_id_type=pl.DeviceIdType.MESH)

        mine = pltpu.make_async_copy(x_ref, slot(x, y, c), lsem)
        mine.start()
        first = [copy(0, (x, y, c), sibling, src=x_ref)]
        first += [copy(1 + j, (x, y, c), (*chip, c), src=x_ref) for j, chip in enumerate(chips)]
        for cp in first:
            cp.start()
        passed = [copy(4 + j, (*chip, c), sibling) for j, chip in enumerate(chips)]
        for j, chip in enumerate(chips):
            copy(1 + j, (*chip, c), (x, y, c)).wait_recv()
            passed[j].start()
        copy(0, sibling, (x, y, c)).wait_recv()
        for j, chip in enumerate(chips):
            copy(4 + j, (*chip, 1 - c), (x, y, c)).wait_recv()
        for cp in first + passed:
            cp.wait_send()
        mine.wait()

    return pl.pallas_call(
        body, name=name,
        in_specs=[pl.BlockSpec(memory_space=pl.ANY)],
        out_specs=pl.BlockSpec(memory_space=pl.ANY),
        out_shape=jax.ShapeDtypeStruct((NDEV,) + shard.shape, shard.dtype),
        scratch_shapes=[pltpu.SemaphoreType.DMA((NPEER,)), pltpu.SemaphoreType.DMA((NPEER,)),
                        pltpu.SemaphoreType.DMA],
        compiler_params=pltpu.CompilerParams(has_side_effects=True),
    )(shard)


def all_to_all(arrs, name):
    return _exchange(arrs, name, True)


_HBM = pl.BlockSpec(memory_space=pltpu.HBM)
_SEM = pl.BlockSpec(memory_space=pltpu.SEMAPHORE)
_EFFECT = pltpu.SideEffectType.DATAFLOW_SIDE_EFFECTING
NPEER = NDEV - 1


def exchange_start(arrs, name, scatter):
    n = len(arrs)
    lands = [lax.empty(a.shape if scatter else (NDEV,) + a.shape, a.dtype) for a in arrs]

    def body(*refs):
        ins, lrefs = refs[:n], refs[n:2 * n]
        send, recv, token = refs[2 * n], refs[2 * n + 1], refs[-1]
        x, y, c, me = _me()
        for k in range(n):
            for d in range(1, NDEV):
                dev, pid = _peer(x, y, c, d)
                src = ins[k].at[pid] if scatter else ins[k]
                pltpu.make_async_remote_copy(src_ref=src, dst_ref=lrefs[k].at[me],
                                             send_sem=send.at[k * NPEER + d - 1], recv_sem=recv.at[k * NPEER + d - 1],
                                             device_id=dev, device_id_type=pl.DeviceIdType.MESH).start()
        token[...] = jnp.zeros_like(token)

    thru = [pltpu.HBM(a.shape, a.dtype) for a in list(arrs) + lands]
    outs = pl.pallas_call(
        body, name=name,
        out_shape=(pltpu.SemaphoreType.DMA((n * NPEER,)), pltpu.SemaphoreType.DMA((n * NPEER,)), *thru,
                   jax.ShapeDtypeStruct((8, 128), F32)),
        in_specs=[_HBM] * (2 * n),
        out_specs=(_SEM, _SEM, *([_HBM] * (2 * n)), pl.BlockSpec(memory_space=pltpu.VMEM)),
        input_output_aliases={i: 2 + i for i in range(2 * n)},
        compiler_params=pltpu.CompilerParams(has_side_effects=_EFFECT),
    )(*[pltpu.with_memory_space_constraint(a, pltpu.HBM) for a in list(arrs) + lands])
    handle = dict(send=outs[0], recv=outs[1], src=list(outs[2:2 + n]), land=list(outs[2 + n:2 + 2 * n]),
                  scatter=scatter)
    return handle, outs[-1][0, 0]


def exchange_wait(handle, after, name):
    n = len(handle["src"])
    scatter = handle["scatter"]

    def body(*refs):
        ins, lrefs = refs[:n], refs[n:2 * n]
        send, recv = refs[2 * n], refs[2 * n + 1]
        x, y, c, _ = _me()
        for k in range(n):
            for d in range(1, NDEV):
                dev, _ = _peer(x, y, c, d)
                src = ins[k].at[0] if scatter else ins[k]
                cp = pltpu.make_async_remote_copy(src_ref=src, dst_ref=lrefs[k].at[0],
                                                  send_sem=send.at[k * NPEER + d - 1],
                                                  recv_sem=recv.at[k * NPEER + d - 1],
                                                  device_id=dev, device_id_type=pl.DeviceIdType.MESH)
                cp.wait_send()
                cp.wait_recv()

    arrs = handle["src"] + handle["land"]
    outs = pl.pallas_call(
        body, name=name,
        out_shape=tuple(pltpu.HBM(a.shape, a.dtype) for a in arrs),
        in_specs=[_HBM] * (2 * n) + [_SEM, _SEM, pl.BlockSpec(memory_space=pl.ANY)],
        out_specs=tuple([_HBM] * (2 * n)),
        input_output_aliases={i: i for i in range(2 * n)},
        compiler_params=pltpu.CompilerParams(has_side_effects=_EFFECT),
    )(*arrs, handle["send"], handle["recv"], after)
    me = 4 * lax.axis_index("x") + 2 * lax.axis_index("y") + lax.axis_index("c")
    landed = []
    for own, land in zip(outs[:n], outs[n:]):
        mine = lax.dynamic_index_in_dim(own, me, 0, keepdims=True) if scatter else own[None]
        landed.append(lax.dynamic_update_slice_in_dim(land, mine, me, 0))
    return landed


def _cat_from_slabs(slabs):
    _, k, n = slabs.shape

    def cols(lo, hi):
        parts, c = [], lo
        while c < hi:
            j = c // n
            e = min(hi, (j + 1) * n)
            parts.append(slabs[j][:, c - j * n:e - j * n])
            c = e
        return parts

    def zeros(w):
        return [jnp.zeros((k, w), slabs.dtype)]

    return jnp.concatenate(cols(0, 4096) + cols(4112, 9232) + cols(4096, 4104) + zeros(LANE - AH)
                           + cols(4104, 4112) + zeros(NCAT - C_BA - LANE - AH), axis=1)


IN_PIECES = (("pre", C_QKVA, 3072), ("z", C_Z, 1024), ("qb", C_QKVB, 1024), ("kb", C_QKVB + 1024, 1024),
             ("vb", C_QKVB + 2048, 1024), ("ga", C_GATE, 1024), ("gb", C_GATE + 1024, 1024))
_ORIG_SEGS = ((0, 3072, "pre", 0), (3072, 4096, "z", 0), (4096, 4104, "ba", 0), (4104, 4112, "ba", LANE),
              (4112, 5136, "qb", 0), (5136, 6160, "kb", 0), (6160, 7184, "vb", 0), (7184, 8208, "ga", 0),
              (8208, 9232, "gb", 0))


def _orig_cols_from_pieces(gp, lo, hi):
    parts = []
    for a, b, name, off in _ORIG_SEGS:
        s, e = max(a, lo), min(b, hi)
        if s < e:
            parts.append(gp[name][:, off + s - a:off + e - a])
    return parts[0] if len(parts) == 1 else jnp.concatenate(parts, axis=1)


def _pad128(v):
    return jnp.pad(v, ((0, 0), (0, 128 - v.shape[1])))


def local_step(x, tgt, mod, wts, small, late_weights=None, on_grads=None):
    if on_grads is None:
        on_grads = lambda group, gd: jnp.zeros((), F32)
    t = x.shape[0]
    nc = t // CH
    shift_t, scale_t, gate_t, shift_f, scale_f, gate_f = mod
    wcat = _cat_from_slabs(wts["w_in_slabs"])
    a_log = _pad128(small["a_log"])
    dtb = _pad128(small["dt_bias"])
    vecs = dict(bga=small["b_gate"][:, :D], bgb=small["b_gate"][:, D:], gate_t=gate_t, g1=small["ln1_g"],
                b1=small["ln1_b"], scale_f=scale_f, shift_f=shift_f)

    h1 = modulate(x, scale_t, shift_t, "modulate_t")
    proj = matmul(h1, wcat, F32, "in_proj")
    q, k, v, gcs, beta = prep_fwd(proj, small["conv_a"], a_log, dtb)

    u, w, qg, kd, qk, eg, tinv = c1_fwd(q, k, v, gcs, beta)
    oa, sall = c2_fwd(u, w, qg, kd, qk, eg, proj, small["norm_a"])
    bias = bias_table(small["rel_bias"])
    ob, probs = attn_fwd(proj, bias)
    if late_weights is not None:
        wts = {**wts, **late_weights(ob)}
    y1, h2 = merge_fwd(x, oa, ob, proj, vecs, wts["w_a"], wts["w_b"], wts["w_o"])
    up = matmul(h2, wts["w_up"], F32, "up_proj")
    a = ffn_act_fwd(up, small["conv_ffn"], small["b_conv_ffn"])

    da, dy1_res, dffn, dgate_f, dg2, db2, loss = head_fwd_bwd(a, y1, tgt, gate_f, small["ln2_g"], small["ln2_b"],
                                                            wts["w_down"])
    g_w_down = matmul(a, dffn, BF16, "wgrad_down", ta=True)
    dup, g_conv_ffn, g_bconv = ffn_act_bwd(up, small["conv_ffn"], small["b_conv_ffn"], da)
    g_w_up = matmul(h2, dup, BF16, "wgrad_up", ta=True)
    tok = on_grads("ffn", dict(w_up=g_w_up, w_down=g_w_down))
    dy1, dscale_f, dshift_f = dgrad_modulated(dup, wts["w_up"], y1, dy1_res, scale_f + tok, "dgrad_up")
    (dx_res, doa, dob, dga, dgb, merged, dmix, dpa, dpb,
     dbga, dbgb, dgate_t, dg1, db1) = merge_bwd(x, oa, ob, proj, vecs, wts["w_a"], wts["w_b"], wts["w_o"], dy1)
    g_w_o = matmul(merged, dmix, BF16, "wgrad_o", ta=True)
    g_w_a = matmul(oa, dpa, BF16, "wgrad_a", ta=True)
    g_w_b = matmul(ob, dpb, BF16, "wgrad_b", ta=True)
    tok = on_grads("mix", dict(w_o=g_w_o, w_a=g_w_a, w_b=g_w_b))
    dqb, dkb, dvb, dbias = attn_bwd(proj, ob, probs, dob)
    g_rel = relbias_reduce(bias_table_bwd_layout(dbias))
    du, dw, dqg, dkd, dqk, deg, dz, g_norm = c2_bwd(u, w, qg, kd, qk, eg, proj, small["norm_a"] + tok, sall, doa)
    dq, dk, dv, dgcs, dbeta = c1_bwd(q, k, v, gcs, beta, tinv, du, dw, dqg, dkd, dqk, deg)
    dpre, dbb, daa, g_conv_a, g_alog, g_dtb = prep_bwd(proj, small["conv_a"], a_log, dtb, dq, dk, dv, dgcs, dbeta)
    tok = on_grads("small", dict(conv_a=g_conv_a, rel_bias=g_rel, conv_ffn=g_conv_ffn))
    dba = jnp.concatenate([dbb, daa, jnp.zeros((t, NCAT - C_BA - 2 * LANE), BF16)], axis=1) + tok.astype(BF16)
    dpieces = dict(pre=dpre, z=dz, qb=dqb, kb=dkb, vb=dvb, ga=dga, gb=dgb)
    g_in = {n: matmul(h1, dpieces[n], BF16, "wgrad_in_" + n, ta=True) for n, _, _ in IN_PIECES}
    g_in["ba"] = matmul(h1, dba, BF16, "wgrad_in_ba", ta=True)
    tok = on_grads("in", g_in)
    dh1 = dgrad_pieces([(dpieces[n], off) for n, off, _ in IN_PIECES], dba + tok.astype(BF16), wcat,
                       "dgrad_in")
    grad_x, dscale_t, dshift_t = modulate_bwd(dh1, x, dx_res, scale_t + tok, "modulate_t_bwd")

    dmod = (dshift_t, dscale_t, dgate_t, dshift_f, dscale_f, dgate_f)
    grads = dict(w_in=_orig_cols_from_pieces(g_in, 0, 9232), w_up=g_w_up, w_down=g_w_down, w_a=g_w_a, w_b=g_w_b, w_o=g_w_o,
                 conv_a=g_conv_a, rel_bias=g_rel, conv_ffn=g_conv_ffn,
                 b_gate=jnp.concatenate([dbga, dbgb], axis=1), a_log=g_alog[:, :AH], dt_bias=g_dtb[:, :AH],
                 norm_a=g_norm, ln1_g=dg1, ln1_b=db1, b_conv_ffn=g_bconv, ln2_g=dg2, ln2_b=db2)
    return loss[0, 0], grad_x, dmod, grads


REP_NAMES = ["b_ada", "b_gate", "a_log", "dt_bias", "norm_a", "ln1_g", "ln1_b", "b_conv_ffn", "ln2_g", "ln2_b"]
SH_NAMES = ["conv_a", "rel_bias", "conv_ffn"]


def _col_shards(a, n):
    return a.reshape(a.shape[0], NDEV, n).transpose(1, 0, 2)


def kernel(x, c, w_ada, b_ada, w_in, b_gate, conv_a, a_log, dt_bias, norm_a, rel_bias, w_branch_a, w_branch_b, w_o, ln1_g, ln1_b, w_up, conv_ffn, b_conv_ffn, w_down, ln2_g, ln2_b, loss_target, m_w_ada, m_b_ada, m_w_in, m_b_gate, m_conv_a, m_a_log, m_dt_bias, m_norm_a, m_rel_bias, m_w_branch_a, m_w_branch_b, m_w_o, m_ln1_g, m_ln1_b, m_w_up, m_conv_ffn, m_b_conv_ffn, m_w_down, m_ln2_g, m_ln2_b, v_w_ada, v_b_ada, v_w_in, v_b_gate, v_conv_a, v_a_log, v_dt_bias, v_norm_a, v_rel_bias, v_w_branch_a, v_w_branch_b, v_w_o, v_ln1_g, v_ln1_b, v_w_up, v_conv_ffn, v_b_conv_ffn, v_w_down, v_ln2_g, v_ln2_b):
    W = dict(w_ada=w_ada, b_ada=b_ada, w_in=w_in, b_gate=b_gate, conv_a=conv_a, a_log=a_log, dt_bias=dt_bias,
             norm_a=norm_a, rel_bias=rel_bias, w_branch_a=w_branch_a, w_branch_b=w_branch_b, w_o=w_o, ln1_g=ln1_g,
             ln1_b=ln1_b, w_up=w_up, conv_ffn=conv_ffn, b_conv_ffn=b_conv_ffn, w_down=w_down, ln2_g=ln2_g,
             ln2_b=ln2_b)
    M = dict(w_ada=m_w_ada, b_ada=m_b_ada, w_in=m_w_in, b_gate=m_b_gate, conv_a=m_conv_a, a_log=m_a_log,
             dt_bias=m_dt_bias, norm_a=m_norm_a, rel_bias=m_rel_bias, w_branch_a=m_w_branch_a,
             w_branch_b=m_w_branch_b, w_o=m_w_o, ln1_g=m_ln1_g, ln1_b=m_ln1_b, w_up=m_w_up, conv_ffn=m_conv_ffn,
             b_conv_ffn=m_b_conv_ffn, w_down=m_w_down, ln2_g=m_ln2_g, ln2_b=m_ln2_b)
    V = dict(w_ada=v_w_ada, b_ada=v_b_ada, w_in=v_w_in, b_gate=v_b_gate, conv_a=v_conv_a, a_log=v_a_log,
             dt_bias=v_dt_bias, norm_a=v_norm_a, rel_bias=v_rel_bias, w_branch_a=v_w_branch_a,
             w_branch_b=v_w_branch_b, w_o=v_w_o, ln1_g=v_ln1_g, ln1_b=v_ln1_b, w_up=v_w_up, conv_ffn=v_conv_ffn,
             b_conv_ffn=v_b_conv_ffn, w_down=v_w_down, ln2_g=v_ln2_g, ln2_b=v_ln2_b)
    W3, M3, V3 = W, M, V
    W, M, V = ({n: a[0] for n, a in dct.items()} for dct in (W, M, V))
    me = 4 * lax.axis_index("x") + 2 * lax.axis_index("y") + lax.axis_index("c")
    big = ("w_in", "w_up", "w_down", "w_branch_a", "w_branch_b", "w_o")

    g_in = all_gather_two_level(W["w_in"].astype(BF16), "gather_w_in")
    wts = dict(w_in_slabs=g_in)
    c_all, *sh_all = all_gather([c] + [W[n] for n in SH_NAMES], "gather_small")
    c_all = c_all.reshape(NDEV, D)

    def full_small(g8):
        return g8.transpose(1, 0, 2).reshape(g8.shape[1], -1)

    small = dict(conv_a=full_small(sh_all[0]), rel_bias=full_small(sh_all[1]), conv_ffn=full_small(sh_all[2]),
                 b_gate=W["b_gate"][None], a_log=W["a_log"][None], dt_bias=W["dt_bias"][None],
                 norm_a=W["norm_a"][None], ln1_g=W["ln1_g"][None], ln1_b=W["ln1_b"][None],
                 b_conv_ffn=W["b_conv_ffn"][None], ln2_g=W["ln2_g"][None], ln2_b=W["ln2_b"][None])

    nsh = w_ada.shape[2]
    b_sh = lax.dynamic_slice(W["b_ada"][None], (0, me * nsh), (1, nsh))
    mod_sh = ada_fwd(c_all, W["w_ada"], b_sh)
    (mod_rows,) = all_to_all([mod_sh[:, None, :]], "scatter_mod")
    mod6 = mod_rows.reshape(6, D)

    after_small = (g_in[0, 0, 0].astype(F32) * 0.0 + mod6[0, 0] * 0.0).astype(BF16)
    late, late_tok = exchange_start([W[n].astype(BF16) + after_small for n in big[1:]], "gather_late_start", False)

    def late_weights(after):
        g_up, g_down, g_a, g_b, g_o = exchange_wait(late, after, "gather_late_wait")
        return dict(w_up=g_up.transpose(1, 0, 2).reshape(D, -1), w_down=g_down.reshape(DFF, D),
                    w_a=g_a.reshape(D, D), w_b=g_b.reshape(D, D), w_o=g_o.reshape(D, D))

    mod6 = mod6 + late_tok
    mod = tuple(mod6[i:i + 1] for i in range(6))

    pending = {}

    def on_grads(group, gd):
        if group == "small":
            pending["small"] = all_to_all([_col_shards(gd[n], W[n].shape[1]) for n in SH_NAMES],
                                          "scatter_small_grads")
            return pending["small"][0][0, 0, 0] * 0.0
        if group == "ffn":
            slabs = [_col_shards(gd["w_up"], w_up.shape[2]), gd["w_down"].reshape(NDEV, -1, D)]
        elif group == "mix":
            slabs = [gd[n].reshape(NDEV, -1, D) for n in ("w_a", "w_b", "w_o")]
        else:
            nin = w_in.shape[2]
            slabs = [jnp.stack([_orig_cols_from_pieces(gd, j * nin, (j + 1) * nin) for j in range(NDEV)], axis=0)]
        pending[group], tok = exchange_start([s.astype(BF16) for s in slabs], "scatter_" + group + "_start", True)
        return tok

    loss, grad_x, dmod, g = local_step(x[0], loss_target[0], mod, wts, small, late_weights, on_grads)

    rep_grads = {n: g[n] for n in REP_NAMES if n != "b_ada"}
    rep_grads["b_ada"] = jnp.concatenate(dmod, axis=1)
    gathered = all_gather([rep_grads[n] for n in REP_NAMES] + [jnp.broadcast_to(loss, (1, LANE))],
                          "gather_small_grads")
    rep_all = dict(zip(REP_NAMES, gathered))
    sh_recv = [p[:, None] for p in pending["small"]]
    small_names = REP_NAMES + SH_NAMES
    sg, sd, sm, sv, loss_row = adamw_small([rep_all[n] for n in REP_NAMES] + sh_recv,
                                           [W3[n] for n in small_names], [M3[n] for n in small_names],
                                           [V3[n] for n in small_names], gathered[-1], "adamw_small")
    loss_total = loss_row[0, 0]

    dmod_all = rep_all["b_ada"][:, 0]
    dmod_sh = lax.dynamic_slice(dmod_all, (0, me * nsh), (NDEV, nsh))
    g_w_ada = ada_wgrad(c_all.T, dmod_sh)

    p_up, p_down = exchange_wait(pending["ffn"], grad_x, "scatter_ffn_wait")
    p_a, p_b, p_o = exchange_wait(pending["mix"], grad_x, "scatter_mix_wait")
    (p_in,) = exchange_wait(pending["in"], grad_x, "scatter_in_wait")
    parts = [p_in, p_up, p_down, p_a, p_b, p_o]

    res = {}
    for n, p in zip(big, parts):
        res[n] = adamw(p, W3[n], M3[n], V3[n], "adamw_" + n)
    res["w_ada"] = adamw(g_w_ada[None], W3["w_ada"], M3["w_ada"], V3["w_ada"], "adamw_w_ada")
    for i, n in enumerate(small_names):
        res[n] = (sg[i], sd[i], sm[i], sv[i])

    order = ("w_ada", "b_ada", "w_in", "b_gate", "conv_a", "a_log", "dt_bias", "norm_a", "rel_bias", "w_branch_a",
             "w_branch_b", "w_o", "ln1_g", "ln1_b", "w_up", "conv_ffn", "b_conv_ffn", "w_down", "ln2_g", "ln2_b")
    outs = [loss_total, grad_x[None]]
    for kind in range(4):
        outs += [res[n][kind] for n in order]
    return tuple(outs)
```

```python
import functools
import math

import numpy as np
import jax
import jax.numpy as jnp
from jax import lax
from jax.experimental import pallas as pl
from jax.experimental.pallas import tpu as pltpu

F32 = jnp.float32
BF16 = jnp.bfloat16
HI = lax.Precision.HIGHEST

D = 1024
CH = 64
AH, ADK = 8, 128
BH, BDH = 16, 64
BPREV = 8
BMAXREL = 256
RELSZ = CH + BMAXREL
DFF = 2816
ALPHA = 2.0 ** 0.25
LN_EPS, RMS_EPS, L2_EPS = 1e-5, 1e-6, 1e-6
NEG = -1e30
LR, B1, B2, AEPS, WD, STEP = 1e-3, 0.9, 0.999, 1e-8, 0.01, 10
NDEV = 8
HALO = 8
LANE = 128
TQ = 512
VMEM_LIMIT = 56 * 1024 * 1024

C_QKVA, C_Z, C_QKVB, C_GATE, C_BA, NCAT = 0, 3072, 4096, 7168, 9216, 9728


def _cparams(n_axes=1, vmem=VMEM_LIMIT):
    return pltpu.CompilerParams(dimension_semantics=("arbitrary",) * n_axes, vmem_limit_bytes=vmem)


def _dg(a, b, ca, cb):
    return lax.dot_general(a.astype(BF16), b.astype(BF16), (((ca,), (cb,)), ((), ())),
                           preferred_element_type=F32)


@jax.custom_vjp
def mm_nn(a, b):
    return _dg(a, b, 1, 0)


@jax.custom_vjp
def mm_nt(a, b):
    return _dg(a, b, 1, 1)


@jax.custom_vjp
def mm_tn(a, b):
    return _dg(a, b, 0, 0)


mm_nn.defvjp(lambda a, b: (mm_nn(a, b), (a, b)),
             lambda r, g: (mm_nt(g, r[1]).astype(r[0].dtype), mm_tn(r[0], g).astype(r[1].dtype)))
mm_nt.defvjp(lambda a, b: (mm_nt(a, b), (a, b)),
             lambda r, g: (mm_nn(g, r[1]).astype(r[0].dtype), mm_tn(g, r[0]).astype(r[1].dtype)))
mm_tn.defvjp(lambda a, b: (mm_tn(a, b), (a, b)),
             lambda r, g: (mm_nt(r[1], g).astype(r[0].dtype), mm_nn(r[0], g).astype(r[1].dtype)))


@jax.custom_vjp
def mm_w(a, w):
    return _dg(a, w, 1, 0)


mm_w.defvjp(lambda a, w: (mm_w(a, w), (a, w)),
            lambda r, g: (mm_nt(g, r[1]).astype(r[0].dtype), jnp.zeros_like(r[1])))


def _mmh(a, b):
    return lax.dot_general(a, b, (((1,), (0,)), ((), ())), precision=HI, preferred_element_type=F32)


def _bdg(a, b, ca, cb):
    return lax.dot_general(a.astype(BF16), b.astype(BF16), (((ca,), (cb,)), ((0,), (0,))),
                           preferred_element_type=F32)


@jax.custom_vjp
def bmm_nn(a, b):
    return _bdg(a, b, 2, 1)


@jax.custom_vjp
def bmm_nt(a, b):
    return _bdg(a, b, 2, 2)


@jax.custom_vjp
def bmm_tn(a, b):
    return _bdg(a, b, 1, 1)


bmm_nn.defvjp(lambda a, b: (bmm_nn(a, b), (a, b)), lambda r, g: (bmm_nt(g, r[1]), bmm_tn(r[0], g)))
bmm_nt.defvjp(lambda a, b: (bmm_nt(a, b), (a, b)), lambda r, g: (bmm_nn(g, r[1]), bmm_tn(g, r[0])))
bmm_tn.defvjp(lambda a, b: (bmm_tn(a, b), (a, b)), lambda r, g: (bmm_nt(r[1], g), bmm_nn(r[0], g)))


def _bdg3(a, b, ca, cb):
    return lax.dot_general(a, b, (((ca,), (cb,)), ((0,), (0,))), precision=lax.Precision.HIGH,
                           preferred_element_type=F32)


NEWTON_STEPS = 2


def _bdgp(a, b, ca, cb):
    return _bdg(a, b, ca, cb)


@jax.custom_vjp
def bmm3_nn(a, b):
    return _bdgp(a, b, 2, 1)


bmm3_nn.defvjp(lambda a, b: (bmm3_nn(a, b), (a, b)),
               lambda r, g: (_bdgp(g, r[1], 2, 2), _bdgp(r[0], g, 1, 1)))


def _sigmoid(x):
    return 0.5 * jnp.tanh(0.5 * x) + 0.5


def _silu(x):
    return x * _sigmoid(x)


def _softplus(x):
    return jnp.maximum(x, 0.0) + jnp.log(1.0 + jnp.exp(-jnp.abs(x)))


def _layernorm(r, g, b):
    mu = jnp.mean(r, axis=-1, keepdims=True)
    xc = r - mu
    var = jnp.mean(xc * xc, axis=-1, keepdims=True)
    return xc * lax.rsqrt(var + LN_EPS) * g + b


def _iota2(shape, dim):
    return lax.broadcasted_iota(jnp.int32, shape, dim)


@jax.custom_vjp
def causal_conv(ext, rows):
    k = len(rows)
    y = None
    for j in range(k):
        s = k - 1 - j
        r = pltpu.roll(ext, s, 0) if s else ext
        t = r[HALO:] * rows[j]
        y = t if y is None else y + t
    return y


def _causal_conv_fwd(ext, rows):
    return causal_conv(ext, rows), (ext, rows)


def _causal_conv_bwd(res, g):
    ext, rows = res
    n = ext.shape[0]
    k = len(rows)
    gext = jnp.concatenate([jnp.zeros((HALO, g.shape[1]), g.dtype), g], axis=0)
    dext = None
    drows = []
    for j in range(k):
        s = k - 1 - j
        up = pltpu.roll(gext, n - s, 0) if s else gext
        t = up * rows[j]
        dext = t if dext is None else dext + t
        r = pltpu.roll(ext, s, 0) if s else ext
        drows.append(jnp.sum(g * r[HALO:], axis=0, keepdims=True))
    return dext, tuple(drows)


causal_conv.defvjp(_causal_conv_fwd, _causal_conv_bwd)


def _chunk_masks(tm):
    i = _iota2((tm, tm), 0)
    j = _iota2((tm, tm), 1)
    same = (i ^ j) < CH
    lower = jnp.where(same & (j <= i), 1.0, 0.0).astype(F32)
    upper = jnp.where(same & (i <= j), 1.0, 0.0).astype(F32)
    return lower, upper


@jax.custom_vjp
def chunk_cumsum(g):
    lower, _ = _chunk_masks(g.shape[0])
    return _mmh(lower, g)


def _chunk_cumsum_bwd(_, ct):
    _, upper = _chunk_masks(ct.shape[0])
    return (_mmh(upper, ct),)


chunk_cumsum.defvjp(lambda g: (chunk_cumsum(g), None), _chunk_cumsum_bwd)


@jax.custom_vjp
def inv_unit_lower(a):
    n = a.shape[-1]
    eye = jnp.where(_iota2((1, n, n), 1) == _iota2((1, n, n), 2), 1.0, 0.0).astype(F32)
    x = eye - a
    p = _bdg3(a, a, 2, 1)
    steps = int(math.log2(n)) - 1
    for s in range(steps):
        x = x + _bdg3(x, p, 2, 1)
        if s + 1 < steps:
            p = _bdg3(p, p, 2, 1)
    for _ in range(NEWTON_STEPS):
        r = (eye - x) - _bdg3(a, x, 2, 1)
        x = x + _bdg3(x, r, 2, 1)
    return x


def _inv_fwd(a):
    t = inv_unit_lower(a)
    return t, t


def _inv_bwd(t, g):
    return (-_bdgp(_bdgp(t, g, 1, 1), t, 2, 2),)


inv_unit_lower.defvjp(_inv_fwd, _inv_bwd)


@jax.custom_vjp
def inv_known(a, t):
    return t


inv_known.defvjp(lambda a, t: (t, t), lambda t, g: (_inv_bwd(t, g)[0], jnp.zeros_like(t)))


def prep_head_fn(ext, rows, scale):
    s = _silu(causal_conv(ext, rows))
    if scale is None:
        return s
    return s * (lax.rsqrt(jnp.sum(s * s, axis=-1, keepdims=True) + L2_EPS) * scale)


def prep_gate_fn(bb, aa, a_log, dtb):
    g = -jnp.exp(a_log) * _softplus(aa + dtb)
    return chunk_cumsum(g), _sigmoid(bb)


PREP_SCALES = (ADK ** -0.5, 1.0, None)


def _head_cols(a, heads):
    lane = _iota2((1, LANE), 1)
    return jnp.concatenate([jnp.sum(jnp.where(lane == h, a, 0.0), axis=1, keepdims=True)[None]
                            for h in heads], axis=0)


def _head_rows(a, heads):
    at = a.T[:AH]
    sub = _iota2((AH, 1), 0)
    return jnp.concatenate([jnp.sum(jnp.where(sub == h, at, 0.0), axis=0, keepdims=True)[None]
                            for h in heads], axis=0)


def c1_heads(q, k, v, gcs, beta, tinv_saved=None):
    heads = range(AH)
    gcol = jnp.concatenate([_head_cols(g, heads) for g in gcs], axis=0)
    grow = jnp.concatenate([_head_rows(g, heads) for g in gcs], axis=0)
    bcol = jnp.concatenate([_head_cols(b, heads) for b in beta], axis=0)
    i = _iota2((1, CH, CH), 1)
    j = _iota2((1, CH, CH), 2)
    causal = j <= i
    strict = j < i
    diff = gcol - grow
    decay = jnp.where(causal, jnp.exp(jnp.where(causal, diff, 0.0)), 0.0)
    kb = k * bcol
    vb = v * bcol
    a_low = jnp.where(strict, bmm_nt(kb, k) * decay, 0.0)
    tinv = inv_unit_lower(a_low) if tinv_saved is None else inv_known(a_low, tinv_saved)
    egc = jnp.exp(gcol)
    u = bmm3_nn(tinv, vb)
    w = bmm3_nn(tinv, kb * egc)
    qk = jnp.where(causal, bmm_nt(q, k) * decay, 0.0)
    glast = jnp.sum(jnp.where(_iota2((1, CH, 1), 1) == CH - 1, gcol, 0.0), axis=1, keepdims=True)
    qg = q * egc
    kd = k * jnp.exp(glast - gcol)
    eg = jnp.exp(glast) * jnp.ones((1, 1, ADK), F32)
    return u, w, qk, qg, kd, eg, tinv


def c2_heads(s, u, w, qk, qg, kd, eg, z, nw):
    vn = u - bmm_nn(w, s)
    o = bmm_nn(qg, s) + bmm_nn(qk, vn)
    s2 = s * eg + bmm_tn(kd, vn)
    ms = jnp.mean(o * o, axis=-1, keepdims=True)
    og = o * lax.rsqrt(ms + RMS_EPS) * nw * _silu(z)
    return og, s2


ATT_SCALE = BDH ** -0.5


def _head_mask(hh):
    lane = _iota2((1, 2 * BDH), 1)
    return jnp.where((lane >= hh * BDH) & (lane < (hh + 1) * BDH), 1.0, 0.0).astype(F32)


def attn_sub_fwd(q, k, v, bias2, r, firstf):
    col = _iota2((1, KWIN), 1) + r * SUBQ
    nokey = jnp.where(col < TQ, firstf, 0.0) * NEG
    out, probs = None, []
    for hh in range(2):
        hm = _head_mask(hh)
        s = mm_nt(q * (hm * ATT_SCALE), k) + (assemble_bias(bias2[hh], r) + nokey)
        p = jnp.exp(s - jnp.max(s, axis=-1, keepdims=True))
        inv = 1.0 / jnp.sum(p, axis=-1, keepdims=True)
        o = mm_nn(p, v) * (inv * hm)
        out = o if out is None else out + o
        probs.append(p * inv)
    return out, probs


def attn_sub_bwd(q, k, v, o, do, probs, r):
    dq, dk, dv, dss = None, None, None, []
    for hh in range(2):
        hm = _head_mask(hh)
        p = probs[hh]
        doh = do * hm
        ds = p * (mm_nt(doh, v) - jnp.sum(doh * o, axis=-1, keepdims=True))
        dqh = mm_nn(ds, k) * (hm * ATT_SCALE)
        dkh = mm_tn(ds, q * (hm * ATT_SCALE))
        dvh = mm_tn(p, doh)
        dq = dqh if dq is None else dq + dqh
        dk = dkh if dk is None else dk + dkh
        dv = dvh if dv is None else dv + dvh
        dss.append(ds)
    return dq, dk, dv, dss


def merge_fn(x, oa, ob, gra, grb, p_pa, p_pb, p_mix, bga, bgb, gate_t, g1, b1, scale_f, shift_f,
             wa, wb, wo):
    ga = _sigmoid(gra + bga)
    gb = _sigmoid(grb + bgb)
    pa = mm_w(oa, wa) + p_pa
    pb = mm_w(ob, wb) + p_pb
    merged = ga * pa + gb * pb
    mix = mm_w(merged, wo) + p_mix
    y1 = _layernorm(ALPHA * x + gate_t * mix, g1, b1)
    return y1, merged


def ffn_act_fn(extg, extv, rows_g, rows_v, bg, bv):
    return _silu(causal_conv(extg, rows_g) + bg) * (causal_conv(extv, rows_v) + bv)


def head_fn(a, y1, p_ffn, gate_f, g2, b2, tgt, wd):
    ffn = mm_w(a, wd) + p_ffn
    y2 = _layernorm(ALPHA * y1 + gate_f * ffn, g2, b2)
    err = y2 - tgt
    return 0.5 * jnp.sum(jnp.mean(err * err, axis=-1, keepdims=True))


def _rows(tm, width, colblk=0, order=None):
    if order is None:
        return pl.BlockSpec((tm, width), lambda i: (i, colblk))
    return pl.BlockSpec((tm, width), lambda i: (order(i), colblk))


def _const(shape):
    nd = len(shape)
    return pl.BlockSpec(shape, lambda *_: (0,) * nd)


def _pick(n, cands):
    for c in cands:
        if n % c == 0:
            return c
    raise ValueError(f"no tile for {n}")


def _tile(n, cap):
    best = None
    for c in range(LANE, min(n, cap) + 1, LANE):
        if n % c == 0:
            best = c
    if best is None:
        raise ValueError(f"no tile for {n}")
    return best


def _onehot_rows(k, j):
    return jnp.where(_iota2((k, 1), 0) == j, 1.0, 0.0).astype(F32)


def _stack_rows(drows):
    k = len(drows)
    out = None
    for j in range(k):
        tj = _onehot_rows(k, j) * drows[j]
        out = tj if out is None else out + tj
    return out


def matmul(a, w, out_dtype, name, ta=False, tb=False):
    kdim, m = a.shape if ta else a.shape[::-1]
    n = w.shape[0] if tb else w.shape[1]
    tm = _tile(m, 2048 if kdim <= 1024 else 1024)
    tn = _tile(n, 1024)
    tk = _tile(kdim, 2560)
    nk = kdim // tk
    a_spec = (pl.BlockSpec((tk, tm), lambda i, j, k: (k, i)) if ta
              else pl.BlockSpec((tm, tk), lambda i, j, k: (i, k)))
    w_spec = (pl.BlockSpec((tn, tk), lambda i, j, k: (j, k)) if tb
              else pl.BlockSpec((tk, tn), lambda i, j, k: (k, j)))

    def body(a_ref, w_ref, o_ref, *scratch):
        p = _dg(a_ref[...], w_ref[...], 0 if ta else 1, 1 if tb else 0)
        if nk == 1:
            o_ref[...] = p.astype(out_dtype)
            return
        acc = scratch[0]
        k = pl.program_id(2)

        @pl.when(k == 0)
        def _():
            acc[...] = p

        @pl.when(k > 0)
        def _():
            acc[...] += p

        @pl.when(k == nk - 1)
        def _():
            o_ref[...] = acc[...].astype(out_dtype)

    return pl.pallas_call(
        body, name=name,
        grid=(m // tm, n // tn, nk),
        in_specs=[a_spec, w_spec],
        out_specs=pl.BlockSpec((tm, tn), lambda i, j, k: (i, j)),
        out_shape=jax.ShapeDtypeStruct((m, n), out_dtype),
        scratch_shapes=[] if nk == 1 else [pltpu.VMEM((tm, tn), F32)],
        compiler_params=_cparams(3),
    )(a, w)


def dgrad_modulated(a, w, xin, dres, scale, name):
    m, kdim = a.shape
    n = w.shape[0]
    tm = _tile(m, 1024)
    tk = _tile(kdim, 2560)
    nk = kdim // tk
    assert nk > 1

    def body(a_ref, w_ref, x_ref, r_ref, sc_ref, o_ref, dsc_ref, dsh_ref, acc):
        i = pl.program_id(0)
        k = pl.program_id(1)
        p = _dg(a_ref[...], w_ref[...], 1, 1)

        @pl.when(k == 0)
        def _():
            acc[...] = p

        @pl.when(k > 0)
        def _():
            acc[...] += p

        @pl.when((i == 0) & (k == 0))
        def _():
            dsc_ref[...] = jnp.zeros_like(dsc_ref)
            dsh_ref[...] = jnp.zeros_like(dsh_ref)

        @pl.when(k == nk - 1)
        def _():
            dh = acc[...]
            o_ref[...] = r_ref[...] + dh * (1.0 + sc_ref[...])
            dsc_ref[...] += jnp.sum(dh * x_ref[...], axis=0, keepdims=True)
            dsh_ref[...] += jnp.sum(dh, axis=0, keepdims=True)

    row = pl.BlockSpec((tm, n), lambda i, k: (i, 0))
    vec = pl.BlockSpec((1, n), lambda i, k: (0, 0))
    return pl.pallas_call(
        body, name=name, grid=(m // tm, nk),
        in_specs=[pl.BlockSpec((tm, tk), lambda i, k: (i, k)), pl.BlockSpec((n, tk), lambda i, k: (0, k)),
                  row, row, vec],
        out_specs=[row, vec, vec],
        out_shape=[jax.ShapeDtypeStruct((m, n), F32), jax.ShapeDtypeStruct((1, n), F32),
                   jax.ShapeDtypeStruct((1, n), F32)],
        scratch_shapes=[pltpu.VMEM((tm, n), F32)],
        compiler_params=_cparams(2),
    )(a, w, xin, dres, scale)


def dgrad_pieces(pieces, tail, w, name):
    m = pieces[0][0].shape[0]
    n, ktot = w.shape
    tk = 1024
    tm = _tile(m, 1024)
    wt = tail.shape[1]
    ranges, k0 = [], 0
    for arr, off in pieces:
        assert off == k0 * tk and arr.shape[1] % tk == 0
        ranges.append((k0, k0 + arr.shape[1] // tk))
        k0 = ranges[-1][1]
    nk = k0
    npc = len(pieces)

    def body(*refs):
        a_refs, t_ref, w_ref, wt_ref, o_ref, acc = refs[:npc], refs[npc], refs[npc + 1], refs[npc + 2], refs[npc + 3], refs[npc + 4]
        k = pl.program_id(1)

        @pl.when(k == 0)
        def _():
            acc[...] = _dg(t_ref[...], wt_ref[...], 1, 1)

        for a_ref, (lo, hi) in zip(a_refs, ranges):
            @pl.when((k >= lo) & (k < hi))
            def _(a_ref=a_ref):
                acc[...] += _dg(a_ref[...], w_ref[...], 1, 1)

        @pl.when(k == nk - 1)
        def _():
            o_ref[...] = acc[...]

    def piece_spec(lo, hi):
        return pl.BlockSpec((tm, tk), lambda i, k: (i, jnp.clip(k - lo, 0, hi - lo - 1)))

    return pl.pallas_call(
        body, name=name, grid=(m // tm, nk),
        in_specs=[piece_spec(lo, hi) for lo, hi in ranges] + [
            pl.BlockSpec((tm, wt), lambda i, k: (i, 0)),
            pl.BlockSpec((n, tk), lambda i, k: (0, k)),
            pl.BlockSpec((n, wt), lambda i, k: (0, (ktot - wt) // wt))],
        out_specs=pl.BlockSpec((tm, n), lambda i, k: (i, 0)),
        out_shape=jax.ShapeDtypeStruct((m, n), F32),
        scratch_shapes=[pltpu.VMEM((tm, n), F32)],
        compiler_params=_cparams(2),
    )(*[a for a, _ in pieces], tail, w, w)


def modulate(x, scale, shift, name):
    t, d = x.shape
    tm = _pick(t, (512, 256, 128))

    def body(x_ref, sc_ref, sh_ref, o_ref):
        o_ref[...] = (x_ref[...] * (1.0 + sc_ref[...]) + sh_ref[...]).astype(BF16)

    return pl.pallas_call(
        body, name=name, grid=(t // tm,),
        in_specs=[_rows(tm, d), _const((1, d)), _const((1, d))],
        out_specs=_rows(tm, d),
        out_shape=jax.ShapeDtypeStruct((t, d), BF16),
        compiler_params=_cparams(),
    )(x, scale, shift)


def modulate_bwd(dh, xin, dres, scale, name):
    t, d = dh.shape
    tm = _pick(t, (512, 256, 128))

    def body(dh_ref, x_ref, dr_ref, sc_ref, o_ref, dsc_ref, dsh_ref):
        i = pl.program_id(0)
        dh_v = dh_ref[...]
        o_ref[...] = dr_ref[...] + dh_v * (1.0 + sc_ref[...])

        @pl.when(i == 0)
        def _():
            dsc_ref[...] = jnp.zeros_like(dsc_ref)
            dsh_ref[...] = jnp.zeros_like(dsh_ref)

        dsc_ref[...] += jnp.sum(dh_v * x_ref[...], axis=0, keepdims=True)
        dsh_ref[...] += jnp.sum(dh_v, axis=0, keepdims=True)

    return pl.pallas_call(
        body, name=name, grid=(t // tm,),
        in_specs=[_rows(tm, d), _rows(tm, d), _rows(tm, d), _const((1, d))],
        out_specs=[_rows(tm, d), _const((1, d)), _const((1, d))],
        out_shape=[jax.ShapeDtypeStruct((t, d), F32), jax.ShapeDtypeStruct((1, d), F32),
                   jax.ShapeDtypeStruct((1, d), F32)],
        compiler_params=_cparams(),
    )(dh, xin, dres, scale)


PREP_TM = 128


def _halo_specs(tm, width, colblk, order):
    per = tm // HALO
    return [pl.BlockSpec((HALO, width), lambda i: (jnp.maximum(order(i) * per - 1, 0), colblk)),
            pl.BlockSpec((tm, width), lambda i: (order(i), colblk))]


def prep_fwd(proj, conv_a, a_log, dtb):
    t = proj.shape[0]
    tm = PREP_TM
    nt = t // tm
    wq = 3 * D

    def body(prev_ref, cur_ref, bb_ref, aa_ref, cw_ref, al_ref, dt_ref, q_ref, k_ref, v_ref, g_ref, b_ref):
        i = pl.program_id(0)
        flag = jnp.where(i > 0, 1.0, 0.0)
        for part, o_ref in enumerate((q_ref, k_ref, v_ref)):
            for h in range(AH):
                sl = slice(part * D + h * ADK, part * D + (h + 1) * ADK)
                ext = jnp.concatenate([prev_ref[:, sl] * flag, cur_ref[:, sl]], axis=0)
                rows = tuple(cw_ref[j:j + 1, sl] for j in range(4))
                o_ref[h] = prep_head_fn(ext, rows, PREP_SCALES[part])
        gcs, beta = prep_gate_fn(bb_ref[...], aa_ref[...], al_ref[...], dt_ref[...])
        g_ref[...] = gcs
        b_ref[...] = beta

    ident = lambda i: i
    hm = pl.BlockSpec((AH, tm, ADK), lambda i: (0, i, 0))
    return pl.pallas_call(
        body, name="prep_fwd", grid=(nt,),
        in_specs=_halo_specs(tm, wq, 0, ident) + [
            _rows(tm, 128, C_BA // 128), _rows(tm, 128, C_BA // 128 + 1),
            _const((4, wq)), _const((1, 128)), _const((1, 128))],
        out_specs=[hm, hm, hm, _rows(tm, 128), _rows(tm, 128)],
        out_shape=[jax.ShapeDtypeStruct((AH, t, ADK), F32)] * 3 + [jax.ShapeDtypeStruct((t, 128), F32)] * 2,
        compiler_params=_cparams(),
    )(proj, proj, proj, proj, conv_a, a_log, dtb)


def prep_bwd(proj, conv_a, a_log, dtb, dq, dk, dv, dgcs, dbeta):
    t = proj.shape[0]
    tm = PREP_TM
    nt = t // tm
    wq = 3 * D
    rev = lambda i: nt - 1 - i

    def body(prev_ref, cur_ref, bb_ref, aa_ref, cw_ref, al_ref, dt_ref,
             dq_ref, dk_ref, dv_ref, dg_ref, db_ref,
             dpre_ref, dbb_ref, daa_ref, dcw_ref, dal_ref, ddt_ref, carry):
        i = pl.program_id(0)
        flag = jnp.where(i < nt - 1, 1.0, 0.0)

        @pl.when(i == 0)
        def _():
            carry[...] = jnp.zeros_like(carry)
            dcw_ref[...] = jnp.zeros_like(dcw_ref)
            dal_ref[...] = jnp.zeros_like(dal_ref)
            ddt_ref[...] = jnp.zeros_like(ddt_ref)

        for part, d_ref in enumerate((dq_ref, dk_ref, dv_ref)):
            for h in range(AH):
                sl = slice(part * D + h * ADK, part * D + (h + 1) * ADK)
                ext = jnp.concatenate([prev_ref[:, sl] * flag, cur_ref[:, sl]], axis=0)
                rows = tuple(cw_ref[j:j + 1, sl] for j in range(4))
                _, vjp = jax.vjp(lambda e, r: prep_head_fn(e, r, PREP_SCALES[part]), ext, rows)
                dext, drows = vjp(d_ref[h])
                dcur = dext[HALO:]
                dpre_ref[:, sl] = jnp.concatenate([dcur[:tm - HALO], dcur[tm - HALO:] + carry[:, sl]],
                                                  axis=0).astype(BF16)
                carry[:, sl] = dext[:HALO]
                dcw_ref[:, sl] += _stack_rows(drows)
        _, vjp = jax.vjp(prep_gate_fn, bb_ref[...], aa_ref[...], al_ref[...], dt_ref[...])
        dbb, daa, dal, ddt = vjp((dg_ref[...], db_ref[...]))
        dbb_ref[...] = dbb.astype(BF16)
        daa_ref[...] = daa.astype(BF16)
        dal_ref[...] += dal
        ddt_ref[...] += ddt

    hm = pl.BlockSpec((AH, tm, ADK), lambda i: (0, rev(i), 0))
    return pl.pallas_call(
        body, name="prep_bwd", grid=(nt,),
        in_specs=_halo_specs(tm, wq, 0, rev) + [
            _rows(tm, 128, C_BA // 128, rev), _rows(tm, 128, C_BA // 128 + 1, rev),
            _const((4, wq)), _const((1, 128)), _const((1, 128)),
            hm, hm, hm, _rows(tm, 128, 0, rev), _rows(tm, 128, 0, rev)],
        out_specs=[_rows(tm, wq, 0, rev), _rows(tm, 128, 0, rev), _rows(tm, 128, 0, rev),
                   _const((4, wq)), _const((1, 128)), _const((1, 128))],
        out_shape=[jax.ShapeDtypeStruct((t, wq), BF16), jax.ShapeDtypeStruct((t, 128), BF16),
                   jax.ShapeDtypeStruct((t, 128), BF16), jax.ShapeDtypeStruct((4, wq), F32),
                   jax.ShapeDtypeStruct((1, 128), F32), jax.ShapeDtypeStruct((1, 128), F32)],
        scratch_shapes=[pltpu.VMEM((HALO, wq), F32)],
        compiler_params=_cparams(),
    )(proj, proj, proj, proj, conv_a, a_log, dtb, dq, dk, dv, dgcs, dbeta)


def _c1_specs(order, cps=1):
    hm = pl.BlockSpec((AH, cps * CH, ADK), lambda n: (0, order(n), 0))
    col = pl.BlockSpec((cps * CH, LANE), lambda n: (order(n), 0))
    qk = pl.BlockSpec((cps, AH, CH, CH), lambda n: (order(n), 0, 0, 0))
    eg = pl.BlockSpec((cps, AH, 1, ADK), lambda n: (order(n), 0, 0, 0))
    return hm, col, qk, eg


C1_CPS = 4


def _heads(ref, rows=slice(None)):
    return jnp.stack([ref[rows, h * ADK:(h + 1) * ADK] for h in range(AH)], axis=0)


C2_CPS = 4


def _chunks(ref):
    return jnp.concatenate([ref[:, c * CH:(c + 1) * CH] for c in range(C1_CPS)], axis=0)


def _chunk_rows(ref):
    return [ref[c * CH:(c + 1) * CH] for c in range(C1_CPS)]


def c1_fwd(q, k, v, gcs, beta):
    t = q.shape[1]
    nc = t // CH
    hm, col, qks, egs = _c1_specs(lambda n: n, C1_CPS)

    def body(q_ref, k_ref, v_ref, g_ref, b_ref, u_ref, w_ref, qg_ref, kd_ref, qk_ref, eg_ref, ti_ref):
        u, w, qk, qg, kd, eg, tinv = c1_heads(_chunks(q_ref), _chunks(k_ref), _chunks(v_ref), _chunk_rows(g_ref),
                                              _chunk_rows(b_ref))
        for c in range(C1_CPS):
            rows, hs = slice(c * CH, (c + 1) * CH), slice(c * AH, (c + 1) * AH)
            u_ref[:, rows] = u[hs]
            w_ref[:, rows] = w[hs].astype(BF16)
            qg_ref[:, rows] = qg[hs].astype(BF16)
            kd_ref[:, rows] = kd[hs].astype(BF16)
            qk_ref[c] = qk[hs].astype(BF16)
            eg_ref[c] = eg[hs]
            ti_ref[c] = tinv[hs]

    return pl.pallas_call(
        body, name="c1_fwd", grid=(nc // C1_CPS,),
        in_specs=[hm, hm, hm, col, col],
        out_specs=[hm, hm, hm, hm, qks, egs, qks],
        out_shape=[jax.ShapeDtypeStruct((AH, t, ADK), F32)] + [jax.ShapeDtypeStruct((AH, t, ADK), BF16)] * 3 + [
            jax.ShapeDtypeStruct((nc, AH, CH, CH), BF16), jax.ShapeDtypeStruct((nc, AH, 1, ADK), F32),
            jax.ShapeDtypeStruct((nc, AH, CH, CH), F32)],
        compiler_params=_cparams(),
    )(q, k, v, gcs, beta)


def c1_bwd(q, k, v, gcs, beta, tinv, du, dw, dqg, dkd, dqk, deg):
    t = q.shape[1]
    nc = t // CH
    hm, col, qks, egs = _c1_specs(lambda n: n, C1_CPS)

    def lead(ref):
        return jnp.concatenate([ref[c] for c in range(C1_CPS)], axis=0)

    def body(q_ref, k_ref, v_ref, g_ref, b_ref, ti_ref, du_ref, dw_ref, dqg_ref, dkd_ref, dqk_ref, deg_ref,
             dq_ref, dk_ref, dv_ref, dg_ref, db_ref):
        tinv = lead(ti_ref)
        _, vjp = jax.vjp(lambda q_, k_, v_, g_, b_: c1_heads(q_, k_, v_, g_, b_, tinv),
                         _chunks(q_ref), _chunks(k_ref), _chunks(v_ref), _chunk_rows(g_ref), _chunk_rows(b_ref))
        dq, dk, dv, dg, db = vjp((_chunks(du_ref).astype(F32), _chunks(dw_ref).astype(F32), lead(dqk_ref),
                                  _chunks(dqg_ref), _chunks(dkd_ref), lead(deg_ref),
                                  jnp.zeros((C1_CPS * AH, CH, CH), F32)))
        for c in range(C1_CPS):
            rows, hs = slice(c * CH, (c + 1) * CH), slice(c * AH, (c + 1) * AH)
            dq_ref[:, rows] = dq[hs]
            dk_ref[:, rows] = dk[hs]
            dv_ref[:, rows] = dv[hs]
            dg_ref[rows] = dg[c]
            db_ref[rows] = db[c]

    return pl.pallas_call(
        body, name="c1_bwd", grid=(nc // C1_CPS,),
        in_specs=[hm, hm, hm, col, col, qks, hm, hm, hm, hm, qks, egs],
        out_specs=[hm, hm, hm, col, col],
        out_shape=[jax.ShapeDtypeStruct((AH, t, ADK), F32)] * 3 + [jax.ShapeDtypeStruct((t, LANE), F32)] * 2,
        compiler_params=_cparams(),
    )(q, k, v, gcs, beta, tinv, du, dw, dqg, dkd, dqk, deg)


def c2_fwd(u, w, qg, kd, qk, eg, proj, norm_a):
    t = u.shape[1]
    nc = t // CH
    cps = C2_CPS
    hm, _, qks, egs = _c1_specs(lambda n: n, cps)
    tok = pl.BlockSpec((cps * CH, D), lambda n: (n, 0))
    zspec = pl.BlockSpec((cps * CH, D), lambda n: (n, C_Z // D))
    sspec = pl.BlockSpec((cps, AH, ADK, ADK), lambda n: (n, 0, 0, 0))

    def body(u_ref, w_ref, qg_ref, kd_ref, qk_ref, eg_ref, z_ref, nw_ref, o_ref, sall_ref, st):
        n = pl.program_id(0)

        @pl.when(n == 0)
        def _():
            st[...] = jnp.zeros_like(st)

        for c in range(cps):
            rows = slice(c * CH, (c + 1) * CH)
            s = st[...]
            sall_ref[c] = s
            og, s2 = c2_heads(s, u_ref[:, rows], w_ref[:, rows], qk_ref[c], qg_ref[:, rows], kd_ref[:, rows],
                              eg_ref[c], _heads(z_ref, rows), nw_ref[...])
            st[...] = s2
            for h in range(AH):
                o_ref[rows, h * ADK:(h + 1) * ADK] = og[h].astype(BF16)

    return pl.pallas_call(
        body, name="c2_fwd", grid=(nc // cps,),
        in_specs=[hm, hm, hm, hm, qks, egs, zspec, _const((1, ADK))],
        out_specs=[tok, sspec],
        out_shape=[jax.ShapeDtypeStruct((t, D), BF16), jax.ShapeDtypeStruct((nc, AH, ADK, ADK), F32)],
        scratch_shapes=[pltpu.VMEM((AH, ADK, ADK), F32)],
        compiler_params=_cparams(),
    )(u, w, qg, kd, qk, eg, proj, norm_a)


def c2_bwd(u, w, qg, kd, qk, eg, proj, norm_a, sall, do):
    t = u.shape[1]
    nc = t // CH
    cps = C2_CPS
    rev = lambda n: nc // cps - 1 - n
    hm, _, qks, egs = _c1_specs(rev, cps)
    tok = pl.BlockSpec((cps * CH, D), lambda n: (rev(n), 0))
    zspec = pl.BlockSpec((cps * CH, D), lambda n: (rev(n), C_Z // D))
    sspec = pl.BlockSpec((cps, AH, ADK, ADK), lambda n: (rev(n), 0, 0, 0))

    def body(u_ref, w_ref, qg_ref, kd_ref, qk_ref, eg_ref, z_ref, nw_ref, sall_ref, do_ref,
             du_ref, dw_ref, dqg_ref, dkd_ref, dqk_ref, deg_ref, dz_ref, dnw_ref, dst):
        n = pl.program_id(0)

        @pl.when(n == 0)
        def _():
            dst[...] = jnp.zeros_like(dst)
            dnw_ref[...] = jnp.zeros_like(dnw_ref)

        for c in reversed(range(cps)):
            rows = slice(c * CH, (c + 1) * CH)
            _, vjp = jax.vjp(c2_heads, sall_ref[c], u_ref[:, rows], w_ref[:, rows].astype(F32),
                             qk_ref[c].astype(F32), qg_ref[:, rows].astype(F32), kd_ref[:, rows].astype(F32),
                             eg_ref[c], _heads(z_ref, rows), nw_ref[...])
            ds, du, dw, dqk, dqg, dkd, deg, dz, dn = vjp((_heads(do_ref, rows), dst[...]))
            dst[...] = ds
            du_ref[:, rows] = du.astype(BF16)
            dw_ref[:, rows] = dw.astype(BF16)
            dqg_ref[:, rows] = dqg
            dkd_ref[:, rows] = dkd
            dqk_ref[c] = dqk
            deg_ref[c] = deg
            for h in range(AH):
                dz_ref[rows, h * ADK:(h + 1) * ADK] = dz[h].astype(BF16)
            dnw_ref[...] += dn

    return pl.pallas_call(
        body, name="c2_bwd", grid=(nc // cps,),
        in_specs=[hm, hm, hm, hm, qks, egs, zspec, _const((1, ADK)), sspec, tok],
        out_specs=[hm, hm, hm, hm, qks, egs, tok, _const((1, ADK))],
        out_shape=[jax.ShapeDtypeStruct((AH, t, ADK), BF16)] * 2 + [jax.ShapeDtypeStruct((AH, t, ADK), F32)] * 2 + [
            jax.ShapeDtypeStruct((nc, AH, CH, CH), F32), jax.ShapeDtypeStruct((nc, AH, 1, ADK), F32),
            jax.ShapeDtypeStruct((t, D), BF16), jax.ShapeDtypeStruct((1, ADK), F32)],
        scratch_shapes=[pltpu.VMEM((AH, ADK, ADK), F32)],
        compiler_params=_cparams(),
    )(u, w, qg, kd, qk, eg, proj, norm_a, sall, do)


NQB = TQ // CH
NKB = 2 * TQ // CH
NDIST = BPREV + 1
KLO = -(NQB - 2)
NPAIR = NKB - 1 - KLO + 1


def bias_table(rel_bias):
    nh = rel_bias.shape[0]
    relx = jnp.concatenate([rel_bias, jnp.broadcast_to(rel_bias[:, -1:], (nh, CH * BPREV + 2 * CH - 1 - RELSZ))],
                           axis=1)
    t = jnp.stack([relx[:, CH * k:CH * k + 2 * CH - 1] for k in range(NDIST)], axis=1)
    trev = t[:, :, ::-1]
    g2 = jnp.concatenate([trev[:, :, CH - 1:], jnp.zeros((nh, NDIST, 1), F32), trev[:, :, :CH - 1]], axis=2)
    flat = jnp.tile(g2, (1, 1, CH + 1))[:, :, :CH * (2 * CH - 1)]
    blk = flat.reshape(nh, NDIST, CH, 2 * CH - 1)[..., :CH]
    neg = jnp.full((nh, NQB - 1, CH, CH), NEG, F32)
    asc = jnp.concatenate([neg, blk, neg], axis=1)
    return jnp.concatenate([asc[:, 1:], asc[:, :-1]], axis=-1)


SUBQ = 4 * CH
NSUB = TQ // SUBQ
KWIN = SUBQ + BPREV * CH


def assemble_bias(tab, r):
    b0 = r * SUBQ // (2 * CH)
    rows = [jnp.concatenate([tab[NQB + a - 2 * b - KLO] for b in range(b0, b0 + KWIN // (2 * CH))], axis=1)
            for a in range(r * SUBQ // CH, (r + 1) * SUBQ // CH)]
    return jnp.concatenate(rows, axis=0)


def bias_table_bwd_layout(dtab):
    nh = dtab.shape[0]
    dasc = (jnp.pad(dtab[..., :CH], ((0, 0), (1, 0), (0, 0), (0, 0)))
            + jnp.pad(dtab[..., CH:], ((0, 0), (0, 1), (0, 0), (0, 0))))
    dblk = dasc[:, NQB - 1:NQB - 1 + NDIST]
    dr = jnp.pad(dblk, ((0, 0), (0, 0), (0, 0), (0, CH - 1)))
    flat = jnp.pad(dr.reshape(nh, NDIST, CH * (2 * CH - 1)), ((0, 0), (0, 0), (0, 3 * CH)))
    return flat.reshape(nh, NDIST, CH + 1, 2 * CH).transpose(0, 2, 1, 3).reshape(nh, CH + 1, NDIST * 2 * CH)


def _fold_matrix_np():
    f = np.zeros((NDIST * 2 * CH, 384), np.float32)
    for k in range(NDIST):
        s = k
        for xx in range(2 * CH):
            if xx == CH:
                continue
            m = CH - 1 - xx if xx < CH else 3 * CH - 1 - xx
            f[s * 2 * CH + xx, min(CH * k + m, RELSZ - 1)] = 1.0
    return f


def relbias_reduce(dlay):
    nh, rows, cols = dlay.shape
    rpad = (-rows) % 8
    dlay = jnp.pad(dlay, ((0, 0), (0, rpad), (0, 0)))
    fold = jnp.asarray(_fold_matrix_np())

    def body(d_ref, f_ref, o_ref):
        cs = jnp.sum(d_ref[0], axis=0, keepdims=True)
        o_ref[0] = _mmh(jnp.broadcast_to(cs, (8, cols)), f_ref[...])

    out = pl.pallas_call(
        body, name="relbias_reduce", grid=(nh,),
        in_specs=[pl.BlockSpec((1, rows + rpad, cols), lambda h: (h, 0, 0)), _const((cols, 384))],
        out_specs=pl.BlockSpec((1, 8, 384), lambda h: (h, 0, 0)),
        out_shape=jax.ShapeDtypeStruct((nh, 8, 384), F32),
        compiler_params=_cparams(),
    )(dlay, fold)
    return out[:, 0, :RELSZ]


def attn_fwd(proj, bias):
    t = proj.shape[0]
    nt = t // TQ
    cb = C_QKVB // 128

    def body(q_ref, kp_ref, kc_ref, vp_ref, vc_ref, b_ref, o_ref, p_ref):
        i = pl.program_id(1)
        firstf = jnp.where(i == 0, 1.0, 0.0)
        for r in range(NSUB):
            lo, hi = r * SUBQ, r * SUBQ + KWIN - TQ
            kw = jnp.concatenate([kp_ref[lo:, :], kc_ref[:hi, :]], axis=0)
            vw = jnp.concatenate([vp_ref[lo:, :], vc_ref[:hi, :]], axis=0)
            out, probs = attn_sub_fwd(q_ref[lo:lo + SUBQ, :].astype(F32), kw, vw, b_ref[...], r, firstf)
            o_ref[lo:lo + SUBQ, :] = out.astype(BF16)
            for hh in range(2):
                p_ref[hh, lo:lo + SUBQ, :] = probs[hh].astype(BF16)

    def blk(off, prev):
        if prev:
            return pl.BlockSpec((TQ, 128), lambda p, i: (jnp.maximum(i - 1, 0), cb + off + p))
        return pl.BlockSpec((TQ, 128), lambda p, i: (i, cb + off + p))

    return pl.pallas_call(
        body, name="attn_fwd", grid=(BH // 2, nt),
        in_specs=[blk(0, False), blk(8, True), blk(8, False), blk(16, True), blk(16, False),
                  pl.BlockSpec((2, NPAIR, CH, 2 * CH), lambda p, i: (p, 0, 0, 0))],
        out_specs=[pl.BlockSpec((TQ, 128), lambda p, i: (i, p)),
                   pl.BlockSpec((2, TQ, KWIN), lambda p, i: (p, i, 0))],
        out_shape=[jax.ShapeDtypeStruct((t, D), BF16), jax.ShapeDtypeStruct((BH, t, KWIN), BF16)],
        compiler_params=_cparams(2),
    )(proj, proj, proj, proj, proj, bias)


def attn_bwd(proj, ob, probs, do):
    t = proj.shape[0]
    nt = t // TQ
    cb = C_QKVB // 128

    def body(q_ref, kp_ref, kc_ref, vp_ref, vc_ref, o_ref, p_ref, do_ref,
             dq_ref, dk_ref, dv_ref, db_ref, ck, cv, ak, av):
        i = pl.program_id(1)

        @pl.when(i == 0)
        def _():
            ck[...] = jnp.zeros_like(ck)
            cv[...] = jnp.zeros_like(cv)
            db_ref[...] = jnp.zeros_like(db_ref)

        @pl.when(i < nt)
        def _():
            ak[...] = jnp.zeros_like(ak)
            av[...] = jnp.zeros_like(av)
            for r in range(NSUB):
                lo, hi = r * SUBQ, r * SUBQ + KWIN - TQ
                rows = slice(lo, lo + SUBQ)
                kw = jnp.concatenate([kp_ref[lo:, :], kc_ref[:hi, :]], axis=0)
                vw = jnp.concatenate([vp_ref[lo:, :], vc_ref[:hi, :]], axis=0)
                probs_r = [p_ref[hh, rows, :].astype(F32) for hh in range(2)]
                dq, dkw, dvw, dss = attn_sub_bwd(q_ref[rows, :].astype(F32), kw, vw, o_ref[rows, :].astype(F32),
                                                 do_ref[rows, :], probs_r, r)
                dq_ref[rows, :] = dq.astype(BF16)
                ak[lo:lo + KWIN, :] += dkw
                av[lo:lo + KWIN, :] += dvw
                for hh in range(2):
                    _, scatter = jax.vjp(lambda tab: assemble_bias(tab, r), jnp.zeros((NPAIR, CH, 2 * CH), F32))
                    db_ref[hh] += scatter(dss[hh])[0]
            dk_ref[...] = (ck[...] + ak[:TQ, :]).astype(BF16)
            dv_ref[...] = (cv[...] + av[:TQ, :]).astype(BF16)
            ck[...] = ak[TQ:, :]
            cv[...] = av[TQ:, :]

        @pl.when(i == nt)
        def _():
            dk_ref[...] = ck[...].astype(BF16)
            dv_ref[...] = cv[...].astype(BF16)

    def blk(off, prev):
        if prev:
            return pl.BlockSpec((TQ, 128), lambda p, i: (jnp.clip(i - 1, 0, nt - 1), cb + off + p))
        return pl.BlockSpec((TQ, 128), lambda p, i: (jnp.minimum(i, nt - 1), cb + off + p))

    own = pl.BlockSpec((TQ, 128), lambda p, i: (jnp.minimum(i, nt - 1), p))
    lag = pl.BlockSpec((TQ, 128), lambda p, i: (jnp.maximum(i - 1, 0), p))
    return pl.pallas_call(
        body, name="attn_bwd", grid=(BH // 2, nt + 1),
        in_specs=[blk(0, False), blk(8, True), blk(8, False), blk(16, True), blk(16, False), own,
                  pl.BlockSpec((2, TQ, KWIN), lambda p, i: (p, jnp.minimum(i, nt - 1), 0)), own],
        out_specs=[own, lag, lag, pl.BlockSpec((2, NPAIR, CH, 2 * CH), lambda p, i: (p, 0, 0, 0))],
        out_shape=[jax.ShapeDtypeStruct((t, D), BF16)] * 3 + [jax.ShapeDtypeStruct((BH, NPAIR, CH, 2 * CH), F32)],
        scratch_shapes=[pltpu.VMEM((TQ, 128), F32), pltpu.VMEM((TQ, 128), F32),
                        pltpu.VMEM((2 * TQ, 128), F32), pltpu.VMEM((2 * TQ, 128), F32)],
        compiler_params=_cparams(2),
    )(proj, proj, proj, proj, proj, ob, probs, do)


MERGE_TM = 256


def merge_fwd(x, oa, ob, proj, vecs, wa, wb, wo):
    t = x.shape[0]
    tm = MERGE_TM
    names = ("bga", "bgb", "gate_t", "g1", "b1", "scale_f", "shift_f")

    def body(x_ref, oa_ref, ob_ref, gra_ref, grb_ref, *rest):
        vrefs = rest[:7]
        wa_ref, wb_ref, wo_ref, y_ref, h_ref = rest[7:]
        vv = [r[...] for r in vrefs]
        zero = jnp.zeros((tm, D), F32)
        y1, _ = merge_fn(x_ref[...], oa_ref[...], ob_ref[...], gra_ref[...], grb_ref[...], zero, zero, zero,
                         *vv, wa_ref[...], wb_ref[...], wo_ref[...])
        y_ref[...] = y1
        h_ref[...] = (y1 * (1.0 + vv[5]) + vv[6]).astype(BF16)

    return pl.pallas_call(
        body, name="merge_fwd", grid=(t // tm,),
        in_specs=[_rows(tm, D), _rows(tm, D), _rows(tm, D), _rows(tm, D, C_GATE // D), _rows(tm, D, C_GATE // D + 1)]
        + [_const((1, D))] * 7 + [_const((D, D))] * 3,
        out_specs=[_rows(tm, D), _rows(tm, D)],
        out_shape=[jax.ShapeDtypeStruct((t, D), F32), jax.ShapeDtypeStruct((t, D), BF16)],
        compiler_params=_cparams(),
    )(x, oa, ob, proj, proj, *[vecs[n] for n in names], wa, wb, wo)


def merge_bwd(x, oa, ob, proj, vecs, wa, wb, wo, dy1):
    t = x.shape[0]
    tm = MERGE_TM
    names = ("bga", "bgb", "gate_t", "g1", "b1", "scale_f", "shift_f")

    def body(x_ref, oa_ref, ob_ref, gra_ref, grb_ref, *rest):
        vrefs = rest[:7]
        wa_ref, wb_ref, wo_ref, dy_ref = rest[7:11]
        (dx_ref, doa_ref, dob_ref, dga_ref, dgb_ref, mg_ref, dmix_ref, dpa_ref, dpb_ref,
         dbga_ref, dbgb_ref, dgt_ref, dg1_ref, db1_ref) = rest[11:]
        i = pl.program_id(0)
        vv = [r[...] for r in vrefs]
        zero = jnp.zeros((tm, D), F32)

        def f(x_, oa_, ob_, gra_, grb_, ppa, ppb, pmix, bga, bgb, gate_t, g1, b1):
            return merge_fn(x_, oa_, ob_, gra_, grb_, ppa, ppb, pmix, bga, bgb, gate_t, g1, b1, vv[5], vv[6],
                            wa_ref[...], wb_ref[...], wo_ref[...])

        _, vjp, merged = jax.vjp(f, x_ref[...], oa_ref[...].astype(F32), ob_ref[...].astype(F32),
                                 gra_ref[...], grb_ref[...], zero, zero, zero, *vv[:5], has_aux=True)
        dx, doa, dob, dga, dgb, dpa, dpb, dmix, dbga, dbgb, dgt, dg1, db1 = vjp(dy_ref[...])
        dx_ref[...] = dx
        doa_ref[...] = doa
        dob_ref[...] = dob
        dga_ref[...] = dga.astype(BF16)
        dgb_ref[...] = dgb.astype(BF16)
        mg_ref[...] = merged.astype(BF16)
        dmix_ref[...] = dmix.astype(BF16)
        dpa_ref[...] = dpa.astype(BF16)
        dpb_ref[...] = dpb.astype(BF16)
        accs = (dbga_ref, dbgb_ref, dgt_ref, dg1_ref, db1_ref)

        @pl.when(i == 0)
        def _():
            for a in accs:
                a[...] = jnp.zeros_like(a)

        for a, val in zip(accs, (dbga, dbgb, dgt, dg1, db1)):
            a[...] += val

    return pl.pallas_call(
        body, name="merge_bwd", grid=(t // tm,),
        in_specs=[_rows(tm, D), _rows(tm, D), _rows(tm, D), _rows(tm, D, C_GATE // D), _rows(tm, D, C_GATE // D + 1)]
        + [_const((1, D))] * 7 + [_const((D, D))] * 3 + [_rows(tm, D)],
        out_specs=[_rows(tm, D)] * 9 + [_const((1, D))] * 5,
        out_shape=[jax.ShapeDtypeStruct((t, D), F32)] * 3 + [jax.ShapeDtypeStruct((t, D), BF16)] * 6
        + [jax.ShapeDtypeStruct((1, D), F32)] * 5,
        compiler_params=_cparams(),
    )(x, oa, ob, proj, proj, *[vecs[n] for n in names], wa, wb, wo, dy1)


FFN_TM = 128


def ffn_act_fwd(up, conv_w, bconv):
    t, wdt = up.shape
    tm = FFN_TM

    def body(prev_ref, cur_ref, cw_ref, bc_ref, a_ref):
        i = pl.program_id(0)
        flag = jnp.where(i > 0, 1.0, 0.0)

        def ext(sl):
            return jnp.concatenate([prev_ref[:, sl] * flag, cur_ref[:, sl]], axis=0)

        def rows(sl):
            return tuple(cw_ref[j:j + 1, sl] for j in range(3))

        for cb in range(DFF // LANE):
            g = slice(cb * LANE, (cb + 1) * LANE)
            v = slice(DFF + cb * LANE, DFF + (cb + 1) * LANE)
            a_ref[:, g] = ffn_act_fn(ext(g), ext(v), rows(g), rows(v), bc_ref[:, g], bc_ref[:, v]).astype(BF16)

    return pl.pallas_call(
        body, name="ffn_act_fwd", grid=(t // tm,),
        in_specs=_halo_specs(tm, wdt, 0, lambda i: i) + [_const((3, wdt)), _const((1, wdt))],
        out_specs=_rows(tm, DFF),
        out_shape=jax.ShapeDtypeStruct((t, DFF), BF16),
        compiler_params=_cparams(),
    )(up, up, conv_w, bconv)


def ffn_act_bwd(up, conv_w, bconv, da):
    t, wdt = up.shape
    tm = FFN_TM
    nt = t // tm
    rev = lambda i: nt - 1 - i

    def body(prev_ref, cur_ref, cw_ref, bc_ref, da_ref, dup_ref, dcw_ref, dbc_ref, carry):
        i = pl.program_id(0)
        flag = jnp.where(i < nt - 1, 1.0, 0.0)

        @pl.when(i == 0)
        def _():
            carry[...] = jnp.zeros_like(carry)
            dcw_ref[...] = jnp.zeros_like(dcw_ref)
            dbc_ref[...] = jnp.zeros_like(dbc_ref)

        def ext(sl):
            return jnp.concatenate([prev_ref[:, sl] * flag, cur_ref[:, sl]], axis=0)

        def rows(sl):
            return tuple(cw_ref[j:j + 1, sl] for j in range(3))

        def emit(sl, dext, drows, dbc):
            dcur = dext[HALO:]
            dup_ref[:, sl] = jnp.concatenate([dcur[:tm - HALO], dcur[tm - HALO:] + carry[:, sl]], axis=0).astype(BF16)
            carry[:, sl] = dext[:HALO]
            dcw_ref[:, sl] += _stack_rows(drows)
            dbc_ref[:, sl] += dbc

        for cb in range(DFF // LANE):
            g = slice(cb * LANE, (cb + 1) * LANE)
            v = slice(DFF + cb * LANE, DFF + (cb + 1) * LANE)
            _, vjp = jax.vjp(ffn_act_fn, ext(g), ext(v), rows(g), rows(v), bc_ref[:, g], bc_ref[:, v])
            dxg, dxv, drg, drv, dbg, dbv = vjp(da_ref[:, g])
            emit(g, dxg, drg, dbg)
            emit(v, dxv, drv, dbv)

    return pl.pallas_call(
        body, name="ffn_act_bwd", grid=(nt,),
        in_specs=_halo_specs(tm, wdt, 0, rev) + [_const((3, wdt)), _const((1, wdt)), _rows(tm, DFF, 0, rev)],
        out_specs=[_rows(tm, wdt, 0, rev), _const((3, wdt)), _const((1, wdt))],
        out_shape=[jax.ShapeDtypeStruct((t, wdt), BF16), jax.ShapeDtypeStruct((3, wdt), F32),
                   jax.ShapeDtypeStruct((1, wdt), F32)],
        scratch_shapes=[pltpu.VMEM((HALO, wdt), F32)],
        compiler_params=_cparams(),
    )(up, up, conv_w, bconv, da)


HEAD_TM = 256


def head_fwd_bwd(a, y1, tgt, gate_f, g2, b2, wd):
    t = a.shape[0]
    tm = HEAD_TM

    def body(a_ref, y_ref, t_ref, gf_ref, g2_ref, b2_ref, wd_ref,
             da_ref, dy_ref, dffn_ref, dgf_ref, dg2_ref, db2_ref, loss_ref):
        i = pl.program_id(0)
        zero = jnp.zeros((tm, D), F32)

        def f(a_, y_, pf, gf, g2_, b2_):
            return head_fn(a_, y_, pf, gf, g2_, b2_, t_ref[...], wd_ref[...])

        loss, vjp = jax.vjp(f, a_ref[...].astype(F32), y_ref[...], zero, gf_ref[...], g2_ref[...], b2_ref[...])
        da, dy, dffn, dgf, dg2, db2 = vjp(jnp.ones((), F32))
        da_ref[...] = da
        dy_ref[...] = dy
        dffn_ref[...] = dffn.astype(BF16)
        accs = (dgf_ref, dg2_ref, db2_ref, loss_ref)

        @pl.when(i == 0)
        def _():
            for r in accs:
                r[...] = jnp.zeros_like(r)

        dgf_ref[...] += dgf
        dg2_ref[...] += dg2
        db2_ref[...] += db2
        loss_ref[...] += loss * jnp.ones((1, 128), F32)

    return pl.pallas_call(
        body, name="head_fwd_bwd", grid=(t // tm,),
        in_specs=[_rows(tm, DFF), _rows(tm, D), _rows(tm, D), _const((1, D)), _const((1, D)), _const((1, D)),
                  _const((DFF, D))],
        out_specs=[_rows(tm, DFF), _rows(tm, D), _rows(tm, D), _const((1, D)), _const((1, D)), _const((1, D)),
                   _const((1, 128))],
        out_shape=[jax.ShapeDtypeStruct((t, DFF), F32), jax.ShapeDtypeStruct((t, D), F32),
                   jax.ShapeDtypeStruct((t, D), BF16)] + [jax.ShapeDtypeStruct((1, D), F32)] * 3
        + [jax.ShapeDtypeStruct((1, 128), F32)],
        compiler_params=_cparams(),
    )(a, y1, tgt, gate_f, g2, b2, wd)


def ada_fwd(c_all, w_sh, b_sh):
    def body(c_ref, w_ref, b_ref, o_ref):
        o_ref[...] = _mmh(_silu(c_ref[...]), w_ref[...]) + b_ref[...]

    n = w_sh.shape[1]
    return pl.pallas_call(
        body, name="ada_fwd", out_shape=jax.ShapeDtypeStruct((NDEV, n), F32),
        in_specs=[pl.BlockSpec(memory_space=pltpu.VMEM)] * 3,
        out_specs=pl.BlockSpec(memory_space=pltpu.VMEM),
        compiler_params=pltpu.CompilerParams(vmem_limit_bytes=VMEM_LIMIT),
    )(c_all, w_sh, b_sh)


def ada_wgrad(c_all_t, dmod_sh):
    def body(c_ref, d_ref, o_ref):
        o_ref[...] = _mmh(_silu(c_ref[...]), d_ref[...])

    return pl.pallas_call(
        body, name="ada_wgrad", out_shape=jax.ShapeDtypeStruct((c_all_t.shape[0], dmod_sh.shape[1]), F32),
        in_specs=[pl.BlockSpec(memory_space=pltpu.VMEM)] * 2,
        out_specs=pl.BlockSpec(memory_space=pltpu.VMEM),
        compiler_params=pltpu.CompilerParams(vmem_limit_bytes=VMEM_LIMIT),
    )(c_all_t, dmod_sh)


def adamw(gparts, w, m, v, name):
    p, r, c = gparts.shape
    tr = r if r <= 256 else _pick(r, (256, 128, 64, 32, 16, 8))
    c1 = 1.0 - B1 ** STEP
    c2 = 1.0 - B2 ** STEP

    def body(g_ref, w_ref, m_ref, v_ref, go_ref, d_ref, mo_ref, vo_ref):
        g = g_ref[0].astype(F32)
        for s in range(1, p):
            g = g + g_ref[s].astype(F32)
        mn = B1 * m_ref[0] + (1.0 - B1) * g
        vn = B2 * v_ref[0] + (1.0 - B2) * (g * g)
        go_ref[0] = g
        d_ref[0] = -LR * ((mn / c1) / (jnp.sqrt(vn / c2) + AEPS) + WD * w_ref[0])
        mo_ref[0] = mn
        vo_ref[0] = vn

    spec = pl.BlockSpec((1, tr, c), lambda i: (0, i, 0))
    return pl.pallas_call(
        body, name=name, grid=(r // tr,),
        in_specs=[pl.BlockSpec((p, tr, c), lambda i: (0, i, 0)), spec, spec, spec],
        out_specs=[spec] * 4,
        out_shape=[jax.ShapeDtypeStruct((1, r, c), F32)] * 4,
        compiler_params=_cparams(),
    )(gparts, w, m, v)


def adamw_small(gs, ws, ms, vs, loss_parts, name):
    n = len(ws)
    c1 = 1.0 - B1 ** STEP
    c2 = 1.0 - B2 ** STEP

    def slots(ref):
        acc = ref[0]
        for s in range(1, ref.shape[0]):
            acc = acc + ref[s]
        return acc

    def body(*refs):
        g_refs, w_refs, m_refs, v_refs = (refs[k * n:(k + 1) * n] for k in range(4))
        l_ref, outs = refs[4 * n], refs[4 * n + 1:]
        for i in range(n):
            g = slots(g_refs[i])
            mn = B1 * m_refs[i][...] + (1.0 - B1) * g
            vn = B2 * v_refs[i][...] + (1.0 - B2) * (g * g)
            outs[i][...] = g
            outs[n + i][...] = -LR * ((mn / c1) / (jnp.sqrt(vn / c2) + AEPS) + WD * w_refs[i][...])
            outs[2 * n + i][...] = mn
            outs[3 * n + i][...] = vn
        outs[4 * n][...] = slots(l_ref)

    vmem = pl.BlockSpec(memory_space=pltpu.VMEM)
    outs = pl.pallas_call(
        body, name=name,
        in_specs=[vmem] * (4 * n + 1), out_specs=[vmem] * (4 * n + 1),
        out_shape=[jax.ShapeDtypeStruct(w.shape, F32) for w in ws] * 4 + [jax.ShapeDtypeStruct((1, LANE), F32)],
        compiler_params=pltpu.CompilerParams(vmem_limit_bytes=VMEM_LIMIT),
    )(*gs, *ws, *ms, *vs, loss_parts)
    return outs[:n], outs[n:2 * n], outs[2 * n:3 * n], outs[3 * n:4 * n], outs[4 * n]


def _me():
    x, y, c = lax.axis_index("x"), lax.axis_index("y"), lax.axis_index("c")
    return x, y, c, 4 * x + 2 * y + c


def _peer(x, y, c, d):
    px = 1 - x if (d >> 2) & 1 else x
    py = 1 - y if (d >> 1) & 1 else y
    pc = 1 - c if d & 1 else c
    return (px, py, pc), 4 * px + 2 * py + pc


def _exchange(arrs, name, scatter):
    n = len(arrs)

    def body(*refs):
        ins, outs = refs[:n], refs[n:2 * n]
        send, recv, lsem = refs[2 * n:]
        x, y, c, me = _me()
        remote, local = [], []
        for k in range(n):
            src = ins[k].at[me] if scatter else ins[k]
            cp = pltpu.make_async_copy(src, outs[k].at[me], lsem.at[k])
            cp.start()
            local.append(cp)
            for d in range(1, NDEV):
                dev, pid = _peer(x, y, c, d)
                src = ins[k].at[pid] if scatter else ins[k]
                cp = pltpu.make_async_remote_copy(src_ref=src, dst_ref=outs[k].at[me],
                                                  send_sem=send.at[k, d - 1], recv_sem=recv.at[k, d - 1],
                                                  device_id=dev, device_id_type=pl.DeviceIdType.MESH)
                cp.start()
                remote.append(cp)
        for cp in remote:
            cp.wait()
        for cp in local:
            cp.wait()

    shapes = [a.shape if scatter else (NDEV,) + a.shape for a in arrs]
    return pl.pallas_call(
        body, name=name,
        in_specs=[pl.BlockSpec(memory_space=pl.ANY)] * n,
        out_specs=[pl.BlockSpec(memory_space=pl.ANY)] * n,
        out_shape=[jax.ShapeDtypeStruct(s, a.dtype) for s, a in zip(shapes, arrs)],
        scratch_shapes=[pltpu.SemaphoreType.DMA((n, NDEV - 1)), pltpu.SemaphoreType.DMA((n, NDEV - 1)),
                        pltpu.SemaphoreType.DMA((n,))],
        compiler_params=pltpu.CompilerParams(has_side_effects=True),
    )(*arrs)


def all_gather(arrs, name):
    return _exchange(arrs, name, False)


def all_gather_two_level(shard, name):
    def body(x_ref, out_ref, send, recv, lsem):
        x, y, c, _ = _me()
        sibling = (x, y, 1 - c)
        chips = [(1 - x, y), (x, 1 - y), (1 - x, 1 - y)]

        def slot(px, py, pc):
            return out_ref.at[4 * px + 2 * py + pc]

        def copy(k, block, to, src=None):
            return pltpu.make_async_remote_copy(
                src_ref=slot(*block) if src is None else src, dst_ref=slot(*block),
                send_sem=send.at[k], recv_sem=recv.at[k], device_id=to, device_id_type=pl.DeviceIdType.MESH)

        mine = pltpu.make_async_copy(x_ref, slot(x, y, c), lsem)
        mine.start()
        first = [copy(0, (x, y, c), sibling, src=x_ref)]
        first += [copy(1 + j, (x, y, c), (*chip, c), src=x_ref) for j, chip in enumerate(chips)]
        for cp in first:
            cp.start()
        passed = [copy(4 + j, (*chip, c), sibling) for j, chip in enumerate(chips)]
        for j, chip in enumerate(chips):
            copy(1 + j, (*chip, c), (x, y, c)).wait_recv()
            passed[j].start()
        copy(0, sibling, (x, y, c)).wait_recv()
        for j, chip in enumerate(chips):
            copy(4 + j, (*chip, 1 - c), (x, y, c)).wait_recv()
        for cp in first + passed:
            cp.wait_send()
        mine.wait()

    return pl.pallas_call(
        body, name=name,
        in_specs=[pl.BlockSpec(memory_space=pl.ANY)],
        out_specs=pl.BlockSpec(memory_space=pl.ANY),
        out_shape=jax.ShapeDtypeStruct((NDEV,) + shard.shape, shard.dtype),
        scratch_shapes=[pltpu.SemaphoreType.DMA((NPEER,)), pltpu.SemaphoreType.DMA((NPEER,)),
                        pltpu.SemaphoreType.DMA],
        compiler_params=pltpu.CompilerParams(has_side_effects=True),
    )(shard)


def all_to_all(arrs, name):
    return _exchange(arrs, name, True)


_HBM = pl.BlockSpec(memory_space=pltpu.HBM)
_SEM = pl.BlockSpec(memory_space=pltpu.SEMAPHORE)
_EFFECT = pltpu.SideEffectType.DATAFLOW_SIDE_EFFECTING
NPEER = NDEV - 1


def exchange_start(arrs, name, scatter):
    n = len(arrs)
    lands = [lax.empty(a.shape if scatter else (NDEV,) + a.shape, a.dtype) for a in arrs]

    def body(*refs):
        ins, lrefs = refs[:n], refs[n:2 * n]
        send, recv, token = refs[2 * n], refs[2 * n + 1], refs[-1]
        x, y, c, me = _me()
        for k in range(n):
            for d in range(1, NDEV):
                dev, pid = _peer(x, y, c, d)
                src = ins[k].at[pid] if scatter else ins[k]
                pltpu.make_async_remote_copy(src_ref=src, dst_ref=lrefs[k].at[me],
                                             send_sem=send.at[k * NPEER + d - 1], recv_sem=recv.at[k * NPEER + d - 1],
                                             device_id=dev, device_id_type=pl.DeviceIdType.MESH).start()
        token[...] = jnp.zeros_like(token)

    thru = [pltpu.HBM(a.shape, a.dtype) for a in list(arrs) + lands]
    outs = pl.pallas_call(
        body, name=name,
        out_shape=(pltpu.SemaphoreType.DMA((n * NPEER,)), pltpu.SemaphoreType.DMA((n * NPEER,)), *thru,
                   jax.ShapeDtypeStruct((8, 128), F32)),
        in_specs=[_HBM] * (2 * n),
        out_specs=(_SEM, _SEM, *([_HBM] * (2 * n)), pl.BlockSpec(memory_space=pltpu.VMEM)),
        input_output_aliases={i: 2 + i for i in range(2 * n)},
        compiler_params=pltpu.CompilerParams(has_side_effects=_EFFECT),
    )(*[pltpu.with_memory_space_constraint(a, pltpu.HBM) for a in list(arrs) + lands])
    handle = dict(send=outs[0], recv=outs[1], src=list(outs[2:2 + n]), land=list(outs[2 + n:2 + 2 * n]),
                  scatter=scatter)
    return handle, outs[-1][0, 0]


def exchange_wait(handle, after, name):
    n = len(handle["src"])
    scatter = handle["scatter"]

    def body(*refs):
        ins, lrefs = refs[:n], refs[n:2 * n]
        send, recv = refs[2 * n], refs[2 * n + 1]
        x, y, c, _ = _me()
        for k in range(n):
            for d in range(1, NDEV):
                dev, _ = _peer(x, y, c, d)
                src = ins[k].at[0] if scatter else ins[k]
                cp = pltpu.make_async_remote_copy(src_ref=src, dst_ref=lrefs[k].at[0],
                                                  send_sem=send.at[k * NPEER + d - 1],
                                                  recv_sem=recv.at[k * NPEER + d - 1],
                                                  device_id=dev, device_id_type=pl.DeviceIdType.MESH)
                cp.wait_send()
                cp.wait_recv()

    arrs = handle["src"] + handle["land"]
    outs = pl.pallas_call(
        body, name=name,
        out_shape=tuple(pltpu.HBM(a.shape, a.dtype) for a in arrs),
        in_specs=[_HBM] * (2 * n) + [_SEM, _SEM, pl.BlockSpec(memory_space=pl.ANY)],
        out_specs=tuple([_HBM] * (2 * n)),
        input_output_aliases={i: i for i in range(2 * n)},
        compiler_params=pltpu.CompilerParams(has_side_effects=_EFFECT),
    )(*arrs, handle["send"], handle["recv"], after)
    me = 4 * lax.axis_index("x") + 2 * lax.axis_index("y") + lax.axis_index("c")
    landed = []
    for own, land in zip(outs[:n], outs[n:]):
        mine = lax.dynamic_index_in_dim(own, me, 0, keepdims=True) if scatter else own[None]
        landed.append(lax.dynamic_update_slice_in_dim(land, mine, me, 0))
    return landed


def _cat_from_slabs(slabs):
    _, k, n = slabs.shape

    def cols(lo, hi):
        parts, c = [], lo
        while c < hi:
            j = c // n
            e = min(hi, (j + 1) * n)
            parts.append(slabs[j][:, c - j * n:e - j * n])
            c = e
        return parts

    def zeros(w):
        return [jnp.zeros((k, w), slabs.dtype)]

    return jnp.concatenate(cols(0, 4096) + cols(4112, 9232) + cols(4096, 4104) + zeros(LANE - AH)
                           + cols(4104, 4112) + zeros(NCAT - C_BA - LANE - AH), axis=1)


IN_PIECES = (("pre", C_QKVA, 3072), ("z", C_Z, 1024), ("qb", C_QKVB, 1024), ("kb", C_QKVB + 1024, 1024),
             ("vb", C_QKVB + 2048, 1024), ("ga", C_GATE, 1024), ("gb", C_GATE + 1024, 1024))
_ORIG_SEGS = ((0, 3072, "pre", 0), (3072, 4096, "z", 0), (4096, 4104, "ba", 0), (4104, 4112, "ba", LANE),
              (4112, 5136, "qb", 0), (5136, 6160, "kb", 0), (6160, 7184, "vb", 0), (7184, 8208, "ga", 0),
              (8208, 9232, "gb", 0))


def _orig_cols_from_pieces(gp, lo, hi):
    parts = []
    for a, b, name, off in _ORIG_SEGS:
        s, e = max(a, lo), min(b, hi)
        if s < e:
            parts.append(gp[name][:, off + s - a:off + e - a])
    return parts[0] if len(parts) == 1 else jnp.concatenate(parts, axis=1)


def _pad128(v):
    return jnp.pad(v, ((0, 0), (0, 128 - v.shape[1])))


def local_step(x, tgt, mod, wts, small, late_weights=None, on_grads=None):
    if on_grads is None:
        on_grads = lambda group, gd: jnp.zeros((), F32)
    t = x.shape[0]
    nc = t // CH
    shift_t, scale_t, gate_t, shift_f, scale_f, gate_f = mod
    wcat = _cat_from_slabs(wts["w_in_slabs"])
    a_log = _pad128(small["a_log"])
    dtb = _pad128(small["dt_bias"])
    vecs = dict(bga=small["b_gate"][:, :D], bgb=small["b_gate"][:, D:], gate_t=gate_t, g1=small["ln1_g"],
                b1=small["ln1_b"], scale_f=scale_f, shift_f=shift_f)

    h1 = modulate(x, scale_t, shift_t, "modulate_t")
    proj = matmul(h1, wcat, F32, "in_proj")
    q, k, v, gcs, beta = prep_fwd(proj, small["conv_a"], a_log, dtb)

    u, w, qg, kd, qk, eg, tinv = c1_fwd(q, k, v, gcs, beta)
    oa, sall = c2_fwd(u, w, qg, kd, qk, eg, proj, small["norm_a"])
    bias = bias_table(small["rel_bias"])
    ob, probs = attn_fwd(proj, bias)
    if late_weights is not None:
        wts = {**wts, **late_weights(ob)}
    y1, h2 = merge_fwd(x, oa, ob, proj, vecs, wts["w_a"], wts["w_b"], wts["w_o"])
    up = matmul(h2, wts["w_up"], F32, "up_proj")
    a = ffn_act_fwd(up, small["conv_ffn"], small["b_conv_ffn"])

    da, dy1_res, dffn, dgate_f, dg2, db2, loss = head_fwd_bwd(a, y1, tgt, gate_f, small["ln2_g"], small["ln2_b"],
                                                            wts["w_down"])
    g_w_down = matmul(a, dffn, BF16, "wgrad_down", ta=True)
    dup, g_conv_ffn, g_bconv = ffn_act_bwd(up, small["conv_ffn"], small["b_conv_ffn"], da)
    g_w_up = matmul(h2, dup, BF16, "wgrad_up", ta=True)
    tok = on_grads("ffn", dict(w_up=g_w_up, w_down=g_w_down))
    dy1, dscale_f, dshift_f = dgrad_modulated(dup, wts["w_up"], y1, dy1_res, scale_f + tok, "dgrad_up")
    (dx_res, doa, dob, dga, dgb, merged, dmix, dpa, dpb,
     dbga, dbgb, dgate_t, dg1, db1) = merge_bwd(x, oa, ob, proj, vecs, wts["w_a"], wts["w_b"], wts["w_o"], dy1)
    g_w_o = matmul(merged, dmix, BF16, "wgrad_o", ta=True)
    g_w_a = matmul(oa, dpa, BF16, "wgrad_a", ta=True)
    g_w_b = matmul(ob, dpb, BF16, "wgrad_b", ta=True)
    tok = on_grads("mix", dict(w_o=g_w_o, w_a=g_w_a, w_b=g_w_b))
    dqb, dkb, dvb, dbias = attn_bwd(proj, ob, probs, dob)
    g_rel = relbias_reduce(bias_table_bwd_layout(dbias))
    du, dw, dqg, dkd, dqk, deg, dz, g_norm = c2_bwd(u, w, qg, kd, qk, eg, proj, small["norm_a"] + tok, sall, doa)
    dq, dk, dv, dgcs, dbeta = c1_bwd(q, k, v, gcs, beta, tinv, du, dw, dqg, dkd, dqk, deg)
    dpre, dbb, daa, g_conv_a, g_alog, g_dtb = prep_bwd(proj, small["conv_a"], a_log, dtb, dq, dk, dv, dgcs, dbeta)
    tok = on_grads("small", dict(conv_a=g_conv_a, rel_bias=g_rel, conv_ffn=g_conv_ffn))
    dba = jnp.concatenate([dbb, daa, jnp.zeros((t, NCAT - C_BA - 2 * LANE), BF16)], axis=1) + tok.astype(BF16)
    dpieces = dict(pre=dpre, z=dz, qb=dqb, kb=dkb, vb=dvb, ga=dga, gb=dgb)
    g_in = {n: matmul(h1, dpieces[n], BF16, "wgrad_in_" + n, ta=True) for n, _, _ in IN_PIECES}
    g_in["ba"] = matmul(h1, dba, BF16, "wgrad_in_ba", ta=True)
    tok = on_grads("in", g_in)
    dh1 = dgrad_pieces([(dpieces[n], off) for n, off, _ in IN_PIECES], dba + tok.astype(BF16), wcat,
                       "dgrad_in")
    grad_x, dscale_t, dshift_t = modulate_bwd(dh1, x, dx_res, scale_t + tok, "modulate_t_bwd")

    dmod = (dshift_t, dscale_t, dgate_t, dshift_f, dscale_f, dgate_f)
    grads = dict(w_in=_orig_cols_from_pieces(g_in, 0, 9232), w_up=g_w_up, w_down=g_w_down, w_a=g_w_a, w_b=g_w_b, w_o=g_w_o,
                 conv_a=g_conv_a, rel_bias=g_rel, conv_ffn=g_conv_ffn,
                 b_gate=jnp.concatenate([dbga, dbgb], axis=1), a_log=g_alog[:, :AH], dt_bias=g_dtb[:, :AH],
                 norm_a=g_norm, ln1_g=dg1, ln1_b=db1, b_conv_ffn=g_bconv, ln2_g=dg2, ln2_b=db2)
    return loss[0, 0], grad_x, dmod, grads


REP_NAMES = ["b_ada", "b_gate", "a_log", "dt_bias", "norm_a", "ln1_g", "ln1_b", "b_conv_ffn", "ln2_g", "ln2_b"]
SH_NAMES = ["conv_a", "rel_bias", "conv_ffn"]


def _col_shards(a, n):
    return a.reshape(a.shape[0], NDEV, n).transpose(1, 0, 2)


def kernel(x, c, w_ada, b_ada, w_in, b_gate, conv_a, a_log, dt_bias, norm_a, rel_bias, w_branch_a, w_branch_b, w_o, ln1_g, ln1_b, w_up, conv_ffn, b_conv_ffn, w_down, ln2_g, ln2_b, loss_target, m_w_ada, m_b_ada, m_w_in, m_b_gate, m_conv_a, m_a_log, m_dt_bias, m_norm_a, m_rel_bias, m_w_branch_a, m_w_branch_b, m_w_o, m_ln1_g, m_ln1_b, m_w_up, m_conv_ffn, m_b_conv_ffn, m_w_down, m_ln2_g, m_ln2_b, v_w_ada, v_b_ada, v_w_in, v_b_gate, v_conv_a, v_a_log, v_dt_bias, v_norm_a, v_rel_bias, v_w_branch_a, v_w_branch_b, v_w_o, v_ln1_g, v_ln1_b, v_w_up, v_conv_ffn, v_b_conv_ffn, v_w_down, v_ln2_g, v_ln2_b):
    W = dict(w_ada=w_ada, b_ada=b_ada, w_in=w_in, b_gate=b_gate, conv_a=conv_a, a_log=a_log, dt_bias=dt_bias,
             norm_a=norm_a, rel_bias=rel_bias, w_branch_a=w_branch_a, w_branch_b=w_branch_b, w_o=w_o, ln1_g=ln1_g,
             ln1_b=ln1_b, w_up=w_up, conv_ffn=conv_ffn, b_conv_ffn=b_conv_ffn, w_down=w_down, ln2_g=ln2_g,
             ln2_b=ln2_b)
    M = dict(w_ada=m_w_ada, b_ada=m_b_ada, w_in=m_w_in, b_gate=m_b_gate, conv_a=m_conv_a, a_log=m_a_log,
             dt_bias=m_dt_bias, norm_a=m_norm_a, rel_bias=m_rel_bias, w_branch_a=m_w_branch_a,
             w_branch_b=m_w_branch_b, w_o=m_w_o, ln1_g=m_ln1_g, ln1_b=m_ln1_b, w_up=m_w_up, conv_ffn=m_conv_ffn,
             b_conv_ffn=m_b_conv_ffn, w_down=m_w_down, ln2_g=m_ln2_g, ln2_b=m_ln2_b)
    V = dict(w_ada=v_w_ada, b_ada=v_b_ada, w_in=v_w_in, b_gate=v_b_gate, conv_a=v_conv_a, a_log=v_a_log,
             dt_bias=v_dt_bias, norm_a=v_norm_a, rel_bias=v_rel_bias, w_branch_a=v_w_branch_a,
             w_branch_b=v_w_branch_b, w_o=v_w_o, ln1_g=v_ln1_g, ln1_b=v_ln1_b, w_up=v_w_up, conv_ffn=v_conv_ffn,
             b_conv_ffn=v_b_conv_ffn, w_down=v_w_down, ln2_g=v_ln2_g, ln2_b=v_ln2_b)
    W3, M3, V3 = W, M, V
    W, M, V = ({n: a[0] for n, a in dct.items()} for dct in (W, M, V))
    me = 4 * lax.axis_index("x") + 2 * lax.axis_index("y") + lax.axis_index("c")
    big = ("w_in", "w_up", "w_down", "w_branch_a", "w_branch_b", "w_o")

    g_in = all_gather_two_level(W["w_in"].astype(BF16), "gather_w_in")
    wts = dict(w_in_slabs=g_in)
    c_all, *sh_all = all_gather([c] + [W[n] for n in SH_NAMES], "gather_small")
    c_all = c_all.reshape(NDEV, D)

    def full_small(g8):
        return g8.transpose(1, 0, 2).reshape(g8.shape[1], -1)

    small = dict(conv_a=full_small(sh_all[0]), rel_bias=full_small(sh_all[1]), conv_ffn=full_small(sh_all[2]),
                 b_gate=W["b_gate"][None], a_log=W["a_log"][None], dt_bias=W["dt_bias"][None],
                 norm_a=W["norm_a"][None], ln1_g=W["ln1_g"][None], ln1_b=W["ln1_b"][None],
                 b_conv_ffn=W["b_conv_ffn"][None], ln2_g=W["ln2_g"][None], ln2_b=W["ln2_b"][None])

    nsh = w_ada.shape[2]
    b_sh = lax.dynamic_slice(W["b_ada"][None], (0, me * nsh), (1, nsh))
    mod_sh = ada_fwd(c_all, W["w_ada"], b_sh)
    (mod_rows,) = all_to_all([mod_sh[:, None, :]], "scatter_mod")
    mod6 = mod_rows.reshape(6, D)

    after_small = (g_in[0, 0, 0].astype(F32) * 0.0 + mod6[0, 0] * 0.0).astype(BF16)
    late, late_tok = exchange_start([W[n].astype(BF16) + after_small for n in big[1:]], "gather_late_start", False)

    def late_weights(after):
        g_up, g_down, g_a, g_b, g_o = exchange_wait(late, after, "gather_late_wait")
        return dict(w_up=g_up.transpose(1, 0, 2).reshape(D, -1), w_down=g_down.reshape(DFF, D),
                    w_a=g_a.reshape(D, D), w_b=g_b.reshape(D, D), w_o=g_o.reshape(D, D))

    mod6 = mod6 + late_tok
    mod = tuple(mod6[i:i + 1] for i in range(6))

    pending = {}

    def on_grads(group, gd):
        if group == "small":
            pending["small"] = all_to_all([_col_shards(gd[n], W[n].shape[1]) for n in SH_NAMES],
                                          "scatter_small_grads")
            return pending["small"][0][0, 0, 0] * 0.0
        if group == "ffn":
            slabs = [_col_shards(gd["w_up"], w_up.shape[2]), gd["w_down"].reshape(NDEV, -1, D)]
        elif group == "mix":
            slabs = [gd[n].reshape(NDEV, -1, D) for n in ("w_a", "w_b", "w_o")]
        else:
            nin = w_in.shape[2]
            slabs = [jnp.stack([_orig_cols_from_pieces(gd, j * nin, (j + 1) * nin) for j in range(NDEV)], axis=0)]
        pending[group], tok = exchange_start([s.astype(BF16) for s in slabs], "scatter_" + group + "_start", True)
        return tok

    loss, grad_x, dmod, g = local_step(x[0], loss_target[0], mod, wts, small, late_weights, on_grads)

    rep_grads = {n: g[n] for n in REP_NAMES if n != "b_ada"}
    rep_grads["b_ada"] = jnp.concatenate(dmod, axis=1)
    gathered = all_gather([rep_grads[n] for n in REP_NAMES] + [jnp.broadcast_to(loss, (1, LANE))],
                          "gather_small_grads")
    rep_all = dict(zip(REP_NAMES, gathered))
    sh_recv = [p[:, None] for p in pending["small"]]
    small_names = REP_NAMES + SH_NAMES
    sg, sd, sm, sv, loss_row = adamw_small([rep_all[n] for n in REP_NAMES] + sh_recv,
                                           [W3[n] for n in small_names], [M3[n] for n in small_names],
                                           [V3[n] for n in small_names], gathered[-1], "adamw_small")
    loss_total = loss_row[0, 0]

    dmod_all = rep_all["b_ada"][:, 0]
    dmod_sh = lax.dynamic_slice(dmod_all, (0, me * nsh), (NDEV, nsh))
    g_w_ada = ada_wgrad(c_all.T, dmod_sh)

    p_up, p_down = exchange_wait(pending["ffn"], grad_x, "scatter_ffn_wait")
    p_a, p_b, p_o = exchange_wait(pending["mix"], grad_x, "scatter_mix_wait")
    (p_in,) = exchange_wait(pending["in"], grad_x, "scatter_in_wait")
    parts = [p_in, p_up, p_down, p_a, p_b, p_o]

    res = {}
    for n, p in zip(big, parts):
        res[n] = adamw(p, W3[n], M3[n], V3[n], "adamw_" + n)
    res["w_ada"] = adamw(g_w_ada[None], W3["w_ada"], M3["w_ada"], V3["w_ada"], "adamw_w_ada")
    for i, n in enumerate(small_names):
        res[n] = (sg[i], sd[i], sm[i], sv[i])

    order = ("w_ada", "b_ada", "w_in", "b_gate", "conv_a", "a_log", "dt_bias", "norm_a", "rel_bias", "w_branch_a",
             "w_branch_b", "w_o", "ln1_g", "ln1_b", "w_up", "conv_ffn", "b_conv_ffn", "w_down", "ln2_g", "ln2_b")
    outs = [loss_total, grad_x[None]]
    for kind in range(4):
        outs += [res[n][kind] for n in order]
    return tuple(outs)
```

```python
import functools
import math

import numpy as np
import jax
import jax.numpy as jnp
from jax import lax
from jax.experimental import pallas as pl
from jax.experimental.pallas import tpu as pltpu

F32 = jnp.float32
BF16 = jnp.bfloat16
HI = lax.Precision.HIGHEST

D = 1024
CH = 64
AH, ADK = 8, 128
BH, BDH = 16, 64
BPREV = 8
BMAXREL = 256
RELSZ = CH + BMAXREL
DFF = 2816
ALPHA = 2.0 ** 0.25
LN_EPS, RMS_EPS, L2_EPS = 1e-5, 1e-6, 1e-6
NEG = -1e30
LR, B1, B2, AEPS, WD, STEP = 1e-3, 0.9, 0.999, 1e-8, 0.01, 10
NDEV = 8
HALO = 8
LANE = 128
TQ = 512
VMEM_LIMIT = 56 * 1024 * 1024

C_QKVA, C_Z, C_QKVB, C_GATE, C_BA, NCAT = 0, 3072, 4096, 7168, 9216, 9728


def _cparams(n_axes=1, vmem=VMEM_LIMIT):
    return pltpu.CompilerParams(dimension_semantics=("arbitrary",) * n_axes, vmem_limit_bytes=vmem)


def _dg(a, b, ca, cb):
    return lax.dot_general(a.astype(BF16), b.astype(BF16), (((ca,), (cb,)), ((), ())),
                           preferred_element_type=F32)


@jax.custom_vjp
def mm_nn(a, b):
    return _dg(a, b, 1, 0)


@jax.custom_vjp
def mm_nt(a, b):
    return _dg(a, b, 1, 1)


@jax.custom_vjp
def mm_tn(a, b):
    return _dg(a, b, 0, 0)


mm_nn.defvjp(lambda a, b: (mm_nn(a, b), (a, b)),
             lambda r, g: (mm_nt(g, r[1]).astype(r[0].dtype), mm_tn(r[0], g).astype(r[1].dtype)))
mm_nt.defvjp(lambda a, b: (mm_nt(a, b), (a, b)),
             lambda r, g: (mm_nn(g, r[1]).astype(r[0].dtype), mm_tn(g, r[0]).astype(r[1].dtype)))
mm_tn.defvjp(lambda a, b: (mm_tn(a, b), (a, b)),
             lambda r, g: (mm_nt(r[1], g).astype(r[0].dtype), mm_nn(r[0], g).astype(r[1].dtype)))


@jax.custom_vjp
def mm_w(a, w):
    return _dg(a, w, 1, 0)


mm_w.defvjp(lambda a, w: (mm_w(a, w), (a, w)),
            lambda r, g: (mm_nt(g, r[1]).astype(r[0].dtype), jnp.zeros_like(r[1])))


def _mmh(a, b):
    return lax.dot_general(a, b, (((1,), (0,)), ((), ())), precision=HI, preferred_element_type=F32)


def _bdg(a, b, ca, cb):
    return lax.dot_general(a.astype(BF16), b.astype(BF16), (((ca,), (cb,)), ((0,), (0,))),
                           preferred_element_type=F32)


@jax.custom_vjp
def bmm_nn(a, b):
    return _bdg(a, b, 2, 1)


@jax.custom_vjp
def bmm_nt(a, b):
    return _bdg(a, b, 2, 2)


@jax.custom_vjp
def bmm_tn(a, b):
    return _bdg(a, b, 1, 1)


bmm_nn.defvjp(lambda a, b: (bmm_nn(a, b), (a, b)), lambda r, g: (bmm_nt(g, r[1]), bmm_tn(r[0], g)))
bmm_nt.defvjp(lambda a, b: (bmm_nt(a, b), (a, b)), lambda r, g: (bmm_nn(g, r[1]), bmm_tn(g, r[0])))
bmm_tn.defvjp(lambda a, b: (bmm_tn(a, b), (a, b)), lambda r, g: (bmm_nt(r[1], g), bmm_nn(r[0], g)))


def _bdg3(a, b, ca, cb):
    return lax.dot_general(a, b, (((ca,), (cb,)), ((0,), (0,))), precision=lax.Precision.HIGH,
                           preferred_element_type=F32)


NEWTON_STEPS = 2


def _bdgp(a, b, ca, cb):
    return _bdg(a, b, ca, cb)


@jax.custom_vjp
def bmm3_nn(a, b):
    return _bdgp(a, b, 2, 1)


bmm3_nn.defvjp(lambda a, b: (bmm3_nn(a, b), (a, b)),
               lambda r, g: (_bdgp(g, r[1], 2, 2), _bdgp(r[0], g, 1, 1)))


def _sigmoid(x):
    return 0.5 * jnp.tanh(0.5 * x) + 0.5


def _silu(x):
    return x * _sigmoid(x)


def _softplus(x):
    return jnp.maximum(x, 0.0) + jnp.log(1.0 + jnp.exp(-jnp.abs(x)))


def _layernorm(r, g, b):
    mu = jnp.mean(r, axis=-1, keepdims=True)
    xc = r - mu
    var = jnp.mean(xc * xc, axis=-1, keepdims=True)
    return xc * lax.rsqrt(var + LN_EPS) * g + b


def _iota2(shape, dim):
    return lax.broadcasted_iota(jnp.int32, shape, dim)


@jax.custom_vjp
def causal_conv(ext, rows):
    k = len(rows)
    y = None
    for j in range(k):
        s = k - 1 - j
        r = pltpu.roll(ext, s, 0) if s else ext
        t = r[HALO:] * rows[j]
        y = t if y is None else y + t
    return y


def _causal_conv_fwd(ext, rows):
    return causal_conv(ext, rows), (ext, rows)


def _causal_conv_bwd(res, g):
    ext, rows = res
    n = ext.shape[0]
    k = len(rows)
    gext = jnp.concatenate([jnp.zeros((HALO, g.shape[1]), g.dtype), g], axis=0)
    dext = None
    drows = []
    for j in range(k):
        s = k - 1 - j
        up = pltpu.roll(gext, n - s, 0) if s else gext
        t = up * rows[j]
        dext = t if dext is None else dext + t
        r = pltpu.roll(ext, s, 0) if s else ext
        drows.append(jnp.sum(g * r[HALO:], axis=0, keepdims=True))
    return dext, tuple(drows)


causal_conv.defvjp(_causal_conv_fwd, _causal_conv_bwd)


def _chunk_masks(tm):
    i = _iota2((tm, tm), 0)
    j = _iota2((tm, tm), 1)
    same = (i ^ j) < CH
    lower = jnp.where(same & (j <= i), 1.0, 0.0).astype(F32)
    upper = jnp.where(same & (i <= j), 1.0, 0.0).astype(F32)
    return lower, upper


@jax.custom_vjp
def chunk_cumsum(g):
    lower, _ = _chunk_masks(g.shape[0])
    return _mmh(lower, g)


def _chunk_cumsum_bwd(_, ct):
    _, upper = _chunk_masks(ct.shape[0])
    return (_mmh(upper, ct),)


chunk_cumsum.defvjp(lambda g: (chunk_cumsum(g), None), _chunk_cumsum_bwd)


@jax.custom_vjp
def inv_unit_lower(a):
    n = a.shape[-1]
    eye = jnp.where(_iota2((1, n, n), 1) == _iota2((1, n, n), 2), 1.0, 0.0).astype(F32)
    x = eye - a
    p = _bdg3(a, a, 2, 1)
    steps = int(math.log2(n)) - 1
    for s in range(steps):
        x = x + _bdg3(x, p, 2, 1)
        if s + 1 < steps:
            p = _bdg3(p, p, 2, 1)
    for _ in range(NEWTON_STEPS):
        r = (eye - x) - _bdg3(a, x, 2, 1)
        x = x + _bdg3(x, r, 2, 1)
    return x


def _inv_fwd(a):
    t = inv_unit_lower(a)
    return t, t


def _inv_bwd(t, g):
    return (-_bdgp(_bdgp(t, g, 1, 1), t, 2, 2),)


inv_unit_lower.defvjp(_inv_fwd, _inv_bwd)


@jax.custom_vjp
def inv_known(a, t):
    return t


inv_known.defvjp(lambda a, t: (t, t), lambda t, g: (_inv_bwd(t, g)[0], jnp.zeros_like(t)))


def prep_head_fn(ext, rows, scale):
    s = _silu(causal_conv(ext, rows))
    if scale is None:
        return s
    return s * (lax.rsqrt(jnp.sum(s * s, axis=-1, keepdims=True) + L2_EPS) * scale)


def prep_gate_fn(bb, aa, a_log, dtb):
    g = -jnp.exp(a_log) * _softplus(aa + dtb)
    return chunk_cumsum(g), _sigmoid(bb)


PREP_SCALES = (ADK ** -0.5, 1.0, None)


def _head_cols(a, heads):
    lane = _iota2((1, LANE), 1)
    return jnp.concatenate([jnp.sum(jnp.where(lane == h, a, 0.0), axis=1, keepdims=True)[None]
                            for h in heads], axis=0)


def _head_rows(a, heads):
    at = a.T[:AH]
    sub = _iota2((AH, 1), 0)
    return jnp.concatenate([jnp.sum(jnp.where(sub == h, at, 0.0), axis=0, keepdims=True)[None]
                            for h in heads], axis=0)


def c1_heads(q, k, v, gcs, beta, tinv_saved=None):
    heads = range(AH)
    gcol = jnp.concatenate([_head_cols(g, heads) for g in gcs], axis=0)
    grow = jnp.concatenate([_head_rows(g, heads) for g in gcs], axis=0)
    bcol = jnp.concatenate([_head_cols(b, heads) for b in beta], axis=0)
    i = _iota2((1, CH, CH), 1)
    j = _iota2((1, CH, CH), 2)
    causal = j <= i
    strict = j < i
    diff = gcol - grow
    decay = jnp.where(causal, jnp.exp(jnp.where(causal, diff, 0.0)), 0.0)
    kb = k * bcol
    vb = v * bcol
    a_low = jnp.where(strict, bmm_nt(kb, k) * decay, 0.0)
    tinv = inv_unit_lower(a_low) if tinv_saved is None else inv_known(a_low, tinv_saved)
    egc = jnp.exp(gcol)
    u = bmm3_nn(tinv, vb)
    w = bmm3_nn(tinv, kb * egc)
    qk = jnp.where(causal, bmm_nt(q, k) * decay, 0.0)
    glast = jnp.sum(jnp.where(_iota2((1, CH, 1), 1) == CH - 1, gcol, 0.0), axis=1, keepdims=True)
    qg = q * egc
    kd = k * jnp.exp(glast - gcol)
    eg = jnp.exp(glast) * jnp.ones((1, 1, ADK), F32)
    return u, w, qk, qg, kd, eg, tinv


def c2_heads(s, u, w, qk, qg, kd, eg, z, nw):
    vn = u - bmm_nn(w, s)
    o = bmm_nn(qg, s) + bmm_nn(qk, vn)
    s2 = s * eg + bmm_tn(kd, vn)
    ms = jnp.mean(o * o, axis=-1, keepdims=True)
    og = o * lax.rsqrt(ms + RMS_EPS) * nw * _silu(z)
    return og, s2


ATT_SCALE = BDH ** -0.5


def _head_mask(hh):
    lane = _iota2((1, 2 * BDH), 1)
    return jnp.where((lane >= hh * BDH) & (lane < (hh + 1) * BDH), 1.0, 0.0).astype(F32)


def attn_sub_fwd(q, k, v, bias2, r, firstf):
    col = _iota2((1, KWIN), 1) + r * SUBQ
    nokey = jnp.where(col < TQ, firstf, 0.0) * NEG
    out, probs = None, []
    for hh in range(2):
        hm = _head_mask(hh)
        s = mm_nt(q * (hm * ATT_SCALE), k) + (assemble_bias(bias2[hh], r) + nokey)
        p = jnp.exp(s - jnp.max(s, axis=-1, keepdims=True))
        inv = 1.0 / jnp.sum(p, axis=-1, keepdims=True)
        o = mm_nn(p, v) * (inv * hm)
        out = o if out is None else out + o
        probs.append(p * inv)
    return out, probs


def attn_sub_bwd(q, k, v, o, do, probs, r):
    dq, dk, dv, dss = None, None, None, []
    for hh in range(2):
        hm = _head_mask(hh)
        p = probs[hh]
        doh = do * hm
        ds = p * (mm_nt(doh, v) - jnp.sum(doh * o, axis=-1, keepdims=True))
        dqh = mm_nn(ds, k) * (hm * ATT_SCALE)
        dkh = mm_tn(ds, q * (hm * ATT_SCALE))
        dvh = mm_tn(p, doh)
        dq = dqh if dq is None else dq + dqh
        dk = dkh if dk is None else dk + dkh
        dv = dvh if dv is None else dv + dvh
        dss.append(ds)
    return dq, dk, dv, dss


def merge_fn(x, oa, ob, gra, grb, p_pa, p_pb, p_mix, bga, bgb, gate_t, g1, b1, scale_f, shift_f,
             wa, wb, wo):
    ga = _sigmoid(gra + bga)
    gb = _sigmoid(grb + bgb)
    pa = mm_w(oa, wa) + p_pa
    pb = mm_w(ob, wb) + p_pb
    merged = ga * pa + gb * pb
    mix = mm_w(merged, wo) + p_mix
    y1 = _layernorm(ALPHA * x + gate_t * mix, g1, b1)
    return y1, merged


def ffn_act_fn(extg, extv, rows_g, rows_v, bg, bv):
    return _silu(causal_conv(extg, rows_g) + bg) * (causal_conv(extv, rows_v) + bv)


def head_fn(a, y1, p_ffn, gate_f, g2, b2, tgt, wd):
    ffn = mm_w(a, wd) + p_ffn
    y2 = _layernorm(ALPHA * y1 + gate_f * ffn, g2, b2)
    err = y2 - tgt
    return 0.5 * jnp.sum(jnp.mean(err * err, axis=-1, keepdims=True))


def _rows(tm, width, colblk=0, order=None):
    if order is None:
        return pl.BlockSpec((tm, width), lambda i: (i, colblk))
    return pl.BlockSpec((tm, width), lambda i: (order(i), colblk))


def _const(shape):
    nd = len(shape)
    return pl.BlockSpec(shape, lambda *_: (0,) * nd)


def _pick(n, cands):
    for c in cands:
        if n % c == 0:
            return c
    raise ValueError(f"no tile for {n}")


def _tile(n, cap):
    best = None
    for c in range(LANE, min(n, cap) + 1, LANE):
        if n % c == 0:
            best = c
    if best is None:
        raise ValueError(f"no tile for {n}")
    return best


def _onehot_rows(k, j):
    return jnp.where(_iota2((k, 1), 0) == j, 1.0, 0.0).astype(F32)


def _stack_rows(drows):
    k = len(drows)
    out = None
    for j in range(k):
        tj = _onehot_rows(k, j) * drows[j]
        out = tj if out is None else out + tj
    return out


def matmul(a, w, out_dtype, name, ta=False, tb=False):
    kdim, m = a.shape if ta else a.shape[::-1]
    n = w.shape[0] if tb else w.shape[1]
    tm = _tile(m, 2048 if kdim <= 1024 else 1024)
    tn = _tile(n, 1024)
    tk = _tile(kdim, 2560)
    nk = kdim // tk
    a_spec = (pl.BlockSpec((tk, tm), lambda i, j, k: (k, i)) if ta
              else pl.BlockSpec((tm, tk), lambda i, j, k: (i, k)))
    w_spec = (pl.BlockSpec((tn, tk), lambda i, j, k: (j, k)) if tb
              else pl.BlockSpec((tk, tn), lambda i, j, k: (k, j)))

    def body(a_ref, w_ref, o_ref, *scratch):
        p = _dg(a_ref[...], w_ref[...], 0 if ta else 1, 1 if tb else 0)
        if nk == 1:
            o_ref[...] = p.astype(out_dtype)
            return
        acc = scratch[0]
        k = pl.program_id(2)

        @pl.when(k == 0)
        def _():
            acc[...] = p

        @pl.when(k > 0)
        def _():
            acc[...] += p

        @pl.when(k == nk - 1)
        def _():
            o_ref[...] = acc[...].astype(out_dtype)

    return pl.pallas_call(
        body, name=name,
        grid=(m // tm, n // tn, nk),
        in_specs=[a_spec, w_spec],
        out_specs=pl.BlockSpec((tm, tn), lambda i, j, k: (i, j)),
        out_shape=jax.ShapeDtypeStruct((m, n), out_dtype),
        scratch_shapes=[] if nk == 1 else [pltpu.VMEM((tm, tn), F32)],
        compiler_params=_cparams(3),
    )(a, w)


def dgrad_modulated(a, w, xin, dres, scale, name):
    m, kdim = a.shape
    n = w.shape[0]
    tm = _tile(m, 1024)
    tk = _tile(kdim, 2560)
    nk = kdim // tk
    assert nk > 1

    def body(a_ref, w_ref, x_ref, r_ref, sc_ref, o_ref, dsc_ref, dsh_ref, acc):
        i = pl.program_id(0)
        k = pl.program_id(1)
        p = _dg(a_ref[...], w_ref[...], 1, 1)

        @pl.when(k == 0)
        def _():
            acc[...] = p

        @pl.when(k > 0)
        def _():
            acc[...] += p

        @pl.when((i == 0) & (k == 0))
        def _():
            dsc_ref[...] = jnp.zeros_like(dsc_ref)
            dsh_ref[...] = jnp.zeros_like(dsh_ref)

        @pl.when(k == nk - 1)
        def _():
            dh = acc[...]
            o_ref[...] = r_ref[...] + dh * (1.0 + sc_ref[...])
            dsc_ref[...] += jnp.sum(dh * x_ref[...], axis=0, keepdims=True)
            dsh_ref[...] += jnp.sum(dh, axis=0, keepdims=True)

    row = pl.BlockSpec((tm, n), lambda i, k: (i, 0))
    vec = pl.BlockSpec((1, n), lambda i, k: (0, 0))
    return pl.pallas_call(
        body, name=name, grid=(m // tm, nk),
        in_specs=[pl.BlockSpec((tm, tk), lambda i, k: (i, k)), pl.BlockSpec((n, tk), lambda i, k: (0, k)),
                  row, row, vec],
        out_specs=[row, vec, vec],
        out_shape=[jax.ShapeDtypeStruct((m, n), F32), jax.ShapeDtypeStruct((1, n), F32),
                   jax.ShapeDtypeStruct((1, n), F32)],
        scratch_shapes=[pltpu.VMEM((tm, n), F32)],
        compiler_params=_cparams(2),
    )(a, w, xin, dres, scale)


def dgrad_pieces(pieces, tail, w, name):
    m = pieces[0][0].shape[0]
    n, ktot = w.shape
    tk = 1024
    tm = _tile(m, 1024)
    wt = tail.shape[1]
    ranges, k0 = [], 0
    for arr, off in pieces:
        assert off == k0 * tk and arr.shape[1] % tk == 0
        ranges.append((k0, k0 + arr.shape[1] // tk))
        k0 = ranges[-1][1]
    nk = k0
    npc = len(pieces)

    def body(*refs):
        a_refs, t_ref, w_ref, wt_ref, o_ref, acc = refs[:npc], refs[npc], refs[npc + 1], refs[npc + 2], refs[npc + 3], refs[npc + 4]
        k = pl.program_id(1)

        @pl.when(k == 0)
        def _():
            acc[...] = _dg(t_ref[...], wt_ref[...], 1, 1)

        for a_ref, (lo, hi) in zip(a_refs, ranges):
            @pl.when((k >= lo) & (k < hi))
            def _(a_ref=a_ref):
                acc[...] += _dg(a_ref[...], w_ref[...], 1, 1)

        @pl.when(k == nk - 1)
        def _():
            o_ref[...] = acc[...]

    def piece_spec(lo, hi):
        return pl.BlockSpec((tm, tk), lambda i, k: (i, jnp.clip(k - lo, 0, hi - lo - 1)))

    return pl.pallas_call(
        body, name=name, grid=(m // tm, nk),
        in_specs=[piece_spec(lo, hi) for lo, hi in ranges] + [
            pl.BlockSpec((tm, wt), lambda i, k: (i, 0)),
            pl.BlockSpec((n, tk), lambda i, k: (0, k)),
            pl.BlockSpec((n, wt), lambda i, k: (0, (ktot - wt) // wt))],
        out_specs=pl.BlockSpec((tm, n), lambda i, k: (i, 0)),
        out_shape=jax.ShapeDtypeStruct((m, n), F32),
        scratch_shapes=[pltpu.VMEM((tm, n), F32)],
        compiler_params=_cparams(2),
    )(*[a for a, _ in pieces], tail, w, w)


def modulate(x, scale, shift, name):
    t, d = x.shape
    tm = _pick(t, (512, 256, 128))

    def body(x_ref, sc_ref, sh_ref, o_ref):
        o_ref[...] = (x_ref[...] * (1.0 + sc_ref[...]) + sh_ref[...]).astype(BF16)

    return pl.pallas_call(
        body, name=name, grid=(t // tm,),
        in_specs=[_rows(tm, d), _const((1, d)), _const((1, d))],
        out_specs=_rows(tm, d),
        out_shape=jax.ShapeDtypeStruct((t, d), BF16),
        compiler_params=_cparams(),
    )(x, scale, shift)


def modulate_bwd(dh, xin, dres, scale, name):
    t, d = dh.shape
    tm = _pick(t, (512, 256, 128))

    def body(dh_ref, x_ref, dr_ref, sc_ref, o_ref, dsc_ref, dsh_ref):
        i = pl.program_id(0)
        dh_v = dh_ref[...]
        o_ref[...] = dr_ref[...] + dh_v * (1.0 + sc_ref[...])

        @pl.when(i == 0)
        def _():
            dsc_ref[...] = jnp.zeros_like(dsc_ref)
            dsh_ref[...] = jnp.zeros_like(dsh_ref)

        dsc_ref[...] += jnp.sum(dh_v * x_ref[...], axis=0, keepdims=True)
        dsh_ref[...] += jnp.sum(dh_v, axis=0, keepdims=True)

    return pl.pallas_call(
        body, name=name, grid=(t // tm,),
        in_specs=[_rows(tm, d), _rows(tm, d), _rows(tm, d), _const((1, d))],
        out_specs=[_rows(tm, d), _const((1, d)), _const((1, d))],
        out_shape=[jax.ShapeDtypeStruct((t, d), F32), jax.ShapeDtypeStruct((1, d), F32),
                   jax.ShapeDtypeStruct((1, d), F32)],
        compiler_params=_cparams(),
    )(dh, xin, dres, scale)


PREP_TM = 128


def _halo_specs(tm, width, colblk, order):
    per = tm // HALO
    return [pl.BlockSpec((HALO, width), lambda i: (jnp.maximum(order(i) * per - 1, 0), colblk)),
            pl.BlockSpec((tm, width), lambda i: (order(i), colblk))]


def prep_fwd(proj, conv_a, a_log, dtb):
    t = proj.shape[0]
    tm = PREP_TM
    nt = t // tm
    wq = 3 * D

    def body(prev_ref, cur_ref, bb_ref, aa_ref, cw_ref, al_ref, dt_ref, q_ref, k_ref, v_ref, g_ref, b_ref):
        i = pl.program_id(0)
        flag = jnp.where(i > 0, 1.0, 0.0)
        for part, o_ref in enumerate((q_ref, k_ref, v_ref)):
            for h in range(AH):
                sl = slice(part * D + h * ADK, part * D + (h + 1) * ADK)
                ext = jnp.concatenate([prev_ref[:, sl] * flag, cur_ref[:, sl]], axis=0)
                rows = tuple(cw_ref[j:j + 1, sl] for j in range(4))
                o_ref[h] = prep_head_fn(ext, rows, PREP_SCALES[part])
        gcs, beta = prep_gate_fn(bb_ref[...], aa_ref[...], al_ref[...], dt_ref[...])
        g_ref[...] = gcs
        b_ref[...] = beta

    ident = lambda i: i
    hm = pl.BlockSpec((AH, tm, ADK), lambda i: (0, i, 0))
    return pl.pallas_call(
        body, name="prep_fwd", grid=(nt,),
        in_specs=_halo_specs(tm, wq, 0, ident) + [
            _rows(tm, 128, C_BA // 128), _rows(tm, 128, C_BA // 128 + 1),
            _const((4, wq)), _const((1, 128)), _const((1, 128))],
        out_specs=[hm, hm, hm, _rows(tm, 128), _rows(tm, 128)],
        out_shape=[jax.ShapeDtypeStruct((AH, t, ADK), F32)] * 3 + [jax.ShapeDtypeStruct((t, 128), F32)] * 2,
        compiler_params=_cparams(),
    )(proj, proj, proj, proj, conv_a, a_log, dtb)


def prep_bwd(proj, conv_a, a_log, dtb, dq, dk, dv, dgcs, dbeta):
    t = proj.shape[0]
    tm = PREP_TM
    nt = t // tm
    wq = 3 * D
    rev = lambda i: nt - 1 - i

    def body(prev_ref, cur_ref, bb_ref, aa_ref, cw_ref, al_ref, dt_ref,
             dq_ref, dk_ref, dv_ref, dg_ref, db_ref,
             dpre_ref, dbb_ref, daa_ref, dcw_ref, dal_ref, ddt_ref, carry):
        i = pl.program_id(0)
        flag = jnp.where(i < nt - 1, 1.0, 0.0)

        @pl.when(i == 0)
        def _():
            carry[...] = jnp.zeros_like(carry)
            dcw_ref[...] = jnp.zeros_like(dcw_ref)
            dal_ref[...] = jnp.zeros_like(dal_ref)
            ddt_ref[...] = jnp.zeros_like(ddt_ref)

        for part, d_ref in enumerate((dq_ref, dk_ref, dv_ref)):
            for h in range(AH):
                sl = slice(part * D + h * ADK, part * D + (h + 1) * ADK)
                ext = jnp.concatenate([prev_ref[:, sl] * flag, cur_ref[:, sl]], axis=0)
                rows = tuple(cw_ref[j:j + 1, sl] for j in range(4))
                _, vjp = jax.vjp(lambda e, r: prep_head_fn(e, r, PREP_SCALES[part]), ext, rows)
                dext, drows = vjp(d_ref[h])
                dcur = dext[HALO:]
                dpre_ref[:, sl] = jnp.concatenate([dcur[:tm - HALO], dcur[tm - HALO:] + carry[:, sl]],
                                                  axis=0).astype(BF16)
                carry[:, sl] = dext[:HALO]
                dcw_ref[:, sl] += _stack_rows(drows)
        _, vjp = jax.vjp(prep_gate_fn, bb_ref[...], aa_ref[...], al_ref[...], dt_ref[...])
        dbb, daa, dal, ddt = vjp((dg_ref[...], db_ref[...]))
        dbb_ref[...] = dbb.astype(BF16)
        daa_ref[...] = daa.astype(BF16)
        dal_ref[...] += dal
        ddt_ref[...] += ddt

    hm = pl.BlockSpec((AH, tm, ADK), lambda i: (0, rev(i), 0))
    return pl.pallas_call(
        body, name="prep_bwd", grid=(nt,),
        in_specs=_halo_specs(tm, wq, 0, rev) + [
            _rows(tm, 128, C_BA // 128, rev), _rows(tm, 128, C_BA // 128 + 1, rev),
            _const((4, wq)), _const((1, 128)), _const((1, 128)),
            hm, hm, hm, _rows(tm, 128, 0, rev), _rows(tm, 128, 0, rev)],
        out_specs=[_rows(tm, wq, 0, rev), _rows(tm, 128, 0, rev), _rows(tm, 128, 0, rev),
                   _const((4, wq)), _const((1, 128)), _const((1, 128))],
        out_shape=[jax.ShapeDtypeStruct((t, wq), BF16), jax.ShapeDtypeStruct((t, 128), BF16),
                   jax.ShapeDtypeStruct((t, 128), BF16), jax.ShapeDtypeStruct((4, wq), F32),
                   jax.ShapeDtypeStruct((1, 128), F32), jax.ShapeDtypeStruct((1, 128), F32)],
        scratch_shapes=[pltpu.VMEM((HALO, wq), F32)],
        compiler_params=_cparams(),
    )(proj, proj, proj, proj, conv_a, a_log, dtb, dq, dk, dv, dgcs, dbeta)


def _c1_specs(order, cps=1):
    hm = pl.BlockSpec((AH, cps * CH, ADK), lambda n: (0, order(n), 0))
    col = pl.BlockSpec((cps * CH, LANE), lambda n: (order(n), 0))
    qk = pl.BlockSpec((cps, AH, CH, CH), lambda n: (order(n), 0, 0, 0))
    eg = pl.BlockSpec((cps, AH, 1, ADK), lambda n: (order(n), 0, 0, 0))
    return hm, col, qk, eg


C1_CPS = 4


def _heads(ref, rows=slice(None)):
    return jnp.stack([ref[rows, h * ADK:(h + 1) * ADK] for h in range(AH)], axis=0)


C2_CPS = 4


def _chunks(ref):
    return jnp.concatenate([ref[:, c * CH:(c + 1) * CH] for c in range(C1_CPS)], axis=0)


def _chunk_rows(ref):
    return [ref[c * CH:(c + 1) * CH] for c in range(C1_CPS)]


def c1_fwd(q, k, v, gcs, beta):
    t = q.shape[1]
    nc = t // CH
    hm, col, qks, egs = _c1_specs(lambda n: n, C1_CPS)

    def body(q_ref, k_ref, v_ref, g_ref, b_ref, u_ref, w_ref, qg_ref, kd_ref, qk_ref, eg_ref, ti_ref):
        u, w, qk, qg, kd, eg, tinv = c1_heads(_chunks(q_ref), _chunks(k_ref), _chunks(v_ref), _chunk_rows(g_ref),
                                              _chunk_rows(b_ref))
        for c in range(C1_CPS):
            rows, hs = slice(c * CH, (c + 1) * CH), slice(c * AH, (c + 1) * AH)
            u_ref[:, rows] = u[hs]
            w_ref[:, rows] = w[hs].astype(BF16)
            qg_ref[:, rows] = qg[hs].astype(BF16)
            kd_ref[:, rows] = kd[hs].astype(BF16)
            qk_ref[c] = qk[hs].astype(BF16)
            eg_ref[c] = eg[hs]
            ti_ref[c] = tinv[hs]

    return pl.pallas_call(
        body, name="c1_fwd", grid=(nc // C1_CPS,),
        in_specs=[hm, hm, hm, col, col],
        out_specs=[hm, hm, hm, hm, qks, egs, qks],
        out_shape=[jax.ShapeDtypeStruct((AH, t, ADK), F32)] + [jax.ShapeDtypeStruct((AH, t, ADK), BF16)] * 3 + [
            jax.ShapeDtypeStruct((nc, AH, CH, CH), BF16), jax.ShapeDtypeStruct((nc, AH, 1, ADK), F32),
            jax.ShapeDtypeStruct((nc, AH, CH, CH), F32)],
        compiler_params=_cparams(),
    )(q, k, v, gcs, beta)


def c1_bwd(q, k, v, gcs, beta, tinv, du, dw, dqg, dkd, dqk, deg):
    t = q.shape[1]
    nc = t // CH
    hm, col, qks, egs = _c1_specs(lambda n: n, C1_CPS)

    def lead(ref):
        return jnp.concatenate([ref[c] for c in range(C1_CPS)], axis=0)

    def body(q_ref, k_ref, v_ref, g_ref, b_ref, ti_ref, du_ref, dw_ref, dqg_ref, dkd_ref, dqk_ref, deg_ref,
             dq_ref, dk_ref, dv_ref, dg_ref, db_ref):
        tinv = lead(ti_ref)
        _, vjp = jax.vjp(lambda q_, k_, v_, g_, b_: c1_heads(q_, k_, v_, g_, b_, tinv),
                         _chunks(q_ref), _chunks(k_ref), _chunks(v_ref), _chunk_rows(g_ref), _chunk_rows(b_ref))
        dq, dk, dv, dg, db = vjp((_chunks(du_ref).astype(F32), _chunks(dw_ref).astype(F32), lead(dqk_ref),
                                  _chunks(dqg_ref), _chunks(dkd_ref), lead(deg_ref),
                                  jnp.zeros((C1_CPS * AH, CH, CH), F32)))
        for c in range(C1_CPS):
            rows, hs = slice(c * CH, (c + 1) * CH), slice(c * AH, (c + 1) * AH)
            dq_ref[:, rows] = dq[hs]
            dk_ref[:, rows] = dk[hs]
            dv_ref[:, rows] = dv[hs]
            dg_ref[rows] = dg[c]
            db_ref[rows] = db[c]

    return pl.pallas_call(
        body, name="c1_bwd", grid=(nc // C1_CPS,),
        in_specs=[hm, hm, hm, col, col, qks, hm, hm, hm, hm, qks, egs],
        out_specs=[hm, hm, hm, col, col],
        out_shape=[jax.ShapeDtypeStruct((AH, t, ADK), F32)] * 3 + [jax.ShapeDtypeStruct((t, LANE), F32)] * 2,
        compiler_params=_cparams(),
    )(q, k, v, gcs, beta, tinv, du, dw, dqg, dkd, dqk, deg)


def c2_fwd(u, w, qg, kd, qk, eg, proj, norm_a):
    t = u.shape[1]
    nc = t // CH
    cps = C2_CPS
    hm, _, qks, egs = _c1_specs(lambda n: n, cps)
    tok = pl.BlockSpec((cps * CH, D), lambda n: (n, 0))
    zspec = pl.BlockSpec((cps * CH, D), lambda n: (n, C_Z // D))
    sspec = pl.BlockSpec((cps, AH, ADK, ADK), lambda n: (n, 0, 0, 0))

    def body(u_ref, w_ref, qg_ref, kd_ref, qk_ref, eg_ref, z_ref, nw_ref, o_ref, sall_ref, st):
        n = pl.program_id(0)

        @pl.when(n == 0)
        def _():
            st[...] = jnp.zeros_like(st)

        for c in range(cps):
            rows = slice(c * CH, (c + 1) * CH)
            s = st[...]
            sall_ref[c] = s
            og, s2 = c2_heads(s, u_ref[:, rows], w_ref[:, rows], qk_ref[c], qg_ref[:, rows], kd_ref[:, rows],
                              eg_ref[c], _heads(z_ref, rows), nw_ref[...])
            st[...] = s2
            for h in range(AH):
                o_ref[rows, h * ADK:(h + 1) * ADK] = og[h].astype(BF16)

    return pl.pallas_call(
        body, name="c2_fwd", grid=(nc // cps,),
        in_specs=[hm, hm, hm, hm, qks, egs, zspec, _const((1, ADK))],
        out_specs=[tok, sspec],
        out_shape=[jax.ShapeDtypeStruct((t, D), BF16), jax.ShapeDtypeStruct((nc, AH, ADK, ADK), F32)],
        scratch_shapes=[pltpu.VMEM((AH, ADK, ADK), F32)],
        compiler_params=_cparams(),
    )(u, w, qg, kd, qk, eg, proj, norm_a)


def c2_bwd(u, w, qg, kd, qk, eg, proj, norm_a, sall, do):
    t = u.shape[1]
    nc = t // CH
    cps = C2_CPS
    rev = lambda n: nc // cps - 1 - n
    hm, _, qks, egs = _c1_specs(rev, cps)
    tok = pl.BlockSpec((cps * CH, D), lambda n: (rev(n), 0))
    zspec = pl.BlockSpec((cps * CH, D), lambda n: (rev(n), C_Z // D))
    sspec = pl.BlockSpec((cps, AH, ADK, ADK), lambda n: (rev(n), 0, 0, 0))

    def body(u_ref, w_ref, qg_ref, kd_ref, qk_ref, eg_ref, z_ref, nw_ref, sall_ref, do_ref,
             du_ref, dw_ref, dqg_ref, dkd_ref, dqk_ref, deg_ref, dz_ref, dnw_ref, dst):
        n = pl.program_id(0)

        @pl.when(n == 0)
        def _():
            dst[...] = jnp.zeros_like(dst)
            dnw_ref[...] = jnp.zeros_like(dnw_ref)

        for c in reversed(range(cps)):
            rows = slice(c * CH, (c + 1) * CH)
            _, vjp = jax.vjp(c2_heads, sall_ref[c], u_ref[:, rows], w_ref[:, rows].astype(F32),
                             qk_ref[c].astype(F32), qg_ref[:, rows].astype(F32), kd_ref[:, rows].astype(F32),
                             eg_ref[c], _heads(z_ref, rows), nw_ref[...])
            ds, du, dw, dqk, dqg, dkd, deg, dz, dn = vjp((_heads(do_ref, rows), dst[...]))
            dst[...] = ds
            du_ref[:, rows] = du.astype(BF16)
            dw_ref[:, rows] = dw.astype(BF16)
            dqg_ref[:, rows] = dqg
            dkd_ref[:, rows] = dkd
            dqk_ref[c] = dqk
            deg_ref[c] = deg
            for h in range(AH):
                dz_ref[rows, h * ADK:(h + 1) * ADK] = dz[h].astype(BF16)
            dnw_ref[...] += dn

    return pl.pallas_call(
        body, name="c2_bwd", grid=(nc // cps,),
        in_specs=[hm, hm, hm, hm, qks, egs, zspec, _const((1, ADK)), sspec, tok],
        out_specs=[hm, hm, hm, hm, qks, egs, tok, _const((1, ADK))],
        out_shape=[jax.ShapeDtypeStruct((AH, t, ADK), BF16)] * 2 + [jax.ShapeDtypeStruct((AH, t, ADK), F32)] * 2 + [
            jax.ShapeDtypeStruct((nc, AH, CH, CH), F32), jax.ShapeDtypeStruct((nc, AH, 1, ADK), F32),
            jax.ShapeDtypeStruct((t, D), BF16), jax.ShapeDtypeStruct((1, ADK), F32)],
        scratch_shapes=[pltpu.VMEM((AH, ADK, ADK), F32)],
        compiler_params=_cparams(),
    )(u, w, qg, kd, qk, eg, proj, norm_a, sall, do)


NQB = TQ // CH
NKB = 2 * TQ // CH
NDIST = BPREV + 1
KLO = -(NQB - 2)
NPAIR = NKB - 1 - KLO + 1


def bias_table(rel_bias):
    nh = rel_bias.shape[0]
    relx = jnp.concatenate([rel_bias, jnp.broadcast_to(rel_bias[:, -1:], (nh, CH * BPREV + 2 * CH - 1 - RELSZ))],
                           axis=1)
    t = jnp.stack([relx[:, CH * k:CH * k + 2 * CH - 1] for k in range(NDIST)], axis=1)
    trev = t[:, :, ::-1]
    g2 = jnp.concatenate([trev[:, :, CH - 1:], jnp.zeros((nh, NDIST, 1), F32), trev[:, :, :CH - 1]], axis=2)
    flat = jnp.tile(g2, (1, 1, CH + 1))[:, :, :CH * (2 * CH - 1)]
    blk = flat.reshape(nh, NDIST, CH, 2 * CH - 1)[..., :CH]
    neg = jnp.full((nh, NQB - 1, CH, CH), NEG, F32)
    asc = jnp.concatenate([neg, blk, neg], axis=1)
    return jnp.concatenate([asc[:, 1:], asc[:, :-1]], axis=-1)


SUBQ = 4 * CH
NSUB = TQ // SUBQ
KWIN = SUBQ + BPREV * CH


def assemble_bias(tab, r):
    b0 = r * SUBQ // (2 * CH)
    rows = [jnp.concatenate([tab[NQB + a - 2 * b - KLO] for b in range(b0, b0 + KWIN // (2 * CH))], axis=1)
            for a in range(r * SUBQ // CH, (r + 1) * SUBQ // CH)]
    return jnp.concatenate(rows, axis=0)


def bias_table_bwd_layout(dtab):
    nh = dtab.shape[0]
    dasc = (jnp.pad(dtab[..., :CH], ((0, 0), (1, 0), (0, 0), (0, 0)))
            + jnp.pad(dtab[..., CH:], ((0, 0), (0, 1), (0, 0), (0, 0))))
    dblk = dasc[:, NQB - 1:NQB - 1 + NDIST]
    dr = jnp.pad(dblk, ((0, 0), (0, 0), (0, 0), (0, CH - 1)))
    flat = jnp.pad(dr.reshape(nh, NDIST, CH * (2 * CH - 1)), ((0, 0), (0, 0), (0, 3 * CH)))
    return flat.reshape(nh, NDIST, CH + 1, 2 * CH).transpose(0, 2, 1, 3).reshape(nh, CH + 1, NDIST * 2 * CH)


def _fold_matrix_np():
    f = np.zeros((NDIST * 2 * CH, 384), np.float32)
    for k in range(NDIST):
        s = k
        for xx in range(2 * CH):
            if xx == CH:
                continue
            m = CH - 1 - xx if xx < CH else 3 * CH - 1 - xx
            f[s * 2 * CH + xx, min(CH * k + m, RELSZ - 1)] = 1.0
    return f


def relbias_reduce(dlay):
    nh, rows, cols = dlay.shape
    rpad = (-rows) % 8
    dlay = jnp.pad(dlay, ((0, 0), (0, rpad), (0, 0)))
    fold = jnp.asarray(_fold_matrix_np())

    def body(d_ref, f_ref, o_ref):
        cs = jnp.sum(d_ref[0], axis=0, keepdims=True)
        o_ref[0] = _mmh(jnp.broadcast_to(cs, (8, cols)), f_ref[...])

    out = pl.pallas_call(
        body, name="relbias_reduce", grid=(nh,),
        in_specs=[pl.BlockSpec((1, rows + rpad, cols), lambda h: (h, 0, 0)), _const((cols, 384))],
        out_specs=pl.BlockSpec((1, 8, 384), lambda h: (h, 0, 0)),
        out_shape=jax.ShapeDtypeStruct((nh, 8, 384), F32),
        compiler_params=_cparams(),
    )(dlay, fold)
    return out[:, 0, :RELSZ]


ATT_PP = 2
ATT_W = ATT_PP * 2 * BDH


def attn_fwd(proj, bias):
    t = proj.shape[0]
    nt = t // TQ
    cb = C_QKVB // ATT_W

    def body(q_ref, kp_ref, kc_ref, vp_ref, vc_ref, b_ref, o_ref, p_ref):
        i = pl.program_id(1)
        firstf = jnp.where(i == 0, 1.0, 0.0)
        for pp in range(ATT_PP):
            ln = slice(pp * 2 * BDH, (pp + 1) * 2 * BDH)
            for r in range(NSUB):
                lo, hi = r * SUBQ, r * SUBQ + KWIN - TQ
                kw = jnp.concatenate([kp_ref[lo:, ln], kc_ref[:hi, ln]], axis=0)
                vw = jnp.concatenate([vp_ref[lo:, ln], vc_ref[:hi, ln]], axis=0)
                out, probs = attn_sub_fwd(q_ref[lo:lo + SUBQ, ln].astype(F32), kw, vw, b_ref[2 * pp:2 * pp + 2], r,
                                          firstf)
                o_ref[lo:lo + SUBQ, ln] = out.astype(BF16)
                for hh in range(2):
                    p_ref[2 * pp + hh, lo:lo + SUBQ, :] = probs[hh].astype(BF16)

    def blk(off, prev):
        if prev:
            return pl.BlockSpec((TQ, ATT_W), lambda p, i: (jnp.maximum(i - 1, 0), cb + off + p))
        return pl.BlockSpec((TQ, ATT_W), lambda p, i: (i, cb + off + p))

    nb = D // ATT_W
    return pl.pallas_call(
        body, name="attn_fwd", grid=(nb, nt),
        in_specs=[blk(0, False), blk(nb, True), blk(nb, False), blk(2 * nb, True), blk(2 * nb, False),
                  pl.BlockSpec((2 * ATT_PP, NPAIR, CH, 2 * CH), lambda p, i: (p, 0, 0, 0))],
        out_specs=[pl.BlockSpec((TQ, ATT_W), lambda p, i: (i, p)),
                   pl.BlockSpec((2 * ATT_PP, TQ, KWIN), lambda p, i: (p, i, 0))],
        out_shape=[jax.ShapeDtypeStruct((t, D), BF16), jax.ShapeDtypeStruct((BH, t, KWIN), BF16)],
        compiler_params=_cparams(2),
    )(proj, proj, proj, proj, proj, bias)


def attn_bwd(proj, ob, probs, do):
    t = proj.shape[0]
    nt = t // TQ
    cb = C_QKVB // ATT_W

    def body(q_ref, kp_ref, kc_ref, vp_ref, vc_ref, o_ref, p_ref, do_ref,
             dq_ref, dk_ref, dv_ref, db_ref, ck, cv, ak, av):
        i = pl.program_id(1)

        @pl.when(i == 0)
        def _():
            ck[...] = jnp.zeros_like(ck)
            cv[...] = jnp.zeros_like(cv)
            db_ref[...] = jnp.zeros_like(db_ref)

        @pl.when(i < nt)
        def _():
            ak[...] = jnp.zeros_like(ak)
            av[...] = jnp.zeros_like(av)
            for pp in range(ATT_PP):
                ln = slice(pp * 2 * BDH, (pp + 1) * 2 * BDH)
                for r in range(NSUB):
                    lo, hi = r * SUBQ, r * SUBQ + KWIN - TQ
                    rows = slice(lo, lo + SUBQ)
                    kw = jnp.concatenate([kp_ref[lo:, ln], kc_ref[:hi, ln]], axis=0)
                    vw = jnp.concatenate([vp_ref[lo:, ln], vc_ref[:hi, ln]], axis=0)
                    probs_r = [p_ref[2 * pp + hh, rows, :].astype(F32) for hh in range(2)]
                    dq, dkw, dvw, dss = attn_sub_bwd(q_ref[rows, ln].astype(F32), kw, vw,
                                                     o_ref[rows, ln].astype(F32), do_ref[rows, ln], probs_r, r)
                    dq_ref[rows, ln] = dq.astype(BF16)
                    ak[lo:lo + KWIN, ln] += dkw
                    av[lo:lo + KWIN, ln] += dvw
                    for hh in range(2):
                        _, scatter = jax.vjp(lambda tab: assemble_bias(tab, r), jnp.zeros((NPAIR, CH, 2 * CH), F32))
                        db_ref[2 * pp + hh] += scatter(dss[hh])[0]
            dk_ref[...] = (ck[...] + ak[:TQ, :]).astype(BF16)
            dv_ref[...] = (cv[...] + av[:TQ, :]).astype(BF16)
            ck[...] = ak[TQ:, :]
            cv[...] = av[TQ:, :]

        @pl.when(i == nt)
        def _():
            dk_ref[...] = ck[...].astype(BF16)
            dv_ref[...] = cv[...].astype(BF16)

    def blk(off, prev):
        if prev:
            return pl.BlockSpec((TQ, ATT_W), lambda p, i: (jnp.clip(i - 1, 0, nt - 1), cb + off + p))
        return pl.BlockSpec((TQ, ATT_W), lambda p, i: (jnp.minimum(i, nt - 1), cb + off + p))

    nb = D // ATT_W
    own = pl.BlockSpec((TQ, ATT_W), lambda p, i: (jnp.minimum(i, nt - 1), p))
    lag = pl.BlockSpec((TQ, ATT_W), lambda p, i: (jnp.maximum(i - 1, 0), p))
    return pl.pallas_call(
        body, name="attn_bwd", grid=(nb, nt + 1),
        in_specs=[blk(0, False), blk(nb, True), blk(nb, False), blk(2 * nb, True), blk(2 * nb, False), own,
                  pl.BlockSpec((2 * ATT_PP, TQ, KWIN), lambda p, i: (p, jnp.minimum(i, nt - 1), 0)), own],
        out_specs=[own, lag, lag, pl.BlockSpec((2 * ATT_PP, NPAIR, CH, 2 * CH), lambda p, i: (p, 0, 0, 0))],
        out_shape=[jax.ShapeDtypeStruct((t, D), BF16)] * 3 + [jax.ShapeDtypeStruct((BH, NPAIR, CH, 2 * CH), F32)],
        scratch_shapes=[pltpu.VMEM((TQ, ATT_W), F32), pltpu.VMEM((TQ, ATT_W), F32),
                        pltpu.VMEM((2 * TQ, ATT_W), F32), pltpu.VMEM((2 * TQ, ATT_W), F32)],
        compiler_params=_cparams(2),
    )(proj, proj, proj, proj, proj, ob, probs, do)


MERGE_TM = 256


def merge_fwd(x, oa, ob, proj, vecs, wa, wb, wo):
    t = x.shape[0]
    tm = MERGE_TM
    names = ("bga", "bgb", "gate_t", "g1", "b1", "scale_f", "shift_f")

    def body(x_ref, oa_ref, ob_ref, gra_ref, grb_ref, *rest):
        vrefs = rest[:7]
        wa_ref, wb_ref, wo_ref, y_ref, h_ref = rest[7:]
        vv = [r[...] for r in vrefs]
        zero = jnp.zeros((tm, D), F32)
        y1, _ = merge_fn(x_ref[...], oa_ref[...], ob_ref[...], gra_ref[...], grb_ref[...], zero, zero, zero,
                         *vv, wa_ref[...], wb_ref[...], wo_ref[...])
        y_ref[...] = y1
        h_ref[...] = (y1 * (1.0 + vv[5]) + vv[6]).astype(BF16)

    return pl.pallas_call(
        body, name="merge_fwd", grid=(t // tm,),
        in_specs=[_rows(tm, D), _rows(tm, D), _rows(tm, D), _rows(tm, D, C_GATE // D), _rows(tm, D, C_GATE // D + 1)]
        + [_const((1, D))] * 7 + [_const((D, D))] * 3,
        out_specs=[_rows(tm, D), _rows(tm, D)],
        out_shape=[jax.ShapeDtypeStruct((t, D), F32), jax.ShapeDtypeStruct((t, D), BF16)],
        compiler_params=_cparams(),
    )(x, oa, ob, proj, proj, *[vecs[n] for n in names], wa, wb, wo)


def merge_bwd(x, oa, ob, proj, vecs, wa, wb, wo, dy1):
    t = x.shape[0]
    tm = MERGE_TM
    names = ("bga", "bgb", "gate_t", "g1", "b1", "scale_f", "shift_f")

    def body(x_ref, oa_ref, ob_ref, gra_ref, grb_ref, *rest):
        vrefs = rest[:7]
        wa_ref, wb_ref, wo_ref, dy_ref = rest[7:11]
        (dx_ref, doa_ref, dob_ref, dga_ref, dgb_ref, mg_ref, dmix_ref, dpa_ref, dpb_ref,
         dbga_ref, dbgb_ref, dgt_ref, dg1_ref, db1_ref) = rest[11:]
        i = pl.program_id(0)
        vv = [r[...] for r in vrefs]
        zero = jnp.zeros((tm, D), F32)

        def f(x_, oa_, ob_, gra_, grb_, ppa, ppb, pmix, bga, bgb, gate_t, g1, b1):
            return merge_fn(x_, oa_, ob_, gra_, grb_, ppa, ppb, pmix, bga, bgb, gate_t, g1, b1, vv[5], vv[6],
                            wa_ref[...], wb_ref[...], wo_ref[...])

        _, vjp, merged = jax.vjp(f, x_ref[...], oa_ref[...].astype(F32), ob_ref[...].astype(F32),
                                 gra_ref[...], grb_ref[...], zero, zero, zero, *vv[:5], has_aux=True)
        dx, doa, dob, dga, dgb, dpa, dpb, dmix, dbga, dbgb, dgt, dg1, db1 = vjp(dy_ref[...])
        dx_ref[...] = dx
        doa_ref[...] = doa
        dob_ref[...] = dob
        dga_ref[...] = dga.astype(BF16)
        dgb_ref[...] = dgb.astype(BF16)
        mg_ref[...] = merged.astype(BF16)
        dmix_ref[...] = dmix.astype(BF16)
        dpa_ref[...] = dpa.astype(BF16)
        dpb_ref[...] = dpb.astype(BF16)
        accs = (dbga_ref, dbgb_ref, dgt_ref, dg1_ref, db1_ref)

        @pl.when(i == 0)
        def _():
            for a in accs:
                a[...] = jnp.zeros_like(a)

        for a, val in zip(accs, (dbga, dbgb, dgt, dg1, db1)):
            a[...] += val

    return pl.pallas_call(
        body, name="merge_bwd", grid=(t // tm,),
        in_specs=[_rows(tm, D), _rows(tm, D), _rows(tm, D), _rows(tm, D, C_GATE // D), _rows(tm, D, C_GATE // D + 1)]
        + [_const((1, D))] * 7 + [_const((D, D))] * 3 + [_rows(tm, D)],
        out_specs=[_rows(tm, D)] * 9 + [_const((1, D))] * 5,
        out_shape=[jax.ShapeDtypeStruct((t, D), F32)] * 3 + [jax.ShapeDtypeStruct((t, D), BF16)] * 6
        + [jax.ShapeDtypeStruct((1, D), F32)] * 5,
        compiler_params=_cparams(),
    )(x, oa, ob, proj, proj, *[vecs[n] for n in names], wa, wb, wo, dy1)


FFN_TM = 128


def ffn_act_fwd(up, conv_w, bconv):
    t, wdt = up.shape
    tm = FFN_TM

    def body(prev_ref, cur_ref, cw_ref, bc_ref, a_ref):
        i = pl.program_id(0)
        flag = jnp.where(i > 0, 1.0, 0.0)

        def ext(sl):
            return jnp.concatenate([prev_ref[:, sl] * flag, cur_ref[:, sl]], axis=0)

        def rows(sl):
            return tuple(cw_ref[j:j + 1, sl] for j in range(3))

        for cb in range(DFF // LANE):
            g = slice(cb * LANE, (cb + 1) * LANE)
            v = slice(DFF + cb * LANE, DFF + (cb + 1) * LANE)
            a_ref[:, g] = ffn_act_fn(ext(g), ext(v), rows(g), rows(v), bc_ref[:, g], bc_ref[:, v]).astype(BF16)

    return pl.pallas_call(
        body, name="ffn_act_fwd", grid=(t // tm,),
        in_specs=_halo_specs(tm, wdt, 0, lambda i: i) + [_const((3, wdt)), _const((1, wdt))],
        out_specs=_rows(tm, DFF),
        out_shape=jax.ShapeDtypeStruct((t, DFF), BF16),
        compiler_params=_cparams(),
    )(up, up, conv_w, bconv)


def ffn_act_bwd(up, conv_w, bconv, da):
    t, wdt = up.shape
    tm = FFN_TM
    nt = t // tm
    rev = lambda i: nt - 1 - i

    def body(prev_ref, cur_ref, cw_ref, bc_ref, da_ref, dup_ref, dcw_ref, dbc_ref, carry):
        i = pl.program_id(0)
        flag = jnp.where(i < nt - 1, 1.0, 0.0)

        @pl.when(i == 0)
        def _():
            carry[...] = jnp.zeros_like(carry)
            dcw_ref[...] = jnp.zeros_like(dcw_ref)
            dbc_ref[...] = jnp.zeros_like(dbc_ref)

        def ext(sl):
            return jnp.concatenate([prev_ref[:, sl] * flag, cur_ref[:, sl]], axis=0)

        def rows(sl):
            return tuple(cw_ref[j:j + 1, sl] for j in range(3))

        def emit(sl, dext, drows, dbc):
            dcur = dext[HALO:]
            dup_ref[:, sl] = jnp.concatenate([dcur[:tm - HALO], dcur[tm - HALO:] + carry[:, sl]], axis=0).astype(BF16)
            carry[:, sl] = dext[:HALO]
            dcw_ref[:, sl] += _stack_rows(drows)
            dbc_ref[:, sl] += dbc

        for cb in range(DFF // LANE):
            g = slice(cb * LANE, (cb + 1) * LANE)
            v = slice(DFF + cb * LANE, DFF + (cb + 1) * LANE)
            _, vjp = jax.vjp(ffn_act_fn, ext(g), ext(v), rows(g), rows(v), bc_ref[:, g], bc_ref[:, v])
            dxg, dxv, drg, drv, dbg, dbv = vjp(da_ref[:, g])
            emit(g, dxg, drg, dbg)
            emit(v, dxv, drv, dbv)

    return pl.pallas_call(
        body, name="ffn_act_bwd", grid=(nt,),
        in_specs=_halo_specs(tm, wdt, 0, rev) + [_const((3, wdt)), _const((1, wdt)), _rows(tm, DFF, 0, rev)],
        out_specs=[_rows(tm, wdt, 0, rev), _const((3, wdt)), _const((1, wdt))],
        out_shape=[jax.ShapeDtypeStruct((t, wdt), BF16), jax.ShapeDtypeStruct((3, wdt), F32),
                   jax.ShapeDtypeStruct((1, wdt), F32)],
        scratch_shapes=[pltpu.VMEM((HALO, wdt), F32)],
        compiler_params=_cparams(),
    )(up, up, conv_w, bconv, da)


HEAD_TM = 256


def head_fwd_bwd(a, y1, tgt, gate_f, g2, b2, wd):
    t = a.shape[0]
    tm = HEAD_TM

    def body(a_ref, y_ref, t_ref, gf_ref, g2_ref, b2_ref, wd_ref,
             da_ref, dy_ref, dffn_ref, dgf_ref, dg2_ref, db2_ref, loss_ref):
        i = pl.program_id(0)
        zero = jnp.zeros((tm, D), F32)

        def f(a_, y_, pf, gf, g2_, b2_):
            return head_fn(a_, y_, pf, gf, g2_, b2_, t_ref[...], wd_ref[...])

        loss, vjp = jax.vjp(f, a_ref[...].astype(F32), y_ref[...], zero, gf_ref[...], g2_ref[...], b2_ref[...])
        da, dy, dffn, dgf, dg2, db2 = vjp(jnp.ones((), F32))
        da_ref[...] = da
        dy_ref[...] = dy
        dffn_ref[...] = dffn.astype(BF16)
        accs = (dgf_ref, dg2_ref, db2_ref, loss_ref)

        @pl.when(i == 0)
        def _():
            for r in accs:
                r[...] = jnp.zeros_like(r)

        dgf_ref[...] += dgf
        dg2_ref[...] += dg2
        db2_ref[...] += db2
        loss_ref[...] += loss * jnp.ones((1, 128), F32)

    return pl.pallas_call(
        body, name="head_fwd_bwd", grid=(t // tm,),
        in_specs=[_rows(tm, DFF), _rows(tm, D), _rows(tm, D), _const((1, D)), _const((1, D)), _const((1, D)),
                  _const((DFF, D))],
        out_specs=[_rows(tm, DFF), _rows(tm, D), _rows(tm, D), _const((1, D)), _const((1, D)), _const((1, D)),
                   _const((1, 128))],
        out_shape=[jax.ShapeDtypeStruct((t, DFF), F32), jax.ShapeDtypeStruct((t, D), F32),
                   jax.ShapeDtypeStruct((t, D), BF16)] + [jax.ShapeDtypeStruct((1, D), F32)] * 3
        + [jax.ShapeDtypeStruct((1, 128), F32)],
        compiler_params=_cparams(),
    )(a, y1, tgt, gate_f, g2, b2, wd)


def ada_fwd(c_all, w_sh, b_sh):
    def body(c_ref, w_ref, b_ref, o_ref):
        o_ref[...] = _mmh(_silu(c_ref[...]), w_ref[...]) + b_ref[...]

    n = w_sh.shape[1]
    return pl.pallas_call(
        body, name="ada_fwd", out_shape=jax.ShapeDtypeStruct((NDEV, n), F32),
        in_specs=[pl.BlockSpec(memory_space=pltpu.VMEM)] * 3,
        out_specs=pl.BlockSpec(memory_space=pltpu.VMEM),
        compiler_params=pltpu.CompilerParams(vmem_limit_bytes=VMEM_LIMIT),
    )(c_all, w_sh, b_sh)


def ada_wgrad(c_all_t, dmod_sh):
    def body(c_ref, d_ref, o_ref):
        o_ref[...] = _mmh(_silu(c_ref[...]), d_ref[...])

    return pl.pallas_call(
        body, name="ada_wgrad", out_shape=jax.ShapeDtypeStruct((c_all_t.shape[0], dmod_sh.shape[1]), F32),
        in_specs=[pl.BlockSpec(memory_space=pltpu.VMEM)] * 2,
        out_specs=pl.BlockSpec(memory_space=pltpu.VMEM),
        compiler_params=pltpu.CompilerParams(vmem_limit_bytes=VMEM_LIMIT),
    )(c_all_t, dmod_sh)


def adamw(gparts, w, m, v, name):
    p, r, c = gparts.shape
    tr = r if r <= 256 else _pick(r, (256, 128, 64, 32, 16, 8))
    c1 = 1.0 - B1 ** STEP
    c2 = 1.0 - B2 ** STEP

    def body(g_ref, w_ref, m_ref, v_ref, go_ref, d_ref, mo_ref, vo_ref):
        g = g_ref[0].astype(F32)
        for s in range(1, p):
            g = g + g_ref[s].astype(F32)
        mn = B1 * m_ref[0] + (1.0 - B1) * g
        vn = B2 * v_ref[0] + (1.0 - B2) * (g * g)
        go_ref[0] = g
        d_ref[0] = -LR * ((mn / c1) / (jnp.sqrt(vn / c2) + AEPS) + WD * w_ref[0])
        mo_ref[0] = mn
        vo_ref[0] = vn

    spec = pl.BlockSpec((1, tr, c), lambda i: (0, i, 0))
    return pl.pallas_call(
        body, name=name, grid=(r // tr,),
        in_specs=[pl.BlockSpec((p, tr, c), lambda i: (0, i, 0)), spec, spec, spec],
        out_specs=[spec] * 4,
        out_shape=[jax.ShapeDtypeStruct((1, r, c), F32)] * 4,
        compiler_params=_cparams(),
    )(gparts, w, m, v)


def adamw_small(gs, ws, ms, vs, loss_parts, name):
    n = len(ws)
    c1 = 1.0 - B1 ** STEP
    c2 = 1.0 - B2 ** STEP

    def slots(ref):
        acc = ref[0]
        for s in range(1, ref.shape[0]):
            acc = acc + ref[s]
        return acc

    def body(*refs):
        g_refs, w_refs, m_refs, v_refs = (refs[k * n:(k + 1) * n] for k in range(4))
        l_ref, outs = refs[4 * n], refs[4 * n + 1:]
        for i in range(n):
            g = slots(g_refs[i])
            mn = B1 * m_refs[i][...] + (1.0 - B1) * g
            vn = B2 * v_refs[i][...] + (1.0 - B2) * (g * g)
            outs[i][...] = g
            outs[n + i][...] = -LR * ((mn / c1) / (jnp.sqrt(vn / c2) + AEPS) + WD * w_refs[i][...])
            outs[2 * n + i][...] = mn
            outs[3 * n + i][...] = vn
        outs[4 * n][...] = slots(l_ref)

    vmem = pl.BlockSpec(memory_space=pltpu.VMEM)
    outs = pl.pallas_call(
        body, name=name,
        in_specs=[vmem] * (4 * n + 1), out_specs=[vmem] * (4 * n + 1),
        out_shape=[jax.ShapeDtypeStruct(w.shape, F32) for w in ws] * 4 + [jax.ShapeDtypeStruct((1, LANE), F32)],
        compiler_params=pltpu.CompilerParams(vmem_limit_bytes=VMEM_LIMIT),
    )(*gs, *ws, *ms, *vs, loss_parts)
    return outs[:n], outs[n:2 * n], outs[2 * n:3 * n], outs[3 * n:4 * n], outs[4 * n]


def _me():
    x, y, c = lax.axis_index("x"), lax.axis_index("y"), lax.axis_index("c")
    return x, y, c, 4 * x + 2 * y + c


def _peer(x, y, c, d):
    px = 1 - x if (d >> 2) & 1 else x
    py = 1 - y if (d >> 1) & 1 else y
    pc = 1 - c if d & 1 else c
    return (px, py, pc), 4 * px + 2 * py + pc


def _exchange(arrs, name, scatter):
    n = len(arrs)

    def body(*refs):
        ins, outs = refs[:n], refs[n:2 * n]
        send, recv, lsem = refs[2 * n:]
        x, y, c, me = _me()
        remote, local = [], []
        for k in range(n):
            src = ins[k].at[me] if scatter else ins[k]
            cp = pltpu.make_async_copy(src, outs[k].at[me], lsem.at[k])
            cp.start()
            local.append(cp)
            for d in range(1, NDEV):
                dev, pid = _peer(x, y, c, d)
                src = ins[k].at[pid] if scatter else ins[k]
                cp = pltpu.make_async_remote_copy(src_ref=src, dst_ref=outs[k].at[me],
                                                  send_sem=send.at[k, d - 1], recv_sem=recv.at[k, d - 1],
                                                  device_id=dev, device_id_type=pl.DeviceIdType.MESH)
                cp.start()
                remote.append(cp)
        for cp in remote:
            cp.wait()
        for cp in local:
            cp.wait()

    shapes = [a.shape if scatter else (NDEV,) + a.shape for a in arrs]
    return pl.pallas_call(
        body, name=name,
        in_specs=[pl.BlockSpec(memory_space=pl.ANY)] * n,
        out_specs=[pl.BlockSpec(memory_space=pl.ANY)] * n,
        out_shape=[jax.ShapeDtypeStruct(s, a.dtype) for s, a in zip(shapes, arrs)],
        scratch_shapes=[pltpu.SemaphoreType.DMA((n, NDEV - 1)), pltpu.SemaphoreType.DMA((n, NDEV - 1)),
                        pltpu.SemaphoreType.DMA((n,))],
        compiler_params=pltpu.CompilerParams(has_side_effects=True),
    )(*arrs)


def all_gather(arrs, name):
    return _exchange(arrs, name, False)


def all_gather_two_level(shard, name):
    def body(x_ref, out_ref, send, recv, lsem):
        x, y, c, _ = _me()
        sibling = (x, y, 1 - c)
        chips = [(1 - x, y), (x, 1 - y), (1 - x, 1 - y)]

        def slot(px, py, pc):
            return out_ref.at[4 * px + 2 * py + pc]

        def copy(k, block, to, src=None):
            return pltpu.make_async_remote_copy(
                src_ref=slot(*block) if src is None else src, dst_ref=slot(*block),
                send_sem=send.at[k], recv_sem=recv.at[k], device_id=to, device_id_type=pl.DeviceIdType.MESH)

        mine = pltpu.make_async_copy(x_ref, slot(x, y, c), lsem)
        mine.start()
        first = [copy(0, (x, y, c), sibling, src=x_ref)]
        first += [copy(1 + j, (x, y, c), (*chip, c), src=x_ref) for j, chip in enumerate(chips)]
        for cp in first:
            cp.start()
        passed = [copy(4 + j, (*chip, c), sibling) for j, chip in enumerate(chips)]
        for j, chip in enumerate(chips):
            copy(1 + j, (*chip, c), (x, y, c)).wait_recv()
            passed[j].start()
        copy(0, sibling, (x, y, c)).wait_recv()
        for j, chip in enumerate(chips):
            copy(4 + j, (*chip, 1 - c), (x, y, c)).wait_recv()
        for cp in first + passed:
            cp.wait_send()
        mine.wait()

    return pl.pallas_call(
        body, name=name,
        in_specs=[pl.BlockSpec(memory_space=pl.ANY)],
        out_specs=pl.BlockSpec(memory_space=pl.ANY),
        out_shape=jax.ShapeDtypeStruct((NDEV,) + shard.shape, shard.dtype),
        scratch_shapes=[pltpu.SemaphoreType.DMA((NPEER,)), pltpu.SemaphoreType.DMA((NPEER,)),
                        pltpu.SemaphoreType.DMA],
        compiler_params=pltpu.CompilerParams(has_side_effects=True),
    )(shard)


def all_to_all(arrs, name):
    return _exchange(arrs, name, True)


_HBM = pl.BlockSpec(memory_space=pltpu.HBM)
_SEM = pl.BlockSpec(memory_space=pltpu.SEMAPHORE)
_EFFECT = pltpu.SideEffectType.DATAFLOW_SIDE_EFFECTING
NPEER = NDEV - 1


def exchange_start(arrs, name, scatter):
    n = len(arrs)
    lands = [lax.empty(a.shape if scatter else (NDEV,) + a.shape, a.dtype) for a in arrs]

    def body(*refs):
        ins, lrefs = refs[:n], refs[n:2 * n]
        send, recv, token = refs[2 * n], refs[2 * n + 1], refs[-1]
        x, y, c, me = _me()
        for k in range(n):
            for d in range(1, NDEV):
                dev, pid = _peer(x, y, c, d)
                src = ins[k].at[pid] if scatter else ins[k]
                pltpu.make_async_remote_copy(src_ref=src, dst_ref=lrefs[k].at[me],
                                             send_sem=send.at[k * NPEER + d - 1], recv_sem=recv.at[k * NPEER + d - 1],
                                             device_id=dev, device_id_type=pl.DeviceIdType.MESH).start()
        token[...] = jnp.zeros_like(token)

    thru = [pltpu.HBM(a.shape, a.dtype) for a in list(arrs) + lands]
    outs = pl.pallas_call(
        body, name=name,
        out_shape=(pltpu.SemaphoreType.DMA((n * NPEER,)), pltpu.SemaphoreType.DMA((n * NPEER,)), *thru,
                   jax.ShapeDtypeStruct((8, 128), F32)),
        in_specs=[_HBM] * (2 * n),
        out_specs=(_SEM, _SEM, *([_HBM] * (2 * n)), pl.BlockSpec(memory_space=pltpu.VMEM)),
        input_output_aliases={i: 2 + i for i in range(2 * n)},
        compiler_params=pltpu.CompilerParams(has_side_effects=_EFFECT),
    )(*[pltpu.with_memory_space_constraint(a, pltpu.HBM) for a in list(arrs) + lands])
    handle = dict(send=outs[0], recv=outs[1], src=list(outs[2:2 + n]), land=list(outs[2 + n:2 + 2 * n]),
                  scatter=scatter)
    return handle, outs[-1][0, 0]


def exchange_wait(handle, after, name):
    n = len(handle["src"])
    scatter = handle["scatter"]

    def body(*refs):
        ins, lrefs = refs[:n], refs[n:2 * n]
        send, recv = refs[2 * n], refs[2 * n + 1]
        x, y, c, _ = _me()
        for k in range(n):
            for d in range(1, NDEV):
                dev, _ = _peer(x, y, c, d)
                src = ins[k].at[0] if scatter else ins[k]
                cp = pltpu.make_async_remote_copy(src_ref=src, dst_ref=lrefs[k].at[0],
                                                  send_sem=send.at[k * NPEER + d - 1],
                                                  recv_sem=recv.at[k * NPEER + d - 1],
                                                  device_id=dev, device_id_type=pl.DeviceIdType.MESH)
                cp.wait_send()
                cp.wait_recv()

    arrs = handle["src"] + handle["land"]
    outs = pl.pallas_call(
        body, name=name,
        out_shape=tuple(pltpu.HBM(a.shape, a.dtype) for a in arrs),
        in_specs=[_HBM] * (2 * n) + [_SEM, _SEM, pl.BlockSpec(memory_space=pl.ANY)],
        out_specs=tuple([_HBM] * (2 * n)),
        input_output_aliases={i: i for i in range(2 * n)},
        compiler_params=pltpu.CompilerParams(has_side_effects=_EFFECT),
    )(*arrs, handle["send"], handle["recv"], after)
    me = 4 * lax.axis_index("x") + 2 * lax.axis_index("y") + lax.axis_index("c")
    landed = []
    for own, land in zip(outs[:n], outs[n:]):
        mine = lax.dynamic_index_in_dim(own, me, 0, keepdims=True) if scatter else own[None]
        landed.append(lax.dynamic_update_slice_in_dim(land, mine, me, 0))
    return landed


def _cat_from_slabs(slabs):
    _, k, n = slabs.shape

    def cols(lo, hi):
        parts, c = [], lo
        while c < hi:
            j = c // n
            e = min(hi, (j + 1) * n)
            parts.append(slabs[j][:, c - j * n:e - j * n])
            c = e
        return parts

    def zeros(w):
        return [jnp.zeros((k, w), slabs.dtype)]

    return jnp.concatenate(cols(0, 4096) + cols(4112, 9232) + cols(4096, 4104) + zeros(LANE - AH)
                           + cols(4104, 4112) + zeros(NCAT - C_BA - LANE - AH), axis=1)


IN_PIECES = (("pre", C_QKVA, 3072), ("z", C_Z, 1024), ("qb", C_QKVB, 1024), ("kb", C_QKVB + 1024, 1024),
             ("vb", C_QKVB + 2048, 1024), ("ga", C_GATE, 1024), ("gb", C_GATE + 1024, 1024))
_ORIG_SEGS = ((0, 3072, "pre", 0), (3072, 4096, "z", 0), (4096, 4104, "ba", 0), (4104, 4112, "ba", LANE),
              (4112, 5136, "qb", 0), (5136, 6160, "kb", 0), (6160, 7184, "vb", 0), (7184, 8208, "ga", 0),
              (8208, 9232, "gb", 0))


def _orig_cols_from_pieces(gp, lo, hi):
    parts = []
    for a, b, name, off in _ORIG_SEGS:
        s, e = max(a, lo), min(b, hi)
        if s < e:
            parts.append(gp[name][:, off + s - a:off + e - a])
    return parts[0] if len(parts) == 1 else jnp.concatenate(parts, axis=1)


def _pad128(v):
    return jnp.pad(v, ((0, 0), (0, 128 - v.shape[1])))


def local_step(x, tgt, mod, wts, small, late_weights=None, on_grads=None):
    if on_grads is None:
        on_grads = lambda group, gd: jnp.zeros((), F32)
    t = x.shape[0]
    nc = t // CH
    shift_t, scale_t, gate_t, shift_f, scale_f, gate_f = mod
    wcat = _cat_from_slabs(wts["w_in_slabs"])
    a_log = _pad128(small["a_log"])
    dtb = _pad128(small["dt_bias"])
    vecs = dict(bga=small["b_gate"][:, :D], bgb=small["b_gate"][:, D:], gate_t=gate_t, g1=small["ln1_g"],
                b1=small["ln1_b"], scale_f=scale_f, shift_f=shift_f)

    h1 = modulate(x, scale_t, shift_t, "modulate_t")
    proj = matmul(h1, wcat, F32, "in_proj")
    q, k, v, gcs, beta = prep_fwd(proj, small["conv_a"], a_log, dtb)

    u, w, qg, kd, qk, eg, tinv = c1_fwd(q, k, v, gcs, beta)
    oa, sall = c2_fwd(u, w, qg, kd, qk, eg, proj, small["norm_a"])
    bias = bias_table(small["rel_bias"])
    ob, probs = attn_fwd(proj, bias)
    if late_weights is not None:
        wts = {**wts, **late_weights(ob)}
    y1, h2 = merge_fwd(x, oa, ob, proj, vecs, wts["w_a"], wts["w_b"], wts["w_o"])
    up = matmul(h2, wts["w_up"], F32, "up_proj")
    a = ffn_act_fwd(up, small["conv_ffn"], small["b_conv_ffn"])

    da, dy1_res, dffn, dgate_f, dg2, db2, loss = head_fwd_bwd(a, y1, tgt, gate_f, small["ln2_g"], small["ln2_b"],
                                                            wts["w_down"])
    g_w_down = matmul(a, dffn, BF16, "wgrad_down", ta=True)
    dup, g_conv_ffn, g_bconv = ffn_act_bwd(up, small["conv_ffn"], small["b_conv_ffn"], da)
    g_w_up = matmul(h2, dup, BF16, "wgrad_up", ta=True)
    tok = on_grads("ffn", dict(w_up=g_w_up, w_down=g_w_down))
    dy1, dscale_f, dshift_f = dgrad_modulated(dup, wts["w_up"], y1, dy1_res, scale_f + tok, "dgrad_up")
    (dx_res, doa, dob, dga, dgb, merged, dmix, dpa, dpb,
     dbga, dbgb, dgate_t, dg1, db1) = merge_bwd(x, oa, ob, proj, vecs, wts["w_a"], wts["w_b"], wts["w_o"], dy1)
    g_w_o = matmul(merged, dmix, BF16, "wgrad_o", ta=True)
    g_w_a = matmul(oa, dpa, BF16, "wgrad_a", ta=True)
    g_w_b = matmul(ob, dpb, BF16, "wgrad_b", ta=True)
    tok = on_grads("mix", dict(w_o=g_w_o, w_a=g_w_a, w_b=g_w_b))
    dqb, dkb, dvb, dbias = attn_bwd(proj, ob, probs, dob)
    g_rel = relbias_reduce(bias_table_bwd_layout(dbias))
    du, dw, dqg, dkd, dqk, deg, dz, g_norm = c2_bwd(u, w, qg, kd, qk, eg, proj, small["norm_a"] + tok, sall, doa)
    dq, dk, dv, dgcs, dbeta = c1_bwd(q, k, v, gcs, beta, tinv, du, dw, dqg, dkd, dqk, deg)
    dpre, dbb, daa, g_conv_a, g_alog, g_dtb = prep_bwd(proj, small["conv_a"], a_log, dtb, dq, dk, dv, dgcs, dbeta)
    tok = on_grads("small", dict(conv_a=g_conv_a, rel_bias=g_rel, conv_ffn=g_conv_ffn))
    dba = jnp.concatenate([dbb, daa, jnp.zeros((t, NCAT - C_BA - 2 * LANE), BF16)], axis=1) + tok.astype(BF16)
    dpieces = dict(pre=dpre, z=dz, qb=dqb, kb=dkb, vb=dvb, ga=dga, gb=dgb)
    g_in = {n: matmul(h1, dpieces[n], BF16, "wgrad_in_" + n, ta=True) for n, _, _ in IN_PIECES}
    g_in["ba"] = matmul(h1, dba, BF16, "wgrad_in_ba", ta=True)
    tok = on_grads("in", g_in)
    dh1 = dgrad_pieces([(dpieces[n], off) for n, off, _ in IN_PIECES], dba + tok.astype(BF16), wcat,
                       "dgrad_in")
    grad_x, dscale_t, dshift_t = modulate_bwd(dh1, x, dx_res, scale_t + tok, "modulate_t_bwd")

    dmod = (dshift_t, dscale_t, dgate_t, dshift_f, dscale_f, dgate_f)
    grads = dict(w_in=_orig_cols_from_pieces(g_in, 0, 9232), w_up=g_w_up, w_down=g_w_down, w_a=g_w_a, w_b=g_w_b, w_o=g_w_o,
                 conv_a=g_conv_a, rel_bias=g_rel, conv_ffn=g_conv_ffn,
                 b_gate=jnp.concatenate([dbga, dbgb], axis=1), a_log=g_alog[:, :AH], dt_bias=g_dtb[:, :AH],
                 norm_a=g_norm, ln1_g=dg1, ln1_b=db1, b_conv_ffn=g_bconv, ln2_g=dg2, ln2_b=db2)
    return loss[0, 0], grad_x, dmod, grads


REP_NAMES = ["b_ada", "b_gate", "a_log", "dt_bias", "norm_a", "ln1_g", "ln1_b", "b_conv_ffn", "ln2_g", "ln2_b"]
SH_NAMES = ["conv_a", "rel_bias", "conv_ffn"]


def _col_shards(a, n):
    return a.reshape(a.shape[0], NDEV, n).transpose(1, 0, 2)


def kernel(x, c, w_ada, b_ada, w_in, b_gate, conv_a, a_log, dt_bias, norm_a, rel_bias, w_branch_a, w_branch_b, w_o, ln1_g, ln1_b, w_up, conv_ffn, b_conv_ffn, w_down, ln2_g, ln2_b, loss_target, m_w_ada, m_b_ada, m_w_in, m_b_gate, m_conv_a, m_a_log, m_dt_bias, m_norm_a, m_rel_bias, m_w_branch_a, m_w_branch_b, m_w_o, m_ln1_g, m_ln1_b, m_w_up, m_conv_ffn, m_b_conv_ffn, m_w_down, m_ln2_g, m_ln2_b, v_w_ada, v_b_ada, v_w_in, v_b_gate, v_conv_a, v_a_log, v_dt_bias, v_norm_a, v_rel_bias, v_w_branch_a, v_w_branch_b, v_w_o, v_ln1_g, v_ln1_b, v_w_up, v_conv_ffn, v_b_conv_ffn, v_w_down, v_ln2_g, v_ln2_b):
    W = dict(w_ada=w_ada, b_ada=b_ada, w_in=w_in, b_gate=b_gate, conv_a=conv_a, a_log=a_log, dt_bias=dt_bias,
             norm_a=norm_a, rel_bias=rel_bias, w_branch_a=w_branch_a, w_branch_b=w_branch_b, w_o=w_o, ln1_g=ln1_g,
             ln1_b=ln1_b, w_up=w_up, conv_ffn=conv_ffn, b_conv_ffn=b_conv_ffn, w_down=w_down, ln2_g=ln2_g,
             ln2_b=ln2_b)
    M = dict(w_ada=m_w_ada, b_ada=m_b_ada, w_in=m_w_in, b_gate=m_b_gate, conv_a=m_conv_a, a_log=m_a_log,
             dt_bias=m_dt_bias, norm_a=m_norm_a, rel_bias=m_rel_bias, w_branch_a=m_w_branch_a,
             w_branch_b=m_w_branch_b, w_o=m_w_o, ln1_g=m_ln1_g, ln1_b=m_ln1_b, w_up=m_w_up, conv_ffn=m_conv_ffn,
             b_conv_ffn=m_b_conv_ffn, w_down=m_w_down, ln2_g=m_ln2_g, ln2_b=m_ln2_b)
    V = dict(w_ada=v_w_ada, b_ada=v_b_ada, w_in=v_w_in, b_gate=v_b_gate, conv_a=v_conv_a, a_log=v_a_log,
             dt_bias=v_dt_bias, norm_a=v_norm_a, rel_bias=v_rel_bias, w_branch_a=v_w_branch_a,
             w_branch_b=v_w_branch_b, w_o=v_w_o, ln1_g=v_ln1_g, ln1_b=v_ln1_b, w_up=v_w_up, conv_ffn=v_conv_ffn,
             b_conv_ffn=v_b_conv_ffn, w_down=v_w_down, ln2_g=v_ln2_g, ln2_b=v_ln2_b)
    W3, M3, V3 = W, M, V
    W, M, V = ({n: a[0] for n, a in dct.items()} for dct in (W, M, V))
    me = 4 * lax.axis_index("x") + 2 * lax.axis_index("y") + lax.axis_index("c")
    big = ("w_in", "w_up", "w_down", "w_branch_a", "w_branch_b", "w_o")

    g_in = all_gather_two_level(W["w_in"].astype(BF16), "gather_w_in")
    wts = dict(w_in_slabs=g_in)
    c_all, *sh_all = all_gather([c] + [W[n] for n in SH_NAMES], "gather_small")
    c_all = c_all.reshape(NDEV, D)

    def full_small(g8):
        return g8.transpose(1, 0, 2).reshape(g8.shape[1], -1)

    small = dict(conv_a=full_small(sh_all[0]), rel_bias=full_small(sh_all[1]), conv_ffn=full_small(sh_all[2]),
                 b_gate=W["b_gate"][None], a_log=W["a_log"][None], dt_bias=W["dt_bias"][None],
                 norm_a=W["norm_a"][None], ln1_g=W["ln1_g"][None], ln1_b=W["ln1_b"][None],
                 b_conv_ffn=W["b_conv_ffn"][None], ln2_g=W["ln2_g"][None], ln2_b=W["ln2_b"][None])

    nsh = w_ada.shape[2]
    b_sh = lax.dynamic_slice(W["b_ada"][None], (0, me * nsh), (1, nsh))
    mod_sh = ada_fwd(c_all, W["w_ada"], b_sh)
    (mod_rows,) = all_to_all([mod_sh[:, None, :]], "scatter_mod")
    mod6 = mod_rows.reshape(6, D)

    after_small = (g_in[0, 0, 0].astype(F32) * 0.0 + mod6[0, 0] * 0.0).astype(BF16)
    late, late_tok = exchange_start([W[n].astype(BF16) + after_small for n in big[1:]], "gather_late_start", False)

    def late_weights(after):
        g_up, g_down, g_a, g_b, g_o = exchange_wait(late, after, "gather_late_wait")
        return dict(w_up=g_up.transpose(1, 0, 2).reshape(D, -1), w_down=g_down.reshape(DFF, D),
                    w_a=g_a.reshape(D, D), w_b=g_b.reshape(D, D), w_o=g_o.reshape(D, D))

    mod6 = mod6 + late_tok
    mod = tuple(mod6[i:i + 1] for i in range(6))

    pending = {}

    def on_grads(group, gd):
        if group == "small":
            pending["small"] = all_to_all([_col_shards(gd[n], W[n].shape[1]) for n in SH_NAMES],
                                          "scatter_small_grads")
            return pending["small"][0][0, 0, 0] * 0.0
        if group == "ffn":
            slabs = [_col_shards(gd["w_up"], w_up.shape[2]), gd["w_down"].reshape(NDEV, -1, D)]
        elif group == "mix":
            slabs = [gd[n].reshape(NDEV, -1, D) for n in ("w_a", "w_b", "w_o")]
        else:
            nin = w_in.shape[2]
            slabs = [jnp.stack([_orig_cols_from_pieces(gd, j * nin, (j + 1) * nin) for j in range(NDEV)], axis=0)]
        pending[group], tok = exchange_start([s.astype(BF16) for s in slabs], "scatter_" + group + "_start", True)
        return tok

    loss, grad_x, dmod, g = local_step(x[0], loss_target[0], mod, wts, small, late_weights, on_grads)

    rep_grads = {n: g[n] for n in REP_NAMES if n != "b_ada"}
    rep_grads["b_ada"] = jnp.concatenate(dmod, axis=1)
    gathered = all_gather([rep_grads[n] for n in REP_NAMES] + [jnp.broadcast_to(loss, (1, LANE))],
                          "gather_small_grads")
    rep_all = dict(zip(REP_NAMES, gathered))
    sh_recv = [p[:, None] for p in pending["small"]]
    small_names = REP_NAMES + SH_NAMES
    sg, sd, sm, sv, loss_row = adamw_small([rep_all[n] for n in REP_NAMES] + sh_recv,
                                           [W3[n] for n in small_names], [M3[n] for n in small_names],
                                           [V3[n] for n in small_names], gathered[-1], "adamw_small")
    loss_total = loss_row[0, 0]

    dmod_all = rep_all["b_ada"][:, 0]
    dmod_sh = lax.dynamic_slice(dmod_all, (0, me * nsh), (NDEV, nsh))
    g_w_ada = ada_wgrad(c_all.T, dmod_sh)

    p_up, p_down = exchange_wait(pending["ffn"], grad_x, "scatter_ffn_wait")
    p_a, p_b, p_o = exchange_wait(pending["mix"], grad_x, "scatter_mix_wait")
    (p_in,) = exchange_wait(pending["in"], grad_x, "scatter_in_wait")
    parts = [p_in, p_up, p_down, p_a, p_b, p_o]

    res = {}
    for n, p in zip(big, parts):
        res[n] = adamw(p, W3[n], M3[n], V3[n], "adamw_" + n)
    res["w_ada"] = adamw(g_w_ada[None], W3["w_ada"], M3["w_ada"], V3["w_ada"], "adamw_w_ada")
    for i, n in enumerate(small_names):
        res[n] = (sg[i], sd[i], sm[i], sv[i])

    order = ("w_ada", "b_ada", "w_in", "b_gate", "conv_a", "a_log", "dt_bias", "norm_a", "rel_bias", "w_branch_a",
             "w_branch_b", "w_o", "ln1_g", "ln1_b", "w_up", "conv_ffn", "b_conv_ffn", "w_down", "ln2_g", "ln2_b")
    outs = [loss_total, grad_x[None]]
    for kind in range(4):
        outs += [res[n][kind] for n in order]
    return tuple(outs)
```

```python
import functools
import math

import numpy as np
import jax
import jax.numpy as jnp
from jax import lax
from jax.experimental import pallas as pl
from jax.experimental.pallas import tpu as pltpu

F32 = jnp.float32
BF16 = jnp.bfloat16
HI = lax.Precision.HIGHEST

D = 1024
CH = 64
AH, ADK = 8, 128
BH, BDH = 16, 64
BPREV = 8
BMAXREL = 256
RELSZ = CH + BMAXREL
DFF = 2816
ALPHA = 2.0 ** 0.25
LN_EPS, RMS_EPS, L2_EPS = 1e-5, 1e-6, 1e-6
NEG = -1e30
LR, B1, B2, AEPS, WD, STEP = 1e-3, 0.9, 0.999, 1e-8, 0.01, 10
NDEV = 8
HALO = 8
LANE = 128
TQ = 512
VMEM_LIMIT = 56 * 1024 * 1024

C_QKVA, C_Z, C_QKVB, C_GATE, C_BA, NCAT = 0, 3072, 4096, 7168, 9216, 9728


def _cparams(n_axes=1, vmem=VMEM_LIMIT):
    return pltpu.CompilerParams(dimension_semantics=("arbitrary",) * n_axes, vmem_limit_bytes=vmem)


def _dg(a, b, ca, cb):
    return lax.dot_general(a.astype(BF16), b.astype(BF16), (((ca,), (cb,)), ((), ())),
                           preferred_element_type=F32)


@jax.custom_vjp
def mm_nn(a, b):
    return _dg(a, b, 1, 0)


@jax.custom_vjp
def mm_nt(a, b):
    return _dg(a, b, 1, 1)


@jax.custom_vjp
def mm_tn(a, b):
    return _dg(a, b, 0, 0)


mm_nn.defvjp(lambda a, b: (mm_nn(a, b), (a, b)),
             lambda r, g: (mm_nt(g, r[1]).astype(r[0].dtype), mm_tn(r[0], g).astype(r[1].dtype)))
mm_nt.defvjp(lambda a, b: (mm_nt(a, b), (a, b)),
             lambda r, g: (mm_nn(g, r[1]).astype(r[0].dtype), mm_tn(g, r[0]).astype(r[1].dtype)))
mm_tn.defvjp(lambda a, b: (mm_tn(a, b), (a, b)),
             lambda r, g: (mm_nt(r[1], g).astype(r[0].dtype), mm_nn(r[0], g).astype(r[1].dtype)))


@jax.custom_vjp
def mm_w(a, w):
    return _dg(a, w, 1, 0)


mm_w.defvjp(lambda a, w: (mm_w(a, w), (a, w)),
            lambda r, g: (mm_nt(g, r[1]).astype(r[0].dtype), jnp.zeros_like(r[1])))


def _mmh(a, b):
    return lax.dot_general(a, b, (((1,), (0,)), ((), ())), precision=HI, preferred_element_type=F32)


def _bdg(a, b, ca, cb):
    return lax.dot_general(a.astype(BF16), b.astype(BF16), (((ca,), (cb,)), ((0,), (0,))),
                           preferred_element_type=F32)


@jax.custom_vjp
def bmm_nn(a, b):
    return _bdg(a, b, 2, 1)


@jax.custom_vjp
def bmm_nt(a, b):
    return _bdg(a, b, 2, 2)


@jax.custom_vjp
def bmm_tn(a, b):
    return _bdg(a, b, 1, 1)


bmm_nn.defvjp(lambda a, b: (bmm_nn(a, b), (a, b)), lambda r, g: (bmm_nt(g, r[1]), bmm_tn(r[0], g)))
bmm_nt.defvjp(lambda a, b: (bmm_nt(a, b), (a, b)), lambda r, g: (bmm_nn(g, r[1]), bmm_tn(g, r[0])))
bmm_tn.defvjp(lambda a, b: (bmm_tn(a, b), (a, b)), lambda r, g: (bmm_nt(r[1], g), bmm_nn(r[0], g)))


def _bdg3(a, b, ca, cb):
    return lax.dot_general(a, b, (((ca,), (cb,)), ((0,), (0,))), precision=lax.Precision.HIGH,
                           preferred_element_type=F32)


NEWTON_STEPS = 2


def _bdgp(a, b, ca, cb):
    return _bdg(a, b, ca, cb)


@jax.custom_vjp
def bmm3_nn(a, b):
    return _bdgp(a, b, 2, 1)


bmm3_nn.defvjp(lambda a, b: (bmm3_nn(a, b), (a, b)),
               lambda r, g: (_bdgp(g, r[1], 2, 2), _bdgp(r[0], g, 1, 1)))


def _sigmoid(x):
    return 0.5 * jnp.tanh(0.5 * x) + 0.5


def _silu(x):
    return x * _sigmoid(x)


def _softplus(x):
    return jnp.maximum(x, 0.0) + jnp.log(1.0 + jnp.exp(-jnp.abs(x)))


def _layernorm(r, g, b):
    mu = jnp.mean(r, axis=-1, keepdims=True)
    xc = r - mu
    var = jnp.mean(xc * xc, axis=-1, keepdims=True)
    return xc * lax.rsqrt(var + LN_EPS) * g + b


def _iota2(shape, dim):
    return lax.broadcasted_iota(jnp.int32, shape, dim)


@jax.custom_vjp
def causal_conv(ext, rows):
    k = len(rows)
    y = None
    for j in range(k):
        s = k - 1 - j
        r = pltpu.roll(ext, s, 0) if s else ext
        t = r[HALO:] * rows[j]
        y = t if y is None else y + t
    return y


def _causal_conv_fwd(ext, rows):
    return causal_conv(ext, rows), (ext, rows)


def _causal_conv_bwd(res, g):
    ext, rows = res
    n = ext.shape[0]
    k = len(rows)
    gext = jnp.concatenate([jnp.zeros((HALO, g.shape[1]), g.dtype), g], axis=0)
    dext = None
    drows = []
    for j in range(k):
        s = k - 1 - j
        up = pltpu.roll(gext, n - s, 0) if s else gext
        t = up * rows[j]
        dext = t if dext is None else dext + t
        r = pltpu.roll(ext, s, 0) if s else ext
        drows.append(jnp.sum(g * r[HALO:], axis=0, keepdims=True))
    return dext, tuple(drows)


causal_conv.defvjp(_causal_conv_fwd, _causal_conv_bwd)


def _chunk_masks(tm):
    i = _iota2((tm, tm), 0)
    j = _iota2((tm, tm), 1)
    same = (i ^ j) < CH
    lower = jnp.where(same & (j <= i), 1.0, 0.0).astype(F32)
    upper = jnp.where(same & (i <= j), 1.0, 0.0).astype(F32)
    return lower, upper


@jax.custom_vjp
def chunk_cumsum(g):
    lower, _ = _chunk_masks(g.shape[0])
    return _mmh(lower, g)


def _chunk_cumsum_bwd(_, ct):
    _, upper = _chunk_masks(ct.shape[0])
    return (_mmh(upper, ct),)


chunk_cumsum.defvjp(lambda g: (chunk_cumsum(g), None), _chunk_cumsum_bwd)


@jax.custom_vjp
def inv_unit_lower(a):
    n = a.shape[-1]
    eye = jnp.where(_iota2((1, n, n), 1) == _iota2((1, n, n), 2), 1.0, 0.0).astype(F32)
    x = eye - a
    p = _bdg3(a, a, 2, 1)
    steps = int(math.log2(n)) - 1
    for s in range(steps):
        x = x + _bdg3(x, p, 2, 1)
        if s + 1 < steps:
            p = _bdg3(p, p, 2, 1)
    for _ in range(NEWTON_STEPS):
        r = (eye - x) - _bdg3(a, x, 2, 1)
        x = x + _bdg3(x, r, 2, 1)
    return x


def _inv_fwd(a):
    t = inv_unit_lower(a)
    return t, t


def _inv_bwd(t, g):
    return (-_bdgp(_bdgp(t, g, 1, 1), t, 2, 2),)


inv_unit_lower.defvjp(_inv_fwd, _inv_bwd)


@jax.custom_vjp
def inv_known(a, t):
    return t


inv_known.defvjp(lambda a, t: (t, t), lambda t, g: (_inv_bwd(t, g)[0], jnp.zeros_like(t)))


def prep_head_fn(ext, rows, scale):
    s = _silu(causal_conv(ext, rows))
    if scale is None:
        return s
    return s * (lax.rsqrt(jnp.sum(s * s, axis=-1, keepdims=True) + L2_EPS) * scale)


def prep_gate_fn(bb, aa, a_log, dtb):
    g = -jnp.exp(a_log) * _softplus(aa + dtb)
    return chunk_cumsum(g), _sigmoid(bb)


PREP_SCALES = (ADK ** -0.5, 1.0, None)


def _head_cols(a, heads):
    lane = _iota2((1, LANE), 1)
    return jnp.concatenate([jnp.sum(jnp.where(lane == h, a, 0.0), axis=1, keepdims=True)[None]
                            for h in heads], axis=0)


def _head_rows(a, heads):
    at = a.T[:AH]
    sub = _iota2((AH, 1), 0)
    return jnp.concatenate([jnp.sum(jnp.where(sub == h, at, 0.0), axis=0, keepdims=True)[None]
                            for h in heads], axis=0)


def c1_heads(q, k, v, gcs, beta, tinv_saved=None):
    heads = range(AH)
    gcol = jnp.concatenate([_head_cols(g, heads) for g in gcs], axis=0)
    grow = jnp.concatenate([_head_rows(g, heads) for g in gcs], axis=0)
    bcol = jnp.concatenate([_head_cols(b, heads) for b in beta], axis=0)
    i = _iota2((1, CH, CH), 1)
    j = _iota2((1, CH, CH), 2)
    causal = j <= i
    strict = j < i
    diff = gcol - grow
    decay = jnp.where(causal, jnp.exp(jnp.where(causal, diff, 0.0)), 0.0)
    kb = k * bcol
    vb = v * bcol
    a_low = jnp.where(strict, bmm_nt(kb, k) * decay, 0.0)
    tinv = inv_unit_lower(a_low) if tinv_saved is None else inv_known(a_low, tinv_saved)
    egc = jnp.exp(gcol)
    u = bmm3_nn(tinv, vb)
    w = bmm3_nn(tinv, kb * egc)
    qk = jnp.where(causal, bmm_nt(q, k) * decay, 0.0)
    glast = jnp.sum(jnp.where(_iota2((1, CH, 1), 1) == CH - 1, gcol, 0.0), axis=1, keepdims=True)
    qg = q * egc
    kd = k * jnp.exp(glast - gcol)
    eg = jnp.exp(glast) * jnp.ones((1, 1, ADK), F32)
    return u, w, qk, qg, kd, eg, tinv


def c2_heads(s, u, w, qk, qg, kd, eg, z, nw):
    vn = u - bmm_nn(w, s)
    o = bmm_nn(qg, s) + bmm_nn(qk, vn)
    s2 = s * eg + bmm_tn(kd, vn)
    ms = jnp.mean(o * o, axis=-1, keepdims=True)
    og = o * lax.rsqrt(ms + RMS_EPS) * nw * _silu(z)
    return og, s2


ATT_SCALE = BDH ** -0.5


def _head_mask(hh):
    lane = _iota2((1, 2 * BDH), 1)
    return jnp.where((lane >= hh * BDH) & (lane < (hh + 1) * BDH), 1.0, 0.0).astype(F32)


def attn_sub_fwd(q, k, v, bias2, r, firstf):
    col = _iota2((1, KWIN), 1) + r * SUBQ
    nokey = jnp.where(col < TQ, firstf, 0.0) * NEG
    out, probs = None, []
    for hh in range(2):
        hm = _head_mask(hh)
        s = mm_nt(q * (hm * ATT_SCALE), k) + (assemble_bias(bias2[hh], r) + nokey)
        p = jnp.exp(s - jnp.max(s, axis=-1, keepdims=True))
        inv = 1.0 / jnp.sum(p, axis=-1, keepdims=True)
        o = mm_nn(p, v) * (inv * hm)
        out = o if out is None else out + o
        probs.append(p * inv)
    return out, probs


def attn_sub_bwd(q, k, v, o, do, probs, r):
    dq, dk, dv, dss = None, None, None, []
    for hh in range(2):
        hm = _head_mask(hh)
        p = probs[hh]
        doh = do * hm
        ds = p * (mm_nt(doh, v) - jnp.sum(doh * o, axis=-1, keepdims=True))
        dqh = mm_nn(ds, k) * (hm * ATT_SCALE)
        dkh = mm_tn(ds, q * (hm * ATT_SCALE))
        dvh = mm_tn(p, doh)
        dq = dqh if dq is None else dq + dqh
        dk = dkh if dk is None else dk + dkh
        dv = dvh if dv is None else dv + dvh
        dss.append(ds)
    return dq, dk, dv, dss


def merge_fn(x, oa, ob, gra, grb, p_pa, p_pb, p_mix, bga, bgb, gate_t, g1, b1, scale_f, shift_f,
             wa, wb, wo):
    ga = _sigmoid(gra + bga)
    gb = _sigmoid(grb + bgb)
    pa = mm_w(oa, wa) + p_pa
    pb = mm_w(ob, wb) + p_pb
    merged = ga * pa + gb * pb
    mix = mm_w(merged, wo) + p_mix
    y1 = _layernorm(ALPHA * x + gate_t * mix, g1, b1)
    return y1, merged


def ffn_act_fn(extg, extv, rows_g, rows_v, bg, bv):
    return _silu(causal_conv(extg, rows_g) + bg) * (causal_conv(extv, rows_v) + bv)


def head_fn(a, y1, p_ffn, gate_f, g2, b2, tgt, wd):
    ffn = mm_w(a, wd) + p_ffn
    y2 = _layernorm(ALPHA * y1 + gate_f * ffn, g2, b2)
    err = y2 - tgt
    return 0.5 * jnp.sum(jnp.mean(err * err, axis=-1, keepdims=True))


def _rows(tm, width, colblk=0, order=None):
    if order is None:
        return pl.BlockSpec((tm, width), lambda i: (i, colblk))
    return pl.BlockSpec((tm, width), lambda i: (order(i), colblk))


def _const(shape):
    nd = len(shape)
    return pl.BlockSpec(shape, lambda *_: (0,) * nd)


def _pick(n, cands):
    for c in cands:
        if n % c == 0:
            return c
    raise ValueError(f"no tile for {n}")


def _tile(n, cap):
    best = None
    for c in range(LANE, min(n, cap) + 1, LANE):
        if n % c == 0:
            best = c
    if best is None:
        raise ValueError(f"no tile for {n}")
    return best


def _onehot_rows(k, j):
    return jnp.where(_iota2((k, 1), 0) == j, 1.0, 0.0).astype(F32)


def _stack_rows(drows):
    k = len(drows)
    out = None
    for j in range(k):
        tj = _onehot_rows(k, j) * drows[j]
        out = tj if out is None else out + tj
    return out


def matmul(a, w, out_dtype, name, ta=False, tb=False):
    kdim, m = a.shape if ta else a.shape[::-1]
    n = w.shape[0] if tb else w.shape[1]
    tm = _tile(m, 2048 if kdim <= 1024 else 1024)
    tn = _tile(n, 1024)
    tk = _tile(kdim, 2560)
    nk = kdim // tk
    a_spec = (pl.BlockSpec((tk, tm), lambda i, j, k: (k, i)) if ta
              else pl.BlockSpec((tm, tk), lambda i, j, k: (i, k)))
    w_spec = (pl.BlockSpec((tn, tk), lambda i, j, k: (j, k)) if tb
              else pl.BlockSpec((tk, tn), lambda i, j, k: (k, j)))

    def body(a_ref, w_ref, o_ref, *scratch):
        p = _dg(a_ref[...], w_ref[...], 0 if ta else 1, 1 if tb else 0)
        if nk == 1:
            o_ref[...] = p.astype(out_dtype)
            return
        acc = scratch[0]
        k = pl.program_id(2)

        @pl.when(k == 0)
        def _():
            acc[...] = p

        @pl.when(k > 0)
        def _():
            acc[...] += p

        @pl.when(k == nk - 1)
        def _():
            o_ref[...] = acc[...].astype(out_dtype)

    return pl.pallas_call(
        body, name=name,
        grid=(m // tm, n // tn, nk),
        in_specs=[a_spec, w_spec],
        out_specs=pl.BlockSpec((tm, tn), lambda i, j, k: (i, j)),
        out_shape=jax.ShapeDtypeStruct((m, n), out_dtype),
        scratch_shapes=[] if nk == 1 else [pltpu.VMEM((tm, tn), F32)],
        compiler_params=_cparams(3),
    )(a, w)


def dgrad_modulated(a, w, xin, dres, scale, name):
    m, kdim = a.shape
    n = w.shape[0]
    tm = _tile(m, 1024)
    tk = _tile(kdim, 2560)
    nk = kdim // tk
    assert nk > 1

    def body(a_ref, w_ref, x_ref, r_ref, sc_ref, o_ref, dsc_ref, dsh_ref, acc):
        i = pl.program_id(0)
        k = pl.program_id(1)
        p = _dg(a_ref[...], w_ref[...], 1, 1)

        @pl.when(k == 0)
        def _():
            acc[...] = p

        @pl.when(k > 0)
        def _():
            acc[...] += p

        @pl.when((i == 0) & (k == 0))
        def _():
            dsc_ref[...] = jnp.zeros_like(dsc_ref)
            dsh_ref[...] = jnp.zeros_like(dsh_ref)

        @pl.when(k == nk - 1)
        def _():
            dh = acc[...]
            o_ref[...] = r_ref[...] + dh * (1.0 + sc_ref[...])
            dsc_ref[...] += jnp.sum(dh * x_ref[...], axis=0, keepdims=True)
            dsh_ref[...] += jnp.sum(dh, axis=0, keepdims=True)

    row = pl.BlockSpec((tm, n), lambda i, k: (i, 0))
    vec = pl.BlockSpec((1, n), lambda i, k: (0, 0))
    return pl.pallas_call(
        body, name=name, grid=(m // tm, nk),
        in_specs=[pl.BlockSpec((tm, tk), lambda i, k: (i, k)), pl.BlockSpec((n, tk), lambda i, k: (0, k)),
                  row, row, vec],
        out_specs=[row, vec, vec],
        out_shape=[jax.ShapeDtypeStruct((m, n), F32), jax.ShapeDtypeStruct((1, n), F32),
                   jax.ShapeDtypeStruct((1, n), F32)],
        scratch_shapes=[pltpu.VMEM((tm, n), F32)],
        compiler_params=_cparams(2),
    )(a, w, xin, dres, scale)


def dgrad_pieces(pieces, tail, w, name):
    m = pieces[0][0].shape[0]
    n, ktot = w.shape
    tk = 1024
    tm = _tile(m, 1024)
    wt = tail.shape[1]
    ranges, k0 = [], 0
    for arr, off in pieces:
        assert off == k0 * tk and arr.shape[1] % tk == 0
        ranges.append((k0, k0 + arr.shape[1] // tk))
        k0 = ranges[-1][1]
    nk = k0
    npc = len(pieces)

    def body(*refs):
        a_refs, t_ref, w_ref, wt_ref, o_ref, acc = refs[:npc], refs[npc], refs[npc + 1], refs[npc + 2], refs[npc + 3], refs[npc + 4]
        k = pl.program_id(1)

        @pl.when(k == 0)
        def _():
            acc[...] = _dg(t_ref[...], wt_ref[...], 1, 1)

        for a_ref, (lo, hi) in zip(a_refs, ranges):
            @pl.when((k >= lo) & (k < hi))
            def _(a_ref=a_ref):
                acc[...] += _dg(a_ref[...], w_ref[...], 1, 1)

        @pl.when(k == nk - 1)
        def _():
            o_ref[...] = acc[...]

    def piece_spec(lo, hi):
        return pl.BlockSpec((tm, tk), lambda i, k: (i, jnp.clip(k - lo, 0, hi - lo - 1)))

    return pl.pallas_call(
        body, name=name, grid=(m // tm, nk),
        in_specs=[piece_spec(lo, hi) for lo, hi in ranges] + [
            pl.BlockSpec((tm, wt), lambda i, k: (i, 0)),
            pl.BlockSpec((n, tk), lambda i, k: (0, k)),
            pl.BlockSpec((n, wt), lambda i, k: (0, (ktot - wt) // wt))],
        out_specs=pl.BlockSpec((tm, n), lambda i, k: (i, 0)),
        out_shape=jax.ShapeDtypeStruct((m, n), F32),
        scratch_shapes=[pltpu.VMEM((tm, n), F32)],
        compiler_params=_cparams(2),
    )(*[a for a, _ in pieces], tail, w, w)


def modulate(x, scale, shift, name):
    t, d = x.shape
    tm = _pick(t, (512, 256, 128))

    def body(x_ref, sc_ref, sh_ref, o_ref):
        o_ref[...] = (x_ref[...] * (1.0 + sc_ref[...]) + sh_ref[...]).astype(BF16)

    return pl.pallas_call(
        body, name=name, grid=(t // tm,),
        in_specs=[_rows(tm, d), _const((1, d)), _const((1, d))],
        out_specs=_rows(tm, d),
        out_shape=jax.ShapeDtypeStruct((t, d), BF16),
        compiler_params=_cparams(),
    )(x, scale, shift)


def modulate_bwd(dh, xin, dres, scale, name):
    t, d = dh.shape
    tm = _pick(t, (512, 256, 128))

    def body(dh_ref, x_ref, dr_ref, sc_ref, o_ref, dsc_ref, dsh_ref):
        i = pl.program_id(0)
        dh_v = dh_ref[...]
        o_ref[...] = dr_ref[...] + dh_v * (1.0 + sc_ref[...])

        @pl.when(i == 0)
        def _():
            dsc_ref[...] = jnp.zeros_like(dsc_ref)
            dsh_ref[...] = jnp.zeros_like(dsh_ref)

        dsc_ref[...] += jnp.sum(dh_v * x_ref[...], axis=0, keepdims=True)
        dsh_ref[...] += jnp.sum(dh_v, axis=0, keepdims=True)

    return pl.pallas_call(
        body, name=name, grid=(t // tm,),
        in_specs=[_rows(tm, d), _rows(tm, d), _rows(tm, d), _const((1, d))],
        out_specs=[_rows(tm, d), _const((1, d)), _const((1, d))],
        out_shape=[jax.ShapeDtypeStruct((t, d), F32), jax.ShapeDtypeStruct((1, d), F32),
                   jax.ShapeDtypeStruct((1, d), F32)],
        compiler_params=_cparams(),
    )(dh, xin, dres, scale)


PREP_TM = 128


def _halo_specs(tm, width, colblk, order):
    per = tm // HALO
    return [pl.BlockSpec((HALO, width), lambda i: (jnp.maximum(order(i) * per - 1, 0), colblk)),
            pl.BlockSpec((tm, width), lambda i: (order(i), colblk))]


def prep_fwd(proj, conv_a, a_log, dtb):
    t = proj.shape[0]
    tm = PREP_TM
    nt = t // tm
    wq = 3 * D

    def body(prev_ref, cur_ref, bb_ref, aa_ref, cw_ref, al_ref, dt_ref, q_ref, k_ref, v_ref, g_ref, b_ref):
        i = pl.program_id(0)
        flag = jnp.where(i > 0, 1.0, 0.0)
        for part, o_ref in enumerate((q_ref, k_ref, v_ref)):
            for h in range(AH):
                sl = slice(part * D + h * ADK, part * D + (h + 1) * ADK)
                ext = jnp.concatenate([prev_ref[:, sl] * flag, cur_ref[:, sl]], axis=0)
                rows = tuple(cw_ref[j:j + 1, sl] for j in range(4))
                o_ref[h] = prep_head_fn(ext, rows, PREP_SCALES[part])
        gcs, beta = prep_gate_fn(bb_ref[...], aa_ref[...], al_ref[...], dt_ref[...])
        g_ref[...] = gcs
        b_ref[...] = beta

    ident = lambda i: i
    hm = pl.BlockSpec((AH, tm, ADK), lambda i: (0, i, 0))
    return pl.pallas_call(
        body, name="prep_fwd", grid=(nt,),
        in_specs=_halo_specs(tm, wq, 0, ident) + [
            _rows(tm, 128, C_BA // 128), _rows(tm, 128, C_BA // 128 + 1),
            _const((4, wq)), _const((1, 128)), _const((1, 128))],
        out_specs=[hm, hm, hm, _rows(tm, 128), _rows(tm, 128)],
        out_shape=[jax.ShapeDtypeStruct((AH, t, ADK), F32)] * 3 + [jax.ShapeDtypeStruct((t, 128), F32)] * 2,
        compiler_params=_cparams(),
    )(proj, proj, proj, proj, conv_a, a_log, dtb)


def prep_bwd(proj, conv_a, a_log, dtb, dq, dk, dv, dgcs, dbeta):
    t = proj.shape[0]
    tm = PREP_TM
    nt = t // tm
    wq = 3 * D
    rev = lambda i: nt - 1 - i

    def body(prev_ref, cur_ref, bb_ref, aa_ref, cw_ref, al_ref, dt_ref,
             dq_ref, dk_ref, dv_ref, dg_ref, db_ref,
             dpre_ref, dbb_ref, daa_ref, dcw_ref, dal_ref, ddt_ref, carry):
        i = pl.program_id(0)
        flag = jnp.where(i < nt - 1, 1.0, 0.0)

        @pl.when(i == 0)
        def _():
            carry[...] = jnp.zeros_like(carry)
            dcw_ref[...] = jnp.zeros_like(dcw_ref)
            dal_ref[...] = jnp.zeros_like(dal_ref)
            ddt_ref[...] = jnp.zeros_like(ddt_ref)

        for part, d_ref in enumerate((dq_ref, dk_ref, dv_ref)):
            for h in range(AH):
                sl = slice(part * D + h * ADK, part * D + (h + 1) * ADK)
                ext = jnp.concatenate([prev_ref[:, sl] * flag, cur_ref[:, sl]], axis=0)
                rows = tuple(cw_ref[j:j + 1, sl] for j in range(4))
                _, vjp = jax.vjp(lambda e, r: prep_head_fn(e, r, PREP_SCALES[part]), ext, rows)
                dext, drows = vjp(d_ref[h])
                dcur = dext[HALO:]
                dpre_ref[:, sl] = jnp.concatenate([dcur[:tm - HALO], dcur[tm - HALO:] + carry[:, sl]],
                                                  axis=0).astype(BF16)
                carry[:, sl] = dext[:HALO]
                dcw_ref[:, sl] += _stack_rows(drows)
        _, vjp = jax.vjp(prep_gate_fn, bb_ref[...], aa_ref[...], al_ref[...], dt_ref[...])
        dbb, daa, dal, ddt = vjp((dg_ref[...], db_ref[...]))
        dbb_ref[...] = dbb.astype(BF16)
        daa_ref[...] = daa.astype(BF16)
        dal_ref[...] += dal
        ddt_ref[...] += ddt

    hm = pl.BlockSpec((AH, tm, ADK), lambda i: (0, rev(i), 0))
    return pl.pallas_call(
        body, name="prep_bwd", grid=(nt,),
        in_specs=_halo_specs(tm, wq, 0, rev) + [
            _rows(tm, 128, C_BA // 128, rev), _rows(tm, 128, C_BA // 128 + 1, rev),
            _const((4, wq)), _const((1, 128)), _const((1, 128)),
            hm, hm, hm, _rows(tm, 128, 0, rev), _rows(tm, 128, 0, rev)],
        out_specs=[_rows(tm, wq, 0, rev), _rows(tm, 128, 0, rev), _rows(tm, 128, 0, rev),
                   _const((4, wq)), _const((1, 128)), _const((1, 128))],
        out_shape=[jax.ShapeDtypeStruct((t, wq), BF16), jax.ShapeDtypeStruct((t, 128), BF16),
                   jax.ShapeDtypeStruct((t, 128), BF16), jax.ShapeDtypeStruct((4, wq), F32),
                   jax.ShapeDtypeStruct((1, 128), F32), jax.ShapeDtypeStruct((1, 128), F32)],
        scratch_shapes=[pltpu.VMEM((HALO, wq), F32)],
        compiler_params=_cparams(),
    )(proj, proj, proj, proj, conv_a, a_log, dtb, dq, dk, dv, dgcs, dbeta)


def _c1_specs(order, cps=1):
    hm = pl.BlockSpec((AH, cps * CH, ADK), lambda n: (0, order(n), 0))
    col = pl.BlockSpec((cps * CH, LANE), lambda n: (order(n), 0))
    qk = pl.BlockSpec((cps, AH, CH, CH), lambda n: (order(n), 0, 0, 0))
    eg = pl.BlockSpec((cps, AH, 1, ADK), lambda n: (order(n), 0, 0, 0))
    return hm, col, qk, eg


C1_CPS = 4


def _heads(ref, rows=slice(None)):
    return jnp.stack([ref[rows, h * ADK:(h + 1) * ADK] for h in range(AH)], axis=0)


C2_CPS = 8


def _chunks(ref):
    return jnp.concatenate([ref[:, c * CH:(c + 1) * CH] for c in range(C1_CPS)], axis=0)


def _chunk_rows(ref):
    return [ref[c * CH:(c + 1) * CH] for c in range(C1_CPS)]


def c1_fwd(q, k, v, gcs, beta):
    t = q.shape[1]
    nc = t // CH
    hm, col, qks, egs = _c1_specs(lambda n: n, C1_CPS)

    def body(q_ref, k_ref, v_ref, g_ref, b_ref, u_ref, w_ref, qg_ref, kd_ref, qk_ref, eg_ref, ti_ref):
        u, w, qk, qg, kd, eg, tinv = c1_heads(_chunks(q_ref), _chunks(k_ref), _chunks(v_ref), _chunk_rows(g_ref),
                                              _chunk_rows(b_ref))
        for c in range(C1_CPS):
            rows, hs = slice(c * CH, (c + 1) * CH), slice(c * AH, (c + 1) * AH)
            u_ref[:, rows] = u[hs]
            w_ref[:, rows] = w[hs].astype(BF16)
            qg_ref[:, rows] = qg[hs].astype(BF16)
            kd_ref[:, rows] = kd[hs].astype(BF16)
            qk_ref[c] = qk[hs].astype(BF16)
            eg_ref[c] = eg[hs]
            ti_ref[c] = tinv[hs]

    return pl.pallas_call(
        body, name="c1_fwd", grid=(nc // C1_CPS,),
        in_specs=[hm, hm, hm, col, col],
        out_specs=[hm, hm, hm, hm, qks, egs, qks],
        out_shape=[jax.ShapeDtypeStruct((AH, t, ADK), F32)] + [jax.ShapeDtypeStruct((AH, t, ADK), BF16)] * 3 + [
            jax.ShapeDtypeStruct((nc, AH, CH, CH), BF16), jax.ShapeDtypeStruct((nc, AH, 1, ADK), F32),
            jax.ShapeDtypeStruct((nc, AH, CH, CH), F32)],
        compiler_params=_cparams(),
    )(q, k, v, gcs, beta)


def c1_bwd(q, k, v, gcs, beta, tinv, du, dw, dqg, dkd, dqk, deg):
    t = q.shape[1]
    nc = t // CH
    hm, col, qks, egs = _c1_specs(lambda n: n, C1_CPS)

    def lead(ref):
        return jnp.concatenate([ref[c] for c in range(C1_CPS)], axis=0)

    def body(q_ref, k_ref, v_ref, g_ref, b_ref, ti_ref, du_ref, dw_ref, dqg_ref, dkd_ref, dqk_ref, deg_ref,
             dq_ref, dk_ref, dv_ref, dg_ref, db_ref):
        tinv = lead(ti_ref)
        _, vjp = jax.vjp(lambda q_, k_, v_, g_, b_: c1_heads(q_, k_, v_, g_, b_, tinv),
                         _chunks(q_ref), _chunks(k_ref), _chunks(v_ref), _chunk_rows(g_ref), _chunk_rows(b_ref))
        dq, dk, dv, dg, db = vjp((_chunks(du_ref).astype(F32), _chunks(dw_ref).astype(F32), lead(dqk_ref),
                                  _chunks(dqg_ref), _chunks(dkd_ref), lead(deg_ref),
                                  jnp.zeros((C1_CPS * AH, CH, CH), F32)))
        for c in range(C1_CPS):
            rows, hs = slice(c * CH, (c + 1) * CH), slice(c * AH, (c + 1) * AH)
            dq_ref[:, rows] = dq[hs]
            dk_ref[:, rows] = dk[hs]
            dv_ref[:, rows] = dv[hs]
            dg_ref[rows] = dg[c]
            db_ref[rows] = db[c]

    return pl.pallas_call(
        body, name="c1_bwd", grid=(nc // C1_CPS,),
        in_specs=[hm, hm, hm, col, col, qks, hm, hm, hm, hm, qks, egs],
        out_specs=[hm, hm, hm, col, col],
        out_shape=[jax.ShapeDtypeStruct((AH, t, ADK), F32)] * 3 + [jax.ShapeDtypeStruct((t, LANE), F32)] * 2,
        compiler_params=_cparams(),
    )(q, k, v, gcs, beta, tinv, du, dw, dqg, dkd, dqk, deg)


def c2_fwd(u, w, qg, kd, qk, eg, proj, norm_a):
    t = u.shape[1]
    nc = t // CH
    cps = C2_CPS
    hm, _, qks, egs = _c1_specs(lambda n: n, cps)
    tok = pl.BlockSpec((cps * CH, D), lambda n: (n, 0))
    zspec = pl.BlockSpec((cps * CH, D), lambda n: (n, C_Z // D))
    sspec = pl.BlockSpec((cps, AH, ADK, ADK), lambda n: (n, 0, 0, 0))

    def body(u_ref, w_ref, qg_ref, kd_ref, qk_ref, eg_ref, z_ref, nw_ref, o_ref, sall_ref, st):
        n = pl.program_id(0)

        @pl.when(n == 0)
        def _():
            st[...] = jnp.zeros_like(st)

        for c in range(cps):
            rows = slice(c * CH, (c + 1) * CH)
            s = st[...]
            sall_ref[c] = s
            og, s2 = c2_heads(s, u_ref[:, rows], w_ref[:, rows], qk_ref[c], qg_ref[:, rows], kd_ref[:, rows],
                              eg_ref[c], _heads(z_ref, rows), nw_ref[...])
            st[...] = s2
            for h in range(AH):
                o_ref[rows, h * ADK:(h + 1) * ADK] = og[h].astype(BF16)

    return pl.pallas_call(
        body, name="c2_fwd", grid=(nc // cps,),
        in_specs=[hm, hm, hm, hm, qks, egs, zspec, _const((1, ADK))],
        out_specs=[tok, sspec],
        out_shape=[jax.ShapeDtypeStruct((t, D), BF16), jax.ShapeDtypeStruct((nc, AH, ADK, ADK), F32)],
        scratch_shapes=[pltpu.VMEM((AH, ADK, ADK), F32)],
        compiler_params=_cparams(),
    )(u, w, qg, kd, qk, eg, proj, norm_a)


def c2_bwd(u, w, qg, kd, qk, eg, proj, norm_a, sall, do):
    t = u.shape[1]
    nc = t // CH
    cps = C2_CPS
    rev = lambda n: nc // cps - 1 - n
    hm, _, qks, egs = _c1_specs(rev, cps)
    tok = pl.BlockSpec((cps * CH, D), lambda n: (rev(n), 0))
    zspec = pl.BlockSpec((cps * CH, D), lambda n: (rev(n), C_Z // D))
    sspec = pl.BlockSpec((cps, AH, ADK, ADK), lambda n: (rev(n), 0, 0, 0))

    def body(u_ref, w_ref, qg_ref, kd_ref, qk_ref, eg_ref, z_ref, nw_ref, sall_ref, do_ref,
             du_ref, dw_ref, dqg_ref, dkd_ref, dqk_ref, deg_ref, dz_ref, dnw_ref, dst):
        n = pl.program_id(0)

        @pl.when(n == 0)
        def _():
            dst[...] = jnp.zeros_like(dst)
            dnw_ref[...] = jnp.zeros_like(dnw_ref)

        for c in reversed(range(cps)):
            rows = slice(c * CH, (c + 1) * CH)
            _, vjp = jax.vjp(c2_heads, sall_ref[c], u_ref[:, rows], w_ref[:, rows].astype(F32),
                             qk_ref[c].astype(F32), qg_ref[:, rows].astype(F32), kd_ref[:, rows].astype(F32),
                             eg_ref[c], _heads(z_ref, rows), nw_ref[...])
            ds, du, dw, dqk, dqg, dkd, deg, dz, dn = vjp((_heads(do_ref, rows), dst[...]))
            dst[...] = ds
            du_ref[:, rows] = du.astype(BF16)
            dw_ref[:, rows] = dw.astype(BF16)
            dqg_ref[:, rows] = dqg
            dkd_ref[:, rows] = dkd
            dqk_ref[c] = dqk
            deg_ref[c] = deg
            for h in range(AH):
                dz_ref[rows, h * ADK:(h + 1) * ADK] = dz[h].astype(BF16)
            dnw_ref[...] += dn

    return pl.pallas_call(
        body, name="c2_bwd", grid=(nc // cps,),
        in_specs=[hm, hm, hm, hm, qks, egs, zspec, _const((1, ADK)), sspec, tok],
        out_specs=[hm, hm, hm, hm, qks, egs, tok, _const((1, ADK))],
        out_shape=[jax.ShapeDtypeStruct((AH, t, ADK), BF16)] * 2 + [jax.ShapeDtypeStruct((AH, t, ADK), F32)] * 2 + [
            jax.ShapeDtypeStruct((nc, AH, CH, CH), F32), jax.ShapeDtypeStruct((nc, AH, 1, ADK), F32),
            jax.ShapeDtypeStruct((t, D), BF16), jax.ShapeDtypeStruct((1, ADK), F32)],
        scratch_shapes=[pltpu.VMEM((AH, ADK, ADK), F32)],
        compiler_params=_cparams(),
    )(u, w, qg, kd, qk, eg, proj, norm_a, sall, do)


NQB = TQ // CH
NKB = 2 * TQ // CH
NDIST = BPREV + 1
KLO = -(NQB - 2)
NPAIR = NKB - 1 - KLO + 1


def bias_table(rel_bias):
    nh = rel_bias.shape[0]
    relx = jnp.concatenate([rel_bias, jnp.broadcast_to(rel_bias[:, -1:], (nh, CH * BPREV + 2 * CH - 1 - RELSZ))],
                           axis=1)
    t = jnp.stack([relx[:, CH * k:CH * k + 2 * CH - 1] for k in range(NDIST)], axis=1)
    trev = t[:, :, ::-1]
    g2 = jnp.concatenate([trev[:, :, CH - 1:], jnp.zeros((nh, NDIST, 1), F32), trev[:, :, :CH - 1]], axis=2)
    flat = jnp.tile(g2, (1, 1, CH + 1))[:, :, :CH * (2 * CH - 1)]
    blk = flat.reshape(nh, NDIST, CH, 2 * CH - 1)[..., :CH]
    neg = jnp.full((nh, NQB - 1, CH, CH), NEG, F32)
    asc = jnp.concatenate([neg, blk, neg], axis=1)
    return jnp.concatenate([asc[:, 1:], asc[:, :-1]], axis=-1)


SUBQ = 4 * CH
NSUB = TQ // SUBQ
KWIN = SUBQ + BPREV * CH


def assemble_bias(tab, r):
    b0 = r * SUBQ // (2 * CH)
    rows = [jnp.concatenate([tab[NQB + a - 2 * b - KLO] for b in range(b0, b0 + KWIN // (2 * CH))], axis=1)
            for a in range(r * SUBQ // CH, (r + 1) * SUBQ // CH)]
    return jnp.concatenate(rows, axis=0)


def bias_table_bwd_layout(dtab):
    nh = dtab.shape[0]
    dasc = (jnp.pad(dtab[..., :CH], ((0, 0), (1, 0), (0, 0), (0, 0)))
            + jnp.pad(dtab[..., CH:], ((0, 0), (0, 1), (0, 0), (0, 0))))
    dblk = dasc[:, NQB - 1:NQB - 1 + NDIST]
    dr = jnp.pad(dblk, ((0, 0), (0, 0), (0, 0), (0, CH - 1)))
    flat = jnp.pad(dr.reshape(nh, NDIST, CH * (2 * CH - 1)), ((0, 0), (0, 0), (0, 3 * CH)))
    return flat.reshape(nh, NDIST, CH + 1, 2 * CH).transpose(0, 2, 1, 3).reshape(nh, CH + 1, NDIST * 2 * CH)


def _fold_matrix_np():
    f = np.zeros((NDIST * 2 * CH, 384), np.float32)
    for k in range(NDIST):
        s = k
        for xx in range(2 * CH):
            if xx == CH:
                continue
            m = CH - 1 - xx if xx < CH else 3 * CH - 1 - xx
            f[s * 2 * CH + xx, min(CH * k + m, RELSZ - 1)] = 1.0
    return f


def relbias_reduce(dlay):
    nh, rows, cols = dlay.shape
    rpad = (-rows) % 8
    dlay = jnp.pad(dlay, ((0, 0), (0, rpad), (0, 0)))
    fold = jnp.asarray(_fold_matrix_np())

    def body(d_ref, f_ref, o_ref):
        cs = jnp.sum(d_ref[0], axis=0, keepdims=True)
        o_ref[0] = _mmh(jnp.broadcast_to(cs, (8, cols)), f_ref[...])

    out = pl.pallas_call(
        body, name="relbias_reduce", grid=(nh,),
        in_specs=[pl.BlockSpec((1, rows + rpad, cols), lambda h: (h, 0, 0)), _const((cols, 384))],
        out_specs=pl.BlockSpec((1, 8, 384), lambda h: (h, 0, 0)),
        out_shape=jax.ShapeDtypeStruct((nh, 8, 384), F32),
        compiler_params=_cparams(),
    )(dlay, fold)
    return out[:, 0, :RELSZ]


ATT_PP = 4
ATT_W = ATT_PP * 2 * BDH


def attn_fwd(proj, bias):
    t = proj.shape[0]
    nt = t // TQ
    cb = C_QKVB // ATT_W

    def body(q_ref, kp_ref, kc_ref, vp_ref, vc_ref, b_ref, o_ref, p_ref):
        i = pl.program_id(1)
        firstf = jnp.where(i == 0, 1.0, 0.0)
        for pp in range(ATT_PP):
            ln = slice(pp * 2 * BDH, (pp + 1) * 2 * BDH)
            for r in range(NSUB):
                lo, hi = r * SUBQ, r * SUBQ + KWIN - TQ
                kw = jnp.concatenate([kp_ref[lo:, ln], kc_ref[:hi, ln]], axis=0)
                vw = jnp.concatenate([vp_ref[lo:, ln], vc_ref[:hi, ln]], axis=0)
                out, probs = attn_sub_fwd(q_ref[lo:lo + SUBQ, ln].astype(F32), kw, vw, b_ref[2 * pp:2 * pp + 2], r,
                                          firstf)
                o_ref[lo:lo + SUBQ, ln] = out.astype(BF16)
                for hh in range(2):
                    p_ref[2 * pp + hh, lo:lo + SUBQ, :] = probs[hh].astype(BF16)

    def blk(off, prev):
        if prev:
            return pl.BlockSpec((TQ, ATT_W), lambda p, i: (jnp.maximum(i - 1, 0), cb + off + p))
        return pl.BlockSpec((TQ, ATT_W), lambda p, i: (i, cb + off + p))

    nb = D // ATT_W
    return pl.pallas_call(
        body, name="attn_fwd", grid=(nb, nt),
        in_specs=[blk(0, False), blk(nb, True), blk(nb, False), blk(2 * nb, True), blk(2 * nb, False),
                  pl.BlockSpec((2 * ATT_PP, NPAIR, CH, 2 * CH), lambda p, i: (p, 0, 0, 0))],
        out_specs=[pl.BlockSpec((TQ, ATT_W), lambda p, i: (i, p)),
                   pl.BlockSpec((2 * ATT_PP, TQ, KWIN), lambda p, i: (p, i, 0))],
        out_shape=[jax.ShapeDtypeStruct((t, D), BF16), jax.ShapeDtypeStruct((BH, t, KWIN), BF16)],
        compiler_params=_cparams(2),
    )(proj, proj, proj, proj, proj, bias)


def attn_bwd(proj, ob, probs, do):
    t = proj.shape[0]
    nt = t // TQ
    cb = C_QKVB // ATT_W

    def body(q_ref, kp_ref, kc_ref, vp_ref, vc_ref, o_ref, p_ref, do_ref,
             dq_ref, dk_ref, dv_ref, db_ref, ck, cv, ak, av):
        i = pl.program_id(1)

        @pl.when(i == 0)
        def _():
            ck[...] = jnp.zeros_like(ck)
            cv[...] = jnp.zeros_like(cv)
            db_ref[...] = jnp.zeros_like(db_ref)

        @pl.when(i < nt)
        def _():
            ak[...] = jnp.zeros_like(ak)
            av[...] = jnp.zeros_like(av)
            for pp in range(ATT_PP):
                ln = slice(pp * 2 * BDH, (pp + 1) * 2 * BDH)
                for r in range(NSUB):
                    lo, hi = r * SUBQ, r * SUBQ + KWIN - TQ
                    rows = slice(lo, lo + SUBQ)
                    kw = jnp.concatenate([kp_ref[lo:, ln], kc_ref[:hi, ln]], axis=0)
                    vw = jnp.concatenate([vp_ref[lo:, ln], vc_ref[:hi, ln]], axis=0)
                    probs_r = [p_ref[2 * pp + hh, rows, :].astype(F32) for hh in range(2)]
                    dq, dkw, dvw, dss = attn_sub_bwd(q_ref[rows, ln].astype(F32), kw, vw,
                                                     o_ref[rows, ln].astype(F32), do_ref[rows, ln], probs_r, r)
                    dq_ref[rows, ln] = dq.astype(BF16)
                    ak[lo:lo + KWIN, ln] += dkw
                    av[lo:lo + KWIN, ln] += dvw
                    for hh in range(2):
                        _, scatter = jax.vjp(lambda tab: assemble_bias(tab, r), jnp.zeros((NPAIR, CH, 2 * CH), F32))
                        db_ref[2 * pp + hh] += scatter(dss[hh])[0]
            dk_ref[...] = (ck[...] + ak[:TQ, :]).astype(BF16)
            dv_ref[...] = (cv[...] + av[:TQ, :]).astype(BF16)
            ck[...] = ak[TQ:, :]
            cv[...] = av[TQ:, :]

        @pl.when(i == nt)
        def _():
            dk_ref[...] = ck[...].astype(BF16)
            dv_ref[...] = cv[...].astype(BF16)

    def blk(off, prev):
        if prev:
            return pl.BlockSpec((TQ, ATT_W), lambda p, i: (jnp.clip(i - 1, 0, nt - 1), cb + off + p))
        return pl.BlockSpec((TQ, ATT_W), lambda p, i: (jnp.minimum(i, nt - 1), cb + off + p))

    nb = D // ATT_W
    own = pl.BlockSpec((TQ, ATT_W), lambda p, i: (jnp.minimum(i, nt - 1), p))
    lag = pl.BlockSpec((TQ, ATT_W), lambda p, i: (jnp.maximum(i - 1, 0), p))
    return pl.pallas_call(
        body, name="attn_bwd", grid=(nb, nt + 1),
        in_specs=[blk(0, False), blk(nb, True), blk(nb, False), blk(2 * nb, True), blk(2 * nb, False), own,
                  pl.BlockSpec((2 * ATT_PP, TQ, KWIN), lambda p, i: (p, jnp.minimum(i, nt - 1), 0)), own],
        out_specs=[own, lag, lag, pl.BlockSpec((2 * ATT_PP, NPAIR, CH, 2 * CH), lambda p, i: (p, 0, 0, 0))],
        out_shape=[jax.ShapeDtypeStruct((t, D), BF16)] * 3 + [jax.ShapeDtypeStruct((BH, NPAIR, CH, 2 * CH), F32)],
        scratch_shapes=[pltpu.VMEM((TQ, ATT_W), F32), pltpu.VMEM((TQ, ATT_W), F32),
                        pltpu.VMEM((2 * TQ, ATT_W), F32), pltpu.VMEM((2 * TQ, ATT_W), F32)],
        compiler_params=_cparams(2),
    )(proj, proj, proj, proj, proj, ob, probs, do)


MERGE_TM = 256


def merge_fwd(x, oa, ob, proj, vecs, wa, wb, wo):
    t = x.shape[0]
    tm = MERGE_TM
    names = ("bga", "bgb", "gate_t", "g1", "b1", "scale_f", "shift_f")

    def body(x_ref, oa_ref, ob_ref, gra_ref, grb_ref, *rest):
        vrefs = rest[:7]
        wa_ref, wb_ref, wo_ref, y_ref, h_ref = rest[7:]
        vv = [r[...] for r in vrefs]
        zero = jnp.zeros((tm, D), F32)
        y1, _ = merge_fn(x_ref[...], oa_ref[...], ob_ref[...], gra_ref[...], grb_ref[...], zero, zero, zero,
                         *vv, wa_ref[...], wb_ref[...], wo_ref[...])
        y_ref[...] = y1
        h_ref[...] = (y1 * (1.0 + vv[5]) + vv[6]).astype(BF16)

    return pl.pallas_call(
        body, name="merge_fwd", grid=(t // tm,),
        in_specs=[_rows(tm, D), _rows(tm, D), _rows(tm, D), _rows(tm, D, C_GATE // D), _rows(tm, D, C_GATE // D + 1)]
        + [_const((1, D))] * 7 + [_const((D, D))] * 3,
        out_specs=[_rows(tm, D), _rows(tm, D)],
        out_shape=[jax.ShapeDtypeStruct((t, D), F32), jax.ShapeDtypeStruct((t, D), BF16)],
        compiler_params=_cparams(),
    )(x, oa, ob, proj, proj, *[vecs[n] for n in names], wa, wb, wo)


def merge_bwd(x, oa, ob, proj, vecs, wa, wb, wo, dy1):
    t = x.shape[0]
    tm = MERGE_TM
    names = ("bga", "bgb", "gate_t", "g1", "b1", "scale_f", "shift_f")

    def body(x_ref, oa_ref, ob_ref, gra_ref, grb_ref, *rest):
        vrefs = rest[:7]
        wa_ref, wb_ref, wo_ref, dy_ref = rest[7:11]
        (dx_ref, doa_ref, dob_ref, dga_ref, dgb_ref, mg_ref, dmix_ref, dpa_ref, dpb_ref,
         dbga_ref, dbgb_ref, dgt_ref, dg1_ref, db1_ref) = rest[11:]
        i = pl.program_id(0)
        vv = [r[...] for r in vrefs]
        zero = jnp.zeros((tm, D), F32)

        def f(x_, oa_, ob_, gra_, grb_, ppa, ppb, pmix, bga, bgb, gate_t, g1, b1):
            return merge_fn(x_, oa_, ob_, gra_, grb_, ppa, ppb, pmix, bga, bgb, gate_t, g1, b1, vv[5], vv[6],
                            wa_ref[...], wb_ref[...], wo_ref[...])

        _, vjp, merged = jax.vjp(f, x_ref[...], oa_ref[...].astype(F32), ob_ref[...].astype(F32),
                                 gra_ref[...], grb_ref[...], zero, zero, zero, *vv[:5], has_aux=True)
        dx, doa, dob, dga, dgb, dpa, dpb, dmix, dbga, dbgb, dgt, dg1, db1 = vjp(dy_ref[...])
        dx_ref[...] = dx
        doa_ref[...] = doa
        dob_ref[...] = dob
        dga_ref[...] = dga.astype(BF16)
        dgb_ref[...] = dgb.astype(BF16)
        mg_ref[...] = merged.astype(BF16)
        dmix_ref[...] = dmix.astype(BF16)
        dpa_ref[...] = dpa.astype(BF16)
        dpb_ref[...] = dpb.astype(BF16)
        accs = (dbga_ref, dbgb_ref, dgt_ref, dg1_ref, db1_ref)

        @pl.when(i == 0)
        def _():
            for a in accs:
                a[...] = jnp.zeros_like(a)

        for a, val in zip(accs, (dbga, dbgb, dgt, dg1, db1)):
            a[...] += val

    return pl.pallas_call(
        body, name="merge_bwd", grid=(t // tm,),
        in_specs=[_rows(tm, D), _rows(tm, D), _rows(tm, D), _rows(tm, D, C_GATE // D), _rows(tm, D, C_GATE // D + 1)]
        + [_const((1, D))] * 7 + [_const((D, D))] * 3 + [_rows(tm, D)],
        out_specs=[_rows(tm, D)] * 9 + [_const((1, D))] * 5,
        out_shape=[jax.ShapeDtypeStruct((t, D), F32)] * 3 + [jax.ShapeDtypeStruct((t, D), BF16)] * 6
        + [jax.ShapeDtypeStruct((1, D), F32)] * 5,
        compiler_params=_cparams(),
    )(x, oa, ob, proj, proj, *[vecs[n] for n in names], wa, wb, wo, dy1)


FFN_TM = 128


def ffn_act_fwd(up, conv_w, bconv):
    t, wdt = up.shape
    tm = FFN_TM

    def body(prev_ref, cur_ref, cw_ref, bc_ref, a_ref):
        i = pl.program_id(0)
        flag = jnp.where(i > 0, 1.0, 0.0)

        def ext(sl):
            return jnp.concatenate([prev_ref[:, sl] * flag, cur_ref[:, sl]], axis=0)

        def rows(sl):
            return tuple(cw_ref[j:j + 1, sl] for j in range(3))

        for cb in range(DFF // LANE):
            g = slice(cb * LANE, (cb + 1) * LANE)
            v = slice(DFF + cb * LANE, DFF + (cb + 1) * LANE)
            a_ref[:, g] = ffn_act_fn(ext(g), ext(v), rows(g), rows(v), bc_ref[:, g], bc_ref[:, v]).astype(BF16)

    return pl.pallas_call(
        body, name="ffn_act_fwd", grid=(t // tm,),
        in_specs=_halo_specs(tm, wdt, 0, lambda i: i) + [_const((3, wdt)), _const((1, wdt))],
        out_specs=_rows(tm, DFF),
        out_shape=jax.ShapeDtypeStruct((t, DFF), BF16),
        compiler_params=_cparams(),
    )(up, up, conv_w, bconv)


def ffn_act_bwd(up, conv_w, bconv, da):
    t, wdt = up.shape
    tm = FFN_TM
    nt = t // tm
    rev = lambda i: nt - 1 - i

    def body(prev_ref, cur_ref, cw_ref, bc_ref, da_ref, dup_ref, dcw_ref, dbc_ref, carry):
        i = pl.program_id(0)
        flag = jnp.where(i < nt - 1, 1.0, 0.0)

        @pl.when(i == 0)
        def _():
            carry[...] = jnp.zeros_like(carry)
            dcw_ref[...] = jnp.zeros_like(dcw_ref)
            dbc_ref[...] = jnp.zeros_like(dbc_ref)

        def ext(sl):
            return jnp.concatenate([prev_ref[:, sl] * flag, cur_ref[:, sl]], axis=0)

        def rows(sl):
            return tuple(cw_ref[j:j + 1, sl] for j in range(3))

        def emit(sl, dext, drows, dbc):
            dcur = dext[HALO:]
            dup_ref[:, sl] = jnp.concatenate([dcur[:tm - HALO], dcur[tm - HALO:] + carry[:, sl]], axis=0).astype(BF16)
            carry[:, sl] = dext[:HALO]
            dcw_ref[:, sl] += _stack_rows(drows)
            dbc_ref[:, sl] += dbc

        for cb in range(DFF // LANE):
            g = slice(cb * LANE, (cb + 1) * LANE)
            v = slice(DFF + cb * LANE, DFF + (cb + 1) * LANE)
            _, vjp = jax.vjp(ffn_act_fn, ext(g), ext(v), rows(g), rows(v), bc_ref[:, g], bc_ref[:, v])
            dxg, dxv, drg, drv, dbg, dbv = vjp(da_ref[:, g])
            emit(g, dxg, drg, dbg)
            emit(v, dxv, drv, dbv)

    return pl.pallas_call(
        body, name="ffn_act_bwd", grid=(nt,),
        in_specs=_halo_specs(tm, wdt, 0, rev) + [_const((3, wdt)), _const((1, wdt)), _rows(tm, DFF, 0, rev)],
        out_specs=[_rows(tm, wdt, 0, rev), _const((3, wdt)), _const((1, wdt))],
        out_shape=[jax.ShapeDtypeStruct((t, wdt), BF16), jax.ShapeDtypeStruct((3, wdt), F32),
                   jax.ShapeDtypeStruct((1, wdt), F32)],
        scratch_shapes=[pltpu.VMEM((HALO, wdt), F32)],
        compiler_params=_cparams(),
    )(up, up, conv_w, bconv, da)


HEAD_TM = 256


def head_fwd_bwd(a, y1, tgt, gate_f, g2, b2, wd):
    t = a.shape[0]
    tm = HEAD_TM

    def body(a_ref, y_ref, t_ref, gf_ref, g2_ref, b2_ref, wd_ref,
             da_ref, dy_ref, dffn_ref, dgf_ref, dg2_ref, db2_ref, loss_ref):
        i = pl.program_id(0)
        zero = jnp.zeros((tm, D), F32)

        def f(a_, y_, pf, gf, g2_, b2_):
            return head_fn(a_, y_, pf, gf, g2_, b2_, t_ref[...], wd_ref[...])

        loss, vjp = jax.vjp(f, a_ref[...].astype(F32), y_ref[...], zero, gf_ref[...], g2_ref[...], b2_ref[...])
        da, dy, dffn, dgf, dg2, db2 = vjp(jnp.ones((), F32))
        da_ref[...] = da
        dy_ref[...] = dy
        dffn_ref[...] = dffn.astype(BF16)
        accs = (dgf_ref, dg2_ref, db2_ref, loss_ref)

        @pl.when(i == 0)
        def _():
            for r in accs:
                r[...] = jnp.zeros_like(r)

        dgf_ref[...] += dgf
        dg2_ref[...] += dg2
        db2_ref[...] += db2
        loss_ref[...] += loss * jnp.ones((1, 128), F32)

    return pl.pallas_call(
        body, name="head_fwd_bwd", grid=(t // tm,),
        in_specs=[_rows(tm, DFF), _rows(tm, D), _rows(tm, D), _const((1, D)), _const((1, D)), _const((1, D)),
                  _const((DFF, D))],
        out_specs=[_rows(tm, DFF), _rows(tm, D), _rows(tm, D), _const((1, D)), _const((1, D)), _const((1, D)),
                   _const((1, 128))],
        out_shape=[jax.ShapeDtypeStruct((t, DFF), F32), jax.ShapeDtypeStruct((t, D), F32),
                   jax.ShapeDtypeStruct((t, D), BF16)] + [jax.ShapeDtypeStruct((1, D), F32)] * 3
        + [jax.ShapeDtypeStruct((1, 128), F32)],
        compiler_params=_cparams(),
    )(a, y1, tgt, gate_f, g2, b2, wd)


def ada_fwd(c_all, w_sh, b_sh):
    def body(c_ref, w_ref, b_ref, o_ref):
        o_ref[...] = _mmh(_silu(c_ref[...]), w_ref[...]) + b_ref[...]

    n = w_sh.shape[1]
    return pl.pallas_call(
        body, name="ada_fwd", out_shape=jax.ShapeDtypeStruct((NDEV, n), F32),
        in_specs=[pl.BlockSpec(memory_space=pltpu.VMEM)] * 3,
        out_specs=pl.BlockSpec(memory_space=pltpu.VMEM),
        compiler_params=pltpu.CompilerParams(vmem_limit_bytes=VMEM_LIMIT),
    )(c_all, w_sh, b_sh)


def ada_wgrad(c_all_t, dmod_sh):
    def body(c_ref, d_ref, o_ref):
        o_ref[...] = _mmh(_silu(c_ref[...]), d_ref[...])

    return pl.pallas_call(
        body, name="ada_wgrad", out_shape=jax.ShapeDtypeStruct((c_all_t.shape[0], dmod_sh.shape[1]), F32),
        in_specs=[pl.BlockSpec(memory_space=pltpu.VMEM)] * 2,
        out_specs=pl.BlockSpec(memory_space=pltpu.VMEM),
        compiler_params=pltpu.CompilerParams(vmem_limit_bytes=VMEM_LIMIT),
    )(c_all_t, dmod_sh)


def adamw(gparts, w, m, v, name):
    p, r, c = gparts.shape
    tr = r if r <= 256 else _pick(r, (256, 128, 64, 32, 16, 8))
    c1 = 1.0 - B1 ** STEP
    c2 = 1.0 - B2 ** STEP

    def body(g_ref, w_ref, m_ref, v_ref, go_ref, d_ref, mo_ref, vo_ref):
        g = g_ref[0].astype(F32)
        for s in range(1, p):
            g = g + g_ref[s].astype(F32)
        mn = B1 * m_ref[0] + (1.0 - B1) * g
        vn = B2 * v_ref[0] + (1.0 - B2) * (g * g)
        go_ref[0] = g
        d_ref[0] = -LR * ((mn / c1) / (jnp.sqrt(vn / c2) + AEPS) + WD * w_ref[0])
        mo_ref[0] = mn
        vo_ref[0] = vn

    spec = pl.BlockSpec((1, tr, c), lambda i: (0, i, 0))
    return pl.pallas_call(
        body, name=name, grid=(r // tr,),
        in_specs=[pl.BlockSpec((p, tr, c), lambda i: (0, i, 0)), spec, spec, spec],
        out_specs=[spec] * 4,
        out_shape=[jax.ShapeDtypeStruct((1, r, c), F32)] * 4,
        compiler_params=_cparams(),
    )(gparts, w, m, v)


def adamw_small(gs, ws, ms, vs, loss_parts, name):
    n = len(ws)
    c1 = 1.0 - B1 ** STEP
    c2 = 1.0 - B2 ** STEP

    def slots(ref):
        acc = ref[0]
        for s in range(1, ref.shape[0]):
            acc = acc + ref[s]
        return acc

    def body(*refs):
        g_refs, w_refs, m_refs, v_refs = (refs[k * n:(k + 1) * n] for k in range(4))
        l_ref, outs = refs[4 * n], refs[4 * n + 1:]
        for i in range(n):
            g = slots(g_refs[i])
            mn = B1 * m_refs[i][...] + (1.0 - B1) * g
            vn = B2 * v_refs[i][...] + (1.0 - B2) * (g * g)
            outs[i][...] = g
            outs[n + i][...] = -LR * ((mn / c1) / (jnp.sqrt(vn / c2) + AEPS) + WD * w_refs[i][...])
            outs[2 * n + i][...] = mn
            outs[3 * n + i][...] = vn
        outs[4 * n][...] = slots(l_ref)

    vmem = pl.BlockSpec(memory_space=pltpu.VMEM)
    outs = pl.pallas_call(
        body, name=name,
        in_specs=[vmem] * (4 * n + 1), out_specs=[vmem] * (4 * n + 1),
        out_shape=[jax.ShapeDtypeStruct(w.shape, F32) for w in ws] * 4 + [jax.ShapeDtypeStruct((1, LANE), F32)],
        compiler_params=pltpu.CompilerParams(vmem_limit_bytes=VMEM_LIMIT),
    )(*gs, *ws, *ms, *vs, loss_parts)
    return outs[:n], outs[n:2 * n], outs[2 * n:3 * n], outs[3 * n:4 * n], outs[4 * n]


def _me():
    x, y, c = lax.axis_index("x"), lax.axis_index("y"), lax.axis_index("c")
    return x, y, c, 4 * x + 2 * y + c


def _peer(x, y, c, d):
    px = 1 - x if (d >> 2) & 1 else x
    py = 1 - y if (d >> 1) & 1 else y
    pc = 1 - c if d & 1 else c
    return (px, py, pc), 4 * px + 2 * py + pc


def _exchange(arrs, name, scatter):
    n = len(arrs)

    def body(*refs):
        ins, outs = refs[:n], refs[n:2 * n]
        send, recv, lsem = refs[2 * n:]
        x, y, c, me = _me()
        remote, local = [], []
        for k in range(n):
            src = ins[k].at[me] if scatter else ins[k]
            cp = pltpu.make_async_copy(src, outs[k].at[me], lsem.at[k])
            cp.start()
            local.append(cp)
            for d in range(1, NDEV):
                dev, pid = _peer(x, y, c, d)
                src = ins[k].at[pid] if scatter else ins[k]
                cp = pltpu.make_async_remote_copy(src_ref=src, dst_ref=outs[k].at[me],
                                                  send_sem=send.at[k, d - 1], recv_sem=recv.at[k, d - 1],
                                                  device_id=dev, device_id_type=pl.DeviceIdType.MESH)
                cp.start()
                remote.append(cp)
        for cp in remote:
            cp.wait()
        for cp in local:
            cp.wait()

    shapes = [a.shape if scatter else (NDEV,) + a.shape for a in arrs]
    return pl.pallas_call(
        body, name=name,
        in_specs=[pl.BlockSpec(memory_space=pl.ANY)] * n,
        out_specs=[pl.BlockSpec(memory_space=pl.ANY)] * n,
        out_shape=[jax.ShapeDtypeStruct(s, a.dtype) for s, a in zip(shapes, arrs)],
        scratch_shapes=[pltpu.SemaphoreType.DMA((n, NDEV - 1)), pltpu.SemaphoreType.DMA((n, NDEV - 1)),
                        pltpu.SemaphoreType.DMA((n,))],
        compiler_params=pltpu.CompilerParams(has_side_effects=True),
    )(*arrs)


def all_gather(arrs, name):
    return _exchange(arrs, name, False)


def all_gather_two_level(shard, name):
    def body(x_ref, out_ref, send, recv, lsem):
        x, y, c, _ = _me()
        sibling = (x, y, 1 - c)
        chips = [(1 - x, y), (x, 1 - y), (1 - x, 1 - y)]

        def slot(px, py, pc):
            return out_ref.at[4 * px + 2 * py + pc]

        def copy(k, block, to, src=None):
            return pltpu.make_async_remote_copy(
                src_ref=slot(*block) if src is None else src, dst_ref=slot(*block),
                send_sem=send.at[k], recv_sem=recv.at[k], device_id=to, device_id_type=pl.DeviceIdType.MESH)

        mine = pltpu.make_async_copy(x_ref, slot(x, y, c), lsem)
        mine.start()
        first = [copy(0, (x, y, c), sibling, src=x_ref)]
        first += [copy(1 + j, (x, y, c), (*chip, c), src=x_ref) for j, chip in enumerate(chips)]
        for cp in first:
            cp.start()
        passed = [copy(4 + j, (*chip, c), sibling) for j, chip in enumerate(chips)]
        for j, chip in enumerate(chips):
            copy(1 + j, (*chip, c), (x, y, c)).wait_recv()
            passed[j].start()
        copy(0, sibling, (x, y, c)).wait_recv()
        for j, chip in enumerate(chips):
            copy(4 + j, (*chip, 1 - c), (x, y, c)).wait_recv()
        for cp in first + passed:
            cp.wait_send()
        mine.wait()

    return pl.pallas_call(
        body, name=name,
        in_specs=[pl.BlockSpec(memory_space=pl.ANY)],
        out_specs=pl.BlockSpec(memory_space=pl.ANY),
        out_shape=jax.ShapeDtypeStruct((NDEV,) + shard.shape, shard.dtype),
        scratch_shapes=[pltpu.SemaphoreType.DMA((NPEER,)), pltpu.SemaphoreType.DMA((NPEER,)),
                        pltpu.SemaphoreType.DMA],
        compiler_params=pltpu.CompilerParams(has_side_effects=True),
    )(shard)


def all_to_all(arrs, name):
    return _exchange(arrs, name, True)


_HBM = pl.BlockSpec(memory_space=pltpu.HBM)
_SEM = pl.BlockSpec(memory_space=pltpu.SEMAPHORE)
_EFFECT = pltpu.SideEffectType.DATAFLOW_SIDE_EFFECTING
NPEER = NDEV - 1


def exchange_start(arrs, name, scatter):
    n = len(arrs)
    lands = [lax.empty(a.shape if scatter else (NDEV,) + a.shape, a.dtype) for a in arrs]

    def body(*refs):
        ins, lrefs = refs[:n], refs[n:2 * n]
        send, recv, token = refs[2 * n], refs[2 * n + 1], refs[-1]
        x, y, c, me = _me()
        for k in range(n):
            for d in range(1, NDEV):
                dev, pid = _peer(x, y, c, d)
                src = ins[k].at[pid] if scatter else ins[k]
                pltpu.make_async_remote_copy(src_ref=src, dst_ref=lrefs[k].at[me],
                                             send_sem=send.at[k * NPEER + d - 1], recv_sem=recv.at[k * NPEER + d - 1],
                                             device_id=dev, device_id_type=pl.DeviceIdType.MESH).start()
        token[...] = jnp.zeros_like(token)

    thru = [pltpu.HBM(a.shape, a.dtype) for a in list(arrs) + lands]
    outs = pl.pallas_call(
        body, name=name,
        out_shape=(pltpu.SemaphoreType.DMA((n * NPEER,)), pltpu.SemaphoreType.DMA((n * NPEER,)), *thru,
                   jax.ShapeDtypeStruct((8, 128), F32)),
        in_specs=[_HBM] * (2 * n),
        out_specs=(_SEM, _SEM, *([_HBM] * (2 * n)), pl.BlockSpec(memory_space=pltpu.VMEM)),
        input_output_aliases={i: 2 + i for i in range(2 * n)},
        compiler_params=pltpu.CompilerParams(has_side_effects=_EFFECT),
    )(*[pltpu.with_memory_space_constraint(a, pltpu.HBM) for a in list(arrs) + lands])
    handle = dict(send=outs[0], recv=outs[1], src=list(outs[2:2 + n]), land=list(outs[2 + n:2 + 2 * n]),
                  scatter=scatter)
    return handle, outs[-1][0, 0]


def exchange_wait(handle, after, name):
    n = len(handle["src"])
    scatter = handle["scatter"]

    def body(*refs):
        ins, lrefs = refs[:n], refs[n:2 * n]
        send, recv = refs[2 * n], refs[2 * n + 1]
        x, y, c, _ = _me()
        for k in range(n):
            for d in range(1, NDEV):
                dev, _ = _peer(x, y, c, d)
                src = ins[k].at[0] if scatter else ins[k]
                cp = pltpu.make_async_remote_copy(src_ref=src, dst_ref=lrefs[k].at[0],
                                                  send_sem=send.at[k * NPEER + d - 1],
                                                  recv_sem=recv.at[k * NPEER + d - 1],
                                                  device_id=dev, device_id_type=pl.DeviceIdType.MESH)
                cp.wait_send()
                cp.wait_recv()

    arrs = handle["src"] + handle["land"]
    outs = pl.pallas_call(
        body, name=name,
        out_shape=tuple(pltpu.HBM(a.shape, a.dtype) for a in arrs),
        in_specs=[_HBM] * (2 * n) + [_SEM, _SEM, pl.BlockSpec(memory_space=pl.ANY)],
        out_specs=tuple([_HBM] * (2 * n)),
        input_output_aliases={i: i for i in range(2 * n)},
        compiler_params=pltpu.CompilerParams(has_side_effects=_EFFECT),
    )(*arrs, handle["send"], handle["recv"], after)
    me = 4 * lax.axis_index("x") + 2 * lax.axis_index("y") + lax.axis_index("c")
    landed = []
    for own, land in zip(outs[:n], outs[n:]):
        mine = lax.dynamic_index_in_dim(own, me, 0, keepdims=True) if scatter else own[None]
        landed.append(lax.dynamic_update_slice_in_dim(land, mine, me, 0))
    return landed


def _cat_from_slabs(slabs):
    _, k, n = slabs.shape

    def cols(lo, hi):
        parts, c = [], lo
        while c < hi:
            j = c // n
            e = min(hi, (j + 1) * n)
            parts.append(slabs[j][:, c - j * n:e - j * n])
            c = e
        return parts

    def zeros(w):
        return [jnp.zeros((k, w), slabs.dtype)]

    return jnp.concatenate(cols(0, 4096) + cols(4112, 9232) + cols(4096, 4104) + zeros(LANE - AH)
                           + cols(4104, 4112) + zeros(NCAT - C_BA - LANE - AH), axis=1)


IN_PIECES = (("pre", C_QKVA, 3072), ("z", C_Z, 1024), ("qb", C_QKVB, 1024), ("kb", C_QKVB + 1024, 1024),
             ("vb", C_QKVB + 2048, 1024), ("ga", C_GATE, 1024), ("gb", C_GATE + 1024, 1024))
_ORIG_SEGS = ((0, 3072, "pre", 0), (3072, 4096, "z", 0), (4096, 4104, "ba", 0), (4104, 4112, "ba", LANE),
              (4112, 5136, "qb", 0), (5136, 6160, "kb", 0), (6160, 7184, "vb", 0), (7184, 8208, "ga", 0),
              (8208, 9232, "gb", 0))


def _orig_cols_from_pieces(gp, lo, hi):
    parts = []
    for a, b, name, off in _ORIG_SEGS:
        s, e = max(a, lo), min(b, hi)
        if s < e:
            parts.append(gp[name][:, off + s - a:off + e - a])
    return parts[0] if len(parts) == 1 else jnp.concatenate(parts, axis=1)


def _pad128(v):
    return jnp.pad(v, ((0, 0), (0, 128 - v.shape[1])))


def local_step(x, tgt, mod, wts, small, late_weights=None, on_grads=None):
    if on_grads is None:
        on_grads = lambda group, gd: jnp.zeros((), F32)
    t = x.shape[0]
    nc = t // CH
    shift_t, scale_t, gate_t, shift_f, scale_f, gate_f = mod
    wcat = _cat_from_slabs(wts["w_in_slabs"])
    a_log = _pad128(small["a_log"])
    dtb = _pad128(small["dt_bias"])
    vecs = dict(bga=small["b_gate"][:, :D], bgb=small["b_gate"][:, D:], gate_t=gate_t, g1=small["ln1_g"],
                b1=small["ln1_b"], scale_f=scale_f, shift_f=shift_f)

    h1 = modulate(x, scale_t, shift_t, "modulate_t")
    proj = matmul(h1, wcat, F32, "in_proj")
    q, k, v, gcs, beta = prep_fwd(proj, small["conv_a"], a_log, dtb)

    u, w, qg, kd, qk, eg, tinv = c1_fwd(q, k, v, gcs, beta)
    oa, sall = c2_fwd(u, w, qg, kd, qk, eg, proj, small["norm_a"])
    bias = bias_table(small["rel_bias"])
    ob, probs = attn_fwd(proj, bias)
    if late_weights is not None:
        wts = {**wts, **late_weights(ob)}
    y1, h2 = merge_fwd(x, oa, ob, proj, vecs, wts["w_a"], wts["w_b"], wts["w_o"])
    up = matmul(h2, wts["w_up"], F32, "up_proj")
    a = ffn_act_fwd(up, small["conv_ffn"], small["b_conv_ffn"])

    da, dy1_res, dffn, dgate_f, dg2, db2, loss = head_fwd_bwd(a, y1, tgt, gate_f, small["ln2_g"], small["ln2_b"],
                                                            wts["w_down"])
    g_w_down = matmul(a, dffn, BF16, "wgrad_down", ta=True)
    dup, g_conv_ffn, g_bconv = ffn_act_bwd(up, small["conv_ffn"], small["b_conv_ffn"], da)
    g_w_up = matmul(h2, dup, BF16, "wgrad_up", ta=True)
    tok = on_grads("ffn", dict(w_up=g_w_up, w_down=g_w_down))
    dy1, dscale_f, dshift_f = dgrad_modulated(dup, wts["w_up"], y1, dy1_res, scale_f + tok, "dgrad_up")
    (dx_res, doa, dob, dga, dgb, merged, dmix, dpa, dpb,
     dbga, dbgb, dgate_t, dg1, db1) = merge_bwd(x, oa, ob, proj, vecs, wts["w_a"], wts["w_b"], wts["w_o"], dy1)
    g_w_o = matmul(merged, dmix, BF16, "wgrad_o", ta=True)
    g_w_a = matmul(oa, dpa, BF16, "wgrad_a", ta=True)
    g_w_b = matmul(ob, dpb, BF16, "wgrad_b", ta=True)
    tok = on_grads("mix", dict(w_o=g_w_o, w_a=g_w_a, w_b=g_w_b))
    dqb, dkb, dvb, dbias = attn_bwd(proj, ob, probs, dob)
    g_rel = relbias_reduce(bias_table_bwd_layout(dbias))
    du, dw, dqg, dkd, dqk, deg, dz, g_norm = c2_bwd(u, w, qg, kd, qk, eg, proj, small["norm_a"] + tok, sall, doa)
    dq, dk, dv, dgcs, dbeta = c1_bwd(q, k, v, gcs, beta, tinv, du, dw, dqg, dkd, dqk, deg)
    dpre, dbb, daa, g_conv_a, g_alog, g_dtb = prep_bwd(proj, small["conv_a"], a_log, dtb, dq, dk, dv, dgcs, dbeta)
    tok = on_grads("small", dict(conv_a=g_conv_a, rel_bias=g_rel, conv_ffn=g_conv_ffn))
    dba = jnp.concatenate([dbb, daa, jnp.zeros((t, NCAT - C_BA - 2 * LANE), BF16)], axis=1) + tok.astype(BF16)
    dpieces = dict(pre=dpre, z=dz, qb=dqb, kb=dkb, vb=dvb, ga=dga, gb=dgb)
    g_in = {n: matmul(h1, dpieces[n], BF16, "wgrad_in_" + n, ta=True) for n, _, _ in IN_PIECES}
    g_in["ba"] = matmul(h1, dba, BF16, "wgrad_in_ba", ta=True)
    tok = on_grads("in", g_in)
    dh1 = dgrad_pieces([(dpieces[n], off) for n, off, _ in IN_PIECES], dba + tok.astype(BF16), wcat,
                       "dgrad_in")
    grad_x, dscale_t, dshift_t = modulate_bwd(dh1, x, dx_res, scale_t + tok, "modulate_t_bwd")

    dmod = (dshift_t, dscale_t, dgate_t, dshift_f, dscale_f, dgate_f)
    grads = dict(w_in=_orig_cols_from_pieces(g_in, 0, 9232), w_up=g_w_up, w_down=g_w_down, w_a=g_w_a, w_b=g_w_b, w_o=g_w_o,
                 conv_a=g_conv_a, rel_bias=g_rel, conv_ffn=g_conv_ffn,
                 b_gate=jnp.concatenate([dbga, dbgb], axis=1), a_log=g_alog[:, :AH], dt_bias=g_dtb[:, :AH],
                 norm_a=g_norm, ln1_g=dg1, ln1_b=db1, b_conv_ffn=g_bconv, ln2_g=dg2, ln2_b=db2)
    return loss[0, 0], grad_x, dmod, grads


REP_NAMES = ["b_ada", "b_gate", "a_log", "dt_bias", "norm_a", "ln1_g", "ln1_b", "b_conv_ffn", "ln2_g", "ln2_b"]
SH_NAMES = ["conv_a", "rel_bias", "conv_ffn"]


def _col_shards(a, n):
    return a.reshape(a.shape[0], NDEV, n).transpose(1, 0, 2)


def kernel(x, c, w_ada, b_ada, w_in, b_gate, conv_a, a_log, dt_bias, norm_a, rel_bias, w_branch_a, w_branch_b, w_o, ln1_g, ln1_b, w_up, conv_ffn, b_conv_ffn, w_down, ln2_g, ln2_b, loss_target, m_w_ada, m_b_ada, m_w_in, m_b_gate, m_conv_a, m_a_log, m_dt_bias, m_norm_a, m_rel_bias, m_w_branch_a, m_w_branch_b, m_w_o, m_ln1_g, m_ln1_b, m_w_up, m_conv_ffn, m_b_conv_ffn, m_w_down, m_ln2_g, m_ln2_b, v_w_ada, v_b_ada, v_w_in, v_b_gate, v_conv_a, v_a_log, v_dt_bias, v_norm_a, v_rel_bias, v_w_branch_a, v_w_branch_b, v_w_o, v_ln1_g, v_ln1_b, v_w_up, v_conv_ffn, v_b_conv_ffn, v_w_down, v_ln2_g, v_ln2_b):
    W = dict(w_ada=w_ada, b_ada=b_ada, w_in=w_in, b_gate=b_gate, conv_a=conv_a, a_log=a_log, dt_bias=dt_bias,
             norm_a=norm_a, rel_bias=rel_bias, w_branch_a=w_branch_a, w_branch_b=w_branch_b, w_o=w_o, ln1_g=ln1_g,
             ln1_b=ln1_b, w_up=w_up, conv_ffn=conv_ffn, b_conv_ffn=b_conv_ffn, w_down=w_down, ln2_g=ln2_g,
             ln2_b=ln2_b)
    M = dict(w_ada=m_w_ada, b_ada=m_b_ada, w_in=m_w_in, b_gate=m_b_gate, conv_a=m_conv_a, a_log=m_a_log,
             dt_bias=m_dt_bias, norm_a=m_norm_a, rel_bias=m_rel_bias, w_branch_a=m_w_branch_a,
             w_branch_b=m_w_branch_b, w_o=m_w_o, ln1_g=m_ln1_g, ln1_b=m_ln1_b, w_up=m_w_up, conv_ffn=m_conv_ffn,
             b_conv_ffn=m_b_conv_ffn, w_down=m_w_down, ln2_g=m_ln2_g, ln2_b=m_ln2_b)
    V = dict(w_ada=v_w_ada, b_ada=v_b_ada, w_in=v_w_in, b_gate=v_b_gate, conv_a=v_conv_a, a_log=v_a_log,
             dt_bias=v_dt_bias, norm_a=v_norm_a, rel_bias=v_rel_bias, w_branch_a=v_w_branch_a,
             w_branch_b=v_w_branch_b, w_o=v_w_o, ln1_g=v_ln1_g, ln1_b=v_ln1_b, w_up=v_w_up, conv_ffn=v_conv_ffn,
             b_conv_ffn=v_b_conv_ffn, w_down=v_w_down, ln2_g=v_ln2_g, ln2_b=v_ln2_b)
    W3, M3, V3 = W, M, V
    W, M, V = ({n: a[0] for n, a in dct.items()} for dct in (W, M, V))
    me = 4 * lax.axis_index("x") + 2 * lax.axis_index("y") + lax.axis_index("c")
    big = ("w_in", "w_up", "w_down", "w_branch_a", "w_branch_b", "w_o")

    g_in = all_gather_two_level(W["w_in"].astype(BF16), "gather_w_in")
    wts = dict(w_in_slabs=g_in)
    c_all, *sh_all = all_gather([c] + [W[n] for n in SH_NAMES], "gather_small")
    c_all = c_all.reshape(NDEV, D)

    def full_small(g8):
        return g8.transpose(1, 0, 2).reshape(g8.shape[1], -1)

    small = dict(conv_a=full_small(sh_all[0]), rel_bias=full_small(sh_all[1]), conv_ffn=full_small(sh_all[2]),
                 b_gate=W["b_gate"][None], a_log=W["a_log"][None], dt_bias=W["dt_bias"][None],
                 norm_a=W["norm_a"][None], ln1_g=W["ln1_g"][None], ln1_b=W["ln1_b"][None],
                 b_conv_ffn=W["b_conv_ffn"][None], ln2_g=W["ln2_g"][None], ln2_b=W["ln2_b"][None])

    nsh = w_ada.shape[2]
    b_sh = lax.dynamic_slice(W["b_ada"][None], (0, me * nsh), (1, nsh))
    mod_sh = ada_fwd(c_all, W["w_ada"], b_sh)
    (mod_rows,) = all_to_all([mod_sh[:, None, :]], "scatter_mod")
    mod6 = mod_rows.reshape(6, D)

    after_small = (g_in[0, 0, 0].astype(F32) * 0.0 + mod6[0, 0] * 0.0).astype(BF16)
    late, late_tok = exchange_start([W[n].astype(BF16) + after_small for n in big[1:]], "gather_late_start", False)

    def late_weights(after):
        g_up, g_down, g_a, g_b, g_o = exchange_wait(late, after, "gather_late_wait")
        return dict(w_up=g_up.transpose(1, 0, 2).reshape(D, -1), w_down=g_down.reshape(DFF, D),
                    w_a=g_a.reshape(D, D), w_b=g_b.reshape(D, D), w_o=g_o.reshape(D, D))

    mod6 = mod6 + late_tok
    mod = tuple(mod6[i:i + 1] for i in range(6))

    pending = {}

    def on_grads(group, gd):
        if group == "small":
            pending["small"] = all_to_all([_col_shards(gd[n], W[n].shape[1]) for n in SH_NAMES],
                                          "scatter_small_grads")
            return pending["small"][0][0, 0, 0] * 0.0
        if group == "ffn":
            slabs = [_col_shards(gd["w_up"], w_up.shape[2]), gd["w_down"].reshape(NDEV, -1, D)]
        elif group == "mix":
            slabs = [gd[n].reshape(NDEV, -1, D) for n in ("w_a", "w_b", "w_o")]
        else:
            nin = w_in.shape[2]
            slabs = [jnp.stack([_orig_cols_from_pieces(gd, j * nin, (j + 1) * nin) for j in range(NDEV)], axis=0)]
        pending[group], tok = exchange_start([s.astype(BF16) for s in slabs], "scatter_" + group + "_start", True)
        return tok

    loss, grad_x, dmod, g = local_step(x[0], loss_target[0], mod, wts, small, late_weights, on_grads)

    rep_grads = {n: g[n] for n in REP_NAMES if n != "b_ada"}
    rep_grads["b_ada"] = jnp.concatenate(dmod, axis=1)
    gathered = all_gather([rep_grads[n] for n in REP_NAMES] + [jnp.broadcast_to(loss, (1, LANE))],
                          "gather_small_grads")
    rep_all = dict(zip(REP_NAMES, gathered))
    sh_recv = [p[:, None] for p in pending["small"]]
    small_names = REP_NAMES + SH_NAMES
    sg, sd, sm, sv, loss_row = adamw_small([rep_all[n] for n in REP_NAMES] + sh_recv,
                                           [W3[n] for n in small_names], [M3[n] for n in small_names],
                                           [V3[n] for n in small_names], gathered[-1], "adamw_small")
    loss_total = loss_row[0, 0]

    dmod_all = rep_all["b_ada"][:, 0]
    dmod_sh = lax.dynamic_slice(dmod_all, (0, me * nsh), (NDEV, nsh))
    g_w_ada = ada_wgrad(c_all.T, dmod_sh)

    p_up, p_down = exchange_wait(pending["ffn"], grad_x, "scatter_ffn_wait")
    p_a, p_b, p_o = exchange_wait(pending["mix"], grad_x, "scatter_mix_wait")
    (p_in,) = exchange_wait(pending["in"], grad_x, "scatter_in_wait")
    parts = [p_in, p_up, p_down, p_a, p_b, p_o]

    res = {}
    for n, p in zip(big, parts):
        res[n] = adamw(p, W3[n], M3[n], V3[n], "adamw_" + n)
    res["w_ada"] = adamw(g_w_ada[None], W3["w_ada"], M3["w_ada"], V3["w_ada"], "adamw_w_ada")
    for i, n in enumerate(small_names):
        res[n] = (sg[i], sd[i], sm[i], sv[i])

    order = ("w_ada", "b_ada", "w_in", "b_gate", "conv_a", "a_log", "dt_bias", "norm_a", "rel_bias", "w_branch_a",
             "w_branch_b", "w_o", "ln1_g", "ln1_b", "w_up", "conv_ffn", "b_conv_ffn", "w_down", "ln2_g", "ln2_b")
    outs = [loss_total, grad_x[None]]
    for kind in range(4):
        outs += [res[n][kind] for n in order]
    return tuple(outs)
```

```python
import functools
import math

import numpy as np
import jax
import jax.numpy as jnp
from jax import lax
from jax.experimental import pallas as pl
from jax.experimental.pallas import tpu as pltpu

F32 = jnp.float32
BF16 = jnp.bfloat16
HI = lax.Precision.HIGHEST

D = 1024
CH = 64
AH, ADK = 8, 128
BH, BDH = 16, 64
BPREV = 8
BMAXREL = 256
RELSZ = CH + BMAXREL
DFF = 2816
ALPHA = 2.0 ** 0.25
LN_EPS, RMS_EPS, L2_EPS = 1e-5, 1e-6, 1e-6
NEG = -1e30
LR, B1, B2, AEPS, WD, STEP = 1e-3, 0.9, 0.999, 1e-8, 0.01, 10
NDEV = 8
HALO = 8
LANE = 128
TQ = 512
VMEM_LIMIT = 56 * 1024 * 1024

C_QKVA, C_Z, C_QKVB, C_GATE, C_BA, NCAT = 0, 3072, 4096, 7168, 9216, 9728
W_QKVB = 3072
PA_GATE, PA_BA = C_GATE - W_QKVB, C_BA - W_QKVB


def _cparams(n_axes=1, vmem=VMEM_LIMIT):
    return pltpu.CompilerParams(dimension_semantics=("arbitrary",) * n_axes, vmem_limit_bytes=vmem)


def _dg(a, b, ca, cb):
    return lax.dot_general(a.astype(BF16), b.astype(BF16), (((ca,), (cb,)), ((), ())),
                           preferred_element_type=F32)


@jax.custom_vjp
def mm_nn(a, b):
    return _dg(a, b, 1, 0)


@jax.custom_vjp
def mm_nt(a, b):
    return _dg(a, b, 1, 1)


@jax.custom_vjp
def mm_tn(a, b):
    return _dg(a, b, 0, 0)


mm_nn.defvjp(lambda a, b: (mm_nn(a, b), (a, b)),
             lambda r, g: (mm_nt(g, r[1]).astype(r[0].dtype), mm_tn(r[0], g).astype(r[1].dtype)))
mm_nt.defvjp(lambda a, b: (mm_nt(a, b), (a, b)),
             lambda r, g: (mm_nn(g, r[1]).astype(r[0].dtype), mm_tn(g, r[0]).astype(r[1].dtype)))
mm_tn.defvjp(lambda a, b: (mm_tn(a, b), (a, b)),
             lambda r, g: (mm_nt(r[1], g).astype(r[0].dtype), mm_nn(r[0], g).astype(r[1].dtype)))


@jax.custom_vjp
def mm_w(a, w):
    return _dg(a, w, 1, 0)


mm_w.defvjp(lambda a, w: (mm_w(a, w), (a, w)),
            lambda r, g: (mm_nt(g, r[1]).astype(r[0].dtype), jnp.zeros_like(r[1])))


def _mmh(a, b):
    return lax.dot_general(a, b, (((1,), (0,)), ((), ())), precision=HI, preferred_element_type=F32)


def _bdg(a, b, ca, cb):
    return lax.dot_general(a.astype(BF16), b.astype(BF16), (((ca,), (cb,)), ((0,), (0,))),
                           preferred_element_type=F32)


@jax.custom_vjp
def bmm_nn(a, b):
    return _bdg(a, b, 2, 1)


@jax.custom_vjp
def bmm_nt(a, b):
    return _bdg(a, b, 2, 2)


@jax.custom_vjp
def bmm_tn(a, b):
    return _bdg(a, b, 1, 1)


bmm_nn.defvjp(lambda a, b: (bmm_nn(a, b), (a, b)), lambda r, g: (bmm_nt(g, r[1]), bmm_tn(r[0], g)))
bmm_nt.defvjp(lambda a, b: (bmm_nt(a, b), (a, b)), lambda r, g: (bmm_nn(g, r[1]), bmm_tn(g, r[0])))
bmm_tn.defvjp(lambda a, b: (bmm_tn(a, b), (a, b)), lambda r, g: (bmm_nt(r[1], g), bmm_nn(r[0], g)))


def _bdg3(a, b, ca, cb):
    return lax.dot_general(a, b, (((ca,), (cb,)), ((0,), (0,))), precision=lax.Precision.HIGH,
                           preferred_element_type=F32)


NEWTON_STEPS = 2


def _bdgp(a, b, ca, cb):
    return _bdg(a, b, ca, cb)


@jax.custom_vjp
def bmm3_nn(a, b):
    return _bdgp(a, b, 2, 1)


bmm3_nn.defvjp(lambda a, b: (bmm3_nn(a, b), (a, b)),
               lambda r, g: (_bdgp(g, r[1], 2, 2), _bdgp(r[0], g, 1, 1)))


def _sigmoid(x):
    return 0.5 * jnp.tanh(0.5 * x) + 0.5


def _silu(x):
    return x * _sigmoid(x)


def _softplus(x):
    return jnp.maximum(x, 0.0) + jnp.log(1.0 + jnp.exp(-jnp.abs(x)))


def _layernorm(r, g, b):
    mu = jnp.mean(r, axis=-1, keepdims=True)
    xc = r - mu
    var = jnp.mean(xc * xc, axis=-1, keepdims=True)
    return xc * lax.rsqrt(var + LN_EPS) * g + b


def _iota2(shape, dim):
    return lax.broadcasted_iota(jnp.int32, shape, dim)


@jax.custom_vjp
def causal_conv(ext, rows):
    k = len(rows)
    y = None
    for j in range(k):
        s = k - 1 - j
        r = pltpu.roll(ext, s, 0) if s else ext
        t = r[HALO:] * rows[j]
        y = t if y is None else y + t
    return y


def _causal_conv_fwd(ext, rows):
    return causal_conv(ext, rows), (ext, rows)


def _causal_conv_bwd(res, g):
    ext, rows = res
    n = ext.shape[0]
    k = len(rows)
    gext = jnp.concatenate([jnp.zeros((HALO, g.shape[1]), g.dtype), g], axis=0)
    dext = None
    drows = []
    for j in range(k):
        s = k - 1 - j
        up = pltpu.roll(gext, n - s, 0) if s else gext
        t = up * rows[j]
        dext = t if dext is None else dext + t
        r = pltpu.roll(ext, s, 0) if s else ext
        drows.append(jnp.sum(g * r[HALO:], axis=0, keepdims=True))
    return dext, tuple(drows)


causal_conv.defvjp(_causal_conv_fwd, _causal_conv_bwd)


def _chunk_masks(tm):
    i = _iota2((tm, tm), 0)
    j = _iota2((tm, tm), 1)
    same = (i ^ j) < CH
    lower = jnp.where(same & (j <= i), 1.0, 0.0).astype(F32)
    upper = jnp.where(same & (i <= j), 1.0, 0.0).astype(F32)
    return lower, upper


@jax.custom_vjp
def chunk_cumsum(g):
    lower, _ = _chunk_masks(g.shape[0])
    return _mmh(lower, g)


def _chunk_cumsum_bwd(_, ct):
    _, upper = _chunk_masks(ct.shape[0])
    return (_mmh(upper, ct),)


chunk_cumsum.defvjp(lambda g: (chunk_cumsum(g), None), _chunk_cumsum_bwd)


@jax.custom_vjp
def inv_unit_lower(a):
    n = a.shape[-1]
    eye = jnp.where(_iota2((1, n, n), 1) == _iota2((1, n, n), 2), 1.0, 0.0).astype(F32)
    x = eye - a
    p = _bdg3(a, a, 2, 1)
    steps = int(math.log2(n)) - 1
    for s in range(steps):
        x = x + _bdg3(x, p, 2, 1)
        if s + 1 < steps:
            p = _bdg3(p, p, 2, 1)
    for _ in range(NEWTON_STEPS):
        r = (eye - x) - _bdg3(a, x, 2, 1)
        x = x + _bdg3(x, r, 2, 1)
    return x


def _inv_fwd(a):
    t = inv_unit_lower(a)
    return t, t


def _inv_bwd(t, g):
    return (-_bdgp(_bdgp(t, g, 1, 1), t, 2, 2),)


inv_unit_lower.defvjp(_inv_fwd, _inv_bwd)


@jax.custom_vjp
def inv_known(a, t):
    return t


inv_known.defvjp(lambda a, t: (t, t), lambda t, g: (_inv_bwd(t, g)[0], jnp.zeros_like(t)))


def prep_head_fn(ext, rows, scale):
    s = _silu(causal_conv(ext, rows))
    if scale is None:
        return s
    return s * (lax.rsqrt(jnp.sum(s * s, axis=-1, keepdims=True) + L2_EPS) * scale)


def prep_gate_fn(bb, aa, a_log, dtb):
    g = -jnp.exp(a_log) * _softplus(aa + dtb)
    return chunk_cumsum(g), _sigmoid(bb)


PREP_SCALES = (ADK ** -0.5, 1.0, None)


def _head_cols(a, heads):
    lane = _iota2((1, LANE), 1)
    return jnp.concatenate([jnp.sum(jnp.where(lane == h, a, 0.0), axis=1, keepdims=True)[None]
                            for h in heads], axis=0)


def _head_rows(a, heads):
    at = a.T[:AH]
    sub = _iota2((AH, 1), 0)
    return jnp.concatenate([jnp.sum(jnp.where(sub == h, at, 0.0), axis=0, keepdims=True)[None]
                            for h in heads], axis=0)


def c1_heads(q, k, v, gcs, beta, tinv_saved=None):
    heads = range(AH)
    gcol = jnp.concatenate([_head_cols(g, heads) for g in gcs], axis=0)
    grow = jnp.concatenate([_head_rows(g, heads) for g in gcs], axis=0)
    bcol = jnp.concatenate([_head_cols(b, heads) for b in beta], axis=0)
    i = _iota2((1, CH, CH), 1)
    j = _iota2((1, CH, CH), 2)
    causal = j <= i
    strict = j < i
    diff = gcol - grow
    decay = jnp.where(causal, jnp.exp(jnp.where(causal, diff, 0.0)), 0.0)
    kb = k * bcol
    vb = v * bcol
    a_low = jnp.where(strict, bmm_nt(kb, k) * decay, 0.0)
    tinv = inv_unit_lower(a_low) if tinv_saved is None else inv_known(a_low, tinv_saved)
    egc = jnp.exp(gcol)
    u = bmm3_nn(tinv, vb)
    w = bmm3_nn(tinv, kb * egc)
    qk = jnp.where(causal, bmm_nt(q, k) * decay, 0.0)
    glast = jnp.sum(jnp.where(_iota2((1, CH, 1), 1) == CH - 1, gcol, 0.0), axis=1, keepdims=True)
    qg = q * egc
    kd = k * jnp.exp(glast - gcol)
    eg = jnp.exp(glast) * jnp.ones((1, 1, ADK), F32)
    return u, w, qk, qg, kd, eg, tinv


def c2_heads(s, u, w, qk, qg, kd, eg, z, nw):
    vn = u - bmm_nn(w, s)
    o = bmm_nn(qg, s) + bmm_nn(qk, vn)
    s2 = s * eg + bmm_tn(kd, vn)
    ms = jnp.mean(o * o, axis=-1, keepdims=True)
    og = o * lax.rsqrt(ms + RMS_EPS) * nw * _silu(z)
    return og, s2


ATT_SCALE = BDH ** -0.5


def _head_mask(hh):
    lane = _iota2((1, 2 * BDH), 1)
    return jnp.where((lane >= hh * BDH) & (lane < (hh + 1) * BDH), 1.0, 0.0).astype(F32)


def attn_sub_fwd(q, k, v, bias2, r, firstf):
    col = _iota2((1, KWIN), 1) + r * SUBQ
    nokey = jnp.where(col < TQ, firstf, 0.0) * NEG
    out, probs = None, []
    for hh in range(2):
        hm = _head_mask(hh)
        s = mm_nt(q * (hm * ATT_SCALE), k) + (assemble_bias(bias2[hh], r) + nokey)
        p = jnp.exp(s - jnp.max(s, axis=-1, keepdims=True))
        inv = 1.0 / jnp.sum(p, axis=-1, keepdims=True)
        o = mm_nn(p, v) * (inv * hm)
        out = o if out is None else out + o
        probs.append(p * inv)
    return out, probs


def attn_sub_bwd(q, k, v, o, do, probs, r):
    dq, dk, dv, dss = None, None, None, []
    for hh in range(2):
        hm = _head_mask(hh)
        p = probs[hh]
        doh = do * hm
        ds = p * (mm_nt(doh, v) - jnp.sum(doh * o, axis=-1, keepdims=True))
        dqh = mm_nn(ds, k) * (hm * ATT_SCALE)
        dkh = mm_tn(ds, q * (hm * ATT_SCALE))
        dvh = mm_tn(p, doh)
        dq = dqh if dq is None else dq + dqh
        dk = dkh if dk is None else dk + dkh
        dv = dvh if dv is None else dv + dvh
        dss.append(ds)
    return dq, dk, dv, dss


def merge_fn(x, oa, ob, gra, grb, p_pa, p_pb, p_mix, bga, bgb, gate_t, g1, b1, scale_f, shift_f,
             wa, wb, wo):
    ga = _sigmoid(gra + bga)
    gb = _sigmoid(grb + bgb)
    pa = mm_w(oa, wa) + p_pa
    pb = mm_w(ob, wb) + p_pb
    merged = ga * pa + gb * pb
    mix = mm_w(merged, wo) + p_mix
    y1 = _layernorm(ALPHA * x + gate_t * mix, g1, b1)
    return y1, merged


def ffn_act_fn(extg, extv, rows_g, rows_v, bg, bv):
    return _silu(causal_conv(extg, rows_g) + bg) * (causal_conv(extv, rows_v) + bv)


def head_fn(a, y1, p_ffn, gate_f, g2, b2, tgt, wd):
    ffn = mm_w(a, wd) + p_ffn
    y2 = _layernorm(ALPHA * y1 + gate_f * ffn, g2, b2)
    err = y2 - tgt
    return 0.5 * jnp.sum(jnp.mean(err * err, axis=-1, keepdims=True))


def _rows(tm, width, colblk=0, order=None):
    if order is None:
        return pl.BlockSpec((tm, width), lambda i: (i, colblk))
    return pl.BlockSpec((tm, width), lambda i: (order(i), colblk))


def _const(shape):
    nd = len(shape)
    return pl.BlockSpec(shape, lambda *_: (0,) * nd)


def _pick(n, cands):
    for c in cands:
        if n % c == 0:
            return c
    raise ValueError(f"no tile for {n}")


def _tile(n, cap):
    best = None
    for c in range(LANE, min(n, cap) + 1, LANE):
        if n % c == 0:
            best = c
    if best is None:
        raise ValueError(f"no tile for {n}")
    return best


def _onehot_rows(k, j):
    return jnp.where(_iota2((k, 1), 0) == j, 1.0, 0.0).astype(F32)


def _stack_rows(drows):
    k = len(drows)
    out = None
    for j in range(k):
        tj = _onehot_rows(k, j) * drows[j]
        out = tj if out is None else out + tj
    return out


def matmul(a, w, out_dtype, name, ta=False, tb=False):
    kdim, m = a.shape if ta else a.shape[::-1]
    n = w.shape[0] if tb else w.shape[1]
    tm = _tile(m, 2048 if kdim <= 1024 else 1024)
    tn = _tile(n, 1024)
    tk = _tile(kdim, 2560)
    nk = kdim // tk
    a_spec = (pl.BlockSpec((tk, tm), lambda i, j, k: (k, i)) if ta
              else pl.BlockSpec((tm, tk), lambda i, j, k: (i, k)))
    w_spec = (pl.BlockSpec((tn, tk), lambda i, j, k: (j, k)) if tb
              else pl.BlockSpec((tk, tn), lambda i, j, k: (k, j)))

    def body(a_ref, w_ref, o_ref, *scratch):
        p = _dg(a_ref[...], w_ref[...], 0 if ta else 1, 1 if tb else 0)
        if nk == 1:
            o_ref[...] = p.astype(out_dtype)
            return
        acc = scratch[0]
        k = pl.program_id(2)

        @pl.when(k == 0)
        def _():
            acc[...] = p

        @pl.when(k > 0)
        def _():
            acc[...] += p

        @pl.when(k == nk - 1)
        def _():
            o_ref[...] = acc[...].astype(out_dtype)

    return pl.pallas_call(
        body, name=name,
        grid=(m // tm, n // tn, nk),
        in_specs=[a_spec, w_spec],
        out_specs=pl.BlockSpec((tm, tn), lambda i, j, k: (i, j)),
        out_shape=jax.ShapeDtypeStruct((m, n), out_dtype),
        scratch_shapes=[] if nk == 1 else [pltpu.VMEM((tm, tn), F32)],
        compiler_params=_cparams(3),
    )(a, w)


def dgrad_modulated(a, w, xin, dres, scale, name):
    m, kdim = a.shape
    n = w.shape[0]
    tm = _tile(m, 1024)
    tk = _tile(kdim, 2560)
    nk = kdim // tk
    assert nk > 1

    def body(a_ref, w_ref, x_ref, r_ref, sc_ref, o_ref, dsc_ref, dsh_ref, acc):
        i = pl.program_id(0)
        k = pl.program_id(1)
        p = _dg(a_ref[...], w_ref[...], 1, 1)

        @pl.when(k == 0)
        def _():
            acc[...] = p

        @pl.when(k > 0)
        def _():
            acc[...] += p

        @pl.when((i == 0) & (k == 0))
        def _():
            dsc_ref[...] = jnp.zeros_like(dsc_ref)
            dsh_ref[...] = jnp.zeros_like(dsh_ref)

        @pl.when(k == nk - 1)
        def _():
            dh = acc[...]
            o_ref[...] = r_ref[...] + dh * (1.0 + sc_ref[...])
            dsc_ref[...] += jnp.sum(dh * x_ref[...], axis=0, keepdims=True)
            dsh_ref[...] += jnp.sum(dh, axis=0, keepdims=True)

    row = pl.BlockSpec((tm, n), lambda i, k: (i, 0))
    vec = pl.BlockSpec((1, n), lambda i, k: (0, 0))
    return pl.pallas_call(
        body, name=name, grid=(m // tm, nk),
        in_specs=[pl.BlockSpec((tm, tk), lambda i, k: (i, k)), pl.BlockSpec((n, tk), lambda i, k: (0, k)),
                  row, row, vec],
        out_specs=[row, vec, vec],
        out_shape=[jax.ShapeDtypeStruct((m, n), F32), jax.ShapeDtypeStruct((1, n), F32),
                   jax.ShapeDtypeStruct((1, n), F32)],
        scratch_shapes=[pltpu.VMEM((tm, n), F32)],
        compiler_params=_cparams(2),
    )(a, w, xin, dres, scale)


def dgrad_pieces(pieces, tail, w, name):
    m = pieces[0][0].shape[0]
    n, ktot = w.shape
    tk = 1024
    tm = _tile(m, 1024)
    wt = tail.shape[1]
    ranges, k0 = [], 0
    for arr, off in pieces:
        assert off == k0 * tk and arr.shape[1] % tk == 0
        ranges.append((k0, k0 + arr.shape[1] // tk))
        k0 = ranges[-1][1]
    nk = k0
    npc = len(pieces)

    def body(*refs):
        a_refs, t_ref, w_ref, wt_ref, o_ref, acc = refs[:npc], refs[npc], refs[npc + 1], refs[npc + 2], refs[npc + 3], refs[npc + 4]
        k = pl.program_id(1)

        @pl.when(k == 0)
        def _():
            acc[...] = _dg(t_ref[...], wt_ref[...], 1, 1)

        for a_ref, (lo, hi) in zip(a_refs, ranges):
            @pl.when((k >= lo) & (k < hi))
            def _(a_ref=a_ref):
                acc[...] += _dg(a_ref[...], w_ref[...], 1, 1)

        @pl.when(k == nk - 1)
        def _():
            o_ref[...] = acc[...]

    def piece_spec(lo, hi):
        return pl.BlockSpec((tm, tk), lambda i, k: (i, jnp.clip(k - lo, 0, hi - lo - 1)))

    return pl.pallas_call(
        body, name=name, grid=(m // tm, nk),
        in_specs=[piece_spec(lo, hi) for lo, hi in ranges] + [
            pl.BlockSpec((tm, wt), lambda i, k: (i, 0)),
            pl.BlockSpec((n, tk), lambda i, k: (0, k)),
            pl.BlockSpec((n, wt), lambda i, k: (0, (ktot - wt) // wt))],
        out_specs=pl.BlockSpec((tm, n), lambda i, k: (i, 0)),
        out_shape=jax.ShapeDtypeStruct((m, n), F32),
        scratch_shapes=[pltpu.VMEM((tm, n), F32)],
        compiler_params=_cparams(2),
    )(*[a for a, _ in pieces], tail, w, w)


def modulate(x, scale, shift, name):
    t, d = x.shape
    tm = _pick(t, (512, 256, 128))

    def body(x_ref, sc_ref, sh_ref, o_ref):
        o_ref[...] = (x_ref[...] * (1.0 + sc_ref[...]) + sh_ref[...]).astype(BF16)

    return pl.pallas_call(
        body, name=name, grid=(t // tm,),
        in_specs=[_rows(tm, d), _const((1, d)), _const((1, d))],
        out_specs=_rows(tm, d),
        out_shape=jax.ShapeDtypeStruct((t, d), BF16),
        compiler_params=_cparams(),
    )(x, scale, shift)


def modulate_bwd(dh, xin, dres, scale, name):
    t, d = dh.shape
    tm = _pick(t, (512, 256, 128))

    def body(dh_ref, x_ref, dr_ref, sc_ref, o_ref, dsc_ref, dsh_ref):
        i = pl.program_id(0)
        dh_v = dh_ref[...]
        o_ref[...] = dr_ref[...] + dh_v * (1.0 + sc_ref[...])

        @pl.when(i == 0)
        def _():
            dsc_ref[...] = jnp.zeros_like(dsc_ref)
            dsh_ref[...] = jnp.zeros_like(dsh_ref)

        dsc_ref[...] += jnp.sum(dh_v * x_ref[...], axis=0, keepdims=True)
        dsh_ref[...] += jnp.sum(dh_v, axis=0, keepdims=True)

    return pl.pallas_call(
        body, name=name, grid=(t // tm,),
        in_specs=[_rows(tm, d), _rows(tm, d), _rows(tm, d), _const((1, d))],
        out_specs=[_rows(tm, d), _const((1, d)), _const((1, d))],
        out_shape=[jax.ShapeDtypeStruct((t, d), F32), jax.ShapeDtypeStruct((1, d), F32),
                   jax.ShapeDtypeStruct((1, d), F32)],
        compiler_params=_cparams(),
    )(dh, xin, dres, scale)


PREP_TM = 128


def _halo_specs(tm, width, colblk, order):
    per = tm // HALO
    return [pl.BlockSpec((HALO, width), lambda i: (jnp.maximum(order(i) * per - 1, 0), colblk)),
            pl.BlockSpec((tm, width), lambda i: (order(i), colblk))]


def prep_fwd(proj, conv_a, a_log, dtb):
    t = proj.shape[0]
    tm = PREP_TM
    nt = t // tm
    wq = 3 * D

    def body(prev_ref, cur_ref, bb_ref, aa_ref, cw_ref, al_ref, dt_ref, q_ref, k_ref, v_ref, g_ref, b_ref):
        i = pl.program_id(0)
        flag = jnp.where(i > 0, 1.0, 0.0)
        for part, o_ref in enumerate((q_ref, k_ref, v_ref)):
            for h in range(AH):
                sl = slice(part * D + h * ADK, part * D + (h + 1) * ADK)
                ext = jnp.concatenate([prev_ref[:, sl] * flag, cur_ref[:, sl]], axis=0)
                rows = tuple(cw_ref[j:j + 1, sl] for j in range(4))
                o_ref[h] = prep_head_fn(ext, rows, PREP_SCALES[part])
        gcs, beta = prep_gate_fn(bb_ref[...], aa_ref[...], al_ref[...], dt_ref[...])
        g_ref[...] = gcs
        b_ref[...] = beta

    ident = lambda i: i
    hm = pl.BlockSpec((AH, tm, ADK), lambda i: (0, i, 0))
    return pl.pallas_call(
        body, name="prep_fwd", grid=(nt,),
        in_specs=_halo_specs(tm, wq, 0, ident) + [
            _rows(tm, 128, PA_BA // 128), _rows(tm, 128, PA_BA // 128 + 1),
            _const((4, wq)), _const((1, 128)), _const((1, 128))],
        out_specs=[hm, hm, hm, _rows(tm, 128), _rows(tm, 128)],
        out_shape=[jax.ShapeDtypeStruct((AH, t, ADK), F32)] * 3 + [jax.ShapeDtypeStruct((t, 128), F32)] * 2,
        compiler_params=_cparams(),
    )(proj, proj, proj, proj, conv_a, a_log, dtb)


def prep_bwd(proj, conv_a, a_log, dtb, dq, dk, dv, dgcs, dbeta):
    t = proj.shape[0]
    tm = PREP_TM
    nt = t // tm
    wq = 3 * D
    rev = lambda i: nt - 1 - i

    def body(prev_ref, cur_ref, bb_ref, aa_ref, cw_ref, al_ref, dt_ref,
             dq_ref, dk_ref, dv_ref, dg_ref, db_ref,
             dpre_ref, dbb_ref, daa_ref, dcw_ref, dal_ref, ddt_ref, carry):
        i = pl.program_id(0)
        flag = jnp.where(i < nt - 1, 1.0, 0.0)

        @pl.when(i == 0)
        def _():
            carry[...] = jnp.zeros_like(carry)
            dcw_ref[...] = jnp.zeros_like(dcw_ref)
            dal_ref[...] = jnp.zeros_like(dal_ref)
            ddt_ref[...] = jnp.zeros_like(ddt_ref)

        for part, d_ref in enumerate((dq_ref, dk_ref, dv_ref)):
            for h in range(AH):
                sl = slice(part * D + h * ADK, part * D + (h + 1) * ADK)
                ext = jnp.concatenate([prev_ref[:, sl] * flag, cur_ref[:, sl]], axis=0)
                rows = tuple(cw_ref[j:j + 1, sl] for j in range(4))
                _, vjp = jax.vjp(lambda e, r: prep_head_fn(e, r, PREP_SCALES[part]), ext, rows)
                dext, drows = vjp(d_ref[h])
                dcur = dext[HALO:]
                dpre_ref[:, sl] = jnp.concatenate([dcur[:tm - HALO], dcur[tm - HALO:] + carry[:, sl]],
                                                  axis=0).astype(BF16)
                carry[:, sl] = dext[:HALO]
                dcw_ref[:, sl] += _stack_rows(drows)
        _, vjp = jax.vjp(prep_gate_fn, bb_ref[...], aa_ref[...], al_ref[...], dt_ref[...])
        dbb, daa, dal, ddt = vjp((dg_ref[...], db_ref[...]))
        dbb_ref[...] = dbb.astype(BF16)
        daa_ref[...] = daa.astype(BF16)
        dal_ref[...] += dal
        ddt_ref[...] += ddt

    hm = pl.BlockSpec((AH, tm, ADK), lambda i: (0, rev(i), 0))
    return pl.pallas_call(
        body, name="prep_bwd", grid=(nt,),
        in_specs=_halo_specs(tm, wq, 0, rev) + [
            _rows(tm, 128, PA_BA // 128, rev), _rows(tm, 128, PA_BA // 128 + 1, rev),
            _const((4, wq)), _const((1, 128)), _const((1, 128)),
            hm, hm, hm, _rows(tm, 128, 0, rev), _rows(tm, 128, 0, rev)],
        out_specs=[_rows(tm, wq, 0, rev), _rows(tm, 128, 0, rev), _rows(tm, 128, 0, rev),
                   _const((4, wq)), _const((1, 128)), _const((1, 128))],
        out_shape=[jax.ShapeDtypeStruct((t, wq), BF16), jax.ShapeDtypeStruct((t, 128), BF16),
                   jax.ShapeDtypeStruct((t, 128), BF16), jax.ShapeDtypeStruct((4, wq), F32),
                   jax.ShapeDtypeStruct((1, 128), F32), jax.ShapeDtypeStruct((1, 128), F32)],
        scratch_shapes=[pltpu.VMEM((HALO, wq), F32)],
        compiler_params=_cparams(),
    )(proj, proj, proj, proj, conv_a, a_log, dtb, dq, dk, dv, dgcs, dbeta)


def _c1_specs(order, cps=1):
    hm = pl.BlockSpec((AH, cps * CH, ADK), lambda n: (0, order(n), 0))
    col = pl.BlockSpec((cps * CH, LANE), lambda n: (order(n), 0))
    qk = pl.BlockSpec((cps, AH, CH, CH), lambda n: (order(n), 0, 0, 0))
    eg = pl.BlockSpec((cps, AH, 1, ADK), lambda n: (order(n), 0, 0, 0))
    return hm, col, qk, eg


C1_CPS = 4


def _heads(ref, rows=slice(None)):
    return jnp.stack([ref[rows, h * ADK:(h + 1) * ADK] for h in range(AH)], axis=0)


C2_CPS = 8


def _chunks(ref):
    return jnp.concatenate([ref[:, c * CH:(c + 1) * CH] for c in range(C1_CPS)], axis=0)


def _chunk_rows(ref):
    return [ref[c * CH:(c + 1) * CH] for c in range(C1_CPS)]


def c1_fwd(q, k, v, gcs, beta):
    t = q.shape[1]
    nc = t // CH
    hm, col, qks, egs = _c1_specs(lambda n: n, C1_CPS)

    def body(q_ref, k_ref, v_ref, g_ref, b_ref, u_ref, w_ref, qg_ref, kd_ref, qk_ref, eg_ref, ti_ref):
        u, w, qk, qg, kd, eg, tinv = c1_heads(_chunks(q_ref), _chunks(k_ref), _chunks(v_ref), _chunk_rows(g_ref),
                                              _chunk_rows(b_ref))
        for c in range(C1_CPS):
            rows, hs = slice(c * CH, (c + 1) * CH), slice(c * AH, (c + 1) * AH)
            u_ref[:, rows] = u[hs]
            w_ref[:, rows] = w[hs].astype(BF16)
            qg_ref[:, rows] = qg[hs].astype(BF16)
            kd_ref[:, rows] = kd[hs].astype(BF16)
            qk_ref[c] = qk[hs].astype(BF16)
            eg_ref[c] = eg[hs]
            ti_ref[c] = tinv[hs]

    return pl.pallas_call(
        body, name="c1_fwd", grid=(nc // C1_CPS,),
        in_specs=[hm, hm, hm, col, col],
        out_specs=[hm, hm, hm, hm, qks, egs, qks],
        out_shape=[jax.ShapeDtypeStruct((AH, t, ADK), F32)] + [jax.ShapeDtypeStruct((AH, t, ADK), BF16)] * 3 + [
            jax.ShapeDtypeStruct((nc, AH, CH, CH), BF16), jax.ShapeDtypeStruct((nc, AH, 1, ADK), F32),
            jax.ShapeDtypeStruct((nc, AH, CH, CH), F32)],
        compiler_params=_cparams(),
    )(q, k, v, gcs, beta)


def c1_bwd(q, k, v, gcs, beta, tinv, du, dw, dqg, dkd, dqk, deg):
    t = q.shape[1]
    nc = t // CH
    hm, col, qks, egs = _c1_specs(lambda n: n, C1_CPS)

    def lead(ref):
        return jnp.concatenate([ref[c] for c in range(C1_CPS)], axis=0)

    def body(q_ref, k_ref, v_ref, g_ref, b_ref, ti_ref, du_ref, dw_ref, dqg_ref, dkd_ref, dqk_ref, deg_ref,
             dq_ref, dk_ref, dv_ref, dg_ref, db_ref):
        tinv = lead(ti_ref)
        _, vjp = jax.vjp(lambda q_, k_, v_, g_, b_: c1_heads(q_, k_, v_, g_, b_, tinv),
                         _chunks(q_ref), _chunks(k_ref), _chunks(v_ref), _chunk_rows(g_ref), _chunk_rows(b_ref))
        dq, dk, dv, dg, db = vjp((_chunks(du_ref).astype(F32), _chunks(dw_ref).astype(F32), lead(dqk_ref),
                                  _chunks(dqg_ref), _chunks(dkd_ref), lead(deg_ref),
                                  jnp.zeros((C1_CPS * AH, CH, CH), F32)))
        for c in range(C1_CPS):
            rows, hs = slice(c * CH, (c + 1) * CH), slice(c * AH, (c + 1) * AH)
            dq_ref[:, rows] = dq[hs]
            dk_ref[:, rows] = dk[hs]
            dv_ref[:, rows] = dv[hs]
            dg_ref[rows] = dg[c]
            db_ref[rows] = db[c]

    return pl.pallas_call(
        body, name="c1_bwd", grid=(nc // C1_CPS,),
        in_specs=[hm, hm, hm, col, col, qks, hm, hm, hm, hm, qks, egs],
        out_specs=[hm, hm, hm, col, col],
        out_shape=[jax.ShapeDtypeStruct((AH, t, ADK), F32)] * 3 + [jax.ShapeDtypeStruct((t, LANE), F32)] * 2,
        compiler_params=_cparams(),
    )(q, k, v, gcs, beta, tinv, du, dw, dqg, dkd, dqk, deg)


def c2_fwd(u, w, qg, kd, qk, eg, proj, norm_a):
    t = u.shape[1]
    nc = t // CH
    cps = C2_CPS
    hm, _, qks, egs = _c1_specs(lambda n: n, cps)
    tok = pl.BlockSpec((cps * CH, D), lambda n: (n, 0))
    zspec = pl.BlockSpec((cps * CH, D), lambda n: (n, C_Z // D))
    sspec = pl.BlockSpec((cps, AH, ADK, ADK), lambda n: (n, 0, 0, 0))

    def body(u_ref, w_ref, qg_ref, kd_ref, qk_ref, eg_ref, z_ref, nw_ref, o_ref, sall_ref, st):
        n = pl.program_id(0)

        @pl.when(n == 0)
        def _():
            st[...] = jnp.zeros_like(st)

        for c in range(cps):
            rows = slice(c * CH, (c + 1) * CH)
            s = st[...]
            sall_ref[c] = s
            og, s2 = c2_heads(s, u_ref[:, rows], w_ref[:, rows], qk_ref[c], qg_ref[:, rows], kd_ref[:, rows],
                              eg_ref[c], _heads(z_ref, rows), nw_ref[...])
            st[...] = s2
            for h in range(AH):
                o_ref[rows, h * ADK:(h + 1) * ADK] = og[h].astype(BF16)

    return pl.pallas_call(
        body, name="c2_fwd", grid=(nc // cps,),
        in_specs=[hm, hm, hm, hm, qks, egs, zspec, _const((1, ADK))],
        out_specs=[tok, sspec],
        out_shape=[jax.ShapeDtypeStruct((t, D), BF16), jax.ShapeDtypeStruct((nc, AH, ADK, ADK), F32)],
        scratch_shapes=[pltpu.VMEM((AH, ADK, ADK), F32)],
        compiler_params=_cparams(),
    )(u, w, qg, kd, qk, eg, proj, norm_a)


def c2_bwd(u, w, qg, kd, qk, eg, proj, norm_a, sall, do):
    t = u.shape[1]
    nc = t // CH
    cps = C2_CPS
    rev = lambda n: nc // cps - 1 - n
    hm, _, qks, egs = _c1_specs(rev, cps)
    tok = pl.BlockSpec((cps * CH, D), lambda n: (rev(n), 0))
    zspec = pl.BlockSpec((cps * CH, D), lambda n: (rev(n), C_Z // D))
    sspec = pl.BlockSpec((cps, AH, ADK, ADK), lambda n: (rev(n), 0, 0, 0))

    def body(u_ref, w_ref, qg_ref, kd_ref, qk_ref, eg_ref, z_ref, nw_ref, sall_ref, do_ref,
             du_ref, dw_ref, dqg_ref, dkd_ref, dqk_ref, deg_ref, dz_ref, dnw_ref, dst):
        n = pl.program_id(0)

        @pl.when(n == 0)
        def _():
            dst[...] = jnp.zeros_like(dst)
            dnw_ref[...] = jnp.zeros_like(dnw_ref)

        for c in reversed(range(cps)):
            rows = slice(c * CH, (c + 1) * CH)
            _, vjp = jax.vjp(c2_heads, sall_ref[c], u_ref[:, rows], w_ref[:, rows].astype(F32),
                             qk_ref[c].astype(F32), qg_ref[:, rows].astype(F32), kd_ref[:, rows].astype(F32),
                             eg_ref[c], _heads(z_ref, rows), nw_ref[...])
            ds, du, dw, dqk, dqg, dkd, deg, dz, dn = vjp((_heads(do_ref, rows), dst[...]))
            dst[...] = ds
            du_ref[:, rows] = du.astype(BF16)
            dw_ref[:, rows] = dw.astype(BF16)
            dqg_ref[:, rows] = dqg
            dkd_ref[:, rows] = dkd
            dqk_ref[c] = dqk
            deg_ref[c] = deg
            for h in range(AH):
                dz_ref[rows, h * ADK:(h + 1) * ADK] = dz[h].astype(BF16)
            dnw_ref[...] += dn

    return pl.pallas_call(
        body, name="c2_bwd", grid=(nc // cps,),
        in_specs=[hm, hm, hm, hm, qks, egs, zspec, _const((1, ADK)), sspec, tok],
        out_specs=[hm, hm, hm, hm, qks, egs, tok, _const((1, ADK))],
        out_shape=[jax.ShapeDtypeStruct((AH, t, ADK), BF16)] * 2 + [jax.ShapeDtypeStruct((AH, t, ADK), F32)] * 2 + [
            jax.ShapeDtypeStruct((nc, AH, CH, CH), F32), jax.ShapeDtypeStruct((nc, AH, 1, ADK), F32),
            jax.ShapeDtypeStruct((t, D), BF16), jax.ShapeDtypeStruct((1, ADK), F32)],
        scratch_shapes=[pltpu.VMEM((AH, ADK, ADK), F32)],
        compiler_params=_cparams(),
    )(u, w, qg, kd, qk, eg, proj, norm_a, sall, do)


NQB = TQ // CH
NKB = 2 * TQ // CH
NDIST = BPREV + 1
KLO = -(NQB - 2)
NPAIR = NKB - 1 - KLO + 1


def bias_table(rel_bias):
    nh = rel_bias.shape[0]
    relx = jnp.concatenate([rel_bias, jnp.broadcast_to(rel_bias[:, -1:], (nh, CH * BPREV + 2 * CH - 1 - RELSZ))],
                           axis=1)
    t = jnp.stack([relx[:, CH * k:CH * k + 2 * CH - 1] for k in range(NDIST)], axis=1)
    trev = t[:, :, ::-1]
    g2 = jnp.concatenate([trev[:, :, CH - 1:], jnp.zeros((nh, NDIST, 1), F32), trev[:, :, :CH - 1]], axis=2)
    flat = jnp.tile(g2, (1, 1, CH + 1))[:, :, :CH * (2 * CH - 1)]
    blk = flat.reshape(nh, NDIST, CH, 2 * CH - 1)[..., :CH]
    neg = jnp.full((nh, NQB - 1, CH, CH), NEG, F32)
    asc = jnp.concatenate([neg, blk, neg], axis=1)
    return jnp.concatenate([asc[:, 1:], asc[:, :-1]], axis=-1)


SUBQ = 4 * CH
NSUB = TQ // SUBQ
KWIN = SUBQ + BPREV * CH


def assemble_bias(tab, r):
    b0 = r * SUBQ // (2 * CH)
    rows = [jnp.concatenate([tab[NQB + a - 2 * b - KLO] for b in range(b0, b0 + KWIN // (2 * CH))], axis=1)
            for a in range(r * SUBQ // CH, (r + 1) * SUBQ // CH)]
    return jnp.concatenate(rows, axis=0)


def bias_table_bwd_layout(dtab):
    nh = dtab.shape[0]
    dasc = (jnp.pad(dtab[..., :CH], ((0, 0), (1, 0), (0, 0), (0, 0)))
            + jnp.pad(dtab[..., CH:], ((0, 0), (0, 1), (0, 0), (0, 0))))
    dblk = dasc[:, NQB - 1:NQB - 1 + NDIST]
    dr = jnp.pad(dblk, ((0, 0), (0, 0), (0, 0), (0, CH - 1)))
    flat = jnp.pad(dr.reshape(nh, NDIST, CH * (2 * CH - 1)), ((0, 0), (0, 0), (0, 3 * CH)))
    return flat.reshape(nh, NDIST, CH + 1, 2 * CH).transpose(0, 2, 1, 3).reshape(nh, CH + 1, NDIST * 2 * CH)


def _fold_matrix_np():
    f = np.zeros((NDIST * 2 * CH, 384), np.float32)
    for k in range(NDIST):
        s = k
        for xx in range(2 * CH):
            if xx == CH:
                continue
            m = CH - 1 - xx if xx < CH else 3 * CH - 1 - xx
            f[s * 2 * CH + xx, min(CH * k + m, RELSZ - 1)] = 1.0
    return f


def relbias_reduce(dlay):
    nh, rows, cols = dlay.shape
    rpad = (-rows) % 8
    dlay = jnp.pad(dlay, ((0, 0), (0, rpad), (0, 0)))
    fold = jnp.asarray(_fold_matrix_np())

    def body(d_ref, f_ref, o_ref):
        cs = jnp.sum(d_ref[0], axis=0, keepdims=True)
        o_ref[0] = _mmh(jnp.broadcast_to(cs, (8, cols)), f_ref[...])

    out = pl.pallas_call(
        body, name="relbias_reduce", grid=(nh,),
        in_specs=[pl.BlockSpec((1, rows + rpad, cols), lambda h: (h, 0, 0)), _const((cols, 384))],
        out_specs=pl.BlockSpec((1, 8, 384), lambda h: (h, 0, 0)),
        out_shape=jax.ShapeDtypeStruct((nh, 8, 384), F32),
        compiler_params=_cparams(),
    )(dlay, fold)
    return out[:, 0, :RELSZ]


ATT_PP = 4
ATT_W = ATT_PP * 2 * BDH


def attn_fwd(proj, bias):
    t = proj.shape[0]
    nt = t // TQ
    cb = 0

    def body(q_ref, kp_ref, kc_ref, vp_ref, vc_ref, b_ref, o_ref, p_ref):
        i = pl.program_id(1)
        firstf = jnp.where(i == 0, 1.0, 0.0)
        for pp in range(ATT_PP):
            ln = slice(pp * 2 * BDH, (pp + 1) * 2 * BDH)
            for r in range(NSUB):
                lo, hi = r * SUBQ, r * SUBQ + KWIN - TQ
                kw = jnp.concatenate([kp_ref[lo:, ln], kc_ref[:hi, ln]], axis=0)
                vw = jnp.concatenate([vp_ref[lo:, ln], vc_ref[:hi, ln]], axis=0)
                out, probs = attn_sub_fwd(q_ref[lo:lo + SUBQ, ln].astype(F32), kw, vw, b_ref[2 * pp:2 * pp + 2], r,
                                          firstf)
                o_ref[lo:lo + SUBQ, ln] = out.astype(BF16)
                for hh in range(2):
                    p_ref[2 * pp + hh, lo:lo + SUBQ, :] = probs[hh].astype(BF16)

    def blk(off, prev):
        if prev:
            return pl.BlockSpec((TQ, ATT_W), lambda p, i: (jnp.maximum(i - 1, 0), cb + off + p))
        return pl.BlockSpec((TQ, ATT_W), lambda p, i: (i, cb + off + p))

    nb = D // ATT_W
    return pl.pallas_call(
        body, name="attn_fwd", grid=(nb, nt),
        in_specs=[blk(0, False), blk(nb, True), blk(nb, False), blk(2 * nb, True), blk(2 * nb, False),
                  pl.BlockSpec((2 * ATT_PP, NPAIR, CH, 2 * CH), lambda p, i: (p, 0, 0, 0))],
        out_specs=[pl.BlockSpec((TQ, ATT_W), lambda p, i: (i, p)),
                   pl.BlockSpec((2 * ATT_PP, TQ, KWIN), lambda p, i: (p, i, 0))],
        out_shape=[jax.ShapeDtypeStruct((t, D), BF16), jax.ShapeDtypeStruct((BH, t, KWIN), BF16)],
        compiler_params=_cparams(2),
    )(proj, proj, proj, proj, proj, bias)


def attn_bwd(proj, ob, probs, do):
    t = proj.shape[0]
    nt = t // TQ
    cb = 0

    def body(q_ref, kp_ref, kc_ref, vp_ref, vc_ref, o_ref, p_ref, do_ref,
             dq_ref, dk_ref, dv_ref, db_ref, ck, cv, ak, av):
        i = pl.program_id(1)

        @pl.when(i == 0)
        def _():
            ck[...] = jnp.zeros_like(ck)
            cv[...] = jnp.zeros_like(cv)
            db_ref[...] = jnp.zeros_like(db_ref)

        @pl.when(i < nt)
        def _():
            ak[...] = jnp.zeros_like(ak)
            av[...] = jnp.zeros_like(av)
            for pp in range(ATT_PP):
                ln = slice(pp * 2 * BDH, (pp + 1) * 2 * BDH)
                for r in range(NSUB):
                    lo, hi = r * SUBQ, r * SUBQ + KWIN - TQ
                    rows = slice(lo, lo + SUBQ)
                    kw = jnp.concatenate([kp_ref[lo:, ln], kc_ref[:hi, ln]], axis=0)
                    vw = jnp.concatenate([vp_ref[lo:, ln], vc_ref[:hi, ln]], axis=0)
                    probs_r = [p_ref[2 * pp + hh, rows, :].astype(F32) for hh in range(2)]
                    dq, dkw, dvw, dss = attn_sub_bwd(q_ref[rows, ln].astype(F32), kw, vw,
                                                     o_ref[rows, ln].astype(F32), do_ref[rows, ln], probs_r, r)
                    dq_ref[rows, ln] = dq.astype(BF16)
                    ak[lo:lo + KWIN, ln] += dkw
                    av[lo:lo + KWIN, ln] += dvw
                    for hh in range(2):
                        _, scatter = jax.vjp(lambda tab: assemble_bias(tab, r), jnp.zeros((NPAIR, CH, 2 * CH), F32))
                        db_ref[2 * pp + hh] += scatter(dss[hh])[0]
            dk_ref[...] = (ck[...] + ak[:TQ, :]).astype(BF16)
            dv_ref[...] = (cv[...] + av[:TQ, :]).astype(BF16)
            ck[...] = ak[TQ:, :]
            cv[...] = av[TQ:, :]

        @pl.when(i == nt)
        def _():
            dk_ref[...] = ck[...].astype(BF16)
            dv_ref[...] = cv[...].astype(BF16)

    def blk(off, prev):
        if prev:
            return pl.BlockSpec((TQ, ATT_W), lambda p, i: (jnp.clip(i - 1, 0, nt - 1), cb + off + p))
        return pl.BlockSpec((TQ, ATT_W), lambda p, i: (jnp.minimum(i, nt - 1), cb + off + p))

    nb = D // ATT_W
    own = pl.BlockSpec((TQ, ATT_W), lambda p, i: (jnp.minimum(i, nt - 1), p))
    lag = pl.BlockSpec((TQ, ATT_W), lambda p, i: (jnp.maximum(i - 1, 0), p))
    return pl.pallas_call(
        body, name="attn_bwd", grid=(nb, nt + 1),
        in_specs=[blk(0, False), blk(nb, True), blk(nb, False), blk(2 * nb, True), blk(2 * nb, False), own,
                  pl.BlockSpec((2 * ATT_PP, TQ, KWIN), lambda p, i: (p, jnp.minimum(i, nt - 1), 0)), own],
        out_specs=[own, lag, lag, pl.BlockSpec((2 * ATT_PP, NPAIR, CH, 2 * CH), lambda p, i: (p, 0, 0, 0))],
        out_shape=[jax.ShapeDtypeStruct((t, D), BF16)] * 3 + [jax.ShapeDtypeStruct((BH, NPAIR, CH, 2 * CH), F32)],
        scratch_shapes=[pltpu.VMEM((TQ, ATT_W), F32), pltpu.VMEM((TQ, ATT_W), F32),
                        pltpu.VMEM((2 * TQ, ATT_W), F32), pltpu.VMEM((2 * TQ, ATT_W), F32)],
        compiler_params=_cparams(2),
    )(proj, proj, proj, proj, proj, ob, probs, do)


MERGE_TM = 256


def merge_fwd(x, oa, ob, proj, vecs, wa, wb, wo):
    t = x.shape[0]
    tm = MERGE_TM
    names = ("bga", "bgb", "gate_t", "g1", "b1", "scale_f", "shift_f")

    def body(x_ref, oa_ref, ob_ref, gra_ref, grb_ref, *rest):
        vrefs = rest[:7]
        wa_ref, wb_ref, wo_ref, y_ref, h_ref = rest[7:]
        vv = [r[...] for r in vrefs]
        zero = jnp.zeros((tm, D), F32)
        y1, _ = merge_fn(x_ref[...], oa_ref[...], ob_ref[...], gra_ref[...], grb_ref[...], zero, zero, zero,
                         *vv, wa_ref[...], wb_ref[...], wo_ref[...])
        y_ref[...] = y1
        h_ref[...] = (y1 * (1.0 + vv[5]) + vv[6]).astype(BF16)

    return pl.pallas_call(
        body, name="merge_fwd", grid=(t // tm,),
        in_specs=[_rows(tm, D), _rows(tm, D), _rows(tm, D), _rows(tm, D, PA_GATE // D), _rows(tm, D, PA_GATE // D + 1)]
        + [_const((1, D))] * 7 + [_const((D, D))] * 3,
        out_specs=[_rows(tm, D), _rows(tm, D)],
        out_shape=[jax.ShapeDtypeStruct((t, D), F32), jax.ShapeDtypeStruct((t, D), BF16)],
        compiler_params=_cparams(),
    )(x, oa, ob, proj, proj, *[vecs[n] for n in names], wa, wb, wo)


def merge_bwd(x, oa, ob, proj, vecs, wa, wb, wo, dy1):
    t = x.shape[0]
    tm = MERGE_TM
    names = ("bga", "bgb", "gate_t", "g1", "b1", "scale_f", "shift_f")

    def body(x_ref, oa_ref, ob_ref, gra_ref, grb_ref, *rest):
        vrefs = rest[:7]
        wa_ref, wb_ref, wo_ref, dy_ref = rest[7:11]
        (dx_ref, doa_ref, dob_ref, dga_ref, dgb_ref, mg_ref, dmix_ref, dpa_ref, dpb_ref,
         dbga_ref, dbgb_ref, dgt_ref, dg1_ref, db1_ref) = rest[11:]
        i = pl.program_id(0)
        vv = [r[...] for r in vrefs]
        zero = jnp.zeros((tm, D), F32)

        def f(x_, oa_, ob_, gra_, grb_, ppa, ppb, pmix, bga, bgb, gate_t, g1, b1):
            return merge_fn(x_, oa_, ob_, gra_, grb_, ppa, ppb, pmix, bga, bgb, gate_t, g1, b1, vv[5], vv[6],
                            wa_ref[...], wb_ref[...], wo_ref[...])

        _, vjp, merged = jax.vjp(f, x_ref[...], oa_ref[...].astype(F32), ob_ref[...].astype(F32),
                                 gra_ref[...], grb_ref[...], zero, zero, zero, *vv[:5], has_aux=True)
        dx, doa, dob, dga, dgb, dpa, dpb, dmix, dbga, dbgb, dgt, dg1, db1 = vjp(dy_ref[...])
        dx_ref[...] = dx
        doa_ref[...] = doa
        dob_ref[...] = dob
        dga_ref[...] = dga.astype(BF16)
        dgb_ref[...] = dgb.astype(BF16)
        mg_ref[...] = merged.astype(BF16)
        dmix_ref[...] = dmix.astype(BF16)
        dpa_ref[...] = dpa.astype(BF16)
        dpb_ref[...] = dpb.astype(BF16)
        accs = (dbga_ref, dbgb_ref, dgt_ref, dg1_ref, db1_ref)

        @pl.when(i == 0)
        def _():
            for a in accs:
                a[...] = jnp.zeros_like(a)

        for a, val in zip(accs, (dbga, dbgb, dgt, dg1, db1)):
            a[...] += val

    return pl.pallas_call(
        body, name="merge_bwd", grid=(t // tm,),
        in_specs=[_rows(tm, D), _rows(tm, D), _rows(tm, D), _rows(tm, D, PA_GATE // D), _rows(tm, D, PA_GATE // D + 1)]
        + [_const((1, D))] * 7 + [_const((D, D))] * 3 + [_rows(tm, D)],
        out_specs=[_rows(tm, D)] * 9 + [_const((1, D))] * 5,
        out_shape=[jax.ShapeDtypeStruct((t, D), F32)] * 3 + [jax.ShapeDtypeStruct((t, D), BF16)] * 6
        + [jax.ShapeDtypeStruct((1, D), F32)] * 5,
        compiler_params=_cparams(),
    )(x, oa, ob, proj, proj, *[vecs[n] for n in names], wa, wb, wo, dy1)


FFN_TM = 128


def ffn_act_fwd(up, conv_w, bconv):
    t, wdt = up.shape
    tm = FFN_TM

    def body(prev_ref, cur_ref, cw_ref, bc_ref, a_ref):
        i = pl.program_id(0)
        flag = jnp.where(i > 0, 1.0, 0.0)

        def ext(sl):
            return jnp.concatenate([prev_ref[:, sl] * flag, cur_ref[:, sl]], axis=0)

        def rows(sl):
            return tuple(cw_ref[j:j + 1, sl] for j in range(3))

        for cb in range(DFF // LANE):
            g = slice(cb * LANE, (cb + 1) * LANE)
            v = slice(DFF + cb * LANE, DFF + (cb + 1) * LANE)
            a_ref[:, g] = ffn_act_fn(ext(g), ext(v), rows(g), rows(v), bc_ref[:, g], bc_ref[:, v]).astype(BF16)

    return pl.pallas_call(
        body, name="ffn_act_fwd", grid=(t // tm,),
        in_specs=_halo_specs(tm, wdt, 0, lambda i: i) + [_const((3, wdt)), _const((1, wdt))],
        out_specs=_rows(tm, DFF),
        out_shape=jax.ShapeDtypeStruct((t, DFF), BF16),
        compiler_params=_cparams(),
    )(up, up, conv_w, bconv)


def ffn_act_bwd(up, conv_w, bconv, da):
    t, wdt = up.shape
    tm = FFN_TM
    nt = t // tm
    rev = lambda i: nt - 1 - i

    def body(prev_ref, cur_ref, cw_ref, bc_ref, da_ref, dup_ref, dcw_ref, dbc_ref, carry):
        i = pl.program_id(0)
        flag = jnp.where(i < nt - 1, 1.0, 0.0)

        @pl.when(i == 0)
        def _():
            carry[...] = jnp.zeros_like(carry)
            dcw_ref[...] = jnp.zeros_like(dcw_ref)
            dbc_ref[...] = jnp.zeros_like(dbc_ref)

        def ext(sl):
            return jnp.concatenate([prev_ref[:, sl] * flag, cur_ref[:, sl]], axis=0)

        def rows(sl):
            return tuple(cw_ref[j:j + 1, sl] for j in range(3))

        def emit(sl, dext, drows, dbc):
            dcur = dext[HALO:]
            dup_ref[:, sl] = jnp.concatenate([dcur[:tm - HALO], dcur[tm - HALO:] + carry[:, sl]], axis=0).astype(BF16)
            carry[:, sl] = dext[:HALO]
            dcw_ref[:, sl] += _stack_rows(drows)
            dbc_ref[:, sl] += dbc

        for cb in range(DFF // LANE):
            g = slice(cb * LANE, (cb + 1) * LANE)
            v = slice(DFF + cb * LANE, DFF + (cb + 1) * LANE)
            _, vjp = jax.vjp(ffn_act_fn, ext(g), ext(v), rows(g), rows(v), bc_ref[:, g], bc_ref[:, v])
            dxg, dxv, drg, drv, dbg, dbv = vjp(da_ref[:, g])
            emit(g, dxg, drg, dbg)
            emit(v, dxv, drv, dbv)

    return pl.pallas_call(
        body, name="ffn_act_bwd", grid=(nt,),
        in_specs=_halo_specs(tm, wdt, 0, rev) + [_const((3, wdt)), _const((1, wdt)), _rows(tm, DFF, 0, rev)],
        out_specs=[_rows(tm, wdt, 0, rev), _const((3, wdt)), _const((1, wdt))],
        out_shape=[jax.ShapeDtypeStruct((t, wdt), BF16), jax.ShapeDtypeStruct((3, wdt), F32),
                   jax.ShapeDtypeStruct((1, wdt), F32)],
        scratch_shapes=[pltpu.VMEM((HALO, wdt), F32)],
        compiler_params=_cparams(),
    )(up, up, conv_w, bconv, da)


HEAD_TM = 256


def head_fwd_bwd(a, y1, tgt, gate_f, g2, b2, wd):
    t = a.shape[0]
    tm = HEAD_TM

    def body(a_ref, y_ref, t_ref, gf_ref, g2_ref, b2_ref, wd_ref,
             da_ref, dy_ref, dffn_ref, dgf_ref, dg2_ref, db2_ref, loss_ref):
        i = pl.program_id(0)
        zero = jnp.zeros((tm, D), F32)

        def f(a_, y_, pf, gf, g2_, b2_):
            return head_fn(a_, y_, pf, gf, g2_, b2_, t_ref[...], wd_ref[...])

        loss, vjp = jax.vjp(f, a_ref[...].astype(F32), y_ref[...], zero, gf_ref[...], g2_ref[...], b2_ref[...])
        da, dy, dffn, dgf, dg2, db2 = vjp(jnp.ones((), F32))
        da_ref[...] = da
        dy_ref[...] = dy
        dffn_ref[...] = dffn.astype(BF16)
        accs = (dgf_ref, dg2_ref, db2_ref, loss_ref)

        @pl.when(i == 0)
        def _():
            for r in accs:
                r[...] = jnp.zeros_like(r)

        dgf_ref[...] += dgf
        dg2_ref[...] += dg2
        db2_ref[...] += db2
        loss_ref[...] += loss * jnp.ones((1, 128), F32)

    return pl.pallas_call(
        body, name="head_fwd_bwd", grid=(t // tm,),
        in_specs=[_rows(tm, DFF), _rows(tm, D), _rows(tm, D), _const((1, D)), _const((1, D)), _const((1, D)),
                  _const((DFF, D))],
        out_specs=[_rows(tm, DFF), _rows(tm, D), _rows(tm, D), _const((1, D)), _const((1, D)), _const((1, D)),
                   _const((1, 128))],
        out_shape=[jax.ShapeDtypeStruct((t, DFF), F32), jax.ShapeDtypeStruct((t, D), F32),
                   jax.ShapeDtypeStruct((t, D), BF16)] + [jax.ShapeDtypeStruct((1, D), F32)] * 3
        + [jax.ShapeDtypeStruct((1, 128), F32)],
        compiler_params=_cparams(),
    )(a, y1, tgt, gate_f, g2, b2, wd)


def ada_fwd(c_all, w_sh, b_sh):
    def body(c_ref, w_ref, b_ref, o_ref):
        o_ref[...] = _mmh(_silu(c_ref[...]), w_ref[...]) + b_ref[...]

    n = w_sh.shape[1]
    return pl.pallas_call(
        body, name="ada_fwd", out_shape=jax.ShapeDtypeStruct((NDEV, n), F32),
        in_specs=[pl.BlockSpec(memory_space=pltpu.VMEM)] * 3,
        out_specs=pl.BlockSpec(memory_space=pltpu.VMEM),
        compiler_params=pltpu.CompilerParams(vmem_limit_bytes=VMEM_LIMIT),
    )(c_all, w_sh, b_sh)


def ada_wgrad(c_all_t, dmod_sh):
    def body(c_ref, d_ref, o_ref):
        o_ref[...] = _mmh(_silu(c_ref[...]), d_ref[...])

    return pl.pallas_call(
        body, name="ada_wgrad", out_shape=jax.ShapeDtypeStruct((c_all_t.shape[0], dmod_sh.shape[1]), F32),
        in_specs=[pl.BlockSpec(memory_space=pltpu.VMEM)] * 2,
        out_specs=pl.BlockSpec(memory_space=pltpu.VMEM),
        compiler_params=pltpu.CompilerParams(vmem_limit_bytes=VMEM_LIMIT),
    )(c_all_t, dmod_sh)


def adamw(gparts, w, m, v, name):
    p, r, c = gparts.shape
    tr = r if r <= 256 else _pick(r, (256, 128, 64, 32, 16, 8))
    c1 = 1.0 - B1 ** STEP
    c2 = 1.0 - B2 ** STEP

    def body(g_ref, w_ref, m_ref, v_ref, go_ref, d_ref, mo_ref, vo_ref):
        g = g_ref[0].astype(F32)
        for s in range(1, p):
            g = g + g_ref[s].astype(F32)
        mn = B1 * m_ref[0] + (1.0 - B1) * g
        vn = B2 * v_ref[0] + (1.0 - B2) * (g * g)
        go_ref[0] = g
        d_ref[0] = -LR * ((mn / c1) / (jnp.sqrt(vn / c2) + AEPS) + WD * w_ref[0])
        mo_ref[0] = mn
        vo_ref[0] = vn

    spec = pl.BlockSpec((1, tr, c), lambda i: (0, i, 0))
    return pl.pallas_call(
        body, name=name, grid=(r // tr,),
        in_specs=[pl.BlockSpec((p, tr, c), lambda i: (0, i, 0)), spec, spec, spec],
        out_specs=[spec] * 4,
        out_shape=[jax.ShapeDtypeStruct((1, r, c), F32)] * 4,
        compiler_params=_cparams(),
    )(gparts, w, m, v)


def adamw_small(gs, ws, ms, vs, loss_parts, name):
    n = len(ws)
    c1 = 1.0 - B1 ** STEP
    c2 = 1.0 - B2 ** STEP

    def slots(ref):
        acc = ref[0]
        for s in range(1, ref.shape[0]):
            acc = acc + ref[s]
        return acc

    def body(*refs):
        g_refs, w_refs, m_refs, v_refs = (refs[k * n:(k + 1) * n] for k in range(4))
        l_ref, outs = refs[4 * n], refs[4 * n + 1:]
        for i in range(n):
            g = slots(g_refs[i])
            mn = B1 * m_refs[i][...] + (1.0 - B1) * g
            vn = B2 * v_refs[i][...] + (1.0 - B2) * (g * g)
            outs[i][...] = g
            outs[n + i][...] = -LR * ((mn / c1) / (jnp.sqrt(vn / c2) + AEPS) + WD * w_refs[i][...])
            outs[2 * n + i][...] = mn
            outs[3 * n + i][...] = vn
        outs[4 * n][...] = slots(l_ref)

    vmem = pl.BlockSpec(memory_space=pltpu.VMEM)
    outs = pl.pallas_call(
        body, name=name,
        in_specs=[vmem] * (4 * n + 1), out_specs=[vmem] * (4 * n + 1),
        out_shape=[jax.ShapeDtypeStruct(w.shape, F32) for w in ws] * 4 + [jax.ShapeDtypeStruct((1, LANE), F32)],
        compiler_params=pltpu.CompilerParams(vmem_limit_bytes=VMEM_LIMIT),
    )(*gs, *ws, *ms, *vs, loss_parts)
    return outs[:n], outs[n:2 * n], outs[2 * n:3 * n], outs[3 * n:4 * n], outs[4 * n]


def _me():
    x, y, c = lax.axis_index("x"), lax.axis_index("y"), lax.axis_index("c")
    return x, y, c, 4 * x + 2 * y + c


def _peer(x, y, c, d):
    px = 1 - x if (d >> 2) & 1 else x
    py = 1 - y if (d >> 1) & 1 else y
    pc = 1 - c if d & 1 else c
    return (px, py, pc), 4 * px + 2 * py + pc


def _exchange(arrs, name, scatter):
    n = len(arrs)

    def body(*refs):
        ins, outs = refs[:n], refs[n:2 * n]
        send, recv, lsem = refs[2 * n:]
        x, y, c, me = _me()
        remote, local = [], []
        for k in range(n):
            src = ins[k].at[me] if scatter else ins[k]
            cp = pltpu.make_async_copy(src, outs[k].at[me], lsem.at[k])
            cp.start()
            local.append(cp)
            for d in range(1, NDEV):
                dev, pid = _peer(x, y, c, d)
                src = ins[k].at[pid] if scatter else ins[k]
                cp = pltpu.make_async_remote_copy(src_ref=src, dst_ref=outs[k].at[me],
                                                  send_sem=send.at[k, d - 1], recv_sem=recv.at[k, d - 1],
                                                  device_id=dev, device_id_type=pl.DeviceIdType.MESH)
                cp.start()
                remote.append(cp)
        for cp in remote:
            cp.wait()
        for cp in local:
            cp.wait()

    shapes = [a.shape if scatter else (NDEV,) + a.shape for a in arrs]
    return pl.pallas_call(
        body, name=name,
        in_specs=[pl.BlockSpec(memory_space=pl.ANY)] * n,
        out_specs=[pl.BlockSpec(memory_space=pl.ANY)] * n,
        out_shape=[jax.ShapeDtypeStruct(s, a.dtype) for s, a in zip(shapes, arrs)],
        scratch_shapes=[pltpu.SemaphoreType.DMA((n, NDEV - 1)), pltpu.SemaphoreType.DMA((n, NDEV - 1)),
                        pltpu.SemaphoreType.DMA((n,))],
        compiler_params=pltpu.CompilerParams(has_side_effects=True),
    )(*arrs)


def all_gather(arrs, name):
    return _exchange(arrs, name, False)


def all_gather_two_level(shard, name):
    def body(x_ref, out_ref, send, recv, lsem):
        x, y, c, _ = _me()
        sibling = (x, y, 1 - c)
        chips = [(1 - x, y), (x, 1 - y), (1 - x, 1 - y)]

        def slot(px, py, pc):
            return out_ref.at[4 * px + 2 * py + pc]

        def copy(k, block, to, src=None):
            return pltpu.make_async_remote_copy(
                src_ref=slot(*block) if src is None else src, dst_ref=slot(*block),
                send_sem=send.at[k], recv_sem=recv.at[k], device_id=to, device_id_type=pl.DeviceIdType.MESH)

        mine = pltpu.make_async_copy(x_ref, slot(x, y, c), lsem)
        mine.start()
        first = [copy(0, (x, y, c), sibling, src=x_ref)]
        first += [copy(1 + j, (x, y, c), (*chip, c), src=x_ref) for j, chip in enumerate(chips)]
        for cp in first:
            cp.start()
        passed = [copy(4 + j, (*chip, c), sibling) for j, chip in enumerate(chips)]
        for j, chip in enumerate(chips):
            copy(1 + j, (*chip, c), (x, y, c)).wait_recv()
            passed[j].start()
        copy(0, sibling, (x, y, c)).wait_recv()
        for j, chip in enumerate(chips):
            copy(4 + j, (*chip, 1 - c), (x, y, c)).wait_recv()
        for cp in first + passed:
            cp.wait_send()
        mine.wait()

    return pl.pallas_call(
        body, name=name,
        in_specs=[pl.BlockSpec(memory_space=pl.ANY)],
        out_specs=pl.BlockSpec(memory_space=pl.ANY),
        out_shape=jax.ShapeDtypeStruct((NDEV,) + shard.shape, shard.dtype),
        scratch_shapes=[pltpu.SemaphoreType.DMA((NPEER,)), pltpu.SemaphoreType.DMA((NPEER,)),
                        pltpu.SemaphoreType.DMA],
        compiler_params=pltpu.CompilerParams(has_side_effects=True),
    )(shard)


def all_to_all(arrs, name):
    return _exchange(arrs, name, True)


_HBM = pl.BlockSpec(memory_space=pltpu.HBM)
_SEM = pl.BlockSpec(memory_space=pltpu.SEMAPHORE)
_EFFECT = pltpu.SideEffectType.DATAFLOW_SIDE_EFFECTING
NPEER = NDEV - 1


def exchange_start(arrs, name, scatter):
    n = len(arrs)
    lands = [lax.empty(a.shape if scatter else (NDEV,) + a.shape, a.dtype) for a in arrs]

    def body(*refs):
        ins, lrefs = refs[:n], refs[n:2 * n]
        send, recv, token = refs[2 * n], refs[2 * n + 1], refs[-1]
        x, y, c, me = _me()
        for k in range(n):
            for d in range(1, NDEV):
                dev, pid = _peer(x, y, c, d)
                src = ins[k].at[pid] if scatter else ins[k]
                pltpu.make_async_remote_copy(src_ref=src, dst_ref=lrefs[k].at[me],
                                             send_sem=send.at[k * NPEER + d - 1], recv_sem=recv.at[k * NPEER + d - 1],
                                             device_id=dev, device_id_type=pl.DeviceIdType.MESH).start()
        token[...] = jnp.zeros_like(token)

    thru = [pltpu.HBM(a.shape, a.dtype) for a in list(arrs) + lands]
    outs = pl.pallas_call(
        body, name=name,
        out_shape=(pltpu.SemaphoreType.DMA((n * NPEER,)), pltpu.SemaphoreType.DMA((n * NPEER,)), *thru,
                   jax.ShapeDtypeStruct((8, 128), F32)),
        in_specs=[_HBM] * (2 * n),
        out_specs=(_SEM, _SEM, *([_HBM] * (2 * n)), pl.BlockSpec(memory_space=pltpu.VMEM)),
        input_output_aliases={i: 2 + i for i in range(2 * n)},
        compiler_params=pltpu.CompilerParams(has_side_effects=_EFFECT),
    )(*[pltpu.with_memory_space_constraint(a, pltpu.HBM) for a in list(arrs) + lands])
    handle = dict(send=outs[0], recv=outs[1], src=list(outs[2:2 + n]), land=list(outs[2 + n:2 + 2 * n]),
                  scatter=scatter)
    return handle, outs[-1][0, 0]


def exchange_wait(handle, after, name):
    n = len(handle["src"])
    scatter = handle["scatter"]

    def body(*refs):
        ins, lrefs = refs[:n], refs[n:2 * n]
        send, recv = refs[2 * n], refs[2 * n + 1]
        x, y, c, _ = _me()
        for k in range(n):
            for d in range(1, NDEV):
                dev, _ = _peer(x, y, c, d)
                src = ins[k].at[0] if scatter else ins[k]
                cp = pltpu.make_async_remote_copy(src_ref=src, dst_ref=lrefs[k].at[0],
                                                  send_sem=send.at[k * NPEER + d - 1],
                                                  recv_sem=recv.at[k * NPEER + d - 1],
                                                  device_id=dev, device_id_type=pl.DeviceIdType.MESH)
                cp.wait_send()
                cp.wait_recv()

    arrs = handle["src"] + handle["land"]
    outs = pl.pallas_call(
        body, name=name,
        out_shape=tuple(pltpu.HBM(a.shape, a.dtype) for a in arrs),
        in_specs=[_HBM] * (2 * n) + [_SEM, _SEM, pl.BlockSpec(memory_space=pl.ANY)],
        out_specs=tuple([_HBM] * (2 * n)),
        input_output_aliases={i: i for i in range(2 * n)},
        compiler_params=pltpu.CompilerParams(has_side_effects=_EFFECT),
    )(*arrs, handle["send"], handle["recv"], after)
    me = 4 * lax.axis_index("x") + 2 * lax.axis_index("y") + lax.axis_index("c")
    landed = []
    for own, land in zip(outs[:n], outs[n:]):
        mine = lax.dynamic_index_in_dim(own, me, 0, keepdims=True) if scatter else own[None]
        landed.append(lax.dynamic_update_slice_in_dim(land, mine, me, 0))
    return landed


def _cat_from_slabs(slabs):
    _, k, n = slabs.shape

    def cols(lo, hi):
        parts, c = [], lo
        while c < hi:
            j = c // n
            e = min(hi, (j + 1) * n)
            parts.append(slabs[j][:, c - j * n:e - j * n])
            c = e
        return parts

    def zeros(w):
        return [jnp.zeros((k, w), slabs.dtype)]

    return jnp.concatenate(cols(0, 4096) + cols(4112, 9232) + cols(4096, 4104) + zeros(LANE - AH)
                           + cols(4104, 4112) + zeros(NCAT - C_BA - LANE - AH), axis=1)


IN_PIECES = (("pre", C_QKVA, 3072), ("z", C_Z, 1024), ("qb", C_QKVB, 1024), ("kb", C_QKVB + 1024, 1024),
             ("vb", C_QKVB + 2048, 1024), ("ga", C_GATE, 1024), ("gb", C_GATE + 1024, 1024))
_ORIG_SEGS = ((0, 3072, "pre", 0), (3072, 4096, "z", 0), (4096, 4104, "ba", 0), (4104, 4112, "ba", LANE),
              (4112, 5136, "qb", 0), (5136, 6160, "kb", 0), (6160, 7184, "vb", 0), (7184, 8208, "ga", 0),
              (8208, 9232, "gb", 0))


def _orig_cols_from_pieces(gp, lo, hi):
    parts = []
    for a, b, name, off in _ORIG_SEGS:
        s, e = max(a, lo), min(b, hi)
        if s < e:
            parts.append(gp[name][:, off + s - a:off + e - a])
    return parts[0] if len(parts) == 1 else jnp.concatenate(parts, axis=1)


def _pad128(v):
    return jnp.pad(v, ((0, 0), (0, 128 - v.shape[1])))


def local_step(x, tgt, mod, wts, small, late_weights=None, on_grads=None):
    if on_grads is None:
        on_grads = lambda group, gd: jnp.zeros((), F32)
    t = x.shape[0]
    nc = t // CH
    shift_t, scale_t, gate_t, shift_f, scale_f, gate_f = mod
    wcat = _cat_from_slabs(wts["w_in_slabs"])
    a_log = _pad128(small["a_log"])
    dtb = _pad128(small["dt_bias"])
    vecs = dict(bga=small["b_gate"][:, :D], bgb=small["b_gate"][:, D:], gate_t=gate_t, g1=small["ln1_g"],
                b1=small["ln1_b"], scale_f=scale_f, shift_f=shift_f)

    h1 = modulate(x, scale_t, shift_t, "modulate_t")
    wcat_a = jnp.concatenate([wcat[:, :C_QKVB], wcat[:, C_QKVB + W_QKVB:]], axis=1)
    proj = matmul(h1, wcat_a, F32, "in_proj")
    proj_b = matmul(h1, wcat[:, C_QKVB:C_QKVB + W_QKVB], BF16, "in_proj_b")
    q, k, v, gcs, beta = prep_fwd(proj, small["conv_a"], a_log, dtb)

    u, w, qg, kd, qk, eg, tinv = c1_fwd(q, k, v, gcs, beta)
    oa, sall = c2_fwd(u, w, qg, kd, qk, eg, proj, small["norm_a"])
    bias = bias_table(small["rel_bias"])
    ob, probs = attn_fwd(proj_b, bias)
    if late_weights is not None:
        wts = {**wts, **late_weights(ob)}
    y1, h2 = merge_fwd(x, oa, ob, proj, vecs, wts["w_a"], wts["w_b"], wts["w_o"])
    up = matmul(h2, wts["w_up"], F32, "up_proj")
    a = ffn_act_fwd(up, small["conv_ffn"], small["b_conv_ffn"])

    da, dy1_res, dffn, dgate_f, dg2, db2, loss = head_fwd_bwd(a, y1, tgt, gate_f, small["ln2_g"], small["ln2_b"],
                                                            wts["w_down"])
    g_w_down = matmul(a, dffn, BF16, "wgrad_down", ta=True)
    dup, g_conv_ffn, g_bconv = ffn_act_bwd(up, small["conv_ffn"], small["b_conv_ffn"], da)
    g_w_up = matmul(h2, dup, BF16, "wgrad_up", ta=True)
    tok = on_grads("ffn", dict(w_up=g_w_up, w_down=g_w_down))
    dy1, dscale_f, dshift_f = dgrad_modulated(dup, wts["w_up"], y1, dy1_res, scale_f + tok, "dgrad_up")
    (dx_res, doa, dob, dga, dgb, merged, dmix, dpa, dpb,
     dbga, dbgb, dgate_t, dg1, db1) = merge_bwd(x, oa, ob, proj, vecs, wts["w_a"], wts["w_b"], wts["w_o"], dy1)
    g_w_o = matmul(merged, dmix, BF16, "wgrad_o", ta=True)
    g_w_a = matmul(oa, dpa, BF16, "wgrad_a", ta=True)
    g_w_b = matmul(ob, dpb, BF16, "wgrad_b", ta=True)
    tok = on_grads("mix", dict(w_o=g_w_o, w_a=g_w_a, w_b=g_w_b))
    dqb, dkb, dvb, dbias = attn_bwd(proj_b, ob, probs, dob)
    g_rel = relbias_reduce(bias_table_bwd_layout(dbias))
    du, dw, dqg, dkd, dqk, deg, dz, g_norm = c2_bwd(u, w, qg, kd, qk, eg, proj, small["norm_a"] + tok, sall, doa)
    dq, dk, dv, dgcs, dbeta = c1_bwd(q, k, v, gcs, beta, tinv, du, dw, dqg, dkd, dqk, deg)
    dpre, dbb, daa, g_conv_a, g_alog, g_dtb = prep_bwd(proj, small["conv_a"], a_log, dtb, dq, dk, dv, dgcs, dbeta)
    tok = on_grads("small", dict(conv_a=g_conv_a, rel_bias=g_rel, conv_ffn=g_conv_ffn))
    dba = jnp.concatenate([dbb, daa, jnp.zeros((t, NCAT - C_BA - 2 * LANE), BF16)], axis=1) + tok.astype(BF16)
    dpieces = dict(pre=dpre, z=dz, qb=dqb, kb=dkb, vb=dvb, ga=dga, gb=dgb)
    g_in = {n: matmul(h1, dpieces[n], BF16, "wgrad_in_" + n, ta=True) for n, _, _ in IN_PIECES}
    g_in["ba"] = matmul(h1, dba, BF16, "wgrad_in_ba", ta=True)
    tok = on_grads("in", g_in)
    dh1 = dgrad_pieces([(dpieces[n], off) for n, off, _ in IN_PIECES], dba + tok.astype(BF16), wcat,
                       "dgrad_in")
    grad_x, dscale_t, dshift_t = modulate_bwd(dh1, x, dx_res, scale_t + tok, "modulate_t_bwd")

    dmod = (dshift_t, dscale_t, dgate_t, dshift_f, dscale_f, dgate_f)
    grads = dict(w_in=_orig_cols_from_pieces(g_in, 0, 9232), w_up=g_w_up, w_down=g_w_down, w_a=g_w_a, w_b=g_w_b, w_o=g_w_o,
                 conv_a=g_conv_a, rel_bias=g_rel, conv_ffn=g_conv_ffn,
                 b_gate=jnp.concatenate([dbga, dbgb], axis=1), a_log=g_alog[:, :AH], dt_bias=g_dtb[:, :AH],
                 norm_a=g_norm, ln1_g=dg1, ln1_b=db1, b_conv_ffn=g_bconv, ln2_g=dg2, ln2_b=db2)
    return loss[0, 0], grad_x, dmod, grads


REP_NAMES = ["b_ada", "b_gate", "a_log", "dt_bias", "norm_a", "ln1_g", "ln1_b", "b_conv_ffn", "ln2_g", "ln2_b"]
SH_NAMES = ["conv_a", "rel_bias", "conv_ffn"]


def _col_shards(a, n):
    return a.reshape(a.shape[0], NDEV, n).transpose(1, 0, 2)


def kernel(x, c, w_ada, b_ada, w_in, b_gate, conv_a, a_log, dt_bias, norm_a, rel_bias, w_branch_a, w_branch_b, w_o, ln1_g, ln1_b, w_up, conv_ffn, b_conv_ffn, w_down, ln2_g, ln2_b, loss_target, m_w_ada, m_b_ada, m_w_in, m_b_gate, m_conv_a, m_a_log, m_dt_bias, m_norm_a, m_rel_bias, m_w_branch_a, m_w_branch_b, m_w_o, m_ln1_g, m_ln1_b, m_w_up, m_conv_ffn, m_b_conv_ffn, m_w_down, m_ln2_g, m_ln2_b, v_w_ada, v_b_ada, v_w_in, v_b_gate, v_conv_a, v_a_log, v_dt_bias, v_norm_a, v_rel_bias, v_w_branch_a, v_w_branch_b, v_w_o, v_ln1_g, v_ln1_b, v_w_up, v_conv_ffn, v_b_conv_ffn, v_w_down, v_ln2_g, v_ln2_b):
    W = dict(w_ada=w_ada, b_ada=b_ada, w_in=w_in, b_gate=b_gate, conv_a=conv_a, a_log=a_log, dt_bias=dt_bias,
             norm_a=norm_a, rel_bias=rel_bias, w_branch_a=w_branch_a, w_branch_b=w_branch_b, w_o=w_o, ln1_g=ln1_g,
             ln1_b=ln1_b, w_up=w_up, conv_ffn=conv_ffn, b_conv_ffn=b_conv_ffn, w_down=w_down, ln2_g=ln2_g,
             ln2_b=ln2_b)
    M = dict(w_ada=m_w_ada, b_ada=m_b_ada, w_in=m_w_in, b_gate=m_b_gate, conv_a=m_conv_a, a_log=m_a_log,
             dt_bias=m_dt_bias, norm_a=m_norm_a, rel_bias=m_rel_bias, w_branch_a=m_w_branch_a,
             w_branch_b=m_w_branch_b, w_o=m_w_o, ln1_g=m_ln1_g, ln1_b=m_ln1_b, w_up=m_w_up, conv_ffn=m_conv_ffn,
             b_conv_ffn=m_b_conv_ffn, w_down=m_w_down, ln2_g=m_ln2_g, ln2_b=m_ln2_b)
    V = dict(w_ada=v_w_ada, b_ada=v_b_ada, w_in=v_w_in, b_gate=v_b_gate, conv_a=v_conv_a, a_log=v_a_log,
             dt_bias=v_dt_bias, norm_a=v_norm_a, rel_bias=v_rel_bias, w_branch_a=v_w_branch_a,
             w_branch_b=v_w_branch_b, w_o=v_w_o, ln1_g=v_ln1_g, ln1_b=v_ln1_b, w_up=v_w_up, conv_ffn=v_conv_ffn,
             b_conv_ffn=v_b_conv_ffn, w_down=v_w_down, ln2_g=v_ln2_g, ln2_b=v_ln2_b)
    W3, M3, V3 = W, M, V
    W, M, V = ({n: a[0] for n, a in dct.items()} for dct in (W, M, V))
    me = 4 * lax.axis_index("x") + 2 * lax.axis_index("y") + lax.axis_index("c")
    big = ("w_in", "w_up", "w_down", "w_branch_a", "w_branch_b", "w_o")

    g_in = all_gather_two_level(W["w_in"].astype(BF16), "gather_w_in")
    wts = dict(w_in_slabs=g_in)
    c_all, *sh_all = all_gather([c] + [W[n] for n in SH_NAMES], "gather_small")
    c_all = c_all.reshape(NDEV, D)

    def full_small(g8):
        return g8.transpose(1, 0, 2).reshape(g8.shape[1], -1)

    small = dict(conv_a=full_small(sh_all[0]), rel_bias=full_small(sh_all[1]), conv_ffn=full_small(sh_all[2]),
                 b_gate=W["b_gate"][None], a_log=W["a_log"][None], dt_bias=W["dt_bias"][None],
                 norm_a=W["norm_a"][None], ln1_g=W["ln1_g"][None], ln1_b=W["ln1_b"][None],
                 b_conv_ffn=W["b_conv_ffn"][None], ln2_g=W["ln2_g"][None], ln2_b=W["ln2_b"][None])

    nsh = w_ada.shape[2]
    b_sh = lax.dynamic_slice(W["b_ada"][None], (0, me * nsh), (1, nsh))
    mod_sh = ada_fwd(c_all, W["w_ada"], b_sh)
    (mod_rows,) = all_to_all([mod_sh[:, None, :]], "scatter_mod")
    mod6 = mod_rows.reshape(6, D)

    after_small = (g_in[0, 0, 0].astype(F32) * 0.0 + mod6[0, 0] * 0.0).astype(BF16)
    late, late_tok = exchange_start([W[n].astype(BF16) + after_small for n in big[1:]], "gather_late_start", False)

    def late_weights(after):
        g_up, g_down, g_a, g_b, g_o = exchange_wait(late, after, "gather_late_wait")
        return dict(w_up=g_up.transpose(1, 0, 2).reshape(D, -1), w_down=g_down.reshape(DFF, D),
                    w_a=g_a.reshape(D, D), w_b=g_b.reshape(D, D), w_o=g_o.reshape(D, D))

    mod6 = mod6 + late_tok
    mod = tuple(mod6[i:i + 1] for i in range(6))

    pending = {}

    def on_grads(group, gd):
        if group == "small":
            pending["small"] = all_to_all([_col_shards(gd[n], W[n].shape[1]) for n in SH_NAMES],
                                          "scatter_small_grads")
            return pending["small"][0][0, 0, 0] * 0.0
        if group == "ffn":
            slabs = [_col_shards(gd["w_up"], w_up.shape[2]), gd["w_down"].reshape(NDEV, -1, D)]
        elif group == "mix":
            slabs = [gd[n].reshape(NDEV, -1, D) for n in ("w_a", "w_b", "w_o")]
        else:
            nin = w_in.shape[2]
            slabs = [jnp.stack([_orig_cols_from_pieces(gd, j * nin, (j + 1) * nin) for j in range(NDEV)], axis=0)]
        pending[group], tok = exchange_start([s.astype(BF16) for s in slabs], "scatter_" + group + "_start", True)
        return tok

    loss, grad_x, dmod, g = local_step(x[0], loss_target[0], mod, wts, small, late_weights, on_grads)

    rep_grads = {n: g[n] for n in REP_NAMES if n != "b_ada"}
    rep_grads["b_ada"] = jnp.concatenate(dmod, axis=1)
    gathered = all_gather([rep_grads[n] for n in REP_NAMES] + [jnp.broadcast_to(loss, (1, LANE))],
                          "gather_small_grads")
    rep_all = dict(zip(REP_NAMES, gathered))
    sh_recv = [p[:, None] for p in pending["small"]]
    small_names = REP_NAMES + SH_NAMES
    sg, sd, sm, sv, loss_row = adamw_small([rep_all[n] for n in REP_NAMES] + sh_recv,
                                           [W3[n] for n in small_names], [M3[n] for n in small_names],
                                           [V3[n] for n in small_names], gathered[-1], "adamw_small")
    loss_total = loss_row[0, 0]

    dmod_all = rep_all["b_ada"][:, 0]
    dmod_sh = lax.dynamic_slice(dmod_all, (0, me * nsh), (NDEV, nsh))
    g_w_ada = ada_wgrad(c_all.T, dmod_sh)

    p_up, p_down = exchange_wait(pending["ffn"], grad_x, "scatter_ffn_wait")
    p_a, p_b, p_o = exchange_wait(pending["mix"], grad_x, "scatter_mix_wait")
    (p_in,) = exchange_wait(pending["in"], grad_x, "scatter_in_wait")
    parts = [p_in, p_up, p_down, p_a, p_b, p_o]

    res = {}
    for n, p in zip(big, parts):
        res[n] = adamw(p, W3[n], M3[n], V3[n], "adamw_" + n)
    res["w_ada"] = adamw(g_w_ada[None], W3["w_ada"], M3["w_ada"], V3["w_ada"], "adamw_w_ada")
    for i, n in enumerate(small_names):
        res[n] = (sg[i], sd[i], sm[i], sv[i])

    order = ("w_ada", "b_ada", "w_in", "b_gate", "conv_a", "a_log", "dt_bias", "norm_a", "rel_bias", "w_branch_a",
             "w_branch_b", "w_o", "ln1_g", "ln1_b", "w_up", "conv_ffn", "b_conv_ffn", "w_down", "ln2_g", "ln2_b")
    outs = [loss_total, grad_x[None]]
    for kind in range(4):
        outs += [res[n][kind] for n in order]
    return tuple(outs)
```
